```python
import math
import jax, jax.numpy as jnp
from jax import lax
import numpy as np

D_MODEL = 1024
BATCH = 8
SEQ = 4096
DEPTH = 1

ATTN_HEAD_DIM = 64
ATTN_PATTERNS = ((128, 1), (512, 4), (2048, 16))
HEADS_PER_PATTERN = 8
N_ATTN_HEADS = HEADS_PER_PATTERN * len(ATTN_PATTERNS)
ATTN_QKV = N_ATTN_HEADS * ATTN_HEAD_DIM
ATTN_OUT = HEADS_PER_PATTERN * ATTN_HEAD_DIM
ATTN_BLOCK = 128
ALIBI_MAX_EXP = 8.0
SSD_EXPAND = 2
SSD_INNER = SSD_EXPAND * D_MODEL
SSD_HEAD_DIM = 64
SSD_HEADS = SSD_INNER // SSD_HEAD_DIM
SSD_STATE = 128
SSD_GROUPS = 4
SSD_CONV = 4
SSD_CHUNK = 128
SSD_CONV_DIM = SSD_INNER + 2 * SSD_GROUPS * SSD_STATE
D_FF = 2816
EPS = 1e-6
IN_COLS = 3 * ATTN_QKV + SSD_INNER + SSD_CONV_DIM + SSD_HEADS + 2 * D_MODEL

kernel_name = "hybrid_dilated_attn_ssd_macaron"


def rmsnorm(x, g):
    x32 = x.astype(jnp.float32)
    y = x32 * lax.rsqrt(jnp.mean(x32 * x32, axis=-1, keepdims=True) + EPS)
    return (y * g.astype(jnp.float32)).astype(x.dtype)


def swiglu(h, w_gate, w_up, w_down):
    return (jax.nn.silu(h @ w_gate) * (h @ w_up)) @ w_down


def alibi_slopes(n):
    return jnp.exp2(-ALIBI_MAX_EXP * jnp.arange(1, n + 1, dtype=jnp.float32) / n)


def dilated_window_attention(q, k, v, slopes, window, dilation):
    b, S, H, hd = q.shape
    L = S // dilation
    n_back = window // dilation
    nb = -(-L // ATTN_BLOCK)
    Lp = nb * ATTN_BLOCK

    def to_blocks(t):
        t = t.reshape(b, L, dilation, H, hd).transpose(0, 2, 1, 3, 4)
        t = jnp.pad(t, ((0, 0), (0, 0), (0, Lp - L), (0, 0), (0, 0)))
        return t.reshape(b, dilation, nb, ATTN_BLOCK, H, hd)

    def with_prev(t):
        prev = jnp.pad(t, ((0, 0), (0, 0), (1, 0), (0, 0), (0, 0), (0, 0)))[:, :, :-1]
        return jnp.concatenate([prev, t], axis=3)

    qb = to_blocks(q)
    kk = with_prev(to_blocks(k))
    vv = with_prev(to_blocks(v))
    scale = 1.0 / math.sqrt(hd)
    logits = jnp.einsum('bdnqhe,bdnkhe->bdnhqk', qb, kk).astype(jnp.float32) * scale

    a_idx = jnp.arange(ATTN_BLOCK)[:, None]
    c_idx = jnp.arange(2 * ATTN_BLOCK)[None, :]
    rel = ATTN_BLOCK + a_idx - c_idx
    band = (rel >= 0) & (rel <= n_back)
    key_pos = jnp.arange(nb)[:, None] * ATTN_BLOCK + jnp.arange(2 * ATTN_BLOCK)[None, :] - ATTN_BLOCK
    mask = band[None] & (key_pos >= 0)[:, None, :]
    bias = -slopes[:, None, None] * (rel * dilation).astype(jnp.float32)[None]
    logits = jnp.where(mask[None, None, :, None], logits + bias[None, None, None], -jnp.inf)

    m = jnp.max(logits, axis=-1, keepdims=True)
    p = jnp.exp(logits - m)
    l = jnp.sum(p, axis=-1, keepdims=True)
    o = jnp.einsum('bdnhqk,bdnkhe->bdnqhe', p / l, vv.astype(jnp.float32))
    lse = (m + jnp.log(l))[..., 0]

    o = o.reshape(b, dilation, Lp, H, hd)[:, :, :L].transpose(0, 2, 1, 3, 4).reshape(b, S, H, hd)
    lse = lse.transpose(0, 1, 2, 4, 3).reshape(b, dilation, Lp, H)[:, :, :L]
    lse = lse.transpose(0, 2, 1, 3).reshape(b, S, H)
    return o, lse


def attention_branch(q, k, v, q_gain, k_gain):
    b, S, _ = q.shape
    q = rmsnorm(q.reshape(b, S, N_ATTN_HEADS, ATTN_HEAD_DIM), q_gain)
    k = rmsnorm(k.reshape(b, S, N_ATTN_HEADS, ATTN_HEAD_DIM), k_gain)
    v = v.reshape(b, S, N_ATTN_HEADS, ATTN_HEAD_DIM)
    slopes = alibi_slopes(N_ATTN_HEADS)
    outs, lses = [], []
    for g, (window, dilation) in enumerate(ATTN_PATTERNS):
        hs = slice(g * HEADS_PER_PATTERN, (g + 1) * HEADS_PER_PATTERN)
        o, lse = dilated_window_attention(q[:, :, hs], k[:, :, hs], v[:, :, hs],
                                          slopes[hs], window, dilation)
        outs.append(o)
        lses.append(lse)
    w = jax.nn.softmax(jnp.stack(lses, axis=0), axis=0)
    o = jnp.sum(w[..., None] * jnp.stack(outs, axis=0), axis=0)
    return o.reshape(b, S, ATTN_OUT).astype(q.dtype)


def causal_depthwise_conv(u, w, bias):
    C = u.shape[-1]
    y = lax.conv_general_dilated(u, w[:, None, :], window_strides=(1,),
                                 padding=[(SSD_CONV - 1, 0)],
                                 dimension_numbers=('NWC', 'WIO', 'NWC'),
                                 feature_group_count=C)
    return y + bias


def ssd_chunked(x, a, Bm, Cm):
    b, S, H, P = x.shape
    G, N = Bm.shape[2], Bm.shape[3]
    J = H // G
    nc = S // SSD_CHUNK
    Q = SSD_CHUNK
    x = x.reshape(b, nc, Q, G, J, P)
    a = a.reshape(b, nc, Q, G, J).transpose(0, 1, 3, 4, 2)
    Bm = Bm.reshape(b, nc, Q, G, N)
    Cm = Cm.reshape(b, nc, Q, G, N)
    a_cs = jnp.cumsum(a, axis=-1)
    diff = a_cs[..., :, None] - a_cs[..., None, :]
    causal = jnp.tril(jnp.ones((Q, Q), dtype=bool))
    decay = jnp.exp(jnp.where(causal, diff, -jnp.inf))
    CB = jnp.einsum('bclgn,bcsgn->bcgls', Cm, Bm)
    y_diag = jnp.einsum('bcgjls,bcsgjp->bclgjp', CB[:, :, :, None] * decay, x)
    decay_states = jnp.exp(a_cs[..., -1:] - a_cs)
    states = jnp.einsum('bclgn,bcgjl,bclgjp->bcgjpn', Bm, decay_states, x)
    chunk_decay = jnp.exp(a_cs[..., -1])

    def step(carry, inp):
        st, dec = inp
        return carry * dec[..., None, None] + st, carry

    init = jnp.zeros((b, G, J, P, N), dtype=x.dtype)
    _, prev_states = lax.scan(step, init, (states.transpose(1, 0, 2, 3, 4, 5),
                                           chunk_decay.transpose(1, 0, 2, 3)))
    prev_states = prev_states.transpose(1, 0, 2, 3, 4, 5)
    y_off = jnp.einsum('bclgn,bcgjpn,bcgjl->bclgjp', Cm, prev_states, jnp.exp(a_cs))
    return (y_diag + y_off).reshape(b, S, H, P)


def ssd_branch(z, xBC, dt_raw, conv_w, conv_b, dt_bias, a_log, d_skip, ssd_norm):
    b, S, _ = z.shape
    xBC = jax.nn.silu(causal_depthwise_conv(xBC, conv_w, conv_b))
    xs, Bm, Cm = jnp.split(xBC, [SSD_INNER, SSD_INNER + SSD_GROUPS * SSD_STATE], axis=-1)
    xs = xs.reshape(b, S, SSD_HEADS, SSD_HEAD_DIM).astype(jnp.float32)
    Bm = Bm.reshape(b, S, SSD_GROUPS, SSD_STATE).astype(jnp.float32)
    Cm = Cm.reshape(b, S, SSD_GROUPS, SSD_STATE).astype(jnp.float32)
    dt = jax.nn.softplus(dt_raw.astype(jnp.float32) + dt_bias.astype(jnp.float32))
    A = -jnp.exp(a_log.astype(jnp.float32))
    y = ssd_chunked(xs * dt[..., None], dt * A, Bm, Cm)
    y = y + xs * d_skip.astype(jnp.float32)[:, None]
    y = y.reshape(b, S, SSD_INNER).astype(z.dtype) * jax.nn.silu(z)
    y = rmsnorm(y.reshape(b, S, SSD_GROUPS, SSD_INNER // SSD_GROUPS),
                ssd_norm.reshape(SSD_GROUPS, SSD_INNER // SSD_GROUPS))
    return y.reshape(b, S, SSD_INNER)


def _fwd_setup_inputs(seed: int = 0) -> dict:
    key = jax.random.key(seed)
    ks = jax.random.split(key, 24)
    f32 = jnp.float32

    def normal(k, shape, scale):
        return jax.random.normal(k, shape, f32) * scale

    def gain(k, n):
        return 1.0 + 0.02 * jax.random.normal(k, (DEPTH, n), f32)

    dt = jnp.exp(jax.random.uniform(ks[12], (DEPTH, SSD_HEADS), f32,
                                    minval=math.log(1e-3), maxval=math.log(1e-1)))
    return {
        "x": normal(ks[0], (BATCH, SEQ, D_MODEL), 1.0),
        "ffn1_norm": gain(ks[1], D_MODEL),
        "ffn1_w_gate": normal(ks[2], (DEPTH, D_MODEL, D_FF), D_MODEL ** -0.5),
        "ffn1_w_up": normal(ks[3], (DEPTH, D_MODEL, D_FF), D_MODEL ** -0.5),
        "ffn1_w_down": normal(ks[4], (DEPTH, D_FF, D_MODEL), D_FF ** -0.5),
        "mix_norm": gain(ks[5], D_MODEL),
        "w_in": normal(ks[6], (DEPTH, D_MODEL, IN_COLS), D_MODEL ** -0.5),
        "q_norm": gain(ks[7], ATTN_HEAD_DIM),
        "k_norm": gain(ks[8], ATTN_HEAD_DIM),
        "conv_w": normal(ks[9], (DEPTH, SSD_CONV, SSD_CONV_DIM), SSD_CONV ** -0.5),
        "conv_b": normal(ks[10], (DEPTH, SSD_CONV_DIM), 0.02),
        "dt_bias": dt + jnp.log(-jnp.expm1(-dt)),
        "a_log": jnp.log(jax.random.uniform(ks[13], (DEPTH, SSD_HEADS), f32, minval=1.0, maxval=16.0)),
        "d_skip": 1.0 + 0.1 * jax.random.normal(ks[14], (DEPTH, SSD_HEADS), f32),
        "ssd_norm": gain(ks[15], SSD_INNER),
        "w_attn_branch": normal(ks[16], (DEPTH, ATTN_OUT, D_MODEL), ATTN_OUT ** -0.5),
        "w_ssd_branch": normal(ks[17], (DEPTH, SSD_INNER, D_MODEL), SSD_INNER ** -0.5),
        "w_out": normal(ks[18], (DEPTH, D_MODEL, D_MODEL), D_MODEL ** -0.5),
        "ffn2_norm": gain(ks[19], D_MODEL),
        "ffn2_w_gate": normal(ks[20], (DEPTH, D_MODEL, D_FF), D_MODEL ** -0.5),
        "ffn2_w_up": normal(ks[21], (DEPTH, D_MODEL, D_FF), D_MODEL ** -0.5),
        "ffn2_w_down": normal(ks[22], (DEPTH, D_FF, D_MODEL), D_FF ** -0.5),
    }


def _fwd_reference(x, ffn1_norm, ffn1_w_gate, ffn1_w_up, ffn1_w_down, mix_norm, w_in,
              q_norm, k_norm, conv_w, conv_b, dt_bias, a_log, d_skip, ssd_norm,
              w_attn_branch, w_ssd_branch, w_out, ffn2_norm, ffn2_w_gate, ffn2_w_up,
              ffn2_w_down):
    split_at = np.cumsum([ATTN_QKV, ATTN_QKV, ATTN_QKV, SSD_INNER, SSD_CONV_DIM,
                          SSD_HEADS, D_MODEL]).tolist()
    for l in range(DEPTH):
        x = x + 0.5 * swiglu(rmsnorm(x, ffn1_norm[l]), ffn1_w_gate[l], ffn1_w_up[l], ffn1_w_down[l])
        h = rmsnorm(x, mix_norm[l])
        proj = h @ w_in[l]
        q, k, v, z, xBC, dt_raw, g_attn, g_ssd = jnp.split(proj, split_at, axis=-1)
        a = attention_branch(q, k, v, q_norm[l], k_norm[l]) @ w_attn_branch[l]
        s = ssd_branch(z, xBC, dt_raw, conv_w[l], conv_b[l], dt_bias[l], a_log[l],
                       d_skip[l], ssd_norm[l]) @ w_ssd_branch[l]
        merged = jax.nn.sigmoid(g_attn) * a + jax.nn.sigmoid(g_ssd) * s
        x = x + merged @ w_out[l]
        x = x + 0.5 * swiglu(rmsnorm(x, ffn2_norm[l]), ffn2_w_gate[l], ffn2_w_up[l], ffn2_w_down[l])
    return x


import jax as _jax
import jax.numpy as _jnp

TWIN_FORMAT = 'train_step'
FWD_PARAMS = ['x', 'ffn1_norm', 'ffn1_w_gate', 'ffn1_w_up', 'ffn1_w_down', 'mix_norm', 'w_in', 'q_norm', 'k_norm', 'conv_w', 'conv_b', 'dt_bias', 'a_log', 'd_skip', 'ssd_norm', 'w_attn_branch', 'w_ssd_branch', 'w_out', 'ffn2_norm', 'ffn2_w_gate', 'ffn2_w_up', 'ffn2_w_down']
TWIN_WEIGHTS = ['ffn1_norm', 'ffn1_w_gate', 'ffn1_w_up', 'ffn1_w_down', 'mix_norm', 'w_in', 'q_norm', 'k_norm', 'conv_w', 'conv_b', 'dt_bias', 'a_log', 'd_skip', 'ssd_norm', 'w_attn_branch', 'w_ssd_branch', 'w_out', 'ffn2_norm', 'ffn2_w_gate', 'ffn2_w_up', 'ffn2_w_down']
TWIN_DIFF_INPUT = 'x'
TWIN_INPUTS = ['x', 'ffn1_norm', 'ffn1_w_gate', 'ffn1_w_up', 'ffn1_w_down', 'mix_norm', 'w_in', 'q_norm', 'k_norm', 'conv_w', 'conv_b', 'dt_bias', 'a_log', 'd_skip', 'ssd_norm', 'w_attn_branch', 'w_ssd_branch', 'w_out', 'ffn2_norm', 'ffn2_w_gate', 'ffn2_w_up', 'ffn2_w_down', 'loss_target', 'm_ffn1_norm', 'm_ffn1_w_gate', 'm_ffn1_w_up', 'm_ffn1_w_down', 'm_mix_norm', 'm_w_in', 'm_q_norm', 'm_k_norm', 'm_conv_w', 'm_conv_b', 'm_dt_bias', 'm_a_log', 'm_d_skip', 'm_ssd_norm', 'm_w_attn_branch', 'm_w_ssd_branch', 'm_w_out', 'm_ffn2_norm', 'm_ffn2_w_gate', 'm_ffn2_w_up', 'm_ffn2_w_down', 'v_ffn1_norm', 'v_ffn1_w_gate', 'v_ffn1_w_up', 'v_ffn1_w_down', 'v_mix_norm', 'v_w_in', 'v_q_norm', 'v_k_norm', 'v_conv_w', 'v_conv_b', 'v_dt_bias', 'v_a_log', 'v_d_skip', 'v_ssd_norm', 'v_w_attn_branch', 'v_w_ssd_branch', 'v_w_out', 'v_ffn2_norm', 'v_ffn2_w_gate', 'v_ffn2_w_up', 'v_ffn2_w_down']
TWIN_OUTPUTS = ['loss', 'grad_x', 'grad_ffn1_norm', 'grad_ffn1_w_gate', 'grad_ffn1_w_up', 'grad_ffn1_w_down', 'grad_mix_norm', 'grad_w_in', 'grad_q_norm', 'grad_k_norm', 'grad_conv_w', 'grad_conv_b', 'grad_dt_bias', 'grad_a_log', 'grad_d_skip', 'grad_ssd_norm', 'grad_w_attn_branch', 'grad_w_ssd_branch', 'grad_w_out', 'grad_ffn2_norm', 'grad_ffn2_w_gate', 'grad_ffn2_w_up', 'grad_ffn2_w_down', 'delta_ffn1_norm', 'delta_ffn1_w_gate', 'delta_ffn1_w_up', 'delta_ffn1_w_down', 'delta_mix_norm', 'delta_w_in', 'delta_q_norm', 'delta_k_norm', 'delta_conv_w', 'delta_conv_b', 'delta_dt_bias', 'delta_a_log', 'delta_d_skip', 'delta_ssd_norm', 'delta_w_attn_branch', 'delta_w_ssd_branch', 'delta_w_out', 'delta_ffn2_norm', 'delta_ffn2_w_gate', 'delta_ffn2_w_up', 'delta_ffn2_w_down', 'new_m_ffn1_norm', 'new_m_ffn1_w_gate', 'new_m_ffn1_w_up', 'new_m_ffn1_w_down', 'new_m_mix_norm', 'new_m_w_in', 'new_m_q_norm', 'new_m_k_norm', 'new_m_conv_w', 'new_m_conv_b', 'new_m_dt_bias', 'new_m_a_log', 'new_m_d_skip', 'new_m_ssd_norm', 'new_m_w_attn_branch', 'new_m_w_ssd_branch', 'new_m_w_out', 'new_m_ffn2_norm', 'new_m_ffn2_w_gate', 'new_m_ffn2_w_up', 'new_m_ffn2_w_down', 'new_v_ffn1_norm', 'new_v_ffn1_w_gate', 'new_v_ffn1_w_up', 'new_v_ffn1_w_down', 'new_v_mix_norm', 'new_v_w_in', 'new_v_q_norm', 'new_v_k_norm', 'new_v_conv_w', 'new_v_conv_b', 'new_v_dt_bias', 'new_v_a_log', 'new_v_d_skip', 'new_v_ssd_norm', 'new_v_w_attn_branch', 'new_v_w_ssd_branch', 'new_v_w_out', 'new_v_ffn2_norm', 'new_v_ffn2_w_gate', 'new_v_ffn2_w_up', 'new_v_ffn2_w_down']
TWIN_LEAF_KINDS = {'loss': 'loss', 'grad_x': 'grad_x', 'grad_ffn1_norm': 'grad_w', 'grad_ffn1_w_gate': 'grad_w', 'grad_ffn1_w_up': 'grad_w', 'grad_ffn1_w_down': 'grad_w', 'grad_mix_norm': 'grad_w', 'grad_w_in': 'grad_w', 'grad_q_norm': 'grad_w', 'grad_k_norm': 'grad_w', 'grad_conv_w': 'grad_w', 'grad_conv_b': 'grad_w', 'grad_dt_bias': 'grad_w', 'grad_a_log': 'grad_w', 'grad_d_skip': 'grad_w', 'grad_ssd_norm': 'grad_w', 'grad_w_attn_branch': 'grad_w', 'grad_w_ssd_branch': 'grad_w', 'grad_w_out': 'grad_w', 'grad_ffn2_norm': 'grad_w', 'grad_ffn2_w_gate': 'grad_w', 'grad_ffn2_w_up': 'grad_w', 'grad_ffn2_w_down': 'grad_w', 'delta_ffn1_norm': 'delta_w', 'delta_ffn1_w_gate': 'delta_w', 'delta_ffn1_w_up': 'delta_w', 'delta_ffn1_w_down': 'delta_w', 'delta_mix_norm': 'delta_w', 'delta_w_in': 'delta_w', 'delta_q_norm': 'delta_w', 'delta_k_norm': 'delta_w', 'delta_conv_w': 'delta_w', 'delta_conv_b': 'delta_w', 'delta_dt_bias': 'delta_w', 'delta_a_log': 'delta_w', 'delta_d_skip': 'delta_w', 'delta_ssd_norm': 'delta_w', 'delta_w_attn_branch': 'delta_w', 'delta_w_ssd_branch': 'delta_w', 'delta_w_out': 'delta_w', 'delta_ffn2_norm': 'delta_w', 'delta_ffn2_w_gate': 'delta_w', 'delta_ffn2_w_up': 'delta_w', 'delta_ffn2_w_down': 'delta_w', 'new_m_ffn1_norm': 'new_m', 'new_m_ffn1_w_gate': 'new_m', 'new_m_ffn1_w_up': 'new_m', 'new_m_ffn1_w_down': 'new_m', 'new_m_mix_norm': 'new_m', 'new_m_w_in': 'new_m', 'new_m_q_norm': 'new_m', 'new_m_k_norm': 'new_m', 'new_m_conv_w': 'new_m', 'new_m_conv_b': 'new_m', 'new_m_dt_bias': 'new_m', 'new_m_a_log': 'new_m', 'new_m_d_skip': 'new_m', 'new_m_ssd_norm': 'new_m', 'new_m_w_attn_branch': 'new_m', 'new_m_w_ssd_branch': 'new_m', 'new_m_w_out': 'new_m', 'new_m_ffn2_norm': 'new_m', 'new_m_ffn2_w_gate': 'new_m', 'new_m_ffn2_w_up': 'new_m', 'new_m_ffn2_w_down': 'new_m', 'new_v_ffn1_norm': 'new_v', 'new_v_ffn1_w_gate': 'new_v', 'new_v_ffn1_w_up': 'new_v', 'new_v_ffn1_w_down': 'new_v', 'new_v_mix_norm': 'new_v', 'new_v_w_in': 'new_v', 'new_v_q_norm': 'new_v', 'new_v_k_norm': 'new_v', 'new_v_conv_w': 'new_v', 'new_v_conv_b': 'new_v', 'new_v_dt_bias': 'new_v', 'new_v_a_log': 'new_v', 'new_v_d_skip': 'new_v', 'new_v_ssd_norm': 'new_v', 'new_v_w_attn_branch': 'new_v', 'new_v_w_ssd_branch': 'new_v', 'new_v_w_out': 'new_v', 'new_v_ffn2_norm': 'new_v', 'new_v_ffn2_w_gate': 'new_v', 'new_v_ffn2_w_up': 'new_v', 'new_v_ffn2_w_down': 'new_v'}


def _forward(args):
    return _fwd_reference(*[args[k] for k in FWD_PARAMS])


def _output_shape():
    out = _jax.eval_shape(lambda: _forward(_fwd_setup_inputs(0)))
    return out.shape, out.dtype

N_MICROBATCH = 1
ADAM_LR = 0.001
ADAM_B1 = 0.9
ADAM_B2 = 0.999
ADAM_EPS = 1e-08
ADAM_WD = 0.01
ADAM_STEP = 10
PER_EXAMPLE_BATCH_AXIS = {'x': 0, 'loss_target': 0}
SHARED_INPUTS = []
_WEIGHT_DTYPES = {'ffn1_norm': _jnp.float32, 'ffn1_w_gate': _jnp.float32, 'ffn1_w_up': _jnp.float32, 'ffn1_w_down': _jnp.float32, 'mix_norm': _jnp.float32, 'w_in': _jnp.float32, 'q_norm': _jnp.float32, 'k_norm': _jnp.float32, 'conv_w': _jnp.float32, 'conv_b': _jnp.float32, 'dt_bias': _jnp.float32, 'a_log': _jnp.float32, 'd_skip': _jnp.float32, 'ssd_norm': _jnp.float32, 'w_attn_branch': _jnp.float32, 'w_ssd_branch': _jnp.float32, 'w_out': _jnp.float32, 'ffn2_norm': _jnp.float32, 'ffn2_w_gate': _jnp.float32, 'ffn2_w_up': _jnp.float32, 'ffn2_w_down': _jnp.float32}
MOMENT_SCALE = {'ffn1_norm': 6.223424e+00, 'ffn1_w_gate': 7.950302e-02, 'ffn1_w_up': 8.427303e-02, 'ffn1_w_down': 1.378826e-01, 'mix_norm': 1.429755e+00, 'w_in': 9.398504e-02, 'q_norm': 5.578112e+00, 'k_norm': 5.574350e+00, 'conv_w': 2.019080e-01, 'conv_b': 6.210370e-01, 'dt_bias': 4.892919e-01, 'a_log': 1.760775e+00, 'd_skip': 1.350314e+00, 'ssd_norm': 6.567465e+00, 'w_attn_branch': 6.170556e-02, 'w_ssd_branch': 4.856481e-01, 'w_out': 4.116852e-01, 'ffn2_norm': 6.222079e+00, 'ffn2_w_gate': 7.130249e-02, 'ffn2_w_up': 7.476225e-02, 'ffn2_w_down': 1.208344e-01}


def _to_microbatches(a, axis):
    t = _jnp.moveaxis(a, axis, 0)
    t = t.reshape((N_MICROBATCH, t.shape[0] // N_MICROBATCH) + t.shape[1:])
    return _jnp.moveaxis(t, 1, axis + 1)


def setup_inputs(seed: int = 0) -> dict:
    inp = _fwd_setup_inputs(seed)
    key = _jax.random.fold_in(_jax.random.key(seed), 7919)
    shape, _ = _output_shape()
    out = dict(inp)
    out["loss_target"] = _jax.random.normal(_jax.random.fold_in(key, 0), shape, _jnp.float32)
    for i, name in enumerate(TWIN_WEIGHTS):
        w = inp[name].astype(_jnp.float32)
        if MOMENT_SCALE is None:
            s = _jnp.sqrt(_jnp.mean(_jnp.square(w)) + 1e-30)
        else:
            s = MOMENT_SCALE[name]
        km, kv = _jax.random.split(_jax.random.fold_in(key, i + 1))
        out[name] = w
        out["m_" + name] = s * _jax.random.normal(km, w.shape, _jnp.float32)
        out["v_" + name] = (s * s) * _jax.random.uniform(kv, w.shape, _jnp.float32, 0.5, 1.5)
    if N_MICROBATCH > 1:
        for name, axis in PER_EXAMPLE_BATCH_AXIS.items():
            out[name] = _to_microbatches(out[name], axis)
    return {'x': out['x'], 'ffn1_norm': out['ffn1_norm'], 'ffn1_w_gate': out['ffn1_w_gate'], 'ffn1_w_up': out['ffn1_w_up'], 'ffn1_w_down': out['ffn1_w_down'], 'mix_norm': out['mix_norm'], 'w_in': out['w_in'], 'q_norm': out['q_norm'], 'k_norm': out['k_norm'], 'conv_w': out['conv_w'], 'conv_b': out['conv_b'], 'dt_bias': out['dt_bias'], 'a_log': out['a_log'], 'd_skip': out['d_skip'], 'ssd_norm': out['ssd_norm'], 'w_attn_branch': out['w_attn_branch'], 'w_ssd_branch': out['w_ssd_branch'], 'w_out': out['w_out'], 'ffn2_norm': out['ffn2_norm'], 'ffn2_w_gate': out['ffn2_w_gate'], 'ffn2_w_up': out['ffn2_w_up'], 'ffn2_w_down': out['ffn2_w_down'], 'loss_target': out['loss_target'], 'm_ffn1_norm': out['m_ffn1_norm'], 'm_ffn1_w_gate': out['m_ffn1_w_gate'], 'm_ffn1_w_up': out['m_ffn1_w_up'], 'm_ffn1_w_down': out['m_ffn1_w_down'], 'm_mix_norm': out['m_mix_norm'], 'm_w_in': out['m_w_in'], 'm_q_norm': out['m_q_norm'], 'm_k_norm': out['m_k_norm'], 'm_conv_w': out['m_conv_w'], 'm_conv_b': out['m_conv_b'], 'm_dt_bias': out['m_dt_bias'], 'm_a_log': out['m_a_log'], 'm_d_skip': out['m_d_skip'], 'm_ssd_norm': out['m_ssd_norm'], 'm_w_attn_branch': out['m_w_attn_branch'], 'm_w_ssd_branch': out['m_w_ssd_branch'], 'm_w_out': out['m_w_out'], 'm_ffn2_norm': out['m_ffn2_norm'], 'm_ffn2_w_gate': out['m_ffn2_w_gate'], 'm_ffn2_w_up': out['m_ffn2_w_up'], 'm_ffn2_w_down': out['m_ffn2_w_down'], 'v_ffn1_norm': out['v_ffn1_norm'], 'v_ffn1_w_gate': out['v_ffn1_w_gate'], 'v_ffn1_w_up': out['v_ffn1_w_up'], 'v_ffn1_w_down': out['v_ffn1_w_down'], 'v_mix_norm': out['v_mix_norm'], 'v_w_in': out['v_w_in'], 'v_q_norm': out['v_q_norm'], 'v_k_norm': out['v_k_norm'], 'v_conv_w': out['v_conv_w'], 'v_conv_b': out['v_conv_b'], 'v_dt_bias': out['v_dt_bias'], 'v_a_log': out['v_a_log'], 'v_d_skip': out['v_d_skip'], 'v_ssd_norm': out['v_ssd_norm'], 'v_w_attn_branch': out['v_w_attn_branch'], 'v_w_ssd_branch': out['v_w_ssd_branch'], 'v_w_out': out['v_w_out'], 'v_ffn2_norm': out['v_ffn2_norm'], 'v_ffn2_w_gate': out['v_ffn2_w_gate'], 'v_ffn2_w_up': out['v_ffn2_w_up'], 'v_ffn2_w_down': out['v_ffn2_w_down']}


def _loss(weights, diff, rest, loss_target):
    with _jax.named_scope("forward"):
        args = {**rest, TWIN_DIFF_INPUT: diff, **{k: w.astype(_WEIGHT_DTYPES[k]) for k, w in weights.items()}}
        y = _forward(args)
    with _jax.named_scope("loss_head"):
        err = _jnp.square(y.astype(_jnp.float32) - loss_target)
        return 0.5 * _jnp.sum(_jnp.mean(err, axis=-1)) if err.ndim else 0.5 * err


def _adamw(w, g, m, v):
    m = ADAM_B1 * m + (1.0 - ADAM_B1) * g
    v = ADAM_B2 * v + (1.0 - ADAM_B2) * _jnp.square(g)
    m_hat = m / (1.0 - ADAM_B1 ** ADAM_STEP)
    v_hat = v / (1.0 - ADAM_B2 ** ADAM_STEP)
    delta = -ADAM_LR * (m_hat / (_jnp.sqrt(v_hat) + ADAM_EPS) + ADAM_WD * w)
    return delta, m, v


def reference(x, ffn1_norm, ffn1_w_gate, ffn1_w_up, ffn1_w_down, mix_norm, w_in, q_norm, k_norm, conv_w, conv_b, dt_bias, a_log, d_skip, ssd_norm, w_attn_branch, w_ssd_branch, w_out, ffn2_norm, ffn2_w_gate, ffn2_w_up, ffn2_w_down, loss_target, m_ffn1_norm, m_ffn1_w_gate, m_ffn1_w_up, m_ffn1_w_down, m_mix_norm, m_w_in, m_q_norm, m_k_norm, m_conv_w, m_conv_b, m_dt_bias, m_a_log, m_d_skip, m_ssd_norm, m_w_attn_branch, m_w_ssd_branch, m_w_out, m_ffn2_norm, m_ffn2_w_gate, m_ffn2_w_up, m_ffn2_w_down, v_ffn1_norm, v_ffn1_w_gate, v_ffn1_w_up, v_ffn1_w_down, v_mix_norm, v_w_in, v_q_norm, v_k_norm, v_conv_w, v_conv_b, v_dt_bias, v_a_log, v_d_skip, v_ssd_norm, v_w_attn_branch, v_w_ssd_branch, v_w_out, v_ffn2_norm, v_ffn2_w_gate, v_ffn2_w_up, v_ffn2_w_down):
    given = dict(x=x, ffn1_norm=ffn1_norm, ffn1_w_gate=ffn1_w_gate, ffn1_w_up=ffn1_w_up, ffn1_w_down=ffn1_w_down, mix_norm=mix_norm, w_in=w_in, q_norm=q_norm, k_norm=k_norm, conv_w=conv_w, conv_b=conv_b, dt_bias=dt_bias, a_log=a_log, d_skip=d_skip, ssd_norm=ssd_norm, w_attn_branch=w_attn_branch, w_ssd_branch=w_ssd_branch, w_out=w_out, ffn2_norm=ffn2_norm, ffn2_w_gate=ffn2_w_gate, ffn2_w_up=ffn2_w_up, ffn2_w_down=ffn2_w_down, loss_target=loss_target, m_ffn1_norm=m_ffn1_norm, m_ffn1_w_gate=m_ffn1_w_gate, m_ffn1_w_up=m_ffn1_w_up, m_ffn1_w_down=m_ffn1_w_down, m_mix_norm=m_mix_norm, m_w_in=m_w_in, m_q_norm=m_q_norm, m_k_norm=m_k_norm, m_conv_w=m_conv_w, m_conv_b=m_conv_b, m_dt_bias=m_dt_bias, m_a_log=m_a_log, m_d_skip=m_d_skip, m_ssd_norm=m_ssd_norm, m_w_attn_branch=m_w_attn_branch, m_w_ssd_branch=m_w_ssd_branch, m_w_out=m_w_out, m_ffn2_norm=m_ffn2_norm, m_ffn2_w_gate=m_ffn2_w_gate, m_ffn2_w_up=m_ffn2_w_up, m_ffn2_w_down=m_ffn2_w_down, v_ffn1_norm=v_ffn1_norm, v_ffn1_w_gate=v_ffn1_w_gate, v_ffn1_w_up=v_ffn1_w_up, v_ffn1_w_down=v_ffn1_w_down, v_mix_norm=v_mix_norm, v_w_in=v_w_in, v_q_norm=v_q_norm, v_k_norm=v_k_norm, v_conv_w=v_conv_w, v_conv_b=v_conv_b, v_dt_bias=v_dt_bias, v_a_log=v_a_log, v_d_skip=v_d_skip, v_ssd_norm=v_ssd_norm, v_w_attn_branch=v_w_attn_branch, v_w_ssd_branch=v_w_ssd_branch, v_w_out=v_w_out, v_ffn2_norm=v_ffn2_norm, v_ffn2_w_gate=v_ffn2_w_gate, v_ffn2_w_up=v_ffn2_w_up, v_ffn2_w_down=v_ffn2_w_down)
    weights = {n: given[n] for n in TWIN_WEIGHTS}
    shared = {n: given[n] for n in SHARED_INPUTS}
    per_example = {n: given[n] for n in ['x']}
    grad_fn = _jax.value_and_grad(_loss, argnums=(0, 1))

    def one_microbatch(ex, loss_target):
        ex = dict(ex)
        diff = ex.pop(TWIN_DIFF_INPUT)
        return grad_fn(weights, diff, {**shared, **ex}, loss_target)

    if N_MICROBATCH == 1:
        loss, (grad_w, grad_x) = one_microbatch(per_example, given["loss_target"])
    else:
        def body(carry, xs):
            loss_sum, grad_sum = carry
            l_k, (gw_k, gx_k) = one_microbatch(xs[0], xs[1])
            with _jax.named_scope("update"):
                return (loss_sum + l_k, _jax.tree.map(_jnp.add, grad_sum, gw_k)), gx_k

        init = (_jnp.zeros((), _jnp.float32), _jax.tree.map(_jnp.zeros_like, weights))
        (loss, grad_w), grad_x = _jax.lax.scan(body, init, (per_example, given["loss_target"]))
    with _jax.named_scope("update"):
        delta_w, new_m, new_v = {}, {}, {}
        for n in TWIN_WEIGHTS:
            delta_w[n], new_m[n], new_v[n] = _adamw(weights[n], grad_w[n], given["m_" + n], given["v_" + n])
    return (loss, grad_x, *[grad_w[n] for n in TWIN_WEIGHTS], *[delta_w[n] for n in TWIN_WEIGHTS],
            *[new_m[n] for n in TWIN_WEIGHTS], *[new_v[n] for n in TWIN_WEIGHTS])
```

```python
import functools
import math

import jax
import jax.numpy as jnp
from jax import lax
from jax.experimental import pallas as pl
from jax.experimental.pallas import tpu as pltpu

F32 = jnp.float32
BF16 = jnp.bfloat16
MESH = pl.DeviceIdType.MESH

EPS = 1e-6
D_MODEL = 1024
D_FF = 2816
N_CHIP = 4
FF_SHARD = D_FF // N_CHIP
HD = 64
BLK = 128
ATTN_DILATIONS = (1, 4, 16)
HEADS_PER_PATTERN = 8
N_ATTN_HEADS = 24
ALIBI_MAX_EXP = 8.0
ATTN_QKV = 1536
GROUP_W = 512
SSD_INNER = 2048
SSD_HEADS = 32
SSD_GROUPS = 4
SSD_CONV = 4
SSD_CONV_DIM = 3072
IN_COLS = 11808
IN_DT0, IN_DT1 = 9728, 9760
COL_K, COL_V, COL_Z, COL_XBC, COL_GA, COL_GS, P_COLS = 1536, 3072, 4608, 6656, 9728, 10752, 11776
DT_PAD = 128

ADAM_LR, ADAM_B1, ADAM_B2, ADAM_EPS, ADAM_WD, ADAM_STEP = 0.001, 0.9, 0.999, 1e-08, 0.01, 10

V7X_VMEM_LIMIT = 56 * 1024 * 1024
NEG = -1e30


def _call(body, *, name, out_shape, in_specs, out_specs, grid=(), scratch_shapes=(), aliases=None):
    return pl.pallas_call(
        body, out_shape=out_shape, grid=grid, in_specs=in_specs, out_specs=out_specs,
        scratch_shapes=scratch_shapes, input_output_aliases=aliases or {}, name=name,
        compiler_params=pltpu.CompilerParams(dimension_semantics=("arbitrary",) * len(grid),
                                             vmem_limit_bytes=V7X_VMEM_LIMIT))


def _sds(shape, dtype):
    return jax.ShapeDtypeStruct(tuple(shape), dtype)


def _dot(a, b):
    return jnp.dot(a, b, preferred_element_type=F32)


def _dot_nt(a, b):
    return lax.dot_general(a, b, (((1,), (1,)), ((), ())), preferred_element_type=F32)


def _dot_tn(a, b):
    return lax.dot_general(a, b, (((0,), (0,)), ((), ())), preferred_element_type=F32)


def _dot_hi(a, b):
    return jnp.dot(a, b, preferred_element_type=F32, precision=lax.Precision.HIGHEST)


def _sigmoid(x):
    return 1.0 / (1.0 + jnp.exp(-x))


def _lane_first_half(shape):
    return lax.broadcasted_iota(jnp.int32, shape, len(shape) - 1) < HD


def _pair_sum(x, first):
    s_all = jnp.sum(x, axis=-1, keepdims=True)
    s_a = jnp.sum(jnp.where(first, x, 0.0), axis=-1, keepdims=True)
    return s_a, s_all - s_a


def _rowwise(name, fn, rows, consts, outs, accs=(), tm=512):
    n_rows = None
    in_arrays, in_specs = [], []
    for r in rows:
        if isinstance(r, tuple):
            arr, w, cb = r
            spec = pl.BlockSpec((tm, w), functools.partial(lambda i, cb: (i, cb), cb=cb))
        else:
            arr = r
            spec = pl.BlockSpec((tm, arr.shape[1]), lambda i: (i, 0))
        n_rows = arr.shape[0]
        in_arrays.append(arr)
        in_specs.append(spec)
    for c in consts:
        in_arrays.append(c)
        in_specs.append(pl.BlockSpec(c.shape, functools.partial(lambda i, n: (0,) * n, n=c.ndim)))
    out_shape = [_sds(s, d) for s, d in outs] + [_sds(s, d) for s, d in accs]
    out_specs = [pl.BlockSpec((tm, s[1]), lambda i: (i, 0)) for s, _ in outs]
    out_specs += [pl.BlockSpec(s, functools.partial(lambda i, n: (0,) * n, n=len(s))) for s, _ in accs]

    def body(*refs):
        fn(pl.program_id(0), *refs)

    res = _call(body, name=name, out_shape=out_shape, grid=(n_rows // tm,), in_specs=in_specs,
                out_specs=out_specs)(*in_arrays)
    return res


def rms_fwd(x, gain, name):
    def fn(i, x_ref, g_ref, h_ref):
        xv = x_ref[...]
        r = lax.rsqrt(jnp.mean(xv * xv, axis=-1, keepdims=True) + EPS)
        h_ref[...] = (xv * r * g_ref[...]).astype(h_ref.dtype)

    return _rowwise(name, fn, [x], [gain], [(x.shape, BF16)])[0]


def rms_bwd(dhs, x, gain, dx_in, name):
    n = len(dhs)

    def fn(i, *refs):
        dh_refs, (x_ref, dxin_ref, g_ref, dx_ref, dg_ref) = refs[:n], refs[n:]
        dh = dh_refs[0][...]
        for r in dh_refs[1:]:
            dh = dh + r[...]
        xv = x_ref[...]
        r = lax.rsqrt(jnp.mean(xv * xv, axis=-1, keepdims=True) + EPS)
        xn = xv * r
        dxn = dh * g_ref[...]
        dx_ref[...] = dxin_ref[...] + r * (dxn - xn * jnp.mean(dxn * xn, axis=-1, keepdims=True))

        @pl.when(i == 0)
        def _():
            dg_ref[...] = jnp.zeros_like(dg_ref)

        dg_ref[...] += jnp.sum(dh * xn, axis=0, keepdims=True)

    return _rowwise(name, fn, list(dhs) + [x, dx_in], [gain], [(x.shape, F32)], [((1, x.shape[1]), F32)])


def loss_grad(y, target, name):
    def fn(i, y_ref, t_ref, dy_ref, sq_ref):
        err = y_ref[...] - t_ref[...]
        dy_ref[...] = err * (1.0 / y_ref.shape[1])

        @pl.when(i == 0)
        def _():
            sq_ref[...] = jnp.zeros_like(sq_ref)

        sq_ref[...] += jnp.sum(err * err, axis=0, keepdims=True)

    return _rowwise(name, fn, [y, target], [], [(y.shape, F32)], [((1, y.shape[1]), F32)])


def matmul_nn(a, b, name, out_dtype, tm, tn, res=None, scale=1.0):
    s, k = a.shape
    n = b.shape[1]

    def body(*refs):
        if res is None:
            a_ref, b_ref, o_ref = refs
            o_ref[...] = _dot(a_ref[...], b_ref[...]).astype(o_ref.dtype)
        else:
            a_ref, b_ref, r_ref, o_ref = refs
            o_ref[...] = (r_ref[...] + scale * _dot(a_ref[...], b_ref[...])).astype(o_ref.dtype)

    in_specs = [pl.BlockSpec((tm, k), lambda i, j: (i, 0)), pl.BlockSpec((k, tn), lambda i, j: (0, j))]
    args = [a, b]
    if res is not None:
        in_specs.append(pl.BlockSpec((tm, tn), lambda i, j: (i, j)))
        args.append(res)
    return _call(body, name=name, out_shape=_sds((s, n), out_dtype), grid=(s // tm, n // tn), in_specs=in_specs,
                 out_specs=pl.BlockSpec((tm, tn), lambda i, j: (i, j)))(*args)


def matmul_nt(a, b, name, out_dtype, tm, tn, tk):
    s, k = a.shape
    n = b.shape[0]
    nk = k // tk

    def body(a_ref, b_ref, o_ref, acc_ref):
        kk = pl.program_id(2)

        @pl.when(kk == 0)
        def _():
            acc_ref[...] = jnp.zeros_like(acc_ref)

        acc_ref[...] += _dot_nt(a_ref[...].astype(BF16), b_ref[...])

        @pl.when(kk == nk - 1)
        def _():
            o_ref[...] = acc_ref[...].astype(o_ref.dtype)

    return _call(body, name=name, out_shape=_sds((s, n), out_dtype), grid=(s // tm, n // tn, nk),
                 in_specs=[pl.BlockSpec((tm, tk), lambda i, j, kk: (i, kk)),
                           pl.BlockSpec((tn, tk), lambda i, j, kk: (j, kk))],
                 out_specs=pl.BlockSpec((tm, tn), lambda i, j, kk: (i, j)),
                 scratch_shapes=[pltpu.VMEM((tm, tn), F32)])(a, b)


def matmul_tn(a, b, name, tn, ts, a_scale=None, b_scale=None):
    s, m = a.shape
    n = b.shape[1]
    ns = s // ts

    def body(a_ref, b_ref, o_ref, acc_ref):
        ss = pl.program_id(1)

        @pl.when(ss == 0)
        def _():
            acc_ref[...] = jnp.zeros_like(acc_ref)

        av, bv = a_ref[...], b_ref[...]
        if a_scale is not None:
            av = av * a_scale
        if b_scale is not None:
            bv = bv * b_scale
        acc_ref[...] += _dot_tn(av.astype(BF16), bv.astype(BF16))

        @pl.when(ss == ns - 1)
        def _():
            o_ref[...] = acc_ref[...].astype(o_ref.dtype)

    return _call(body, name=name, out_shape=_sds((m, n), BF16), grid=(n // tn, ns),
                 in_specs=[pl.BlockSpec((ts, m), lambda j, ss: (ss, 0)), pl.BlockSpec((ts, tn), lambda j, ss: (ss, j))],
                 out_specs=pl.BlockSpec((m, tn), lambda j, ss: (0, j)),
                 scratch_shapes=[pltpu.VMEM((m, tn), F32)])(a, b)


def ffn_up(h, wg, wu, name, tm=1024):
    s = h.shape[0]

    def body(h_ref, wg_ref, wu_ref, g_ref, u_ref, a_ref):
        hv = h_ref[...]
        g = _dot(hv, wg_ref[...])
        u = _dot(hv, wu_ref[...])
        g_ref[...] = g.astype(BF16)
        u_ref[...] = u.astype(BF16)
        a_ref[...] = (g * _sigmoid(g) * u).astype(BF16)

    wspec = pl.BlockSpec((None, D_MODEL, FF_SHARD), lambda j, i: (j, 0, 0))
    ospec = pl.BlockSpec((None, tm, FF_SHARD), lambda j, i: (j, i, 0))
    shp = _sds((N_CHIP, s, FF_SHARD), BF16)
    return _call(body, name=name, out_shape=[shp, shp, shp], grid=(N_CHIP, s // tm),
                 in_specs=[pl.BlockSpec((tm, D_MODEL), lambda j, i: (i, 0)), wspec, wspec],
                 out_specs=[ospec, ospec, ospec])(h, wg, wu)


def ffn_down(a, wd, x, name, tm=512):
    s = x.shape[0]

    def body(a_ref, wd_ref, x_ref, o_ref):
        acc = _dot(a_ref[0], wd_ref[0])
        for j in range(1, N_CHIP):
            acc += _dot(a_ref[j], wd_ref[j])
        o_ref[...] = x_ref[...] + 0.5 * acc

    return _call(body, name=name, out_shape=_sds((s, D_MODEL), F32), grid=(s // tm,),
                 in_specs=[pl.BlockSpec((N_CHIP, tm, FF_SHARD), lambda i: (0, i, 0)),
                           pl.BlockSpec((N_CHIP, FF_SHARD, D_MODEL), lambda i: (0, 0, 0)),
                           pl.BlockSpec((tm, D_MODEL), lambda i: (i, 0))],
                 out_specs=pl.BlockSpec((tm, D_MODEL), lambda i: (i, 0)))(a, wd, x)


def ffn_bwd_hidden(dx, wd, g, u, name, tm=1024):
    s = dx.shape[0]

    def body(dx_ref, wd_ref, g_ref, u_ref, dg_ref, du_ref):
        dy = (0.5 * dx_ref[...]).astype(BF16)
        da = _dot_nt(dy, wd_ref[...])
        gv = g_ref[...].astype(F32)
        uv = u_ref[...].astype(F32)
        sg = _sigmoid(gv)
        dg_ref[...] = (da * uv * (sg * (1.0 + gv * (1.0 - sg)))).astype(BF16)
        du_ref[...] = (da * gv * sg).astype(BF16)

    hspec = pl.BlockSpec((None, tm, FF_SHARD), lambda j, i: (j, i, 0))
    shp = _sds((N_CHIP, s, FF_SHARD), BF16)
    return _call(body, name=name, out_shape=[shp, shp], grid=(N_CHIP, s // tm),
                 in_specs=[pl.BlockSpec((tm, D_MODEL), lambda j, i: (i, 0)),
                           pl.BlockSpec((None, FF_SHARD, D_MODEL), lambda j, i: (j, 0, 0)), hspec, hspec],
                 out_specs=[hspec, hspec])(dx, wd, g, u)


def ffn_bwd_input(dg, du, wg, wu, name, tm=512):
    s = dg.shape[1]

    def body(dg_ref, du_ref, wg_ref, wu_ref, o_ref):
        acc = _dot_nt(dg_ref[0], wg_ref[0]) + _dot_nt(du_ref[0], wu_ref[0])
        for j in range(1, N_CHIP):
            acc += _dot_nt(dg_ref[j], wg_ref[j]) + _dot_nt(du_ref[j], wu_ref[j])
        o_ref[...] = acc

    hspec = pl.BlockSpec((N_CHIP, tm, FF_SHARD), lambda i: (0, i, 0))
    wspec = pl.BlockSpec((N_CHIP, D_MODEL, FF_SHARD), lambda i: (0, 0, 0))
    return _call(body, name=name, out_shape=_sds((s, D_MODEL), F32), grid=(s // tm,),
                 in_specs=[hspec, hspec, wspec, wspec],
                 out_specs=pl.BlockSpec((tm, D_MODEL), lambda i: (i, 0)))(dg, du, wg, wu)


def ffn_wgrad_in(h, dgu, name, ts=1024):
    s = h.shape[0]
    ns = s // ts

    def body(h_ref, d_ref, o_ref, acc_ref):
        ss = pl.program_id(1)

        @pl.when(ss == 0)
        def _():
            acc_ref[...] = jnp.zeros_like(acc_ref)

        acc_ref[...] += _dot_tn(h_ref[...], d_ref[...])

        @pl.when(ss == ns - 1)
        def _():
            o_ref[...] = acc_ref[...].astype(BF16)

    return _call(body, name=name, out_shape=_sds((N_CHIP, D_MODEL, FF_SHARD), BF16), grid=(N_CHIP, ns),
                 in_specs=[pl.BlockSpec((ts, D_MODEL), lambda j, ss: (ss, 0)),
                           pl.BlockSpec((None, ts, FF_SHARD), lambda j, ss: (j, ss, 0))],
                 out_specs=pl.BlockSpec((None, D_MODEL, FF_SHARD), lambda j, ss: (j, 0, 0)),
                 scratch_shapes=[pltpu.VMEM((D_MODEL, FF_SHARD), F32)])(h, dgu)


def ffn_wgrad_down(a, dx, name, ts=1024):
    s = dx.shape[0]
    ns = s // ts

    def body(a_ref, dx_ref, o_ref, acc_ref):
        ss = pl.program_id(1)

        @pl.when(ss == 0)
        def _():
            acc_ref[...] = jnp.zeros_like(acc_ref)

        acc_ref[...] += _dot_tn(a_ref[...], (0.5 * dx_ref[...]).astype(BF16))

        @pl.when(ss == ns - 1)
        def _():
            o_ref[...] = acc_ref[...].astype(BF16)

    return _call(body, name=name, out_shape=_sds((N_CHIP, FF_SHARD, D_MODEL), BF16), grid=(N_CHIP, ns),
                 in_specs=[pl.BlockSpec((None, ts, FF_SHARD), lambda j, ss: (j, ss, 0)),
                           pl.BlockSpec((ts, D_MODEL), lambda j, ss: (ss, 0))],
                 out_specs=pl.BlockSpec((None, FF_SHARD, D_MODEL), lambda j, ss: (j, 0, 0)),
                 scratch_shapes=[pltpu.VMEM((FF_SHARD, D_MODEL), F32)])(a, dx)


def ffn_forward(x, gain, wg, wu, wd, tag):
    h = rms_fwd(x, gain, f"{tag}_rms")
    g, u, a = ffn_up(h, wg, wu, f"{tag}_up")
    y = ffn_down(a, wd, x, f"{tag}_down")
    return y, (h, g, u, a)


def ffn_backward(dy, x, gain, wg, wu, wd, saved, tag):
    h, g, u, a = saved
    d_wd = ffn_wgrad_down(a, dy, f"{tag}_dwd")
    dg, du = ffn_bwd_hidden(dy, wd, g, u, f"{tag}_dhid")
    d_wg = ffn_wgrad_in(h, dg, f"{tag}_dwg")
    d_wu = ffn_wgrad_in(h, du, f"{tag}_dwu")
    dh = ffn_bwd_input(dg, du, wg, wu, f"{tag}_dh")
    dx, d_gain = rms_bwd([dh], x, gain, dy, f"{tag}_drms")
    return dx, d_gain, d_wg, d_wu, d_wd


def _alibi_slope(head):
    return float(2.0 ** (-ALIBI_MAX_EXP * (head + 1) / N_ATTN_HEADS))


def _head_norm(t, gain_pair, first):
    sa, sb = _pair_sum(t * t, first)
    r = jnp.where(first, lax.rsqrt(sa * (1.0 / HD) + EPS), lax.rsqrt(sb * (1.0 / HD) + EPS))
    return t * r * gain_pair, r


def qk_norm_fwd(p, q_gain, k_gain, name):
    s = p.shape[0]

    def fn(i, q_ref, k_ref, qg_ref, kg_ref, qn_ref, kn_ref):
        first = _lane_first_half((q_ref.shape[0], 2 * HD))
        for src, g_ref, dst in ((q_ref, qg_ref, qn_ref), (k_ref, kg_ref, kn_ref)):
            for pr in range(ATTN_QKV // (2 * HD)):
                cols = slice(pr * 2 * HD, (pr + 1) * 2 * HD)
                y, _ = _head_norm(src[:, cols].astype(F32), g_ref[...], first)
                dst[:, cols] = y.astype(BF16)

    return _rowwise(name, fn, [(p, ATTN_QKV, 0), (p, ATTN_QKV, 1)], [q_gain, k_gain],
                    [((s, ATTN_QKV), BF16), ((s, ATTN_QKV), BF16)])


def qk_norm_bwd(p, dqn, dkn, q_gain, k_gain, name):
    s = p.shape[0]

    def fn(i, q_ref, k_ref, dqn_ref, dkn_ref, qg_ref, kg_ref, dq_ref, dk_ref, dqg_ref, dkg_ref):
        first = _lane_first_half((q_ref.shape[0], 2 * HD))

        @pl.when(i == 0)
        def _():
            dqg_ref[...] = jnp.zeros_like(dqg_ref)
            dkg_ref[...] = jnp.zeros_like(dkg_ref)

        for src, d_ref, g_ref, dst, dg_ref in ((q_ref, dqn_ref, qg_ref, dq_ref, dqg_ref),
                                               (k_ref, dkn_ref, kg_ref, dk_ref, dkg_ref)):
            for pr in range(ATTN_QKV // (2 * HD)):
                cols = slice(pr * 2 * HD, (pr + 1) * 2 * HD)
                t = src[:, cols].astype(F32)
                sa, sb = _pair_sum(t * t, first)
                r = jnp.where(first, lax.rsqrt(sa * (1.0 / HD) + EPS), lax.rsqrt(sb * (1.0 / HD) + EPS))
                xn = t * r
                dy = d_ref[:, cols]
                dg_ref[:, cols] += jnp.sum(dy * xn, axis=0, keepdims=True)
                dxn = dy * g_ref[...]
                ma, mb = _pair_sum(dxn * xn, first)
                mean = jnp.where(first, ma, mb) * (1.0 / HD)
                dst[:, cols] = (r * (dxn - xn * mean)).astype(BF16)

    return _rowwise(name, fn, [(p, ATTN_QKV, 0), (p, ATTN_QKV, 1), dqn, dkn], [q_gain, k_gain],
                    [((s, ATTN_QKV), BF16), ((s, ATTN_QKV), BF16)], [((1, ATTN_QKV), F32), ((1, ATTN_QKV), F32)])


def _to_streams(a, d):
    if d == 1:
        return a
    s, c = a.shape
    return a.reshape(s // d, d, c).transpose(1, 0, 2).reshape(s, c)


def _from_streams(a, d):
    if d == 1:
        return a
    s, c = a.shape
    return a.reshape(d, s // d, c).transpose(1, 0, 2).reshape(s, c)


def _attn_masks():
    row = lax.broadcasted_iota(jnp.int32, (BLK, BLK), 0)
    col = lax.broadcasted_iota(jnp.int32, (BLK, BLK), 1)
    rel_diag = row - col
    rel_prev = rel_diag + BLK
    return rel_diag, rel_prev


def attn_fwd(q, k, v, pattern, name, tq=512):
    s = q.shape[0]
    d = ATTN_DILATIONS[pattern]
    blocks_per_stream = (s // d) // BLK
    nsb = tq // BLK

    def body(q_ref, k_ref, v_ref, kp_ref, vp_ref, o_ref, l_ref):
        i = pl.program_id(0)
        rel_diag, rel_prev = _attn_masks()
        first = _lane_first_half((BLK, 2 * HD))
        rd_f = (rel_diag * d).astype(F32)
        rp_f = (rel_prev * d).astype(F32)
        for sb in range(nsb):
            rows = slice(sb * BLK, (sb + 1) * BLK)
            has_prev = ((i * nsb + sb) % blocks_per_stream != 0).astype(jnp.int32)
            m_diag = rel_diag >= 0
            m_prev = (rel_prev + (1 - has_prev) * (4 * BLK)) <= BLK
            for pr in range(GROUP_W // (2 * HD)):
                cols = slice(pr * 2 * HD, (pr + 1) * 2 * HD)
                qp = q_ref[rows, cols]
                kc, vc = k_ref[rows, cols], v_ref[rows, cols]
                if sb == 0:
                    kp, vp = kp_ref[:, cols], vp_ref[:, cols]
                else:
                    prows = slice((sb - 1) * BLK, sb * BLK)
                    kp, vp = k_ref[prows, cols], v_ref[prows, cols]
                outs, lses = [], []
                for e in range(2):
                    slope = _alibi_slope(pattern * HEADS_PER_PATTERN + 2 * pr + e)
                    qm = jnp.where(first if e == 0 else jnp.logical_not(first), qp, jnp.zeros_like(qp))
                    s1 = jnp.where(m_diag, _dot_nt(qm, kc) * 0.125 - slope * rd_f, NEG)
                    s0 = jnp.where(m_prev, _dot_nt(qm, kp) * 0.125 - slope * rp_f, NEG)
                    m = jnp.maximum(jnp.max(s1, axis=-1, keepdims=True), jnp.max(s0, axis=-1, keepdims=True))
                    p1 = jnp.exp(s1 - m)
                    p0 = jnp.exp(s0 - m)
                    l = jnp.sum(p1, axis=-1, keepdims=True) + jnp.sum(p0, axis=-1, keepdims=True)
                    inv = 1.0 / l
                    outs.append(_dot((p1 * inv).astype(BF16), vc) + _dot((p0 * inv).astype(BF16), vp))
                    lses.append(m + jnp.log(l))
                o_ref[rows, cols] = jnp.where(first, outs[0], outs[1])
                l_ref[rows, cols] = jnp.where(first, lses[0], lses[1])

    cur = pl.BlockSpec((tq, GROUP_W), lambda i: (i, 0))
    prev = pl.BlockSpec((BLK, GROUP_W), lambda i: (jnp.maximum(i * nsb - 1, 0), 0))
    return _call(body, name=name, out_shape=[_sds((s, GROUP_W), F32), _sds((s, GROUP_W), F32)], grid=(s // tq,),
                 in_specs=[cur, cur, cur, prev, prev], out_specs=[cur, cur])(q, k, v, k, v)


def attn_merge_fwd(os_, lses, name):
    s = os_[0].shape[0]

    def fn(i, o0, o1, o2, l0, l1, l2, out_ref):
        m = jnp.maximum(jnp.maximum(l0[...], l1[...]), l2[...])
        e0, e1, e2 = jnp.exp(l0[...] - m), jnp.exp(l1[...] - m), jnp.exp(l2[...] - m)
        inv = 1.0 / (e0 + e1 + e2)
        out_ref[...] = ((e0 * inv) * o0[...] + (e1 * inv) * o1[...] + (e2 * inv) * o2[...]).astype(BF16)

    return _rowwise(name, fn, list(os_) + list(lses), [], [((s, GROUP_W), BF16)])[0]


def attn_merge_bwd(d_out, os_, lses, name):
    s = d_out.shape[0]

    def fn(i, do_ref, o0, o1, o2, l0, l1, l2, d0, d1, d2, c0, c1, c2):
        first = _lane_first_half((do_ref.shape[0], 2 * HD))
        m = jnp.maximum(jnp.maximum(l0[...], l1[...]), l2[...])
        e0, e1, e2 = jnp.exp(l0[...] - m), jnp.exp(l1[...] - m), jnp.exp(l2[...] - m)
        inv = 1.0 / (e0 + e1 + e2)
        w0, w1, w2 = e0 * inv, e1 * inv, e2 * inv
        do = do_ref[...]
        prod = do * (w0 * o0[...] + w1 * o1[...] + w2 * o2[...])
        for pr in range(GROUP_W // (2 * HD)):
            cols = slice(pr * 2 * HD, (pr + 1) * 2 * HD)
            ta, tb = _pair_sum(prod[:, cols], first)
            t = jnp.where(first, ta, tb)
            for w, c_ref in ((w0, c0), (w1, c1), (w2, c2)):
                c_ref[:, cols] = w[:, cols] * t
        for w, d_ref in ((w0, d0), (w1, d1), (w2, d2)):
            d_ref[...] = (w * do).astype(BF16)

    shp = (s, GROUP_W)
    return _rowwise(name, fn, [d_out] + list(os_) + list(lses), [],
                    [(shp, BF16)] * 3 + [(shp, F32)] * 3)


def attn_bwd(q, k, v, d_o, cterm, lse, pattern, name, tq=512):
    s = q.shape[0]
    d = ATTN_DILATIONS[pattern]
    blocks_per_stream = (s // d) // BLK
    nsb = tq // BLK
    n_blocks = s // BLK

    def body(q_ref, k_ref, v_ref, do_ref, c_ref, l_ref, kp_ref, vp_ref, qn_ref, don_ref, cn_ref, ln_ref,
             dq_ref, dk_ref, dv_ref):
        i = pl.program_id(0)
        rel_diag, rel_prev = _attn_masks()
        first = _lane_first_half((BLK, 2 * HD))
        second = jnp.logical_not(first)
        rd_f = (rel_diag * d).astype(F32)
        rp_f = (rel_prev * d).astype(F32)
        m_diag = rel_diag >= 0
        dq_ref[...] = jnp.zeros_like(dq_ref)
        dk_ref[...] = jnp.zeros_like(dk_ref)
        dv_ref[...] = jnp.zeros_like(dv_ref)

        def pair(qp, dop, cp, lp, kp, vp, rel_f, mask):
            dq = dk = dv = None
            for e in range(2):
                lanes = first if e == 0 else second
                slope = slopes[e]
                qm = jnp.where(lanes, qp, jnp.zeros_like(qp))
                dom = jnp.where(lanes, dop, jnp.zeros_like(dop))
                km = jnp.where(lanes, kp, jnp.zeros_like(kp))
                sc = jnp.where(mask, _dot_nt(qm, kp) * 0.125 - slope * rel_f, NEG)
                pm = jnp.exp(sc - lp[:, e * HD:e * HD + 1])
                dl = pm * (_dot_nt(dom, vp) - cp[:, e * HD:e * HD + 1])
                dl16 = dl.astype(BF16)
                t_dq = _dot(dl16, km)
                t_dk = _dot_tn(dl16, qm)
                t_dv = _dot_tn(pm.astype(BF16), dom)
                dq = t_dq if dq is None else dq + t_dq
                dk = t_dk if dk is None else dk + t_dk
                dv = t_dv if dv is None else dv + t_dv
            return dq * 0.125, dk * 0.125, dv

        for pr in range(GROUP_W // (2 * HD)):
            cols = slice(pr * 2 * HD, (pr + 1) * 2 * HD)
            slopes = [_alibi_slope(pattern * HEADS_PER_PATTERN + 2 * pr + e) for e in range(2)]
            for sb in range(nsb + 1):
                gb = i * nsb + sb
                if sb < nsb:
                    rows = slice(sb * BLK, (sb + 1) * BLK)
                    qp, dop, cp, lp = q_ref[rows, cols], do_ref[rows, cols], c_ref[rows, cols], l_ref[rows, cols]
                else:
                    qp, dop, cp, lp = qn_ref[:, cols], don_ref[:, cols], cn_ref[:, cols], ln_ref[:, cols]
                if sb < nsb:
                    dq1, dk1, dv1 = pair(qp, dop, cp, lp, k_ref[rows, cols], v_ref[rows, cols], rd_f, m_diag)
                    dq_ref[rows, cols] += dq1
                    dk_ref[rows, cols] += dk1
                    dv_ref[rows, cols] += dv1
                valid = jnp.logical_and(gb % blocks_per_stream != 0, gb < n_blocks).astype(jnp.int32)
                m_prev = jnp.logical_and(rel_prev <= BLK, (rel_prev + (1 - valid) * (4 * BLK)) <= BLK)
                if sb == 0:
                    kp, vp = kp_ref[:, cols], vp_ref[:, cols]
                else:
                    prows = slice((sb - 1) * BLK, sb * BLK)
                    kp, vp = k_ref[prows, cols], v_ref[prows, cols]
                dq0, dk0, dv0 = pair(qp, dop, cp, lp, kp, vp, rp_f, m_prev)
                if sb < nsb:
                    dq_ref[rows, cols] += dq0
                if sb > 0:
                    dk_ref[prows, cols] += dk0
                    dv_ref[prows, cols] += dv0

    cur = pl.BlockSpec((tq, GROUP_W), lambda i: (i, 0))
    prev = pl.BlockSpec((BLK, GROUP_W), lambda i: (jnp.maximum(i * nsb - 1, 0), 0))
    nxt = pl.BlockSpec((BLK, GROUP_W), lambda i: (jnp.minimum((i + 1) * nsb, n_blocks - 1), 0))
    shp = _sds((s, GROUP_W), F32)
    return _call(body, name=name, out_shape=[shp, shp, shp], grid=(s // tq,),
                 in_specs=[cur] * 6 + [prev, prev] + [nxt] * 4, out_specs=[cur, cur, cur])(
                     q, k, v, d_o, cterm, lse, k, v, q, d_o, cterm, lse)


HALO = 16
CONV_TQ = 512


def conv_fwd(p, w, b, name):
    s = p.shape[0]
    tq = CONV_TQ
    ncol = SSD_CONV_DIM // GROUP_W
    cb0 = COL_XBC // GROUP_W

    def body(u_ref, up_ref, w_ref, b_ref, c_ref, xc_ref):
        i = pl.program_id(0)
        prev = up_ref[...].astype(F32) * (i > 0).astype(F32)
        ext = jnp.concatenate([prev, u_ref[...].astype(F32)], axis=0)
        acc = b_ref[...] + w_ref[SSD_CONV - 1:SSD_CONV, :] * ext[HALO:HALO + tq]
        for kk in range(SSD_CONV - 1):
            off = HALO - (SSD_CONV - 1) + kk
            acc += w_ref[kk:kk + 1, :] * ext[off:off + tq]
        c_ref[...] = acc.astype(BF16)
        xc_ref[...] = (acc * _sigmoid(acc)).astype(BF16)

    cur_in = pl.BlockSpec((tq, GROUP_W), lambda i, j: (i, cb0 + j))
    prev_in = pl.BlockSpec((HALO, GROUP_W), lambda i, j: (jnp.maximum(i * (tq // HALO) - 1, 0), cb0 + j))
    cur_out = pl.BlockSpec((tq, GROUP_W), lambda i, j: (i, j))
    shp = _sds((s, SSD_CONV_DIM), BF16)
    return _call(body, name=name, out_shape=[shp, shp], grid=(s // tq, ncol),
                 in_specs=[cur_in, prev_in, pl.BlockSpec((SSD_CONV, GROUP_W), lambda i, j: (0, j)),
                           pl.BlockSpec((1, GROUP_W), lambda i, j: (0, j))],
                 out_specs=[cur_out, cur_out])(p, p, w, b)


def conv_bwd(p, cpre, dxc, w, name):
    s = p.shape[0]
    tq = CONV_TQ
    ncol = SSD_CONV_DIM // GROUP_W
    cb0 = COL_XBC // GROUP_W
    nt = s // tq

    def body(u_ref, up_ref, c_ref, cn_ref, d_ref, dn_ref, w_ref, du_ref, dw_ref, db_ref):
        i = pl.program_id(1)

        def dpre(c16, dx):
            c = c16.astype(F32)
            sg = _sigmoid(c)
            return dx * (sg * (1.0 + c * (1.0 - sg)))

        dc = dpre(c_ref[...], d_ref[...])
        dcn = dpre(cn_ref[...], dn_ref[...]) * (i < nt - 1).astype(F32)
        dext = jnp.concatenate([dc, dcn], axis=0)
        prev = up_ref[...].astype(F32) * (i > 0).astype(F32)
        uext = jnp.concatenate([prev, u_ref[...].astype(F32)], axis=0)

        @pl.when(i == 0)
        def _():
            dw_ref[...] = jnp.zeros_like(dw_ref)
            db_ref[...] = jnp.zeros_like(db_ref)

        du = w_ref[SSD_CONV - 1:SSD_CONV, :] * dc
        for kk in range(SSD_CONV - 1):
            sh = SSD_CONV - 1 - kk
            du += w_ref[kk:kk + 1, :] * dext[sh:sh + tq]
        du_ref[...] = du.astype(BF16)
        for kk in range(SSD_CONV):
            off = HALO - (SSD_CONV - 1) + kk
            dw_ref[kk:kk + 1, :] += jnp.sum(dc * uext[off:off + tq], axis=0, keepdims=True)
        db_ref[...] += jnp.sum(dc, axis=0, keepdims=True)

    hb = tq // HALO
    cur_p = pl.BlockSpec((tq, GROUP_W), lambda j, i: (i, cb0 + j))
    prev_p = pl.BlockSpec((HALO, GROUP_W), lambda j, i: (jnp.maximum(i * hb - 1, 0), cb0 + j))
    cur = pl.BlockSpec((tq, GROUP_W), lambda j, i: (i, j))
    nxt = pl.BlockSpec((HALO, GROUP_W), lambda j, i: (jnp.minimum((i + 1) * hb, s // HALO - 1), j))
    return _call(body, name=name,
                 out_shape=[_sds((s, SSD_CONV_DIM), BF16), _sds((8, SSD_CONV_DIM), F32), _sds((1, SSD_CONV_DIM), F32)],
                 grid=(ncol, nt),
                 in_specs=[cur_p, prev_p, cur, nxt, cur, nxt, pl.BlockSpec((SSD_CONV, GROUP_W), lambda j, i: (0, j))],
                 out_specs=[cur, pl.BlockSpec((8, GROUP_W), lambda j, i: (0, j)),
                            pl.BlockSpec((1, GROUP_W), lambda j, i: (0, j))])(p, p, cpre, cpre, dxc, dxc, w)


def _softplus(x):
    return jnp.maximum(x, 0.0) + jnp.log(1.0 + jnp.exp(-jnp.abs(x)))


def _ssd_decays(dtr_ref, dtrt_ref, bias_ref, biast_ref, alog_ref, alogt_ref):
    row = lax.broadcasted_iota(jnp.int32, (BLK, BLK), 0)
    col = lax.broadcasted_iota(jnp.int32, (BLK, BLK), 1)
    lower = (row >= col).astype(F32)
    upper = (row <= col).astype(F32)
    dtb = dtr_ref[...] + bias_ref[...]
    dt = _softplus(dtb)
    a = dt * (-jnp.exp(alog_ref[...]))
    cs = _dot_hi(lower, a)
    a_t = _softplus(dtrt_ref[...] + biast_ref[...]) * (-jnp.exp(alogt_ref[...]))
    cs_t = _dot_hi(a_t, upper)
    return dtb, dt, cs, cs_t, row, col, upper


def ssd_fwd(p, xc, dtg, dtg_t, params, gn, name):
    s = p.shape[0]
    nc = s // BLK
    bias, bias_t, alog, alog_t, dskip = params

    def body(xs_ref, b_ref, c_ref, z_ref, dtr_ref, dtrt_ref, bias_ref, biast_ref, alog_ref, alogt_ref, dsk_ref,
             gn_ref, y_ref, sin_ref, hp_ref, h_ref):
        c_idx = pl.program_id(1)

        @pl.when(c_idx == 0)
        def _():
            h_ref[...] = jnp.zeros_like(h_ref)

        _, dt, cs, cs_t, row, col, _ = _ssd_decays(dtr_ref, dtrt_ref, bias_ref, biast_ref, alog_ref, alogt_ref)
        first = _lane_first_half((BLK, 2 * HD))
        first_row = _lane_first_half((1, 2 * HD))
        tril = row >= col
        b16, c16 = b_ref[...], c_ref[...]
        cb = _dot_nt(c16, b16)
        ys = []
        for pr in range(GROUP_W // (2 * HD)):
            cols = slice(pr * 2 * HD, (pr + 1) * 2 * HD)
            ha, hb = 2 * pr, 2 * pr + 1
            xs = xs_ref[:, cols].astype(F32)
            dt_pair = jnp.where(first, dt[:, ha:ha + 1], dt[:, hb:hb + 1])
            xt = xs * dt_pair
            xt16 = xt.astype(BF16)
            y_heads = []
            for h in (ha, hb):
                decay = jnp.exp(jnp.where(tril, cs[:, h:h + 1] - cs_t[h:h + 1, :], NEG))
                y_heads.append(_dot((cb * decay).astype(BF16), xt16))
            y_diag = jnp.where(first, y_heads[0], y_heads[1])
            hstate = h_ref[pr]
            hp_ref[pr] = hstate
            e_pair = jnp.where(first, jnp.exp(cs[:, ha:ha + 1]), jnp.exp(cs[:, hb:hb + 1]))
            y_off = e_pair * _dot(c16, hstate.astype(BF16))
            tot_a, tot_b = cs[BLK - 1:BLK, ha:ha + 1], cs[BLK - 1:BLK, hb:hb + 1]
            f_pair = jnp.where(first, jnp.exp(tot_a - cs[:, ha:ha + 1]), jnp.exp(tot_b - cs[:, hb:hb + 1]))
            new = _dot_tn(b16, (f_pair * xt).astype(BF16))
            dec = jnp.where(first_row, jnp.exp(tot_a), jnp.exp(tot_b))
            h_ref[pr] = dec * hstate + new
            d_pair = jnp.where(first_row, dsk_ref[:, ha:ha + 1], dsk_ref[:, hb:hb + 1])
            ys.append(y_diag + y_off + xs * d_pair)
        y = jnp.concatenate(ys, axis=1)
        y_ref[...] = y
        zv = z_ref[...].astype(F32)
        yz = y * (zv * _sigmoid(zv))
        r = lax.rsqrt(jnp.mean(yz * yz, axis=-1, keepdims=True) + EPS)
        sin_ref[...] = (yz * r * gn_ref[...]).astype(BF16)

    nb0 = SSD_INNER // BLK
    gparam = pl.BlockSpec((None, 1, 8), lambda g, c: (g, 0, 0))
    gparam_t = pl.BlockSpec((None, 8, 1), lambda g, c: (g, 0, 0))
    return _call(
        body, name=name,
        out_shape=[_sds((s, SSD_INNER), F32), _sds((s, SSD_INNER), BF16),
                   _sds((SSD_GROUPS, nc, 4, BLK, 2 * HD), F32)],
        grid=(SSD_GROUPS, nc),
        in_specs=[pl.BlockSpec((BLK, GROUP_W), lambda g, c: (c, g)),
                  pl.BlockSpec((BLK, BLK), lambda g, c: (c, nb0 + g)),
                  pl.BlockSpec((BLK, BLK), lambda g, c: (c, nb0 + SSD_GROUPS + g)),
                  pl.BlockSpec((BLK, GROUP_W), lambda g, c: (c, COL_Z // GROUP_W + g)),
                  pl.BlockSpec((None, BLK, 8), lambda g, c: (g, c, 0)),
                  pl.BlockSpec((None, 8, BLK), lambda g, c: (g, 0, c)),
                  gparam, gparam_t, gparam, gparam_t, gparam,
                  pl.BlockSpec((1, GROUP_W), lambda g, c: (0, g))],
        out_specs=[pl.BlockSpec((BLK, GROUP_W), lambda g, c: (c, g)),
                   pl.BlockSpec((BLK, GROUP_W), lambda g, c: (c, g)),
                   pl.BlockSpec((None, None, 4, BLK, 2 * HD), lambda g, c: (g, c, 0, 0, 0))],
        scratch_shapes=[pltpu.VMEM((4, BLK, 2 * HD), F32)],
    )(xc, xc, xc, p, dtg, dtg_t, bias, bias_t, alog, alog_t, dskip, gn)


def ssd_bwd(p, xc, bc_t, y, d_sin, hprev, dtg, dtg_t, params, gn, name):
    s = p.shape[0]
    nc = s // BLK
    bias, bias_t, alog, alog_t, dskip = params

    def body(xs_ref, b_ref, c_ref, bt_ref, ct_ref, z_ref, y_ref, dsin_ref, hp_ref, dtr_ref, dtrt_ref, bias_ref,
             biast_ref, alog_ref, alogt_ref, dsk_ref, gn_ref,
             dxs_ref, db_ref, dc_ref, dz_ref, ddt_ref, da_ref, dbias_ref, ddsk_ref, dgn_ref, dh_ref):
        c_idx = pl.program_id(1)

        @pl.when(c_idx == 0)
        def _():
            dh_ref[...] = jnp.zeros_like(dh_ref)
            da_ref[...] = jnp.zeros_like(da_ref)
            dbias_ref[...] = jnp.zeros_like(dbias_ref)
            ddsk_ref[...] = jnp.zeros_like(ddsk_ref)
            dgn_ref[...] = jnp.zeros_like(dgn_ref)

        dtb, dt, cs, cs_t, row, col, upper = _ssd_decays(dtr_ref, dtrt_ref, bias_ref, biast_ref, alog_ref, alogt_ref)
        first = _lane_first_half((BLK, 2 * HD))
        second = jnp.logical_not(first)
        first_row = _lane_first_half((1, 2 * HD))
        tril = row >= col
        triu = row <= col
        last_row = lax.broadcasted_iota(jnp.int32, (BLK, 1), 0) == BLK - 1
        lane8 = lax.broadcasted_iota(jnp.int32, (BLK, 8), 1)

        yv = y_ref[...]
        zv = z_ref[...].astype(F32)
        sg = _sigmoid(zv)
        yz = yv * (zv * sg)
        r = lax.rsqrt(jnp.mean(yz * yz, axis=-1, keepdims=True) + EPS)
        yzn = yz * r
        dsn = dsin_ref[...]
        dgn_ref[...] += jnp.sum(dsn * yzn, axis=0, keepdims=True)
        dsn = dsn * gn_ref[...]
        dyz = r * (dsn - yzn * jnp.mean(dsn * yzn, axis=-1, keepdims=True))
        dy = dyz * (zv * sg)
        dz_ref[...] = (dyz * yv * (sg * (1.0 + zv * (1.0 - sg)))).astype(BF16)
        xs_all = xs_ref[...].astype(F32)
        ddsk_ref[...] += jnp.sum(dy * xs_all, axis=0, keepdims=True)

        b16, c16, bt16, ct16 = b_ref[...], c_ref[...], bt_ref[...], ct_ref[...]
        cb = _dot_nt(c16, b16)
        cb_t = _dot_nt(b16, c16)
        g_sum = jnp.zeros((BLK, BLK), F32)
        gt_sum = jnp.zeros((BLK, BLK), F32)
        dc_acc = jnp.zeros((BLK, BLK), F32)
        db_acc = jnp.zeros((BLK, BLK), F32)
        dcs_all = jnp.zeros((BLK, 8), F32)
        ddtx_all = jnp.zeros((BLK, 8), F32)
        for pr in range(GROUP_W // (2 * HD)):
            cols = slice(pr * 2 * HD, (pr + 1) * 2 * HD)
            heads = (2 * pr, 2 * pr + 1)
            xs = xs_all[:, cols]
            dy_p = dy[:, cols]
            dt_pair = jnp.where(first, dt[:, heads[0]:heads[0] + 1], dt[:, heads[1]:heads[1] + 1])
            xt = xs * dt_pair
            xt16 = xt.astype(BF16)
            hstate = hp_ref[pr]
            h16 = hstate.astype(BF16)
            dhn = dh_ref[pr]
            dhn16 = dhn.astype(BF16)
            d_xt = jnp.zeros((BLK, 2 * HD), F32)
            dcs = []
            for e, h in enumerate(heads):
                lanes = first if e == 0 else second
                cs_c, cs_r = cs[:, h:h + 1], cs_t[h:h + 1, :]
                decay = jnp.exp(jnp.where(tril, cs_c - cs_r, NEG))
                decay_t = jnp.exp(jnp.where(triu, cs_r - cs_c, NEG))
                dym16 = jnp.where(lanes, dy_p, 0.0).astype(BF16)
                d_m = _dot_nt(dym16, xt16)
                d_mt = _dot_nt(xt16, dym16)
                d_xt += _dot((cb_t * decay_t).astype(BF16), dym16)
                gm = d_m * decay
                gmt = d_mt * decay_t
                g_sum += gm
                gt_sum += gmt
                dcs.append(jnp.sum(gm * cb, axis=-1, keepdims=True) - jnp.sum(gmt * cb_t, axis=-1, keepdims=True))
            exp_cs = [jnp.exp(cs[:, h:h + 1]) for h in heads]
            tots = [cs[BLK - 1:BLK, h:h + 1] for h in heads]
            f_col = [jnp.exp(tots[e] - cs[:, h:h + 1]) for e, h in enumerate(heads)]
            e_pair = jnp.where(first, exp_cs[0], exp_cs[1])
            f_pair = jnp.where(first, f_col[0], f_col[1])
            dec = [jnp.exp(t) for t in tots]
            dec_pair = jnp.where(first_row, dec[0], dec[1])
            edy = e_pair * dy_p
            edy16 = edy.astype(BF16)
            y_off = e_pair * _dot(c16, h16)
            oa, ob = _pair_sum(dy_p * y_off, first)
            dc_acc += _dot_nt(edy16, h16)
            dh_prev = _dot(ct16, edy16)
            zmat = _dot(b16, dhn16)
            d_xt += f_pair * zmat
            fa, fb = _pair_sum(zmat * xt, first)
            ta, tb = fa * f_col[0], fb * f_col[1]
            hh = dhn * hstate
            ha_sum = jnp.sum(jnp.sum(jnp.where(first, hh, 0.0), axis=-1, keepdims=True), axis=0, keepdims=True)
            hb_sum = jnp.sum(jnp.sum(hh, axis=-1, keepdims=True), axis=0, keepdims=True) - ha_sum
            dtot_a = jnp.sum(ta, axis=0, keepdims=True) + ha_sum * dec[0]
            dtot_b = jnp.sum(tb, axis=0, keepdims=True) + hb_sum * dec[1]
            dcs[0] = dcs[0] + oa - ta + jnp.where(last_row, dtot_a, 0.0)
            dcs[1] = dcs[1] + ob - tb + jnp.where(last_row, dtot_b, 0.0)
            db_acc += _dot_nt((f_pair * xt).astype(BF16), dhn16)
            dh_ref[pr] = dh_prev + dec_pair * dhn
            d_pair = jnp.where(first_row, dsk_ref[:, heads[0]:heads[0] + 1], dsk_ref[:, heads[1]:heads[1] + 1])
            dxs_ref[:, cols] = dy_p * d_pair + d_xt * dt_pair
            xa, xb = _pair_sum(d_xt * xs, first)
            for e, h in enumerate(heads):
                dcs_all = jnp.where(lane8 == h, dcs[e], dcs_all)
                ddtx_all = jnp.where(lane8 == h, (xa, xb)[e], ddtx_all)

        dc_ref[...] = dc_acc + _dot(g_sum.astype(BF16), b16)
        db_ref[...] = db_acc + _dot(gt_sum.astype(BF16), c16)
        d_a = _dot_hi(upper, dcs_all)
        a_neg = -jnp.exp(alog_ref[...])
        ddt = ddtx_all + d_a * a_neg
        da_ref[...] += jnp.sum(d_a * dt, axis=0, keepdims=True)
        ddtr = ddt * _sigmoid(dtb)
        ddt_ref[...] = ddtr
        dbias_ref[...] += jnp.sum(ddtr, axis=0, keepdims=True)

    nb0 = SSD_INNER // BLK
    rc = lambda c: nc - 1 - c
    gparam = pl.BlockSpec((None, 1, 8), lambda g, c: (g, 0, 0))
    gparam_t = pl.BlockSpec((None, 8, 1), lambda g, c: (g, 0, 0))
    wide = pl.BlockSpec((BLK, GROUP_W), lambda g, c: (rc(c), g))
    narrow = pl.BlockSpec((BLK, BLK), lambda g, c: (rc(c), g))
    return _call(
        body, name=name,
        out_shape=[_sds((s, SSD_INNER), F32), _sds((s, GROUP_W), F32), _sds((s, GROUP_W), F32),
                   _sds((s, SSD_INNER), BF16), _sds((SSD_GROUPS, s, 8), F32),
                   _sds((SSD_GROUPS, 1, 8), F32), _sds((SSD_GROUPS, 1, 8), F32),
                   _sds((SSD_GROUPS, 1, GROUP_W), F32), _sds((1, SSD_INNER), F32)],
        grid=(SSD_GROUPS, nc),
        in_specs=[wide,
                  pl.BlockSpec((BLK, BLK), lambda g, c: (rc(c), nb0 + g)),
                  pl.BlockSpec((BLK, BLK), lambda g, c: (rc(c), nb0 + SSD_GROUPS + g)),
                  pl.BlockSpec((BLK, BLK), lambda g, c: (g, rc(c))),
                  pl.BlockSpec((BLK, BLK), lambda g, c: (SSD_GROUPS + g, rc(c))),
                  pl.BlockSpec((BLK, GROUP_W), lambda g, c: (rc(c), COL_Z // GROUP_W + g)),
                  wide, wide,
                  pl.BlockSpec((None, None, 4, BLK, 2 * HD), lambda g, c: (g, rc(c), 0, 0, 0)),
                  pl.BlockSpec((None, BLK, 8), lambda g, c: (g, rc(c), 0)),
                  pl.BlockSpec((None, 8, BLK), lambda g, c: (g, 0, rc(c))),
                  gparam, gparam_t, gparam, gparam_t, gparam,
                  pl.BlockSpec((1, GROUP_W), lambda g, c: (0, g))],
        out_specs=[wide, narrow, narrow, wide,
                   pl.BlockSpec((None, BLK, 8), lambda g, c: (g, rc(c), 0)),
                   gparam, gparam,
                   pl.BlockSpec((None, 1, GROUP_W), lambda g, c: (g, 0, 0)),
                   pl.BlockSpec((1, GROUP_W), lambda g, c: (0, g))],
        scratch_shapes=[pltpu.VMEM((4, BLK, 2 * HD), F32)],
    )(xc, xc, xc, bc_t, bc_t, p, y, d_sin, hprev, dtg, dtg_t, bias, bias_t, alog, alog_t, dskip, gn)


def merge_fwd(p, a, sbr, name, tm=512):
    s = p.shape[0]
    nj = D_MODEL // GROUP_W

    def body(ga_ref, gs_ref, a_ref, s_ref, o_ref):
        o_ref[...] = (_sigmoid(ga_ref[...].astype(F32)) * a_ref[...]
                      + _sigmoid(gs_ref[...].astype(F32)) * s_ref[...]).astype(BF16)

    blk = pl.BlockSpec((tm, GROUP_W), lambda i, j: (i, j))
    return _call(body, name=name, out_shape=_sds((s, D_MODEL), BF16), grid=(s // tm, nj),
                 in_specs=[pl.BlockSpec((tm, GROUP_W), lambda i, j: (i, COL_GA // GROUP_W + j)),
                           pl.BlockSpec((tm, GROUP_W), lambda i, j: (i, COL_GS // GROUP_W + j)), blk, blk],
                 out_specs=blk)(p, p, a, sbr)


def merge_bwd(p, a, sbr, dmerged, name, tm=512):
    s = p.shape[0]
    nj = D_MODEL // GROUP_W

    def body(ga_ref, gs_ref, a_ref, s_ref, dm_ref, da_ref, ds_ref, dga_ref, dgs_ref):
        dm = dm_ref[...]
        sa = _sigmoid(ga_ref[...].astype(F32))
        ss = _sigmoid(gs_ref[...].astype(F32))
        da_ref[...] = (dm * sa).astype(BF16)
        ds_ref[...] = (dm * ss).astype(BF16)
        dga_ref[...] = (dm * a_ref[...] * sa * (1.0 - sa)).astype(BF16)
        dgs_ref[...] = (dm * s_ref[...] * ss * (1.0 - ss)).astype(BF16)

    blk = pl.BlockSpec((tm, GROUP_W), lambda i, j: (i, j))
    shp = _sds((s, D_MODEL), BF16)
    return _call(body, name=name, out_shape=[shp] * 4, grid=(s // tm, nj),
                 in_specs=[pl.BlockSpec((tm, GROUP_W), lambda i, j: (i, COL_GA // GROUP_W + j)),
                           pl.BlockSpec((tm, GROUP_W), lambda i, j: (i, COL_GS // GROUP_W + j)), blk, blk, blk],
                 out_specs=[blk] * 4)(p, p, a, sbr, dmerged)


def _group_major(v):
    return v.reshape(SSD_GROUPS, 1, 8), v.reshape(SSD_GROUPS, 8, 1)


def mixer_forward(x, w):
    s = x.shape[0]
    h = rms_fwd(x, w["mix_norm"], "mix_rms")
    p = matmul_nn(h, w["w_in_main"], "mix_proj", BF16, tm=1024, tn=512)
    dt_raw = matmul_nn(h, w["w_in_dt"], "mix_proj_dt", F32, tm=1024, tn=DT_PAD)
    qn, kn = qk_norm_fwd(p, w["q_gain"], w["k_gain"], "qk_norm")
    streams, os_, lses = [], [], []
    for g, d in enumerate(ATTN_DILATIONS):
        cols = slice(g * GROUP_W, (g + 1) * GROUP_W)
        qs, ks = _to_streams(qn[:, cols], d), _to_streams(kn[:, cols], d)
        vs = _to_streams(p[:, COL_V + g * GROUP_W:COL_V + (g + 1) * GROUP_W], d)
        o, lse = attn_fwd(qs, ks, vs, g, f"attn_fwd{g}")
        streams.append((qs, ks, vs, lse))
        os_.append(_from_streams(o, d))
        lses.append(_from_streams(lse, d))
    attn_o = attn_merge_fwd(os_, lses, "attn_merge")
    cpre, xc = conv_fwd(p, w["conv_w"], w["conv_b"], "conv_fwd")
    dtg = dt_raw[:, :SSD_HEADS].reshape(s, SSD_GROUPS, 8).transpose(1, 0, 2)
    dtg_t = dtg.transpose(0, 2, 1)
    params = (*_group_major(w["dt_bias"]), *_group_major(w["a_log"]), _group_major(w["d_skip"])[0])
    y, s_in, hprev = ssd_fwd(p, xc, dtg, dtg_t, params, w["ssd_norm"], "ssd_fwd")
    a = matmul_nn(attn_o, w["w_attn_branch"], "attn_branch", F32, tm=1024, tn=512)
    sbr = matmul_nn(s_in, w["w_ssd_branch"], "ssd_branch", F32, tm=1024, tn=512)
    merged = merge_fwd(p, a, sbr, "merge")
    x_out = matmul_nn(merged, w["w_out"], "mix_out", F32, tm=1024, tn=512, res=x)
    saved = dict(h=h, p=p, streams=streams, os=os_, lses=lses, attn_o=attn_o, cpre=cpre, xc=xc, dtg=dtg,
                 dtg_t=dtg_t, params=params, y=y, s_in=s_in, hprev=hprev, a=a, sbr=sbr, merged=merged)
    return x_out, saved


def mixer_backward(dx_out, x, w, sv):
    s = x.shape[0]
    p = sv["p"]
    g = {}
    dmerged = matmul_nt(dx_out, w["w_out"], "d_merged", F32, tm=1024, tn=512, tk=1024)
    g["w_out"] = matmul_tn(sv["merged"], dx_out, "dw_out", tn=512, ts=1024)
    da, ds, dga, dgs = merge_bwd(p, sv["a"], sv["sbr"], dmerged, "merge_bwd")
    g["w_attn_branch"] = matmul_tn(sv["attn_o"], da, "dw_attn_branch", tn=512, ts=1024)
    g["w_ssd_branch"] = matmul_tn(sv["s_in"], ds, "dw_ssd_branch", tn=512, ts=1024)
    d_attn_o = matmul_nt(da, w["w_attn_branch"], "d_attn_o", F32, tm=1024, tn=512, tk=1024)
    d_sin = matmul_nt(ds, w["w_ssd_branch"], "d_ssd_in", F32, tm=1024, tn=512, tk=1024)
    bc_t = sv["xc"][:, SSD_INNER:].T
    dxs, d_b, d_c, dz, ddt, d_asum, d_bias, d_dsk, d_gn = ssd_bwd(
        p, sv["xc"], bc_t, sv["y"], d_sin, sv["hprev"], sv["dtg"], sv["dtg_t"], sv["params"], w["ssd_norm"], "ssd_bwd")
    dxc = jnp.concatenate([dxs, d_b, d_c], axis=1)
    dxbc, d_convw, d_convb = conv_bwd(p, sv["cpre"], dxc, w["conv_w"], "conv_bwd")
    g["conv_w"] = d_convw[:SSD_CONV]
    g["conv_b"] = d_convb
    g["dt_bias"] = d_bias.reshape(1, SSD_HEADS)
    g["a_log"] = (d_asum * (-jnp.exp(sv["params"][2]))).reshape(1, SSD_HEADS)
    g["d_skip"] = jnp.sum(d_dsk.reshape(SSD_HEADS, HD), axis=1).reshape(1, SSD_HEADS)
    g["ssd_norm"] = d_gn
    merged_bwd = attn_merge_bwd(d_attn_o, sv["os"], sv["lses"], "attn_merge_bwd")
    dqs, dks, dvs = [], [], []
    for gi, d in enumerate(ATTN_DILATIONS):
        qs, ks, vs, lse = sv["streams"][gi]
        d_o = _to_streams(merged_bwd[gi], d)
        cterm = _to_streams(merged_bwd[3 + gi], d)
        dq, dk, dv = attn_bwd(qs, ks, vs, d_o, cterm, lse, gi, f"attn_bwd{gi}")
        dqs.append(_from_streams(dq, d))
        dks.append(_from_streams(dk, d))
        dvs.append(_from_streams(dv, d).astype(BF16))
    dqn = jnp.concatenate(dqs, axis=1)
    dkn = jnp.concatenate(dks, axis=1)
    dq, dk, d_qg, d_kg = qk_norm_bwd(p, dqn, dkn, w["q_gain"], w["k_gain"], "qk_norm_bwd")
    g["q_norm"] = jnp.sum(d_qg.reshape(N_ATTN_HEADS, HD), axis=0).reshape(1, HD)
    g["k_norm"] = jnp.sum(d_kg.reshape(N_ATTN_HEADS, HD), axis=0).reshape(1, HD)
    dp = jnp.concatenate([dq, dk] + dvs + [dz, dxbc, dga, dgs], axis=1)
    ddt_pad = jnp.pad(ddt.transpose(1, 0, 2).reshape(s, SSD_HEADS), ((0, 0), (0, DT_PAD - SSD_HEADS)))
    g["w_in_main"] = matmul_tn(sv["h"], dp, "dw_in", tn=512, ts=1024)
    g["w_in_dt"] = matmul_tn(sv["h"], ddt_pad, "dw_in_dt", tn=DT_PAD, ts=1024)
    dh_main = matmul_nt(dp, w["w_in_main"], "d_mix_h", F32, tm=1024, tn=1024, tk=512)
    dh_dt = matmul_nt(ddt_pad, w["w_in_dt"], "d_mix_h_dt", F32, tm=1024, tn=1024, tk=DT_PAD)
    dx, g["mix_norm"] = rms_bwd([dh_main, dh_dt], x, w["mix_norm"], dx_out, "mix_drms")
    return dx, g


ANY = pl.BlockSpec(memory_space=pl.ANY)


def _place():
    x, y, c = lax.axis_index("x"), lax.axis_index("y"), lax.axis_index("c")
    chips = [(1 - x, y), (x, 1 - y), (1 - x, 1 - y)]
    return x, y, c, 2 * x + y, chips


def _comm_call(body, *, name, out_shape, n_in, scratch_shapes, aliases=None):
    return pl.pallas_call(
        body, out_shape=out_shape, in_specs=[ANY] * n_in, out_specs=[ANY] * len(out_shape),
        scratch_shapes=scratch_shapes, input_output_aliases=aliases or {}, name=name,
        compiler_params=pltpu.CompilerParams(has_side_effects=True))


def gather_weights(shards, small):
    n = len(shards)
    halves = [a.shape[0] // 2 for a in shards]
    out_shape = [_sds((N_CHIP,) + a.shape, a.dtype) for a in shards] + [_sds((N_CHIP,) + small.shape, small.dtype)]

    def body(*refs):
        ins, outs = refs[:n + 1], refs[n + 1:2 * n + 2]
        send1, recv1, send2, recv2, local = refs[2 * n + 2:]
        x, y, c, me, chips = _place()
        sibling = (x, y, 1 - c)

        def rows(k, chip, core):
            if k == n:
                return outs[k].at[chip]
            return outs[k].at[chip, pl.ds(core * halves[k], halves[k])]

        def level1(k, t, incoming):
            chip = 2 * chips[t][0] + chips[t][1]
            src = ins[k] if k == n else ins[k].at[pl.ds(c * halves[k], halves[k])]
            return pltpu.make_async_remote_copy(
                src_ref=src, dst_ref=rows(k, chip if incoming else me, c), send_sem=send1.at[3 * k + t],
                recv_sem=recv1.at[3 * k + t], device_id=(*chips[t], c), device_id_type=MESH)

        def level2(k, t, incoming):
            chip = 2 * chips[t][0] + chips[t][1]
            core = (1 - c) if incoming else c
            return pltpu.make_async_remote_copy(
                src_ref=rows(k, chip, core), dst_ref=rows(k, chip, core), send_sem=send2.at[3 * k + t],
                recv_sem=recv2.at[3 * k + t], device_id=sibling, device_id_type=MESH)

        own = [pltpu.make_async_copy(ins[k], outs[k].at[me], local.at[k]) for k in range(n + 1)]
        for cp in own:
            cp.start()
        first = [level1(k, t, False) for k in range(n + 1) for t in range(3)]
        for cp in first:
            cp.start()
        passed = []
        for k in range(n + 1):
            for t in range(3):
                level1(k, t, True).wait_recv()
                if k < n:
                    cp = level2(k, t, False)
                    cp.start()
                    passed.append(cp)
        for k in range(n):
            for t in range(3):
                level2(k, t, True).wait_recv()
        for cp in first + passed:
            cp.wait_send()
        for cp in own:
            cp.wait()

    dma = pltpu.SemaphoreType.DMA
    return _comm_call(body, name="gather_weights", out_shape=out_shape, n_in=n + 1,
                      scratch_shapes=[dma((3 * n + 3,)), dma((3 * n + 3,)), dma((3 * n,)), dma((3 * n,)),
                                      dma((n + 1,))])(*shards, small)


def reduce_to_sibling(grads):
    n = len(grads)
    halves = [a.shape[1] // 2 for a in grads]
    shapes = [_sds((N_CHIP, h, a.shape[2]), a.dtype) for a, h in zip(grads, halves)]

    def body(*refs):
        ins, got, kept = refs[:n], refs[n:2 * n], refs[2 * n:3 * n]
        send, recv, local = refs[3 * n:]
        x, y, c, _, _ = _place()
        copies, locals_ = [], []
        for k in range(n):
            h = halves[k]
            locals_.append(pltpu.make_async_copy(ins[k].at[:, pl.ds(c * h, h)], kept[k], local.at[k]))
            copies.append(pltpu.make_async_remote_copy(
                src_ref=ins[k].at[:, pl.ds((1 - c) * h, h)], dst_ref=got[k], send_sem=send.at[k], recv_sem=recv.at[k],
                device_id=(x, y, 1 - c), device_id_type=MESH))
        for cp in locals_ + copies:
            cp.start()
        for cp in copies:
            cp.wait_recv()
        for cp in copies:
            cp.wait_send()
        for cp in locals_:
            cp.wait()

    dma = pltpu.SemaphoreType.DMA
    res = _comm_call(body, name="reduce_to_sibling", out_shape=shapes + shapes, n_in=n,
                     scratch_shapes=[dma((n,)), dma((n,)), dma((n,))])(*grads)
    return res[:n], res[n:]


def reduce_to_owner(sums):
    n = len(sums)
    shapes = [_sds(a.shape, a.dtype) for a in sums]

    def body(*refs):
        ins, outs = refs[:n], refs[n:2 * n]
        send, recv, local = refs[2 * n:]
        x, y, c, me, chips = _place()
        copies, locals_ = [], []
        for k in range(n):
            locals_.append(pltpu.make_async_copy(ins[k].at[me], outs[k].at[3], local.at[k]))
            for t in range(3):
                chip = 2 * chips[t][0] + chips[t][1]
                copies.append(pltpu.make_async_remote_copy(
                    src_ref=ins[k].at[chip], dst_ref=outs[k].at[t], send_sem=send.at[3 * k + t],
                    recv_sem=recv.at[3 * k + t], device_id=(*chips[t], c), device_id_type=MESH))
        for cp in locals_ + copies:
            cp.start()
        for cp in copies:
            cp.wait_recv()
        for cp in copies:
            cp.wait_send()
        for cp in locals_:
            cp.wait()

    dma = pltpu.SemaphoreType.DMA
    return _comm_call(body, name="reduce_to_owner", out_shape=shapes, n_in=n,
                      scratch_shapes=[dma((3 * n,)), dma((3 * n,)), dma((n,))])(*sums)


def share_with_sibling(halves_):
    n = len(halves_)
    shapes = [_sds((2 * a.shape[0], a.shape[1]), a.dtype) for a in halves_]

    def body(*refs):
        ins, outs = refs[:n], refs[n:2 * n]
        send, recv, local = refs[2 * n:]
        x, y, c, _, _ = _place()
        copies, locals_ = [], []
        for k in range(n):
            h = ins[k].shape[0]
            mine = outs[k].at[pl.ds(c * h, h)]
            locals_.append(pltpu.make_async_copy(ins[k], mine, local.at[k]))
            copies.append(pltpu.make_async_remote_copy(
                src_ref=ins[k], dst_ref=mine, send_sem=send.at[k], recv_sem=recv.at[k],
                device_id=(x, y, 1 - c), device_id_type=MESH))
        for cp in locals_ + copies:
            cp.start()
        for cp in copies:
            cp.wait_recv()
        for cp in copies:
            cp.wait_send()
        for cp in locals_:
            cp.wait()

    dma = pltpu.SemaphoreType.DMA
    return _comm_call(body, name="share_with_sibling", out_shape=shapes, n_in=n,
                      scratch_shapes=[dma((n,)), dma((n,)), dma((n,))])(*halves_)


N_DEV = 8
SMALL_ROWS = 24


def all_reduce_small(pack):
    def body(in_ref, out_ref, buf, send, recv):
        x, y, c, _, _ = _place()
        me = 4 * x + 2 * y + c
        buf[me] = in_ref[...]
        copies = []
        for r in range(1, N_DEV):
            px = (1 - x) if r & 4 else x
            py = (1 - y) if r & 2 else y
            pc = (1 - c) if r & 1 else c
            copies.append(pltpu.make_async_remote_copy(
                src_ref=buf.at[me], dst_ref=buf.at[me], send_sem=send.at[r - 1], recv_sem=recv.at[r - 1],
                device_id=(px, py, pc), device_id_type=MESH))
        for cp in copies:
            cp.start()
        for cp in copies:
            cp.wait_recv()
        for cp in copies:
            cp.wait_send()
        acc = buf[0]
        for j in range(1, N_DEV):
            acc = acc + buf[j]
        out_ref[...] = acc

    dma = pltpu.SemaphoreType.DMA
    vmem = pl.BlockSpec(memory_space=pltpu.VMEM)
    return pl.pallas_call(
        body, out_shape=_sds(pack.shape, F32), in_specs=[vmem], out_specs=vmem, name="all_reduce_small",
        scratch_shapes=[pltpu.VMEM((N_DEV,) + pack.shape, F32), dma((N_DEV - 1,)), dma((N_DEV - 1,))],
        compiler_params=pltpu.CompilerParams(has_side_effects=True))(pack)


def _row_tile(rows, limit, multiple):
    return max(t for t in range(multiple, min(rows, limit) + 1, multiple) if rows % t == 0)


def add_pair(a, b, name):
    _, h, c = a.shape

    def body(a_ref, b_ref, o_ref):
        o_ref[...] = (a_ref[...].astype(F32) + b_ref[...].astype(F32)).astype(o_ref.dtype)

    blk = pl.BlockSpec((None, h, c), lambda j: (j, 0, 0))
    return _call(body, name=name, out_shape=_sds(a.shape, a.dtype), grid=(N_CHIP,), in_specs=[blk, blk],
                 out_specs=blk)(a, b)


def sum_slots(buf, name):
    _, h, c = buf.shape
    tm = _row_tile(h, 256, 16)

    def body(b_ref, o_ref):
        acc = b_ref[3].astype(F32)
        for t in range(3):
            acc = acc + b_ref[t].astype(F32)
        o_ref[...] = acc

    return _call(body, name=name, out_shape=_sds((h, c), F32), grid=(h // tm,),
                 in_specs=[pl.BlockSpec((N_CHIP, tm, c), lambda i: (0, i, 0))],
                 out_specs=pl.BlockSpec((tm, c), lambda i: (i, 0)))(buf)


def adamw(w, g, m, v, name):
    r, c = w.shape
    tm = r if r < 8 else _row_tile(r, 128, 8)
    c1 = 1.0 - ADAM_B1 ** ADAM_STEP
    c2 = 1.0 - ADAM_B2 ** ADAM_STEP

    def body(w_ref, g_ref, m_ref, v_ref, d_ref, m2_ref, v2_ref):
        gv = g_ref[...]
        m2 = ADAM_B1 * m_ref[...] + (1.0 - ADAM_B1) * gv
        v2 = ADAM_B2 * v_ref[...] + (1.0 - ADAM_B2) * (gv * gv)
        m2_ref[...] = m2
        v2_ref[...] = v2
        d_ref[...] = -ADAM_LR * ((m2 / c1) / (jnp.sqrt(v2 / c2) + ADAM_EPS) + ADAM_WD * w_ref[...])

    blk = pl.BlockSpec((tm, c), lambda i: (i, 0))
    shp = _sds((r, c), F32)
    return _call(body, name=name, out_shape=[shp, shp, shp], grid=(r // tm,), in_specs=[blk] * 4,
                 out_specs=[blk] * 3)(w, g, m, v)


BIG = ("ffn1_w_gate", "ffn1_w_up", "ffn1_w_down", "w_in", "w_attn_branch", "w_ssd_branch", "w_out",
       "ffn2_w_gate", "ffn2_w_up", "ffn2_w_down")
SMALL = ("ffn1_norm", "mix_norm", "q_norm", "k_norm", "conv_b", "dt_bias", "a_log", "d_skip", "ssd_norm", "ffn2_norm")
WEIGHTS = ("ffn1_norm", "ffn1_w_gate", "ffn1_w_up", "ffn1_w_down", "mix_norm", "w_in", "q_norm", "k_norm", "conv_w",
           "conv_b", "dt_bias", "a_log", "d_skip", "ssd_norm", "w_attn_branch", "w_ssd_branch", "w_out", "ffn2_norm",
           "ffn2_w_gate", "ffn2_w_up", "ffn2_w_down")
CONV_SHARD = SSD_CONV_DIM // N_CHIP


def _pack_small(vals, conv_part):
    flat = [vals[k].reshape(-1) for k in SMALL]
    flat.append(jnp.zeros((SSD_CONV * SSD_CONV_DIM,), F32) if conv_part is None else conv_part.reshape(-1))
    flat = jnp.concatenate(flat)
    return jnp.pad(flat, (0, SMALL_ROWS * D_MODEL - flat.shape[0])).reshape(SMALL_ROWS, D_MODEL)


def _unpack_small(pack, like):
    flat = pack.reshape(-1)
    out, off = {}, 0
    for k in SMALL:
        n = like[k].size
        out[k] = flat[off:off + n].reshape(like[k].shape)
        off += n
    conv = flat[off:off + SSD_CONV * SSD_CONV_DIM].reshape(SSD_CONV, SSD_CONV_DIM)
    return out, conv


def _chip_major_cols(a):
    r = a.shape[0]
    return a.reshape(r, N_CHIP, -1).transpose(1, 0, 2)


def _from_chip_major_cols(a):
    return a.transpose(1, 0, 2).reshape(a.shape[1], -1)


def kernel(x, ffn1_norm, ffn1_w_gate, ffn1_w_up, ffn1_w_down, mix_norm, w_in, q_norm, k_norm, conv_w, conv_b, dt_bias, a_log, d_skip, ssd_norm, w_attn_branch, w_ssd_branch, w_out, ffn2_norm, ffn2_w_gate, ffn2_w_up, ffn2_w_down, loss_target, m_ffn1_norm, m_ffn1_w_gate, m_ffn1_w_up, m_ffn1_w_down, m_mix_norm, m_w_in, m_q_norm, m_k_norm, m_conv_w, m_conv_b, m_dt_bias, m_a_log, m_d_skip, m_ssd_norm, m_w_attn_branch, m_w_ssd_branch, m_w_out, m_ffn2_norm, m_ffn2_w_gate, m_ffn2_w_up, m_ffn2_w_down, v_ffn1_norm, v_ffn1_w_gate, v_ffn1_w_up, v_ffn1_w_down, v_mix_norm, v_w_in, v_q_norm, v_k_norm, v_conv_w, v_conv_b, v_dt_bias, v_a_log, v_d_skip, v_ssd_norm, v_w_attn_branch, v_w_ssd_branch, v_w_out, v_ffn2_norm, v_ffn2_w_gate, v_ffn2_w_up, v_ffn2_w_down):
    env = dict(locals())
    wts = {k: env[k] for k in WEIGHTS}
    moms = {k: env["m_" + k] for k in WEIGHTS}
    vars_ = {k: env["v_" + k] for k in WEIGHTS}
    x0 = x[0]
    target = loss_target[0]

    gathered = gather_weights([wts[k][0].astype(BF16) for k in BIG], conv_w[0])
    full = dict(zip(BIG, gathered[:-1]))
    w_in_full = _from_chip_major_cols(full["w_in"])
    mixer_w = dict(
        mix_norm=mix_norm,
        w_in_main=jnp.concatenate([w_in_full[:, :IN_DT0], w_in_full[:, IN_DT1:]], axis=1),
        w_in_dt=jnp.pad(w_in_full[:, IN_DT0:IN_DT1], ((0, 0), (0, DT_PAD - SSD_HEADS))),
        q_gain=jnp.tile(q_norm, (1, 2)), k_gain=jnp.tile(k_norm, (1, 2)),
        conv_w=_from_chip_major_cols(gathered[-1]), conv_b=conv_b, dt_bias=dt_bias, a_log=a_log, d_skip=d_skip,
        ssd_norm=ssd_norm, w_attn_branch=_from_chip_major_cols(full["w_attn_branch"]),
        w_ssd_branch=full["w_ssd_branch"].reshape(SSD_INNER, D_MODEL), w_out=full["w_out"].reshape(D_MODEL, D_MODEL))

    x1, saved1 = ffn_forward(x0, ffn1_norm, full["ffn1_w_gate"], full["ffn1_w_up"], full["ffn1_w_down"], "ffn1")
    x2, saved_mix = mixer_forward(x1, mixer_w)
    x3, saved2 = ffn_forward(x2, ffn2_norm, full["ffn2_w_gate"], full["ffn2_w_up"], full["ffn2_w_down"], "ffn2")
    dx3, sq = loss_grad(x3, target, "loss")
    loss = lax.psum(0.5 * jnp.sum(sq) / D_MODEL, ("x", "y", "c"))
    grads = {}
    dx2, grads["ffn2_norm"], grads["ffn2_w_gate"], grads["ffn2_w_up"], grads["ffn2_w_down"] = ffn_backward(
        dx3, x2, ffn2_norm, full["ffn2_w_gate"], full["ffn2_w_up"], full["ffn2_w_down"], saved2, "ffn2")
    dx1, gmix = mixer_backward(dx2, x1, mixer_w, saved_mix)
    dx0, grads["ffn1_norm"], grads["ffn1_w_gate"], grads["ffn1_w_up"], grads["ffn1_w_down"] = ffn_backward(
        dx1, x0, ffn1_norm, full["ffn1_w_gate"], full["ffn1_w_up"], full["ffn1_w_down"], saved1, "ffn1")
    for k in ("mix_norm", "q_norm", "k_norm", "conv_b", "dt_bias", "a_log", "d_skip", "ssd_norm"):
        grads[k] = gmix[k]
    g_in = jnp.concatenate([gmix["w_in_main"][:, :IN_DT0], gmix["w_in_dt"][:, :SSD_HEADS],
                            gmix["w_in_main"][:, IN_DT0:]], axis=1)
    grads["w_in"] = _chip_major_cols(g_in)
    grads["w_attn_branch"] = _chip_major_cols(gmix["w_attn_branch"])
    grads["w_ssd_branch"] = gmix["w_ssd_branch"].reshape(N_CHIP, -1, D_MODEL)
    grads["w_out"] = gmix["w_out"].reshape(N_CHIP, -1, D_MODEL)

    got, kept = reduce_to_sibling([grads[k] for k in BIG])
    chip_sums = [add_pair(a, b, f"chip_sum_{k}") for k, a, b in zip(BIG, kept, got)]
    slots = reduce_to_owner(chip_sums)
    reduced = share_with_sibling([sum_slots(b, f"owner_sum_{k}") for k, b in zip(BIG, slots)])
    small_sum = all_reduce_small(_pack_small(grads, gmix["conv_w"]))
    g_small, g_conv_full = _unpack_small(small_sum, wts)
    chip = 2 * lax.axis_index("x") + lax.axis_index("y")
    g_final = dict(g_small)
    g_final["conv_w"] = lax.dynamic_slice_in_dim(g_conv_full, chip * CONV_SHARD, CONV_SHARD, axis=1)[None]
    for k, g in zip(BIG, reduced):
        g_final[k] = g.reshape(wts[k].shape)

    delta, new_m, new_v = {}, {}, {}
    for k in BIG + ("conv_w",):
        shp = wts[k].shape
        two_d = shp[1:]
        d, m2, v2 = adamw(wts[k].reshape(two_d), g_final[k].reshape(two_d), moms[k].reshape(two_d),
                          vars_[k].reshape(two_d), f"adamw_{k}")
        delta[k], new_m[k], new_v[k] = d.reshape(shp), m2.reshape(shp), v2.reshape(shp)
    d, m2, v2 = adamw(_pack_small(wts, None), _pack_small(g_small, None), _pack_small(moms, None),
                      _pack_small(vars_, None), "adamw_small")
    for res, packed in ((delta, d), (new_m, m2), (new_v, v2)):
        res.update(_unpack_small(packed, wts)[0])

    return (loss, dx0[None], *[g_final[k] for k in WEIGHTS], *[delta[k] for k in WEIGHTS],
            *[new_m[k] for k in WEIGHTS], *[new_v[k] for k in WEIGHTS])
```

```python
import functools
import math

import jax
import jax.numpy as jnp
from jax import lax
from jax.experimental import pallas as pl
from jax.experimental.pallas import tpu as pltpu

F32 = jnp.float32
BF16 = jnp.bfloat16
MESH = pl.DeviceIdType.MESH

EPS = 1e-6
D_MODEL = 1024
D_FF = 2816
N_CHIP = 4
FF_SHARD = D_FF // N_CHIP
HD = 64
BLK = 128
ATTN_DILATIONS = (1, 4, 16)
HEADS_PER_PATTERN = 8
N_ATTN_HEADS = 24
ALIBI_MAX_EXP = 8.0
ATTN_QKV = 1536
GROUP_W = 512
SSD_INNER = 2048
SSD_HEADS = 32
SSD_GROUPS = 4
SSD_CONV = 4
SSD_CONV_DIM = 3072
IN_COLS = 11808
IN_DT0, IN_DT1 = 9728, 9760
COL_K, COL_V, COL_Z, COL_XBC, COL_GA, COL_GS, P_COLS = 1536, 3072, 4608, 6656, 9728, 10752, 11776
DT_PAD = 128

ADAM_LR, ADAM_B1, ADAM_B2, ADAM_EPS, ADAM_WD, ADAM_STEP = 0.001, 0.9, 0.999, 1e-08, 0.01, 10

V7X_VMEM_LIMIT = 56 * 1024 * 1024
NEG = -1e30


def _call(body, *, name, out_shape, in_specs, out_specs, grid=(), scratch_shapes=(), aliases=None):
    return pl.pallas_call(
        body, out_shape=out_shape, grid=grid, in_specs=in_specs, out_specs=out_specs,
        scratch_shapes=scratch_shapes, input_output_aliases=aliases or {}, name=name,
        compiler_params=pltpu.CompilerParams(dimension_semantics=("arbitrary",) * len(grid),
                                             vmem_limit_bytes=V7X_VMEM_LIMIT))


def _sds(shape, dtype):
    return jax.ShapeDtypeStruct(tuple(shape), dtype)


def _dot(a, b):
    return jnp.dot(a, b, preferred_element_type=F32)


def _dot_nt(a, b):
    return lax.dot_general(a, b, (((1,), (1,)), ((), ())), preferred_element_type=F32)


def _dot_tn(a, b):
    return lax.dot_general(a, b, (((0,), (0,)), ((), ())), preferred_element_type=F32)


def _dot_hi(a, b):
    return jnp.dot(a, b, preferred_element_type=F32, precision=lax.Precision.HIGHEST)


def _sigmoid(x):
    return 1.0 / (1.0 + jnp.exp(-x))


def _lane_first_half(shape):
    return lax.broadcasted_iota(jnp.int32, shape, len(shape) - 1) < HD


def _pair_sum(x, first):
    s_all = jnp.sum(x, axis=-1, keepdims=True)
    s_a = jnp.sum(jnp.where(first, x, 0.0), axis=-1, keepdims=True)
    return s_a, s_all - s_a


def _rowwise(name, fn, rows, consts, outs, accs=(), tm=512):
    n_rows = None
    in_arrays, in_specs = [], []
    for r in rows:
        if isinstance(r, tuple):
            arr, w, cb = r
            spec = pl.BlockSpec((tm, w), functools.partial(lambda i, cb: (i, cb), cb=cb))
        else:
            arr = r
            spec = pl.BlockSpec((tm, arr.shape[1]), lambda i: (i, 0))
        n_rows = arr.shape[0]
        in_arrays.append(arr)
        in_specs.append(spec)
    for c in consts:
        in_arrays.append(c)
        in_specs.append(pl.BlockSpec(c.shape, functools.partial(lambda i, n: (0,) * n, n=c.ndim)))
    out_shape = [_sds(s, d) for s, d in outs] + [_sds(s, d) for s, d in accs]
    out_specs = [pl.BlockSpec((tm, s[1]), lambda i: (i, 0)) for s, _ in outs]
    out_specs += [pl.BlockSpec(s, functools.partial(lambda i, n: (0,) * n, n=len(s))) for s, _ in accs]

    def body(*refs):
        fn(pl.program_id(0), *refs)

    res = _call(body, name=name, out_shape=out_shape, grid=(n_rows // tm,), in_specs=in_specs,
                out_specs=out_specs)(*in_arrays)
    return res


def rms_fwd(x, gain, name):
    def fn(i, x_ref, g_ref, h_ref):
        xv = x_ref[...]
        r = lax.rsqrt(jnp.mean(xv * xv, axis=-1, keepdims=True) + EPS)
        h_ref[...] = (xv * r * g_ref[...]).astype(h_ref.dtype)

    return _rowwise(name, fn, [x], [gain], [(x.shape, BF16)])[0]


def rms_bwd(dhs, x, gain, dx_in, name):
    n = len(dhs)

    def fn(i, *refs):
        dh_refs, (x_ref, dxin_ref, g_ref, dx_ref, dg_ref) = refs[:n], refs[n:]
        dh = dh_refs[0][...]
        for r in dh_refs[1:]:
            dh = dh + r[...]
        xv = x_ref[...]
        r = lax.rsqrt(jnp.mean(xv * xv, axis=-1, keepdims=True) + EPS)
        xn = xv * r
        dxn = dh * g_ref[...]
        dx_ref[...] = dxin_ref[...] + r * (dxn - xn * jnp.mean(dxn * xn, axis=-1, keepdims=True))

        @pl.when(i == 0)
        def _():
            dg_ref[...] = jnp.zeros_like(dg_ref)

        dg_ref[...] += jnp.sum(dh * xn, axis=0, keepdims=True)

    return _rowwise(name, fn, list(dhs) + [x, dx_in], [gain], [(x.shape, F32)], [((1, x.shape[1]), F32)])


def loss_grad(y, target, name):
    def fn(i, y_ref, t_ref, dy_ref, sq_ref):
        err = y_ref[...] - t_ref[...]
        dy_ref[...] = err * (1.0 / y_ref.shape[1])

        @pl.when(i == 0)
        def _():
            sq_ref[...] = jnp.zeros_like(sq_ref)

        sq_ref[...] += jnp.sum(err * err, axis=0, keepdims=True)

    return _rowwise(name, fn, [y, target], [], [(y.shape, F32)], [((1, y.shape[1]), F32)])


def matmul_nn(a, b, name, out_dtype, tm, tn, res=None, scale=1.0):
    s, k = a.shape
    n = b.shape[1]

    def body(*refs):
        if res is None:
            a_ref, b_ref, o_ref = refs
            o_ref[...] = _dot(a_ref[...], b_ref[...]).astype(o_ref.dtype)
        else:
            a_ref, b_ref, r_ref, o_ref = refs
            o_ref[...] = (r_ref[...] + scale * _dot(a_ref[...], b_ref[...])).astype(o_ref.dtype)

    in_specs = [pl.BlockSpec((tm, k), lambda i, j: (i, 0)), pl.BlockSpec((k, tn), lambda i, j: (0, j))]
    args = [a, b]
    if res is not None:
        in_specs.append(pl.BlockSpec((tm, tn), lambda i, j: (i, j)))
        args.append(res)
    return _call(body, name=name, out_shape=_sds((s, n), out_dtype), grid=(s // tm, n // tn), in_specs=in_specs,
                 out_specs=pl.BlockSpec((tm, tn), lambda i, j: (i, j)))(*args)


def matmul_nt(a, b, name, out_dtype, tm, tn, tk):
    s, k = a.shape
    n = b.shape[0]
    nk = k // tk

    def body(a_ref, b_ref, o_ref, acc_ref):
        kk = pl.program_id(2)

        @pl.when(kk == 0)
        def _():
            acc_ref[...] = jnp.zeros_like(acc_ref)

        acc_ref[...] += _dot_nt(a_ref[...].astype(BF16), b_ref[...])

        @pl.when(kk == nk - 1)
        def _():
            o_ref[...] = acc_ref[...].astype(o_ref.dtype)

    return _call(body, name=name, out_shape=_sds((s, n), out_dtype), grid=(s // tm, n // tn, nk),
                 in_specs=[pl.BlockSpec((tm, tk), lambda i, j, kk: (i, kk)),
                           pl.BlockSpec((tn, tk), lambda i, j, kk: (j, kk))],
                 out_specs=pl.BlockSpec((tm, tn), lambda i, j, kk: (i, j)),
                 scratch_shapes=[pltpu.VMEM((tm, tn), F32)])(a, b)


def matmul_tn(a, b, name, tn, ts, a_scale=None, b_scale=None):
    s, m = a.shape
    n = b.shape[1]
    ns = s // ts

    def body(a_ref, b_ref, o_ref, acc_ref):
        ss = pl.program_id(1)

        @pl.when(ss == 0)
        def _():
            acc_ref[...] = jnp.zeros_like(acc_ref)

        av, bv = a_ref[...], b_ref[...]
        if a_scale is not None:
            av = av * a_scale
        if b_scale is not None:
            bv = bv * b_scale
        acc_ref[...] += _dot_tn(av.astype(BF16), bv.astype(BF16))

        @pl.when(ss == ns - 1)
        def _():
            o_ref[...] = acc_ref[...].astype(o_ref.dtype)

    return _call(body, name=name, out_shape=_sds((m, n), BF16), grid=(n // tn, ns),
                 in_specs=[pl.BlockSpec((ts, m), lambda j, ss: (ss, 0)), pl.BlockSpec((ts, tn), lambda j, ss: (ss, j))],
                 out_specs=pl.BlockSpec((m, tn), lambda j, ss: (0, j)),
                 scratch_shapes=[pltpu.VMEM((m, tn), F32)])(a, b)


def ffn_up(h, w704, gate_blk, up_blk, name, tm=1024):
    s = h.shape[0]

    def body(h_ref, wg_ref, wu_ref, g_ref, u_ref, a_ref):
        hv = h_ref[...]
        g = _dot(hv, wg_ref[...])
        u = _dot(hv, wu_ref[...])
        g_ref[...] = g.astype(BF16)
        u_ref[...] = u.astype(BF16)
        a_ref[...] = (g * _sigmoid(g) * u).astype(BF16)

    ospec = pl.BlockSpec((None, tm, FF_SHARD), lambda j, i: (j, i, 0))
    shp = _sds((N_CHIP, s, FF_SHARD), BF16)
    return _call(body, name=name, out_shape=[shp, shp, shp], grid=(N_CHIP, s // tm),
                 in_specs=[pl.BlockSpec((tm, D_MODEL), lambda j, i: (i, 0)),
                           pl.BlockSpec((None, D_MODEL, FF_SHARD), lambda j, i: (j, gate_blk, 0)),
                           pl.BlockSpec((None, D_MODEL, FF_SHARD), lambda j, i: (j, up_blk, 0))],
                 out_specs=[ospec, ospec, ospec])(h, w704, w704)


def ffn_down(a, w1024, blk, x, name, tm=512):
    s = x.shape[0]

    def body(a_ref, wd_ref, x_ref, o_ref):
        acc = _dot(a_ref[0], wd_ref[0])
        for j in range(1, N_CHIP):
            acc += _dot(a_ref[j], wd_ref[j])
        o_ref[...] = x_ref[...] + 0.5 * acc

    return _call(body, name=name, out_shape=_sds((s, D_MODEL), F32), grid=(s // tm,),
                 in_specs=[pl.BlockSpec((N_CHIP, tm, FF_SHARD), lambda i: (0, i, 0)),
                           pl.BlockSpec((N_CHIP, FF_SHARD, D_MODEL), lambda i: (0, blk, 0)),
                           pl.BlockSpec((tm, D_MODEL), lambda i: (i, 0))],
                 out_specs=pl.BlockSpec((tm, D_MODEL), lambda i: (i, 0)))(a, w1024, x)


def ffn_bwd_hidden(dx, w1024, blk, g, u, name, tm=1024):
    s = dx.shape[0]

    def body(dx_ref, wd_ref, g_ref, u_ref, dg_ref, du_ref):
        dy = (0.5 * dx_ref[...]).astype(BF16)
        da = _dot_nt(dy, wd_ref[...])
        gv = g_ref[...].astype(F32)
        uv = u_ref[...].astype(F32)
        sg = _sigmoid(gv)
        dg_ref[...] = (da * uv * (sg * (1.0 + gv * (1.0 - sg)))).astype(BF16)
        du_ref[...] = (da * gv * sg).astype(BF16)

    hspec = pl.BlockSpec((None, tm, FF_SHARD), lambda j, i: (j, i, 0))
    shp = _sds((N_CHIP, s, FF_SHARD), BF16)
    return _call(body, name=name, out_shape=[shp, shp], grid=(N_CHIP, s // tm),
                 in_specs=[pl.BlockSpec((tm, D_MODEL), lambda j, i: (i, 0)),
                           pl.BlockSpec((None, FF_SHARD, D_MODEL), lambda j, i: (j, blk, 0)), hspec, hspec],
                 out_specs=[hspec, hspec])(dx, w1024, g, u)


def ffn_bwd_input(dg, du, w704, gate_blk, up_blk, name, tm=512):
    s = dg.shape[1]

    def body(dg_ref, du_ref, wg_ref, wu_ref, o_ref):
        acc = _dot_nt(dg_ref[0], wg_ref[0]) + _dot_nt(du_ref[0], wu_ref[0])
        for j in range(1, N_CHIP):
            acc += _dot_nt(dg_ref[j], wg_ref[j]) + _dot_nt(du_ref[j], wu_ref[j])
        o_ref[...] = acc

    hspec = pl.BlockSpec((N_CHIP, tm, FF_SHARD), lambda i: (0, i, 0))
    return _call(body, name=name, out_shape=_sds((s, D_MODEL), F32), grid=(s // tm,),
                 in_specs=[hspec, hspec, pl.BlockSpec((N_CHIP, D_MODEL, FF_SHARD), lambda i: (0, gate_blk, 0)),
                           pl.BlockSpec((N_CHIP, D_MODEL, FF_SHARD), lambda i: (0, up_blk, 0))],
                 out_specs=pl.BlockSpec((tm, D_MODEL), lambda i: (i, 0)))(dg, du, w704, w704)


def ffn_wgrad_in(h, dgu, name, ts=1024):
    s = h.shape[0]
    ns = s // ts

    def body(h_ref, d_ref, o_ref, acc_ref):
        ss = pl.program_id(1)

        @pl.when(ss == 0)
        def _():
            acc_ref[...] = jnp.zeros_like(acc_ref)

        acc_ref[...] += _dot_tn(h_ref[...], d_ref[...])

        @pl.when(ss == ns - 1)
        def _():
            o_ref[...] = acc_ref[...].astype(BF16)

    return _call(body, name=name, out_shape=_sds((N_CHIP, D_MODEL, FF_SHARD), BF16), grid=(N_CHIP, ns),
                 in_specs=[pl.BlockSpec((ts, D_MODEL), lambda j, ss: (ss, 0)),
                           pl.BlockSpec((None, ts, FF_SHARD), lambda j, ss: (j, ss, 0))],
                 out_specs=pl.BlockSpec((None, D_MODEL, FF_SHARD), lambda j, ss: (j, 0, 0)),
                 scratch_shapes=[pltpu.VMEM((D_MODEL, FF_SHARD), F32)])(h, dgu)


def ffn_wgrad_down(a, dx, name, ts=1024):
    s = dx.shape[0]
    ns = s // ts

    def body(a_ref, dx_ref, o_ref, acc_ref):
        ss = pl.program_id(1)

        @pl.when(ss == 0)
        def _():
            acc_ref[...] = jnp.zeros_like(acc_ref)

        acc_ref[...] += _dot_tn(a_ref[...], (0.5 * dx_ref[...]).astype(BF16))

        @pl.when(ss == ns - 1)
        def _():
            o_ref[...] = acc_ref[...].astype(BF16)

    return _call(body, name=name, out_shape=_sds((N_CHIP, FF_SHARD, D_MODEL), BF16), grid=(N_CHIP, ns),
                 in_specs=[pl.BlockSpec((None, ts, FF_SHARD), lambda j, ss: (j, ss, 0)),
                           pl.BlockSpec((ts, D_MODEL), lambda j, ss: (ss, 0))],
                 out_specs=pl.BlockSpec((None, FF_SHARD, D_MODEL), lambda j, ss: (j, 0, 0)),
                 scratch_shapes=[pltpu.VMEM((FF_SHARD, D_MODEL), F32)])(a, dx)


def ffn_forward(x, gain, w704, w1024, layer, tag):
    h = rms_fwd(x, gain, f"{tag}_rms")
    g, u, a = ffn_up(h, w704, 2 * layer, 2 * layer + 1, f"{tag}_up")
    y = ffn_down(a, w1024, layer, x, f"{tag}_down")
    return y, (h, g, u, a)


def ffn_backward(dy, x, gain, w704, w1024, layer, saved, tag):
    h, g, u, a = saved
    d_wd = ffn_wgrad_down(a, dy, f"{tag}_dwd")
    dg, du = ffn_bwd_hidden(dy, w1024, layer, g, u, f"{tag}_dhid")
    d_wg = ffn_wgrad_in(h, dg, f"{tag}_dwg")
    d_wu = ffn_wgrad_in(h, du, f"{tag}_dwu")
    dh = ffn_bwd_input(dg, du, w704, 2 * layer, 2 * layer + 1, f"{tag}_dh")
    dx, d_gain = rms_bwd([dh], x, gain, dy, f"{tag}_drms")
    return dx, d_gain, d_wg, d_wu, d_wd


def _alibi_slope(head):
    return float(2.0 ** (-ALIBI_MAX_EXP * (head + 1) / N_ATTN_HEADS))


def _head_norm(t, gain_pair, first):
    sa, sb = _pair_sum(t * t, first)
    r = jnp.where(first, lax.rsqrt(sa * (1.0 / HD) + EPS), lax.rsqrt(sb * (1.0 / HD) + EPS))
    return t * r * gain_pair, r


def qk_norm_fwd(p, q_gain, k_gain, name):
    s = p.shape[0]

    def fn(i, q_ref, k_ref, qg_ref, kg_ref, qn_ref, kn_ref):
        first = _lane_first_half((q_ref.shape[0], 2 * HD))
        for src, g_ref, dst in ((q_ref, qg_ref, qn_ref), (k_ref, kg_ref, kn_ref)):
            for pr in range(ATTN_QKV // (2 * HD)):
                cols = slice(pr * 2 * HD, (pr + 1) * 2 * HD)
                y, _ = _head_norm(src[:, cols].astype(F32), g_ref[...], first)
                dst[:, cols] = y.astype(BF16)

    return _rowwise(name, fn, [(p, ATTN_QKV, 0), (p, ATTN_QKV, 1)], [q_gain, k_gain],
                    [((s, ATTN_QKV), BF16), ((s, ATTN_QKV), BF16)])


def qk_norm_bwd(p, dqn, dkn, q_gain, k_gain, name):
    s = p.shape[0]

    def fn(i, q_ref, k_ref, dqn_ref, dkn_ref, qg_ref, kg_ref, dq_ref, dk_ref, dqg_ref, dkg_ref):
        first = _lane_first_half((q_ref.shape[0], 2 * HD))

        @pl.when(i == 0)
        def _():
            dqg_ref[...] = jnp.zeros_like(dqg_ref)
            dkg_ref[...] = jnp.zeros_like(dkg_ref)

        for src, d_ref, g_ref, dst, dg_ref in ((q_ref, dqn_ref, qg_ref, dq_ref, dqg_ref),
                                               (k_ref, dkn_ref, kg_ref, dk_ref, dkg_ref)):
            for pr in range(ATTN_QKV // (2 * HD)):
                cols = slice(pr * 2 * HD, (pr + 1) * 2 * HD)
                t = src[:, cols].astype(F32)
                sa, sb = _pair_sum(t * t, first)
                r = jnp.where(first, lax.rsqrt(sa * (1.0 / HD) + EPS), lax.rsqrt(sb * (1.0 / HD) + EPS))
                xn = t * r
                dy = d_ref[:, cols]
                dg_ref[:, cols] += jnp.sum(dy * xn, axis=0, keepdims=True)
                dxn = dy * g_ref[...]
                ma, mb = _pair_sum(dxn * xn, first)
                mean = jnp.where(first, ma, mb) * (1.0 / HD)
                dst[:, cols] = (r * (dxn - xn * mean)).astype(BF16)

    return _rowwise(name, fn, [(p, ATTN_QKV, 0), (p, ATTN_QKV, 1), dqn, dkn], [q_gain, k_gain],
                    [((s, ATTN_QKV), BF16), ((s, ATTN_QKV), BF16)], [((1, ATTN_QKV), F32), ((1, ATTN_QKV), F32)])


def _to_streams(a, d):
    if d == 1:
        return a
    s, c = a.shape
    return a.reshape(s // d, d, c).transpose(1, 0, 2).reshape(s, c)


def _from_streams(a, d):
    if d == 1:
        return a
    s, c = a.shape
    return a.reshape(d, s // d, c).transpose(1, 0, 2).reshape(s, c)


def _attn_masks():
    row = lax.broadcasted_iota(jnp.int32, (BLK, BLK), 0)
    col = lax.broadcasted_iota(jnp.int32, (BLK, BLK), 1)
    rel_diag = row - col
    rel_prev = rel_diag + BLK
    return rel_diag, rel_prev


def attn_fwd(q, k, v, pattern, name, tq=512):
    s = q.shape[0]
    d = ATTN_DILATIONS[pattern]
    blocks_per_stream = (s // d) // BLK
    nsb = tq // BLK

    def body(q_ref, k_ref, v_ref, kp_ref, vp_ref, o_ref, l_ref):
        i = pl.program_id(0)
        rel_diag, rel_prev = _attn_masks()
        first = _lane_first_half((BLK, 2 * HD))
        rd_f = (rel_diag * d).astype(F32)
        rp_f = (rel_prev * d).astype(F32)
        for sb in range(nsb):
            rows = slice(sb * BLK, (sb + 1) * BLK)
            has_prev = ((i * nsb + sb) % blocks_per_stream != 0).astype(jnp.int32)
            m_diag = rel_diag >= 0
            m_prev = (rel_prev + (1 - has_prev) * (4 * BLK)) <= BLK
            for pr in range(GROUP_W // (2 * HD)):
                cols = slice(pr * 2 * HD, (pr + 1) * 2 * HD)
                qp = q_ref[rows, cols]
                kc, vc = k_ref[rows, cols], v_ref[rows, cols]
                if sb == 0:
                    kp, vp = kp_ref[:, cols], vp_ref[:, cols]
                else:
                    prows = slice((sb - 1) * BLK, sb * BLK)
                    kp, vp = k_ref[prows, cols], v_ref[prows, cols]
                outs, lses = [], []
                for e in range(2):
                    slope = _alibi_slope(pattern * HEADS_PER_PATTERN + 2 * pr + e)
                    qm = jnp.where(first if e == 0 else jnp.logical_not(first), qp, jnp.zeros_like(qp))
                    s1 = jnp.where(m_diag, _dot_nt(qm, kc) * 0.125 - slope * rd_f, NEG)
                    s0 = jnp.where(m_prev, _dot_nt(qm, kp) * 0.125 - slope * rp_f, NEG)
                    m = jnp.maximum(jnp.max(s1, axis=-1, keepdims=True), jnp.max(s0, axis=-1, keepdims=True))
                    p1 = jnp.exp(s1 - m)
                    p0 = jnp.exp(s0 - m)
                    l = jnp.sum(p1, axis=-1, keepdims=True) + jnp.sum(p0, axis=-1, keepdims=True)
                    inv = 1.0 / l
                    outs.append(_dot((p1 * inv).astype(BF16), vc) + _dot((p0 * inv).astype(BF16), vp))
                    lses.append(m + jnp.log(l))
                o_ref[rows, cols] = jnp.where(first, outs[0], outs[1])
                l_ref[rows, cols] = jnp.where(first, lses[0], lses[1])

    cur = pl.BlockSpec((tq, GROUP_W), lambda i: (i, 0))
    prev = pl.BlockSpec((BLK, GROUP_W), lambda i: (jnp.maximum(i * nsb - 1, 0), 0))
    return _call(body, name=name, out_shape=[_sds((s, GROUP_W), F32), _sds((s, GROUP_W), F32)], grid=(s // tq,),
                 in_specs=[cur, cur, cur, prev, prev], out_specs=[cur, cur])(q, k, v, k, v)


def attn_merge_fwd(os_, lses, name):
    s = os_[0].shape[0]

    def fn(i, o0, o1, o2, l0, l1, l2, out_ref):
        m = jnp.maximum(jnp.maximum(l0[...], l1[...]), l2[...])
        e0, e1, e2 = jnp.exp(l0[...] - m), jnp.exp(l1[...] - m), jnp.exp(l2[...] - m)
        inv = 1.0 / (e0 + e1 + e2)
        out_ref[...] = ((e0 * inv) * o0[...] + (e1 * inv) * o1[...] + (e2 * inv) * o2[...]).astype(BF16)

    return _rowwise(name, fn, list(os_) + list(lses), [], [((s, GROUP_W), BF16)])[0]


def attn_merge_bwd(d_out, os_, lses, name):
    s = d_out.shape[0]

    def fn(i, do_ref, o0, o1, o2, l0, l1, l2, d0, d1, d2, c0, c1, c2):
        first = _lane_first_half((do_ref.shape[0], 2 * HD))
        m = jnp.maximum(jnp.maximum(l0[...], l1[...]), l2[...])
        e0, e1, e2 = jnp.exp(l0[...] - m), jnp.exp(l1[...] - m), jnp.exp(l2[...] - m)
        inv = 1.0 / (e0 + e1 + e2)
        w0, w1, w2 = e0 * inv, e1 * inv, e2 * inv
        do = do_ref[...]
        prod = do * (w0 * o0[...] + w1 * o1[...] + w2 * o2[...])
        for pr in range(GROUP_W // (2 * HD)):
            cols = slice(pr * 2 * HD, (pr + 1) * 2 * HD)
            ta, tb = _pair_sum(prod[:, cols], first)
            t = jnp.where(first, ta, tb)
            for w, c_ref in ((w0, c0), (w1, c1), (w2, c2)):
                c_ref[:, cols] = w[:, cols] * t
        for w, d_ref in ((w0, d0), (w1, d1), (w2, d2)):
            d_ref[...] = (w * do).astype(BF16)

    shp = (s, GROUP_W)
    return _rowwise(name, fn, [d_out] + list(os_) + list(lses), [],
                    [(shp, BF16)] * 3 + [(shp, F32)] * 3)


def attn_bwd(q, k, v, d_o, cterm, lse, pattern, name, tq=512):
    s = q.shape[0]
    d = ATTN_DILATIONS[pattern]
    blocks_per_stream = (s // d) // BLK
    nsb = tq // BLK
    n_blocks = s // BLK

    def body(q_ref, k_ref, v_ref, do_ref, c_ref, l_ref, kp_ref, vp_ref, qn_ref, don_ref, cn_ref, ln_ref,
             dq_ref, dk_ref, dv_ref):
        i = pl.program_id(0)
        rel_diag, rel_prev = _attn_masks()
        first = _lane_first_half((BLK, 2 * HD))
        second = jnp.logical_not(first)
        rd_f = (rel_diag * d).astype(F32)
        rp_f = (rel_prev * d).astype(F32)
        m_diag = rel_diag >= 0
        dq_ref[...] = jnp.zeros_like(dq_ref)
        dk_ref[...] = jnp.zeros_like(dk_ref)
        dv_ref[...] = jnp.zeros_like(dv_ref)

        def pair(qp, dop, cp, lp, kp, vp, rel_f, mask):
            dq = dk = dv = None
            for e in range(2):
                lanes = first if e == 0 else second
                slope = slopes[e]
                qm = jnp.where(lanes, qp, jnp.zeros_like(qp))
                dom = jnp.where(lanes, dop, jnp.zeros_like(dop))
                km = jnp.where(lanes, kp, jnp.zeros_like(kp))
                sc = jnp.where(mask, _dot_nt(qm, kp) * 0.125 - slope * rel_f, NEG)
                pm = jnp.exp(sc - lp[:, e * HD:e * HD + 1])
                dl = pm * (_dot_nt(dom, vp) - cp[:, e * HD:e * HD + 1])
                dl16 = dl.astype(BF16)
                t_dq = _dot(dl16, km)
                t_dk = _dot_tn(dl16, qm)
                t_dv = _dot_tn(pm.astype(BF16), dom)
                dq = t_dq if dq is None else dq + t_dq
                dk = t_dk if dk is None else dk + t_dk
                dv = t_dv if dv is None else dv + t_dv
            return dq * 0.125, dk * 0.125, dv

        for pr in range(GROUP_W // (2 * HD)):
            cols = slice(pr * 2 * HD, (pr + 1) * 2 * HD)
            slopes = [_alibi_slope(pattern * HEADS_PER_PATTERN + 2 * pr + e) for e in range(2)]
            for sb in range(nsb + 1):
                gb = i * nsb + sb
                if sb < nsb:
                    rows = slice(sb * BLK, (sb + 1) * BLK)
                    qp, dop, cp, lp = q_ref[rows, cols], do_ref[rows, cols], c_ref[rows, cols], l_ref[rows, cols]
                else:
                    qp, dop, cp, lp = qn_ref[:, cols], don_ref[:, cols], cn_ref[:, cols], ln_ref[:, cols]
                if sb < nsb:
                    dq1, dk1, dv1 = pair(qp, dop, cp, lp, k_ref[rows, cols], v_ref[rows, cols], rd_f, m_diag)
                    dq_ref[rows, cols] += dq1
                    dk_ref[rows, cols] += dk1
                    dv_ref[rows, cols] += dv1
                valid = jnp.logical_and(gb % blocks_per_stream != 0, gb < n_blocks).astype(jnp.int32)
                m_prev = jnp.logical_and(rel_prev <= BLK, (rel_prev + (1 - valid) * (4 * BLK)) <= BLK)
                if sb == 0:
                    kp, vp = kp_ref[:, cols], vp_ref[:, cols]
                else:
                    prows = slice((sb - 1) * BLK, sb * BLK)
                    kp, vp = k_ref[prows, cols], v_ref[prows, cols]
                dq0, dk0, dv0 = pair(qp, dop, cp, lp, kp, vp, rp_f, m_prev)
                if sb < nsb:
                    dq_ref[rows, cols] += dq0
                if sb > 0:
                    dk_ref[prows, cols] += dk0
                    dv_ref[prows, cols] += dv0

    cur = pl.BlockSpec((tq, GROUP_W), lambda i: (i, 0))
    prev = pl.BlockSpec((BLK, GROUP_W), lambda i: (jnp.maximum(i * nsb - 1, 0), 0))
    nxt = pl.BlockSpec((BLK, GROUP_W), lambda i: (jnp.minimum((i + 1) * nsb, n_blocks - 1), 0))
    shp = _sds((s, GROUP_W), F32)
    return _call(body, name=name, out_shape=[shp, shp, shp], grid=(s // tq,),
                 in_specs=[cur] * 6 + [prev, prev] + [nxt] * 4, out_specs=[cur, cur, cur])(
                     q, k, v, d_o, cterm, lse, k, v, q, d_o, cterm, lse)


HALO = 16
CONV_TQ = 512


def conv_fwd(p, w, b, name):
    s = p.shape[0]
    tq = CONV_TQ
    ncol = SSD_CONV_DIM // GROUP_W
    cb0 = COL_XBC // GROUP_W

    def body(u_ref, up_ref, w_ref, b_ref, c_ref, xc_ref):
        i = pl.program_id(0)
        prev = up_ref[...].astype(F32) * (i > 0).astype(F32)
        ext = jnp.concatenate([prev, u_ref[...].astype(F32)], axis=0)
        acc = b_ref[...] + w_ref[SSD_CONV - 1:SSD_CONV, :] * ext[HALO:HALO + tq]
        for kk in range(SSD_CONV - 1):
            off = HALO - (SSD_CONV - 1) + kk
            acc += w_ref[kk:kk + 1, :] * ext[off:off + tq]
        c_ref[...] = acc.astype(BF16)
        xc_ref[...] = (acc * _sigmoid(acc)).astype(BF16)

    cur_in = pl.BlockSpec((tq, GROUP_W), lambda i, j: (i, cb0 + j))
    prev_in = pl.BlockSpec((HALO, GROUP_W), lambda i, j: (jnp.maximum(i * (tq // HALO) - 1, 0), cb0 + j))
    cur_out = pl.BlockSpec((tq, GROUP_W), lambda i, j: (i, j))
    shp = _sds((s, SSD_CONV_DIM), BF16)
    return _call(body, name=name, out_shape=[shp, shp], grid=(s // tq, ncol),
                 in_specs=[cur_in, prev_in, pl.BlockSpec((SSD_CONV, GROUP_W), lambda i, j: (0, j)),
                           pl.BlockSpec((1, GROUP_W), lambda i, j: (0, j))],
                 out_specs=[cur_out, cur_out])(p, p, w, b)


def conv_bwd(p, cpre, dxc, w, name):
    s = p.shape[0]
    tq = CONV_TQ
    ncol = SSD_CONV_DIM // GROUP_W
    cb0 = COL_XBC // GROUP_W
    nt = s // tq

    def body(u_ref, up_ref, c_ref, cn_ref, d_ref, dn_ref, w_ref, du_ref, dw_ref, db_ref):
        i = pl.program_id(1)

        def dpre(c16, dx):
            c = c16.astype(F32)
            sg = _sigmoid(c)
            return dx * (sg * (1.0 + c * (1.0 - sg)))

        dc = dpre(c_ref[...], d_ref[...])
        dcn = dpre(cn_ref[...], dn_ref[...]) * (i < nt - 1).astype(F32)
        dext = jnp.concatenate([dc, dcn], axis=0)
        prev = up_ref[...].astype(F32) * (i > 0).astype(F32)
        uext = jnp.concatenate([prev, u_ref[...].astype(F32)], axis=0)

        @pl.when(i == 0)
        def _():
            dw_ref[...] = jnp.zeros_like(dw_ref)
            db_ref[...] = jnp.zeros_like(db_ref)

        du = w_ref[SSD_CONV - 1:SSD_CONV, :] * dc
        for kk in range(SSD_CONV - 1):
            sh = SSD_CONV - 1 - kk
            du += w_ref[kk:kk + 1, :] * dext[sh:sh + tq]
        du_ref[...] = du.astype(BF16)
        for kk in range(SSD_CONV):
            off = HALO - (SSD_CONV - 1) + kk
            dw_ref[kk:kk + 1, :] += jnp.sum(dc * uext[off:off + tq], axis=0, keepdims=True)
        db_ref[...] += jnp.sum(dc, axis=0, keepdims=True)

    hb = tq // HALO
    cur_p = pl.BlockSpec((tq, GROUP_W), lambda j, i: (i, cb0 + j))
    prev_p = pl.BlockSpec((HALO, GROUP_W), lambda j, i: (jnp.maximum(i * hb - 1, 0), cb0 + j))
    cur = pl.BlockSpec((tq, GROUP_W), lambda j, i: (i, j))
    nxt = pl.BlockSpec((HALO, GROUP_W), lambda j, i: (jnp.minimum((i + 1) * hb, s // HALO - 1), j))
    return _call(body, name=name,
                 out_shape=[_sds((s, SSD_CONV_DIM), BF16), _sds((8, SSD_CONV_DIM), F32), _sds((1, SSD_CONV_DIM), F32)],
                 grid=(ncol, nt),
                 in_specs=[cur_p, prev_p, cur, nxt, cur, nxt, pl.BlockSpec((SSD_CONV, GROUP_W), lambda j, i: (0, j))],
                 out_specs=[cur, pl.BlockSpec((8, GROUP_W), lambda j, i: (0, j)),
                            pl.BlockSpec((1, GROUP_W), lambda j, i: (0, j))])(p, p, cpre, cpre, dxc, dxc, w)


def _softplus(x):
    return jnp.maximum(x, 0.0) + jnp.log(1.0 + jnp.exp(-jnp.abs(x)))


def _ssd_decays(dtr_ref, dtrt_ref, bias_ref, biast_ref, alog_ref, alogt_ref):
    row = lax.broadcasted_iota(jnp.int32, (BLK, BLK), 0)
    col = lax.broadcasted_iota(jnp.int32, (BLK, BLK), 1)
    lower = (row >= col).astype(F32)
    upper = (row <= col).astype(F32)
    dtb = dtr_ref[...] + bias_ref[...]
    dt = _softplus(dtb)
    a = dt * (-jnp.exp(alog_ref[...]))
    cs = _dot_hi(lower, a)
    a_t = _softplus(dtrt_ref[...] + biast_ref[...]) * (-jnp.exp(alogt_ref[...]))
    cs_t = _dot_hi(a_t, upper)
    return dtb, dt, cs, cs_t, row, col, upper


def ssd_fwd(p, xc, dtg, dtg_t, params, gn, name):
    s = p.shape[0]
    nc = s // BLK
    bias, bias_t, alog, alog_t, dskip = params

    def body(xs_ref, b_ref, c_ref, z_ref, dtr_ref, dtrt_ref, bias_ref, biast_ref, alog_ref, alogt_ref, dsk_ref,
             gn_ref, y_ref, sin_ref, hp_ref, h_ref):
        c_idx = pl.program_id(1)

        @pl.when(c_idx == 0)
        def _():
            h_ref[...] = jnp.zeros_like(h_ref)

        _, dt, cs, cs_t, row, col, _ = _ssd_decays(dtr_ref, dtrt_ref, bias_ref, biast_ref, alog_ref, alogt_ref)
        first = _lane_first_half((BLK, 2 * HD))
        first_row = _lane_first_half((1, 2 * HD))
        tril = row >= col
        b16, c16 = b_ref[...], c_ref[...]
        cb = _dot_nt(c16, b16)
        ys = []
        for pr in range(GROUP_W // (2 * HD)):
            cols = slice(pr * 2 * HD, (pr + 1) * 2 * HD)
            ha, hb = 2 * pr, 2 * pr + 1
            xs = xs_ref[:, cols].astype(F32)
            dt_pair = jnp.where(first, dt[:, ha:ha + 1], dt[:, hb:hb + 1])
            xt = xs * dt_pair
            xt16 = xt.astype(BF16)
            y_heads = []
            for h in (ha, hb):
                decay = jnp.exp(jnp.where(tril, cs[:, h:h + 1] - cs_t[h:h + 1, :], NEG))
                y_heads.append(_dot((cb * decay).astype(BF16), xt16))
            y_diag = jnp.where(first, y_heads[0], y_heads[1])
            hstate = h_ref[pr]
            hp_ref[pr] = hstate
            e_pair = jnp.where(first, jnp.exp(cs[:, ha:ha + 1]), jnp.exp(cs[:, hb:hb + 1]))
            y_off = e_pair * _dot(c16, hstate.astype(BF16))
            tot_a, tot_b = cs[BLK - 1:BLK, ha:ha + 1], cs[BLK - 1:BLK, hb:hb + 1]
            f_pair = jnp.where(first, jnp.exp(tot_a - cs[:, ha:ha + 1]), jnp.exp(tot_b - cs[:, hb:hb + 1]))
            new = _dot_tn(b16, (f_pair * xt).astype(BF16))
            dec = jnp.where(first_row, jnp.exp(tot_a), jnp.exp(tot_b))
            h_ref[pr] = dec * hstate + new
            d_pair = jnp.where(first_row, dsk_ref[:, ha:ha + 1], dsk_ref[:, hb:hb + 1])
            ys.append(y_diag + y_off + xs * d_pair)
        y = jnp.concatenate(ys, axis=1)
        y_ref[...] = y
        zv = z_ref[...].astype(F32)
        yz = y * (zv * _sigmoid(zv))
        r = lax.rsqrt(jnp.mean(yz * yz, axis=-1, keepdims=True) + EPS)
        sin_ref[...] = (yz * r * gn_ref[...]).astype(BF16)

    nb0 = SSD_INNER // BLK
    gparam = pl.BlockSpec((None, 1, 8), lambda g, c: (g, 0, 0))
    gparam_t = pl.BlockSpec((None, 8, 1), lambda g, c: (g, 0, 0))
    return _call(
        body, name=name,
        out_shape=[_sds((s, SSD_INNER), F32), _sds((s, SSD_INNER), BF16),
                   _sds((SSD_GROUPS, nc, 4, BLK, 2 * HD), F32)],
        grid=(SSD_GROUPS, nc),
        in_specs=[pl.BlockSpec((BLK, GROUP_W), lambda g, c: (c, g)),
                  pl.BlockSpec((BLK, BLK), lambda g, c: (c, nb0 + g)),
                  pl.BlockSpec((BLK, BLK), lambda g, c: (c, nb0 + SSD_GROUPS + g)),
                  pl.BlockSpec((BLK, GROUP_W), lambda g, c: (c, COL_Z // GROUP_W + g)),
                  pl.BlockSpec((None, BLK, 8), lambda g, c: (g, c, 0)),
                  pl.BlockSpec((None, 8, BLK), lambda g, c: (g, 0, c)),
                  gparam, gparam_t, gparam, gparam_t, gparam,
                  pl.BlockSpec((1, GROUP_W), lambda g, c: (0, g))],
        out_specs=[pl.BlockSpec((BLK, GROUP_W), lambda g, c: (c, g)),
                   pl.BlockSpec((BLK, GROUP_W), lambda g, c: (c, g)),
                   pl.BlockSpec((None, None, 4, BLK, 2 * HD), lambda g, c: (g, c, 0, 0, 0))],
        scratch_shapes=[pltpu.VMEM((4, BLK, 2 * HD), F32)],
    )(xc, xc, xc, p, dtg, dtg_t, bias, bias_t, alog, alog_t, dskip, gn)


def ssd_bwd(p, xc, bc_t, y, d_sin, hprev, dtg, dtg_t, params, gn, name):
    s = p.shape[0]
    nc = s // BLK
    bias, bias_t, alog, alog_t, dskip = params

    def body(xs_ref, b_ref, c_ref, bt_ref, ct_ref, z_ref, y_ref, dsin_ref, hp_ref, dtr_ref, dtrt_ref, bias_ref,
             biast_ref, alog_ref, alogt_ref, dsk_ref, gn_ref,
             dxs_ref, db_ref, dc_ref, dz_ref, ddt_ref, da_ref, dbias_ref, ddsk_ref, dgn_ref, dh_ref):
        c_idx = pl.program_id(1)

        @pl.when(c_idx == 0)
        def _():
            dh_ref[...] = jnp.zeros_like(dh_ref)
            da_ref[...] = jnp.zeros_like(da_ref)
            dbias_ref[...] = jnp.zeros_like(dbias_ref)
            ddsk_ref[...] = jnp.zeros_like(ddsk_ref)
            dgn_ref[...] = jnp.zeros_like(dgn_ref)

        dtb, dt, cs, cs_t, row, col, upper = _ssd_decays(dtr_ref, dtrt_ref, bias_ref, biast_ref, alog_ref, alogt_ref)
        first = _lane_first_half((BLK, 2 * HD))
        second = jnp.logical_not(first)
        first_row = _lane_first_half((1, 2 * HD))
        tril = row >= col
        triu = row <= col
        last_row = lax.broadcasted_iota(jnp.int32, (BLK, 1), 0) == BLK - 1
        lane8 = lax.broadcasted_iota(jnp.int32, (BLK, 8), 1)

        yv = y_ref[...]
        zv = z_ref[...].astype(F32)
        sg = _sigmoid(zv)
        yz = yv * (zv * sg)
        r = lax.rsqrt(jnp.mean(yz * yz, axis=-1, keepdims=True) + EPS)
        yzn = yz * r
        dsn = dsin_ref[...]
        dgn_ref[...] += jnp.sum(dsn * yzn, axis=0, keepdims=True)
        dsn = dsn * gn_ref[...]
        dyz = r * (dsn - yzn * jnp.mean(dsn * yzn, axis=-1, keepdims=True))
        dy = dyz * (zv * sg)
        dz_ref[...] = (dyz * yv * (sg * (1.0 + zv * (1.0 - sg)))).astype(BF16)
        xs_all = xs_ref[...].astype(F32)
        ddsk_ref[...] += jnp.sum(dy * xs_all, axis=0, keepdims=True)

        b16, c16, bt16, ct16 = b_ref[...], c_ref[...], bt_ref[...], ct_ref[...]
        cb = _dot_nt(c16, b16)
        cb_t = _dot_nt(b16, c16)
        g_sum = jnp.zeros((BLK, BLK), F32)
        gt_sum = jnp.zeros((BLK, BLK), F32)
        dc_acc = jnp.zeros((BLK, BLK), F32)
        db_acc = jnp.zeros((BLK, BLK), F32)
        dcs_all = jnp.zeros((BLK, 8), F32)
        ddtx_all = jnp.zeros((BLK, 8), F32)
        for pr in range(GROUP_W // (2 * HD)):
            cols = slice(pr * 2 * HD, (pr + 1) * 2 * HD)
            heads = (2 * pr, 2 * pr + 1)
            xs = xs_all[:, cols]
            dy_p = dy[:, cols]
            dt_pair = jnp.where(first, dt[:, heads[0]:heads[0] + 1], dt[:, heads[1]:heads[1] + 1])
            xt = xs * dt_pair
            xt16 = xt.astype(BF16)
            hstate = hp_ref[pr]
            h16 = hstate.astype(BF16)
            dhn = dh_ref[pr]
            dhn16 = dhn.astype(BF16)
            d_xt = jnp.zeros((BLK, 2 * HD), F32)
            dcs = []
            for e, h in enumerate(heads):
                lanes = first if e == 0 else second
                cs_c, cs_r = cs[:, h:h + 1], cs_t[h:h + 1, :]
                decay = jnp.exp(jnp.where(tril, cs_c - cs_r, NEG))
                decay_t = jnp.exp(jnp.where(triu, cs_r - cs_c, NEG))
                dym16 = jnp.where(lanes, dy_p, 0.0).astype(BF16)
                d_m = _dot_nt(dym16, xt16)
                d_mt = _dot_nt(xt16, dym16)
                d_xt += _dot((cb_t * decay_t).astype(BF16), dym16)
                gm = d_m * decay
                gmt = d_mt * decay_t
                g_sum += gm
                gt_sum += gmt
                dcs.append(jnp.sum(gm * cb, axis=-1, keepdims=True) - jnp.sum(gmt * cb_t, axis=-1, keepdims=True))
            exp_cs = [jnp.exp(cs[:, h:h + 1]) for h in heads]
            tots = [cs[BLK - 1:BLK, h:h + 1] for h in heads]
            f_col = [jnp.exp(tots[e] - cs[:, h:h + 1]) for e, h in enumerate(heads)]
            e_pair = jnp.where(first, exp_cs[0], exp_cs[1])
            f_pair = jnp.where(first, f_col[0], f_col[1])
            dec = [jnp.exp(t) for t in tots]
            dec_pair = jnp.where(first_row, dec[0], dec[1])
            edy = e_pair * dy_p
            edy16 = edy.astype(BF16)
            y_off = e_pair * _dot(c16, h16)
            oa, ob = _pair_sum(dy_p * y_off, first)
            dc_acc += _dot_nt(edy16, h16)
            dh_prev = _dot(ct16, edy16)
            zmat = _dot(b16, dhn16)
            d_xt += f_pair * zmat
            fa, fb = _pair_sum(zmat * xt, first)
            ta, tb = fa * f_col[0], fb * f_col[1]
            hh = dhn * hstate
            ha_sum = jnp.sum(jnp.sum(jnp.where(first, hh, 0.0), axis=-1, keepdims=True), axis=0, keepdims=True)
            hb_sum = jnp.sum(jnp.sum(hh, axis=-1, keepdims=True), axis=0, keepdims=True) - ha_sum
            dtot_a = jnp.sum(ta, axis=0, keepdims=True) + ha_sum * dec[0]
            dtot_b = jnp.sum(tb, axis=0, keepdims=True) + hb_sum * dec[1]
            dcs[0] = dcs[0] + oa - ta + jnp.where(last_row, dtot_a, 0.0)
            dcs[1] = dcs[1] + ob - tb + jnp.where(last_row, dtot_b, 0.0)
            db_acc += _dot_nt((f_pair * xt).astype(BF16), dhn16)
            dh_ref[pr] = dh_prev + dec_pair * dhn
            d_pair = jnp.where(first_row, dsk_ref[:, heads[0]:heads[0] + 1], dsk_ref[:, heads[1]:heads[1] + 1])
            dxs_ref[:, cols] = dy_p * d_pair + d_xt * dt_pair
            xa, xb = _pair_sum(d_xt * xs, first)
            for e, h in enumerate(heads):
                dcs_all = jnp.where(lane8 == h, dcs[e], dcs_all)
                ddtx_all = jnp.where(lane8 == h, (xa, xb)[e], ddtx_all)

        dc_ref[...] = dc_acc + _dot(g_sum.astype(BF16), b16)
        db_ref[...] = db_acc + _dot(gt_sum.astype(BF16), c16)
        d_a = _dot_hi(upper, dcs_all)
        a_neg = -jnp.exp(alog_ref[...])
        ddt = ddtx_all + d_a * a_neg
        da_ref[...] += jnp.sum(d_a * dt, axis=0, keepdims=True)
        ddtr = ddt * _sigmoid(dtb)
        ddt_ref[...] = ddtr
        dbias_ref[...] += jnp.sum(ddtr, axis=0, keepdims=True)

    nb0 = SSD_INNER // BLK
    rc = lambda c: nc - 1 - c
    gparam = pl.BlockSpec((None, 1, 8), lambda g, c: (g, 0, 0))
    gparam_t = pl.BlockSpec((None, 8, 1), lambda g, c: (g, 0, 0))
    wide = pl.BlockSpec((BLK, GROUP_W), lambda g, c: (rc(c), g))
    narrow = pl.BlockSpec((BLK, BLK), lambda g, c: (rc(c), g))
    return _call(
        body, name=name,
        out_shape=[_sds((s, SSD_INNER), F32), _sds((s, GROUP_W), F32), _sds((s, GROUP_W), F32),
                   _sds((s, SSD_INNER), BF16), _sds((SSD_GROUPS, s, 8), F32),
                   _sds((SSD_GROUPS, 1, 8), F32), _sds((SSD_GROUPS, 1, 8), F32),
                   _sds((SSD_GROUPS, 1, GROUP_W), F32), _sds((1, SSD_INNER), F32)],
        grid=(SSD_GROUPS, nc),
        in_specs=[wide,
                  pl.BlockSpec((BLK, BLK), lambda g, c: (rc(c), nb0 + g)),
                  pl.BlockSpec((BLK, BLK), lambda g, c: (rc(c), nb0 + SSD_GROUPS + g)),
                  pl.BlockSpec((BLK, BLK), lambda g, c: (g, rc(c))),
                  pl.BlockSpec((BLK, BLK), lambda g, c: (SSD_GROUPS + g, rc(c))),
                  pl.BlockSpec((BLK, GROUP_W), lambda g, c: (rc(c), COL_Z // GROUP_W + g)),
                  wide, wide,
                  pl.BlockSpec((None, None, 4, BLK, 2 * HD), lambda g, c: (g, rc(c), 0, 0, 0)),
                  pl.BlockSpec((None, BLK, 8), lambda g, c: (g, rc(c), 0)),
                  pl.BlockSpec((None, 8, BLK), lambda g, c: (g, 0, rc(c))),
                  gparam, gparam_t, gparam, gparam_t, gparam,
                  pl.BlockSpec((1, GROUP_W), lambda g, c: (0, g))],
        out_specs=[wide, narrow, narrow, wide,
                   pl.BlockSpec((None, BLK, 8), lambda g, c: (g, rc(c), 0)),
                   gparam, gparam,
                   pl.BlockSpec((None, 1, GROUP_W), lambda g, c: (g, 0, 0)),
                   pl.BlockSpec((1, GROUP_W), lambda g, c: (0, g))],
        scratch_shapes=[pltpu.VMEM((4, BLK, 2 * HD), F32)],
    )(xc, xc, xc, bc_t, bc_t, p, y, d_sin, hprev, dtg, dtg_t, bias, bias_t, alog, alog_t, dskip, gn)


def merge_fwd(p, a, sbr, name, tm=512):
    s = p.shape[0]
    nj = D_MODEL // GROUP_W

    def body(ga_ref, gs_ref, a_ref, s_ref, o_ref):
        o_ref[...] = (_sigmoid(ga_ref[...].astype(F32)) * a_ref[...]
                      + _sigmoid(gs_ref[...].astype(F32)) * s_ref[...]).astype(BF16)

    blk = pl.BlockSpec((tm, GROUP_W), lambda i, j: (i, j))
    return _call(body, name=name, out_shape=_sds((s, D_MODEL), BF16), grid=(s // tm, nj),
                 in_specs=[pl.BlockSpec((tm, GROUP_W), lambda i, j: (i, COL_GA // GROUP_W + j)),
                           pl.BlockSpec((tm, GROUP_W), lambda i, j: (i, COL_GS // GROUP_W + j)), blk, blk],
                 out_specs=blk)(p, p, a, sbr)


def merge_bwd(p, a, sbr, dmerged, name, tm=512):
    s = p.shape[0]
    nj = D_MODEL // GROUP_W

    def body(ga_ref, gs_ref, a_ref, s_ref, dm_ref, da_ref, ds_ref, dga_ref, dgs_ref):
        dm = dm_ref[...]
        sa = _sigmoid(ga_ref[...].astype(F32))
        ss = _sigmoid(gs_ref[...].astype(F32))
        da_ref[...] = (dm * sa).astype(BF16)
        ds_ref[...] = (dm * ss).astype(BF16)
        dga_ref[...] = (dm * a_ref[...] * sa * (1.0 - sa)).astype(BF16)
        dgs_ref[...] = (dm * s_ref[...] * ss * (1.0 - ss)).astype(BF16)

    blk = pl.BlockSpec((tm, GROUP_W), lambda i, j: (i, j))
    shp = _sds((s, D_MODEL), BF16)
    return _call(body, name=name, out_shape=[shp] * 4, grid=(s // tm, nj),
                 in_specs=[pl.BlockSpec((tm, GROUP_W), lambda i, j: (i, COL_GA // GROUP_W + j)),
                           pl.BlockSpec((tm, GROUP_W), lambda i, j: (i, COL_GS // GROUP_W + j)), blk, blk, blk],
                 out_specs=[blk] * 4)(p, p, a, sbr, dmerged)


def _group_major(v):
    return v.reshape(SSD_GROUPS, 1, 8), v.reshape(SSD_GROUPS, 8, 1)


def mixer_forward(x, w):
    s = x.shape[0]
    h = rms_fwd(x, w["mix_norm"], "mix_rms")
    p = matmul_nn(h, w["w_in_main"], "mix_proj", BF16, tm=1024, tn=512)
    dt_raw = matmul_nn(h, w["w_in_dt"], "mix_proj_dt", F32, tm=1024, tn=DT_PAD)
    qn, kn = qk_norm_fwd(p, w["q_gain"], w["k_gain"], "qk_norm")
    streams, os_, lses = [], [], []
    for g, d in enumerate(ATTN_DILATIONS):
        cols = slice(g * GROUP_W, (g + 1) * GROUP_W)
        qs, ks = _to_streams(qn[:, cols], d), _to_streams(kn[:, cols], d)
        vs = _to_streams(p[:, COL_V + g * GROUP_W:COL_V + (g + 1) * GROUP_W], d)
        o, lse = attn_fwd(qs, ks, vs, g, f"attn_fwd{g}")
        streams.append((qs, ks, vs, lse))
        os_.append(_from_streams(o, d))
        lses.append(_from_streams(lse, d))
    attn_o = attn_merge_fwd(os_, lses, "attn_merge")
    cpre, xc = conv_fwd(p, w["conv_w"], w["conv_b"], "conv_fwd")
    dtg = dt_raw[:, :SSD_HEADS].reshape(s, SSD_GROUPS, 8).transpose(1, 0, 2)
    dtg_t = dtg.transpose(0, 2, 1)
    params = (*_group_major(w["dt_bias"]), *_group_major(w["a_log"]), _group_major(w["d_skip"])[0])
    y, s_in, hprev = ssd_fwd(p, xc, dtg, dtg_t, params, w["ssd_norm"], "ssd_fwd")
    a = matmul_nn(attn_o, w["w_attn_branch"], "attn_branch", F32, tm=1024, tn=512)
    sbr = matmul_nn(s_in, w["w_ssd_branch"], "ssd_branch", F32, tm=1024, tn=512)
    merged = merge_fwd(p, a, sbr, "merge")
    x_out = matmul_nn(merged, w["w_out"], "mix_out", F32, tm=1024, tn=512, res=x)
    saved = dict(h=h, p=p, streams=streams, os=os_, lses=lses, attn_o=attn_o, cpre=cpre, xc=xc, dtg=dtg,
                 dtg_t=dtg_t, params=params, y=y, s_in=s_in, hprev=hprev, a=a, sbr=sbr, merged=merged)
    return x_out, saved


def mixer_backward(dx_out, x, w, sv):
    s = x.shape[0]
    p = sv["p"]
    g = {}
    dmerged = matmul_nt(dx_out, w["w_out"], "d_merged", F32, tm=1024, tn=512, tk=1024)
    g["w_out"] = matmul_tn(sv["merged"], dx_out, "dw_out", tn=512, ts=1024)
    da, ds, dga, dgs = merge_bwd(p, sv["a"], sv["sbr"], dmerged, "merge_bwd")
    g["w_attn_branch"] = matmul_tn(sv["attn_o"], da, "dw_attn_branch", tn=512, ts=1024)
    g["w_ssd_branch"] = matmul_tn(sv["s_in"], ds, "dw_ssd_branch", tn=512, ts=1024)
    d_attn_o = matmul_nt(da, w["w_attn_branch"], "d_attn_o", F32, tm=1024, tn=512, tk=1024)
    d_sin = matmul_nt(ds, w["w_ssd_branch"], "d_ssd_in", F32, tm=1024, tn=512, tk=1024)
    bc_t = sv["xc"][:, SSD_INNER:].T
    dxs, d_b, d_c, dz, ddt, d_asum, d_bias, d_dsk, d_gn = ssd_bwd(
        p, sv["xc"], bc_t, sv["y"], d_sin, sv["hprev"], sv["dtg"], sv["dtg_t"], sv["params"], w["ssd_norm"], "ssd_bwd")
    dxc = jnp.concatenate([dxs, d_b, d_c], axis=1)
    dxbc, d_convw, d_convb = conv_bwd(p, sv["cpre"], dxc, w["conv_w"], "conv_bwd")
    g["conv_w"] = d_convw[:SSD_CONV]
    g["conv_b"] = d_convb
    g["dt_bias"] = d_bias.reshape(1, SSD_HEADS)
    g["a_log"] = (d_asum * (-jnp.exp(sv["params"][2]))).reshape(1, SSD_HEADS)
    g["d_skip"] = jnp.sum(d_dsk.reshape(SSD_HEADS, HD), axis=1).reshape(1, SSD_HEADS)
    g["ssd_norm"] = d_gn
    merged_bwd = attn_merge_bwd(d_attn_o, sv["os"], sv["lses"], "attn_merge_bwd")
    dqs, dks, dvs = [], [], []
    for gi, d in enumerate(ATTN_DILATIONS):
        qs, ks, vs, lse = sv["streams"][gi]
        d_o = _to_streams(merged_bwd[gi], d)
        cterm = _to_streams(merged_bwd[3 + gi], d)
        dq, dk, dv = attn_bwd(qs, ks, vs, d_o, cterm, lse, gi, f"attn_bwd{gi}")
        dqs.append(_from_streams(dq, d))
        dks.append(_from_streams(dk, d))
        dvs.append(_from_streams(dv, d).astype(BF16))
    dqn = jnp.concatenate(dqs, axis=1)
    dkn = jnp.concatenate(dks, axis=1)
    dq, dk, d_qg, d_kg = qk_norm_bwd(p, dqn, dkn, w["q_gain"], w["k_gain"], "qk_norm_bwd")
    g["q_norm"] = jnp.sum(d_qg.reshape(N_ATTN_HEADS, HD), axis=0).reshape(1, HD)
    g["k_norm"] = jnp.sum(d_kg.reshape(N_ATTN_HEADS, HD), axis=0).reshape(1, HD)
    dp = jnp.concatenate([dq, dk] + dvs + [dz, dxbc, dga, dgs], axis=1)
    ddt_pad = jnp.pad(ddt.transpose(1, 0, 2).reshape(s, SSD_HEADS), ((0, 0), (0, DT_PAD - SSD_HEADS)))
    g["w_in_main"] = matmul_tn(sv["h"], dp, "dw_in", tn=512, ts=1024)
    g["w_in_dt"] = matmul_tn(sv["h"], ddt_pad, "dw_in_dt", tn=DT_PAD, ts=1024)
    dh_main = matmul_nt(dp, w["w_in_main"], "d_mix_h", F32, tm=1024, tn=1024, tk=512)
    dh_dt = matmul_nt(ddt_pad, w["w_in_dt"], "d_mix_h_dt", F32, tm=1024, tn=1024, tk=DT_PAD)
    dx, g["mix_norm"] = rms_bwd([dh_main, dh_dt], x, w["mix_norm"], dx_out, "mix_drms")
    return dx, g


ANY = pl.BlockSpec(memory_space=pl.ANY)


def _place():
    x, y, c = lax.axis_index("x"), lax.axis_index("y"), lax.axis_index("c")
    chips = [(1 - x, y), (x, 1 - y), (1 - x, 1 - y)]
    return x, y, c, 2 * x + y, chips


def _comm_call(body, *, name, out_shape, n_in, scratch_shapes, aliases=None):
    return pl.pallas_call(
        body, out_shape=out_shape, in_specs=[ANY] * n_in, out_specs=[ANY] * len(out_shape),
        scratch_shapes=scratch_shapes, input_output_aliases=aliases or {}, name=name,
        compiler_params=pltpu.CompilerParams(has_side_effects=True))


def gather_weights(shards, small):
    n = len(shards)
    halves = [a.shape[0] // 2 for a in shards]
    out_shape = [_sds((N_CHIP,) + a.shape, a.dtype) for a in shards] + [_sds((N_CHIP,) + small.shape, small.dtype)]

    def body(*refs):
        ins, outs = refs[:n + 1], refs[n + 1:2 * n + 2]
        send1, recv1, send2, recv2, local = refs[2 * n + 2:]
        x, y, c, me, chips = _place()
        sibling = (x, y, 1 - c)

        def rows(k, chip, core):
            if k == n:
                return outs[k].at[chip]
            return outs[k].at[chip, pl.ds(core * halves[k], halves[k])]

        def level1(k, t, incoming):
            chip = 2 * chips[t][0] + chips[t][1]
            src = ins[k] if k == n else ins[k].at[pl.ds(c * halves[k], halves[k])]
            return pltpu.make_async_remote_copy(
                src_ref=src, dst_ref=rows(k, chip if incoming else me, c), send_sem=send1.at[3 * k + t],
                recv_sem=recv1.at[3 * k + t], device_id=(*chips[t], c), device_id_type=MESH)

        def level2(k, t, incoming):
            chip = 2 * chips[t][0] + chips[t][1]
            core = (1 - c) if incoming else c
            return pltpu.make_async_remote_copy(
                src_ref=rows(k, chip, core), dst_ref=rows(k, chip, core), send_sem=send2.at[3 * k + t],
                recv_sem=recv2.at[3 * k + t], device_id=sibling, device_id_type=MESH)

        own = [pltpu.make_async_copy(ins[k], outs[k].at[me], local.at[k]) for k in range(n + 1)]
        for cp in own:
            cp.start()
        first = [level1(k, t, False) for k in range(n + 1) for t in range(3)]
        for cp in first:
            cp.start()
        passed = []
        for k in range(n + 1):
            for t in range(3):
                level1(k, t, True).wait_recv()
                if k < n:
                    cp = level2(k, t, False)
                    cp.start()
                    passed.append(cp)
        for k in range(n):
            for t in range(3):
                level2(k, t, True).wait_recv()
        for cp in first + passed:
            cp.wait_send()
        for cp in own:
            cp.wait()

    dma = pltpu.SemaphoreType.DMA
    return _comm_call(body, name="gather_weights", out_shape=out_shape, n_in=n + 1,
                      scratch_shapes=[dma((3 * n + 3,)), dma((3 * n + 3,)), dma((3 * n,)), dma((3 * n,)),
                                      dma((n + 1,))])(*shards, small)


def reduce_to_sibling(grads):
    n = len(grads)
    halves = [a.shape[1] // 2 for a in grads]
    shapes = [_sds((N_CHIP, h, a.shape[2]), a.dtype) for a, h in zip(grads, halves)]

    def body(*refs):
        ins, got, kept = refs[:n], refs[n:2 * n], refs[2 * n:3 * n]
        send, recv, local = refs[3 * n:]
        x, y, c, _, _ = _place()
        copies, locals_ = [], []
        for k in range(n):
            h = halves[k]
            locals_.append(pltpu.make_async_copy(ins[k].at[:, pl.ds(c * h, h)], kept[k], local.at[k]))
            copies.append(pltpu.make_async_remote_copy(
                src_ref=ins[k].at[:, pl.ds((1 - c) * h, h)], dst_ref=got[k], send_sem=send.at[k], recv_sem=recv.at[k],
                device_id=(x, y, 1 - c), device_id_type=MESH))
        for cp in locals_ + copies:
            cp.start()
        for cp in copies:
            cp.wait_recv()
        for cp in copies:
            cp.wait_send()
        for cp in locals_:
            cp.wait()

    dma = pltpu.SemaphoreType.DMA
    res = _comm_call(body, name="reduce_to_sibling", out_shape=shapes + shapes, n_in=n,
                     scratch_shapes=[dma((n,)), dma((n,)), dma((n,))])(*grads)
    return res[:n], res[n:]


def reduce_to_owner(sums):
    n = len(sums)
    shapes = [_sds(a.shape, a.dtype) for a in sums]

    def body(*refs):
        ins, outs = refs[:n], refs[n:2 * n]
        send, recv, local = refs[2 * n:]
        x, y, c, me, chips = _place()
        copies, locals_ = [], []
        for k in range(n):
            locals_.append(pltpu.make_async_copy(ins[k].at[me], outs[k].at[3], local.at[k]))
            for t in range(3):
                chip = 2 * chips[t][0] + chips[t][1]
                copies.append(pltpu.make_async_remote_copy(
                    src_ref=ins[k].at[chip], dst_ref=outs[k].at[t], send_sem=send.at[3 * k + t],
                    recv_sem=recv.at[3 * k + t], device_id=(*chips[t], c), device_id_type=MESH))
        for cp in locals_ + copies:
            cp.start()
        for cp in copies:
            cp.wait_recv()
        for cp in copies:
            cp.wait_send()
        for cp in locals_:
            cp.wait()

    dma = pltpu.SemaphoreType.DMA
    return _comm_call(body, name="reduce_to_owner", out_shape=shapes, n_in=n,
                      scratch_shapes=[dma((3 * n,)), dma((3 * n,)), dma((n,))])(*sums)


def share_with_sibling(halves_):
    n = len(halves_)
    shapes = [_sds((2 * a.shape[0], a.shape[1]), a.dtype) for a in halves_]

    def body(*refs):
        ins, outs = refs[:n], refs[n:2 * n]
        send, recv, local = refs[2 * n:]
        x, y, c, _, _ = _place()
        copies, locals_ = [], []
        for k in range(n):
            h = ins[k].shape[0]
            mine = outs[k].at[pl.ds(c * h, h)]
            locals_.append(pltpu.make_async_copy(ins[k], mine, local.at[k]))
            copies.append(pltpu.make_async_remote_copy(
                src_ref=ins[k], dst_ref=mine, send_sem=send.at[k], recv_sem=recv.at[k],
                device_id=(x, y, 1 - c), device_id_type=MESH))
        for cp in locals_ + copies:
            cp.start()
        for cp in copies:
            cp.wait_recv()
        for cp in copies:
            cp.wait_send()
        for cp in locals_:
            cp.wait()

    dma = pltpu.SemaphoreType.DMA
    return _comm_call(body, name="share_with_sibling", out_shape=shapes, n_in=n,
                      scratch_shapes=[dma((n,)), dma((n,)), dma((n,))])(*halves_)


def _cores():
    c = lax.axis_index("c")
    return jnp.stack([c, 1 - c]).astype(jnp.int32)


def _staged_call(body, *, name, grid, in_specs, out_specs, out_shape, scratch_shapes):
    return pl.pallas_call(
        body, out_shape=out_shape, name=name,
        grid_spec=pltpu.PrefetchScalarGridSpec(num_scalar_prefetch=1, grid=grid, in_specs=in_specs,
                                               out_specs=out_specs, scratch_shapes=scratch_shapes),
        compiler_params=pltpu.CompilerParams(dimension_semantics=("arbitrary",) * len(grid),
                                             vmem_limit_bytes=V7X_VMEM_LIMIT, has_side_effects=True))


def gather_class(w, tm, name):
    r, cdim = w.shape
    h = r // 2
    dma = pltpu.SemaphoreType.DMA

    def body(cores_ref, in_ref, out_ref, slots, send, recv):
        par = pl.program_id(0) % 2
        x, y, c, me, chips = _place()
        sibling = (x, y, 1 - c)

        def copy(src, slot, to):
            return pltpu.make_async_remote_copy(
                src_ref=src, dst_ref=slots.at[par, slot], send_sem=send.at[par, slot], recv_sem=recv.at[par, slot],
                device_id=to, device_id_type=MESH)

        first = [copy(in_ref, t, (*chips[t], c)) for t in range(3)] + [copy(in_ref, 3, sibling)]
        for cp in first:
            cp.start()
        passed = []
        for t in range(3):
            first[t].wait_recv()
            cp = copy(slots.at[par, t], 4 + t, sibling)
            cp.start()
            passed.append(cp)
        first[3].wait_recv()
        for cp in passed:
            cp.wait_recv()
        out_ref[me, c] = in_ref[...]
        out_ref[me, 1 - c] = slots[par, 3]
        for t in range(3):
            chip = 2 * chips[t][0] + chips[t][1]
            out_ref[chip, c] = slots[par, t]
            out_ref[chip, 1 - c] = slots[par, 4 + t]
        for cp in first + passed:
            cp.wait_send()

    out = _staged_call(
        body, name=name, grid=(h // tm,),
        in_specs=[pl.BlockSpec((tm, cdim), lambda i, cores: (cores[0] * (h // tm) + i, 0))],
        out_specs=pl.BlockSpec((N_CHIP, 2, tm, cdim), lambda i, cores: (0, 0, i, 0)),
        out_shape=_sds((N_CHIP, 2, h, cdim), w.dtype),
        scratch_shapes=[pltpu.VMEM((2, 7, tm, cdim), w.dtype), dma((2, 7)), dma((2, 7))],
    )(_cores(), w)
    return out.reshape(N_CHIP, r, cdim)


def sibling_sum(g, tm, name):
    _, r, cdim = g.shape
    h = r // 2
    ni = h // tm
    dma = pltpu.SemaphoreType.DMA

    def body(cores_ref, keep_ref, give_ref, out_ref, slot, send, recv):
        par = (pl.program_id(0) * ni + pl.program_id(1)) % 2
        x, y, c, _, _ = _place()
        cp = pltpu.make_async_remote_copy(src_ref=give_ref, dst_ref=slot.at[par], send_sem=send.at[par],
                                          recv_sem=recv.at[par], device_id=(x, y, 1 - c), device_id_type=MESH)
        cp.start()
        cp.wait_recv()
        out_ref[...] = (keep_ref[...].astype(F32) + slot[par].astype(F32)).astype(out_ref.dtype)
        cp.wait_send()

    flat = g.reshape(N_CHIP * r, cdim)
    return _staged_call(
        body, name=name, grid=(N_CHIP, ni),
        in_specs=[pl.BlockSpec((tm, cdim), lambda j, i, cores: ((2 * j + cores[0]) * ni + i, 0)),
                  pl.BlockSpec((tm, cdim), lambda j, i, cores: ((2 * j + cores[1]) * ni + i, 0))],
        out_specs=pl.BlockSpec((None, tm, cdim), lambda j, i, cores: (j, i, 0)),
        out_shape=_sds((N_CHIP, h, cdim), g.dtype),
        scratch_shapes=[pltpu.VMEM((2, tm, cdim), g.dtype), dma((2,)), dma((2,))],
    )(_cores(), flat, flat)


def owner_sum(sums, tm, name):
    _, h, cdim = sums.shape
    dma = pltpu.SemaphoreType.DMA

    def body(cores_ref, in_ref, out_ref, slots, sib, stage, send, recv, send2, recv2):
        par = pl.program_id(0) % 2
        x, y, c, me, chips = _place()
        copies = []
        for t in range(3):
            chip = 2 * chips[t][0] + chips[t][1]
            copies.append(pltpu.make_async_remote_copy(
                src_ref=in_ref.at[chip], dst_ref=slots.at[par, t], send_sem=send.at[par, t], recv_sem=recv.at[par, t],
                device_id=(*chips[t], c), device_id_type=MESH))
        for cp in copies:
            cp.start()
        for cp in copies:
            cp.wait_recv()
        acc = in_ref[me].astype(F32)
        for t in range(3):
            acc = acc + slots[par, t].astype(F32)
        stage[par] = acc
        to_sib = pltpu.make_async_remote_copy(
            src_ref=stage.at[par], dst_ref=sib.at[par], send_sem=send2.at[par], recv_sem=recv2.at[par],
            device_id=(x, y, 1 - c), device_id_type=MESH)
        to_sib.start()
        out_ref[c] = acc
        to_sib.wait_recv()
        out_ref[1 - c] = sib[par]
        for cp in copies + [to_sib]:
            cp.wait_send()

    out = _staged_call(
        body, name=name, grid=(h // tm,),
        in_specs=[pl.BlockSpec((N_CHIP, tm, cdim), lambda i, cores: (0, i, 0))],
        out_specs=pl.BlockSpec((2, tm, cdim), lambda i, cores: (0, i, 0)),
        out_shape=_sds((2, h, cdim), F32),
        scratch_shapes=[pltpu.VMEM((2, 3, tm, cdim), sums.dtype), pltpu.VMEM((2, tm, cdim), F32),
                        pltpu.VMEM((2, tm, cdim), F32), dma((2, 3)), dma((2, 3)), dma((2,)), dma((2,))],
    )(_cores(), sums)
    return out.reshape(2 * h, cdim)


def gather_conv_w(w):
    def body(in_ref, out_ref, send, recv):
        x, y, c, me, chips = _place()
        out_ref[me] = in_ref[...]
        copies = []
        for t in range(3):
            copies.append(pltpu.make_async_remote_copy(
                src_ref=out_ref.at[me], dst_ref=out_ref.at[me], send_sem=send.at[t], recv_sem=recv.at[t],
                device_id=(*chips[t], c), device_id_type=MESH))
        for cp in copies:
            cp.start()
        for cp in copies:
            cp.wait_recv()
        for cp in copies:
            cp.wait_send()

    dma = pltpu.SemaphoreType.DMA
    vmem = pl.BlockSpec(memory_space=pltpu.VMEM)
    return pl.pallas_call(
        body, out_shape=_sds((N_CHIP,) + w.shape, w.dtype), in_specs=[vmem], out_specs=vmem, name="gather_conv_w",
        scratch_shapes=[dma((3,)), dma((3,))],
        compiler_params=pltpu.CompilerParams(has_side_effects=True))(w)


N_DEV = 8
SMALL_ROWS = 24


def all_reduce_small(pack):
    def body(in_ref, out_ref, buf, send, recv):
        x, y, c, _, _ = _place()
        me = 4 * x + 2 * y + c
        buf[me] = in_ref[...]
        copies = []
        for r in range(1, N_DEV):
            px = (1 - x) if r & 4 else x
            py = (1 - y) if r & 2 else y
            pc = (1 - c) if r & 1 else c
            copies.append(pltpu.make_async_remote_copy(
                src_ref=buf.at[me], dst_ref=buf.at[me], send_sem=send.at[r - 1], recv_sem=recv.at[r - 1],
                device_id=(px, py, pc), device_id_type=MESH))
        for cp in copies:
            cp.start()
        for cp in copies:
            cp.wait_recv()
        for cp in copies:
            cp.wait_send()
        acc = buf[0]
        for j in range(1, N_DEV):
            acc = acc + buf[j]
        out_ref[...] = acc

    dma = pltpu.SemaphoreType.DMA
    vmem = pl.BlockSpec(memory_space=pltpu.VMEM)
    return pl.pallas_call(
        body, out_shape=_sds(pack.shape, F32), in_specs=[vmem], out_specs=vmem, name="all_reduce_small",
        scratch_shapes=[pltpu.VMEM((N_DEV,) + pack.shape, F32), dma((N_DEV - 1,)), dma((N_DEV - 1,))],
        compiler_params=pltpu.CompilerParams(has_side_effects=True))(pack)


def _row_tile(rows, limit, multiple):
    return max(t for t in range(multiple, min(rows, limit) + 1, multiple) if rows % t == 0)


def add_pair(a, b, name):
    _, h, c = a.shape

    def body(a_ref, b_ref, o_ref):
        o_ref[...] = (a_ref[...].astype(F32) + b_ref[...].astype(F32)).astype(o_ref.dtype)

    blk = pl.BlockSpec((None, h, c), lambda j: (j, 0, 0))
    return _call(body, name=name, out_shape=_sds(a.shape, a.dtype), grid=(N_CHIP,), in_specs=[blk, blk],
                 out_specs=blk)(a, b)


def sum_slots(buf, name):
    _, h, c = buf.shape
    tm = _row_tile(h, 256, 16)

    def body(b_ref, o_ref):
        acc = b_ref[3].astype(F32)
        for t in range(3):
            acc = acc + b_ref[t].astype(F32)
        o_ref[...] = acc

    return _call(body, name=name, out_shape=_sds((h, c), F32), grid=(h // tm,),
                 in_specs=[pl.BlockSpec((N_CHIP, tm, c), lambda i: (0, i, 0))],
                 out_specs=pl.BlockSpec((tm, c), lambda i: (i, 0)))(buf)


def adamw(w, g, row_off, m, v, name):
    r, c = w.shape
    tm = r if r < 8 else _row_tile(math.gcd(r, row_off) if row_off else r, 128, 8)
    c1 = 1.0 - ADAM_B1 ** ADAM_STEP
    c2 = 1.0 - ADAM_B2 ** ADAM_STEP

    def body(w_ref, g_ref, m_ref, v_ref, go_ref, d_ref, m2_ref, v2_ref):
        gv = g_ref[...]
        go_ref[...] = gv
        m2 = ADAM_B1 * m_ref[...] + (1.0 - ADAM_B1) * gv
        v2 = ADAM_B2 * v_ref[...] + (1.0 - ADAM_B2) * (gv * gv)
        m2_ref[...] = m2
        v2_ref[...] = v2
        d_ref[...] = -ADAM_LR * ((m2 / c1) / (jnp.sqrt(v2 / c2) + ADAM_EPS) + ADAM_WD * w_ref[...])

    blk = pl.BlockSpec((tm, c), lambda i: (i, 0))
    shp = _sds((r, c), F32)
    return _call(body, name=name, out_shape=[shp] * 4, grid=(r // tm,),
                 in_specs=[blk, pl.BlockSpec((tm, c), lambda i: (row_off // tm + i, 0)), blk, blk],
                 out_specs=[blk] * 4)(w, g, m, v)


BIG = ("ffn1_w_gate", "ffn1_w_up", "ffn1_w_down", "w_in", "w_attn_branch", "w_ssd_branch", "w_out",
       "ffn2_w_gate", "ffn2_w_up", "ffn2_w_down")
SMALL = ("ffn1_norm", "mix_norm", "q_norm", "k_norm", "conv_b", "dt_bias", "a_log", "d_skip", "ssd_norm", "ffn2_norm")
WEIGHTS = ("ffn1_norm", "ffn1_w_gate", "ffn1_w_up", "ffn1_w_down", "mix_norm", "w_in", "q_norm", "k_norm", "conv_w",
           "conv_b", "dt_bias", "a_log", "d_skip", "ssd_norm", "w_attn_branch", "w_ssd_branch", "w_out", "ffn2_norm",
           "ffn2_w_gate", "ffn2_w_up", "ffn2_w_down")
CONV_SHARD = SSD_CONV_DIM // N_CHIP
CLASSES = {
    "c704": (("ffn1_w_gate", 1024), ("ffn1_w_up", 1024), ("ffn2_w_gate", 1024), ("ffn2_w_up", 1024)),
    "c1024": (("ffn1_w_down", 704), ("ffn2_w_down", 704), ("w_ssd_branch", 512), ("w_out", 256)),
    "c2952": (("w_in", 1024),),
    "c256": (("w_attn_branch", 512),),
}
CLASS_TILE = {"c704": 512, "c1024": 272, "c2952": 128, "c256": 256}


def _pack_small(vals, conv_part, loss_part=None):
    flat = [vals[k].reshape(-1) for k in SMALL]
    flat.append(jnp.zeros((SSD_CONV * SSD_CONV_DIM,), F32) if conv_part is None else conv_part.reshape(-1))
    flat.append(jnp.zeros((1,), F32) if loss_part is None else loss_part.reshape(1))
    flat = jnp.concatenate(flat)
    return jnp.pad(flat, (0, SMALL_ROWS * D_MODEL - flat.shape[0])).reshape(SMALL_ROWS, D_MODEL)


def _unpack_small(pack, like):
    flat = pack.reshape(-1)
    out, off = {}, 0
    for k in SMALL:
        n = like[k].size
        out[k] = flat[off:off + n].reshape(like[k].shape)
        off += n
    conv = flat[off:off + SSD_CONV * SSD_CONV_DIM].reshape(SSD_CONV, SSD_CONV_DIM)
    return out, conv, flat[off + SSD_CONV * SSD_CONV_DIM]


def _chip_major_cols(a):
    r = a.shape[0]
    return a.reshape(r, N_CHIP, -1).transpose(1, 0, 2)


def _from_chip_major_cols(a):
    return a.transpose(1, 0, 2).reshape(a.shape[1], -1)


def kernel(x, ffn1_norm, ffn1_w_gate, ffn1_w_up, ffn1_w_down, mix_norm, w_in, q_norm, k_norm, conv_w, conv_b, dt_bias, a_log, d_skip, ssd_norm, w_attn_branch, w_ssd_branch, w_out, ffn2_norm, ffn2_w_gate, ffn2_w_up, ffn2_w_down, loss_target, m_ffn1_norm, m_ffn1_w_gate, m_ffn1_w_up, m_ffn1_w_down, m_mix_norm, m_w_in, m_q_norm, m_k_norm, m_conv_w, m_conv_b, m_dt_bias, m_a_log, m_d_skip, m_ssd_norm, m_w_attn_branch, m_w_ssd_branch, m_w_out, m_ffn2_norm, m_ffn2_w_gate, m_ffn2_w_up, m_ffn2_w_down, v_ffn1_norm, v_ffn1_w_gate, v_ffn1_w_up, v_ffn1_w_down, v_mix_norm, v_w_in, v_q_norm, v_k_norm, v_conv_w, v_conv_b, v_dt_bias, v_a_log, v_d_skip, v_ssd_norm, v_w_attn_branch, v_w_ssd_branch, v_w_out, v_ffn2_norm, v_ffn2_w_gate, v_ffn2_w_up, v_ffn2_w_down):
    env = dict(locals())
    wts = {k: env[k] for k in WEIGHTS}
    moms = {k: env["m_" + k] for k in WEIGHTS}
    vars_ = {k: env["v_" + k] for k in WEIGHTS}
    x0 = x[0]
    target = loss_target[0]

    full = {}
    for cls, members in CLASSES.items():
        local = jnp.concatenate([wts[k][0] for k, _ in members], axis=0).astype(BF16)
        full[cls] = gather_class(local, CLASS_TILE[cls], f"gather_{cls}")
    w704, w1024 = full["c704"], full["c1024"]
    w_in_full = _from_chip_major_cols(full["c2952"])
    mixer_w = dict(
        mix_norm=mix_norm,
        w_in_main=jnp.concatenate([w_in_full[:, :IN_DT0], w_in_full[:, IN_DT1:]], axis=1),
        w_in_dt=jnp.pad(w_in_full[:, IN_DT0:IN_DT1], ((0, 0), (0, DT_PAD - SSD_HEADS))),
        q_gain=jnp.tile(q_norm, (1, 2)), k_gain=jnp.tile(k_norm, (1, 2)),
        conv_w=_from_chip_major_cols(gather_conv_w(conv_w[0])), conv_b=conv_b, dt_bias=dt_bias, a_log=a_log,
        d_skip=d_skip, ssd_norm=ssd_norm, w_attn_branch=_from_chip_major_cols(full["c256"]),
        w_ssd_branch=w1024[:, 1408:1920].reshape(SSD_INNER, D_MODEL),
        w_out=w1024[:, 1920:2176].reshape(D_MODEL, D_MODEL))

    x1, saved1 = ffn_forward(x0, ffn1_norm, w704, w1024, 0, "ffn1")
    x2, saved_mix = mixer_forward(x1, mixer_w)
    x3, saved2 = ffn_forward(x2, ffn2_norm, w704, w1024, 1, "ffn2")
    dx3, sq = loss_grad(x3, target, "loss")
    grads = {}
    dx2, grads["ffn2_norm"], grads["ffn2_w_gate"], grads["ffn2_w_up"], grads["ffn2_w_down"] = ffn_backward(
        dx3, x2, ffn2_norm, w704, w1024, 1, saved2, "ffn2")
    dx1, gmix = mixer_backward(dx2, x1, mixer_w, saved_mix)
    dx0, grads["ffn1_norm"], grads["ffn1_w_gate"], grads["ffn1_w_up"], grads["ffn1_w_down"] = ffn_backward(
        dx1, x0, ffn1_norm, w704, w1024, 0, saved1, "ffn1")
    for k in ("mix_norm", "q_norm", "k_norm", "conv_b", "dt_bias", "a_log", "d_skip", "ssd_norm"):
        grads[k] = gmix[k]
    g_in = jnp.concatenate([gmix["w_in_main"][:, :IN_DT0], gmix["w_in_dt"][:, :SSD_HEADS],
                            gmix["w_in_main"][:, IN_DT0:]], axis=1)
    grads["w_in"] = _chip_major_cols(g_in)
    grads["w_attn_branch"] = _chip_major_cols(gmix["w_attn_branch"])
    grads["w_ssd_branch"] = gmix["w_ssd_branch"].reshape(N_CHIP, -1, D_MODEL)
    grads["w_out"] = gmix["w_out"].reshape(N_CHIP, -1, D_MODEL)

    reduced = {}
    for cls, members in CLASSES.items():
        part = jnp.concatenate([grads[k] for k, _ in members], axis=1) if len(members) > 1 else grads[members[0][0]]
        chip_sum = sibling_sum(part, CLASS_TILE[cls], f"sibling_sum_{cls}")
        reduced[cls] = owner_sum(chip_sum, CLASS_TILE[cls], f"owner_sum_{cls}")
    small_sum = all_reduce_small(_pack_small(grads, gmix["conv_w"], 0.5 * jnp.sum(sq) / D_MODEL))
    g_small, g_conv_full, loss = _unpack_small(small_sum, wts)
    chip = 2 * lax.axis_index("x") + lax.axis_index("y")
    g_conv = lax.dynamic_slice_in_dim(g_conv_full, chip * CONV_SHARD, CONV_SHARD, axis=1)

    g_final, delta, new_m, new_v = dict(g_small), {}, {}, {}

    def update(k, g_arr, row_off):
        shp = wts[k].shape
        two_d = shp[1:]
        res = adamw(wts[k].reshape(two_d), g_arr, row_off, moms[k].reshape(two_d), vars_[k].reshape(two_d),
                    f"adamw_{k}")
        g_final[k], delta[k], new_m[k], new_v[k] = (r.reshape(shp) for r in res)

    for cls, members in CLASSES.items():
        off = 0
        for k, rows in members:
            update(k, reduced[cls], off)
            off += rows
    update("conv_w", g_conv, 0)
    _, d, m2, v2 = adamw(_pack_small(wts, None), _pack_small(g_small, None), 0, _pack_small(moms, None),
                         _pack_small(vars_, None), "adamw_small")
    for res, packed in ((delta, d), (new_m, m2), (new_v, v2)):
        res.update(_unpack_small(packed, wts)[0])

    return (loss, dx0[None], *[g_final[k] for k in WEIGHTS], *[delta[k] for k in WEIGHTS],
            *[new_m[k] for k in WEIGHTS], *[new_v[k] for k in WEIGHTS])
```

```python
import functools
import math

import jax
import jax.numpy as jnp
from jax import lax
from jax.experimental import pallas as pl
from jax.experimental.pallas import tpu as pltpu

F32 = jnp.float32
BF16 = jnp.bfloat16
MESH = pl.DeviceIdType.MESH

EPS = 1e-6
D_MODEL = 1024
D_FF = 2816
N_CHIP = 4
FF_SHARD = D_FF // N_CHIP
HD = 64
BLK = 128
ATTN_DILATIONS = (1, 4, 16)
HEADS_PER_PATTERN = 8
N_ATTN_HEADS = 24
ALIBI_MAX_EXP = 8.0
ATTN_QKV = 1536
GROUP_W = 512
SSD_INNER = 2048
SSD_HEADS = 32
SSD_GROUPS = 4
SSD_CONV = 4
SSD_CONV_DIM = 3072
IN_COLS = 11808
IN_DT0, IN_DT1 = 9728, 9760
COL_K, COL_V, COL_Z, COL_XBC, COL_GA, COL_GS, P_COLS = 1536, 3072, 4608, 6656, 9728, 10752, 11776
DT_PAD = 128

ADAM_LR, ADAM_B1, ADAM_B2, ADAM_EPS, ADAM_WD, ADAM_STEP = 0.001, 0.9, 0.999, 1e-08, 0.01, 10

V7X_VMEM_LIMIT = 56 * 1024 * 1024
NEG = -1e30


def _call(body, *, name, out_shape, in_specs, out_specs, grid=(), scratch_shapes=(), aliases=None):
    return pl.pallas_call(
        body, out_shape=out_shape, grid=grid, in_specs=in_specs, out_specs=out_specs,
        scratch_shapes=scratch_shapes, input_output_aliases=aliases or {}, name=name,
        compiler_params=pltpu.CompilerParams(dimension_semantics=("arbitrary",) * len(grid),
                                             vmem_limit_bytes=V7X_VMEM_LIMIT))


def _sds(shape, dtype):
    return jax.ShapeDtypeStruct(tuple(shape), dtype)


def _dot(a, b):
    return jnp.dot(a, b, preferred_element_type=F32)


def _dot_nt(a, b):
    return lax.dot_general(a, b, (((1,), (1,)), ((), ())), preferred_element_type=F32)


def _dot_tn(a, b):
    return lax.dot_general(a, b, (((0,), (0,)), ((), ())), preferred_element_type=F32)


def _dot_hi(a, b):
    return jnp.dot(a, b, preferred_element_type=F32, precision=lax.Precision.HIGHEST)


def _sigmoid(x):
    return 1.0 / (1.0 + jnp.exp(-x))


def _lane_first_half(shape):
    return lax.broadcasted_iota(jnp.int32, shape, len(shape) - 1) < HD


def _pair_sum(x, first):
    s_all = jnp.sum(x, axis=-1, keepdims=True)
    s_a = jnp.sum(jnp.where(first, x, 0.0), axis=-1, keepdims=True)
    return s_a, s_all - s_a


def _rowwise(name, fn, rows, consts, outs, accs=(), tm=512):
    n_rows = None
    in_arrays, in_specs = [], []
    for r in rows:
        if isinstance(r, tuple):
            arr, w, cb = r
            spec = pl.BlockSpec((tm, w), functools.partial(lambda i, cb: (i, cb), cb=cb))
        else:
            arr = r
            spec = pl.BlockSpec((tm, arr.shape[1]), lambda i: (i, 0))
        n_rows = arr.shape[0]
        in_arrays.append(arr)
        in_specs.append(spec)
    for c in consts:
        in_arrays.append(c)
        in_specs.append(pl.BlockSpec(c.shape, functools.partial(lambda i, n: (0,) * n, n=c.ndim)))
    out_shape = [_sds(s, d) for s, d in outs] + [_sds(s, d) for s, d in accs]
    out_specs = [pl.BlockSpec((tm, s[1]), lambda i: (i, 0)) for s, _ in outs]
    out_specs += [pl.BlockSpec(s, functools.partial(lambda i, n: (0,) * n, n=len(s))) for s, _ in accs]

    def body(*refs):
        fn(pl.program_id(0), *refs)

    res = _call(body, name=name, out_shape=out_shape, grid=(n_rows // tm,), in_specs=in_specs,
                out_specs=out_specs)(*in_arrays)
    return res


def rms_fwd(x, gain, name):
    def fn(i, x_ref, g_ref, h_ref):
        xv = x_ref[...]
        r = lax.rsqrt(jnp.mean(xv * xv, axis=-1, keepdims=True) + EPS)
        h_ref[...] = (xv * r * g_ref[...]).astype(h_ref.dtype)

    return _rowwise(name, fn, [x], [gain], [(x.shape, BF16)])[0]


def rms_bwd(dhs, x, gain, dx_in, name):
    n = len(dhs)

    def fn(i, *refs):
        dh_refs, (x_ref, dxin_ref, g_ref, dx_ref, dg_ref) = refs[:n], refs[n:]
        dh = dh_refs[0][...]
        for r in dh_refs[1:]:
            dh = dh + r[...]
        xv = x_ref[...]
        r = lax.rsqrt(jnp.mean(xv * xv, axis=-1, keepdims=True) + EPS)
        xn = xv * r
        dxn = dh * g_ref[...]
        dx_ref[...] = dxin_ref[...] + r * (dxn - xn * jnp.mean(dxn * xn, axis=-1, keepdims=True))

        @pl.when(i == 0)
        def _():
            dg_ref[...] = jnp.zeros_like(dg_ref)

        dg_ref[...] += jnp.sum(dh * xn, axis=0, keepdims=True)

    return _rowwise(name, fn, list(dhs) + [x, dx_in], [gain], [(x.shape, F32)], [((1, x.shape[1]), F32)])


def loss_grad(y, target, name):
    def fn(i, y_ref, t_ref, dy_ref, sq_ref):
        err = y_ref[...] - t_ref[...]
        dy_ref[...] = err * (1.0 / y_ref.shape[1])

        @pl.when(i == 0)
        def _():
            sq_ref[...] = jnp.zeros_like(sq_ref)

        sq_ref[...] += jnp.sum(err * err, axis=0, keepdims=True)

    return _rowwise(name, fn, [y, target], [], [(y.shape, F32)], [((1, y.shape[1]), F32)])


def matmul_nn(a, b, name, out_dtype, tm, tn, res=None, scale=1.0):
    s, k = a.shape
    n = b.shape[1]

    def body(*refs):
        if res is None:
            a_ref, b_ref, o_ref = refs
            o_ref[...] = _dot(a_ref[...], b_ref[...]).astype(o_ref.dtype)
        else:
            a_ref, b_ref, r_ref, o_ref = refs
            o_ref[...] = (r_ref[...] + scale * _dot(a_ref[...], b_ref[...])).astype(o_ref.dtype)

    in_specs = [pl.BlockSpec((tm, k), lambda i, j: (i, 0)), pl.BlockSpec((k, tn), lambda i, j: (0, j))]
    args = [a, b]
    if res is not None:
        in_specs.append(pl.BlockSpec((tm, tn), lambda i, j: (i, j)))
        args.append(res)
    return _call(body, name=name, out_shape=_sds((s, n), out_dtype), grid=(s // tm, n // tn), in_specs=in_specs,
                 out_specs=pl.BlockSpec((tm, tn), lambda i, j: (i, j)))(*args)


def matmul_nt(a, b, name, out_dtype, tm, tn, tk):
    s, k = a.shape
    n = b.shape[0]
    nk = k // tk

    def body(a_ref, b_ref, o_ref, acc_ref):
        kk = pl.program_id(2)

        @pl.when(kk == 0)
        def _():
            acc_ref[...] = jnp.zeros_like(acc_ref)

        acc_ref[...] += _dot_nt(a_ref[...].astype(BF16), b_ref[...])

        @pl.when(kk == nk - 1)
        def _():
            o_ref[...] = acc_ref[...].astype(o_ref.dtype)

    return _call(body, name=name, out_shape=_sds((s, n), out_dtype), grid=(s // tm, n // tn, nk),
                 in_specs=[pl.BlockSpec((tm, tk), lambda i, j, kk: (i, kk)),
                           pl.BlockSpec((tn, tk), lambda i, j, kk: (j, kk))],
                 out_specs=pl.BlockSpec((tm, tn), lambda i, j, kk: (i, j)),
                 scratch_shapes=[pltpu.VMEM((tm, tn), F32)])(a, b)


def matmul_tn(a, b, name, tn, ts, a_scale=None, b_scale=None):
    s, m = a.shape
    n = b.shape[1]
    ns = s // ts

    def body(a_ref, b_ref, o_ref, acc_ref):
        ss = pl.program_id(1)

        @pl.when(ss == 0)
        def _():
            acc_ref[...] = jnp.zeros_like(acc_ref)

        av, bv = a_ref[...], b_ref[...]
        if a_scale is not None:
            av = av * a_scale
        if b_scale is not None:
            bv = bv * b_scale
        acc_ref[...] += _dot_tn(av.astype(BF16), bv.astype(BF16))

        @pl.when(ss == ns - 1)
        def _():
            o_ref[...] = acc_ref[...].astype(o_ref.dtype)

    return _call(body, name=name, out_shape=_sds((m, n), BF16), grid=(n // tn, ns),
                 in_specs=[pl.BlockSpec((ts, m), lambda j, ss: (ss, 0)), pl.BlockSpec((ts, tn), lambda j, ss: (ss, j))],
                 out_specs=pl.BlockSpec((m, tn), lambda j, ss: (0, j)),
                 scratch_shapes=[pltpu.VMEM((m, tn), F32)])(a, b)


def ffn_up(h, w704, gate_blk, up_blk, name, tm=1024):
    s = h.shape[0]

    def body(h_ref, wg_ref, wu_ref, g_ref, u_ref, a_ref):
        hv = h_ref[...]
        g = _dot(hv, wg_ref[...])
        u = _dot(hv, wu_ref[...])
        g_ref[...] = g.astype(BF16)
        u_ref[...] = u.astype(BF16)
        a_ref[...] = (g * _sigmoid(g) * u).astype(BF16)

    ospec = pl.BlockSpec((None, tm, FF_SHARD), lambda j, i: (j, i, 0))
    shp = _sds((N_CHIP, s, FF_SHARD), BF16)
    return _call(body, name=name, out_shape=[shp, shp, shp], grid=(N_CHIP, s // tm),
                 in_specs=[pl.BlockSpec((tm, D_MODEL), lambda j, i: (i, 0)),
                           pl.BlockSpec((None, D_MODEL, FF_SHARD), lambda j, i: (j, gate_blk, 0)),
                           pl.BlockSpec((None, D_MODEL, FF_SHARD), lambda j, i: (j, up_blk, 0))],
                 out_specs=[ospec, ospec, ospec])(h, w704, w704)


def ffn_down(a, w1024, blk, x, name, tm=512):
    s = x.shape[0]

    def body(a_ref, wd_ref, x_ref, o_ref):
        acc = _dot(a_ref[0], wd_ref[0])
        for j in range(1, N_CHIP):
            acc += _dot(a_ref[j], wd_ref[j])
        o_ref[...] = x_ref[...] + 0.5 * acc

    return _call(body, name=name, out_shape=_sds((s, D_MODEL), F32), grid=(s // tm,),
                 in_specs=[pl.BlockSpec((N_CHIP, tm, FF_SHARD), lambda i: (0, i, 0)),
                           pl.BlockSpec((N_CHIP, FF_SHARD, D_MODEL), lambda i: (0, blk, 0)),
                           pl.BlockSpec((tm, D_MODEL), lambda i: (i, 0))],
                 out_specs=pl.BlockSpec((tm, D_MODEL), lambda i: (i, 0)))(a, w1024, x)


def ffn_bwd_hidden(dx, w1024, blk, g, u, name, tm=1024):
    s = dx.shape[0]

    def body(dx_ref, wd_ref, g_ref, u_ref, dg_ref, du_ref):
        dy = (0.5 * dx_ref[...]).astype(BF16)
        da = _dot_nt(dy, wd_ref[...])
        gv = g_ref[...].astype(F32)
        uv = u_ref[...].astype(F32)
        sg = _sigmoid(gv)
        dg_ref[...] = (da * uv * (sg * (1.0 + gv * (1.0 - sg)))).astype(BF16)
        du_ref[...] = (da * gv * sg).astype(BF16)

    hspec = pl.BlockSpec((None, tm, FF_SHARD), lambda j, i: (j, i, 0))
    shp = _sds((N_CHIP, s, FF_SHARD), BF16)
    return _call(body, name=name, out_shape=[shp, shp], grid=(N_CHIP, s // tm),
                 in_specs=[pl.BlockSpec((tm, D_MODEL), lambda j, i: (i, 0)),
                           pl.BlockSpec((None, FF_SHARD, D_MODEL), lambda j, i: (j, blk, 0)), hspec, hspec],
                 out_specs=[hspec, hspec])(dx, w1024, g, u)


def ffn_bwd_input(dg, du, w704, gate_blk, up_blk, name, tm=512):
    s = dg.shape[1]

    def body(dg_ref, du_ref, wg_ref, wu_ref, o_ref):
        acc = _dot_nt(dg_ref[0], wg_ref[0]) + _dot_nt(du_ref[0], wu_ref[0])
        for j in range(1, N_CHIP):
            acc += _dot_nt(dg_ref[j], wg_ref[j]) + _dot_nt(du_ref[j], wu_ref[j])
        o_ref[...] = acc

    hspec = pl.BlockSpec((N_CHIP, tm, FF_SHARD), lambda i: (0, i, 0))
    return _call(body, name=name, out_shape=_sds((s, D_MODEL), F32), grid=(s // tm,),
                 in_specs=[hspec, hspec, pl.BlockSpec((N_CHIP, D_MODEL, FF_SHARD), lambda i: (0, gate_blk, 0)),
                           pl.BlockSpec((N_CHIP, D_MODEL, FF_SHARD), lambda i: (0, up_blk, 0))],
                 out_specs=pl.BlockSpec((tm, D_MODEL), lambda i: (i, 0)))(dg, du, w704, w704)


def ffn_wgrad_in(h, dgu, name, ts=1024):
    s = h.shape[0]
    ns = s // ts

    def body(h_ref, d_ref, o_ref, acc_ref):
        ss = pl.program_id(1)

        @pl.when(ss == 0)
        def _():
            acc_ref[...] = jnp.zeros_like(acc_ref)

        acc_ref[...] += _dot_tn(h_ref[...], d_ref[...])

        @pl.when(ss == ns - 1)
        def _():
            o_ref[...] = acc_ref[...].astype(BF16)

    return _call(body, name=name, out_shape=_sds((N_CHIP, D_MODEL, FF_SHARD), BF16), grid=(N_CHIP, ns),
                 in_specs=[pl.BlockSpec((ts, D_MODEL), lambda j, ss: (ss, 0)),
                           pl.BlockSpec((None, ts, FF_SHARD), lambda j, ss: (j, ss, 0))],
                 out_specs=pl.BlockSpec((None, D_MODEL, FF_SHARD), lambda j, ss: (j, 0, 0)),
                 scratch_shapes=[pltpu.VMEM((D_MODEL, FF_SHARD), F32)])(h, dgu)


def ffn_wgrad_down(a, dx, name, ts=1024):
    s = dx.shape[0]
    ns = s // ts

    def body(a_ref, dx_ref, o_ref, acc_ref):
        ss = pl.program_id(1)

        @pl.when(ss == 0)
        def _():
            acc_ref[...] = jnp.zeros_like(acc_ref)

        acc_ref[...] += _dot_tn(a_ref[...], (0.5 * dx_ref[...]).astype(BF16))

        @pl.when(ss == ns - 1)
        def _():
            o_ref[...] = acc_ref[...].astype(BF16)

    return _call(body, name=name, out_shape=_sds((N_CHIP, FF_SHARD, D_MODEL), BF16), grid=(N_CHIP, ns),
                 in_specs=[pl.BlockSpec((None, ts, FF_SHARD), lambda j, ss: (j, ss, 0)),
                           pl.BlockSpec((ts, D_MODEL), lambda j, ss: (ss, 0))],
                 out_specs=pl.BlockSpec((None, FF_SHARD, D_MODEL), lambda j, ss: (j, 0, 0)),
                 scratch_shapes=[pltpu.VMEM((FF_SHARD, D_MODEL), F32)])(a, dx)


def ffn_forward(x, gain, w704, w1024, layer, tag):
    h = rms_fwd(x, gain, f"{tag}_rms")
    g, u, a = ffn_up(h, w704, 2 * layer, 2 * layer + 1, f"{tag}_up")
    y = ffn_down(a, w1024, layer, x, f"{tag}_down")
    return y, (h, g, u, a)


def ffn_backward(dy, x, gain, w704, w1024, layer, saved, tag):
    h, g, u, a = saved
    d_wd = ffn_wgrad_down(a, dy, f"{tag}_dwd")
    dg, du = ffn_bwd_hidden(dy, w1024, layer, g, u, f"{tag}_dhid")
    d_wg = ffn_wgrad_in(h, dg, f"{tag}_dwg")
    d_wu = ffn_wgrad_in(h, du, f"{tag}_dwu")
    dh = ffn_bwd_input(dg, du, w704, 2 * layer, 2 * layer + 1, f"{tag}_dh")
    dx, d_gain = rms_bwd([dh], x, gain, dy, f"{tag}_drms")
    return dx, d_gain, d_wg, d_wu, d_wd


def _alibi_slope(head):
    return float(2.0 ** (-ALIBI_MAX_EXP * (head + 1) / N_ATTN_HEADS))


def _head_norm(t, gain_pair, first):
    sa, sb = _pair_sum(t * t, first)
    r = jnp.where(first, lax.rsqrt(sa * (1.0 / HD) + EPS), lax.rsqrt(sb * (1.0 / HD) + EPS))
    return t * r * gain_pair, r


def qk_norm_fwd(p, q_gain, k_gain, name):
    s = p.shape[0]

    def fn(i, q_ref, k_ref, qg_ref, kg_ref, qn_ref, kn_ref):
        first = _lane_first_half((q_ref.shape[0], 2 * HD))
        for src, g_ref, dst in ((q_ref, qg_ref, qn_ref), (k_ref, kg_ref, kn_ref)):
            for pr in range(ATTN_QKV // (2 * HD)):
                cols = slice(pr * 2 * HD, (pr + 1) * 2 * HD)
                y, _ = _head_norm(src[:, cols].astype(F32), g_ref[...], first)
                dst[:, cols] = y.astype(BF16)

    return _rowwise(name, fn, [(p, ATTN_QKV, 0), (p, ATTN_QKV, 1)], [q_gain, k_gain],
                    [((s, ATTN_QKV), BF16), ((s, ATTN_QKV), BF16)])


def qk_norm_bwd(p, dqn, dkn, q_gain, k_gain, name):
    s = p.shape[0]

    def fn(i, q_ref, k_ref, dqn_ref, dkn_ref, qg_ref, kg_ref, dq_ref, dk_ref, dqg_ref, dkg_ref):
        first = _lane_first_half((q_ref.shape[0], 2 * HD))

        @pl.when(i == 0)
        def _():
            dqg_ref[...] = jnp.zeros_like(dqg_ref)
            dkg_ref[...] = jnp.zeros_like(dkg_ref)

        for src, d_ref, g_ref, dst, dg_ref in ((q_ref, dqn_ref, qg_ref, dq_ref, dqg_ref),
                                               (k_ref, dkn_ref, kg_ref, dk_ref, dkg_ref)):
            for pr in range(ATTN_QKV // (2 * HD)):
                cols = slice(pr * 2 * HD, (pr + 1) * 2 * HD)
                t = src[:, cols].astype(F32)
                sa, sb = _pair_sum(t * t, first)
                r = jnp.where(first, lax.rsqrt(sa * (1.0 / HD) + EPS), lax.rsqrt(sb * (1.0 / HD) + EPS))
                xn = t * r
                dy = d_ref[:, cols]
                dg_ref[:, cols] += jnp.sum(dy * xn, axis=0, keepdims=True)
                dxn = dy * g_ref[...]
                ma, mb = _pair_sum(dxn * xn, first)
                mean = jnp.where(first, ma, mb) * (1.0 / HD)
                dst[:, cols] = (r * (dxn - xn * mean)).astype(BF16)

    return _rowwise(name, fn, [(p, ATTN_QKV, 0), (p, ATTN_QKV, 1), dqn, dkn], [q_gain, k_gain],
                    [((s, ATTN_QKV), BF16), ((s, ATTN_QKV), BF16)], [((1, ATTN_QKV), F32), ((1, ATTN_QKV), F32)])


def _to_streams(a, d):
    if d == 1:
        return a
    s, c = a.shape
    return a.reshape(s // d, d, c).transpose(1, 0, 2).reshape(s, c)


def _from_streams(a, d):
    if d == 1:
        return a
    s, c = a.shape
    return a.reshape(d, s // d, c).transpose(1, 0, 2).reshape(s, c)


def _attn_masks():
    row = lax.broadcasted_iota(jnp.int32, (BLK, BLK), 0)
    col = lax.broadcasted_iota(jnp.int32, (BLK, BLK), 1)
    rel_diag = row - col
    rel_prev = rel_diag + BLK
    return rel_diag, rel_prev


def attn_fwd(q, k, v, pattern, name, tq=512):
    s = q.shape[0]
    d = ATTN_DILATIONS[pattern]
    blocks_per_stream = (s // d) // BLK
    nsb = tq // BLK

    def body(q_ref, k_ref, v_ref, kp_ref, vp_ref, o_ref, l_ref):
        i = pl.program_id(0)
        rel_diag, rel_prev = _attn_masks()
        first = _lane_first_half((BLK, 2 * HD))
        rd_f = (rel_diag * d).astype(F32)
        rp_f = (rel_prev * d).astype(F32)
        for sb in range(nsb):
            rows = slice(sb * BLK, (sb + 1) * BLK)
            has_prev = ((i * nsb + sb) % blocks_per_stream != 0).astype(jnp.int32)
            m_diag = rel_diag >= 0
            m_prev = (rel_prev + (1 - has_prev) * (4 * BLK)) <= BLK
            for pr in range(GROUP_W // (2 * HD)):
                cols = slice(pr * 2 * HD, (pr + 1) * 2 * HD)
                qp = q_ref[rows, cols]
                kc, vc = k_ref[rows, cols], v_ref[rows, cols]
                if sb == 0:
                    kp, vp = kp_ref[:, cols], vp_ref[:, cols]
                else:
                    prows = slice((sb - 1) * BLK, sb * BLK)
                    kp, vp = k_ref[prows, cols], v_ref[prows, cols]
                outs, lses = [], []
                for e in range(2):
                    slope = _alibi_slope(pattern * HEADS_PER_PATTERN + 2 * pr + e)
                    qm = jnp.where(first if e == 0 else jnp.logical_not(first), qp, jnp.zeros_like(qp))
                    s1 = jnp.where(m_diag, _dot_nt(qm, kc) * 0.125 - slope * rd_f, NEG)
                    s0 = jnp.where(m_prev, _dot_nt(qm, kp) * 0.125 - slope * rp_f, NEG)
                    m = jnp.maximum(jnp.max(s1, axis=-1, keepdims=True), jnp.max(s0, axis=-1, keepdims=True))
                    p1 = jnp.exp(s1 - m)
                    p0 = jnp.exp(s0 - m)
                    l = jnp.sum(p1, axis=-1, keepdims=True) + jnp.sum(p0, axis=-1, keepdims=True)
                    inv = 1.0 / l
                    outs.append(_dot((p1 * inv).astype(BF16), vc) + _dot((p0 * inv).astype(BF16), vp))
                    lses.append(m + jnp.log(l))
                o_ref[rows, cols] = jnp.where(first, outs[0], outs[1])
                l_ref[rows, cols] = jnp.where(first, lses[0], lses[1])

    cur = pl.BlockSpec((tq, GROUP_W), lambda i: (i, 0))
    prev = pl.BlockSpec((BLK, GROUP_W), lambda i: (jnp.maximum(i * nsb - 1, 0), 0))
    return _call(body, name=name, out_shape=[_sds((s, GROUP_W), F32), _sds((s, GROUP_W), F32)], grid=(s // tq,),
                 in_specs=[cur, cur, cur, prev, prev], out_specs=[cur, cur])(q, k, v, k, v)


def attn_merge_fwd(os_, lses, name):
    s = os_[0].shape[0]

    def fn(i, o0, o1, o2, l0, l1, l2, out_ref):
        m = jnp.maximum(jnp.maximum(l0[...], l1[...]), l2[...])
        e0, e1, e2 = jnp.exp(l0[...] - m), jnp.exp(l1[...] - m), jnp.exp(l2[...] - m)
        inv = 1.0 / (e0 + e1 + e2)
        out_ref[...] = ((e0 * inv) * o0[...] + (e1 * inv) * o1[...] + (e2 * inv) * o2[...]).astype(BF16)

    return _rowwise(name, fn, list(os_) + list(lses), [], [((s, GROUP_W), BF16)])[0]


def attn_merge_bwd(d_out, os_, lses, name):
    s = d_out.shape[0]

    def fn(i, do_ref, o0, o1, o2, l0, l1, l2, d0, d1, d2, c0, c1, c2):
        first = _lane_first_half((do_ref.shape[0], 2 * HD))
        m = jnp.maximum(jnp.maximum(l0[...], l1[...]), l2[...])
        e0, e1, e2 = jnp.exp(l0[...] - m), jnp.exp(l1[...] - m), jnp.exp(l2[...] - m)
        inv = 1.0 / (e0 + e1 + e2)
        w0, w1, w2 = e0 * inv, e1 * inv, e2 * inv
        do = do_ref[...]
        prod = do * (w0 * o0[...] + w1 * o1[...] + w2 * o2[...])
        for pr in range(GROUP_W // (2 * HD)):
            cols = slice(pr * 2 * HD, (pr + 1) * 2 * HD)
            ta, tb = _pair_sum(prod[:, cols], first)
            t = jnp.where(first, ta, tb)
            for w, c_ref in ((w0, c0), (w1, c1), (w2, c2)):
                c_ref[:, cols] = w[:, cols] * t
        for w, d_ref in ((w0, d0), (w1, d1), (w2, d2)):
            d_ref[...] = (w * do).astype(BF16)

    shp = (s, GROUP_W)
    return _rowwise(name, fn, [d_out] + list(os_) + list(lses), [],
                    [(shp, BF16)] * 3 + [(shp, F32)] * 3)


def attn_bwd(q, k, v, d_o, cterm, lse, pattern, name, tq=512):
    s = q.shape[0]
    d = ATTN_DILATIONS[pattern]
    blocks_per_stream = (s // d) // BLK
    nsb = tq // BLK
    n_blocks = s // BLK

    def body(q_ref, k_ref, v_ref, do_ref, c_ref, l_ref, kp_ref, vp_ref, qn_ref, don_ref, cn_ref, ln_ref,
             dq_ref, dk_ref, dv_ref):
        i = pl.program_id(0)
        rel_diag, rel_prev = _attn_masks()
        first = _lane_first_half((BLK, 2 * HD))
        second = jnp.logical_not(first)
        rd_f = (rel_diag * d).astype(F32)
        rp_f = (rel_prev * d).astype(F32)
        m_diag = rel_diag >= 0
        dq_ref[...] = jnp.zeros_like(dq_ref)
        dk_ref[...] = jnp.zeros_like(dk_ref)
        dv_ref[...] = jnp.zeros_like(dv_ref)

        def pair(qp, dop, cp, lp, kp, vp, rel_f, mask):
            dq = dk = dv = None
            for e in range(2):
                lanes = first if e == 0 else second
                slope = slopes[e]
                qm = jnp.where(lanes, qp, jnp.zeros_like(qp))
                dom = jnp.where(lanes, dop, jnp.zeros_like(dop))
                km = jnp.where(lanes, kp, jnp.zeros_like(kp))
                sc = jnp.where(mask, _dot_nt(qm, kp) * 0.125 - slope * rel_f, NEG)
                pm = jnp.exp(sc - lp[:, e * HD:e * HD + 1])
                dl = pm * (_dot_nt(dom, vp) - cp[:, e * HD:e * HD + 1])
                dl16 = dl.astype(BF16)
                t_dq = _dot(dl16, km)
                t_dk = _dot_tn(dl16, qm)
                t_dv = _dot_tn(pm.astype(BF16), dom)
                dq = t_dq if dq is None else dq + t_dq
                dk = t_dk if dk is None else dk + t_dk
                dv = t_dv if dv is None else dv + t_dv
            return dq * 0.125, dk * 0.125, dv

        for pr in range(GROUP_W // (2 * HD)):
            cols = slice(pr * 2 * HD, (pr + 1) * 2 * HD)
            slopes = [_alibi_slope(pattern * HEADS_PER_PATTERN + 2 * pr + e) for e in range(2)]
            for sb in range(nsb + 1):
                gb = i * nsb + sb
                if sb < nsb:
                    rows = slice(sb * BLK, (sb + 1) * BLK)
                    qp, dop, cp, lp = q_ref[rows, cols], do_ref[rows, cols], c_ref[rows, cols], l_ref[rows, cols]
                else:
                    qp, dop, cp, lp = qn_ref[:, cols], don_ref[:, cols], cn_ref[:, cols], ln_ref[:, cols]
                if sb < nsb:
                    dq1, dk1, dv1 = pair(qp, dop, cp, lp, k_ref[rows, cols], v_ref[rows, cols], rd_f, m_diag)
                    dq_ref[rows, cols] += dq1
                    dk_ref[rows, cols] += dk1
                    dv_ref[rows, cols] += dv1
                valid = jnp.logical_and(gb % blocks_per_stream != 0, gb < n_blocks).astype(jnp.int32)
                m_prev = jnp.logical_and(rel_prev <= BLK, (rel_prev + (1 - valid) * (4 * BLK)) <= BLK)
                if sb == 0:
                    kp, vp = kp_ref[:, cols], vp_ref[:, cols]
                else:
                    prows = slice((sb - 1) * BLK, sb * BLK)
                    kp, vp = k_ref[prows, cols], v_ref[prows, cols]
                dq0, dk0, dv0 = pair(qp, dop, cp, lp, kp, vp, rp_f, m_prev)
                if sb < nsb:
                    dq_ref[rows, cols] += dq0
                if sb > 0:
                    dk_ref[prows, cols] += dk0
                    dv_ref[prows, cols] += dv0

    cur = pl.BlockSpec((tq, GROUP_W), lambda i: (i, 0))
    prev = pl.BlockSpec((BLK, GROUP_W), lambda i: (jnp.maximum(i * nsb - 1, 0), 0))
    nxt = pl.BlockSpec((BLK, GROUP_W), lambda i: (jnp.minimum((i + 1) * nsb, n_blocks - 1), 0))
    shp = _sds((s, GROUP_W), F32)
    return _call(body, name=name, out_shape=[shp, shp, shp], grid=(s // tq,),
                 in_specs=[cur] * 6 + [prev, prev] + [nxt] * 4, out_specs=[cur, cur, cur])(
                     q, k, v, d_o, cterm, lse, k, v, q, d_o, cterm, lse)


HALO = 16
CONV_TQ = 512


def conv_fwd(p, w, b, name):
    s = p.shape[0]
    tq = CONV_TQ
    ncol = SSD_CONV_DIM // GROUP_W
    cb0 = COL_XBC // GROUP_W

    def body(u_ref, up_ref, w_ref, b_ref, c_ref, xc_ref):
        i = pl.program_id(0)
        prev = up_ref[...].astype(F32) * (i > 0).astype(F32)
        ext = jnp.concatenate([prev, u_ref[...].astype(F32)], axis=0)
        acc = b_ref[...] + w_ref[SSD_CONV - 1:SSD_CONV, :] * ext[HALO:HALO + tq]
        for kk in range(SSD_CONV - 1):
            off = HALO - (SSD_CONV - 1) + kk
            acc += w_ref[kk:kk + 1, :] * ext[off:off + tq]
        c_ref[...] = acc.astype(BF16)
        xc_ref[...] = (acc * _sigmoid(acc)).astype(BF16)

    cur_in = pl.BlockSpec((tq, GROUP_W), lambda i, j: (i, cb0 + j))
    prev_in = pl.BlockSpec((HALO, GROUP_W), lambda i, j: (jnp.maximum(i * (tq // HALO) - 1, 0), cb0 + j))
    cur_out = pl.BlockSpec((tq, GROUP_W), lambda i, j: (i, j))
    shp = _sds((s, SSD_CONV_DIM), BF16)
    return _call(body, name=name, out_shape=[shp, shp], grid=(s // tq, ncol),
                 in_specs=[cur_in, prev_in, pl.BlockSpec((SSD_CONV, GROUP_W), lambda i, j: (0, j)),
                           pl.BlockSpec((1, GROUP_W), lambda i, j: (0, j))],
                 out_specs=[cur_out, cur_out])(p, p, w, b)


def conv_bwd(p, cpre, dxc, w, name):
    s = p.shape[0]
    tq = CONV_TQ
    ncol = SSD_CONV_DIM // GROUP_W
    cb0 = COL_XBC // GROUP_W
    nt = s // tq

    def body(u_ref, up_ref, c_ref, cn_ref, d_ref, dn_ref, w_ref, du_ref, dw_ref, db_ref):
        i = pl.program_id(1)

        def dpre(c16, dx):
            c = c16.astype(F32)
            sg = _sigmoid(c)
            return dx * (sg * (1.0 + c * (1.0 - sg)))

        dc = dpre(c_ref[...], d_ref[...])
        dcn = dpre(cn_ref[...], dn_ref[...]) * (i < nt - 1).astype(F32)
        dext = jnp.concatenate([dc, dcn], axis=0)
        prev = up_ref[...].astype(F32) * (i > 0).astype(F32)
        uext = jnp.concatenate([prev, u_ref[...].astype(F32)], axis=0)

        @pl.when(i == 0)
        def _():
            dw_ref[...] = jnp.zeros_like(dw_ref)
            db_ref[...] = jnp.zeros_like(db_ref)

        du = w_ref[SSD_CONV - 1:SSD_CONV, :] * dc
        for kk in range(SSD_CONV - 1):
            sh = SSD_CONV - 1 - kk
            du += w_ref[kk:kk + 1, :] * dext[sh:sh + tq]
        du_ref[...] = du.astype(BF16)
        for kk in range(SSD_CONV):
            off = HALO - (SSD_CONV - 1) + kk
            dw_ref[kk:kk + 1, :] += jnp.sum(dc * uext[off:off + tq], axis=0, keepdims=True)
        db_ref[...] += jnp.sum(dc, axis=0, keepdims=True)

    hb = tq // HALO
    cur_p = pl.BlockSpec((tq, GROUP_W), lambda j, i: (i, cb0 + j))
    prev_p = pl.BlockSpec((HALO, GROUP_W), lambda j, i: (jnp.maximum(i * hb - 1, 0), cb0 + j))
    cur = pl.BlockSpec((tq, GROUP_W), lambda j, i: (i, j))
    nxt = pl.BlockSpec((HALO, GROUP_W), lambda j, i: (jnp.minimum((i + 1) * hb, s // HALO - 1), j))
    return _call(body, name=name,
                 out_shape=[_sds((s, SSD_CONV_DIM), BF16), _sds((8, SSD_CONV_DIM), F32), _sds((1, SSD_CONV_DIM), F32)],
                 grid=(ncol, nt),
                 in_specs=[cur_p, prev_p, cur, nxt, cur, nxt, pl.BlockSpec((SSD_CONV, GROUP_W), lambda j, i: (0, j))],
                 out_specs=[cur, pl.BlockSpec((8, GROUP_W), lambda j, i: (0, j)),
                            pl.BlockSpec((1, GROUP_W), lambda j, i: (0, j))])(p, p, cpre, cpre, dxc, dxc, w)


def _softplus(x):
    return jnp.maximum(x, 0.0) + jnp.log(1.0 + jnp.exp(-jnp.abs(x)))


def _ssd_decays(dtr_ref, dtrt_ref, bias_ref, biast_ref, alog_ref, alogt_ref):
    row = lax.broadcasted_iota(jnp.int32, (BLK, BLK), 0)
    col = lax.broadcasted_iota(jnp.int32, (BLK, BLK), 1)
    lower = (row >= col).astype(F32)
    upper = (row <= col).astype(F32)
    dtb = dtr_ref[...] + bias_ref[...]
    dt = _softplus(dtb)
    a = dt * (-jnp.exp(alog_ref[...]))
    cs = _dot_hi(lower, a)
    a_t = _softplus(dtrt_ref[...] + biast_ref[...]) * (-jnp.exp(alogt_ref[...]))
    cs_t = _dot_hi(a_t, upper)
    return dtb, dt, cs, cs_t, row, col, upper


def ssd_fwd(p, xc, dtg, dtg_t, params, gn, name):
    s = p.shape[0]
    nc = s // BLK
    bias, bias_t, alog, alog_t, dskip = params

    def body(xs_ref, b_ref, c_ref, z_ref, dtr_ref, dtrt_ref, bias_ref, biast_ref, alog_ref, alogt_ref, dsk_ref,
             gn_ref, y_ref, sin_ref, hp_ref, h_ref):
        c_idx = pl.program_id(1)

        @pl.when(c_idx == 0)
        def _():
            h_ref[...] = jnp.zeros_like(h_ref)

        _, dt, cs, cs_t, row, col, _ = _ssd_decays(dtr_ref, dtrt_ref, bias_ref, biast_ref, alog_ref, alogt_ref)
        first = _lane_first_half((BLK, 2 * HD))
        first_row = _lane_first_half((1, 2 * HD))
        tril = row >= col
        b16, c16 = b_ref[...], c_ref[...]
        cb = _dot_nt(c16, b16)
        ys = []
        for pr in range(GROUP_W // (2 * HD)):
            cols = slice(pr * 2 * HD, (pr + 1) * 2 * HD)
            ha, hb = 2 * pr, 2 * pr + 1
            xs = xs_ref[:, cols].astype(F32)
            dt_pair = jnp.where(first, dt[:, ha:ha + 1], dt[:, hb:hb + 1])
            xt = xs * dt_pair
            xt16 = xt.astype(BF16)
            y_heads = []
            for h in (ha, hb):
                decay = jnp.exp(jnp.where(tril, cs[:, h:h + 1] - cs_t[h:h + 1, :], NEG))
                y_heads.append(_dot((cb * decay).astype(BF16), xt16))
            y_diag = jnp.where(first, y_heads[0], y_heads[1])
            hstate = h_ref[pr]
            hp_ref[pr] = hstate
            e_pair = jnp.where(first, jnp.exp(cs[:, ha:ha + 1]), jnp.exp(cs[:, hb:hb + 1]))
            y_off = e_pair * _dot(c16, hstate.astype(BF16))
            tot_a, tot_b = cs[BLK - 1:BLK, ha:ha + 1], cs[BLK - 1:BLK, hb:hb + 1]
            f_pair = jnp.where(first, jnp.exp(tot_a - cs[:, ha:ha + 1]), jnp.exp(tot_b - cs[:, hb:hb + 1]))
            new = _dot_tn(b16, (f_pair * xt).astype(BF16))
            dec = jnp.where(first_row, jnp.exp(tot_a), jnp.exp(tot_b))
            h_ref[pr] = dec * hstate + new
            d_pair = jnp.where(first_row, dsk_ref[:, ha:ha + 1], dsk_ref[:, hb:hb + 1])
            ys.append(y_diag + y_off + xs * d_pair)
        y = jnp.concatenate(ys, axis=1)
        y_ref[...] = y
        zv = z_ref[...].astype(F32)
        yz = y * (zv * _sigmoid(zv))
        r = lax.rsqrt(jnp.mean(yz * yz, axis=-1, keepdims=True) + EPS)
        sin_ref[...] = (yz * r * gn_ref[...]).astype(BF16)

    nb0 = SSD_INNER // BLK
    gparam = pl.BlockSpec((None, 1, 8), lambda g, c: (g, 0, 0))
    gparam_t = pl.BlockSpec((None, 8, 1), lambda g, c: (g, 0, 0))
    return _call(
        body, name=name,
        out_shape=[_sds((s, SSD_INNER), F32), _sds((s, SSD_INNER), BF16),
                   _sds((SSD_GROUPS, nc, 4, BLK, 2 * HD), F32)],
        grid=(SSD_GROUPS, nc),
        in_specs=[pl.BlockSpec((BLK, GROUP_W), lambda g, c: (c, g)),
                  pl.BlockSpec((BLK, BLK), lambda g, c: (c, nb0 + g)),
                  pl.BlockSpec((BLK, BLK), lambda g, c: (c, nb0 + SSD_GROUPS + g)),
                  pl.BlockSpec((BLK, GROUP_W), lambda g, c: (c, COL_Z // GROUP_W + g)),
                  pl.BlockSpec((None, BLK, 8), lambda g, c: (g, c, 0)),
                  pl.BlockSpec((None, 8, BLK), lambda g, c: (g, 0, c)),
                  gparam, gparam_t, gparam, gparam_t, gparam,
                  pl.BlockSpec((1, GROUP_W), lambda g, c: (0, g))],
        out_specs=[pl.BlockSpec((BLK, GROUP_W), lambda g, c: (c, g)),
                   pl.BlockSpec((BLK, GROUP_W), lambda g, c: (c, g)),
                   pl.BlockSpec((None, None, 4, BLK, 2 * HD), lambda g, c: (g, c, 0, 0, 0))],
        scratch_shapes=[pltpu.VMEM((4, BLK, 2 * HD), F32)],
    )(xc, xc, xc, p, dtg, dtg_t, bias, bias_t, alog, alog_t, dskip, gn)


def ssd_bwd(p, xc, bc_t, y, d_sin, hprev, dtg, dtg_t, params, gn, name):
    s = p.shape[0]
    nc = s // BLK
    bias, bias_t, alog, alog_t, dskip = params

    def body(xs_ref, b_ref, c_ref, bt_ref, ct_ref, z_ref, y_ref, dsin_ref, hp_ref, dtr_ref, dtrt_ref, bias_ref,
             biast_ref, alog_ref, alogt_ref, dsk_ref, gn_ref,
             dxs_ref, db_ref, dc_ref, dz_ref, ddt_ref, da_ref, dbias_ref, ddsk_ref, dgn_ref, dh_ref):
        c_idx = pl.program_id(1)

        @pl.when(c_idx == 0)
        def _():
            dh_ref[...] = jnp.zeros_like(dh_ref)
            da_ref[...] = jnp.zeros_like(da_ref)
            dbias_ref[...] = jnp.zeros_like(dbias_ref)
            ddsk_ref[...] = jnp.zeros_like(ddsk_ref)
            dgn_ref[...] = jnp.zeros_like(dgn_ref)

        dtb, dt, cs, cs_t, row, col, upper = _ssd_decays(dtr_ref, dtrt_ref, bias_ref, biast_ref, alog_ref, alogt_ref)
        first = _lane_first_half((BLK, 2 * HD))
        second = jnp.logical_not(first)
        first_row = _lane_first_half((1, 2 * HD))
        tril = row >= col
        triu = row <= col
        last_row = lax.broadcasted_iota(jnp.int32, (BLK, 1), 0) == BLK - 1
        lane8 = lax.broadcasted_iota(jnp.int32, (BLK, 8), 1)

        yv = y_ref[...]
        zv = z_ref[...].astype(F32)
        sg = _sigmoid(zv)
        yz = yv * (zv * sg)
        r = lax.rsqrt(jnp.mean(yz * yz, axis=-1, keepdims=True) + EPS)
        yzn = yz * r
        dsn = dsin_ref[...]
        dgn_ref[...] += jnp.sum(dsn * yzn, axis=0, keepdims=True)
        dsn = dsn * gn_ref[...]
        dyz = r * (dsn - yzn * jnp.mean(dsn * yzn, axis=-1, keepdims=True))
        dy = dyz * (zv * sg)
        dz_ref[...] = (dyz * yv * (sg * (1.0 + zv * (1.0 - sg)))).astype(BF16)
        xs_all = xs_ref[...].astype(F32)
        ddsk_ref[...] += jnp.sum(dy * xs_all, axis=0, keepdims=True)

        b16, c16, bt16, ct16 = b_ref[...], c_ref[...], bt_ref[...], ct_ref[...]
        cb = _dot_nt(c16, b16)
        cb_t = _dot_nt(b16, c16)
        g_sum = jnp.zeros((BLK, BLK), F32)
        gt_sum = jnp.zeros((BLK, BLK), F32)
        dc_acc = jnp.zeros((BLK, BLK), F32)
        db_acc = jnp.zeros((BLK, BLK), F32)
        dcs_all = jnp.zeros((BLK, 8), F32)
        ddtx_all = jnp.zeros((BLK, 8), F32)
        for pr in range(GROUP_W // (2 * HD)):
            cols = slice(pr * 2 * HD, (pr + 1) * 2 * HD)
            heads = (2 * pr, 2 * pr + 1)
            xs = xs_all[:, cols]
            dy_p = dy[:, cols]
            dt_pair = jnp.where(first, dt[:, heads[0]:heads[0] + 1], dt[:, heads[1]:heads[1] + 1])
            xt = xs * dt_pair
            xt16 = xt.astype(BF16)
            hstate = hp_ref[pr]
            h16 = hstate.astype(BF16)
            dhn = dh_ref[pr]
            dhn16 = dhn.astype(BF16)
            d_xt = jnp.zeros((BLK, 2 * HD), F32)
            dcs = []
            for e, h in enumerate(heads):
                lanes = first if e == 0 else second
                cs_c, cs_r = cs[:, h:h + 1], cs_t[h:h + 1, :]
                decay = jnp.exp(jnp.where(tril, cs_c - cs_r, NEG))
                decay_t = jnp.exp(jnp.where(triu, cs_r - cs_c, NEG))
                dym16 = jnp.where(lanes, dy_p, 0.0).astype(BF16)
                d_m = _dot_nt(dym16, xt16)
                d_mt = _dot_nt(xt16, dym16)
                d_xt += _dot((cb_t * decay_t).astype(BF16), dym16)
                gm = d_m * decay
                gmt = d_mt * decay_t
                g_sum += gm
                gt_sum += gmt
                dcs.append(jnp.sum(gm * cb, axis=-1, keepdims=True) - jnp.sum(gmt * cb_t, axis=-1, keepdims=True))
            exp_cs = [jnp.exp(cs[:, h:h + 1]) for h in heads]
            tots = [cs[BLK - 1:BLK, h:h + 1] for h in heads]
            f_col = [jnp.exp(tots[e] - cs[:, h:h + 1]) for e, h in enumerate(heads)]
            e_pair = jnp.where(first, exp_cs[0], exp_cs[1])
            f_pair = jnp.where(first, f_col[0], f_col[1])
            dec = [jnp.exp(t) for t in tots]
            dec_pair = jnp.where(first_row, dec[0], dec[1])
            edy = e_pair * dy_p
            edy16 = edy.astype(BF16)
            y_off = e_pair * _dot(c16, h16)
            oa, ob = _pair_sum(dy_p * y_off, first)
            dc_acc += _dot_nt(edy16, h16)
            dh_prev = _dot(ct16, edy16)
            zmat = _dot(b16, dhn16)
            d_xt += f_pair * zmat
            fa, fb = _pair_sum(zmat * xt, first)
            ta, tb = fa * f_col[0], fb * f_col[1]
            hh = dhn * hstate
            ha_sum = jnp.sum(jnp.sum(jnp.where(first, hh, 0.0), axis=-1, keepdims=True), axis=0, keepdims=True)
            hb_sum = jnp.sum(jnp.sum(hh, axis=-1, keepdims=True), axis=0, keepdims=True) - ha_sum
            dtot_a = jnp.sum(ta, axis=0, keepdims=True) + ha_sum * dec[0]
            dtot_b = jnp.sum(tb, axis=0, keepdims=True) + hb_sum * dec[1]
            dcs[0] = dcs[0] + oa - ta + jnp.where(last_row, dtot_a, 0.0)
            dcs[1] = dcs[1] + ob - tb + jnp.where(last_row, dtot_b, 0.0)
            db_acc += _dot_nt((f_pair * xt).astype(BF16), dhn16)
            dh_ref[pr] = dh_prev + dec_pair * dhn
            d_pair = jnp.where(first_row, dsk_ref[:, heads[0]:heads[0] + 1], dsk_ref[:, heads[1]:heads[1] + 1])
            dxs_ref[:, cols] = dy_p * d_pair + d_xt * dt_pair
            xa, xb = _pair_sum(d_xt * xs, first)
            for e, h in enumerate(heads):
                dcs_all = jnp.where(lane8 == h, dcs[e], dcs_all)
                ddtx_all = jnp.where(lane8 == h, (xa, xb)[e], ddtx_all)

        dc_ref[...] = dc_acc + _dot(g_sum.astype(BF16), b16)
        db_ref[...] = db_acc + _dot(gt_sum.astype(BF16), c16)
        d_a = _dot_hi(upper, dcs_all)
        a_neg = -jnp.exp(alog_ref[...])
        ddt = ddtx_all + d_a * a_neg
        da_ref[...] += jnp.sum(d_a * dt, axis=0, keepdims=True)
        ddtr = ddt * _sigmoid(dtb)
        ddt_ref[...] = ddtr
        dbias_ref[...] += jnp.sum(ddtr, axis=0, keepdims=True)

    nb0 = SSD_INNER // BLK
    rc = lambda c: nc - 1 - c
    gparam = pl.BlockSpec((None, 1, 8), lambda g, c: (g, 0, 0))
    gparam_t = pl.BlockSpec((None, 8, 1), lambda g, c: (g, 0, 0))
    wide = pl.BlockSpec((BLK, GROUP_W), lambda g, c: (rc(c), g))
    narrow = pl.BlockSpec((BLK, BLK), lambda g, c: (rc(c), g))
    return _call(
        body, name=name,
        out_shape=[_sds((s, SSD_INNER), F32), _sds((s, GROUP_W), F32), _sds((s, GROUP_W), F32),
                   _sds((s, SSD_INNER), BF16), _sds((SSD_GROUPS, s, 8), F32),
                   _sds((SSD_GROUPS, 1, 8), F32), _sds((SSD_GROUPS, 1, 8), F32),
                   _sds((SSD_GROUPS, 1, GROUP_W), F32), _sds((1, SSD_INNER), F32)],
        grid=(SSD_GROUPS, nc),
        in_specs=[wide,
                  pl.BlockSpec((BLK, BLK), lambda g, c: (rc(c), nb0 + g)),
                  pl.BlockSpec((BLK, BLK), lambda g, c: (rc(c), nb0 + SSD_GROUPS + g)),
                  pl.BlockSpec((BLK, BLK), lambda g, c: (g, rc(c))),
                  pl.BlockSpec((BLK, BLK), lambda g, c: (SSD_GROUPS + g, rc(c))),
                  pl.BlockSpec((BLK, GROUP_W), lambda g, c: (rc(c), COL_Z // GROUP_W + g)),
                  wide, wide,
                  pl.BlockSpec((None, None, 4, BLK, 2 * HD), lambda g, c: (g, rc(c), 0, 0, 0)),
                  pl.BlockSpec((None, BLK, 8), lambda g, c: (g, rc(c), 0)),
                  pl.BlockSpec((None, 8, BLK), lambda g, c: (g, 0, rc(c))),
                  gparam, gparam_t, gparam, gparam_t, gparam,
                  pl.BlockSpec((1, GROUP_W), lambda g, c: (0, g))],
        out_specs=[wide, narrow, narrow, wide,
                   pl.BlockSpec((None, BLK, 8), lambda g, c: (g, rc(c), 0)),
                   gparam, gparam,
                   pl.BlockSpec((None, 1, GROUP_W), lambda g, c: (g, 0, 0)),
                   pl.BlockSpec((1, GROUP_W), lambda g, c: (0, g))],
        scratch_shapes=[pltpu.VMEM((4, BLK, 2 * HD), F32)],
    )(xc, xc, xc, bc_t, bc_t, p, y, d_sin, hprev, dtg, dtg_t, bias, bias_t, alog, alog_t, dskip, gn)


def merge_fwd(p, a, sbr, name, tm=512):
    s = p.shape[0]
    nj = D_MODEL // GROUP_W

    def body(ga_ref, gs_ref, a_ref, s_ref, o_ref):
        o_ref[...] = (_sigmoid(ga_ref[...].astype(F32)) * a_ref[...]
                      + _sigmoid(gs_ref[...].astype(F32)) * s_ref[...]).astype(BF16)

    blk = pl.BlockSpec((tm, GROUP_W), lambda i, j: (i, j))
    return _call(body, name=name, out_shape=_sds((s, D_MODEL), BF16), grid=(s // tm, nj),
                 in_specs=[pl.BlockSpec((tm, GROUP_W), lambda i, j: (i, COL_GA // GROUP_W + j)),
                           pl.BlockSpec((tm, GROUP_W), lambda i, j: (i, COL_GS // GROUP_W + j)), blk, blk],
                 out_specs=blk)(p, p, a, sbr)


def merge_bwd(p, a, sbr, dmerged, name, tm=512):
    s = p.shape[0]
    nj = D_MODEL // GROUP_W

    def body(ga_ref, gs_ref, a_ref, s_ref, dm_ref, da_ref, ds_ref, dga_ref, dgs_ref):
        dm = dm_ref[...]
        sa = _sigmoid(ga_ref[...].astype(F32))
        ss = _sigmoid(gs_ref[...].astype(F32))
        da_ref[...] = (dm * sa).astype(BF16)
        ds_ref[...] = (dm * ss).astype(BF16)
        dga_ref[...] = (dm * a_ref[...] * sa * (1.0 - sa)).astype(BF16)
        dgs_ref[...] = (dm * s_ref[...] * ss * (1.0 - ss)).astype(BF16)

    blk = pl.BlockSpec((tm, GROUP_W), lambda i, j: (i, j))
    shp = _sds((s, D_MODEL), BF16)
    return _call(body, name=name, out_shape=[shp] * 4, grid=(s // tm, nj),
                 in_specs=[pl.BlockSpec((tm, GROUP_W), lambda i, j: (i, COL_GA // GROUP_W + j)),
                           pl.BlockSpec((tm, GROUP_W), lambda i, j: (i, COL_GS // GROUP_W + j)), blk, blk, blk],
                 out_specs=[blk] * 4)(p, p, a, sbr, dmerged)


def _group_major(v):
    return v.reshape(SSD_GROUPS, 1, 8), v.reshape(SSD_GROUPS, 8, 1)


def mixer_forward(x, w):
    s = x.shape[0]
    h = rms_fwd(x, w["mix_norm"], "mix_rms")
    p = matmul_nn(h, w["w_in_main"], "mix_proj", BF16, tm=1024, tn=512)
    dt_raw = matmul_nn(h, w["w_in_dt"], "mix_proj_dt", F32, tm=1024, tn=DT_PAD)
    qn, kn = qk_norm_fwd(p, w["q_gain"], w["k_gain"], "qk_norm")
    streams, os_, lses = [], [], []
    for g, d in enumerate(ATTN_DILATIONS):
        cols = slice(g * GROUP_W, (g + 1) * GROUP_W)
        qs, ks = _to_streams(qn[:, cols], d), _to_streams(kn[:, cols], d)
        vs = _to_streams(p[:, COL_V + g * GROUP_W:COL_V + (g + 1) * GROUP_W], d)
        o, lse = attn_fwd(qs, ks, vs, g, f"attn_fwd{g}")
        streams.append((qs, ks, vs, lse))
        os_.append(_from_streams(o, d))
        lses.append(_from_streams(lse, d))
    attn_o = attn_merge_fwd(os_, lses, "attn_merge")
    cpre, xc = conv_fwd(p, w["conv_w"], w["conv_b"], "conv_fwd")
    dtg = dt_raw[:, :SSD_HEADS].reshape(s, SSD_GROUPS, 8).transpose(1, 0, 2)
    dtg_t = dtg.transpose(0, 2, 1)
    params = (*_group_major(w["dt_bias"]), *_group_major(w["a_log"]), _group_major(w["d_skip"])[0])
    y, s_in, hprev = ssd_fwd(p, xc, dtg, dtg_t, params, w["ssd_norm"], "ssd_fwd")
    a = matmul_nn(attn_o, w["w_attn_branch"], "attn_branch", F32, tm=1024, tn=512)
    sbr = matmul_nn(s_in, w["w_ssd_branch"], "ssd_branch", F32, tm=1024, tn=512)
    merged = merge_fwd(p, a, sbr, "merge")
    x_out = matmul_nn(merged, w["w_out"], "mix_out", F32, tm=1024, tn=512, res=x)
    saved = dict(h=h, p=p, streams=streams, os=os_, lses=lses, attn_o=attn_o, cpre=cpre, xc=xc, dtg=dtg,
                 dtg_t=dtg_t, params=params, y=y, s_in=s_in, hprev=hprev, a=a, sbr=sbr, merged=merged)
    return x_out, saved


def mixer_backward(dx_out, x, w, sv):
    s = x.shape[0]
    p = sv["p"]
    g = {}
    dmerged = matmul_nt(dx_out, w["w_out"], "d_merged", F32, tm=1024, tn=512, tk=1024)
    g["w_out"] = matmul_tn(sv["merged"], dx_out, "dw_out", tn=512, ts=1024)
    da, ds, dga, dgs = merge_bwd(p, sv["a"], sv["sbr"], dmerged, "merge_bwd")
    g["w_attn_branch"] = matmul_tn(sv["attn_o"], da, "dw_attn_branch", tn=512, ts=1024)
    g["w_ssd_branch"] = matmul_tn(sv["s_in"], ds, "dw_ssd_branch", tn=512, ts=1024)
    d_attn_o = matmul_nt(da, w["w_attn_branch"], "d_attn_o", F32, tm=1024, tn=512, tk=1024)
    d_sin = matmul_nt(ds, w["w_ssd_branch"], "d_ssd_in", F32, tm=1024, tn=512, tk=1024)
    bc_t = sv["xc"][:, SSD_INNER:].T
    dxs, d_b, d_c, dz, ddt, d_asum, d_bias, d_dsk, d_gn = ssd_bwd(
        p, sv["xc"], bc_t, sv["y"], d_sin, sv["hprev"], sv["dtg"], sv["dtg_t"], sv["params"], w["ssd_norm"], "ssd_bwd")
    dxc = jnp.concatenate([dxs, d_b, d_c], axis=1)
    dxbc, d_convw, d_convb = conv_bwd(p, sv["cpre"], dxc, w["conv_w"], "conv_bwd")
    g["conv_w"] = d_convw[:SSD_CONV]
    g["conv_b"] = d_convb
    g["dt_bias"] = d_bias.reshape(1, SSD_HEADS)
    g["a_log"] = (d_asum * (-jnp.exp(sv["params"][2]))).reshape(1, SSD_HEADS)
    g["d_skip"] = jnp.sum(d_dsk.reshape(SSD_HEADS, HD), axis=1).reshape(1, SSD_HEADS)
    g["ssd_norm"] = d_gn
    merged_bwd = attn_merge_bwd(d_attn_o, sv["os"], sv["lses"], "attn_merge_bwd")
    dqs, dks, dvs = [], [], []
    for gi, d in enumerate(ATTN_DILATIONS):
        qs, ks, vs, lse = sv["streams"][gi]
        d_o = _to_streams(merged_bwd[gi], d)
        cterm = _to_streams(merged_bwd[3 + gi], d)
        dq, dk, dv = attn_bwd(qs, ks, vs, d_o, cterm, lse, gi, f"attn_bwd{gi}")
        dqs.append(_from_streams(dq, d))
        dks.append(_from_streams(dk, d))
        dvs.append(_from_streams(dv, d).astype(BF16))
    dqn = jnp.concatenate(dqs, axis=1)
    dkn = jnp.concatenate(dks, axis=1)
    dq, dk, d_qg, d_kg = qk_norm_bwd(p, dqn, dkn, w["q_gain"], w["k_gain"], "qk_norm_bwd")
    g["q_norm"] = jnp.sum(d_qg.reshape(N_ATTN_HEADS, HD), axis=0).reshape(1, HD)
    g["k_norm"] = jnp.sum(d_kg.reshape(N_ATTN_HEADS, HD), axis=0).reshape(1, HD)
    dp = jnp.concatenate([dq, dk] + dvs + [dz, dxbc, dga, dgs], axis=1)
    ddt_pad = jnp.pad(ddt.transpose(1, 0, 2).reshape(s, SSD_HEADS), ((0, 0), (0, DT_PAD - SSD_HEADS)))
    g["w_in_main"] = matmul_tn(sv["h"], dp, "dw_in", tn=512, ts=1024)
    g["w_in_dt"] = matmul_tn(sv["h"], ddt_pad, "dw_in_dt", tn=DT_PAD, ts=1024)
    dh_main = matmul_nt(dp, w["w_in_main"], "d_mix_h", F32, tm=1024, tn=1024, tk=512)
    dh_dt = matmul_nt(ddt_pad, w["w_in_dt"], "d_mix_h_dt", F32, tm=1024, tn=1024, tk=DT_PAD)
    dx, g["mix_norm"] = rms_bwd([dh_main, dh_dt], x, w["mix_norm"], dx_out, "mix_drms")
    return dx, g


ANY = pl.BlockSpec(memory_space=pl.ANY)


def _place():
    x, y, c = lax.axis_index("x"), lax.axis_index("y"), lax.axis_index("c")
    chips = [(1 - x, y), (x, 1 - y), (1 - x, 1 - y)]
    return x, y, c, 2 * x + y, chips


def _comm_call(body, *, name, out_shape, n_in, scratch_shapes, aliases=None):
    return pl.pallas_call(
        body, out_shape=out_shape, in_specs=[ANY] * n_in, out_specs=[ANY] * len(out_shape),
        scratch_shapes=scratch_shapes, input_output_aliases=aliases or {}, name=name,
        compiler_params=pltpu.CompilerParams(has_side_effects=True))


def gather_weights(shards, small):
    n = len(shards)
    halves = [a.shape[0] // 2 for a in shards]
    out_shape = [_sds((N_CHIP,) + a.shape, a.dtype) for a in shards] + [_sds((N_CHIP,) + small.shape, small.dtype)]

    def body(*refs):
        ins, outs = refs[:n + 1], refs[n + 1:2 * n + 2]
        send1, recv1, send2, recv2, local = refs[2 * n + 2:]
        x, y, c, me, chips = _place()
        sibling = (x, y, 1 - c)

        def rows(k, chip, core):
            if k == n:
                return outs[k].at[chip]
            return outs[k].at[chip, pl.ds(core * halves[k], halves[k])]

        def level1(k, t, incoming):
            chip = 2 * chips[t][0] + chips[t][1]
            src = ins[k] if k == n else ins[k].at[pl.ds(c * halves[k], halves[k])]
            return pltpu.make_async_remote_copy(
                src_ref=src, dst_ref=rows(k, chip if incoming else me, c), send_sem=send1.at[3 * k + t],
                recv_sem=recv1.at[3 * k + t], device_id=(*chips[t], c), device_id_type=MESH)

        def level2(k, t, incoming):
            chip = 2 * chips[t][0] + chips[t][1]
            core = (1 - c) if incoming else c
            return pltpu.make_async_remote_copy(
                src_ref=rows(k, chip, core), dst_ref=rows(k, chip, core), send_sem=send2.at[3 * k + t],
                recv_sem=recv2.at[3 * k + t], device_id=sibling, device_id_type=MESH)

        own = [pltpu.make_async_copy(ins[k], outs[k].at[me], local.at[k]) for k in range(n + 1)]
        for cp in own:
            cp.start()
        first = [level1(k, t, False) for k in range(n + 1) for t in range(3)]
        for cp in first:
            cp.start()
        passed = []
        for k in range(n + 1):
            for t in range(3):
                level1(k, t, True).wait_recv()
                if k < n:
                    cp = level2(k, t, False)
                    cp.start()
                    passed.append(cp)
        for k in range(n):
            for t in range(3):
                level2(k, t, True).wait_recv()
        for cp in first + passed:
            cp.wait_send()
        for cp in own:
            cp.wait()

    dma = pltpu.SemaphoreType.DMA
    return _comm_call(body, name="gather_weights", out_shape=out_shape, n_in=n + 1,
                      scratch_shapes=[dma((3 * n + 3,)), dma((3 * n + 3,)), dma((3 * n,)), dma((3 * n,)),
                                      dma((n + 1,))])(*shards, small)


def reduce_to_sibling(grads):
    n = len(grads)
    halves = [a.shape[1] // 2 for a in grads]
    shapes = [_sds((N_CHIP, h, a.shape[2]), a.dtype) for a, h in zip(grads, halves)]

    def body(*refs):
        ins, got, kept = refs[:n], refs[n:2 * n], refs[2 * n:3 * n]
        send, recv, local = refs[3 * n:]
        x, y, c, _, _ = _place()
        copies, locals_ = [], []
        for k in range(n):
            h = halves[k]
            locals_.append(pltpu.make_async_copy(ins[k].at[:, pl.ds(c * h, h)], kept[k], local.at[k]))
            copies.append(pltpu.make_async_remote_copy(
                src_ref=ins[k].at[:, pl.ds((1 - c) * h, h)], dst_ref=got[k], send_sem=send.at[k], recv_sem=recv.at[k],
                device_id=(x, y, 1 - c), device_id_type=MESH))
        for cp in locals_ + copies:
            cp.start()
        for cp in copies:
            cp.wait_recv()
        for cp in copies:
            cp.wait_send()
        for cp in locals_:
            cp.wait()

    dma = pltpu.SemaphoreType.DMA
    res = _comm_call(body, name="reduce_to_sibling", out_shape=shapes + shapes, n_in=n,
                     scratch_shapes=[dma((n,)), dma((n,)), dma((n,))])(*grads)
    return res[:n], res[n:]


def reduce_to_owner(sums):
    n = len(sums)
    shapes = [_sds(a.shape, a.dtype) for a in sums]

    def body(*refs):
        ins, outs = refs[:n], refs[n:2 * n]
        send, recv, local = refs[2 * n:]
        x, y, c, me, chips = _place()
        copies, locals_ = [], []
        for k in range(n):
            locals_.append(pltpu.make_async_copy(ins[k].at[me], outs[k].at[3], local.at[k]))
            for t in range(3):
                chip = 2 * chips[t][0] + chips[t][1]
                copies.append(pltpu.make_async_remote_copy(
                    src_ref=ins[k].at[chip], dst_ref=outs[k].at[t], send_sem=send.at[3 * k + t],
                    recv_sem=recv.at[3 * k + t], device_id=(*chips[t], c), device_id_type=MESH))
        for cp in locals_ + copies:
            cp.start()
        for cp in copies:
            cp.wait_recv()
        for cp in copies:
            cp.wait_send()
        for cp in locals_:
            cp.wait()

    dma = pltpu.SemaphoreType.DMA
    return _comm_call(body, name="reduce_to_owner", out_shape=shapes, n_in=n,
                      scratch_shapes=[dma((3 * n,)), dma((3 * n,)), dma((n,))])(*sums)


def share_with_sibling(halves_):
    n = len(halves_)
    shapes = [_sds((2 * a.shape[0], a.shape[1]), a.dtype) for a in halves_]

    def body(*refs):
        ins, outs = refs[:n], refs[n:2 * n]
        send, recv, local = refs[2 * n:]
        x, y, c, _, _ = _place()
        copies, locals_ = [], []
        for k in range(n):
            h = ins[k].shape[0]
            mine = outs[k].at[pl.ds(c * h, h)]
            locals_.append(pltpu.make_async_copy(ins[k], mine, local.at[k]))
            copies.append(pltpu.make_async_remote_copy(
                src_ref=ins[k], dst_ref=mine, send_sem=send.at[k], recv_sem=recv.at[k],
                device_id=(x, y, 1 - c), device_id_type=MESH))
        for cp in locals_ + copies:
            cp.start()
        for cp in copies:
            cp.wait_recv()
        for cp in copies:
            cp.wait_send()
        for cp in locals_:
            cp.wait()

    dma = pltpu.SemaphoreType.DMA
    return _comm_call(body, name="share_with_sibling", out_shape=shapes, n_in=n,
                      scratch_shapes=[dma((n,)), dma((n,)), dma((n,))])(*halves_)


def _cores():
    c = lax.axis_index("c")
    return jnp.stack([c, 1 - c]).astype(jnp.int32)


def _staged_call(body, *, name, grid, in_specs, out_specs, out_shape, scratch_shapes):
    return pl.pallas_call(
        body, out_shape=out_shape, name=name,
        grid_spec=pltpu.PrefetchScalarGridSpec(num_scalar_prefetch=1, grid=grid, in_specs=in_specs,
                                               out_specs=out_specs, scratch_shapes=scratch_shapes),
        compiler_params=pltpu.CompilerParams(dimension_semantics=("arbitrary",) * len(grid),
                                             vmem_limit_bytes=V7X_VMEM_LIMIT, has_side_effects=True))


def gather_class(w, tm, name):
    r, cdim = w.shape
    h = r // 2
    nk = h // tm
    dma = pltpu.SemaphoreType.DMA

    def body(in_ref, out_ref, send1, recv1, send2, recv2, local):
        x, y, c, me, chips = _place()
        sibling = (x, y, 1 - c)
        chip_of = [2 * chips[t][0] + chips[t][1] for t in range(3)]

        def rows(chip, core, k):
            return out_ref.at[chip, pl.ds(core * h + k * tm, tm)]

        mine = [pltpu.make_async_copy(in_ref.at[pl.ds(c * h + k * tm, tm)], rows(me, c, k), local.at[k])
                for k in range(nk)]
        mine.append(pltpu.make_async_copy(in_ref.at[pl.ds((1 - c) * h, h)], out_ref.at[me, pl.ds((1 - c) * h, h)],
                                          local.at[nk]))
        for cp in mine:
            cp.start()

        def level1(t, k, incoming):
            place = rows(chip_of[t] if incoming else me, c, k)
            return pltpu.make_async_remote_copy(src_ref=place, dst_ref=place, send_sem=send1.at[t, k],
                                                recv_sem=recv1.at[t, k], device_id=(*chips[t], c), device_id_type=MESH)

        def level2(t, k, incoming):
            place = rows(chip_of[t], (1 - c) if incoming else c, k)
            return pltpu.make_async_remote_copy(src_ref=place, dst_ref=place, send_sem=send2.at[t, k],
                                                recv_sem=recv2.at[t, k], device_id=sibling, device_id_type=MESH)

        sent = []
        for k in range(nk):
            mine[k].wait()
            for t in range(3):
                cp = level1(t, k, False)
                cp.start()
                sent.append(cp)
        for k in range(nk):
            for t in range(3):
                level1(t, k, True).wait_recv()
                cp = level2(t, k, False)
                cp.start()
                sent.append(cp)
        for k in range(nk):
            for t in range(3):
                level2(t, k, True).wait_recv()
        for cp in sent:
            cp.wait_send()
        mine[nk].wait()

    return pl.pallas_call(
        body, out_shape=_sds((N_CHIP, r, cdim), w.dtype), in_specs=[ANY],
        out_specs=pl.BlockSpec(memory_space=pltpu.VMEM), name=name,
        scratch_shapes=[dma((3, nk)), dma((3, nk)), dma((3, nk)), dma((3, nk)), dma((nk + 1,))],
        compiler_params=pltpu.CompilerParams(vmem_limit_bytes=V7X_VMEM_LIMIT, has_side_effects=True))(w)


def sibling_sum(g, tm, name):
    _, r, cdim = g.shape
    h = r // 2
    ni = h // tm
    dma = pltpu.SemaphoreType.DMA

    def body(cores_ref, keep_ref, give_ref, out_ref, slot, send, recv):
        par = (pl.program_id(0) * ni + pl.program_id(1)) % 2
        x, y, c, _, _ = _place()
        cp = pltpu.make_async_remote_copy(src_ref=give_ref, dst_ref=slot.at[par], send_sem=send.at[par],
                                          recv_sem=recv.at[par], device_id=(x, y, 1 - c), device_id_type=MESH)
        cp.start()
        cp.wait_recv()
        out_ref[...] = (keep_ref[...].astype(F32) + slot[par].astype(F32)).astype(out_ref.dtype)
        cp.wait_send()

    flat = g.reshape(N_CHIP * r, cdim)
    return _staged_call(
        body, name=name, grid=(N_CHIP, ni),
        in_specs=[pl.BlockSpec((tm, cdim), lambda j, i, cores: ((2 * j + cores[0]) * ni + i, 0)),
                  pl.BlockSpec((tm, cdim), lambda j, i, cores: ((2 * j + cores[1]) * ni + i, 0))],
        out_specs=pl.BlockSpec((None, tm, cdim), lambda j, i, cores: (j, i, 0)),
        out_shape=_sds((N_CHIP, h, cdim), g.dtype),
        scratch_shapes=[pltpu.VMEM((2, tm, cdim), g.dtype), dma((2,)), dma((2,))],
    )(_cores(), flat, flat)


def owner_sum(sums, tm, name):
    _, h, cdim = sums.shape
    nk = h // tm
    dma = pltpu.SemaphoreType.DMA

    def body(in_ref, out_ref, got, send, recv, send2, recv2):
        x, y, c, me, chips = _place()
        sent = []
        for k in range(nk):
            for t in range(3):
                chip = 2 * chips[t][0] + chips[t][1]
                cp = pltpu.make_async_remote_copy(
                    src_ref=in_ref.at[chip, pl.ds(k * tm, tm)], dst_ref=got.at[t, pl.ds(k * tm, tm)],
                    send_sem=send.at[t, k], recv_sem=recv.at[t, k], device_id=(*chips[t], c), device_id_type=MESH)
                cp.start()
                sent.append(cp)
        shared = []
        for k in range(nk):
            rows = pl.ds(k * tm, tm)
            for t in range(3):
                sent[3 * k + t].wait_recv()
            acc = in_ref[me, rows, :].astype(F32)
            for t in range(3):
                acc = acc + got[t, rows, :].astype(F32)
            out_ref[c, rows, :] = acc
            cp = pltpu.make_async_remote_copy(
                src_ref=out_ref.at[c, rows], dst_ref=out_ref.at[c, rows], send_sem=send2.at[k], recv_sem=recv2.at[k],
                device_id=(x, y, 1 - c), device_id_type=MESH)
            cp.start()
            shared.append(cp)
        for cp in shared:
            cp.wait_recv()
        for cp in sent + shared:
            cp.wait_send()

    vmem = pl.BlockSpec(memory_space=pltpu.VMEM)
    out = pl.pallas_call(
        body, out_shape=_sds((2, h, cdim), F32), in_specs=[vmem], out_specs=vmem, name=name,
        scratch_shapes=[pltpu.VMEM((3, h, cdim), sums.dtype), dma((3, nk)), dma((3, nk)), dma((nk,)), dma((nk,))],
        compiler_params=pltpu.CompilerParams(vmem_limit_bytes=V7X_VMEM_LIMIT, has_side_effects=True))(sums)
    return out.reshape(2 * h, cdim)


def gather_conv_w(w):
    def body(in_ref, out_ref, send, recv):
        x, y, c, me, chips = _place()
        out_ref[me] = in_ref[...]
        copies = []
        for t in range(3):
            copies.append(pltpu.make_async_remote_copy(
                src_ref=out_ref.at[me], dst_ref=out_ref.at[me], send_sem=send.at[t], recv_sem=recv.at[t],
                device_id=(*chips[t], c), device_id_type=MESH))
        for cp in copies:
            cp.start()
        for cp in copies:
            cp.wait_recv()
        for cp in copies:
            cp.wait_send()

    dma = pltpu.SemaphoreType.DMA
    vmem = pl.BlockSpec(memory_space=pltpu.VMEM)
    return pl.pallas_call(
        body, out_shape=_sds((N_CHIP,) + w.shape, w.dtype), in_specs=[vmem], out_specs=vmem, name="gather_conv_w",
        scratch_shapes=[dma((3,)), dma((3,))],
        compiler_params=pltpu.CompilerParams(has_side_effects=True))(w)


N_DEV = 8
SMALL_ROWS = 24


def all_reduce_small(pack):
    def body(in_ref, out_ref, buf, send, recv):
        x, y, c, _, _ = _place()
        me = 4 * x + 2 * y + c
        buf[me] = in_ref[...]
        copies = []
        for r in range(1, N_DEV):
            px = (1 - x) if r & 4 else x
            py = (1 - y) if r & 2 else y
            pc = (1 - c) if r & 1 else c
            copies.append(pltpu.make_async_remote_copy(
                src_ref=buf.at[me], dst_ref=buf.at[me], send_sem=send.at[r - 1], recv_sem=recv.at[r - 1],
                device_id=(px, py, pc), device_id_type=MESH))
        for cp in copies:
            cp.start()
        for cp in copies:
            cp.wait_recv()
        for cp in copies:
            cp.wait_send()
        acc = buf[0]
        for j in range(1, N_DEV):
            acc = acc + buf[j]
        out_ref[...] = acc

    dma = pltpu.SemaphoreType.DMA
    vmem = pl.BlockSpec(memory_space=pltpu.VMEM)
    return pl.pallas_call(
        body, out_shape=_sds(pack.shape, F32), in_specs=[vmem], out_specs=vmem, name="all_reduce_small",
        scratch_shapes=[pltpu.VMEM((N_DEV,) + pack.shape, F32), dma((N_DEV - 1,)), dma((N_DEV - 1,))],
        compiler_params=pltpu.CompilerParams(has_side_effects=True))(pack)


def _row_tile(rows, limit, multiple):
    return max(t for t in range(multiple, min(rows, limit) + 1, multiple) if rows % t == 0)


def add_pair(a, b, name):
    _, h, c = a.shape

    def body(a_ref, b_ref, o_ref):
        o_ref[...] = (a_ref[...].astype(F32) + b_ref[...].astype(F32)).astype(o_ref.dtype)

    blk = pl.BlockSpec((None, h, c), lambda j: (j, 0, 0))
    return _call(body, name=name, out_shape=_sds(a.shape, a.dtype), grid=(N_CHIP,), in_specs=[blk, blk],
                 out_specs=blk)(a, b)


def sum_slots(buf, name):
    _, h, c = buf.shape
    tm = _row_tile(h, 256, 16)

    def body(b_ref, o_ref):
        acc = b_ref[3].astype(F32)
        for t in range(3):
            acc = acc + b_ref[t].astype(F32)
        o_ref[...] = acc

    return _call(body, name=name, out_shape=_sds((h, c), F32), grid=(h // tm,),
                 in_specs=[pl.BlockSpec((N_CHIP, tm, c), lambda i: (0, i, 0))],
                 out_specs=pl.BlockSpec((tm, c), lambda i: (i, 0)))(buf)


def adamw(w, g, row_off, m, v, name):
    r, c = w.shape
    tm = r if r < 8 else _row_tile(math.gcd(r, row_off) if row_off else r, 128, 8)
    c1 = 1.0 - ADAM_B1 ** ADAM_STEP
    c2 = 1.0 - ADAM_B2 ** ADAM_STEP

    def body(w_ref, g_ref, m_ref, v_ref, go_ref, d_ref, m2_ref, v2_ref):
        gv = g_ref[...]
        go_ref[...] = gv
        m2 = ADAM_B1 * m_ref[...] + (1.0 - ADAM_B1) * gv
        v2 = ADAM_B2 * v_ref[...] + (1.0 - ADAM_B2) * (gv * gv)
        m2_ref[...] = m2
        v2_ref[...] = v2
        d_ref[...] = -ADAM_LR * ((m2 / c1) / (jnp.sqrt(v2 / c2) + ADAM_EPS) + ADAM_WD * w_ref[...])

    blk = pl.BlockSpec((tm, c), lambda i: (i, 0))
    shp = _sds((r, c), F32)
    return _call(body, name=name, out_shape=[shp] * 4, grid=(r // tm,),
                 in_specs=[blk, pl.BlockSpec((tm, c), lambda i: (row_off // tm + i, 0)), blk, blk],
                 out_specs=[blk] * 4)(w, g, m, v)


BIG = ("ffn1_w_gate", "ffn1_w_up", "ffn1_w_down", "w_in", "w_attn_branch", "w_ssd_branch", "w_out",
       "ffn2_w_gate", "ffn2_w_up", "ffn2_w_down")
SMALL = ("ffn1_norm", "mix_norm", "q_norm", "k_norm", "conv_b", "dt_bias", "a_log", "d_skip", "ssd_norm", "ffn2_norm")
WEIGHTS = ("ffn1_norm", "ffn1_w_gate", "ffn1_w_up", "ffn1_w_down", "mix_norm", "w_in", "q_norm", "k_norm", "conv_w",
           "conv_b", "dt_bias", "a_log", "d_skip", "ssd_norm", "w_attn_branch", "w_ssd_branch", "w_out", "ffn2_norm",
           "ffn2_w_gate", "ffn2_w_up", "ffn2_w_down")
CONV_SHARD = SSD_CONV_DIM // N_CHIP
CLASSES = {
    "c704": (("ffn1_w_gate", 1024), ("ffn1_w_up", 1024), ("ffn2_w_gate", 1024), ("ffn2_w_up", 1024)),
    "c1024": (("ffn1_w_down", 704), ("ffn2_w_down", 704), ("w_ssd_branch", 512), ("w_out", 256)),
    "c2952": (("w_in", 1024),),
    "c256": (("w_attn_branch", 512),),
}
CLASS_TILE = {"c704": 512, "c1024": 272, "c2952": 128, "c256": 256}


def _pack_small(vals, conv_part, loss_part=None):
    flat = [vals[k].reshape(-1) for k in SMALL]
    flat.append(jnp.zeros((SSD_CONV * SSD_CONV_DIM,), F32) if conv_part is None else conv_part.reshape(-1))
    flat.append(jnp.zeros((1,), F32) if loss_part is None else loss_part.reshape(1))
    flat = jnp.concatenate(flat)
    return jnp.pad(flat, (0, SMALL_ROWS * D_MODEL - flat.shape[0])).reshape(SMALL_ROWS, D_MODEL)


def _unpack_small(pack, like):
    flat = pack.reshape(-1)
    out, off = {}, 0
    for k in SMALL:
        n = like[k].size
        out[k] = flat[off:off + n].reshape(like[k].shape)
        off += n
    conv = flat[off:off + SSD_CONV * SSD_CONV_DIM].reshape(SSD_CONV, SSD_CONV_DIM)
    return out, conv, flat[off + SSD_CONV * SSD_CONV_DIM]


def _chip_major_cols(a):
    r = a.shape[0]
    return a.reshape(r, N_CHIP, -1).transpose(1, 0, 2)


def _from_chip_major_cols(a):
    return a.transpose(1, 0, 2).reshape(a.shape[1], -1)


def kernel(x, ffn1_norm, ffn1_w_gate, ffn1_w_up, ffn1_w_down, mix_norm, w_in, q_norm, k_norm, conv_w, conv_b, dt_bias, a_log, d_skip, ssd_norm, w_attn_branch, w_ssd_branch, w_out, ffn2_norm, ffn2_w_gate, ffn2_w_up, ffn2_w_down, loss_target, m_ffn1_norm, m_ffn1_w_gate, m_ffn1_w_up, m_ffn1_w_down, m_mix_norm, m_w_in, m_q_norm, m_k_norm, m_conv_w, m_conv_b, m_dt_bias, m_a_log, m_d_skip, m_ssd_norm, m_w_attn_branch, m_w_ssd_branch, m_w_out, m_ffn2_norm, m_ffn2_w_gate, m_ffn2_w_up, m_ffn2_w_down, v_ffn1_norm, v_ffn1_w_gate, v_ffn1_w_up, v_ffn1_w_down, v_mix_norm, v_w_in, v_q_norm, v_k_norm, v_conv_w, v_conv_b, v_dt_bias, v_a_log, v_d_skip, v_ssd_norm, v_w_attn_branch, v_w_ssd_branch, v_w_out, v_ffn2_norm, v_ffn2_w_gate, v_ffn2_w_up, v_ffn2_w_down):
    env = dict(locals())
    wts = {k: env[k] for k in WEIGHTS}
    moms = {k: env["m_" + k] for k in WEIGHTS}
    vars_ = {k: env["v_" + k] for k in WEIGHTS}
    x0 = x[0]
    target = loss_target[0]

    full = {}
    for cls, members in CLASSES.items():
        local = jnp.concatenate([wts[k][0] for k, _ in members], axis=0).astype(BF16)
        full[cls] = gather_class(local, CLASS_TILE[cls], f"gather_{cls}")
    w704, w1024 = full["c704"], full["c1024"]
    w_in_full = _from_chip_major_cols(full["c2952"])
    mixer_w = dict(
        mix_norm=mix_norm,
        w_in_main=jnp.concatenate([w_in_full[:, :IN_DT0], w_in_full[:, IN_DT1:]], axis=1),
        w_in_dt=jnp.pad(w_in_full[:, IN_DT0:IN_DT1], ((0, 0), (0, DT_PAD - SSD_HEADS))),
        q_gain=jnp.tile(q_norm, (1, 2)), k_gain=jnp.tile(k_norm, (1, 2)),
        conv_w=_from_chip_major_cols(gather_conv_w(conv_w[0])), conv_b=conv_b, dt_bias=dt_bias, a_log=a_log,
        d_skip=d_skip, ssd_norm=ssd_norm, w_attn_branch=_from_chip_major_cols(full["c256"]),
        w_ssd_branch=w1024[:, 1408:1920].reshape(SSD_INNER, D_MODEL),
        w_out=w1024[:, 1920:2176].reshape(D_MODEL, D_MODEL))

    x1, saved1 = ffn_forward(x0, ffn1_norm, w704, w1024, 0, "ffn1")
    x2, saved_mix = mixer_forward(x1, mixer_w)
    x3, saved2 = ffn_forward(x2, ffn2_norm, w704, w1024, 1, "ffn2")
    dx3, sq = loss_grad(x3, target, "loss")
    grads = {}
    dx2, grads["ffn2_norm"], grads["ffn2_w_gate"], grads["ffn2_w_up"], grads["ffn2_w_down"] = ffn_backward(
        dx3, x2, ffn2_norm, w704, w1024, 1, saved2, "ffn2")
    dx1, gmix = mixer_backward(dx2, x1, mixer_w, saved_mix)
    dx0, grads["ffn1_norm"], grads["ffn1_w_gate"], grads["ffn1_w_up"], grads["ffn1_w_down"] = ffn_backward(
        dx1, x0, ffn1_norm, w704, w1024, 0, saved1, "ffn1")
    for k in ("mix_norm", "q_norm", "k_norm", "conv_b", "dt_bias", "a_log", "d_skip", "ssd_norm"):
        grads[k] = gmix[k]
    g_in = jnp.concatenate([gmix["w_in_main"][:, :IN_DT0], gmix["w_in_dt"][:, :SSD_HEADS],
                            gmix["w_in_main"][:, IN_DT0:]], axis=1)
    grads["w_in"] = _chip_major_cols(g_in)
    grads["w_attn_branch"] = _chip_major_cols(gmix["w_attn_branch"])
    grads["w_ssd_branch"] = gmix["w_ssd_branch"].reshape(N_CHIP, -1, D_MODEL)
    grads["w_out"] = gmix["w_out"].reshape(N_CHIP, -1, D_MODEL)

    reduced = {}
    for cls, members in CLASSES.items():
        part = jnp.concatenate([grads[k] for k, _ in members], axis=1) if len(members) > 1 else grads[members[0][0]]
        chip_sum = sibling_sum(part, CLASS_TILE[cls], f"sibling_sum_{cls}")
        reduced[cls] = owner_sum(chip_sum, CLASS_TILE[cls], f"owner_sum_{cls}")
    small_sum = all_reduce_small(_pack_small(grads, gmix["conv_w"], 0.5 * jnp.sum(sq) / D_MODEL))
    g_small, g_conv_full, loss = _unpack_small(small_sum, wts)
    chip = 2 * lax.axis_index("x") + lax.axis_index("y")
    g_conv = lax.dynamic_slice_in_dim(g_conv_full, chip * CONV_SHARD, CONV_SHARD, axis=1)

    g_final, delta, new_m, new_v = dict(g_small), {}, {}, {}

    def update(k, g_arr, row_off):
        shp = wts[k].shape
        two_d = shp[1:]
        res = adamw(wts[k].reshape(two_d), g_arr, row_off, moms[k].reshape(two_d), vars_[k].reshape(two_d),
                    f"adamw_{k}")
        g_final[k], delta[k], new_m[k], new_v[k] = (r.reshape(shp) for r in res)

    for cls, members in CLASSES.items():
        off = 0
        for k, rows in members:
            update(k, reduced[cls], off)
            off += rows
    update("conv_w", g_conv, 0)
    _, d, m2, v2 = adamw(_pack_small(wts, None), _pack_small(g_small, None), 0, _pack_small(moms, None),
                         _pack_small(vars_, None), "adamw_small")
    for res, packed in ((delta, d), (new_m, m2), (new_v, v2)):
        res.update(_unpack_small(packed, wts)[0])

    return (loss, dx0[None], *[g_final[k] for k in WEIGHTS], *[delta[k] for k in WEIGHTS],
            *[new_m[k] for k in WEIGHTS], *[new_v[k] for k in WEIGHTS])
```

```python
import collections
import functools
import math

import jax
import jax.numpy as jnp
from jax import lax
from jax.experimental import pallas as pl
from jax.experimental.pallas import tpu as pltpu

F32 = jnp.float32
BF16 = jnp.bfloat16
MESH = pl.DeviceIdType.MESH

EPS = 1e-6
D_MODEL = 1024
D_FF = 2816
N_CHIP = 4
FF_SHARD = D_FF // N_CHIP
HD = 64
BLK = 128
ATTN_DILATIONS = (1, 4, 16)
HEADS_PER_PATTERN = 8
N_ATTN_HEADS = 24
ALIBI_MAX_EXP = 8.0
ATTN_QKV = 1536
GROUP_W = 512
SSD_INNER = 2048
SSD_HEADS = 32
SSD_GROUPS = 4
SSD_CONV = 4
SSD_CONV_DIM = 3072
IN_COLS = 11808
IN_DT0, IN_DT1 = 9728, 9760
COL_K, COL_V, COL_Z, COL_XBC, COL_GA, COL_GS, P_COLS = 1536, 3072, 4608, 6656, 9728, 10752, 11776
DT_PAD = 128

ADAM_LR, ADAM_B1, ADAM_B2, ADAM_EPS, ADAM_WD, ADAM_STEP = 0.001, 0.9, 0.999, 1e-08, 0.01, 10

V7X_VMEM_LIMIT = 56 * 1024 * 1024
NEG = -1e30


Rider = collections.namedtuple("Rider", "arrays out_shape scratch start finish")
Rider.__doc__ = """An exchange between devices that rides in a compute kernel: its copies are started in the host's
first grid step and waited for in its last, so they travel while the host computes.  arrays / out_shape: extra HBM
operands and results; scratch: extra scratch; start, finish: f(in_refs, out_refs, scratch_refs)."""


def _call(body, *, name, out_shape, in_specs, out_specs, grid=(), scratch_shapes=(), aliases=None, rider=None):
    params = dict(dimension_semantics=("arbitrary",) * len(grid), vmem_limit_bytes=V7X_VMEM_LIMIT)
    if rider is None:
        return pl.pallas_call(
            body, out_shape=out_shape, grid=grid, in_specs=in_specs, out_specs=out_specs,
            scratch_shapes=scratch_shapes, input_output_aliases=aliases or {}, name=name,
            compiler_params=pltpu.CompilerParams(**params))
    single = not isinstance(out_shape, (list, tuple))
    main_out = [out_shape] if single else list(out_shape)
    main_specs = [out_specs] if single else list(out_specs)
    n_in, n_out, n_scr = len(in_specs), len(main_out), len(scratch_shapes)
    r_in, r_out = len(rider.arrays), len(rider.out_shape)

    def wrapped(*refs):
        ins, refs = refs[:n_in], refs[n_in:]
        r_ins, refs = refs[:r_in], refs[r_in:]
        outs, refs = refs[:n_out], refs[n_out:]
        r_outs, refs = refs[:r_out], refs[r_out:]
        scr, r_scr = refs[:n_scr], refs[n_scr:]
        first = last = None
        for axis, size in enumerate(grid):
            at_start, at_end = pl.program_id(axis) == 0, pl.program_id(axis) == size - 1
            first = at_start if first is None else jnp.logical_and(first, at_start)
            last = at_end if last is None else jnp.logical_and(last, at_end)

        @pl.when(first)
        def _():
            rider.start(r_ins, r_outs, r_scr)

        body(*ins, *outs, *scr)

        @pl.when(last)
        def _():
            rider.finish(r_ins, r_outs, r_scr)

    hbm = pl.BlockSpec(memory_space=pl.ANY)
    call = pl.pallas_call(
        wrapped, out_shape=main_out + list(rider.out_shape), grid=grid, in_specs=list(in_specs) + [hbm] * r_in,
        out_specs=main_specs + [hbm] * r_out, scratch_shapes=list(scratch_shapes) + list(rider.scratch), name=name,
        compiler_params=pltpu.CompilerParams(has_side_effects=True, **params))

    def run(*args):
        res = call(*args, *rider.arrays)
        main = res[:n_out]
        return (main[0] if single else main), res[n_out:]

    return run


def _sds(shape, dtype):
    return jax.ShapeDtypeStruct(tuple(shape), dtype)


def _dot(a, b):
    return jnp.dot(a, b, preferred_element_type=F32)


def _dot_nt(a, b):
    return lax.dot_general(a, b, (((1,), (1,)), ((), ())), preferred_element_type=F32)


def _dot_tn(a, b):
    return lax.dot_general(a, b, (((0,), (0,)), ((), ())), preferred_element_type=F32)


def _dot_hi(a, b):
    return jnp.dot(a, b, preferred_element_type=F32, precision=lax.Precision.HIGHEST)


def _sigmoid(x):
    return 1.0 / (1.0 + jnp.exp(-x))


def _lane_first_half(shape):
    return lax.broadcasted_iota(jnp.int32, shape, len(shape) - 1) < HD


def _pair_sum(x, first):
    s_all = jnp.sum(x, axis=-1, keepdims=True)
    s_a = jnp.sum(jnp.where(first, x, 0.0), axis=-1, keepdims=True)
    return s_a, s_all - s_a


def _rowwise(name, fn, rows, consts, outs, accs=(), tm=512):
    n_rows = None
    in_arrays, in_specs = [], []
    for r in rows:
        if isinstance(r, tuple):
            arr, w, cb = r
            spec = pl.BlockSpec((tm, w), functools.partial(lambda i, cb: (i, cb), cb=cb))
        else:
            arr = r
            spec = pl.BlockSpec((tm, arr.shape[1]), lambda i: (i, 0))
        n_rows = arr.shape[0]
        in_arrays.append(arr)
        in_specs.append(spec)
    for c in consts:
        in_arrays.append(c)
        in_specs.append(pl.BlockSpec(c.shape, functools.partial(lambda i, n: (0,) * n, n=c.ndim)))
    out_shape = [_sds(s, d) for s, d in outs] + [_sds(s, d) for s, d in accs]
    out_specs = [pl.BlockSpec((tm, s[1]), lambda i: (i, 0)) for s, _ in outs]
    out_specs += [pl.BlockSpec(s, functools.partial(lambda i, n: (0,) * n, n=len(s))) for s, _ in accs]

    def body(*refs):
        fn(pl.program_id(0), *refs)

    res = _call(body, name=name, out_shape=out_shape, grid=(n_rows // tm,), in_specs=in_specs,
                out_specs=out_specs)(*in_arrays)
    return res


def rms_fwd(x, gain, name):
    def fn(i, x_ref, g_ref, h_ref):
        xv = x_ref[...]
        r = lax.rsqrt(jnp.mean(xv * xv, axis=-1, keepdims=True) + EPS)
        h_ref[...] = (xv * r * g_ref[...]).astype(h_ref.dtype)

    return _rowwise(name, fn, [x], [gain], [(x.shape, BF16)])[0]


def rms_bwd(dhs, x, gain, dx_in, name):
    n = len(dhs)

    def fn(i, *refs):
        dh_refs, (x_ref, dxin_ref, g_ref, dx_ref, dg_ref) = refs[:n], refs[n:]
        dh = dh_refs[0][...]
        for r in dh_refs[1:]:
            dh = dh + r[...]
        xv = x_ref[...]
        r = lax.rsqrt(jnp.mean(xv * xv, axis=-1, keepdims=True) + EPS)
        xn = xv * r
        dxn = dh * g_ref[...]
        dx_ref[...] = dxin_ref[...] + r * (dxn - xn * jnp.mean(dxn * xn, axis=-1, keepdims=True))

        @pl.when(i == 0)
        def _():
            dg_ref[...] = jnp.zeros_like(dg_ref)

        dg_ref[...] += jnp.sum(dh * xn, axis=0, keepdims=True)

    return _rowwise(name, fn, list(dhs) + [x, dx_in], [gain], [(x.shape, F32)], [((1, x.shape[1]), F32)])


def loss_grad(y, target, name):
    def fn(i, y_ref, t_ref, dy_ref, sq_ref):
        err = y_ref[...] - t_ref[...]
        dy_ref[...] = err * (1.0 / y_ref.shape[1])

        @pl.when(i == 0)
        def _():
            sq_ref[...] = jnp.zeros_like(sq_ref)

        sq_ref[...] += jnp.sum(err * err, axis=0, keepdims=True)

    return _rowwise(name, fn, [y, target], [], [(y.shape, F32)], [((1, y.shape[1]), F32)])


def matmul_nn(a, b, name, out_dtype, tm, tn, res=None, scale=1.0, rider=None):
    s, k = a.shape
    n = b.shape[1]

    def body(*refs):
        if res is None:
            a_ref, b_ref, o_ref = refs
            o_ref[...] = _dot(a_ref[...], b_ref[...]).astype(o_ref.dtype)
        else:
            a_ref, b_ref, r_ref, o_ref = refs
            o_ref[...] = (r_ref[...] + scale * _dot(a_ref[...], b_ref[...])).astype(o_ref.dtype)

    in_specs = [pl.BlockSpec((tm, k), lambda i, j: (i, 0)), pl.BlockSpec((k, tn), lambda i, j: (0, j))]
    args = [a, b]
    if res is not None:
        in_specs.append(pl.BlockSpec((tm, tn), lambda i, j: (i, j)))
        args.append(res)
    return _call(body, name=name, out_shape=_sds((s, n), out_dtype), grid=(s // tm, n // tn), in_specs=in_specs,
                 out_specs=pl.BlockSpec((tm, tn), lambda i, j: (i, j)), rider=rider)(*args)


def matmul_nt(a, b, name, out_dtype, tm, tn, tk, rider=None):
    s, k = a.shape
    n = b.shape[0]
    nk = k // tk

    def body(a_ref, b_ref, o_ref, acc_ref):
        kk = pl.program_id(2)

        @pl.when(kk == 0)
        def _():
            acc_ref[...] = jnp.zeros_like(acc_ref)

        acc_ref[...] += _dot_nt(a_ref[...].astype(BF16), b_ref[...])

        @pl.when(kk == nk - 1)
        def _():
            o_ref[...] = acc_ref[...].astype(o_ref.dtype)

    return _call(body, name=name, out_shape=_sds((s, n), out_dtype), grid=(s // tm, n // tn, nk),
                 in_specs=[pl.BlockSpec((tm, tk), lambda i, j, kk: (i, kk)),
                           pl.BlockSpec((tn, tk), lambda i, j, kk: (j, kk))],
                 out_specs=pl.BlockSpec((tm, tn), lambda i, j, kk: (i, j)),
                 scratch_shapes=[pltpu.VMEM((tm, tn), F32)], rider=rider)(a, b)


def matmul_tn(a, b, name, tn, ts, a_scale=None, b_scale=None, rider=None):
    s, m = a.shape
    n = b.shape[1]
    ns = s // ts

    def body(a_ref, b_ref, o_ref, acc_ref):
        ss = pl.program_id(1)

        @pl.when(ss == 0)
        def _():
            acc_ref[...] = jnp.zeros_like(acc_ref)

        av, bv = a_ref[...], b_ref[...]
        if a_scale is not None:
            av = av * a_scale
        if b_scale is not None:
            bv = bv * b_scale
        acc_ref[...] += _dot_tn(av.astype(BF16), bv.astype(BF16))

        @pl.when(ss == ns - 1)
        def _():
            o_ref[...] = acc_ref[...].astype(o_ref.dtype)

    return _call(body, name=name, out_shape=_sds((m, n), BF16), grid=(n // tn, ns),
                 in_specs=[pl.BlockSpec((ts, m), lambda j, ss: (ss, 0)), pl.BlockSpec((ts, tn), lambda j, ss: (ss, j))],
                 out_specs=pl.BlockSpec((m, tn), lambda j, ss: (0, j)),
                 scratch_shapes=[pltpu.VMEM((m, tn), F32)], rider=rider)(a, b)


def ffn_up(h, w704, gate_blk, up_blk, name, tm=512, rider=None):
    s = h.shape[0]

    def body(h_ref, wg_ref, wu_ref, g_ref, u_ref, a_ref):
        hv = h_ref[...]
        g = _dot(hv, wg_ref[...])
        u = _dot(hv, wu_ref[...])
        g_ref[...] = g.astype(BF16)
        u_ref[...] = u.astype(BF16)
        a_ref[...] = (g * _sigmoid(g) * u).astype(BF16)

    ospec = pl.BlockSpec((None, tm, FF_SHARD), lambda j, i: (j, i, 0))
    shp = _sds((N_CHIP, s, FF_SHARD), BF16)
    return _call(body, name=name, out_shape=[shp, shp, shp], grid=(N_CHIP, s // tm),
                 in_specs=[pl.BlockSpec((tm, D_MODEL), lambda j, i: (i, 0)),
                           pl.BlockSpec((None, D_MODEL, FF_SHARD), lambda j, i: (j, gate_blk, 0)),
                           pl.BlockSpec((None, D_MODEL, FF_SHARD), lambda j, i: (j, up_blk, 0))],
                 out_specs=[ospec, ospec, ospec], rider=rider)(h, w704, w704)


def ffn_down(a, w1024, blk, x, name, tm=512):
    s = x.shape[0]

    def body(a_ref, wd_ref, x_ref, o_ref):
        acc = _dot(a_ref[0], wd_ref[0])
        for j in range(1, N_CHIP):
            acc += _dot(a_ref[j], wd_ref[j])
        o_ref[...] = x_ref[...] + 0.5 * acc

    return _call(body, name=name, out_shape=_sds((s, D_MODEL), F32), grid=(s // tm,),
                 in_specs=[pl.BlockSpec((N_CHIP, tm, FF_SHARD), lambda i: (0, i, 0)),
                           pl.BlockSpec((N_CHIP, FF_SHARD, D_MODEL), lambda i: (0, blk, 0)),
                           pl.BlockSpec((tm, D_MODEL), lambda i: (i, 0))],
                 out_specs=pl.BlockSpec((tm, D_MODEL), lambda i: (i, 0)))(a, w1024, x)


def ffn_bwd_hidden(dx, w1024, blk, g, u, name, tm=1024, rider=None):
    s = dx.shape[0]

    def body(dx_ref, wd_ref, g_ref, u_ref, dg_ref, du_ref):
        dy = (0.5 * dx_ref[...]).astype(BF16)
        da = _dot_nt(dy, wd_ref[...])
        gv = g_ref[...].astype(F32)
        uv = u_ref[...].astype(F32)
        sg = _sigmoid(gv)
        dg_ref[...] = (da * uv * (sg * (1.0 + gv * (1.0 - sg)))).astype(BF16)
        du_ref[...] = (da * gv * sg).astype(BF16)

    hspec = pl.BlockSpec((None, tm, FF_SHARD), lambda j, i: (j, i, 0))
    shp = _sds((N_CHIP, s, FF_SHARD), BF16)
    return _call(body, name=name, out_shape=[shp, shp], grid=(N_CHIP, s // tm),
                 in_specs=[pl.BlockSpec((tm, D_MODEL), lambda j, i: (i, 0)),
                           pl.BlockSpec((None, FF_SHARD, D_MODEL), lambda j, i: (j, blk, 0)), hspec, hspec],
                 out_specs=[hspec, hspec], rider=rider)(dx, w1024, g, u)


def ffn_bwd_input(dg, du, w704, gate_blk, up_blk, name, tm=512, rider=None):
    s = dg.shape[1]

    def body(dg_ref, du_ref, wg_ref, wu_ref, o_ref):
        acc = _dot_nt(dg_ref[0], wg_ref[0]) + _dot_nt(du_ref[0], wu_ref[0])
        for j in range(1, N_CHIP):
            acc += _dot_nt(dg_ref[j], wg_ref[j]) + _dot_nt(du_ref[j], wu_ref[j])
        o_ref[...] = acc

    hspec = pl.BlockSpec((N_CHIP, tm, FF_SHARD), lambda i: (0, i, 0))
    return _call(body, name=name, out_shape=_sds((s, D_MODEL), F32), grid=(s // tm,),
                 in_specs=[hspec, hspec,
                           pl.BlockSpec((N_CHIP, D_MODEL, FF_SHARD), lambda i: (0, gate_blk, 0), pl.Buffered(1)),
                           pl.BlockSpec((N_CHIP, D_MODEL, FF_SHARD), lambda i: (0, up_blk, 0), pl.Buffered(1))],
                 out_specs=pl.BlockSpec((tm, D_MODEL), lambda i: (i, 0)), rider=rider)(dg, du, w704, w704)


def ffn_wgrad_in(h, dgu, name, ts=1024):
    s = h.shape[0]
    ns = s // ts

    def body(h_ref, d_ref, o_ref, acc_ref):
        ss = pl.program_id(1)

        @pl.when(ss == 0)
        def _():
            acc_ref[...] = jnp.zeros_like(acc_ref)

        acc_ref[...] += _dot_tn(h_ref[...], d_ref[...])

        @pl.when(ss == ns - 1)
        def _():
            o_ref[...] = acc_ref[...].astype(BF16)

    return _call(body, name=name, out_shape=_sds((N_CHIP, D_MODEL, FF_SHARD), BF16), grid=(N_CHIP, ns),
                 in_specs=[pl.BlockSpec((ts, D_MODEL), lambda j, ss: (ss, 0)),
                           pl.BlockSpec((None, ts, FF_SHARD), lambda j, ss: (j, ss, 0))],
                 out_specs=pl.BlockSpec((None, D_MODEL, FF_SHARD), lambda j, ss: (j, 0, 0)),
                 scratch_shapes=[pltpu.VMEM((D_MODEL, FF_SHARD), F32)])(h, dgu)


def ffn_wgrad_down(a, dx, name, ts=1024):
    s = dx.shape[0]
    ns = s // ts

    def body(a_ref, dx_ref, o_ref, acc_ref):
        ss = pl.program_id(1)

        @pl.when(ss == 0)
        def _():
            acc_ref[...] = jnp.zeros_like(acc_ref)

        acc_ref[...] += _dot_tn(a_ref[...], (0.5 * dx_ref[...]).astype(BF16))

        @pl.when(ss == ns - 1)
        def _():
            o_ref[...] = acc_ref[...].astype(BF16)

    return _call(body, name=name, out_shape=_sds((N_CHIP, FF_SHARD, D_MODEL), BF16), grid=(N_CHIP, ns),
                 in_specs=[pl.BlockSpec((None, ts, FF_SHARD), lambda j, ss: (j, ss, 0)),
                           pl.BlockSpec((ts, D_MODEL), lambda j, ss: (ss, 0))],
                 out_specs=pl.BlockSpec((None, FF_SHARD, D_MODEL), lambda j, ss: (j, 0, 0)),
                 scratch_shapes=[pltpu.VMEM((FF_SHARD, D_MODEL), F32)])(a, dx)


def ffn_forward(x, gain, w704, get_w1024, tag, rider=None):
    h = rms_fwd(x, gain, f"{tag}_rms")
    res = ffn_up(h, w704, 0, 1, f"{tag}_up", rider=rider)
    (g, u, a), rode = res if rider is not None else (res, None)
    y = ffn_down(a, get_w1024(rode), 0, x, f"{tag}_down")
    return y, (h, g, u, a), rode


def ffn_backward(dy, x, gain, w704, w1024, saved, tag, ride_down=None, ride_in=None):
    h, g, u, a = saved
    d_wd = ffn_wgrad_down(a, dy, f"{tag}_dwd")
    if ride_down is not None:
        (dg, du), d_wd = ffn_bwd_hidden(dy, w1024, 0, g, u, f"{tag}_dhid", rider=ride_down(d_wd))
    else:
        dg, du = ffn_bwd_hidden(dy, w1024, 0, g, u, f"{tag}_dhid")
    d_win = jnp.concatenate([ffn_wgrad_in(h, dg, f"{tag}_dwg"), ffn_wgrad_in(h, du, f"{tag}_dwu")], axis=1)
    if ride_in is not None:
        dh, d_win = ffn_bwd_input(dg, du, w704, 0, 1, f"{tag}_dh", rider=ride_in(d_win))
    else:
        dh = ffn_bwd_input(dg, du, w704, 0, 1, f"{tag}_dh")
    dx, d_gain = rms_bwd([dh], x, gain, dy, f"{tag}_drms")
    return dx, d_gain, d_win, d_wd


def _alibi_slope(head):
    return float(2.0 ** (-ALIBI_MAX_EXP * (head + 1) / N_ATTN_HEADS))


def _head_norm(t, gain_pair, first):
    sa, sb = _pair_sum(t * t, first)
    r = jnp.where(first, lax.rsqrt(sa * (1.0 / HD) + EPS), lax.rsqrt(sb * (1.0 / HD) + EPS))
    return t * r * gain_pair, r


def qk_norm_fwd(p, q_gain, k_gain, name):
    s = p.shape[0]

    def fn(i, q_ref, k_ref, qg_ref, kg_ref, qn_ref, kn_ref):
        first = _lane_first_half((q_ref.shape[0], 2 * HD))
        for src, g_ref, dst in ((q_ref, qg_ref, qn_ref), (k_ref, kg_ref, kn_ref)):
            for pr in range(ATTN_QKV // (2 * HD)):
                cols = slice(pr * 2 * HD, (pr + 1) * 2 * HD)
                y, _ = _head_norm(src[:, cols].astype(F32), g_ref[...], first)
                dst[:, cols] = y.astype(BF16)

    return _rowwise(name, fn, [(p, ATTN_QKV, 0), (p, ATTN_QKV, 1)], [q_gain, k_gain],
                    [((s, ATTN_QKV), BF16), ((s, ATTN_QKV), BF16)])


def qk_norm_bwd(p, dqn, dkn, q_gain, k_gain, name):
    s = p.shape[0]

    def fn(i, q_ref, k_ref, dqn_ref, dkn_ref, qg_ref, kg_ref, dq_ref, dk_ref, dqg_ref, dkg_ref):
        first = _lane_first_half((q_ref.shape[0], 2 * HD))

        @pl.when(i == 0)
        def _():
            dqg_ref[...] = jnp.zeros_like(dqg_ref)
            dkg_ref[...] = jnp.zeros_like(dkg_ref)

        for src, d_ref, g_ref, dst, dg_ref in ((q_ref, dqn_ref, qg_ref, dq_ref, dqg_ref),
                                               (k_ref, dkn_ref, kg_ref, dk_ref, dkg_ref)):
            for pr in range(ATTN_QKV // (2 * HD)):
                cols = slice(pr * 2 * HD, (pr + 1) * 2 * HD)
                t = src[:, cols].astype(F32)
                sa, sb = _pair_sum(t * t, first)
                r = jnp.where(first, lax.rsqrt(sa * (1.0 / HD) + EPS), lax.rsqrt(sb * (1.0 / HD) + EPS))
                xn = t * r
                dy = d_ref[:, cols]
                dg_ref[:, cols] += jnp.sum(dy * xn, axis=0, keepdims=True)
                dxn = dy * g_ref[...]
                ma, mb = _pair_sum(dxn * xn, first)
                mean = jnp.where(first, ma, mb) * (1.0 / HD)
                dst[:, cols] = (r * (dxn - xn * mean)).astype(BF16)

    return _rowwise(name, fn, [(p, ATTN_QKV, 0), (p, ATTN_QKV, 1), dqn, dkn], [q_gain, k_gain],
                    [((s, ATTN_QKV), BF16), ((s, ATTN_QKV), BF16)], [((1, ATTN_QKV), F32), ((1, ATTN_QKV), F32)])


def _to_streams(a, d):
    if d == 1:
        return a
    s, c = a.shape
    return a.reshape(s // d, d, c).transpose(1, 0, 2).reshape(s, c)


def _from_streams(a, d):
    if d == 1:
        return a
    s, c = a.shape
    return a.reshape(d, s // d, c).transpose(1, 0, 2).reshape(s, c)


def _attn_masks():
    row = lax.broadcasted_iota(jnp.int32, (BLK, BLK), 0)
    col = lax.broadcasted_iota(jnp.int32, (BLK, BLK), 1)
    rel_diag = row - col
    rel_prev = rel_diag + BLK
    return rel_diag, rel_prev


def attn_fwd(q, k, v, pattern, name, tq=512):
    s = q.shape[0]
    d = ATTN_DILATIONS[pattern]
    blocks_per_stream = (s // d) // BLK
    nsb = tq // BLK

    def body(q_ref, k_ref, v_ref, kp_ref, vp_ref, o_ref, l_ref):
        i = pl.program_id(0)
        rel_diag, rel_prev = _attn_masks()
        first = _lane_first_half((BLK, 2 * HD))
        rd_f = (rel_diag * d).astype(F32)
        rp_f = (rel_prev * d).astype(F32)
        for sb in range(nsb):
            rows = slice(sb * BLK, (sb + 1) * BLK)
            has_prev = ((i * nsb + sb) % blocks_per_stream != 0).astype(jnp.int32)
            m_diag = rel_diag >= 0
            m_prev = (rel_prev + (1 - has_prev) * (4 * BLK)) <= BLK
            for pr in range(GROUP_W // (2 * HD)):
                cols = slice(pr * 2 * HD, (pr + 1) * 2 * HD)
                qp = q_ref[rows, cols]
                kc, vc = k_ref[rows, cols], v_ref[rows, cols]
                if sb == 0:
                    kp, vp = kp_ref[:, cols], vp_ref[:, cols]
                else:
                    prows = slice((sb - 1) * BLK, sb * BLK)
                    kp, vp = k_ref[prows, cols], v_ref[prows, cols]
                outs, lses = [], []
                for e in range(2):
                    slope = _alibi_slope(pattern * HEADS_PER_PATTERN + 2 * pr + e)
                    qm = jnp.where(first if e == 0 else jnp.logical_not(first), qp, jnp.zeros_like(qp))
                    s1 = jnp.where(m_diag, _dot_nt(qm, kc) * 0.125 - slope * rd_f, NEG)
                    s0 = jnp.where(m_prev, _dot_nt(qm, kp) * 0.125 - slope * rp_f, NEG)
                    m = jnp.maximum(jnp.max(s1, axis=-1, keepdims=True), jnp.max(s0, axis=-1, keepdims=True))
                    p1 = jnp.exp(s1 - m)
                    p0 = jnp.exp(s0 - m)
                    l = jnp.sum(p1, axis=-1, keepdims=True) + jnp.sum(p0, axis=-1, keepdims=True)
                    inv = 1.0 / l
                    outs.append(_dot((p1 * inv).astype(BF16), vc) + _dot((p0 * inv).astype(BF16), vp))
                    lses.append(m + jnp.log(l))
                o_ref[rows, cols] = jnp.where(first, outs[0], outs[1])
                l_ref[rows, cols] = jnp.where(first, lses[0], lses[1])

    cur = pl.BlockSpec((tq, GROUP_W), lambda i: (i, 0))
    prev = pl.BlockSpec((BLK, GROUP_W), lambda i: (jnp.maximum(i * nsb - 1, 0), 0))
    return _call(body, name=name, out_shape=[_sds((s, GROUP_W), F32), _sds((s, GROUP_W), F32)], grid=(s // tq,),
                 in_specs=[cur, cur, cur, prev, prev], out_specs=[cur, cur])(q, k, v, k, v)


def attn_merge_fwd(os_, lses, name):
    s = os_[0].shape[0]

    def fn(i, o0, o1, o2, l0, l1, l2, out_ref):
        m = jnp.maximum(jnp.maximum(l0[...], l1[...]), l2[...])
        e0, e1, e2 = jnp.exp(l0[...] - m), jnp.exp(l1[...] - m), jnp.exp(l2[...] - m)
        inv = 1.0 / (e0 + e1 + e2)
        out_ref[...] = ((e0 * inv) * o0[...] + (e1 * inv) * o1[...] + (e2 * inv) * o2[...]).astype(BF16)

    return _rowwise(name, fn, list(os_) + list(lses), [], [((s, GROUP_W), BF16)])[0]


def attn_merge_bwd(d_out, os_, lses, name):
    s = d_out.shape[0]

    def fn(i, do_ref, o0, o1, o2, l0, l1, l2, d0, d1, d2, c0, c1, c2):
        first = _lane_first_half((do_ref.shape[0], 2 * HD))
        m = jnp.maximum(jnp.maximum(l0[...], l1[...]), l2[...])
        e0, e1, e2 = jnp.exp(l0[...] - m), jnp.exp(l1[...] - m), jnp.exp(l2[...] - m)
        inv = 1.0 / (e0 + e1 + e2)
        w0, w1, w2 = e0 * inv, e1 * inv, e2 * inv
        do = do_ref[...]
        prod = do * (w0 * o0[...] + w1 * o1[...] + w2 * o2[...])
        for pr in range(GROUP_W // (2 * HD)):
            cols = slice(pr * 2 * HD, (pr + 1) * 2 * HD)
            ta, tb = _pair_sum(prod[:, cols], first)
            t = jnp.where(first, ta, tb)
            for w, c_ref in ((w0, c0), (w1, c1), (w2, c2)):
                c_ref[:, cols] = w[:, cols] * t
        for w, d_ref in ((w0, d0), (w1, d1), (w2, d2)):
            d_ref[...] = (w * do).astype(BF16)

    shp = (s, GROUP_W)
    return _rowwise(name, fn, [d_out] + list(os_) + list(lses), [],
                    [(shp, BF16)] * 3 + [(shp, F32)] * 3)


def attn_bwd(q, k, v, d_o, cterm, lse, pattern, name, tq=512):
    s = q.shape[0]
    d = ATTN_DILATIONS[pattern]
    blocks_per_stream = (s // d) // BLK
    nsb = tq // BLK
    n_blocks = s // BLK

    def body(q_ref, k_ref, v_ref, do_ref, c_ref, l_ref, kp_ref, vp_ref, qn_ref, don_ref, cn_ref, ln_ref,
             dq_ref, dk_ref, dv_ref):
        i = pl.program_id(0)
        rel_diag, rel_prev = _attn_masks()
        first = _lane_first_half((BLK, 2 * HD))
        second = jnp.logical_not(first)
        rd_f = (rel_diag * d).astype(F32)
        rp_f = (rel_prev * d).astype(F32)
        m_diag = rel_diag >= 0
        dq_ref[...] = jnp.zeros_like(dq_ref)
        dk_ref[...] = jnp.zeros_like(dk_ref)
        dv_ref[...] = jnp.zeros_like(dv_ref)

        def pair(qp, dop, cp, lp, kp, vp, rel_f, mask):
            dq = dk = dv = None
            for e in range(2):
                lanes = first if e == 0 else second
                slope = slopes[e]
                qm = jnp.where(lanes, qp, jnp.zeros_like(qp))
                dom = jnp.where(lanes, dop, jnp.zeros_like(dop))
                km = jnp.where(lanes, kp, jnp.zeros_like(kp))
                sc = jnp.where(mask, _dot_nt(qm, kp) * 0.125 - slope * rel_f, NEG)
                pm = jnp.exp(sc - lp[:, e * HD:e * HD + 1])
                dl = pm * (_dot_nt(dom, vp) - cp[:, e * HD:e * HD + 1])
                dl16 = dl.astype(BF16)
                t_dq = _dot(dl16, km)
                t_dk = _dot_tn(dl16, qm)
                t_dv = _dot_tn(pm.astype(BF16), dom)
                dq = t_dq if dq is None else dq + t_dq
                dk = t_dk if dk is None else dk + t_dk
                dv = t_dv if dv is None else dv + t_dv
            return dq * 0.125, dk * 0.125, dv

        for pr in range(GROUP_W // (2 * HD)):
            cols = slice(pr * 2 * HD, (pr + 1) * 2 * HD)
            slopes = [_alibi_slope(pattern * HEADS_PER_PATTERN + 2 * pr + e) for e in range(2)]
            for sb in range(nsb + 1):
                gb = i * nsb + sb
                if sb < nsb:
                    rows = slice(sb * BLK, (sb + 1) * BLK)
                    qp, dop, cp, lp = q_ref[rows, cols], do_ref[rows, cols], c_ref[rows, cols], l_ref[rows, cols]
                else:
                    qp, dop, cp, lp = qn_ref[:, cols], don_ref[:, cols], cn_ref[:, cols], ln_ref[:, cols]
                if sb < nsb:
                    dq1, dk1, dv1 = pair(qp, dop, cp, lp, k_ref[rows, cols], v_ref[rows, cols], rd_f, m_diag)
                    dq_ref[rows, cols] += dq1
                    dk_ref[rows, cols] += dk1
                    dv_ref[rows, cols] += dv1
                valid = jnp.logical_and(gb % blocks_per_stream != 0, gb < n_blocks).astype(jnp.int32)
                m_prev = jnp.logical_and(rel_prev <= BLK, (rel_prev + (1 - valid) * (4 * BLK)) <= BLK)
                if sb == 0:
                    kp, vp = kp_ref[:, cols], vp_ref[:, cols]
                else:
                    prows = slice((sb - 1) * BLK, sb * BLK)
                    kp, vp = k_ref[prows, cols], v_ref[prows, cols]
                dq0, dk0, dv0 = pair(qp, dop, cp, lp, kp, vp, rp_f, m_prev)
                if sb < nsb:
                    dq_ref[rows, cols] += dq0
                if sb > 0:
                    dk_ref[prows, cols] += dk0
                    dv_ref[prows, cols] += dv0

    cur = pl.BlockSpec((tq, GROUP_W), lambda i: (i, 0))
    prev = pl.BlockSpec((BLK, GROUP_W), lambda i: (jnp.maximum(i * nsb - 1, 0), 0))
    nxt = pl.BlockSpec((BLK, GROUP_W), lambda i: (jnp.minimum((i + 1) * nsb, n_blocks - 1), 0))
    shp = _sds((s, GROUP_W), F32)
    return _call(body, name=name, out_shape=[shp, shp, shp], grid=(s // tq,),
                 in_specs=[cur] * 6 + [prev, prev] + [nxt] * 4, out_specs=[cur, cur, cur])(
                     q, k, v, d_o, cterm, lse, k, v, q, d_o, cterm, lse)


HALO = 16
CONV_TQ = 512


def conv_fwd(p, w, b, name):
    s = p.shape[0]
    tq = CONV_TQ
    ncol = SSD_CONV_DIM // GROUP_W
    cb0 = COL_XBC // GROUP_W

    def body(u_ref, up_ref, w_ref, b_ref, c_ref, xc_ref):
        i = pl.program_id(0)
        prev = up_ref[...].astype(F32) * (i > 0).astype(F32)
        ext = jnp.concatenate([prev, u_ref[...].astype(F32)], axis=0)
        acc = b_ref[...] + w_ref[SSD_CONV - 1:SSD_CONV, :] * ext[HALO:HALO + tq]
        for kk in range(SSD_CONV - 1):
            off = HALO - (SSD_CONV - 1) + kk
            acc += w_ref[kk:kk + 1, :] * ext[off:off + tq]
        c_ref[...] = acc.astype(BF16)
        xc_ref[...] = (acc * _sigmoid(acc)).astype(BF16)

    cur_in = pl.BlockSpec((tq, GROUP_W), lambda i, j: (i, cb0 + j))
    prev_in = pl.BlockSpec((HALO, GROUP_W), lambda i, j: (jnp.maximum(i * (tq // HALO) - 1, 0), cb0 + j))
    cur_out = pl.BlockSpec((tq, GROUP_W), lambda i, j: (i, j))
    shp = _sds((s, SSD_CONV_DIM), BF16)
    return _call(body, name=name, out_shape=[shp, shp], grid=(s // tq, ncol),
                 in_specs=[cur_in, prev_in, pl.BlockSpec((SSD_CONV, GROUP_W), lambda i, j: (0, j)),
                           pl.BlockSpec((1, GROUP_W), lambda i, j: (0, j))],
                 out_specs=[cur_out, cur_out])(p, p, w, b)


def conv_bwd(p, cpre, dxc, w, name):
    s = p.shape[0]
    tq = CONV_TQ
    ncol = SSD_CONV_DIM // GROUP_W
    cb0 = COL_XBC // GROUP_W
    nt = s // tq

    def body(u_ref, up_ref, c_ref, cn_ref, d_ref, dn_ref, w_ref, du_ref, dw_ref, db_ref):
        i = pl.program_id(1)

        def dpre(c16, dx):
            c = c16.astype(F32)
            sg = _sigmoid(c)
            return dx * (sg * (1.0 + c * (1.0 - sg)))

        dc = dpre(c_ref[...], d_ref[...])
        dcn = dpre(cn_ref[...], dn_ref[...]) * (i < nt - 1).astype(F32)
        dext = jnp.concatenate([dc, dcn], axis=0)
        prev = up_ref[...].astype(F32) * (i > 0).astype(F32)
        uext = jnp.concatenate([prev, u_ref[...].astype(F32)], axis=0)

        @pl.when(i == 0)
        def _():
            dw_ref[...] = jnp.zeros_like(dw_ref)
            db_ref[...] = jnp.zeros_like(db_ref)

        du = w_ref[SSD_CONV - 1:SSD_CONV, :] * dc
        for kk in range(SSD_CONV - 1):
            sh = SSD_CONV - 1 - kk
            du += w_ref[kk:kk + 1, :] * dext[sh:sh + tq]
        du_ref[...] = du.astype(BF16)
        for kk in range(SSD_CONV):
            off = HALO - (SSD_CONV - 1) + kk
            dw_ref[kk:kk + 1, :] += jnp.sum(dc * uext[off:off + tq], axis=0, keepdims=True)
        db_ref[...] += jnp.sum(dc, axis=0, keepdims=True)

    hb = tq // HALO
    cur_p = pl.BlockSpec((tq, GROUP_W), lambda j, i: (i, cb0 + j))
    prev_p = pl.BlockSpec((HALO, GROUP_W), lambda j, i: (jnp.maximum(i * hb - 1, 0), cb0 + j))
    cur = pl.BlockSpec((tq, GROUP_W), lambda j, i: (i, j))
    nxt = pl.BlockSpec((HALO, GROUP_W), lambda j, i: (jnp.minimum((i + 1) * hb, s // HALO - 1), j))
    return _call(body, name=name,
                 out_shape=[_sds((s, SSD_CONV_DIM), BF16), _sds((8, SSD_CONV_DIM), F32), _sds((1, SSD_CONV_DIM), F32)],
                 grid=(ncol, nt),
                 in_specs=[cur_p, prev_p, cur, nxt, cur, nxt, pl.BlockSpec((SSD_CONV, GROUP_W), lambda j, i: (0, j))],
                 out_specs=[cur, pl.BlockSpec((8, GROUP_W), lambda j, i: (0, j)),
                            pl.BlockSpec((1, GROUP_W), lambda j, i: (0, j))])(p, p, cpre, cpre, dxc, dxc, w)


def _softplus(x):
    return jnp.maximum(x, 0.0) + jnp.log(1.0 + jnp.exp(-jnp.abs(x)))


def _ssd_decays(dtr_ref, dtrt_ref, bias_ref, biast_ref, alog_ref, alogt_ref):
    row = lax.broadcasted_iota(jnp.int32, (BLK, BLK), 0)
    col = lax.broadcasted_iota(jnp.int32, (BLK, BLK), 1)
    lower = (row >= col).astype(F32)
    upper = (row <= col).astype(F32)
    dtb = dtr_ref[...] + bias_ref[...]
    dt = _softplus(dtb)
    a = dt * (-jnp.exp(alog_ref[...]))
    cs = _dot_hi(lower, a)
    a_t = _softplus(dtrt_ref[...] + biast_ref[...]) * (-jnp.exp(alogt_ref[...]))
    cs_t = _dot_hi(a_t, upper)
    return dtb, dt, cs, cs_t, row, col, upper


def ssd_fwd(p, xc, dtg, dtg_t, params, gn, name):
    s = p.shape[0]
    nc = s // BLK
    bias, bias_t, alog, alog_t, dskip = params

    def body(xs_ref, b_ref, c_ref, z_ref, dtr_ref, dtrt_ref, bias_ref, biast_ref, alog_ref, alogt_ref, dsk_ref,
             gn_ref, y_ref, sin_ref, hp_ref, h_ref):
        c_idx = pl.program_id(1)

        @pl.when(c_idx == 0)
        def _():
            h_ref[...] = jnp.zeros_like(h_ref)

        _, dt, cs, cs_t, row, col, _ = _ssd_decays(dtr_ref, dtrt_ref, bias_ref, biast_ref, alog_ref, alogt_ref)
        first = _lane_first_half((BLK, 2 * HD))
        first_row = _lane_first_half((1, 2 * HD))
        tril = row >= col
        b16, c16 = b_ref[...], c_ref[...]
        cb = _dot_nt(c16, b16)
        ys = []
        for pr in range(GROUP_W // (2 * HD)):
            cols = slice(pr * 2 * HD, (pr + 1) * 2 * HD)
            ha, hb = 2 * pr, 2 * pr + 1
            xs = xs_ref[:, cols].astype(F32)
            dt_pair = jnp.where(first, dt[:, ha:ha + 1], dt[:, hb:hb + 1])
            xt = xs * dt_pair
            xt16 = xt.astype(BF16)
            y_heads = []
            for h in (ha, hb):
                decay = jnp.exp(jnp.where(tril, cs[:, h:h + 1] - cs_t[h:h + 1, :], NEG))
                y_heads.append(_dot((cb * decay).astype(BF16), xt16))
            y_diag = jnp.where(first, y_heads[0], y_heads[1])
            hstate = h_ref[pr]
            hp_ref[pr] = hstate
            e_pair = jnp.where(first, jnp.exp(cs[:, ha:ha + 1]), jnp.exp(cs[:, hb:hb + 1]))
            y_off = e_pair * _dot(c16, hstate.astype(BF16))
            tot_a, tot_b = cs[BLK - 1:BLK, ha:ha + 1], cs[BLK - 1:BLK, hb:hb + 1]
            f_pair = jnp.where(first, jnp.exp(tot_a - cs[:, ha:ha + 1]), jnp.exp(tot_b - cs[:, hb:hb + 1]))
            new = _dot_tn(b16, (f_pair * xt).astype(BF16))
            dec = jnp.where(first_row, jnp.exp(tot_a), jnp.exp(tot_b))
            h_ref[pr] = dec * hstate + new
            d_pair = jnp.where(first_row, dsk_ref[:, ha:ha + 1], dsk_ref[:, hb:hb + 1])
            ys.append(y_diag + y_off + xs * d_pair)
        y = jnp.concatenate(ys, axis=1)
        y_ref[...] = y
        zv = z_ref[...].astype(F32)
        yz = y * (zv * _sigmoid(zv))
        r = lax.rsqrt(jnp.mean(yz * yz, axis=-1, keepdims=True) + EPS)
        sin_ref[...] = (yz * r * gn_ref[...]).astype(BF16)

    nb0 = SSD_INNER // BLK
    gparam = pl.BlockSpec((None, 1, 8), lambda g, c: (g, 0, 0))
    gparam_t = pl.BlockSpec((None, 8, 1), lambda g, c: (g, 0, 0))
    return _call(
        body, name=name,
        out_shape=[_sds((s, SSD_INNER), F32), _sds((s, SSD_INNER), BF16),
                   _sds((SSD_GROUPS, nc, 4, BLK, 2 * HD), F32)],
        grid=(SSD_GROUPS, nc),
        in_specs=[pl.BlockSpec((BLK, GROUP_W), lambda g, c: (c, g)),
                  pl.BlockSpec((BLK, BLK), lambda g, c: (c, nb0 + g)),
                  pl.BlockSpec((BLK, BLK), lambda g, c: (c, nb0 + SSD_GROUPS + g)),
                  pl.BlockSpec((BLK, GROUP_W), lambda g, c: (c, COL_Z // GROUP_W + g)),
                  pl.BlockSpec((None, BLK, 8), lambda g, c: (g, c, 0)),
                  pl.BlockSpec((None, 8, BLK), lambda g, c: (g, 0, c)),
                  gparam, gparam_t, gparam, gparam_t, gparam,
                  pl.BlockSpec((1, GROUP_W), lambda g, c: (0, g))],
        out_specs=[pl.BlockSpec((BLK, GROUP_W), lambda g, c: (c, g)),
                   pl.BlockSpec((BLK, GROUP_W), lambda g, c: (c, g)),
                   pl.BlockSpec((None, None, 4, BLK, 2 * HD), lambda g, c: (g, c, 0, 0, 0))],
        scratch_shapes=[pltpu.VMEM((4, BLK, 2 * HD), F32)],
    )(xc, xc, xc, p, dtg, dtg_t, bias, bias_t, alog, alog_t, dskip, gn)


def ssd_bwd(p, xc, bc_t, y, d_sin, hprev, dtg, dtg_t, params, gn, name):
    s = p.shape[0]
    nc = s // BLK
    bias, bias_t, alog, alog_t, dskip = params

    def body(xs_ref, b_ref, c_ref, bt_ref, ct_ref, z_ref, y_ref, dsin_ref, hp_ref, dtr_ref, dtrt_ref, bias_ref,
             biast_ref, alog_ref, alogt_ref, dsk_ref, gn_ref,
             dxs_ref, db_ref, dc_ref, dz_ref, ddt_ref, da_ref, dbias_ref, ddsk_ref, dgn_ref, dh_ref):
        c_idx = pl.program_id(1)

        @pl.when(c_idx == 0)
        def _():
            dh_ref[...] = jnp.zeros_like(dh_ref)
            da_ref[...] = jnp.zeros_like(da_ref)
            dbias_ref[...] = jnp.zeros_like(dbias_ref)
            ddsk_ref[...] = jnp.zeros_like(ddsk_ref)
            dgn_ref[...] = jnp.zeros_like(dgn_ref)

        dtb, dt, cs, cs_t, row, col, upper = _ssd_decays(dtr_ref, dtrt_ref, bias_ref, biast_ref, alog_ref, alogt_ref)
        first = _lane_first_half((BLK, 2 * HD))
        second = jnp.logical_not(first)
        first_row = _lane_first_half((1, 2 * HD))
        tril = row >= col
        triu = row <= col
        last_row = lax.broadcasted_iota(jnp.int32, (BLK, 1), 0) == BLK - 1
        lane8 = lax.broadcasted_iota(jnp.int32, (BLK, 8), 1)

        yv = y_ref[...]
        zv = z_ref[...].astype(F32)
        sg = _sigmoid(zv)
        yz = yv * (zv * sg)
        r = lax.rsqrt(jnp.mean(yz * yz, axis=-1, keepdims=True) + EPS)
        yzn = yz * r
        dsn = dsin_ref[...]
        dgn_ref[...] += jnp.sum(dsn * yzn, axis=0, keepdims=True)
        dsn = dsn * gn_ref[...]
        dyz = r * (dsn - yzn * jnp.mean(dsn * yzn, axis=-1, keepdims=True))
        dy = dyz * (zv * sg)
        dz_ref[...] = (dyz * yv * (sg * (1.0 + zv * (1.0 - sg)))).astype(BF16)
        xs_all = xs_ref[...].astype(F32)
        ddsk_ref[...] += jnp.sum(dy * xs_all, axis=0, keepdims=True)

        b16, c16, bt16, ct16 = b_ref[...], c_ref[...], bt_ref[...], ct_ref[...]
        cb = _dot_nt(c16, b16)
        cb_t = _dot_nt(b16, c16)
        g_sum = jnp.zeros((BLK, BLK), F32)
        gt_sum = jnp.zeros((BLK, BLK), F32)
        dc_acc = jnp.zeros((BLK, BLK), F32)
        db_acc = jnp.zeros((BLK, BLK), F32)
        dcs_all = jnp.zeros((BLK, 8), F32)
        ddtx_all = jnp.zeros((BLK, 8), F32)
        for pr in range(GROUP_W // (2 * HD)):
            cols = slice(pr * 2 * HD, (pr + 1) * 2 * HD)
            heads = (2 * pr, 2 * pr + 1)
            xs = xs_all[:, cols]
            dy_p = dy[:, cols]
            dt_pair = jnp.where(first, dt[:, heads[0]:heads[0] + 1], dt[:, heads[1]:heads[1] + 1])
            xt = xs * dt_pair
            xt16 = xt.astype(BF16)
            hstate = hp_ref[pr]
            h16 = hstate.astype(BF16)
            dhn = dh_ref[pr]
            dhn16 = dhn.astype(BF16)
            d_xt = jnp.zeros((BLK, 2 * HD), F32)
            dcs = []
            for e, h in enumerate(heads):
                lanes = first if e == 0 else second
                cs_c, cs_r = cs[:, h:h + 1], cs_t[h:h + 1, :]
                decay = jnp.exp(jnp.where(tril, cs_c - cs_r, NEG))
                decay_t = jnp.exp(jnp.where(triu, cs_r - cs_c, NEG))
                dym16 = jnp.where(lanes, dy_p, 0.0).astype(BF16)
                d_m = _dot_nt(dym16, xt16)
                d_mt = _dot_nt(xt16, dym16)
                d_xt += _dot((cb_t * decay_t).astype(BF16), dym16)
                gm = d_m * decay
                gmt = d_mt * decay_t
                g_sum += gm
                gt_sum += gmt
                dcs.append(jnp.sum(gm * cb, axis=-1, keepdims=True) - jnp.sum(gmt * cb_t, axis=-1, keepdims=True))
            exp_cs = [jnp.exp(cs[:, h:h + 1]) for h in heads]
            tots = [cs[BLK - 1:BLK, h:h + 1] for h in heads]
            f_col = [jnp.exp(tots[e] - cs[:, h:h + 1]) for e, h in enumerate(heads)]
            e_pair = jnp.where(first, exp_cs[0], exp_cs[1])
            f_pair = jnp.where(first, f_col[0], f_col[1])
            dec = [jnp.exp(t) for t in tots]
            dec_pair = jnp.where(first_row, dec[0], dec[1])
            edy = e_pair * dy_p
            edy16 = edy.astype(BF16)
            y_off = e_pair * _dot(c16, h16)
            oa, ob = _pair_sum(dy_p * y_off, first)
            dc_acc += _dot_nt(edy16, h16)
            dh_prev = _dot(ct16, edy16)
            zmat = _dot(b16, dhn16)
            d_xt += f_pair * zmat
            fa, fb = _pair_sum(zmat * xt, first)
            ta, tb = fa * f_col[0], fb * f_col[1]
            hh = dhn * hstate
            ha_sum = jnp.sum(jnp.sum(jnp.where(first, hh, 0.0), axis=-1, keepdims=True), axis=0, keepdims=True)
            hb_sum = jnp.sum(jnp.sum(hh, axis=-1, keepdims=True), axis=0, keepdims=True) - ha_sum
            dtot_a = jnp.sum(ta, axis=0, keepdims=True) + ha_sum * dec[0]
            dtot_b = jnp.sum(tb, axis=0, keepdims=True) + hb_sum * dec[1]
            dcs[0] = dcs[0] + oa - ta + jnp.where(last_row, dtot_a, 0.0)
            dcs[1] = dcs[1] + ob - tb + jnp.where(last_row, dtot_b, 0.0)
            db_acc += _dot_nt((f_pair * xt).astype(BF16), dhn16)
            dh_ref[pr] = dh_prev + dec_pair * dhn
            d_pair = jnp.where(first_row, dsk_ref[:, heads[0]:heads[0] + 1], dsk_ref[:, heads[1]:heads[1] + 1])
            dxs_ref[:, cols] = dy_p * d_pair + d_xt * dt_pair
            xa, xb = _pair_sum(d_xt * xs, first)
            for e, h in enumerate(heads):
                dcs_all = jnp.where(lane8 == h, dcs[e], dcs_all)
                ddtx_all = jnp.where(lane8 == h, (xa, xb)[e], ddtx_all)

        dc_ref[...] = dc_acc + _dot(g_sum.astype(BF16), b16)
        db_ref[...] = db_acc + _dot(gt_sum.astype(BF16), c16)
        d_a = _dot_hi(upper, dcs_all)
        a_neg = -jnp.exp(alog_ref[...])
        ddt = ddtx_all + d_a * a_neg
        da_ref[...] += jnp.sum(d_a * dt, axis=0, keepdims=True)
        ddtr = ddt * _sigmoid(dtb)
        ddt_ref[...] = ddtr
        dbias_ref[...] += jnp.sum(ddtr, axis=0, keepdims=True)

    nb0 = SSD_INNER // BLK
    rc = lambda c: nc - 1 - c
    gparam = pl.BlockSpec((None, 1, 8), lambda g, c: (g, 0, 0))
    gparam_t = pl.BlockSpec((None, 8, 1), lambda g, c: (g, 0, 0))
    wide = pl.BlockSpec((BLK, GROUP_W), lambda g, c: (rc(c), g))
    narrow = pl.BlockSpec((BLK, BLK), lambda g, c: (rc(c), g))
    return _call(
        body, name=name,
        out_shape=[_sds((s, SSD_INNER), F32), _sds((s, GROUP_W), F32), _sds((s, GROUP_W), F32),
                   _sds((s, SSD_INNER), BF16), _sds((SSD_GROUPS, s, 8), F32),
                   _sds((SSD_GROUPS, 1, 8), F32), _sds((SSD_GROUPS, 1, 8), F32),
                   _sds((SSD_GROUPS, 1, GROUP_W), F32), _sds((1, SSD_INNER), F32)],
        grid=(SSD_GROUPS, nc),
        in_specs=[wide,
                  pl.BlockSpec((BLK, BLK), lambda g, c: (rc(c), nb0 + g)),
                  pl.BlockSpec((BLK, BLK), lambda g, c: (rc(c), nb0 + SSD_GROUPS + g)),
                  pl.BlockSpec((BLK, BLK), lambda g, c: (g, rc(c))),
                  pl.BlockSpec((BLK, BLK), lambda g, c: (SSD_GROUPS + g, rc(c))),
                  pl.BlockSpec((BLK, GROUP_W), lambda g, c: (rc(c), COL_Z // GROUP_W + g)),
                  wide, wide,
                  pl.BlockSpec((None, None, 4, BLK, 2 * HD), lambda g, c: (g, rc(c), 0, 0, 0)),
                  pl.BlockSpec((None, BLK, 8), lambda g, c: (g, rc(c), 0)),
                  pl.BlockSpec((None, 8, BLK), lambda g, c: (g, 0, rc(c))),
                  gparam, gparam_t, gparam, gparam_t, gparam,
                  pl.BlockSpec((1, GROUP_W), lambda g, c: (0, g))],
        out_specs=[wide, narrow, narrow, wide,
                   pl.BlockSpec((None, BLK, 8), lambda g, c: (g, rc(c), 0)),
                   gparam, gparam,
                   pl.BlockSpec((None, 1, GROUP_W), lambda g, c: (g, 0, 0)),
                   pl.BlockSpec((1, GROUP_W), lambda g, c: (0, g))],
        scratch_shapes=[pltpu.VMEM((4, BLK, 2 * HD), F32)],
    )(xc, xc, xc, bc_t, bc_t, p, y, d_sin, hprev, dtg, dtg_t, bias, bias_t, alog, alog_t, dskip, gn)


def merge_fwd(p, a, sbr, name, tm=512):
    s = p.shape[0]
    nj = D_MODEL // GROUP_W

    def body(ga_ref, gs_ref, a_ref, s_ref, o_ref):
        o_ref[...] = (_sigmoid(ga_ref[...].astype(F32)) * a_ref[...]
                      + _sigmoid(gs_ref[...].astype(F32)) * s_ref[...]).astype(BF16)

    blk = pl.BlockSpec((tm, GROUP_W), lambda i, j: (i, j))
    return _call(body, name=name, out_shape=_sds((s, D_MODEL), BF16), grid=(s // tm, nj),
                 in_specs=[pl.BlockSpec((tm, GROUP_W), lambda i, j: (i, COL_GA // GROUP_W + j)),
                           pl.BlockSpec((tm, GROUP_W), lambda i, j: (i, COL_GS // GROUP_W + j)), blk, blk],
                 out_specs=blk)(p, p, a, sbr)


def merge_bwd(p, a, sbr, dmerged, name, tm=512):
    s = p.shape[0]
    nj = D_MODEL // GROUP_W

    def body(ga_ref, gs_ref, a_ref, s_ref, dm_ref, da_ref, ds_ref, dga_ref, dgs_ref):
        dm = dm_ref[...]
        sa = _sigmoid(ga_ref[...].astype(F32))
        ss = _sigmoid(gs_ref[...].astype(F32))
        da_ref[...] = (dm * sa).astype(BF16)
        ds_ref[...] = (dm * ss).astype(BF16)
        dga_ref[...] = (dm * a_ref[...] * sa * (1.0 - sa)).astype(BF16)
        dgs_ref[...] = (dm * s_ref[...] * ss * (1.0 - ss)).astype(BF16)

    blk = pl.BlockSpec((tm, GROUP_W), lambda i, j: (i, j))
    shp = _sds((s, D_MODEL), BF16)
    return _call(body, name=name, out_shape=[shp] * 4, grid=(s // tm, nj),
                 in_specs=[pl.BlockSpec((tm, GROUP_W), lambda i, j: (i, COL_GA // GROUP_W + j)),
                           pl.BlockSpec((tm, GROUP_W), lambda i, j: (i, COL_GS // GROUP_W + j)), blk, blk, blk],
                 out_specs=[blk] * 4)(p, p, a, sbr, dmerged)


def _group_major(v):
    return v.reshape(SSD_GROUPS, 1, 8), v.reshape(SSD_GROUPS, 8, 1)


def mixer_forward(x, w, rider=None, later_weights=None):
    s = x.shape[0]
    h = rms_fwd(x, w["mix_norm"], "mix_rms")
    p = matmul_nn(h, w["w_in_main"], "mix_proj", BF16, tm=1024, tn=512, rider=rider)
    rode = None
    if rider is not None:
        p, rode = p
        w = dict(w, **later_weights(rode))
    dt_raw = matmul_nn(h, w["w_in_dt"], "mix_proj_dt", F32, tm=1024, tn=DT_PAD)
    qn, kn = qk_norm_fwd(p, w["q_gain"], w["k_gain"], "qk_norm")
    streams, os_, lses = [], [], []
    for g, d in enumerate(ATTN_DILATIONS):
        cols = slice(g * GROUP_W, (g + 1) * GROUP_W)
        qs, ks = _to_streams(qn[:, cols], d), _to_streams(kn[:, cols], d)
        vs = _to_streams(p[:, COL_V + g * GROUP_W:COL_V + (g + 1) * GROUP_W], d)
        o, lse = attn_fwd(qs, ks, vs, g, f"attn_fwd{g}")
        streams.append((qs, ks, vs, lse))
        os_.append(_from_streams(o, d))
        lses.append(_from_streams(lse, d))
    attn_o = attn_merge_fwd(os_, lses, "attn_merge")
    cpre, xc = conv_fwd(p, w["conv_w"], w["conv_b"], "conv_fwd")
    dtg = dt_raw[:, :SSD_HEADS].reshape(s, SSD_GROUPS, 8).transpose(1, 0, 2)
    dtg_t = dtg.transpose(0, 2, 1)
    params = (*_group_major(w["dt_bias"]), *_group_major(w["a_log"]), _group_major(w["d_skip"])[0])
    y, s_in, hprev = ssd_fwd(p, xc, dtg, dtg_t, params, w["ssd_norm"], "ssd_fwd")
    a = matmul_nn(attn_o, w["w_attn_branch"], "attn_branch", F32, tm=1024, tn=512)
    sbr = matmul_nn(s_in, w["w_ssd_branch"], "ssd_branch", F32, tm=1024, tn=512)
    merged = merge_fwd(p, a, sbr, "merge")
    x_out = matmul_nn(merged, w["w_out"], "mix_out", F32, tm=1024, tn=512, res=x)
    saved = dict(h=h, p=p, streams=streams, os=os_, lses=lses, attn_o=attn_o, cpre=cpre, xc=xc, dtg=dtg,
                 dtg_t=dtg_t, params=params, y=y, s_in=s_in, hprev=hprev, a=a, sbr=sbr, merged=merged, w=w)
    return x_out, saved, rode


def mixer_backward(dx_out, x, sv, ride_early=None, ride_late=None):
    s = x.shape[0]
    p = sv["p"]
    w = sv["w"]
    g = {}
    dmerged = matmul_nt(dx_out, w["w_out"], "d_merged", F32, tm=1024, tn=512, tk=1024)
    g["w_out"] = matmul_tn(sv["merged"], dx_out, "dw_out", tn=512, ts=1024)
    da, ds, dga, dgs = merge_bwd(p, sv["a"], sv["sbr"], dmerged, "merge_bwd")
    g["w_attn_branch"] = matmul_tn(sv["attn_o"], da, "dw_attn_branch", tn=512, ts=1024)
    g["w_ssd_branch"] = matmul_tn(sv["s_in"], ds, "dw_ssd_branch", tn=512, ts=1024)
    d_attn_o = matmul_nt(da, w["w_attn_branch"], "d_attn_o", F32, tm=1024, tn=512, tk=1024)
    d_sin = matmul_nt(ds, w["w_ssd_branch"], "d_ssd_in", F32, tm=1024, tn=512, tk=1024)
    bc_t = sv["xc"][:, SSD_INNER:].T
    dxs, d_b, d_c, dz, ddt, d_asum, d_bias, d_dsk, d_gn = ssd_bwd(
        p, sv["xc"], bc_t, sv["y"], d_sin, sv["hprev"], sv["dtg"], sv["dtg_t"], sv["params"], w["ssd_norm"], "ssd_bwd")
    dxc = jnp.concatenate([dxs, d_b, d_c], axis=1)
    dxbc, d_convw, d_convb = conv_bwd(p, sv["cpre"], dxc, w["conv_w"], "conv_bwd")
    g["conv_w"] = d_convw[:SSD_CONV]
    g["conv_b"] = d_convb
    g["dt_bias"] = d_bias.reshape(1, SSD_HEADS)
    g["a_log"] = (d_asum * (-jnp.exp(sv["params"][2]))).reshape(1, SSD_HEADS)
    g["d_skip"] = jnp.sum(d_dsk.reshape(SSD_HEADS, HD), axis=1).reshape(1, SSD_HEADS)
    g["ssd_norm"] = d_gn
    merged_bwd = attn_merge_bwd(d_attn_o, sv["os"], sv["lses"], "attn_merge_bwd")
    dqs, dks, dvs = [], [], []
    for gi, d in enumerate(ATTN_DILATIONS):
        qs, ks, vs, lse = sv["streams"][gi]
        d_o = _to_streams(merged_bwd[gi], d)
        cterm = _to_streams(merged_bwd[3 + gi], d)
        dq, dk, dv = attn_bwd(qs, ks, vs, d_o, cterm, lse, gi, f"attn_bwd{gi}")
        dqs.append(_from_streams(dq, d))
        dks.append(_from_streams(dk, d))
        dvs.append(_from_streams(dv, d).astype(BF16))
    dqn = jnp.concatenate(dqs, axis=1)
    dkn = jnp.concatenate(dks, axis=1)
    dq, dk, d_qg, d_kg = qk_norm_bwd(p, dqn, dkn, w["q_gain"], w["k_gain"], "qk_norm_bwd")
    g["q_norm"] = jnp.sum(d_qg.reshape(N_ATTN_HEADS, HD), axis=0).reshape(1, HD)
    g["k_norm"] = jnp.sum(d_kg.reshape(N_ATTN_HEADS, HD), axis=0).reshape(1, HD)
    dp = jnp.concatenate([dq, dk] + dvs + [dz, dxbc, dga, dgs], axis=1)
    ddt_pad = jnp.pad(ddt.transpose(1, 0, 2).reshape(s, SSD_HEADS), ((0, 0), (0, DT_PAD - SSD_HEADS)))
    if ride_early is not None:
        g["w_in_main"], g["rode_early"] = matmul_tn(sv["h"], dp, "dw_in", tn=512, ts=1024, rider=ride_early(g))
    else:
        g["w_in_main"] = matmul_tn(sv["h"], dp, "dw_in", tn=512, ts=1024)
    g["w_in_dt"] = matmul_tn(sv["h"], ddt_pad, "dw_in_dt", tn=DT_PAD, ts=1024)
    if ride_late is not None:
        dh_main, g["rode_late"] = matmul_nt(dp, w["w_in_main"], "d_mix_h", F32, tm=1024, tn=512, tk=512,
                                            rider=ride_late(g))
    else:
        dh_main = matmul_nt(dp, w["w_in_main"], "d_mix_h", F32, tm=1024, tn=512, tk=512)
    dh_dt = matmul_nt(ddt_pad, w["w_in_dt"], "d_mix_h_dt", F32, tm=1024, tn=1024, tk=DT_PAD)
    dx, g["mix_norm"] = rms_bwd([dh_main, dh_dt], x, w["mix_norm"], dx_out, "mix_drms")
    return dx, g


ANY = pl.BlockSpec(memory_space=pl.ANY)


def _place():
    x, y, c = lax.axis_index("x"), lax.axis_index("y"), lax.axis_index("c")
    chips = [(1 - x, y), (x, 1 - y), (1 - x, 1 - y)]
    return x, y, c, 2 * x + y, chips


def _comm_call(body, *, name, out_shape, n_in, scratch_shapes, aliases=None):
    return pl.pallas_call(
        body, out_shape=out_shape, in_specs=[ANY] * n_in, out_specs=[ANY] * len(out_shape),
        scratch_shapes=scratch_shapes, input_output_aliases=aliases or {}, name=name,
        compiler_params=pltpu.CompilerParams(has_side_effects=True))


def gather_weights(shards, small):
    n = len(shards)
    halves = [a.shape[0] // 2 for a in shards]
    out_shape = [_sds((N_CHIP,) + a.shape, a.dtype) for a in shards] + [_sds((N_CHIP,) + small.shape, small.dtype)]

    def body(*refs):
        ins, outs = refs[:n + 1], refs[n + 1:2 * n + 2]
        send1, recv1, send2, recv2, local = refs[2 * n + 2:]
        x, y, c, me, chips = _place()
        sibling = (x, y, 1 - c)

        def rows(k, chip, core):
            if k == n:
                return outs[k].at[chip]
            return outs[k].at[chip, pl.ds(core * halves[k], halves[k])]

        def level1(k, t, incoming):
            chip = 2 * chips[t][0] + chips[t][1]
            src = ins[k] if k == n else ins[k].at[pl.ds(c * halves[k], halves[k])]
            return pltpu.make_async_remote_copy(
                src_ref=src, dst_ref=rows(k, chip if incoming else me, c), send_sem=send1.at[3 * k + t],
                recv_sem=recv1.at[3 * k + t], device_id=(*chips[t], c), device_id_type=MESH)

        def level2(k, t, incoming):
            chip = 2 * chips[t][0] + chips[t][1]
            core = (1 - c) if incoming else c
            return pltpu.make_async_remote_copy(
                src_ref=rows(k, chip, core), dst_ref=rows(k, chip, core), send_sem=send2.at[3 * k + t],
                recv_sem=recv2.at[3 * k + t], device_id=sibling, device_id_type=MESH)

        own = [pltpu.make_async_copy(ins[k], outs[k].at[me], local.at[k]) for k in range(n + 1)]
        for cp in own:
            cp.start()
        first = [level1(k, t, False) for k in range(n + 1) for t in range(3)]
        for cp in first:
            cp.start()
        passed = []
        for k in range(n + 1):
            for t in range(3):
                level1(k, t, True).wait_recv()
                if k < n:
                    cp = level2(k, t, False)
                    cp.start()
                    passed.append(cp)
        for k in range(n):
            for t in range(3):
                level2(k, t, True).wait_recv()
        for cp in first + passed:
            cp.wait_send()
        for cp in own:
            cp.wait()

    dma = pltpu.SemaphoreType.DMA
    return _comm_call(body, name="gather_weights", out_shape=out_shape, n_in=n + 1,
                      scratch_shapes=[dma((3 * n + 3,)), dma((3 * n + 3,)), dma((3 * n,)), dma((3 * n,)),
                                      dma((n + 1,))])(*shards, small)


def reduce_to_sibling(grads):
    n = len(grads)
    halves = [a.shape[1] // 2 for a in grads]
    shapes = [_sds((N_CHIP, h, a.shape[2]), a.dtype) for a, h in zip(grads, halves)]

    def body(*refs):
        ins, got, kept = refs[:n], refs[n:2 * n], refs[2 * n:3 * n]
        send, recv, local = refs[3 * n:]
        x, y, c, _, _ = _place()
        copies, locals_ = [], []
        for k in range(n):
            h = halves[k]
            locals_.append(pltpu.make_async_copy(ins[k].at[:, pl.ds(c * h, h)], kept[k], local.at[k]))
            copies.append(pltpu.make_async_remote_copy(
                src_ref=ins[k].at[:, pl.ds((1 - c) * h, h)], dst_ref=got[k], send_sem=send.at[k], recv_sem=recv.at[k],
                device_id=(x, y, 1 - c), device_id_type=MESH))
        for cp in locals_ + copies:
            cp.start()
        for cp in copies:
            cp.wait_recv()
        for cp in copies:
            cp.wait_send()
        for cp in locals_:
            cp.wait()

    dma = pltpu.SemaphoreType.DMA
    res = _comm_call(body, name="reduce_to_sibling", out_shape=shapes + shapes, n_in=n,
                     scratch_shapes=[dma((n,)), dma((n,)), dma((n,))])(*grads)
    return res[:n], res[n:]


def reduce_to_owner(sums):
    n = len(sums)
    shapes = [_sds(a.shape, a.dtype) for a in sums]

    def body(*refs):
        ins, outs = refs[:n], refs[n:2 * n]
        send, recv, local = refs[2 * n:]
        x, y, c, me, chips = _place()
        copies, locals_ = [], []
        for k in range(n):
            locals_.append(pltpu.make_async_copy(ins[k].at[me], outs[k].at[3], local.at[k]))
            for t in range(3):
                chip = 2 * chips[t][0] + chips[t][1]
                copies.append(pltpu.make_async_remote_copy(
                    src_ref=ins[k].at[chip], dst_ref=outs[k].at[t], send_sem=send.at[3 * k + t],
                    recv_sem=recv.at[3 * k + t], device_id=(*chips[t], c), device_id_type=MESH))
        for cp in locals_ + copies:
            cp.start()
        for cp in copies:
            cp.wait_recv()
        for cp in copies:
            cp.wait_send()
        for cp in locals_:
            cp.wait()

    dma = pltpu.SemaphoreType.DMA
    return _comm_call(body, name="reduce_to_owner", out_shape=shapes, n_in=n,
                      scratch_shapes=[dma((3 * n,)), dma((3 * n,)), dma((n,))])(*sums)


def share_with_sibling(halves_):
    n = len(halves_)
    shapes = [_sds((2 * a.shape[0], a.shape[1]), a.dtype) for a in halves_]

    def body(*refs):
        ins, outs = refs[:n], refs[n:2 * n]
        send, recv, local = refs[2 * n:]
        x, y, c, _, _ = _place()
        copies, locals_ = [], []
        for k in range(n):
            h = ins[k].shape[0]
            mine = outs[k].at[pl.ds(c * h, h)]
            locals_.append(pltpu.make_async_copy(ins[k], mine, local.at[k]))
            copies.append(pltpu.make_async_remote_copy(
                src_ref=ins[k], dst_ref=mine, send_sem=send.at[k], recv_sem=recv.at[k],
                device_id=(x, y, 1 - c), device_id_type=MESH))
        for cp in locals_ + copies:
            cp.start()
        for cp in copies:
            cp.wait_recv()
        for cp in copies:
            cp.wait_send()
        for cp in locals_:
            cp.wait()

    dma = pltpu.SemaphoreType.DMA
    return _comm_call(body, name="share_with_sibling", out_shape=shapes, n_in=n,
                      scratch_shapes=[dma((n,)), dma((n,)), dma((n,))])(*halves_)


def _cores():
    c = lax.axis_index("c")
    return jnp.stack([c, 1 - c]).astype(jnp.int32)


def _staged_call(body, *, name, grid, in_specs, out_specs, out_shape, scratch_shapes):
    return pl.pallas_call(
        body, out_shape=out_shape, name=name,
        grid_spec=pltpu.PrefetchScalarGridSpec(num_scalar_prefetch=1, grid=grid, in_specs=in_specs,
                                               out_specs=out_specs, scratch_shapes=scratch_shapes),
        compiler_params=pltpu.CompilerParams(dimension_semantics=("arbitrary",) * len(grid),
                                             vmem_limit_bytes=V7X_VMEM_LIMIT, has_side_effects=True))


def gather_rider(shards, tiles):
    dma = pltpu.SemaphoreType.DMA
    n = len(shards)
    geo = [(a.shape[0] // 2, tm, (a.shape[0] // 2) // tm) for a, tm in zip(shards, tiles)]
    scratch = []
    for a, (h, tm, nk) in zip(shards, geo):
        scratch += [pltpu.VMEM((N_CHIP,) + a.shape, a.dtype), dma((3, nk)), dma((3, nk)), dma((3, nk)), dma((3, nk)),
                    dma((nk + 2,))]

    def copies(j, in_ref, scr):
        buf, send1, recv1, send2, recv2, local = scr[6 * j:6 * j + 6]
        h, tm, nk = geo[j]
        x, y, c, me, chips = _place()
        chip_of = [2 * chips[t][0] + chips[t][1] for t in range(3)]

        def rows(chip, core, k):
            return buf.at[chip, pl.ds(core * h + k * tm, tm)]

        def mine(k):
            if k == nk:
                return pltpu.make_async_copy(in_ref.at[pl.ds((1 - c) * h, h)], buf.at[me, pl.ds((1 - c) * h, h)],
                                             local.at[nk])
            return pltpu.make_async_copy(in_ref.at[pl.ds(c * h + k * tm, tm)], rows(me, c, k), local.at[k])

        def level1(t, k, incoming):
            place = rows(chip_of[t] if incoming else me, c, k)
            return pltpu.make_async_remote_copy(src_ref=place, dst_ref=place, send_sem=send1.at[t, k],
                                                recv_sem=recv1.at[t, k], device_id=(*chips[t], c), device_id_type=MESH)

        def level2(t, k, incoming):
            place = rows(chip_of[t], (1 - c) if incoming else c, k)
            return pltpu.make_async_remote_copy(src_ref=place, dst_ref=place, send_sem=send2.at[t, k],
                                                recv_sem=recv2.at[t, k], device_id=(x, y, 1 - c),
                                                device_id_type=MESH)

        return buf, local, nk, mine, level1, level2

    def start(ins, outs, scr):
        for j in range(n):
            _, _, nk, mine, _, _ = copies(j, ins[j], scr)
            for k in range(nk + 1):
                mine(k).start()
        for j in range(n):
            _, _, nk, mine, level1, _ = copies(j, ins[j], scr)
            for k in range(nk):
                mine(k).wait()
                for t in range(3):
                    level1(t, k, False).start()

    def finish(ins, outs, scr):
        for j in range(n):
            _, _, nk, _, level1, level2 = copies(j, ins[j], scr)
            for k in range(nk):
                for t in range(3):
                    level1(t, k, True).wait_recv()
                    level2(t, k, False).start()
        for j in range(n):
            buf, local, nk, mine, level1, level2 = copies(j, ins[j], scr)
            for k in range(nk):
                for t in range(3):
                    level2(t, k, True).wait_recv()
            for k in range(nk):
                for t in range(3):
                    level1(t, k, False).wait_send()
                    level2(t, k, False).wait_send()
            mine(nk).wait()
            pltpu.make_async_copy(buf, outs[j], local.at[nk + 1]).start()
        for j in range(n):
            buf, local, nk, _, _, _ = copies(j, ins[j], scr)
            pltpu.make_async_copy(buf, outs[j], local.at[nk + 1]).wait()

    return Rider(list(shards), [_sds((N_CHIP,) + a.shape, a.dtype) for a in shards], scratch, start, finish)


def run_alone(rider, name):
    return _call(lambda: None, name=name, out_shape=[], in_specs=[], out_specs=[], grid=(1,), rider=rider)()[1]


def sibling_sum(g, tm, name):
    _, r, cdim = g.shape
    h = r // 2
    ni = h // tm
    dma = pltpu.SemaphoreType.DMA

    def body(cores_ref, keep_ref, give_ref, out_ref, slot, send, recv):
        par = (pl.program_id(0) * ni + pl.program_id(1)) % 2
        x, y, c, _, _ = _place()
        cp = pltpu.make_async_remote_copy(src_ref=give_ref, dst_ref=slot.at[par], send_sem=send.at[par],
                                          recv_sem=recv.at[par], device_id=(x, y, 1 - c), device_id_type=MESH)
        cp.start()
        cp.wait_recv()
        out_ref[...] = (keep_ref[...].astype(F32) + slot[par].astype(F32)).astype(out_ref.dtype)
        cp.wait_send()

    flat = g.reshape(N_CHIP * r, cdim)
    return _staged_call(
        body, name=name, grid=(N_CHIP, ni),
        in_specs=[pl.BlockSpec((tm, cdim), lambda j, i, cores: ((2 * j + cores[0]) * ni + i, 0)),
                  pl.BlockSpec((tm, cdim), lambda j, i, cores: ((2 * j + cores[1]) * ni + i, 0))],
        out_specs=pl.BlockSpec((None, tm, cdim), lambda j, i, cores: (j, i, 0)),
        out_shape=_sds((N_CHIP, h, cdim), g.dtype),
        scratch_shapes=[pltpu.VMEM((2, tm, cdim), g.dtype), dma((2,)), dma((2,))],
    )(_cores(), flat, flat)


def owner_sum_rider(sums, tiles):
    dma = pltpu.SemaphoreType.DMA
    n = len(sums)
    geo = [(a.shape[1], tm, a.shape[1] // tm) for a, tm in zip(sums, tiles)]
    scratch = []
    for a, (h, tm, nk) in zip(sums, geo):
        cdim = a.shape[2]
        scratch += [pltpu.VMEM(a.shape, a.dtype), pltpu.VMEM((3, h, cdim), a.dtype), pltpu.VMEM((2, h, cdim), F32),
                    dma((3, nk)), dma((3, nk)), dma((nk,)), dma((nk,)), dma((2,))]

    def copies(j, scr):
        part, got, res, send, recv, send2, recv2, local = scr[8 * j:8 * j + 8]
        h, tm, nk = geo[j]
        x, y, c, me, chips = _place()

        def to_owner(t, k):
            chip = 2 * chips[t][0] + chips[t][1]
            return pltpu.make_async_remote_copy(
                src_ref=part.at[chip, pl.ds(k * tm, tm)], dst_ref=got.at[t, pl.ds(k * tm, tm)],
                send_sem=send.at[t, k], recv_sem=recv.at[t, k], device_id=(*chips[t], c), device_id_type=MESH)

        def to_sibling(k):
            place = res.at[c, pl.ds(k * tm, tm)]
            return pltpu.make_async_remote_copy(src_ref=place, dst_ref=place, send_sem=send2.at[k],
                                                recv_sem=recv2.at[k], device_id=(x, y, 1 - c), device_id_type=MESH)

        return part, got, res, local, to_owner, to_sibling, (tm, nk, c, me)

    def start(ins, outs, scr):
        for j in range(n):
            part, _, _, local, _, _, _ = copies(j, scr)
            pltpu.make_async_copy(ins[j], part, local.at[0]).start()
        for j in range(n):
            part, _, _, local, to_owner, _, (tm, nk, c, me) = copies(j, scr)
            pltpu.make_async_copy(ins[j], part, local.at[0]).wait()
            for k in range(nk):
                for t in range(3):
                    to_owner(t, k).start()

    def finish(ins, outs, scr):
        for j in range(n):
            part, got, res, _, to_owner, to_sibling, (tm, nk, c, me) = copies(j, scr)
            for k in range(nk):
                rows = pl.ds(k * tm, tm)
                for t in range(3):
                    to_owner(t, k).wait_recv()
                acc = part[me, rows, :].astype(F32)
                for t in range(3):
                    acc = acc + got[t, rows, :].astype(F32)
                res[c, rows, :] = acc
                to_sibling(k).start()
        for j in range(n):
            _, _, res, local, to_owner, to_sibling, (tm, nk, c, me) = copies(j, scr)
            for k in range(nk):
                to_sibling(k).wait_recv()
            for k in range(nk):
                to_sibling(k).wait_send()
                for t in range(3):
                    to_owner(t, k).wait_send()
            pltpu.make_async_copy(res, outs[j], local.at[1]).start()
        for j in range(n):
            _, _, res, local, _, _, _ = copies(j, scr)
            pltpu.make_async_copy(res, outs[j], local.at[1]).wait()

    return Rider(list(sums), [_sds((2, a.shape[1], a.shape[2]), F32) for a in sums], scratch, start, finish)


def gather_conv_w(w):
    def body(in_ref, out_ref, send, recv):
        x, y, c, me, chips = _place()
        out_ref[me] = in_ref[...]
        copies = []
        for t in range(3):
            copies.append(pltpu.make_async_remote_copy(
                src_ref=out_ref.at[me], dst_ref=out_ref.at[me], send_sem=send.at[t], recv_sem=recv.at[t],
                device_id=(*chips[t], c), device_id_type=MESH))
        for cp in copies:
            cp.start()
        for cp in copies:
            cp.wait_recv()
        for cp in copies:
            cp.wait_send()

    dma = pltpu.SemaphoreType.DMA
    vmem = pl.BlockSpec(memory_space=pltpu.VMEM)
    return pl.pallas_call(
        body, out_shape=_sds((N_CHIP,) + w.shape, w.dtype), in_specs=[vmem], out_specs=vmem, name="gather_conv_w",
        scratch_shapes=[dma((3,)), dma((3,))],
        compiler_params=pltpu.CompilerParams(has_side_effects=True))(w)


N_DEV = 8
SMALL_ROWS = 24


def all_reduce_small(pack):
    def body(in_ref, out_ref, buf, send, recv):
        x, y, c, _, _ = _place()
        me = 4 * x + 2 * y + c
        buf[me] = in_ref[...]
        copies = []
        for r in range(1, N_DEV):
            px = (1 - x) if r & 4 else x
            py = (1 - y) if r & 2 else y
            pc = (1 - c) if r & 1 else c
            copies.append(pltpu.make_async_remote_copy(
                src_ref=buf.at[me], dst_ref=buf.at[me], send_sem=send.at[r - 1], recv_sem=recv.at[r - 1],
                device_id=(px, py, pc), device_id_type=MESH))
        for cp in copies:
            cp.start()
        for cp in copies:
            cp.wait_recv()
        for cp in copies:
            cp.wait_send()
        acc = buf[0]
        for j in range(1, N_DEV):
            acc = acc + buf[j]
        out_ref[...] = acc

    dma = pltpu.SemaphoreType.DMA
    vmem = pl.BlockSpec(memory_space=pltpu.VMEM)
    return pl.pallas_call(
        body, out_shape=_sds(pack.shape, F32), in_specs=[vmem], out_specs=vmem, name="all_reduce_small",
        scratch_shapes=[pltpu.VMEM((N_DEV,) + pack.shape, F32), dma((N_DEV - 1,)), dma((N_DEV - 1,))],
        compiler_params=pltpu.CompilerParams(has_side_effects=True))(pack)


def _row_tile(rows, limit, multiple):
    return max(t for t in range(multiple, min(rows, limit) + 1, multiple) if rows % t == 0)


def add_pair(a, b, name):
    _, h, c = a.shape

    def body(a_ref, b_ref, o_ref):
        o_ref[...] = (a_ref[...].astype(F32) + b_ref[...].astype(F32)).astype(o_ref.dtype)

    blk = pl.BlockSpec((None, h, c), lambda j: (j, 0, 0))
    return _call(body, name=name, out_shape=_sds(a.shape, a.dtype), grid=(N_CHIP,), in_specs=[blk, blk],
                 out_specs=blk)(a, b)


def sum_slots(buf, name):
    _, h, c = buf.shape
    tm = _row_tile(h, 256, 16)

    def body(b_ref, o_ref):
        acc = b_ref[3].astype(F32)
        for t in range(3):
            acc = acc + b_ref[t].astype(F32)
        o_ref[...] = acc

    return _call(body, name=name, out_shape=_sds((h, c), F32), grid=(h // tm,),
                 in_specs=[pl.BlockSpec((N_CHIP, tm, c), lambda i: (0, i, 0))],
                 out_specs=pl.BlockSpec((tm, c), lambda i: (i, 0)))(buf)


def adamw(w, g, row_off, m, v, name):
    r, c = w.shape
    tm = r if r < 8 else _row_tile(math.gcd(r, row_off) if row_off else r, 128, 8)
    c1 = 1.0 - ADAM_B1 ** ADAM_STEP
    c2 = 1.0 - ADAM_B2 ** ADAM_STEP

    def body(w_ref, g_ref, m_ref, v_ref, go_ref, d_ref, m2_ref, v2_ref):
        gv = g_ref[...]
        go_ref[...] = gv
        m2 = ADAM_B1 * m_ref[...] + (1.0 - ADAM_B1) * gv
        v2 = ADAM_B2 * v_ref[...] + (1.0 - ADAM_B2) * (gv * gv)
        m2_ref[...] = m2
        v2_ref[...] = v2
        d_ref[...] = -ADAM_LR * ((m2 / c1) / (jnp.sqrt(v2 / c2) + ADAM_EPS) + ADAM_WD * w_ref[...])

    blk = pl.BlockSpec((tm, c), lambda i: (i, 0))
    shp = _sds((r, c), F32)
    return _call(body, name=name, out_shape=[shp] * 4, grid=(r // tm,),
                 in_specs=[blk, pl.BlockSpec((tm, c), lambda i: (row_off // tm + i, 0)), blk, blk],
                 out_specs=[blk] * 4)(w, g, m, v)


BIG = ("ffn1_w_gate", "ffn1_w_up", "ffn1_w_down", "w_in", "w_attn_branch", "w_ssd_branch", "w_out",
       "ffn2_w_gate", "ffn2_w_up", "ffn2_w_down")
SMALL = ("ffn1_norm", "mix_norm", "q_norm", "k_norm", "conv_b", "dt_bias", "a_log", "d_skip", "ssd_norm", "ffn2_norm")
WEIGHTS = ("ffn1_norm", "ffn1_w_gate", "ffn1_w_up", "ffn1_w_down", "mix_norm", "w_in", "q_norm", "k_norm", "conv_w",
           "conv_b", "dt_bias", "a_log", "d_skip", "ssd_norm", "w_attn_branch", "w_ssd_branch", "w_out", "ffn2_norm",
           "ffn2_w_gate", "ffn2_w_up", "ffn2_w_down")
CONV_SHARD = SSD_CONV_DIM // N_CHIP
CLASSES = {
    "ffn1_in": (("ffn1_w_gate", 1024), ("ffn1_w_up", 1024)),
    "ffn1_out": (("ffn1_w_down", 704),),
    "mix_in": (("w_in", 1024),),
    "mix_attn": (("w_attn_branch", 512),),
    "late_out": (("ffn2_w_down", 704), ("w_ssd_branch", 512), ("w_out", 256)),
    "ffn2_in": (("ffn2_w_gate", 1024), ("ffn2_w_up", 1024)),
}
CLASS_TILE = {"ffn1_in": 256, "ffn1_out": 176, "mix_in": 128, "mix_attn": 256, "late_out": 368, "ffn2_in": 256}


def _pack_small(vals, conv_part, loss_part=None):
    flat = [vals[k].reshape(-1) for k in SMALL]
    flat.append(jnp.zeros((SSD_CONV * SSD_CONV_DIM,), F32) if conv_part is None else conv_part.reshape(-1))
    flat.append(jnp.zeros((1,), F32) if loss_part is None else loss_part.reshape(1))
    flat = jnp.concatenate(flat)
    return jnp.pad(flat, (0, SMALL_ROWS * D_MODEL - flat.shape[0])).reshape(SMALL_ROWS, D_MODEL)


def _unpack_small(pack, like):
    flat = pack.reshape(-1)
    out, off = {}, 0
    for k in SMALL:
        n = like[k].size
        out[k] = flat[off:off + n].reshape(like[k].shape)
        off += n
    conv = flat[off:off + SSD_CONV * SSD_CONV_DIM].reshape(SSD_CONV, SSD_CONV_DIM)
    return out, conv, flat[off + SSD_CONV * SSD_CONV_DIM]


def _chip_major_cols(a):
    r = a.shape[0]
    return a.reshape(r, N_CHIP, -1).transpose(1, 0, 2)


def _from_chip_major_cols(a):
    return a.transpose(1, 0, 2).reshape(a.shape[1], -1)


def kernel(x, ffn1_norm, ffn1_w_gate, ffn1_w_up, ffn1_w_down, mix_norm, w_in, q_norm, k_norm, conv_w, conv_b, dt_bias, a_log, d_skip, ssd_norm, w_attn_branch, w_ssd_branch, w_out, ffn2_norm, ffn2_w_gate, ffn2_w_up, ffn2_w_down, loss_target, m_ffn1_norm, m_ffn1_w_gate, m_ffn1_w_up, m_ffn1_w_down, m_mix_norm, m_w_in, m_q_norm, m_k_norm, m_conv_w, m_conv_b, m_dt_bias, m_a_log, m_d_skip, m_ssd_norm, m_w_attn_branch, m_w_ssd_branch, m_w_out, m_ffn2_norm, m_ffn2_w_gate, m_ffn2_w_up, m_ffn2_w_down, v_ffn1_norm, v_ffn1_w_gate, v_ffn1_w_up, v_ffn1_w_down, v_mix_norm, v_w_in, v_q_norm, v_k_norm, v_conv_w, v_conv_b, v_dt_bias, v_a_log, v_d_skip, v_ssd_norm, v_w_attn_branch, v_w_ssd_branch, v_w_out, v_ffn2_norm, v_ffn2_w_gate, v_ffn2_w_up, v_ffn2_w_down):
    env = dict(locals())
    wts = {k: env[k] for k in WEIGHTS}
    moms = {k: env["m_" + k] for k in WEIGHTS}
    vars_ = {k: env["v_" + k] for k in WEIGHTS}
    x0 = x[0]
    target = loss_target[0]

    def gather(classes):
        shards = [jnp.concatenate([wts[k][0] for k, _ in CLASSES[c]], axis=0).astype(BF16) for c in classes]
        return gather_rider(shards, [CLASS_TILE[c] for c in classes])

    def reducer(classes, parts):
        sums = [sibling_sum(p, CLASS_TILE[c], f"sibling_sum_{c}") for c, p in zip(classes, parts)]
        return owner_sum_rider(sums, [CLASS_TILE[c] for c in classes])

    (w_ffn1_in,) = run_alone(gather(["ffn1_in"]), "gather_ffn1_in")
    x1, saved1, (w_ffn1_out, w_mix_in, w_mix_attn) = ffn_forward(
        x0, ffn1_norm, w_ffn1_in, lambda rode: rode[0], "ffn1", rider=gather(["ffn1_out", "mix_in", "mix_attn"]))
    w_in_full = _from_chip_major_cols(w_mix_in)
    mixer_w = dict(
        mix_norm=mix_norm,
        w_in_main=jnp.concatenate([w_in_full[:, :IN_DT0], w_in_full[:, IN_DT1:]], axis=1),
        w_in_dt=jnp.pad(w_in_full[:, IN_DT0:IN_DT1], ((0, 0), (0, DT_PAD - SSD_HEADS))),
        q_gain=jnp.tile(q_norm, (1, 2)), k_gain=jnp.tile(k_norm, (1, 2)),
        conv_w=_from_chip_major_cols(gather_conv_w(conv_w[0])), conv_b=conv_b, dt_bias=dt_bias, a_log=a_log,
        d_skip=d_skip, ssd_norm=ssd_norm, w_attn_branch=_from_chip_major_cols(w_mix_attn))

    def later_weights(rode):
        late = rode[0]
        return dict(w_ssd_branch=late[:, 704:1216].reshape(SSD_INNER, D_MODEL),
                    w_out=late[:, 1216:1472].reshape(D_MODEL, D_MODEL))

    x2, saved_mix, (w_late_out, w_ffn2_in) = mixer_forward(x1, mixer_w, gather(["late_out", "ffn2_in"]), later_weights)
    x3, saved2, _ = ffn_forward(x2, ffn2_norm, w_ffn2_in, lambda rode: w_late_out, "ffn2")
    dx3, sq = loss_grad(x3, target, "loss")

    grads = {}
    dx2, grads["ffn2_norm"], d_ffn2_in, d_ffn2_down = ffn_backward(dx3, x2, ffn2_norm, w_ffn2_in, w_late_out, saved2,
                                                                   "ffn2")

    def ride_early(g):
        late = jnp.concatenate([d_ffn2_down, g["w_ssd_branch"].reshape(N_CHIP, -1, D_MODEL),
                                g["w_out"].reshape(N_CHIP, -1, D_MODEL)], axis=1)
        return reducer(["ffn2_in", "late_out"], [d_ffn2_in, late])

    def ride_late(g):
        g_in = jnp.concatenate([g["w_in_main"][:, :IN_DT0], g["w_in_dt"][:, :SSD_HEADS], g["w_in_main"][:, IN_DT0:]],
                               axis=1)
        return reducer(["mix_in", "mix_attn"], [_chip_major_cols(g_in), _chip_major_cols(g["w_attn_branch"])])

    dx1, gmix = mixer_backward(dx2, x1, saved_mix, ride_early, ride_late)
    dx0, grads["ffn1_norm"], rode_in, rode_out = ffn_backward(
        dx1, x0, ffn1_norm, w_ffn1_in, w_ffn1_out, saved1, "ffn1",
        ride_down=lambda d: reducer(["ffn1_out"], [d]), ride_in=lambda d: reducer(["ffn1_in"], [d]))
    for k in ("mix_norm", "q_norm", "k_norm", "conv_b", "dt_bias", "a_log", "d_skip", "ssd_norm"):
        grads[k] = gmix[k]
    reduced = dict(zip(("ffn2_in", "late_out", "mix_in", "mix_attn", "ffn1_in", "ffn1_out"),
                       (*gmix["rode_early"], *gmix["rode_late"], rode_in[0], rode_out[0])))
    reduced = {c: r.reshape(-1, r.shape[2]) for c, r in reduced.items()}
    small_sum = all_reduce_small(_pack_small(grads, gmix["conv_w"], 0.5 * jnp.sum(sq) / D_MODEL))
    g_small, g_conv_full, loss = _unpack_small(small_sum, wts)
    chip = 2 * lax.axis_index("x") + lax.axis_index("y")
    g_conv = lax.dynamic_slice_in_dim(g_conv_full, chip * CONV_SHARD, CONV_SHARD, axis=1)

    g_final, delta, new_m, new_v = dict(g_small), {}, {}, {}

    def update(k, g_arr, row_off):
        shp = wts[k].shape
        two_d = shp[1:]
        res = adamw(wts[k].reshape(two_d), g_arr, row_off, moms[k].reshape(two_d), vars_[k].reshape(two_d),
                    f"adamw_{k}")
        g_final[k], delta[k], new_m[k], new_v[k] = (r.reshape(shp) for r in res)

    for cls, members in CLASSES.items():
        off = 0
        for k, rows in members:
            update(k, reduced[cls], off)
            off += rows
    update("conv_w", g_conv, 0)
    _, d, m2, v2 = adamw(_pack_small(wts, None), _pack_small(g_small, None), 0, _pack_small(moms, None),
                         _pack_small(vars_, None), "adamw_small")
    for res, packed in ((delta, d), (new_m, m2), (new_v, v2)):
        res.update(_unpack_small(packed, wts)[0])

    return (loss, dx0[None], *[g_final[k] for k in WEIGHTS], *[delta[k] for k in WEIGHTS],
            *[new_m[k] for k in WEIGHTS], *[new_v[k] for k in WEIGHTS])
```

```python
import collections
import functools
import math

import jax
import jax.numpy as jnp
from jax import lax
from jax.experimental import pallas as pl
from jax.experimental.pallas import tpu as pltpu

F32 = jnp.float32
BF16 = jnp.bfloat16
MESH = pl.DeviceIdType.MESH

EPS = 1e-6
D_MODEL = 1024
D_FF = 2816
N_CHIP = 4
FF_SHARD = D_FF // N_CHIP
HD = 64
BLK = 128
ATTN_DILATIONS = (1, 4, 16)
HEADS_PER_PATTERN = 8
N_ATTN_HEADS = 24
ALIBI_MAX_EXP = 8.0
ATTN_QKV = 1536
GROUP_W = 512
SSD_INNER = 2048
SSD_HEADS = 32
SSD_GROUPS = 4
SSD_CONV = 4
SSD_CONV_DIM = 3072
IN_COLS = 11808
IN_DT0, IN_DT1 = 9728, 9760
COL_K, COL_V, COL_Z, COL_XBC, COL_GA, COL_GS, P_COLS = 1536, 3072, 4608, 6656, 9728, 10752, 11776
DT_PAD = 128

ADAM_LR, ADAM_B1, ADAM_B2, ADAM_EPS, ADAM_WD, ADAM_STEP = 0.001, 0.9, 0.999, 1e-08, 0.01, 10

V7X_VMEM_LIMIT = 56 * 1024 * 1024
NEG = -1e30


Rider = collections.namedtuple("Rider", "arrays out_shape scratch start finish")
Rider.__doc__ = """An exchange between devices that rides in a compute kernel: its copies are started in the host's
first grid step and waited for in its last, so they travel while the host computes.  arrays / out_shape: extra HBM
operands and results; scratch: extra scratch; start, finish: f(in_refs, out_refs, scratch_refs)."""


def _call(body, *, name, out_shape, in_specs, out_specs, grid=(), scratch_shapes=(), aliases=None, rider=None):
    params = dict(dimension_semantics=("arbitrary",) * len(grid), vmem_limit_bytes=V7X_VMEM_LIMIT)
    if rider is None:
        return pl.pallas_call(
            body, out_shape=out_shape, grid=grid, in_specs=in_specs, out_specs=out_specs,
            scratch_shapes=scratch_shapes, input_output_aliases=aliases or {}, name=name,
            compiler_params=pltpu.CompilerParams(**params))
    single = not isinstance(out_shape, (list, tuple))
    main_out = [out_shape] if single else list(out_shape)
    main_specs = [out_specs] if single else list(out_specs)
    n_in, n_out, n_scr = len(in_specs), len(main_out), len(scratch_shapes)
    r_in, r_out = len(rider.arrays), len(rider.out_shape)

    def wrapped(*refs):
        ins, refs = refs[:n_in], refs[n_in:]
        r_ins, refs = refs[:r_in], refs[r_in:]
        outs, refs = refs[:n_out], refs[n_out:]
        r_outs, refs = refs[:r_out], refs[r_out:]
        scr, r_scr = refs[:n_scr], refs[n_scr:]
        first = last = None
        for axis, size in enumerate(grid):
            at_start, at_end = pl.program_id(axis) == 0, pl.program_id(axis) == size - 1
            first = at_start if first is None else jnp.logical_and(first, at_start)
            last = at_end if last is None else jnp.logical_and(last, at_end)

        @pl.when(first)
        def _():
            rider.start(r_ins, r_outs, r_scr)

        body(*ins, *outs, *scr)

        @pl.when(last)
        def _():
            rider.finish(r_ins, r_outs, r_scr)

    hbm = pl.BlockSpec(memory_space=pl.ANY)
    call = pl.pallas_call(
        wrapped, out_shape=main_out + list(rider.out_shape), grid=grid, in_specs=list(in_specs) + [hbm] * r_in,
        out_specs=main_specs + [hbm] * r_out, scratch_shapes=list(scratch_shapes) + list(rider.scratch), name=name,
        compiler_params=pltpu.CompilerParams(has_side_effects=True, **params))

    def run(*args):
        res = call(*args, *rider.arrays)
        main = res[:n_out]
        return (main[0] if single else main), res[n_out:]

    return run


def _sds(shape, dtype):
    return jax.ShapeDtypeStruct(tuple(shape), dtype)


def _dot(a, b):
    return jnp.dot(a, b, preferred_element_type=F32)


def _dot_nt(a, b):
    return lax.dot_general(a, b, (((1,), (1,)), ((), ())), preferred_element_type=F32)


def _dot_tn(a, b):
    return lax.dot_general(a, b, (((0,), (0,)), ((), ())), preferred_element_type=F32)


def _dot_hi(a, b):
    return jnp.dot(a, b, preferred_element_type=F32, precision=lax.Precision.HIGHEST)


def _sigmoid(x):
    return 1.0 / (1.0 + jnp.exp(-x))


def _lane_first_half(shape):
    return lax.broadcasted_iota(jnp.int32, shape, len(shape) - 1) < HD


def _pair_sum(x, first):
    s_all = jnp.sum(x, axis=-1, keepdims=True)
    s_a = jnp.sum(jnp.where(first, x, 0.0), axis=-1, keepdims=True)
    return s_a, s_all - s_a


def _rowwise(name, fn, rows, consts, outs, accs=(), tm=512):
    n_rows = None
    in_arrays, in_specs = [], []
    for r in rows:
        if isinstance(r, tuple):
            arr, w, cb = r
            spec = pl.BlockSpec((tm, w), functools.partial(lambda i, cb: (i, cb), cb=cb))
        else:
            arr = r
            spec = pl.BlockSpec((tm, arr.shape[1]), lambda i: (i, 0))
        n_rows = arr.shape[0]
        in_arrays.append(arr)
        in_specs.append(spec)
    for c in consts:
        in_arrays.append(c)
        in_specs.append(pl.BlockSpec(c.shape, functools.partial(lambda i, n: (0,) * n, n=c.ndim)))
    out_shape = [_sds(s, d) for s, d in outs] + [_sds(s, d) for s, d in accs]
    out_specs = [pl.BlockSpec((tm, s[1]), lambda i: (i, 0)) for s, _ in outs]
    out_specs += [pl.BlockSpec(s, functools.partial(lambda i, n: (0,) * n, n=len(s))) for s, _ in accs]

    def body(*refs):
        fn(pl.program_id(0), *refs)

    res = _call(body, name=name, out_shape=out_shape, grid=(n_rows // tm,), in_specs=in_specs,
                out_specs=out_specs)(*in_arrays)
    return res


def rms_fwd(x, gain, name):
    def fn(i, x_ref, g_ref, h_ref):
        xv = x_ref[...]
        r = lax.rsqrt(jnp.mean(xv * xv, axis=-1, keepdims=True) + EPS)
        h_ref[...] = (xv * r * g_ref[...]).astype(h_ref.dtype)

    return _rowwise(name, fn, [x], [gain], [(x.shape, BF16)])[0]


def rms_bwd(dhs, x, gain, dx_in, name):
    n = len(dhs)

    def fn(i, *refs):
        dh_refs, (x_ref, dxin_ref, g_ref, dx_ref, dg_ref) = refs[:n], refs[n:]
        dh = dh_refs[0][...]
        for r in dh_refs[1:]:
            dh = dh + r[...]
        xv = x_ref[...]
        r = lax.rsqrt(jnp.mean(xv * xv, axis=-1, keepdims=True) + EPS)
        xn = xv * r
        dxn = dh * g_ref[...]
        dx_ref[...] = dxin_ref[...] + r * (dxn - xn * jnp.mean(dxn * xn, axis=-1, keepdims=True))

        @pl.when(i == 0)
        def _():
            dg_ref[...] = jnp.zeros_like(dg_ref)

        dg_ref[...] += jnp.sum(dh * xn, axis=0, keepdims=True)

    return _rowwise(name, fn, list(dhs) + [x, dx_in], [gain], [(x.shape, F32)], [((1, x.shape[1]), F32)])


def loss_grad(y, target, name):
    def fn(i, y_ref, t_ref, dy_ref, sq_ref):
        err = y_ref[...] - t_ref[...]
        dy_ref[...] = err * (1.0 / y_ref.shape[1])

        @pl.when(i == 0)
        def _():
            sq_ref[...] = jnp.zeros_like(sq_ref)

        sq_ref[...] += jnp.sum(err * err, axis=0, keepdims=True)

    return _rowwise(name, fn, [y, target], [], [(y.shape, F32)], [((1, y.shape[1]), F32)])


def matmul_nn(a, b, name, out_dtype, tm, tn, res=None, scale=1.0, rider=None):
    s, k = a.shape
    n = b.shape[1]

    def body(*refs):
        if res is None:
            a_ref, b_ref, o_ref = refs
            o_ref[...] = _dot(a_ref[...], b_ref[...]).astype(o_ref.dtype)
        else:
            a_ref, b_ref, r_ref, o_ref = refs
            o_ref[...] = (r_ref[...] + scale * _dot(a_ref[...], b_ref[...])).astype(o_ref.dtype)

    in_specs = [pl.BlockSpec((tm, k), lambda i, j: (i, 0)), pl.BlockSpec((k, tn), lambda i, j: (0, j))]
    args = [a, b]
    if res is not None:
        in_specs.append(pl.BlockSpec((tm, tn), lambda i, j: (i, j)))
        args.append(res)
    return _call(body, name=name, out_shape=_sds((s, n), out_dtype), grid=(s // tm, n // tn), in_specs=in_specs,
                 out_specs=pl.BlockSpec((tm, tn), lambda i, j: (i, j)), rider=rider)(*args)


def matmul_nt(a, b, name, out_dtype, tm, tn, tk, rider=None):
    s, k = a.shape
    n = b.shape[0]
    nk = k // tk

    def body(a_ref, b_ref, o_ref, acc_ref):
        kk = pl.program_id(2)

        @pl.when(kk == 0)
        def _():
            acc_ref[...] = jnp.zeros_like(acc_ref)

        acc_ref[...] += _dot_nt(a_ref[...].astype(BF16), b_ref[...])

        @pl.when(kk == nk - 1)
        def _():
            o_ref[...] = acc_ref[...].astype(o_ref.dtype)

    return _call(body, name=name, out_shape=_sds((s, n), out_dtype), grid=(s // tm, n // tn, nk),
                 in_specs=[pl.BlockSpec((tm, tk), lambda i, j, kk: (i, kk)),
                           pl.BlockSpec((tn, tk), lambda i, j, kk: (j, kk))],
                 out_specs=pl.BlockSpec((tm, tn), lambda i, j, kk: (i, j)),
                 scratch_shapes=[pltpu.VMEM((tm, tn), F32)], rider=rider)(a, b)


def matmul_tn(a, b, name, tn, ts, a_scale=None, b_scale=None, rider=None):
    s, m = a.shape
    n = b.shape[1]
    ns = s // ts

    def body(a_ref, b_ref, o_ref, acc_ref):
        ss = pl.program_id(1)

        @pl.when(ss == 0)
        def _():
            acc_ref[...] = jnp.zeros_like(acc_ref)

        av, bv = a_ref[...], b_ref[...]
        if a_scale is not None:
            av = av * a_scale
        if b_scale is not None:
            bv = bv * b_scale
        acc_ref[...] += _dot_tn(av.astype(BF16), bv.astype(BF16))

        @pl.when(ss == ns - 1)
        def _():
            o_ref[...] = acc_ref[...].astype(o_ref.dtype)

    return _call(body, name=name, out_shape=_sds((m, n), BF16), grid=(n // tn, ns),
                 in_specs=[pl.BlockSpec((ts, m), lambda j, ss: (ss, 0)), pl.BlockSpec((ts, tn), lambda j, ss: (ss, j))],
                 out_specs=pl.BlockSpec((m, tn), lambda j, ss: (0, j)),
                 scratch_shapes=[pltpu.VMEM((m, tn), F32)], rider=rider)(a, b)


def ffn_up(h, w704, gate_blk, up_blk, name, tm=512, rider=None):
    s = h.shape[0]

    def body(h_ref, wg_ref, wu_ref, g_ref, u_ref, a_ref):
        hv = h_ref[...]
        g = _dot(hv, wg_ref[...])
        u = _dot(hv, wu_ref[...])
        g_ref[...] = g.astype(BF16)
        u_ref[...] = u.astype(BF16)
        a_ref[...] = (g * _sigmoid(g) * u).astype(BF16)

    ospec = pl.BlockSpec((None, tm, FF_SHARD), lambda j, i: (j, i, 0))
    shp = _sds((N_CHIP, s, FF_SHARD), BF16)
    return _call(body, name=name, out_shape=[shp, shp, shp], grid=(N_CHIP, s // tm),
                 in_specs=[pl.BlockSpec((tm, D_MODEL), lambda j, i: (i, 0)),
                           pl.BlockSpec((None, D_MODEL, FF_SHARD), lambda j, i: (j, gate_blk, 0)),
                           pl.BlockSpec((None, D_MODEL, FF_SHARD), lambda j, i: (j, up_blk, 0))],
                 out_specs=[ospec, ospec, ospec], rider=rider)(h, w704, w704)


def ffn_down(a, w1024, blk, x, name, tm=512):
    s = x.shape[0]

    def body(a_ref, wd_ref, x_ref, o_ref):
        acc = _dot(a_ref[0], wd_ref[0])
        for j in range(1, N_CHIP):
            acc += _dot(a_ref[j], wd_ref[j])
        o_ref[...] = x_ref[...] + 0.5 * acc

    return _call(body, name=name, out_shape=_sds((s, D_MODEL), F32), grid=(s // tm,),
                 in_specs=[pl.BlockSpec((N_CHIP, tm, FF_SHARD), lambda i: (0, i, 0)),
                           pl.BlockSpec((N_CHIP, FF_SHARD, D_MODEL), lambda i: (0, blk, 0)),
                           pl.BlockSpec((tm, D_MODEL), lambda i: (i, 0))],
                 out_specs=pl.BlockSpec((tm, D_MODEL), lambda i: (i, 0)))(a, w1024, x)


def ffn_bwd_hidden(dx, w1024, blk, g, u, name, tm=1024, rider=None):
    s = dx.shape[0]

    def body(dx_ref, wd_ref, g_ref, u_ref, dg_ref, du_ref):
        dy = (0.5 * dx_ref[...]).astype(BF16)
        da = _dot_nt(dy, wd_ref[...])
        gv = g_ref[...].astype(F32)
        uv = u_ref[...].astype(F32)
        sg = _sigmoid(gv)
        dg_ref[...] = (da * uv * (sg * (1.0 + gv * (1.0 - sg)))).astype(BF16)
        du_ref[...] = (da * gv * sg).astype(BF16)

    hspec = pl.BlockSpec((None, tm, FF_SHARD), lambda j, i: (j, i, 0))
    shp = _sds((N_CHIP, s, FF_SHARD), BF16)
    return _call(body, name=name, out_shape=[shp, shp], grid=(N_CHIP, s // tm),
                 in_specs=[pl.BlockSpec((tm, D_MODEL), lambda j, i: (i, 0)),
                           pl.BlockSpec((None, FF_SHARD, D_MODEL), lambda j, i: (j, blk, 0)), hspec, hspec],
                 out_specs=[hspec, hspec], rider=rider)(dx, w1024, g, u)


def ffn_bwd_input(dg, du, w704, gate_blk, up_blk, name, tm=512, rider=None):
    s = dg.shape[1]

    def body(dg_ref, du_ref, wg_ref, wu_ref, o_ref):
        acc = _dot_nt(dg_ref[0], wg_ref[0]) + _dot_nt(du_ref[0], wu_ref[0])
        for j in range(1, N_CHIP):
            acc += _dot_nt(dg_ref[j], wg_ref[j]) + _dot_nt(du_ref[j], wu_ref[j])
        o_ref[...] = acc

    hspec = pl.BlockSpec((N_CHIP, tm, FF_SHARD), lambda i: (0, i, 0))
    return _call(body, name=name, out_shape=_sds((s, D_MODEL), F32), grid=(s // tm,),
                 in_specs=[hspec, hspec,
                           pl.BlockSpec((N_CHIP, D_MODEL, FF_SHARD), lambda i: (0, gate_blk, 0), pl.Buffered(1)),
                           pl.BlockSpec((N_CHIP, D_MODEL, FF_SHARD), lambda i: (0, up_blk, 0), pl.Buffered(1))],
                 out_specs=pl.BlockSpec((tm, D_MODEL), lambda i: (i, 0)), rider=rider)(dg, du, w704, w704)


def ffn_wgrad_in(h, dgu, name, ts=1024):
    s = h.shape[0]
    ns = s // ts

    def body(h_ref, d_ref, o_ref, acc_ref):
        ss = pl.program_id(1)

        @pl.when(ss == 0)
        def _():
            acc_ref[...] = jnp.zeros_like(acc_ref)

        acc_ref[...] += _dot_tn(h_ref[...], d_ref[...])

        @pl.when(ss == ns - 1)
        def _():
            o_ref[...] = acc_ref[...].astype(BF16)

    return _call(body, name=name, out_shape=_sds((N_CHIP, D_MODEL, FF_SHARD), BF16), grid=(N_CHIP, ns),
                 in_specs=[pl.BlockSpec((ts, D_MODEL), lambda j, ss: (ss, 0)),
                           pl.BlockSpec((None, ts, FF_SHARD), lambda j, ss: (j, ss, 0))],
                 out_specs=pl.BlockSpec((None, D_MODEL, FF_SHARD), lambda j, ss: (j, 0, 0)),
                 scratch_shapes=[pltpu.VMEM((D_MODEL, FF_SHARD), F32)])(h, dgu)


def ffn_wgrad_down(a, dx, name, ts=1024):
    s = dx.shape[0]
    ns = s // ts

    def body(a_ref, dx_ref, o_ref, acc_ref):
        ss = pl.program_id(1)

        @pl.when(ss == 0)
        def _():
            acc_ref[...] = jnp.zeros_like(acc_ref)

        acc_ref[...] += _dot_tn(a_ref[...], (0.5 * dx_ref[...]).astype(BF16))

        @pl.when(ss == ns - 1)
        def _():
            o_ref[...] = acc_ref[...].astype(BF16)

    return _call(body, name=name, out_shape=_sds((N_CHIP, FF_SHARD, D_MODEL), BF16), grid=(N_CHIP, ns),
                 in_specs=[pl.BlockSpec((None, ts, FF_SHARD), lambda j, ss: (j, ss, 0)),
                           pl.BlockSpec((ts, D_MODEL), lambda j, ss: (ss, 0))],
                 out_specs=pl.BlockSpec((None, FF_SHARD, D_MODEL), lambda j, ss: (j, 0, 0)),
                 scratch_shapes=[pltpu.VMEM((FF_SHARD, D_MODEL), F32)])(a, dx)


def ffn_forward(x, gain, w704, get_w1024, tag, rider=None):
    h = rms_fwd(x, gain, f"{tag}_rms")
    res = ffn_up(h, w704, 0, 1, f"{tag}_up", rider=rider)
    (g, u, a), rode = res if rider is not None else (res, None)
    y = ffn_down(a, get_w1024(rode), 0, x, f"{tag}_down")
    return y, (h, g, u, a), rode


def ffn_backward(dy, x, gain, w704, w1024, saved, tag, ride_down=None, ride_in=None):
    h, g, u, a = saved
    d_wd = ffn_wgrad_down(a, dy, f"{tag}_dwd")
    if ride_down is not None:
        (dg, du), d_wd = ffn_bwd_hidden(dy, w1024, 0, g, u, f"{tag}_dhid", rider=ride_down(d_wd))
    else:
        dg, du = ffn_bwd_hidden(dy, w1024, 0, g, u, f"{tag}_dhid")
    d_win = jnp.concatenate([ffn_wgrad_in(h, dg, f"{tag}_dwg"), ffn_wgrad_in(h, du, f"{tag}_dwu")], axis=1)
    if ride_in is not None:
        dh, d_win = ffn_bwd_input(dg, du, w704, 0, 1, f"{tag}_dh", rider=ride_in(d_win))
    else:
        dh = ffn_bwd_input(dg, du, w704, 0, 1, f"{tag}_dh")
    dx, d_gain = rms_bwd([dh], x, gain, dy, f"{tag}_drms")
    return dx, d_gain, d_win, d_wd


def _alibi_slope(head):
    return float(2.0 ** (-ALIBI_MAX_EXP * (head + 1) / N_ATTN_HEADS))


def _head_norm(t, gain_pair, first):
    sa, sb = _pair_sum(t * t, first)
    r = jnp.where(first, lax.rsqrt(sa * (1.0 / HD) + EPS), lax.rsqrt(sb * (1.0 / HD) + EPS))
    return t * r * gain_pair, r


def qk_norm_fwd(p, q_gain, k_gain, name):
    s = p.shape[0]

    def fn(i, q_ref, k_ref, qg_ref, kg_ref, qn_ref, kn_ref):
        first = _lane_first_half((q_ref.shape[0], 2 * HD))
        for src, g_ref, dst in ((q_ref, qg_ref, qn_ref), (k_ref, kg_ref, kn_ref)):
            for pr in range(ATTN_QKV // (2 * HD)):
                cols = slice(pr * 2 * HD, (pr + 1) * 2 * HD)
                y, _ = _head_norm(src[:, cols].astype(F32), g_ref[...], first)
                dst[:, cols] = y.astype(BF16)

    return _rowwise(name, fn, [(p, ATTN_QKV, 0), (p, ATTN_QKV, 1)], [q_gain, k_gain],
                    [((s, ATTN_QKV), BF16), ((s, ATTN_QKV), BF16)])


def qk_norm_bwd(p, dqn, dkn, q_gain, k_gain, name):
    s = p.shape[0]

    def fn(i, q_ref, k_ref, dqn_ref, dkn_ref, qg_ref, kg_ref, dq_ref, dk_ref, dqg_ref, dkg_ref):
        first = _lane_first_half((q_ref.shape[0], 2 * HD))

        @pl.when(i == 0)
        def _():
            dqg_ref[...] = jnp.zeros_like(dqg_ref)
            dkg_ref[...] = jnp.zeros_like(dkg_ref)

        for src, d_ref, g_ref, dst, dg_ref in ((q_ref, dqn_ref, qg_ref, dq_ref, dqg_ref),
                                               (k_ref, dkn_ref, kg_ref, dk_ref, dkg_ref)):
            for pr in range(ATTN_QKV // (2 * HD)):
                cols = slice(pr * 2 * HD, (pr + 1) * 2 * HD)
                t = src[:, cols].astype(F32)
                sa, sb = _pair_sum(t * t, first)
                r = jnp.where(first, lax.rsqrt(sa * (1.0 / HD) + EPS), lax.rsqrt(sb * (1.0 / HD) + EPS))
                xn = t * r
                dy = d_ref[:, cols]
                dg_ref[:, cols] += jnp.sum(dy * xn, axis=0, keepdims=True)
                dxn = dy * g_ref[...]
                ma, mb = _pair_sum(dxn * xn, first)
                mean = jnp.where(first, ma, mb) * (1.0 / HD)
                dst[:, cols] = (r * (dxn - xn * mean)).astype(BF16)

    return _rowwise(name, fn, [(p, ATTN_QKV, 0), (p, ATTN_QKV, 1), dqn, dkn], [q_gain, k_gain],
                    [((s, ATTN_QKV), BF16), ((s, ATTN_QKV), BF16)], [((1, ATTN_QKV), F32), ((1, ATTN_QKV), F32)])


def _to_streams(a, d):
    if d == 1:
        return a
    s, c = a.shape
    return a.reshape(s // d, d, c).transpose(1, 0, 2).reshape(s, c)


def _from_streams(a, d):
    if d == 1:
        return a
    s, c = a.shape
    return a.reshape(d, s // d, c).transpose(1, 0, 2).reshape(s, c)


def _attn_masks():
    row = lax.broadcasted_iota(jnp.int32, (BLK, BLK), 0)
    col = lax.broadcasted_iota(jnp.int32, (BLK, BLK), 1)
    rel_diag = row - col
    rel_prev = rel_diag + BLK
    return rel_diag, rel_prev


def attn_fwd(q, k, v, pattern, name, tq=512):
    s = q.shape[0]
    d = ATTN_DILATIONS[pattern]
    blocks_per_stream = (s // d) // BLK
    nsb = tq // BLK

    def body(q_ref, k_ref, v_ref, kp_ref, vp_ref, o_ref, l_ref):
        i = pl.program_id(0)
        rel_diag, rel_prev = _attn_masks()
        first = _lane_first_half((BLK, 2 * HD))
        rd_f = (rel_diag * d).astype(F32)
        rp_f = (rel_prev * d).astype(F32)
        for sb in range(nsb):
            rows = slice(sb * BLK, (sb + 1) * BLK)
            has_prev = ((i * nsb + sb) % blocks_per_stream != 0).astype(jnp.int32)
            m_diag = rel_diag >= 0
            m_prev = (rel_prev + (1 - has_prev) * (4 * BLK)) <= BLK
            for pr in range(GROUP_W // (2 * HD)):
                cols = slice(pr * 2 * HD, (pr + 1) * 2 * HD)
                qp = q_ref[rows, cols]
                kc, vc = k_ref[rows, cols], v_ref[rows, cols]
                if sb == 0:
                    kp, vp = kp_ref[:, cols], vp_ref[:, cols]
                else:
                    prows = slice((sb - 1) * BLK, sb * BLK)
                    kp, vp = k_ref[prows, cols], v_ref[prows, cols]
                outs, lses = [], []
                for e in range(2):
                    slope = _alibi_slope(pattern * HEADS_PER_PATTERN + 2 * pr + e)
                    qm = jnp.where(first if e == 0 else jnp.logical_not(first), qp, jnp.zeros_like(qp))
                    s1 = jnp.where(m_diag, _dot_nt(qm, kc) * 0.125 - slope * rd_f, NEG)
                    s0 = jnp.where(m_prev, _dot_nt(qm, kp) * 0.125 - slope * rp_f, NEG)
                    m = jnp.maximum(jnp.max(s1, axis=-1, keepdims=True), jnp.max(s0, axis=-1, keepdims=True))
                    p1 = jnp.exp(s1 - m)
                    p0 = jnp.exp(s0 - m)
                    l = jnp.sum(p1, axis=-1, keepdims=True) + jnp.sum(p0, axis=-1, keepdims=True)
                    inv = 1.0 / l
                    outs.append(_dot((p1 * inv).astype(BF16), vc) + _dot((p0 * inv).astype(BF16), vp))
                    lses.append(m + jnp.log(l))
                o_ref[rows, cols] = jnp.where(first, outs[0], outs[1])
                l_ref[rows, cols] = jnp.where(first, lses[0], lses[1])

    cur = pl.BlockSpec((tq, GROUP_W), lambda i: (i, 0))
    prev = pl.BlockSpec((BLK, GROUP_W), lambda i: (jnp.maximum(i * nsb - 1, 0), 0))
    return _call(body, name=name, out_shape=[_sds((s, GROUP_W), F32), _sds((s, GROUP_W), F32)], grid=(s // tq,),
                 in_specs=[cur, cur, cur, prev, prev], out_specs=[cur, cur])(q, k, v, k, v)


def attn_merge_fwd(os_, lses, name):
    s = os_[0].shape[0]

    def fn(i, o0, o1, o2, l0, l1, l2, out_ref):
        m = jnp.maximum(jnp.maximum(l0[...], l1[...]), l2[...])
        e0, e1, e2 = jnp.exp(l0[...] - m), jnp.exp(l1[...] - m), jnp.exp(l2[...] - m)
        inv = 1.0 / (e0 + e1 + e2)
        out_ref[...] = ((e0 * inv) * o0[...] + (e1 * inv) * o1[...] + (e2 * inv) * o2[...]).astype(BF16)

    return _rowwise(name, fn, list(os_) + list(lses), [], [((s, GROUP_W), BF16)])[0]


def attn_merge_bwd(d_out, os_, lses, name):
    s = d_out.shape[0]

    def fn(i, do_ref, o0, o1, o2, l0, l1, l2, d0, d1, d2, c0, c1, c2):
        first = _lane_first_half((do_ref.shape[0], 2 * HD))
        m = jnp.maximum(jnp.maximum(l0[...], l1[...]), l2[...])
        e0, e1, e2 = jnp.exp(l0[...] - m), jnp.exp(l1[...] - m), jnp.exp(l2[...] - m)
        inv = 1.0 / (e0 + e1 + e2)
        w0, w1, w2 = e0 * inv, e1 * inv, e2 * inv
        do = do_ref[...]
        prod = do * (w0 * o0[...] + w1 * o1[...] + w2 * o2[...])
        for pr in range(GROUP_W // (2 * HD)):
            cols = slice(pr * 2 * HD, (pr + 1) * 2 * HD)
            ta, tb = _pair_sum(prod[:, cols], first)
            t = jnp.where(first, ta, tb)
            for w, c_ref in ((w0, c0), (w1, c1), (w2, c2)):
                c_ref[:, cols] = w[:, cols] * t
        for w, d_ref in ((w0, d0), (w1, d1), (w2, d2)):
            d_ref[...] = (w * do).astype(BF16)

    shp = (s, GROUP_W)
    return _rowwise(name, fn, [d_out] + list(os_) + list(lses), [],
                    [(shp, BF16)] * 3 + [(shp, F32)] * 3)


def attn_bwd(q, k, v, d_o, cterm, lse, pattern, name, tq=512):
    s = q.shape[0]
    d = ATTN_DILATIONS[pattern]
    blocks_per_stream = (s // d) // BLK
    nsb = tq // BLK
    n_blocks = s // BLK

    def body(q_ref, k_ref, v_ref, do_ref, c_ref, l_ref, kp_ref, vp_ref, qn_ref, don_ref, cn_ref, ln_ref,
             dq_ref, dk_ref, dv_ref):
        i = pl.program_id(0)
        rel_diag, rel_prev = _attn_masks()
        first = _lane_first_half((BLK, 2 * HD))
        second = jnp.logical_not(first)
        rd_f = (rel_diag * d).astype(F32)
        rp_f = (rel_prev * d).astype(F32)
        m_diag = rel_diag >= 0
        dq_ref[...] = jnp.zeros_like(dq_ref)
        dk_ref[...] = jnp.zeros_like(dk_ref)
        dv_ref[...] = jnp.zeros_like(dv_ref)

        def pair(qp, dop, cp, lp, kp, vp, rel_f, mask):
            dq = dk = dv = None
            for e in range(2):
                lanes = first if e == 0 else second
                slope = slopes[e]
                qm = jnp.where(lanes, qp, jnp.zeros_like(qp))
                dom = jnp.where(lanes, dop, jnp.zeros_like(dop))
                km = jnp.where(lanes, kp, jnp.zeros_like(kp))
                sc = jnp.where(mask, _dot_nt(qm, kp) * 0.125 - slope * rel_f, NEG)
                pm = jnp.exp(sc - lp[:, e * HD:e * HD + 1])
                dl = pm * (_dot_nt(dom, vp) - cp[:, e * HD:e * HD + 1])
                dl16 = dl.astype(BF16)
                t_dq = _dot(dl16, km)
                t_dk = _dot_tn(dl16, qm)
                t_dv = _dot_tn(pm.astype(BF16), dom)
                dq = t_dq if dq is None else dq + t_dq
                dk = t_dk if dk is None else dk + t_dk
                dv = t_dv if dv is None else dv + t_dv
            return dq * 0.125, dk * 0.125, dv

        for pr in range(GROUP_W // (2 * HD)):
            cols = slice(pr * 2 * HD, (pr + 1) * 2 * HD)
            slopes = [_alibi_slope(pattern * HEADS_PER_PATTERN + 2 * pr + e) for e in range(2)]
            for sb in range(nsb + 1):
                gb = i * nsb + sb
                if sb < nsb:
                    rows = slice(sb * BLK, (sb + 1) * BLK)
                    qp, dop, cp, lp = q_ref[rows, cols], do_ref[rows, cols], c_ref[rows, cols], l_ref[rows, cols]
                else:
                    qp, dop, cp, lp = qn_ref[:, cols], don_ref[:, cols], cn_ref[:, cols], ln_ref[:, cols]
                if sb < nsb:
                    dq1, dk1, dv1 = pair(qp, dop, cp, lp, k_ref[rows, cols], v_ref[rows, cols], rd_f, m_diag)
                    dq_ref[rows, cols] += dq1
                    dk_ref[rows, cols] += dk1
                    dv_ref[rows, cols] += dv1
                valid = jnp.logical_and(gb % blocks_per_stream != 0, gb < n_blocks).astype(jnp.int32)
                m_prev = jnp.logical_and(rel_prev <= BLK, (rel_prev + (1 - valid) * (4 * BLK)) <= BLK)
                if sb == 0:
                    kp, vp = kp_ref[:, cols], vp_ref[:, cols]
                else:
                    prows = slice((sb - 1) * BLK, sb * BLK)
                    kp, vp = k_ref[prows, cols], v_ref[prows, cols]
                dq0, dk0, dv0 = pair(qp, dop, cp, lp, kp, vp, rp_f, m_prev)
                if sb < nsb:
                    dq_ref[rows, cols] += dq0
                if sb > 0:
                    dk_ref[prows, cols] += dk0
                    dv_ref[prows, cols] += dv0

    cur = pl.BlockSpec((tq, GROUP_W), lambda i: (i, 0))
    prev = pl.BlockSpec((BLK, GROUP_W), lambda i: (jnp.maximum(i * nsb - 1, 0), 0))
    nxt = pl.BlockSpec((BLK, GROUP_W), lambda i: (jnp.minimum((i + 1) * nsb, n_blocks - 1), 0))
    shp = _sds((s, GROUP_W), F32)
    return _call(body, name=name, out_shape=[shp, shp, shp], grid=(s // tq,),
                 in_specs=[cur] * 6 + [prev, prev] + [nxt] * 4, out_specs=[cur, cur, cur])(
                     q, k, v, d_o, cterm, lse, k, v, q, d_o, cterm, lse)


def _band_constants(d):
    row = lax.broadcasted_iota(jnp.int32, (2 * BLK, 2 * BLK), 0)
    col = lax.broadcasted_iota(jnp.int32, (2 * BLK, 2 * BLK), 1)
    rel = BLK + jnp.where(row >= BLK, row - BLK, row) - col
    band = jnp.logical_and(rel >= 0, rel <= BLK)
    return (rel * d).astype(F32), band, (col >= BLK).astype(jnp.int32)


def _stack_heads(x, first):
    zero = jnp.zeros_like(x)
    return jnp.concatenate([jnp.where(first, x, zero), jnp.where(first, zero, x)], axis=0)


def _unstack_heads(x2, first):
    return jnp.where(first, x2[:BLK], x2[BLK:])


def _head_column(x):
    return jnp.concatenate([x[:, 0:1], x[:, HD:HD + 1]], axis=0)


def attn_fwd2(q, k, v, pattern, name, tq=512):
    s = q.shape[0]
    d = ATTN_DILATIONS[pattern]
    blocks_per_stream = (s // d) // BLK
    nsb = tq // BLK

    def body(q_ref, k_ref, v_ref, kp_ref, vp_ref, o_ref, l_ref):
        i = pl.program_id(0)
        rel_f, band, own = _band_constants(d)
        first = _lane_first_half((BLK, 2 * HD))
        upper = lax.broadcasted_iota(jnp.int32, (2 * BLK, 1), 0) < BLK
        for sb in range(nsb):
            rows = slice(sb * BLK, (sb + 1) * BLK)
            has_prev = ((i * nsb + sb) % blocks_per_stream != 0).astype(jnp.int32)
            mask = jnp.logical_and(band, (own + has_prev) > 0)
            for pr in range(GROUP_W // (2 * HD)):
                cols = slice(pr * 2 * HD, (pr + 1) * 2 * HD)
                if sb == 0:
                    kcat = jnp.concatenate([kp_ref[:, cols], k_ref[rows, cols]], axis=0)
                    vcat = jnp.concatenate([vp_ref[:, cols], v_ref[rows, cols]], axis=0)
                else:
                    both = slice((sb - 1) * BLK, (sb + 1) * BLK)
                    kcat, vcat = k_ref[both, cols], v_ref[both, cols]
                h0 = pattern * HEADS_PER_PATTERN + 2 * pr
                slope = jnp.where(upper, _alibi_slope(h0), _alibi_slope(h0 + 1))
                sc = _dot_nt(_stack_heads(q_ref[rows, cols], first), kcat) * 0.125 - slope * rel_f
                sc = jnp.where(mask, sc, NEG)
                m = jnp.max(sc, axis=-1, keepdims=True)
                p = jnp.exp(sc - m)
                l = jnp.sum(p, axis=-1, keepdims=True)
                o2 = _dot((p * (1.0 / l)).astype(BF16), vcat)
                o_ref[rows, cols] = _unstack_heads(o2, first)
                lse = m + jnp.log(l)
                l_ref[rows, cols] = jnp.where(first, lse[:BLK], lse[BLK:])

    cur = pl.BlockSpec((tq, GROUP_W), lambda i: (i, 0))
    prev = pl.BlockSpec((BLK, GROUP_W), lambda i: (jnp.maximum(i * nsb - 1, 0), 0))
    return _call(body, name=name, out_shape=[_sds((s, GROUP_W), F32), _sds((s, GROUP_W), F32)], grid=(s // tq,),
                 in_specs=[cur, cur, cur, prev, prev], out_specs=[cur, cur])(q, k, v, k, v)


def attn_bwd2(q, k, v, d_o, cterm, lse, pattern, name, tq=512):
    s = q.shape[0]
    d = ATTN_DILATIONS[pattern]
    blocks_per_stream = (s // d) // BLK
    nsb = tq // BLK
    n_blocks = s // BLK

    def body(q_ref, k_ref, v_ref, do_ref, c_ref, l_ref, kp_ref, vp_ref, qn_ref, kn_ref, vn_ref, don_ref, cn_ref,
             ln_ref, dq_ref, dk_ref, dv_ref):
        i = pl.program_id(0)
        rel_f, band, own = _band_constants(d)
        first = _lane_first_half((BLK, 2 * HD))
        upper = lax.broadcasted_iota(jnp.int32, (2 * BLK, 1), 0) < BLK
        dk_ref[...] = jnp.zeros_like(dk_ref)
        dv_ref[...] = jnp.zeros_like(dv_ref)
        for sb in range(nsb + 1):
            gb = i * nsb + sb
            rows = slice(sb * BLK, (sb + 1) * BLK)
            before = slice((sb - 1) * BLK, sb * BLK)
            inside = (gb < n_blocks).astype(jnp.int32)
            has_prev = jnp.logical_and(gb % blocks_per_stream != 0, gb < n_blocks).astype(jnp.int32)
            mask = jnp.logical_and(band, (own * inside + has_prev) > 0)
            for pr in range(GROUP_W // (2 * HD)):
                cols = slice(pr * 2 * HD, (pr + 1) * 2 * HD)
                if sb == 0:
                    kcat = jnp.concatenate([kp_ref[:, cols], k_ref[rows, cols]], axis=0)
                    vcat = jnp.concatenate([vp_ref[:, cols], v_ref[rows, cols]], axis=0)
                elif sb == nsb:
                    kcat = jnp.concatenate([k_ref[before, cols], kn_ref[:, cols]], axis=0)
                    vcat = jnp.concatenate([v_ref[before, cols], vn_ref[:, cols]], axis=0)
                else:
                    both = slice((sb - 1) * BLK, (sb + 1) * BLK)
                    kcat, vcat = k_ref[both, cols], v_ref[both, cols]
                if sb < nsb:
                    qp, dop, cp, lp = q_ref[rows, cols], do_ref[rows, cols], c_ref[rows, cols], l_ref[rows, cols]
                else:
                    qp, dop, cp, lp = qn_ref[:, cols], don_ref[:, cols], cn_ref[:, cols], ln_ref[:, cols]
                h0 = pattern * HEADS_PER_PATTERN + 2 * pr
                slope = jnp.where(upper, _alibi_slope(h0), _alibi_slope(h0 + 1))
                q2 = _stack_heads(qp, first)
                do2 = _stack_heads(dop, first)
                sc = jnp.where(mask, _dot_nt(q2, kcat) * 0.125 - slope * rel_f, NEG)
                pm = jnp.exp(sc - _head_column(lp))
                dl = (pm * (_dot_nt(do2, vcat) - _head_column(cp))).astype(BF16)
                if sb < nsb:
                    dq_ref[rows, cols] = _unstack_heads(_dot(dl, kcat), first) * 0.125
                dk2 = _dot_tn(dl, q2) * 0.125
                dv2 = _dot_tn(pm.astype(BF16), do2)
                if sb > 0:
                    dk_ref[before, cols] += dk2[:BLK]
                    dv_ref[before, cols] += dv2[:BLK]
                if sb < nsb:
                    dk_ref[rows, cols] += dk2[BLK:]
                    dv_ref[rows, cols] += dv2[BLK:]

    cur = pl.BlockSpec((tq, GROUP_W), lambda i: (i, 0))
    prev = pl.BlockSpec((BLK, GROUP_W), lambda i: (jnp.maximum(i * nsb - 1, 0), 0))
    nxt = pl.BlockSpec((BLK, GROUP_W), lambda i: (jnp.minimum((i + 1) * nsb, n_blocks - 1), 0))
    shp = _sds((s, GROUP_W), F32)
    return _call(body, name=name, out_shape=[shp, shp, shp], grid=(s // tq,),
                 in_specs=[cur] * 6 + [prev, prev] + [nxt] * 6, out_specs=[cur, cur, cur])(
                     q, k, v, d_o, cterm, lse, k, v, q, k, v, d_o, cterm, lse)


HALO = 16
CONV_TQ = 512


def conv_fwd(p, w, b, name):
    s = p.shape[0]
    tq = CONV_TQ
    ncol = SSD_CONV_DIM // GROUP_W
    cb0 = COL_XBC // GROUP_W

    def body(u_ref, up_ref, w_ref, b_ref, c_ref, xc_ref):
        i = pl.program_id(0)
        prev = up_ref[...].astype(F32) * (i > 0).astype(F32)
        ext = jnp.concatenate([prev, u_ref[...].astype(F32)], axis=0)
        acc = b_ref[...] + w_ref[SSD_CONV - 1:SSD_CONV, :] * ext[HALO:HALO + tq]
        for kk in range(SSD_CONV - 1):
            off = HALO - (SSD_CONV - 1) + kk
            acc += w_ref[kk:kk + 1, :] * ext[off:off + tq]
        c_ref[...] = acc.astype(BF16)
        xc_ref[...] = (acc * _sigmoid(acc)).astype(BF16)

    cur_in = pl.BlockSpec((tq, GROUP_W), lambda i, j: (i, cb0 + j))
    prev_in = pl.BlockSpec((HALO, GROUP_W), lambda i, j: (jnp.maximum(i * (tq // HALO) - 1, 0), cb0 + j))
    cur_out = pl.BlockSpec((tq, GROUP_W), lambda i, j: (i, j))
    shp = _sds((s, SSD_CONV_DIM), BF16)
    return _call(body, name=name, out_shape=[shp, shp], grid=(s // tq, ncol),
                 in_specs=[cur_in, prev_in, pl.BlockSpec((SSD_CONV, GROUP_W), lambda i, j: (0, j)),
                           pl.BlockSpec((1, GROUP_W), lambda i, j: (0, j))],
                 out_specs=[cur_out, cur_out])(p, p, w, b)


def conv_bwd(p, cpre, dxc, w, name):
    s = p.shape[0]
    tq = CONV_TQ
    ncol = SSD_CONV_DIM // GROUP_W
    cb0 = COL_XBC // GROUP_W
    nt = s // tq

    def body(u_ref, up_ref, c_ref, cn_ref, d_ref, dn_ref, w_ref, du_ref, dw_ref, db_ref):
        i = pl.program_id(1)

        def dpre(c16, dx):
            c = c16.astype(F32)
            sg = _sigmoid(c)
            return dx * (sg * (1.0 + c * (1.0 - sg)))

        dc = dpre(c_ref[...], d_ref[...])
        dcn = dpre(cn_ref[...], dn_ref[...]) * (i < nt - 1).astype(F32)
        dext = jnp.concatenate([dc, dcn], axis=0)
        prev = up_ref[...].astype(F32) * (i > 0).astype(F32)
        uext = jnp.concatenate([prev, u_ref[...].astype(F32)], axis=0)

        @pl.when(i == 0)
        def _():
            dw_ref[...] = jnp.zeros_like(dw_ref)
            db_ref[...] = jnp.zeros_like(db_ref)

        du = w_ref[SSD_CONV - 1:SSD_CONV, :] * dc
        for kk in range(SSD_CONV - 1):
            sh = SSD_CONV - 1 - kk
            du += w_ref[kk:kk + 1, :] * dext[sh:sh + tq]
        du_ref[...] = du.astype(BF16)
        for kk in range(SSD_CONV):
            off = HALO - (SSD_CONV - 1) + kk
            dw_ref[kk:kk + 1, :] += jnp.sum(dc * uext[off:off + tq], axis=0, keepdims=True)
        db_ref[...] += jnp.sum(dc, axis=0, keepdims=True)

    hb = tq // HALO
    cur_p = pl.BlockSpec((tq, GROUP_W), lambda j, i: (i, cb0 + j))
    prev_p = pl.BlockSpec((HALO, GROUP_W), lambda j, i: (jnp.maximum(i * hb - 1, 0), cb0 + j))
    cur = pl.BlockSpec((tq, GROUP_W), lambda j, i: (i, j))
    nxt = pl.BlockSpec((HALO, GROUP_W), lambda j, i: (jnp.minimum((i + 1) * hb, s // HALO - 1), j))
    return _call(body, name=name,
                 out_shape=[_sds((s, SSD_CONV_DIM), BF16), _sds((8, SSD_CONV_DIM), F32), _sds((1, SSD_CONV_DIM), F32)],
                 grid=(ncol, nt),
                 in_specs=[cur_p, prev_p, cur, nxt, cur, nxt, pl.BlockSpec((SSD_CONV, GROUP_W), lambda j, i: (0, j))],
                 out_specs=[cur, pl.BlockSpec((8, GROUP_W), lambda j, i: (0, j)),
                            pl.BlockSpec((1, GROUP_W), lambda j, i: (0, j))])(p, p, cpre, cpre, dxc, dxc, w)


def _softplus(x):
    return jnp.maximum(x, 0.0) + jnp.log(1.0 + jnp.exp(-jnp.abs(x)))


def _ssd_decays(dtr_ref, dtrt_ref, bias_ref, biast_ref, alog_ref, alogt_ref):
    row = lax.broadcasted_iota(jnp.int32, (BLK, BLK), 0)
    col = lax.broadcasted_iota(jnp.int32, (BLK, BLK), 1)
    lower = (row >= col).astype(F32)
    upper = (row <= col).astype(F32)
    dtb = dtr_ref[...] + bias_ref[...]
    dt = _softplus(dtb)
    a = dt * (-jnp.exp(alog_ref[...]))
    cs = _dot_hi(lower, a)
    a_t = _softplus(dtrt_ref[...] + biast_ref[...]) * (-jnp.exp(alogt_ref[...]))
    cs_t = _dot_hi(a_t, upper)
    return dtb, dt, cs, cs_t, row, col, upper


def ssd_fwd(p, xc, dtg, dtg_t, params, gn, name):
    s = p.shape[0]
    nc = s // BLK
    bias, bias_t, alog, alog_t, dskip = params

    def body(xs_ref, b_ref, c_ref, z_ref, dtr_ref, dtrt_ref, bias_ref, biast_ref, alog_ref, alogt_ref, dsk_ref,
             gn_ref, y_ref, sin_ref, hp_ref, h_ref):
        c_idx = pl.program_id(1)

        @pl.when(c_idx == 0)
        def _():
            h_ref[...] = jnp.zeros_like(h_ref)

        _, dt, cs, cs_t, row, col, _ = _ssd_decays(dtr_ref, dtrt_ref, bias_ref, biast_ref, alog_ref, alogt_ref)
        first = _lane_first_half((BLK, 2 * HD))
        first_row = _lane_first_half((1, 2 * HD))
        tril = row >= col
        b16, c16 = b_ref[...], c_ref[...]
        cb = _dot_nt(c16, b16)
        ys = []
        for pr in range(GROUP_W // (2 * HD)):
            cols = slice(pr * 2 * HD, (pr + 1) * 2 * HD)
            ha, hb = 2 * pr, 2 * pr + 1
            xs = xs_ref[:, cols].astype(F32)
            dt_pair = jnp.where(first, dt[:, ha:ha + 1], dt[:, hb:hb + 1])
            xt = xs * dt_pair
            xt16 = xt.astype(BF16)
            y_heads = []
            for h in (ha, hb):
                decay = jnp.exp(jnp.where(tril, cs[:, h:h + 1] - cs_t[h:h + 1, :], NEG))
                y_heads.append(_dot((cb * decay).astype(BF16), xt16))
            y_diag = jnp.where(first, y_heads[0], y_heads[1])
            hstate = h_ref[pr]
            hp_ref[pr] = hstate
            e_pair = jnp.where(first, jnp.exp(cs[:, ha:ha + 1]), jnp.exp(cs[:, hb:hb + 1]))
            y_off = e_pair * _dot(c16, hstate.astype(BF16))
            tot_a, tot_b = cs[BLK - 1:BLK, ha:ha + 1], cs[BLK - 1:BLK, hb:hb + 1]
            f_pair = jnp.where(first, jnp.exp(tot_a - cs[:, ha:ha + 1]), jnp.exp(tot_b - cs[:, hb:hb + 1]))
            new = _dot_tn(b16, (f_pair * xt).astype(BF16))
            dec = jnp.where(first_row, jnp.exp(tot_a), jnp.exp(tot_b))
            h_ref[pr] = dec * hstate + new
            d_pair = jnp.where(first_row, dsk_ref[:, ha:ha + 1], dsk_ref[:, hb:hb + 1])
            ys.append(y_diag + y_off + xs * d_pair)
        y = jnp.concatenate(ys, axis=1)
        y_ref[...] = y
        zv = z_ref[...].astype(F32)
        yz = y * (zv * _sigmoid(zv))
        r = lax.rsqrt(jnp.mean(yz * yz, axis=-1, keepdims=True) + EPS)
        sin_ref[...] = (yz * r * gn_ref[...]).astype(BF16)

    nb0 = SSD_INNER // BLK
    gparam = pl.BlockSpec((None, 1, 8), lambda g, c: (g, 0, 0))
    gparam_t = pl.BlockSpec((None, 8, 1), lambda g, c: (g, 0, 0))
    return _call(
        body, name=name,
        out_shape=[_sds((s, SSD_INNER), F32), _sds((s, SSD_INNER), BF16),
                   _sds((SSD_GROUPS, nc, 4, BLK, 2 * HD), F32)],
        grid=(SSD_GROUPS, nc),
        in_specs=[pl.BlockSpec((BLK, GROUP_W), lambda g, c: (c, g)),
                  pl.BlockSpec((BLK, BLK), lambda g, c: (c, nb0 + g)),
                  pl.BlockSpec((BLK, BLK), lambda g, c: (c, nb0 + SSD_GROUPS + g)),
                  pl.BlockSpec((BLK, GROUP_W), lambda g, c: (c, COL_Z // GROUP_W + g)),
                  pl.BlockSpec((None, BLK, 8), lambda g, c: (g, c, 0)),
                  pl.BlockSpec((None, 8, BLK), lambda g, c: (g, 0, c)),
                  gparam, gparam_t, gparam, gparam_t, gparam,
                  pl.BlockSpec((1, GROUP_W), lambda g, c: (0, g))],
        out_specs=[pl.BlockSpec((BLK, GROUP_W), lambda g, c: (c, g)),
                   pl.BlockSpec((BLK, GROUP_W), lambda g, c: (c, g)),
                   pl.BlockSpec((None, None, 4, BLK, 2 * HD), lambda g, c: (g, c, 0, 0, 0))],
        scratch_shapes=[pltpu.VMEM((4, BLK, 2 * HD), F32)],
    )(xc, xc, xc, p, dtg, dtg_t, bias, bias_t, alog, alog_t, dskip, gn)


def ssd_bwd(p, xc, bc_t, y, d_sin, hprev, dtg, dtg_t, params, gn, name):
    s = p.shape[0]
    nc = s // BLK
    bias, bias_t, alog, alog_t, dskip = params

    def body(xs_ref, b_ref, c_ref, bt_ref, ct_ref, z_ref, y_ref, dsin_ref, hp_ref, dtr_ref, dtrt_ref, bias_ref,
             biast_ref, alog_ref, alogt_ref, dsk_ref, gn_ref,
             dxs_ref, db_ref, dc_ref, dz_ref, ddt_ref, da_ref, dbias_ref, ddsk_ref, dgn_ref, dh_ref):
        c_idx = pl.program_id(1)

        @pl.when(c_idx == 0)
        def _():
            dh_ref[...] = jnp.zeros_like(dh_ref)
            da_ref[...] = jnp.zeros_like(da_ref)
            dbias_ref[...] = jnp.zeros_like(dbias_ref)
            ddsk_ref[...] = jnp.zeros_like(ddsk_ref)
            dgn_ref[...] = jnp.zeros_like(dgn_ref)

        dtb, dt, cs, cs_t, row, col, upper = _ssd_decays(dtr_ref, dtrt_ref, bias_ref, biast_ref, alog_ref, alogt_ref)
        first = _lane_first_half((BLK, 2 * HD))
        second = jnp.logical_not(first)
        first_row = _lane_first_half((1, 2 * HD))
        tril = row >= col
        triu = row <= col
        last_row = lax.broadcasted_iota(jnp.int32, (BLK, 1), 0) == BLK - 1
        lane8 = lax.broadcasted_iota(jnp.int32, (BLK, 8), 1)

        yv = y_ref[...]
        zv = z_ref[...].astype(F32)
        sg = _sigmoid(zv)
        yz = yv * (zv * sg)
        r = lax.rsqrt(jnp.mean(yz * yz, axis=-1, keepdims=True) + EPS)
        yzn = yz * r
        dsn = dsin_ref[...]
        dgn_ref[...] += jnp.sum(dsn * yzn, axis=0, keepdims=True)
        dsn = dsn * gn_ref[...]
        dyz = r * (dsn - yzn * jnp.mean(dsn * yzn, axis=-1, keepdims=True))
        dy = dyz * (zv * sg)
        dz_ref[...] = (dyz * yv * (sg * (1.0 + zv * (1.0 - sg)))).astype(BF16)
        xs_all = xs_ref[...].astype(F32)
        ddsk_ref[...] += jnp.sum(dy * xs_all, axis=0, keepdims=True)

        b16, c16, bt16, ct16 = b_ref[...], c_ref[...], bt_ref[...], ct_ref[...]
        cb = _dot_nt(c16, b16)
        cb_t = _dot_nt(b16, c16)
        g_sum = jnp.zeros((BLK, BLK), F32)
        gt_sum = jnp.zeros((BLK, BLK), F32)
        dc_acc = jnp.zeros((BLK, BLK), F32)
        db_acc = jnp.zeros((BLK, BLK), F32)
        dcs_all = jnp.zeros((BLK, 8), F32)
        ddtx_all = jnp.zeros((BLK, 8), F32)
        for pr in range(GROUP_W // (2 * HD)):
            cols = slice(pr * 2 * HD, (pr + 1) * 2 * HD)
            heads = (2 * pr, 2 * pr + 1)
            xs = xs_all[:, cols]
            dy_p = dy[:, cols]
            dt_pair = jnp.where(first, dt[:, heads[0]:heads[0] + 1], dt[:, heads[1]:heads[1] + 1])
            xt = xs * dt_pair
            xt16 = xt.astype(BF16)
            hstate = hp_ref[pr]
            h16 = hstate.astype(BF16)
            dhn = dh_ref[pr]
            dhn16 = dhn.astype(BF16)
            d_xt = jnp.zeros((BLK, 2 * HD), F32)
            dcs = []
            for e, h in enumerate(heads):
                lanes = first if e == 0 else second
                cs_c, cs_r = cs[:, h:h + 1], cs_t[h:h + 1, :]
                decay = jnp.exp(jnp.where(tril, cs_c - cs_r, NEG))
                decay_t = jnp.exp(jnp.where(triu, cs_r - cs_c, NEG))
                dym16 = jnp.where(lanes, dy_p, 0.0).astype(BF16)
                d_m = _dot_nt(dym16, xt16)
                d_mt = _dot_nt(xt16, dym16)
                d_xt += _dot((cb_t * decay_t).astype(BF16), dym16)
                gm = d_m * decay
                gmt = d_mt * decay_t
                g_sum += gm
                gt_sum += gmt
                dcs.append(jnp.sum(gm * cb, axis=-1, keepdims=True) - jnp.sum(gmt * cb_t, axis=-1, keepdims=True))
            exp_cs = [jnp.exp(cs[:, h:h + 1]) for h in heads]
            tots = [cs[BLK - 1:BLK, h:h + 1] for h in heads]
            f_col = [jnp.exp(tots[e] - cs[:, h:h + 1]) for e, h in enumerate(heads)]
            e_pair = jnp.where(first, exp_cs[0], exp_cs[1])
            f_pair = jnp.where(first, f_col[0], f_col[1])
            dec = [jnp.exp(t) for t in tots]
            dec_pair = jnp.where(first_row, dec[0], dec[1])
            edy = e_pair * dy_p
            edy16 = edy.astype(BF16)
            y_off = e_pair * _dot(c16, h16)
            oa, ob = _pair_sum(dy_p * y_off, first)
            dc_acc += _dot_nt(edy16, h16)
            dh_prev = _dot(ct16, edy16)
            zmat = _dot(b16, dhn16)
            d_xt += f_pair * zmat
            fa, fb = _pair_sum(zmat * xt, first)
            ta, tb = fa * f_col[0], fb * f_col[1]
            hh = dhn * hstate
            ha_sum = jnp.sum(jnp.sum(jnp.where(first, hh, 0.0), axis=-1, keepdims=True), axis=0, keepdims=True)
            hb_sum = jnp.sum(jnp.sum(hh, axis=-1, keepdims=True), axis=0, keepdims=True) - ha_sum
            dtot_a = jnp.sum(ta, axis=0, keepdims=True) + ha_sum * dec[0]
            dtot_b = jnp.sum(tb, axis=0, keepdims=True) + hb_sum * dec[1]
            dcs[0] = dcs[0] + oa - ta + jnp.where(last_row, dtot_a, 0.0)
            dcs[1] = dcs[1] + ob - tb + jnp.where(last_row, dtot_b, 0.0)
            db_acc += _dot_nt((f_pair * xt).astype(BF16), dhn16)
            dh_ref[pr] = dh_prev + dec_pair * dhn
            d_pair = jnp.where(first_row, dsk_ref[:, heads[0]:heads[0] + 1], dsk_ref[:, heads[1]:heads[1] + 1])
            dxs_ref[:, cols] = dy_p * d_pair + d_xt * dt_pair
            xa, xb = _pair_sum(d_xt * xs, first)
            for e, h in enumerate(heads):
                dcs_all = jnp.where(lane8 == h, dcs[e], dcs_all)
                ddtx_all = jnp.where(lane8 == h, (xa, xb)[e], ddtx_all)

        dc_ref[...] = dc_acc + _dot(g_sum.astype(BF16), b16)
        db_ref[...] = db_acc + _dot(gt_sum.astype(BF16), c16)
        d_a = _dot_hi(upper, dcs_all)
        a_neg = -jnp.exp(alog_ref[...])
        ddt = ddtx_all + d_a * a_neg
        da_ref[...] += jnp.sum(d_a * dt, axis=0, keepdims=True)
        ddtr = ddt * _sigmoid(dtb)
        ddt_ref[...] = ddtr
        dbias_ref[...] += jnp.sum(ddtr, axis=0, keepdims=True)

    nb0 = SSD_INNER // BLK
    rc = lambda c: nc - 1 - c
    gparam = pl.BlockSpec((None, 1, 8), lambda g, c: (g, 0, 0))
    gparam_t = pl.BlockSpec((None, 8, 1), lambda g, c: (g, 0, 0))
    wide = pl.BlockSpec((BLK, GROUP_W), lambda g, c: (rc(c), g))
    narrow = pl.BlockSpec((BLK, BLK), lambda g, c: (rc(c), g))
    return _call(
        body, name=name,
        out_shape=[_sds((s, SSD_INNER), F32), _sds((s, GROUP_W), F32), _sds((s, GROUP_W), F32),
                   _sds((s, SSD_INNER), BF16), _sds((SSD_GROUPS, s, 8), F32),
                   _sds((SSD_GROUPS, 1, 8), F32), _sds((SSD_GROUPS, 1, 8), F32),
                   _sds((SSD_GROUPS, 1, GROUP_W), F32), _sds((1, SSD_INNER), F32)],
        grid=(SSD_GROUPS, nc),
        in_specs=[wide,
                  pl.BlockSpec((BLK, BLK), lambda g, c: (rc(c), nb0 + g)),
                  pl.BlockSpec((BLK, BLK), lambda g, c: (rc(c), nb0 + SSD_GROUPS + g)),
                  pl.BlockSpec((BLK, BLK), lambda g, c: (g, rc(c))),
                  pl.BlockSpec((BLK, BLK), lambda g, c: (SSD_GROUPS + g, rc(c))),
                  pl.BlockSpec((BLK, GROUP_W), lambda g, c: (rc(c), COL_Z // GROUP_W + g)),
                  wide, wide,
                  pl.BlockSpec((None, None, 4, BLK, 2 * HD), lambda g, c: (g, rc(c), 0, 0, 0)),
                  pl.BlockSpec((None, BLK, 8), lambda g, c: (g, rc(c), 0)),
                  pl.BlockSpec((None, 8, BLK), lambda g, c: (g, 0, rc(c))),
                  gparam, gparam_t, gparam, gparam_t, gparam,
                  pl.BlockSpec((1, GROUP_W), lambda g, c: (0, g))],
        out_specs=[wide, narrow, narrow, wide,
                   pl.BlockSpec((None, BLK, 8), lambda g, c: (g, rc(c), 0)),
                   gparam, gparam,
                   pl.BlockSpec((None, 1, GROUP_W), lambda g, c: (g, 0, 0)),
                   pl.BlockSpec((1, GROUP_W), lambda g, c: (0, g))],
        scratch_shapes=[pltpu.VMEM((4, BLK, 2 * HD), F32)],
    )(xc, xc, xc, bc_t, bc_t, p, y, d_sin, hprev, dtg, dtg_t, bias, bias_t, alog, alog_t, dskip, gn)


def merge_fwd(p, a, sbr, name, tm=512):
    s = p.shape[0]
    nj = D_MODEL // GROUP_W

    def body(ga_ref, gs_ref, a_ref, s_ref, o_ref):
        o_ref[...] = (_sigmoid(ga_ref[...].astype(F32)) * a_ref[...]
                      + _sigmoid(gs_ref[...].astype(F32)) * s_ref[...]).astype(BF16)

    blk = pl.BlockSpec((tm, GROUP_W), lambda i, j: (i, j))
    return _call(body, name=name, out_shape=_sds((s, D_MODEL), BF16), grid=(s // tm, nj),
                 in_specs=[pl.BlockSpec((tm, GROUP_W), lambda i, j: (i, COL_GA // GROUP_W + j)),
                           pl.BlockSpec((tm, GROUP_W), lambda i, j: (i, COL_GS // GROUP_W + j)), blk, blk],
                 out_specs=blk)(p, p, a, sbr)


def merge_bwd(p, a, sbr, dmerged, name, tm=512):
    s = p.shape[0]
    nj = D_MODEL // GROUP_W

    def body(ga_ref, gs_ref, a_ref, s_ref, dm_ref, da_ref, ds_ref, dga_ref, dgs_ref):
        dm = dm_ref[...]
        sa = _sigmoid(ga_ref[...].astype(F32))
        ss = _sigmoid(gs_ref[...].astype(F32))
        da_ref[...] = (dm * sa).astype(BF16)
        ds_ref[...] = (dm * ss).astype(BF16)
        dga_ref[...] = (dm * a_ref[...] * sa * (1.0 - sa)).astype(BF16)
        dgs_ref[...] = (dm * s_ref[...] * ss * (1.0 - ss)).astype(BF16)

    blk = pl.BlockSpec((tm, GROUP_W), lambda i, j: (i, j))
    shp = _sds((s, D_MODEL), BF16)
    return _call(body, name=name, out_shape=[shp] * 4, grid=(s // tm, nj),
                 in_specs=[pl.BlockSpec((tm, GROUP_W), lambda i, j: (i, COL_GA // GROUP_W + j)),
                           pl.BlockSpec((tm, GROUP_W), lambda i, j: (i, COL_GS // GROUP_W + j)), blk, blk, blk],
                 out_specs=[blk] * 4)(p, p, a, sbr, dmerged)


def _group_major(v):
    return v.reshape(SSD_GROUPS, 1, 8), v.reshape(SSD_GROUPS, 8, 1)


def mixer_forward(x, w, rider=None, later_weights=None):
    s = x.shape[0]
    h = rms_fwd(x, w["mix_norm"], "mix_rms")
    p = matmul_nn(h, w["w_in_main"], "mix_proj", BF16, tm=1024, tn=512, rider=rider)
    rode = None
    if rider is not None:
        p, rode = p
        w = dict(w, **later_weights(rode))
    dt_raw = matmul_nn(h, w["w_in_dt"], "mix_proj_dt", F32, tm=1024, tn=DT_PAD)
    qn, kn = qk_norm_fwd(p, w["q_gain"], w["k_gain"], "qk_norm")
    streams, os_, lses = [], [], []
    for g, d in enumerate(ATTN_DILATIONS):
        cols = slice(g * GROUP_W, (g + 1) * GROUP_W)
        qs, ks = _to_streams(qn[:, cols], d), _to_streams(kn[:, cols], d)
        vs = _to_streams(p[:, COL_V + g * GROUP_W:COL_V + (g + 1) * GROUP_W], d)
        o, lse = attn_fwd2(qs, ks, vs, g, f"attn_fwd{g}")
        streams.append((qs, ks, vs, lse))
        os_.append(_from_streams(o, d))
        lses.append(_from_streams(lse, d))
    attn_o = attn_merge_fwd(os_, lses, "attn_merge")
    cpre, xc = conv_fwd(p, w["conv_w"], w["conv_b"], "conv_fwd")
    dtg = dt_raw[:, :SSD_HEADS].reshape(s, SSD_GROUPS, 8).transpose(1, 0, 2)
    dtg_t = dtg.transpose(0, 2, 1)
    params = (*_group_major(w["dt_bias"]), *_group_major(w["a_log"]), _group_major(w["d_skip"])[0])
    y, s_in, hprev = ssd_fwd(p, xc, dtg, dtg_t, params, w["ssd_norm"], "ssd_fwd")
    a = matmul_nn(attn_o, w["w_attn_branch"], "attn_branch", F32, tm=1024, tn=512)
    sbr = matmul_nn(s_in, w["w_ssd_branch"], "ssd_branch", F32, tm=1024, tn=512)
    merged = merge_fwd(p, a, sbr, "merge")
    x_out = matmul_nn(merged, w["w_out"], "mix_out", F32, tm=1024, tn=512, res=x)
    saved = dict(h=h, p=p, streams=streams, os=os_, lses=lses, attn_o=attn_o, cpre=cpre, xc=xc, dtg=dtg,
                 dtg_t=dtg_t, params=params, y=y, s_in=s_in, hprev=hprev, a=a, sbr=sbr, merged=merged, w=w)
    return x_out, saved, rode


def mixer_backward(dx_out, x, sv, ride_early=None, ride_late=None):
    s = x.shape[0]
    p = sv["p"]
    w = sv["w"]
    g = {}
    dmerged = matmul_nt(dx_out, w["w_out"], "d_merged", F32, tm=1024, tn=512, tk=1024)
    g["w_out"] = matmul_tn(sv["merged"], dx_out, "dw_out", tn=512, ts=1024)
    da, ds, dga, dgs = merge_bwd(p, sv["a"], sv["sbr"], dmerged, "merge_bwd")
    g["w_attn_branch"] = matmul_tn(sv["attn_o"], da, "dw_attn_branch", tn=512, ts=1024)
    g["w_ssd_branch"] = matmul_tn(sv["s_in"], ds, "dw_ssd_branch", tn=512, ts=1024)
    d_attn_o = matmul_nt(da, w["w_attn_branch"], "d_attn_o", F32, tm=1024, tn=512, tk=1024)
    d_sin = matmul_nt(ds, w["w_ssd_branch"], "d_ssd_in", F32, tm=1024, tn=512, tk=1024)
    bc_t = sv["xc"][:, SSD_INNER:].T
    dxs, d_b, d_c, dz, ddt, d_asum, d_bias, d_dsk, d_gn = ssd_bwd(
        p, sv["xc"], bc_t, sv["y"], d_sin, sv["hprev"], sv["dtg"], sv["dtg_t"], sv["params"], w["ssd_norm"], "ssd_bwd")
    dxc = jnp.concatenate([dxs, d_b, d_c], axis=1)
    dxbc, d_convw, d_convb = conv_bwd(p, sv["cpre"], dxc, w["conv_w"], "conv_bwd")
    g["conv_w"] = d_convw[:SSD_CONV]
    g["conv_b"] = d_convb
    g["dt_bias"] = d_bias.reshape(1, SSD_HEADS)
    g["a_log"] = (d_asum * (-jnp.exp(sv["params"][2]))).reshape(1, SSD_HEADS)
    g["d_skip"] = jnp.sum(d_dsk.reshape(SSD_HEADS, HD), axis=1).reshape(1, SSD_HEADS)
    g["ssd_norm"] = d_gn
    merged_bwd = attn_merge_bwd(d_attn_o, sv["os"], sv["lses"], "attn_merge_bwd")
    dqs, dks, dvs = [], [], []
    for gi, d in enumerate(ATTN_DILATIONS):
        qs, ks, vs, lse = sv["streams"][gi]
        d_o = _to_streams(merged_bwd[gi], d)
        cterm = _to_streams(merged_bwd[3 + gi], d)
        dq, dk, dv = attn_bwd2(qs, ks, vs, d_o, cterm, lse, gi, f"attn_bwd{gi}")
        dqs.append(_from_streams(dq, d))
        dks.append(_from_streams(dk, d))
        dvs.append(_from_streams(dv, d).astype(BF16))
    dqn = jnp.concatenate(dqs, axis=1)
    dkn = jnp.concatenate(dks, axis=1)
    dq, dk, d_qg, d_kg = qk_norm_bwd(p, dqn, dkn, w["q_gain"], w["k_gain"], "qk_norm_bwd")
    g["q_norm"] = jnp.sum(d_qg.reshape(N_ATTN_HEADS, HD), axis=0).reshape(1, HD)
    g["k_norm"] = jnp.sum(d_kg.reshape(N_ATTN_HEADS, HD), axis=0).reshape(1, HD)
    dp = jnp.concatenate([dq, dk] + dvs + [dz, dxbc, dga, dgs], axis=1)
    ddt_pad = jnp.pad(ddt.transpose(1, 0, 2).reshape(s, SSD_HEADS), ((0, 0), (0, DT_PAD - SSD_HEADS)))
    if ride_early is not None:
        g["w_in_main"], g["rode_early"] = matmul_tn(sv["h"], dp, "dw_in", tn=512, ts=1024, rider=ride_early(g))
    else:
        g["w_in_main"] = matmul_tn(sv["h"], dp, "dw_in", tn=512, ts=1024)
    g["w_in_dt"] = matmul_tn(sv["h"], ddt_pad, "dw_in_dt", tn=DT_PAD, ts=1024)
    if ride_late is not None:
        dh_main, g["rode_late"] = matmul_nt(dp, w["w_in_main"], "d_mix_h", F32, tm=1024, tn=512, tk=512,
                                            rider=ride_late(g))
    else:
        dh_main = matmul_nt(dp, w["w_in_main"], "d_mix_h", F32, tm=1024, tn=512, tk=512)
    dh_dt = matmul_nt(ddt_pad, w["w_in_dt"], "d_mix_h_dt", F32, tm=1024, tn=1024, tk=DT_PAD)
    dx, g["mix_norm"] = rms_bwd([dh_main, dh_dt], x, w["mix_norm"], dx_out, "mix_drms")
    return dx, g


ANY = pl.BlockSpec(memory_space=pl.ANY)


def _place():
    x, y, c = lax.axis_index("x"), lax.axis_index("y"), lax.axis_index("c")
    chips = [(1 - x, y), (x, 1 - y), (1 - x, 1 - y)]
    return x, y, c, 2 * x + y, chips


def _comm_call(body, *, name, out_shape, n_in, scratch_shapes, aliases=None):
    return pl.pallas_call(
        body, out_shape=out_shape, in_specs=[ANY] * n_in, out_specs=[ANY] * len(out_shape),
        scratch_shapes=scratch_shapes, input_output_aliases=aliases or {}, name=name,
        compiler_params=pltpu.CompilerParams(has_side_effects=True))


def gather_weights(shards, small):
    n = len(shards)
    halves = [a.shape[0] // 2 for a in shards]
    out_shape = [_sds((N_CHIP,) + a.shape, a.dtype) for a in shards] + [_sds((N_CHIP,) + small.shape, small.dtype)]

    def body(*refs):
        ins, outs = refs[:n + 1], refs[n + 1:2 * n + 2]
        send1, recv1, send2, recv2, local = refs[2 * n + 2:]
        x, y, c, me, chips = _place()
        sibling = (x, y, 1 - c)

        def rows(k, chip, core):
            if k == n:
                return outs[k].at[chip]
            return outs[k].at[chip, pl.ds(core * halves[k], halves[k])]

        def level1(k, t, incoming):
            chip = 2 * chips[t][0] + chips[t][1]
            src = ins[k] if k == n else ins[k].at[pl.ds(c * halves[k], halves[k])]
            return pltpu.make_async_remote_copy(
                src_ref=src, dst_ref=rows(k, chip if incoming else me, c), send_sem=send1.at[3 * k + t],
                recv_sem=recv1.at[3 * k + t], device_id=(*chips[t], c), device_id_type=MESH)

        def level2(k, t, incoming):
            chip = 2 * chips[t][0] + chips[t][1]
            core = (1 - c) if incoming else c
            return pltpu.make_async_remote_copy(
                src_ref=rows(k, chip, core), dst_ref=rows(k, chip, core), send_sem=send2.at[3 * k + t],
                recv_sem=recv2.at[3 * k + t], device_id=sibling, device_id_type=MESH)

        own = [pltpu.make_async_copy(ins[k], outs[k].at[me], local.at[k]) for k in range(n + 1)]
        for cp in own:
            cp.start()
        first = [level1(k, t, False) for k in range(n + 1) for t in range(3)]
        for cp in first:
            cp.start()
        passed = []
        for k in range(n + 1):
            for t in range(3):
                level1(k, t, True).wait_recv()
                if k < n:
                    cp = level2(k, t, False)
                    cp.start()
                    passed.append(cp)
        for k in range(n):
            for t in range(3):
                level2(k, t, True).wait_recv()
        for cp in first + passed:
            cp.wait_send()
        for cp in own:
            cp.wait()

    dma = pltpu.SemaphoreType.DMA
    return _comm_call(body, name="gather_weights", out_shape=out_shape, n_in=n + 1,
                      scratch_shapes=[dma((3 * n + 3,)), dma((3 * n + 3,)), dma((3 * n,)), dma((3 * n,)),
                                      dma((n + 1,))])(*shards, small)


def reduce_to_sibling(grads):
    n = len(grads)
    halves = [a.shape[1] // 2 for a in grads]
    shapes = [_sds((N_CHIP, h, a.shape[2]), a.dtype) for a, h in zip(grads, halves)]

    def body(*refs):
        ins, got, kept = refs[:n], refs[n:2 * n], refs[2 * n:3 * n]
        send, recv, local = refs[3 * n:]
        x, y, c, _, _ = _place()
        copies, locals_ = [], []
        for k in range(n):
            h = halves[k]
            locals_.append(pltpu.make_async_copy(ins[k].at[:, pl.ds(c * h, h)], kept[k], local.at[k]))
            copies.append(pltpu.make_async_remote_copy(
                src_ref=ins[k].at[:, pl.ds((1 - c) * h, h)], dst_ref=got[k], send_sem=send.at[k], recv_sem=recv.at[k],
                device_id=(x, y, 1 - c), device_id_type=MESH))
        for cp in locals_ + copies:
            cp.start()
        for cp in copies:
            cp.wait_recv()
        for cp in copies:
            cp.wait_send()
        for cp in locals_:
            cp.wait()

    dma = pltpu.SemaphoreType.DMA
    res = _comm_call(body, name="reduce_to_sibling", out_shape=shapes + shapes, n_in=n,
                     scratch_shapes=[dma((n,)), dma((n,)), dma((n,))])(*grads)
    return res[:n], res[n:]


def reduce_to_owner(sums):
    n = len(sums)
    shapes = [_sds(a.shape, a.dtype) for a in sums]

    def body(*refs):
        ins, outs = refs[:n], refs[n:2 * n]
        send, recv, local = refs[2 * n:]
        x, y, c, me, chips = _place()
        copies, locals_ = [], []
        for k in range(n):
            locals_.append(pltpu.make_async_copy(ins[k].at[me], outs[k].at[3], local.at[k]))
            for t in range(3):
                chip = 2 * chips[t][0] + chips[t][1]
                copies.append(pltpu.make_async_remote_copy(
                    src_ref=ins[k].at[chip], dst_ref=outs[k].at[t], send_sem=send.at[3 * k + t],
                    recv_sem=recv.at[3 * k + t], device_id=(*chips[t], c), device_id_type=MESH))
        for cp in locals_ + copies:
            cp.start()
        for cp in copies:
            cp.wait_recv()
        for cp in copies:
            cp.wait_send()
        for cp in locals_:
            cp.wait()

    dma = pltpu.SemaphoreType.DMA
    return _comm_call(body, name="reduce_to_owner", out_shape=shapes, n_in=n,
                      scratch_shapes=[dma((3 * n,)), dma((3 * n,)), dma((n,))])(*sums)


def share_with_sibling(halves_):
    n = len(halves_)
    shapes = [_sds((2 * a.shape[0], a.shape[1]), a.dtype) for a in halves_]

    def body(*refs):
        ins, outs = refs[:n], refs[n:2 * n]
        send, recv, local = refs[2 * n:]
        x, y, c, _, _ = _place()
        copies, locals_ = [], []
        for k in range(n):
            h = ins[k].shape[0]
            mine = outs[k].at[pl.ds(c * h, h)]
            locals_.append(pltpu.make_async_copy(ins[k], mine, local.at[k]))
            copies.append(pltpu.make_async_remote_copy(
                src_ref=ins[k], dst_ref=mine, send_sem=send.at[k], recv_sem=recv.at[k],
                device_id=(x, y, 1 - c), device_id_type=MESH))
        for cp in locals_ + copies:
            cp.start()
        for cp in copies:
            cp.wait_recv()
        for cp in copies:
            cp.wait_send()
        for cp in locals_:
            cp.wait()

    dma = pltpu.SemaphoreType.DMA
    return _comm_call(body, name="share_with_sibling", out_shape=shapes, n_in=n,
                      scratch_shapes=[dma((n,)), dma((n,)), dma((n,))])(*halves_)


def _cores():
    c = lax.axis_index("c")
    return jnp.stack([c, 1 - c]).astype(jnp.int32)


def _staged_call(body, *, name, grid, in_specs, out_specs, out_shape, scratch_shapes):
    return pl.pallas_call(
        body, out_shape=out_shape, name=name,
        grid_spec=pltpu.PrefetchScalarGridSpec(num_scalar_prefetch=1, grid=grid, in_specs=in_specs,
                                               out_specs=out_specs, scratch_shapes=scratch_shapes),
        compiler_params=pltpu.CompilerParams(dimension_semantics=("arbitrary",) * len(grid),
                                             vmem_limit_bytes=V7X_VMEM_LIMIT, has_side_effects=True))


def gather_rider(shards, tiles):
    dma = pltpu.SemaphoreType.DMA
    n = len(shards)
    geo = [(a.shape[0] // 2, tm, (a.shape[0] // 2) // tm) for a, tm in zip(shards, tiles)]
    scratch = []
    for a, (h, tm, nk) in zip(shards, geo):
        scratch += [pltpu.VMEM((N_CHIP,) + a.shape, a.dtype), dma((3, nk)), dma((3, nk)), dma((3, nk)), dma((3, nk)),
                    dma((nk + 2,))]

    def copies(j, in_ref, scr):
        buf, send1, recv1, send2, recv2, local = scr[6 * j:6 * j + 6]
        h, tm, nk = geo[j]
        x, y, c, me, chips = _place()
        chip_of = [2 * chips[t][0] + chips[t][1] for t in range(3)]

        def rows(chip, core, k):
            return buf.at[chip, pl.ds(core * h + k * tm, tm)]

        def mine(k):
            if k == nk:
                return pltpu.make_async_copy(in_ref.at[pl.ds((1 - c) * h, h)], buf.at[me, pl.ds((1 - c) * h, h)],
                                             local.at[nk])
            return pltpu.make_async_copy(in_ref.at[pl.ds(c * h + k * tm, tm)], rows(me, c, k), local.at[k])

        def level1(t, k, incoming):
            place = rows(chip_of[t] if incoming else me, c, k)
            return pltpu.make_async_remote_copy(src_ref=place, dst_ref=place, send_sem=send1.at[t, k],
                                                recv_sem=recv1.at[t, k], device_id=(*chips[t], c), device_id_type=MESH)

        def level2(t, k, incoming):
            place = rows(chip_of[t], (1 - c) if incoming else c, k)
            return pltpu.make_async_remote_copy(src_ref=place, dst_ref=place, send_sem=send2.at[t, k],
                                                recv_sem=recv2.at[t, k], device_id=(x, y, 1 - c),
                                                device_id_type=MESH)

        return buf, local, nk, mine, level1, level2

    def start(ins, outs, scr):
        for j in range(n):
            _, _, nk, mine, _, _ = copies(j, ins[j], scr)
            for k in range(nk + 1):
                mine(k).start()
        for j in range(n):
            _, _, nk, mine, level1, _ = copies(j, ins[j], scr)
            for k in range(nk):
                mine(k).wait()
                for t in range(3):
                    level1(t, k, False).start()

    def finish(ins, outs, scr):
        for j in range(n):
            _, _, nk, _, level1, level2 = copies(j, ins[j], scr)
            for k in range(nk):
                for t in range(3):
                    level1(t, k, True).wait_recv()
                    level2(t, k, False).start()
        for j in range(n):
            buf, local, nk, mine, level1, level2 = copies(j, ins[j], scr)
            for k in range(nk):
                for t in range(3):
                    level2(t, k, True).wait_recv()
            for k in range(nk):
                for t in range(3):
                    level1(t, k, False).wait_send()
                    level2(t, k, False).wait_send()
            mine(nk).wait()
            pltpu.make_async_copy(buf, outs[j], local.at[nk + 1]).start()
        for j in range(n):
            buf, local, nk, _, _, _ = copies(j, ins[j], scr)
            pltpu.make_async_copy(buf, outs[j], local.at[nk + 1]).wait()

    return Rider(list(shards), [_sds((N_CHIP,) + a.shape, a.dtype) for a in shards], scratch, start, finish)


def run_alone(rider, name):
    return _call(lambda: None, name=name, out_shape=[], in_specs=[], out_specs=[], grid=(1,), rider=rider)()[1]


def sibling_sum(g, tm, name):
    _, r, cdim = g.shape
    h = r // 2
    ni = h // tm
    dma = pltpu.SemaphoreType.DMA

    def body(cores_ref, keep_ref, give_ref, out_ref, slot, send, recv):
        par = (pl.program_id(0) * ni + pl.program_id(1)) % 2
        x, y, c, _, _ = _place()
        cp = pltpu.make_async_remote_copy(src_ref=give_ref, dst_ref=slot.at[par], send_sem=send.at[par],
                                          recv_sem=recv.at[par], device_id=(x, y, 1 - c), device_id_type=MESH)
        cp.start()
        cp.wait_recv()
        out_ref[...] = (keep_ref[...].astype(F32) + slot[par].astype(F32)).astype(out_ref.dtype)
        cp.wait_send()

    flat = g.reshape(N_CHIP * r, cdim)
    return _staged_call(
        body, name=name, grid=(N_CHIP, ni),
        in_specs=[pl.BlockSpec((tm, cdim), lambda j, i, cores: ((2 * j + cores[0]) * ni + i, 0)),
                  pl.BlockSpec((tm, cdim), lambda j, i, cores: ((2 * j + cores[1]) * ni + i, 0))],
        out_specs=pl.BlockSpec((None, tm, cdim), lambda j, i, cores: (j, i, 0)),
        out_shape=_sds((N_CHIP, h, cdim), g.dtype),
        scratch_shapes=[pltpu.VMEM((2, tm, cdim), g.dtype), dma((2,)), dma((2,))],
    )(_cores(), flat, flat)


def owner_sum_rider(sums, tiles):
    dma = pltpu.SemaphoreType.DMA
    n = len(sums)
    geo = [(a.shape[1], tm, a.shape[1] // tm) for a, tm in zip(sums, tiles)]
    scratch = []
    for a, (h, tm, nk) in zip(sums, geo):
        cdim = a.shape[2]
        scratch += [pltpu.VMEM(a.shape, a.dtype), pltpu.VMEM((3, h, cdim), a.dtype), pltpu.VMEM((2, h, cdim), F32),
                    dma((3, nk)), dma((3, nk)), dma((nk,)), dma((nk,)), dma((2,))]

    def copies(j, scr):
        part, got, res, send, recv, send2, recv2, local = scr[8 * j:8 * j + 8]
        h, tm, nk = geo[j]
        x, y, c, me, chips = _place()

        def to_owner(t, k):
            chip = 2 * chips[t][0] + chips[t][1]
            return pltpu.make_async_remote_copy(
                src_ref=part.at[chip, pl.ds(k * tm, tm)], dst_ref=got.at[t, pl.ds(k * tm, tm)],
                send_sem=send.at[t, k], recv_sem=recv.at[t, k], device_id=(*chips[t], c), device_id_type=MESH)

        def to_sibling(k):
            place = res.at[c, pl.ds(k * tm, tm)]
            return pltpu.make_async_remote_copy(src_ref=place, dst_ref=place, send_sem=send2.at[k],
                                                recv_sem=recv2.at[k], device_id=(x, y, 1 - c), device_id_type=MESH)

        return part, got, res, local, to_owner, to_sibling, (tm, nk, c, me)

    def start(ins, outs, scr):
        for j in range(n):
            part, _, _, local, _, _, _ = copies(j, scr)
            pltpu.make_async_copy(ins[j], part, local.at[0]).start()
        for j in range(n):
            part, _, _, local, to_owner, _, (tm, nk, c, me) = copies(j, scr)
            pltpu.make_async_copy(ins[j], part, local.at[0]).wait()
            for k in range(nk):
                for t in range(3):
                    to_owner(t, k).start()

    def finish(ins, outs, scr):
        for j in range(n):
            part, got, res, _, to_owner, to_sibling, (tm, nk, c, me) = copies(j, scr)
            for k in range(nk):
                rows = pl.ds(k * tm, tm)
                for t in range(3):
                    to_owner(t, k).wait_recv()
                acc = part[me, rows, :].astype(F32)
                for t in range(3):
                    acc = acc + got[t, rows, :].astype(F32)
                res[c, rows, :] = acc
                to_sibling(k).start()
        for j in range(n):
            _, _, res, local, to_owner, to_sibling, (tm, nk, c, me) = copies(j, scr)
            for k in range(nk):
                to_sibling(k).wait_recv()
            for k in range(nk):
                to_sibling(k).wait_send()
                for t in range(3):
                    to_owner(t, k).wait_send()
            pltpu.make_async_copy(res, outs[j], local.at[1]).start()
        for j in range(n):
            _, _, res, local, _, _, _ = copies(j, scr)
            pltpu.make_async_copy(res, outs[j], local.at[1]).wait()

    return Rider(list(sums), [_sds((2, a.shape[1], a.shape[2]), F32) for a in sums], scratch, start, finish)


def gather_conv_w(w):
    def body(in_ref, out_ref, send, recv):
        x, y, c, me, chips = _place()
        out_ref[me] = in_ref[...]
        copies = []
        for t in range(3):
            copies.append(pltpu.make_async_remote_copy(
                src_ref=out_ref.at[me], dst_ref=out_ref.at[me], send_sem=send.at[t], recv_sem=recv.at[t],
                device_id=(*chips[t], c), device_id_type=MESH))
        for cp in copies:
            cp.start()
        for cp in copies:
            cp.wait_recv()
        for cp in copies:
            cp.wait_send()

    dma = pltpu.SemaphoreType.DMA
    vmem = pl.BlockSpec(memory_space=pltpu.VMEM)
    return pl.pallas_call(
        body, out_shape=_sds((N_CHIP,) + w.shape, w.dtype), in_specs=[vmem], out_specs=vmem, name="gather_conv_w",
        scratch_shapes=[dma((3,)), dma((3,))],
        compiler_params=pltpu.CompilerParams(has_side_effects=True))(w)


N_DEV = 8
SMALL_ROWS = 24


def all_reduce_small(pack):
    def body(in_ref, out_ref, buf, send, recv):
        x, y, c, _, _ = _place()
        me = 4 * x + 2 * y + c
        buf[me] = in_ref[...]
        copies = []
        for r in range(1, N_DEV):
            px = (1 - x) if r & 4 else x
            py = (1 - y) if r & 2 else y
            pc = (1 - c) if r & 1 else c
            copies.append(pltpu.make_async_remote_copy(
                src_ref=buf.at[me], dst_ref=buf.at[me], send_sem=send.at[r - 1], recv_sem=recv.at[r - 1],
                device_id=(px, py, pc), device_id_type=MESH))
        for cp in copies:
            cp.start()
        for cp in copies:
            cp.wait_recv()
        for cp in copies:
            cp.wait_send()
        acc = buf[0]
        for j in range(1, N_DEV):
            acc = acc + buf[j]
        out_ref[...] = acc

    dma = pltpu.SemaphoreType.DMA
    vmem = pl.BlockSpec(memory_space=pltpu.VMEM)
    return pl.pallas_call(
        body, out_shape=_sds(pack.shape, F32), in_specs=[vmem], out_specs=vmem, name="all_reduce_small",
        scratch_shapes=[pltpu.VMEM((N_DEV,) + pack.shape, F32), dma((N_DEV - 1,)), dma((N_DEV - 1,))],
        compiler_params=pltpu.CompilerParams(has_side_effects=True))(pack)


def _row_tile(rows, limit, multiple):
    return max(t for t in range(multiple, min(rows, limit) + 1, multiple) if rows % t == 0)


def add_pair(a, b, name):
    _, h, c = a.shape

    def body(a_ref, b_ref, o_ref):
        o_ref[...] = (a_ref[...].astype(F32) + b_ref[...].astype(F32)).astype(o_ref.dtype)

    blk = pl.BlockSpec((None, h, c), lambda j: (j, 0, 0))
    return _call(body, name=name, out_shape=_sds(a.shape, a.dtype), grid=(N_CHIP,), in_specs=[blk, blk],
                 out_specs=blk)(a, b)


def sum_slots(buf, name):
    _, h, c = buf.shape
    tm = _row_tile(h, 256, 16)

    def body(b_ref, o_ref):
        acc = b_ref[3].astype(F32)
        for t in range(3):
            acc = acc + b_ref[t].astype(F32)
        o_ref[...] = acc

    return _call(body, name=name, out_shape=_sds((h, c), F32), grid=(h // tm,),
                 in_specs=[pl.BlockSpec((N_CHIP, tm, c), lambda i: (0, i, 0))],
                 out_specs=pl.BlockSpec((tm, c), lambda i: (i, 0)))(buf)


def adamw(w, g, row_off, m, v, name):
    r, c = w.shape
    tm = r if r < 8 else _row_tile(math.gcd(r, row_off) if row_off else r, 128, 8)
    c1 = 1.0 - ADAM_B1 ** ADAM_STEP
    c2 = 1.0 - ADAM_B2 ** ADAM_STEP

    def body(w_ref, g_ref, m_ref, v_ref, go_ref, d_ref, m2_ref, v2_ref):
        gv = g_ref[...]
        go_ref[...] = gv
        m2 = ADAM_B1 * m_ref[...] + (1.0 - ADAM_B1) * gv
        v2 = ADAM_B2 * v_ref[...] + (1.0 - ADAM_B2) * (gv * gv)
        m2_ref[...] = m2
        v2_ref[...] = v2
        d_ref[...] = -ADAM_LR * ((m2 / c1) / (jnp.sqrt(v2 / c2) + ADAM_EPS) + ADAM_WD * w_ref[...])

    blk = pl.BlockSpec((tm, c), lambda i: (i, 0))
    shp = _sds((r, c), F32)
    return _call(body, name=name, out_shape=[shp] * 4, grid=(r // tm,),
                 in_specs=[blk, pl.BlockSpec((tm, c), lambda i: (row_off // tm + i, 0)), blk, blk],
                 out_specs=[blk] * 4)(w, g, m, v)


BIG = ("ffn1_w_gate", "ffn1_w_up", "ffn1_w_down", "w_in", "w_attn_branch", "w_ssd_branch", "w_out",
       "ffn2_w_gate", "ffn2_w_up", "ffn2_w_down")
SMALL = ("ffn1_norm", "mix_norm", "q_norm", "k_norm", "conv_b", "dt_bias", "a_log", "d_skip", "ssd_norm", "ffn2_norm")
WEIGHTS = ("ffn1_norm", "ffn1_w_gate", "ffn1_w_up", "ffn1_w_down", "mix_norm", "w_in", "q_norm", "k_norm", "conv_w",
           "conv_b", "dt_bias", "a_log", "d_skip", "ssd_norm", "w_attn_branch", "w_ssd_branch", "w_out", "ffn2_norm",
           "ffn2_w_gate", "ffn2_w_up", "ffn2_w_down")
CONV_SHARD = SSD_CONV_DIM // N_CHIP
CLASSES = {
    "ffn1_in": (("ffn1_w_gate", 1024), ("ffn1_w_up", 1024)),
    "ffn1_out": (("ffn1_w_down", 704),),
    "mix_in": (("w_in", 1024),),
    "mix_attn": (("w_attn_branch", 512),),
    "late_out": (("ffn2_w_down", 704), ("w_ssd_branch", 512), ("w_out", 256)),
    "ffn2_in": (("ffn2_w_gate", 1024), ("ffn2_w_up", 1024)),
}
CLASS_TILE = {"ffn1_in": 256, "ffn1_out": 176, "mix_in": 128, "mix_attn": 256, "late_out": 368, "ffn2_in": 256}


def _pack_small(vals, conv_part, loss_part=None):
    flat = [vals[k].reshape(-1) for k in SMALL]
    flat.append(jnp.zeros((SSD_CONV * SSD_CONV_DIM,), F32) if conv_part is None else conv_part.reshape(-1))
    flat.append(jnp.zeros((1,), F32) if loss_part is None else loss_part.reshape(1))
    flat = jnp.concatenate(flat)
    return jnp.pad(flat, (0, SMALL_ROWS * D_MODEL - flat.shape[0])).reshape(SMALL_ROWS, D_MODEL)


def _unpack_small(pack, like):
    flat = pack.reshape(-1)
    out, off = {}, 0
    for k in SMALL:
        n = like[k].size
        out[k] = flat[off:off + n].reshape(like[k].shape)
        off += n
    conv = flat[off:off + SSD_CONV * SSD_CONV_DIM].reshape(SSD_CONV, SSD_CONV_DIM)
    return out, conv, flat[off + SSD_CONV * SSD_CONV_DIM]


def _chip_major_cols(a):
    r = a.shape[0]
    return a.reshape(r, N_CHIP, -1).transpose(1, 0, 2)


def _from_chip_major_cols(a):
    return a.transpose(1, 0, 2).reshape(a.shape[1], -1)


def kernel(x, ffn1_norm, ffn1_w_gate, ffn1_w_up, ffn1_w_down, mix_norm, w_in, q_norm, k_norm, conv_w, conv_b, dt_bias, a_log, d_skip, ssd_norm, w_attn_branch, w_ssd_branch, w_out, ffn2_norm, ffn2_w_gate, ffn2_w_up, ffn2_w_down, loss_target, m_ffn1_norm, m_ffn1_w_gate, m_ffn1_w_up, m_ffn1_w_down, m_mix_norm, m_w_in, m_q_norm, m_k_norm, m_conv_w, m_conv_b, m_dt_bias, m_a_log, m_d_skip, m_ssd_norm, m_w_attn_branch, m_w_ssd_branch, m_w_out, m_ffn2_norm, m_ffn2_w_gate, m_ffn2_w_up, m_ffn2_w_down, v_ffn1_norm, v_ffn1_w_gate, v_ffn1_w_up, v_ffn1_w_down, v_mix_norm, v_w_in, v_q_norm, v_k_norm, v_conv_w, v_conv_b, v_dt_bias, v_a_log, v_d_skip, v_ssd_norm, v_w_attn_branch, v_w_ssd_branch, v_w_out, v_ffn2_norm, v_ffn2_w_gate, v_ffn2_w_up, v_ffn2_w_down):
    env = dict(locals())
    wts = {k: env[k] for k in WEIGHTS}
    moms = {k: env["m_" + k] for k in WEIGHTS}
    vars_ = {k: env["v_" + k] for k in WEIGHTS}
    x0 = x[0]
    target = loss_target[0]

    def gather(classes):
        shards = [jnp.concatenate([wts[k][0] for k, _ in CLASSES[c]], axis=0).astype(BF16) for c in classes]
        return gather_rider(shards, [CLASS_TILE[c] for c in classes])

    def reducer(classes, parts):
        sums = [sibling_sum(p, CLASS_TILE[c], f"sibling_sum_{c}") for c, p in zip(classes, parts)]
        return owner_sum_rider(sums, [CLASS_TILE[c] for c in classes])

    (w_ffn1_in,) = run_alone(gather(["ffn1_in"]), "gather_ffn1_in")
    x1, saved1, (w_ffn1_out, w_mix_in, w_mix_attn) = ffn_forward(
        x0, ffn1_norm, w_ffn1_in, lambda rode: rode[0], "ffn1", rider=gather(["ffn1_out", "mix_in", "mix_attn"]))
    w_in_full = _from_chip_major_cols(w_mix_in)
    mixer_w = dict(
        mix_norm=mix_norm,
        w_in_main=jnp.concatenate([w_in_full[:, :IN_DT0], w_in_full[:, IN_DT1:]], axis=1),
        w_in_dt=jnp.pad(w_in_full[:, IN_DT0:IN_DT1], ((0, 0), (0, DT_PAD - SSD_HEADS))),
        q_gain=jnp.tile(q_norm, (1, 2)), k_gain=jnp.tile(k_norm, (1, 2)),
        conv_w=_from_chip_major_cols(gather_conv_w(conv_w[0])), conv_b=conv_b, dt_bias=dt_bias, a_log=a_log,
        d_skip=d_skip, ssd_norm=ssd_norm, w_attn_branch=_from_chip_major_cols(w_mix_attn))

    def later_weights(rode):
        late = rode[0]
        return dict(w_ssd_branch=late[:, 704:1216].reshape(SSD_INNER, D_MODEL),
                    w_out=late[:, 1216:1472].reshape(D_MODEL, D_MODEL))

    x2, saved_mix, (w_late_out, w_ffn2_in) = mixer_forward(x1, mixer_w, gather(["late_out", "ffn2_in"]), later_weights)
    x3, saved2, _ = ffn_forward(x2, ffn2_norm, w_ffn2_in, lambda rode: w_late_out, "ffn2")
    dx3, sq = loss_grad(x3, target, "loss")

    grads = {}
    dx2, grads["ffn2_norm"], d_ffn2_in, d_ffn2_down = ffn_backward(dx3, x2, ffn2_norm, w_ffn2_in, w_late_out, saved2,
                                                                   "ffn2")

    def ride_early(g):
        late = jnp.concatenate([d_ffn2_down, g["w_ssd_branch"].reshape(N_CHIP, -1, D_MODEL),
                                g["w_out"].reshape(N_CHIP, -1, D_MODEL)], axis=1)
        return reducer(["ffn2_in", "late_out"], [d_ffn2_in, late])

    def ride_late(g):
        g_in = jnp.concatenate([g["w_in_main"][:, :IN_DT0], g["w_in_dt"][:, :SSD_HEADS], g["w_in_main"][:, IN_DT0:]],
                               axis=1)
        return reducer(["mix_in", "mix_attn"], [_chip_major_cols(g_in), _chip_major_cols(g["w_attn_branch"])])

    dx1, gmix = mixer_backward(dx2, x1, saved_mix, ride_early, ride_late)
    dx0, grads["ffn1_norm"], rode_in, rode_out = ffn_backward(
        dx1, x0, ffn1_norm, w_ffn1_in, w_ffn1_out, saved1, "ffn1",
        ride_down=lambda d: reducer(["ffn1_out"], [d]), ride_in=lambda d: reducer(["ffn1_in"], [d]))
    for k in ("mix_norm", "q_norm", "k_norm", "conv_b", "dt_bias", "a_log", "d_skip", "ssd_norm"):
        grads[k] = gmix[k]
    reduced = dict(zip(("ffn2_in", "late_out", "mix_in", "mix_attn", "ffn1_in", "ffn1_out"),
                       (*gmix["rode_early"], *gmix["rode_late"], rode_in[0], rode_out[0])))
    reduced = {c: r.reshape(-1, r.shape[2]) for c, r in reduced.items()}
    small_sum = all_reduce_small(_pack_small(grads, gmix["conv_w"], 0.5 * jnp.sum(sq) / D_MODEL))
    g_small, g_conv_full, loss = _unpack_small(small_sum, wts)
    chip = 2 * lax.axis_index("x") + lax.axis_index("y")
    g_conv = lax.dynamic_slice_in_dim(g_conv_full, chip * CONV_SHARD, CONV_SHARD, axis=1)

    g_final, delta, new_m, new_v = dict(g_small), {}, {}, {}

    def update(k, g_arr, row_off):
        shp = wts[k].shape
        two_d = shp[1:]
        res = adamw(wts[k].reshape(two_d), g_arr, row_off, moms[k].reshape(two_d), vars_[k].reshape(two_d),
                    f"adamw_{k}")
        g_final[k], delta[k], new_m[k], new_v[k] = (r.reshape(shp) for r in res)

    for cls, members in CLASSES.items():
        off = 0
        for k, rows in members:
            update(k, reduced[cls], off)
            off += rows
    update("conv_w", g_conv, 0)
    _, d, m2, v2 = adamw(_pack_small(wts, None), _pack_small(g_small, None), 0, _pack_small(moms, None),
                         _pack_small(vars_, None), "adamw_small")
    for res, packed in ((delta, d), (new_m, m2), (new_v, v2)):
        res.update(_unpack_small(packed, wts)[0])

    return (loss, dx0[None], *[g_final[k] for k in WEIGHTS], *[delta[k] for k in WEIGHTS],
            *[new_m[k] for k in WEIGHTS], *[new_v[k] for k in WEIGHTS])
```

```python
import collections
import functools
import math

import jax
import jax.numpy as jnp
from jax import lax
from jax.experimental import pallas as pl
from jax.experimental.pallas import tpu as pltpu

F32 = jnp.float32
BF16 = jnp.bfloat16
MESH = pl.DeviceIdType.MESH

EPS = 1e-6
D_MODEL = 1024
D_FF = 2816
N_CHIP = 4
FF_SHARD = D_FF // N_CHIP
HD = 64
BLK = 128
ATTN_DILATIONS = (1, 4, 16)
HEADS_PER_PATTERN = 8
N_ATTN_HEADS = 24
ALIBI_MAX_EXP = 8.0
ATTN_QKV = 1536
GROUP_W = 512
SSD_INNER = 2048
SSD_HEADS = 32
SSD_GROUPS = 4
SSD_CONV = 4
SSD_CONV_DIM = 3072
IN_COLS = 11808
IN_DT0, IN_DT1 = 9728, 9760
IN_SHARD = IN_COLS // 4
COL_K, COL_V, COL_Z, COL_XBC, COL_GA, COL_GS, P_COLS = 1536, 3072, 4608, 6656, 9728, 10752, 11776
DT_PAD = 128

ADAM_LR, ADAM_B1, ADAM_B2, ADAM_EPS, ADAM_WD, ADAM_STEP = 0.001, 0.9, 0.999, 1e-08, 0.01, 10

V7X_VMEM_LIMIT = 56 * 1024 * 1024
NEG = -1e30


Rider = collections.namedtuple("Rider", "arrays out_shape scratch start finish")
Rider.__doc__ = """An exchange between devices that rides in a compute kernel: its copies are started in the host's
first grid step and waited for in its last, so they travel while the host computes.  arrays / out_shape: extra HBM
operands and results; scratch: extra scratch; start, finish: f(in_refs, out_refs, scratch_refs)."""


def _call(body, *, name, out_shape, in_specs, out_specs, grid=(), scratch_shapes=(), aliases=None, rider=None):
    params = dict(dimension_semantics=("arbitrary",) * len(grid), vmem_limit_bytes=V7X_VMEM_LIMIT)
    if rider is None:
        return pl.pallas_call(
            body, out_shape=out_shape, grid=grid, in_specs=in_specs, out_specs=out_specs,
            scratch_shapes=scratch_shapes, input_output_aliases=aliases or {}, name=name,
            compiler_params=pltpu.CompilerParams(**params))
    single = not isinstance(out_shape, (list, tuple))
    main_out = [out_shape] if single else list(out_shape)
    main_specs = [out_specs] if single else list(out_specs)
    n_in, n_out, n_scr = len(in_specs), len(main_out), len(scratch_shapes)
    r_in, r_out = len(rider.arrays), len(rider.out_shape)

    def wrapped(*refs):
        ins, refs = refs[:n_in], refs[n_in:]
        r_ins, refs = refs[:r_in], refs[r_in:]
        outs, refs = refs[:n_out], refs[n_out:]
        r_outs, refs = refs[:r_out], refs[r_out:]
        scr, r_scr = refs[:n_scr], refs[n_scr:]
        first = last = None
        for axis, size in enumerate(grid):
            at_start, at_end = pl.program_id(axis) == 0, pl.program_id(axis) == size - 1
            first = at_start if first is None else jnp.logical_and(first, at_start)
            last = at_end if last is None else jnp.logical_and(last, at_end)

        @pl.when(first)
        def _():
            rider.start(r_ins, r_outs, r_scr)

        body(*ins, *outs, *scr)

        @pl.when(last)
        def _():
            rider.finish(r_ins, r_outs, r_scr)

    hbm = pl.BlockSpec(memory_space=pl.ANY)
    call = pl.pallas_call(
        wrapped, out_shape=main_out + list(rider.out_shape), grid=grid, in_specs=list(in_specs) + [hbm] * r_in,
        out_specs=main_specs + [hbm] * r_out, scratch_shapes=list(scratch_shapes) + list(rider.scratch), name=name,
        compiler_params=pltpu.CompilerParams(has_side_effects=True, **params))

    def run(*args):
        res = call(*args, *rider.arrays)
        main = res[:n_out]
        return (main[0] if single else main), res[n_out:]

    return run


def _sds(shape, dtype):
    return jax.ShapeDtypeStruct(tuple(shape), dtype)


def _dot(a, b):
    return jnp.dot(a, b, preferred_element_type=F32)


def _dot_nt(a, b):
    return lax.dot_general(a, b, (((1,), (1,)), ((), ())), preferred_element_type=F32)


def _dot_tn(a, b):
    return lax.dot_general(a, b, (((0,), (0,)), ((), ())), preferred_element_type=F32)


def _dot_hi(a, b):
    return jnp.dot(a, b, preferred_element_type=F32, precision=lax.Precision.HIGHEST)


def _sigmoid(x):
    return 1.0 / (1.0 + jnp.exp(-x))


def _lane_first_half(shape):
    return lax.broadcasted_iota(jnp.int32, shape, len(shape) - 1) < HD


def _pair_sum(x, first):
    s_all = jnp.sum(x, axis=-1, keepdims=True)
    s_a = jnp.sum(jnp.where(first, x, 0.0), axis=-1, keepdims=True)
    return s_a, s_all - s_a


def _rowwise(name, fn, rows, consts, outs, accs=(), tm=512):
    n_rows = None
    in_arrays, in_specs = [], []
    for r in rows:
        if isinstance(r, tuple):
            arr, w, cb = r
            spec = pl.BlockSpec((tm, w), functools.partial(lambda i, cb: (i, cb), cb=cb))
        else:
            arr = r
            spec = pl.BlockSpec((tm, arr.shape[1]), lambda i: (i, 0))
        n_rows = arr.shape[0]
        in_arrays.append(arr)
        in_specs.append(spec)
    for c in consts:
        in_arrays.append(c)
        in_specs.append(pl.BlockSpec(c.shape, functools.partial(lambda i, n: (0,) * n, n=c.ndim)))
    out_shape = [_sds(s, d) for s, d in outs] + [_sds(s, d) for s, d in accs]
    out_specs = [pl.BlockSpec((tm, s[1]), lambda i: (i, 0)) for s, _ in outs]
    out_specs += [pl.BlockSpec(s, functools.partial(lambda i, n: (0,) * n, n=len(s))) for s, _ in accs]

    def body(*refs):
        fn(pl.program_id(0), *refs)

    res = _call(body, name=name, out_shape=out_shape, grid=(n_rows // tm,), in_specs=in_specs,
                out_specs=out_specs)(*in_arrays)
    return res


def rms_fwd(x, gain, name):
    def fn(i, x_ref, g_ref, h_ref):
        xv = x_ref[...]
        r = lax.rsqrt(jnp.mean(xv * xv, axis=-1, keepdims=True) + EPS)
        h_ref[...] = (xv * r * g_ref[...]).astype(h_ref.dtype)

    return _rowwise(name, fn, [x], [gain], [(x.shape, BF16)])[0]


def rms_bwd(dhs, x, gain, dx_in, name):
    n = len(dhs)

    def fn(i, *refs):
        dh_refs, (x_ref, dxin_ref, g_ref, dx_ref, dg_ref) = refs[:n], refs[n:]
        dh = dh_refs[0][...]
        for r in dh_refs[1:]:
            dh = dh + r[...]
        xv = x_ref[...]
        r = lax.rsqrt(jnp.mean(xv * xv, axis=-1, keepdims=True) + EPS)
        xn = xv * r
        dxn = dh * g_ref[...]
        dx_ref[...] = dxin_ref[...] + r * (dxn - xn * jnp.mean(dxn * xn, axis=-1, keepdims=True))

        @pl.when(i == 0)
        def _():
            dg_ref[...] = jnp.zeros_like(dg_ref)

        dg_ref[...] += jnp.sum(dh * xn, axis=0, keepdims=True)

    return _rowwise(name, fn, list(dhs) + [x, dx_in], [gain], [(x.shape, F32)], [((1, x.shape[1]), F32)])


def loss_grad(y, target, name):
    def fn(i, y_ref, t_ref, dy_ref, sq_ref):
        err = y_ref[...] - t_ref[...]
        dy_ref[...] = err * (1.0 / y_ref.shape[1])

        @pl.when(i == 0)
        def _():
            sq_ref[...] = jnp.zeros_like(sq_ref)

        sq_ref[...] += jnp.sum(err * err, axis=0, keepdims=True)

    return _rowwise(name, fn, [y, target], [], [(y.shape, F32)], [((1, y.shape[1]), F32)])


def matmul_nn(a, b, name, out_dtype, tm, tn, res=None, scale=1.0, rider=None):
    s, k = a.shape
    n = b.shape[1]

    def body(*refs):
        if res is None:
            a_ref, b_ref, o_ref = refs
            o_ref[...] = _dot(a_ref[...], b_ref[...]).astype(o_ref.dtype)
        else:
            a_ref, b_ref, r_ref, o_ref = refs
            o_ref[...] = (r_ref[...] + scale * _dot(a_ref[...], b_ref[...])).astype(o_ref.dtype)

    in_specs = [pl.BlockSpec((tm, k), lambda i, j: (i, 0)), pl.BlockSpec((k, tn), lambda i, j: (0, j))]
    args = [a, b]
    if res is not None:
        in_specs.append(pl.BlockSpec((tm, tn), lambda i, j: (i, j)))
        args.append(res)
    return _call(body, name=name, out_shape=_sds((s, n), out_dtype), grid=(s // tm, n // tn), in_specs=in_specs,
                 out_specs=pl.BlockSpec((tm, tn), lambda i, j: (i, j)), rider=rider)(*args)


def matmul_nt(a, b, name, out_dtype, tm, tn, tk, rider=None):
    s, k = a.shape
    n = b.shape[0]
    nk = k // tk

    def body(a_ref, b_ref, o_ref, acc_ref):
        kk = pl.program_id(2)

        @pl.when(kk == 0)
        def _():
            acc_ref[...] = jnp.zeros_like(acc_ref)

        acc_ref[...] += _dot_nt(a_ref[...].astype(BF16), b_ref[...])

        @pl.when(kk == nk - 1)
        def _():
            o_ref[...] = acc_ref[...].astype(o_ref.dtype)

    return _call(body, name=name, out_shape=_sds((s, n), out_dtype), grid=(s // tm, n // tn, nk),
                 in_specs=[pl.BlockSpec((tm, tk), lambda i, j, kk: (i, kk)),
                           pl.BlockSpec((tn, tk), lambda i, j, kk: (j, kk))],
                 out_specs=pl.BlockSpec((tm, tn), lambda i, j, kk: (i, j)),
                 scratch_shapes=[pltpu.VMEM((tm, tn), F32)], rider=rider)(a, b)


def matmul_tn(a, b, name, tn, ts, a_scale=None, b_scale=None, rider=None):
    s, m = a.shape
    n = b.shape[1]
    ns = s // ts

    def body(a_ref, b_ref, o_ref, acc_ref):
        ss = pl.program_id(1)

        @pl.when(ss == 0)
        def _():
            acc_ref[...] = jnp.zeros_like(acc_ref)

        av, bv = a_ref[...], b_ref[...]
        if a_scale is not None:
            av = av * a_scale
        if b_scale is not None:
            bv = bv * b_scale
        acc_ref[...] += _dot_tn(av.astype(BF16), bv.astype(BF16))

        @pl.when(ss == ns - 1)
        def _():
            o_ref[...] = acc_ref[...].astype(o_ref.dtype)

    return _call(body, name=name, out_shape=_sds((m, n), BF16), grid=(n // tn, ns),
                 in_specs=[pl.BlockSpec((ts, m), lambda j, ss: (ss, 0)), pl.BlockSpec((ts, tn), lambda j, ss: (ss, j))],
                 out_specs=pl.BlockSpec((m, tn), lambda j, ss: (0, j)),
                 scratch_shapes=[pltpu.VMEM((m, tn), F32)], rider=rider)(a, b)


def ffn_up(h, w704, gate_blk, up_blk, name, tm=512, rider=None):
    s = h.shape[0]

    def body(h_ref, wg_ref, wu_ref, g_ref, u_ref, a_ref):
        hv = h_ref[...]
        g = _dot(hv, wg_ref[...])
        u = _dot(hv, wu_ref[...])
        g_ref[...] = g.astype(BF16)
        u_ref[...] = u.astype(BF16)
        a_ref[...] = (g * _sigmoid(g) * u).astype(BF16)

    ospec = pl.BlockSpec((None, tm, FF_SHARD), lambda j, i: (j, i, 0))
    shp = _sds((N_CHIP, s, FF_SHARD), BF16)
    return _call(body, name=name, out_shape=[shp, shp, shp], grid=(N_CHIP, s // tm),
                 in_specs=[pl.BlockSpec((tm, D_MODEL), lambda j, i: (i, 0)),
                           pl.BlockSpec((None, D_MODEL, FF_SHARD), lambda j, i: (j, gate_blk, 0)),
                           pl.BlockSpec((None, D_MODEL, FF_SHARD), lambda j, i: (j, up_blk, 0))],
                 out_specs=[ospec, ospec, ospec], rider=rider)(h, w704, w704)


def ffn_down(a, w1024, blk, x, name, tm=512):
    s = x.shape[0]

    def body(a_ref, wd_ref, x_ref, o_ref):
        acc = _dot(a_ref[0], wd_ref[0])
        for j in range(1, N_CHIP):
            acc += _dot(a_ref[j], wd_ref[j])
        o_ref[...] = x_ref[...] + 0.5 * acc

    return _call(body, name=name, out_shape=_sds((s, D_MODEL), F32), grid=(s // tm,),
                 in_specs=[pl.BlockSpec((N_CHIP, tm, FF_SHARD), lambda i: (0, i, 0)),
                           pl.BlockSpec((N_CHIP, FF_SHARD, D_MODEL), lambda i: (0, blk, 0)),
                           pl.BlockSpec((tm, D_MODEL), lambda i: (i, 0))],
                 out_specs=pl.BlockSpec((tm, D_MODEL), lambda i: (i, 0)))(a, w1024, x)


def ffn_bwd_hidden(dx, w1024, blk, g, u, name, tm=1024, rider=None):
    s = dx.shape[0]

    def body(dx_ref, wd_ref, g_ref, u_ref, dg_ref, du_ref):
        dy = (0.5 * dx_ref[...]).astype(BF16)
        da = _dot_nt(dy, wd_ref[...])
        gv = g_ref[...].astype(F32)
        uv = u_ref[...].astype(F32)
        sg = _sigmoid(gv)
        dg_ref[...] = (da * uv * (sg * (1.0 + gv * (1.0 - sg)))).astype(BF16)
        du_ref[...] = (da * gv * sg).astype(BF16)

    hspec = pl.BlockSpec((None, tm, FF_SHARD), lambda j, i: (j, i, 0))
    shp = _sds((N_CHIP, s, FF_SHARD), BF16)
    return _call(body, name=name, out_shape=[shp, shp], grid=(N_CHIP, s // tm),
                 in_specs=[pl.BlockSpec((tm, D_MODEL), lambda j, i: (i, 0)),
                           pl.BlockSpec((None, FF_SHARD, D_MODEL), lambda j, i: (j, blk, 0)), hspec, hspec],
                 out_specs=[hspec, hspec], rider=rider)(dx, w1024, g, u)


def ffn_bwd_input(dg, du, w704, gate_blk, up_blk, name, tm=512, rider=None):
    s = dg.shape[1]

    def body(dg_ref, du_ref, wg_ref, wu_ref, o_ref):
        acc = _dot_nt(dg_ref[0], wg_ref[0]) + _dot_nt(du_ref[0], wu_ref[0])
        for j in range(1, N_CHIP):
            acc += _dot_nt(dg_ref[j], wg_ref[j]) + _dot_nt(du_ref[j], wu_ref[j])
        o_ref[...] = acc

    hspec = pl.BlockSpec((N_CHIP, tm, FF_SHARD), lambda i: (0, i, 0))
    return _call(body, name=name, out_shape=_sds((s, D_MODEL), F32), grid=(s // tm,),
                 in_specs=[hspec, hspec,
                           pl.BlockSpec((N_CHIP, D_MODEL, FF_SHARD), lambda i: (0, gate_blk, 0), pl.Buffered(1)),
                           pl.BlockSpec((N_CHIP, D_MODEL, FF_SHARD), lambda i: (0, up_blk, 0), pl.Buffered(1))],
                 out_specs=pl.BlockSpec((tm, D_MODEL), lambda i: (i, 0)), rider=rider)(dg, du, w704, w704)


def ffn_wgrad_in(h, dgu, name, ts=1024):
    s = h.shape[0]
    ns = s // ts

    def body(h_ref, d_ref, o_ref, acc_ref):
        ss = pl.program_id(1)

        @pl.when(ss == 0)
        def _():
            acc_ref[...] = jnp.zeros_like(acc_ref)

        acc_ref[...] += _dot_tn(h_ref[...], d_ref[...])

        @pl.when(ss == ns - 1)
        def _():
            o_ref[...] = acc_ref[...].astype(BF16)

    return _call(body, name=name, out_shape=_sds((N_CHIP, D_MODEL, FF_SHARD), BF16), grid=(N_CHIP, ns),
                 in_specs=[pl.BlockSpec((ts, D_MODEL), lambda j, ss: (ss, 0)),
                           pl.BlockSpec((None, ts, FF_SHARD), lambda j, ss: (j, ss, 0))],
                 out_specs=pl.BlockSpec((None, D_MODEL, FF_SHARD), lambda j, ss: (j, 0, 0)),
                 scratch_shapes=[pltpu.VMEM((D_MODEL, FF_SHARD), F32)])(h, dgu)


def ffn_wgrad_down(a, dx, name, ts=1024):
    s = dx.shape[0]
    ns = s // ts

    def body(a_ref, dx_ref, o_ref, acc_ref):
        ss = pl.program_id(1)

        @pl.when(ss == 0)
        def _():
            acc_ref[...] = jnp.zeros_like(acc_ref)

        acc_ref[...] += _dot_tn(a_ref[...], (0.5 * dx_ref[...]).astype(BF16))

        @pl.when(ss == ns - 1)
        def _():
            o_ref[...] = acc_ref[...].astype(BF16)

    return _call(body, name=name, out_shape=_sds((N_CHIP, FF_SHARD, D_MODEL), BF16), grid=(N_CHIP, ns),
                 in_specs=[pl.BlockSpec((None, ts, FF_SHARD), lambda j, ss: (j, ss, 0)),
                           pl.BlockSpec((ts, D_MODEL), lambda j, ss: (ss, 0))],
                 out_specs=pl.BlockSpec((None, FF_SHARD, D_MODEL), lambda j, ss: (j, 0, 0)),
                 scratch_shapes=[pltpu.VMEM((FF_SHARD, D_MODEL), F32)])(a, dx)


def ffn_forward(x, gain, w704, get_w1024, tag, rider=None):
    h = rms_fwd(x, gain, f"{tag}_rms")
    res = ffn_up(h, w704, 0, 1, f"{tag}_up", rider=rider)
    (g, u, a), rode = res if rider is not None else (res, None)
    y = ffn_down(a, get_w1024(rode), 0, x, f"{tag}_down")
    return y, (h, g, u, a), rode


def ffn_backward(dy, x, gain, w704, w1024, saved, tag, ride_down=None, ride_in=None):
    h, g, u, a = saved
    d_wd = ffn_wgrad_down(a, dy, f"{tag}_dwd")
    if ride_down is not None:
        (dg, du), d_wd = ffn_bwd_hidden(dy, w1024, 0, g, u, f"{tag}_dhid", rider=ride_down(d_wd))
    else:
        dg, du = ffn_bwd_hidden(dy, w1024, 0, g, u, f"{tag}_dhid")
    d_win = jnp.concatenate([ffn_wgrad_in(h, dg, f"{tag}_dwg"), ffn_wgrad_in(h, du, f"{tag}_dwu")], axis=1)
    if ride_in is not None:
        dh, d_win = ffn_bwd_input(dg, du, w704, 0, 1, f"{tag}_dh", rider=ride_in(d_win))
    else:
        dh = ffn_bwd_input(dg, du, w704, 0, 1, f"{tag}_dh")
    dx, d_gain = rms_bwd([dh], x, gain, dy, f"{tag}_drms")
    return dx, d_gain, d_win, d_wd


def _alibi_slope(head):
    return float(2.0 ** (-ALIBI_MAX_EXP * (head + 1) / N_ATTN_HEADS))


def _head_norm(t, gain_pair, first):
    sa, sb = _pair_sum(t * t, first)
    r = jnp.where(first, lax.rsqrt(sa * (1.0 / HD) + EPS), lax.rsqrt(sb * (1.0 / HD) + EPS))
    return t * r * gain_pair, r


def qk_norm_fwd(p, q_gain, k_gain, name):
    s = p.shape[0]

    def fn(i, q_ref, k_ref, qg_ref, kg_ref, qn_ref, kn_ref):
        first = _lane_first_half((q_ref.shape[0], 2 * HD))
        for src, g_ref, dst in ((q_ref, qg_ref, qn_ref), (k_ref, kg_ref, kn_ref)):
            for pr in range(ATTN_QKV // (2 * HD)):
                cols = slice(pr * 2 * HD, (pr + 1) * 2 * HD)
                y, _ = _head_norm(src[:, cols].astype(F32), g_ref[...], first)
                dst[:, cols] = y.astype(BF16)

    return _rowwise(name, fn, [(p, ATTN_QKV, 0), (p, ATTN_QKV, 1)], [q_gain, k_gain],
                    [((s, ATTN_QKV), BF16), ((s, ATTN_QKV), BF16)])


def qk_norm_bwd(p, dqs, dks, q_gain, k_gain, name):
    s = p.shape[0]
    pairs_per_pattern = GROUP_W // (2 * HD)

    def fn(i, q_ref, k_ref, dq0, dq1, dq2, dk0, dk1, dk2, qg_ref, kg_ref, dq_ref, dk_ref, dqg_ref, dkg_ref):
        first = _lane_first_half((q_ref.shape[0], 2 * HD))

        @pl.when(i == 0)
        def _():
            dqg_ref[...] = jnp.zeros_like(dqg_ref)
            dkg_ref[...] = jnp.zeros_like(dkg_ref)

        for src, d_refs, g_ref, dst, dg_ref in ((q_ref, (dq0, dq1, dq2), qg_ref, dq_ref, dqg_ref),
                                                (k_ref, (dk0, dk1, dk2), kg_ref, dk_ref, dkg_ref)):
            for pr in range(ATTN_QKV // (2 * HD)):
                cols = slice(pr * 2 * HD, (pr + 1) * 2 * HD)
                t = src[:, cols].astype(F32)
                sa, sb = _pair_sum(t * t, first)
                r = jnp.where(first, lax.rsqrt(sa * (1.0 / HD) + EPS), lax.rsqrt(sb * (1.0 / HD) + EPS))
                xn = t * r
                within = (pr % pairs_per_pattern) * 2 * HD
                dy = d_refs[pr // pairs_per_pattern][:, within:within + 2 * HD]
                dg_ref[:, cols] += jnp.sum(dy * xn, axis=0, keepdims=True)
                dxn = dy * g_ref[...]
                ma, mb = _pair_sum(dxn * xn, first)
                mean = jnp.where(first, ma, mb) * (1.0 / HD)
                dst[:, cols] = (r * (dxn - xn * mean)).astype(BF16)

    return _rowwise(name, fn, [(p, ATTN_QKV, 0), (p, ATTN_QKV, 1)] + list(dqs) + list(dks), [q_gain, k_gain],
                    [((s, ATTN_QKV), BF16), ((s, ATTN_QKV), BF16)], [((1, ATTN_QKV), F32), ((1, ATTN_QKV), F32)])


def _to_streams(a, d):
    if d == 1:
        return a
    s, c = a.shape
    return a.reshape(s // d, d, c).transpose(1, 0, 2).reshape(s, c)


def _from_streams(a, d):
    if d == 1:
        return a
    s, c = a.shape
    return a.reshape(d, s // d, c).transpose(1, 0, 2).reshape(s, c)


def _attn_masks():
    row = lax.broadcasted_iota(jnp.int32, (BLK, BLK), 0)
    col = lax.broadcasted_iota(jnp.int32, (BLK, BLK), 1)
    rel_diag = row - col
    rel_prev = rel_diag + BLK
    return rel_diag, rel_prev


def attn_fwd(q, k, v, pattern, name, tq=512):
    s = q.shape[0]
    d = ATTN_DILATIONS[pattern]
    blocks_per_stream = (s // d) // BLK
    nsb = tq // BLK

    def body(q_ref, k_ref, v_ref, kp_ref, vp_ref, o_ref, l_ref):
        i = pl.program_id(0)
        rel_diag, rel_prev = _attn_masks()
        first = _lane_first_half((BLK, 2 * HD))
        rd_f = (rel_diag * d).astype(F32)
        rp_f = (rel_prev * d).astype(F32)
        for sb in range(nsb):
            rows = slice(sb * BLK, (sb + 1) * BLK)
            has_prev = ((i * nsb + sb) % blocks_per_stream != 0).astype(jnp.int32)
            m_diag = rel_diag >= 0
            m_prev = (rel_prev + (1 - has_prev) * (4 * BLK)) <= BLK
            for pr in range(GROUP_W // (2 * HD)):
                cols = slice(pr * 2 * HD, (pr + 1) * 2 * HD)
                qp = q_ref[rows, cols]
                kc, vc = k_ref[rows, cols], v_ref[rows, cols]
                if sb == 0:
                    kp, vp = kp_ref[:, cols], vp_ref[:, cols]
                else:
                    prows = slice((sb - 1) * BLK, sb * BLK)
                    kp, vp = k_ref[prows, cols], v_ref[prows, cols]
                outs, lses = [], []
                for e in range(2):
                    slope = _alibi_slope(pattern * HEADS_PER_PATTERN + 2 * pr + e)
                    qm = jnp.where(first if e == 0 else jnp.logical_not(first), qp, jnp.zeros_like(qp))
                    s1 = jnp.where(m_diag, _dot_nt(qm, kc) * 0.125 - slope * rd_f, NEG)
                    s0 = jnp.where(m_prev, _dot_nt(qm, kp) * 0.125 - slope * rp_f, NEG)
                    m = jnp.maximum(jnp.max(s1, axis=-1, keepdims=True), jnp.max(s0, axis=-1, keepdims=True))
                    p1 = jnp.exp(s1 - m)
                    p0 = jnp.exp(s0 - m)
                    l = jnp.sum(p1, axis=-1, keepdims=True) + jnp.sum(p0, axis=-1, keepdims=True)
                    inv = 1.0 / l
                    outs.append(_dot((p1 * inv).astype(BF16), vc) + _dot((p0 * inv).astype(BF16), vp))
                    lses.append(m + jnp.log(l))
                o_ref[rows, cols] = jnp.where(first, outs[0], outs[1])
                l_ref[rows, cols] = jnp.where(first, lses[0], lses[1])

    cur = pl.BlockSpec((tq, GROUP_W), lambda i: (i, 0))
    prev = pl.BlockSpec((BLK, GROUP_W), lambda i: (jnp.maximum(i * nsb - 1, 0), 0))
    return _call(body, name=name, out_shape=[_sds((s, GROUP_W), F32), _sds((s, GROUP_W), F32)], grid=(s // tq,),
                 in_specs=[cur, cur, cur, prev, prev], out_specs=[cur, cur])(q, k, v, k, v)


def attn_merge_fwd(os_, lses, name):
    s = os_[0].shape[0]

    def fn(i, o0, o1, o2, l0, l1, l2, out_ref):
        m = jnp.maximum(jnp.maximum(l0[...], l1[...]), l2[...])
        e0, e1, e2 = jnp.exp(l0[...] - m), jnp.exp(l1[...] - m), jnp.exp(l2[...] - m)
        inv = 1.0 / (e0 + e1 + e2)
        out_ref[...] = ((e0 * inv) * o0[...] + (e1 * inv) * o1[...] + (e2 * inv) * o2[...]).astype(BF16)

    return _rowwise(name, fn, list(os_) + list(lses), [], [((s, GROUP_W), BF16)])[0]


def attn_merge_bwd(d_out, os_, lses, name):
    s = d_out.shape[0]

    def fn(i, do_ref, o0, o1, o2, l0, l1, l2, d0, d1, d2, c0, c1, c2):
        first = _lane_first_half((do_ref.shape[0], 2 * HD))
        m = jnp.maximum(jnp.maximum(l0[...], l1[...]), l2[...])
        e0, e1, e2 = jnp.exp(l0[...] - m), jnp.exp(l1[...] - m), jnp.exp(l2[...] - m)
        inv = 1.0 / (e0 + e1 + e2)
        w0, w1, w2 = e0 * inv, e1 * inv, e2 * inv
        do = do_ref[...]
        prod = do * (w0 * o0[...] + w1 * o1[...] + w2 * o2[...])
        for pr in range(GROUP_W // (2 * HD)):
            cols = slice(pr * 2 * HD, (pr + 1) * 2 * HD)
            ta, tb = _pair_sum(prod[:, cols], first)
            t = jnp.where(first, ta, tb)
            for w, c_ref in ((w0, c0), (w1, c1), (w2, c2)):
                c_ref[:, cols] = w[:, cols] * t
        for w, d_ref in ((w0, d0), (w1, d1), (w2, d2)):
            d_ref[...] = (w * do).astype(BF16)

    shp = (s, GROUP_W)
    return _rowwise(name, fn, [d_out] + list(os_) + list(lses), [],
                    [(shp, BF16)] * 3 + [(shp, F32)] * 3)


def attn_bwd(q, k, v, d_o, cterm, lse, pattern, name, tq=512):
    s = q.shape[0]
    d = ATTN_DILATIONS[pattern]
    blocks_per_stream = (s // d) // BLK
    nsb = tq // BLK
    n_blocks = s // BLK

    def body(q_ref, k_ref, v_ref, do_ref, c_ref, l_ref, kp_ref, vp_ref, qn_ref, don_ref, cn_ref, ln_ref,
             dq_ref, dk_ref, dv_ref):
        i = pl.program_id(0)
        rel_diag, rel_prev = _attn_masks()
        first = _lane_first_half((BLK, 2 * HD))
        second = jnp.logical_not(first)
        rd_f = (rel_diag * d).astype(F32)
        rp_f = (rel_prev * d).astype(F32)
        m_diag = rel_diag >= 0
        dq_ref[...] = jnp.zeros_like(dq_ref)
        dk_ref[...] = jnp.zeros_like(dk_ref)
        dv_ref[...] = jnp.zeros_like(dv_ref)

        def pair(qp, dop, cp, lp, kp, vp, rel_f, mask):
            dq = dk = dv = None
            for e in range(2):
                lanes = first if e == 0 else second
                slope = slopes[e]
                qm = jnp.where(lanes, qp, jnp.zeros_like(qp))
                dom = jnp.where(lanes, dop, jnp.zeros_like(dop))
                km = jnp.where(lanes, kp, jnp.zeros_like(kp))
                sc = jnp.where(mask, _dot_nt(qm, kp) * 0.125 - slope * rel_f, NEG)
                pm = jnp.exp(sc - lp[:, e * HD:e * HD + 1])
                dl = pm * (_dot_nt(dom, vp) - cp[:, e * HD:e * HD + 1])
                dl16 = dl.astype(BF16)
                t_dq = _dot(dl16, km)
                t_dk = _dot_tn(dl16, qm)
                t_dv = _dot_tn(pm.astype(BF16), dom)
                dq = t_dq if dq is None else dq + t_dq
                dk = t_dk if dk is None else dk + t_dk
                dv = t_dv if dv is None else dv + t_dv
            return dq * 0.125, dk * 0.125, dv

        for pr in range(GROUP_W // (2 * HD)):
            cols = slice(pr * 2 * HD, (pr + 1) * 2 * HD)
            slopes = [_alibi_slope(pattern * HEADS_PER_PATTERN + 2 * pr + e) for e in range(2)]
            for sb in range(nsb + 1):
                gb = i * nsb + sb
                if sb < nsb:
                    rows = slice(sb * BLK, (sb + 1) * BLK)
                    qp, dop, cp, lp = q_ref[rows, cols], do_ref[rows, cols], c_ref[rows, cols], l_ref[rows, cols]
                else:
                    qp, dop, cp, lp = qn_ref[:, cols], don_ref[:, cols], cn_ref[:, cols], ln_ref[:, cols]
                if sb < nsb:
                    dq1, dk1, dv1 = pair(qp, dop, cp, lp, k_ref[rows, cols], v_ref[rows, cols], rd_f, m_diag)
                    dq_ref[rows, cols] += dq1
                    dk_ref[rows, cols] += dk1
                    dv_ref[rows, cols] += dv1
                valid = jnp.logical_and(gb % blocks_per_stream != 0, gb < n_blocks).astype(jnp.int32)
                m_prev = jnp.logical_and(rel_prev <= BLK, (rel_prev + (1 - valid) * (4 * BLK)) <= BLK)
                if sb == 0:
                    kp, vp = kp_ref[:, cols], vp_ref[:, cols]
                else:
                    prows = slice((sb - 1) * BLK, sb * BLK)
                    kp, vp = k_ref[prows, cols], v_ref[prows, cols]
                dq0, dk0, dv0 = pair(qp, dop, cp, lp, kp, vp, rp_f, m_prev)
                if sb < nsb:
                    dq_ref[rows, cols] += dq0
                if sb > 0:
                    dk_ref[prows, cols] += dk0
                    dv_ref[prows, cols] += dv0

    cur = pl.BlockSpec((tq, GROUP_W), lambda i: (i, 0))
    prev = pl.BlockSpec((BLK, GROUP_W), lambda i: (jnp.maximum(i * nsb - 1, 0), 0))
    nxt = pl.BlockSpec((BLK, GROUP_W), lambda i: (jnp.minimum((i + 1) * nsb, n_blocks - 1), 0))
    shp = _sds((s, GROUP_W), F32)
    return _call(body, name=name, out_shape=[shp, shp, shp], grid=(s // tq,),
                 in_specs=[cur] * 6 + [prev, prev] + [nxt] * 4, out_specs=[cur, cur, cur])(
                     q, k, v, d_o, cterm, lse, k, v, q, d_o, cterm, lse)


def _band_constants(d):
    row = lax.broadcasted_iota(jnp.int32, (2 * BLK, 2 * BLK), 0)
    col = lax.broadcasted_iota(jnp.int32, (2 * BLK, 2 * BLK), 1)
    rel = BLK + jnp.where(row >= BLK, row - BLK, row) - col
    band = jnp.logical_and(rel >= 0, rel <= BLK)
    return (rel * d).astype(F32), band, (col >= BLK).astype(jnp.int32)


def _stack_heads(x, first):
    zero = jnp.zeros_like(x)
    return jnp.concatenate([jnp.where(first, x, zero), jnp.where(first, zero, x)], axis=0)


def _unstack_heads(x2, first):
    return jnp.where(first, x2[:BLK], x2[BLK:])


def _head_column(x):
    return jnp.concatenate([x[:, 0:1], x[:, HD:HD + 1]], axis=0)


def attn_fwd2(q, k, v, pattern, name, tq=512):
    s = q.shape[0]
    d = ATTN_DILATIONS[pattern]
    blocks_per_stream = (s // d) // BLK
    nsb = tq // BLK

    def body(q_ref, k_ref, v_ref, kp_ref, vp_ref, o_ref, l_ref):
        i = pl.program_id(0)
        rel_f, band, own = _band_constants(d)
        first = _lane_first_half((BLK, 2 * HD))
        upper = lax.broadcasted_iota(jnp.int32, (2 * BLK, 1), 0) < BLK
        for sb in range(nsb):
            rows = slice(sb * BLK, (sb + 1) * BLK)
            has_prev = ((i * nsb + sb) % blocks_per_stream != 0).astype(jnp.int32)
            mask = jnp.logical_and(band, (own + has_prev) > 0)
            for pr in range(GROUP_W // (2 * HD)):
                cols = slice(pr * 2 * HD, (pr + 1) * 2 * HD)
                if sb == 0:
                    kcat = jnp.concatenate([kp_ref[:, cols], k_ref[rows, cols]], axis=0)
                    vcat = jnp.concatenate([vp_ref[:, cols], v_ref[rows, cols]], axis=0)
                else:
                    both = slice((sb - 1) * BLK, (sb + 1) * BLK)
                    kcat, vcat = k_ref[both, cols], v_ref[both, cols]
                h0 = pattern * HEADS_PER_PATTERN + 2 * pr
                slope = jnp.where(upper, _alibi_slope(h0), _alibi_slope(h0 + 1))
                sc = _dot_nt(_stack_heads(q_ref[rows, cols], first), kcat) * 0.125 - slope * rel_f
                sc = jnp.where(mask, sc, NEG)
                m = jnp.max(sc, axis=-1, keepdims=True)
                p = jnp.exp(sc - m)
                l = jnp.sum(p, axis=-1, keepdims=True)
                o2 = _dot((p * (1.0 / l)).astype(BF16), vcat)
                o_ref[rows, cols] = _unstack_heads(o2, first)
                lse = m + jnp.log(l)
                l_ref[rows, cols] = jnp.where(first, lse[:BLK], lse[BLK:])

    cur = pl.BlockSpec((tq, GROUP_W), lambda i: (i, 0))
    prev = pl.BlockSpec((BLK, GROUP_W), lambda i: (jnp.maximum(i * nsb - 1, 0), 0))
    return _call(body, name=name, out_shape=[_sds((s, GROUP_W), F32), _sds((s, GROUP_W), F32)], grid=(s // tq,),
                 in_specs=[cur, cur, cur, prev, prev], out_specs=[cur, cur])(q, k, v, k, v)


def attn_bwd2(q, k, v, d_o, cterm, lse, pattern, name, tq=512):
    s = q.shape[0]
    d = ATTN_DILATIONS[pattern]
    blocks_per_stream = (s // d) // BLK
    nsb = tq // BLK
    n_blocks = s // BLK

    def body(q_ref, k_ref, v_ref, do_ref, c_ref, l_ref, kp_ref, vp_ref, qn_ref, kn_ref, vn_ref, don_ref, cn_ref,
             ln_ref, dq_ref, dk_ref, dv_ref):
        i = pl.program_id(0)
        rel_f, band, own = _band_constants(d)
        first = _lane_first_half((BLK, 2 * HD))
        upper = lax.broadcasted_iota(jnp.int32, (2 * BLK, 1), 0) < BLK
        dk_ref[...] = jnp.zeros_like(dk_ref)
        dv_ref[...] = jnp.zeros_like(dv_ref)
        for sb in range(nsb + 1):
            gb = i * nsb + sb
            rows = slice(sb * BLK, (sb + 1) * BLK)
            before = slice((sb - 1) * BLK, sb * BLK)
            inside = (gb < n_blocks).astype(jnp.int32)
            has_prev = jnp.logical_and(gb % blocks_per_stream != 0, gb < n_blocks).astype(jnp.int32)
            mask = jnp.logical_and(band, (own * inside + has_prev) > 0)
            for pr in range(GROUP_W // (2 * HD)):
                cols = slice(pr * 2 * HD, (pr + 1) * 2 * HD)
                if sb == 0:
                    kcat = jnp.concatenate([kp_ref[:, cols], k_ref[rows, cols]], axis=0)
                    vcat = jnp.concatenate([vp_ref[:, cols], v_ref[rows, cols]], axis=0)
                elif sb == nsb:
                    kcat = jnp.concatenate([k_ref[before, cols], kn_ref[:, cols]], axis=0)
                    vcat = jnp.concatenate([v_ref[before, cols], vn_ref[:, cols]], axis=0)
                else:
                    both = slice((sb - 1) * BLK, (sb + 1) * BLK)
                    kcat, vcat = k_ref[both, cols], v_ref[both, cols]
                if sb < nsb:
                    qp, dop, cp, lp = q_ref[rows, cols], do_ref[rows, cols], c_ref[rows, cols], l_ref[rows, cols]
                else:
                    qp, dop, cp, lp = qn_ref[:, cols], don_ref[:, cols], cn_ref[:, cols], ln_ref[:, cols]
                h0 = pattern * HEADS_PER_PATTERN + 2 * pr
                slope = jnp.where(upper, _alibi_slope(h0), _alibi_slope(h0 + 1))
                q2 = _stack_heads(qp, first)
                do2 = _stack_heads(dop, first)
                sc = jnp.where(mask, _dot_nt(q2, kcat) * 0.125 - slope * rel_f, NEG)
                pm = jnp.exp(sc - _head_column(lp))
                dl = (pm * (_dot_nt(do2, vcat) - _head_column(cp))).astype(BF16)
                if sb < nsb:
                    dq_ref[rows, cols] = _unstack_heads(_dot(dl, kcat), first) * 0.125
                dk2 = _dot_tn(dl, q2) * 0.125
                dv2 = _dot_tn(pm.astype(BF16), do2)
                if sb > 0:
                    dk_ref[before, cols] += dk2[:BLK]
                    dv_ref[before, cols] += dv2[:BLK]
                if sb < nsb:
                    dk_ref[rows, cols] += dk2[BLK:]
                    dv_ref[rows, cols] += dv2[BLK:]

    cur = pl.BlockSpec((tq, GROUP_W), lambda i: (i, 0))
    prev = pl.BlockSpec((BLK, GROUP_W), lambda i: (jnp.maximum(i * nsb - 1, 0), 0))
    nxt = pl.BlockSpec((BLK, GROUP_W), lambda i: (jnp.minimum((i + 1) * nsb, n_blocks - 1), 0))
    shp = _sds((s, GROUP_W), F32)
    return _call(body, name=name, out_shape=[shp, shp, shp], grid=(s // tq,),
                 in_specs=[cur] * 6 + [prev, prev] + [nxt] * 6, out_specs=[cur, cur, cur])(
                     q, k, v, d_o, cterm, lse, k, v, q, k, v, d_o, cterm, lse)


HALO = 16
CONV_TQ = 512


def conv_fwd(p, w, b, name):
    s = p.shape[0]
    tq = CONV_TQ
    ncol = SSD_CONV_DIM // GROUP_W
    cb0 = COL_XBC // GROUP_W

    def body(u_ref, up_ref, w_ref, b_ref, c_ref, xc_ref):
        i = pl.program_id(0)
        prev = up_ref[...].astype(F32) * (i > 0).astype(F32)
        ext = jnp.concatenate([prev, u_ref[...].astype(F32)], axis=0)
        acc = b_ref[...] + w_ref[SSD_CONV - 1:SSD_CONV, :] * ext[HALO:HALO + tq]
        for kk in range(SSD_CONV - 1):
            off = HALO - (SSD_CONV - 1) + kk
            acc += w_ref[kk:kk + 1, :] * ext[off:off + tq]
        c_ref[...] = acc.astype(BF16)
        xc_ref[...] = (acc * _sigmoid(acc)).astype(BF16)

    cur_in = pl.BlockSpec((tq, GROUP_W), lambda i, j: (i, cb0 + j))
    prev_in = pl.BlockSpec((HALO, GROUP_W), lambda i, j: (jnp.maximum(i * (tq // HALO) - 1, 0), cb0 + j))
    cur_out = pl.BlockSpec((tq, GROUP_W), lambda i, j: (i, j))
    shp = _sds((s, SSD_CONV_DIM), BF16)
    return _call(body, name=name, out_shape=[shp, shp], grid=(s // tq, ncol),
                 in_specs=[cur_in, prev_in, pl.BlockSpec((SSD_CONV, GROUP_W), lambda i, j: (0, j)),
                           pl.BlockSpec((1, GROUP_W), lambda i, j: (0, j))],
                 out_specs=[cur_out, cur_out])(p, p, w, b)


def conv_bwd(p, cpre, dxc, w, name):
    s = p.shape[0]
    tq = CONV_TQ
    ncol = SSD_CONV_DIM // GROUP_W
    cb0 = COL_XBC // GROUP_W
    nt = s // tq

    def body(u_ref, up_ref, c_ref, cn_ref, d_ref, dn_ref, w_ref, du_ref, dw_ref, db_ref):
        i = pl.program_id(1)

        def dpre(c16, dx):
            c = c16.astype(F32)
            sg = _sigmoid(c)
            return dx * (sg * (1.0 + c * (1.0 - sg)))

        dc = dpre(c_ref[...], d_ref[...])
        dcn = dpre(cn_ref[...], dn_ref[...]) * (i < nt - 1).astype(F32)
        dext = jnp.concatenate([dc, dcn], axis=0)
        prev = up_ref[...].astype(F32) * (i > 0).astype(F32)
        uext = jnp.concatenate([prev, u_ref[...].astype(F32)], axis=0)

        @pl.when(i == 0)
        def _():
            dw_ref[...] = jnp.zeros_like(dw_ref)
            db_ref[...] = jnp.zeros_like(db_ref)

        du = w_ref[SSD_CONV - 1:SSD_CONV, :] * dc
        for kk in range(SSD_CONV - 1):
            sh = SSD_CONV - 1 - kk
            du += w_ref[kk:kk + 1, :] * dext[sh:sh + tq]
        du_ref[...] = du.astype(BF16)
        for kk in range(SSD_CONV):
            off = HALO - (SSD_CONV - 1) + kk
            dw_ref[kk:kk + 1, :] += jnp.sum(dc * uext[off:off + tq], axis=0, keepdims=True)
        db_ref[...] += jnp.sum(dc, axis=0, keepdims=True)

    hb = tq // HALO
    cur_p = pl.BlockSpec((tq, GROUP_W), lambda j, i: (i, cb0 + j))
    prev_p = pl.BlockSpec((HALO, GROUP_W), lambda j, i: (jnp.maximum(i * hb - 1, 0), cb0 + j))
    cur = pl.BlockSpec((tq, GROUP_W), lambda j, i: (i, j))
    nxt = pl.BlockSpec((HALO, GROUP_W), lambda j, i: (jnp.minimum((i + 1) * hb, s // HALO - 1), j))
    return _call(body, name=name,
                 out_shape=[_sds((s, SSD_CONV_DIM), BF16), _sds((8, SSD_CONV_DIM), F32), _sds((1, SSD_CONV_DIM), F32)],
                 grid=(ncol, nt),
                 in_specs=[cur_p, prev_p, cur, nxt, cur, nxt, pl.BlockSpec((SSD_CONV, GROUP_W), lambda j, i: (0, j))],
                 out_specs=[cur, pl.BlockSpec((8, GROUP_W), lambda j, i: (0, j)),
                            pl.BlockSpec((1, GROUP_W), lambda j, i: (0, j))])(p, p, cpre, cpre, dxc, dxc, w)


def _softplus(x):
    return jnp.maximum(x, 0.0) + jnp.log(1.0 + jnp.exp(-jnp.abs(x)))


def _ssd_decays(dtr_ref, dtrt_ref, bias_ref, biast_ref, alog_ref, alogt_ref):
    row = lax.broadcasted_iota(jnp.int32, (BLK, BLK), 0)
    col = lax.broadcasted_iota(jnp.int32, (BLK, BLK), 1)
    lower = (row >= col).astype(F32)
    upper = (row <= col).astype(F32)
    dtb = dtr_ref[...] + bias_ref[...]
    dt = _softplus(dtb)
    a = dt * (-jnp.exp(alog_ref[...]))
    cs = _dot_hi(lower, a)
    a_t = _softplus(dtrt_ref[...] + biast_ref[...]) * (-jnp.exp(alogt_ref[...]))
    cs_t = _dot_hi(a_t, upper)
    return dtb, dt, cs, cs_t, row, col, upper


def ssd_fwd(p, xc, dtg, dtg_t, params, gn, name):
    s = p.shape[0]
    nc = s // BLK
    bias, bias_t, alog, alog_t, dskip = params

    def body(xs_ref, b_ref, c_ref, z_ref, dtr_ref, dtrt_ref, bias_ref, biast_ref, alog_ref, alogt_ref, dsk_ref,
             gn_ref, y_ref, sin_ref, hp_ref, h_ref):
        c_idx = pl.program_id(1)

        @pl.when(c_idx == 0)
        def _():
            h_ref[...] = jnp.zeros_like(h_ref)

        _, dt, cs, cs_t, row, col, _ = _ssd_decays(dtr_ref, dtrt_ref, bias_ref, biast_ref, alog_ref, alogt_ref)
        first = _lane_first_half((BLK, 2 * HD))
        first_row = _lane_first_half((1, 2 * HD))
        tril = row >= col
        b16, c16 = b_ref[...], c_ref[...]
        cb = _dot_nt(c16, b16)
        ys = []
        for pr in range(GROUP_W // (2 * HD)):
            cols = slice(pr * 2 * HD, (pr + 1) * 2 * HD)
            ha, hb = 2 * pr, 2 * pr + 1
            xs = xs_ref[:, cols].astype(F32)
            dt_pair = jnp.where(first, dt[:, ha:ha + 1], dt[:, hb:hb + 1])
            xt = xs * dt_pair
            xt16 = xt.astype(BF16)
            y_heads = []
            for h in (ha, hb):
                decay = jnp.exp(jnp.where(tril, cs[:, h:h + 1] - cs_t[h:h + 1, :], NEG))
                y_heads.append(_dot((cb * decay).astype(BF16), xt16))
            y_diag = jnp.where(first, y_heads[0], y_heads[1])
            hstate = h_ref[pr]
            hp_ref[pr] = hstate
            e_pair = jnp.where(first, jnp.exp(cs[:, ha:ha + 1]), jnp.exp(cs[:, hb:hb + 1]))
            y_off = e_pair * _dot(c16, hstate.astype(BF16))
            tot_a, tot_b = cs[BLK - 1:BLK, ha:ha + 1], cs[BLK - 1:BLK, hb:hb + 1]
            f_pair = jnp.where(first, jnp.exp(tot_a - cs[:, ha:ha + 1]), jnp.exp(tot_b - cs[:, hb:hb + 1]))
            new = _dot_tn(b16, (f_pair * xt).astype(BF16))
            dec = jnp.where(first_row, jnp.exp(tot_a), jnp.exp(tot_b))
            h_ref[pr] = dec * hstate + new
            d_pair = jnp.where(first_row, dsk_ref[:, ha:ha + 1], dsk_ref[:, hb:hb + 1])
            ys.append(y_diag + y_off + xs * d_pair)
        y = jnp.concatenate(ys, axis=1)
        y_ref[...] = y
        zv = z_ref[...].astype(F32)
        yz = y * (zv * _sigmoid(zv))
        r = lax.rsqrt(jnp.mean(yz * yz, axis=-1, keepdims=True) + EPS)
        sin_ref[...] = (yz * r * gn_ref[...]).astype(BF16)

    nb0 = SSD_INNER // BLK
    gparam = pl.BlockSpec((None, 1, 8), lambda g, c: (g, 0, 0))
    gparam_t = pl.BlockSpec((None, 8, 1), lambda g, c: (g, 0, 0))
    return _call(
        body, name=name,
        out_shape=[_sds((s, SSD_INNER), F32), _sds((s, SSD_INNER), BF16),
                   _sds((SSD_GROUPS, nc, 4, BLK, 2 * HD), F32)],
        grid=(SSD_GROUPS, nc),
        in_specs=[pl.BlockSpec((BLK, GROUP_W), lambda g, c: (c, g)),
                  pl.BlockSpec((BLK, BLK), lambda g, c: (c, nb0 + g)),
                  pl.BlockSpec((BLK, BLK), lambda g, c: (c, nb0 + SSD_GROUPS + g)),
                  pl.BlockSpec((BLK, GROUP_W), lambda g, c: (c, COL_Z // GROUP_W + g)),
                  pl.BlockSpec((None, BLK, 8), lambda g, c: (g, c, 0)),
                  pl.BlockSpec((None, 8, BLK), lambda g, c: (g, 0, c)),
                  gparam, gparam_t, gparam, gparam_t, gparam,
                  pl.BlockSpec((1, GROUP_W), lambda g, c: (0, g))],
        out_specs=[pl.BlockSpec((BLK, GROUP_W), lambda g, c: (c, g)),
                   pl.BlockSpec((BLK, GROUP_W), lambda g, c: (c, g)),
                   pl.BlockSpec((None, None, 4, BLK, 2 * HD), lambda g, c: (g, c, 0, 0, 0))],
        scratch_shapes=[pltpu.VMEM((4, BLK, 2 * HD), F32)],
    )(xc, xc, xc, p, dtg, dtg_t, bias, bias_t, alog, alog_t, dskip, gn)


def ssd_bwd(p, xc, y, d_sin, hprev, dtg, dtg_t, params, gn, name):
    s = p.shape[0]
    nc = s // BLK
    bias, bias_t, alog, alog_t, dskip = params

    def body(xs_ref, b_ref, c_ref, z_ref, y_ref, dsin_ref, hp_ref, dtr_ref, dtrt_ref, bias_ref,
             biast_ref, alog_ref, alogt_ref, dsk_ref, gn_ref,
             dxs_ref, db_ref, dc_ref, dz_ref, ddt_ref, da_ref, dbias_ref, ddsk_ref, dgn_ref, dh_ref):
        c_idx = pl.program_id(1)

        @pl.when(c_idx == 0)
        def _():
            dh_ref[...] = jnp.zeros_like(dh_ref)
            da_ref[...] = jnp.zeros_like(da_ref)
            dbias_ref[...] = jnp.zeros_like(dbias_ref)
            ddsk_ref[...] = jnp.zeros_like(ddsk_ref)
            dgn_ref[...] = jnp.zeros_like(dgn_ref)

        dtb, dt, cs, cs_t, row, col, upper = _ssd_decays(dtr_ref, dtrt_ref, bias_ref, biast_ref, alog_ref, alogt_ref)
        first = _lane_first_half((BLK, 2 * HD))
        second = jnp.logical_not(first)
        first_row = _lane_first_half((1, 2 * HD))
        tril = row >= col
        triu = row <= col
        last_row = lax.broadcasted_iota(jnp.int32, (BLK, 1), 0) == BLK - 1
        lane8 = lax.broadcasted_iota(jnp.int32, (BLK, 8), 1)

        yv = y_ref[...]
        zv = z_ref[...].astype(F32)
        sg = _sigmoid(zv)
        yz = yv * (zv * sg)
        r = lax.rsqrt(jnp.mean(yz * yz, axis=-1, keepdims=True) + EPS)
        yzn = yz * r
        dsn = dsin_ref[...]
        dgn_ref[...] += jnp.sum(dsn * yzn, axis=0, keepdims=True)
        dsn = dsn * gn_ref[...]
        dyz = r * (dsn - yzn * jnp.mean(dsn * yzn, axis=-1, keepdims=True))
        dy = dyz * (zv * sg)
        dz_ref[...] = (dyz * yv * (sg * (1.0 + zv * (1.0 - sg)))).astype(BF16)
        xs_all = xs_ref[...].astype(F32)
        ddsk_ref[...] += jnp.sum(dy * xs_all, axis=0, keepdims=True)

        b16, c16 = b_ref[...], c_ref[...]
        cb = _dot_nt(c16, b16)
        cb_t = _dot_nt(b16, c16)
        g_sum = jnp.zeros((BLK, BLK), F32)
        gt_sum = jnp.zeros((BLK, BLK), F32)
        dc_acc = jnp.zeros((BLK, BLK), F32)
        db_acc = jnp.zeros((BLK, BLK), F32)
        dcs_all = jnp.zeros((BLK, 8), F32)
        ddtx_all = jnp.zeros((BLK, 8), F32)
        for pr in range(GROUP_W // (2 * HD)):
            cols = slice(pr * 2 * HD, (pr + 1) * 2 * HD)
            heads = (2 * pr, 2 * pr + 1)
            xs = xs_all[:, cols]
            dy_p = dy[:, cols]
            dt_pair = jnp.where(first, dt[:, heads[0]:heads[0] + 1], dt[:, heads[1]:heads[1] + 1])
            xt = xs * dt_pair
            xt16 = xt.astype(BF16)
            hstate = hp_ref[pr]
            h16 = hstate.astype(BF16)
            dhn = dh_ref[pr]
            dhn16 = dhn.astype(BF16)
            d_xt = jnp.zeros((BLK, 2 * HD), F32)
            dcs = []
            for e, h in enumerate(heads):
                lanes = first if e == 0 else second
                cs_c, cs_r = cs[:, h:h + 1], cs_t[h:h + 1, :]
                decay = jnp.exp(jnp.where(tril, cs_c - cs_r, NEG))
                decay_t = jnp.exp(jnp.where(triu, cs_r - cs_c, NEG))
                dym16 = jnp.where(lanes, dy_p, 0.0).astype(BF16)
                d_m = _dot_nt(dym16, xt16)
                d_mt = _dot_nt(xt16, dym16)
                d_xt += _dot((cb_t * decay_t).astype(BF16), dym16)
                gm = d_m * decay
                gmt = d_mt * decay_t
                g_sum += gm
                gt_sum += gmt
                dcs.append(jnp.sum(gm * cb, axis=-1, keepdims=True) - jnp.sum(gmt * cb_t, axis=-1, keepdims=True))
            exp_cs = [jnp.exp(cs[:, h:h + 1]) for h in heads]
            tots = [cs[BLK - 1:BLK, h:h + 1] for h in heads]
            f_col = [jnp.exp(tots[e] - cs[:, h:h + 1]) for e, h in enumerate(heads)]
            e_pair = jnp.where(first, exp_cs[0], exp_cs[1])
            f_pair = jnp.where(first, f_col[0], f_col[1])
            dec = [jnp.exp(t) for t in tots]
            dec_pair = jnp.where(first_row, dec[0], dec[1])
            edy = e_pair * dy_p
            edy16 = edy.astype(BF16)
            y_off = e_pair * _dot(c16, h16)
            oa, ob = _pair_sum(dy_p * y_off, first)
            dc_acc += _dot_nt(edy16, h16)
            dh_prev = _dot_tn(c16, edy16)
            zmat = _dot(b16, dhn16)
            d_xt += f_pair * zmat
            fa, fb = _pair_sum(zmat * xt, first)
            ta, tb = fa * f_col[0], fb * f_col[1]
            hh = dhn * hstate
            ha_sum = jnp.sum(jnp.sum(jnp.where(first, hh, 0.0), axis=-1, keepdims=True), axis=0, keepdims=True)
            hb_sum = jnp.sum(jnp.sum(hh, axis=-1, keepdims=True), axis=0, keepdims=True) - ha_sum
            dtot_a = jnp.sum(ta, axis=0, keepdims=True) + ha_sum * dec[0]
            dtot_b = jnp.sum(tb, axis=0, keepdims=True) + hb_sum * dec[1]
            dcs[0] = dcs[0] + oa - ta + jnp.where(last_row, dtot_a, 0.0)
            dcs[1] = dcs[1] + ob - tb + jnp.where(last_row, dtot_b, 0.0)
            db_acc += _dot_nt((f_pair * xt).astype(BF16), dhn16)
            dh_ref[pr] = dh_prev + dec_pair * dhn
            d_pair = jnp.where(first_row, dsk_ref[:, heads[0]:heads[0] + 1], dsk_ref[:, heads[1]:heads[1] + 1])
            dxs_ref[:, cols] = dy_p * d_pair + d_xt * dt_pair
            xa, xb = _pair_sum(d_xt * xs, first)
            for e, h in enumerate(heads):
                dcs_all = jnp.where(lane8 == h, dcs[e], dcs_all)
                ddtx_all = jnp.where(lane8 == h, (xa, xb)[e], ddtx_all)

        dc_ref[...] = dc_acc + _dot(g_sum.astype(BF16), b16)
        db_ref[...] = db_acc + _dot(gt_sum.astype(BF16), c16)
        d_a = _dot_hi(upper, dcs_all)
        a_neg = -jnp.exp(alog_ref[...])
        ddt = ddtx_all + d_a * a_neg
        da_ref[...] += jnp.sum(d_a * dt, axis=0, keepdims=True)
        ddtr = ddt * _sigmoid(dtb)
        ddt_ref[...] = ddtr
        dbias_ref[...] += jnp.sum(ddtr, axis=0, keepdims=True)

    nb0 = SSD_INNER // BLK
    rc = lambda c: nc - 1 - c
    gparam = pl.BlockSpec((None, 1, 8), lambda g, c: (g, 0, 0))
    gparam_t = pl.BlockSpec((None, 8, 1), lambda g, c: (g, 0, 0))
    wide = pl.BlockSpec((BLK, GROUP_W), lambda g, c: (rc(c), g))
    narrow = pl.BlockSpec((BLK, BLK), lambda g, c: (rc(c), g))
    return _call(
        body, name=name,
        out_shape=[_sds((s, SSD_INNER), F32), _sds((s, GROUP_W), F32), _sds((s, GROUP_W), F32),
                   _sds((s, SSD_INNER), BF16), _sds((SSD_GROUPS, s, 8), F32),
                   _sds((SSD_GROUPS, 1, 8), F32), _sds((SSD_GROUPS, 1, 8), F32),
                   _sds((SSD_GROUPS, 1, GROUP_W), F32), _sds((1, SSD_INNER), F32)],
        grid=(SSD_GROUPS, nc),
        in_specs=[wide,
                  pl.BlockSpec((BLK, BLK), lambda g, c: (rc(c), nb0 + g)),
                  pl.BlockSpec((BLK, BLK), lambda g, c: (rc(c), nb0 + SSD_GROUPS + g)),
                  pl.BlockSpec((BLK, GROUP_W), lambda g, c: (rc(c), COL_Z // GROUP_W + g)),
                  wide, wide,
                  pl.BlockSpec((None, None, 4, BLK, 2 * HD), lambda g, c: (g, rc(c), 0, 0, 0)),
                  pl.BlockSpec((None, BLK, 8), lambda g, c: (g, rc(c), 0)),
                  pl.BlockSpec((None, 8, BLK), lambda g, c: (g, 0, rc(c))),
                  gparam, gparam_t, gparam, gparam_t, gparam,
                  pl.BlockSpec((1, GROUP_W), lambda g, c: (0, g))],
        out_specs=[wide, narrow, narrow, wide,
                   pl.BlockSpec((None, BLK, 8), lambda g, c: (g, rc(c), 0)),
                   gparam, gparam,
                   pl.BlockSpec((None, 1, GROUP_W), lambda g, c: (g, 0, 0)),
                   pl.BlockSpec((1, GROUP_W), lambda g, c: (0, g))],
        scratch_shapes=[pltpu.VMEM((4, BLK, 2 * HD), F32)],
    )(xc, xc, xc, p, y, d_sin, hprev, dtg, dtg_t, bias, bias_t, alog, alog_t, dskip, gn)


def merge_fwd(p, a, sbr, name, tm=512):
    s = p.shape[0]
    nj = D_MODEL // GROUP_W

    def body(ga_ref, gs_ref, a_ref, s_ref, o_ref):
        o_ref[...] = (_sigmoid(ga_ref[...].astype(F32)) * a_ref[...]
                      + _sigmoid(gs_ref[...].astype(F32)) * s_ref[...]).astype(BF16)

    blk = pl.BlockSpec((tm, GROUP_W), lambda i, j: (i, j))
    return _call(body, name=name, out_shape=_sds((s, D_MODEL), BF16), grid=(s // tm, nj),
                 in_specs=[pl.BlockSpec((tm, GROUP_W), lambda i, j: (i, COL_GA // GROUP_W + j)),
                           pl.BlockSpec((tm, GROUP_W), lambda i, j: (i, COL_GS // GROUP_W + j)), blk, blk],
                 out_specs=blk)(p, p, a, sbr)


def merge_bwd(p, a, sbr, dmerged, name, tm=512):
    s = p.shape[0]
    nj = D_MODEL // GROUP_W

    def body(ga_ref, gs_ref, a_ref, s_ref, dm_ref, da_ref, ds_ref, dga_ref, dgs_ref):
        dm = dm_ref[...]
        sa = _sigmoid(ga_ref[...].astype(F32))
        ss = _sigmoid(gs_ref[...].astype(F32))
        da_ref[...] = (dm * sa).astype(BF16)
        ds_ref[...] = (dm * ss).astype(BF16)
        dga_ref[...] = (dm * a_ref[...] * sa * (1.0 - sa)).astype(BF16)
        dgs_ref[...] = (dm * s_ref[...] * ss * (1.0 - ss)).astype(BF16)

    blk = pl.BlockSpec((tm, GROUP_W), lambda i, j: (i, j))
    shp = _sds((s, D_MODEL), BF16)
    return _call(body, name=name, out_shape=[shp] * 4, grid=(s // tm, nj),
                 in_specs=[pl.BlockSpec((tm, GROUP_W), lambda i, j: (i, COL_GA // GROUP_W + j)),
                           pl.BlockSpec((tm, GROUP_W), lambda i, j: (i, COL_GS // GROUP_W + j)), blk, blk, blk],
                 out_specs=[blk] * 4)(p, p, a, sbr, dmerged)


def _group_major(v):
    return v.reshape(SSD_GROUPS, 1, 8), v.reshape(SSD_GROUPS, 8, 1)


def mixer_forward(x, w, rider=None, later_weights=None):
    s = x.shape[0]
    h = rms_fwd(x, w["mix_norm"], "mix_rms")
    p = matmul_nn(h, w["w_in_main"], "mix_proj", BF16, tm=1024, tn=512, rider=rider)
    rode = None
    if rider is not None:
        p, rode = p
        w = dict(w, **later_weights(rode))
    dt_raw = matmul_nn(h, w["w_in_dt"], "mix_proj_dt", F32, tm=1024, tn=DT_PAD)
    qn, kn = qk_norm_fwd(p, w["q_gain"], w["k_gain"], "qk_norm")
    streams, os_, lses = [], [], []
    for g, d in enumerate(ATTN_DILATIONS):
        cols = slice(g * GROUP_W, (g + 1) * GROUP_W)
        qs, ks = _to_streams(qn[:, cols], d), _to_streams(kn[:, cols], d)
        vs = _to_streams(p[:, COL_V + g * GROUP_W:COL_V + (g + 1) * GROUP_W], d)
        o, lse = attn_fwd2(qs, ks, vs, g, f"attn_fwd{g}")
        streams.append((qs, ks, vs, lse))
        os_.append(_from_streams(o, d))
        lses.append(_from_streams(lse, d))
    attn_o = attn_merge_fwd(os_, lses, "attn_merge")
    cpre, xc = conv_fwd(p, w["conv_w"], w["conv_b"], "conv_fwd")
    dtg = dt_raw[:, :SSD_HEADS].reshape(s, SSD_GROUPS, 8).transpose(1, 0, 2)
    dtg_t = dtg.transpose(0, 2, 1)
    params = (*_group_major(w["dt_bias"]), *_group_major(w["a_log"]), _group_major(w["d_skip"])[0])
    y, s_in, hprev = ssd_fwd(p, xc, dtg, dtg_t, params, w["ssd_norm"], "ssd_fwd")
    a = matmul_nn(attn_o, w["w_attn_branch"], "attn_branch", F32, tm=1024, tn=512)
    sbr = matmul_nn(s_in, w["w_ssd_branch"], "ssd_branch", F32, tm=1024, tn=512)
    merged = merge_fwd(p, a, sbr, "merge")
    x_out = matmul_nn(merged, w["w_out"], "mix_out", F32, tm=1024, tn=512, res=x)
    saved = dict(h=h, p=p, streams=streams, os=os_, lses=lses, attn_o=attn_o, cpre=cpre, xc=xc, dtg=dtg,
                 dtg_t=dtg_t, params=params, y=y, s_in=s_in, hprev=hprev, a=a, sbr=sbr, merged=merged, w=w)
    return x_out, saved, rode


def mixer_backward(dx_out, x, sv, ride_early=None, ride_late=None):
    s = x.shape[0]
    p = sv["p"]
    w = sv["w"]
    g = {}
    dmerged = matmul_nt(dx_out, w["w_out"], "d_merged", F32, tm=1024, tn=512, tk=1024)
    g["w_out"] = matmul_tn(sv["merged"], dx_out, "dw_out", tn=512, ts=1024)
    da, ds, dga, dgs = merge_bwd(p, sv["a"], sv["sbr"], dmerged, "merge_bwd")
    g["w_attn_branch"] = matmul_tn(sv["attn_o"], da, "dw_attn_branch", tn=512, ts=1024)
    g["w_ssd_branch"] = matmul_tn(sv["s_in"], ds, "dw_ssd_branch", tn=512, ts=1024)
    d_attn_o = matmul_nt(da, w["w_attn_branch"], "d_attn_o", F32, tm=1024, tn=512, tk=1024)
    d_sin = matmul_nt(ds, w["w_ssd_branch"], "d_ssd_in", F32, tm=1024, tn=512, tk=1024)
    dxs, d_b, d_c, dz, ddt, d_asum, d_bias, d_dsk, d_gn = ssd_bwd(
        p, sv["xc"], sv["y"], d_sin, sv["hprev"], sv["dtg"], sv["dtg_t"], sv["params"], w["ssd_norm"], "ssd_bwd")
    dxc = jnp.concatenate([dxs, d_b, d_c], axis=1)
    dxbc, d_convw, d_convb = conv_bwd(p, sv["cpre"], dxc, w["conv_w"], "conv_bwd")
    g["conv_w"] = d_convw[:SSD_CONV]
    g["conv_b"] = d_convb
    g["dt_bias"] = d_bias.reshape(1, SSD_HEADS)
    g["a_log"] = (d_asum * (-jnp.exp(sv["params"][2]))).reshape(1, SSD_HEADS)
    g["d_skip"] = jnp.sum(d_dsk.reshape(SSD_HEADS, HD), axis=1).reshape(1, SSD_HEADS)
    g["ssd_norm"] = d_gn
    merged_bwd = attn_merge_bwd(d_attn_o, sv["os"], sv["lses"], "attn_merge_bwd")
    dqs, dks, dvs = [], [], []
    for gi, d in enumerate(ATTN_DILATIONS):
        qs, ks, vs, lse = sv["streams"][gi]
        d_o = _to_streams(merged_bwd[gi], d)
        cterm = _to_streams(merged_bwd[3 + gi], d)
        dq, dk, dv = attn_bwd2(qs, ks, vs, d_o, cterm, lse, gi, f"attn_bwd{gi}")
        dqs.append(_from_streams(dq, d))
        dks.append(_from_streams(dk, d))
        dvs.append(_from_streams(dv, d).astype(BF16))
    dq, dk, d_qg, d_kg = qk_norm_bwd(p, dqs, dks, w["q_gain"], w["k_gain"], "qk_norm_bwd")
    g["q_norm"] = jnp.sum(d_qg.reshape(N_ATTN_HEADS, HD), axis=0).reshape(1, HD)
    g["k_norm"] = jnp.sum(d_kg.reshape(N_ATTN_HEADS, HD), axis=0).reshape(1, HD)
    dp = jnp.concatenate([dq, dk] + dvs + [dz, dxbc, dga, dgs], axis=1)
    ddt_pad = jnp.pad(ddt.transpose(1, 0, 2).reshape(s, SSD_HEADS), ((0, 0), (0, DT_PAD - SSD_HEADS)))
    if ride_early is not None:
        g["w_in_main"], g["rode_early"] = matmul_tn(sv["h"], dp, "dw_in", tn=512, ts=1024, rider=ride_early(g))
    else:
        g["w_in_main"] = matmul_tn(sv["h"], dp, "dw_in", tn=512, ts=1024)
    g["w_in_dt"] = matmul_tn(sv["h"], ddt_pad, "dw_in_dt", tn=DT_PAD, ts=1024)
    if ride_late is not None:
        dh_main, g["rode_late"] = matmul_nt(dp, w["w_in_main"], "d_mix_h", F32, tm=1024, tn=512, tk=512,
                                            rider=ride_late(g))
    else:
        dh_main = matmul_nt(dp, w["w_in_main"], "d_mix_h", F32, tm=1024, tn=512, tk=512)
    dh_dt = matmul_nt(ddt_pad, w["w_in_dt"], "d_mix_h_dt", F32, tm=1024, tn=1024, tk=DT_PAD)
    dx, g["mix_norm"] = rms_bwd([dh_main, dh_dt], x, w["mix_norm"], dx_out, "mix_drms")
    return dx, g


ANY = pl.BlockSpec(memory_space=pl.ANY)


def _place():
    x, y, c = lax.axis_index("x"), lax.axis_index("y"), lax.axis_index("c")
    chips = [(1 - x, y), (x, 1 - y), (1 - x, 1 - y)]
    return x, y, c, 2 * x + y, chips


def _comm_call(body, *, name, out_shape, n_in, scratch_shapes, aliases=None):
    return pl.pallas_call(
        body, out_shape=out_shape, in_specs=[ANY] * n_in, out_specs=[ANY] * len(out_shape),
        scratch_shapes=scratch_shapes, input_output_aliases=aliases or {}, name=name,
        compiler_params=pltpu.CompilerParams(has_side_effects=True))


def gather_weights(shards, small):
    n = len(shards)
    halves = [a.shape[0] // 2 for a in shards]
    out_shape = [_sds((N_CHIP,) + a.shape, a.dtype) for a in shards] + [_sds((N_CHIP,) + small.shape, small.dtype)]

    def body(*refs):
        ins, outs = refs[:n + 1], refs[n + 1:2 * n + 2]
        send1, recv1, send2, recv2, local = refs[2 * n + 2:]
        x, y, c, me, chips = _place()
        sibling = (x, y, 1 - c)

        def rows(k, chip, core):
            if k == n:
                return outs[k].at[chip]
            return outs[k].at[chip, pl.ds(core * halves[k], halves[k])]

        def level1(k, t, incoming):
            chip = 2 * chips[t][0] + chips[t][1]
            src = ins[k] if k == n else ins[k].at[pl.ds(c * halves[k], halves[k])]
            return pltpu.make_async_remote_copy(
                src_ref=src, dst_ref=rows(k, chip if incoming else me, c), send_sem=send1.at[3 * k + t],
                recv_sem=recv1.at[3 * k + t], device_id=(*chips[t], c), device_id_type=MESH)

        def level2(k, t, incoming):
            chip = 2 * chips[t][0] + chips[t][1]
            core = (1 - c) if incoming else c
            return pltpu.make_async_remote_copy(
                src_ref=rows(k, chip, core), dst_ref=rows(k, chip, core), send_sem=send2.at[3 * k + t],
                recv_sem=recv2.at[3 * k + t], device_id=sibling, device_id_type=MESH)

        own = [pltpu.make_async_copy(ins[k], outs[k].at[me], local.at[k]) for k in range(n + 1)]
        for cp in own:
            cp.start()
        first = [level1(k, t, False) for k in range(n + 1) for t in range(3)]
        for cp in first:
            cp.start()
        passed = []
        for k in range(n + 1):
            for t in range(3):
                level1(k, t, True).wait_recv()
                if k < n:
                    cp = level2(k, t, False)
                    cp.start()
                    passed.append(cp)
        for k in range(n):
            for t in range(3):
                level2(k, t, True).wait_recv()
        for cp in first + passed:
            cp.wait_send()
        for cp in own:
            cp.wait()

    dma = pltpu.SemaphoreType.DMA
    return _comm_call(body, name="gather_weights", out_shape=out_shape, n_in=n + 1,
                      scratch_shapes=[dma((3 * n + 3,)), dma((3 * n + 3,)), dma((3 * n,)), dma((3 * n,)),
                                      dma((n + 1,))])(*shards, small)


def reduce_to_sibling(grads):
    n = len(grads)
    halves = [a.shape[1] // 2 for a in grads]
    shapes = [_sds((N_CHIP, h, a.shape[2]), a.dtype) for a, h in zip(grads, halves)]

    def body(*refs):
        ins, got, kept = refs[:n], refs[n:2 * n], refs[2 * n:3 * n]
        send, recv, local = refs[3 * n:]
        x, y, c, _, _ = _place()
        copies, locals_ = [], []
        for k in range(n):
            h = halves[k]
            locals_.append(pltpu.make_async_copy(ins[k].at[:, pl.ds(c * h, h)], kept[k], local.at[k]))
            copies.append(pltpu.make_async_remote_copy(
                src_ref=ins[k].at[:, pl.ds((1 - c) * h, h)], dst_ref=got[k], send_sem=send.at[k], recv_sem=recv.at[k],
                device_id=(x, y, 1 - c), device_id_type=MESH))
        for cp in locals_ + copies:
            cp.start()
        for cp in copies:
            cp.wait_recv()
        for cp in copies:
            cp.wait_send()
        for cp in locals_:
            cp.wait()

    dma = pltpu.SemaphoreType.DMA
    res = _comm_call(body, name="reduce_to_sibling", out_shape=shapes + shapes, n_in=n,
                     scratch_shapes=[dma((n,)), dma((n,)), dma((n,))])(*grads)
    return res[:n], res[n:]


def reduce_to_owner(sums):
    n = len(sums)
    shapes = [_sds(a.shape, a.dtype) for a in sums]

    def body(*refs):
        ins, outs = refs[:n], refs[n:2 * n]
        send, recv, local = refs[2 * n:]
        x, y, c, me, chips = _place()
        copies, locals_ = [], []
        for k in range(n):
            locals_.append(pltpu.make_async_copy(ins[k].at[me], outs[k].at[3], local.at[k]))
            for t in range(3):
                chip = 2 * chips[t][0] + chips[t][1]
                copies.append(pltpu.make_async_remote_copy(
                    src_ref=ins[k].at[chip], dst_ref=outs[k].at[t], send_sem=send.at[3 * k + t],
                    recv_sem=recv.at[3 * k + t], device_id=(*chips[t], c), device_id_type=MESH))
        for cp in locals_ + copies:
            cp.start()
        for cp in copies:
            cp.wait_recv()
        for cp in copies:
            cp.wait_send()
        for cp in locals_:
            cp.wait()

    dma = pltpu.SemaphoreType.DMA
    return _comm_call(body, name="reduce_to_owner", out_shape=shapes, n_in=n,
                      scratch_shapes=[dma((3 * n,)), dma((3 * n,)), dma((n,))])(*sums)


def share_with_sibling(halves_):
    n = len(halves_)
    shapes = [_sds((2 * a.shape[0], a.shape[1]), a.dtype) for a in halves_]

    def body(*refs):
        ins, outs = refs[:n], refs[n:2 * n]
        send, recv, local = refs[2 * n:]
        x, y, c, _, _ = _place()
        copies, locals_ = [], []
        for k in range(n):
            h = ins[k].shape[0]
            mine = outs[k].at[pl.ds(c * h, h)]
            locals_.append(pltpu.make_async_copy(ins[k], mine, local.at[k]))
            copies.append(pltpu.make_async_remote_copy(
                src_ref=ins[k], dst_ref=mine, send_sem=send.at[k], recv_sem=recv.at[k],
                device_id=(x, y, 1 - c), device_id_type=MESH))
        for cp in locals_ + copies:
            cp.start()
        for cp in copies:
            cp.wait_recv()
        for cp in copies:
            cp.wait_send()
        for cp in locals_:
            cp.wait()

    dma = pltpu.SemaphoreType.DMA
    return _comm_call(body, name="share_with_sibling", out_shape=shapes, n_in=n,
                      scratch_shapes=[dma((n,)), dma((n,)), dma((n,))])(*halves_)


def _cores():
    c = lax.axis_index("c")
    return jnp.stack([c, 1 - c]).astype(jnp.int32)


def _staged_call(body, *, name, grid, in_specs, out_specs, out_shape, scratch_shapes):
    return pl.pallas_call(
        body, out_shape=out_shape, name=name,
        grid_spec=pltpu.PrefetchScalarGridSpec(num_scalar_prefetch=1, grid=grid, in_specs=in_specs,
                                               out_specs=out_specs, scratch_shapes=scratch_shapes),
        compiler_params=pltpu.CompilerParams(dimension_semantics=("arbitrary",) * len(grid),
                                             vmem_limit_bytes=V7X_VMEM_LIMIT, has_side_effects=True))


def gather_rider(shards, tiles):
    dma = pltpu.SemaphoreType.DMA
    n = len(shards)
    geo = [(a.shape[0] // 2, tm, (a.shape[0] // 2) // tm) for a, tm in zip(shards, tiles)]
    scratch = []
    for a, (h, tm, nk) in zip(shards, geo):
        scratch += [pltpu.VMEM((N_CHIP,) + a.shape, a.dtype), dma((3, nk)), dma((3, nk)), dma((3, nk)), dma((3, nk)),
                    dma((nk + 2,))]

    def copies(j, in_ref, scr):
        buf, send1, recv1, send2, recv2, local = scr[6 * j:6 * j + 6]
        h, tm, nk = geo[j]
        x, y, c, me, chips = _place()
        chip_of = [2 * chips[t][0] + chips[t][1] for t in range(3)]

        def rows(chip, core, k):
            return buf.at[chip, pl.ds(core * h + k * tm, tm)]

        def mine(k):
            if k == nk:
                return pltpu.make_async_copy(in_ref.at[pl.ds((1 - c) * h, h)], buf.at[me, pl.ds((1 - c) * h, h)],
                                             local.at[nk])
            return pltpu.make_async_copy(in_ref.at[pl.ds(c * h + k * tm, tm)], rows(me, c, k), local.at[k])

        def level1(t, k, incoming):
            place = rows(chip_of[t] if incoming else me, c, k)
            return pltpu.make_async_remote_copy(src_ref=place, dst_ref=place, send_sem=send1.at[t, k],
                                                recv_sem=recv1.at[t, k], device_id=(*chips[t], c), device_id_type=MESH)

        def level2(t, k, incoming):
            place = rows(chip_of[t], (1 - c) if incoming else c, k)
            return pltpu.make_async_remote_copy(src_ref=place, dst_ref=place, send_sem=send2.at[t, k],
                                                recv_sem=recv2.at[t, k], device_id=(x, y, 1 - c),
                                                device_id_type=MESH)

        return buf, local, nk, mine, level1, level2

    def start(ins, outs, scr):
        for j in range(n):
            _, _, nk, mine, _, _ = copies(j, ins[j], scr)
            for k in range(nk + 1):
                mine(k).start()
        for j in range(n):
            _, _, nk, mine, level1, _ = copies(j, ins[j], scr)
            for k in range(nk):
                mine(k).wait()
                for t in range(3):
                    level1(t, k, False).start()

    def finish(ins, outs, scr):
        for j in range(n):
            _, _, nk, _, level1, level2 = copies(j, ins[j], scr)
            for k in range(nk):
                for t in range(3):
                    level1(t, k, True).wait_recv()
                    level2(t, k, False).start()
        for j in range(n):
            buf, local, nk, mine, level1, level2 = copies(j, ins[j], scr)
            for k in range(nk):
                for t in range(3):
                    level2(t, k, True).wait_recv()
            for k in range(nk):
                for t in range(3):
                    level1(t, k, False).wait_send()
                    level2(t, k, False).wait_send()
            mine(nk).wait()
            pltpu.make_async_copy(buf, outs[j], local.at[nk + 1]).start()
        for j in range(n):
            buf, local, nk, _, _, _ = copies(j, ins[j], scr)
            pltpu.make_async_copy(buf, outs[j], local.at[nk + 1]).wait()

    return Rider(list(shards), [_sds((N_CHIP,) + a.shape, a.dtype) for a in shards], scratch, start, finish)


def run_alone(rider, name):
    return _call(lambda: None, name=name, out_shape=[], in_specs=[], out_specs=[], grid=(1,), rider=rider)()[1]


def sibling_sum(g, tm, name):
    _, r, cdim = g.shape
    h = r // 2
    ni = h // tm
    dma = pltpu.SemaphoreType.DMA

    def body(cores_ref, keep_ref, give_ref, out_ref, slot, send, recv):
        par = (pl.program_id(0) * ni + pl.program_id(1)) % 2
        x, y, c, _, _ = _place()
        cp = pltpu.make_async_remote_copy(src_ref=give_ref, dst_ref=slot.at[par], send_sem=send.at[par],
                                          recv_sem=recv.at[par], device_id=(x, y, 1 - c), device_id_type=MESH)
        cp.start()
        cp.wait_recv()
        out_ref[...] = (keep_ref[...].astype(F32) + slot[par].astype(F32)).astype(out_ref.dtype)
        cp.wait_send()

    flat = g.reshape(N_CHIP * r, cdim)
    return _staged_call(
        body, name=name, grid=(N_CHIP, ni),
        in_specs=[pl.BlockSpec((tm, cdim), lambda j, i, cores: ((2 * j + cores[0]) * ni + i, 0)),
                  pl.BlockSpec((tm, cdim), lambda j, i, cores: ((2 * j + cores[1]) * ni + i, 0))],
        out_specs=pl.BlockSpec((None, tm, cdim), lambda j, i, cores: (j, i, 0)),
        out_shape=_sds((N_CHIP, h, cdim), g.dtype),
        scratch_shapes=[pltpu.VMEM((2, tm, cdim), g.dtype), dma((2,)), dma((2,))],
    )(_cores(), flat, flat)


def owner_sum_rider(sums, tiles):
    dma = pltpu.SemaphoreType.DMA
    n = len(sums)
    geo = [(a.shape[1], tm, a.shape[1] // tm) for a, tm in zip(sums, tiles)]
    scratch = []
    for a, (h, tm, nk) in zip(sums, geo):
        cdim = a.shape[2]
        scratch += [pltpu.VMEM(a.shape, a.dtype), pltpu.VMEM((3, h, cdim), a.dtype), pltpu.VMEM((2, h, cdim), F32),
                    dma((3, nk)), dma((3, nk)), dma((nk,)), dma((nk,)), dma((2,))]

    def copies(j, scr):
        part, got, res, send, recv, send2, recv2, local = scr[8 * j:8 * j + 8]
        h, tm, nk = geo[j]
        x, y, c, me, chips = _place()

        def to_owner(t, k):
            chip = 2 * chips[t][0] + chips[t][1]
            return pltpu.make_async_remote_copy(
                src_ref=part.at[chip, pl.ds(k * tm, tm)], dst_ref=got.at[t, pl.ds(k * tm, tm)],
                send_sem=send.at[t, k], recv_sem=recv.at[t, k], device_id=(*chips[t], c), device_id_type=MESH)

        def to_sibling(k):
            place = res.at[c, pl.ds(k * tm, tm)]
            return pltpu.make_async_remote_copy(src_ref=place, dst_ref=place, send_sem=send2.at[k],
                                                recv_sem=recv2.at[k], device_id=(x, y, 1 - c), device_id_type=MESH)

        return part, got, res, local, to_owner, to_sibling, (tm, nk, c, me)

    def start(ins, outs, scr):
        for j in range(n):
            part, _, _, local, _, _, _ = copies(j, scr)
            pltpu.make_async_copy(ins[j], part, local.at[0]).start()
        for j in range(n):
            part, _, _, local, to_owner, _, (tm, nk, c, me) = copies(j, scr)
            pltpu.make_async_copy(ins[j], part, local.at[0]).wait()
            for k in range(nk):
                for t in range(3):
                    to_owner(t, k).start()

    def finish(ins, outs, scr):
        for j in range(n):
            part, got, res, _, to_owner, to_sibling, (tm, nk, c, me) = copies(j, scr)
            for k in range(nk):
                rows = pl.ds(k * tm, tm)
                for t in range(3):
                    to_owner(t, k).wait_recv()
                acc = part[me, rows, :].astype(F32)
                for t in range(3):
                    acc = acc + got[t, rows, :].astype(F32)
                res[c, rows, :] = acc
                to_sibling(k).start()
        for j in range(n):
            _, _, res, local, to_owner, to_sibling, (tm, nk, c, me) = copies(j, scr)
            for k in range(nk):
                to_sibling(k).wait_recv()
            for k in range(nk):
                to_sibling(k).wait_send()
                for t in range(3):
                    to_owner(t, k).wait_send()
            pltpu.make_async_copy(res, outs[j], local.at[1]).start()
        for j in range(n):
            _, _, res, local, _, _, _ = copies(j, scr)
            pltpu.make_async_copy(res, outs[j], local.at[1]).wait()

    return Rider(list(sums), [_sds((2, a.shape[1], a.shape[2]), F32) for a in sums], scratch, start, finish)


def gather_conv_w(w):
    def body(in_ref, out_ref, send, recv):
        x, y, c, me, chips = _place()
        out_ref[me] = in_ref[...]
        copies = []
        for t in range(3):
            copies.append(pltpu.make_async_remote_copy(
                src_ref=out_ref.at[me], dst_ref=out_ref.at[me], send_sem=send.at[t], recv_sem=recv.at[t],
                device_id=(*chips[t], c), device_id_type=MESH))
        for cp in copies:
            cp.start()
        for cp in copies:
            cp.wait_recv()
        for cp in copies:
            cp.wait_send()

    dma = pltpu.SemaphoreType.DMA
    vmem = pl.BlockSpec(memory_space=pltpu.VMEM)
    return pl.pallas_call(
        body, out_shape=_sds((N_CHIP,) + w.shape, w.dtype), in_specs=[vmem], out_specs=vmem, name="gather_conv_w",
        scratch_shapes=[dma((3,)), dma((3,))],
        compiler_params=pltpu.CompilerParams(has_side_effects=True))(w)


N_DEV = 8
SMALL_ROWS = 32
SMALL_LANES = 1024


def all_reduce_small(arrays):
    n_arr = len(arrays)
    places = []
    for k, a in enumerate(arrays):
        for ri in range(a.shape[0]):
            for c0 in range(0, a.shape[1], SMALL_LANES):
                places.append((k, ri, c0, min(SMALL_LANES, a.shape[1] - c0), len(places)))
    assert len(places) <= SMALL_ROWS

    def body(*refs):
        ins, outs = refs[:n_arr], refs[n_arr:2 * n_arr]
        buf, send, recv = refs[2 * n_arr:]
        x, y, c, _, _ = _place()
        me = 4 * x + 2 * y + c
        buf[me] = jnp.zeros((SMALL_ROWS, SMALL_LANES), F32)
        for k, ri, c0, width, row in places:
            buf[me, row:row + 1, 0:width] = ins[k][ri:ri + 1, c0:c0 + width]
        copies = []
        for r in range(1, N_DEV):
            px = (1 - x) if r & 4 else x
            py = (1 - y) if r & 2 else y
            pc = (1 - c) if r & 1 else c
            copies.append(pltpu.make_async_remote_copy(
                src_ref=buf.at[me], dst_ref=buf.at[me], send_sem=send.at[r - 1], recv_sem=recv.at[r - 1],
                device_id=(px, py, pc), device_id_type=MESH))
        for cp in copies:
            cp.start()
        for cp in copies:
            cp.wait_recv()
        for cp in copies:
            cp.wait_send()
        acc = buf[0]
        for j in range(1, N_DEV):
            acc = acc + buf[j]
        for k, ri, c0, width, row in places:
            outs[k][ri:ri + 1, c0:c0 + width] = acc[row:row + 1, 0:width]

    dma = pltpu.SemaphoreType.DMA
    vmem = pl.BlockSpec(memory_space=pltpu.VMEM)
    return pl.pallas_call(
        body, out_shape=[_sds(a.shape, F32) for a in arrays], in_specs=[vmem] * n_arr, out_specs=[vmem] * n_arr,
        name="all_reduce_small",
        scratch_shapes=[pltpu.VMEM((N_DEV, SMALL_ROWS, SMALL_LANES), F32), dma((N_DEV - 1,)), dma((N_DEV - 1,))],
        compiler_params=pltpu.CompilerParams(has_side_effects=True))(*arrays)


def _row_tile(rows, limit, multiple):
    return max(t for t in range(multiple, min(rows, limit) + 1, multiple) if rows % t == 0)


def add_pair(a, b, name):
    _, h, c = a.shape

    def body(a_ref, b_ref, o_ref):
        o_ref[...] = (a_ref[...].astype(F32) + b_ref[...].astype(F32)).astype(o_ref.dtype)

    blk = pl.BlockSpec((None, h, c), lambda j: (j, 0, 0))
    return _call(body, name=name, out_shape=_sds(a.shape, a.dtype), grid=(N_CHIP,), in_specs=[blk, blk],
                 out_specs=blk)(a, b)


def sum_slots(buf, name):
    _, h, c = buf.shape
    tm = _row_tile(h, 256, 16)

    def body(b_ref, o_ref):
        acc = b_ref[3].astype(F32)
        for t in range(3):
            acc = acc + b_ref[t].astype(F32)
        o_ref[...] = acc

    return _call(body, name=name, out_shape=_sds((h, c), F32), grid=(h // tm,),
                 in_specs=[pl.BlockSpec((N_CHIP, tm, c), lambda i: (0, i, 0))],
                 out_specs=pl.BlockSpec((tm, c), lambda i: (i, 0)))(buf)


def _adamw_math(w, g, m, v):
    c1 = 1.0 - ADAM_B1 ** ADAM_STEP
    c2 = 1.0 - ADAM_B2 ** ADAM_STEP
    m2 = ADAM_B1 * m + (1.0 - ADAM_B1) * g
    v2 = ADAM_B2 * v + (1.0 - ADAM_B2) * (g * g)
    return -ADAM_LR * ((m2 / c1) / (jnp.sqrt(v2 / c2) + ADAM_EPS) + ADAM_WD * w), m2, v2


def adamw(w, g, row_off, m, v, name):
    _, r, c = w.shape
    tm = r if r < 8 else _row_tile(math.gcd(r, row_off) if row_off else r, 128, 8)

    def body(w_ref, g_ref, m_ref, v_ref, go_ref, d_ref, m2_ref, v2_ref):
        gv = g_ref[...]
        go_ref[...] = gv
        d_ref[...], m2_ref[...], v2_ref[...] = _adamw_math(w_ref[...], gv, m_ref[...], v_ref[...])

    blk = pl.BlockSpec((None, tm, c), lambda i: (0, i, 0))
    shp = _sds((1, r, c), F32)
    return _call(body, name=name, out_shape=[shp] * 4, grid=(r // tm,),
                 in_specs=[blk, pl.BlockSpec((tm, c), lambda i: (row_off // tm + i, 0)), blk, blk],
                 out_specs=[blk] * 4)(w, g, m, v)


def adamw_small(ws, gs, ms, vs):
    n = len(ws)

    def body(*refs):
        ins, outs = refs[:4 * n], refs[4 * n:]
        for k in range(n):
            w_ref, g_ref, m_ref, v_ref = (ins[j * n + k] for j in range(4))
            outs[k][...], outs[n + k][...], outs[2 * n + k][...] = _adamw_math(w_ref[...], g_ref[...], m_ref[...],
                                                                               v_ref[...])

    vmem = pl.BlockSpec(memory_space=pltpu.VMEM)
    shapes = [_sds(w.shape, F32) for w in ws] * 3
    res = pl.pallas_call(body, out_shape=shapes, in_specs=[vmem] * (4 * n), out_specs=[vmem] * (3 * n),
                         name="adamw_small")(*ws, *gs, *ms, *vs)
    return res[:n], res[n:2 * n], res[2 * n:]


BIG = ("ffn1_w_gate", "ffn1_w_up", "ffn1_w_down", "w_in", "w_attn_branch", "w_ssd_branch", "w_out",
       "ffn2_w_gate", "ffn2_w_up", "ffn2_w_down")
SMALL = ("ffn1_norm", "mix_norm", "q_norm", "k_norm", "conv_b", "dt_bias", "a_log", "d_skip", "ssd_norm", "ffn2_norm")
WEIGHTS = ("ffn1_norm", "ffn1_w_gate", "ffn1_w_up", "ffn1_w_down", "mix_norm", "w_in", "q_norm", "k_norm", "conv_w",
           "conv_b", "dt_bias", "a_log", "d_skip", "ssd_norm", "w_attn_branch", "w_ssd_branch", "w_out", "ffn2_norm",
           "ffn2_w_gate", "ffn2_w_up", "ffn2_w_down")
CONV_SHARD = SSD_CONV_DIM // N_CHIP
CLASSES = {
    "ffn1_in": (("ffn1_w_gate", 1024), ("ffn1_w_up", 1024)),
    "ffn1_out": (("ffn1_w_down", 704),),
    "mix_in": (("w_in", 1024),),
    "mix_attn": (("w_attn_branch", 512),),
    "late_out": (("ffn2_w_down", 704), ("w_ssd_branch", 512), ("w_out", 256)),
    "ffn2_in": (("ffn2_w_gate", 1024), ("ffn2_w_up", 1024)),
}
CLASS_TILE = {"ffn1_in": 256, "ffn1_out": 176, "mix_in": 128, "mix_attn": 256, "late_out": 368, "ffn2_in": 256}


def _pack_small(vals, conv_part, loss_part=None):
    flat = [vals[k].reshape(-1) for k in SMALL]
    flat.append(jnp.zeros((SSD_CONV * SSD_CONV_DIM,), F32) if conv_part is None else conv_part.reshape(-1))
    flat.append(jnp.zeros((1,), F32) if loss_part is None else loss_part.reshape(1))
    flat = jnp.concatenate(flat)
    return jnp.pad(flat, (0, SMALL_ROWS * D_MODEL - flat.shape[0])).reshape(SMALL_ROWS, D_MODEL)


def _unpack_small(pack, like):
    flat = pack.reshape(-1)
    out, off = {}, 0
    for k in SMALL:
        n = like[k].size
        out[k] = flat[off:off + n].reshape(like[k].shape)
        off += n
    conv = flat[off:off + SSD_CONV * SSD_CONV_DIM].reshape(SSD_CONV, SSD_CONV_DIM)
    return out, conv, flat[off + SSD_CONV * SSD_CONV_DIM]


def _chip_major_cols(a):
    r = a.shape[0]
    return a.reshape(r, N_CHIP, -1).transpose(1, 0, 2)


def _from_chip_major_cols(a):
    return a.transpose(1, 0, 2).reshape(a.shape[1], -1)


def kernel(x, ffn1_norm, ffn1_w_gate, ffn1_w_up, ffn1_w_down, mix_norm, w_in, q_norm, k_norm, conv_w, conv_b, dt_bias, a_log, d_skip, ssd_norm, w_attn_branch, w_ssd_branch, w_out, ffn2_norm, ffn2_w_gate, ffn2_w_up, ffn2_w_down, loss_target, m_ffn1_norm, m_ffn1_w_gate, m_ffn1_w_up, m_ffn1_w_down, m_mix_norm, m_w_in, m_q_norm, m_k_norm, m_conv_w, m_conv_b, m_dt_bias, m_a_log, m_d_skip, m_ssd_norm, m_w_attn_branch, m_w_ssd_branch, m_w_out, m_ffn2_norm, m_ffn2_w_gate, m_ffn2_w_up, m_ffn2_w_down, v_ffn1_norm, v_ffn1_w_gate, v_ffn1_w_up, v_ffn1_w_down, v_mix_norm, v_w_in, v_q_norm, v_k_norm, v_conv_w, v_conv_b, v_dt_bias, v_a_log, v_d_skip, v_ssd_norm, v_w_attn_branch, v_w_ssd_branch, v_w_out, v_ffn2_norm, v_ffn2_w_gate, v_ffn2_w_up, v_ffn2_w_down):
    env = dict(locals())
    wts = {k: env[k] for k in WEIGHTS}
    moms = {k: env["m_" + k] for k in WEIGHTS}
    vars_ = {k: env["v_" + k] for k in WEIGHTS}
    x0 = x[0]
    target = loss_target[0]

    def gather(classes):
        shards = [jnp.concatenate([wts[k][0] for k, _ in CLASSES[c]], axis=0).astype(BF16) for c in classes]
        return gather_rider(shards, [CLASS_TILE[c] for c in classes])

    def reducer(classes, parts):
        sums = [sibling_sum(p, CLASS_TILE[c], f"sibling_sum_{c}") for c, p in zip(classes, parts)]
        return owner_sum_rider(sums, [CLASS_TILE[c] for c in classes])

    (w_ffn1_in,) = run_alone(gather(["ffn1_in"]), "gather_ffn1_in")
    x1, saved1, (w_ffn1_out, w_mix_in, w_mix_attn) = ffn_forward(
        x0, ffn1_norm, w_ffn1_in, lambda rode: rode[0], "ffn1", rider=gather(["ffn1_out", "mix_in", "mix_attn"]))
    dt0, dt1 = IN_DT0 - 3 * IN_SHARD, IN_DT1 - 3 * IN_SHARD
    mixer_w = dict(
        mix_norm=mix_norm,
        w_in_main=jnp.concatenate([w_mix_in[0], w_mix_in[1], w_mix_in[2], w_mix_in[3][:, :dt0], w_mix_in[3][:, dt1:]],
                                  axis=1),
        w_in_dt=jnp.pad(w_mix_in[3][:, dt0:dt1], ((0, 0), (0, DT_PAD - SSD_HEADS))),
        q_gain=jnp.tile(q_norm, (1, 2)), k_gain=jnp.tile(k_norm, (1, 2)),
        conv_w=_from_chip_major_cols(gather_conv_w(conv_w[0])), conv_b=conv_b, dt_bias=dt_bias, a_log=a_log,
        d_skip=d_skip, ssd_norm=ssd_norm, w_attn_branch=_from_chip_major_cols(w_mix_attn))

    def later_weights(rode):
        late = rode[0]
        return dict(w_ssd_branch=late[:, 704:1216].reshape(SSD_INNER, D_MODEL),
                    w_out=late[:, 1216:1472].reshape(D_MODEL, D_MODEL))

    x2, saved_mix, (w_late_out, w_ffn2_in) = mixer_forward(x1, mixer_w, gather(["late_out", "ffn2_in"]), later_weights)
    x3, saved2, _ = ffn_forward(x2, ffn2_norm, w_ffn2_in, lambda rode: w_late_out, "ffn2")
    dx3, sq = loss_grad(x3, target, "loss")

    grads = {}
    dx2, grads["ffn2_norm"], d_ffn2_in, d_ffn2_down = ffn_backward(dx3, x2, ffn2_norm, w_ffn2_in, w_late_out, saved2,
                                                                   "ffn2")

    def ride_early(g):
        late = jnp.concatenate([d_ffn2_down, g["w_ssd_branch"].reshape(N_CHIP, -1, D_MODEL),
                                g["w_out"].reshape(N_CHIP, -1, D_MODEL)], axis=1)
        return reducer(["ffn2_in", "late_out"], [d_ffn2_in, late])

    def ride_late(g):
        main = g["w_in_main"]
        last = jnp.concatenate([main[:, 3 * IN_SHARD:IN_DT0], g["w_in_dt"][:, :SSD_HEADS], main[:, IN_DT0:]], axis=1)
        g_in = jnp.stack([main[:, j * IN_SHARD:(j + 1) * IN_SHARD] for j in range(3)] + [last])
        return reducer(["mix_in", "mix_attn"], [g_in, _chip_major_cols(g["w_attn_branch"])])

    dx1, gmix = mixer_backward(dx2, x1, saved_mix, ride_early, ride_late)
    dx0, grads["ffn1_norm"], rode_in, rode_out = ffn_backward(
        dx1, x0, ffn1_norm, w_ffn1_in, w_ffn1_out, saved1, "ffn1",
        ride_down=lambda d: reducer(["ffn1_out"], [d]), ride_in=lambda d: reducer(["ffn1_in"], [d]))
    for k in ("mix_norm", "q_norm", "k_norm", "conv_b", "dt_bias", "a_log", "d_skip", "ssd_norm"):
        grads[k] = gmix[k]
    reduced = dict(zip(("ffn2_in", "late_out", "mix_in", "mix_attn", "ffn1_in", "ffn1_out"),
                       (*gmix["rode_early"], *gmix["rode_late"], rode_in[0], rode_out[0])))
    reduced = {c: r.reshape(-1, r.shape[2]) for c, r in reduced.items()}
    summed = all_reduce_small([grads[k] for k in SMALL]
                              + [gmix["conv_w"], (0.5 * jnp.sum(sq) / D_MODEL).reshape(1, 1)])
    g_small = dict(zip(SMALL, summed))
    loss = summed[-1].reshape(())
    chip = 2 * lax.axis_index("x") + lax.axis_index("y")
    g_conv = lax.dynamic_slice_in_dim(summed[-2], chip * CONV_SHARD, CONV_SHARD, axis=1)

    g_final, delta, new_m, new_v = dict(g_small), {}, {}, {}

    def update(k, g_arr, row_off):
        g_final[k], delta[k], new_m[k], new_v[k] = adamw(wts[k], g_arr, row_off, moms[k], vars_[k], f"adamw_{k}")

    for cls, members in CLASSES.items():
        off = 0
        for k, rows in members:
            update(k, reduced[cls], off)
            off += rows
    update("conv_w", g_conv, 0)
    small = adamw_small(*([d[k] for k in SMALL] for d in (wts, g_small, moms, vars_)))
    for res, vals in zip((delta, new_m, new_v), small):
        res.update(zip(SMALL, vals))

    return (loss, dx0[None], *[g_final[k] for k in WEIGHTS], *[delta[k] for k in WEIGHTS],
            *[new_m[k] for k in WEIGHTS], *[new_v[k] for k in WEIGHTS])
```

```python
import collections
import functools
import math

import jax
import jax.numpy as jnp
from jax import lax
from jax.experimental import pallas as pl
from jax.experimental.pallas import tpu as pltpu

F32 = jnp.float32
BF16 = jnp.bfloat16
MESH = pl.DeviceIdType.MESH

EPS = 1e-6
D_MODEL = 1024
D_FF = 2816
N_CHIP = 4
FF_SHARD = D_FF // N_CHIP
HD = 64
BLK = 128
ATTN_DILATIONS = (1, 4, 16)
HEADS_PER_PATTERN = 8
N_ATTN_HEADS = 24
ALIBI_MAX_EXP = 8.0
ATTN_QKV = 1536
GROUP_W = 512
SSD_INNER = 2048
SSD_HEADS = 32
SSD_GROUPS = 4
SSD_CONV = 4
SSD_CONV_DIM = 3072
IN_COLS = 11808
IN_DT0, IN_DT1 = 9728, 9760
IN_SHARD = IN_COLS // 4
COL_K, COL_V, COL_Z, COL_XBC, COL_GA, COL_GS, P_COLS = 1536, 3072, 4608, 6656, 9728, 10752, 11776
DT_PAD = 128

ADAM_LR, ADAM_B1, ADAM_B2, ADAM_EPS, ADAM_WD, ADAM_STEP = 0.001, 0.9, 0.999, 1e-08, 0.01, 10

V7X_VMEM_LIMIT = 56 * 1024 * 1024
NEG = -1e30


Rider = collections.namedtuple("Rider", "arrays out_shape scratch start finish")
Rider.__doc__ = """An exchange between devices that rides in a compute kernel: its copies are started in the host's
first grid step and waited for in its last, so they travel while the host computes.  arrays / out_shape: extra HBM
operands and results; scratch: extra scratch; start, finish: f(in_refs, out_refs, scratch_refs)."""


def _call(body, *, name, out_shape, in_specs, out_specs, grid=(), scratch_shapes=(), aliases=None, rider=None):
    params = dict(dimension_semantics=("arbitrary",) * len(grid), vmem_limit_bytes=V7X_VMEM_LIMIT)
    if rider is None:
        return pl.pallas_call(
            body, out_shape=out_shape, grid=grid, in_specs=in_specs, out_specs=out_specs,
            scratch_shapes=scratch_shapes, input_output_aliases=aliases or {}, name=name,
            compiler_params=pltpu.CompilerParams(**params))
    single = not isinstance(out_shape, (list, tuple))
    main_out = [out_shape] if single else list(out_shape)
    main_specs = [out_specs] if single else list(out_specs)
    n_in, n_out, n_scr = len(in_specs), len(main_out), len(scratch_shapes)
    r_in, r_out = len(rider.arrays), len(rider.out_shape)

    def wrapped(*refs):
        ins, refs = refs[:n_in], refs[n_in:]
        r_ins, refs = refs[:r_in], refs[r_in:]
        outs, refs = refs[:n_out], refs[n_out:]
        r_outs, refs = refs[:r_out], refs[r_out:]
        scr, r_scr = refs[:n_scr], refs[n_scr:]
        first = last = None
        for axis, size in enumerate(grid):
            at_start, at_end = pl.program_id(axis) == 0, pl.program_id(axis) == size - 1
            first = at_start if first is None else jnp.logical_and(first, at_start)
            last = at_end if last is None else jnp.logical_and(last, at_end)

        @pl.when(first)
        def _():
            rider.start(r_ins, r_outs, r_scr)

        body(*ins, *outs, *scr)

        @pl.when(last)
        def _():
            rider.finish(r_ins, r_outs, r_scr)

    hbm = pl.BlockSpec(memory_space=pl.ANY)
    call = pl.pallas_call(
        wrapped, out_shape=main_out + list(rider.out_shape), grid=grid, in_specs=list(in_specs) + [hbm] * r_in,
        out_specs=main_specs + [hbm] * r_out, scratch_shapes=list(scratch_shapes) + list(rider.scratch), name=name,
        compiler_params=pltpu.CompilerParams(has_side_effects=True, **params))

    def run(*args):
        res = call(*args, *rider.arrays)
        main = res[:n_out]
        return (main[0] if single else main), res[n_out:]

    return run


def _sds(shape, dtype):
    return jax.ShapeDtypeStruct(tuple(shape), dtype)


def _dot(a, b):
    return jnp.dot(a, b, preferred_element_type=F32)


def _dot_nt(a, b):
    return lax.dot_general(a, b, (((1,), (1,)), ((), ())), preferred_element_type=F32)


def _dot_tn(a, b):
    return lax.dot_general(a, b, (((0,), (0,)), ((), ())), preferred_element_type=F32)


def _dot_hi(a, b):
    return jnp.dot(a, b, preferred_element_type=F32, precision=lax.Precision.HIGHEST)


def _sigmoid(x):
    return 1.0 / (1.0 + jnp.exp(-x))


def _lane_first_half(shape):
    return lax.broadcasted_iota(jnp.int32, shape, len(shape) - 1) < HD


def _pair_sum(x, first):
    s_all = jnp.sum(x, axis=-1, keepdims=True)
    s_a = jnp.sum(jnp.where(first, x, 0.0), axis=-1, keepdims=True)
    return s_a, s_all - s_a


def _rowwise(name, fn, rows, consts, outs, accs=(), tm=512):
    n_rows = None
    in_arrays, in_specs = [], []
    for r in rows:
        if isinstance(r, tuple):
            arr, w, cb = r
            spec = pl.BlockSpec((tm, w), functools.partial(lambda i, cb: (i, cb), cb=cb))
        else:
            arr = r
            spec = pl.BlockSpec((tm, arr.shape[1]), lambda i: (i, 0))
        n_rows = arr.shape[0]
        in_arrays.append(arr)
        in_specs.append(spec)
    for c in consts:
        in_arrays.append(c)
        in_specs.append(pl.BlockSpec(c.shape, functools.partial(lambda i, n: (0,) * n, n=c.ndim)))
    out_shape = [_sds(s, d) for s, d in outs] + [_sds(s, d) for s, d in accs]
    out_specs = [pl.BlockSpec((tm, s[1]), lambda i: (i, 0)) for s, _ in outs]
    out_specs += [pl.BlockSpec(s, functools.partial(lambda i, n: (0,) * n, n=len(s))) for s, _ in accs]

    def body(*refs):
        fn(pl.program_id(0), *refs)

    res = _call(body, name=name, out_shape=out_shape, grid=(n_rows // tm,), in_specs=in_specs,
                out_specs=out_specs)(*in_arrays)
    return res


def rms_fwd(x, gain, name):
    def fn(i, x_ref, g_ref, h_ref):
        xv = x_ref[...]
        r = lax.rsqrt(jnp.mean(xv * xv, axis=-1, keepdims=True) + EPS)
        h_ref[...] = (xv * r * g_ref[...]).astype(h_ref.dtype)

    return _rowwise(name, fn, [x], [gain], [(x.shape, BF16)])[0]


def rms_bwd(dhs, x, gain, dx_in, name):
    n = len(dhs)

    def fn(i, *refs):
        dh_refs, (x_ref, dxin_ref, g_ref, dx_ref, dg_ref) = refs[:n], refs[n:]
        dh = dh_refs[0][...]
        for r in dh_refs[1:]:
            dh = dh + r[...]
        xv = x_ref[...]
        r = lax.rsqrt(jnp.mean(xv * xv, axis=-1, keepdims=True) + EPS)
        xn = xv * r
        dxn = dh * g_ref[...]
        dx_ref[...] = dxin_ref[...] + r * (dxn - xn * jnp.mean(dxn * xn, axis=-1, keepdims=True))

        @pl.when(i == 0)
        def _():
            dg_ref[...] = jnp.zeros_like(dg_ref)

        dg_ref[...] += jnp.sum(dh * xn, axis=0, keepdims=True)

    return _rowwise(name, fn, list(dhs) + [x, dx_in], [gain], [(x.shape, F32)], [((1, x.shape[1]), F32)])


def loss_grad(y, target, name):
    def fn(i, y_ref, t_ref, dy_ref, sq_ref):
        err = y_ref[...] - t_ref[...]
        dy_ref[...] = err * (1.0 / y_ref.shape[1])

        @pl.when(i == 0)
        def _():
            sq_ref[...] = jnp.zeros_like(sq_ref)

        sq_ref[...] += jnp.sum(err * err, axis=0, keepdims=True)

    return _rowwise(name, fn, [y, target], [], [(y.shape, F32)], [((1, y.shape[1]), F32)])


def matmul_nn(a, b, name, out_dtype, tm, tn, res=None, scale=1.0, rider=None):
    s, k = a.shape
    n = b.shape[1]

    def body(*refs):
        if res is None:
            a_ref, b_ref, o_ref = refs
            o_ref[...] = _dot(a_ref[...], b_ref[...]).astype(o_ref.dtype)
        else:
            a_ref, b_ref, r_ref, o_ref = refs
            o_ref[...] = (r_ref[...] + scale * _dot(a_ref[...], b_ref[...])).astype(o_ref.dtype)

    in_specs = [pl.BlockSpec((tm, k), lambda i, j: (i, 0)), pl.BlockSpec((k, tn), lambda i, j: (0, j))]
    args = [a, b]
    if res is not None:
        in_specs.append(pl.BlockSpec((tm, tn), lambda i, j: (i, j)))
        args.append(res)
    return _call(body, name=name, out_shape=_sds((s, n), out_dtype), grid=(s // tm, n // tn), in_specs=in_specs,
                 out_specs=pl.BlockSpec((tm, tn), lambda i, j: (i, j)), rider=rider)(*args)


def matmul_nt(a, b, name, out_dtype, tm, tn, tk, rider=None):
    s, k = a.shape
    n = b.shape[0]
    nk = k // tk

    def body(a_ref, b_ref, o_ref, acc_ref):
        kk = pl.program_id(2)

        @pl.when(kk == 0)
        def _():
            acc_ref[...] = jnp.zeros_like(acc_ref)

        acc_ref[...] += _dot_nt(a_ref[...].astype(BF16), b_ref[...])

        @pl.when(kk == nk - 1)
        def _():
            o_ref[...] = acc_ref[...].astype(o_ref.dtype)

    return _call(body, name=name, out_shape=_sds((s, n), out_dtype), grid=(s // tm, n // tn, nk),
                 in_specs=[pl.BlockSpec((tm, tk), lambda i, j, kk: (i, kk)),
                           pl.BlockSpec((tn, tk), lambda i, j, kk: (j, kk))],
                 out_specs=pl.BlockSpec((tm, tn), lambda i, j, kk: (i, j)),
                 scratch_shapes=[pltpu.VMEM((tm, tn), F32)], rider=rider)(a, b)


def matmul_tn(a, b, name, tn, ts, a_scale=None, b_scale=None, rider=None):
    s, m = a.shape
    n = b.shape[1]
    ns = s // ts

    def body(a_ref, b_ref, o_ref, acc_ref):
        ss = pl.program_id(1)

        @pl.when(ss == 0)
        def _():
            acc_ref[...] = jnp.zeros_like(acc_ref)

        av, bv = a_ref[...], b_ref[...]
        if a_scale is not None:
            av = av * a_scale
        if b_scale is not None:
            bv = bv * b_scale
        acc_ref[...] += _dot_tn(av.astype(BF16), bv.astype(BF16))

        @pl.when(ss == ns - 1)
        def _():
            o_ref[...] = acc_ref[...].astype(o_ref.dtype)

    return _call(body, name=name, out_shape=_sds((m, n), BF16), grid=(n // tn, ns),
                 in_specs=[pl.BlockSpec((ts, m), lambda j, ss: (ss, 0)), pl.BlockSpec((ts, tn), lambda j, ss: (ss, j))],
                 out_specs=pl.BlockSpec((m, tn), lambda j, ss: (0, j)),
                 scratch_shapes=[pltpu.VMEM((m, tn), F32)], rider=rider)(a, b)


def ffn_up(h, w704, gate_blk, up_blk, name, tm=512, rider=None):
    s = h.shape[0]

    def body(h_ref, wg_ref, wu_ref, g_ref, u_ref, a_ref):
        hv = h_ref[...]
        g = _dot(hv, wg_ref[...])
        u = _dot(hv, wu_ref[...])
        g_ref[...] = g.astype(BF16)
        u_ref[...] = u.astype(BF16)
        a_ref[...] = (g * _sigmoid(g) * u).astype(BF16)

    ospec = pl.BlockSpec((None, tm, FF_SHARD), lambda j, i: (j, i, 0))
    shp = _sds((N_CHIP, s, FF_SHARD), BF16)
    return _call(body, name=name, out_shape=[shp, shp, shp], grid=(N_CHIP, s // tm),
                 in_specs=[pl.BlockSpec((tm, D_MODEL), lambda j, i: (i, 0)),
                           pl.BlockSpec((None, D_MODEL, FF_SHARD), lambda j, i: (j, gate_blk, 0)),
                           pl.BlockSpec((None, D_MODEL, FF_SHARD), lambda j, i: (j, up_blk, 0))],
                 out_specs=[ospec, ospec, ospec], rider=rider)(h, w704, w704)


def ffn_down(a, w1024, blk, x, name, tm=512):
    s = x.shape[0]

    def body(a_ref, wd_ref, x_ref, o_ref):
        acc = _dot(a_ref[0], wd_ref[0])
        for j in range(1, N_CHIP):
            acc += _dot(a_ref[j], wd_ref[j])
        o_ref[...] = x_ref[...] + 0.5 * acc

    return _call(body, name=name, out_shape=_sds((s, D_MODEL), F32), grid=(s // tm,),
                 in_specs=[pl.BlockSpec((N_CHIP, tm, FF_SHARD), lambda i: (0, i, 0)),
                           pl.BlockSpec((N_CHIP, FF_SHARD, D_MODEL), lambda i: (0, blk, 0)),
                           pl.BlockSpec((tm, D_MODEL), lambda i: (i, 0))],
                 out_specs=pl.BlockSpec((tm, D_MODEL), lambda i: (i, 0)))(a, w1024, x)


def ffn_bwd_hidden(dx, w1024, blk, g, u, name, tm=1024, rider=None):
    s = dx.shape[0]

    def body(dx_ref, wd_ref, g_ref, u_ref, dg_ref, du_ref):
        dy = (0.5 * dx_ref[...]).astype(BF16)
        da = _dot_nt(dy, wd_ref[...])
        gv = g_ref[...].astype(F32)
        uv = u_ref[...].astype(F32)
        sg = _sigmoid(gv)
        dg_ref[...] = (da * uv * (sg * (1.0 + gv * (1.0 - sg)))).astype(BF16)
        du_ref[...] = (da * gv * sg).astype(BF16)

    hspec = pl.BlockSpec((None, tm, FF_SHARD), lambda j, i: (j, i, 0))
    shp = _sds((N_CHIP, s, FF_SHARD), BF16)
    return _call(body, name=name, out_shape=[shp, shp], grid=(N_CHIP, s // tm),
                 in_specs=[pl.BlockSpec((tm, D_MODEL), lambda j, i: (i, 0)),
                           pl.BlockSpec((None, FF_SHARD, D_MODEL), lambda j, i: (j, blk, 0)), hspec, hspec],
                 out_specs=[hspec, hspec], rider=rider)(dx, w1024, g, u)


def ffn_bwd_input(dg, du, w704, gate_blk, up_blk, name, tm=512, rider=None):
    s = dg.shape[1]

    def body(dg_ref, du_ref, wg_ref, wu_ref, o_ref):
        acc = _dot_nt(dg_ref[0], wg_ref[0]) + _dot_nt(du_ref[0], wu_ref[0])
        for j in range(1, N_CHIP):
            acc += _dot_nt(dg_ref[j], wg_ref[j]) + _dot_nt(du_ref[j], wu_ref[j])
        o_ref[...] = acc

    hspec = pl.BlockSpec((N_CHIP, tm, FF_SHARD), lambda i: (0, i, 0))
    return _call(body, name=name, out_shape=_sds((s, D_MODEL), F32), grid=(s // tm,),
                 in_specs=[hspec, hspec,
                           pl.BlockSpec((N_CHIP, D_MODEL, FF_SHARD), lambda i: (0, gate_blk, 0), pl.Buffered(1)),
                           pl.BlockSpec((N_CHIP, D_MODEL, FF_SHARD), lambda i: (0, up_blk, 0), pl.Buffered(1))],
                 out_specs=pl.BlockSpec((tm, D_MODEL), lambda i: (i, 0)), rider=rider)(dg, du, w704, w704)


def ffn_wgrad_in(h, dgu, name, ts=1024):
    s = h.shape[0]
    ns = s // ts

    def body(h_ref, d_ref, o_ref, acc_ref):
        ss = pl.program_id(1)

        @pl.when(ss == 0)
        def _():
            acc_ref[...] = jnp.zeros_like(acc_ref)

        acc_ref[...] += _dot_tn(h_ref[...], d_ref[...])

        @pl.when(ss == ns - 1)
        def _():
            o_ref[...] = acc_ref[...].astype(BF16)

    return _call(body, name=name, out_shape=_sds((N_CHIP, D_MODEL, FF_SHARD), BF16), grid=(N_CHIP, ns),
                 in_specs=[pl.BlockSpec((ts, D_MODEL), lambda j, ss: (ss, 0)),
                           pl.BlockSpec((None, ts, FF_SHARD), lambda j, ss: (j, ss, 0))],
                 out_specs=pl.BlockSpec((None, D_MODEL, FF_SHARD), lambda j, ss: (j, 0, 0)),
                 scratch_shapes=[pltpu.VMEM((D_MODEL, FF_SHARD), F32)])(h, dgu)


def ffn_wgrad_down(a, dx, name, ts=1024):
    s = dx.shape[0]
    ns = s // ts

    def body(a_ref, dx_ref, o_ref, acc_ref):
        ss = pl.program_id(1)

        @pl.when(ss == 0)
        def _():
            acc_ref[...] = jnp.zeros_like(acc_ref)

        acc_ref[...] += _dot_tn(a_ref[...], (0.5 * dx_ref[...]).astype(BF16))

        @pl.when(ss == ns - 1)
        def _():
            o_ref[...] = acc_ref[...].astype(BF16)

    return _call(body, name=name, out_shape=_sds((N_CHIP, FF_SHARD, D_MODEL), BF16), grid=(N_CHIP, ns),
                 in_specs=[pl.BlockSpec((None, ts, FF_SHARD), lambda j, ss: (j, ss, 0)),
                           pl.BlockSpec((ts, D_MODEL), lambda j, ss: (ss, 0))],
                 out_specs=pl.BlockSpec((None, FF_SHARD, D_MODEL), lambda j, ss: (j, 0, 0)),
                 scratch_shapes=[pltpu.VMEM((FF_SHARD, D_MODEL), F32)])(a, dx)


def ffn_forward(x, gain, w704, get_w1024, tag, rider=None):
    h = rms_fwd(x, gain, f"{tag}_rms")
    res = ffn_up(h, w704, 0, 1, f"{tag}_up", rider=rider)
    (g, u, a), rode = res if rider is not None else (res, None)
    y = ffn_down(a, get_w1024(rode), 0, x, f"{tag}_down")
    return y, (h, g, u, a), rode


def ffn_backward(dy, x, gain, w704, w1024, saved, tag, ride_down=None, ride_in=None):
    h, g, u, a = saved
    d_wd = ffn_wgrad_down(a, dy, f"{tag}_dwd")
    if ride_down is not None:
        (dg, du), d_wd = ffn_bwd_hidden(dy, w1024, 0, g, u, f"{tag}_dhid", rider=ride_down(d_wd))
    else:
        dg, du = ffn_bwd_hidden(dy, w1024, 0, g, u, f"{tag}_dhid")
    d_win = jnp.concatenate([ffn_wgrad_in(h, dg, f"{tag}_dwg"), ffn_wgrad_in(h, du, f"{tag}_dwu")], axis=1)
    if ride_in is not None:
        dh, d_win = ffn_bwd_input(dg, du, w704, 0, 1, f"{tag}_dh", rider=ride_in(d_win))
    else:
        dh = ffn_bwd_input(dg, du, w704, 0, 1, f"{tag}_dh")
    dx, d_gain = rms_bwd([dh], x, gain, dy, f"{tag}_drms")
    return dx, d_gain, d_win, d_wd


def _alibi_slope(head):
    return float(2.0 ** (-ALIBI_MAX_EXP * (head + 1) / N_ATTN_HEADS))


def _head_norm(t, gain_pair, first):
    sa, sb = _pair_sum(t * t, first)
    r = jnp.where(first, lax.rsqrt(sa * (1.0 / HD) + EPS), lax.rsqrt(sb * (1.0 / HD) + EPS))
    return t * r * gain_pair, r


def qk_norm_fwd(p, q_gain, k_gain, name):
    s = p.shape[0]

    def fn(i, q_ref, k_ref, qg_ref, kg_ref, qn_ref, kn_ref):
        first = _lane_first_half((q_ref.shape[0], 2 * HD))
        for src, g_ref, dst in ((q_ref, qg_ref, qn_ref), (k_ref, kg_ref, kn_ref)):
            for pr in range(ATTN_QKV // (2 * HD)):
                cols = slice(pr * 2 * HD, (pr + 1) * 2 * HD)
                y, _ = _head_norm(src[:, cols].astype(F32), g_ref[...], first)
                dst[:, cols] = y.astype(BF16)

    return _rowwise(name, fn, [(p, ATTN_QKV, 0), (p, ATTN_QKV, 1)], [q_gain, k_gain],
                    [((s, ATTN_QKV), BF16), ((s, ATTN_QKV), BF16)])


def qk_norm_bwd(p, dqs, dks, q_gain, k_gain, name):
    s = p.shape[0]
    pairs_per_pattern = GROUP_W // (2 * HD)

    def fn(i, q_ref, k_ref, dq0, dq1, dq2, dk0, dk1, dk2, qg_ref, kg_ref, dq_ref, dk_ref, dqg_ref, dkg_ref):
        first = _lane_first_half((q_ref.shape[0], 2 * HD))

        @pl.when(i == 0)
        def _():
            dqg_ref[...] = jnp.zeros_like(dqg_ref)
            dkg_ref[...] = jnp.zeros_like(dkg_ref)

        for src, d_refs, g_ref, dst, dg_ref in ((q_ref, (dq0, dq1, dq2), qg_ref, dq_ref, dqg_ref),
                                                (k_ref, (dk0, dk1, dk2), kg_ref, dk_ref, dkg_ref)):
            for pr in range(ATTN_QKV // (2 * HD)):
                cols = slice(pr * 2 * HD, (pr + 1) * 2 * HD)
                t = src[:, cols].astype(F32)
                sa, sb = _pair_sum(t * t, first)
                r = jnp.where(first, lax.rsqrt(sa * (1.0 / HD) + EPS), lax.rsqrt(sb * (1.0 / HD) + EPS))
                xn = t * r
                within = (pr % pairs_per_pattern) * 2 * HD
                dy = d_refs[pr // pairs_per_pattern][:, within:within + 2 * HD]
                dg_ref[:, cols] += jnp.sum(dy * xn, axis=0, keepdims=True)
                dxn = dy * g_ref[...]
                ma, mb = _pair_sum(dxn * xn, first)
                mean = jnp.where(first, ma, mb) * (1.0 / HD)
                dst[:, cols] = (r * (dxn - xn * mean)).astype(BF16)

    return _rowwise(name, fn, [(p, ATTN_QKV, 0), (p, ATTN_QKV, 1)] + list(dqs) + list(dks), [q_gain, k_gain],
                    [((s, ATTN_QKV), BF16), ((s, ATTN_QKV), BF16)], [((1, ATTN_QKV), F32), ((1, ATTN_QKV), F32)])


def _to_streams(a, d):
    if d == 1:
        return a
    s, c = a.shape
    return a.reshape(s // d, d, c).transpose(1, 0, 2).reshape(s, c)


def _from_streams(a, d):
    if d == 1:
        return a
    s, c = a.shape
    return a.reshape(d, s // d, c).transpose(1, 0, 2).reshape(s, c)


def _attn_masks():
    row = lax.broadcasted_iota(jnp.int32, (BLK, BLK), 0)
    col = lax.broadcasted_iota(jnp.int32, (BLK, BLK), 1)
    rel_diag = row - col
    rel_prev = rel_diag + BLK
    return rel_diag, rel_prev


def attn_fwd(q, k, v, pattern, name, tq=512):
    s = q.shape[0]
    d = ATTN_DILATIONS[pattern]
    blocks_per_stream = (s // d) // BLK
    nsb = tq // BLK

    def body(q_ref, k_ref, v_ref, kp_ref, vp_ref, o_ref, l_ref):
        i = pl.program_id(0)
        rel_diag, rel_prev = _attn_masks()
        first = _lane_first_half((BLK, 2 * HD))
        rd_f = (rel_diag * d).astype(F32)
        rp_f = (rel_prev * d).astype(F32)
        for sb in range(nsb):
            rows = slice(sb * BLK, (sb + 1) * BLK)
            has_prev = ((i * nsb + sb) % blocks_per_stream != 0).astype(jnp.int32)
            m_diag = rel_diag >= 0
            m_prev = (rel_prev + (1 - has_prev) * (4 * BLK)) <= BLK
            for pr in range(GROUP_W // (2 * HD)):
                cols = slice(pr * 2 * HD, (pr + 1) * 2 * HD)
                qp = q_ref[rows, cols]
                kc, vc = k_ref[rows, cols], v_ref[rows, cols]
                if sb == 0:
                    kp, vp = kp_ref[:, cols], vp_ref[:, cols]
                else:
                    prows = slice((sb - 1) * BLK, sb * BLK)
                    kp, vp = k_ref[prows, cols], v_ref[prows, cols]
                outs, lses = [], []
                for e in range(2):
                    slope = _alibi_slope(pattern * HEADS_PER_PATTERN + 2 * pr + e)
                    qm = jnp.where(first if e == 0 else jnp.logical_not(first), qp, jnp.zeros_like(qp))
                    s1 = jnp.where(m_diag, _dot_nt(qm, kc) * 0.125 - slope * rd_f, NEG)
                    s0 = jnp.where(m_prev, _dot_nt(qm, kp) * 0.125 - slope * rp_f, NEG)
                    m = jnp.maximum(jnp.max(s1, axis=-1, keepdims=True), jnp.max(s0, axis=-1, keepdims=True))
                    p1 = jnp.exp(s1 - m)
                    p0 = jnp.exp(s0 - m)
                    l = jnp.sum(p1, axis=-1, keepdims=True) + jnp.sum(p0, axis=-1, keepdims=True)
                    inv = 1.0 / l
                    outs.append(_dot((p1 * inv).astype(BF16), vc) + _dot((p0 * inv).astype(BF16), vp))
                    lses.append(m + jnp.log(l))
                o_ref[rows, cols] = jnp.where(first, outs[0], outs[1])
                l_ref[rows, cols] = jnp.where(first, lses[0], lses[1])

    cur = pl.BlockSpec((tq, GROUP_W), lambda i: (i, 0))
    prev = pl.BlockSpec((BLK, GROUP_W), lambda i: (jnp.maximum(i * nsb - 1, 0), 0))
    return _call(body, name=name, out_shape=[_sds((s, GROUP_W), F32), _sds((s, GROUP_W), F32)], grid=(s // tq,),
                 in_specs=[cur, cur, cur, prev, prev], out_specs=[cur, cur])(q, k, v, k, v)


def attn_merge_fwd(os_, lses, name):
    s = os_[0].shape[0]

    def fn(i, o0, o1, o2, l0, l1, l2, out_ref):
        m = jnp.maximum(jnp.maximum(l0[...], l1[...]), l2[...])
        e0, e1, e2 = jnp.exp(l0[...] - m), jnp.exp(l1[...] - m), jnp.exp(l2[...] - m)
        inv = 1.0 / (e0 + e1 + e2)
        out_ref[...] = ((e0 * inv) * o0[...] + (e1 * inv) * o1[...] + (e2 * inv) * o2[...]).astype(BF16)

    return _rowwise(name, fn, list(os_) + list(lses), [], [((s, GROUP_W), BF16)])[0]


def attn_merge_bwd(d_out, os_, lses, name):
    s = d_out.shape[0]

    def fn(i, do_ref, o0, o1, o2, l0, l1, l2, d0, d1, d2, c0, c1, c2):
        first = _lane_first_half((do_ref.shape[0], 2 * HD))
        m = jnp.maximum(jnp.maximum(l0[...], l1[...]), l2[...])
        e0, e1, e2 = jnp.exp(l0[...] - m), jnp.exp(l1[...] - m), jnp.exp(l2[...] - m)
        inv = 1.0 / (e0 + e1 + e2)
        w0, w1, w2 = e0 * inv, e1 * inv, e2 * inv
        do = do_ref[...]
        prod = do * (w0 * o0[...] + w1 * o1[...] + w2 * o2[...])
        for pr in range(GROUP_W // (2 * HD)):
            cols = slice(pr * 2 * HD, (pr + 1) * 2 * HD)
            ta, tb = _pair_sum(prod[:, cols], first)
            t = jnp.where(first, ta, tb)
            for w, c_ref in ((w0, c0), (w1, c1), (w2, c2)):
                c_ref[:, cols] = w[:, cols] * t
        for w, d_ref in ((w0, d0), (w1, d1), (w2, d2)):
            d_ref[...] = (w * do).astype(BF16)

    shp = (s, GROUP_W)
    return _rowwise(name, fn, [d_out] + list(os_) + list(lses), [],
                    [(shp, BF16)] * 3 + [(shp, F32)] * 3)


def attn_bwd(q, k, v, d_o, cterm, lse, pattern, name, tq=512):
    s = q.shape[0]
    d = ATTN_DILATIONS[pattern]
    blocks_per_stream = (s // d) // BLK
    nsb = tq // BLK
    n_blocks = s // BLK

    def body(q_ref, k_ref, v_ref, do_ref, c_ref, l_ref, kp_ref, vp_ref, qn_ref, don_ref, cn_ref, ln_ref,
             dq_ref, dk_ref, dv_ref):
        i = pl.program_id(0)
        rel_diag, rel_prev = _attn_masks()
        first = _lane_first_half((BLK, 2 * HD))
        second = jnp.logical_not(first)
        rd_f = (rel_diag * d).astype(F32)
        rp_f = (rel_prev * d).astype(F32)
        m_diag = rel_diag >= 0
        dq_ref[...] = jnp.zeros_like(dq_ref)
        dk_ref[...] = jnp.zeros_like(dk_ref)
        dv_ref[...] = jnp.zeros_like(dv_ref)

        def pair(qp, dop, cp, lp, kp, vp, rel_f, mask):
            dq = dk = dv = None
            for e in range(2):
                lanes = first if e == 0 else second
                slope = slopes[e]
                qm = jnp.where(lanes, qp, jnp.zeros_like(qp))
                dom = jnp.where(lanes, dop, jnp.zeros_like(dop))
                km = jnp.where(lanes, kp, jnp.zeros_like(kp))
                sc = jnp.where(mask, _dot_nt(qm, kp) * 0.125 - slope * rel_f, NEG)
                pm = jnp.exp(sc - lp[:, e * HD:e * HD + 1])
                dl = pm * (_dot_nt(dom, vp) - cp[:, e * HD:e * HD + 1])
                dl16 = dl.astype(BF16)
                t_dq = _dot(dl16, km)
                t_dk = _dot_tn(dl16, qm)
                t_dv = _dot_tn(pm.astype(BF16), dom)
                dq = t_dq if dq is None else dq + t_dq
                dk = t_dk if dk is None else dk + t_dk
                dv = t_dv if dv is None else dv + t_dv
            return dq * 0.125, dk * 0.125, dv

        for pr in range(GROUP_W // (2 * HD)):
            cols = slice(pr * 2 * HD, (pr + 1) * 2 * HD)
            slopes = [_alibi_slope(pattern * HEADS_PER_PATTERN + 2 * pr + e) for e in range(2)]
            for sb in range(nsb + 1):
                gb = i * nsb + sb
                if sb < nsb:
                    rows = slice(sb * BLK, (sb + 1) * BLK)
                    qp, dop, cp, lp = q_ref[rows, cols], do_ref[rows, cols], c_ref[rows, cols], l_ref[rows, cols]
                else:
                    qp, dop, cp, lp = qn_ref[:, cols], don_ref[:, cols], cn_ref[:, cols], ln_ref[:, cols]
                if sb < nsb:
                    dq1, dk1, dv1 = pair(qp, dop, cp, lp, k_ref[rows, cols], v_ref[rows, cols], rd_f, m_diag)
                    dq_ref[rows, cols] += dq1
                    dk_ref[rows, cols] += dk1
                    dv_ref[rows, cols] += dv1
                valid = jnp.logical_and(gb % blocks_per_stream != 0, gb < n_blocks).astype(jnp.int32)
                m_prev = jnp.logical_and(rel_prev <= BLK, (rel_prev + (1 - valid) * (4 * BLK)) <= BLK)
                if sb == 0:
                    kp, vp = kp_ref[:, cols], vp_ref[:, cols]
                else:
                    prows = slice((sb - 1) * BLK, sb * BLK)
                    kp, vp = k_ref[prows, cols], v_ref[prows, cols]
                dq0, dk0, dv0 = pair(qp, dop, cp, lp, kp, vp, rp_f, m_prev)
                if sb < nsb:
                    dq_ref[rows, cols] += dq0
                if sb > 0:
                    dk_ref[prows, cols] += dk0
                    dv_ref[prows, cols] += dv0

    cur = pl.BlockSpec((tq, GROUP_W), lambda i: (i, 0))
    prev = pl.BlockSpec((BLK, GROUP_W), lambda i: (jnp.maximum(i * nsb - 1, 0), 0))
    nxt = pl.BlockSpec((BLK, GROUP_W), lambda i: (jnp.minimum((i + 1) * nsb, n_blocks - 1), 0))
    shp = _sds((s, GROUP_W), F32)
    return _call(body, name=name, out_shape=[shp, shp, shp], grid=(s // tq,),
                 in_specs=[cur] * 6 + [prev, prev] + [nxt] * 4, out_specs=[cur, cur, cur])(
                     q, k, v, d_o, cterm, lse, k, v, q, d_o, cterm, lse)


def _band_constants(d):
    row = lax.broadcasted_iota(jnp.int32, (2 * BLK, 2 * BLK), 0)
    col = lax.broadcasted_iota(jnp.int32, (2 * BLK, 2 * BLK), 1)
    rel = BLK + jnp.where(row >= BLK, row - BLK, row) - col
    band = jnp.logical_and(rel >= 0, rel <= BLK)
    return (rel * d).astype(F32), band, (col >= BLK).astype(jnp.int32)


def _stack_heads(x, first):
    zero = jnp.zeros_like(x)
    return jnp.concatenate([jnp.where(first, x, zero), jnp.where(first, zero, x)], axis=0)


def _unstack_heads(x2, first):
    return jnp.where(first, x2[:BLK], x2[BLK:])


def _head_column(x):
    return jnp.concatenate([x[:, 0:1], x[:, HD:HD + 1]], axis=0)


def attn_fwd2(q, k, v, pattern, name, tq=512):
    s = q.shape[0]
    d = ATTN_DILATIONS[pattern]
    blocks_per_stream = (s // d) // BLK
    nsb = tq // BLK

    def body(q_ref, k_ref, v_ref, kp_ref, vp_ref, o_ref, l_ref):
        i = pl.program_id(0)
        rel_f, band, own = _band_constants(d)
        first = _lane_first_half((BLK, 2 * HD))
        upper = lax.broadcasted_iota(jnp.int32, (2 * BLK, 1), 0) < BLK
        for sb in range(nsb):
            rows = slice(sb * BLK, (sb + 1) * BLK)
            has_prev = ((i * nsb + sb) % blocks_per_stream != 0).astype(jnp.int32)
            mask = jnp.logical_and(band, (own + has_prev) > 0)
            for pr in range(GROUP_W // (2 * HD)):
                cols = slice(pr * 2 * HD, (pr + 1) * 2 * HD)
                if sb == 0:
                    kcat = jnp.concatenate([kp_ref[:, cols], k_ref[rows, cols]], axis=0)
                    vcat = jnp.concatenate([vp_ref[:, cols], v_ref[rows, cols]], axis=0)
                else:
                    both = slice((sb - 1) * BLK, (sb + 1) * BLK)
                    kcat, vcat = k_ref[both, cols], v_ref[both, cols]
                h0 = pattern * HEADS_PER_PATTERN + 2 * pr
                slope = jnp.where(upper, _alibi_slope(h0), _alibi_slope(h0 + 1))
                sc = _dot_nt(_stack_heads(q_ref[rows, cols], first), kcat) * 0.125 - slope * rel_f
                sc = jnp.where(mask, sc, NEG)
                m = jnp.max(sc, axis=-1, keepdims=True)
                p = jnp.exp(sc - m)
                l = jnp.sum(p, axis=-1, keepdims=True)
                o2 = _dot((p * (1.0 / l)).astype(BF16), vcat)
                o_ref[rows, cols] = _unstack_heads(o2, first)
                lse = m + jnp.log(l)
                l_ref[rows, cols] = jnp.where(first, lse[:BLK], lse[BLK:])

    cur = pl.BlockSpec((tq, GROUP_W), lambda i: (i, 0))
    prev = pl.BlockSpec((BLK, GROUP_W), lambda i: (jnp.maximum(i * nsb - 1, 0), 0))
    return _call(body, name=name, out_shape=[_sds((s, GROUP_W), F32), _sds((s, GROUP_W), F32)], grid=(s // tq,),
                 in_specs=[cur, cur, cur, prev, prev], out_specs=[cur, cur])(q, k, v, k, v)


def attn_bwd2(q, k, v, d_o, cterm, lse, pattern, name, tq=512):
    s = q.shape[0]
    d = ATTN_DILATIONS[pattern]
    blocks_per_stream = (s // d) // BLK
    nsb = tq // BLK
    n_blocks = s // BLK

    def body(q_ref, k_ref, v_ref, do_ref, c_ref, l_ref, kp_ref, vp_ref, qn_ref, kn_ref, vn_ref, don_ref, cn_ref,
             ln_ref, dq_ref, dk_ref, dv_ref):
        i = pl.program_id(0)
        rel_f, band, own = _band_constants(d)
        first = _lane_first_half((BLK, 2 * HD))
        upper = lax.broadcasted_iota(jnp.int32, (2 * BLK, 1), 0) < BLK
        dk_ref[...] = jnp.zeros_like(dk_ref)
        dv_ref[...] = jnp.zeros_like(dv_ref)
        for sb in range(nsb + 1):
            gb = i * nsb + sb
            rows = slice(sb * BLK, (sb + 1) * BLK)
            before = slice((sb - 1) * BLK, sb * BLK)
            inside = (gb < n_blocks).astype(jnp.int32)
            has_prev = jnp.logical_and(gb % blocks_per_stream != 0, gb < n_blocks).astype(jnp.int32)
            mask = jnp.logical_and(band, (own * inside + has_prev) > 0)
            for pr in range(GROUP_W // (2 * HD)):
                cols = slice(pr * 2 * HD, (pr + 1) * 2 * HD)
                if sb == 0:
                    kcat = jnp.concatenate([kp_ref[:, cols], k_ref[rows, cols]], axis=0)
                    vcat = jnp.concatenate([vp_ref[:, cols], v_ref[rows, cols]], axis=0)
                elif sb == nsb:
                    kcat = jnp.concatenate([k_ref[before, cols], kn_ref[:, cols]], axis=0)
                    vcat = jnp.concatenate([v_ref[before, cols], vn_ref[:, cols]], axis=0)
                else:
                    both = slice((sb - 1) * BLK, (sb + 1) * BLK)
                    kcat, vcat = k_ref[both, cols], v_ref[both, cols]
                if sb < nsb:
                    qp, dop, cp, lp = q_ref[rows, cols], do_ref[rows, cols], c_ref[rows, cols], l_ref[rows, cols]
                else:
                    qp, dop, cp, lp = qn_ref[:, cols], don_ref[:, cols], cn_ref[:, cols], ln_ref[:, cols]
                h0 = pattern * HEADS_PER_PATTERN + 2 * pr
                slope = jnp.where(upper, _alibi_slope(h0), _alibi_slope(h0 + 1))
                q2 = _stack_heads(qp, first)
                do2 = _stack_heads(dop, first)
                sc = jnp.where(mask, _dot_nt(q2, kcat) * 0.125 - slope * rel_f, NEG)
                pm = jnp.exp(sc - _head_column(lp))
                dl = (pm * (_dot_nt(do2, vcat) - _head_column(cp))).astype(BF16)
                if sb < nsb:
                    dq_ref[rows, cols] = _unstack_heads(_dot(dl, kcat), first) * 0.125
                dk2 = _dot_tn(dl, q2) * 0.125
                dv2 = _dot_tn(pm.astype(BF16), do2)
                if sb > 0:
                    dk_ref[before, cols] += dk2[:BLK]
                    dv_ref[before, cols] += dv2[:BLK]
                if sb < nsb:
                    dk_ref[rows, cols] += dk2[BLK:]
                    dv_ref[rows, cols] += dv2[BLK:]

    cur = pl.BlockSpec((tq, GROUP_W), lambda i: (i, 0))
    prev = pl.BlockSpec((BLK, GROUP_W), lambda i: (jnp.maximum(i * nsb - 1, 0), 0))
    nxt = pl.BlockSpec((BLK, GROUP_W), lambda i: (jnp.minimum((i + 1) * nsb, n_blocks - 1), 0))
    shp = _sds((s, GROUP_W), F32)
    return _call(body, name=name, out_shape=[shp, shp, shp], grid=(s // tq,),
                 in_specs=[cur] * 6 + [prev, prev] + [nxt] * 6, out_specs=[cur, cur, cur])(
                     q, k, v, d_o, cterm, lse, k, v, q, k, v, d_o, cterm, lse)


HALO = 16
CONV_TQ = 512


def conv_fwd(p, w, b, name):
    s = p.shape[0]
    tq = CONV_TQ
    ncol = SSD_CONV_DIM // GROUP_W
    cb0 = COL_XBC // GROUP_W

    def body(u_ref, up_ref, w_ref, b_ref, c_ref, xc_ref):
        i = pl.program_id(0)
        prev = up_ref[...].astype(F32) * (i > 0).astype(F32)
        ext = jnp.concatenate([prev, u_ref[...].astype(F32)], axis=0)
        acc = b_ref[...] + w_ref[SSD_CONV - 1:SSD_CONV, :] * ext[HALO:HALO + tq]
        for kk in range(SSD_CONV - 1):
            off = HALO - (SSD_CONV - 1) + kk
            acc += w_ref[kk:kk + 1, :] * ext[off:off + tq]
        c_ref[...] = acc.astype(BF16)
        xc_ref[...] = (acc * _sigmoid(acc)).astype(BF16)

    cur_in = pl.BlockSpec((tq, GROUP_W), lambda i, j: (i, cb0 + j))
    prev_in = pl.BlockSpec((HALO, GROUP_W), lambda i, j: (jnp.maximum(i * (tq // HALO) - 1, 0), cb0 + j))
    cur_out = pl.BlockSpec((tq, GROUP_W), lambda i, j: (i, j))
    shp = _sds((s, SSD_CONV_DIM), BF16)
    return _call(body, name=name, out_shape=[shp, shp], grid=(s // tq, ncol),
                 in_specs=[cur_in, prev_in, pl.BlockSpec((SSD_CONV, GROUP_W), lambda i, j: (0, j)),
                           pl.BlockSpec((1, GROUP_W), lambda i, j: (0, j))],
                 out_specs=[cur_out, cur_out])(p, p, w, b)


def conv_bwd(p, cpre, dxc, w, name):
    s = p.shape[0]
    tq = CONV_TQ
    ncol = SSD_CONV_DIM // GROUP_W
    cb0 = COL_XBC // GROUP_W
    nt = s // tq

    def body(u_ref, up_ref, c_ref, cn_ref, d_ref, dn_ref, w_ref, du_ref, dw_ref, db_ref):
        i = pl.program_id(1)

        def dpre(c16, dx):
            c = c16.astype(F32)
            sg = _sigmoid(c)
            return dx * (sg * (1.0 + c * (1.0 - sg)))

        dc = dpre(c_ref[...], d_ref[...])
        dcn = dpre(cn_ref[...], dn_ref[...]) * (i < nt - 1).astype(F32)
        dext = jnp.concatenate([dc, dcn], axis=0)
        prev = up_ref[...].astype(F32) * (i > 0).astype(F32)
        uext = jnp.concatenate([prev, u_ref[...].astype(F32)], axis=0)

        @pl.when(i == 0)
        def _():
            dw_ref[...] = jnp.zeros_like(dw_ref)
            db_ref[...] = jnp.zeros_like(db_ref)

        du = w_ref[SSD_CONV - 1:SSD_CONV, :] * dc
        for kk in range(SSD_CONV - 1):
            sh = SSD_CONV - 1 - kk
            du += w_ref[kk:kk + 1, :] * dext[sh:sh + tq]
        du_ref[...] = du.astype(BF16)
        for kk in range(SSD_CONV):
            off = HALO - (SSD_CONV - 1) + kk
            dw_ref[kk:kk + 1, :] += jnp.sum(dc * uext[off:off + tq], axis=0, keepdims=True)
        db_ref[...] += jnp.sum(dc, axis=0, keepdims=True)

    hb = tq // HALO
    cur_p = pl.BlockSpec((tq, GROUP_W), lambda j, i: (i, cb0 + j))
    prev_p = pl.BlockSpec((HALO, GROUP_W), lambda j, i: (jnp.maximum(i * hb - 1, 0), cb0 + j))
    cur = pl.BlockSpec((tq, GROUP_W), lambda j, i: (i, j))
    nxt = pl.BlockSpec((HALO, GROUP_W), lambda j, i: (jnp.minimum((i + 1) * hb, s // HALO - 1), j))
    return _call(body, name=name,
                 out_shape=[_sds((s, SSD_CONV_DIM), BF16), _sds((8, SSD_CONV_DIM), F32), _sds((1, SSD_CONV_DIM), F32)],
                 grid=(ncol, nt),
                 in_specs=[cur_p, prev_p, cur, nxt, cur, nxt, pl.BlockSpec((SSD_CONV, GROUP_W), lambda j, i: (0, j))],
                 out_specs=[cur, pl.BlockSpec((8, GROUP_W), lambda j, i: (0, j)),
                            pl.BlockSpec((1, GROUP_W), lambda j, i: (0, j))])(p, p, cpre, cpre, dxc, dxc, w)


def _softplus(x):
    return jnp.maximum(x, 0.0) + jnp.log(1.0 + jnp.exp(-jnp.abs(x)))


def _ssd_decays(dtr_ref, dtrt_ref, bias_ref, biast_ref, alog_ref, alogt_ref):
    row = lax.broadcasted_iota(jnp.int32, (BLK, BLK), 0)
    col = lax.broadcasted_iota(jnp.int32, (BLK, BLK), 1)
    lower = (row >= col).astype(F32)
    upper = (row <= col).astype(F32)
    dtb = dtr_ref[...] + bias_ref[...]
    dt = _softplus(dtb)
    a = dt * (-jnp.exp(alog_ref[...]))
    cs = _dot_hi(lower, a)
    a_t = _softplus(dtrt_ref[...] + biast_ref[...]) * (-jnp.exp(alogt_ref[...]))
    cs_t = _dot_hi(a_t, upper)
    return dtb, dt, cs, cs_t, row, col, upper


def ssd_fwd(p, xc, dtg, dtg_t, params, gn, name):
    s = p.shape[0]
    nc = s // BLK
    bias, bias_t, alog, alog_t, dskip = params

    def body(xs_ref, b_ref, c_ref, z_ref, dtr_ref, dtrt_ref, bias_ref, biast_ref, alog_ref, alogt_ref, dsk_ref,
             gn_ref, y_ref, sin_ref, hp_ref, h_ref):
        c_idx = pl.program_id(1)

        @pl.when(c_idx == 0)
        def _():
            h_ref[...] = jnp.zeros_like(h_ref)

        _, dt, cs, cs_t, row, col, _ = _ssd_decays(dtr_ref, dtrt_ref, bias_ref, biast_ref, alog_ref, alogt_ref)
        first = _lane_first_half((BLK, 2 * HD))
        first_row = _lane_first_half((1, 2 * HD))
        tril = row >= col
        b16, c16 = b_ref[...], c_ref[...]
        cb = _dot_nt(c16, b16)
        n_pairs = GROUP_W // (2 * HD)
        tot = cs[BLK - 1:BLK, :]
        exp_cs, exp_rest, exp_tot = jnp.exp(cs), jnp.exp(tot - cs), jnp.exp(tot)

        def per_head(v, mask):
            return jnp.concatenate([jnp.where(mask, v[:, 2 * pr:2 * pr + 1], v[:, 2 * pr + 1:2 * pr + 2])
                                    for pr in range(n_pairs)], axis=1)

        xs = xs_ref[...].astype(F32)
        xt = xs * per_head(dt, first)
        xt16 = xt.astype(BF16)
        hstate = jnp.concatenate([h_ref[pr] for pr in range(n_pairs)], axis=1)
        for pr in range(n_pairs):
            hp_ref[pr] = h_ref[pr]
        y_off = per_head(exp_cs, first) * _dot(c16, hstate.astype(BF16))
        new = per_head(exp_tot, first_row) * hstate + _dot_tn(b16, (per_head(exp_rest, first) * xt).astype(BF16))
        for pr in range(n_pairs):
            h_ref[pr] = new[:, pr * 2 * HD:(pr + 1) * 2 * HD]
        y_diag = []
        for pr in range(n_pairs):
            cols = slice(pr * 2 * HD, (pr + 1) * 2 * HD)
            m2 = jnp.concatenate(
                [(cb * jnp.exp(jnp.where(tril, cs[:, h:h + 1] - cs_t[h:h + 1, :], NEG))).astype(BF16)
                 for h in (2 * pr, 2 * pr + 1)], axis=1)
            y_diag.append(_dot(m2, _stack_heads(xt16[:, cols], first)))
        y = jnp.concatenate(y_diag, axis=1) + y_off + xs * per_head(dsk_ref[...], first_row)
        y_ref[...] = y
        zv = z_ref[...].astype(F32)
        yz = y * (zv * _sigmoid(zv))
        r = lax.rsqrt(jnp.mean(yz * yz, axis=-1, keepdims=True) + EPS)
        sin_ref[...] = (yz * r * gn_ref[...]).astype(BF16)

    nb0 = SSD_INNER // BLK
    gparam = pl.BlockSpec((None, 1, 8), lambda g, c: (g, 0, 0))
    gparam_t = pl.BlockSpec((None, 8, 1), lambda g, c: (g, 0, 0))
    return _call(
        body, name=name,
        out_shape=[_sds((s, SSD_INNER), F32), _sds((s, SSD_INNER), BF16),
                   _sds((SSD_GROUPS, nc, 4, BLK, 2 * HD), F32)],
        grid=(SSD_GROUPS, nc),
        in_specs=[pl.BlockSpec((BLK, GROUP_W), lambda g, c: (c, g)),
                  pl.BlockSpec((BLK, BLK), lambda g, c: (c, nb0 + g)),
                  pl.BlockSpec((BLK, BLK), lambda g, c: (c, nb0 + SSD_GROUPS + g)),
                  pl.BlockSpec((BLK, GROUP_W), lambda g, c: (c, COL_Z // GROUP_W + g)),
                  pl.BlockSpec((None, BLK, 8), lambda g, c: (g, c, 0)),
                  pl.BlockSpec((None, 8, BLK), lambda g, c: (g, 0, c)),
                  gparam, gparam_t, gparam, gparam_t, gparam,
                  pl.BlockSpec((1, GROUP_W), lambda g, c: (0, g))],
        out_specs=[pl.BlockSpec((BLK, GROUP_W), lambda g, c: (c, g)),
                   pl.BlockSpec((BLK, GROUP_W), lambda g, c: (c, g)),
                   pl.BlockSpec((None, None, 4, BLK, 2 * HD), lambda g, c: (g, c, 0, 0, 0))],
        scratch_shapes=[pltpu.VMEM((4, BLK, 2 * HD), F32)],
    )(xc, xc, xc, p, dtg, dtg_t, bias, bias_t, alog, alog_t, dskip, gn)


def ssd_bwd(p, xc, y, d_sin, hprev, dtg, dtg_t, params, gn, name):
    s = p.shape[0]
    nc = s // BLK
    bias, bias_t, alog, alog_t, dskip = params

    def body(xs_ref, b_ref, c_ref, z_ref, y_ref, dsin_ref, hp_ref, dtr_ref, dtrt_ref, bias_ref,
             biast_ref, alog_ref, alogt_ref, dsk_ref, gn_ref,
             dxs_ref, db_ref, dc_ref, dz_ref, ddt_ref, da_ref, dbias_ref, ddsk_ref, dgn_ref, dh_ref):
        c_idx = pl.program_id(1)

        @pl.when(c_idx == 0)
        def _():
            dh_ref[...] = jnp.zeros_like(dh_ref)
            da_ref[...] = jnp.zeros_like(da_ref)
            dbias_ref[...] = jnp.zeros_like(dbias_ref)
            ddsk_ref[...] = jnp.zeros_like(ddsk_ref)
            dgn_ref[...] = jnp.zeros_like(dgn_ref)

        dtb, dt, cs, cs_t, row, col, upper = _ssd_decays(dtr_ref, dtrt_ref, bias_ref, biast_ref, alog_ref, alogt_ref)
        first = _lane_first_half((BLK, 2 * HD))
        second = jnp.logical_not(first)
        first_row = _lane_first_half((1, 2 * HD))
        tril = row >= col
        triu = row <= col
        last_row = lax.broadcasted_iota(jnp.int32, (BLK, 1), 0) == BLK - 1
        lane8 = lax.broadcasted_iota(jnp.int32, (BLK, 8), 1)

        yv = y_ref[...]
        zv = z_ref[...].astype(F32)
        sg = _sigmoid(zv)
        yz = yv * (zv * sg)
        r = lax.rsqrt(jnp.mean(yz * yz, axis=-1, keepdims=True) + EPS)
        yzn = yz * r
        dsn = dsin_ref[...]
        dgn_ref[...] += jnp.sum(dsn * yzn, axis=0, keepdims=True)
        dsn = dsn * gn_ref[...]
        dyz = r * (dsn - yzn * jnp.mean(dsn * yzn, axis=-1, keepdims=True))
        dy = dyz * (zv * sg)
        dz_ref[...] = (dyz * yv * (sg * (1.0 + zv * (1.0 - sg)))).astype(BF16)
        xs_all = xs_ref[...].astype(F32)
        ddsk_ref[...] += jnp.sum(dy * xs_all, axis=0, keepdims=True)

        b16, c16 = b_ref[...], c_ref[...]
        cb = _dot_nt(c16, b16)
        cb_t = _dot_nt(b16, c16)
        n_pairs = GROUP_W // (2 * HD)
        tot = cs[BLK - 1:BLK, :]
        exp_cs, exp_rest, exp_tot = jnp.exp(cs), jnp.exp(tot - cs), jnp.exp(tot)

        def per_head(v, mask):
            return jnp.concatenate([jnp.where(mask, v[:, 2 * pr:2 * pr + 1], v[:, 2 * pr + 1:2 * pr + 2])
                                    for pr in range(n_pairs)], axis=1)

        def head_sums(v):
            out = jnp.zeros((BLK, 8), F32)
            for pr in range(n_pairs):
                sa, sb = _pair_sum(v[:, pr * 2 * HD:(pr + 1) * 2 * HD], first)
                out = jnp.where(lane8 == 2 * pr, sa, jnp.where(lane8 == 2 * pr + 1, sb, out))
            return out

        dt_w, e_w, f_w = per_head(dt, first), per_head(exp_cs, first), per_head(exp_rest, first)
        xt = xs_all * dt_w
        xt16 = xt.astype(BF16)
        hstate = jnp.concatenate([hp_ref[pr] for pr in range(n_pairs)], axis=1)
        h16 = hstate.astype(BF16)
        dhn = jnp.concatenate([dh_ref[pr] for pr in range(n_pairs)], axis=1)
        dhn16 = dhn.astype(BF16)
        edy16 = (e_w * dy).astype(BF16)
        y_off = e_w * _dot(c16, h16)
        dcs_all = head_sums(dy * y_off)
        dc_acc = _dot_nt(edy16, h16)
        zmat = _dot(b16, dhn16)
        t_all = head_sums(zmat * xt) * exp_rest
        hh_rows = jnp.sum(head_sums(dhn * hstate), axis=0, keepdims=True)
        dtot = jnp.sum(t_all, axis=0, keepdims=True) + hh_rows * exp_tot
        dcs_all = dcs_all - t_all + jnp.where(last_row, dtot, 0.0)
        fxt16 = (f_w * xt).astype(BF16)
        db_acc = _dot_nt(fxt16, dhn16)
        dh_new = _dot_tn(c16, edy16) + per_head(exp_tot, first_row) * dhn
        for pr in range(n_pairs):
            dh_ref[pr] = dh_new[:, pr * 2 * HD:(pr + 1) * 2 * HD]
        g_sum = jnp.zeros((BLK, BLK), F32)
        gt_sum = jnp.zeros((BLK, BLK), F32)
        d_xt_parts = []
        for pr in range(n_pairs):
            cols = slice(pr * 2 * HD, (pr + 1) * 2 * HD)
            dym2 = _stack_heads(dy[:, cols].astype(BF16), first)
            d_m2 = _dot_nt(dym2, xt16[:, cols])
            d_mt2 = _dot_nt(xt16[:, cols], dym2)
            mt2 = []
            for e, h in enumerate((2 * pr, 2 * pr + 1)):
                cs_c, cs_r = cs[:, h:h + 1], cs_t[h:h + 1, :]
                decay = jnp.exp(jnp.where(tril, cs_c - cs_r, NEG))
                decay_t = jnp.exp(jnp.where(triu, cs_r - cs_c, NEG))
                gm = d_m2[e * BLK:(e + 1) * BLK] * decay
                gmt = d_mt2[:, e * BLK:(e + 1) * BLK] * decay_t
                g_sum += gm
                gt_sum += gmt
                dcs_h = jnp.sum(gm * cb, axis=-1, keepdims=True) - jnp.sum(gmt * cb_t, axis=-1, keepdims=True)
                dcs_all = dcs_all + jnp.where(lane8 == h, dcs_h, 0.0)
                mt2.append((cb_t * decay_t).astype(BF16))
            d_xt_parts.append(_dot(jnp.concatenate(mt2, axis=1), dym2))
        d_xt = jnp.concatenate(d_xt_parts, axis=1) + f_w * zmat
        dxs_ref[...] = dy * per_head(dsk_ref[...], first_row) + d_xt * dt_w
        ddtx_all = head_sums(d_xt * xs_all)

        dc_ref[...] = dc_acc + _dot(g_sum.astype(BF16), b16)
        db_ref[...] = db_acc + _dot(gt_sum.astype(BF16), c16)
        d_a = _dot_hi(upper, dcs_all)
        a_neg = -jnp.exp(alog_ref[...])
        ddt = ddtx_all + d_a * a_neg
        da_ref[...] += jnp.sum(d_a * dt, axis=0, keepdims=True)
        ddtr = ddt * _sigmoid(dtb)
        ddt_ref[...] = ddtr
        dbias_ref[...] += jnp.sum(ddtr, axis=0, keepdims=True)

    nb0 = SSD_INNER // BLK
    rc = lambda c: nc - 1 - c
    gparam = pl.BlockSpec((None, 1, 8), lambda g, c: (g, 0, 0))
    gparam_t = pl.BlockSpec((None, 8, 1), lambda g, c: (g, 0, 0))
    wide = pl.BlockSpec((BLK, GROUP_W), lambda g, c: (rc(c), g))
    narrow = pl.BlockSpec((BLK, BLK), lambda g, c: (rc(c), g))
    return _call(
        body, name=name,
        out_shape=[_sds((s, SSD_INNER), F32), _sds((s, GROUP_W), F32), _sds((s, GROUP_W), F32),
                   _sds((s, SSD_INNER), BF16), _sds((SSD_GROUPS, s, 8), F32),
                   _sds((SSD_GROUPS, 1, 8), F32), _sds((SSD_GROUPS, 1, 8), F32),
                   _sds((SSD_GROUPS, 1, GROUP_W), F32), _sds((1, SSD_INNER), F32)],
        grid=(SSD_GROUPS, nc),
        in_specs=[wide,
                  pl.BlockSpec((BLK, BLK), lambda g, c: (rc(c), nb0 + g)),
                  pl.BlockSpec((BLK, BLK), lambda g, c: (rc(c), nb0 + SSD_GROUPS + g)),
                  pl.BlockSpec((BLK, GROUP_W), lambda g, c: (rc(c), COL_Z // GROUP_W + g)),
                  wide, wide,
                  pl.BlockSpec((None, None, 4, BLK, 2 * HD), lambda g, c: (g, rc(c), 0, 0, 0)),
                  pl.BlockSpec((None, BLK, 8), lambda g, c: (g, rc(c), 0)),
                  pl.BlockSpec((None, 8, BLK), lambda g, c: (g, 0, rc(c))),
                  gparam, gparam_t, gparam, gparam_t, gparam,
                  pl.BlockSpec((1, GROUP_W), lambda g, c: (0, g))],
        out_specs=[wide, narrow, narrow, wide,
                   pl.BlockSpec((None, BLK, 8), lambda g, c: (g, rc(c), 0)),
                   gparam, gparam,
                   pl.BlockSpec((None, 1, GROUP_W), lambda g, c: (g, 0, 0)),
                   pl.BlockSpec((1, GROUP_W), lambda g, c: (0, g))],
        scratch_shapes=[pltpu.VMEM((4, BLK, 2 * HD), F32)],
    )(xc, xc, xc, p, y, d_sin, hprev, dtg, dtg_t, bias, bias_t, alog, alog_t, dskip, gn)


def merge_fwd(p, a, sbr, name, tm=512):
    s = p.shape[0]
    nj = D_MODEL // GROUP_W

    def body(ga_ref, gs_ref, a_ref, s_ref, o_ref):
        o_ref[...] = (_sigmoid(ga_ref[...].astype(F32)) * a_ref[...]
                      + _sigmoid(gs_ref[...].astype(F32)) * s_ref[...]).astype(BF16)

    blk = pl.BlockSpec((tm, GROUP_W), lambda i, j: (i, j))
    return _call(body, name=name, out_shape=_sds((s, D_MODEL), BF16), grid=(s // tm, nj),
                 in_specs=[pl.BlockSpec((tm, GROUP_W), lambda i, j: (i, COL_GA // GROUP_W + j)),
                           pl.BlockSpec((tm, GROUP_W), lambda i, j: (i, COL_GS // GROUP_W + j)), blk, blk],
                 out_specs=blk)(p, p, a, sbr)


def merge_bwd(p, a, sbr, dmerged, name, tm=512):
    s = p.shape[0]
    nj = D_MODEL // GROUP_W

    def body(ga_ref, gs_ref, a_ref, s_ref, dm_ref, da_ref, ds_ref, dga_ref, dgs_ref):
        dm = dm_ref[...]
        sa = _sigmoid(ga_ref[...].astype(F32))
        ss = _sigmoid(gs_ref[...].astype(F32))
        da_ref[...] = (dm * sa).astype(BF16)
        ds_ref[...] = (dm * ss).astype(BF16)
        dga_ref[...] = (dm * a_ref[...] * sa * (1.0 - sa)).astype(BF16)
        dgs_ref[...] = (dm * s_ref[...] * ss * (1.0 - ss)).astype(BF16)

    blk = pl.BlockSpec((tm, GROUP_W), lambda i, j: (i, j))
    shp = _sds((s, D_MODEL), BF16)
    return _call(body, name=name, out_shape=[shp] * 4, grid=(s // tm, nj),
                 in_specs=[pl.BlockSpec((tm, GROUP_W), lambda i, j: (i, COL_GA // GROUP_W + j)),
                           pl.BlockSpec((tm, GROUP_W), lambda i, j: (i, COL_GS // GROUP_W + j)), blk, blk, blk],
                 out_specs=[blk] * 4)(p, p, a, sbr, dmerged)


def _group_major(v):
    return v.reshape(SSD_GROUPS, 1, 8), v.reshape(SSD_GROUPS, 8, 1)


def mixer_forward(x, w, rider=None, later_weights=None):
    s = x.shape[0]
    h = rms_fwd(x, w["mix_norm"], "mix_rms")
    p = matmul_nn(h, w["w_in_main"], "mix_proj", BF16, tm=1024, tn=512, rider=rider)
    rode = None
    if rider is not None:
        p, rode = p
        w = dict(w, **later_weights(rode))
    dt_raw = matmul_nn(h, w["w_in_dt"], "mix_proj_dt", F32, tm=1024, tn=DT_PAD)
    qn, kn = qk_norm_fwd(p, w["q_gain"], w["k_gain"], "qk_norm")
    streams, os_, lses = [], [], []
    for g, d in enumerate(ATTN_DILATIONS):
        cols = slice(g * GROUP_W, (g + 1) * GROUP_W)
        qs, ks = _to_streams(qn[:, cols], d), _to_streams(kn[:, cols], d)
        vs = _to_streams(p[:, COL_V + g * GROUP_W:COL_V + (g + 1) * GROUP_W], d)
        o, lse = attn_fwd2(qs, ks, vs, g, f"attn_fwd{g}")
        streams.append((qs, ks, vs, lse))
        os_.append(_from_streams(o, d))
        lses.append(_from_streams(lse, d))
    attn_o = attn_merge_fwd(os_, lses, "attn_merge")
    cpre, xc = conv_fwd(p, w["conv_w"], w["conv_b"], "conv_fwd")
    dtg = dt_raw[:, :SSD_HEADS].reshape(s, SSD_GROUPS, 8).transpose(1, 0, 2)
    dtg_t = dtg.transpose(0, 2, 1)
    params = (*_group_major(w["dt_bias"]), *_group_major(w["a_log"]), _group_major(w["d_skip"])[0])
    y, s_in, hprev = ssd_fwd(p, xc, dtg, dtg_t, params, w["ssd_norm"], "ssd_fwd")
    a = matmul_nn(attn_o, w["w_attn_branch"], "attn_branch", F32, tm=1024, tn=512)
    sbr = matmul_nn(s_in, w["w_ssd_branch"], "ssd_branch", F32, tm=1024, tn=512)
    merged = merge_fwd(p, a, sbr, "merge")
    x_out = matmul_nn(merged, w["w_out"], "mix_out", F32, tm=1024, tn=512, res=x)
    saved = dict(h=h, p=p, streams=streams, os=os_, lses=lses, attn_o=attn_o, cpre=cpre, xc=xc, dtg=dtg,
                 dtg_t=dtg_t, params=params, y=y, s_in=s_in, hprev=hprev, a=a, sbr=sbr, merged=merged, w=w)
    return x_out, saved, rode


def mixer_backward(dx_out, x, sv, ride_early=None, ride_late=None):
    s = x.shape[0]
    p = sv["p"]
    w = sv["w"]
    g = {}
    dmerged = matmul_nt(dx_out, w["w_out"], "d_merged", F32, tm=1024, tn=512, tk=1024)
    g["w_out"] = matmul_tn(sv["merged"], dx_out, "dw_out", tn=512, ts=1024)
    da, ds, dga, dgs = merge_bwd(p, sv["a"], sv["sbr"], dmerged, "merge_bwd")
    g["w_attn_branch"] = matmul_tn(sv["attn_o"], da, "dw_attn_branch", tn=512, ts=1024)
    g["w_ssd_branch"] = matmul_tn(sv["s_in"], ds, "dw_ssd_branch", tn=512, ts=1024)
    d_attn_o = matmul_nt(da, w["w_attn_branch"], "d_attn_o", F32, tm=1024, tn=512, tk=1024)
    d_sin = matmul_nt(ds, w["w_ssd_branch"], "d_ssd_in", F32, tm=1024, tn=512, tk=1024)
    dxs, d_b, d_c, dz, ddt, d_asum, d_bias, d_dsk, d_gn = ssd_bwd(
        p, sv["xc"], sv["y"], d_sin, sv["hprev"], sv["dtg"], sv["dtg_t"], sv["params"], w["ssd_norm"], "ssd_bwd")
    dxc = jnp.concatenate([dxs, d_b, d_c], axis=1)
    dxbc, d_convw, d_convb = conv_bwd(p, sv["cpre"], dxc, w["conv_w"], "conv_bwd")
    g["conv_w"] = d_convw[:SSD_CONV]
    g["conv_b"] = d_convb
    g["dt_bias"] = d_bias.reshape(1, SSD_HEADS)
    g["a_log"] = (d_asum * (-jnp.exp(sv["params"][2]))).reshape(1, SSD_HEADS)
    g["d_skip"] = jnp.sum(d_dsk.reshape(SSD_HEADS, HD), axis=1).reshape(1, SSD_HEADS)
    g["ssd_norm"] = d_gn
    merged_bwd = attn_merge_bwd(d_attn_o, sv["os"], sv["lses"], "attn_merge_bwd")
    dqs, dks, dvs = [], [], []
    for gi, d in enumerate(ATTN_DILATIONS):
        qs, ks, vs, lse = sv["streams"][gi]
        d_o = _to_streams(merged_bwd[gi], d)
        cterm = _to_streams(merged_bwd[3 + gi], d)
        dq, dk, dv = attn_bwd2(qs, ks, vs, d_o, cterm, lse, gi, f"attn_bwd{gi}")
        dqs.append(_from_streams(dq, d))
        dks.append(_from_streams(dk, d))
        dvs.append(_from_streams(dv, d).astype(BF16))
    dq, dk, d_qg, d_kg = qk_norm_bwd(p, dqs, dks, w["q_gain"], w["k_gain"], "qk_norm_bwd")
    g["q_norm"] = jnp.sum(d_qg.reshape(N_ATTN_HEADS, HD), axis=0).reshape(1, HD)
    g["k_norm"] = jnp.sum(d_kg.reshape(N_ATTN_HEADS, HD), axis=0).reshape(1, HD)
    dp = jnp.concatenate([dq, dk] + dvs + [dz, dxbc, dga, dgs], axis=1)
    ddt_pad = jnp.pad(ddt.transpose(1, 0, 2).reshape(s, SSD_HEADS), ((0, 0), (0, DT_PAD - SSD_HEADS)))
    if ride_early is not None:
        g["w_in_main"], g["rode_early"] = matmul_tn(sv["h"], dp, "dw_in", tn=512, ts=1024, rider=ride_early(g))
    else:
        g["w_in_main"] = matmul_tn(sv["h"], dp, "dw_in", tn=512, ts=1024)
    g["w_in_dt"] = matmul_tn(sv["h"], ddt_pad, "dw_in_dt", tn=DT_PAD, ts=1024)
    if ride_late is not None:
        dh_main, g["rode_late"] = matmul_nt(dp, w["w_in_main"], "d_mix_h", F32, tm=1024, tn=512, tk=512,
                                            rider=ride_late(g))
    else:
        dh_main = matmul_nt(dp, w["w_in_main"], "d_mix_h", F32, tm=1024, tn=512, tk=512)
    dh_dt = matmul_nt(ddt_pad, w["w_in_dt"], "d_mix_h_dt", F32, tm=1024, tn=1024, tk=DT_PAD)
    dx, g["mix_norm"] = rms_bwd([dh_main, dh_dt], x, w["mix_norm"], dx_out, "mix_drms")
    return dx, g


ANY = pl.BlockSpec(memory_space=pl.ANY)


def _place():
    x, y, c = lax.axis_index("x"), lax.axis_index("y"), lax.axis_index("c")
    chips = [(1 - x, y), (x, 1 - y), (1 - x, 1 - y)]
    return x, y, c, 2 * x + y, chips


def _comm_call(body, *, name, out_shape, n_in, scratch_shapes, aliases=None):
    return pl.pallas_call(
        body, out_shape=out_shape, in_specs=[ANY] * n_in, out_specs=[ANY] * len(out_shape),
        scratch_shapes=scratch_shapes, input_output_aliases=aliases or {}, name=name,
        compiler_params=pltpu.CompilerParams(has_side_effects=True))


def gather_weights(shards, small):
    n = len(shards)
    halves = [a.shape[0] // 2 for a in shards]
    out_shape = [_sds((N_CHIP,) + a.shape, a.dtype) for a in shards] + [_sds((N_CHIP,) + small.shape, small.dtype)]

    def body(*refs):
        ins, outs = refs[:n + 1], refs[n + 1:2 * n + 2]
        send1, recv1, send2, recv2, local = refs[2 * n + 2:]
        x, y, c, me, chips = _place()
        sibling = (x, y, 1 - c)

        def rows(k, chip, core):
            if k == n:
                return outs[k].at[chip]
            return outs[k].at[chip, pl.ds(core * halves[k], halves[k])]

        def level1(k, t, incoming):
            chip = 2 * chips[t][0] + chips[t][1]
            src = ins[k] if k == n else ins[k].at[pl.ds(c * halves[k], halves[k])]
            return pltpu.make_async_remote_copy(
                src_ref=src, dst_ref=rows(k, chip if incoming else me, c), send_sem=send1.at[3 * k + t],
                recv_sem=recv1.at[3 * k + t], device_id=(*chips[t], c), device_id_type=MESH)

        def level2(k, t, incoming):
            chip = 2 * chips[t][0] + chips[t][1]
            core = (1 - c) if incoming else c
            return pltpu.make_async_remote_copy(
                src_ref=rows(k, chip, core), dst_ref=rows(k, chip, core), send_sem=send2.at[3 * k + t],
                recv_sem=recv2.at[3 * k + t], device_id=sibling, device_id_type=MESH)

        own = [pltpu.make_async_copy(ins[k], outs[k].at[me], local.at[k]) for k in range(n + 1)]
        for cp in own:
            cp.start()
        first = [level1(k, t, False) for k in range(n + 1) for t in range(3)]
        for cp in first:
            cp.start()
        passed = []
        for k in range(n + 1):
            for t in range(3):
                level1(k, t, True).wait_recv()
                if k < n:
                    cp = level2(k, t, False)
                    cp.start()
                    passed.append(cp)
        for k in range(n):
            for t in range(3):
                level2(k, t, True).wait_recv()
        for cp in first + passed:
            cp.wait_send()
        for cp in own:
            cp.wait()

    dma = pltpu.SemaphoreType.DMA
    return _comm_call(body, name="gather_weights", out_shape=out_shape, n_in=n + 1,
                      scratch_shapes=[dma((3 * n + 3,)), dma((3 * n + 3,)), dma((3 * n,)), dma((3 * n,)),
                                      dma((n + 1,))])(*shards, small)


def reduce_to_sibling(grads):
    n = len(grads)
    halves = [a.shape[1] // 2 for a in grads]
    shapes = [_sds((N_CHIP, h, a.shape[2]), a.dtype) for a, h in zip(grads, halves)]

    def body(*refs):
        ins, got, kept = refs[:n], refs[n:2 * n], refs[2 * n:3 * n]
        send, recv, local = refs[3 * n:]
        x, y, c, _, _ = _place()
        copies, locals_ = [], []
        for k in range(n):
            h = halves[k]
            locals_.append(pltpu.make_async_copy(ins[k].at[:, pl.ds(c * h, h)], kept[k], local.at[k]))
            copies.append(pltpu.make_async_remote_copy(
                src_ref=ins[k].at[:, pl.ds((1 - c) * h, h)], dst_ref=got[k], send_sem=send.at[k], recv_sem=recv.at[k],
                device_id=(x, y, 1 - c), device_id_type=MESH))
        for cp in locals_ + copies:
            cp.start()
        for cp in copies:
            cp.wait_recv()
        for cp in copies:
            cp.wait_send()
        for cp in locals_:
            cp.wait()

    dma = pltpu.SemaphoreType.DMA
    res = _comm_call(body, name="reduce_to_sibling", out_shape=shapes + shapes, n_in=n,
                     scratch_shapes=[dma((n,)), dma((n,)), dma((n,))])(*grads)
    return res[:n], res[n:]


def reduce_to_owner(sums):
    n = len(sums)
    shapes = [_sds(a.shape, a.dtype) for a in sums]

    def body(*refs):
        ins, outs = refs[:n], refs[n:2 * n]
        send, recv, local = refs[2 * n:]
        x, y, c, me, chips = _place()
        copies, locals_ = [], []
        for k in range(n):
            locals_.append(pltpu.make_async_copy(ins[k].at[me], outs[k].at[3], local.at[k]))
            for t in range(3):
                chip = 2 * chips[t][0] + chips[t][1]
                copies.append(pltpu.make_async_remote_copy(
                    src_ref=ins[k].at[chip], dst_ref=outs[k].at[t], send_sem=send.at[3 * k + t],
                    recv_sem=recv.at[3 * k + t], device_id=(*chips[t], c), device_id_type=MESH))
        for cp in locals_ + copies:
            cp.start()
        for cp in copies:
            cp.wait_recv()
        for cp in copies:
            cp.wait_send()
        for cp in locals_:
            cp.wait()

    dma = pltpu.SemaphoreType.DMA
    return _comm_call(body, name="reduce_to_owner", out_shape=shapes, n_in=n,
                      scratch_shapes=[dma((3 * n,)), dma((3 * n,)), dma((n,))])(*sums)


def share_with_sibling(halves_):
    n = len(halves_)
    shapes = [_sds((2 * a.shape[0], a.shape[1]), a.dtype) for a in halves_]

    def body(*refs):
        ins, outs = refs[:n], refs[n:2 * n]
        send, recv, local = refs[2 * n:]
        x, y, c, _, _ = _place()
        copies, locals_ = [], []
        for k in range(n):
            h = ins[k].shape[0]
            mine = outs[k].at[pl.ds(c * h, h)]
            locals_.append(pltpu.make_async_copy(ins[k], mine, local.at[k]))
            copies.append(pltpu.make_async_remote_copy(
                src_ref=ins[k], dst_ref=mine, send_sem=send.at[k], recv_sem=recv.at[k],
                device_id=(x, y, 1 - c), device_id_type=MESH))
        for cp in locals_ + copies:
            cp.start()
        for cp in copies:
            cp.wait_recv()
        for cp in copies:
            cp.wait_send()
        for cp in locals_:
            cp.wait()

    dma = pltpu.SemaphoreType.DMA
    return _comm_call(body, name="share_with_sibling", out_shape=shapes, n_in=n,
                      scratch_shapes=[dma((n,)), dma((n,)), dma((n,))])(*halves_)


def _cores():
    c = lax.axis_index("c")
    return jnp.stack([c, 1 - c]).astype(jnp.int32)


def _staged_call(body, *, name, grid, in_specs, out_specs, out_shape, scratch_shapes):
    return pl.pallas_call(
        body, out_shape=out_shape, name=name,
        grid_spec=pltpu.PrefetchScalarGridSpec(num_scalar_prefetch=1, grid=grid, in_specs=in_specs,
                                               out_specs=out_specs, scratch_shapes=scratch_shapes),
        compiler_params=pltpu.CompilerParams(dimension_semantics=("arbitrary",) * len(grid),
                                             vmem_limit_bytes=V7X_VMEM_LIMIT, has_side_effects=True))


def gather_rider(shards, tiles):
    dma = pltpu.SemaphoreType.DMA
    n = len(shards)
    geo = [(a.shape[0] // 2, tm, (a.shape[0] // 2) // tm) for a, tm in zip(shards, tiles)]
    scratch = []
    for a, (h, tm, nk) in zip(shards, geo):
        scratch += [pltpu.VMEM((N_CHIP,) + a.shape, a.dtype), dma((3, nk)), dma((3, nk)), dma((3, nk)), dma((3, nk)),
                    dma((nk + 2,))]

    def copies(j, in_ref, scr):
        buf, send1, recv1, send2, recv2, local = scr[6 * j:6 * j + 6]
        h, tm, nk = geo[j]
        x, y, c, me, chips = _place()
        chip_of = [2 * chips[t][0] + chips[t][1] for t in range(3)]

        def rows(chip, core, k):
            return buf.at[chip, pl.ds(core * h + k * tm, tm)]

        def mine(k):
            if k == nk:
                return pltpu.make_async_copy(in_ref.at[pl.ds((1 - c) * h, h)], buf.at[me, pl.ds((1 - c) * h, h)],
                                             local.at[nk])
            return pltpu.make_async_copy(in_ref.at[pl.ds(c * h + k * tm, tm)], rows(me, c, k), local.at[k])

        def level1(t, k, incoming):
            place = rows(chip_of[t] if incoming else me, c, k)
            return pltpu.make_async_remote_copy(src_ref=place, dst_ref=place, send_sem=send1.at[t, k],
                                                recv_sem=recv1.at[t, k], device_id=(*chips[t], c), device_id_type=MESH)

        def level2(t, k, incoming):
            place = rows(chip_of[t], (1 - c) if incoming else c, k)
            return pltpu.make_async_remote_copy(src_ref=place, dst_ref=place, send_sem=send2.at[t, k],
                                                recv_sem=recv2.at[t, k], device_id=(x, y, 1 - c),
                                                device_id_type=MESH)

        return buf, local, nk, mine, level1, level2

    def start(ins, outs, scr):
        for j in range(n):
            _, _, nk, mine, _, _ = copies(j, ins[j], scr)
            for k in range(nk + 1):
                mine(k).start()
        for j in range(n):
            _, _, nk, mine, level1, _ = copies(j, ins[j], scr)
            for k in range(nk):
                mine(k).wait()
                for t in range(3):
                    level1(t, k, False).start()

    def finish(ins, outs, scr):
        for j in range(n):
            _, _, nk, _, level1, level2 = copies(j, ins[j], scr)
            for k in range(nk):
                for t in range(3):
                    level1(t, k, True).wait_recv()
                    level2(t, k, False).start()
        for j in range(n):
            buf, local, nk, mine, level1, level2 = copies(j, ins[j], scr)
            for k in range(nk):
                for t in range(3):
                    level2(t, k, True).wait_recv()
            for k in range(nk):
                for t in range(3):
                    level1(t, k, False).wait_send()
                    level2(t, k, False).wait_send()
            mine(nk).wait()
            pltpu.make_async_copy(buf, outs[j], local.at[nk + 1]).start()
        for j in range(n):
            buf, local, nk, _, _, _ = copies(j, ins[j], scr)
            pltpu.make_async_copy(buf, outs[j], local.at[nk + 1]).wait()

    return Rider(list(shards), [_sds((N_CHIP,) + a.shape, a.dtype) for a in shards], scratch, start, finish)


def run_alone(rider, name):
    return _call(lambda: None, name=name, out_shape=[], in_specs=[], out_specs=[], grid=(1,), rider=rider)()[1]


def sibling_sum(g, tm, name):
    _, r, cdim = g.shape
    h = r // 2
    ni = h // tm
    dma = pltpu.SemaphoreType.DMA

    def body(cores_ref, keep_ref, give_ref, out_ref, slot, send, recv):
        par = (pl.program_id(0) * ni + pl.program_id(1)) % 2
        x, y, c, _, _ = _place()
        cp = pltpu.make_async_remote_copy(src_ref=give_ref, dst_ref=slot.at[par], send_sem=send.at[par],
                                          recv_sem=recv.at[par], device_id=(x, y, 1 - c), device_id_type=MESH)
        cp.start()
        cp.wait_recv()
        out_ref[...] = (keep_ref[...].astype(F32) + slot[par].astype(F32)).astype(out_ref.dtype)
        cp.wait_send()

    flat = g.reshape(N_CHIP * r, cdim)
    return _staged_call(
        body, name=name, grid=(N_CHIP, ni),
        in_specs=[pl.BlockSpec((tm, cdim), lambda j, i, cores: ((2 * j + cores[0]) * ni + i, 0)),
                  pl.BlockSpec((tm, cdim), lambda j, i, cores: ((2 * j + cores[1]) * ni + i, 0))],
        out_specs=pl.BlockSpec((None, tm, cdim), lambda j, i, cores: (j, i, 0)),
        out_shape=_sds((N_CHIP, h, cdim), g.dtype),
        scratch_shapes=[pltpu.VMEM((2, tm, cdim), g.dtype), dma((2,)), dma((2,))],
    )(_cores(), flat, flat)


def owner_sum_rider(sums, tiles):
    dma = pltpu.SemaphoreType.DMA
    n = len(sums)
    geo = [(a.shape[1], tm, a.shape[1] // tm) for a, tm in zip(sums, tiles)]
    scratch = []
    for a, (h, tm, nk) in zip(sums, geo):
        cdim = a.shape[2]
        scratch += [pltpu.VMEM(a.shape, a.dtype), pltpu.VMEM((3, h, cdim), a.dtype), pltpu.VMEM((2, h, cdim), F32),
                    dma((3, nk)), dma((3, nk)), dma((nk,)), dma((nk,)), dma((2,))]

    def copies(j, scr):
        part, got, res, send, recv, send2, recv2, local = scr[8 * j:8 * j + 8]
        h, tm, nk = geo[j]
        x, y, c, me, chips = _place()

        def to_owner(t, k):
            chip = 2 * chips[t][0] + chips[t][1]
            return pltpu.make_async_remote_copy(
                src_ref=part.at[chip, pl.ds(k * tm, tm)], dst_ref=got.at[t, pl.ds(k * tm, tm)],
                send_sem=send.at[t, k], recv_sem=recv.at[t, k], device_id=(*chips[t], c), device_id_type=MESH)

        def to_sibling(k):
            place = res.at[c, pl.ds(k * tm, tm)]
            return pltpu.make_async_remote_copy(src_ref=place, dst_ref=place, send_sem=send2.at[k],
                                                recv_sem=recv2.at[k], device_id=(x, y, 1 - c), device_id_type=MESH)

        return part, got, res, local, to_owner, to_sibling, (tm, nk, c, me)

    def start(ins, outs, scr):
        for j in range(n):
            part, _, _, local, _, _, _ = copies(j, scr)
            pltpu.make_async_copy(ins[j], part, local.at[0]).start()
        for j in range(n):
            part, _, _, local, to_owner, _, (tm, nk, c, me) = copies(j, scr)
            pltpu.make_async_copy(ins[j], part, local.at[0]).wait()
            for k in range(nk):
                for t in range(3):
                    to_owner(t, k).start()

    def finish(ins, outs, scr):
        for j in range(n):
            part, got, res, _, to_owner, to_sibling, (tm, nk, c, me) = copies(j, scr)
            for k in range(nk):
                rows = pl.ds(k * tm, tm)
                for t in range(3):
                    to_owner(t, k).wait_recv()
                acc = part[me, rows, :].astype(F32)
                for t in range(3):
                    acc = acc + got[t, rows, :].astype(F32)
                res[c, rows, :] = acc
                to_sibling(k).start()
        for j in range(n):
            _, _, res, local, to_owner, to_sibling, (tm, nk, c, me) = copies(j, scr)
            for k in range(nk):
                to_sibling(k).wait_recv()
            for k in range(nk):
                to_sibling(k).wait_send()
                for t in range(3):
                    to_owner(t, k).wait_send()
            pltpu.make_async_copy(res, outs[j], local.at[1]).start()
        for j in range(n):
            _, _, res, local, _, _, _ = copies(j, scr)
            pltpu.make_async_copy(res, outs[j], local.at[1]).wait()

    return Rider(list(sums), [_sds((2, a.shape[1], a.shape[2]), F32) for a in sums], scratch, start, finish)


def gather_conv_w(w):
    def body(in_ref, out_ref, send, recv):
        x, y, c, me, chips = _place()
        out_ref[me] = in_ref[...]
        copies = []
        for t in range(3):
            copies.append(pltpu.make_async_remote_copy(
                src_ref=out_ref.at[me], dst_ref=out_ref.at[me], send_sem=send.at[t], recv_sem=recv.at[t],
                device_id=(*chips[t], c), device_id_type=MESH))
        for cp in copies:
            cp.start()
        for cp in copies:
            cp.wait_recv()
        for cp in copies:
            cp.wait_send()

    dma = pltpu.SemaphoreType.DMA
    vmem = pl.BlockSpec(memory_space=pltpu.VMEM)
    return pl.pallas_call(
        body, out_shape=_sds((N_CHIP,) + w.shape, w.dtype), in_specs=[vmem], out_specs=vmem, name="gather_conv_w",
        scratch_shapes=[dma((3,)), dma((3,))],
        compiler_params=pltpu.CompilerParams(has_side_effects=True))(w)


N_DEV = 8
SMALL_ROWS = 32
SMALL_LANES = 1024


def all_reduce_small(arrays):
    n_arr = len(arrays)
    places = []
    for k, a in enumerate(arrays):
        for ri in range(a.shape[0]):
            for c0 in range(0, a.shape[1], SMALL_LANES):
                places.append((k, ri, c0, min(SMALL_LANES, a.shape[1] - c0), len(places)))
    assert len(places) <= SMALL_ROWS

    def body(*refs):
        ins, outs = refs[:n_arr], refs[n_arr:2 * n_arr]
        buf, send, recv = refs[2 * n_arr:]
        x, y, c, _, _ = _place()
        me = 4 * x + 2 * y + c
        buf[me] = jnp.zeros((SMALL_ROWS, SMALL_LANES), F32)
        for k, ri, c0, width, row in places:
            buf[me, row:row + 1, 0:width] = ins[k][ri:ri + 1, c0:c0 + width]
        copies = []
        for r in range(1, N_DEV):
            px = (1 - x) if r & 4 else x
            py = (1 - y) if r & 2 else y
            pc = (1 - c) if r & 1 else c
            copies.append(pltpu.make_async_remote_copy(
                src_ref=buf.at[me], dst_ref=buf.at[me], send_sem=send.at[r - 1], recv_sem=recv.at[r - 1],
                device_id=(px, py, pc), device_id_type=MESH))
        for cp in copies:
            cp.start()
        for cp in copies:
            cp.wait_recv()
        for cp in copies:
            cp.wait_send()
        acc = buf[0]
        for j in range(1, N_DEV):
            acc = acc + buf[j]
        for k, ri, c0, width, row in places:
            outs[k][ri:ri + 1, c0:c0 + width] = acc[row:row + 1, 0:width]

    dma = pltpu.SemaphoreType.DMA
    vmem = pl.BlockSpec(memory_space=pltpu.VMEM)
    return pl.pallas_call(
        body, out_shape=[_sds(a.shape, F32) for a in arrays], in_specs=[vmem] * n_arr, out_specs=[vmem] * n_arr,
        name="all_reduce_small",
        scratch_shapes=[pltpu.VMEM((N_DEV, SMALL_ROWS, SMALL_LANES), F32), dma((N_DEV - 1,)), dma((N_DEV - 1,))],
        compiler_params=pltpu.CompilerParams(has_side_effects=True))(*arrays)


def _row_tile(rows, limit, multiple):
    return max(t for t in range(multiple, min(rows, limit) + 1, multiple) if rows % t == 0)


def add_pair(a, b, name):
    _, h, c = a.shape

    def body(a_ref, b_ref, o_ref):
        o_ref[...] = (a_ref[...].astype(F32) + b_ref[...].astype(F32)).astype(o_ref.dtype)

    blk = pl.BlockSpec((None, h, c), lambda j: (j, 0, 0))
    return _call(body, name=name, out_shape=_sds(a.shape, a.dtype), grid=(N_CHIP,), in_specs=[blk, blk],
                 out_specs=blk)(a, b)


def sum_slots(buf, name):
    _, h, c = buf.shape
    tm = _row_tile(h, 256, 16)

    def body(b_ref, o_ref):
        acc = b_ref[3].astype(F32)
        for t in range(3):
            acc = acc + b_ref[t].astype(F32)
        o_ref[...] = acc

    return _call(body, name=name, out_shape=_sds((h, c), F32), grid=(h // tm,),
                 in_specs=[pl.BlockSpec((N_CHIP, tm, c), lambda i: (0, i, 0))],
                 out_specs=pl.BlockSpec((tm, c), lambda i: (i, 0)))(buf)


def _adamw_math(w, g, m, v):
    c1 = 1.0 - ADAM_B1 ** ADAM_STEP
    c2 = 1.0 - ADAM_B2 ** ADAM_STEP
    m2 = ADAM_B1 * m + (1.0 - ADAM_B1) * g
    v2 = ADAM_B2 * v + (1.0 - ADAM_B2) * (g * g)
    return -ADAM_LR * ((m2 / c1) / (jnp.sqrt(v2 / c2) + ADAM_EPS) + ADAM_WD * w), m2, v2


def adamw(w, g, row_off, m, v, name):
    _, r, c = w.shape
    tm = r if r < 8 else _row_tile(math.gcd(r, row_off) if row_off else r, 128, 8)

    def body(w_ref, g_ref, m_ref, v_ref, go_ref, d_ref, m2_ref, v2_ref):
        gv = g_ref[...]
        go_ref[...] = gv
        d_ref[...], m2_ref[...], v2_ref[...] = _adamw_math(w_ref[...], gv, m_ref[...], v_ref[...])

    blk = pl.BlockSpec((None, tm, c), lambda i: (0, i, 0))
    shp = _sds((1, r, c), F32)
    return _call(body, name=name, out_shape=[shp] * 4, grid=(r // tm,),
                 in_specs=[blk, pl.BlockSpec((tm, c), lambda i: (row_off // tm + i, 0)), blk, blk],
                 out_specs=[blk] * 4)(w, g, m, v)


def adamw_small(ws, gs, ms, vs):
    n = len(ws)

    def body(*refs):
        ins, outs = refs[:4 * n], refs[4 * n:]
        for k in range(n):
            w_ref, g_ref, m_ref, v_ref = (ins[j * n + k] for j in range(4))
            outs[k][...], outs[n + k][...], outs[2 * n + k][...] = _adamw_math(w_ref[...], g_ref[...], m_ref[...],
                                                                               v_ref[...])

    vmem = pl.BlockSpec(memory_space=pltpu.VMEM)
    shapes = [_sds(w.shape, F32) for w in ws] * 3
    res = pl.pallas_call(body, out_shape=shapes, in_specs=[vmem] * (4 * n), out_specs=[vmem] * (3 * n),
                         name="adamw_small")(*ws, *gs, *ms, *vs)
    return res[:n], res[n:2 * n], res[2 * n:]


BIG = ("ffn1_w_gate", "ffn1_w_up", "ffn1_w_down", "w_in", "w_attn_branch", "w_ssd_branch", "w_out",
       "ffn2_w_gate", "ffn2_w_up", "ffn2_w_down")
SMALL = ("ffn1_norm", "mix_norm", "q_norm", "k_norm", "conv_b", "dt_bias", "a_log", "d_skip", "ssd_norm", "ffn2_norm")
WEIGHTS = ("ffn1_norm", "ffn1_w_gate", "ffn1_w_up", "ffn1_w_down", "mix_norm", "w_in", "q_norm", "k_norm", "conv_w",
           "conv_b", "dt_bias", "a_log", "d_skip", "ssd_norm", "w_attn_branch", "w_ssd_branch", "w_out", "ffn2_norm",
           "ffn2_w_gate", "ffn2_w_up", "ffn2_w_down")
CONV_SHARD = SSD_CONV_DIM // N_CHIP
CLASSES = {
    "ffn1_in": (("ffn1_w_gate", 1024), ("ffn1_w_up", 1024)),
    "ffn1_out": (("ffn1_w_down", 704),),
    "mix_in": (("w_in", 1024),),
    "mix_attn": (("w_attn_branch", 512),),
    "late_out": (("ffn2_w_down", 704), ("w_ssd_branch", 512), ("w_out", 256)),
    "ffn2_in": (("ffn2_w_gate", 1024), ("ffn2_w_up", 1024)),
}
CLASS_TILE = {"ffn1_in": 256, "ffn1_out": 176, "mix_in": 128, "mix_attn": 256, "late_out": 368, "ffn2_in": 256}


def _pack_small(vals, conv_part, loss_part=None):
    flat = [vals[k].reshape(-1) for k in SMALL]
    flat.append(jnp.zeros((SSD_CONV * SSD_CONV_DIM,), F32) if conv_part is None else conv_part.reshape(-1))
    flat.append(jnp.zeros((1,), F32) if loss_part is None else loss_part.reshape(1))
    flat = jnp.concatenate(flat)
    return jnp.pad(flat, (0, SMALL_ROWS * D_MODEL - flat.shape[0])).reshape(SMALL_ROWS, D_MODEL)


def _unpack_small(pack, like):
    flat = pack.reshape(-1)
    out, off = {}, 0
    for k in SMALL:
        n = like[k].size
        out[k] = flat[off:off + n].reshape(like[k].shape)
        off += n
    conv = flat[off:off + SSD_CONV * SSD_CONV_DIM].reshape(SSD_CONV, SSD_CONV_DIM)
    return out, conv, flat[off + SSD_CONV * SSD_CONV_DIM]


def _chip_major_cols(a):
    r = a.shape[0]
    return a.reshape(r, N_CHIP, -1).transpose(1, 0, 2)


def _from_chip_major_cols(a):
    return a.transpose(1, 0, 2).reshape(a.shape[1], -1)


def kernel(x, ffn1_norm, ffn1_w_gate, ffn1_w_up, ffn1_w_down, mix_norm, w_in, q_norm, k_norm, conv_w, conv_b, dt_bias, a_log, d_skip, ssd_norm, w_attn_branch, w_ssd_branch, w_out, ffn2_norm, ffn2_w_gate, ffn2_w_up, ffn2_w_down, loss_target, m_ffn1_norm, m_ffn1_w_gate, m_ffn1_w_up, m_ffn1_w_down, m_mix_norm, m_w_in, m_q_norm, m_k_norm, m_conv_w, m_conv_b, m_dt_bias, m_a_log, m_d_skip, m_ssd_norm, m_w_attn_branch, m_w_ssd_branch, m_w_out, m_ffn2_norm, m_ffn2_w_gate, m_ffn2_w_up, m_ffn2_w_down, v_ffn1_norm, v_ffn1_w_gate, v_ffn1_w_up, v_ffn1_w_down, v_mix_norm, v_w_in, v_q_norm, v_k_norm, v_conv_w, v_conv_b, v_dt_bias, v_a_log, v_d_skip, v_ssd_norm, v_w_attn_branch, v_w_ssd_branch, v_w_out, v_ffn2_norm, v_ffn2_w_gate, v_ffn2_w_up, v_ffn2_w_down):
    env = dict(locals())
    wts = {k: env[k] for k in WEIGHTS}
    moms = {k: env["m_" + k] for k in WEIGHTS}
    vars_ = {k: env["v_" + k] for k in WEIGHTS}
    x0 = x[0]
    target = loss_target[0]

    def gather(classes):
        shards = [jnp.concatenate([wts[k][0] for k, _ in CLASSES[c]], axis=0).astype(BF16) for c in classes]
        return gather_rider(shards, [CLASS_TILE[c] for c in classes])

    def reducer(classes, parts):
        sums = [sibling_sum(p, CLASS_TILE[c], f"sibling_sum_{c}") for c, p in zip(classes, parts)]
        return owner_sum_rider(sums, [CLASS_TILE[c] for c in classes])

    (w_ffn1_in,) = run_alone(gather(["ffn1_in"]), "gather_ffn1_in")
    x1, saved1, (w_ffn1_out, w_mix_in, w_mix_attn) = ffn_forward(
        x0, ffn1_norm, w_ffn1_in, lambda rode: rode[0], "ffn1", rider=gather(["ffn1_out", "mix_in", "mix_attn"]))
    dt0, dt1 = IN_DT0 - 3 * IN_SHARD, IN_DT1 - 3 * IN_SHARD
    mixer_w = dict(
        mix_norm=mix_norm,
        w_in_main=jnp.concatenate([w_mix_in[0], w_mix_in[1], w_mix_in[2], w_mix_in[3][:, :dt0], w_mix_in[3][:, dt1:]],
                                  axis=1),
        w_in_dt=jnp.pad(w_mix_in[3][:, dt0:dt1], ((0, 0), (0, DT_PAD - SSD_HEADS))),
        q_gain=jnp.tile(q_norm, (1, 2)), k_gain=jnp.tile(k_norm, (1, 2)),
        conv_w=_from_chip_major_cols(gather_conv_w(conv_w[0])), conv_b=conv_b, dt_bias=dt_bias, a_log=a_log,
        d_skip=d_skip, ssd_norm=ssd_norm, w_attn_branch=_from_chip_major_cols(w_mix_attn))

    def later_weights(rode):
        late = rode[0]
        return dict(w_ssd_branch=late[:, 704:1216].reshape(SSD_INNER, D_MODEL),
                    w_out=late[:, 1216:1472].reshape(D_MODEL, D_MODEL))

    x2, saved_mix, (w_late_out, w_ffn2_in) = mixer_forward(x1, mixer_w, gather(["late_out", "ffn2_in"]), later_weights)
    x3, saved2, _ = ffn_forward(x2, ffn2_norm, w_ffn2_in, lambda rode: w_late_out, "ffn2")
    dx3, sq = loss_grad(x3, target, "loss")

    grads = {}
    dx2, grads["ffn2_norm"], d_ffn2_in, d_ffn2_down = ffn_backward(dx3, x2, ffn2_norm, w_ffn2_in, w_late_out, saved2,
                                                                   "ffn2")

    def ride_early(g):
        late = jnp.concatenate([d_ffn2_down, g["w_ssd_branch"].reshape(N_CHIP, -1, D_MODEL),
                                g["w_out"].reshape(N_CHIP, -1, D_MODEL)], axis=1)
        return reducer(["ffn2_in", "late_out"], [d_ffn2_in, late])

    def ride_late(g):
        main = g["w_in_main"]
        last = jnp.concatenate([main[:, 3 * IN_SHARD:IN_DT0], g["w_in_dt"][:, :SSD_HEADS], main[:, IN_DT0:]], axis=1)
        g_in = jnp.stack([main[:, j * IN_SHARD:(j + 1) * IN_SHARD] for j in range(3)] + [last])
        return reducer(["mix_in", "mix_attn"], [g_in, _chip_major_cols(g["w_attn_branch"])])

    dx1, gmix = mixer_backward(dx2, x1, saved_mix, ride_early, ride_late)
    dx0, grads["ffn1_norm"], rode_in, rode_out = ffn_backward(
        dx1, x0, ffn1_norm, w_ffn1_in, w_ffn1_out, saved1, "ffn1",
        ride_down=lambda d: reducer(["ffn1_out"], [d]), ride_in=lambda d: reducer(["ffn1_in"], [d]))
    for k in ("mix_norm", "q_norm", "k_norm", "conv_b", "dt_bias", "a_log", "d_skip", "ssd_norm"):
        grads[k] = gmix[k]
    reduced = dict(zip(("ffn2_in", "late_out", "mix_in", "mix_attn", "ffn1_in", "ffn1_out"),
                       (*gmix["rode_early"], *gmix["rode_late"], rode_in[0], rode_out[0])))
    reduced = {c: r.reshape(-1, r.shape[2]) for c, r in reduced.items()}
    summed = all_reduce_small([grads[k] for k in SMALL]
                              + [gmix["conv_w"], (0.5 * jnp.sum(sq) / D_MODEL).reshape(1, 1)])
    g_small = dict(zip(SMALL, summed))
    loss = summed[-1].reshape(())
    chip = 2 * lax.axis_index("x") + lax.axis_index("y")
    g_conv = lax.dynamic_slice_in_dim(summed[-2], chip * CONV_SHARD, CONV_SHARD, axis=1)

    g_final, delta, new_m, new_v = dict(g_small), {}, {}, {}

    def update(k, g_arr, row_off):
        w, m, v = wts[k], moms[k], vars_[k]
        rows, cols = w.shape[1:]
        if cols % 128:
            res = adamw(jnp.swapaxes(w, 1, 2), g_arr[row_off:row_off + rows].T, 0, jnp.swapaxes(m, 1, 2),
                        jnp.swapaxes(v, 1, 2), f"adamw_{k}")
            res = [jnp.swapaxes(r, 1, 2) for r in res]
        else:
            res = adamw(w, g_arr, row_off, m, v, f"adamw_{k}")
        g_final[k], delta[k], new_m[k], new_v[k] = res

    for cls, members in CLASSES.items():
        off = 0
        for k, rows in members:
            update(k, reduced[cls], off)
            off += rows
    update("conv_w", g_conv, 0)
    small = adamw_small(*([d[k] for k in SMALL] for d in (wts, g_small, moms, vars_)))
    for res, vals in zip((delta, new_m, new_v), small):
        res.update(zip(SMALL, vals))

    return (loss, dx0[None], *[g_final[k] for k in WEIGHTS], *[delta[k] for k in WEIGHTS],
            *[new_m[k] for k in WEIGHTS], *[new_v[k] for k in WEIGHTS])
```

```python
import collections
import functools
import math

import jax
import jax.numpy as jnp
from jax import lax
from jax.experimental import pallas as pl
from jax.experimental.pallas import tpu as pltpu

F32 = jnp.float32
BF16 = jnp.bfloat16
MESH = pl.DeviceIdType.MESH

EPS = 1e-6
D_MODEL = 1024
D_FF = 2816
N_CHIP = 4
FF_SHARD = D_FF // N_CHIP
HD = 64
BLK = 128
ATTN_DILATIONS = (1, 4, 16)
HEADS_PER_PATTERN = 8
N_ATTN_HEADS = 24
ALIBI_MAX_EXP = 8.0
ATTN_QKV = 1536
GROUP_W = 512
SSD_INNER = 2048
SSD_HEADS = 32
SSD_GROUPS = 4
SSD_CONV = 4
SSD_CONV_DIM = 3072
IN_COLS = 11808
IN_DT0, IN_DT1 = 9728, 9760
IN_SHARD = IN_COLS // 4
COL_K, COL_V, COL_Z, COL_XBC, COL_GA, COL_GS, P_COLS = 1536, 3072, 4608, 6656, 9728, 10752, 11776
DT_PAD = 128

ADAM_LR, ADAM_B1, ADAM_B2, ADAM_EPS, ADAM_WD, ADAM_STEP = 0.001, 0.9, 0.999, 1e-08, 0.01, 10

V7X_VMEM_LIMIT = 56 * 1024 * 1024
NEG = -1e30


Rider = collections.namedtuple("Rider", "arrays out_shape scratch start finish")
Rider.__doc__ = """An exchange between devices that rides in a compute kernel: its copies are started in the host's
first grid step and waited for in its last, so they travel while the host computes.  arrays / out_shape: extra HBM
operands and results; scratch: extra scratch; start, finish: f(in_refs, out_refs, scratch_refs)."""


def _call(body, *, name, out_shape, in_specs, out_specs, grid=(), scratch_shapes=(), aliases=None, rider=None):
    params = dict(dimension_semantics=("arbitrary",) * len(grid), vmem_limit_bytes=V7X_VMEM_LIMIT)
    if rider is None:
        return pl.pallas_call(
            body, out_shape=out_shape, grid=grid, in_specs=in_specs, out_specs=out_specs,
            scratch_shapes=scratch_shapes, input_output_aliases=aliases or {}, name=name,
            compiler_params=pltpu.CompilerParams(**params))
    single = not isinstance(out_shape, (list, tuple))
    main_out = [out_shape] if single else list(out_shape)
    main_specs = [out_specs] if single else list(out_specs)
    n_in, n_out, n_scr = len(in_specs), len(main_out), len(scratch_shapes)
    r_in, r_out = len(rider.arrays), len(rider.out_shape)

    def wrapped(*refs):
        ins, refs = refs[:n_in], refs[n_in:]
        r_ins, refs = refs[:r_in], refs[r_in:]
        outs, refs = refs[:n_out], refs[n_out:]
        r_outs, refs = refs[:r_out], refs[r_out:]
        scr, r_scr = refs[:n_scr], refs[n_scr:]
        first = last = None
        for axis, size in enumerate(grid):
            at_start, at_end = pl.program_id(axis) == 0, pl.program_id(axis) == size - 1
            first = at_start if first is None else jnp.logical_and(first, at_start)
            last = at_end if last is None else jnp.logical_and(last, at_end)

        @pl.when(first)
        def _():
            rider.start(r_ins, r_outs, r_scr)

        body(*ins, *outs, *scr)

        @pl.when(last)
        def _():
            rider.finish(r_ins, r_outs, r_scr)

    hbm = pl.BlockSpec(memory_space=pl.ANY)
    call = pl.pallas_call(
        wrapped, out_shape=main_out + list(rider.out_shape), grid=grid, in_specs=list(in_specs) + [hbm] * r_in,
        out_specs=main_specs + [hbm] * r_out, scratch_shapes=list(scratch_shapes) + list(rider.scratch), name=name,
        compiler_params=pltpu.CompilerParams(has_side_effects=True, **params))

    def run(*args):
        res = call(*args, *rider.arrays)
        main = res[:n_out]
        return (main[0] if single else main), res[n_out:]

    return run


def _sds(shape, dtype):
    return jax.ShapeDtypeStruct(tuple(shape), dtype)


def _dot(a, b):
    return jnp.dot(a, b, preferred_element_type=F32)


def _dot_nt(a, b):
    return lax.dot_general(a, b, (((1,), (1,)), ((), ())), preferred_element_type=F32)


def _dot_tn(a, b):
    return lax.dot_general(a, b, (((0,), (0,)), ((), ())), preferred_element_type=F32)


def _dot_hi(a, b):
    return jnp.dot(a, b, preferred_element_type=F32, precision=lax.Precision.HIGHEST)


def _sigmoid(x):
    return 1.0 / (1.0 + jnp.exp(-x))


def _lane_first_half(shape):
    return lax.broadcasted_iota(jnp.int32, shape, len(shape) - 1) < HD


def _pair_sum(x, first):
    s_all = jnp.sum(x, axis=-1, keepdims=True)
    s_a = jnp.sum(jnp.where(first, x, 0.0), axis=-1, keepdims=True)
    return s_a, s_all - s_a


def _rowwise(name, fn, rows, consts, outs, accs=(), tm=512):
    n_rows = None
    in_arrays, in_specs = [], []
    for r in rows:
        if isinstance(r, tuple):
            arr, w, cb = r
            spec = pl.BlockSpec((tm, w), functools.partial(lambda i, cb: (i, cb), cb=cb))
        else:
            arr = r
            spec = pl.BlockSpec((tm, arr.shape[1]), lambda i: (i, 0))
        n_rows = arr.shape[0]
        in_arrays.append(arr)
        in_specs.append(spec)
    for c in consts:
        in_arrays.append(c)
        in_specs.append(pl.BlockSpec(c.shape, functools.partial(lambda i, n: (0,) * n, n=c.ndim)))
    out_shape = [_sds(s, d) for s, d in outs] + [_sds(s, d) for s, d in accs]
    out_specs = [pl.BlockSpec((tm, s[1]), lambda i: (i, 0)) for s, _ in outs]
    out_specs += [pl.BlockSpec(s, functools.partial(lambda i, n: (0,) * n, n=len(s))) for s, _ in accs]

    def body(*refs):
        fn(pl.program_id(0), *refs)

    res = _call(body, name=name, out_shape=out_shape, grid=(n_rows // tm,), in_specs=in_specs,
                out_specs=out_specs)(*in_arrays)
    return res


def rms_fwd(x, gain, name):
    def fn(i, x_ref, g_ref, h_ref):
        xv = x_ref[...]
        r = lax.rsqrt(jnp.mean(xv * xv, axis=-1, keepdims=True) + EPS)
        h_ref[...] = (xv * r * g_ref[...]).astype(h_ref.dtype)

    return _rowwise(name, fn, [x], [gain], [(x.shape, BF16)])[0]


def rms_bwd(dhs, x, gain, dx_in, name):
    n = len(dhs)

    def fn(i, *refs):
        dh_refs, (x_ref, dxin_ref, g_ref, dx_ref, dg_ref) = refs[:n], refs[n:]
        dh = dh_refs[0][...]
        for r in dh_refs[1:]:
            dh = dh + r[...]
        xv = x_ref[...]
        r = lax.rsqrt(jnp.mean(xv * xv, axis=-1, keepdims=True) + EPS)
        xn = xv * r
        dxn = dh * g_ref[...]
        dx_ref[...] = dxin_ref[...] + r * (dxn - xn * jnp.mean(dxn * xn, axis=-1, keepdims=True))

        @pl.when(i == 0)
        def _():
            dg_ref[...] = jnp.zeros_like(dg_ref)

        dg_ref[...] += jnp.sum(dh * xn, axis=0, keepdims=True)

    return _rowwise(name, fn, list(dhs) + [x, dx_in], [gain], [(x.shape, F32)], [((1, x.shape[1]), F32)])


def loss_grad(y, target, name):
    def fn(i, y_ref, t_ref, dy_ref, sq_ref):
        err = y_ref[...] - t_ref[...]
        dy_ref[...] = err * (1.0 / y_ref.shape[1])

        @pl.when(i == 0)
        def _():
            sq_ref[...] = jnp.zeros_like(sq_ref)

        sq_ref[...] += jnp.sum(err * err, axis=0, keepdims=True)

    return _rowwise(name, fn, [y, target], [], [(y.shape, F32)], [((1, y.shape[1]), F32)])


def matmul_nn(a, b, name, out_dtype, tm, tn, res=None, scale=1.0, rider=None):
    s, k = a.shape
    n = b.shape[1]

    def body(*refs):
        if res is None:
            a_ref, b_ref, o_ref = refs
            o_ref[...] = _dot(a_ref[...], b_ref[...]).astype(o_ref.dtype)
        else:
            a_ref, b_ref, r_ref, o_ref = refs
            o_ref[...] = (r_ref[...] + scale * _dot(a_ref[...], b_ref[...])).astype(o_ref.dtype)

    in_specs = [pl.BlockSpec((tm, k), lambda i, j: (i, 0)), pl.BlockSpec((k, tn), lambda i, j: (0, j))]
    args = [a, b]
    if res is not None:
        in_specs.append(pl.BlockSpec((tm, tn), lambda i, j: (i, j)))
        args.append(res)
    return _call(body, name=name, out_shape=_sds((s, n), out_dtype), grid=(s // tm, n // tn), in_specs=in_specs,
                 out_specs=pl.BlockSpec((tm, tn), lambda i, j: (i, j)), rider=rider)(*args)


def matmul_nt(a, b, name, out_dtype, tm, tn, tk, rider=None):
    s, k = a.shape
    n = b.shape[0]
    nk = k // tk

    def body(a_ref, b_ref, o_ref, acc_ref):
        kk = pl.program_id(2)

        @pl.when(kk == 0)
        def _():
            acc_ref[...] = jnp.zeros_like(acc_ref)

        acc_ref[...] += _dot_nt(a_ref[...].astype(BF16), b_ref[...])

        @pl.when(kk == nk - 1)
        def _():
            o_ref[...] = acc_ref[...].astype(o_ref.dtype)

    return _call(body, name=name, out_shape=_sds((s, n), out_dtype), grid=(s // tm, n // tn, nk),
                 in_specs=[pl.BlockSpec((tm, tk), lambda i, j, kk: (i, kk)),
                           pl.BlockSpec((tn, tk), lambda i, j, kk: (j, kk))],
                 out_specs=pl.BlockSpec((tm, tn), lambda i, j, kk: (i, j)),
                 scratch_shapes=[pltpu.VMEM((tm, tn), F32)], rider=rider)(a, b)


def matmul_tn(a, b, name, tn, ts, a_scale=None, b_scale=None, rider=None):
    s, m = a.shape
    n = b.shape[1]
    ns = s // ts

    def body(a_ref, b_ref, o_ref, acc_ref):
        ss = pl.program_id(1)

        @pl.when(ss == 0)
        def _():
            acc_ref[...] = jnp.zeros_like(acc_ref)

        av, bv = a_ref[...], b_ref[...]
        if a_scale is not None:
            av = av * a_scale
        if b_scale is not None:
            bv = bv * b_scale
        acc_ref[...] += _dot_tn(av.astype(BF16), bv.astype(BF16))

        @pl.when(ss == ns - 1)
        def _():
            o_ref[...] = acc_ref[...].astype(o_ref.dtype)

    return _call(body, name=name, out_shape=_sds((m, n), BF16), grid=(n // tn, ns),
                 in_specs=[pl.BlockSpec((ts, m), lambda j, ss: (ss, 0)), pl.BlockSpec((ts, tn), lambda j, ss: (ss, j))],
                 out_specs=pl.BlockSpec((m, tn), lambda j, ss: (0, j)),
                 scratch_shapes=[pltpu.VMEM((m, tn), F32)], rider=rider)(a, b)


def _piece_specs(pieces, tile, rows_tile, tile_axis_first):
    specs, ranges, t0 = [], [], 0
    for a in pieces:
        n = a.shape[1] // tile

        def index(*ids, t0=t0, n=n):
            t, r = (ids[0], ids[1]) if tile_axis_first else (ids[2], ids[0])
            on = jnp.logical_and(t >= t0, t < t0 + n)
            return jnp.where(on, r, 0), jnp.clip(t - t0, 0, n - 1)

        specs.append(pl.BlockSpec((rows_tile, tile), index))
        ranges.append((t0, n))
        t0 += n
    return specs, ranges


def matmul_tn_pieces(a, pieces, name, tn, ts, rider=None):
    s, m = a.shape
    ns = s // ts
    specs, ranges = _piece_specs(pieces, tn, ts, True)
    n_total = sum(n for _, n in ranges)

    def body(a_ref, *refs):
        b_refs, o_ref, acc_ref = refs[:len(pieces)], refs[-2], refs[-1]
        j, ss = pl.program_id(0), pl.program_id(1)

        @pl.when(ss == 0)
        def _():
            acc_ref[...] = jnp.zeros_like(acc_ref)

        for b_ref, (t0, n) in zip(b_refs, ranges):
            @pl.when(jnp.logical_and(j >= t0, j < t0 + n))
            def _(b_ref=b_ref):
                acc_ref[...] += _dot_tn(a_ref[...], b_ref[...])

        @pl.when(ss == ns - 1)
        def _():
            o_ref[...] = acc_ref[...].astype(o_ref.dtype)

    return _call(body, name=name, out_shape=_sds((m, n_total * tn), BF16), grid=(n_total, ns),
                 in_specs=[pl.BlockSpec((ts, m), lambda j, ss: (ss, 0))] + specs,
                 out_specs=pl.BlockSpec((m, tn), lambda j, ss: (0, j)),
                 scratch_shapes=[pltpu.VMEM((m, tn), F32)], rider=rider)(a, *pieces)


def matmul_nt_pieces(pieces, b, name, out_dtype, tm, tn, tk, rider=None):
    s = pieces[0].shape[0]
    n = b.shape[0]
    specs, ranges = _piece_specs(pieces, tk, tm, False)
    nk = sum(cnt for _, cnt in ranges)

    def body(*refs):
        a_refs, b_ref, o_ref, acc_ref = refs[:len(pieces)], refs[-3], refs[-2], refs[-1]
        kk = pl.program_id(2)

        @pl.when(kk == 0)
        def _():
            acc_ref[...] = jnp.zeros_like(acc_ref)

        for a_ref, (t0, cnt) in zip(a_refs, ranges):
            @pl.when(jnp.logical_and(kk >= t0, kk < t0 + cnt))
            def _(a_ref=a_ref):
                acc_ref[...] += _dot_nt(a_ref[...], b_ref[...])

        @pl.when(kk == nk - 1)
        def _():
            o_ref[...] = acc_ref[...].astype(o_ref.dtype)

    return _call(body, name=name, out_shape=_sds((s, n), out_dtype), grid=(s // tm, n // tn, nk),
                 in_specs=specs + [pl.BlockSpec((tn, tk), lambda i, j, kk: (j, kk))],
                 out_specs=pl.BlockSpec((tm, tn), lambda i, j, kk: (i, j)),
                 scratch_shapes=[pltpu.VMEM((tm, tn), F32)], rider=rider)(*pieces, b)


def ffn_up(h, w704, gate_blk, up_blk, name, tm=512, rider=None):
    s = h.shape[0]

    def body(h_ref, wg_ref, wu_ref, g_ref, u_ref, a_ref):
        hv = h_ref[...]
        g = _dot(hv, wg_ref[...])
        u = _dot(hv, wu_ref[...])
        g_ref[...] = g.astype(BF16)
        u_ref[...] = u.astype(BF16)
        a_ref[...] = (g * _sigmoid(g) * u).astype(BF16)

    ospec = pl.BlockSpec((None, tm, FF_SHARD), lambda j, i: (j, i, 0))
    shp = _sds((N_CHIP, s, FF_SHARD), BF16)
    return _call(body, name=name, out_shape=[shp, shp, shp], grid=(N_CHIP, s // tm),
                 in_specs=[pl.BlockSpec((tm, D_MODEL), lambda j, i: (i, 0)),
                           pl.BlockSpec((None, D_MODEL, FF_SHARD), lambda j, i: (j, gate_blk, 0)),
                           pl.BlockSpec((None, D_MODEL, FF_SHARD), lambda j, i: (j, up_blk, 0))],
                 out_specs=[ospec, ospec, ospec], rider=rider)(h, w704, w704)


def ffn_down(a, w1024, blk, x, name, tm=512):
    s = x.shape[0]

    def body(a_ref, wd_ref, x_ref, o_ref):
        acc = _dot(a_ref[0], wd_ref[0])
        for j in range(1, N_CHIP):
            acc += _dot(a_ref[j], wd_ref[j])
        o_ref[...] = x_ref[...] + 0.5 * acc

    return _call(body, name=name, out_shape=_sds((s, D_MODEL), F32), grid=(s // tm,),
                 in_specs=[pl.BlockSpec((N_CHIP, tm, FF_SHARD), lambda i: (0, i, 0)),
                           pl.BlockSpec((N_CHIP, FF_SHARD, D_MODEL), lambda i: (0, blk, 0)),
                           pl.BlockSpec((tm, D_MODEL), lambda i: (i, 0))],
                 out_specs=pl.BlockSpec((tm, D_MODEL), lambda i: (i, 0)))(a, w1024, x)


def ffn_bwd_hidden(dx, w1024, blk, g, u, name, tm=1024, rider=None):
    s = dx.shape[0]

    def body(dx_ref, wd_ref, g_ref, u_ref, dg_ref, du_ref):
        dy = (0.5 * dx_ref[...]).astype(BF16)
        da = _dot_nt(dy, wd_ref[...])
        gv = g_ref[...].astype(F32)
        uv = u_ref[...].astype(F32)
        sg = _sigmoid(gv)
        dg_ref[...] = (da * uv * (sg * (1.0 + gv * (1.0 - sg)))).astype(BF16)
        du_ref[...] = (da * gv * sg).astype(BF16)

    hspec = pl.BlockSpec((None, tm, FF_SHARD), lambda j, i: (j, i, 0))
    shp = _sds((N_CHIP, s, FF_SHARD), BF16)
    return _call(body, name=name, out_shape=[shp, shp], grid=(N_CHIP, s // tm),
                 in_specs=[pl.BlockSpec((tm, D_MODEL), lambda j, i: (i, 0)),
                           pl.BlockSpec((None, FF_SHARD, D_MODEL), lambda j, i: (j, blk, 0)), hspec, hspec],
                 out_specs=[hspec, hspec], rider=rider)(dx, w1024, g, u)


def ffn_bwd_input(dg, du, w704, gate_blk, up_blk, name, tm=512, rider=None):
    s = dg.shape[1]

    def body(dg_ref, du_ref, wg_ref, wu_ref, o_ref):
        acc = _dot_nt(dg_ref[0], wg_ref[0]) + _dot_nt(du_ref[0], wu_ref[0])
        for j in range(1, N_CHIP):
            acc += _dot_nt(dg_ref[j], wg_ref[j]) + _dot_nt(du_ref[j], wu_ref[j])
        o_ref[...] = acc

    hspec = pl.BlockSpec((N_CHIP, tm, FF_SHARD), lambda i: (0, i, 0))
    return _call(body, name=name, out_shape=_sds((s, D_MODEL), F32), grid=(s // tm,),
                 in_specs=[hspec, hspec,
                           pl.BlockSpec((N_CHIP, D_MODEL, FF_SHARD), lambda i: (0, gate_blk, 0), pl.Buffered(1)),
                           pl.BlockSpec((N_CHIP, D_MODEL, FF_SHARD), lambda i: (0, up_blk, 0), pl.Buffered(1))],
                 out_specs=pl.BlockSpec((tm, D_MODEL), lambda i: (i, 0)), rider=rider)(dg, du, w704, w704)


def ffn_wgrad_in(h, dgu, name, ts=1024):
    s = h.shape[0]
    ns = s // ts

    def body(h_ref, d_ref, o_ref, acc_ref):
        ss = pl.program_id(1)

        @pl.when(ss == 0)
        def _():
            acc_ref[...] = jnp.zeros_like(acc_ref)

        acc_ref[...] += _dot_tn(h_ref[...], d_ref[...])

        @pl.when(ss == ns - 1)
        def _():
            o_ref[...] = acc_ref[...].astype(BF16)

    return _call(body, name=name, out_shape=_sds((N_CHIP, D_MODEL, FF_SHARD), BF16), grid=(N_CHIP, ns),
                 in_specs=[pl.BlockSpec((ts, D_MODEL), lambda j, ss: (ss, 0)),
                           pl.BlockSpec((None, ts, FF_SHARD), lambda j, ss: (j, ss, 0))],
                 out_specs=pl.BlockSpec((None, D_MODEL, FF_SHARD), lambda j, ss: (j, 0, 0)),
                 scratch_shapes=[pltpu.VMEM((D_MODEL, FF_SHARD), F32)])(h, dgu)


def ffn_wgrad_down(a, dx, name, ts=1024):
    s = dx.shape[0]
    ns = s // ts

    def body(a_ref, dx_ref, o_ref, acc_ref):
        ss = pl.program_id(1)

        @pl.when(ss == 0)
        def _():
            acc_ref[...] = jnp.zeros_like(acc_ref)

        acc_ref[...] += _dot_tn(a_ref[...], (0.5 * dx_ref[...]).astype(BF16))

        @pl.when(ss == ns - 1)
        def _():
            o_ref[...] = acc_ref[...].astype(BF16)

    return _call(body, name=name, out_shape=_sds((N_CHIP, FF_SHARD, D_MODEL), BF16), grid=(N_CHIP, ns),
                 in_specs=[pl.BlockSpec((None, ts, FF_SHARD), lambda j, ss: (j, ss, 0)),
                           pl.BlockSpec((ts, D_MODEL), lambda j, ss: (ss, 0))],
                 out_specs=pl.BlockSpec((None, FF_SHARD, D_MODEL), lambda j, ss: (j, 0, 0)),
                 scratch_shapes=[pltpu.VMEM((FF_SHARD, D_MODEL), F32)])(a, dx)


def ffn_forward(x, gain, w704, get_w1024, tag, rider=None):
    h = rms_fwd(x, gain, f"{tag}_rms")
    res = ffn_up(h, w704, 0, 1, f"{tag}_up", rider=rider)
    (g, u, a), rode = res if rider is not None else (res, None)
    y = ffn_down(a, get_w1024(rode), 0, x, f"{tag}_down")
    return y, (h, g, u, a), rode


def ffn_backward(dy, x, gain, w704, w1024, saved, tag, ride_down=None, ride_in=None):
    h, g, u, a = saved
    d_wd = ffn_wgrad_down(a, dy, f"{tag}_dwd")
    if ride_down is not None:
        (dg, du), d_wd = ffn_bwd_hidden(dy, w1024, 0, g, u, f"{tag}_dhid", rider=ride_down(d_wd))
    else:
        dg, du = ffn_bwd_hidden(dy, w1024, 0, g, u, f"{tag}_dhid")
    d_win = jnp.concatenate([ffn_wgrad_in(h, dg, f"{tag}_dwg"), ffn_wgrad_in(h, du, f"{tag}_dwu")], axis=1)
    if ride_in is not None:
        dh, d_win = ffn_bwd_input(dg, du, w704, 0, 1, f"{tag}_dh", rider=ride_in(d_win))
    else:
        dh = ffn_bwd_input(dg, du, w704, 0, 1, f"{tag}_dh")
    dx, d_gain = rms_bwd([dh], x, gain, dy, f"{tag}_drms")
    return dx, d_gain, d_win, d_wd


def _alibi_slope(head):
    return float(2.0 ** (-ALIBI_MAX_EXP * (head + 1) / N_ATTN_HEADS))


def _head_norm(t, gain_pair, first):
    sa, sb = _pair_sum(t * t, first)
    r = jnp.where(first, lax.rsqrt(sa * (1.0 / HD) + EPS), lax.rsqrt(sb * (1.0 / HD) + EPS))
    return t * r * gain_pair, r


def qk_norm_fwd(p, q_gain, k_gain, name):
    s = p.shape[0]

    def fn(i, q_ref, k_ref, qg_ref, kg_ref, qn_ref, kn_ref):
        first = _lane_first_half((q_ref.shape[0], 2 * HD))
        for src, g_ref, dst in ((q_ref, qg_ref, qn_ref), (k_ref, kg_ref, kn_ref)):
            for pr in range(ATTN_QKV // (2 * HD)):
                cols = slice(pr * 2 * HD, (pr + 1) * 2 * HD)
                y, _ = _head_norm(src[:, cols].astype(F32), g_ref[...], first)
                dst[:, cols] = y.astype(BF16)

    return _rowwise(name, fn, [(p, ATTN_QKV, 0), (p, ATTN_QKV, 1)], [q_gain, k_gain],
                    [((s, ATTN_QKV), BF16), ((s, ATTN_QKV), BF16)])


def qk_norm_bwd(p, dqs, dks, q_gain, k_gain, name):
    s = p.shape[0]
    pairs_per_pattern = GROUP_W // (2 * HD)

    def fn(i, q_ref, k_ref, dq0, dq1, dq2, dk0, dk1, dk2, qg_ref, kg_ref, dqk_ref, dqg_ref, dkg_ref):
        first = _lane_first_half((q_ref.shape[0], 2 * HD))

        @pl.when(i == 0)
        def _():
            dqg_ref[...] = jnp.zeros_like(dqg_ref)
            dkg_ref[...] = jnp.zeros_like(dkg_ref)

        for src, d_refs, g_ref, dst, dg_ref in (
                (q_ref, (dq0, dq1, dq2), qg_ref, dqk_ref.at[:, 0:ATTN_QKV], dqg_ref),
                (k_ref, (dk0, dk1, dk2), kg_ref, dqk_ref.at[:, ATTN_QKV:2 * ATTN_QKV], dkg_ref)):
            for pr in range(ATTN_QKV // (2 * HD)):
                cols = slice(pr * 2 * HD, (pr + 1) * 2 * HD)
                t = src[:, cols].astype(F32)
                sa, sb = _pair_sum(t * t, first)
                r = jnp.where(first, lax.rsqrt(sa * (1.0 / HD) + EPS), lax.rsqrt(sb * (1.0 / HD) + EPS))
                xn = t * r
                within = (pr % pairs_per_pattern) * 2 * HD
                dy = d_refs[pr // pairs_per_pattern][:, within:within + 2 * HD]
                dg_ref[:, cols] += jnp.sum(dy * xn, axis=0, keepdims=True)
                dxn = dy * g_ref[...]
                ma, mb = _pair_sum(dxn * xn, first)
                mean = jnp.where(first, ma, mb) * (1.0 / HD)
                dst[:, cols] = (r * (dxn - xn * mean)).astype(BF16)

    return _rowwise(name, fn, [(p, ATTN_QKV, 0), (p, ATTN_QKV, 1)] + list(dqs) + list(dks), [q_gain, k_gain],
                    [((s, 2 * ATTN_QKV), BF16)], [((1, ATTN_QKV), F32), ((1, ATTN_QKV), F32)])


def _to_streams(a, d):
    if d == 1:
        return a
    s, c = a.shape
    return a.reshape(s // d, d, c).transpose(1, 0, 2).reshape(s, c)


def _from_streams(a, d):
    if d == 1:
        return a
    s, c = a.shape
    return a.reshape(d, s // d, c).transpose(1, 0, 2).reshape(s, c)


def _attn_masks():
    row = lax.broadcasted_iota(jnp.int32, (BLK, BLK), 0)
    col = lax.broadcasted_iota(jnp.int32, (BLK, BLK), 1)
    rel_diag = row - col
    rel_prev = rel_diag + BLK
    return rel_diag, rel_prev


def attn_fwd(q, k, v, pattern, name, tq=512):
    s = q.shape[0]
    d = ATTN_DILATIONS[pattern]
    blocks_per_stream = (s // d) // BLK
    nsb = tq // BLK

    def body(q_ref, k_ref, v_ref, kp_ref, vp_ref, o_ref, l_ref):
        i = pl.program_id(0)
        rel_diag, rel_prev = _attn_masks()
        first = _lane_first_half((BLK, 2 * HD))
        rd_f = (rel_diag * d).astype(F32)
        rp_f = (rel_prev * d).astype(F32)
        for sb in range(nsb):
            rows = slice(sb * BLK, (sb + 1) * BLK)
            has_prev = ((i * nsb + sb) % blocks_per_stream != 0).astype(jnp.int32)
            m_diag = rel_diag >= 0
            m_prev = (rel_prev + (1 - has_prev) * (4 * BLK)) <= BLK
            for pr in range(GROUP_W // (2 * HD)):
                cols = slice(pr * 2 * HD, (pr + 1) * 2 * HD)
                qp = q_ref[rows, cols]
                kc, vc = k_ref[rows, cols], v_ref[rows, cols]
                if sb == 0:
                    kp, vp = kp_ref[:, cols], vp_ref[:, cols]
                else:
                    prows = slice((sb - 1) * BLK, sb * BLK)
                    kp, vp = k_ref[prows, cols], v_ref[prows, cols]
                outs, lses = [], []
                for e in range(2):
                    slope = _alibi_slope(pattern * HEADS_PER_PATTERN + 2 * pr + e)
                    qm = jnp.where(first if e == 0 else jnp.logical_not(first), qp, jnp.zeros_like(qp))
                    s1 = jnp.where(m_diag, _dot_nt(qm, kc) * 0.125 - slope * rd_f, NEG)
                    s0 = jnp.where(m_prev, _dot_nt(qm, kp) * 0.125 - slope * rp_f, NEG)
                    m = jnp.maximum(jnp.max(s1, axis=-1, keepdims=True), jnp.max(s0, axis=-1, keepdims=True))
                    p1 = jnp.exp(s1 - m)
                    p0 = jnp.exp(s0 - m)
                    l = jnp.sum(p1, axis=-1, keepdims=True) + jnp.sum(p0, axis=-1, keepdims=True)
                    inv = 1.0 / l
                    outs.append(_dot((p1 * inv).astype(BF16), vc) + _dot((p0 * inv).astype(BF16), vp))
                    lses.append(m + jnp.log(l))
                o_ref[rows, cols] = jnp.where(first, outs[0], outs[1])
                l_ref[rows, cols] = jnp.where(first, lses[0], lses[1])

    cur = pl.BlockSpec((tq, GROUP_W), lambda i: (i, 0))
    prev = pl.BlockSpec((BLK, GROUP_W), lambda i: (jnp.maximum(i * nsb - 1, 0), 0))
    return _call(body, name=name, out_shape=[_sds((s, GROUP_W), F32), _sds((s, GROUP_W), F32)], grid=(s // tq,),
                 in_specs=[cur, cur, cur, prev, prev], out_specs=[cur, cur])(q, k, v, k, v)


def attn_merge_fwd(os_, lses, name):
    s = os_[0].shape[0]

    def fn(i, o0, o1, o2, l0, l1, l2, out_ref):
        m = jnp.maximum(jnp.maximum(l0[...], l1[...]), l2[...])
        e0, e1, e2 = jnp.exp(l0[...] - m), jnp.exp(l1[...] - m), jnp.exp(l2[...] - m)
        inv = 1.0 / (e0 + e1 + e2)
        out_ref[...] = ((e0 * inv) * o0[...] + (e1 * inv) * o1[...] + (e2 * inv) * o2[...]).astype(BF16)

    return _rowwise(name, fn, list(os_) + list(lses), [], [((s, GROUP_W), BF16)])[0]


def attn_merge_bwd(d_out, os_, lses, name):
    s = d_out.shape[0]

    def fn(i, do_ref, o0, o1, o2, l0, l1, l2, d0, d1, d2, c0, c1, c2):
        first = _lane_first_half((do_ref.shape[0], 2 * HD))
        m = jnp.maximum(jnp.maximum(l0[...], l1[...]), l2[...])
        e0, e1, e2 = jnp.exp(l0[...] - m), jnp.exp(l1[...] - m), jnp.exp(l2[...] - m)
        inv = 1.0 / (e0 + e1 + e2)
        w0, w1, w2 = e0 * inv, e1 * inv, e2 * inv
        do = do_ref[...]
        prod = do * (w0 * o0[...] + w1 * o1[...] + w2 * o2[...])
        for pr in range(GROUP_W // (2 * HD)):
            cols = slice(pr * 2 * HD, (pr + 1) * 2 * HD)
            ta, tb = _pair_sum(prod[:, cols], first)
            t = jnp.where(first, ta, tb)
            for w, c_ref in ((w0, c0), (w1, c1), (w2, c2)):
                c_ref[:, cols] = w[:, cols] * t
        for w, d_ref in ((w0, d0), (w1, d1), (w2, d2)):
            d_ref[...] = (w * do).astype(BF16)

    shp = (s, GROUP_W)
    return _rowwise(name, fn, [d_out] + list(os_) + list(lses), [],
                    [(shp, BF16)] * 3 + [(shp, F32)] * 3)


def attn_bwd(q, k, v, d_o, cterm, lse, pattern, name, tq=512):
    s = q.shape[0]
    d = ATTN_DILATIONS[pattern]
    blocks_per_stream = (s // d) // BLK
    nsb = tq // BLK
    n_blocks = s // BLK

    def body(q_ref, k_ref, v_ref, do_ref, c_ref, l_ref, kp_ref, vp_ref, qn_ref, don_ref, cn_ref, ln_ref,
             dq_ref, dk_ref, dv_ref):
        i = pl.program_id(0)
        rel_diag, rel_prev = _attn_masks()
        first = _lane_first_half((BLK, 2 * HD))
        second = jnp.logical_not(first)
        rd_f = (rel_diag * d).astype(F32)
        rp_f = (rel_prev * d).astype(F32)
        m_diag = rel_diag >= 0
        dq_ref[...] = jnp.zeros_like(dq_ref)
        dk_ref[...] = jnp.zeros_like(dk_ref)
        dv_ref[...] = jnp.zeros_like(dv_ref)

        def pair(qp, dop, cp, lp, kp, vp, rel_f, mask):
            dq = dk = dv = None
            for e in range(2):
                lanes = first if e == 0 else second
                slope = slopes[e]
                qm = jnp.where(lanes, qp, jnp.zeros_like(qp))
                dom = jnp.where(lanes, dop, jnp.zeros_like(dop))
                km = jnp.where(lanes, kp, jnp.zeros_like(kp))
                sc = jnp.where(mask, _dot_nt(qm, kp) * 0.125 - slope * rel_f, NEG)
                pm = jnp.exp(sc - lp[:, e * HD:e * HD + 1])
                dl = pm * (_dot_nt(dom, vp) - cp[:, e * HD:e * HD + 1])
                dl16 = dl.astype(BF16)
                t_dq = _dot(dl16, km)
                t_dk = _dot_tn(dl16, qm)
                t_dv = _dot_tn(pm.astype(BF16), dom)
                dq = t_dq if dq is None else dq + t_dq
                dk = t_dk if dk is None else dk + t_dk
                dv = t_dv if dv is None else dv + t_dv
            return dq * 0.125, dk * 0.125, dv

        for pr in range(GROUP_W // (2 * HD)):
            cols = slice(pr * 2 * HD, (pr + 1) * 2 * HD)
            slopes = [_alibi_slope(pattern * HEADS_PER_PATTERN + 2 * pr + e) for e in range(2)]
            for sb in range(nsb + 1):
                gb = i * nsb + sb
                if sb < nsb:
                    rows = slice(sb * BLK, (sb + 1) * BLK)
                    qp, dop, cp, lp = q_ref[rows, cols], do_ref[rows, cols], c_ref[rows, cols], l_ref[rows, cols]
                else:
                    qp, dop, cp, lp = qn_ref[:, cols], don_ref[:, cols], cn_ref[:, cols], ln_ref[:, cols]
                if sb < nsb:
                    dq1, dk1, dv1 = pair(qp, dop, cp, lp, k_ref[rows, cols], v_ref[rows, cols], rd_f, m_diag)
                    dq_ref[rows, cols] += dq1
                    dk_ref[rows, cols] += dk1
                    dv_ref[rows, cols] += dv1
                valid = jnp.logical_and(gb % blocks_per_stream != 0, gb < n_blocks).astype(jnp.int32)
                m_prev = jnp.logical_and(rel_prev <= BLK, (rel_prev + (1 - valid) * (4 * BLK)) <= BLK)
                if sb == 0:
                    kp, vp = kp_ref[:, cols], vp_ref[:, cols]
                else:
                    prows = slice((sb - 1) * BLK, sb * BLK)
                    kp, vp = k_ref[prows, cols], v_ref[prows, cols]
                dq0, dk0, dv0 = pair(qp, dop, cp, lp, kp, vp, rp_f, m_prev)
                if sb < nsb:
                    dq_ref[rows, cols] += dq0
                if sb > 0:
                    dk_ref[prows, cols] += dk0
                    dv_ref[prows, cols] += dv0

    cur = pl.BlockSpec((tq, GROUP_W), lambda i: (i, 0))
    prev = pl.BlockSpec((BLK, GROUP_W), lambda i: (jnp.maximum(i * nsb - 1, 0), 0))
    nxt = pl.BlockSpec((BLK, GROUP_W), lambda i: (jnp.minimum((i + 1) * nsb, n_blocks - 1), 0))
    shp = _sds((s, GROUP_W), F32)
    return _call(body, name=name, out_shape=[shp, shp, shp], grid=(s // tq,),
                 in_specs=[cur] * 6 + [prev, prev] + [nxt] * 4, out_specs=[cur, cur, cur])(
                     q, k, v, d_o, cterm, lse, k, v, q, d_o, cterm, lse)


def _band_constants(d):
    row = lax.broadcasted_iota(jnp.int32, (2 * BLK, 2 * BLK), 0)
    col = lax.broadcasted_iota(jnp.int32, (2 * BLK, 2 * BLK), 1)
    rel = BLK + jnp.where(row >= BLK, row - BLK, row) - col
    band = jnp.logical_and(rel >= 0, rel <= BLK)
    return (rel * d).astype(F32), band, (col >= BLK).astype(jnp.int32)


def _stack_heads(x, first):
    zero = jnp.zeros_like(x)
    return jnp.concatenate([jnp.where(first, x, zero), jnp.where(first, zero, x)], axis=0)


def _unstack_heads(x2, first):
    return jnp.where(first, x2[:BLK], x2[BLK:])


def _head_column(x):
    return jnp.concatenate([x[:, 0:1], x[:, HD:HD + 1]], axis=0)


def attn_fwd2(q, k, v, pattern, name, tq=512):
    s = q.shape[0]
    d = ATTN_DILATIONS[pattern]
    blocks_per_stream = (s // d) // BLK
    nsb = tq // BLK

    def body(q_ref, k_ref, v_ref, kp_ref, vp_ref, o_ref, l_ref):
        i = pl.program_id(0)
        rel_f, band, own = _band_constants(d)
        first = _lane_first_half((BLK, 2 * HD))
        upper = lax.broadcasted_iota(jnp.int32, (2 * BLK, 1), 0) < BLK
        for sb in range(nsb):
            rows = slice(sb * BLK, (sb + 1) * BLK)
            has_prev = ((i * nsb + sb) % blocks_per_stream != 0).astype(jnp.int32)
            mask = jnp.logical_and(band, (own + has_prev) > 0)
            for pr in range(GROUP_W // (2 * HD)):
                cols = slice(pr * 2 * HD, (pr + 1) * 2 * HD)
                if sb == 0:
                    kcat = jnp.concatenate([kp_ref[:, cols], k_ref[rows, cols]], axis=0)
                    vcat = jnp.concatenate([vp_ref[:, cols], v_ref[rows, cols]], axis=0)
                else:
                    both = slice((sb - 1) * BLK, (sb + 1) * BLK)
                    kcat, vcat = k_ref[both, cols], v_ref[both, cols]
                h0 = pattern * HEADS_PER_PATTERN + 2 * pr
                slope = jnp.where(upper, _alibi_slope(h0), _alibi_slope(h0 + 1))
                sc = _dot_nt(_stack_heads(q_ref[rows, cols], first), kcat) * 0.125 - slope * rel_f
                sc = jnp.where(mask, sc, NEG)
                m = jnp.max(sc, axis=-1, keepdims=True)
                p = jnp.exp(sc - m)
                l = jnp.sum(p, axis=-1, keepdims=True)
                o2 = _dot((p * (1.0 / l)).astype(BF16), vcat)
                o_ref[rows, cols] = _unstack_heads(o2, first)
                lse = m + jnp.log(l)
                l_ref[rows, cols] = jnp.where(first, lse[:BLK], lse[BLK:])

    cur = pl.BlockSpec((tq, GROUP_W), lambda i: (i, 0))
    prev = pl.BlockSpec((BLK, GROUP_W), lambda i: (jnp.maximum(i * nsb - 1, 0), 0))
    return _call(body, name=name, out_shape=[_sds((s, GROUP_W), F32), _sds((s, GROUP_W), F32)], grid=(s // tq,),
                 in_specs=[cur, cur, cur, prev, prev], out_specs=[cur, cur])(q, k, v, k, v)


def attn_bwd2(q, k, v, d_o, cterm, lse, pattern, name, tq=512):
    s = q.shape[0]
    d = ATTN_DILATIONS[pattern]
    blocks_per_stream = (s // d) // BLK
    nsb = tq // BLK
    n_blocks = s // BLK

    def body(q_ref, k_ref, v_ref, do_ref, c_ref, l_ref, kp_ref, vp_ref, qn_ref, kn_ref, vn_ref, don_ref, cn_ref,
             ln_ref, dq_ref, dk_ref, dv_ref):
        i = pl.program_id(0)
        rel_f, band, own = _band_constants(d)
        first = _lane_first_half((BLK, 2 * HD))
        upper = lax.broadcasted_iota(jnp.int32, (2 * BLK, 1), 0) < BLK
        dk_ref[...] = jnp.zeros_like(dk_ref)
        dv_ref[...] = jnp.zeros_like(dv_ref)
        for sb in range(nsb + 1):
            gb = i * nsb + sb
            rows = slice(sb * BLK, (sb + 1) * BLK)
            before = slice((sb - 1) * BLK, sb * BLK)
            inside = (gb < n_blocks).astype(jnp.int32)
            has_prev = jnp.logical_and(gb % blocks_per_stream != 0, gb < n_blocks).astype(jnp.int32)
            mask = jnp.logical_and(band, (own * inside + has_prev) > 0)
            for pr in range(GROUP_W // (2 * HD)):
                cols = slice(pr * 2 * HD, (pr + 1) * 2 * HD)
                if sb == 0:
                    kcat = jnp.concatenate([kp_ref[:, cols], k_ref[rows, cols]], axis=0)
                    vcat = jnp.concatenate([vp_ref[:, cols], v_ref[rows, cols]], axis=0)
                elif sb == nsb:
                    kcat = jnp.concatenate([k_ref[before, cols], kn_ref[:, cols]], axis=0)
                    vcat = jnp.concatenate([v_ref[before, cols], vn_ref[:, cols]], axis=0)
                else:
                    both = slice((sb - 1) * BLK, (sb + 1) * BLK)
                    kcat, vcat = k_ref[both, cols], v_ref[both, cols]
                if sb < nsb:
                    qp, dop, cp, lp = q_ref[rows, cols], do_ref[rows, cols], c_ref[rows, cols], l_ref[rows, cols]
                else:
                    qp, dop, cp, lp = qn_ref[:, cols], don_ref[:, cols], cn_ref[:, cols], ln_ref[:, cols]
                h0 = pattern * HEADS_PER_PATTERN + 2 * pr
                slope = jnp.where(upper, _alibi_slope(h0), _alibi_slope(h0 + 1))
                q2 = _stack_heads(qp, first)
                do2 = _stack_heads(dop, first)
                sc = jnp.where(mask, _dot_nt(q2, kcat) * 0.125 - slope * rel_f, NEG)
                pm = jnp.exp(sc - _head_column(lp))
                dl = (pm * (_dot_nt(do2, vcat) - _head_column(cp))).astype(BF16)
                if sb < nsb:
                    dq_ref[rows, cols] = _unstack_heads(_dot(dl, kcat), first) * 0.125
                dk2 = _dot_tn(dl, q2) * 0.125
                dv2 = _dot_tn(pm.astype(BF16), do2)
                if sb > 0:
                    dk_ref[before, cols] += dk2[:BLK]
                    dv_ref[before, cols] += dv2[:BLK]
                if sb < nsb:
                    dk_ref[rows, cols] += dk2[BLK:]
                    dv_ref[rows, cols] += dv2[BLK:]

    cur = pl.BlockSpec((tq, GROUP_W), lambda i: (i, 0))
    prev = pl.BlockSpec((BLK, GROUP_W), lambda i: (jnp.maximum(i * nsb - 1, 0), 0))
    nxt = pl.BlockSpec((BLK, GROUP_W), lambda i: (jnp.minimum((i + 1) * nsb, n_blocks - 1), 0))
    shp = _sds((s, GROUP_W), F32)
    return _call(body, name=name, out_shape=[shp, shp, shp], grid=(s // tq,),
                 in_specs=[cur] * 6 + [prev, prev] + [nxt] * 6, out_specs=[cur, cur, cur])(
                     q, k, v, d_o, cterm, lse, k, v, q, k, v, d_o, cterm, lse)


HALO = 16
CONV_TQ = 512


def conv_fwd(p, w, b, name):
    s = p.shape[0]
    tq = CONV_TQ
    ncol = SSD_CONV_DIM // GROUP_W
    cb0 = COL_XBC // GROUP_W

    def body(u_ref, up_ref, w_ref, b_ref, c_ref, xc_ref):
        i = pl.program_id(0)
        prev = up_ref[...].astype(F32) * (i > 0).astype(F32)
        ext = jnp.concatenate([prev, u_ref[...].astype(F32)], axis=0)
        acc = b_ref[...] + w_ref[SSD_CONV - 1:SSD_CONV, :] * ext[HALO:HALO + tq]
        for kk in range(SSD_CONV - 1):
            acc += w_ref[kk:kk + 1, :] * pltpu.roll(ext, SSD_CONV - 1 - kk, 0)[HALO:HALO + tq]
        c_ref[...] = acc.astype(BF16)
        xc_ref[...] = (acc * _sigmoid(acc)).astype(BF16)

    cur_in = pl.BlockSpec((tq, GROUP_W), lambda i, j: (i, cb0 + j))
    prev_in = pl.BlockSpec((HALO, GROUP_W), lambda i, j: (jnp.maximum(i * (tq // HALO) - 1, 0), cb0 + j))
    cur_out = pl.BlockSpec((tq, GROUP_W), lambda i, j: (i, j))
    shp = _sds((s, SSD_CONV_DIM), BF16)
    return _call(body, name=name, out_shape=[shp, shp], grid=(s // tq, ncol),
                 in_specs=[cur_in, prev_in, pl.BlockSpec((SSD_CONV, GROUP_W), lambda i, j: (0, j)),
                           pl.BlockSpec((1, GROUP_W), lambda i, j: (0, j))],
                 out_specs=[cur_out, cur_out])(p, p, w, b)


def conv_bwd(p, cpre, dxs, d_b, d_c, w, name):
    s = p.shape[0]
    tq = CONV_TQ
    ncol = SSD_CONV_DIM // GROUP_W
    n_xs = SSD_INNER // GROUP_W
    cb0 = COL_XBC // GROUP_W
    nt = s // tq

    def body(u_ref, up_ref, c_ref, cn_ref, dx_ref, dxn_ref, dbm_ref, dbmn_ref, dcm_ref, dcmn_ref, w_ref,
             du_ref, dw_ref, db_ref):
        j, i = pl.program_id(0), pl.program_id(1)

        def dpre(c16, dx):
            c = c16.astype(F32)
            sg = _sigmoid(c)
            return dx * (sg * (1.0 + c * (1.0 - sg)))

        def pick(a_ref, b_ref, c_ref_):
            return jnp.where(j < n_xs, a_ref[...], jnp.where(j == n_xs, b_ref[...], c_ref_[...]))

        dc = dpre(c_ref[...], pick(dx_ref, dbm_ref, dcm_ref))
        dcn = dpre(cn_ref[...], pick(dxn_ref, dbmn_ref, dcmn_ref)) * (i < nt - 1).astype(F32)
        dext = jnp.concatenate([dc, dcn], axis=0)
        prev = up_ref[...].astype(F32) * (i > 0).astype(F32)
        uext = jnp.concatenate([prev, u_ref[...].astype(F32)], axis=0)

        @pl.when(i == 0)
        def _():
            dw_ref[...] = jnp.zeros_like(dw_ref)
            db_ref[...] = jnp.zeros_like(db_ref)

        du = w_ref[SSD_CONV - 1:SSD_CONV, :] * dc
        for kk in range(SSD_CONV - 1):
            sh = SSD_CONV - 1 - kk
            du += w_ref[kk:kk + 1, :] * pltpu.roll(dext, tq + HALO - sh, 0)[0:tq]
        du_ref[...] = du.astype(BF16)
        for kk in range(SSD_CONV):
            shifted = uext if kk == SSD_CONV - 1 else pltpu.roll(uext, SSD_CONV - 1 - kk, 0)
            dw_ref[kk:kk + 1, :] += jnp.sum(dc * shifted[HALO:HALO + tq], axis=0, keepdims=True)
        db_ref[...] += jnp.sum(dc, axis=0, keepdims=True)

    hb = tq // HALO
    cur_p = pl.BlockSpec((tq, GROUP_W), lambda j, i: (i, cb0 + j))
    prev_p = pl.BlockSpec((HALO, GROUP_W), lambda j, i: (jnp.maximum(i * hb - 1, 0), cb0 + j))
    cur = pl.BlockSpec((tq, GROUP_W), lambda j, i: (i, j))
    nxt = pl.BlockSpec((HALO, GROUP_W), lambda j, i: (jnp.minimum((i + 1) * hb, s // HALO - 1), j))

    def piece(first_tile, n_tiles):
        def on(j):
            return jnp.logical_and(j >= first_tile, j < first_tile + n_tiles)

        def col(j):
            return jnp.clip(j - first_tile, 0, n_tiles - 1)

        return (pl.BlockSpec((tq, GROUP_W), lambda j, i: (jnp.where(on(j), i, 0), col(j))),
                pl.BlockSpec((HALO, GROUP_W),
                             lambda j, i: (jnp.where(on(j), jnp.minimum((i + 1) * hb, s // HALO - 1), 0), col(j))))

    return _call(body, name=name,
                 out_shape=[_sds((s, SSD_CONV_DIM), BF16), _sds((8, SSD_CONV_DIM), F32), _sds((1, SSD_CONV_DIM), F32)],
                 grid=(ncol, nt),
                 in_specs=[cur_p, prev_p, cur, nxt, *piece(0, n_xs), *piece(n_xs, 1), *piece(n_xs + 1, 1),
                           pl.BlockSpec((SSD_CONV, GROUP_W), lambda j, i: (0, j))],
                 out_specs=[cur, pl.BlockSpec((8, GROUP_W), lambda j, i: (0, j)),
                            pl.BlockSpec((1, GROUP_W), lambda j, i: (0, j))])(
                                p, p, cpre, cpre, dxs, dxs, d_b, d_b, d_c, d_c, w)


def _softplus(x):
    return jnp.maximum(x, 0.0) + jnp.log(1.0 + jnp.exp(-jnp.abs(x)))


def _ssd_decays(dtr_ref, dtrt_ref, bias_ref, biast_ref, alog_ref, alogt_ref):
    row = lax.broadcasted_iota(jnp.int32, (BLK, BLK), 0)
    col = lax.broadcasted_iota(jnp.int32, (BLK, BLK), 1)
    lower = (row >= col).astype(F32)
    upper = (row <= col).astype(F32)
    dtb = dtr_ref[...] + bias_ref[...]
    dt = _softplus(dtb)
    a = dt * (-jnp.exp(alog_ref[...]))
    cs = _dot_hi(lower, a)
    a_t = _softplus(dtrt_ref[...] + biast_ref[...]) * (-jnp.exp(alogt_ref[...]))
    cs_t = _dot_hi(a_t, upper)
    return dtb, dt, cs, cs_t, row, col, upper


def ssd_fwd(p, xc, dtg, dtg_t, params, gn, name):
    s = p.shape[0]
    nc = s // BLK
    bias, bias_t, alog, alog_t, dskip = params

    def body(xs_ref, b_ref, c_ref, z_ref, dtr_ref, dtrt_ref, bias_ref, biast_ref, alog_ref, alogt_ref, dsk_ref,
             gn_ref, y_ref, sin_ref, hp_ref, h_ref):
        c_idx = pl.program_id(1)

        @pl.when(c_idx == 0)
        def _():
            h_ref[...] = jnp.zeros_like(h_ref)

        _, dt, cs, cs_t, row, col, _ = _ssd_decays(dtr_ref, dtrt_ref, bias_ref, biast_ref, alog_ref, alogt_ref)
        first = _lane_first_half((BLK, 2 * HD))
        first_row = _lane_first_half((1, 2 * HD))
        tril = row >= col
        b16, c16 = b_ref[...], c_ref[...]
        cb = _dot_nt(c16, b16)
        n_pairs = GROUP_W // (2 * HD)
        tot = cs[BLK - 1:BLK, :]
        exp_cs, exp_rest, exp_tot = jnp.exp(cs), jnp.exp(tot - cs), jnp.exp(tot)

        def per_head(v, mask):
            return jnp.concatenate([jnp.where(mask, v[:, 2 * pr:2 * pr + 1], v[:, 2 * pr + 1:2 * pr + 2])
                                    for pr in range(n_pairs)], axis=1)

        xs = xs_ref[...].astype(F32)
        xt = xs * per_head(dt, first)
        xt16 = xt.astype(BF16)
        hstate = jnp.concatenate([h_ref[pr] for pr in range(n_pairs)], axis=1)
        for pr in range(n_pairs):
            hp_ref[pr] = h_ref[pr]
        y_off = per_head(exp_cs, first) * _dot(c16, hstate.astype(BF16))
        new = per_head(exp_tot, first_row) * hstate + _dot_tn(b16, (per_head(exp_rest, first) * xt).astype(BF16))
        for pr in range(n_pairs):
            h_ref[pr] = new[:, pr * 2 * HD:(pr + 1) * 2 * HD]
        y_diag = []
        for pr in range(n_pairs):
            cols = slice(pr * 2 * HD, (pr + 1) * 2 * HD)
            m2 = jnp.concatenate(
                [(cb * jnp.exp(jnp.where(tril, cs[:, h:h + 1] - cs_t[h:h + 1, :], NEG))).astype(BF16)
                 for h in (2 * pr, 2 * pr + 1)], axis=1)
            y_diag.append(_dot(m2, _stack_heads(xt16[:, cols], first)))
        y = jnp.concatenate(y_diag, axis=1) + y_off + xs * per_head(dsk_ref[...], first_row)
        y_ref[...] = y
        zv = z_ref[...].astype(F32)
        yz = y * (zv * _sigmoid(zv))
        r = lax.rsqrt(jnp.mean(yz * yz, axis=-1, keepdims=True) + EPS)
        sin_ref[...] = (yz * r * gn_ref[...]).astype(BF16)

    nb0 = SSD_INNER // BLK
    gparam = pl.BlockSpec((None, 1, 8), lambda g, c: (g, 0, 0))
    gparam_t = pl.BlockSpec((None, 8, 1), lambda g, c: (g, 0, 0))
    return _call(
        body, name=name,
        out_shape=[_sds((s, SSD_INNER), F32), _sds((s, SSD_INNER), BF16),
                   _sds((SSD_GROUPS, nc, 4, BLK, 2 * HD), F32)],
        grid=(SSD_GROUPS, nc),
        in_specs=[pl.BlockSpec((BLK, GROUP_W), lambda g, c: (c, g)),
                  pl.BlockSpec((BLK, BLK), lambda g, c: (c, nb0 + g)),
                  pl.BlockSpec((BLK, BLK), lambda g, c: (c, nb0 + SSD_GROUPS + g)),
                  pl.BlockSpec((BLK, GROUP_W), lambda g, c: (c, COL_Z // GROUP_W + g)),
                  pl.BlockSpec((None, BLK, 8), lambda g, c: (g, c, 0)),
                  pl.BlockSpec((None, 8, BLK), lambda g, c: (g, 0, c)),
                  gparam, gparam_t, gparam, gparam_t, gparam,
                  pl.BlockSpec((1, GROUP_W), lambda g, c: (0, g))],
        out_specs=[pl.BlockSpec((BLK, GROUP_W), lambda g, c: (c, g)),
                   pl.BlockSpec((BLK, GROUP_W), lambda g, c: (c, g)),
                   pl.BlockSpec((None, None, 4, BLK, 2 * HD), lambda g, c: (g, c, 0, 0, 0))],
        scratch_shapes=[pltpu.VMEM((4, BLK, 2 * HD), F32)],
    )(xc, xc, xc, p, dtg, dtg_t, bias, bias_t, alog, alog_t, dskip, gn)


def ssd_bwd(p, xc, y, d_sin, hprev, dtg, dtg_t, params, gn, name):
    s = p.shape[0]
    nc = s // BLK
    bias, bias_t, alog, alog_t, dskip = params

    def body(xs_ref, b_ref, c_ref, z_ref, y_ref, dsin_ref, hp_ref, dtr_ref, dtrt_ref, bias_ref,
             biast_ref, alog_ref, alogt_ref, dsk_ref, gn_ref,
             dxs_ref, db_ref, dc_ref, dz_ref, ddt_ref, da_ref, dbias_ref, ddsk_ref, dgn_ref, dh_ref):
        c_idx = pl.program_id(1)

        @pl.when(c_idx == 0)
        def _():
            dh_ref[...] = jnp.zeros_like(dh_ref)
            da_ref[...] = jnp.zeros_like(da_ref)
            dbias_ref[...] = jnp.zeros_like(dbias_ref)
            ddsk_ref[...] = jnp.zeros_like(ddsk_ref)
            dgn_ref[...] = jnp.zeros_like(dgn_ref)

        dtb, dt, cs, cs_t, row, col, upper = _ssd_decays(dtr_ref, dtrt_ref, bias_ref, biast_ref, alog_ref, alogt_ref)
        first = _lane_first_half((BLK, 2 * HD))
        second = jnp.logical_not(first)
        first_row = _lane_first_half((1, 2 * HD))
        tril = row >= col
        triu = row <= col
        last_row = lax.broadcasted_iota(jnp.int32, (BLK, 1), 0) == BLK - 1
        lane8 = lax.broadcasted_iota(jnp.int32, (BLK, 8), 1)

        yv = y_ref[...]
        zv = z_ref[...].astype(F32)
        sg = _sigmoid(zv)
        yz = yv * (zv * sg)
        r = lax.rsqrt(jnp.mean(yz * yz, axis=-1, keepdims=True) + EPS)
        yzn = yz * r
        dsn = dsin_ref[...]
        dgn_ref[...] += jnp.sum(dsn * yzn, axis=0, keepdims=True)
        dsn = dsn * gn_ref[...]
        dyz = r * (dsn - yzn * jnp.mean(dsn * yzn, axis=-1, keepdims=True))
        dy = dyz * (zv * sg)
        dz_ref[...] = (dyz * yv * (sg * (1.0 + zv * (1.0 - sg)))).astype(BF16)
        xs_all = xs_ref[...].astype(F32)
        ddsk_ref[...] += jnp.sum(dy * xs_all, axis=0, keepdims=True)

        b16, c16 = b_ref[...], c_ref[...]
        cb = _dot_nt(c16, b16)
        cb_t = _dot_nt(b16, c16)
        n_pairs = GROUP_W // (2 * HD)
        tot = cs[BLK - 1:BLK, :]
        exp_cs, exp_rest, exp_tot = jnp.exp(cs), jnp.exp(tot - cs), jnp.exp(tot)

        def per_head(v, mask):
            return jnp.concatenate([jnp.where(mask, v[:, 2 * pr:2 * pr + 1], v[:, 2 * pr + 1:2 * pr + 2])
                                    for pr in range(n_pairs)], axis=1)

        def head_sums(v):
            out = jnp.zeros((BLK, 8), F32)
            for pr in range(n_pairs):
                sa, sb = _pair_sum(v[:, pr * 2 * HD:(pr + 1) * 2 * HD], first)
                out = jnp.where(lane8 == 2 * pr, sa, jnp.where(lane8 == 2 * pr + 1, sb, out))
            return out

        dt_w, e_w, f_w = per_head(dt, first), per_head(exp_cs, first), per_head(exp_rest, first)
        xt = xs_all * dt_w
        xt16 = xt.astype(BF16)
        hstate = jnp.concatenate([hp_ref[pr] for pr in range(n_pairs)], axis=1)
        h16 = hstate.astype(BF16)
        dhn = jnp.concatenate([dh_ref[pr] for pr in range(n_pairs)], axis=1)
        dhn16 = dhn.astype(BF16)
        edy16 = (e_w * dy).astype(BF16)
        y_off = e_w * _dot(c16, h16)
        dcs_all = head_sums(dy * y_off)
        dc_acc = _dot_nt(edy16, h16)
        zmat = _dot(b16, dhn16)
        t_all = head_sums(zmat * xt) * exp_rest
        hh_rows = jnp.sum(head_sums(dhn * hstate), axis=0, keepdims=True)
        dtot = jnp.sum(t_all, axis=0, keepdims=True) + hh_rows * exp_tot
        dcs_all = dcs_all - t_all + jnp.where(last_row, dtot, 0.0)
        fxt16 = (f_w * xt).astype(BF16)
        db_acc = _dot_nt(fxt16, dhn16)
        dh_new = _dot_tn(c16, edy16) + per_head(exp_tot, first_row) * dhn
        for pr in range(n_pairs):
            dh_ref[pr] = dh_new[:, pr * 2 * HD:(pr + 1) * 2 * HD]
        g_sum = jnp.zeros((BLK, BLK), F32)
        gt_sum = jnp.zeros((BLK, BLK), F32)
        d_xt_parts = []
        for pr in range(n_pairs):
            cols = slice(pr * 2 * HD, (pr + 1) * 2 * HD)
            dym2 = _stack_heads(dy[:, cols].astype(BF16), first)
            d_m2 = _dot_nt(dym2, xt16[:, cols])
            d_mt2 = _dot_nt(xt16[:, cols], dym2)
            mt2 = []
            for e, h in enumerate((2 * pr, 2 * pr + 1)):
                cs_c, cs_r = cs[:, h:h + 1], cs_t[h:h + 1, :]
                decay = jnp.exp(jnp.where(tril, cs_c - cs_r, NEG))
                decay_t = jnp.exp(jnp.where(triu, cs_r - cs_c, NEG))
                gm = d_m2[e * BLK:(e + 1) * BLK] * decay
                gmt = d_mt2[:, e * BLK:(e + 1) * BLK] * decay_t
                g_sum += gm
                gt_sum += gmt
                dcs_h = jnp.sum(gm * cb, axis=-1, keepdims=True) - jnp.sum(gmt * cb_t, axis=-1, keepdims=True)
                dcs_all = dcs_all + jnp.where(lane8 == h, dcs_h, 0.0)
                mt2.append((cb_t * decay_t).astype(BF16))
            d_xt_parts.append(_dot(jnp.concatenate(mt2, axis=1), dym2))
        d_xt = jnp.concatenate(d_xt_parts, axis=1) + f_w * zmat
        dxs_ref[...] = dy * per_head(dsk_ref[...], first_row) + d_xt * dt_w
        ddtx_all = head_sums(d_xt * xs_all)

        dc_ref[...] = dc_acc + _dot(g_sum.astype(BF16), b16)
        db_ref[...] = db_acc + _dot(gt_sum.astype(BF16), c16)
        d_a = _dot_hi(upper, dcs_all)
        a_neg = -jnp.exp(alog_ref[...])
        ddt = ddtx_all + d_a * a_neg
        da_ref[...] += jnp.sum(d_a * dt, axis=0, keepdims=True)
        ddtr = ddt * _sigmoid(dtb)
        ddt_ref[...] = ddtr
        dbias_ref[...] += jnp.sum(ddtr, axis=0, keepdims=True)

    nb0 = SSD_INNER // BLK
    rc = lambda c: nc - 1 - c
    gparam = pl.BlockSpec((None, 1, 8), lambda g, c: (g, 0, 0))
    gparam_t = pl.BlockSpec((None, 8, 1), lambda g, c: (g, 0, 0))
    wide = pl.BlockSpec((BLK, GROUP_W), lambda g, c: (rc(c), g))
    narrow = pl.BlockSpec((BLK, BLK), lambda g, c: (rc(c), g))
    return _call(
        body, name=name,
        out_shape=[_sds((s, SSD_INNER), F32), _sds((s, GROUP_W), F32), _sds((s, GROUP_W), F32),
                   _sds((s, SSD_INNER), BF16), _sds((SSD_GROUPS, s, 8), F32),
                   _sds((SSD_GROUPS, 1, 8), F32), _sds((SSD_GROUPS, 1, 8), F32),
                   _sds((SSD_GROUPS, 1, GROUP_W), F32), _sds((1, SSD_INNER), F32)],
        grid=(SSD_GROUPS, nc),
        in_specs=[wide,
                  pl.BlockSpec((BLK, BLK), lambda g, c: (rc(c), nb0 + g)),
                  pl.BlockSpec((BLK, BLK), lambda g, c: (rc(c), nb0 + SSD_GROUPS + g)),
                  pl.BlockSpec((BLK, GROUP_W), lambda g, c: (rc(c), COL_Z // GROUP_W + g)),
                  wide, wide,
                  pl.BlockSpec((None, None, 4, BLK, 2 * HD), lambda g, c: (g, rc(c), 0, 0, 0)),
                  pl.BlockSpec((None, BLK, 8), lambda g, c: (g, rc(c), 0)),
                  pl.BlockSpec((None, 8, BLK), lambda g, c: (g, 0, rc(c))),
                  gparam, gparam_t, gparam, gparam_t, gparam,
                  pl.BlockSpec((1, GROUP_W), lambda g, c: (0, g))],
        out_specs=[wide, narrow, narrow, wide,
                   pl.BlockSpec((None, BLK, 8), lambda g, c: (g, rc(c), 0)),
                   gparam, gparam,
                   pl.BlockSpec((None, 1, GROUP_W), lambda g, c: (g, 0, 0)),
                   pl.BlockSpec((1, GROUP_W), lambda g, c: (0, g))],
        scratch_shapes=[pltpu.VMEM((4, BLK, 2 * HD), F32)],
    )(xc, xc, xc, p, y, d_sin, hprev, dtg, dtg_t, bias, bias_t, alog, alog_t, dskip, gn)


def merge_fwd(p, a, sbr, name, tm=512):
    s = p.shape[0]
    nj = D_MODEL // GROUP_W

    def body(ga_ref, gs_ref, a_ref, s_ref, o_ref):
        o_ref[...] = (_sigmoid(ga_ref[...].astype(F32)) * a_ref[...]
                      + _sigmoid(gs_ref[...].astype(F32)) * s_ref[...]).astype(BF16)

    blk = pl.BlockSpec((tm, GROUP_W), lambda i, j: (i, j))
    return _call(body, name=name, out_shape=_sds((s, D_MODEL), BF16), grid=(s // tm, nj),
                 in_specs=[pl.BlockSpec((tm, GROUP_W), lambda i, j: (i, COL_GA // GROUP_W + j)),
                           pl.BlockSpec((tm, GROUP_W), lambda i, j: (i, COL_GS // GROUP_W + j)), blk, blk],
                 out_specs=blk)(p, p, a, sbr)


def merge_bwd(p, a, sbr, dmerged, name, tm=512):
    s = p.shape[0]
    nj = D_MODEL // GROUP_W

    def body(ga_ref, gs_ref, a_ref, s_ref, dm_ref, da_ref, ds_ref, dga_ref, dgs_ref):
        dm = dm_ref[...]
        sa = _sigmoid(ga_ref[...].astype(F32))
        ss = _sigmoid(gs_ref[...].astype(F32))
        da_ref[...] = (dm * sa).astype(BF16)
        ds_ref[...] = (dm * ss).astype(BF16)
        dga_ref[...] = (dm * a_ref[...] * sa * (1.0 - sa)).astype(BF16)
        dgs_ref[...] = (dm * s_ref[...] * ss * (1.0 - ss)).astype(BF16)

    blk = pl.BlockSpec((tm, GROUP_W), lambda i, j: (i, j))
    shp = _sds((s, D_MODEL), BF16)
    return _call(body, name=name, out_shape=[shp] * 4, grid=(s // tm, nj),
                 in_specs=[pl.BlockSpec((tm, GROUP_W), lambda i, j: (i, COL_GA // GROUP_W + j)),
                           pl.BlockSpec((tm, GROUP_W), lambda i, j: (i, COL_GS // GROUP_W + j)), blk, blk, blk],
                 out_specs=[blk] * 4)(p, p, a, sbr, dmerged)


def _group_major(v):
    return v.reshape(SSD_GROUPS, 1, 8), v.reshape(SSD_GROUPS, 8, 1)


def mixer_forward(x, w, rider=None, later_weights=None):
    s = x.shape[0]
    h = rms_fwd(x, w["mix_norm"], "mix_rms")
    p = matmul_nn(h, w["w_in_main"], "mix_proj", BF16, tm=1024, tn=512, rider=rider)
    rode = None
    if rider is not None:
        p, rode = p
        w = dict(w, **later_weights(rode))
    dt_raw = matmul_nn(h, w["w_in_dt"], "mix_proj_dt", F32, tm=1024, tn=DT_PAD)
    qn, kn = qk_norm_fwd(p, w["q_gain"], w["k_gain"], "qk_norm")
    streams, os_, lses = [], [], []
    for g, d in enumerate(ATTN_DILATIONS):
        cols = slice(g * GROUP_W, (g + 1) * GROUP_W)
        qs, ks = _to_streams(qn[:, cols], d), _to_streams(kn[:, cols], d)
        vs = _to_streams(p[:, COL_V + g * GROUP_W:COL_V + (g + 1) * GROUP_W], d)
        o, lse = attn_fwd2(qs, ks, vs, g, f"attn_fwd{g}")
        streams.append((qs, ks, vs, lse))
        os_.append(_from_streams(o, d))
        lses.append(_from_streams(lse, d))
    attn_o = attn_merge_fwd(os_, lses, "attn_merge")
    cpre, xc = conv_fwd(p, w["conv_w"], w["conv_b"], "conv_fwd")
    dtg = dt_raw[:, :SSD_HEADS].reshape(s, SSD_GROUPS, 8).transpose(1, 0, 2)
    dtg_t = dtg.transpose(0, 2, 1)
    params = (*_group_major(w["dt_bias"]), *_group_major(w["a_log"]), _group_major(w["d_skip"])[0])
    y, s_in, hprev = ssd_fwd(p, xc, dtg, dtg_t, params, w["ssd_norm"], "ssd_fwd")
    a = matmul_nn(attn_o, w["w_attn_branch"], "attn_branch", F32, tm=1024, tn=512)
    sbr = matmul_nn(s_in, w["w_ssd_branch"], "ssd_branch", F32, tm=1024, tn=512)
    merged = merge_fwd(p, a, sbr, "merge")
    x_out = matmul_nn(merged, w["w_out"], "mix_out", F32, tm=1024, tn=512, res=x)
    saved = dict(h=h, p=p, streams=streams, os=os_, lses=lses, attn_o=attn_o, cpre=cpre, xc=xc, dtg=dtg,
                 dtg_t=dtg_t, params=params, y=y, s_in=s_in, hprev=hprev, a=a, sbr=sbr, merged=merged, w=w)
    return x_out, saved, rode


def mixer_backward(dx_out, x, sv, ride_early=None, ride_late=None):
    s = x.shape[0]
    p = sv["p"]
    w = sv["w"]
    g = {}
    dmerged = matmul_nt(dx_out, w["w_out"], "d_merged", F32, tm=1024, tn=512, tk=1024)
    g["w_out"] = matmul_tn(sv["merged"], dx_out, "dw_out", tn=512, ts=1024)
    da, ds, dga, dgs = merge_bwd(p, sv["a"], sv["sbr"], dmerged, "merge_bwd")
    g["w_attn_branch"] = matmul_tn(sv["attn_o"], da, "dw_attn_branch", tn=512, ts=1024)
    g["w_ssd_branch"] = matmul_tn(sv["s_in"], ds, "dw_ssd_branch", tn=512, ts=1024)
    d_attn_o = matmul_nt(da, w["w_attn_branch"], "d_attn_o", F32, tm=1024, tn=512, tk=1024)
    d_sin = matmul_nt(ds, w["w_ssd_branch"], "d_ssd_in", F32, tm=1024, tn=512, tk=1024)
    dxs, d_b, d_c, dz, ddt, d_asum, d_bias, d_dsk, d_gn = ssd_bwd(
        p, sv["xc"], sv["y"], d_sin, sv["hprev"], sv["dtg"], sv["dtg_t"], sv["params"], w["ssd_norm"], "ssd_bwd")
    dxbc, d_convw, d_convb = conv_bwd(p, sv["cpre"], dxs, d_b, d_c, w["conv_w"], "conv_bwd")
    g["conv_w"] = d_convw[:SSD_CONV]
    g["conv_b"] = d_convb
    g["dt_bias"] = d_bias.reshape(1, SSD_HEADS)
    g["a_log"] = (d_asum * (-jnp.exp(sv["params"][2]))).reshape(1, SSD_HEADS)
    g["d_skip"] = jnp.sum(d_dsk.reshape(SSD_HEADS, HD), axis=1).reshape(1, SSD_HEADS)
    g["ssd_norm"] = d_gn
    merged_bwd = attn_merge_bwd(d_attn_o, sv["os"], sv["lses"], "attn_merge_bwd")
    dqs, dks, dvs = [], [], []
    for gi, d in enumerate(ATTN_DILATIONS):
        qs, ks, vs, lse = sv["streams"][gi]
        d_o = _to_streams(merged_bwd[gi], d)
        cterm = _to_streams(merged_bwd[3 + gi], d)
        dq, dk, dv = attn_bwd2(qs, ks, vs, d_o, cterm, lse, gi, f"attn_bwd{gi}")
        dqs.append(_from_streams(dq, d))
        dks.append(_from_streams(dk, d))
        dvs.append(_from_streams(dv, d).astype(BF16))
    dqk, d_qg, d_kg = qk_norm_bwd(p, dqs, dks, w["q_gain"], w["k_gain"], "qk_norm_bwd")
    g["q_norm"] = jnp.sum(d_qg.reshape(N_ATTN_HEADS, HD), axis=0).reshape(1, HD)
    g["k_norm"] = jnp.sum(d_kg.reshape(N_ATTN_HEADS, HD), axis=0).reshape(1, HD)
    dp = [dqk, jnp.concatenate(dvs, axis=1), dz, dxbc, dga, dgs]
    ddt_pad = jnp.pad(ddt.transpose(1, 0, 2).reshape(s, SSD_HEADS), ((0, 0), (0, DT_PAD - SSD_HEADS)))
    if ride_early is not None:
        g["w_in_main"], g["rode_early"] = matmul_tn_pieces(sv["h"], dp, "dw_in", tn=512, ts=1024,
                                                           rider=ride_early(g))
    else:
        g["w_in_main"] = matmul_tn_pieces(sv["h"], dp, "dw_in", tn=512, ts=1024)
    g["w_in_dt"] = matmul_tn(sv["h"], ddt_pad, "dw_in_dt", tn=DT_PAD, ts=1024)
    if ride_late is not None:
        dh_main, g["rode_late"] = matmul_nt_pieces(dp, w["w_in_main"], "d_mix_h", F32, tm=1024, tn=512, tk=512,
                                                   rider=ride_late(g))
    else:
        dh_main = matmul_nt_pieces(dp, w["w_in_main"], "d_mix_h", F32, tm=1024, tn=512, tk=512)
    dh_dt = matmul_nt(ddt_pad, w["w_in_dt"], "d_mix_h_dt", F32, tm=1024, tn=1024, tk=DT_PAD)
    dx, g["mix_norm"] = rms_bwd([dh_main, dh_dt], x, w["mix_norm"], dx_out, "mix_drms")
    return dx, g


ANY = pl.BlockSpec(memory_space=pl.ANY)


def _place():
    x, y, c = lax.axis_index("x"), lax.axis_index("y"), lax.axis_index("c")
    chips = [(1 - x, y), (x, 1 - y), (1 - x, 1 - y)]
    return x, y, c, 2 * x + y, chips


def _comm_call(body, *, name, out_shape, n_in, scratch_shapes, aliases=None):
    return pl.pallas_call(
        body, out_shape=out_shape, in_specs=[ANY] * n_in, out_specs=[ANY] * len(out_shape),
        scratch_shapes=scratch_shapes, input_output_aliases=aliases or {}, name=name,
        compiler_params=pltpu.CompilerParams(has_side_effects=True))


def gather_weights(shards, small):
    n = len(shards)
    halves = [a.shape[0] // 2 for a in shards]
    out_shape = [_sds((N_CHIP,) + a.shape, a.dtype) for a in shards] + [_sds((N_CHIP,) + small.shape, small.dtype)]

    def body(*refs):
        ins, outs = refs[:n + 1], refs[n + 1:2 * n + 2]
        send1, recv1, send2, recv2, local = refs[2 * n + 2:]
        x, y, c, me, chips = _place()
        sibling = (x, y, 1 - c)

        def rows(k, chip, core):
            if k == n:
                return outs[k].at[chip]
            return outs[k].at[chip, pl.ds(core * halves[k], halves[k])]

        def level1(k, t, incoming):
            chip = 2 * chips[t][0] + chips[t][1]
            src = ins[k] if k == n else ins[k].at[pl.ds(c * halves[k], halves[k])]
            return pltpu.make_async_remote_copy(
                src_ref=src, dst_ref=rows(k, chip if incoming else me, c), send_sem=send1.at[3 * k + t],
                recv_sem=recv1.at[3 * k + t], device_id=(*chips[t], c), device_id_type=MESH)

        def level2(k, t, incoming):
            chip = 2 * chips[t][0] + chips[t][1]
            core = (1 - c) if incoming else c
            return pltpu.make_async_remote_copy(
                src_ref=rows(k, chip, core), dst_ref=rows(k, chip, core), send_sem=send2.at[3 * k + t],
                recv_sem=recv2.at[3 * k + t], device_id=sibling, device_id_type=MESH)

        own = [pltpu.make_async_copy(ins[k], outs[k].at[me], local.at[k]) for k in range(n + 1)]
        for cp in own:
            cp.start()
        first = [level1(k, t, False) for k in range(n + 1) for t in range(3)]
        for cp in first:
            cp.start()
        passed = []
        for k in range(n + 1):
            for t in range(3):
                level1(k, t, True).wait_recv()
                if k < n:
                    cp = level2(k, t, False)
                    cp.start()
                    passed.append(cp)
        for k in range(n):
            for t in range(3):
                level2(k, t, True).wait_recv()
        for cp in first + passed:
            cp.wait_send()
        for cp in own:
            cp.wait()

    dma = pltpu.SemaphoreType.DMA
    return _comm_call(body, name="gather_weights", out_shape=out_shape, n_in=n + 1,
                      scratch_shapes=[dma((3 * n + 3,)), dma((3 * n + 3,)), dma((3 * n,)), dma((3 * n,)),
                                      dma((n + 1,))])(*shards, small)


def reduce_to_sibling(grads):
    n = len(grads)
    halves = [a.shape[1] // 2 for a in grads]
    shapes = [_sds((N_CHIP, h, a.shape[2]), a.dtype) for a, h in zip(grads, halves)]

    def body(*refs):
        ins, got, kept = refs[:n], refs[n:2 * n], refs[2 * n:3 * n]
        send, recv, local = refs[3 * n:]
        x, y, c, _, _ = _place()
        copies, locals_ = [], []
        for k in range(n):
            h = halves[k]
            locals_.append(pltpu.make_async_copy(ins[k].at[:, pl.ds(c * h, h)], kept[k], local.at[k]))
            copies.append(pltpu.make_async_remote_copy(
                src_ref=ins[k].at[:, pl.ds((1 - c) * h, h)], dst_ref=got[k], send_sem=send.at[k], recv_sem=recv.at[k],
                device_id=(x, y, 1 - c), device_id_type=MESH))
        for cp in locals_ + copies:
            cp.start()
        for cp in copies:
            cp.wait_recv()
        for cp in copies:
            cp.wait_send()
        for cp in locals_:
            cp.wait()

    dma = pltpu.SemaphoreType.DMA
    res = _comm_call(body, name="reduce_to_sibling", out_shape=shapes + shapes, n_in=n,
                     scratch_shapes=[dma((n,)), dma((n,)), dma((n,))])(*grads)
    return res[:n], res[n:]


def reduce_to_owner(sums):
    n = len(sums)
    shapes = [_sds(a.shape, a.dtype) for a in sums]

    def body(*refs):
        ins, outs = refs[:n], refs[n:2 * n]
        send, recv, local = refs[2 * n:]
        x, y, c, me, chips = _place()
        copies, locals_ = [], []
        for k in range(n):
            locals_.append(pltpu.make_async_copy(ins[k].at[me], outs[k].at[3], local.at[k]))
            for t in range(3):
                chip = 2 * chips[t][0] + chips[t][1]
                copies.append(pltpu.make_async_remote_copy(
                    src_ref=ins[k].at[chip], dst_ref=outs[k].at[t], send_sem=send.at[3 * k + t],
                    recv_sem=recv.at[3 * k + t], device_id=(*chips[t], c), device_id_type=MESH))
        for cp in locals_ + copies:
            cp.start()
        for cp in copies:
            cp.wait_recv()
        for cp in copies:
            cp.wait_send()
        for cp in locals_:
            cp.wait()

    dma = pltpu.SemaphoreType.DMA
    return _comm_call(body, name="reduce_to_owner", out_shape=shapes, n_in=n,
                      scratch_shapes=[dma((3 * n,)), dma((3 * n,)), dma((n,))])(*sums)


def share_with_sibling(halves_):
    n = len(halves_)
    shapes = [_sds((2 * a.shape[0], a.shape[1]), a.dtype) for a in halves_]

    def body(*refs):
        ins, outs = refs[:n], refs[n:2 * n]
        send, recv, local = refs[2 * n:]
        x, y, c, _, _ = _place()
        copies, locals_ = [], []
        for k in range(n):
            h = ins[k].shape[0]
            mine = outs[k].at[pl.ds(c * h, h)]
            locals_.append(pltpu.make_async_copy(ins[k], mine, local.at[k]))
            copies.append(pltpu.make_async_remote_copy(
                src_ref=ins[k], dst_ref=mine, send_sem=send.at[k], recv_sem=recv.at[k],
                device_id=(x, y, 1 - c), device_id_type=MESH))
        for cp in locals_ + copies:
            cp.start()
        for cp in copies:
            cp.wait_recv()
        for cp in copies:
            cp.wait_send()
        for cp in locals_:
            cp.wait()

    dma = pltpu.SemaphoreType.DMA
    return _comm_call(body, name="share_with_sibling", out_shape=shapes, n_in=n,
                      scratch_shapes=[dma((n,)), dma((n,)), dma((n,))])(*halves_)


def _cores():
    c = lax.axis_index("c")
    return jnp.stack([c, 1 - c]).astype(jnp.int32)


def _staged_call(body, *, name, grid, in_specs, out_specs, out_shape, scratch_shapes):
    return pl.pallas_call(
        body, out_shape=out_shape, name=name,
        grid_spec=pltpu.PrefetchScalarGridSpec(num_scalar_prefetch=1, grid=grid, in_specs=in_specs,
                                               out_specs=out_specs, scratch_shapes=scratch_shapes),
        compiler_params=pltpu.CompilerParams(dimension_semantics=("arbitrary",) * len(grid),
                                             vmem_limit_bytes=V7X_VMEM_LIMIT, has_side_effects=True))


def gather_rider(shards, tiles):
    dma = pltpu.SemaphoreType.DMA
    n = len(shards)
    geo = [(a.shape[0] // 2, tm, (a.shape[0] // 2) // tm) for a, tm in zip(shards, tiles)]
    scratch = []
    for a, (h, tm, nk) in zip(shards, geo):
        scratch += [pltpu.VMEM((N_CHIP,) + a.shape, a.dtype), dma((3, nk)), dma((3, nk)), dma((3, nk)), dma((3, nk)),
                    dma((nk + 2,))]

    def copies(j, in_ref, scr):
        buf, send1, recv1, send2, recv2, local = scr[6 * j:6 * j + 6]
        h, tm, nk = geo[j]
        x, y, c, me, chips = _place()
        chip_of = [2 * chips[t][0] + chips[t][1] for t in range(3)]

        def rows(chip, core, k):
            return buf.at[chip, pl.ds(core * h + k * tm, tm)]

        def mine(k):
            if k == nk:
                return pltpu.make_async_copy(in_ref.at[pl.ds((1 - c) * h, h)], buf.at[me, pl.ds((1 - c) * h, h)],
                                             local.at[nk])
            return pltpu.make_async_copy(in_ref.at[pl.ds(c * h + k * tm, tm)], rows(me, c, k), local.at[k])

        def level1(t, k, incoming):
            place = rows(chip_of[t] if incoming else me, c, k)
            return pltpu.make_async_remote_copy(src_ref=place, dst_ref=place, send_sem=send1.at[t, k],
                                                recv_sem=recv1.at[t, k], device_id=(*chips[t], c), device_id_type=MESH)

        def level2(t, k, incoming):
            place = rows(chip_of[t], (1 - c) if incoming else c, k)
            return pltpu.make_async_remote_copy(src_ref=place, dst_ref=place, send_sem=send2.at[t, k],
                                                recv_sem=recv2.at[t, k], device_id=(x, y, 1 - c),
                                                device_id_type=MESH)

        return buf, local, nk, mine, level1, level2

    def start(ins, outs, scr):
        for j in range(n):
            _, _, nk, mine, _, _ = copies(j, ins[j], scr)
            for k in range(nk + 1):
                mine(k).start()
        for j in range(n):
            _, _, nk, mine, level1, _ = copies(j, ins[j], scr)
            for k in range(nk):
                mine(k).wait()
                for t in range(3):
                    level1(t, k, False).start()

    def finish(ins, outs, scr):
        for j in range(n):
            _, _, nk, _, level1, level2 = copies(j, ins[j], scr)
            for k in range(nk):
                for t in range(3):
                    level1(t, k, True).wait_recv()
                    level2(t, k, False).start()
        for j in range(n):
            buf, local, nk, mine, level1, level2 = copies(j, ins[j], scr)
            for k in range(nk):
                for t in range(3):
                    level2(t, k, True).wait_recv()
            for k in range(nk):
                for t in range(3):
                    level1(t, k, False).wait_send()
                    level2(t, k, False).wait_send()
            mine(nk).wait()
            pltpu.make_async_copy(buf, outs[j], local.at[nk + 1]).start()
        for j in range(n):
            buf, local, nk, _, _, _ = copies(j, ins[j], scr)
            pltpu.make_async_copy(buf, outs[j], local.at[nk + 1]).wait()

    return Rider(list(shards), [_sds((N_CHIP,) + a.shape, a.dtype) for a in shards], scratch, start, finish)


def run_alone(rider, name):
    return _call(lambda: None, name=name, out_shape=[], in_specs=[], out_specs=[], grid=(1,), rider=rider)()[1]


def sibling_sum(g, tm, name):
    _, r, cdim = g.shape
    h = r // 2
    ni = h // tm
    dma = pltpu.SemaphoreType.DMA

    def body(cores_ref, keep_ref, give_ref, out_ref, slot, send, recv):
        par = (pl.program_id(0) * ni + pl.program_id(1)) % 2
        x, y, c, _, _ = _place()
        cp = pltpu.make_async_remote_copy(src_ref=give_ref, dst_ref=slot.at[par], send_sem=send.at[par],
                                          recv_sem=recv.at[par], device_id=(x, y, 1 - c), device_id_type=MESH)
        cp.start()
        cp.wait_recv()
        out_ref[...] = (keep_ref[...].astype(F32) + slot[par].astype(F32)).astype(out_ref.dtype)
        cp.wait_send()

    flat = g.reshape(N_CHIP * r, cdim)
    return _staged_call(
        body, name=name, grid=(N_CHIP, ni),
        in_specs=[pl.BlockSpec((tm, cdim), lambda j, i, cores: ((2 * j + cores[0]) * ni + i, 0)),
                  pl.BlockSpec((tm, cdim), lambda j, i, cores: ((2 * j + cores[1]) * ni + i, 0))],
        out_specs=pl.BlockSpec((None, tm, cdim), lambda j, i, cores: (j, i, 0)),
        out_shape=_sds((N_CHIP, h, cdim), g.dtype),
        scratch_shapes=[pltpu.VMEM((2, tm, cdim), g.dtype), dma((2,)), dma((2,))],
    )(_cores(), flat, flat)


def owner_sum_rider(sums, tiles):
    dma = pltpu.SemaphoreType.DMA
    n = len(sums)
    geo = [(a.shape[1], tm, a.shape[1] // tm) for a, tm in zip(sums, tiles)]
    scratch = []
    for a, (h, tm, nk) in zip(sums, geo):
        cdim = a.shape[2]
        scratch += [pltpu.VMEM(a.shape, a.dtype), pltpu.VMEM((3, h, cdim), a.dtype), pltpu.VMEM((2, h, cdim), F32),
                    dma((3, nk)), dma((3, nk)), dma((nk,)), dma((nk,)), dma((2,))]

    def copies(j, scr):
        part, got, res, send, recv, send2, recv2, local = scr[8 * j:8 * j + 8]
        h, tm, nk = geo[j]
        x, y, c, me, chips = _place()

        def to_owner(t, k):
            chip = 2 * chips[t][0] + chips[t][1]
            return pltpu.make_async_remote_copy(
                src_ref=part.at[chip, pl.ds(k * tm, tm)], dst_ref=got.at[t, pl.ds(k * tm, tm)],
                send_sem=send.at[t, k], recv_sem=recv.at[t, k], device_id=(*chips[t], c), device_id_type=MESH)

        def to_sibling(k):
            place = res.at[c, pl.ds(k * tm, tm)]
            return pltpu.make_async_remote_copy(src_ref=place, dst_ref=place, send_sem=send2.at[k],
                                                recv_sem=recv2.at[k], device_id=(x, y, 1 - c), device_id_type=MESH)

        return part, got, res, local, to_owner, to_sibling, (tm, nk, c, me)

    def start(ins, outs, scr):
        for j in range(n):
            part, _, _, local, _, _, _ = copies(j, scr)
            pltpu.make_async_copy(ins[j], part, local.at[0]).start()
        for j in range(n):
            part, _, _, local, to_owner, _, (tm, nk, c, me) = copies(j, scr)
            pltpu.make_async_copy(ins[j], part, local.at[0]).wait()
            for k in range(nk):
                for t in range(3):
                    to_owner(t, k).start()

    def finish(ins, outs, scr):
        for j in range(n):
            part, got, res, _, to_owner, to_sibling, (tm, nk, c, me) = copies(j, scr)
            for k in range(nk):
                rows = pl.ds(k * tm, tm)
                for t in range(3):
                    to_owner(t, k).wait_recv()
                acc = part[me, rows, :].astype(F32)
                for t in range(3):
                    acc = acc + got[t, rows, :].astype(F32)
                res[c, rows, :] = acc
                to_sibling(k).start()
        for j in range(n):
            _, _, res, local, to_owner, to_sibling, (tm, nk, c, me) = copies(j, scr)
            for k in range(nk):
                to_sibling(k).wait_recv()
            for k in range(nk):
                to_sibling(k).wait_send()
                for t in range(3):
                    to_owner(t, k).wait_send()
            pltpu.make_async_copy(res, outs[j], local.at[1]).start()
        for j in range(n):
            _, _, res, local, _, _, _ = copies(j, scr)
            pltpu.make_async_copy(res, outs[j], local.at[1]).wait()

    return Rider(list(sums), [_sds((2, a.shape[1], a.shape[2]), F32) for a in sums], scratch, start, finish)


def gather_conv_w(w):
    def body(in_ref, out_ref, send, recv):
        x, y, c, me, chips = _place()
        out_ref[me] = in_ref[...]
        copies = []
        for t in range(3):
            copies.append(pltpu.make_async_remote_copy(
                src_ref=out_ref.at[me], dst_ref=out_ref.at[me], send_sem=send.at[t], recv_sem=recv.at[t],
                device_id=(*chips[t], c), device_id_type=MESH))
        for cp in copies:
            cp.start()
        for cp in copies:
            cp.wait_recv()
        for cp in copies:
            cp.wait_send()

    dma = pltpu.SemaphoreType.DMA
    vmem = pl.BlockSpec(memory_space=pltpu.VMEM)
    return pl.pallas_call(
        body, out_shape=_sds((N_CHIP,) + w.shape, w.dtype), in_specs=[vmem], out_specs=vmem, name="gather_conv_w",
        scratch_shapes=[dma((3,)), dma((3,))],
        compiler_params=pltpu.CompilerParams(has_side_effects=True))(w)


N_DEV = 8
SMALL_ROWS = 32
SMALL_LANES = 1024


def all_reduce_small(arrays):
    n_arr = len(arrays)
    places = []
    for k, a in enumerate(arrays):
        for ri in range(a.shape[0]):
            for c0 in range(0, a.shape[1], SMALL_LANES):
                places.append((k, ri, c0, min(SMALL_LANES, a.shape[1] - c0), len(places)))
    assert len(places) <= SMALL_ROWS

    def body(*refs):
        ins, outs = refs[:n_arr], refs[n_arr:2 * n_arr]
        buf, send, recv = refs[2 * n_arr:]
        x, y, c, _, _ = _place()
        me = 4 * x + 2 * y + c
        buf[me] = jnp.zeros((SMALL_ROWS, SMALL_LANES), F32)
        for k, ri, c0, width, row in places:
            buf[me, row:row + 1, 0:width] = ins[k][ri:ri + 1, c0:c0 + width]
        copies = []
        for r in range(1, N_DEV):
            px = (1 - x) if r & 4 else x
            py = (1 - y) if r & 2 else y
            pc = (1 - c) if r & 1 else c
            copies.append(pltpu.make_async_remote_copy(
                src_ref=buf.at[me], dst_ref=buf.at[me], send_sem=send.at[r - 1], recv_sem=recv.at[r - 1],
                device_id=(px, py, pc), device_id_type=MESH))
        for cp in copies:
            cp.start()
        for cp in copies:
            cp.wait_recv()
        for cp in copies:
            cp.wait_send()
        acc = buf[0]
        for j in range(1, N_DEV):
            acc = acc + buf[j]
        for k, ri, c0, width, row in places:
            outs[k][ri:ri + 1, c0:c0 + width] = acc[row:row + 1, 0:width]

    dma = pltpu.SemaphoreType.DMA
    vmem = pl.BlockSpec(memory_space=pltpu.VMEM)
    return pl.pallas_call(
        body, out_shape=[_sds(a.shape, F32) for a in arrays], in_specs=[vmem] * n_arr, out_specs=[vmem] * n_arr,
        name="all_reduce_small",
        scratch_shapes=[pltpu.VMEM((N_DEV, SMALL_ROWS, SMALL_LANES), F32), dma((N_DEV - 1,)), dma((N_DEV - 1,))],
        compiler_params=pltpu.CompilerParams(has_side_effects=True))(*arrays)


def _row_tile(rows, limit, multiple):
    return max(t for t in range(multiple, min(rows, limit) + 1, multiple) if rows % t == 0)


def add_pair(a, b, name):
    _, h, c = a.shape

    def body(a_ref, b_ref, o_ref):
        o_ref[...] = (a_ref[...].astype(F32) + b_ref[...].astype(F32)).astype(o_ref.dtype)

    blk = pl.BlockSpec((None, h, c), lambda j: (j, 0, 0))
    return _call(body, name=name, out_shape=_sds(a.shape, a.dtype), grid=(N_CHIP,), in_specs=[blk, blk],
                 out_specs=blk)(a, b)


def sum_slots(buf, name):
    _, h, c = buf.shape
    tm = _row_tile(h, 256, 16)

    def body(b_ref, o_ref):
        acc = b_ref[3].astype(F32)
        for t in range(3):
            acc = acc + b_ref[t].astype(F32)
        o_ref[...] = acc

    return _call(body, name=name, out_shape=_sds((h, c), F32), grid=(h // tm,),
                 in_specs=[pl.BlockSpec((N_CHIP, tm, c), lambda i: (0, i, 0))],
                 out_specs=pl.BlockSpec((tm, c), lambda i: (i, 0)))(buf)


def _adamw_math(w, g, m, v):
    c1 = 1.0 - ADAM_B1 ** ADAM_STEP
    c2 = 1.0 - ADAM_B2 ** ADAM_STEP
    m2 = ADAM_B1 * m + (1.0 - ADAM_B1) * g
    v2 = ADAM_B2 * v + (1.0 - ADAM_B2) * (g * g)
    return -ADAM_LR * ((m2 / c1) / (jnp.sqrt(v2 / c2) + ADAM_EPS) + ADAM_WD * w), m2, v2


def adamw(w, g, row_off, m, v, name):
    _, r, c = w.shape
    tm = r if r < 8 else _row_tile(math.gcd(r, row_off) if row_off else r, 128, 8)

    def body(w_ref, g_ref, m_ref, v_ref, go_ref, d_ref, m2_ref, v2_ref):
        gv = g_ref[...]
        go_ref[...] = gv
        d_ref[...], m2_ref[...], v2_ref[...] = _adamw_math(w_ref[...], gv, m_ref[...], v_ref[...])

    blk = pl.BlockSpec((None, tm, c), lambda i: (0, i, 0))
    shp = _sds((1, r, c), F32)
    return _call(body, name=name, out_shape=[shp] * 4, grid=(r // tm,),
                 in_specs=[blk, pl.BlockSpec((tm, c), lambda i: (row_off // tm + i, 0)), blk, blk],
                 out_specs=[blk] * 4)(w, g, m, v)


def adamw_small(ws, gs, ms, vs):
    n = len(ws)

    def body(*refs):
        ins, outs = refs[:4 * n], refs[4 * n:]
        for k in range(n):
            w_ref, g_ref, m_ref, v_ref = (ins[j * n + k] for j in range(4))
            outs[k][...], outs[n + k][...], outs[2 * n + k][...] = _adamw_math(w_ref[...], g_ref[...], m_ref[...],
                                                                               v_ref[...])

    vmem = pl.BlockSpec(memory_space=pltpu.VMEM)
    shapes = [_sds(w.shape, F32) for w in ws] * 3
    res = pl.pallas_call(body, out_shape=shapes, in_specs=[vmem] * (4 * n), out_specs=[vmem] * (3 * n),
                         name="adamw_small")(*ws, *gs, *ms, *vs)
    return res[:n], res[n:2 * n], res[2 * n:]


BIG = ("ffn1_w_gate", "ffn1_w_up", "ffn1_w_down", "w_in", "w_attn_branch", "w_ssd_branch", "w_out",
       "ffn2_w_gate", "ffn2_w_up", "ffn2_w_down")
SMALL = ("ffn1_norm", "mix_norm", "q_norm", "k_norm", "conv_b", "dt_bias", "a_log", "d_skip", "ssd_norm", "ffn2_norm")
WEIGHTS = ("ffn1_norm", "ffn1_w_gate", "ffn1_w_up", "ffn1_w_down", "mix_norm", "w_in", "q_norm", "k_norm", "conv_w",
           "conv_b", "dt_bias", "a_log", "d_skip", "ssd_norm", "w_attn_branch", "w_ssd_branch", "w_out", "ffn2_norm",
           "ffn2_w_gate", "ffn2_w_up", "ffn2_w_down")
CONV_SHARD = SSD_CONV_DIM // N_CHIP
CLASSES = {
    "ffn1_in": (("ffn1_w_gate", 1024), ("ffn1_w_up", 1024)),
    "ffn1_out": (("ffn1_w_down", 704),),
    "mix_in": (("w_in", 1024),),
    "mix_attn": (("w_attn_branch", 512),),
    "late_out": (("ffn2_w_down", 704), ("w_ssd_branch", 512), ("w_out", 256)),
    "ffn2_in": (("ffn2_w_gate", 1024), ("ffn2_w_up", 1024)),
}
CLASS_TILE = {"ffn1_in": 256, "ffn1_out": 176, "mix_in": 128, "mix_attn": 256, "late_out": 368, "ffn2_in": 256}


def _pack_small(vals, conv_part, loss_part=None):
    flat = [vals[k].reshape(-1) for k in SMALL]
    flat.append(jnp.zeros((SSD_CONV * SSD_CONV_DIM,), F32) if conv_part is None else conv_part.reshape(-1))
    flat.append(jnp.zeros((1,), F32) if loss_part is None else loss_part.reshape(1))
    flat = jnp.concatenate(flat)
    return jnp.pad(flat, (0, SMALL_ROWS * D_MODEL - flat.shape[0])).reshape(SMALL_ROWS, D_MODEL)


def _unpack_small(pack, like):
    flat = pack.reshape(-1)
    out, off = {}, 0
    for k in SMALL:
        n = like[k].size
        out[k] = flat[off:off + n].reshape(like[k].shape)
        off += n
    conv = flat[off:off + SSD_CONV * SSD_CONV_DIM].reshape(SSD_CONV, SSD_CONV_DIM)
    return out, conv, flat[off + SSD_CONV * SSD_CONV_DIM]


def _chip_major_cols(a):
    r = a.shape[0]
    return a.reshape(r, N_CHIP, -1).transpose(1, 0, 2)


def _from_chip_major_cols(a):
    return a.transpose(1, 0, 2).reshape(a.shape[1], -1)


def kernel(x, ffn1_norm, ffn1_w_gate, ffn1_w_up, ffn1_w_down, mix_norm, w_in, q_norm, k_norm, conv_w, conv_b, dt_bias, a_log, d_skip, ssd_norm, w_attn_branch, w_ssd_branch, w_out, ffn2_norm, ffn2_w_gate, ffn2_w_up, ffn2_w_down, loss_target, m_ffn1_norm, m_ffn1_w_gate, m_ffn1_w_up, m_ffn1_w_down, m_mix_norm, m_w_in, m_q_norm, m_k_norm, m_conv_w, m_conv_b, m_dt_bias, m_a_log, m_d_skip, m_ssd_norm, m_w_attn_branch, m_w_ssd_branch, m_w_out, m_ffn2_norm, m_ffn2_w_gate, m_ffn2_w_up, m_ffn2_w_down, v_ffn1_norm, v_ffn1_w_gate, v_ffn1_w_up, v_ffn1_w_down, v_mix_norm, v_w_in, v_q_norm, v_k_norm, v_conv_w, v_conv_b, v_dt_bias, v_a_log, v_d_skip, v_ssd_norm, v_w_attn_branch, v_w_ssd_branch, v_w_out, v_ffn2_norm, v_ffn2_w_gate, v_ffn2_w_up, v_ffn2_w_down):
    env = dict(locals())
    wts = {k: env[k] for k in WEIGHTS}
    moms = {k: env["m_" + k] for k in WEIGHTS}
    vars_ = {k: env["v_" + k] for k in WEIGHTS}
    x0 = x[0]
    target = loss_target[0]

    def gather(classes):
        shards = [jnp.concatenate([wts[k][0] for k, _ in CLASSES[c]], axis=0).astype(BF16) for c in classes]
        return gather_rider(shards, [CLASS_TILE[c] for c in classes])

    def reducer(classes, parts):
        sums = [sibling_sum(p, CLASS_TILE[c], f"sibling_sum_{c}") for c, p in zip(classes, parts)]
        return owner_sum_rider(sums, [CLASS_TILE[c] for c in classes])

    (w_ffn1_in,) = run_alone(gather(["ffn1_in"]), "gather_ffn1_in")
    x1, saved1, (w_ffn1_out, w_mix_in, w_mix_attn) = ffn_forward(
        x0, ffn1_norm, w_ffn1_in, lambda rode: rode[0], "ffn1", rider=gather(["ffn1_out", "mix_in", "mix_attn"]))
    dt0, dt1 = IN_DT0 - 3 * IN_SHARD, IN_DT1 - 3 * IN_SHARD
    mixer_w = dict(
        mix_norm=mix_norm,
        w_in_main=jnp.concatenate([w_mix_in[0], w_mix_in[1], w_mix_in[2], w_mix_in[3][:, :dt0], w_mix_in[3][:, dt1:]],
                                  axis=1),
        w_in_dt=jnp.pad(w_mix_in[3][:, dt0:dt1], ((0, 0), (0, DT_PAD - SSD_HEADS))),
        q_gain=jnp.tile(q_norm, (1, 2)), k_gain=jnp.tile(k_norm, (1, 2)),
        conv_w=_from_chip_major_cols(gather_conv_w(conv_w[0])), conv_b=conv_b, dt_bias=dt_bias, a_log=a_log,
        d_skip=d_skip, ssd_norm=ssd_norm, w_attn_branch=_from_chip_major_cols(w_mix_attn))

    def later_weights(rode):
        late = rode[0]
        return dict(w_ssd_branch=late[:, 704:1216].reshape(SSD_INNER, D_MODEL),
                    w_out=late[:, 1216:1472].reshape(D_MODEL, D_MODEL))

    x2, saved_mix, (w_late_out, w_ffn2_in) = mixer_forward(x1, mixer_w, gather(["late_out", "ffn2_in"]), later_weights)
    x3, saved2, _ = ffn_forward(x2, ffn2_norm, w_ffn2_in, lambda rode: w_late_out, "ffn2")
    dx3, sq = loss_grad(x3, target, "loss")

    grads = {}
    dx2, grads["ffn2_norm"], d_ffn2_in, d_ffn2_down = ffn_backward(dx3, x2, ffn2_norm, w_ffn2_in, w_late_out, saved2,
                                                                   "ffn2")

    def ride_early(g):
        late = jnp.concatenate([d_ffn2_down, g["w_ssd_branch"].reshape(N_CHIP, -1, D_MODEL),
                                g["w_out"].reshape(N_CHIP, -1, D_MODEL)], axis=1)
        return reducer(["ffn2_in", "late_out"], [d_ffn2_in, late])

    def ride_late(g):
        main = g["w_in_main"]
        last = jnp.concatenate([main[:, 3 * IN_SHARD:IN_DT0], g["w_in_dt"][:, :SSD_HEADS], main[:, IN_DT0:]], axis=1)
        g_in = jnp.stack([main[:, j * IN_SHARD:(j + 1) * IN_SHARD] for j in range(3)] + [last])
        return reducer(["mix_in", "mix_attn"], [g_in, _chip_major_cols(g["w_attn_branch"])])

    dx1, gmix = mixer_backward(dx2, x1, saved_mix, ride_early, ride_late)
    dx0, grads["ffn1_norm"], rode_in, rode_out = ffn_backward(
        dx1, x0, ffn1_norm, w_ffn1_in, w_ffn1_out, saved1, "ffn1",
        ride_down=lambda d: reducer(["ffn1_out"], [d]), ride_in=lambda d: reducer(["ffn1_in"], [d]))
    for k in ("mix_norm", "q_norm", "k_norm", "conv_b", "dt_bias", "a_log", "d_skip", "ssd_norm"):
        grads[k] = gmix[k]
    reduced = dict(zip(("ffn2_in", "late_out", "mix_in", "mix_attn", "ffn1_in", "ffn1_out"),
                       (*gmix["rode_early"], *gmix["rode_late"], rode_in[0], rode_out[0])))
    reduced = {c: r.reshape(-1, r.shape[2]) for c, r in reduced.items()}
    summed = all_reduce_small([grads[k] for k in SMALL]
                              + [gmix["conv_w"], (0.5 * jnp.sum(sq) / D_MODEL).reshape(1, 1)])
    g_small = dict(zip(SMALL, summed))
    loss = summed[-1].reshape(())
    chip = 2 * lax.axis_index("x") + lax.axis_index("y")
    g_conv = lax.dynamic_slice_in_dim(summed[-2], chip * CONV_SHARD, CONV_SHARD, axis=1)

    g_final, delta, new_m, new_v = dict(g_small), {}, {}, {}

    def update(k, g_arr, row_off):
        w, m, v = wts[k], moms[k], vars_[k]
        rows, cols = w.shape[1:]
        if cols % 128:
            res = adamw(jnp.swapaxes(w, 1, 2), g_arr[row_off:row_off + rows].T, 0, jnp.swapaxes(m, 1, 2),
                        jnp.swapaxes(v, 1, 2), f"adamw_{k}")
            res = [jnp.swapaxes(r, 1, 2) for r in res]
        else:
            res = adamw(w, g_arr, row_off, m, v, f"adamw_{k}")
        g_final[k], delta[k], new_m[k], new_v[k] = res

    for cls, members in CLASSES.items():
        off = 0
        for k, rows in members:
            update(k, reduced[cls], off)
            off += rows
    update("conv_w", g_conv, 0)
    small = adamw_small(*([d[k] for k in SMALL] for d in (wts, g_small, moms, vars_)))
    for res, vals in zip((delta, new_m, new_v), small):
        res.update(zip(SMALL, vals))

    return (loss, dx0[None], *[g_final[k] for k in WEIGHTS], *[delta[k] for k in WEIGHTS],
            *[new_m[k] for k in WEIGHTS], *[new_v[k] for k in WEIGHTS])
```

```python
import collections
import functools
import math

import jax
import jax.numpy as jnp
from jax import lax
from jax.experimental import pallas as pl
from jax.experimental.pallas import tpu as pltpu

F32 = jnp.float32
BF16 = jnp.bfloat16
MESH = pl.DeviceIdType.MESH

EPS = 1e-6
D_MODEL = 1024
D_FF = 2816
N_CHIP = 4
FF_SHARD = D_FF // N_CHIP
HD = 64
BLK = 128
ATTN_DILATIONS = (1, 4, 16)
HEADS_PER_PATTERN = 8
N_ATTN_HEADS = 24
ALIBI_MAX_EXP = 8.0
ATTN_QKV = 1536
GROUP_W = 512
SSD_INNER = 2048
SSD_HEADS = 32
SSD_GROUPS = 4
SSD_CONV = 4
SSD_CONV_DIM = 3072
IN_COLS = 11808
IN_DT0, IN_DT1 = 9728, 9760
IN_SHARD = IN_COLS // 4
COL_K, COL_V, COL_Z, COL_XBC, COL_GA, COL_GS, P_COLS = 1536, 3072, 4608, 6656, 9728, 10752, 11776
DT_PAD = 128

ADAM_LR, ADAM_B1, ADAM_B2, ADAM_EPS, ADAM_WD, ADAM_STEP = 0.001, 0.9, 0.999, 1e-08, 0.01, 10

V7X_VMEM_LIMIT = 56 * 1024 * 1024
NEG = -1e30


Rider = collections.namedtuple("Rider", "arrays out_shape scratch start finish")
Rider.__doc__ = """An exchange between devices that rides in a compute kernel: its copies are started in the host's
first grid step and waited for in its last, so they travel while the host computes.  arrays / out_shape: extra HBM
operands and results; scratch: extra scratch; start, finish: f(in_refs, out_refs, scratch_refs)."""


def _call(body, *, name, out_shape, in_specs, out_specs, grid=(), scratch_shapes=(), aliases=None, rider=None):
    params = dict(dimension_semantics=("arbitrary",) * len(grid), vmem_limit_bytes=V7X_VMEM_LIMIT)
    if rider is None:
        return pl.pallas_call(
            body, out_shape=out_shape, grid=grid, in_specs=in_specs, out_specs=out_specs,
            scratch_shapes=scratch_shapes, input_output_aliases=aliases or {}, name=name,
            compiler_params=pltpu.CompilerParams(**params))
    single = not isinstance(out_shape, (list, tuple))
    main_out = [out_shape] if single else list(out_shape)
    main_specs = [out_specs] if single else list(out_specs)
    n_in, n_out, n_scr = len(in_specs), len(main_out), len(scratch_shapes)
    r_in, r_out = len(rider.arrays), len(rider.out_shape)

    def wrapped(*refs):
        ins, refs = refs[:n_in], refs[n_in:]
        r_ins, refs = refs[:r_in], refs[r_in:]
        outs, refs = refs[:n_out], refs[n_out:]
        r_outs, refs = refs[:r_out], refs[r_out:]
        scr, r_scr = refs[:n_scr], refs[n_scr:]
        first = last = None
        for axis, size in enumerate(grid):
            at_start, at_end = pl.program_id(axis) == 0, pl.program_id(axis) == size - 1
            first = at_start if first is None else jnp.logical_and(first, at_start)
            last = at_end if last is None else jnp.logical_and(last, at_end)

        @pl.when(first)
        def _():
            rider.start(r_ins, r_outs, r_scr)

        body(*ins, *outs, *scr)

        @pl.when(last)
        def _():
            rider.finish(r_ins, r_outs, r_scr)

    hbm = pl.BlockSpec(memory_space=pl.ANY)
    call = pl.pallas_call(
        wrapped, out_shape=main_out + list(rider.out_shape), grid=grid, in_specs=list(in_specs) + [hbm] * r_in,
        out_specs=main_specs + [hbm] * r_out, scratch_shapes=list(scratch_shapes) + list(rider.scratch), name=name,
        compiler_params=pltpu.CompilerParams(has_side_effects=True, **params))

    def run(*args):
        res = call(*args, *rider.arrays)
        main = res[:n_out]
        return (main[0] if single else main), res[n_out:]

    return run


def _sds(shape, dtype):
    return jax.ShapeDtypeStruct(tuple(shape), dtype)


def _dot(a, b):
    return jnp.dot(a, b, preferred_element_type=F32)


def _dot_nt(a, b):
    return lax.dot_general(a, b, (((1,), (1,)), ((), ())), preferred_element_type=F32)


def _dot_tn(a, b):
    return lax.dot_general(a, b, (((0,), (0,)), ((), ())), preferred_element_type=F32)


def _dot_hi(a, b):
    return jnp.dot(a, b, preferred_element_type=F32, precision=lax.Precision.HIGHEST)


def _sigmoid(x):
    return 1.0 / (1.0 + jnp.exp(-x))


def _lane_first_half(shape):
    return lax.broadcasted_iota(jnp.int32, shape, len(shape) - 1) < HD


def _pair_sum(x, first):
    s_all = jnp.sum(x, axis=-1, keepdims=True)
    s_a = jnp.sum(jnp.where(first, x, 0.0), axis=-1, keepdims=True)
    return s_a, s_all - s_a


def _rowwise(name, fn, rows, consts, outs, accs=(), tm=512):
    n_rows = None
    in_arrays, in_specs = [], []
    for r in rows:
        if isinstance(r, tuple):
            arr, w, cb = r
            spec = pl.BlockSpec((tm, w), functools.partial(lambda i, cb: (i, cb), cb=cb))
        else:
            arr = r
            spec = pl.BlockSpec((tm, arr.shape[1]), lambda i: (i, 0))
        n_rows = arr.shape[0]
        in_arrays.append(arr)
        in_specs.append(spec)
    for c in consts:
        in_arrays.append(c)
        in_specs.append(pl.BlockSpec(c.shape, functools.partial(lambda i, n: (0,) * n, n=c.ndim)))
    out_shape = [_sds(s, d) for s, d in outs] + [_sds(s, d) for s, d in accs]
    out_specs = [pl.BlockSpec((tm, s[1]), lambda i: (i, 0)) for s, _ in outs]
    out_specs += [pl.BlockSpec(s, functools.partial(lambda i, n: (0,) * n, n=len(s))) for s, _ in accs]

    def body(*refs):
        fn(pl.program_id(0), *refs)

    res = _call(body, name=name, out_shape=out_shape, grid=(n_rows // tm,), in_specs=in_specs,
                out_specs=out_specs)(*in_arrays)
    return res


def rms_fwd(x, gain, name):
    def fn(i, x_ref, g_ref, h_ref):
        xv = x_ref[...]
        r = lax.rsqrt(jnp.mean(xv * xv, axis=-1, keepdims=True) + EPS)
        h_ref[...] = (xv * r * g_ref[...]).astype(h_ref.dtype)

    return _rowwise(name, fn, [x], [gain], [(x.shape, BF16)])[0]


def rms_bwd(dhs, x, gain, dx_in, name):
    n = len(dhs)

    def fn(i, *refs):
        dh_refs, (x_ref, dxin_ref, g_ref, dx_ref, dg_ref) = refs[:n], refs[n:]
        dh = dh_refs[0][...]
        for r in dh_refs[1:]:
            dh = dh + r[...]
        xv = x_ref[...]
        r = lax.rsqrt(jnp.mean(xv * xv, axis=-1, keepdims=True) + EPS)
        xn = xv * r
        dxn = dh * g_ref[...]
        dx_ref[...] = dxin_ref[...] + r * (dxn - xn * jnp.mean(dxn * xn, axis=-1, keepdims=True))

        @pl.when(i == 0)
        def _():
            dg_ref[...] = jnp.zeros_like(dg_ref)

        dg_ref[...] += jnp.sum(dh * xn, axis=0, keepdims=True)

    return _rowwise(name, fn, list(dhs) + [x, dx_in], [gain], [(x.shape, F32)], [((1, x.shape[1]), F32)])


def loss_grad(y, target, name):
    def fn(i, y_ref, t_ref, dy_ref, sq_ref):
        err = y_ref[...] - t_ref[...]
        dy_ref[...] = err * (1.0 / y_ref.shape[1])

        @pl.when(i == 0)
        def _():
            sq_ref[...] = jnp.zeros_like(sq_ref)

        sq_ref[...] += jnp.sum(err * err, axis=0, keepdims=True)

    return _rowwise(name, fn, [y, target], [], [(y.shape, F32)], [((1, y.shape[1]), F32)])


def matmul_nn(a, b, name, out_dtype, tm, tn, res=None, scale=1.0, rider=None):
    s, k = a.shape
    n = b.shape[1]

    def body(*refs):
        if res is None:
            a_ref, b_ref, o_ref = refs
            o_ref[...] = _dot(a_ref[...], b_ref[...]).astype(o_ref.dtype)
        else:
            a_ref, b_ref, r_ref, o_ref = refs
            o_ref[...] = (r_ref[...] + scale * _dot(a_ref[...], b_ref[...])).astype(o_ref.dtype)

    in_specs = [pl.BlockSpec((tm, k), lambda i, j: (i, 0)), pl.BlockSpec((k, tn), lambda i, j: (0, j))]
    args = [a, b]
    if res is not None:
        in_specs.append(pl.BlockSpec((tm, tn), lambda i, j: (i, j)))
        args.append(res)
    return _call(body, name=name, out_shape=_sds((s, n), out_dtype), grid=(s // tm, n // tn), in_specs=in_specs,
                 out_specs=pl.BlockSpec((tm, tn), lambda i, j: (i, j)), rider=rider)(*args)


def matmul_nt(a, b, name, out_dtype, tm, tn, tk, rider=None):
    s, k = a.shape
    n = b.shape[0]
    nk = k // tk

    def body(a_ref, b_ref, o_ref, acc_ref):
        kk = pl.program_id(2)

        @pl.when(kk == 0)
        def _():
            acc_ref[...] = jnp.zeros_like(acc_ref)

        acc_ref[...] += _dot_nt(a_ref[...].astype(BF16), b_ref[...])

        @pl.when(kk == nk - 1)
        def _():
            o_ref[...] = acc_ref[...].astype(o_ref.dtype)

    return _call(body, name=name, out_shape=_sds((s, n), out_dtype), grid=(s // tm, n // tn, nk),
                 in_specs=[pl.BlockSpec((tm, tk), lambda i, j, kk: (i, kk)),
                           pl.BlockSpec((tn, tk), lambda i, j, kk: (j, kk))],
                 out_specs=pl.BlockSpec((tm, tn), lambda i, j, kk: (i, j)),
                 scratch_shapes=[pltpu.VMEM((tm, tn), F32)], rider=rider)(a, b)


def matmul_tn(a, b, name, tn, ts, a_scale=None, b_scale=None, rider=None):
    s, m = a.shape
    n = b.shape[1]
    ns = s // ts

    def body(a_ref, b_ref, o_ref, acc_ref):
        ss = pl.program_id(1)

        @pl.when(ss == 0)
        def _():
            acc_ref[...] = jnp.zeros_like(acc_ref)

        av, bv = a_ref[...], b_ref[...]
        if a_scale is not None:
            av = av * a_scale
        if b_scale is not None:
            bv = bv * b_scale
        acc_ref[...] += _dot_tn(av.astype(BF16), bv.astype(BF16))

        @pl.when(ss == ns - 1)
        def _():
            o_ref[...] = acc_ref[...].astype(o_ref.dtype)

    return _call(body, name=name, out_shape=_sds((m, n), BF16), grid=(n // tn, ns),
                 in_specs=[pl.BlockSpec((ts, m), lambda j, ss: (ss, 0)), pl.BlockSpec((ts, tn), lambda j, ss: (ss, j))],
                 out_specs=pl.BlockSpec((m, tn), lambda j, ss: (0, j)),
                 scratch_shapes=[pltpu.VMEM((m, tn), F32)], rider=rider)(a, b)


def _piece_specs(pieces, tile, rows_tile, tile_axis_first):
    specs, ranges, t0 = [], [], 0
    for a in pieces:
        n = a.shape[1] // tile

        def index(*ids, t0=t0, n=n):
            t, r = (ids[0], ids[1]) if tile_axis_first else (ids[2], ids[0])
            on = jnp.logical_and(t >= t0, t < t0 + n)
            return jnp.where(on, r, 0), jnp.clip(t - t0, 0, n - 1)

        specs.append(pl.BlockSpec((rows_tile, tile), index))
        ranges.append((t0, n))
        t0 += n
    return specs, ranges


def matmul_tn_pieces(a, pieces, name, tn, ts, rider=None):
    s, m = a.shape
    ns = s // ts
    specs, ranges = _piece_specs(pieces, tn, ts, True)
    n_total = sum(n for _, n in ranges)

    def body(a_ref, *refs):
        b_refs, o_ref, acc_ref = refs[:len(pieces)], refs[-2], refs[-1]
        j, ss = pl.program_id(0), pl.program_id(1)

        @pl.when(ss == 0)
        def _():
            acc_ref[...] = jnp.zeros_like(acc_ref)

        for b_ref, (t0, n) in zip(b_refs, ranges):
            @pl.when(jnp.logical_and(j >= t0, j < t0 + n))
            def _(b_ref=b_ref):
                acc_ref[...] += _dot_tn(a_ref[...], b_ref[...])

        @pl.when(ss == ns - 1)
        def _():
            o_ref[...] = acc_ref[...].astype(o_ref.dtype)

    return _call(body, name=name, out_shape=_sds((m, n_total * tn), BF16), grid=(n_total, ns),
                 in_specs=[pl.BlockSpec((ts, m), lambda j, ss: (ss, 0))] + specs,
                 out_specs=pl.BlockSpec((m, tn), lambda j, ss: (0, j)),
                 scratch_shapes=[pltpu.VMEM((m, tn), F32)], rider=rider)(a, *pieces)


def matmul_nt_pieces(pieces, b, name, out_dtype, tm, tn, tk, rider=None):
    s = pieces[0].shape[0]
    n = b.shape[0]
    specs, ranges = _piece_specs(pieces, tk, tm, False)
    nk = sum(cnt for _, cnt in ranges)

    def body(*refs):
        a_refs, b_ref, o_ref, acc_ref = refs[:len(pieces)], refs[-3], refs[-2], refs[-1]
        kk = pl.program_id(2)

        @pl.when(kk == 0)
        def _():
            acc_ref[...] = jnp.zeros_like(acc_ref)

        for a_ref, (t0, cnt) in zip(a_refs, ranges):
            @pl.when(jnp.logical_and(kk >= t0, kk < t0 + cnt))
            def _(a_ref=a_ref):
                acc_ref[...] += _dot_nt(a_ref[...], b_ref[...])

        @pl.when(kk == nk - 1)
        def _():
            o_ref[...] = acc_ref[...].astype(o_ref.dtype)

    return _call(body, name=name, out_shape=_sds((s, n), out_dtype), grid=(s // tm, n // tn, nk),
                 in_specs=specs + [pl.BlockSpec((tn, tk), lambda i, j, kk: (j, kk))],
                 out_specs=pl.BlockSpec((tm, tn), lambda i, j, kk: (i, j)),
                 scratch_shapes=[pltpu.VMEM((tm, tn), F32)], rider=rider)(*pieces, b)


def ffn_up(h, w704, gate_blk, up_blk, name, tm=512, rider=None):
    s = h.shape[0]

    def body(h_ref, wg_ref, wu_ref, g_ref, u_ref, a_ref):
        hv = h_ref[...]
        g = _dot(hv, wg_ref[...])
        u = _dot(hv, wu_ref[...])
        g_ref[...] = g.astype(BF16)
        u_ref[...] = u.astype(BF16)
        a_ref[...] = (g * _sigmoid(g) * u).astype(BF16)

    ospec = pl.BlockSpec((None, tm, FF_SHARD), lambda j, i: (j, i, 0))
    shp = _sds((N_CHIP, s, FF_SHARD), BF16)
    return _call(body, name=name, out_shape=[shp, shp, shp], grid=(N_CHIP, s // tm),
                 in_specs=[pl.BlockSpec((tm, D_MODEL), lambda j, i: (i, 0)),
                           pl.BlockSpec((None, D_MODEL, FF_SHARD), lambda j, i: (j, gate_blk, 0)),
                           pl.BlockSpec((None, D_MODEL, FF_SHARD), lambda j, i: (j, up_blk, 0))],
                 out_specs=[ospec, ospec, ospec], rider=rider)(h, w704, w704)


def ffn_down(a, w1024, blk, x, name, tm=512):
    s = x.shape[0]

    def body(a_ref, wd_ref, x_ref, o_ref):
        acc = _dot(a_ref[0], wd_ref[0])
        for j in range(1, N_CHIP):
            acc += _dot(a_ref[j], wd_ref[j])
        o_ref[...] = x_ref[...] + 0.5 * acc

    return _call(body, name=name, out_shape=_sds((s, D_MODEL), F32), grid=(s // tm,),
                 in_specs=[pl.BlockSpec((N_CHIP, tm, FF_SHARD), lambda i: (0, i, 0)),
                           pl.BlockSpec((N_CHIP, FF_SHARD, D_MODEL), lambda i: (0, blk, 0)),
                           pl.BlockSpec((tm, D_MODEL), lambda i: (i, 0))],
                 out_specs=pl.BlockSpec((tm, D_MODEL), lambda i: (i, 0)))(a, w1024, x)


def ffn_bwd_hidden(dx, w1024, blk, g, u, name, tm=1024, rider=None):
    s = dx.shape[0]

    def body(dx_ref, wd_ref, g_ref, u_ref, dg_ref, du_ref):
        dy = (0.5 * dx_ref[...]).astype(BF16)
        da = _dot_nt(dy, wd_ref[...])
        gv = g_ref[...].astype(F32)
        uv = u_ref[...].astype(F32)
        sg = _sigmoid(gv)
        dg_ref[...] = (da * uv * (sg * (1.0 + gv * (1.0 - sg)))).astype(BF16)
        du_ref[...] = (da * gv * sg).astype(BF16)

    hspec = pl.BlockSpec((None, tm, FF_SHARD), lambda j, i: (j, i, 0))
    shp = _sds((N_CHIP, s, FF_SHARD), BF16)
    return _call(body, name=name, out_shape=[shp, shp], grid=(N_CHIP, s // tm),
                 in_specs=[pl.BlockSpec((tm, D_MODEL), lambda j, i: (i, 0)),
                           pl.BlockSpec((None, FF_SHARD, D_MODEL), lambda j, i: (j, blk, 0)), hspec, hspec],
                 out_specs=[hspec, hspec], rider=rider)(dx, w1024, g, u)


def ffn_bwd_input(dg, du, w704, gate_blk, up_blk, name, tm=512, rider=None):
    s = dg.shape[1]

    def body(dg_ref, du_ref, wg_ref, wu_ref, o_ref):
        acc = _dot_nt(dg_ref[0], wg_ref[0]) + _dot_nt(du_ref[0], wu_ref[0])
        for j in range(1, N_CHIP):
            acc += _dot_nt(dg_ref[j], wg_ref[j]) + _dot_nt(du_ref[j], wu_ref[j])
        o_ref[...] = acc

    hspec = pl.BlockSpec((N_CHIP, tm, FF_SHARD), lambda i: (0, i, 0))
    return _call(body, name=name, out_shape=_sds((s, D_MODEL), F32), grid=(s // tm,),
                 in_specs=[hspec, hspec,
                           pl.BlockSpec((N_CHIP, D_MODEL, FF_SHARD), lambda i: (0, gate_blk, 0), pl.Buffered(1)),
                           pl.BlockSpec((N_CHIP, D_MODEL, FF_SHARD), lambda i: (0, up_blk, 0), pl.Buffered(1))],
                 out_specs=pl.BlockSpec((tm, D_MODEL), lambda i: (i, 0)), rider=rider)(dg, du, w704, w704)


def ffn_wgrad_in(h, dgu, name, ts=1024):
    s = h.shape[0]
    ns = s // ts

    def body(h_ref, d_ref, o_ref, acc_ref):
        ss = pl.program_id(1)

        @pl.when(ss == 0)
        def _():
            acc_ref[...] = jnp.zeros_like(acc_ref)

        acc_ref[...] += _dot_tn(h_ref[...], d_ref[...])

        @pl.when(ss == ns - 1)
        def _():
            o_ref[...] = acc_ref[...].astype(BF16)

    return _call(body, name=name, out_shape=_sds((N_CHIP, D_MODEL, FF_SHARD), BF16), grid=(N_CHIP, ns),
                 in_specs=[pl.BlockSpec((ts, D_MODEL), lambda j, ss: (ss, 0)),
                           pl.BlockSpec((None, ts, FF_SHARD), lambda j, ss: (j, ss, 0))],
                 out_specs=pl.BlockSpec((None, D_MODEL, FF_SHARD), lambda j, ss: (j, 0, 0)),
                 scratch_shapes=[pltpu.VMEM((D_MODEL, FF_SHARD), F32)])(h, dgu)


def ffn_wgrad_down(a, dx, name, ts=1024):
    s = dx.shape[0]
    ns = s // ts

    def body(a_ref, dx_ref, o_ref, acc_ref):
        ss = pl.program_id(1)

        @pl.when(ss == 0)
        def _():
            acc_ref[...] = jnp.zeros_like(acc_ref)

        acc_ref[...] += _dot_tn(a_ref[...], (0.5 * dx_ref[...]).astype(BF16))

        @pl.when(ss == ns - 1)
        def _():
            o_ref[...] = acc_ref[...].astype(BF16)

    return _call(body, name=name, out_shape=_sds((N_CHIP, FF_SHARD, D_MODEL), BF16), grid=(N_CHIP, ns),
                 in_specs=[pl.BlockSpec((None, ts, FF_SHARD), lambda j, ss: (j, ss, 0)),
                           pl.BlockSpec((ts, D_MODEL), lambda j, ss: (ss, 0))],
                 out_specs=pl.BlockSpec((None, FF_SHARD, D_MODEL), lambda j, ss: (j, 0, 0)),
                 scratch_shapes=[pltpu.VMEM((FF_SHARD, D_MODEL), F32)])(a, dx)


def ffn_forward(x, gain, w704, get_w1024, tag, rider=None):
    h = rms_fwd(x, gain, f"{tag}_rms")
    res = ffn_up(h, w704, 0, 1, f"{tag}_up", rider=rider)
    (g, u, a), rode = res if rider is not None else (res, None)
    y = ffn_down(a, get_w1024(rode), 0, x, f"{tag}_down")
    return y, (h, g, u, a), rode


def ffn_backward(dy, x, gain, w704, w1024, saved, tag, ride_down=None, ride_in=None):
    h, g, u, a = saved
    d_wd = ffn_wgrad_down(a, dy, f"{tag}_dwd")
    if ride_down is not None:
        (dg, du), d_wd = ffn_bwd_hidden(dy, w1024, 0, g, u, f"{tag}_dhid", rider=ride_down(d_wd))
    else:
        dg, du = ffn_bwd_hidden(dy, w1024, 0, g, u, f"{tag}_dhid")
    d_win = jnp.concatenate([ffn_wgrad_in(h, dg, f"{tag}_dwg"), ffn_wgrad_in(h, du, f"{tag}_dwu")], axis=1)
    if ride_in is not None:
        dh, d_win = ffn_bwd_input(dg, du, w704, 0, 1, f"{tag}_dh", rider=ride_in(d_win))
    else:
        dh = ffn_bwd_input(dg, du, w704, 0, 1, f"{tag}_dh")
    dx, d_gain = rms_bwd([dh], x, gain, dy, f"{tag}_drms")
    return dx, d_gain, d_win, d_wd


def _alibi_slope(head):
    return float(2.0 ** (-ALIBI_MAX_EXP * (head + 1) / N_ATTN_HEADS))


def _head_norm(t, gain_pair, first):
    sa, sb = _pair_sum(t * t, first)
    r = jnp.where(first, lax.rsqrt(sa * (1.0 / HD) + EPS), lax.rsqrt(sb * (1.0 / HD) + EPS))
    return t * r * gain_pair, r


def qk_norm_fwd(p, q_gain, k_gain, name):
    s = p.shape[0]

    def fn(i, q_ref, k_ref, qg_ref, kg_ref, qn_ref, kn_ref):
        first = _lane_first_half((q_ref.shape[0], 2 * HD))
        for src, g_ref, dst in ((q_ref, qg_ref, qn_ref), (k_ref, kg_ref, kn_ref)):
            for pr in range(ATTN_QKV // (2 * HD)):
                cols = slice(pr * 2 * HD, (pr + 1) * 2 * HD)
                y, _ = _head_norm(src[:, cols].astype(F32), g_ref[...], first)
                dst[:, cols] = y.astype(BF16)

    return _rowwise(name, fn, [(p, ATTN_QKV, 0), (p, ATTN_QKV, 1)], [q_gain, k_gain],
                    [((s, ATTN_QKV), BF16), ((s, ATTN_QKV), BF16)])


def qk_norm_bwd(p, dqs, dks, q_gain, k_gain, name):
    s = p.shape[0]
    pairs_per_pattern = GROUP_W // (2 * HD)

    def fn(i, q_ref, k_ref, dq0, dq1, dq2, dk0, dk1, dk2, qg_ref, kg_ref, dqk_ref, dqg_ref, dkg_ref):
        first = _lane_first_half((q_ref.shape[0], 2 * HD))

        @pl.when(i == 0)
        def _():
            dqg_ref[...] = jnp.zeros_like(dqg_ref)
            dkg_ref[...] = jnp.zeros_like(dkg_ref)

        for src, d_refs, g_ref, dst, dg_ref in (
                (q_ref, (dq0, dq1, dq2), qg_ref, dqk_ref.at[:, 0:ATTN_QKV], dqg_ref),
                (k_ref, (dk0, dk1, dk2), kg_ref, dqk_ref.at[:, ATTN_QKV:2 * ATTN_QKV], dkg_ref)):
            for pr in range(ATTN_QKV // (2 * HD)):
                cols = slice(pr * 2 * HD, (pr + 1) * 2 * HD)
                t = src[:, cols].astype(F32)
                sa, sb = _pair_sum(t * t, first)
                r = jnp.where(first, lax.rsqrt(sa * (1.0 / HD) + EPS), lax.rsqrt(sb * (1.0 / HD) + EPS))
                xn = t * r
                within = (pr % pairs_per_pattern) * 2 * HD
                dy = d_refs[pr // pairs_per_pattern][:, within:within + 2 * HD]
                dg_ref[:, cols] += jnp.sum(dy * xn, axis=0, keepdims=True)
                dxn = dy * g_ref[...]
                ma, mb = _pair_sum(dxn * xn, first)
                mean = jnp.where(first, ma, mb) * (1.0 / HD)
                dst[:, cols] = (r * (dxn - xn * mean)).astype(BF16)

    return _rowwise(name, fn, [(p, ATTN_QKV, 0), (p, ATTN_QKV, 1)] + list(dqs) + list(dks), [q_gain, k_gain],
                    [((s, 2 * ATTN_QKV), BF16)], [((1, ATTN_QKV), F32), ((1, ATTN_QKV), F32)])


def _to_streams(a, d):
    if d == 1:
        return a
    s, c = a.shape
    return a.reshape(s // d, d, c).transpose(1, 0, 2).reshape(s, c)


def _from_streams(a, d):
    if d == 1:
        return a
    s, c = a.shape
    return a.reshape(d, s // d, c).transpose(1, 0, 2).reshape(s, c)


def _attn_masks():
    row = lax.broadcasted_iota(jnp.int32, (BLK, BLK), 0)
    col = lax.broadcasted_iota(jnp.int32, (BLK, BLK), 1)
    rel_diag = row - col
    rel_prev = rel_diag + BLK
    return rel_diag, rel_prev


def attn_fwd(q, k, v, pattern, name, tq=512):
    s = q.shape[0]
    d = ATTN_DILATIONS[pattern]
    blocks_per_stream = (s // d) // BLK
    nsb = tq // BLK

    def body(q_ref, k_ref, v_ref, kp_ref, vp_ref, o_ref, l_ref):
        i = pl.program_id(0)
        rel_diag, rel_prev = _attn_masks()
        first = _lane_first_half((BLK, 2 * HD))
        rd_f = (rel_diag * d).astype(F32)
        rp_f = (rel_prev * d).astype(F32)
        for sb in range(nsb):
            rows = slice(sb * BLK, (sb + 1) * BLK)
            has_prev = ((i * nsb + sb) % blocks_per_stream != 0).astype(jnp.int32)
            m_diag = rel_diag >= 0
            m_prev = (rel_prev + (1 - has_prev) * (4 * BLK)) <= BLK
            for pr in range(GROUP_W // (2 * HD)):
                cols = slice(pr * 2 * HD, (pr + 1) * 2 * HD)
                qp = q_ref[rows, cols]
                kc, vc = k_ref[rows, cols], v_ref[rows, cols]
                if sb == 0:
                    kp, vp = kp_ref[:, cols], vp_ref[:, cols]
                else:
                    prows = slice((sb - 1) * BLK, sb * BLK)
                    kp, vp = k_ref[prows, cols], v_ref[prows, cols]
                outs, lses = [], []
                for e in range(2):
                    slope = _alibi_slope(pattern * HEADS_PER_PATTERN + 2 * pr + e)
                    qm = jnp.where(first if e == 0 else jnp.logical_not(first), qp, jnp.zeros_like(qp))
                    s1 = jnp.where(m_diag, _dot_nt(qm, kc) * 0.125 - slope * rd_f, NEG)
                    s0 = jnp.where(m_prev, _dot_nt(qm, kp) * 0.125 - slope * rp_f, NEG)
                    m = jnp.maximum(jnp.max(s1, axis=-1, keepdims=True), jnp.max(s0, axis=-1, keepdims=True))
                    p1 = jnp.exp(s1 - m)
                    p0 = jnp.exp(s0 - m)
                    l = jnp.sum(p1, axis=-1, keepdims=True) + jnp.sum(p0, axis=-1, keepdims=True)
                    inv = 1.0 / l
                    outs.append(_dot((p1 * inv).astype(BF16), vc) + _dot((p0 * inv).astype(BF16), vp))
                    lses.append(m + jnp.log(l))
                o_ref[rows, cols] = jnp.where(first, outs[0], outs[1])
                l_ref[rows, cols] = jnp.where(first, lses[0], lses[1])

    cur = pl.BlockSpec((tq, GROUP_W), lambda i: (i, 0))
    prev = pl.BlockSpec((BLK, GROUP_W), lambda i: (jnp.maximum(i * nsb - 1, 0), 0))
    return _call(body, name=name, out_shape=[_sds((s, GROUP_W), F32), _sds((s, GROUP_W), F32)], grid=(s // tq,),
                 in_specs=[cur, cur, cur, prev, prev], out_specs=[cur, cur])(q, k, v, k, v)


def attn_merge_fwd(os_, lses, name):
    s = os_[0].shape[0]

    def fn(i, o0, o1, o2, l0, l1, l2, out_ref):
        m = jnp.maximum(jnp.maximum(l0[...], l1[...]), l2[...])
        e0, e1, e2 = jnp.exp(l0[...] - m), jnp.exp(l1[...] - m), jnp.exp(l2[...] - m)
        inv = 1.0 / (e0 + e1 + e2)
        out_ref[...] = ((e0 * inv) * o0[...] + (e1 * inv) * o1[...] + (e2 * inv) * o2[...]).astype(BF16)

    return _rowwise(name, fn, list(os_) + list(lses), [], [((s, GROUP_W), BF16)])[0]


def attn_merge_bwd(d_out, os_, lses, name):
    s = d_out.shape[0]

    def fn(i, do_ref, o0, o1, o2, l0, l1, l2, d0, d1, d2, c0, c1, c2):
        first = _lane_first_half((do_ref.shape[0], 2 * HD))
        m = jnp.maximum(jnp.maximum(l0[...], l1[...]), l2[...])
        e0, e1, e2 = jnp.exp(l0[...] - m), jnp.exp(l1[...] - m), jnp.exp(l2[...] - m)
        inv = 1.0 / (e0 + e1 + e2)
        w0, w1, w2 = e0 * inv, e1 * inv, e2 * inv
        do = do_ref[...]
        prod = do * (w0 * o0[...] + w1 * o1[...] + w2 * o2[...])
        for pr in range(GROUP_W // (2 * HD)):
            cols = slice(pr * 2 * HD, (pr + 1) * 2 * HD)
            ta, tb = _pair_sum(prod[:, cols], first)
            t = jnp.where(first, ta, tb)
            for w, c_ref in ((w0, c0), (w1, c1), (w2, c2)):
                c_ref[:, cols] = w[:, cols] * t
        for w, d_ref in ((w0, d0), (w1, d1), (w2, d2)):
            d_ref[...] = (w * do).astype(BF16)

    shp = (s, GROUP_W)
    return _rowwise(name, fn, [d_out] + list(os_) + list(lses), [],
                    [(shp, BF16)] * 3 + [(shp, F32)] * 3)


def attn_bwd(q, k, v, d_o, cterm, lse, pattern, name, tq=512):
    s = q.shape[0]
    d = ATTN_DILATIONS[pattern]
    blocks_per_stream = (s // d) // BLK
    nsb = tq // BLK
    n_blocks = s // BLK

    def body(q_ref, k_ref, v_ref, do_ref, c_ref, l_ref, kp_ref, vp_ref, qn_ref, don_ref, cn_ref, ln_ref,
             dq_ref, dk_ref, dv_ref):
        i = pl.program_id(0)
        rel_diag, rel_prev = _attn_masks()
        first = _lane_first_half((BLK, 2 * HD))
        second = jnp.logical_not(first)
        rd_f = (rel_diag * d).astype(F32)
        rp_f = (rel_prev * d).astype(F32)
        m_diag = rel_diag >= 0
        dq_ref[...] = jnp.zeros_like(dq_ref)
        dk_ref[...] = jnp.zeros_like(dk_ref)
        dv_ref[...] = jnp.zeros_like(dv_ref)

        def pair(qp, dop, cp, lp, kp, vp, rel_f, mask):
            dq = dk = dv = None
            for e in range(2):
                lanes = first if e == 0 else second
                slope = slopes[e]
                qm = jnp.where(lanes, qp, jnp.zeros_like(qp))
                dom = jnp.where(lanes, dop, jnp.zeros_like(dop))
                km = jnp.where(lanes, kp, jnp.zeros_like(kp))
                sc = jnp.where(mask, _dot_nt(qm, kp) * 0.125 - slope * rel_f, NEG)
                pm = jnp.exp(sc - lp[:, e * HD:e * HD + 1])
                dl = pm * (_dot_nt(dom, vp) - cp[:, e * HD:e * HD + 1])
                dl16 = dl.astype(BF16)
                t_dq = _dot(dl16, km)
                t_dk = _dot_tn(dl16, qm)
                t_dv = _dot_tn(pm.astype(BF16), dom)
                dq = t_dq if dq is None else dq + t_dq
                dk = t_dk if dk is None else dk + t_dk
                dv = t_dv if dv is None else dv + t_dv
            return dq * 0.125, dk * 0.125, dv

        for pr in range(GROUP_W // (2 * HD)):
            cols = slice(pr * 2 * HD, (pr + 1) * 2 * HD)
            slopes = [_alibi_slope(pattern * HEADS_PER_PATTERN + 2 * pr + e) for e in range(2)]
            for sb in range(nsb + 1):
                gb = i * nsb + sb
                if sb < nsb:
                    rows = slice(sb * BLK, (sb + 1) * BLK)
                    qp, dop, cp, lp = q_ref[rows, cols], do_ref[rows, cols], c_ref[rows, cols], l_ref[rows, cols]
                else:
                    qp, dop, cp, lp = qn_ref[:, cols], don_ref[:, cols], cn_ref[:, cols], ln_ref[:, cols]
                if sb < nsb:
                    dq1, dk1, dv1 = pair(qp, dop, cp, lp, k_ref[rows, cols], v_ref[rows, cols], rd_f, m_diag)
                    dq_ref[rows, cols] += dq1
                    dk_ref[rows, cols] += dk1
                    dv_ref[rows, cols] += dv1
                valid = jnp.logical_and(gb % blocks_per_stream != 0, gb < n_blocks).astype(jnp.int32)
                m_prev = jnp.logical_and(rel_prev <= BLK, (rel_prev + (1 - valid) * (4 * BLK)) <= BLK)
                if sb == 0:
                    kp, vp = kp_ref[:, cols], vp_ref[:, cols]
                else:
                    prows = slice((sb - 1) * BLK, sb * BLK)
                    kp, vp = k_ref[prows, cols], v_ref[prows, cols]
                dq0, dk0, dv0 = pair(qp, dop, cp, lp, kp, vp, rp_f, m_prev)
                if sb < nsb:
                    dq_ref[rows, cols] += dq0
                if sb > 0:
                    dk_ref[prows, cols] += dk0
                    dv_ref[prows, cols] += dv0

    cur = pl.BlockSpec((tq, GROUP_W), lambda i: (i, 0))
    prev = pl.BlockSpec((BLK, GROUP_W), lambda i: (jnp.maximum(i * nsb - 1, 0), 0))
    nxt = pl.BlockSpec((BLK, GROUP_W), lambda i: (jnp.minimum((i + 1) * nsb, n_blocks - 1), 0))
    shp = _sds((s, GROUP_W), F32)
    return _call(body, name=name, out_shape=[shp, shp, shp], grid=(s // tq,),
                 in_specs=[cur] * 6 + [prev, prev] + [nxt] * 4, out_specs=[cur, cur, cur])(
                     q, k, v, d_o, cterm, lse, k, v, q, d_o, cterm, lse)


def _band_constants(d):
    row = lax.broadcasted_iota(jnp.int32, (2 * BLK, 2 * BLK), 0)
    col = lax.broadcasted_iota(jnp.int32, (2 * BLK, 2 * BLK), 1)
    rel = BLK + jnp.where(row >= BLK, row - BLK, row) - col
    band = jnp.logical_and(rel >= 0, rel <= BLK)
    return (rel * d).astype(F32), band, (col >= BLK).astype(jnp.int32)


def _stack_heads(x, first):
    zero = jnp.zeros_like(x)
    return jnp.concatenate([jnp.where(first, x, zero), jnp.where(first, zero, x)], axis=0)


def _unstack_heads(x2, first):
    return jnp.where(first, x2[:BLK], x2[BLK:])


def _head_column(x):
    return jnp.concatenate([x[:, 0:1], x[:, HD:HD + 1]], axis=0)


def attn_fwd2(q, k, v, pattern, name, tq=512):
    s = q.shape[0]
    d = ATTN_DILATIONS[pattern]
    blocks_per_stream = (s // d) // BLK
    nsb = tq // BLK

    def body(q_ref, k_ref, v_ref, kp_ref, vp_ref, o_ref, l_ref):
        i = pl.program_id(0)
        rel_f, band, own = _band_constants(d)
        first = _lane_first_half((BLK, 2 * HD))
        upper = lax.broadcasted_iota(jnp.int32, (2 * BLK, 1), 0) < BLK
        for sb in range(nsb):
            rows = slice(sb * BLK, (sb + 1) * BLK)
            has_prev = ((i * nsb + sb) % blocks_per_stream != 0).astype(jnp.int32)
            mask = jnp.logical_and(band, (own + has_prev) > 0)
            for pr in range(GROUP_W // (2 * HD)):
                cols = slice(pr * 2 * HD, (pr + 1) * 2 * HD)
                if sb == 0:
                    kcat = jnp.concatenate([kp_ref[:, cols], k_ref[rows, cols]], axis=0)
                    vcat = jnp.concatenate([vp_ref[:, cols], v_ref[rows, cols]], axis=0)
                else:
                    both = slice((sb - 1) * BLK, (sb + 1) * BLK)
                    kcat, vcat = k_ref[both, cols], v_ref[both, cols]
                h0 = pattern * HEADS_PER_PATTERN + 2 * pr
                slope = jnp.where(upper, _alibi_slope(h0), _alibi_slope(h0 + 1))
                sc = _dot_nt(_stack_heads(q_ref[rows, cols], first), kcat) * 0.125 - slope * rel_f
                sc = jnp.where(mask, sc, NEG)
                m = jnp.max(sc, axis=-1, keepdims=True)
                p = jnp.exp(sc - m)
                l = jnp.sum(p, axis=-1, keepdims=True)
                o2 = _dot((p * (1.0 / l)).astype(BF16), vcat)
                o_ref[rows, cols] = _unstack_heads(o2, first)
                lse = m + jnp.log(l)
                l_ref[rows, cols] = jnp.where(first, lse[:BLK], lse[BLK:])

    cur = pl.BlockSpec((tq, GROUP_W), lambda i: (i, 0))
    prev = pl.BlockSpec((BLK, GROUP_W), lambda i: (jnp.maximum(i * nsb - 1, 0), 0))
    return _call(body, name=name, out_shape=[_sds((s, GROUP_W), F32), _sds((s, GROUP_W), F32)], grid=(s // tq,),
                 in_specs=[cur, cur, cur, prev, prev], out_specs=[cur, cur])(q, k, v, k, v)


def attn_bwd2(q, k, v, d_o, cterm, lse, pattern, name, tq=512):
    s = q.shape[0]
    d = ATTN_DILATIONS[pattern]
    blocks_per_stream = (s // d) // BLK
    nsb = tq // BLK
    n_blocks = s // BLK

    def body(q_ref, k_ref, v_ref, do_ref, c_ref, l_ref, kp_ref, vp_ref, qn_ref, kn_ref, vn_ref, don_ref, cn_ref,
             ln_ref, dq_ref, dk_ref, dv_ref):
        i = pl.program_id(0)
        rel_f, band, own = _band_constants(d)
        first = _lane_first_half((BLK, 2 * HD))
        upper = lax.broadcasted_iota(jnp.int32, (2 * BLK, 1), 0) < BLK
        dk_ref[...] = jnp.zeros_like(dk_ref)
        dv_ref[...] = jnp.zeros_like(dv_ref)
        for sb in range(nsb + 1):
            gb = i * nsb + sb
            rows = slice(sb * BLK, (sb + 1) * BLK)
            before = slice((sb - 1) * BLK, sb * BLK)
            inside = (gb < n_blocks).astype(jnp.int32)
            has_prev = jnp.logical_and(gb % blocks_per_stream != 0, gb < n_blocks).astype(jnp.int32)
            mask = jnp.logical_and(band, (own * inside + has_prev) > 0)
            for pr in range(GROUP_W // (2 * HD)):
                cols = slice(pr * 2 * HD, (pr + 1) * 2 * HD)
                if sb == 0:
                    kcat = jnp.concatenate([kp_ref[:, cols], k_ref[rows, cols]], axis=0)
                    vcat = jnp.concatenate([vp_ref[:, cols], v_ref[rows, cols]], axis=0)
                elif sb == nsb:
                    kcat = jnp.concatenate([k_ref[before, cols], kn_ref[:, cols]], axis=0)
                    vcat = jnp.concatenate([v_ref[before, cols], vn_ref[:, cols]], axis=0)
                else:
                    both = slice((sb - 1) * BLK, (sb + 1) * BLK)
                    kcat, vcat = k_ref[both, cols], v_ref[both, cols]
                if sb < nsb:
                    qp, dop, cp, lp = q_ref[rows, cols], do_ref[rows, cols], c_ref[rows, cols], l_ref[rows, cols]
                else:
                    qp, dop, cp, lp = qn_ref[:, cols], don_ref[:, cols], cn_ref[:, cols], ln_ref[:, cols]
                h0 = pattern * HEADS_PER_PATTERN + 2 * pr
                slope = jnp.where(upper, _alibi_slope(h0), _alibi_slope(h0 + 1))
                q2 = _stack_heads(qp, first)
                do2 = _stack_heads(dop, first)
                sc = jnp.where(mask, _dot_nt(q2, kcat) * 0.125 - slope * rel_f, NEG)
                pm = jnp.exp(sc - _head_column(lp))
                dl = (pm * (_dot_nt(do2, vcat) - _head_column(cp))).astype(BF16)
                if sb < nsb:
                    dq_ref[rows, cols] = _unstack_heads(_dot(dl, kcat), first) * 0.125
                dk2 = _dot_tn(dl, q2) * 0.125
                dv2 = _dot_tn(pm.astype(BF16), do2)
                if sb > 0:
                    dk_ref[before, cols] += dk2[:BLK]
                    dv_ref[before, cols] += dv2[:BLK]
                if sb < nsb:
                    dk_ref[rows, cols] += dk2[BLK:]
                    dv_ref[rows, cols] += dv2[BLK:]

    cur = pl.BlockSpec((tq, GROUP_W), lambda i: (i, 0))
    prev = pl.BlockSpec((BLK, GROUP_W), lambda i: (jnp.maximum(i * nsb - 1, 0), 0))
    nxt = pl.BlockSpec((BLK, GROUP_W), lambda i: (jnp.minimum((i + 1) * nsb, n_blocks - 1), 0))
    shp = _sds((s, GROUP_W), F32)
    return _call(body, name=name, out_shape=[shp, shp, shp], grid=(s // tq,),
                 in_specs=[cur] * 6 + [prev, prev] + [nxt] * 6, out_specs=[cur, cur, cur])(
                     q, k, v, d_o, cterm, lse, k, v, q, k, v, d_o, cterm, lse)


HALO = 16
CONV_TQ = 512


def conv_fwd(p, w, b, name):
    s = p.shape[0]
    tq = CONV_TQ
    ncol = SSD_CONV_DIM // GROUP_W
    cb0 = COL_XBC // GROUP_W

    def body(u_ref, up_ref, w_ref, b_ref, c_ref, xc_ref):
        i = pl.program_id(0)
        prev = up_ref[...].astype(F32) * (i > 0).astype(F32)
        ext = jnp.concatenate([prev, u_ref[...].astype(F32)], axis=0)
        acc = b_ref[...] + w_ref[SSD_CONV - 1:SSD_CONV, :] * ext[HALO:HALO + tq]
        for kk in range(SSD_CONV - 1):
            acc += w_ref[kk:kk + 1, :] * pltpu.roll(ext, SSD_CONV - 1 - kk, 0)[HALO:HALO + tq]
        c_ref[...] = acc.astype(BF16)
        xc_ref[...] = (acc * _sigmoid(acc)).astype(BF16)

    cur_in = pl.BlockSpec((tq, GROUP_W), lambda i, j: (i, cb0 + j))
    prev_in = pl.BlockSpec((HALO, GROUP_W), lambda i, j: (jnp.maximum(i * (tq // HALO) - 1, 0), cb0 + j))
    cur_out = pl.BlockSpec((tq, GROUP_W), lambda i, j: (i, j))
    shp = _sds((s, SSD_CONV_DIM), BF16)
    return _call(body, name=name, out_shape=[shp, shp], grid=(s // tq, ncol),
                 in_specs=[cur_in, prev_in, pl.BlockSpec((SSD_CONV, GROUP_W), lambda i, j: (0, j)),
                           pl.BlockSpec((1, GROUP_W), lambda i, j: (0, j))],
                 out_specs=[cur_out, cur_out])(p, p, w, b)


def conv_bwd(p, cpre, dxs, d_b, d_c, w, name):
    s = p.shape[0]
    tq = CONV_TQ
    ncol = SSD_CONV_DIM // GROUP_W
    n_xs = SSD_INNER // GROUP_W
    cb0 = COL_XBC // GROUP_W
    nt = s // tq

    def body(u_ref, up_ref, c_ref, cn_ref, dx_ref, dxn_ref, dbm_ref, dbmn_ref, dcm_ref, dcmn_ref, w_ref,
             du_ref, dw_ref, db_ref):
        j, i = pl.program_id(0), pl.program_id(1)

        def dpre(c16, dx):
            c = c16.astype(F32)
            sg = _sigmoid(c)
            return dx * (sg * (1.0 + c * (1.0 - sg)))

        def pick(a_ref, b_ref, c_ref_):
            return jnp.where(j < n_xs, a_ref[...], jnp.where(j == n_xs, b_ref[...], c_ref_[...]))

        dc = dpre(c_ref[...], pick(dx_ref, dbm_ref, dcm_ref))
        dcn = dpre(cn_ref[...], pick(dxn_ref, dbmn_ref, dcmn_ref)) * (i < nt - 1).astype(F32)
        dext = jnp.concatenate([dc, dcn], axis=0)
        prev = up_ref[...].astype(F32) * (i > 0).astype(F32)
        uext = jnp.concatenate([prev, u_ref[...].astype(F32)], axis=0)

        @pl.when(i == 0)
        def _():
            dw_ref[...] = jnp.zeros_like(dw_ref)
            db_ref[...] = jnp.zeros_like(db_ref)

        du = w_ref[SSD_CONV - 1:SSD_CONV, :] * dc
        for kk in range(SSD_CONV - 1):
            sh = SSD_CONV - 1 - kk
            du += w_ref[kk:kk + 1, :] * pltpu.roll(dext, tq + HALO - sh, 0)[0:tq]
        du_ref[...] = du.astype(BF16)
        for kk in range(SSD_CONV):
            shifted = uext if kk == SSD_CONV - 1 else pltpu.roll(uext, SSD_CONV - 1 - kk, 0)
            dw_ref[kk:kk + 1, :] += jnp.sum(dc * shifted[HALO:HALO + tq], axis=0, keepdims=True)
        db_ref[...] += jnp.sum(dc, axis=0, keepdims=True)

    hb = tq // HALO
    cur_p = pl.BlockSpec((tq, GROUP_W), lambda j, i: (i, cb0 + j))
    prev_p = pl.BlockSpec((HALO, GROUP_W), lambda j, i: (jnp.maximum(i * hb - 1, 0), cb0 + j))
    cur = pl.BlockSpec((tq, GROUP_W), lambda j, i: (i, j))
    nxt = pl.BlockSpec((HALO, GROUP_W), lambda j, i: (jnp.minimum((i + 1) * hb, s // HALO - 1), j))

    def piece(first_tile, n_tiles):
        def on(j):
            return jnp.logical_and(j >= first_tile, j < first_tile + n_tiles)

        def col(j):
            return jnp.clip(j - first_tile, 0, n_tiles - 1)

        return (pl.BlockSpec((tq, GROUP_W), lambda j, i: (jnp.where(on(j), i, 0), col(j))),
                pl.BlockSpec((HALO, GROUP_W),
                             lambda j, i: (jnp.where(on(j), jnp.minimum((i + 1) * hb, s // HALO - 1), 0), col(j))))

    return _call(body, name=name,
                 out_shape=[_sds((s, SSD_CONV_DIM), BF16), _sds((8, SSD_CONV_DIM), F32), _sds((1, SSD_CONV_DIM), F32)],
                 grid=(ncol, nt),
                 in_specs=[cur_p, prev_p, cur, nxt, *piece(0, n_xs), *piece(n_xs, 1), *piece(n_xs + 1, 1),
                           pl.BlockSpec((SSD_CONV, GROUP_W), lambda j, i: (0, j))],
                 out_specs=[cur, pl.BlockSpec((8, GROUP_W), lambda j, i: (0, j)),
                            pl.BlockSpec((1, GROUP_W), lambda j, i: (0, j))])(
                                p, p, cpre, cpre, dxs, dxs, d_b, d_b, d_c, d_c, w)


def _softplus(x):
    return jnp.maximum(x, 0.0) + jnp.log(1.0 + jnp.exp(-jnp.abs(x)))


def _ssd_decays(dtr_ref, dtrt_ref, bias_ref, biast_ref, alog_ref, alogt_ref):
    row = lax.broadcasted_iota(jnp.int32, (BLK, BLK), 0)
    col = lax.broadcasted_iota(jnp.int32, (BLK, BLK), 1)
    lower = (row >= col).astype(F32)
    upper = (row <= col).astype(F32)
    dtb = dtr_ref[...] + bias_ref[...]
    dt = _softplus(dtb)
    a = dt * (-jnp.exp(alog_ref[...]))
    cs = _dot_hi(lower, a)
    a_t = _softplus(dtrt_ref[...] + biast_ref[...]) * (-jnp.exp(alogt_ref[...]))
    cs_t = _dot_hi(a_t, upper)
    return dtb, dt, cs, cs_t, row, col, upper


SSD_GROUPS_PER_STEP = 4


def _per_group(body, gps, kinds):
    def wrapped(*refs):
        for gi in range(gps):
            args, pos = [], 0
            for kind, n in kinds:
                if kind == "each":
                    args.append(refs[pos + gi])
                    pos += gps
                    continue
                ref = refs[pos]
                pos += 1
                if kind == "cols":
                    args.append(ref.at[:, gi * n:(gi + 1) * n])
                else:
                    args.append(ref.at[gi] if n == 1 else ref.at[pl.ds(gi * n, n)])
            body(*args)

    return wrapped


def ssd_fwd(p, xc, dtg, dtg_t, params, gn, name):
    s = p.shape[0]
    nc = s // BLK
    bias, bias_t, alog, alog_t, dskip = params

    def body(xs_ref, b_ref, c_ref, z_ref, dtr_ref, dtrt_ref, bias_ref, biast_ref, alog_ref, alogt_ref, dsk_ref,
             gn_ref, y_ref, sin_ref, hp_ref, h_ref):
        c_idx = pl.program_id(1)

        @pl.when(c_idx == 0)
        def _():
            h_ref[...] = jnp.zeros_like(h_ref)

        _, dt, cs, cs_t, row, col, _ = _ssd_decays(dtr_ref, dtrt_ref, bias_ref, biast_ref, alog_ref, alogt_ref)
        first = _lane_first_half((BLK, 2 * HD))
        first_row = _lane_first_half((1, 2 * HD))
        tril = row >= col
        b16, c16 = b_ref[...], c_ref[...]
        cb = _dot_nt(c16, b16)
        n_pairs = GROUP_W // (2 * HD)
        tot = cs[BLK - 1:BLK, :]
        exp_cs, exp_rest, exp_tot = jnp.exp(cs), jnp.exp(tot - cs), jnp.exp(tot)

        def per_head(v, mask):
            return jnp.concatenate([jnp.where(mask, v[:, 2 * pr:2 * pr + 1], v[:, 2 * pr + 1:2 * pr + 2])
                                    for pr in range(n_pairs)], axis=1)

        xs = xs_ref[...].astype(F32)
        xt = xs * per_head(dt, first)
        xt16 = xt.astype(BF16)
        hstate = jnp.concatenate([h_ref[pr] for pr in range(n_pairs)], axis=1)
        for pr in range(n_pairs):
            hp_ref[pr] = h_ref[pr]
        y_off = per_head(exp_cs, first) * _dot(c16, hstate.astype(BF16))
        new = per_head(exp_tot, first_row) * hstate + _dot_tn(b16, (per_head(exp_rest, first) * xt).astype(BF16))
        for pr in range(n_pairs):
            h_ref[pr] = new[:, pr * 2 * HD:(pr + 1) * 2 * HD]
        y_diag = []
        for pr in range(n_pairs):
            cols = slice(pr * 2 * HD, (pr + 1) * 2 * HD)
            m2 = jnp.concatenate(
                [(cb * jnp.exp(jnp.where(tril, cs[:, h:h + 1] - cs_t[h:h + 1, :], NEG))).astype(BF16)
                 for h in (2 * pr, 2 * pr + 1)], axis=1)
            y_diag.append(_dot(m2, _stack_heads(xt16[:, cols], first)))
        y = jnp.concatenate(y_diag, axis=1) + y_off + xs * per_head(dsk_ref[...], first_row)
        y_ref[...] = y
        zv = z_ref[...].astype(F32)
        yz = y * (zv * _sigmoid(zv))
        r = lax.rsqrt(jnp.mean(yz * yz, axis=-1, keepdims=True) + EPS)
        sin_ref[...] = (yz * r * gn_ref[...]).astype(BF16)

    gps = SSD_GROUPS_PER_STEP
    wide, narrow, lead = ("cols", GROUP_W), ("cols", BLK), ("lead", 1)
    kinds = [wide, narrow, narrow, ("each", 0)] + [lead] * 7 + [wide, wide, wide, lead, ("lead", 4)]
    wide_w, narrow_w = GROUP_W * gps, BLK * gps
    gparam = pl.BlockSpec((gps, 1, 8), lambda g, c: (g, 0, 0))
    gparam_t = pl.BlockSpec((gps, 8, 1), lambda g, c: (g, 0, 0))
    z_specs = [pl.BlockSpec((BLK, GROUP_W), functools.partial(lambda g, c, gi: (c, COL_Z // GROUP_W + gps * g + gi),
                                                              gi=gi)) for gi in range(gps)]
    return _call(
        _per_group(body, gps, kinds), name=name,
        out_shape=[_sds((s, SSD_INNER), F32), _sds((s, SSD_INNER), BF16),
                   _sds((SSD_GROUPS, nc, 4, BLK, 2 * HD), F32)],
        grid=(SSD_GROUPS // gps, nc),
        in_specs=[pl.BlockSpec((BLK, wide_w), lambda g, c: (c, g)),
                  pl.BlockSpec((BLK, narrow_w), lambda g, c: (c, SSD_INNER // narrow_w + g)),
                  pl.BlockSpec((BLK, narrow_w), lambda g, c: (c, (SSD_INNER + SSD_GROUPS * BLK) // narrow_w + g)),
                  *z_specs,
                  pl.BlockSpec((gps, BLK, 8), lambda g, c: (g, c, 0)),
                  pl.BlockSpec((gps, 8, BLK), lambda g, c: (g, 0, c)),
                  gparam, gparam_t, gparam, gparam_t, gparam,
                  pl.BlockSpec((1, wide_w), lambda g, c: (0, g))],
        out_specs=[pl.BlockSpec((BLK, wide_w), lambda g, c: (c, g)),
                   pl.BlockSpec((BLK, wide_w), lambda g, c: (c, g)),
                   pl.BlockSpec((gps, None, 4, BLK, 2 * HD), lambda g, c: (g, c, 0, 0, 0))],
        scratch_shapes=[pltpu.VMEM((4 * gps, BLK, 2 * HD), F32)],
    )(xc, xc, xc, *([p] * gps), dtg, dtg_t, bias, bias_t, alog, alog_t, dskip, gn)


def ssd_bwd(p, xc, y, d_sin, hprev, dtg, dtg_t, params, gn, name):
    s = p.shape[0]
    nc = s // BLK
    bias, bias_t, alog, alog_t, dskip = params

    def body(xs_ref, b_ref, c_ref, z_ref, y_ref, dsin_ref, hp_ref, dtr_ref, dtrt_ref, bias_ref,
             biast_ref, alog_ref, alogt_ref, dsk_ref, gn_ref,
             dxs_ref, db_ref, dc_ref, dz_ref, ddt_ref, da_ref, dbias_ref, ddsk_ref, dgn_ref, dh_ref):
        c_idx = pl.program_id(1)

        @pl.when(c_idx == 0)
        def _():
            dh_ref[...] = jnp.zeros_like(dh_ref)
            da_ref[...] = jnp.zeros_like(da_ref)
            dbias_ref[...] = jnp.zeros_like(dbias_ref)
            ddsk_ref[...] = jnp.zeros_like(ddsk_ref)
            dgn_ref[...] = jnp.zeros_like(dgn_ref)

        dtb, dt, cs, cs_t, row, col, upper = _ssd_decays(dtr_ref, dtrt_ref, bias_ref, biast_ref, alog_ref, alogt_ref)
        first = _lane_first_half((BLK, 2 * HD))
        second = jnp.logical_not(first)
        first_row = _lane_first_half((1, 2 * HD))
        tril = row >= col
        triu = row <= col
        last_row = lax.broadcasted_iota(jnp.int32, (BLK, 1), 0) == BLK - 1
        lane8 = lax.broadcasted_iota(jnp.int32, (BLK, 8), 1)

        yv = y_ref[...]
        zv = z_ref[...].astype(F32)
        sg = _sigmoid(zv)
        yz = yv * (zv * sg)
        r = lax.rsqrt(jnp.mean(yz * yz, axis=-1, keepdims=True) + EPS)
        yzn = yz * r
        dsn = dsin_ref[...]
        dgn_ref[...] += jnp.sum(dsn * yzn, axis=0, keepdims=True)
        dsn = dsn * gn_ref[...]
        dyz = r * (dsn - yzn * jnp.mean(dsn * yzn, axis=-1, keepdims=True))
        dy = dyz * (zv * sg)
        dz_ref[...] = (dyz * yv * (sg * (1.0 + zv * (1.0 - sg)))).astype(BF16)
        xs_all = xs_ref[...].astype(F32)
        ddsk_ref[...] += jnp.sum(dy * xs_all, axis=0, keepdims=True)

        b16, c16 = b_ref[...], c_ref[...]
        cb = _dot_nt(c16, b16)
        cb_t = _dot_nt(b16, c16)
        n_pairs = GROUP_W // (2 * HD)
        tot = cs[BLK - 1:BLK, :]
        exp_cs, exp_rest, exp_tot = jnp.exp(cs), jnp.exp(tot - cs), jnp.exp(tot)

        def per_head(v, mask):
            return jnp.concatenate([jnp.where(mask, v[:, 2 * pr:2 * pr + 1], v[:, 2 * pr + 1:2 * pr + 2])
                                    for pr in range(n_pairs)], axis=1)

        def head_sums(v):
            out = jnp.zeros((BLK, 8), F32)
            for pr in range(n_pairs):
                sa, sb = _pair_sum(v[:, pr * 2 * HD:(pr + 1) * 2 * HD], first)
                out = jnp.where(lane8 == 2 * pr, sa, jnp.where(lane8 == 2 * pr + 1, sb, out))
            return out

        dt_w, e_w, f_w = per_head(dt, first), per_head(exp_cs, first), per_head(exp_rest, first)
        xt = xs_all * dt_w
        xt16 = xt.astype(BF16)
        hstate = jnp.concatenate([hp_ref[pr] for pr in range(n_pairs)], axis=1)
        h16 = hstate.astype(BF16)
        dhn = jnp.concatenate([dh_ref[pr] for pr in range(n_pairs)], axis=1)
        dhn16 = dhn.astype(BF16)
        edy16 = (e_w * dy).astype(BF16)
        y_off = e_w * _dot(c16, h16)
        dcs_all = head_sums(dy * y_off)
        dc_acc = _dot_nt(edy16, h16)
        zmat = _dot(b16, dhn16)
        t_all = head_sums(zmat * xt) * exp_rest
        hh_rows = jnp.sum(head_sums(dhn * hstate), axis=0, keepdims=True)
        dtot = jnp.sum(t_all, axis=0, keepdims=True) + hh_rows * exp_tot
        dcs_all = dcs_all - t_all + jnp.where(last_row, dtot, 0.0)
        fxt16 = (f_w * xt).astype(BF16)
        db_acc = _dot_nt(fxt16, dhn16)
        dh_new = _dot_tn(c16, edy16) + per_head(exp_tot, first_row) * dhn
        for pr in range(n_pairs):
            dh_ref[pr] = dh_new[:, pr * 2 * HD:(pr + 1) * 2 * HD]
        g_sum = jnp.zeros((BLK, BLK), F32)
        gt_sum = jnp.zeros((BLK, BLK), F32)
        d_xt_parts = []
        for pr in range(n_pairs):
            cols = slice(pr * 2 * HD, (pr + 1) * 2 * HD)
            dym2 = _stack_heads(dy[:, cols].astype(BF16), first)
            d_m2 = _dot_nt(dym2, xt16[:, cols])
            d_mt2 = _dot_nt(xt16[:, cols], dym2)
            mt2 = []
            for e, h in enumerate((2 * pr, 2 * pr + 1)):
                cs_c, cs_r = cs[:, h:h + 1], cs_t[h:h + 1, :]
                decay = jnp.exp(jnp.where(tril, cs_c - cs_r, NEG))
                decay_t = jnp.exp(jnp.where(triu, cs_r - cs_c, NEG))
                gm = d_m2[e * BLK:(e + 1) * BLK] * decay
                gmt = d_mt2[:, e * BLK:(e + 1) * BLK] * decay_t
                g_sum += gm
                gt_sum += gmt
                dcs_h = jnp.sum(gm * cb, axis=-1, keepdims=True) - jnp.sum(gmt * cb_t, axis=-1, keepdims=True)
                dcs_all = dcs_all + jnp.where(lane8 == h, dcs_h, 0.0)
                mt2.append((cb_t * decay_t).astype(BF16))
            d_xt_parts.append(_dot(jnp.concatenate(mt2, axis=1), dym2))
        d_xt = jnp.concatenate(d_xt_parts, axis=1) + f_w * zmat
        dxs_ref[...] = dy * per_head(dsk_ref[...], first_row) + d_xt * dt_w
        ddtx_all = head_sums(d_xt * xs_all)

        dc_ref[...] = dc_acc + _dot(g_sum.astype(BF16), b16)
        db_ref[...] = db_acc + _dot(gt_sum.astype(BF16), c16)
        d_a = _dot_hi(upper, dcs_all)
        a_neg = -jnp.exp(alog_ref[...])
        ddt = ddtx_all + d_a * a_neg
        da_ref[...] += jnp.sum(d_a * dt, axis=0, keepdims=True)
        ddtr = ddt * _sigmoid(dtb)
        ddt_ref[...] = ddtr
        dbias_ref[...] += jnp.sum(ddtr, axis=0, keepdims=True)

    gps = SSD_GROUPS_PER_STEP
    k_wide, k_narrow, k_lead = ("cols", GROUP_W), ("cols", BLK), ("lead", 1)
    kinds = ([k_wide, k_narrow, k_narrow, ("each", 0), k_wide, k_wide] + [k_lead] * 8 + [k_wide]
             + [k_wide, k_narrow, k_narrow, k_wide] + [k_lead] * 4 + [k_wide] + [("lead", 4)])
    wide_w, narrow_w = GROUP_W * gps, BLK * gps
    rc = lambda c: nc - 1 - c
    gparam = pl.BlockSpec((gps, 1, 8), lambda g, c: (g, 0, 0))
    gparam_t = pl.BlockSpec((gps, 8, 1), lambda g, c: (g, 0, 0))
    wide = pl.BlockSpec((BLK, wide_w), lambda g, c: (rc(c), g))
    narrow = pl.BlockSpec((BLK, narrow_w), lambda g, c: (rc(c), g))
    z_specs = [pl.BlockSpec((BLK, GROUP_W),
                            functools.partial(lambda g, c, gi: (rc(c), COL_Z // GROUP_W + gps * g + gi), gi=gi))
               for gi in range(gps)]
    return _call(
        _per_group(body, gps, kinds), name=name,
        out_shape=[_sds((s, SSD_INNER), F32), _sds((s, GROUP_W), F32), _sds((s, GROUP_W), F32),
                   _sds((s, SSD_INNER), BF16), _sds((SSD_GROUPS, s, 8), F32),
                   _sds((SSD_GROUPS, 1, 8), F32), _sds((SSD_GROUPS, 1, 8), F32),
                   _sds((SSD_GROUPS, 1, GROUP_W), F32), _sds((1, SSD_INNER), F32)],
        grid=(SSD_GROUPS // gps, nc),
        in_specs=[wide,
                  pl.BlockSpec((BLK, narrow_w), lambda g, c: (rc(c), SSD_INNER // narrow_w + g)),
                  pl.BlockSpec((BLK, narrow_w), lambda g, c: (rc(c), (SSD_INNER + SSD_GROUPS * BLK) // narrow_w + g)),
                  *z_specs,
                  wide, wide,
                  pl.BlockSpec((gps, None, 4, BLK, 2 * HD), lambda g, c: (g, rc(c), 0, 0, 0)),
                  pl.BlockSpec((gps, BLK, 8), lambda g, c: (g, rc(c), 0)),
                  pl.BlockSpec((gps, 8, BLK), lambda g, c: (g, 0, rc(c))),
                  gparam, gparam_t, gparam, gparam_t, gparam,
                  pl.BlockSpec((1, wide_w), lambda g, c: (0, g))],
        out_specs=[wide, narrow, narrow, wide,
                   pl.BlockSpec((gps, BLK, 8), lambda g, c: (g, rc(c), 0)),
                   gparam, gparam,
                   pl.BlockSpec((gps, 1, GROUP_W), lambda g, c: (g, 0, 0)),
                   pl.BlockSpec((1, wide_w), lambda g, c: (0, g))],
        scratch_shapes=[pltpu.VMEM((4 * gps, BLK, 2 * HD), F32)],
    )(xc, xc, xc, *([p] * gps), y, d_sin, hprev, dtg, dtg_t, bias, bias_t, alog, alog_t, dskip, gn)


def merge_fwd(p, a, sbr, name, tm=512):
    s = p.shape[0]
    nj = D_MODEL // GROUP_W

    def body(ga_ref, gs_ref, a_ref, s_ref, o_ref):
        o_ref[...] = (_sigmoid(ga_ref[...].astype(F32)) * a_ref[...]
                      + _sigmoid(gs_ref[...].astype(F32)) * s_ref[...]).astype(BF16)

    blk = pl.BlockSpec((tm, GROUP_W), lambda i, j: (i, j))
    return _call(body, name=name, out_shape=_sds((s, D_MODEL), BF16), grid=(s // tm, nj),
                 in_specs=[pl.BlockSpec((tm, GROUP_W), lambda i, j: (i, COL_GA // GROUP_W + j)),
                           pl.BlockSpec((tm, GROUP_W), lambda i, j: (i, COL_GS // GROUP_W + j)), blk, blk],
                 out_specs=blk)(p, p, a, sbr)


def merge_bwd(p, a, sbr, dmerged, name, tm=512):
    s = p.shape[0]
    nj = D_MODEL // GROUP_W

    def body(ga_ref, gs_ref, a_ref, s_ref, dm_ref, da_ref, ds_ref, dga_ref, dgs_ref):
        dm = dm_ref[...]
        sa = _sigmoid(ga_ref[...].astype(F32))
        ss = _sigmoid(gs_ref[...].astype(F32))
        da_ref[...] = (dm * sa).astype(BF16)
        ds_ref[...] = (dm * ss).astype(BF16)
        dga_ref[...] = (dm * a_ref[...] * sa * (1.0 - sa)).astype(BF16)
        dgs_ref[...] = (dm * s_ref[...] * ss * (1.0 - ss)).astype(BF16)

    blk = pl.BlockSpec((tm, GROUP_W), lambda i, j: (i, j))
    shp = _sds((s, D_MODEL), BF16)
    return _call(body, name=name, out_shape=[shp] * 4, grid=(s // tm, nj),
                 in_specs=[pl.BlockSpec((tm, GROUP_W), lambda i, j: (i, COL_GA // GROUP_W + j)),
                           pl.BlockSpec((tm, GROUP_W), lambda i, j: (i, COL_GS // GROUP_W + j)), blk, blk, blk],
                 out_specs=[blk] * 4)(p, p, a, sbr, dmerged)


def _group_major(v):
    return v.reshape(SSD_GROUPS, 1, 8), v.reshape(SSD_GROUPS, 8, 1)


def mixer_forward(x, w, rider=None, later_weights=None):
    s = x.shape[0]
    h = rms_fwd(x, w["mix_norm"], "mix_rms")
    p = matmul_nn(h, w["w_in_main"], "mix_proj", BF16, tm=1024, tn=512, rider=rider)
    rode = None
    if rider is not None:
        p, rode = p
        w = dict(w, **later_weights(rode))
    dt_raw = matmul_nn(h, w["w_in_dt"], "mix_proj_dt", F32, tm=1024, tn=DT_PAD)
    qn, kn = qk_norm_fwd(p, w["q_gain"], w["k_gain"], "qk_norm")
    streams, os_, lses = [], [], []
    for g, d in enumerate(ATTN_DILATIONS):
        cols = slice(g * GROUP_W, (g + 1) * GROUP_W)
        qs, ks = _to_streams(qn[:, cols], d), _to_streams(kn[:, cols], d)
        vs = _to_streams(p[:, COL_V + g * GROUP_W:COL_V + (g + 1) * GROUP_W], d)
        o, lse = attn_fwd2(qs, ks, vs, g, f"attn_fwd{g}")
        streams.append((qs, ks, vs, lse))
        os_.append(_from_streams(o, d))
        lses.append(_from_streams(lse, d))
    attn_o = attn_merge_fwd(os_, lses, "attn_merge")
    cpre, xc = conv_fwd(p, w["conv_w"], w["conv_b"], "conv_fwd")
    dtg = dt_raw[:, :SSD_HEADS].reshape(s, SSD_GROUPS, 8).transpose(1, 0, 2)
    dtg_t = dtg.transpose(0, 2, 1)
    params = (*_group_major(w["dt_bias"]), *_group_major(w["a_log"]), _group_major(w["d_skip"])[0])
    y, s_in, hprev = ssd_fwd(p, xc, dtg, dtg_t, params, w["ssd_norm"], "ssd_fwd")
    a = matmul_nn(attn_o, w["w_attn_branch"], "attn_branch", F32, tm=1024, tn=512)
    sbr = matmul_nn(s_in, w["w_ssd_branch"], "ssd_branch", F32, tm=1024, tn=512)
    merged = merge_fwd(p, a, sbr, "merge")
    x_out = matmul_nn(merged, w["w_out"], "mix_out", F32, tm=1024, tn=512, res=x)
    saved = dict(h=h, p=p, streams=streams, os=os_, lses=lses, attn_o=attn_o, cpre=cpre, xc=xc, dtg=dtg,
                 dtg_t=dtg_t, params=params, y=y, s_in=s_in, hprev=hprev, a=a, sbr=sbr, merged=merged, w=w)
    return x_out, saved, rode


def mixer_backward(dx_out, x, sv, ride_early=None, ride_late=None):
    s = x.shape[0]
    p = sv["p"]
    w = sv["w"]
    g = {}
    dmerged = matmul_nt(dx_out, w["w_out"], "d_merged", F32, tm=1024, tn=512, tk=1024)
    g["w_out"] = matmul_tn(sv["merged"], dx_out, "dw_out", tn=512, ts=1024)
    da, ds, dga, dgs = merge_bwd(p, sv["a"], sv["sbr"], dmerged, "merge_bwd")
    g["w_attn_branch"] = matmul_tn(sv["attn_o"], da, "dw_attn_branch", tn=512, ts=1024)
    g["w_ssd_branch"] = matmul_tn(sv["s_in"], ds, "dw_ssd_branch", tn=512, ts=1024)
    d_attn_o = matmul_nt(da, w["w_attn_branch"], "d_attn_o", F32, tm=1024, tn=512, tk=1024)
    d_sin = matmul_nt(ds, w["w_ssd_branch"], "d_ssd_in", F32, tm=1024, tn=512, tk=1024)
    dxs, d_b, d_c, dz, ddt, d_asum, d_bias, d_dsk, d_gn = ssd_bwd(
        p, sv["xc"], sv["y"], d_sin, sv["hprev"], sv["dtg"], sv["dtg_t"], sv["params"], w["ssd_norm"], "ssd_bwd")
    dxbc, d_convw, d_convb = conv_bwd(p, sv["cpre"], dxs, d_b, d_c, w["conv_w"], "conv_bwd")
    g["conv_w"] = d_convw[:SSD_CONV]
    g["conv_b"] = d_convb
    g["dt_bias"] = d_bias.reshape(1, SSD_HEADS)
    g["a_log"] = (d_asum * (-jnp.exp(sv["params"][2]))).reshape(1, SSD_HEADS)
    g["d_skip"] = jnp.sum(d_dsk.reshape(SSD_HEADS, HD), axis=1).reshape(1, SSD_HEADS)
    g["ssd_norm"] = d_gn
    merged_bwd = attn_merge_bwd(d_attn_o, sv["os"], sv["lses"], "attn_merge_bwd")
    dqs, dks, dvs = [], [], []
    for gi, d in enumerate(ATTN_DILATIONS):
        qs, ks, vs, lse = sv["streams"][gi]
        d_o = _to_streams(merged_bwd[gi], d)
        cterm = _to_streams(merged_bwd[3 + gi], d)
        dq, dk, dv = attn_bwd2(qs, ks, vs, d_o, cterm, lse, gi, f"attn_bwd{gi}")
        dqs.append(_from_streams(dq, d))
        dks.append(_from_streams(dk, d))
        dvs.append(_from_streams(dv, d).astype(BF16))
    dqk, d_qg, d_kg = qk_norm_bwd(p, dqs, dks, w["q_gain"], w["k_gain"], "qk_norm_bwd")
    g["q_norm"] = jnp.sum(d_qg.reshape(N_ATTN_HEADS, HD), axis=0).reshape(1, HD)
    g["k_norm"] = jnp.sum(d_kg.reshape(N_ATTN_HEADS, HD), axis=0).reshape(1, HD)
    dp = [dqk, jnp.concatenate(dvs, axis=1), dz, dxbc, dga, dgs]
    ddt_pad = jnp.pad(ddt.transpose(1, 0, 2).reshape(s, SSD_HEADS), ((0, 0), (0, DT_PAD - SSD_HEADS)))
    if ride_early is not None:
        g["w_in_main"], g["rode_early"] = matmul_tn_pieces(sv["h"], dp, "dw_in", tn=512, ts=1024,
                                                           rider=ride_early(g))
    else:
        g["w_in_main"] = matmul_tn_pieces(sv["h"], dp, "dw_in", tn=512, ts=1024)
    g["w_in_dt"] = matmul_tn(sv["h"], ddt_pad, "dw_in_dt", tn=DT_PAD, ts=1024)
    if ride_late is not None:
        dh_main, g["rode_late"] = matmul_nt_pieces(dp, w["w_in_main"], "d_mix_h", F32, tm=1024, tn=512, tk=512,
                                                   rider=ride_late(g))
    else:
        dh_main = matmul_nt_pieces(dp, w["w_in_main"], "d_mix_h", F32, tm=1024, tn=512, tk=512)
    dh_dt = matmul_nt(ddt_pad, w["w_in_dt"], "d_mix_h_dt", F32, tm=1024, tn=1024, tk=DT_PAD)
    dx, g["mix_norm"] = rms_bwd([dh_main, dh_dt], x, w["mix_norm"], dx_out, "mix_drms")
    return dx, g


ANY = pl.BlockSpec(memory_space=pl.ANY)


def _place():
    x, y, c = lax.axis_index("x"), lax.axis_index("y"), lax.axis_index("c")
    chips = [(1 - x, y), (x, 1 - y), (1 - x, 1 - y)]
    return x, y, c, 2 * x + y, chips


def _comm_call(body, *, name, out_shape, n_in, scratch_shapes, aliases=None):
    return pl.pallas_call(
        body, out_shape=out_shape, in_specs=[ANY] * n_in, out_specs=[ANY] * len(out_shape),
        scratch_shapes=scratch_shapes, input_output_aliases=aliases or {}, name=name,
        compiler_params=pltpu.CompilerParams(has_side_effects=True))


def gather_weights(shards, small):
    n = len(shards)
    halves = [a.shape[0] // 2 for a in shards]
    out_shape = [_sds((N_CHIP,) + a.shape, a.dtype) for a in shards] + [_sds((N_CHIP,) + small.shape, small.dtype)]

    def body(*refs):
        ins, outs = refs[:n + 1], refs[n + 1:2 * n + 2]
        send1, recv1, send2, recv2, local = refs[2 * n + 2:]
        x, y, c, me, chips = _place()
        sibling = (x, y, 1 - c)

        def rows(k, chip, core):
            if k == n:
                return outs[k].at[chip]
            return outs[k].at[chip, pl.ds(core * halves[k], halves[k])]

        def level1(k, t, incoming):
            chip = 2 * chips[t][0] + chips[t][1]
            src = ins[k] if k == n else ins[k].at[pl.ds(c * halves[k], halves[k])]
            return pltpu.make_async_remote_copy(
                src_ref=src, dst_ref=rows(k, chip if incoming else me, c), send_sem=send1.at[3 * k + t],
                recv_sem=recv1.at[3 * k + t], device_id=(*chips[t], c), device_id_type=MESH)

        def level2(k, t, incoming):
            chip = 2 * chips[t][0] + chips[t][1]
            core = (1 - c) if incoming else c
            return pltpu.make_async_remote_copy(
                src_ref=rows(k, chip, core), dst_ref=rows(k, chip, core), send_sem=send2.at[3 * k + t],
                recv_sem=recv2.at[3 * k + t], device_id=sibling, device_id_type=MESH)

        own = [pltpu.make_async_copy(ins[k], outs[k].at[me], local.at[k]) for k in range(n + 1)]
        for cp in own:
            cp.start()
        first = [level1(k, t, False) for k in range(n + 1) for t in range(3)]
        for cp in first:
            cp.start()
        passed = []
        for k in range(n + 1):
            for t in range(3):
                level1(k, t, True).wait_recv()
                if k < n:
                    cp = level2(k, t, False)
                    cp.start()
                    passed.append(cp)
        for k in range(n):
            for t in range(3):
                level2(k, t, True).wait_recv()
        for cp in first + passed:
            cp.wait_send()
        for cp in own:
            cp.wait()

    dma = pltpu.SemaphoreType.DMA
    return _comm_call(body, name="gather_weights", out_shape=out_shape, n_in=n + 1,
                      scratch_shapes=[dma((3 * n + 3,)), dma((3 * n + 3,)), dma((3 * n,)), dma((3 * n,)),
                                      dma((n + 1,))])(*shards, small)


def reduce_to_sibling(grads):
    n = len(grads)
    halves = [a.shape[1] // 2 for a in grads]
    shapes = [_sds((N_CHIP, h, a.shape[2]), a.dtype) for a, h in zip(grads, halves)]

    def body(*refs):
        ins, got, kept = refs[:n], refs[n:2 * n], refs[2 * n:3 * n]
        send, recv, local = refs[3 * n:]
        x, y, c, _, _ = _place()
        copies, locals_ = [], []
        for k in range(n):
            h = halves[k]
            locals_.append(pltpu.make_async_copy(ins[k].at[:, pl.ds(c * h, h)], kept[k], local.at[k]))
            copies.append(pltpu.make_async_remote_copy(
                src_ref=ins[k].at[:, pl.ds((1 - c) * h, h)], dst_ref=got[k], send_sem=send.at[k], recv_sem=recv.at[k],
                device_id=(x, y, 1 - c), device_id_type=MESH))
        for cp in locals_ + copies:
            cp.start()
        for cp in copies:
            cp.wait_recv()
        for cp in copies:
            cp.wait_send()
        for cp in locals_:
            cp.wait()

    dma = pltpu.SemaphoreType.DMA
    res = _comm_call(body, name="reduce_to_sibling", out_shape=shapes + shapes, n_in=n,
                     scratch_shapes=[dma((n,)), dma((n,)), dma((n,))])(*grads)
    return res[:n], res[n:]


def reduce_to_owner(sums):
    n = len(sums)
    shapes = [_sds(a.shape, a.dtype) for a in sums]

    def body(*refs):
        ins, outs = refs[:n], refs[n:2 * n]
        send, recv, local = refs[2 * n:]
        x, y, c, me, chips = _place()
        copies, locals_ = [], []
        for k in range(n):
            locals_.append(pltpu.make_async_copy(ins[k].at[me], outs[k].at[3], local.at[k]))
            for t in range(3):
                chip = 2 * chips[t][0] + chips[t][1]
                copies.append(pltpu.make_async_remote_copy(
                    src_ref=ins[k].at[chip], dst_ref=outs[k].at[t], send_sem=send.at[3 * k + t],
                    recv_sem=recv.at[3 * k + t], device_id=(*chips[t], c), device_id_type=MESH))
        for cp in locals_ + copies:
            cp.start()
        for cp in copies:
            cp.wait_recv()
        for cp in copies:
            cp.wait_send()
        for cp in locals_:
            cp.wait()

    dma = pltpu.SemaphoreType.DMA
    return _comm_call(body, name="reduce_to_owner", out_shape=shapes, n_in=n,
                      scratch_shapes=[dma((3 * n,)), dma((3 * n,)), dma((n,))])(*sums)


def share_with_sibling(halves_):
    n = len(halves_)
    shapes = [_sds((2 * a.shape[0], a.shape[1]), a.dtype) for a in halves_]

    def body(*refs):
        ins, outs = refs[:n], refs[n:2 * n]
        send, recv, local = refs[2 * n:]
        x, y, c, _, _ = _place()
        copies, locals_ = [], []
        for k in range(n):
            h = ins[k].shape[0]
            mine = outs[k].at[pl.ds(c * h, h)]
            locals_.append(pltpu.make_async_copy(ins[k], mine, local.at[k]))
            copies.append(pltpu.make_async_remote_copy(
                src_ref=ins[k], dst_ref=mine, send_sem=send.at[k], recv_sem=recv.at[k],
                device_id=(x, y, 1 - c), device_id_type=MESH))
        for cp in locals_ + copies:
            cp.start()
        for cp in copies:
            cp.wait_recv()
        for cp in copies:
            cp.wait_send()
        for cp in locals_:
            cp.wait()

    dma = pltpu.SemaphoreType.DMA
    return _comm_call(body, name="share_with_sibling", out_shape=shapes, n_in=n,
                      scratch_shapes=[dma((n,)), dma((n,)), dma((n,))])(*halves_)


def _cores():
    c = lax.axis_index("c")
    return jnp.stack([c, 1 - c]).astype(jnp.int32)


def _staged_call(body, *, name, grid, in_specs, out_specs, out_shape, scratch_shapes):
    return pl.pallas_call(
        body, out_shape=out_shape, name=name,
        grid_spec=pltpu.PrefetchScalarGridSpec(num_scalar_prefetch=1, grid=grid, in_specs=in_specs,
                                               out_specs=out_specs, scratch_shapes=scratch_shapes),
        compiler_params=pltpu.CompilerParams(dimension_semantics=("arbitrary",) * len(grid),
                                             vmem_limit_bytes=V7X_VMEM_LIMIT, has_side_effects=True))


def gather_rider(shards, tiles):
    dma = pltpu.SemaphoreType.DMA
    n = len(shards)
    geo = [(a.shape[0] // 2, tm, (a.shape[0] // 2) // tm) for a, tm in zip(shards, tiles)]
    scratch = []
    for a, (h, tm, nk) in zip(shards, geo):
        scratch += [pltpu.VMEM((N_CHIP,) + a.shape, a.dtype), dma((3, nk)), dma((3, nk)), dma((3, nk)), dma((3, nk)),
                    dma((nk + 2,))]

    def copies(j, in_ref, scr):
        buf, send1, recv1, send2, recv2, local = scr[6 * j:6 * j + 6]
        h, tm, nk = geo[j]
        x, y, c, me, chips = _place()
        chip_of = [2 * chips[t][0] + chips[t][1] for t in range(3)]

        def rows(chip, core, k):
            return buf.at[chip, pl.ds(core * h + k * tm, tm)]

        def mine(k):
            if k == nk:
                return pltpu.make_async_copy(in_ref.at[pl.ds((1 - c) * h, h)], buf.at[me, pl.ds((1 - c) * h, h)],
                                             local.at[nk])
            return pltpu.make_async_copy(in_ref.at[pl.ds(c * h + k * tm, tm)], rows(me, c, k), local.at[k])

        def level1(t, k, incoming):
            place = rows(chip_of[t] if incoming else me, c, k)
            return pltpu.make_async_remote_copy(src_ref=place, dst_ref=place, send_sem=send1.at[t, k],
                                                recv_sem=recv1.at[t, k], device_id=(*chips[t], c), device_id_type=MESH)

        def level2(t, k, incoming):
            place = rows(chip_of[t], (1 - c) if incoming else c, k)
            return pltpu.make_async_remote_copy(src_ref=place, dst_ref=place, send_sem=send2.at[t, k],
                                                recv_sem=recv2.at[t, k], device_id=(x, y, 1 - c),
                                                device_id_type=MESH)

        return buf, local, nk, mine, level1, level2

    def start(ins, outs, scr):
        for j in range(n):
            _, _, nk, mine, _, _ = copies(j, ins[j], scr)
            for k in range(nk + 1):
                mine(k).start()
        for j in range(n):
            _, _, nk, mine, level1, _ = copies(j, ins[j], scr)
            for k in range(nk):
                mine(k).wait()
                for t in range(3):
                    level1(t, k, False).start()

    def finish(ins, outs, scr):
        for j in range(n):
            _, _, nk, _, level1, level2 = copies(j, ins[j], scr)
            for k in range(nk):
                for t in range(3):
                    level1(t, k, True).wait_recv()
                    level2(t, k, False).start()
        for j in range(n):
            buf, local, nk, mine, level1, level2 = copies(j, ins[j], scr)
            for k in range(nk):
                for t in range(3):
                    level2(t, k, True).wait_recv()
            for k in range(nk):
                for t in range(3):
                    level1(t, k, False).wait_send()
                    level2(t, k, False).wait_send()
            mine(nk).wait()
            pltpu.make_async_copy(buf, outs[j], local.at[nk + 1]).start()
        for j in range(n):
            buf, local, nk, _, _, _ = copies(j, ins[j], scr)
            pltpu.make_async_copy(buf, outs[j], local.at[nk + 1]).wait()

    return Rider(list(shards), [_sds((N_CHIP,) + a.shape, a.dtype) for a in shards], scratch, start, finish)


def run_alone(rider, name):
    return _call(lambda: None, name=name, out_shape=[], in_specs=[], out_specs=[], grid=(1,), rider=rider)()[1]


def sibling_sum(g, tm, name):
    _, r, cdim = g.shape
    h = r // 2
    ni = h // tm
    dma = pltpu.SemaphoreType.DMA

    def body(cores_ref, keep_ref, give_ref, out_ref, slot, send, recv):
        par = (pl.program_id(0) * ni + pl.program_id(1)) % 2
        x, y, c, _, _ = _place()
        cp = pltpu.make_async_remote_copy(src_ref=give_ref, dst_ref=slot.at[par], send_sem=send.at[par],
                                          recv_sem=recv.at[par], device_id=(x, y, 1 - c), device_id_type=MESH)
        cp.start()
        cp.wait_recv()
        out_ref[...] = (keep_ref[...].astype(F32) + slot[par].astype(F32)).astype(out_ref.dtype)
        cp.wait_send()

    flat = g.reshape(N_CHIP * r, cdim)
    return _staged_call(
        body, name=name, grid=(N_CHIP, ni),
        in_specs=[pl.BlockSpec((tm, cdim), lambda j, i, cores: ((2 * j + cores[0]) * ni + i, 0)),
                  pl.BlockSpec((tm, cdim), lambda j, i, cores: ((2 * j + cores[1]) * ni + i, 0))],
        out_specs=pl.BlockSpec((None, tm, cdim), lambda j, i, cores: (j, i, 0)),
        out_shape=_sds((N_CHIP, h, cdim), g.dtype),
        scratch_shapes=[pltpu.VMEM((2, tm, cdim), g.dtype), dma((2,)), dma((2,))],
    )(_cores(), flat, flat)


def owner_sum_rider(sums, tiles):
    dma = pltpu.SemaphoreType.DMA
    n = len(sums)
    geo = [(a.shape[1], tm, a.shape[1] // tm) for a, tm in zip(sums, tiles)]
    scratch = []
    for a, (h, tm, nk) in zip(sums, geo):
        cdim = a.shape[2]
        scratch += [pltpu.VMEM(a.shape, a.dtype), pltpu.VMEM((3, h, cdim), a.dtype), pltpu.VMEM((2, h, cdim), F32),
                    dma((3, nk)), dma((3, nk)), dma((nk,)), dma((nk,)), dma((2,))]

    def copies(j, scr):
        part, got, res, send, recv, send2, recv2, local = scr[8 * j:8 * j + 8]
        h, tm, nk = geo[j]
        x, y, c, me, chips = _place()

        def to_owner(t, k):
            chip = 2 * chips[t][0] + chips[t][1]
            return pltpu.make_async_remote_copy(
                src_ref=part.at[chip, pl.ds(k * tm, tm)], dst_ref=got.at[t, pl.ds(k * tm, tm)],
                send_sem=send.at[t, k], recv_sem=recv.at[t, k], device_id=(*chips[t], c), device_id_type=MESH)

        def to_sibling(k):
            place = res.at[c, pl.ds(k * tm, tm)]
            return pltpu.make_async_remote_copy(src_ref=place, dst_ref=place, send_sem=send2.at[k],
                                                recv_sem=recv2.at[k], device_id=(x, y, 1 - c), device_id_type=MESH)

        return part, got, res, local, to_owner, to_sibling, (tm, nk, c, me)

    def start(ins, outs, scr):
        for j in range(n):
            part, _, _, local, _, _, _ = copies(j, scr)
            pltpu.make_async_copy(ins[j], part, local.at[0]).start()
        for j in range(n):
            part, _, _, local, to_owner, _, (tm, nk, c, me) = copies(j, scr)
            pltpu.make_async_copy(ins[j], part, local.at[0]).wait()
            for k in range(nk):
                for t in range(3):
                    to_owner(t, k).start()

    def finish(ins, outs, scr):
        for j in range(n):
            part, got, res, _, to_owner, to_sibling, (tm, nk, c, me) = copies(j, scr)
            for k in range(nk):
                rows = pl.ds(k * tm, tm)
                for t in range(3):
                    to_owner(t, k).wait_recv()
                acc = part[me, rows, :].astype(F32)
                for t in range(3):
                    acc = acc + got[t, rows, :].astype(F32)
                res[c, rows, :] = acc
                to_sibling(k).start()
        for j in range(n):
            _, _, res, local, to_owner, to_sibling, (tm, nk, c, me) = copies(j, scr)
            for k in range(nk):
                to_sibling(k).wait_recv()
            for k in range(nk):
                to_sibling(k).wait_send()
                for t in range(3):
                    to_owner(t, k).wait_send()
            pltpu.make_async_copy(res, outs[j], local.at[1]).start()
        for j in range(n):
            _, _, res, local, _, _, _ = copies(j, scr)
            pltpu.make_async_copy(res, outs[j], local.at[1]).wait()

    return Rider(list(sums), [_sds((2, a.shape[1], a.shape[2]), F32) for a in sums], scratch, start, finish)


def gather_conv_w(w):
    def body(in_ref, out_ref, send, recv):
        x, y, c, me, chips = _place()
        out_ref[me] = in_ref[...]
        copies = []
        for t in range(3):
            copies.append(pltpu.make_async_remote_copy(
                src_ref=out_ref.at[me], dst_ref=out_ref.at[me], send_sem=send.at[t], recv_sem=recv.at[t],
                device_id=(*chips[t], c), device_id_type=MESH))
        for cp in copies:
            cp.start()
        for cp in copies:
            cp.wait_recv()
        for cp in copies:
            cp.wait_send()

    dma = pltpu.SemaphoreType.DMA
    vmem = pl.BlockSpec(memory_space=pltpu.VMEM)
    return pl.pallas_call(
        body, out_shape=_sds((N_CHIP,) + w.shape, w.dtype), in_specs=[vmem], out_specs=vmem, name="gather_conv_w",
        scratch_shapes=[dma((3,)), dma((3,))],
        compiler_params=pltpu.CompilerParams(has_side_effects=True))(w)


N_DEV = 8
SMALL_ROWS = 32
SMALL_LANES = 1024


def all_reduce_small(arrays):
    n_arr = len(arrays)
    places = []
    for k, a in enumerate(arrays):
        for ri in range(a.shape[0]):
            for c0 in range(0, a.shape[1], SMALL_LANES):
                places.append((k, ri, c0, min(SMALL_LANES, a.shape[1] - c0), len(places)))
    assert len(places) <= SMALL_ROWS

    def body(*refs):
        ins, outs = refs[:n_arr], refs[n_arr:2 * n_arr]
        buf, send, recv = refs[2 * n_arr:]
        x, y, c, _, _ = _place()
        me = 4 * x + 2 * y + c
        buf[me] = jnp.zeros((SMALL_ROWS, SMALL_LANES), F32)
        for k, ri, c0, width, row in places:
            buf[me, row:row + 1, 0:width] = ins[k][ri:ri + 1, c0:c0 + width]
        copies = []
        for r in range(1, N_DEV):
            px = (1 - x) if r & 4 else x
            py = (1 - y) if r & 2 else y
            pc = (1 - c) if r & 1 else c
            copies.append(pltpu.make_async_remote_copy(
                src_ref=buf.at[me], dst_ref=buf.at[me], send_sem=send.at[r - 1], recv_sem=recv.at[r - 1],
                device_id=(px, py, pc), device_id_type=MESH))
        for cp in copies:
            cp.start()
        for cp in copies:
            cp.wait_recv()
        for cp in copies:
            cp.wait_send()
        acc = buf[0]
        for j in range(1, N_DEV):
            acc = acc + buf[j]
        for k, ri, c0, width, row in places:
            outs[k][ri:ri + 1, c0:c0 + width] = acc[row:row + 1, 0:width]

    dma = pltpu.SemaphoreType.DMA
    vmem = pl.BlockSpec(memory_space=pltpu.VMEM)
    return pl.pallas_call(
        body, out_shape=[_sds(a.shape, F32) for a in arrays], in_specs=[vmem] * n_arr, out_specs=[vmem] * n_arr,
        name="all_reduce_small",
        scratch_shapes=[pltpu.VMEM((N_DEV, SMALL_ROWS, SMALL_LANES), F32), dma((N_DEV - 1,)), dma((N_DEV - 1,))],
        compiler_params=pltpu.CompilerParams(has_side_effects=True))(*arrays)


def _row_tile(rows, limit, multiple):
    return max(t for t in range(multiple, min(rows, limit) + 1, multiple) if rows % t == 0)


def add_pair(a, b, name):
    _, h, c = a.shape

    def body(a_ref, b_ref, o_ref):
        o_ref[...] = (a_ref[...].astype(F32) + b_ref[...].astype(F32)).astype(o_ref.dtype)

    blk = pl.BlockSpec((None, h, c), lambda j: (j, 0, 0))
    return _call(body, name=name, out_shape=_sds(a.shape, a.dtype), grid=(N_CHIP,), in_specs=[blk, blk],
                 out_specs=blk)(a, b)


def sum_slots(buf, name):
    _, h, c = buf.shape
    tm = _row_tile(h, 256, 16)

    def body(b_ref, o_ref):
        acc = b_ref[3].astype(F32)
        for t in range(3):
            acc = acc + b_ref[t].astype(F32)
        o_ref[...] = acc

    return _call(body, name=name, out_shape=_sds((h, c), F32), grid=(h // tm,),
                 in_specs=[pl.BlockSpec((N_CHIP, tm, c), lambda i: (0, i, 0))],
                 out_specs=pl.BlockSpec((tm, c), lambda i: (i, 0)))(buf)


def _adamw_math(w, g, m, v):
    c1 = 1.0 - ADAM_B1 ** ADAM_STEP
    c2 = 1.0 - ADAM_B2 ** ADAM_STEP
    m2 = ADAM_B1 * m + (1.0 - ADAM_B1) * g
    v2 = ADAM_B2 * v + (1.0 - ADAM_B2) * (g * g)
    return -ADAM_LR * ((m2 / c1) / (jnp.sqrt(v2 / c2) + ADAM_EPS) + ADAM_WD * w), m2, v2


def adamw(w, g, row_off, m, v, name):
    _, r, c = w.shape
    tm = r if r < 8 else _row_tile(math.gcd(r, row_off) if row_off else r, 128, 8)

    def body(w_ref, g_ref, m_ref, v_ref, go_ref, d_ref, m2_ref, v2_ref):
        gv = g_ref[...]
        go_ref[...] = gv
        d_ref[...], m2_ref[...], v2_ref[...] = _adamw_math(w_ref[...], gv, m_ref[...], v_ref[...])

    blk = pl.BlockSpec((None, tm, c), lambda i: (0, i, 0))
    shp = _sds((1, r, c), F32)
    return _call(body, name=name, out_shape=[shp] * 4, grid=(r // tm,),
                 in_specs=[blk, pl.BlockSpec((tm, c), lambda i: (row_off // tm + i, 0)), blk, blk],
                 out_specs=[blk] * 4)(w, g, m, v)


def adamw_small(ws, gs, ms, vs):
    n = len(ws)

    def body(*refs):
        ins, outs = refs[:4 * n], refs[4 * n:]
        for k in range(n):
            w_ref, g_ref, m_ref, v_ref = (ins[j * n + k] for j in range(4))
            outs[k][...], outs[n + k][...], outs[2 * n + k][...] = _adamw_math(w_ref[...], g_ref[...], m_ref[...],
                                                                               v_ref[...])

    vmem = pl.BlockSpec(memory_space=pltpu.VMEM)
    shapes = [_sds(w.shape, F32) for w in ws] * 3
    res = pl.pallas_call(body, out_shape=shapes, in_specs=[vmem] * (4 * n), out_specs=[vmem] * (3 * n),
                         name="adamw_small")(*ws, *gs, *ms, *vs)
    return res[:n], res[n:2 * n], res[2 * n:]


BIG = ("ffn1_w_gate", "ffn1_w_up", "ffn1_w_down", "w_in", "w_attn_branch", "w_ssd_branch", "w_out",
       "ffn2_w_gate", "ffn2_w_up", "ffn2_w_down")
SMALL = ("ffn1_norm", "mix_norm", "q_norm", "k_norm", "conv_b", "dt_bias", "a_log", "d_skip", "ssd_norm", "ffn2_norm")
WEIGHTS = ("ffn1_norm", "ffn1_w_gate", "ffn1_w_up", "ffn1_w_down", "mix_norm", "w_in", "q_norm", "k_norm", "conv_w",
           "conv_b", "dt_bias", "a_log", "d_skip", "ssd_norm", "w_attn_branch", "w_ssd_branch", "w_out", "ffn2_norm",
           "ffn2_w_gate", "ffn2_w_up", "ffn2_w_down")
CONV_SHARD = SSD_CONV_DIM // N_CHIP
CLASSES = {
    "ffn1_in": (("ffn1_w_gate", 1024), ("ffn1_w_up", 1024)),
    "ffn1_out": (("ffn1_w_down", 704),),
    "mix_in": (("w_in", 1024),),
    "mix_attn": (("w_attn_branch", 512),),
    "late_out": (("ffn2_w_down", 704), ("w_ssd_branch", 512), ("w_out", 256)),
    "ffn2_in": (("ffn2_w_gate", 1024), ("ffn2_w_up", 1024)),
}
CLASS_TILE = {"ffn1_in": 256, "ffn1_out": 176, "mix_in": 128, "mix_attn": 256, "late_out": 368, "ffn2_in": 256}


def _pack_small(vals, conv_part, loss_part=None):
    flat = [vals[k].reshape(-1) for k in SMALL]
    flat.append(jnp.zeros((SSD_CONV * SSD_CONV_DIM,), F32) if conv_part is None else conv_part.reshape(-1))
    flat.append(jnp.zeros((1,), F32) if loss_part is None else loss_part.reshape(1))
    flat = jnp.concatenate(flat)
    return jnp.pad(flat, (0, SMALL_ROWS * D_MODEL - flat.shape[0])).reshape(SMALL_ROWS, D_MODEL)


def _unpack_small(pack, like):
    flat = pack.reshape(-1)
    out, off = {}, 0
    for k in SMALL:
        n = like[k].size
        out[k] = flat[off:off + n].reshape(like[k].shape)
        off += n
    conv = flat[off:off + SSD_CONV * SSD_CONV_DIM].reshape(SSD_CONV, SSD_CONV_DIM)
    return out, conv, flat[off + SSD_CONV * SSD_CONV_DIM]


def _chip_major_cols(a):
    r = a.shape[0]
    return a.reshape(r, N_CHIP, -1).transpose(1, 0, 2)


def _from_chip_major_cols(a):
    return a.transpose(1, 0, 2).reshape(a.shape[1], -1)


def kernel(x, ffn1_norm, ffn1_w_gate, ffn1_w_up, ffn1_w_down, mix_norm, w_in, q_norm, k_norm, conv_w, conv_b, dt_bias, a_log, d_skip, ssd_norm, w_attn_branch, w_ssd_branch, w_out, ffn2_norm, ffn2_w_gate, ffn2_w_up, ffn2_w_down, loss_target, m_ffn1_norm, m_ffn1_w_gate, m_ffn1_w_up, m_ffn1_w_down, m_mix_norm, m_w_in, m_q_norm, m_k_norm, m_conv_w, m_conv_b, m_dt_bias, m_a_log, m_d_skip, m_ssd_norm, m_w_attn_branch, m_w_ssd_branch, m_w_out, m_ffn2_norm, m_ffn2_w_gate, m_ffn2_w_up, m_ffn2_w_down, v_ffn1_norm, v_ffn1_w_gate, v_ffn1_w_up, v_ffn1_w_down, v_mix_norm, v_w_in, v_q_norm, v_k_norm, v_conv_w, v_conv_b, v_dt_bias, v_a_log, v_d_skip, v_ssd_norm, v_w_attn_branch, v_w_ssd_branch, v_w_out, v_ffn2_norm, v_ffn2_w_gate, v_ffn2_w_up, v_ffn2_w_down):
    env = dict(locals())
    wts = {k: env[k] for k in WEIGHTS}
    moms = {k: env["m_" + k] for k in WEIGHTS}
    vars_ = {k: env["v_" + k] for k in WEIGHTS}
    x0 = x[0]
    target = loss_target[0]

    def gather(classes):
        shards = [jnp.concatenate([wts[k][0] for k, _ in CLASSES[c]], axis=0).astype(BF16) for c in classes]
        return gather_rider(shards, [CLASS_TILE[c] for c in classes])

    def reducer(classes, parts):
        sums = [sibling_sum(p, CLASS_TILE[c], f"sibling_sum_{c}") for c, p in zip(classes, parts)]
        return owner_sum_rider(sums, [CLASS_TILE[c] for c in classes])

    (w_ffn1_in,) = run_alone(gather(["ffn1_in"]), "gather_ffn1_in")
    x1, saved1, (w_ffn1_out, w_mix_in, w_mix_attn) = ffn_forward(
        x0, ffn1_norm, w_ffn1_in, lambda rode: rode[0], "ffn1", rider=gather(["ffn1_out", "mix_in", "mix_attn"]))
    dt0, dt1 = IN_DT0 - 3 * IN_SHARD, IN_DT1 - 3 * IN_SHARD
    mixer_w = dict(
        mix_norm=mix_norm,
        w_in_main=jnp.concatenate([w_mix_in[0], w_mix_in[1], w_mix_in[2], w_mix_in[3][:, :dt0], w_mix_in[3][:, dt1:]],
                                  axis=1),
        w_in_dt=jnp.pad(w_mix_in[3][:, dt0:dt1], ((0, 0), (0, DT_PAD - SSD_HEADS))),
        q_gain=jnp.tile(q_norm, (1, 2)), k_gain=jnp.tile(k_norm, (1, 2)),
        conv_w=_from_chip_major_cols(gather_conv_w(conv_w[0])), conv_b=conv_b, dt_bias=dt_bias, a_log=a_log,
        d_skip=d_skip, ssd_norm=ssd_norm, w_attn_branch=_from_chip_major_cols(w_mix_attn))

    def later_weights(rode):
        late = rode[0]
        return dict(w_ssd_branch=late[:, 704:1216].reshape(SSD_INNER, D_MODEL),
                    w_out=late[:, 1216:1472].reshape(D_MODEL, D_MODEL))

    x2, saved_mix, (w_late_out, w_ffn2_in) = mixer_forward(x1, mixer_w, gather(["late_out", "ffn2_in"]), later_weights)
    x3, saved2, _ = ffn_forward(x2, ffn2_norm, w_ffn2_in, lambda rode: w_late_out, "ffn2")
    dx3, sq = loss_grad(x3, target, "loss")

    grads = {}
    dx2, grads["ffn2_norm"], d_ffn2_in, d_ffn2_down = ffn_backward(dx3, x2, ffn2_norm, w_ffn2_in, w_late_out, saved2,
                                                                   "ffn2")

    def ride_early(g):
        late = jnp.concatenate([d_ffn2_down, g["w_ssd_branch"].reshape(N_CHIP, -1, D_MODEL),
                                g["w_out"].reshape(N_CHIP, -1, D_MODEL)], axis=1)
        return reducer(["ffn2_in", "late_out"], [d_ffn2_in, late])

    def ride_late(g):
        main = g["w_in_main"]
        last = jnp.concatenate([main[:, 3 * IN_SHARD:IN_DT0], g["w_in_dt"][:, :SSD_HEADS], main[:, IN_DT0:]], axis=1)
        g_in = jnp.stack([main[:, j * IN_SHARD:(j + 1) * IN_SHARD] for j in range(3)] + [last])
        return reducer(["mix_in", "mix_attn"], [g_in, _chip_major_cols(g["w_attn_branch"])])

    dx1, gmix = mixer_backward(dx2, x1, saved_mix, ride_early, ride_late)
    dx0, grads["ffn1_norm"], rode_in, rode_out = ffn_backward(
        dx1, x0, ffn1_norm, w_ffn1_in, w_ffn1_out, saved1, "ffn1",
        ride_down=lambda d: reducer(["ffn1_out"], [d]), ride_in=lambda d: reducer(["ffn1_in"], [d]))
    for k in ("mix_norm", "q_norm", "k_norm", "conv_b", "dt_bias", "a_log", "d_skip", "ssd_norm"):
        grads[k] = gmix[k]
    reduced = dict(zip(("ffn2_in", "late_out", "mix_in", "mix_attn", "ffn1_in", "ffn1_out"),
                       (*gmix["rode_early"], *gmix["rode_late"], rode_in[0], rode_out[0])))
    reduced = {c: r.reshape(-1, r.shape[2]) for c, r in reduced.items()}
    summed = all_reduce_small([grads[k] for k in SMALL]
                              + [gmix["conv_w"], (0.5 * jnp.sum(sq) / D_MODEL).reshape(1, 1)])
    g_small = dict(zip(SMALL, summed))
    loss = summed[-1].reshape(())
    chip = 2 * lax.axis_index("x") + lax.axis_index("y")
    g_conv = lax.dynamic_slice_in_dim(summed[-2], chip * CONV_SHARD, CONV_SHARD, axis=1)

    g_final, delta, new_m, new_v = dict(g_small), {}, {}, {}

    def update(k, g_arr, row_off):
        w, m, v = wts[k], moms[k], vars_[k]
        rows, cols = w.shape[1:]
        if cols % 128:
            res = adamw(jnp.swapaxes(w, 1, 2), g_arr[row_off:row_off + rows].T, 0, jnp.swapaxes(m, 1, 2),
                        jnp.swapaxes(v, 1, 2), f"adamw_{k}")
            res = [jnp.swapaxes(r, 1, 2) for r in res]
        else:
            res = adamw(w, g_arr, row_off, m, v, f"adamw_{k}")
        g_final[k], delta[k], new_m[k], new_v[k] = res

    for cls, members in CLASSES.items():
        off = 0
        for k, rows in members:
            update(k, reduced[cls], off)
            off += rows
    update("conv_w", g_conv, 0)
    small = adamw_small(*([d[k] for k in SMALL] for d in (wts, g_small, moms, vars_)))
    for res, vals in zip((delta, new_m, new_v), small):
        res.update(zip(SMALL, vals))

    return (loss, dx0[None], *[g_final[k] for k in WEIGHTS], *[delta[k] for k in WEIGHTS],
            *[new_m[k] for k in WEIGHTS], *[new_v[k] for k in WEIGHTS])
```

```python
import collections
import functools
import math

import jax
import jax.numpy as jnp
from jax import lax
from jax.experimental import pallas as pl
from jax.experimental.pallas import tpu as pltpu

F32 = jnp.float32
BF16 = jnp.bfloat16
MESH = pl.DeviceIdType.MESH

EPS = 1e-6
D_MODEL = 1024
D_FF = 2816
N_CHIP = 4
FF_SHARD = D_FF // N_CHIP
HD = 64
BLK = 128
ATTN_DILATIONS = (1, 4, 16)
HEADS_PER_PATTERN = 8
N_ATTN_HEADS = 24
ALIBI_MAX_EXP = 8.0
ATTN_QKV = 1536
GROUP_W = 512
SSD_INNER = 2048
SSD_HEADS = 32
SSD_GROUPS = 4
SSD_CONV = 4
SSD_CONV_DIM = 3072
IN_COLS = 11808
IN_DT0, IN_DT1 = 9728, 9760
IN_SHARD = IN_COLS // 4
COL_K, COL_V, COL_Z, COL_XBC, COL_GA, COL_GS, P_COLS = 1536, 3072, 4608, 6656, 9728, 10752, 11776
DT_PAD = 128

ADAM_LR, ADAM_B1, ADAM_B2, ADAM_EPS, ADAM_WD, ADAM_STEP = 0.001, 0.9, 0.999, 1e-08, 0.01, 10

V7X_VMEM_LIMIT = 56 * 1024 * 1024
NEG = -1e30


Rider = collections.namedtuple("Rider", "arrays out_shape scratch start finish")
Rider.__doc__ = """An exchange between devices that rides in a compute kernel: its copies are started in the host's
first grid step and waited for in its last, so they travel while the host computes.  arrays / out_shape: extra HBM
operands and results; scratch: extra scratch; start, finish: f(in_refs, out_refs, scratch_refs)."""


def _call(body, *, name, out_shape, in_specs, out_specs, grid=(), scratch_shapes=(), aliases=None, rider=None):
    params = dict(dimension_semantics=("arbitrary",) * len(grid), vmem_limit_bytes=V7X_VMEM_LIMIT)
    if rider is None:
        return pl.pallas_call(
            body, out_shape=out_shape, grid=grid, in_specs=in_specs, out_specs=out_specs,
            scratch_shapes=scratch_shapes, input_output_aliases=aliases or {}, name=name,
            compiler_params=pltpu.CompilerParams(**params))
    single = not isinstance(out_shape, (list, tuple))
    main_out = [out_shape] if single else list(out_shape)
    main_specs = [out_specs] if single else list(out_specs)
    n_in, n_out, n_scr = len(in_specs), len(main_out), len(scratch_shapes)
    r_in, r_out = len(rider.arrays), len(rider.out_shape)

    def wrapped(*refs):
        ins, refs = refs[:n_in], refs[n_in:]
        r_ins, refs = refs[:r_in], refs[r_in:]
        outs, refs = refs[:n_out], refs[n_out:]
        r_outs, refs = refs[:r_out], refs[r_out:]
        scr, r_scr = refs[:n_scr], refs[n_scr:]
        first = last = None
        for axis, size in enumerate(grid):
            at_start, at_end = pl.program_id(axis) == 0, pl.program_id(axis) == size - 1
            first = at_start if first is None else jnp.logical_and(first, at_start)
            last = at_end if last is None else jnp.logical_and(last, at_end)

        @pl.when(first)
        def _():
            rider.start(r_ins, r_outs, r_scr)

        body(*ins, *outs, *scr)

        @pl.when(last)
        def _():
            rider.finish(r_ins, r_outs, r_scr)

    hbm = pl.BlockSpec(memory_space=pl.ANY)
    call = pl.pallas_call(
        wrapped, out_shape=main_out + list(rider.out_shape), grid=grid, in_specs=list(in_specs) + [hbm] * r_in,
        out_specs=main_specs + [hbm] * r_out, scratch_shapes=list(scratch_shapes) + list(rider.scratch), name=name,
        compiler_params=pltpu.CompilerParams(has_side_effects=True, **params))

    def run(*args):
        res = call(*args, *rider.arrays)
        main = res[:n_out]
        return (main[0] if single else main), res[n_out:]

    return run


def _sds(shape, dtype):
    return jax.ShapeDtypeStruct(tuple(shape), dtype)


def _dot(a, b):
    return jnp.dot(a, b, preferred_element_type=F32)


def _dot_nt(a, b):
    return lax.dot_general(a, b, (((1,), (1,)), ((), ())), preferred_element_type=F32)


def _dot_tn(a, b):
    return lax.dot_general(a, b, (((0,), (0,)), ((), ())), preferred_element_type=F32)


def _dot_hi(a, b):
    return jnp.dot(a, b, preferred_element_type=F32, precision=lax.Precision.HIGHEST)


def _sigmoid(x):
    return 1.0 / (1.0 + jnp.exp(-x))


def _lane_first_half(shape):
    return lax.broadcasted_iota(jnp.int32, shape, len(shape) - 1) < HD


def _pair_sum(x, first):
    s_all = jnp.sum(x, axis=-1, keepdims=True)
    s_a = jnp.sum(jnp.where(first, x, 0.0), axis=-1, keepdims=True)
    return s_a, s_all - s_a


def _rowwise(name, fn, rows, consts, outs, accs=(), tm=512):
    n_rows = None
    in_arrays, in_specs = [], []
    for r in rows:
        if isinstance(r, tuple):
            arr, w, cb = r
            spec = pl.BlockSpec((tm, w), functools.partial(lambda i, cb: (i, cb), cb=cb))
        else:
            arr = r
            spec = pl.BlockSpec((tm, arr.shape[1]), lambda i: (i, 0))
        n_rows = arr.shape[0]
        in_arrays.append(arr)
        in_specs.append(spec)
    for c in consts:
        in_arrays.append(c)
        in_specs.append(pl.BlockSpec(c.shape, functools.partial(lambda i, n: (0,) * n, n=c.ndim)))
    out_shape = [_sds(s, d) for s, d in outs] + [_sds(s, d) for s, d in accs]
    out_specs = [pl.BlockSpec((tm, s[1]), lambda i: (i, 0)) for s, _ in outs]
    out_specs += [pl.BlockSpec(s, functools.partial(lambda i, n: (0,) * n, n=len(s))) for s, _ in accs]

    def body(*refs):
        fn(pl.program_id(0), *refs)

    res = _call(body, name=name, out_shape=out_shape, grid=(n_rows // tm,), in_specs=in_specs,
                out_specs=out_specs)(*in_arrays)
    return res


def rms_fwd(x, gain, name):
    def fn(i, x_ref, g_ref, h_ref):
        xv = x_ref[...]
        r = lax.rsqrt(jnp.mean(xv * xv, axis=-1, keepdims=True) + EPS)
        h_ref[...] = (xv * r * g_ref[...]).astype(h_ref.dtype)

    return _rowwise(name, fn, [x], [gain], [(x.shape, BF16)])[0]


def rms_bwd(dhs, x, gain, dx_in, name):
    n = len(dhs)

    def fn(i, *refs):
        dh_refs, (x_ref, dxin_ref, g_ref, dx_ref, dg_ref) = refs[:n], refs[n:]
        dh = dh_refs[0][...]
        for r in dh_refs[1:]:
            dh = dh + r[...]
        xv = x_ref[...]
        r = lax.rsqrt(jnp.mean(xv * xv, axis=-1, keepdims=True) + EPS)
        xn = xv * r
        dxn = dh * g_ref[...]
        dx_ref[...] = dxin_ref[...] + r * (dxn - xn * jnp.mean(dxn * xn, axis=-1, keepdims=True))

        @pl.when(i == 0)
        def _():
            dg_ref[...] = jnp.zeros_like(dg_ref)

        dg_ref[...] += jnp.sum(dh * xn, axis=0, keepdims=True)

    return _rowwise(name, fn, list(dhs) + [x, dx_in], [gain], [(x.shape, F32)], [((1, x.shape[1]), F32)])


def loss_grad(y, target, name):
    def fn(i, y_ref, t_ref, dy_ref, sq_ref):
        err = y_ref[...] - t_ref[...]
        dy_ref[...] = err * (1.0 / y_ref.shape[1])

        @pl.when(i == 0)
        def _():
            sq_ref[...] = jnp.zeros_like(sq_ref)

        sq_ref[...] += jnp.sum(err * err, axis=0, keepdims=True)

    return _rowwise(name, fn, [y, target], [], [(y.shape, F32)], [((1, y.shape[1]), F32)])


def matmul_nn(a, b, name, out_dtype, tm, tn, res=None, scale=1.0, rider=None):
    s, k = a.shape
    n = b.shape[1]

    def body(*refs):
        if res is None:
            a_ref, b_ref, o_ref = refs
            o_ref[...] = _dot(a_ref[...], b_ref[...]).astype(o_ref.dtype)
        else:
            a_ref, b_ref, r_ref, o_ref = refs
            o_ref[...] = (r_ref[...] + scale * _dot(a_ref[...], b_ref[...])).astype(o_ref.dtype)

    in_specs = [pl.BlockSpec((tm, k), lambda i, j: (i, 0)), pl.BlockSpec((k, tn), lambda i, j: (0, j))]
    args = [a, b]
    if res is not None:
        in_specs.append(pl.BlockSpec((tm, tn), lambda i, j: (i, j)))
        args.append(res)
    return _call(body, name=name, out_shape=_sds((s, n), out_dtype), grid=(s // tm, n // tn), in_specs=in_specs,
                 out_specs=pl.BlockSpec((tm, tn), lambda i, j: (i, j)), rider=rider)(*args)


def matmul_nt(a, b, name, out_dtype, tm, tn, tk, rider=None):
    s, k = a.shape
    n = b.shape[0]
    nk = k // tk

    def body(a_ref, b_ref, o_ref, acc_ref):
        kk = pl.program_id(2)

        @pl.when(kk == 0)
        def _():
            acc_ref[...] = jnp.zeros_like(acc_ref)

        acc_ref[...] += _dot_nt(a_ref[...].astype(BF16), b_ref[...])

        @pl.when(kk == nk - 1)
        def _():
            o_ref[...] = acc_ref[...].astype(o_ref.dtype)

    return _call(body, name=name, out_shape=_sds((s, n), out_dtype), grid=(s // tm, n // tn, nk),
                 in_specs=[pl.BlockSpec((tm, tk), lambda i, j, kk: (i, kk)),
                           pl.BlockSpec((tn, tk), lambda i, j, kk: (j, kk))],
                 out_specs=pl.BlockSpec((tm, tn), lambda i, j, kk: (i, j)),
                 scratch_shapes=[pltpu.VMEM((tm, tn), F32)], rider=rider)(a, b)


def matmul_tn(a, b, name, tn, ts, a_scale=None, b_scale=None, rider=None):
    s, m = a.shape
    n = b.shape[1]
    ns = s // ts

    def body(a_ref, b_ref, o_ref, acc_ref):
        ss = pl.program_id(1)

        @pl.when(ss == 0)
        def _():
            acc_ref[...] = jnp.zeros_like(acc_ref)

        av, bv = a_ref[...], b_ref[...]
        if a_scale is not None:
            av = av * a_scale
        if b_scale is not None:
            bv = bv * b_scale
        acc_ref[...] += _dot_tn(av.astype(BF16), bv.astype(BF16))

        @pl.when(ss == ns - 1)
        def _():
            o_ref[...] = acc_ref[...].astype(o_ref.dtype)

    return _call(body, name=name, out_shape=_sds((m, n), BF16), grid=(n // tn, ns),
                 in_specs=[pl.BlockSpec((ts, m), lambda j, ss: (ss, 0)), pl.BlockSpec((ts, tn), lambda j, ss: (ss, j))],
                 out_specs=pl.BlockSpec((m, tn), lambda j, ss: (0, j)),
                 scratch_shapes=[pltpu.VMEM((m, tn), F32)], rider=rider)(a, b)


def _piece_specs(pieces, tile, rows_tile, tile_axis_first):
    specs, ranges, t0 = [], [], 0
    for a in pieces:
        n = a.shape[1] // tile

        def index(*ids, t0=t0, n=n):
            t, r = (ids[0], ids[1]) if tile_axis_first else (ids[2], ids[0])
            on = jnp.logical_and(t >= t0, t < t0 + n)
            return jnp.where(on, r, 0), jnp.clip(t - t0, 0, n - 1)

        specs.append(pl.BlockSpec((rows_tile, tile), index))
        ranges.append((t0, n))
        t0 += n
    return specs, ranges


def matmul_tn_pieces(a, pieces, name, tn, ts, rider=None):
    s, m = a.shape
    ns = s // ts
    specs, ranges = _piece_specs(pieces, tn, ts, True)
    n_total = sum(n for _, n in ranges)

    def body(a_ref, *refs):
        b_refs, o_ref, acc_ref = refs[:len(pieces)], refs[-2], refs[-1]
        j, ss = pl.program_id(0), pl.program_id(1)

        @pl.when(ss == 0)
        def _():
            acc_ref[...] = jnp.zeros_like(acc_ref)

        for b_ref, (t0, n) in zip(b_refs, ranges):
            @pl.when(jnp.logical_and(j >= t0, j < t0 + n))
            def _(b_ref=b_ref):
                acc_ref[...] += _dot_tn(a_ref[...], b_ref[...])

        @pl.when(ss == ns - 1)
        def _():
            o_ref[...] = acc_ref[...].astype(o_ref.dtype)

    return _call(body, name=name, out_shape=_sds((m, n_total * tn), BF16), grid=(n_total, ns),
                 in_specs=[pl.BlockSpec((ts, m), lambda j, ss: (ss, 0))] + specs,
                 out_specs=pl.BlockSpec((m, tn), lambda j, ss: (0, j)),
                 scratch_shapes=[pltpu.VMEM((m, tn), F32)], rider=rider)(a, *pieces)


def matmul_nt_pieces(pieces, b, name, out_dtype, tm, tn, tk, rider=None):
    s = pieces[0].shape[0]
    n = b.shape[0]
    specs, ranges = _piece_specs(pieces, tk, tm, False)
    nk = sum(cnt for _, cnt in ranges)

    def body(*refs):
        a_refs, b_ref, o_ref, acc_ref = refs[:len(pieces)], refs[-3], refs[-2], refs[-1]
        kk = pl.program_id(2)

        @pl.when(kk == 0)
        def _():
            acc_ref[...] = jnp.zeros_like(acc_ref)

        for a_ref, (t0, cnt) in zip(a_refs, ranges):
            @pl.when(jnp.logical_and(kk >= t0, kk < t0 + cnt))
            def _(a_ref=a_ref):
                acc_ref[...] += _dot_nt(a_ref[...], b_ref[...])

        @pl.when(kk == nk - 1)
        def _():
            o_ref[...] = acc_ref[...].astype(o_ref.dtype)

    return _call(body, name=name, out_shape=_sds((s, n), out_dtype), grid=(s // tm, n // tn, nk),
                 in_specs=specs + [pl.BlockSpec((tn, tk), lambda i, j, kk: (j, kk))],
                 out_specs=pl.BlockSpec((tm, tn), lambda i, j, kk: (i, j)),
                 scratch_shapes=[pltpu.VMEM((tm, tn), F32)], rider=rider)(*pieces, b)


def ffn_up(h, w704, gate_blk, up_blk, name, tm=512, rider=None):
    s = h.shape[0]

    def body(h_ref, wg_ref, wu_ref, g_ref, u_ref, a_ref):
        hv = h_ref[...]
        g = _dot(hv, wg_ref[...])
        u = _dot(hv, wu_ref[...])
        g_ref[...] = g.astype(BF16)
        u_ref[...] = u.astype(BF16)
        a_ref[...] = (g * _sigmoid(g) * u).astype(BF16)

    ospec = pl.BlockSpec((None, tm, FF_SHARD), lambda j, i: (j, i, 0))
    shp = _sds((N_CHIP, s, FF_SHARD), BF16)
    return _call(body, name=name, out_shape=[shp, shp, shp], grid=(N_CHIP, s // tm),
                 in_specs=[pl.BlockSpec((tm, D_MODEL), lambda j, i: (i, 0)),
                           pl.BlockSpec((None, D_MODEL, FF_SHARD), lambda j, i: (j, gate_blk, 0)),
                           pl.BlockSpec((None, D_MODEL, FF_SHARD), lambda j, i: (j, up_blk, 0))],
                 out_specs=[ospec, ospec, ospec], rider=rider)(h, w704, w704)


def ffn_down(a, w1024, blk, x, name, tm=512):
    s = x.shape[0]

    def body(a_ref, wd_ref, x_ref, o_ref):
        acc = _dot(a_ref[0], wd_ref[0])
        for j in range(1, N_CHIP):
            acc += _dot(a_ref[j], wd_ref[j])
        o_ref[...] = x_ref[...] + 0.5 * acc

    return _call(body, name=name, out_shape=_sds((s, D_MODEL), F32), grid=(s // tm,),
                 in_specs=[pl.BlockSpec((N_CHIP, tm, FF_SHARD), lambda i: (0, i, 0)),
                           pl.BlockSpec((N_CHIP, FF_SHARD, D_MODEL), lambda i: (0, blk, 0)),
                           pl.BlockSpec((tm, D_MODEL), lambda i: (i, 0))],
                 out_specs=pl.BlockSpec((tm, D_MODEL), lambda i: (i, 0)))(a, w1024, x)


def ffn_bwd_hidden(dx, w1024, blk, g, u, name, tm=1024, rider=None):
    s = dx.shape[0]

    def body(dx_ref, wd_ref, g_ref, u_ref, dg_ref, du_ref):
        dy = (0.5 * dx_ref[...]).astype(BF16)
        da = _dot_nt(dy, wd_ref[...])
        gv = g_ref[...].astype(F32)
        uv = u_ref[...].astype(F32)
        sg = _sigmoid(gv)
        dg_ref[...] = (da * uv * (sg * (1.0 + gv * (1.0 - sg)))).astype(BF16)
        du_ref[...] = (da * gv * sg).astype(BF16)

    hspec = pl.BlockSpec((None, tm, FF_SHARD), lambda j, i: (j, i, 0))
    shp = _sds((N_CHIP, s, FF_SHARD), BF16)
    return _call(body, name=name, out_shape=[shp, shp], grid=(N_CHIP, s // tm),
                 in_specs=[pl.BlockSpec((tm, D_MODEL), lambda j, i: (i, 0)),
                           pl.BlockSpec((None, FF_SHARD, D_MODEL), lambda j, i: (j, blk, 0)), hspec, hspec],
                 out_specs=[hspec, hspec], rider=rider)(dx, w1024, g, u)


def ffn_bwd_input(dg, du, w704, gate_blk, up_blk, name, tm=512, rider=None):
    s = dg.shape[1]

    def body(dg_ref, du_ref, wg_ref, wu_ref, o_ref):
        acc = _dot_nt(dg_ref[0], wg_ref[0]) + _dot_nt(du_ref[0], wu_ref[0])
        for j in range(1, N_CHIP):
            acc += _dot_nt(dg_ref[j], wg_ref[j]) + _dot_nt(du_ref[j], wu_ref[j])
        o_ref[...] = acc

    hspec = pl.BlockSpec((N_CHIP, tm, FF_SHARD), lambda i: (0, i, 0))
    return _call(body, name=name, out_shape=_sds((s, D_MODEL), F32), grid=(s // tm,),
                 in_specs=[hspec, hspec,
                           pl.BlockSpec((N_CHIP, D_MODEL, FF_SHARD), lambda i: (0, gate_blk, 0), pl.Buffered(1)),
                           pl.BlockSpec((N_CHIP, D_MODEL, FF_SHARD), lambda i: (0, up_blk, 0), pl.Buffered(1))],
                 out_specs=pl.BlockSpec((tm, D_MODEL), lambda i: (i, 0)), rider=rider)(dg, du, w704, w704)


def ffn_wgrad_in(h, dgu, name, ts=1024):
    s = h.shape[0]
    ns = s // ts

    def body(h_ref, d_ref, o_ref, acc_ref):
        ss = pl.program_id(1)

        @pl.when(ss == 0)
        def _():
            acc_ref[...] = jnp.zeros_like(acc_ref)

        acc_ref[...] += _dot_tn(h_ref[...], d_ref[...])

        @pl.when(ss == ns - 1)
        def _():
            o_ref[...] = acc_ref[...].astype(BF16)

    return _call(body, name=name, out_shape=_sds((N_CHIP, D_MODEL, FF_SHARD), BF16), grid=(N_CHIP, ns),
                 in_specs=[pl.BlockSpec((ts, D_MODEL), lambda j, ss: (ss, 0)),
                           pl.BlockSpec((None, ts, FF_SHARD), lambda j, ss: (j, ss, 0))],
                 out_specs=pl.BlockSpec((None, D_MODEL, FF_SHARD), lambda j, ss: (j, 0, 0)),
                 scratch_shapes=[pltpu.VMEM((D_MODEL, FF_SHARD), F32)])(h, dgu)


def ffn_wgrad_down(a, dx, name, ts=1024):
    s = dx.shape[0]
    ns = s // ts

    def body(a_ref, dx_ref, o_ref, acc_ref):
        ss = pl.program_id(1)

        @pl.when(ss == 0)
        def _():
            acc_ref[...] = jnp.zeros_like(acc_ref)

        acc_ref[...] += _dot_tn(a_ref[...], (0.5 * dx_ref[...]).astype(BF16))

        @pl.when(ss == ns - 1)
        def _():
            o_ref[...] = acc_ref[...].astype(BF16)

    return _call(body, name=name, out_shape=_sds((N_CHIP, FF_SHARD, D_MODEL), BF16), grid=(N_CHIP, ns),
                 in_specs=[pl.BlockSpec((None, ts, FF_SHARD), lambda j, ss: (j, ss, 0)),
                           pl.BlockSpec((ts, D_MODEL), lambda j, ss: (ss, 0))],
                 out_specs=pl.BlockSpec((None, FF_SHARD, D_MODEL), lambda j, ss: (j, 0, 0)),
                 scratch_shapes=[pltpu.VMEM((FF_SHARD, D_MODEL), F32)])(a, dx)


def ffn_forward(x, gain, w704, get_w1024, tag, rider=None):
    h = rms_fwd(x, gain, f"{tag}_rms")
    res = ffn_up(h, w704, 0, 1, f"{tag}_up", rider=rider)
    (g, u, a), rode = res if rider is not None else (res, None)
    y = ffn_down(a, get_w1024(rode), 0, x, f"{tag}_down")
    return y, (h, g, u, a), rode


def ffn_backward(dy, x, gain, w704, w1024, saved, tag, ride_down=None, ride_in=None):
    h, g, u, a = saved
    d_wd = ffn_wgrad_down(a, dy, f"{tag}_dwd")
    if ride_down is not None:
        (dg, du), d_wd = ffn_bwd_hidden(dy, w1024, 0, g, u, f"{tag}_dhid", rider=ride_down(d_wd))
    else:
        dg, du = ffn_bwd_hidden(dy, w1024, 0, g, u, f"{tag}_dhid")
    d_win = jnp.concatenate([ffn_wgrad_in(h, dg, f"{tag}_dwg"), ffn_wgrad_in(h, du, f"{tag}_dwu")], axis=1)
    if ride_in is not None:
        dh, d_win = ffn_bwd_input(dg, du, w704, 0, 1, f"{tag}_dh", rider=ride_in(d_win))
    else:
        dh = ffn_bwd_input(dg, du, w704, 0, 1, f"{tag}_dh")
    dx, d_gain = rms_bwd([dh], x, gain, dy, f"{tag}_drms")
    return dx, d_gain, d_win, d_wd


def _alibi_slope(head):
    return float(2.0 ** (-ALIBI_MAX_EXP * (head + 1) / N_ATTN_HEADS))


def _head_norm(t, gain_pair, first):
    sa, sb = _pair_sum(t * t, first)
    r = jnp.where(first, lax.rsqrt(sa * (1.0 / HD) + EPS), lax.rsqrt(sb * (1.0 / HD) + EPS))
    return t * r * gain_pair, r


def qk_norm_fwd(p, q_gain, k_gain, name):
    s = p.shape[0]

    def fn(i, q_ref, k_ref, qg_ref, kg_ref, qn_ref, kn_ref):
        first = _lane_first_half((q_ref.shape[0], 2 * HD))
        for src, g_ref, dst in ((q_ref, qg_ref, qn_ref), (k_ref, kg_ref, kn_ref)):
            for pr in range(ATTN_QKV // (2 * HD)):
                cols = slice(pr * 2 * HD, (pr + 1) * 2 * HD)
                y, _ = _head_norm(src[:, cols].astype(F32), g_ref[...], first)
                dst[:, cols] = y.astype(BF16)

    return _rowwise(name, fn, [(p, ATTN_QKV, 0), (p, ATTN_QKV, 1)], [q_gain, k_gain],
                    [((s, ATTN_QKV), BF16), ((s, ATTN_QKV), BF16)])


def qk_norm_bwd(p, dqs, dks, q_gain, k_gain, name):
    s = p.shape[0]
    pairs_per_pattern = GROUP_W // (2 * HD)

    def fn(i, q_ref, k_ref, dq0, dq1, dq2, dk0, dk1, dk2, qg_ref, kg_ref, dqk_ref, dqg_ref, dkg_ref):
        first = _lane_first_half((q_ref.shape[0], 2 * HD))

        @pl.when(i == 0)
        def _():
            dqg_ref[...] = jnp.zeros_like(dqg_ref)
            dkg_ref[...] = jnp.zeros_like(dkg_ref)

        for src, d_refs, g_ref, dst, dg_ref in (
                (q_ref, (dq0, dq1, dq2), qg_ref, dqk_ref.at[:, 0:ATTN_QKV], dqg_ref),
                (k_ref, (dk0, dk1, dk2), kg_ref, dqk_ref.at[:, ATTN_QKV:2 * ATTN_QKV], dkg_ref)):
            for pr in range(ATTN_QKV // (2 * HD)):
                cols = slice(pr * 2 * HD, (pr + 1) * 2 * HD)
                t = src[:, cols].astype(F32)
                sa, sb = _pair_sum(t * t, first)
                r = jnp.where(first, lax.rsqrt(sa * (1.0 / HD) + EPS), lax.rsqrt(sb * (1.0 / HD) + EPS))
                xn = t * r
                within = (pr % pairs_per_pattern) * 2 * HD
                dy = d_refs[pr // pairs_per_pattern][:, within:within + 2 * HD]
                dg_ref[:, cols] += jnp.sum(dy * xn, axis=0, keepdims=True)
                dxn = dy * g_ref[...]
                ma, mb = _pair_sum(dxn * xn, first)
                mean = jnp.where(first, ma, mb) * (1.0 / HD)
                dst[:, cols] = (r * (dxn - xn * mean)).astype(BF16)

    return _rowwise(name, fn, [(p, ATTN_QKV, 0), (p, ATTN_QKV, 1)] + list(dqs) + list(dks), [q_gain, k_gain],
                    [((s, 2 * ATTN_QKV), BF16)], [((1, ATTN_QKV), F32), ((1, ATTN_QKV), F32)])


def _to_streams(a, d):
    if d == 1:
        return a
    s, c = a.shape
    return a.reshape(s // d, d, c).transpose(1, 0, 2).reshape(s, c)


def _from_streams(a, d):
    if d == 1:
        return a
    s, c = a.shape
    return a.reshape(d, s // d, c).transpose(1, 0, 2).reshape(s, c)


def _attn_masks():
    row = lax.broadcasted_iota(jnp.int32, (BLK, BLK), 0)
    col = lax.broadcasted_iota(jnp.int32, (BLK, BLK), 1)
    rel_diag = row - col
    rel_prev = rel_diag + BLK
    return rel_diag, rel_prev


def attn_fwd(q, k, v, pattern, name, tq=512):
    s = q.shape[0]
    d = ATTN_DILATIONS[pattern]
    blocks_per_stream = (s // d) // BLK
    nsb = tq // BLK

    def body(q_ref, k_ref, v_ref, kp_ref, vp_ref, o_ref, l_ref):
        i = pl.program_id(0)
        rel_diag, rel_prev = _attn_masks()
        first = _lane_first_half((BLK, 2 * HD))
        rd_f = (rel_diag * d).astype(F32)
        rp_f = (rel_prev * d).astype(F32)
        for sb in range(nsb):
            rows = slice(sb * BLK, (sb + 1) * BLK)
            has_prev = ((i * nsb + sb) % blocks_per_stream != 0).astype(jnp.int32)
            m_diag = rel_diag >= 0
            m_prev = (rel_prev + (1 - has_prev) * (4 * BLK)) <= BLK
            for pr in range(GROUP_W // (2 * HD)):
                cols = slice(pr * 2 * HD, (pr + 1) * 2 * HD)
                qp = q_ref[rows, cols]
                kc, vc = k_ref[rows, cols], v_ref[rows, cols]
                if sb == 0:
                    kp, vp = kp_ref[:, cols], vp_ref[:, cols]
                else:
                    prows = slice((sb - 1) * BLK, sb * BLK)
                    kp, vp = k_ref[prows, cols], v_ref[prows, cols]
                outs, lses = [], []
                for e in range(2):
                    slope = _alibi_slope(pattern * HEADS_PER_PATTERN + 2 * pr + e)
                    qm = jnp.where(first if e == 0 else jnp.logical_not(first), qp, jnp.zeros_like(qp))
                    s1 = jnp.where(m_diag, _dot_nt(qm, kc) * 0.125 - slope * rd_f, NEG)
                    s0 = jnp.where(m_prev, _dot_nt(qm, kp) * 0.125 - slope * rp_f, NEG)
                    m = jnp.maximum(jnp.max(s1, axis=-1, keepdims=True), jnp.max(s0, axis=-1, keepdims=True))
                    p1 = jnp.exp(s1 - m)
                    p0 = jnp.exp(s0 - m)
                    l = jnp.sum(p1, axis=-1, keepdims=True) + jnp.sum(p0, axis=-1, keepdims=True)
                    inv = 1.0 / l
                    outs.append(_dot((p1 * inv).astype(BF16), vc) + _dot((p0 * inv).astype(BF16), vp))
                    lses.append(m + jnp.log(l))
                o_ref[rows, cols] = jnp.where(first, outs[0], outs[1])
                l_ref[rows, cols] = jnp.where(first, lses[0], lses[1])

    cur = pl.BlockSpec((tq, GROUP_W), lambda i: (i, 0))
    prev = pl.BlockSpec((BLK, GROUP_W), lambda i: (jnp.maximum(i * nsb - 1, 0), 0))
    return _call(body, name=name, out_shape=[_sds((s, GROUP_W), F32), _sds((s, GROUP_W), F32)], grid=(s // tq,),
                 in_specs=[cur, cur, cur, prev, prev], out_specs=[cur, cur])(q, k, v, k, v)


def attn_merge_fwd(os_, lses, name):
    s = os_[0].shape[0]

    def fn(i, o0, o1, o2, l0, l1, l2, out_ref):
        m = jnp.maximum(jnp.maximum(l0[...], l1[...]), l2[...])
        e0, e1, e2 = jnp.exp(l0[...] - m), jnp.exp(l1[...] - m), jnp.exp(l2[...] - m)
        inv = 1.0 / (e0 + e1 + e2)
        out_ref[...] = ((e0 * inv) * o0[...] + (e1 * inv) * o1[...] + (e2 * inv) * o2[...]).astype(BF16)

    return _rowwise(name, fn, list(os_) + list(lses), [], [((s, GROUP_W), BF16)])[0]


def attn_merge_bwd(d_out, os_, lses, name):
    s = d_out.shape[0]

    def fn(i, do_ref, o0, o1, o2, l0, l1, l2, d0, d1, d2, c0, c1, c2):
        first = _lane_first_half((do_ref.shape[0], 2 * HD))
        m = jnp.maximum(jnp.maximum(l0[...], l1[...]), l2[...])
        e0, e1, e2 = jnp.exp(l0[...] - m), jnp.exp(l1[...] - m), jnp.exp(l2[...] - m)
        inv = 1.0 / (e0 + e1 + e2)
        w0, w1, w2 = e0 * inv, e1 * inv, e2 * inv
        do = do_ref[...]
        prod = do * (w0 * o0[...] + w1 * o1[...] + w2 * o2[...])
        for pr in range(GROUP_W // (2 * HD)):
            cols = slice(pr * 2 * HD, (pr + 1) * 2 * HD)
            ta, tb = _pair_sum(prod[:, cols], first)
            t = jnp.where(first, ta, tb)
            for w, c_ref in ((w0, c0), (w1, c1), (w2, c2)):
                c_ref[:, cols] = w[:, cols] * t
        for w, d_ref in ((w0, d0), (w1, d1), (w2, d2)):
            d_ref[...] = (w * do).astype(BF16)

    shp = (s, GROUP_W)
    return _rowwise(name, fn, [d_out] + list(os_) + list(lses), [],
                    [(shp, BF16)] * 3 + [(shp, F32)] * 3)


def attn_bwd(q, k, v, d_o, cterm, lse, pattern, name, tq=512):
    s = q.shape[0]
    d = ATTN_DILATIONS[pattern]
    blocks_per_stream = (s // d) // BLK
    nsb = tq // BLK
    n_blocks = s // BLK

    def body(q_ref, k_ref, v_ref, do_ref, c_ref, l_ref, kp_ref, vp_ref, qn_ref, don_ref, cn_ref, ln_ref,
             dq_ref, dk_ref, dv_ref):
        i = pl.program_id(0)
        rel_diag, rel_prev = _attn_masks()
        first = _lane_first_half((BLK, 2 * HD))
        second = jnp.logical_not(first)
        rd_f = (rel_diag * d).astype(F32)
        rp_f = (rel_prev * d).astype(F32)
        m_diag = rel_diag >= 0
        dq_ref[...] = jnp.zeros_like(dq_ref)
        dk_ref[...] = jnp.zeros_like(dk_ref)
        dv_ref[...] = jnp.zeros_like(dv_ref)

        def pair(qp, dop, cp, lp, kp, vp, rel_f, mask):
            dq = dk = dv = None
            for e in range(2):
                lanes = first if e == 0 else second
                slope = slopes[e]
                qm = jnp.where(lanes, qp, jnp.zeros_like(qp))
                dom = jnp.where(lanes, dop, jnp.zeros_like(dop))
                km = jnp.where(lanes, kp, jnp.zeros_like(kp))
                sc = jnp.where(mask, _dot_nt(qm, kp) * 0.125 - slope * rel_f, NEG)
                pm = jnp.exp(sc - lp[:, e * HD:e * HD + 1])
                dl = pm * (_dot_nt(dom, vp) - cp[:, e * HD:e * HD + 1])
                dl16 = dl.astype(BF16)
                t_dq = _dot(dl16, km)
                t_dk = _dot_tn(dl16, qm)
                t_dv = _dot_tn(pm.astype(BF16), dom)
                dq = t_dq if dq is None else dq + t_dq
                dk = t_dk if dk is None else dk + t_dk
                dv = t_dv if dv is None else dv + t_dv
            return dq * 0.125, dk * 0.125, dv

        for pr in range(GROUP_W // (2 * HD)):
            cols = slice(pr * 2 * HD, (pr + 1) * 2 * HD)
            slopes = [_alibi_slope(pattern * HEADS_PER_PATTERN + 2 * pr + e) for e in range(2)]
            for sb in range(nsb + 1):
                gb = i * nsb + sb
                if sb < nsb:
                    rows = slice(sb * BLK, (sb + 1) * BLK)
                    qp, dop, cp, lp = q_ref[rows, cols], do_ref[rows, cols], c_ref[rows, cols], l_ref[rows, cols]
                else:
                    qp, dop, cp, lp = qn_ref[:, cols], don_ref[:, cols], cn_ref[:, cols], ln_ref[:, cols]
                if sb < nsb:
                    dq1, dk1, dv1 = pair(qp, dop, cp, lp, k_ref[rows, cols], v_ref[rows, cols], rd_f, m_diag)
                    dq_ref[rows, cols] += dq1
                    dk_ref[rows, cols] += dk1
                    dv_ref[rows, cols] += dv1
                valid = jnp.logical_and(gb % blocks_per_stream != 0, gb < n_blocks).astype(jnp.int32)
                m_prev = jnp.logical_and(rel_prev <= BLK, (rel_prev + (1 - valid) * (4 * BLK)) <= BLK)
                if sb == 0:
                    kp, vp = kp_ref[:, cols], vp_ref[:, cols]
                else:
                    prows = slice((sb - 1) * BLK, sb * BLK)
                    kp, vp = k_ref[prows, cols], v_ref[prows, cols]
                dq0, dk0, dv0 = pair(qp, dop, cp, lp, kp, vp, rp_f, m_prev)
                if sb < nsb:
                    dq_ref[rows, cols] += dq0
                if sb > 0:
                    dk_ref[prows, cols] += dk0
                    dv_ref[prows, cols] += dv0

    cur = pl.BlockSpec((tq, GROUP_W), lambda i: (i, 0))
    prev = pl.BlockSpec((BLK, GROUP_W), lambda i: (jnp.maximum(i * nsb - 1, 0), 0))
    nxt = pl.BlockSpec((BLK, GROUP_W), lambda i: (jnp.minimum((i + 1) * nsb, n_blocks - 1), 0))
    shp = _sds((s, GROUP_W), F32)
    return _call(body, name=name, out_shape=[shp, shp, shp], grid=(s // tq,),
                 in_specs=[cur] * 6 + [prev, prev] + [nxt] * 4, out_specs=[cur, cur, cur])(
                     q, k, v, d_o, cterm, lse, k, v, q, d_o, cterm, lse)


def _band_constants(d):
    row = lax.broadcasted_iota(jnp.int32, (2 * BLK, 2 * BLK), 0)
    col = lax.broadcasted_iota(jnp.int32, (2 * BLK, 2 * BLK), 1)
    rel = BLK + jnp.where(row >= BLK, row - BLK, row) - col
    band = jnp.logical_and(rel >= 0, rel <= BLK)
    return (rel * d).astype(F32), band, (col >= BLK).astype(jnp.int32)


def _stack_heads(x, first):
    zero = jnp.zeros_like(x)
    return jnp.concatenate([jnp.where(first, x, zero), jnp.where(first, zero, x)], axis=0)


def _unstack_heads(x2, first):
    return jnp.where(first, x2[:BLK], x2[BLK:])


def _head_column(x):
    return jnp.concatenate([x[:, 0:1], x[:, HD:HD + 1]], axis=0)


def attn_fwd2(q, k, v, pattern, name, tq=512):
    s = q.shape[0]
    d = ATTN_DILATIONS[pattern]
    blocks_per_stream = (s // d) // BLK
    nsb = tq // BLK

    def body(q_ref, k_ref, v_ref, kp_ref, vp_ref, o_ref, l_ref):
        i = pl.program_id(0)
        rel_f, band, own = _band_constants(d)
        first = _lane_first_half((BLK, 2 * HD))
        upper = lax.broadcasted_iota(jnp.int32, (2 * BLK, 1), 0) < BLK
        for sb in range(nsb):
            rows = slice(sb * BLK, (sb + 1) * BLK)
            has_prev = ((i * nsb + sb) % blocks_per_stream != 0).astype(jnp.int32)
            mask = jnp.logical_and(band, (own + has_prev) > 0)
            for pr in range(GROUP_W // (2 * HD)):
                cols = slice(pr * 2 * HD, (pr + 1) * 2 * HD)
                if sb == 0:
                    kcat = jnp.concatenate([kp_ref[:, cols], k_ref[rows, cols]], axis=0)
                    vcat = jnp.concatenate([vp_ref[:, cols], v_ref[rows, cols]], axis=0)
                else:
                    both = slice((sb - 1) * BLK, (sb + 1) * BLK)
                    kcat, vcat = k_ref[both, cols], v_ref[both, cols]
                h0 = pattern * HEADS_PER_PATTERN + 2 * pr
                slope = jnp.where(upper, _alibi_slope(h0), _alibi_slope(h0 + 1))
                sc = _dot_nt(_stack_heads(q_ref[rows, cols], first), kcat) * 0.125 - slope * rel_f
                sc = jnp.where(mask, sc, NEG)
                m = jnp.max(sc, axis=-1, keepdims=True)
                p = jnp.exp(sc - m)
                l = jnp.sum(p, axis=-1, keepdims=True)
                o2 = _dot((p * (1.0 / l)).astype(BF16), vcat)
                o_ref[rows, cols] = _unstack_heads(o2, first)
                lse = m + jnp.log(l)
                l_ref[rows, cols] = jnp.where(first, lse[:BLK], lse[BLK:])

    cur = pl.BlockSpec((tq, GROUP_W), lambda i: (i, 0))
    prev = pl.BlockSpec((BLK, GROUP_W), lambda i: (jnp.maximum(i * nsb - 1, 0), 0))
    return _call(body, name=name, out_shape=[_sds((s, GROUP_W), F32), _sds((s, GROUP_W), F32)], grid=(s // tq,),
                 in_specs=[cur, cur, cur, prev, prev], out_specs=[cur, cur])(q, k, v, k, v)


def attn_bwd2(q, k, v, d_o, cterm, lse, pattern, name, tq=512):
    s = q.shape[0]
    d = ATTN_DILATIONS[pattern]
    blocks_per_stream = (s // d) // BLK
    nsb = tq // BLK
    n_blocks = s // BLK

    def body(q_ref, k_ref, v_ref, do_ref, c_ref, l_ref, kp_ref, vp_ref, qn_ref, kn_ref, vn_ref, don_ref, cn_ref,
             ln_ref, dq_ref, dk_ref, dv_ref):
        i = pl.program_id(0)
        rel_f, band, own = _band_constants(d)
        first = _lane_first_half((BLK, 2 * HD))
        upper = lax.broadcasted_iota(jnp.int32, (2 * BLK, 1), 0) < BLK
        dk_ref[...] = jnp.zeros_like(dk_ref)
        dv_ref[...] = jnp.zeros_like(dv_ref)
        for sb in range(nsb + 1):
            gb = i * nsb + sb
            rows = slice(sb * BLK, (sb + 1) * BLK)
            before = slice((sb - 1) * BLK, sb * BLK)
            inside = (gb < n_blocks).astype(jnp.int32)
            has_prev = jnp.logical_and(gb % blocks_per_stream != 0, gb < n_blocks).astype(jnp.int32)
            mask = jnp.logical_and(band, (own * inside + has_prev) > 0)
            for pr in range(GROUP_W // (2 * HD)):
                cols = slice(pr * 2 * HD, (pr + 1) * 2 * HD)
                if sb == 0:
                    kcat = jnp.concatenate([kp_ref[:, cols], k_ref[rows, cols]], axis=0)
                    vcat = jnp.concatenate([vp_ref[:, cols], v_ref[rows, cols]], axis=0)
                elif sb == nsb:
                    kcat = jnp.concatenate([k_ref[before, cols], kn_ref[:, cols]], axis=0)
                    vcat = jnp.concatenate([v_ref[before, cols], vn_ref[:, cols]], axis=0)
                else:
                    both = slice((sb - 1) * BLK, (sb + 1) * BLK)
                    kcat, vcat = k_ref[both, cols], v_ref[both, cols]
                if sb < nsb:
                    qp, dop, cp, lp = q_ref[rows, cols], do_ref[rows, cols], c_ref[rows, cols], l_ref[rows, cols]
                else:
                    qp, dop, cp, lp = qn_ref[:, cols], don_ref[:, cols], cn_ref[:, cols], ln_ref[:, cols]
                h0 = pattern * HEADS_PER_PATTERN + 2 * pr
                slope = jnp.where(upper, _alibi_slope(h0), _alibi_slope(h0 + 1))
                q2 = _stack_heads(qp, first)
                do2 = _stack_heads(dop, first)
                sc = jnp.where(mask, _dot_nt(q2, kcat) * 0.125 - slope * rel_f, NEG)
                pm = jnp.exp(sc - _head_column(lp))
                dl = (pm * (_dot_nt(do2, vcat) - _head_column(cp))).astype(BF16)
                if sb < nsb:
                    dq_ref[rows, cols] = _unstack_heads(_dot(dl, kcat), first) * 0.125
                dk2 = _dot_tn(dl, q2) * 0.125
                dv2 = _dot_tn(pm.astype(BF16), do2)
                if sb > 0:
                    dk_ref[before, cols] += dk2[:BLK]
                    dv_ref[before, cols] += dv2[:BLK]
                if sb < nsb:
                    dk_ref[rows, cols] += dk2[BLK:]
                    dv_ref[rows, cols] += dv2[BLK:]

    cur = pl.BlockSpec((tq, GROUP_W), lambda i: (i, 0))
    prev = pl.BlockSpec((BLK, GROUP_W), lambda i: (jnp.maximum(i * nsb - 1, 0), 0))
    nxt = pl.BlockSpec((BLK, GROUP_W), lambda i: (jnp.minimum((i + 1) * nsb, n_blocks - 1), 0))
    shp = _sds((s, GROUP_W), F32)
    return _call(body, name=name, out_shape=[shp, shp, shp], grid=(s // tq,),
                 in_specs=[cur] * 6 + [prev, prev] + [nxt] * 6, out_specs=[cur, cur, cur])(
                     q, k, v, d_o, cterm, lse, k, v, q, k, v, d_o, cterm, lse)


HALO = 16
CONV_TQ = 512


def conv_fwd(p, w, b, name):
    s = p.shape[0]
    tq = CONV_TQ
    ncol = SSD_CONV_DIM // GROUP_W
    cb0 = COL_XBC // GROUP_W

    def body(u_ref, up_ref, w_ref, b_ref, c_ref, xc_ref):
        i = pl.program_id(0)
        prev = up_ref[...].astype(F32) * (i > 0).astype(F32)
        ext = jnp.concatenate([prev, u_ref[...].astype(F32)], axis=0)
        acc = b_ref[...] + w_ref[SSD_CONV - 1:SSD_CONV, :] * ext[HALO:HALO + tq]
        for kk in range(SSD_CONV - 1):
            acc += w_ref[kk:kk + 1, :] * pltpu.roll(ext, SSD_CONV - 1 - kk, 0)[HALO:HALO + tq]
        c_ref[...] = acc.astype(BF16)
        xc_ref[...] = (acc * _sigmoid(acc)).astype(BF16)

    cur_in = pl.BlockSpec((tq, GROUP_W), lambda i, j: (i, cb0 + j))
    prev_in = pl.BlockSpec((HALO, GROUP_W), lambda i, j: (jnp.maximum(i * (tq // HALO) - 1, 0), cb0 + j))
    cur_out = pl.BlockSpec((tq, GROUP_W), lambda i, j: (i, j))
    shp = _sds((s, SSD_CONV_DIM), BF16)
    return _call(body, name=name, out_shape=[shp, shp], grid=(s // tq, ncol),
                 in_specs=[cur_in, prev_in, pl.BlockSpec((SSD_CONV, GROUP_W), lambda i, j: (0, j)),
                           pl.BlockSpec((1, GROUP_W), lambda i, j: (0, j))],
                 out_specs=[cur_out, cur_out])(p, p, w, b)


def conv_bwd(p, cpre, dxs, d_b, d_c, w, name):
    s = p.shape[0]
    tq = CONV_TQ
    ncol = SSD_CONV_DIM // GROUP_W
    n_xs = SSD_INNER // GROUP_W
    cb0 = COL_XBC // GROUP_W
    nt = s // tq

    def body(u_ref, up_ref, c_ref, cn_ref, dx_ref, dxn_ref, dbm_ref, dbmn_ref, dcm_ref, dcmn_ref, w_ref,
             du_ref, dw_ref, db_ref):
        j, i = pl.program_id(0), pl.program_id(1)

        def dpre(c16, dx):
            c = c16.astype(F32)
            sg = _sigmoid(c)
            return dx * (sg * (1.0 + c * (1.0 - sg)))

        def pick(a_ref, b_ref, c_ref_):
            return jnp.where(j < n_xs, a_ref[...], jnp.where(j == n_xs, b_ref[...], c_ref_[...]))

        dc = dpre(c_ref[...], pick(dx_ref, dbm_ref, dcm_ref))
        dcn = dpre(cn_ref[...], pick(dxn_ref, dbmn_ref, dcmn_ref)) * (i < nt - 1).astype(F32)
        dext = jnp.concatenate([dc, dcn], axis=0)
        prev = up_ref[...].astype(F32) * (i > 0).astype(F32)
        uext = jnp.concatenate([prev, u_ref[...].astype(F32)], axis=0)

        @pl.when(i == 0)
        def _():
            dw_ref[...] = jnp.zeros_like(dw_ref)
            db_ref[...] = jnp.zeros_like(db_ref)

        du = w_ref[SSD_CONV - 1:SSD_CONV, :] * dc
        for kk in range(SSD_CONV - 1):
            sh = SSD_CONV - 1 - kk
            du += w_ref[kk:kk + 1, :] * pltpu.roll(dext, tq + HALO - sh, 0)[0:tq]
        du_ref[...] = du.astype(BF16)
        for kk in range(SSD_CONV):
            shifted = uext if kk == SSD_CONV - 1 else pltpu.roll(uext, SSD_CONV - 1 - kk, 0)
            dw_ref[kk:kk + 1, :] += jnp.sum(dc * shifted[HALO:HALO + tq], axis=0, keepdims=True)
        db_ref[...] += jnp.sum(dc, axis=0, keepdims=True)

    hb = tq // HALO
    cur_p = pl.BlockSpec((tq, GROUP_W), lambda j, i: (i, cb0 + j))
    prev_p = pl.BlockSpec((HALO, GROUP_W), lambda j, i: (jnp.maximum(i * hb - 1, 0), cb0 + j))
    cur = pl.BlockSpec((tq, GROUP_W), lambda j, i: (i, j))
    nxt = pl.BlockSpec((HALO, GROUP_W), lambda j, i: (jnp.minimum((i + 1) * hb, s // HALO - 1), j))

    def piece(first_tile, n_tiles):
        def on(j):
            return jnp.logical_and(j >= first_tile, j < first_tile + n_tiles)

        def col(j):
            return jnp.clip(j - first_tile, 0, n_tiles - 1)

        return (pl.BlockSpec((tq, GROUP_W), lambda j, i: (jnp.where(on(j), i, 0), col(j))),
                pl.BlockSpec((HALO, GROUP_W),
                             lambda j, i: (jnp.where(on(j), jnp.minimum((i + 1) * hb, s // HALO - 1), 0), col(j))))

    return _call(body, name=name,
                 out_shape=[_sds((s, SSD_CONV_DIM), BF16), _sds((8, SSD_CONV_DIM), F32), _sds((1, SSD_CONV_DIM), F32)],
                 grid=(ncol, nt),
                 in_specs=[cur_p, prev_p, cur, nxt, *piece(0, n_xs), *piece(n_xs, 1), *piece(n_xs + 1, 1),
                           pl.BlockSpec((SSD_CONV, GROUP_W), lambda j, i: (0, j))],
                 out_specs=[cur, pl.BlockSpec((8, GROUP_W), lambda j, i: (0, j)),
                            pl.BlockSpec((1, GROUP_W), lambda j, i: (0, j))])(
                                p, p, cpre, cpre, dxs, dxs, d_b, d_b, d_c, d_c, w)


def _softplus(x):
    return jnp.maximum(x, 0.0) + jnp.log(1.0 + jnp.exp(-jnp.abs(x)))


def _ssd_decays(dtr_ref, dtrt_ref, bias_ref, biast_ref, alog_ref, alogt_ref):
    row = lax.broadcasted_iota(jnp.int32, (BLK, BLK), 0)
    col = lax.broadcasted_iota(jnp.int32, (BLK, BLK), 1)
    lower = (row >= col).astype(F32)
    upper = (row <= col).astype(F32)
    dtb = dtr_ref[...] + bias_ref[...]
    dt = _softplus(dtb)
    a = dt * (-jnp.exp(alog_ref[...]))
    cs = _dot_hi(lower, a)
    a_t = _softplus(dtrt_ref[...] + biast_ref[...]) * (-jnp.exp(alogt_ref[...]))
    cs_t = _dot_hi(a_t, upper)
    return dtb, dt, cs, cs_t, row, col, upper


SSD_GROUPS_PER_STEP = 4


def _per_group(body, gps, kinds):
    def wrapped(*refs):
        for gi in range(gps):
            args, pos = [], 0
            for kind, n in kinds:
                if kind == "each":
                    args.append(refs[pos + gi])
                    pos += gps
                    continue
                ref = refs[pos]
                pos += 1
                if kind == "cols":
                    args.append(ref.at[:, gi * n:(gi + 1) * n])
                else:
                    args.append(ref.at[gi] if n == 1 else ref.at[pl.ds(gi * n, n)])
            body(*args)

    return wrapped


def ssd_fwd(p, xc, dtg, dtg_t, params, gn, name):
    s = p.shape[0]
    nc = s // BLK
    bias, bias_t, alog, alog_t, dskip = params

    def body(xs_ref, b_ref, c_ref, z_ref, dtr_ref, dtrt_ref, bias_ref, biast_ref, alog_ref, alogt_ref, dsk_ref,
             gn_ref, y_ref, sin_ref, hp_ref, h_ref):
        c_idx = pl.program_id(1)

        @pl.when(c_idx == 0)
        def _():
            h_ref[...] = jnp.zeros_like(h_ref)

        _, dt, cs, cs_t, row, col, _ = _ssd_decays(dtr_ref, dtrt_ref, bias_ref, biast_ref, alog_ref, alogt_ref)
        first = _lane_first_half((BLK, 2 * HD))
        first_row = _lane_first_half((1, 2 * HD))
        tril = row >= col
        b16, c16 = b_ref[...], c_ref[...]
        cb = _dot_nt(c16, b16)
        n_pairs = GROUP_W // (2 * HD)
        tot = cs[BLK - 1:BLK, :]
        exp_cs, exp_rest, exp_tot = jnp.exp(cs), jnp.exp(tot - cs), jnp.exp(tot)

        def per_head(v, mask):
            return jnp.concatenate([jnp.where(mask, v[:, 2 * pr:2 * pr + 1], v[:, 2 * pr + 1:2 * pr + 2])
                                    for pr in range(n_pairs)], axis=1)

        xs = xs_ref[...].astype(F32)
        xt = xs * per_head(dt, first)
        xt16 = xt.astype(BF16)
        hstate = jnp.concatenate([h_ref[pr] for pr in range(n_pairs)], axis=1)
        for pr in range(n_pairs):
            hp_ref[pr] = h_ref[pr]
        y_off = per_head(exp_cs, first) * _dot(c16, hstate.astype(BF16))
        new = per_head(exp_tot, first_row) * hstate + _dot_tn(b16, (per_head(exp_rest, first) * xt).astype(BF16))
        for pr in range(n_pairs):
            h_ref[pr] = new[:, pr * 2 * HD:(pr + 1) * 2 * HD]
        y_diag = []
        for pr in range(n_pairs):
            cols = slice(pr * 2 * HD, (pr + 1) * 2 * HD)
            m2 = jnp.concatenate(
                [(cb * jnp.exp(jnp.where(tril, cs[:, h:h + 1] - cs_t[h:h + 1, :], NEG))).astype(BF16)
                 for h in (2 * pr, 2 * pr + 1)], axis=1)
            y_diag.append(_dot(m2, _stack_heads(xt16[:, cols], first)))
        y = jnp.concatenate(y_diag, axis=1) + y_off + xs * per_head(dsk_ref[...], first_row)
        y_ref[...] = y
        zv = z_ref[...].astype(F32)
        yz = y * (zv * _sigmoid(zv))
        r = lax.rsqrt(jnp.mean(yz * yz, axis=-1, keepdims=True) + EPS)
        sin_ref[...] = (yz * r * gn_ref[...]).astype(BF16)

    gps = SSD_GROUPS_PER_STEP
    wide, narrow, lead = ("cols", GROUP_W), ("cols", BLK), ("lead", 1)
    kinds = [wide, narrow, narrow, ("each", 0)] + [lead] * 7 + [wide, wide, wide, lead, ("lead", 4)]
    wide_w, narrow_w = GROUP_W * gps, BLK * gps
    gparam = pl.BlockSpec((gps, 1, 8), lambda g, c: (g, 0, 0))
    gparam_t = pl.BlockSpec((gps, 8, 1), lambda g, c: (g, 0, 0))
    z_specs = [pl.BlockSpec((BLK, GROUP_W), functools.partial(lambda g, c, gi: (c, COL_Z // GROUP_W + gps * g + gi),
                                                              gi=gi)) for gi in range(gps)]
    return _call(
        _per_group(body, gps, kinds), name=name,
        out_shape=[_sds((s, SSD_INNER), F32), _sds((s, SSD_INNER), BF16),
                   _sds((SSD_GROUPS, nc, 4, BLK, 2 * HD), F32)],
        grid=(SSD_GROUPS // gps, nc),
        in_specs=[pl.BlockSpec((BLK, wide_w), lambda g, c: (c, g)),
                  pl.BlockSpec((BLK, narrow_w), lambda g, c: (c, SSD_INNER // narrow_w + g)),
                  pl.BlockSpec((BLK, narrow_w), lambda g, c: (c, (SSD_INNER + SSD_GROUPS * BLK) // narrow_w + g)),
                  *z_specs,
                  pl.BlockSpec((gps, BLK, 8), lambda g, c: (g, c, 0)),
                  pl.BlockSpec((gps, 8, BLK), lambda g, c: (g, 0, c)),
                  gparam, gparam_t, gparam, gparam_t, gparam,
                  pl.BlockSpec((1, wide_w), lambda g, c: (0, g))],
        out_specs=[pl.BlockSpec((BLK, wide_w), lambda g, c: (c, g)),
                   pl.BlockSpec((BLK, wide_w), lambda g, c: (c, g)),
                   pl.BlockSpec((gps, None, 4, BLK, 2 * HD), lambda g, c: (g, c, 0, 0, 0))],
        scratch_shapes=[pltpu.VMEM((4 * gps, BLK, 2 * HD), F32)],
    )(xc, xc, xc, *([p] * gps), dtg, dtg_t, bias, bias_t, alog, alog_t, dskip, gn)


def ssd_bwd(p, xc, y, d_sin, hprev, dtg, dtg_t, params, gn, name):
    s = p.shape[0]
    nc = s // BLK
    bias, bias_t, alog, alog_t, dskip = params

    def body(xs_ref, b_ref, c_ref, z_ref, y_ref, dsin_ref, hp_ref, dtr_ref, dtrt_ref, bias_ref,
             biast_ref, alog_ref, alogt_ref, dsk_ref, gn_ref,
             dxs_ref, db_ref, dc_ref, dz_ref, ddt_ref, da_ref, dbias_ref, ddsk_ref, dgn_ref, dh_ref):
        c_idx = pl.program_id(1)

        @pl.when(c_idx == 0)
        def _():
            dh_ref[...] = jnp.zeros_like(dh_ref)
            da_ref[...] = jnp.zeros_like(da_ref)
            dbias_ref[...] = jnp.zeros_like(dbias_ref)
            ddsk_ref[...] = jnp.zeros_like(ddsk_ref)
            dgn_ref[...] = jnp.zeros_like(dgn_ref)

        dtb, dt, cs, cs_t, row, col, upper = _ssd_decays(dtr_ref, dtrt_ref, bias_ref, biast_ref, alog_ref, alogt_ref)
        first = _lane_first_half((BLK, 2 * HD))
        second = jnp.logical_not(first)
        first_row = _lane_first_half((1, 2 * HD))
        tril = row >= col
        triu = row <= col
        last_row = lax.broadcasted_iota(jnp.int32, (BLK, 1), 0) == BLK - 1
        lane8 = lax.broadcasted_iota(jnp.int32, (BLK, 8), 1)

        yv = y_ref[...]
        zv = z_ref[...].astype(F32)
        sg = _sigmoid(zv)
        yz = yv * (zv * sg)
        r = lax.rsqrt(jnp.mean(yz * yz, axis=-1, keepdims=True) + EPS)
        yzn = yz * r
        dsn = dsin_ref[...]
        dgn_ref[...] += jnp.sum(dsn * yzn, axis=0, keepdims=True)
        dsn = dsn * gn_ref[...]
        dyz = r * (dsn - yzn * jnp.mean(dsn * yzn, axis=-1, keepdims=True))
        dy = dyz * (zv * sg)
        dz_ref[...] = (dyz * yv * (sg * (1.0 + zv * (1.0 - sg)))).astype(BF16)
        xs_all = xs_ref[...].astype(F32)
        ddsk_ref[...] += jnp.sum(dy * xs_all, axis=0, keepdims=True)

        b16, c16 = b_ref[...], c_ref[...]
        cb = _dot_nt(c16, b16)
        cb_t = _dot_nt(b16, c16)
        n_pairs = GROUP_W // (2 * HD)
        tot = cs[BLK - 1:BLK, :]
        exp_cs, exp_rest, exp_tot = jnp.exp(cs), jnp.exp(tot - cs), jnp.exp(tot)

        def per_head(v, mask):
            return jnp.concatenate([jnp.where(mask, v[:, 2 * pr:2 * pr + 1], v[:, 2 * pr + 1:2 * pr + 2])
                                    for pr in range(n_pairs)], axis=1)

        def head_sums(v):
            out = jnp.zeros((BLK, 8), F32)
            for pr in range(n_pairs):
                sa, sb = _pair_sum(v[:, pr * 2 * HD:(pr + 1) * 2 * HD], first)
                out = jnp.where(lane8 == 2 * pr, sa, jnp.where(lane8 == 2 * pr + 1, sb, out))
            return out

        dt_w, e_w, f_w = per_head(dt, first), per_head(exp_cs, first), per_head(exp_rest, first)
        xt = xs_all * dt_w
        xt16 = xt.astype(BF16)
        hstate = jnp.concatenate([hp_ref[pr] for pr in range(n_pairs)], axis=1)
        h16 = hstate.astype(BF16)
        dhn = jnp.concatenate([dh_ref[pr] for pr in range(n_pairs)], axis=1)
        dhn16 = dhn.astype(BF16)
        edy16 = (e_w * dy).astype(BF16)
        y_off = e_w * _dot(c16, h16)
        dcs_all = head_sums(dy * y_off)
        dc_acc = _dot_nt(edy16, h16)
        zmat = _dot(b16, dhn16)
        t_all = head_sums(zmat * xt) * exp_rest
        hh_rows = jnp.sum(head_sums(dhn * hstate), axis=0, keepdims=True)
        dtot = jnp.sum(t_all, axis=0, keepdims=True) + hh_rows * exp_tot
        dcs_all = dcs_all - t_all + jnp.where(last_row, dtot, 0.0)
        fxt16 = (f_w * xt).astype(BF16)
        db_acc = _dot_nt(fxt16, dhn16)
        dh_new = _dot_tn(c16, edy16) + per_head(exp_tot, first_row) * dhn
        for pr in range(n_pairs):
            dh_ref[pr] = dh_new[:, pr * 2 * HD:(pr + 1) * 2 * HD]
        g_sum = jnp.zeros((BLK, BLK), F32)
        gt_sum = jnp.zeros((BLK, BLK), F32)
        d_xt_parts = []
        for pr in range(n_pairs):
            cols = slice(pr * 2 * HD, (pr + 1) * 2 * HD)
            dym2 = _stack_heads(dy[:, cols].astype(BF16), first)
            d_m2 = _dot_nt(dym2, xt16[:, cols])
            d_mt2 = _dot_nt(xt16[:, cols], dym2)
            mt2 = []
            for e, h in enumerate((2 * pr, 2 * pr + 1)):
                cs_c, cs_r = cs[:, h:h + 1], cs_t[h:h + 1, :]
                decay = jnp.exp(jnp.where(tril, cs_c - cs_r, NEG))
                decay_t = jnp.exp(jnp.where(triu, cs_r - cs_c, NEG))
                gm = d_m2[e * BLK:(e + 1) * BLK] * decay
                gmt = d_mt2[:, e * BLK:(e + 1) * BLK] * decay_t
                g_sum += gm
                gt_sum += gmt
                dcs_h = jnp.sum(gm * cb, axis=-1, keepdims=True) - jnp.sum(gmt * cb_t, axis=-1, keepdims=True)
                dcs_all = dcs_all + jnp.where(lane8 == h, dcs_h, 0.0)
                mt2.append((cb_t * decay_t).astype(BF16))
            d_xt_parts.append(_dot(jnp.concatenate(mt2, axis=1), dym2))
        d_xt = jnp.concatenate(d_xt_parts, axis=1) + f_w * zmat
        dxs_ref[...] = dy * per_head(dsk_ref[...], first_row) + d_xt * dt_w
        ddtx_all = head_sums(d_xt * xs_all)

        dc_ref[...] = dc_acc + _dot(g_sum.astype(BF16), b16)
        db_ref[...] = db_acc + _dot(gt_sum.astype(BF16), c16)
        d_a = _dot_hi(upper, dcs_all)
        a_neg = -jnp.exp(alog_ref[...])
        ddt = ddtx_all + d_a * a_neg
        da_ref[...] += jnp.sum(d_a * dt, axis=0, keepdims=True)
        ddtr = ddt * _sigmoid(dtb)
        ddt_ref[...] = ddtr
        dbias_ref[...] += jnp.sum(ddtr, axis=0, keepdims=True)

    gps = SSD_GROUPS_PER_STEP
    k_wide, k_narrow, k_lead = ("cols", GROUP_W), ("cols", BLK), ("lead", 1)
    kinds = ([k_wide, k_narrow, k_narrow, ("each", 0), k_wide, k_wide] + [k_lead] * 8 + [k_wide]
             + [k_wide, k_narrow, k_narrow, k_wide] + [k_lead] * 4 + [k_wide] + [("lead", 4)])
    wide_w, narrow_w = GROUP_W * gps, BLK * gps
    rc = lambda c: nc - 1 - c
    gparam = pl.BlockSpec((gps, 1, 8), lambda g, c: (g, 0, 0))
    gparam_t = pl.BlockSpec((gps, 8, 1), lambda g, c: (g, 0, 0))
    wide = pl.BlockSpec((BLK, wide_w), lambda g, c: (rc(c), g))
    narrow = pl.BlockSpec((BLK, narrow_w), lambda g, c: (rc(c), g))
    z_specs = [pl.BlockSpec((BLK, GROUP_W),
                            functools.partial(lambda g, c, gi: (rc(c), COL_Z // GROUP_W + gps * g + gi), gi=gi))
               for gi in range(gps)]
    return _call(
        _per_group(body, gps, kinds), name=name,
        out_shape=[_sds((s, SSD_INNER), F32), _sds((s, GROUP_W), F32), _sds((s, GROUP_W), F32),
                   _sds((s, SSD_INNER), BF16), _sds((SSD_GROUPS, s, 8), F32),
                   _sds((SSD_GROUPS, 1, 8), F32), _sds((SSD_GROUPS, 1, 8), F32),
                   _sds((SSD_GROUPS, 1, GROUP_W), F32), _sds((1, SSD_INNER), F32)],
        grid=(SSD_GROUPS // gps, nc),
        in_specs=[wide,
                  pl.BlockSpec((BLK, narrow_w), lambda g, c: (rc(c), SSD_INNER // narrow_w + g)),
                  pl.BlockSpec((BLK, narrow_w), lambda g, c: (rc(c), (SSD_INNER + SSD_GROUPS * BLK) // narrow_w + g)),
                  *z_specs,
                  wide, wide,
                  pl.BlockSpec((gps, None, 4, BLK, 2 * HD), lambda g, c: (g, rc(c), 0, 0, 0)),
                  pl.BlockSpec((gps, BLK, 8), lambda g, c: (g, rc(c), 0)),
                  pl.BlockSpec((gps, 8, BLK), lambda g, c: (g, 0, rc(c))),
                  gparam, gparam_t, gparam, gparam_t, gparam,
                  pl.BlockSpec((1, wide_w), lambda g, c: (0, g))],
        out_specs=[wide, narrow, narrow, wide,
                   pl.BlockSpec((gps, BLK, 8), lambda g, c: (g, rc(c), 0)),
                   gparam, gparam,
                   pl.BlockSpec((gps, 1, GROUP_W), lambda g, c: (g, 0, 0)),
                   pl.BlockSpec((1, wide_w), lambda g, c: (0, g))],
        scratch_shapes=[pltpu.VMEM((4 * gps, BLK, 2 * HD), F32)],
    )(xc, xc, xc, *([p] * gps), y, d_sin, hprev, dtg, dtg_t, bias, bias_t, alog, alog_t, dskip, gn)


def merge_fwd(p, a, sbr, name, tm=512):
    s = p.shape[0]
    nj = D_MODEL // GROUP_W

    def body(ga_ref, gs_ref, a_ref, s_ref, o_ref):
        o_ref[...] = (_sigmoid(ga_ref[...].astype(F32)) * a_ref[...]
                      + _sigmoid(gs_ref[...].astype(F32)) * s_ref[...]).astype(BF16)

    blk = pl.BlockSpec((tm, GROUP_W), lambda i, j: (i, j))
    return _call(body, name=name, out_shape=_sds((s, D_MODEL), BF16), grid=(s // tm, nj),
                 in_specs=[pl.BlockSpec((tm, GROUP_W), lambda i, j: (i, COL_GA // GROUP_W + j)),
                           pl.BlockSpec((tm, GROUP_W), lambda i, j: (i, COL_GS // GROUP_W + j)), blk, blk],
                 out_specs=blk)(p, p, a, sbr)


def merge_bwd(p, a, sbr, dmerged, name, tm=512):
    s = p.shape[0]
    nj = D_MODEL // GROUP_W

    def body(ga_ref, gs_ref, a_ref, s_ref, dm_ref, da_ref, ds_ref, dga_ref, dgs_ref):
        dm = dm_ref[...]
        sa = _sigmoid(ga_ref[...].astype(F32))
        ss = _sigmoid(gs_ref[...].astype(F32))
        da_ref[...] = (dm * sa).astype(BF16)
        ds_ref[...] = (dm * ss).astype(BF16)
        dga_ref[...] = (dm * a_ref[...] * sa * (1.0 - sa)).astype(BF16)
        dgs_ref[...] = (dm * s_ref[...] * ss * (1.0 - ss)).astype(BF16)

    blk = pl.BlockSpec((tm, GROUP_W), lambda i, j: (i, j))
    shp = _sds((s, D_MODEL), BF16)
    return _call(body, name=name, out_shape=[shp] * 4, grid=(s // tm, nj),
                 in_specs=[pl.BlockSpec((tm, GROUP_W), lambda i, j: (i, COL_GA // GROUP_W + j)),
                           pl.BlockSpec((tm, GROUP_W), lambda i, j: (i, COL_GS // GROUP_W + j)), blk, blk, blk],
                 out_specs=[blk] * 4)(p, p, a, sbr, dmerged)


def _group_major(v):
    return v.reshape(SSD_GROUPS, 1, 8), v.reshape(SSD_GROUPS, 8, 1)


def mixer_forward(x, w, rider=None, later_weights=None):
    s = x.shape[0]
    h = rms_fwd(x, w["mix_norm"], "mix_rms")
    p = matmul_nn(h, w["w_in_main"], "mix_proj", BF16, tm=1024, tn=512, rider=rider)
    rode = None
    if rider is not None:
        p, rode = p
        w = dict(w, **later_weights(rode))
    dt_raw = matmul_nn(h, w["w_in_dt"], "mix_proj_dt", F32, tm=1024, tn=DT_PAD)
    qn, kn = qk_norm_fwd(p, w["q_gain"], w["k_gain"], "qk_norm")
    streams, os_, lses = [], [], []
    for g, d in enumerate(ATTN_DILATIONS):
        cols = slice(g * GROUP_W, (g + 1) * GROUP_W)
        qs, ks = _to_streams(qn[:, cols], d), _to_streams(kn[:, cols], d)
        vs = _to_streams(p[:, COL_V + g * GROUP_W:COL_V + (g + 1) * GROUP_W], d)
        o, lse = attn_fwd2(qs, ks, vs, g, f"attn_fwd{g}")
        streams.append((qs, ks, vs, lse))
        os_.append(_from_streams(o, d))
        lses.append(_from_streams(lse, d))
    attn_o = attn_merge_fwd(os_, lses, "attn_merge")
    cpre, xc = conv_fwd(p, w["conv_w"], w["conv_b"], "conv_fwd")
    dtg = dt_raw[:, :SSD_HEADS].reshape(s, SSD_GROUPS, 8).transpose(1, 0, 2)
    dtg_t = dtg.transpose(0, 2, 1)
    params = (*_group_major(w["dt_bias"]), *_group_major(w["a_log"]), _group_major(w["d_skip"])[0])
    y, s_in, hprev = ssd_fwd(p, xc, dtg, dtg_t, params, w["ssd_norm"], "ssd_fwd")
    a = matmul_nn(attn_o, w["w_attn_branch"], "attn_branch", F32, tm=1024, tn=512)
    sbr = matmul_nn(s_in, w["w_ssd_branch"], "ssd_branch", F32, tm=1024, tn=512)
    merged = merge_fwd(p, a, sbr, "merge")
    x_out = matmul_nn(merged, w["w_out"], "mix_out", F32, tm=1024, tn=512, res=x)
    saved = dict(h=h, p=p, streams=streams, os=os_, lses=lses, attn_o=attn_o, cpre=cpre, xc=xc, dtg=dtg,
                 dtg_t=dtg_t, params=params, y=y, s_in=s_in, hprev=hprev, a=a, sbr=sbr, merged=merged, w=w)
    return x_out, saved, rode


def mixer_backward(dx_out, x, sv, ride_early=None, ride_late=None):
    s = x.shape[0]
    p = sv["p"]
    w = sv["w"]
    g = {}
    dmerged = matmul_nt(dx_out, w["w_out"], "d_merged", F32, tm=1024, tn=512, tk=1024)
    g["w_out"] = matmul_tn(sv["merged"], dx_out, "dw_out", tn=512, ts=1024)
    da, ds, dga, dgs = merge_bwd(p, sv["a"], sv["sbr"], dmerged, "merge_bwd")
    g["w_attn_branch"] = matmul_tn(sv["attn_o"], da, "dw_attn_branch", tn=512, ts=1024)
    g["w_ssd_branch"] = matmul_tn(sv["s_in"], ds, "dw_ssd_branch", tn=512, ts=1024)
    d_attn_o = matmul_nt(da, w["w_attn_branch"], "d_attn_o", F32, tm=1024, tn=512, tk=1024)
    d_sin = matmul_nt(ds, w["w_ssd_branch"], "d_ssd_in", F32, tm=1024, tn=512, tk=1024)
    dxs, d_b, d_c, dz, ddt, d_asum, d_bias, d_dsk, d_gn = ssd_bwd(
        p, sv["xc"], sv["y"], d_sin, sv["hprev"], sv["dtg"], sv["dtg_t"], sv["params"], w["ssd_norm"], "ssd_bwd")
    dxbc, d_convw, d_convb = conv_bwd(p, sv["cpre"], dxs, d_b, d_c, w["conv_w"], "conv_bwd")
    g["conv_w"] = d_convw[:SSD_CONV]
    g["conv_b"] = d_convb
    g["dt_bias"] = d_bias.reshape(1, SSD_HEADS)
    g["a_log"] = (d_asum * (-jnp.exp(sv["params"][2]))).reshape(1, SSD_HEADS)
    g["d_skip"] = jnp.sum(d_dsk.reshape(SSD_HEADS, HD), axis=1).reshape(1, SSD_HEADS)
    g["ssd_norm"] = d_gn
    merged_bwd = attn_merge_bwd(d_attn_o, sv["os"], sv["lses"], "attn_merge_bwd")
    dqs, dks, dvs = [], [], []
    for gi, d in enumerate(ATTN_DILATIONS):
        qs, ks, vs, lse = sv["streams"][gi]
        d_o = _to_streams(merged_bwd[gi], d)
        cterm = _to_streams(merged_bwd[3 + gi], d)
        dq, dk, dv = attn_bwd2(qs, ks, vs, d_o, cterm, lse, gi, f"attn_bwd{gi}")
        dqs.append(_from_streams(dq, d))
        dks.append(_from_streams(dk, d))
        dvs.append(_from_streams(dv, d).astype(BF16))
    dqk, d_qg, d_kg = qk_norm_bwd(p, dqs, dks, w["q_gain"], w["k_gain"], "qk_norm_bwd")
    g["q_norm"] = jnp.sum(d_qg.reshape(N_ATTN_HEADS, HD), axis=0).reshape(1, HD)
    g["k_norm"] = jnp.sum(d_kg.reshape(N_ATTN_HEADS, HD), axis=0).reshape(1, HD)
    dp = [dqk, jnp.concatenate(dvs, axis=1), dz, dxbc, dga, dgs]
    ddt_pad = jnp.pad(ddt.transpose(1, 0, 2).reshape(s, SSD_HEADS), ((0, 0), (0, DT_PAD - SSD_HEADS)))
    if ride_early is not None:
        g["w_in_main"], g["rode_early"] = matmul_tn_pieces(sv["h"], dp, "dw_in", tn=512, ts=1024,
                                                           rider=ride_early(g))
    else:
        g["w_in_main"] = matmul_tn_pieces(sv["h"], dp, "dw_in", tn=512, ts=1024)
    g["w_in_dt"] = matmul_tn(sv["h"], ddt_pad, "dw_in_dt", tn=DT_PAD, ts=1024)
    if ride_late is not None:
        dh_main, g["rode_late"] = matmul_nt_pieces(dp, w["w_in_main"], "d_mix_h", F32, tm=1024, tn=512, tk=512,
                                                   rider=ride_late(g))
    else:
        dh_main = matmul_nt_pieces(dp, w["w_in_main"], "d_mix_h", F32, tm=1024, tn=512, tk=512)
    dh_dt = matmul_nt(ddt_pad, w["w_in_dt"], "d_mix_h_dt", F32, tm=1024, tn=1024, tk=DT_PAD)
    dx, g["mix_norm"] = rms_bwd([dh_main, dh_dt], x, w["mix_norm"], dx_out, "mix_drms")
    return dx, g


ANY = pl.BlockSpec(memory_space=pl.ANY)


def _place():
    x, y, c = lax.axis_index("x"), lax.axis_index("y"), lax.axis_index("c")
    chips = [(1 - x, y), (x, 1 - y), (1 - x, 1 - y)]
    return x, y, c, 2 * x + y, chips


def _comm_call(body, *, name, out_shape, n_in, scratch_shapes, aliases=None):
    return pl.pallas_call(
        body, out_shape=out_shape, in_specs=[ANY] * n_in, out_specs=[ANY] * len(out_shape),
        scratch_shapes=scratch_shapes, input_output_aliases=aliases or {}, name=name,
        compiler_params=pltpu.CompilerParams(has_side_effects=True))


def gather_weights(shards, small):
    n = len(shards)
    halves = [a.shape[0] // 2 for a in shards]
    out_shape = [_sds((N_CHIP,) + a.shape, a.dtype) for a in shards] + [_sds((N_CHIP,) + small.shape, small.dtype)]

    def body(*refs):
        ins, outs = refs[:n + 1], refs[n + 1:2 * n + 2]
        send1, recv1, send2, recv2, local = refs[2 * n + 2:]
        x, y, c, me, chips = _place()
        sibling = (x, y, 1 - c)

        def rows(k, chip, core):
            if k == n:
                return outs[k].at[chip]
            return outs[k].at[chip, pl.ds(core * halves[k], halves[k])]

        def level1(k, t, incoming):
            chip = 2 * chips[t][0] + chips[t][1]
            src = ins[k] if k == n else ins[k].at[pl.ds(c * halves[k], halves[k])]
            return pltpu.make_async_remote_copy(
                src_ref=src, dst_ref=rows(k, chip if incoming else me, c), send_sem=send1.at[3 * k + t],
                recv_sem=recv1.at[3 * k + t], device_id=(*chips[t], c), device_id_type=MESH)

        def level2(k, t, incoming):
            chip = 2 * chips[t][0] + chips[t][1]
            core = (1 - c) if incoming else c
            return pltpu.make_async_remote_copy(
                src_ref=rows(k, chip, core), dst_ref=rows(k, chip, core), send_sem=send2.at[3 * k + t],
                recv_sem=recv2.at[3 * k + t], device_id=sibling, device_id_type=MESH)

        own = [pltpu.make_async_copy(ins[k], outs[k].at[me], local.at[k]) for k in range(n + 1)]
        for cp in own:
            cp.start()
        first = [level1(k, t, False) for k in range(n + 1) for t in range(3)]
        for cp in first:
            cp.start()
        passed = []
        for k in range(n + 1):
            for t in range(3):
                level1(k, t, True).wait_recv()
                if k < n:
                    cp = level2(k, t, False)
                    cp.start()
                    passed.append(cp)
        for k in range(n):
            for t in range(3):
                level2(k, t, True).wait_recv()
        for cp in first + passed:
            cp.wait_send()
        for cp in own:
            cp.wait()

    dma = pltpu.SemaphoreType.DMA
    return _comm_call(body, name="gather_weights", out_shape=out_shape, n_in=n + 1,
                      scratch_shapes=[dma((3 * n + 3,)), dma((3 * n + 3,)), dma((3 * n,)), dma((3 * n,)),
                                      dma((n + 1,))])(*shards, small)


def reduce_to_sibling(grads):
    n = len(grads)
    halves = [a.shape[1] // 2 for a in grads]
    shapes = [_sds((N_CHIP, h, a.shape[2]), a.dtype) for a, h in zip(grads, halves)]

    def body(*refs):
        ins, got, kept = refs[:n], refs[n:2 * n], refs[2 * n:3 * n]
        send, recv, local = refs[3 * n:]
        x, y, c, _, _ = _place()
        copies, locals_ = [], []
        for k in range(n):
            h = halves[k]
            locals_.append(pltpu.make_async_copy(ins[k].at[:, pl.ds(c * h, h)], kept[k], local.at[k]))
            copies.append(pltpu.make_async_remote_copy(
                src_ref=ins[k].at[:, pl.ds((1 - c) * h, h)], dst_ref=got[k], send_sem=send.at[k], recv_sem=recv.at[k],
                device_id=(x, y, 1 - c), device_id_type=MESH))
        for cp in locals_ + copies:
            cp.start()
        for cp in copies:
            cp.wait_recv()
        for cp in copies:
            cp.wait_send()
        for cp in locals_:
            cp.wait()

    dma = pltpu.SemaphoreType.DMA
    res = _comm_call(body, name="reduce_to_sibling", out_shape=shapes + shapes, n_in=n,
                     scratch_shapes=[dma((n,)), dma((n,)), dma((n,))])(*grads)
    return res[:n], res[n:]


def reduce_to_owner(sums):
    n = len(sums)
    shapes = [_sds(a.shape, a.dtype) for a in sums]

    def body(*refs):
        ins, outs = refs[:n], refs[n:2 * n]
        send, recv, local = refs[2 * n:]
        x, y, c, me, chips = _place()
        copies, locals_ = [], []
        for k in range(n):
            locals_.append(pltpu.make_async_copy(ins[k].at[me], outs[k].at[3], local.at[k]))
            for t in range(3):
                chip = 2 * chips[t][0] + chips[t][1]
                copies.append(pltpu.make_async_remote_copy(
                    src_ref=ins[k].at[chip], dst_ref=outs[k].at[t], send_sem=send.at[3 * k + t],
                    recv_sem=recv.at[3 * k + t], device_id=(*chips[t], c), device_id_type=MESH))
        for cp in locals_ + copies:
            cp.start()
        for cp in copies:
            cp.wait_recv()
        for cp in copies:
            cp.wait_send()
        for cp in locals_:
            cp.wait()

    dma = pltpu.SemaphoreType.DMA
    return _comm_call(body, name="reduce_to_owner", out_shape=shapes, n_in=n,
                      scratch_shapes=[dma((3 * n,)), dma((3 * n,)), dma((n,))])(*sums)


def share_with_sibling(halves_):
    n = len(halves_)
    shapes = [_sds((2 * a.shape[0], a.shape[1]), a.dtype) for a in halves_]

    def body(*refs):
        ins, outs = refs[:n], refs[n:2 * n]
        send, recv, local = refs[2 * n:]
        x, y, c, _, _ = _place()
        copies, locals_ = [], []
        for k in range(n):
            h = ins[k].shape[0]
            mine = outs[k].at[pl.ds(c * h, h)]
            locals_.append(pltpu.make_async_copy(ins[k], mine, local.at[k]))
            copies.append(pltpu.make_async_remote_copy(
                src_ref=ins[k], dst_ref=mine, send_sem=send.at[k], recv_sem=recv.at[k],
                device_id=(x, y, 1 - c), device_id_type=MESH))
        for cp in locals_ + copies:
            cp.start()
        for cp in copies:
            cp.wait_recv()
        for cp in copies:
            cp.wait_send()
        for cp in locals_:
            cp.wait()

    dma = pltpu.SemaphoreType.DMA
    return _comm_call(body, name="share_with_sibling", out_shape=shapes, n_in=n,
                      scratch_shapes=[dma((n,)), dma((n,)), dma((n,))])(*halves_)


def _cores():
    c = lax.axis_index("c")
    return jnp.stack([c, 1 - c]).astype(jnp.int32)


def _staged_call(body, *, name, grid, in_specs, out_specs, out_shape, scratch_shapes):
    return pl.pallas_call(
        body, out_shape=out_shape, name=name,
        grid_spec=pltpu.PrefetchScalarGridSpec(num_scalar_prefetch=1, grid=grid, in_specs=in_specs,
                                               out_specs=out_specs, scratch_shapes=scratch_shapes),
        compiler_params=pltpu.CompilerParams(dimension_semantics=("arbitrary",) * len(grid),
                                             vmem_limit_bytes=V7X_VMEM_LIMIT, has_side_effects=True))


def gather_rider(shards, tiles):
    dma = pltpu.SemaphoreType.DMA
    n = len(shards)
    geo = [(a.shape[0] // 2, tm, (a.shape[0] // 2) // tm) for a, tm in zip(shards, tiles)]
    scratch = []
    for a, (h, tm, nk) in zip(shards, geo):
        scratch += [pltpu.VMEM((N_CHIP,) + a.shape, a.dtype), dma((3, nk)), dma((3, nk)), dma((3, nk)), dma((3, nk)),
                    dma((nk + 2,))]

    def copies(j, in_ref, scr):
        buf, send1, recv1, send2, recv2, local = scr[6 * j:6 * j + 6]
        h, tm, nk = geo[j]
        x, y, c, me, chips = _place()
        chip_of = [2 * chips[t][0] + chips[t][1] for t in range(3)]

        def rows(chip, core, k):
            return buf.at[chip, pl.ds(core * h + k * tm, tm)]

        def mine(k):
            if k == nk:
                return pltpu.make_async_copy(in_ref.at[pl.ds((1 - c) * h, h)], buf.at[me, pl.ds((1 - c) * h, h)],
                                             local.at[nk])
            return pltpu.make_async_copy(in_ref.at[pl.ds(c * h + k * tm, tm)], rows(me, c, k), local.at[k])

        def level1(t, k, incoming):
            place = rows(chip_of[t] if incoming else me, c, k)
            return pltpu.make_async_remote_copy(src_ref=place, dst_ref=place, send_sem=send1.at[t, k],
                                                recv_sem=recv1.at[t, k], device_id=(*chips[t], c), device_id_type=MESH)

        def level2(t, k, incoming):
            place = rows(chip_of[t], (1 - c) if incoming else c, k)
            return pltpu.make_async_remote_copy(src_ref=place, dst_ref=place, send_sem=send2.at[t, k],
                                                recv_sem=recv2.at[t, k], device_id=(x, y, 1 - c),
                                                device_id_type=MESH)

        return buf, local, nk, mine, level1, level2

    def start(ins, outs, scr):
        for j in range(n):
            _, _, nk, mine, _, _ = copies(j, ins[j], scr)
            for k in range(nk + 1):
                mine(k).start()
        for j in range(n):
            _, _, nk, mine, level1, _ = copies(j, ins[j], scr)
            for k in range(nk):
                mine(k).wait()
                for t in range(3):
                    level1(t, k, False).start()

    def finish(ins, outs, scr):
        for j in range(n):
            _, _, nk, _, level1, level2 = copies(j, ins[j], scr)
            for k in range(nk):
                for t in range(3):
                    level1(t, k, True).wait_recv()
                    level2(t, k, False).start()
        for j in range(n):
            buf, local, nk, mine, level1, level2 = copies(j, ins[j], scr)
            for k in range(nk):
                for t in range(3):
                    level2(t, k, True).wait_recv()
            for k in range(nk):
                for t in range(3):
                    level1(t, k, False).wait_send()
                    level2(t, k, False).wait_send()
            mine(nk).wait()
            pltpu.make_async_copy(buf, outs[j], local.at[nk + 1]).start()
        for j in range(n):
            buf, local, nk, _, _, _ = copies(j, ins[j], scr)
            pltpu.make_async_copy(buf, outs[j], local.at[nk + 1]).wait()

    return Rider(list(shards), [_sds((N_CHIP,) + a.shape, a.dtype) for a in shards], scratch, start, finish)


def run_alone(rider, name):
    return _call(lambda: None, name=name, out_shape=[], in_specs=[], out_specs=[], grid=(1,), rider=rider)()[1]


def sibling_sum(g, tm, name):
    _, r, cdim = g.shape
    h = r // 2
    ni = h // tm
    dma = pltpu.SemaphoreType.DMA

    def body(cores_ref, keep_ref, give_ref, out_ref, slot, send, recv):
        par = (pl.program_id(0) * ni + pl.program_id(1)) % 2
        x, y, c, _, _ = _place()
        cp = pltpu.make_async_remote_copy(src_ref=give_ref, dst_ref=slot.at[par], send_sem=send.at[par],
                                          recv_sem=recv.at[par], device_id=(x, y, 1 - c), device_id_type=MESH)
        cp.start()
        cp.wait_recv()
        out_ref[...] = (keep_ref[...].astype(F32) + slot[par].astype(F32)).astype(out_ref.dtype)
        cp.wait_send()

    flat = g.reshape(N_CHIP * r, cdim)
    return _staged_call(
        body, name=name, grid=(N_CHIP, ni),
        in_specs=[pl.BlockSpec((tm, cdim), lambda j, i, cores: ((2 * j + cores[0]) * ni + i, 0)),
                  pl.BlockSpec((tm, cdim), lambda j, i, cores: ((2 * j + cores[1]) * ni + i, 0))],
        out_specs=pl.BlockSpec((None, tm, cdim), lambda j, i, cores: (j, i, 0)),
        out_shape=_sds((N_CHIP, h, cdim), g.dtype),
        scratch_shapes=[pltpu.VMEM((2, tm, cdim), g.dtype), dma((2,)), dma((2,))],
    )(_cores(), flat, flat)


def owner_sum_rider(sums, tiles):
    dma = pltpu.SemaphoreType.DMA
    n = len(sums)
    geo = [(a.shape[1], tm, a.shape[1] // tm) for a, tm in zip(sums, tiles)]
    scratch = []
    for a, (h, tm, nk) in zip(sums, geo):
        cdim = a.shape[2]
        scratch += [pltpu.VMEM(a.shape, a.dtype), pltpu.VMEM((3, h, cdim), a.dtype), pltpu.VMEM((2, h, cdim), F32),
                    dma((3, nk)), dma((3, nk)), dma((nk,)), dma((nk,)), dma((2,))]

    def copies(j, scr):
        part, got, res, send, recv, send2, recv2, local = scr[8 * j:8 * j + 8]
        h, tm, nk = geo[j]
        x, y, c, me, chips = _place()

        def to_owner(t, k):
            chip = 2 * chips[t][0] + chips[t][1]
            return pltpu.make_async_remote_copy(
                src_ref=part.at[chip, pl.ds(k * tm, tm)], dst_ref=got.at[t, pl.ds(k * tm, tm)],
                send_sem=send.at[t, k], recv_sem=recv.at[t, k], device_id=(*chips[t], c), device_id_type=MESH)

        def to_sibling(k):
            place = res.at[c, pl.ds(k * tm, tm)]
            return pltpu.make_async_remote_copy(src_ref=place, dst_ref=place, send_sem=send2.at[k],
                                                recv_sem=recv2.at[k], device_id=(x, y, 1 - c), device_id_type=MESH)

        return part, got, res, local, to_owner, to_sibling, (tm, nk, c, me)

    def start(ins, outs, scr):
        for j in range(n):
            part, _, _, local, _, _, _ = copies(j, scr)
            pltpu.make_async_copy(ins[j], part, local.at[0]).start()
        for j in range(n):
            part, _, _, local, to_owner, _, (tm, nk, c, me) = copies(j, scr)
            pltpu.make_async_copy(ins[j], part, local.at[0]).wait()
            for k in range(nk):
                for t in range(3):
                    to_owner(t, k).start()

    def finish(ins, outs, scr):
        for j in range(n):
            part, got, res, _, to_owner, to_sibling, (tm, nk, c, me) = copies(j, scr)
            for k in range(nk):
                rows = pl.ds(k * tm, tm)
                for t in range(3):
                    to_owner(t, k).wait_recv()
                acc = part[me, rows, :].astype(F32)
                for t in range(3):
                    acc = acc + got[t, rows, :].astype(F32)
                res[c, rows, :] = acc
                to_sibling(k).start()
        for j in range(n):
            _, _, res, local, to_owner, to_sibling, (tm, nk, c, me) = copies(j, scr)
            for k in range(nk):
                to_sibling(k).wait_recv()
            for k in range(nk):
                to_sibling(k).wait_send()
                for t in range(3):
                    to_owner(t, k).wait_send()
            pltpu.make_async_copy(res, outs[j], local.at[1]).start()
        for j in range(n):
            _, _, res, local, _, _, _ = copies(j, scr)
            pltpu.make_async_copy(res, outs[j], local.at[1]).wait()

    return Rider(list(sums), [_sds((2, a.shape[1], a.shape[2]), F32) for a in sums], scratch, start, finish)


def gather_conv_w(w):
    def body(in_ref, out_ref, send, recv):
        x, y, c, me, chips = _place()
        out_ref[me] = in_ref[...]
        copies = []
        for t in range(3):
            copies.append(pltpu.make_async_remote_copy(
                src_ref=out_ref.at[me], dst_ref=out_ref.at[me], send_sem=send.at[t], recv_sem=recv.at[t],
                device_id=(*chips[t], c), device_id_type=MESH))
        for cp in copies:
            cp.start()
        for cp in copies:
            cp.wait_recv()
        for cp in copies:
            cp.wait_send()

    dma = pltpu.SemaphoreType.DMA
    vmem = pl.BlockSpec(memory_space=pltpu.VMEM)
    return pl.pallas_call(
        body, out_shape=_sds((N_CHIP,) + w.shape, w.dtype), in_specs=[vmem], out_specs=vmem, name="gather_conv_w",
        scratch_shapes=[dma((3,)), dma((3,))],
        compiler_params=pltpu.CompilerParams(has_side_effects=True))(w)


N_DEV = 8
SMALL_ROWS = 32
SMALL_LANES = 1024


def all_reduce_small(arrays):
    n_arr = len(arrays)
    places = []
    for k, a in enumerate(arrays):
        for ri in range(a.shape[0]):
            for c0 in range(0, a.shape[1], SMALL_LANES):
                places.append((k, ri, c0, min(SMALL_LANES, a.shape[1] - c0), len(places)))
    assert len(places) <= SMALL_ROWS

    def body(*refs):
        ins, outs = refs[:n_arr], refs[n_arr:2 * n_arr]
        buf, send, recv = refs[2 * n_arr:]
        x, y, c, _, _ = _place()
        me = 4 * x + 2 * y + c
        buf[me] = jnp.zeros((SMALL_ROWS, SMALL_LANES), F32)
        for k, ri, c0, width, row in places:
            buf[me, row:row + 1, 0:width] = ins[k][ri:ri + 1, c0:c0 + width]
        copies = []
        for r in range(1, N_DEV):
            px = (1 - x) if r & 4 else x
            py = (1 - y) if r & 2 else y
            pc = (1 - c) if r & 1 else c
            copies.append(pltpu.make_async_remote_copy(
                src_ref=buf.at[me], dst_ref=buf.at[me], send_sem=send.at[r - 1], recv_sem=recv.at[r - 1],
                device_id=(px, py, pc), device_id_type=MESH))
        for cp in copies:
            cp.start()
        for cp in copies:
            cp.wait_recv()
        for cp in copies:
            cp.wait_send()
        acc = buf[0]
        for j in range(1, N_DEV):
            acc = acc + buf[j]
        for k, ri, c0, width, row in places:
            outs[k][ri:ri + 1, c0:c0 + width] = acc[row:row + 1, 0:width]

    dma = pltpu.SemaphoreType.DMA
    vmem = pl.BlockSpec(memory_space=pltpu.VMEM)
    return pl.pallas_call(
        body, out_shape=[_sds(a.shape, F32) for a in arrays], in_specs=[vmem] * n_arr, out_specs=[vmem] * n_arr,
        name="all_reduce_small",
        scratch_shapes=[pltpu.VMEM((N_DEV, SMALL_ROWS, SMALL_LANES), F32), dma((N_DEV - 1,)), dma((N_DEV - 1,))],
        compiler_params=pltpu.CompilerParams(has_side_effects=True))(*arrays)


def _row_tile(rows, limit, multiple):
    return max(t for t in range(multiple, min(rows, limit) + 1, multiple) if rows % t == 0)


def add_pair(a, b, name):
    _, h, c = a.shape

    def body(a_ref, b_ref, o_ref):
        o_ref[...] = (a_ref[...].astype(F32) + b_ref[...].astype(F32)).astype(o_ref.dtype)

    blk = pl.BlockSpec((None, h, c), lambda j: (j, 0, 0))
    return _call(body, name=name, out_shape=_sds(a.shape, a.dtype), grid=(N_CHIP,), in_specs=[blk, blk],
                 out_specs=blk)(a, b)


def sum_slots(buf, name):
    _, h, c = buf.shape
    tm = _row_tile(h, 256, 16)

    def body(b_ref, o_ref):
        acc = b_ref[3].astype(F32)
        for t in range(3):
            acc = acc + b_ref[t].astype(F32)
        o_ref[...] = acc

    return _call(body, name=name, out_shape=_sds((h, c), F32), grid=(h // tm,),
                 in_specs=[pl.BlockSpec((N_CHIP, tm, c), lambda i: (0, i, 0))],
                 out_specs=pl.BlockSpec((tm, c), lambda i: (i, 0)))(buf)


def _adamw_math(w, g, m, v):
    c1 = 1.0 - ADAM_B1 ** ADAM_STEP
    c2 = 1.0 - ADAM_B2 ** ADAM_STEP
    m2 = ADAM_B1 * m + (1.0 - ADAM_B1) * g
    v2 = ADAM_B2 * v + (1.0 - ADAM_B2) * (g * g)
    return -ADAM_LR * ((m2 / c1) / (jnp.sqrt(v2 / c2) + ADAM_EPS) + ADAM_WD * w), m2, v2


def adamw(w, g, row_off, m, v, name):
    _, r, c = w.shape
    tm = r if r < 8 else _row_tile(math.gcd(r, row_off) if row_off else r, 128, 8)

    def body(w_ref, g_ref, m_ref, v_ref, go_ref, d_ref, m2_ref, v2_ref):
        gv = g_ref[...]
        go_ref[...] = gv
        d_ref[...], m2_ref[...], v2_ref[...] = _adamw_math(w_ref[...], gv, m_ref[...], v_ref[...])

    blk = pl.BlockSpec((None, tm, c), lambda i: (0, i, 0))
    shp = _sds((1, r, c), F32)
    return _call(body, name=name, out_shape=[shp] * 4, grid=(r // tm,),
                 in_specs=[blk, pl.BlockSpec((tm, c), lambda i: (row_off // tm + i, 0)), blk, blk],
                 out_specs=[blk] * 4)(w, g, m, v)


def adamw_small(ws, gs, ms, vs):
    n = len(ws)

    def body(*refs):
        ins, outs = refs[:4 * n], refs[4 * n:]
        for k in range(n):
            w_ref, g_ref, m_ref, v_ref = (ins[j * n + k] for j in range(4))
            outs[k][...], outs[n + k][...], outs[2 * n + k][...] = _adamw_math(w_ref[...], g_ref[...], m_ref[...],
                                                                               v_ref[...])

    vmem = pl.BlockSpec(memory_space=pltpu.VMEM)
    shapes = [_sds(w.shape, F32) for w in ws] * 3
    res = pl.pallas_call(body, out_shape=shapes, in_specs=[vmem] * (4 * n), out_specs=[vmem] * (3 * n),
                         name="adamw_small")(*ws, *gs, *ms, *vs)
    return res[:n], res[n:2 * n], res[2 * n:]


BIG = ("ffn1_w_gate", "ffn1_w_up", "ffn1_w_down", "w_in", "w_attn_branch", "w_ssd_branch", "w_out",
       "ffn2_w_gate", "ffn2_w_up", "ffn2_w_down")
SMALL = ("ffn1_norm", "mix_norm", "q_norm", "k_norm", "conv_b", "dt_bias", "a_log", "d_skip", "ssd_norm", "ffn2_norm")
WEIGHTS = ("ffn1_norm", "ffn1_w_gate", "ffn1_w_up", "ffn1_w_down", "mix_norm", "w_in", "q_norm", "k_norm", "conv_w",
           "conv_b", "dt_bias", "a_log", "d_skip", "ssd_norm", "w_attn_branch", "w_ssd_branch", "w_out", "ffn2_norm",
           "ffn2_w_gate", "ffn2_w_up", "ffn2_w_down")
CONV_SHARD = SSD_CONV_DIM // N_CHIP
CLASSES = {
    "ffn1_in": (("ffn1_w_gate", 1024), ("ffn1_w_up", 1024)),
    "ffn1_out": (("ffn1_w_down", 704),),
    "mix_in": (("w_in", 1024),),
    "mix_attn": (("w_attn_branch", 512),),
    "late_out": (("ffn2_w_down", 704), ("w_ssd_branch", 512), ("w_out", 256)),
    "ffn2_in": (("ffn2_w_gate", 1024), ("ffn2_w_up", 1024)),
}
CLASS_TILE = {"ffn1_in": 256, "ffn1_out": 176, "mix_in": 128, "mix_attn": 256, "late_out": 368, "ffn2_in": 256}
SIBLING_TILE = {"ffn1_in": 1024, "ffn1_out": 352, "mix_in": 256, "mix_attn": 256, "late_out": 736, "ffn2_in": 1024}


def _pack_small(vals, conv_part, loss_part=None):
    flat = [vals[k].reshape(-1) for k in SMALL]
    flat.append(jnp.zeros((SSD_CONV * SSD_CONV_DIM,), F32) if conv_part is None else conv_part.reshape(-1))
    flat.append(jnp.zeros((1,), F32) if loss_part is None else loss_part.reshape(1))
    flat = jnp.concatenate(flat)
    return jnp.pad(flat, (0, SMALL_ROWS * D_MODEL - flat.shape[0])).reshape(SMALL_ROWS, D_MODEL)


def _unpack_small(pack, like):
    flat = pack.reshape(-1)
    out, off = {}, 0
    for k in SMALL:
        n = like[k].size
        out[k] = flat[off:off + n].reshape(like[k].shape)
        off += n
    conv = flat[off:off + SSD_CONV * SSD_CONV_DIM].reshape(SSD_CONV, SSD_CONV_DIM)
    return out, conv, flat[off + SSD_CONV * SSD_CONV_DIM]


def _chip_major_cols(a):
    r = a.shape[0]
    return a.reshape(r, N_CHIP, -1).transpose(1, 0, 2)


def _from_chip_major_cols(a):
    return a.transpose(1, 0, 2).reshape(a.shape[1], -1)


def kernel(x, ffn1_norm, ffn1_w_gate, ffn1_w_up, ffn1_w_down, mix_norm, w_in, q_norm, k_norm, conv_w, conv_b, dt_bias, a_log, d_skip, ssd_norm, w_attn_branch, w_ssd_branch, w_out, ffn2_norm, ffn2_w_gate, ffn2_w_up, ffn2_w_down, loss_target, m_ffn1_norm, m_ffn1_w_gate, m_ffn1_w_up, m_ffn1_w_down, m_mix_norm, m_w_in, m_q_norm, m_k_norm, m_conv_w, m_conv_b, m_dt_bias, m_a_log, m_d_skip, m_ssd_norm, m_w_attn_branch, m_w_ssd_branch, m_w_out, m_ffn2_norm, m_ffn2_w_gate, m_ffn2_w_up, m_ffn2_w_down, v_ffn1_norm, v_ffn1_w_gate, v_ffn1_w_up, v_ffn1_w_down, v_mix_norm, v_w_in, v_q_norm, v_k_norm, v_conv_w, v_conv_b, v_dt_bias, v_a_log, v_d_skip, v_ssd_norm, v_w_attn_branch, v_w_ssd_branch, v_w_out, v_ffn2_norm, v_ffn2_w_gate, v_ffn2_w_up, v_ffn2_w_down):
    env = dict(locals())
    wts = {k: env[k] for k in WEIGHTS}
    moms = {k: env["m_" + k] for k in WEIGHTS}
    vars_ = {k: env["v_" + k] for k in WEIGHTS}
    x0 = x[0]
    target = loss_target[0]

    def gather(classes):
        shards = [jnp.concatenate([wts[k][0] for k, _ in CLASSES[c]], axis=0).astype(BF16) for c in classes]
        return gather_rider(shards, [CLASS_TILE[c] for c in classes])

    def reducer(classes, parts):
        sums = [sibling_sum(p, SIBLING_TILE[c], f"sibling_sum_{c}") for c, p in zip(classes, parts)]
        return owner_sum_rider(sums, [CLASS_TILE[c] for c in classes])

    (w_ffn1_in,) = run_alone(gather(["ffn1_in"]), "gather_ffn1_in")
    x1, saved1, (w_ffn1_out, w_mix_in, w_mix_attn) = ffn_forward(
        x0, ffn1_norm, w_ffn1_in, lambda rode: rode[0], "ffn1", rider=gather(["ffn1_out", "mix_in", "mix_attn"]))
    dt0, dt1 = IN_DT0 - 3 * IN_SHARD, IN_DT1 - 3 * IN_SHARD
    mixer_w = dict(
        mix_norm=mix_norm,
        w_in_main=jnp.concatenate([w_mix_in[0], w_mix_in[1], w_mix_in[2], w_mix_in[3][:, :dt0], w_mix_in[3][:, dt1:]],
                                  axis=1),
        w_in_dt=jnp.pad(w_mix_in[3][:, dt0:dt1], ((0, 0), (0, DT_PAD - SSD_HEADS))),
        q_gain=jnp.tile(q_norm, (1, 2)), k_gain=jnp.tile(k_norm, (1, 2)),
        conv_w=_from_chip_major_cols(gather_conv_w(conv_w[0])), conv_b=conv_b, dt_bias=dt_bias, a_log=a_log,
        d_skip=d_skip, ssd_norm=ssd_norm, w_attn_branch=_from_chip_major_cols(w_mix_attn))

    def later_weights(rode):
        late = rode[0]
        return dict(w_ssd_branch=late[:, 704:1216].reshape(SSD_INNER, D_MODEL),
                    w_out=late[:, 1216:1472].reshape(D_MODEL, D_MODEL))

    x2, saved_mix, (w_late_out, w_ffn2_in) = mixer_forward(x1, mixer_w, gather(["late_out", "ffn2_in"]), later_weights)
    x3, saved2, _ = ffn_forward(x2, ffn2_norm, w_ffn2_in, lambda rode: w_late_out, "ffn2")
    dx3, sq = loss_grad(x3, target, "loss")

    grads = {}
    dx2, grads["ffn2_norm"], d_ffn2_in, d_ffn2_down = ffn_backward(dx3, x2, ffn2_norm, w_ffn2_in, w_late_out, saved2,
                                                                   "ffn2")

    def ride_early(g):
        late = jnp.concatenate([d_ffn2_down, g["w_ssd_branch"].reshape(N_CHIP, -1, D_MODEL),
                                g["w_out"].reshape(N_CHIP, -1, D_MODEL)], axis=1)
        return reducer(["ffn2_in", "late_out"], [d_ffn2_in, late])

    def ride_late(g):
        main = g["w_in_main"]
        last = jnp.concatenate([main[:, 3 * IN_SHARD:IN_DT0], g["w_in_dt"][:, :SSD_HEADS], main[:, IN_DT0:]], axis=1)
        g_in = jnp.stack([main[:, j * IN_SHARD:(j + 1) * IN_SHARD] for j in range(3)] + [last])
        return reducer(["mix_in", "mix_attn"], [g_in, _chip_major_cols(g["w_attn_branch"])])

    dx1, gmix = mixer_backward(dx2, x1, saved_mix, ride_early, ride_late)
    dx0, grads["ffn1_norm"], rode_in, rode_out = ffn_backward(
        dx1, x0, ffn1_norm, w_ffn1_in, w_ffn1_out, saved1, "ffn1",
        ride_down=lambda d: reducer(["ffn1_out"], [d]), ride_in=lambda d: reducer(["ffn1_in"], [d]))
    for k in ("mix_norm", "q_norm", "k_norm", "conv_b", "dt_bias", "a_log", "d_skip", "ssd_norm"):
        grads[k] = gmix[k]
    reduced = dict(zip(("ffn2_in", "late_out", "mix_in", "mix_attn", "ffn1_in", "ffn1_out"),
                       (*gmix["rode_early"], *gmix["rode_late"], rode_in[0], rode_out[0])))
    reduced = {c: r.reshape(-1, r.shape[2]) for c, r in reduced.items()}
    summed = all_reduce_small([grads[k] for k in SMALL]
                              + [gmix["conv_w"], (0.5 * jnp.sum(sq) / D_MODEL).reshape(1, 1)])
    g_small = dict(zip(SMALL, summed))
    loss = summed[-1].reshape(())
    chip = 2 * lax.axis_index("x") + lax.axis_index("y")
    g_conv = lax.dynamic_slice_in_dim(summed[-2], chip * CONV_SHARD, CONV_SHARD, axis=1)

    g_final, delta, new_m, new_v = dict(g_small), {}, {}, {}

    def update(k, g_arr, row_off):
        w, m, v = wts[k], moms[k], vars_[k]
        rows, cols = w.shape[1:]
        if cols % 128:
            res = adamw(jnp.swapaxes(w, 1, 2), g_arr[row_off:row_off + rows].T, 0, jnp.swapaxes(m, 1, 2),
                        jnp.swapaxes(v, 1, 2), f"adamw_{k}")
            res = [jnp.swapaxes(r, 1, 2) for r in res]
        else:
            res = adamw(w, g_arr, row_off, m, v, f"adamw_{k}")
        g_final[k], delta[k], new_m[k], new_v[k] = res

    for cls, members in CLASSES.items():
        off = 0
        for k, rows in members:
            update(k, reduced[cls], off)
            off += rows
    update("conv_w", g_conv, 0)
    small = adamw_small(*([d[k] for k in SMALL] for d in (wts, g_small, moms, vars_)))
    for res, vals in zip((delta, new_m, new_v), small):
        res.update(zip(SMALL, vals))

    return (loss, dx0[None], *[g_final[k] for k in WEIGHTS], *[delta[k] for k in WEIGHTS],
            *[new_m[k] for k in WEIGHTS], *[new_v[k] for k in WEIGHTS])
```

```python
import collections
import functools
import math

import jax
import jax.numpy as jnp
from jax import lax
from jax.experimental import pallas as pl
from jax.experimental.pallas import tpu as pltpu

F32 = jnp.float32
BF16 = jnp.bfloat16
MESH = pl.DeviceIdType.MESH

EPS = 1e-6
D_MODEL = 1024
D_FF = 2816
N_CHIP = 4
FF_SHARD = D_FF // N_CHIP
HD = 64
BLK = 128
ATTN_DILATIONS = (1, 4, 16)
HEADS_PER_PATTERN = 8
N_ATTN_HEADS = 24
ALIBI_MAX_EXP = 8.0
ATTN_QKV = 1536
GROUP_W = 512
SSD_INNER = 2048
SSD_HEADS = 32
SSD_GROUPS = 4
SSD_CONV = 4
SSD_CONV_DIM = 3072
IN_COLS = 11808
IN_DT0, IN_DT1 = 9728, 9760
IN_SHARD = IN_COLS // 4
COL_K, COL_V, COL_Z, COL_XBC, COL_GA, COL_GS, P_COLS = 1536, 3072, 4608, 6656, 9728, 10752, 11776
DT_PAD = 128

ADAM_LR, ADAM_B1, ADAM_B2, ADAM_EPS, ADAM_WD, ADAM_STEP = 0.001, 0.9, 0.999, 1e-08, 0.01, 10

V7X_VMEM_LIMIT = 56 * 1024 * 1024
NEG = -1e30


Rider = collections.namedtuple("Rider", "arrays out_shape scratch start finish")
Rider.__doc__ = """An exchange between devices that rides in a compute kernel: its copies are started in the host's
first grid step and waited for in its last, so they travel while the host computes.  arrays / out_shape: extra HBM
operands and results; scratch: extra scratch; start, finish: f(in_refs, out_refs, scratch_refs)."""


def _call(body, *, name, out_shape, in_specs, out_specs, grid=(), scratch_shapes=(), aliases=None, rider=None):
    params = dict(dimension_semantics=("arbitrary",) * len(grid), vmem_limit_bytes=V7X_VMEM_LIMIT)
    if rider is None:
        return pl.pallas_call(
            body, out_shape=out_shape, grid=grid, in_specs=in_specs, out_specs=out_specs,
            scratch_shapes=scratch_shapes, input_output_aliases=aliases or {}, name=name,
            compiler_params=pltpu.CompilerParams(**params))
    single = not isinstance(out_shape, (list, tuple))
    main_out = [out_shape] if single else list(out_shape)
    main_specs = [out_specs] if single else list(out_specs)
    n_in, n_out, n_scr = len(in_specs), len(main_out), len(scratch_shapes)
    r_in, r_out = len(rider.arrays), len(rider.out_shape)

    def wrapped(*refs):
        ins, refs = refs[:n_in], refs[n_in:]
        r_ins, refs = refs[:r_in], refs[r_in:]
        outs, refs = refs[:n_out], refs[n_out:]
        r_outs, refs = refs[:r_out], refs[r_out:]
        scr, r_scr = refs[:n_scr], refs[n_scr:]
        first = last = None
        for axis, size in enumerate(grid):
            at_start, at_end = pl.program_id(axis) == 0, pl.program_id(axis) == size - 1
            first = at_start if first is None else jnp.logical_and(first, at_start)
            last = at_end if last is None else jnp.logical_and(last, at_end)

        @pl.when(first)
        def _():
            rider.start(r_ins, r_outs, r_scr)

        body(*ins, *outs, *scr)

        @pl.when(last)
        def _():
            rider.finish(r_ins, r_outs, r_scr)

    hbm = pl.BlockSpec(memory_space=pl.ANY)
    call = pl.pallas_call(
        wrapped, out_shape=main_out + list(rider.out_shape), grid=grid, in_specs=list(in_specs) + [hbm] * r_in,
        out_specs=main_specs + [hbm] * r_out, scratch_shapes=list(scratch_shapes) + list(rider.scratch), name=name,
        compiler_params=pltpu.CompilerParams(has_side_effects=True, **params))

    def run(*args):
        res = call(*args, *rider.arrays)
        main = res[:n_out]
        return (main[0] if single else main), res[n_out:]

    return run


def _sds(shape, dtype):
    return jax.ShapeDtypeStruct(tuple(shape), dtype)


def _dot(a, b):
    return jnp.dot(a, b, preferred_element_type=F32)


def _dot_nt(a, b):
    return lax.dot_general(a, b, (((1,), (1,)), ((), ())), preferred_element_type=F32)


def _dot_tn(a, b):
    return lax.dot_general(a, b, (((0,), (0,)), ((), ())), preferred_element_type=F32)


def _dot_hi(a, b):
    return jnp.dot(a, b, preferred_element_type=F32, precision=lax.Precision.HIGHEST)


def _sigmoid(x):
    return 1.0 / (1.0 + jnp.exp(-x))


def _lane_first_half(shape):
    return lax.broadcasted_iota(jnp.int32, shape, len(shape) - 1) < HD


def _pair_sum(x, first):
    s_all = jnp.sum(x, axis=-1, keepdims=True)
    s_a = jnp.sum(jnp.where(first, x, 0.0), axis=-1, keepdims=True)
    return s_a, s_all - s_a


def _rowwise(name, fn, rows, consts, outs, accs=(), tm=512, rider=None):
    n_rows = None
    in_arrays, in_specs = [], []
    for r in rows:
        if isinstance(r, tuple):
            arr, w, cb = r
            spec = pl.BlockSpec((tm, w), functools.partial(lambda i, cb: (i, cb), cb=cb))
        else:
            arr = r
            spec = pl.BlockSpec((tm, arr.shape[1]), lambda i: (i, 0))
        n_rows = arr.shape[0]
        in_arrays.append(arr)
        in_specs.append(spec)
    for c in consts:
        in_arrays.append(c)
        in_specs.append(pl.BlockSpec(c.shape, functools.partial(lambda i, n: (0,) * n, n=c.ndim)))
    out_shape = [_sds(s, d) for s, d in outs] + [_sds(s, d) for s, d in accs]
    out_specs = [pl.BlockSpec((tm, s[1]), lambda i: (i, 0)) for s, _ in outs]
    out_specs += [pl.BlockSpec(s, functools.partial(lambda i, n: (0,) * n, n=len(s))) for s, _ in accs]

    def body(*refs):
        fn(pl.program_id(0), *refs)

    res = _call(body, name=name, out_shape=out_shape, grid=(n_rows // tm,), in_specs=in_specs,
                out_specs=out_specs, rider=rider)(*in_arrays)
    return res


def rms_fwd(x, gain, name, rider=None):
    def fn(i, x_ref, g_ref, h_ref):
        xv = x_ref[...]
        r = lax.rsqrt(jnp.mean(xv * xv, axis=-1, keepdims=True) + EPS)
        h_ref[...] = (xv * r * g_ref[...]).astype(h_ref.dtype)

    res = _rowwise(name, fn, [x], [gain], [(x.shape, BF16)], rider=rider)
    return res[0] if rider is None else (res[0][0], res[1])


def rms_bwd(dhs, x, gain, dx_in, name):
    n = len(dhs)

    def fn(i, *refs):
        dh_refs, (x_ref, dxin_ref, g_ref, dx_ref, dg_ref) = refs[:n], refs[n:]
        dh = dh_refs[0][...]
        for r in dh_refs[1:]:
            dh = dh + r[...]
        xv = x_ref[...]
        r = lax.rsqrt(jnp.mean(xv * xv, axis=-1, keepdims=True) + EPS)
        xn = xv * r
        dxn = dh * g_ref[...]
        dx_ref[...] = dxin_ref[...] + r * (dxn - xn * jnp.mean(dxn * xn, axis=-1, keepdims=True))

        @pl.when(i == 0)
        def _():
            dg_ref[...] = jnp.zeros_like(dg_ref)

        dg_ref[...] += jnp.sum(dh * xn, axis=0, keepdims=True)

    return _rowwise(name, fn, list(dhs) + [x, dx_in], [gain], [(x.shape, F32)], [((1, x.shape[1]), F32)])


def loss_grad(y, target, name):
    def fn(i, y_ref, t_ref, dy_ref, sq_ref):
        err = y_ref[...] - t_ref[...]
        dy_ref[...] = err * (1.0 / y_ref.shape[1])

        @pl.when(i == 0)
        def _():
            sq_ref[...] = jnp.zeros_like(sq_ref)

        sq_ref[...] += jnp.sum(err * err, axis=0, keepdims=True)

    return _rowwise(name, fn, [y, target], [], [(y.shape, F32)], [((1, y.shape[1]), F32)])


def matmul_nn(a, b, name, out_dtype, tm, tn, res=None, scale=1.0, rider=None):
    s, k = a.shape
    n = b.shape[1]

    def body(*refs):
        if res is None:
            a_ref, b_ref, o_ref = refs
            o_ref[...] = _dot(a_ref[...], b_ref[...]).astype(o_ref.dtype)
        else:
            a_ref, b_ref, r_ref, o_ref = refs
            o_ref[...] = (r_ref[...] + scale * _dot(a_ref[...], b_ref[...])).astype(o_ref.dtype)

    in_specs = [pl.BlockSpec((tm, k), lambda i, j: (i, 0)), pl.BlockSpec((k, tn), lambda i, j: (0, j))]
    args = [a, b]
    if res is not None:
        in_specs.append(pl.BlockSpec((tm, tn), lambda i, j: (i, j)))
        args.append(res)
    return _call(body, name=name, out_shape=_sds((s, n), out_dtype), grid=(s // tm, n // tn), in_specs=in_specs,
                 out_specs=pl.BlockSpec((tm, tn), lambda i, j: (i, j)), rider=rider)(*args)


def matmul_nt(a, b, name, out_dtype, tm, tn, tk, rider=None):
    s, k = a.shape
    n = b.shape[0]
    nk = k // tk

    def body(a_ref, b_ref, o_ref, acc_ref):
        kk = pl.program_id(2)

        @pl.when(kk == 0)
        def _():
            acc_ref[...] = jnp.zeros_like(acc_ref)

        acc_ref[...] += _dot_nt(a_ref[...].astype(BF16), b_ref[...])

        @pl.when(kk == nk - 1)
        def _():
            o_ref[...] = acc_ref[...].astype(o_ref.dtype)

    return _call(body, name=name, out_shape=_sds((s, n), out_dtype), grid=(s // tm, n // tn, nk),
                 in_specs=[pl.BlockSpec((tm, tk), lambda i, j, kk: (i, kk)),
                           pl.BlockSpec((tn, tk), lambda i, j, kk: (j, kk))],
                 out_specs=pl.BlockSpec((tm, tn), lambda i, j, kk: (i, j)),
                 scratch_shapes=[pltpu.VMEM((tm, tn), F32)], rider=rider)(a, b)


def matmul_tn(a, b, name, tn, ts, a_scale=None, b_scale=None, rider=None):
    s, m = a.shape
    n = b.shape[1]
    ns = s // ts

    def body(a_ref, b_ref, o_ref, acc_ref):
        ss = pl.program_id(1)

        @pl.when(ss == 0)
        def _():
            acc_ref[...] = jnp.zeros_like(acc_ref)

        av, bv = a_ref[...], b_ref[...]
        if a_scale is not None:
            av = av * a_scale
        if b_scale is not None:
            bv = bv * b_scale
        acc_ref[...] += _dot_tn(av.astype(BF16), bv.astype(BF16))

        @pl.when(ss == ns - 1)
        def _():
            o_ref[...] = acc_ref[...].astype(o_ref.dtype)

    return _call(body, name=name, out_shape=_sds((m, n), BF16), grid=(n // tn, ns),
                 in_specs=[pl.BlockSpec((ts, m), lambda j, ss: (ss, 0)), pl.BlockSpec((ts, tn), lambda j, ss: (ss, j))],
                 out_specs=pl.BlockSpec((m, tn), lambda j, ss: (0, j)),
                 scratch_shapes=[pltpu.VMEM((m, tn), F32)], rider=rider)(a, b)


def _piece_specs(pieces, tile, rows_tile, tile_axis_first):
    specs, ranges, t0 = [], [], 0
    for a in pieces:
        n = a.shape[1] // tile

        def index(*ids, t0=t0, n=n):
            t, r = (ids[0], ids[1]) if tile_axis_first else (ids[2], ids[0])
            on = jnp.logical_and(t >= t0, t < t0 + n)
            return jnp.where(on, r, 0), jnp.clip(t - t0, 0, n - 1)

        specs.append(pl.BlockSpec((rows_tile, tile), index))
        ranges.append((t0, n))
        t0 += n
    return specs, ranges


def matmul_tn_pieces(a, pieces, name, tn, ts, rider=None):
    s, m = a.shape
    ns = s // ts
    specs, ranges = _piece_specs(pieces, tn, ts, True)
    n_total = sum(n for _, n in ranges)

    def body(a_ref, *refs):
        b_refs, o_ref, acc_ref = refs[:len(pieces)], refs[-2], refs[-1]
        j, ss = pl.program_id(0), pl.program_id(1)

        @pl.when(ss == 0)
        def _():
            acc_ref[...] = jnp.zeros_like(acc_ref)

        for b_ref, (t0, n) in zip(b_refs, ranges):
            @pl.when(jnp.logical_and(j >= t0, j < t0 + n))
            def _(b_ref=b_ref):
                acc_ref[...] += _dot_tn(a_ref[...], b_ref[...])

        @pl.when(ss == ns - 1)
        def _():
            o_ref[...] = acc_ref[...].astype(o_ref.dtype)

    return _call(body, name=name, out_shape=_sds((m, n_total * tn), BF16), grid=(n_total, ns),
                 in_specs=[pl.BlockSpec((ts, m), lambda j, ss: (ss, 0))] + specs,
                 out_specs=pl.BlockSpec((m, tn), lambda j, ss: (0, j)),
                 scratch_shapes=[pltpu.VMEM((m, tn), F32)], rider=rider)(a, *pieces)


def matmul_nt_pieces(pieces, b, name, out_dtype, tm, tn, tk, rider=None):
    s = pieces[0].shape[0]
    n = b.shape[0]
    specs, ranges = _piece_specs(pieces, tk, tm, False)
    nk = sum(cnt for _, cnt in ranges)

    def body(*refs):
        a_refs, b_ref, o_ref, acc_ref = refs[:len(pieces)], refs[-3], refs[-2], refs[-1]
        kk = pl.program_id(2)

        @pl.when(kk == 0)
        def _():
            acc_ref[...] = jnp.zeros_like(acc_ref)

        for a_ref, (t0, cnt) in zip(a_refs, ranges):
            @pl.when(jnp.logical_and(kk >= t0, kk < t0 + cnt))
            def _(a_ref=a_ref):
                acc_ref[...] += _dot_nt(a_ref[...], b_ref[...])

        @pl.when(kk == nk - 1)
        def _():
            o_ref[...] = acc_ref[...].astype(o_ref.dtype)

    return _call(body, name=name, out_shape=_sds((s, n), out_dtype), grid=(s // tm, n // tn, nk),
                 in_specs=specs + [pl.BlockSpec((tn, tk), lambda i, j, kk: (j, kk))],
                 out_specs=pl.BlockSpec((tm, tn), lambda i, j, kk: (i, j)),
                 scratch_shapes=[pltpu.VMEM((tm, tn), F32)], rider=rider)(*pieces, b)


def ffn_up(h, w704, gate_blk, up_blk, name, tm=512, rider=None):
    s = h.shape[0]

    def body(h_ref, wg_ref, wu_ref, g_ref, u_ref, a_ref):
        hv = h_ref[...]
        g = _dot(hv, wg_ref[...])
        u = _dot(hv, wu_ref[...])
        g_ref[...] = g.astype(BF16)
        u_ref[...] = u.astype(BF16)
        a_ref[...] = (g * _sigmoid(g) * u).astype(BF16)

    ospec = pl.BlockSpec((None, tm, FF_SHARD), lambda j, i: (j, i, 0))
    shp = _sds((N_CHIP, s, FF_SHARD), BF16)
    return _call(body, name=name, out_shape=[shp, shp, shp], grid=(N_CHIP, s // tm),
                 in_specs=[pl.BlockSpec((tm, D_MODEL), lambda j, i: (i, 0)),
                           pl.BlockSpec((None, D_MODEL, FF_SHARD), lambda j, i: (j, gate_blk, 0)),
                           pl.BlockSpec((None, D_MODEL, FF_SHARD), lambda j, i: (j, up_blk, 0))],
                 out_specs=[ospec, ospec, ospec], rider=rider)(h, w704, w704)


def ffn_down(a, w1024, blk, x, name, tm=512):
    s = x.shape[0]

    def body(a_ref, wd_ref, x_ref, o_ref):
        acc = _dot(a_ref[0], wd_ref[0])
        for j in range(1, N_CHIP):
            acc += _dot(a_ref[j], wd_ref[j])
        o_ref[...] = x_ref[...] + 0.5 * acc

    return _call(body, name=name, out_shape=_sds((s, D_MODEL), F32), grid=(s // tm,),
                 in_specs=[pl.BlockSpec((N_CHIP, tm, FF_SHARD), lambda i: (0, i, 0)),
                           pl.BlockSpec((N_CHIP, FF_SHARD, D_MODEL), lambda i: (0, blk, 0)),
                           pl.BlockSpec((tm, D_MODEL), lambda i: (i, 0))],
                 out_specs=pl.BlockSpec((tm, D_MODEL), lambda i: (i, 0)))(a, w1024, x)


def ffn_bwd_hidden(dx, w1024, blk, g, u, name, tm=512, rider=None):
    s = dx.shape[0]

    def body(dx_ref, wd_ref, g_ref, u_ref, dg_ref, du_ref):
        dy = (0.5 * dx_ref[...]).astype(BF16)
        da = _dot_nt(dy, wd_ref[...])
        gv = g_ref[...].astype(F32)
        uv = u_ref[...].astype(F32)
        sg = _sigmoid(gv)
        dg_ref[...] = (da * uv * (sg * (1.0 + gv * (1.0 - sg)))).astype(BF16)
        du_ref[...] = (da * gv * sg).astype(BF16)

    hspec = pl.BlockSpec((None, tm, FF_SHARD), lambda j, i: (j, i, 0))
    shp = _sds((N_CHIP, s, FF_SHARD), BF16)
    return _call(body, name=name, out_shape=[shp, shp], grid=(N_CHIP, s // tm),
                 in_specs=[pl.BlockSpec((tm, D_MODEL), lambda j, i: (i, 0)),
                           pl.BlockSpec((None, FF_SHARD, D_MODEL), lambda j, i: (j, blk, 0)), hspec, hspec],
                 out_specs=[hspec, hspec], rider=rider)(dx, w1024, g, u)


def ffn_bwd_input(dg, du, w704, gate_blk, up_blk, name, tm=512, rider=None):
    s = dg.shape[1]

    def body(dg_ref, du_ref, wg_ref, wu_ref, o_ref):
        acc = _dot_nt(dg_ref[0], wg_ref[0]) + _dot_nt(du_ref[0], wu_ref[0])
        for j in range(1, N_CHIP):
            acc += _dot_nt(dg_ref[j], wg_ref[j]) + _dot_nt(du_ref[j], wu_ref[j])
        o_ref[...] = acc

    hspec = pl.BlockSpec((N_CHIP, tm, FF_SHARD), lambda i: (0, i, 0))
    return _call(body, name=name, out_shape=_sds((s, D_MODEL), F32), grid=(s // tm,),
                 in_specs=[hspec, hspec,
                           pl.BlockSpec((N_CHIP, D_MODEL, FF_SHARD), lambda i: (0, gate_blk, 0), pl.Buffered(1)),
                           pl.BlockSpec((N_CHIP, D_MODEL, FF_SHARD), lambda i: (0, up_blk, 0), pl.Buffered(1))],
                 out_specs=pl.BlockSpec((tm, D_MODEL), lambda i: (i, 0)), rider=rider)(dg, du, w704, w704)


def ffn_wgrad_in(h, dgu, name, ts=1024):
    s = h.shape[0]
    ns = s // ts

    def body(h_ref, d_ref, o_ref, acc_ref):
        ss = pl.program_id(1)

        @pl.when(ss == 0)
        def _():
            acc_ref[...] = jnp.zeros_like(acc_ref)

        acc_ref[...] += _dot_tn(h_ref[...], d_ref[...])

        @pl.when(ss == ns - 1)
        def _():
            o_ref[...] = acc_ref[...].astype(BF16)

    return _call(body, name=name, out_shape=_sds((N_CHIP, D_MODEL, FF_SHARD), BF16), grid=(N_CHIP, ns),
                 in_specs=[pl.BlockSpec((ts, D_MODEL), lambda j, ss: (ss, 0)),
                           pl.BlockSpec((None, ts, FF_SHARD), lambda j, ss: (j, ss, 0))],
                 out_specs=pl.BlockSpec((None, D_MODEL, FF_SHARD), lambda j, ss: (j, 0, 0)),
                 scratch_shapes=[pltpu.VMEM((D_MODEL, FF_SHARD), F32)])(h, dgu)


def ffn_wgrad_down(a, dx, name, ts=1024):
    s = dx.shape[0]
    ns = s // ts

    def body(a_ref, dx_ref, o_ref, acc_ref):
        ss = pl.program_id(1)

        @pl.when(ss == 0)
        def _():
            acc_ref[...] = jnp.zeros_like(acc_ref)

        acc_ref[...] += _dot_tn(a_ref[...], (0.5 * dx_ref[...]).astype(BF16))

        @pl.when(ss == ns - 1)
        def _():
            o_ref[...] = acc_ref[...].astype(BF16)

    return _call(body, name=name, out_shape=_sds((N_CHIP, FF_SHARD, D_MODEL), BF16), grid=(N_CHIP, ns),
                 in_specs=[pl.BlockSpec((None, ts, FF_SHARD), lambda j, ss: (j, ss, 0)),
                           pl.BlockSpec((ts, D_MODEL), lambda j, ss: (ss, 0))],
                 out_specs=pl.BlockSpec((None, FF_SHARD, D_MODEL), lambda j, ss: (j, 0, 0)),
                 scratch_shapes=[pltpu.VMEM((FF_SHARD, D_MODEL), F32)])(a, dx)


def ffn_forward(x, gain, get_w704, get_w1024, tag, rms_rider=None, up_rider=None):
    h = rms_fwd(x, gain, f"{tag}_rms", rider=rms_rider)
    h, rode_rms = h if rms_rider is not None else (h, None)
    res = ffn_up(h, get_w704(rode_rms), 0, 1, f"{tag}_up", rider=up_rider)
    (g, u, a), rode_up = res if up_rider is not None else (res, None)
    y = ffn_down(a, get_w1024(rode_up), 0, x, f"{tag}_down")
    return y, (h, g, u, a), rode_rms, rode_up


def ffn_backward(dy, x, gain, w704, w1024, saved, tag, ride_down=None, ride_in=None):
    h, g, u, a = saved
    d_wd = ffn_wgrad_down(a, dy, f"{tag}_dwd")
    if ride_down is not None:
        (dg, du), d_wd = ffn_bwd_hidden(dy, w1024, 0, g, u, f"{tag}_dhid", rider=ride_down(d_wd))
    else:
        dg, du = ffn_bwd_hidden(dy, w1024, 0, g, u, f"{tag}_dhid")
    d_win = jnp.concatenate([ffn_wgrad_in(h, dg, f"{tag}_dwg"), ffn_wgrad_in(h, du, f"{tag}_dwu")], axis=1)
    if ride_in is not None:
        dh, d_win = ffn_bwd_input(dg, du, w704, 0, 1, f"{tag}_dh", rider=ride_in(d_win))
    else:
        dh = ffn_bwd_input(dg, du, w704, 0, 1, f"{tag}_dh")
    dx, d_gain = rms_bwd([dh], x, gain, dy, f"{tag}_drms")
    return dx, d_gain, d_win, d_wd


def _alibi_slope(head):
    return float(2.0 ** (-ALIBI_MAX_EXP * (head + 1) / N_ATTN_HEADS))


def _head_norm(t, gain_pair, first):
    sa, sb = _pair_sum(t * t, first)
    r = jnp.where(first, lax.rsqrt(sa * (1.0 / HD) + EPS), lax.rsqrt(sb * (1.0 / HD) + EPS))
    return t * r * gain_pair, r


def qk_norm_fwd(p, q_gain, k_gain, name):
    s = p.shape[0]

    def fn(i, q_ref, k_ref, qg_ref, kg_ref, qn_ref, kn_ref):
        first = _lane_first_half((q_ref.shape[0], 2 * HD))
        for src, g_ref, dst in ((q_ref, qg_ref, qn_ref), (k_ref, kg_ref, kn_ref)):
            for pr in range(ATTN_QKV // (2 * HD)):
                cols = slice(pr * 2 * HD, (pr + 1) * 2 * HD)
                y, _ = _head_norm(src[:, cols].astype(F32), g_ref[...], first)
                dst[:, cols] = y.astype(BF16)

    return _rowwise(name, fn, [(p, ATTN_QKV, 0), (p, ATTN_QKV, 1)], [q_gain, k_gain],
                    [((s, ATTN_QKV), BF16), ((s, ATTN_QKV), BF16)])


def qk_norm_bwd(p, dqs, dks, q_gain, k_gain, name):
    s = p.shape[0]
    pairs_per_pattern = GROUP_W // (2 * HD)

    def fn(i, q_ref, k_ref, dq0, dq1, dq2, dk0, dk1, dk2, qg_ref, kg_ref, dqk_ref, dqg_ref, dkg_ref):
        first = _lane_first_half((q_ref.shape[0], 2 * HD))

        @pl.when(i == 0)
        def _():
            dqg_ref[...] = jnp.zeros_like(dqg_ref)
            dkg_ref[...] = jnp.zeros_like(dkg_ref)

        for src, d_refs, g_ref, dst, dg_ref in (
                (q_ref, (dq0, dq1, dq2), qg_ref, dqk_ref.at[:, 0:ATTN_QKV], dqg_ref),
                (k_ref, (dk0, dk1, dk2), kg_ref, dqk_ref.at[:, ATTN_QKV:2 * ATTN_QKV], dkg_ref)):
            for pr in range(ATTN_QKV // (2 * HD)):
                cols = slice(pr * 2 * HD, (pr + 1) * 2 * HD)
                t = src[:, cols].astype(F32)
                sa, sb = _pair_sum(t * t, first)
                r = jnp.where(first, lax.rsqrt(sa * (1.0 / HD) + EPS), lax.rsqrt(sb * (1.0 / HD) + EPS))
                xn = t * r
                within = (pr % pairs_per_pattern) * 2 * HD
                dy = d_refs[pr // pairs_per_pattern][:, within:within + 2 * HD]
                dg_ref[:, cols] += jnp.sum(dy * xn, axis=0, keepdims=True)
                dxn = dy * g_ref[...]
                ma, mb = _pair_sum(dxn * xn, first)
                mean = jnp.where(first, ma, mb) * (1.0 / HD)
                dst[:, cols] = (r * (dxn - xn * mean)).astype(BF16)

    return _rowwise(name, fn, [(p, ATTN_QKV, 0), (p, ATTN_QKV, 1)] + list(dqs) + list(dks), [q_gain, k_gain],
                    [((s, 2 * ATTN_QKV), BF16)], [((1, ATTN_QKV), F32), ((1, ATTN_QKV), F32)])


def _to_streams(a, d):
    if d == 1:
        return a
    s, c = a.shape
    return a.reshape(s // d, d, c).transpose(1, 0, 2).reshape(s, c)


def _from_streams(a, d):
    if d == 1:
        return a
    s, c = a.shape
    return a.reshape(d, s // d, c).transpose(1, 0, 2).reshape(s, c)


def _attn_masks():
    row = lax.broadcasted_iota(jnp.int32, (BLK, BLK), 0)
    col = lax.broadcasted_iota(jnp.int32, (BLK, BLK), 1)
    rel_diag = row - col
    rel_prev = rel_diag + BLK
    return rel_diag, rel_prev


def attn_fwd(q, k, v, pattern, name, tq=512):
    s = q.shape[0]
    d = ATTN_DILATIONS[pattern]
    blocks_per_stream = (s // d) // BLK
    nsb = tq // BLK

    def body(q_ref, k_ref, v_ref, kp_ref, vp_ref, o_ref, l_ref):
        i = pl.program_id(0)
        rel_diag, rel_prev = _attn_masks()
        first = _lane_first_half((BLK, 2 * HD))
        rd_f = (rel_diag * d).astype(F32)
        rp_f = (rel_prev * d).astype(F32)
        for sb in range(nsb):
            rows = slice(sb * BLK, (sb + 1) * BLK)
            has_prev = ((i * nsb + sb) % blocks_per_stream != 0).astype(jnp.int32)
            m_diag = rel_diag >= 0
            m_prev = (rel_prev + (1 - has_prev) * (4 * BLK)) <= BLK
            for pr in range(GROUP_W // (2 * HD)):
                cols = slice(pr * 2 * HD, (pr + 1) * 2 * HD)
                qp = q_ref[rows, cols]
                kc, vc = k_ref[rows, cols], v_ref[rows, cols]
                if sb == 0:
                    kp, vp = kp_ref[:, cols], vp_ref[:, cols]
                else:
                    prows = slice((sb - 1) * BLK, sb * BLK)
                    kp, vp = k_ref[prows, cols], v_ref[prows, cols]
                outs, lses = [], []
                for e in range(2):
                    slope = _alibi_slope(pattern * HEADS_PER_PATTERN + 2 * pr + e)
                    qm = jnp.where(first if e == 0 else jnp.logical_not(first), qp, jnp.zeros_like(qp))
                    s1 = jnp.where(m_diag, _dot_nt(qm, kc) * 0.125 - slope * rd_f, NEG)
                    s0 = jnp.where(m_prev, _dot_nt(qm, kp) * 0.125 - slope * rp_f, NEG)
                    m = jnp.maximum(jnp.max(s1, axis=-1, keepdims=True), jnp.max(s0, axis=-1, keepdims=True))
                    p1 = jnp.exp(s1 - m)
                    p0 = jnp.exp(s0 - m)
                    l = jnp.sum(p1, axis=-1, keepdims=True) + jnp.sum(p0, axis=-1, keepdims=True)
                    inv = 1.0 / l
                    outs.append(_dot((p1 * inv).astype(BF16), vc) + _dot((p0 * inv).astype(BF16), vp))
                    lses.append(m + jnp.log(l))
                o_ref[rows, cols] = jnp.where(first, outs[0], outs[1])
                l_ref[rows, cols] = jnp.where(first, lses[0], lses[1])

    cur = pl.BlockSpec((tq, GROUP_W), lambda i: (i, 0))
    prev = pl.BlockSpec((BLK, GROUP_W), lambda i: (jnp.maximum(i * nsb - 1, 0), 0))
    return _call(body, name=name, out_shape=[_sds((s, GROUP_W), F32), _sds((s, GROUP_W), F32)], grid=(s // tq,),
                 in_specs=[cur, cur, cur, prev, prev], out_specs=[cur, cur])(q, k, v, k, v)


def attn_merge_fwd(os_, lses, name):
    s = os_[0].shape[0]

    def fn(i, o0, o1, o2, l0, l1, l2, out_ref):
        m = jnp.maximum(jnp.maximum(l0[...], l1[...]), l2[...])
        e0, e1, e2 = jnp.exp(l0[...] - m), jnp.exp(l1[...] - m), jnp.exp(l2[...] - m)
        inv = 1.0 / (e0 + e1 + e2)
        out_ref[...] = ((e0 * inv) * o0[...] + (e1 * inv) * o1[...] + (e2 * inv) * o2[...]).astype(BF16)

    return _rowwise(name, fn, list(os_) + list(lses), [], [((s, GROUP_W), BF16)])[0]


def attn_merge_bwd(d_out, os_, lses, name):
    s = d_out.shape[0]

    def fn(i, do_ref, o0, o1, o2, l0, l1, l2, d0, d1, d2, c0, c1, c2):
        first = _lane_first_half((do_ref.shape[0], 2 * HD))
        m = jnp.maximum(jnp.maximum(l0[...], l1[...]), l2[...])
        e0, e1, e2 = jnp.exp(l0[...] - m), jnp.exp(l1[...] - m), jnp.exp(l2[...] - m)
        inv = 1.0 / (e0 + e1 + e2)
        w0, w1, w2 = e0 * inv, e1 * inv, e2 * inv
        do = do_ref[...]
        prod = do * (w0 * o0[...] + w1 * o1[...] + w2 * o2[...])
        for pr in range(GROUP_W // (2 * HD)):
            cols = slice(pr * 2 * HD, (pr + 1) * 2 * HD)
            ta, tb = _pair_sum(prod[:, cols], first)
            t = jnp.where(first, ta, tb)
            for w, c_ref in ((w0, c0), (w1, c1), (w2, c2)):
                c_ref[:, cols] = w[:, cols] * t
        for w, d_ref in ((w0, d0), (w1, d1), (w2, d2)):
            d_ref[...] = (w * do).astype(BF16)

    shp = (s, GROUP_W)
    return _rowwise(name, fn, [d_out] + list(os_) + list(lses), [],
                    [(shp, BF16)] * 3 + [(shp, F32)] * 3)


def attn_bwd(q, k, v, d_o, cterm, lse, pattern, name, tq=512):
    s = q.shape[0]
    d = ATTN_DILATIONS[pattern]
    blocks_per_stream = (s // d) // BLK
    nsb = tq // BLK
    n_blocks = s // BLK

    def body(q_ref, k_ref, v_ref, do_ref, c_ref, l_ref, kp_ref, vp_ref, qn_ref, don_ref, cn_ref, ln_ref,
             dq_ref, dk_ref, dv_ref):
        i = pl.program_id(0)
        rel_diag, rel_prev = _attn_masks()
        first = _lane_first_half((BLK, 2 * HD))
        second = jnp.logical_not(first)
        rd_f = (rel_diag * d).astype(F32)
        rp_f = (rel_prev * d).astype(F32)
        m_diag = rel_diag >= 0
        dq_ref[...] = jnp.zeros_like(dq_ref)
        dk_ref[...] = jnp.zeros_like(dk_ref)
        dv_ref[...] = jnp.zeros_like(dv_ref)

        def pair(qp, dop, cp, lp, kp, vp, rel_f, mask):
            dq = dk = dv = None
            for e in range(2):
                lanes = first if e == 0 else second
                slope = slopes[e]
                qm = jnp.where(lanes, qp, jnp.zeros_like(qp))
                dom = jnp.where(lanes, dop, jnp.zeros_like(dop))
                km = jnp.where(lanes, kp, jnp.zeros_like(kp))
                sc = jnp.where(mask, _dot_nt(qm, kp) * 0.125 - slope * rel_f, NEG)
                pm = jnp.exp(sc - lp[:, e * HD:e * HD + 1])
                dl = pm * (_dot_nt(dom, vp) - cp[:, e * HD:e * HD + 1])
                dl16 = dl.astype(BF16)
                t_dq = _dot(dl16, km)
                t_dk = _dot_tn(dl16, qm)
                t_dv = _dot_tn(pm.astype(BF16), dom)
                dq = t_dq if dq is None else dq + t_dq
                dk = t_dk if dk is None else dk + t_dk
                dv = t_dv if dv is None else dv + t_dv
            return dq * 0.125, dk * 0.125, dv

        for pr in range(GROUP_W // (2 * HD)):
            cols = slice(pr * 2 * HD, (pr + 1) * 2 * HD)
            slopes = [_alibi_slope(pattern * HEADS_PER_PATTERN + 2 * pr + e) for e in range(2)]
            for sb in range(nsb + 1):
                gb = i * nsb + sb
                if sb < nsb:
                    rows = slice(sb * BLK, (sb + 1) * BLK)
                    qp, dop, cp, lp = q_ref[rows, cols], do_ref[rows, cols], c_ref[rows, cols], l_ref[rows, cols]
                else:
                    qp, dop, cp, lp = qn_ref[:, cols], don_ref[:, cols], cn_ref[:, cols], ln_ref[:, cols]
                if sb < nsb:
                    dq1, dk1, dv1 = pair(qp, dop, cp, lp, k_ref[rows, cols], v_ref[rows, cols], rd_f, m_diag)
                    dq_ref[rows, cols] += dq1
                    dk_ref[rows, cols] += dk1
                    dv_ref[rows, cols] += dv1
                valid = jnp.logical_and(gb % blocks_per_stream != 0, gb < n_blocks).astype(jnp.int32)
                m_prev = jnp.logical_and(rel_prev <= BLK, (rel_prev + (1 - valid) * (4 * BLK)) <= BLK)
                if sb == 0:
                    kp, vp = kp_ref[:, cols], vp_ref[:, cols]
                else:
                    prows = slice((sb - 1) * BLK, sb * BLK)
                    kp, vp = k_ref[prows, cols], v_ref[prows, cols]
                dq0, dk0, dv0 = pair(qp, dop, cp, lp, kp, vp, rp_f, m_prev)
                if sb < nsb:
                    dq_ref[rows, cols] += dq0
                if sb > 0:
                    dk_ref[prows, cols] += dk0
                    dv_ref[prows, cols] += dv0

    cur = pl.BlockSpec((tq, GROUP_W), lambda i: (i, 0))
    prev = pl.BlockSpec((BLK, GROUP_W), lambda i: (jnp.maximum(i * nsb - 1, 0), 0))
    nxt = pl.BlockSpec((BLK, GROUP_W), lambda i: (jnp.minimum((i + 1) * nsb, n_blocks - 1), 0))
    shp = _sds((s, GROUP_W), F32)
    return _call(body, name=name, out_shape=[shp, shp, shp], grid=(s // tq,),
                 in_specs=[cur] * 6 + [prev, prev] + [nxt] * 4, out_specs=[cur, cur, cur])(
                     q, k, v, d_o, cterm, lse, k, v, q, d_o, cterm, lse)


def _band_constants(d):
    row = lax.broadcasted_iota(jnp.int32, (2 * BLK, 2 * BLK), 0)
    col = lax.broadcasted_iota(jnp.int32, (2 * BLK, 2 * BLK), 1)
    rel = BLK + jnp.where(row >= BLK, row - BLK, row) - col
    band = jnp.logical_and(rel >= 0, rel <= BLK)
    return (rel * d).astype(F32), band, (col >= BLK).astype(jnp.int32)


def _stack_heads(x, first):
    zero = jnp.zeros_like(x)
    return jnp.concatenate([jnp.where(first, x, zero), jnp.where(first, zero, x)], axis=0)


def _unstack_heads(x2, first):
    return jnp.where(first, x2[:BLK], x2[BLK:])


def _head_column(x):
    return jnp.concatenate([x[:, 0:1], x[:, HD:HD + 1]], axis=0)


def attn_fwd2(q, k, v, pattern, name, tq=512):
    s = q.shape[0]
    d = ATTN_DILATIONS[pattern]
    blocks_per_stream = (s // d) // BLK
    nsb = tq // BLK

    def body(q_ref, k_ref, v_ref, kp_ref, vp_ref, o_ref, l_ref):
        i = pl.program_id(0)
        rel_f, band, own = _band_constants(d)
        first = _lane_first_half((BLK, 2 * HD))
        upper = lax.broadcasted_iota(jnp.int32, (2 * BLK, 1), 0) < BLK
        for sb in range(nsb):
            rows = slice(sb * BLK, (sb + 1) * BLK)
            has_prev = ((i * nsb + sb) % blocks_per_stream != 0).astype(jnp.int32)
            mask = jnp.logical_and(band, (own + has_prev) > 0)
            for pr in range(GROUP_W // (2 * HD)):
                cols = slice(pr * 2 * HD, (pr + 1) * 2 * HD)
                if sb == 0:
                    kcat = jnp.concatenate([kp_ref[:, cols], k_ref[rows, cols]], axis=0)
                    vcat = jnp.concatenate([vp_ref[:, cols], v_ref[rows, cols]], axis=0)
                else:
                    both = slice((sb - 1) * BLK, (sb + 1) * BLK)
                    kcat, vcat = k_ref[both, cols], v_ref[both, cols]
                h0 = pattern * HEADS_PER_PATTERN + 2 * pr
                slope = jnp.where(upper, _alibi_slope(h0), _alibi_slope(h0 + 1))
                sc = _dot_nt(_stack_heads(q_ref[rows, cols], first), kcat) * 0.125 - slope * rel_f
                sc = jnp.where(mask, sc, NEG)
                m = jnp.max(sc, axis=-1, keepdims=True)
                p = jnp.exp(sc - m)
                l = jnp.sum(p, axis=-1, keepdims=True)
                o2 = _dot((p * (1.0 / l)).astype(BF16), vcat)
                o_ref[rows, cols] = _unstack_heads(o2, first)
                lse = m + jnp.log(l)
                l_ref[rows, cols] = jnp.where(first, lse[:BLK], lse[BLK:])

    cur = pl.BlockSpec((tq, GROUP_W), lambda i: (i, 0))
    prev = pl.BlockSpec((BLK, GROUP_W), lambda i: (jnp.maximum(i * nsb - 1, 0), 0))
    return _call(body, name=name, out_shape=[_sds((s, GROUP_W), F32), _sds((s, GROUP_W), F32)], grid=(s // tq,),
                 in_specs=[cur, cur, cur, prev, prev], out_specs=[cur, cur])(q, k, v, k, v)


def attn_bwd2(q, k, v, d_o, cterm, lse, pattern, name, tq=512):
    s = q.shape[0]
    d = ATTN_DILATIONS[pattern]
    blocks_per_stream = (s // d) // BLK
    nsb = tq // BLK
    n_blocks = s // BLK

    def body(q_ref, k_ref, v_ref, do_ref, c_ref, l_ref, kp_ref, vp_ref, qn_ref, kn_ref, vn_ref, don_ref, cn_ref,
             ln_ref, dq_ref, dk_ref, dv_ref):
        i = pl.program_id(0)
        rel_f, band, own = _band_constants(d)
        first = _lane_first_half((BLK, 2 * HD))
        upper = lax.broadcasted_iota(jnp.int32, (2 * BLK, 1), 0) < BLK
        dk_ref[...] = jnp.zeros_like(dk_ref)
        dv_ref[...] = jnp.zeros_like(dv_ref)
        for sb in range(nsb + 1):
            gb = i * nsb + sb
            rows = slice(sb * BLK, (sb + 1) * BLK)
            before = slice((sb - 1) * BLK, sb * BLK)
            inside = (gb < n_blocks).astype(jnp.int32)
            has_prev = jnp.logical_and(gb % blocks_per_stream != 0, gb < n_blocks).astype(jnp.int32)
            mask = jnp.logical_and(band, (own * inside + has_prev) > 0)
            for pr in range(GROUP_W // (2 * HD)):
                cols = slice(pr * 2 * HD, (pr + 1) * 2 * HD)
                if sb == 0:
                    kcat = jnp.concatenate([kp_ref[:, cols], k_ref[rows, cols]], axis=0)
                    vcat = jnp.concatenate([vp_ref[:, cols], v_ref[rows, cols]], axis=0)
                elif sb == nsb:
                    kcat = jnp.concatenate([k_ref[before, cols], kn_ref[:, cols]], axis=0)
                    vcat = jnp.concatenate([v_ref[before, cols], vn_ref[:, cols]], axis=0)
                else:
                    both = slice((sb - 1) * BLK, (sb + 1) * BLK)
                    kcat, vcat = k_ref[both, cols], v_ref[both, cols]
                if sb < nsb:
                    qp, dop, cp, lp = q_ref[rows, cols], do_ref[rows, cols], c_ref[rows, cols], l_ref[rows, cols]
                else:
                    qp, dop, cp, lp = qn_ref[:, cols], don_ref[:, cols], cn_ref[:, cols], ln_ref[:, cols]
                h0 = pattern * HEADS_PER_PATTERN + 2 * pr
                slope = jnp.where(upper, _alibi_slope(h0), _alibi_slope(h0 + 1))
                q2 = _stack_heads(qp, first)
                do2 = _stack_heads(dop, first)
                sc = jnp.where(mask, _dot_nt(q2, kcat) * 0.125 - slope * rel_f, NEG)
                pm = jnp.exp(sc - _head_column(lp))
                dl = (pm * (_dot_nt(do2, vcat) - _head_column(cp))).astype(BF16)
                if sb < nsb:
                    dq_ref[rows, cols] = _unstack_heads(_dot(dl, kcat), first) * 0.125
                dk2 = _dot_tn(dl, q2) * 0.125
                dv2 = _dot_tn(pm.astype(BF16), do2)
                if sb > 0:
                    dk_ref[before, cols] += dk2[:BLK]
                    dv_ref[before, cols] += dv2[:BLK]
                if sb < nsb:
                    dk_ref[rows, cols] += dk2[BLK:]
                    dv_ref[rows, cols] += dv2[BLK:]

    cur = pl.BlockSpec((tq, GROUP_W), lambda i: (i, 0))
    prev = pl.BlockSpec((BLK, GROUP_W), lambda i: (jnp.maximum(i * nsb - 1, 0), 0))
    nxt = pl.BlockSpec((BLK, GROUP_W), lambda i: (jnp.minimum((i + 1) * nsb, n_blocks - 1), 0))
    shp = _sds((s, GROUP_W), F32)
    return _call(body, name=name, out_shape=[shp, shp, shp], grid=(s // tq,),
                 in_specs=[cur] * 6 + [prev, prev] + [nxt] * 6, out_specs=[cur, cur, cur])(
                     q, k, v, d_o, cterm, lse, k, v, q, k, v, d_o, cterm, lse)


HALO = 16
CONV_TQ = 512


def conv_fwd(p, w, b, name):
    s = p.shape[0]
    tq = CONV_TQ
    ncol = SSD_CONV_DIM // GROUP_W
    cb0 = COL_XBC // GROUP_W

    def body(u_ref, up_ref, w_ref, b_ref, c_ref, xc_ref):
        i = pl.program_id(0)
        prev = up_ref[...].astype(F32) * (i > 0).astype(F32)
        ext = jnp.concatenate([prev, u_ref[...].astype(F32)], axis=0)
        acc = b_ref[...] + w_ref[SSD_CONV - 1:SSD_CONV, :] * ext[HALO:HALO + tq]
        for kk in range(SSD_CONV - 1):
            acc += w_ref[kk:kk + 1, :] * pltpu.roll(ext, SSD_CONV - 1 - kk, 0)[HALO:HALO + tq]
        c_ref[...] = acc.astype(BF16)
        xc_ref[...] = (acc * _sigmoid(acc)).astype(BF16)

    cur_in = pl.BlockSpec((tq, GROUP_W), lambda i, j: (i, cb0 + j))
    prev_in = pl.BlockSpec((HALO, GROUP_W), lambda i, j: (jnp.maximum(i * (tq // HALO) - 1, 0), cb0 + j))
    cur_out = pl.BlockSpec((tq, GROUP_W), lambda i, j: (i, j))
    shp = _sds((s, SSD_CONV_DIM), BF16)
    return _call(body, name=name, out_shape=[shp, shp], grid=(s // tq, ncol),
                 in_specs=[cur_in, prev_in, pl.BlockSpec((SSD_CONV, GROUP_W), lambda i, j: (0, j)),
                           pl.BlockSpec((1, GROUP_W), lambda i, j: (0, j))],
                 out_specs=[cur_out, cur_out])(p, p, w, b)


def conv_bwd(p, cpre, dxs, d_b, d_c, w, name):
    s = p.shape[0]
    tq = CONV_TQ
    ncol = SSD_CONV_DIM // GROUP_W
    n_xs = SSD_INNER // GROUP_W
    cb0 = COL_XBC // GROUP_W
    nt = s // tq

    def body(u_ref, up_ref, c_ref, cn_ref, dx_ref, dxn_ref, dbm_ref, dbmn_ref, dcm_ref, dcmn_ref, w_ref,
             du_ref, dw_ref, db_ref):
        j, i = pl.program_id(0), pl.program_id(1)

        def dpre(c16, dx):
            c = c16.astype(F32)
            sg = _sigmoid(c)
            return dx * (sg * (1.0 + c * (1.0 - sg)))

        def pick(a_ref, b_ref, c_ref_):
            return jnp.where(j < n_xs, a_ref[...], jnp.where(j == n_xs, b_ref[...], c_ref_[...]))

        dc = dpre(c_ref[...], pick(dx_ref, dbm_ref, dcm_ref))
        dcn = dpre(cn_ref[...], pick(dxn_ref, dbmn_ref, dcmn_ref)) * (i < nt - 1).astype(F32)
        dext = jnp.concatenate([dc, dcn], axis=0)
        prev = up_ref[...].astype(F32) * (i > 0).astype(F32)
        uext = jnp.concatenate([prev, u_ref[...].astype(F32)], axis=0)

        @pl.when(i == 0)
        def _():
            dw_ref[...] = jnp.zeros_like(dw_ref)
            db_ref[...] = jnp.zeros_like(db_ref)

        du = w_ref[SSD_CONV - 1:SSD_CONV, :] * dc
        for kk in range(SSD_CONV - 1):
            sh = SSD_CONV - 1 - kk
            du += w_ref[kk:kk + 1, :] * pltpu.roll(dext, tq + HALO - sh, 0)[0:tq]
        du_ref[...] = du.astype(BF16)
        for kk in range(SSD_CONV):
            shifted = uext if kk == SSD_CONV - 1 else pltpu.roll(uext, SSD_CONV - 1 - kk, 0)
            dw_ref[kk:kk + 1, :] += jnp.sum(dc * shifted[HALO:HALO + tq], axis=0, keepdims=True)
        db_ref[...] += jnp.sum(dc, axis=0, keepdims=True)

    hb = tq // HALO
    cur_p = pl.BlockSpec((tq, GROUP_W), lambda j, i: (i, cb0 + j))
    prev_p = pl.BlockSpec((HALO, GROUP_W), lambda j, i: (jnp.maximum(i * hb - 1, 0), cb0 + j))
    cur = pl.BlockSpec((tq, GROUP_W), lambda j, i: (i, j))
    nxt = pl.BlockSpec((HALO, GROUP_W), lambda j, i: (jnp.minimum((i + 1) * hb, s // HALO - 1), j))

    def piece(first_tile, n_tiles):
        def on(j):
            return jnp.logical_and(j >= first_tile, j < first_tile + n_tiles)

        def col(j):
            return jnp.clip(j - first_tile, 0, n_tiles - 1)

        return (pl.BlockSpec((tq, GROUP_W), lambda j, i: (jnp.where(on(j), i, 0), col(j))),
                pl.BlockSpec((HALO, GROUP_W),
                             lambda j, i: (jnp.where(on(j), jnp.minimum((i + 1) * hb, s // HALO - 1), 0), col(j))))

    return _call(body, name=name,
                 out_shape=[_sds((s, SSD_CONV_DIM), BF16), _sds((8, SSD_CONV_DIM), F32), _sds((1, SSD_CONV_DIM), F32)],
                 grid=(ncol, nt),
                 in_specs=[cur_p, prev_p, cur, nxt, *piece(0, n_xs), *piece(n_xs, 1), *piece(n_xs + 1, 1),
                           pl.BlockSpec((SSD_CONV, GROUP_W), lambda j, i: (0, j))],
                 out_specs=[cur, pl.BlockSpec((8, GROUP_W), lambda j, i: (0, j)),
                            pl.BlockSpec((1, GROUP_W), lambda j, i: (0, j))])(
                                p, p, cpre, cpre, dxs, dxs, d_b, d_b, d_c, d_c, w)


def _softplus(x):
    return jnp.maximum(x, 0.0) + jnp.log(1.0 + jnp.exp(-jnp.abs(x)))


def _ssd_decays(dtr_ref, dtrt_ref, bias_ref, biast_ref, alog_ref, alogt_ref):
    row = lax.broadcasted_iota(jnp.int32, (BLK, BLK), 0)
    col = lax.broadcasted_iota(jnp.int32, (BLK, BLK), 1)
    lower = (row >= col).astype(F32)
    upper = (row <= col).astype(F32)
    dtb = dtr_ref[...] + bias_ref[...]
    dt = _softplus(dtb)
    a = dt * (-jnp.exp(alog_ref[...]))
    cs = _dot_hi(lower, a)
    a_t = _softplus(dtrt_ref[...] + biast_ref[...]) * (-jnp.exp(alogt_ref[...]))
    cs_t = _dot_hi(a_t, upper)
    return dtb, dt, cs, cs_t, row, col, upper


SSD_GROUPS_PER_STEP = 4


def _per_group(body, gps, kinds):
    def wrapped(*refs):
        for gi in range(gps):
            args, pos = [], 0
            for kind, n in kinds:
                if kind == "each":
                    args.append(refs[pos + gi])
                    pos += gps
                    continue
                ref = refs[pos]
                pos += 1
                if kind == "cols":
                    args.append(ref.at[:, gi * n:(gi + 1) * n])
                else:
                    args.append(ref.at[gi] if n == 1 else ref.at[pl.ds(gi * n, n)])
            body(*args)

    return wrapped


def ssd_fwd(p, xc, dtg, dtg_t, params, gn, name):
    s = p.shape[0]
    nc = s // BLK
    bias, bias_t, alog, alog_t, dskip = params

    def body(xs_ref, b_ref, c_ref, z_ref, dtr_ref, dtrt_ref, bias_ref, biast_ref, alog_ref, alogt_ref, dsk_ref,
             gn_ref, y_ref, sin_ref, hp_ref, h_ref):
        c_idx = pl.program_id(1)

        @pl.when(c_idx == 0)
        def _():
            h_ref[...] = jnp.zeros_like(h_ref)

        _, dt, cs, cs_t, row, col, _ = _ssd_decays(dtr_ref, dtrt_ref, bias_ref, biast_ref, alog_ref, alogt_ref)
        first = _lane_first_half((BLK, 2 * HD))
        first_row = _lane_first_half((1, 2 * HD))
        tril = row >= col
        b16, c16 = b_ref[...], c_ref[...]
        cb = _dot_nt(c16, b16)
        n_pairs = GROUP_W // (2 * HD)
        tot = cs[BLK - 1:BLK, :]
        exp_cs, exp_rest, exp_tot = jnp.exp(cs), jnp.exp(tot - cs), jnp.exp(tot)

        def per_head(v, mask):
            return jnp.concatenate([jnp.where(mask, v[:, 2 * pr:2 * pr + 1], v[:, 2 * pr + 1:2 * pr + 2])
                                    for pr in range(n_pairs)], axis=1)

        xs = xs_ref[...].astype(F32)
        xt = xs * per_head(dt, first)
        xt16 = xt.astype(BF16)
        hstate = jnp.concatenate([h_ref[pr] for pr in range(n_pairs)], axis=1)
        for pr in range(n_pairs):
            hp_ref[pr] = h_ref[pr]
        y_off = per_head(exp_cs, first) * _dot(c16, hstate.astype(BF16))
        new = per_head(exp_tot, first_row) * hstate + _dot_tn(b16, (per_head(exp_rest, first) * xt).astype(BF16))
        for pr in range(n_pairs):
            h_ref[pr] = new[:, pr * 2 * HD:(pr + 1) * 2 * HD]
        y_diag = []
        for pr in range(n_pairs):
            cols = slice(pr * 2 * HD, (pr + 1) * 2 * HD)
            m2 = jnp.concatenate(
                [(cb * jnp.exp(jnp.where(tril, cs[:, h:h + 1] - cs_t[h:h + 1, :], NEG))).astype(BF16)
                 for h in (2 * pr, 2 * pr + 1)], axis=1)
            y_diag.append(_dot(m2, _stack_heads(xt16[:, cols], first)))
        y = jnp.concatenate(y_diag, axis=1) + y_off + xs * per_head(dsk_ref[...], first_row)
        y_ref[...] = y
        zv = z_ref[...].astype(F32)
        yz = y * (zv * _sigmoid(zv))
        r = lax.rsqrt(jnp.mean(yz * yz, axis=-1, keepdims=True) + EPS)
        sin_ref[...] = (yz * r * gn_ref[...]).astype(BF16)

    gps = SSD_GROUPS_PER_STEP
    wide, narrow, lead = ("cols", GROUP_W), ("cols", BLK), ("lead", 1)
    kinds = [wide, narrow, narrow, ("each", 0)] + [lead] * 7 + [wide, wide, wide, lead, ("lead", 4)]
    wide_w, narrow_w = GROUP_W * gps, BLK * gps
    gparam = pl.BlockSpec((gps, 1, 8), lambda g, c: (g, 0, 0))
    gparam_t = pl.BlockSpec((gps, 8, 1), lambda g, c: (g, 0, 0))
    z_specs = [pl.BlockSpec((BLK, GROUP_W), functools.partial(lambda g, c, gi: (c, COL_Z // GROUP_W + gps * g + gi),
                                                              gi=gi)) for gi in range(gps)]
    return _call(
        _per_group(body, gps, kinds), name=name,
        out_shape=[_sds((s, SSD_INNER), F32), _sds((s, SSD_INNER), BF16),
                   _sds((SSD_GROUPS, nc, 4, BLK, 2 * HD), F32)],
        grid=(SSD_GROUPS // gps, nc),
        in_specs=[pl.BlockSpec((BLK, wide_w), lambda g, c: (c, g)),
                  pl.BlockSpec((BLK, narrow_w), lambda g, c: (c, SSD_INNER // narrow_w + g)),
                  pl.BlockSpec((BLK, narrow_w), lambda g, c: (c, (SSD_INNER + SSD_GROUPS * BLK) // narrow_w + g)),
                  *z_specs,
                  pl.BlockSpec((gps, BLK, 8), lambda g, c: (g, c, 0)),
                  pl.BlockSpec((gps, 8, BLK), lambda g, c: (g, 0, c)),
                  gparam, gparam_t, gparam, gparam_t, gparam,
                  pl.BlockSpec((1, wide_w), lambda g, c: (0, g))],
        out_specs=[pl.BlockSpec((BLK, wide_w), lambda g, c: (c, g)),
                   pl.BlockSpec((BLK, wide_w), lambda g, c: (c, g)),
                   pl.BlockSpec((gps, None, 4, BLK, 2 * HD), lambda g, c: (g, c, 0, 0, 0))],
        scratch_shapes=[pltpu.VMEM((4 * gps, BLK, 2 * HD), F32)],
    )(xc, xc, xc, *([p] * gps), dtg, dtg_t, bias, bias_t, alog, alog_t, dskip, gn)


def ssd_bwd(p, xc, y, d_sin, hprev, dtg, dtg_t, params, gn, name):
    s = p.shape[0]
    nc = s // BLK
    bias, bias_t, alog, alog_t, dskip = params

    def body(xs_ref, b_ref, c_ref, z_ref, y_ref, dsin_ref, hp_ref, dtr_ref, dtrt_ref, bias_ref,
             biast_ref, alog_ref, alogt_ref, dsk_ref, gn_ref,
             dxs_ref, db_ref, dc_ref, dz_ref, ddt_ref, da_ref, dbias_ref, ddsk_ref, dgn_ref, dh_ref):
        c_idx = pl.program_id(1)

        @pl.when(c_idx == 0)
        def _():
            dh_ref[...] = jnp.zeros_like(dh_ref)
            da_ref[...] = jnp.zeros_like(da_ref)
            dbias_ref[...] = jnp.zeros_like(dbias_ref)
            ddsk_ref[...] = jnp.zeros_like(ddsk_ref)
            dgn_ref[...] = jnp.zeros_like(dgn_ref)

        dtb, dt, cs, cs_t, row, col, upper = _ssd_decays(dtr_ref, dtrt_ref, bias_ref, biast_ref, alog_ref, alogt_ref)
        first = _lane_first_half((BLK, 2 * HD))
        second = jnp.logical_not(first)
        first_row = _lane_first_half((1, 2 * HD))
        tril = row >= col
        triu = row <= col
        last_row = lax.broadcasted_iota(jnp.int32, (BLK, 1), 0) == BLK - 1
        lane8 = lax.broadcasted_iota(jnp.int32, (BLK, 8), 1)

        yv = y_ref[...]
        zv = z_ref[...].astype(F32)
        sg = _sigmoid(zv)
        yz = yv * (zv * sg)
        r = lax.rsqrt(jnp.mean(yz * yz, axis=-1, keepdims=True) + EPS)
        yzn = yz * r
        dsn = dsin_ref[...]
        dgn_ref[...] += jnp.sum(dsn * yzn, axis=0, keepdims=True)
        dsn = dsn * gn_ref[...]
        dyz = r * (dsn - yzn * jnp.mean(dsn * yzn, axis=-1, keepdims=True))
        dy = dyz * (zv * sg)
        dz_ref[...] = (dyz * yv * (sg * (1.0 + zv * (1.0 - sg)))).astype(BF16)
        xs_all = xs_ref[...].astype(F32)
        ddsk_ref[...] += jnp.sum(dy * xs_all, axis=0, keepdims=True)

        b16, c16 = b_ref[...], c_ref[...]
        cb = _dot_nt(c16, b16)
        cb_t = _dot_nt(b16, c16)
        n_pairs = GROUP_W // (2 * HD)
        tot = cs[BLK - 1:BLK, :]
        exp_cs, exp_rest, exp_tot = jnp.exp(cs), jnp.exp(tot - cs), jnp.exp(tot)

        def per_head(v, mask):
            return jnp.concatenate([jnp.where(mask, v[:, 2 * pr:2 * pr + 1], v[:, 2 * pr + 1:2 * pr + 2])
                                    for pr in range(n_pairs)], axis=1)

        def head_sums(v):
            out = jnp.zeros((BLK, 8), F32)
            for pr in range(n_pairs):
                sa, sb = _pair_sum(v[:, pr * 2 * HD:(pr + 1) * 2 * HD], first)
                out = jnp.where(lane8 == 2 * pr, sa, jnp.where(lane8 == 2 * pr + 1, sb, out))
            return out

        dt_w, e_w, f_w = per_head(dt, first), per_head(exp_cs, first), per_head(exp_rest, first)
        xt = xs_all * dt_w
        xt16 = xt.astype(BF16)
        hstate = jnp.concatenate([hp_ref[pr] for pr in range(n_pairs)], axis=1)
        h16 = hstate.astype(BF16)
        dhn = jnp.concatenate([dh_ref[pr] for pr in range(n_pairs)], axis=1)
        dhn16 = dhn.astype(BF16)
        edy16 = (e_w * dy).astype(BF16)
        y_off = e_w * _dot(c16, h16)
        dcs_all = head_sums(dy * y_off)
        dc_acc = _dot_nt(edy16, h16)
        zmat = _dot(b16, dhn16)
        t_all = head_sums(zmat * xt) * exp_rest
        hh_rows = jnp.sum(head_sums(dhn * hstate), axis=0, keepdims=True)
        dtot = jnp.sum(t_all, axis=0, keepdims=True) + hh_rows * exp_tot
        dcs_all = dcs_all - t_all + jnp.where(last_row, dtot, 0.0)
        fxt16 = (f_w * xt).astype(BF16)
        db_acc = _dot_nt(fxt16, dhn16)
        dh_new = _dot_tn(c16, edy16) + per_head(exp_tot, first_row) * dhn
        for pr in range(n_pairs):
            dh_ref[pr] = dh_new[:, pr * 2 * HD:(pr + 1) * 2 * HD]
        g_sum = jnp.zeros((BLK, BLK), F32)
        gt_sum = jnp.zeros((BLK, BLK), F32)
        d_xt_parts = []
        for pr in range(n_pairs):
            cols = slice(pr * 2 * HD, (pr + 1) * 2 * HD)
            dym2 = _stack_heads(dy[:, cols].astype(BF16), first)
            d_m2 = _dot_nt(dym2, xt16[:, cols])
            d_mt2 = _dot_nt(xt16[:, cols], dym2)
            mt2 = []
            for e, h in enumerate((2 * pr, 2 * pr + 1)):
                cs_c, cs_r = cs[:, h:h + 1], cs_t[h:h + 1, :]
                decay = jnp.exp(jnp.where(tril, cs_c - cs_r, NEG))
                decay_t = jnp.exp(jnp.where(triu, cs_r - cs_c, NEG))
                gm = d_m2[e * BLK:(e + 1) * BLK] * decay
                gmt = d_mt2[:, e * BLK:(e + 1) * BLK] * decay_t
                g_sum += gm
                gt_sum += gmt
                dcs_h = jnp.sum(gm * cb, axis=-1, keepdims=True) - jnp.sum(gmt * cb_t, axis=-1, keepdims=True)
                dcs_all = dcs_all + jnp.where(lane8 == h, dcs_h, 0.0)
                mt2.append((cb_t * decay_t).astype(BF16))
            d_xt_parts.append(_dot(jnp.concatenate(mt2, axis=1), dym2))
        d_xt = jnp.concatenate(d_xt_parts, axis=1) + f_w * zmat
        dxs_ref[...] = dy * per_head(dsk_ref[...], first_row) + d_xt * dt_w
        ddtx_all = head_sums(d_xt * xs_all)

        dc_ref[...] = dc_acc + _dot(g_sum.astype(BF16), b16)
        db_ref[...] = db_acc + _dot(gt_sum.astype(BF16), c16)
        d_a = _dot_hi(upper, dcs_all)
        a_neg = -jnp.exp(alog_ref[...])
        ddt = ddtx_all + d_a * a_neg
        da_ref[...] += jnp.sum(d_a * dt, axis=0, keepdims=True)
        ddtr = ddt * _sigmoid(dtb)
        ddt_ref[...] = ddtr
        dbias_ref[...] += jnp.sum(ddtr, axis=0, keepdims=True)

    gps = SSD_GROUPS_PER_STEP
    k_wide, k_narrow, k_lead = ("cols", GROUP_W), ("cols", BLK), ("lead", 1)
    kinds = ([k_wide, k_narrow, k_narrow, ("each", 0), k_wide, k_wide] + [k_lead] * 8 + [k_wide]
             + [k_wide, k_narrow, k_narrow, k_wide] + [k_lead] * 4 + [k_wide] + [("lead", 4)])
    wide_w, narrow_w = GROUP_W * gps, BLK * gps
    rc = lambda c: nc - 1 - c
    gparam = pl.BlockSpec((gps, 1, 8), lambda g, c: (g, 0, 0))
    gparam_t = pl.BlockSpec((gps, 8, 1), lambda g, c: (g, 0, 0))
    wide = pl.BlockSpec((BLK, wide_w), lambda g, c: (rc(c), g))
    narrow = pl.BlockSpec((BLK, narrow_w), lambda g, c: (rc(c), g))
    z_specs = [pl.BlockSpec((BLK, GROUP_W),
                            functools.partial(lambda g, c, gi: (rc(c), COL_Z // GROUP_W + gps * g + gi), gi=gi))
               for gi in range(gps)]
    return _call(
        _per_group(body, gps, kinds), name=name,
        out_shape=[_sds((s, SSD_INNER), F32), _sds((s, GROUP_W), F32), _sds((s, GROUP_W), F32),
                   _sds((s, SSD_INNER), BF16), _sds((SSD_GROUPS, s, 8), F32),
                   _sds((SSD_GROUPS, 1, 8), F32), _sds((SSD_GROUPS, 1, 8), F32),
                   _sds((SSD_GROUPS, 1, GROUP_W), F32), _sds((1, SSD_INNER), F32)],
        grid=(SSD_GROUPS // gps, nc),
        in_specs=[wide,
                  pl.BlockSpec((BLK, narrow_w), lambda g, c: (rc(c), SSD_INNER // narrow_w + g)),
                  pl.BlockSpec((BLK, narrow_w), lambda g, c: (rc(c), (SSD_INNER + SSD_GROUPS * BLK) // narrow_w + g)),
                  *z_specs,
                  wide, wide,
                  pl.BlockSpec((gps, None, 4, BLK, 2 * HD), lambda g, c: (g, rc(c), 0, 0, 0)),
                  pl.BlockSpec((gps, BLK, 8), lambda g, c: (g, rc(c), 0)),
                  pl.BlockSpec((gps, 8, BLK), lambda g, c: (g, 0, rc(c))),
                  gparam, gparam_t, gparam, gparam_t, gparam,
                  pl.BlockSpec((1, wide_w), lambda g, c: (0, g))],
        out_specs=[wide, narrow, narrow, wide,
                   pl.BlockSpec((gps, BLK, 8), lambda g, c: (g, rc(c), 0)),
                   gparam, gparam,
                   pl.BlockSpec((gps, 1, GROUP_W), lambda g, c: (g, 0, 0)),
                   pl.BlockSpec((1, wide_w), lambda g, c: (0, g))],
        scratch_shapes=[pltpu.VMEM((4 * gps, BLK, 2 * HD), F32)],
    )(xc, xc, xc, *([p] * gps), y, d_sin, hprev, dtg, dtg_t, bias, bias_t, alog, alog_t, dskip, gn)


def merge_fwd(p, a, sbr, name, tm=512):
    s = p.shape[0]
    nj = D_MODEL // GROUP_W

    def body(ga_ref, gs_ref, a_ref, s_ref, o_ref):
        o_ref[...] = (_sigmoid(ga_ref[...].astype(F32)) * a_ref[...]
                      + _sigmoid(gs_ref[...].astype(F32)) * s_ref[...]).astype(BF16)

    blk = pl.BlockSpec((tm, GROUP_W), lambda i, j: (i, j))
    return _call(body, name=name, out_shape=_sds((s, D_MODEL), BF16), grid=(s // tm, nj),
                 in_specs=[pl.BlockSpec((tm, GROUP_W), lambda i, j: (i, COL_GA // GROUP_W + j)),
                           pl.BlockSpec((tm, GROUP_W), lambda i, j: (i, COL_GS // GROUP_W + j)), blk, blk],
                 out_specs=blk)(p, p, a, sbr)


def merge_bwd(p, a, sbr, dmerged, name, tm=512):
    s = p.shape[0]
    nj = D_MODEL // GROUP_W

    def body(ga_ref, gs_ref, a_ref, s_ref, dm_ref, da_ref, ds_ref, dga_ref, dgs_ref):
        dm = dm_ref[...]
        sa = _sigmoid(ga_ref[...].astype(F32))
        ss = _sigmoid(gs_ref[...].astype(F32))
        da_ref[...] = (dm * sa).astype(BF16)
        ds_ref[...] = (dm * ss).astype(BF16)
        dga_ref[...] = (dm * a_ref[...] * sa * (1.0 - sa)).astype(BF16)
        dgs_ref[...] = (dm * s_ref[...] * ss * (1.0 - ss)).astype(BF16)

    blk = pl.BlockSpec((tm, GROUP_W), lambda i, j: (i, j))
    shp = _sds((s, D_MODEL), BF16)
    return _call(body, name=name, out_shape=[shp] * 4, grid=(s // tm, nj),
                 in_specs=[pl.BlockSpec((tm, GROUP_W), lambda i, j: (i, COL_GA // GROUP_W + j)),
                           pl.BlockSpec((tm, GROUP_W), lambda i, j: (i, COL_GS // GROUP_W + j)), blk, blk, blk],
                 out_specs=[blk] * 4)(p, p, a, sbr, dmerged)


def _group_major(v):
    return v.reshape(SSD_GROUPS, 1, 8), v.reshape(SSD_GROUPS, 8, 1)


def mixer_forward(x, w, rider=None, later_weights=None):
    s = x.shape[0]
    h = rms_fwd(x, w["mix_norm"], "mix_rms")
    p = matmul_nn(h, w["w_in_main"], "mix_proj", BF16, tm=1024, tn=512, rider=rider)
    rode = None
    if rider is not None:
        p, rode = p
        w = dict(w, **later_weights(rode))
    dt_raw = matmul_nn(h, w["w_in_dt"], "mix_proj_dt", F32, tm=1024, tn=DT_PAD)
    qn, kn = qk_norm_fwd(p, w["q_gain"], w["k_gain"], "qk_norm")
    streams, os_, lses = [], [], []
    for g, d in enumerate(ATTN_DILATIONS):
        cols = slice(g * GROUP_W, (g + 1) * GROUP_W)
        qs, ks = _to_streams(qn[:, cols], d), _to_streams(kn[:, cols], d)
        vs = _to_streams(p[:, COL_V + g * GROUP_W:COL_V + (g + 1) * GROUP_W], d)
        o, lse = attn_fwd2(qs, ks, vs, g, f"attn_fwd{g}")
        streams.append((qs, ks, vs, lse))
        os_.append(_from_streams(o, d))
        lses.append(_from_streams(lse, d))
    attn_o = attn_merge_fwd(os_, lses, "attn_merge")
    cpre, xc = conv_fwd(p, w["conv_w"], w["conv_b"], "conv_fwd")
    dtg = dt_raw[:, :SSD_HEADS].reshape(s, SSD_GROUPS, 8).transpose(1, 0, 2)
    dtg_t = dtg.transpose(0, 2, 1)
    params = (*_group_major(w["dt_bias"]), *_group_major(w["a_log"]), _group_major(w["d_skip"])[0])
    y, s_in, hprev = ssd_fwd(p, xc, dtg, dtg_t, params, w["ssd_norm"], "ssd_fwd")
    a = matmul_nn(attn_o, w["w_attn_branch"], "attn_branch", F32, tm=1024, tn=512)
    sbr = matmul_nn(s_in, w["w_ssd_branch"], "ssd_branch", F32, tm=1024, tn=512)
    merged = merge_fwd(p, a, sbr, "merge")
    x_out = matmul_nn(merged, w["w_out"], "mix_out", F32, tm=1024, tn=512, res=x)
    saved = dict(h=h, p=p, streams=streams, os=os_, lses=lses, attn_o=attn_o, cpre=cpre, xc=xc, dtg=dtg,
                 dtg_t=dtg_t, params=params, y=y, s_in=s_in, hprev=hprev, a=a, sbr=sbr, merged=merged, w=w)
    return x_out, saved, rode


def mixer_backward(dx_out, x, sv, ride_early=None, ride_late=None):
    s = x.shape[0]
    p = sv["p"]
    w = sv["w"]
    g = {}
    dmerged = matmul_nt(dx_out, w["w_out"], "d_merged", F32, tm=1024, tn=512, tk=1024)
    g["w_out"] = matmul_tn(sv["merged"], dx_out, "dw_out", tn=512, ts=1024)
    da, ds, dga, dgs = merge_bwd(p, sv["a"], sv["sbr"], dmerged, "merge_bwd")
    g["w_attn_branch"] = matmul_tn(sv["attn_o"], da, "dw_attn_branch", tn=512, ts=1024)
    g["w_ssd_branch"] = matmul_tn(sv["s_in"], ds, "dw_ssd_branch", tn=512, ts=1024)
    d_attn_o = matmul_nt(da, w["w_attn_branch"], "d_attn_o", F32, tm=1024, tn=512, tk=1024)
    d_sin = matmul_nt(ds, w["w_ssd_branch"], "d_ssd_in", F32, tm=1024, tn=512, tk=1024)
    dxs, d_b, d_c, dz, ddt, d_asum, d_bias, d_dsk, d_gn = ssd_bwd(
        p, sv["xc"], sv["y"], d_sin, sv["hprev"], sv["dtg"], sv["dtg_t"], sv["params"], w["ssd_norm"], "ssd_bwd")
    dxbc, d_convw, d_convb = conv_bwd(p, sv["cpre"], dxs, d_b, d_c, w["conv_w"], "conv_bwd")
    g["conv_w"] = d_convw[:SSD_CONV]
    g["conv_b"] = d_convb
    g["dt_bias"] = d_bias.reshape(1, SSD_HEADS)
    g["a_log"] = (d_asum * (-jnp.exp(sv["params"][2]))).reshape(1, SSD_HEADS)
    g["d_skip"] = jnp.sum(d_dsk.reshape(SSD_HEADS, HD), axis=1).reshape(1, SSD_HEADS)
    g["ssd_norm"] = d_gn
    merged_bwd = attn_merge_bwd(d_attn_o, sv["os"], sv["lses"], "attn_merge_bwd")
    dqs, dks, dvs = [], [], []
    for gi, d in enumerate(ATTN_DILATIONS):
        qs, ks, vs, lse = sv["streams"][gi]
        d_o = _to_streams(merged_bwd[gi], d)
        cterm = _to_streams(merged_bwd[3 + gi], d)
        dq, dk, dv = attn_bwd2(qs, ks, vs, d_o, cterm, lse, gi, f"attn_bwd{gi}")
        dqs.append(_from_streams(dq, d))
        dks.append(_from_streams(dk, d))
        dvs.append(_from_streams(dv, d).astype(BF16))
    dqk, d_qg, d_kg = qk_norm_bwd(p, dqs, dks, w["q_gain"], w["k_gain"], "qk_norm_bwd")
    g["q_norm"] = jnp.sum(d_qg.reshape(N_ATTN_HEADS, HD), axis=0).reshape(1, HD)
    g["k_norm"] = jnp.sum(d_kg.reshape(N_ATTN_HEADS, HD), axis=0).reshape(1, HD)
    dp = [dqk, jnp.concatenate(dvs, axis=1), dz, dxbc, dga, dgs]
    ddt_pad = jnp.pad(ddt.transpose(1, 0, 2).reshape(s, SSD_HEADS), ((0, 0), (0, DT_PAD - SSD_HEADS)))
    if ride_early is not None:
        g["w_in_main"], g["rode_early"] = matmul_tn_pieces(sv["h"], dp, "dw_in", tn=512, ts=1024,
                                                           rider=ride_early(g))
    else:
        g["w_in_main"] = matmul_tn_pieces(sv["h"], dp, "dw_in", tn=512, ts=1024)
    g["w_in_dt"] = matmul_tn(sv["h"], ddt_pad, "dw_in_dt", tn=DT_PAD, ts=1024)
    if ride_late is not None:
        dh_main, g["rode_late"] = matmul_nt_pieces(dp, w["w_in_main"], "d_mix_h", F32, tm=1024, tn=512, tk=512,
                                                   rider=ride_late(g))
    else:
        dh_main = matmul_nt_pieces(dp, w["w_in_main"], "d_mix_h", F32, tm=1024, tn=512, tk=512)
    dh_dt = matmul_nt(ddt_pad, w["w_in_dt"], "d_mix_h_dt", F32, tm=1024, tn=1024, tk=DT_PAD)
    dx, g["mix_norm"] = rms_bwd([dh_main, dh_dt], x, w["mix_norm"], dx_out, "mix_drms")
    return dx, g


ANY = pl.BlockSpec(memory_space=pl.ANY)


def _place():
    x, y, c = lax.axis_index("x"), lax.axis_index("y"), lax.axis_index("c")
    chips = [(1 - x, y), (x, 1 - y), (1 - x, 1 - y)]
    return x, y, c, 2 * x + y, chips


def _comm_call(body, *, name, out_shape, n_in, scratch_shapes, aliases=None):
    return pl.pallas_call(
        body, out_shape=out_shape, in_specs=[ANY] * n_in, out_specs=[ANY] * len(out_shape),
        scratch_shapes=scratch_shapes, input_output_aliases=aliases or {}, name=name,
        compiler_params=pltpu.CompilerParams(has_side_effects=True))


def gather_weights(shards, small):
    n = len(shards)
    halves = [a.shape[0] // 2 for a in shards]
    out_shape = [_sds((N_CHIP,) + a.shape, a.dtype) for a in shards] + [_sds((N_CHIP,) + small.shape, small.dtype)]

    def body(*refs):
        ins, outs = refs[:n + 1], refs[n + 1:2 * n + 2]
        send1, recv1, send2, recv2, local = refs[2 * n + 2:]
        x, y, c, me, chips = _place()
        sibling = (x, y, 1 - c)

        def rows(k, chip, core):
            if k == n:
                return outs[k].at[chip]
            return outs[k].at[chip, pl.ds(core * halves[k], halves[k])]

        def level1(k, t, incoming):
            chip = 2 * chips[t][0] + chips[t][1]
            src = ins[k] if k == n else ins[k].at[pl.ds(c * halves[k], halves[k])]
            return pltpu.make_async_remote_copy(
                src_ref=src, dst_ref=rows(k, chip if incoming else me, c), send_sem=send1.at[3 * k + t],
                recv_sem=recv1.at[3 * k + t], device_id=(*chips[t], c), device_id_type=MESH)

        def level2(k, t, incoming):
            chip = 2 * chips[t][0] + chips[t][1]
            core = (1 - c) if incoming else c
            return pltpu.make_async_remote_copy(
                src_ref=rows(k, chip, core), dst_ref=rows(k, chip, core), send_sem=send2.at[3 * k + t],
                recv_sem=recv2.at[3 * k + t], device_id=sibling, device_id_type=MESH)

        own = [pltpu.make_async_copy(ins[k], outs[k].at[me], local.at[k]) for k in range(n + 1)]
        for cp in own:
            cp.start()
        first = [level1(k, t, False) for k in range(n + 1) for t in range(3)]
        for cp in first:
            cp.start()
        passed = []
        for k in range(n + 1):
            for t in range(3):
                level1(k, t, True).wait_recv()
                if k < n:
                    cp = level2(k, t, False)
                    cp.start()
                    passed.append(cp)
        for k in range(n):
            for t in range(3):
                level2(k, t, True).wait_recv()
        for cp in first + passed:
            cp.wait_send()
        for cp in own:
            cp.wait()

    dma = pltpu.SemaphoreType.DMA
    return _comm_call(body, name="gather_weights", out_shape=out_shape, n_in=n + 1,
                      scratch_shapes=[dma((3 * n + 3,)), dma((3 * n + 3,)), dma((3 * n,)), dma((3 * n,)),
                                      dma((n + 1,))])(*shards, small)


def reduce_to_sibling(grads):
    n = len(grads)
    halves = [a.shape[1] // 2 for a in grads]
    shapes = [_sds((N_CHIP, h, a.shape[2]), a.dtype) for a, h in zip(grads, halves)]

    def body(*refs):
        ins, got, kept = refs[:n], refs[n:2 * n], refs[2 * n:3 * n]
        send, recv, local = refs[3 * n:]
        x, y, c, _, _ = _place()
        copies, locals_ = [], []
        for k in range(n):
            h = halves[k]
            locals_.append(pltpu.make_async_copy(ins[k].at[:, pl.ds(c * h, h)], kept[k], local.at[k]))
            copies.append(pltpu.make_async_remote_copy(
                src_ref=ins[k].at[:, pl.ds((1 - c) * h, h)], dst_ref=got[k], send_sem=send.at[k], recv_sem=recv.at[k],
                device_id=(x, y, 1 - c), device_id_type=MESH))
        for cp in locals_ + copies:
            cp.start()
        for cp in copies:
            cp.wait_recv()
        for cp in copies:
            cp.wait_send()
        for cp in locals_:
            cp.wait()

    dma = pltpu.SemaphoreType.DMA
    res = _comm_call(body, name="reduce_to_sibling", out_shape=shapes + shapes, n_in=n,
                     scratch_shapes=[dma((n,)), dma((n,)), dma((n,))])(*grads)
    return res[:n], res[n:]


def reduce_to_owner(sums):
    n = len(sums)
    shapes = [_sds(a.shape, a.dtype) for a in sums]

    def body(*refs):
        ins, outs = refs[:n], refs[n:2 * n]
        send, recv, local = refs[2 * n:]
        x, y, c, me, chips = _place()
        copies, locals_ = [], []
        for k in range(n):
            locals_.append(pltpu.make_async_copy(ins[k].at[me], outs[k].at[3], local.at[k]))
            for t in range(3):
                chip = 2 * chips[t][0] + chips[t][1]
                copies.append(pltpu.make_async_remote_copy(
                    src_ref=ins[k].at[chip], dst_ref=outs[k].at[t], send_sem=send.at[3 * k + t],
                    recv_sem=recv.at[3 * k + t], device_id=(*chips[t], c), device_id_type=MESH))
        for cp in locals_ + copies:
            cp.start()
        for cp in copies:
            cp.wait_recv()
        for cp in copies:
            cp.wait_send()
        for cp in locals_:
            cp.wait()

    dma = pltpu.SemaphoreType.DMA
    return _comm_call(body, name="reduce_to_owner", out_shape=shapes, n_in=n,
                      scratch_shapes=[dma((3 * n,)), dma((3 * n,)), dma((n,))])(*sums)


def share_with_sibling(halves_):
    n = len(halves_)
    shapes = [_sds((2 * a.shape[0], a.shape[1]), a.dtype) for a in halves_]

    def body(*refs):
        ins, outs = refs[:n], refs[n:2 * n]
        send, recv, local = refs[2 * n:]
        x, y, c, _, _ = _place()
        copies, locals_ = [], []
        for k in range(n):
            h = ins[k].shape[0]
            mine = outs[k].at[pl.ds(c * h, h)]
            locals_.append(pltpu.make_async_copy(ins[k], mine, local.at[k]))
            copies.append(pltpu.make_async_remote_copy(
                src_ref=ins[k], dst_ref=mine, send_sem=send.at[k], recv_sem=recv.at[k],
                device_id=(x, y, 1 - c), device_id_type=MESH))
        for cp in locals_ + copies:
            cp.start()
        for cp in copies:
            cp.wait_recv()
        for cp in copies:
            cp.wait_send()
        for cp in locals_:
            cp.wait()

    dma = pltpu.SemaphoreType.DMA
    return _comm_call(body, name="share_with_sibling", out_shape=shapes, n_in=n,
                      scratch_shapes=[dma((n,)), dma((n,)), dma((n,))])(*halves_)


def _cores():
    c = lax.axis_index("c")
    return jnp.stack([c, 1 - c]).astype(jnp.int32)


def _staged_call(body, *, name, grid, in_specs, out_specs, out_shape, scratch_shapes):
    return pl.pallas_call(
        body, out_shape=out_shape, name=name,
        grid_spec=pltpu.PrefetchScalarGridSpec(num_scalar_prefetch=1, grid=grid, in_specs=in_specs,
                                               out_specs=out_specs, scratch_shapes=scratch_shapes),
        compiler_params=pltpu.CompilerParams(dimension_semantics=("arbitrary",) * len(grid),
                                             vmem_limit_bytes=V7X_VMEM_LIMIT, has_side_effects=True))


def gather_rider(shards, tiles):
    dma = pltpu.SemaphoreType.DMA
    n = len(shards)
    geo = [(a.shape[0] // 2, tm, (a.shape[0] // 2) // tm) for a, tm in zip(shards, tiles)]
    scratch = []
    for a, (h, tm, nk) in zip(shards, geo):
        scratch += [pltpu.VMEM((N_CHIP,) + a.shape, a.dtype), dma((3, nk)), dma((3, nk)), dma((3, nk)), dma((3, nk)),
                    dma((nk + 2,))]

    def copies(j, in_ref, scr):
        buf, send1, recv1, send2, recv2, local = scr[6 * j:6 * j + 6]
        h, tm, nk = geo[j]
        x, y, c, me, chips = _place()
        chip_of = [2 * chips[t][0] + chips[t][1] for t in range(3)]

        def rows(chip, core, k):
            return buf.at[chip, pl.ds(core * h + k * tm, tm)]

        def mine(k):
            if k == nk:
                return pltpu.make_async_copy(in_ref.at[pl.ds((1 - c) * h, h)], buf.at[me, pl.ds((1 - c) * h, h)],
                                             local.at[nk])
            return pltpu.make_async_copy(in_ref.at[pl.ds(c * h + k * tm, tm)], rows(me, c, k), local.at[k])

        def level1(t, k, incoming):
            place = rows(chip_of[t] if incoming else me, c, k)
            return pltpu.make_async_remote_copy(src_ref=place, dst_ref=place, send_sem=send1.at[t, k],
                                                recv_sem=recv1.at[t, k], device_id=(*chips[t], c), device_id_type=MESH)

        def level2(t, k, incoming):
            place = rows(chip_of[t], (1 - c) if incoming else c, k)
            return pltpu.make_async_remote_copy(src_ref=place, dst_ref=place, send_sem=send2.at[t, k],
                                                recv_sem=recv2.at[t, k], device_id=(x, y, 1 - c),
                                                device_id_type=MESH)

        return buf, local, nk, mine, level1, level2

    def start(ins, outs, scr):
        for j in range(n):
            _, _, nk, mine, _, _ = copies(j, ins[j], scr)
            for k in range(nk + 1):
                mine(k).start()
        for j in range(n):
            _, _, nk, mine, level1, _ = copies(j, ins[j], scr)
            for k in range(nk):
                mine(k).wait()
                for t in range(3):
                    level1(t, k, False).start()

    def finish(ins, outs, scr):
        for j in range(n):
            _, _, nk, _, level1, level2 = copies(j, ins[j], scr)
            for k in range(nk):
                for t in range(3):
                    level1(t, k, True).wait_recv()
                    level2(t, k, False).start()
        for j in range(n):
            buf, local, nk, mine, level1, level2 = copies(j, ins[j], scr)
            for k in range(nk):
                for t in range(3):
                    level2(t, k, True).wait_recv()
            for k in range(nk):
                for t in range(3):
                    level1(t, k, False).wait_send()
                    level2(t, k, False).wait_send()
            mine(nk).wait()
            pltpu.make_async_copy(buf, outs[j], local.at[nk + 1]).start()
        for j in range(n):
            buf, local, nk, _, _, _ = copies(j, ins[j], scr)
            pltpu.make_async_copy(buf, outs[j], local.at[nk + 1]).wait()

    return Rider(list(shards), [_sds((N_CHIP,) + a.shape, a.dtype) for a in shards], scratch, start, finish)


def run_alone(rider, name):
    return _call(lambda: None, name=name, out_shape=[], in_specs=[], out_specs=[], grid=(1,), rider=rider)()[1]


def sibling_sum(g, tm, name):
    _, r, cdim = g.shape
    h = r // 2
    ni = h // tm
    dma = pltpu.SemaphoreType.DMA

    def body(cores_ref, keep_ref, give_ref, out_ref, slot, send, recv):
        par = (pl.program_id(0) * ni + pl.program_id(1)) % 2
        x, y, c, _, _ = _place()
        cp = pltpu.make_async_remote_copy(src_ref=give_ref, dst_ref=slot.at[par], send_sem=send.at[par],
                                          recv_sem=recv.at[par], device_id=(x, y, 1 - c), device_id_type=MESH)
        cp.start()
        cp.wait_recv()
        out_ref[...] = (keep_ref[...].astype(F32) + slot[par].astype(F32)).astype(out_ref.dtype)
        cp.wait_send()

    flat = g.reshape(N_CHIP * r, cdim)
    return _staged_call(
        body, name=name, grid=(N_CHIP, ni),
        in_specs=[pl.BlockSpec((tm, cdim), lambda j, i, cores: ((2 * j + cores[0]) * ni + i, 0)),
                  pl.BlockSpec((tm, cdim), lambda j, i, cores: ((2 * j + cores[1]) * ni + i, 0))],
        out_specs=pl.BlockSpec((None, tm, cdim), lambda j, i, cores: (j, i, 0)),
        out_shape=_sds((N_CHIP, h, cdim), g.dtype),
        scratch_shapes=[pltpu.VMEM((2, tm, cdim), g.dtype), dma((2,)), dma((2,))],
    )(_cores(), flat, flat)


def owner_sum_rider(sums, tiles):
    dma = pltpu.SemaphoreType.DMA
    n = len(sums)
    geo = [(a.shape[1], tm, a.shape[1] // tm) for a, tm in zip(sums, tiles)]
    scratch = []
    for a, (h, tm, nk) in zip(sums, geo):
        cdim = a.shape[2]
        scratch += [pltpu.VMEM(a.shape, a.dtype), pltpu.VMEM((3, h, cdim), a.dtype), pltpu.VMEM((2, h, cdim), F32),
                    dma((3, nk)), dma((3, nk)), dma((nk,)), dma((nk,)), dma((2,))]

    def copies(j, scr):
        part, got, res, send, recv, send2, recv2, local = scr[8 * j:8 * j + 8]
        h, tm, nk = geo[j]
        x, y, c, me, chips = _place()

        def to_owner(t, k):
            chip = 2 * chips[t][0] + chips[t][1]
            return pltpu.make_async_remote_copy(
                src_ref=part.at[chip, pl.ds(k * tm, tm)], dst_ref=got.at[t, pl.ds(k * tm, tm)],
                send_sem=send.at[t, k], recv_sem=recv.at[t, k], device_id=(*chips[t], c), device_id_type=MESH)

        def to_sibling(k):
            place = res.at[c, pl.ds(k * tm, tm)]
            return pltpu.make_async_remote_copy(src_ref=place, dst_ref=place, send_sem=send2.at[k],
                                                recv_sem=recv2.at[k], device_id=(x, y, 1 - c), device_id_type=MESH)

        return part, got, res, local, to_owner, to_sibling, (tm, nk, c, me)

    def start(ins, outs, scr):
        for j in range(n):
            part, _, _, local, _, _, _ = copies(j, scr)
            pltpu.make_async_copy(ins[j], part, local.at[0]).start()
        for j in range(n):
            part, _, _, local, to_owner, _, (tm, nk, c, me) = copies(j, scr)
            pltpu.make_async_copy(ins[j], part, local.at[0]).wait()
            for k in range(nk):
                for t in range(3):
                    to_owner(t, k).start()

    def finish(ins, outs, scr):
        for j in range(n):
            part, got, res, _, to_owner, to_sibling, (tm, nk, c, me) = copies(j, scr)
            for k in range(nk):
                rows = pl.ds(k * tm, tm)
                for t in range(3):
                    to_owner(t, k).wait_recv()
                acc = part[me, rows, :].astype(F32)
                for t in range(3):
                    acc = acc + got[t, rows, :].astype(F32)
                res[c, rows, :] = acc
                to_sibling(k).start()
        for j in range(n):
            _, _, res, local, to_owner, to_sibling, (tm, nk, c, me) = copies(j, scr)
            for k in range(nk):
                to_sibling(k).wait_recv()
            for k in range(nk):
                to_sibling(k).wait_send()
                for t in range(3):
                    to_owner(t, k).wait_send()
            pltpu.make_async_copy(res, outs[j], local.at[1]).start()
        for j in range(n):
            _, _, res, local, _, _, _ = copies(j, scr)
            pltpu.make_async_copy(res, outs[j], local.at[1]).wait()

    return Rider(list(sums), [_sds((2, a.shape[1], a.shape[2]), F32) for a in sums], scratch, start, finish)


def gather_conv_w(w):
    def body(in_ref, out_ref, send, recv):
        x, y, c, me, chips = _place()
        out_ref[me] = in_ref[...]
        copies = []
        for t in range(3):
            copies.append(pltpu.make_async_remote_copy(
                src_ref=out_ref.at[me], dst_ref=out_ref.at[me], send_sem=send.at[t], recv_sem=recv.at[t],
                device_id=(*chips[t], c), device_id_type=MESH))
        for cp in copies:
            cp.start()
        for cp in copies:
            cp.wait_recv()
        for cp in copies:
            cp.wait_send()

    dma = pltpu.SemaphoreType.DMA
    vmem = pl.BlockSpec(memory_space=pltpu.VMEM)
    return pl.pallas_call(
        body, out_shape=_sds((N_CHIP,) + w.shape, w.dtype), in_specs=[vmem], out_specs=vmem, name="gather_conv_w",
        scratch_shapes=[dma((3,)), dma((3,))],
        compiler_params=pltpu.CompilerParams(has_side_effects=True))(w)


N_DEV = 8
SMALL_ROWS = 32
SMALL_LANES = 1024


def all_reduce_small(arrays):
    n_arr = len(arrays)
    places = []
    for k, a in enumerate(arrays):
        for ri in range(a.shape[0]):
            for c0 in range(0, a.shape[1], SMALL_LANES):
                places.append((k, ri, c0, min(SMALL_LANES, a.shape[1] - c0), len(places)))
    assert len(places) <= SMALL_ROWS

    def body(*refs):
        ins, outs = refs[:n_arr], refs[n_arr:2 * n_arr]
        buf, send, recv = refs[2 * n_arr:]
        x, y, c, _, _ = _place()
        me = 4 * x + 2 * y + c
        buf[me] = jnp.zeros((SMALL_ROWS, SMALL_LANES), F32)
        for k, ri, c0, width, row in places:
            buf[me, row:row + 1, 0:width] = ins[k][ri:ri + 1, c0:c0 + width]
        copies = []
        for r in range(1, N_DEV):
            px = (1 - x) if r & 4 else x
            py = (1 - y) if r & 2 else y
            pc = (1 - c) if r & 1 else c
            copies.append(pltpu.make_async_remote_copy(
                src_ref=buf.at[me], dst_ref=buf.at[me], send_sem=send.at[r - 1], recv_sem=recv.at[r - 1],
                device_id=(px, py, pc), device_id_type=MESH))
        for cp in copies:
            cp.start()
        for cp in copies:
            cp.wait_recv()
        for cp in copies:
            cp.wait_send()
        acc = buf[0]
        for j in range(1, N_DEV):
            acc = acc + buf[j]
        for k, ri, c0, width, row in places:
            outs[k][ri:ri + 1, c0:c0 + width] = acc[row:row + 1, 0:width]

    dma = pltpu.SemaphoreType.DMA
    vmem = pl.BlockSpec(memory_space=pltpu.VMEM)
    return pl.pallas_call(
        body, out_shape=[_sds(a.shape, F32) for a in arrays], in_specs=[vmem] * n_arr, out_specs=[vmem] * n_arr,
        name="all_reduce_small",
        scratch_shapes=[pltpu.VMEM((N_DEV, SMALL_ROWS, SMALL_LANES), F32), dma((N_DEV - 1,)), dma((N_DEV - 1,))],
        compiler_params=pltpu.CompilerParams(has_side_effects=True))(*arrays)


def _row_tile(rows, limit, multiple):
    return max(t for t in range(multiple, min(rows, limit) + 1, multiple) if rows % t == 0)


def add_pair(a, b, name):
    _, h, c = a.shape

    def body(a_ref, b_ref, o_ref):
        o_ref[...] = (a_ref[...].astype(F32) + b_ref[...].astype(F32)).astype(o_ref.dtype)

    blk = pl.BlockSpec((None, h, c), lambda j: (j, 0, 0))
    return _call(body, name=name, out_shape=_sds(a.shape, a.dtype), grid=(N_CHIP,), in_specs=[blk, blk],
                 out_specs=blk)(a, b)


def sum_slots(buf, name):
    _, h, c = buf.shape
    tm = _row_tile(h, 256, 16)

    def body(b_ref, o_ref):
        acc = b_ref[3].astype(F32)
        for t in range(3):
            acc = acc + b_ref[t].astype(F32)
        o_ref[...] = acc

    return _call(body, name=name, out_shape=_sds((h, c), F32), grid=(h // tm,),
                 in_specs=[pl.BlockSpec((N_CHIP, tm, c), lambda i: (0, i, 0))],
                 out_specs=pl.BlockSpec((tm, c), lambda i: (i, 0)))(buf)


def _adamw_math(w, g, m, v):
    c1 = 1.0 - ADAM_B1 ** ADAM_STEP
    c2 = 1.0 - ADAM_B2 ** ADAM_STEP
    m2 = ADAM_B1 * m + (1.0 - ADAM_B1) * g
    v2 = ADAM_B2 * v + (1.0 - ADAM_B2) * (g * g)
    return -ADAM_LR * ((m2 / c1) / (jnp.sqrt(v2 / c2) + ADAM_EPS) + ADAM_WD * w), m2, v2


def adamw(w, g, row_off, m, v, name):
    _, r, c = w.shape
    tm = r if r < 8 else _row_tile(math.gcd(r, row_off) if row_off else r, 128, 8)

    def body(w_ref, g_ref, m_ref, v_ref, go_ref, d_ref, m2_ref, v2_ref):
        gv = g_ref[...]
        go_ref[...] = gv
        d_ref[...], m2_ref[...], v2_ref[...] = _adamw_math(w_ref[...], gv, m_ref[...], v_ref[...])

    blk = pl.BlockSpec((None, tm, c), lambda i: (0, i, 0))
    shp = _sds((1, r, c), F32)
    return _call(body, name=name, out_shape=[shp] * 4, grid=(r // tm,),
                 in_specs=[blk, pl.BlockSpec((tm, c), lambda i: (row_off // tm + i, 0)), blk, blk],
                 out_specs=[blk] * 4)(w, g, m, v)


def adamw_small(ws, gs, ms, vs):
    n = len(ws)

    def body(*refs):
        ins, outs = refs[:4 * n], refs[4 * n:]
        for k in range(n):
            w_ref, g_ref, m_ref, v_ref = (ins[j * n + k] for j in range(4))
            outs[k][...], outs[n + k][...], outs[2 * n + k][...] = _adamw_math(w_ref[...], g_ref[...], m_ref[...],
                                                                               v_ref[...])

    vmem = pl.BlockSpec(memory_space=pltpu.VMEM)
    shapes = [_sds(w.shape, F32) for w in ws] * 3
    res = pl.pallas_call(body, out_shape=shapes, in_specs=[vmem] * (4 * n), out_specs=[vmem] * (3 * n),
                         name="adamw_small")(*ws, *gs, *ms, *vs)
    return res[:n], res[n:2 * n], res[2 * n:]


BIG = ("ffn1_w_gate", "ffn1_w_up", "ffn1_w_down", "w_in", "w_attn_branch", "w_ssd_branch", "w_out",
       "ffn2_w_gate", "ffn2_w_up", "ffn2_w_down")
SMALL = ("ffn1_norm", "mix_norm", "q_norm", "k_norm", "conv_b", "dt_bias", "a_log", "d_skip", "ssd_norm", "ffn2_norm")
WEIGHTS = ("ffn1_norm", "ffn1_w_gate", "ffn1_w_up", "ffn1_w_down", "mix_norm", "w_in", "q_norm", "k_norm", "conv_w",
           "conv_b", "dt_bias", "a_log", "d_skip", "ssd_norm", "w_attn_branch", "w_ssd_branch", "w_out", "ffn2_norm",
           "ffn2_w_gate", "ffn2_w_up", "ffn2_w_down")
CONV_SHARD = SSD_CONV_DIM // N_CHIP
CLASSES = {
    "ffn1_in": (("ffn1_w_gate", 1024), ("ffn1_w_up", 1024)),
    "ffn1_out": (("ffn1_w_down", 704),),
    "mix_in": (("w_in", 1024),),
    "mix_attn": (("w_attn_branch", 512),),
    "late_out": (("ffn2_w_down", 704), ("w_ssd_branch", 512), ("w_out", 256)),
    "ffn2_in": (("ffn2_w_gate", 1024), ("ffn2_w_up", 1024)),
}
CLASS_TILE = {"ffn1_in": 256, "ffn1_out": 176, "mix_in": 128, "mix_attn": 256, "late_out": 368, "ffn2_in": 256,
              "mix_in_top": 128, "mix_in_bottom": 128}
SIBLING_TILE = {"ffn1_in": 1024, "ffn1_out": 352, "mix_attn": 256, "late_out": 736, "ffn2_in": 1024,
                "mix_in_top": 256, "mix_in_bottom": 256}


def _pack_small(vals, conv_part, loss_part=None):
    flat = [vals[k].reshape(-1) for k in SMALL]
    flat.append(jnp.zeros((SSD_CONV * SSD_CONV_DIM,), F32) if conv_part is None else conv_part.reshape(-1))
    flat.append(jnp.zeros((1,), F32) if loss_part is None else loss_part.reshape(1))
    flat = jnp.concatenate(flat)
    return jnp.pad(flat, (0, SMALL_ROWS * D_MODEL - flat.shape[0])).reshape(SMALL_ROWS, D_MODEL)


def _unpack_small(pack, like):
    flat = pack.reshape(-1)
    out, off = {}, 0
    for k in SMALL:
        n = like[k].size
        out[k] = flat[off:off + n].reshape(like[k].shape)
        off += n
    conv = flat[off:off + SSD_CONV * SSD_CONV_DIM].reshape(SSD_CONV, SSD_CONV_DIM)
    return out, conv, flat[off + SSD_CONV * SSD_CONV_DIM]


def _chip_major_cols(a):
    r = a.shape[0]
    return a.reshape(r, N_CHIP, -1).transpose(1, 0, 2)


def _from_chip_major_cols(a):
    return a.transpose(1, 0, 2).reshape(a.shape[1], -1)


def kernel(x, ffn1_norm, ffn1_w_gate, ffn1_w_up, ffn1_w_down, mix_norm, w_in, q_norm, k_norm, conv_w, conv_b, dt_bias, a_log, d_skip, ssd_norm, w_attn_branch, w_ssd_branch, w_out, ffn2_norm, ffn2_w_gate, ffn2_w_up, ffn2_w_down, loss_target, m_ffn1_norm, m_ffn1_w_gate, m_ffn1_w_up, m_ffn1_w_down, m_mix_norm, m_w_in, m_q_norm, m_k_norm, m_conv_w, m_conv_b, m_dt_bias, m_a_log, m_d_skip, m_ssd_norm, m_w_attn_branch, m_w_ssd_branch, m_w_out, m_ffn2_norm, m_ffn2_w_gate, m_ffn2_w_up, m_ffn2_w_down, v_ffn1_norm, v_ffn1_w_gate, v_ffn1_w_up, v_ffn1_w_down, v_mix_norm, v_w_in, v_q_norm, v_k_norm, v_conv_w, v_conv_b, v_dt_bias, v_a_log, v_d_skip, v_ssd_norm, v_w_attn_branch, v_w_ssd_branch, v_w_out, v_ffn2_norm, v_ffn2_w_gate, v_ffn2_w_up, v_ffn2_w_down):
    env = dict(locals())
    wts = {k: env[k] for k in WEIGHTS}
    moms = {k: env["m_" + k] for k in WEIGHTS}
    vars_ = {k: env["v_" + k] for k in WEIGHTS}
    x0 = x[0]
    target = loss_target[0]

    def gather(classes):
        shards = [jnp.concatenate([wts[k][0] for k, _ in CLASSES[c]], axis=0).astype(BF16) for c in classes]
        return gather_rider(shards, [CLASS_TILE[c] for c in classes])

    def reducer(classes, parts):
        sums = [sibling_sum(p, SIBLING_TILE[c], f"sibling_sum_{c}") for c, p in zip(classes, parts)]
        return owner_sum_rider(sums, [CLASS_TILE[c] for c in classes])

    x1, saved1, (w_ffn1_in,), (w_ffn1_out, w_mix_in, w_mix_attn) = ffn_forward(
        x0, ffn1_norm, lambda rode: rode[0], lambda rode: rode[0], "ffn1", rms_rider=gather(["ffn1_in"]),
        up_rider=gather(["ffn1_out", "mix_in", "mix_attn"]))
    dt0, dt1 = IN_DT0 - 3 * IN_SHARD, IN_DT1 - 3 * IN_SHARD
    mixer_w = dict(
        mix_norm=mix_norm,
        w_in_main=jnp.concatenate([w_mix_in[0], w_mix_in[1], w_mix_in[2], w_mix_in[3][:, :dt0], w_mix_in[3][:, dt1:]],
                                  axis=1),
        w_in_dt=jnp.pad(w_mix_in[3][:, dt0:dt1], ((0, 0), (0, DT_PAD - SSD_HEADS))),
        q_gain=jnp.tile(q_norm, (1, 2)), k_gain=jnp.tile(k_norm, (1, 2)),
        conv_w=_from_chip_major_cols(gather_conv_w(conv_w[0])), conv_b=conv_b, dt_bias=dt_bias, a_log=a_log,
        d_skip=d_skip, ssd_norm=ssd_norm, w_attn_branch=_from_chip_major_cols(w_mix_attn))

    def later_weights(rode):
        late = rode[0]
        return dict(w_ssd_branch=late[:, 704:1216].reshape(SSD_INNER, D_MODEL),
                    w_out=late[:, 1216:1472].reshape(D_MODEL, D_MODEL))

    x2, saved_mix, (w_late_out, w_ffn2_in) = mixer_forward(x1, mixer_w, gather(["late_out", "ffn2_in"]), later_weights)
    x3, saved2, _, _ = ffn_forward(x2, ffn2_norm, lambda rode: w_ffn2_in, lambda rode: w_late_out, "ffn2")
    dx3, sq = loss_grad(x3, target, "loss")

    grads = {}
    dx2, grads["ffn2_norm"], d_ffn2_in, d_ffn2_down = ffn_backward(dx3, x2, ffn2_norm, w_ffn2_in, w_late_out, saved2,
                                                                   "ffn2")

    def ride_early(g):
        late = jnp.concatenate([d_ffn2_down, g["w_ssd_branch"].reshape(N_CHIP, -1, D_MODEL),
                                g["w_out"].reshape(N_CHIP, -1, D_MODEL)], axis=1)
        return reducer(["ffn2_in", "late_out"], [d_ffn2_in, late])

    g_in_rows = {}

    def ride_late(g):
        main = g["w_in_main"]
        last = jnp.concatenate([main[:, 3 * IN_SHARD:IN_DT0], g["w_in_dt"][:, :SSD_HEADS], main[:, IN_DT0:]], axis=1)
        for part, rows in (("top", slice(0, D_MODEL // 2)), ("bottom", slice(D_MODEL // 2, D_MODEL))):
            g_in_rows[part] = jnp.stack([main[rows, j * IN_SHARD:(j + 1) * IN_SHARD] for j in range(3)]
                                        + [last[rows]])
        return reducer(["mix_in_top", "mix_attn"], [g_in_rows["top"], _chip_major_cols(g["w_attn_branch"])])

    dx1, gmix = mixer_backward(dx2, x1, saved_mix, ride_early, ride_late)
    dx0, grads["ffn1_norm"], rode_in, rode_out = ffn_backward(
        dx1, x0, ffn1_norm, w_ffn1_in, w_ffn1_out, saved1, "ffn1",
        ride_down=lambda d: reducer(["mix_in_bottom", "ffn1_out"], [g_in_rows["bottom"], d]),
        ride_in=lambda d: reducer(["ffn1_in"], [d]))
    for k in ("mix_norm", "q_norm", "k_norm", "conv_b", "dt_bias", "a_log", "d_skip", "ssd_norm"):
        grads[k] = gmix[k]
    reduced = dict(zip(("ffn2_in", "late_out", "mix_in_top", "mix_attn", "ffn1_in", "mix_in_bottom", "ffn1_out"),
                       (*gmix["rode_early"], *gmix["rode_late"], rode_in[0], *rode_out)))
    reduced = {c: r.reshape(-1, r.shape[2]) for c, r in reduced.items()}
    reduced["mix_in"] = jnp.concatenate([reduced.pop("mix_in_top"), reduced.pop("mix_in_bottom")], axis=0)
    summed = all_reduce_small([grads[k] for k in SMALL]
                              + [gmix["conv_w"], (0.5 * jnp.sum(sq) / D_MODEL).reshape(1, 1)])
    g_small = dict(zip(SMALL, summed))
    loss = summed[-1].reshape(())
    chip = 2 * lax.axis_index("x") + lax.axis_index("y")
    g_conv = lax.dynamic_slice_in_dim(summed[-2], chip * CONV_SHARD, CONV_SHARD, axis=1)

    g_final, delta, new_m, new_v = dict(g_small), {}, {}, {}

    def update(k, g_arr, row_off):
        w, m, v = wts[k], moms[k], vars_[k]
        rows, cols = w.shape[1:]
        if cols % 128:
            res = adamw(jnp.swapaxes(w, 1, 2), g_arr[row_off:row_off + rows].T, 0, jnp.swapaxes(m, 1, 2),
                        jnp.swapaxes(v, 1, 2), f"adamw_{k}")
            res = [jnp.swapaxes(r, 1, 2) for r in res]
        else:
            res = adamw(w, g_arr, row_off, m, v, f"adamw_{k}")
        g_final[k], delta[k], new_m[k], new_v[k] = res

    for cls, members in CLASSES.items():
        off = 0
        for k, rows in members:
            update(k, reduced[cls], off)
            off += rows
    update("conv_w", g_conv, 0)
    small = adamw_small(*([d[k] for k in SMALL] for d in (wts, g_small, moms, vars_)))
    for res, vals in zip((delta, new_m, new_v), small):
        res.update(zip(SMALL, vals))

    return (loss, dx0[None], *[g_final[k] for k in WEIGHTS], *[delta[k] for k in WEIGHTS],
            *[new_m[k] for k in WEIGHTS], *[new_v[k] for k in WEIGHTS])
```

```python
import collections
import functools
import math

import jax
import jax.numpy as jnp
from jax import lax
from jax.experimental import pallas as pl
from jax.experimental.pallas import tpu as pltpu

F32 = jnp.float32
BF16 = jnp.bfloat16
MESH = pl.DeviceIdType.MESH

EPS = 1e-6
D_MODEL = 1024
D_FF = 2816
N_CHIP = 4
FF_SHARD = D_FF // N_CHIP
HD = 64
BLK = 128
ATTN_DILATIONS = (1, 4, 16)
HEADS_PER_PATTERN = 8
N_ATTN_HEADS = 24
ALIBI_MAX_EXP = 8.0
ATTN_QKV = 1536
GROUP_W = 512
SSD_INNER = 2048
SSD_HEADS = 32
SSD_GROUPS = 4
SSD_CONV = 4
SSD_CONV_DIM = 3072
IN_COLS = 11808
IN_DT0, IN_DT1 = 9728, 9760
IN_SHARD = IN_COLS // 4
COL_K, COL_V, COL_Z, COL_XBC, COL_GA, COL_GS, P_COLS = 1536, 3072, 4608, 6656, 9728, 10752, 11776
DT_PAD = 128

ADAM_LR, ADAM_B1, ADAM_B2, ADAM_EPS, ADAM_WD, ADAM_STEP = 0.001, 0.9, 0.999, 1e-08, 0.01, 10

V7X_VMEM_LIMIT = 56 * 1024 * 1024
NEG = -1e30


Rider = collections.namedtuple("Rider", "arrays out_shape scratch start finish")
Rider.__doc__ = """An exchange between devices that rides in a compute kernel: its copies are started in the host's
first grid step and waited for in its last, so they travel while the host computes.  arrays / out_shape: extra HBM
operands and results; scratch: extra scratch; start, finish: f(in_refs, out_refs, scratch_refs)."""


def _call(body, *, name, out_shape, in_specs, out_specs, grid=(), scratch_shapes=(), aliases=None, rider=None):
    params = dict(dimension_semantics=("arbitrary",) * len(grid), vmem_limit_bytes=V7X_VMEM_LIMIT)
    if rider is None:
        return pl.pallas_call(
            body, out_shape=out_shape, grid=grid, in_specs=in_specs, out_specs=out_specs,
            scratch_shapes=scratch_shapes, input_output_aliases=aliases or {}, name=name,
            compiler_params=pltpu.CompilerParams(**params))
    single = not isinstance(out_shape, (list, tuple))
    main_out = [out_shape] if single else list(out_shape)
    main_specs = [out_specs] if single else list(out_specs)
    n_in, n_out, n_scr = len(in_specs), len(main_out), len(scratch_shapes)
    r_in, r_out = len(rider.arrays), len(rider.out_shape)

    def wrapped(*refs):
        ins, refs = refs[:n_in], refs[n_in:]
        r_ins, refs = refs[:r_in], refs[r_in:]
        outs, refs = refs[:n_out], refs[n_out:]
        r_outs, refs = refs[:r_out], refs[r_out:]
        scr, r_scr = refs[:n_scr], refs[n_scr:]
        first = last = None
        for axis, size in enumerate(grid):
            at_start, at_end = pl.program_id(axis) == 0, pl.program_id(axis) == size - 1
            first = at_start if first is None else jnp.logical_and(first, at_start)
            last = at_end if last is None else jnp.logical_and(last, at_end)

        @pl.when(first)
        def _():
            rider.start(r_ins, r_outs, r_scr)

        body(*ins, *outs, *scr)

        @pl.when(last)
        def _():
            rider.finish(r_ins, r_outs, r_scr)

    hbm = pl.BlockSpec(memory_space=pl.ANY)
    call = pl.pallas_call(
        wrapped, out_shape=main_out + list(rider.out_shape), grid=grid, in_specs=list(in_specs) + [hbm] * r_in,
        out_specs=main_specs + [hbm] * r_out, scratch_shapes=list(scratch_shapes) + list(rider.scratch), name=name,
        compiler_params=pltpu.CompilerParams(has_side_effects=True, **params))

    def run(*args):
        res = call(*args, *rider.arrays)
        main = res[:n_out]
        return (main[0] if single else main), res[n_out:]

    return run


def _sds(shape, dtype):
    return jax.ShapeDtypeStruct(tuple(shape), dtype)


def _dot(a, b):
    return jnp.dot(a, b, preferred_element_type=F32)


def _dot_nt(a, b):
    return lax.dot_general(a, b, (((1,), (1,)), ((), ())), preferred_element_type=F32)


def _dot_tn(a, b):
    return lax.dot_general(a, b, (((0,), (0,)), ((), ())), preferred_element_type=F32)


def _dot_hi(a, b):
    return jnp.dot(a, b, preferred_element_type=F32, precision=lax.Precision.HIGHEST)


def _sigmoid(x):
    return 1.0 / (1.0 + jnp.exp(-x))


def _lane_first_half(shape):
    return lax.broadcasted_iota(jnp.int32, shape, len(shape) - 1) < HD


def _pair_sum(x, first):
    s_all = jnp.sum(x, axis=-1, keepdims=True)
    s_a = jnp.sum(jnp.where(first, x, 0.0), axis=-1, keepdims=True)
    return s_a, s_all - s_a


def _rowwise(name, fn, rows, consts, outs, accs=(), tm=512, rider=None):
    n_rows = None
    in_arrays, in_specs = [], []
    for r in rows:
        if isinstance(r, tuple):
            arr, w, cb = r
            spec = pl.BlockSpec((tm, w), functools.partial(lambda i, cb: (i, cb), cb=cb))
        else:
            arr = r
            spec = pl.BlockSpec((tm, arr.shape[1]), lambda i: (i, 0))
        n_rows = arr.shape[0]
        in_arrays.append(arr)
        in_specs.append(spec)
    for c in consts:
        in_arrays.append(c)
        in_specs.append(pl.BlockSpec(c.shape, functools.partial(lambda i, n: (0,) * n, n=c.ndim)))
    out_shape = [_sds(s, d) for s, d in outs] + [_sds(s, d) for s, d in accs]
    out_specs = [pl.BlockSpec((tm, s[1]), lambda i: (i, 0)) for s, _ in outs]
    out_specs += [pl.BlockSpec(s, functools.partial(lambda i, n: (0,) * n, n=len(s))) for s, _ in accs]

    def body(*refs):
        fn(pl.program_id(0), *refs)

    res = _call(body, name=name, out_shape=out_shape, grid=(n_rows // tm,), in_specs=in_specs,
                out_specs=out_specs, rider=rider)(*in_arrays)
    return res


def rms_fwd(x, gain, name, rider=None):
    def fn(i, x_ref, g_ref, h_ref):
        xv = x_ref[...]
        r = lax.rsqrt(jnp.mean(xv * xv, axis=-1, keepdims=True) + EPS)
        h_ref[...] = (xv * r * g_ref[...]).astype(h_ref.dtype)

    res = _rowwise(name, fn, [x], [gain], [(x.shape, BF16)], rider=rider)
    return res[0] if rider is None else (res[0][0], res[1])


def rms_bwd(dhs, x, gain, dx_in, name):
    n = len(dhs)

    def fn(i, *refs):
        dh_refs, (x_ref, dxin_ref, g_ref, dx_ref, dg_ref) = refs[:n], refs[n:]
        dh = dh_refs[0][...]
        for r in dh_refs[1:]:
            dh = dh + r[...]
        xv = x_ref[...]
        r = lax.rsqrt(jnp.mean(xv * xv, axis=-1, keepdims=True) + EPS)
        xn = xv * r
        dxn = dh * g_ref[...]
        dx_ref[...] = dxin_ref[...] + r * (dxn - xn * jnp.mean(dxn * xn, axis=-1, keepdims=True))

        @pl.when(i == 0)
        def _():
            dg_ref[...] = jnp.zeros_like(dg_ref)

        dg_ref[...] += jnp.sum(dh * xn, axis=0, keepdims=True)

    return _rowwise(name, fn, list(dhs) + [x, dx_in], [gain], [(x.shape, F32)], [((1, x.shape[1]), F32)])


def loss_grad(y, target, name):
    def fn(i, y_ref, t_ref, dy_ref, sq_ref):
        err = y_ref[...] - t_ref[...]
        dy_ref[...] = err * (1.0 / y_ref.shape[1])

        @pl.when(i == 0)
        def _():
            sq_ref[...] = jnp.zeros_like(sq_ref)

        sq_ref[...] += jnp.sum(err * err, axis=0, keepdims=True)

    return _rowwise(name, fn, [y, target], [], [(y.shape, F32)], [((1, y.shape[1]), F32)])


def matmul_nn(a, b, name, out_dtype, tm, tn, res=None, scale=1.0, rider=None):
    s, k = a.shape
    n = b.shape[1]

    def body(*refs):
        if res is None:
            a_ref, b_ref, o_ref = refs
            o_ref[...] = _dot(a_ref[...], b_ref[...]).astype(o_ref.dtype)
        else:
            a_ref, b_ref, r_ref, o_ref = refs
            o_ref[...] = (r_ref[...] + scale * _dot(a_ref[...], b_ref[...])).astype(o_ref.dtype)

    in_specs = [pl.BlockSpec((tm, k), lambda i, j: (i, 0)), pl.BlockSpec((k, tn), lambda i, j: (0, j))]
    args = [a, b]
    if res is not None:
        in_specs.append(pl.BlockSpec((tm, tn), lambda i, j: (i, j)))
        args.append(res)
    return _call(body, name=name, out_shape=_sds((s, n), out_dtype), grid=(s // tm, n // tn), in_specs=in_specs,
                 out_specs=pl.BlockSpec((tm, tn), lambda i, j: (i, j)), rider=rider)(*args)


def matmul_nt(a, b, name, out_dtype, tm, tn, tk, rider=None):
    s, k = a.shape
    n = b.shape[0]
    nk = k // tk

    def body(a_ref, b_ref, o_ref, acc_ref):
        kk = pl.program_id(2)

        @pl.when(kk == 0)
        def _():
            acc_ref[...] = jnp.zeros_like(acc_ref)

        acc_ref[...] += _dot_nt(a_ref[...].astype(BF16), b_ref[...])

        @pl.when(kk == nk - 1)
        def _():
            o_ref[...] = acc_ref[...].astype(o_ref.dtype)

    return _call(body, name=name, out_shape=_sds((s, n), out_dtype), grid=(s // tm, n // tn, nk),
                 in_specs=[pl.BlockSpec((tm, tk), lambda i, j, kk: (i, kk)),
                           pl.BlockSpec((tn, tk), lambda i, j, kk: (j, kk))],
                 out_specs=pl.BlockSpec((tm, tn), lambda i, j, kk: (i, j)),
                 scratch_shapes=[pltpu.VMEM((tm, tn), F32)], rider=rider)(a, b)


def matmul_tn(a, b, name, tn, ts, a_scale=None, b_scale=None, rider=None):
    s, m = a.shape
    n = b.shape[1]
    ns = s // ts

    def body(a_ref, b_ref, o_ref, acc_ref):
        ss = pl.program_id(1)

        @pl.when(ss == 0)
        def _():
            acc_ref[...] = jnp.zeros_like(acc_ref)

        av, bv = a_ref[...], b_ref[...]
        if a_scale is not None:
            av = av * a_scale
        if b_scale is not None:
            bv = bv * b_scale
        acc_ref[...] += _dot_tn(av.astype(BF16), bv.astype(BF16))

        @pl.when(ss == ns - 1)
        def _():
            o_ref[...] = acc_ref[...].astype(o_ref.dtype)

    return _call(body, name=name, out_shape=_sds((m, n), BF16), grid=(n // tn, ns),
                 in_specs=[pl.BlockSpec((ts, m), lambda j, ss: (ss, 0)), pl.BlockSpec((ts, tn), lambda j, ss: (ss, j))],
                 out_specs=pl.BlockSpec((m, tn), lambda j, ss: (0, j)),
                 scratch_shapes=[pltpu.VMEM((m, tn), F32)], rider=rider)(a, b)


def _piece_specs(pieces, tile, rows_tile, tile_axis_first):
    specs, ranges, t0 = [], [], 0
    for a in pieces:
        n = a.shape[1] // tile

        def index(*ids, t0=t0, n=n):
            t, r = (ids[0], ids[1]) if tile_axis_first else (ids[2], ids[0])
            on = jnp.logical_and(t >= t0, t < t0 + n)
            return jnp.where(on, r, 0), jnp.clip(t - t0, 0, n - 1)

        specs.append(pl.BlockSpec((rows_tile, tile), index))
        ranges.append((t0, n))
        t0 += n
    return specs, ranges


def matmul_tn_pieces(a, pieces, name, tn, ts, rider=None):
    s, m = a.shape
    ns = s // ts
    specs, ranges = _piece_specs(pieces, tn, ts, True)
    n_total = sum(n for _, n in ranges)

    def body(a_ref, *refs):
        b_refs, o_ref, acc_ref = refs[:len(pieces)], refs[-2], refs[-1]
        j, ss = pl.program_id(0), pl.program_id(1)

        @pl.when(ss == 0)
        def _():
            acc_ref[...] = jnp.zeros_like(acc_ref)

        for b_ref, (t0, n) in zip(b_refs, ranges):
            @pl.when(jnp.logical_and(j >= t0, j < t0 + n))
            def _(b_ref=b_ref):
                acc_ref[...] += _dot_tn(a_ref[...], b_ref[...])

        @pl.when(ss == ns - 1)
        def _():
            o_ref[...] = acc_ref[...].astype(o_ref.dtype)

    return _call(body, name=name, out_shape=_sds((m, n_total * tn), BF16), grid=(n_total, ns),
                 in_specs=[pl.BlockSpec((ts, m), lambda j, ss: (ss, 0))] + specs,
                 out_specs=pl.BlockSpec((m, tn), lambda j, ss: (0, j)),
                 scratch_shapes=[pltpu.VMEM((m, tn), F32)], rider=rider)(a, *pieces)


def matmul_nt_pieces(pieces, b, name, out_dtype, tm, tn, tk, rider=None):
    s = pieces[0].shape[0]
    n = b.shape[0]
    specs, ranges = _piece_specs(pieces, tk, tm, False)
    nk = sum(cnt for _, cnt in ranges)

    def body(*refs):
        a_refs, b_ref, o_ref, acc_ref = refs[:len(pieces)], refs[-3], refs[-2], refs[-1]
        kk = pl.program_id(2)

        @pl.when(kk == 0)
        def _():
            acc_ref[...] = jnp.zeros_like(acc_ref)

        for a_ref, (t0, cnt) in zip(a_refs, ranges):
            @pl.when(jnp.logical_and(kk >= t0, kk < t0 + cnt))
            def _(a_ref=a_ref):
                acc_ref[...] += _dot_nt(a_ref[...], b_ref[...])

        @pl.when(kk == nk - 1)
        def _():
            o_ref[...] = acc_ref[...].astype(o_ref.dtype)

    return _call(body, name=name, out_shape=_sds((s, n), out_dtype), grid=(s // tm, n // tn, nk),
                 in_specs=specs + [pl.BlockSpec((tn, tk), lambda i, j, kk: (j, kk))],
                 out_specs=pl.BlockSpec((tm, tn), lambda i, j, kk: (i, j)),
                 scratch_shapes=[pltpu.VMEM((tm, tn), F32)], rider=rider)(*pieces, b)


def ffn_up(h, w704, gate_blk, up_blk, name, tm=512, rider=None):
    s = h.shape[0]

    def body(h_ref, wg_ref, wu_ref, g_ref, u_ref, a_ref):
        hv = h_ref[...]
        g = _dot(hv, wg_ref[...])
        u = _dot(hv, wu_ref[...])
        g_ref[...] = g.astype(BF16)
        u_ref[...] = u.astype(BF16)
        a_ref[...] = (g * _sigmoid(g) * u).astype(BF16)

    ospec = pl.BlockSpec((None, tm, FF_SHARD), lambda j, i: (j, i, 0))
    shp = _sds((N_CHIP, s, FF_SHARD), BF16)
    return _call(body, name=name, out_shape=[shp, shp, shp], grid=(N_CHIP, s // tm),
                 in_specs=[pl.BlockSpec((tm, D_MODEL), lambda j, i: (i, 0)),
                           pl.BlockSpec((None, D_MODEL, FF_SHARD), lambda j, i: (j, gate_blk, 0)),
                           pl.BlockSpec((None, D_MODEL, FF_SHARD), lambda j, i: (j, up_blk, 0))],
                 out_specs=[ospec, ospec, ospec], rider=rider)(h, w704, w704)


def ffn_down(a, w1024, blk, x, name, tm=512):
    s = x.shape[0]

    def body(a_ref, wd_ref, x_ref, o_ref):
        acc = _dot(a_ref[0], wd_ref[0])
        for j in range(1, N_CHIP):
            acc += _dot(a_ref[j], wd_ref[j])
        o_ref[...] = x_ref[...] + 0.5 * acc

    return _call(body, name=name, out_shape=_sds((s, D_MODEL), F32), grid=(s // tm,),
                 in_specs=[pl.BlockSpec((N_CHIP, tm, FF_SHARD), lambda i: (0, i, 0)),
                           pl.BlockSpec((N_CHIP, FF_SHARD, D_MODEL), lambda i: (0, blk, 0)),
                           pl.BlockSpec((tm, D_MODEL), lambda i: (i, 0))],
                 out_specs=pl.BlockSpec((tm, D_MODEL), lambda i: (i, 0)))(a, w1024, x)


def ffn_bwd_hidden(dx, w1024, blk, g, u, name, tm=512, rider=None):
    s = dx.shape[0]

    def body(dx_ref, wd_ref, g_ref, u_ref, dg_ref, du_ref):
        dy = (0.5 * dx_ref[...]).astype(BF16)
        da = _dot_nt(dy, wd_ref[...])
        gv = g_ref[...].astype(F32)
        uv = u_ref[...].astype(F32)
        sg = _sigmoid(gv)
        dg_ref[...] = (da * uv * (sg * (1.0 + gv * (1.0 - sg)))).astype(BF16)
        du_ref[...] = (da * gv * sg).astype(BF16)

    hspec = pl.BlockSpec((None, tm, FF_SHARD), lambda j, i: (j, i, 0))
    shp = _sds((N_CHIP, s, FF_SHARD), BF16)
    return _call(body, name=name, out_shape=[shp, shp], grid=(N_CHIP, s // tm),
                 in_specs=[pl.BlockSpec((tm, D_MODEL), lambda j, i: (i, 0)),
                           pl.BlockSpec((None, FF_SHARD, D_MODEL), lambda j, i: (j, blk, 0)), hspec, hspec],
                 out_specs=[hspec, hspec], rider=rider)(dx, w1024, g, u)


def ffn_bwd_input(dg, du, w704, gate_blk, up_blk, name, tm=512, rider=None):
    s = dg.shape[1]

    def body(dg_ref, du_ref, wg_ref, wu_ref, o_ref):
        acc = _dot_nt(dg_ref[0], wg_ref[0]) + _dot_nt(du_ref[0], wu_ref[0])
        for j in range(1, N_CHIP):
            acc += _dot_nt(dg_ref[j], wg_ref[j]) + _dot_nt(du_ref[j], wu_ref[j])
        o_ref[...] = acc

    hspec = pl.BlockSpec((N_CHIP, tm, FF_SHARD), lambda i: (0, i, 0))
    return _call(body, name=name, out_shape=_sds((s, D_MODEL), F32), grid=(s // tm,),
                 in_specs=[hspec, hspec,
                           pl.BlockSpec((N_CHIP, D_MODEL, FF_SHARD), lambda i: (0, gate_blk, 0), pl.Buffered(1)),
                           pl.BlockSpec((N_CHIP, D_MODEL, FF_SHARD), lambda i: (0, up_blk, 0), pl.Buffered(1))],
                 out_specs=pl.BlockSpec((tm, D_MODEL), lambda i: (i, 0)), rider=rider)(dg, du, w704, w704)


def ffn_wgrad_in(h, dgu, name, ts=1024):
    s = h.shape[0]
    ns = s // ts

    def body(h_ref, d_ref, o_ref, acc_ref):
        ss = pl.program_id(1)

        @pl.when(ss == 0)
        def _():
            acc_ref[...] = jnp.zeros_like(acc_ref)

        acc_ref[...] += _dot_tn(h_ref[...], d_ref[...])

        @pl.when(ss == ns - 1)
        def _():
            o_ref[...] = acc_ref[...].astype(BF16)

    return _call(body, name=name, out_shape=_sds((N_CHIP, D_MODEL, FF_SHARD), BF16), grid=(N_CHIP, ns),
                 in_specs=[pl.BlockSpec((ts, D_MODEL), lambda j, ss: (ss, 0)),
                           pl.BlockSpec((None, ts, FF_SHARD), lambda j, ss: (j, ss, 0))],
                 out_specs=pl.BlockSpec((None, D_MODEL, FF_SHARD), lambda j, ss: (j, 0, 0)),
                 scratch_shapes=[pltpu.VMEM((D_MODEL, FF_SHARD), F32)])(h, dgu)


def ffn_wgrad_down(a, dx, name, ts=1024):
    s = dx.shape[0]
    ns = s // ts

    def body(a_ref, dx_ref, o_ref, acc_ref):
        ss = pl.program_id(1)

        @pl.when(ss == 0)
        def _():
            acc_ref[...] = jnp.zeros_like(acc_ref)

        acc_ref[...] += _dot_tn(a_ref[...], (0.5 * dx_ref[...]).astype(BF16))

        @pl.when(ss == ns - 1)
        def _():
            o_ref[...] = acc_ref[...].astype(BF16)

    return _call(body, name=name, out_shape=_sds((N_CHIP, FF_SHARD, D_MODEL), BF16), grid=(N_CHIP, ns),
                 in_specs=[pl.BlockSpec((None, ts, FF_SHARD), lambda j, ss: (j, ss, 0)),
                           pl.BlockSpec((ts, D_MODEL), lambda j, ss: (ss, 0))],
                 out_specs=pl.BlockSpec((None, FF_SHARD, D_MODEL), lambda j, ss: (j, 0, 0)),
                 scratch_shapes=[pltpu.VMEM((FF_SHARD, D_MODEL), F32)])(a, dx)


def ffn_forward(x, gain, get_w704, get_w1024, tag, rms_rider=None, up_rider=None):
    h = rms_fwd(x, gain, f"{tag}_rms", rider=rms_rider)
    h, rode_rms = h if rms_rider is not None else (h, None)
    res = ffn_up(h, get_w704(rode_rms), 0, 1, f"{tag}_up", rider=up_rider)
    (g, u, a), rode_up = res if up_rider is not None else (res, None)
    y = ffn_down(a, get_w1024(rode_up), 0, x, f"{tag}_down")
    return y, (h, g, u, a), rode_rms, rode_up


def ffn_backward(dy, x, gain, w704, w1024, saved, tag, ride_down=None, ride_in=None):
    h, g, u, a = saved
    d_wd = ffn_wgrad_down(a, dy, f"{tag}_dwd")
    if ride_down is not None:
        (dg, du), d_wd = ffn_bwd_hidden(dy, w1024, 0, g, u, f"{tag}_dhid", rider=ride_down(d_wd))
    else:
        dg, du = ffn_bwd_hidden(dy, w1024, 0, g, u, f"{tag}_dhid")
    d_win = jnp.concatenate([ffn_wgrad_in(h, dg, f"{tag}_dwg"), ffn_wgrad_in(h, du, f"{tag}_dwu")], axis=1)
    if ride_in is not None:
        dh, d_win = ffn_bwd_input(dg, du, w704, 0, 1, f"{tag}_dh", rider=ride_in(d_win))
    else:
        dh = ffn_bwd_input(dg, du, w704, 0, 1, f"{tag}_dh")
    dx, d_gain = rms_bwd([dh], x, gain, dy, f"{tag}_drms")
    return dx, d_gain, d_win, d_wd


def _alibi_slope(head):
    return float(2.0 ** (-ALIBI_MAX_EXP * (head + 1) / N_ATTN_HEADS))


def _head_norm(t, gain_pair, first):
    sa, sb = _pair_sum(t * t, first)
    r = jnp.where(first, lax.rsqrt(sa * (1.0 / HD) + EPS), lax.rsqrt(sb * (1.0 / HD) + EPS))
    return t * r * gain_pair, r


def qk_norm_fwd(p, q_gain, k_gain, name):
    s = p.shape[0]

    def fn(i, q_ref, k_ref, qg_ref, kg_ref, qn_ref, kn_ref):
        first = _lane_first_half((q_ref.shape[0], 2 * HD))
        for src, g_ref, dst in ((q_ref, qg_ref, qn_ref), (k_ref, kg_ref, kn_ref)):
            for pr in range(ATTN_QKV // (2 * HD)):
                cols = slice(pr * 2 * HD, (pr + 1) * 2 * HD)
                y, _ = _head_norm(src[:, cols].astype(F32), g_ref[...], first)
                dst[:, cols] = y.astype(BF16)

    return _rowwise(name, fn, [(p, ATTN_QKV, 0), (p, ATTN_QKV, 1)], [q_gain, k_gain],
                    [((s, ATTN_QKV), BF16), ((s, ATTN_QKV), BF16)])


def qk_norm_bwd(p, dqs, dks, q_gain, k_gain, name):
    s = p.shape[0]
    pairs_per_pattern = GROUP_W // (2 * HD)

    def fn(i, q_ref, k_ref, dq0, dq1, dq2, dk0, dk1, dk2, qg_ref, kg_ref, dqk_ref, dqg_ref, dkg_ref):
        first = _lane_first_half((q_ref.shape[0], 2 * HD))

        @pl.when(i == 0)
        def _():
            dqg_ref[...] = jnp.zeros_like(dqg_ref)
            dkg_ref[...] = jnp.zeros_like(dkg_ref)

        for src, d_refs, g_ref, dst, dg_ref in (
                (q_ref, (dq0, dq1, dq2), qg_ref, dqk_ref.at[:, 0:ATTN_QKV], dqg_ref),
                (k_ref, (dk0, dk1, dk2), kg_ref, dqk_ref.at[:, ATTN_QKV:2 * ATTN_QKV], dkg_ref)):
            for pr in range(ATTN_QKV // (2 * HD)):
                cols = slice(pr * 2 * HD, (pr + 1) * 2 * HD)
                t = src[:, cols].astype(F32)
                sa, sb = _pair_sum(t * t, first)
                r = jnp.where(first, lax.rsqrt(sa * (1.0 / HD) + EPS), lax.rsqrt(sb * (1.0 / HD) + EPS))
                xn = t * r
                within = (pr % pairs_per_pattern) * 2 * HD
                dy = d_refs[pr // pairs_per_pattern][:, within:within + 2 * HD]
                dg_ref[:, cols] += jnp.sum(dy * xn, axis=0, keepdims=True)
                dxn = dy * g_ref[...]
                ma, mb = _pair_sum(dxn * xn, first)
                mean = jnp.where(first, ma, mb) * (1.0 / HD)
                dst[:, cols] = (r * (dxn - xn * mean)).astype(BF16)

    return _rowwise(name, fn, [(p, ATTN_QKV, 0), (p, ATTN_QKV, 1)] + list(dqs) + list(dks), [q_gain, k_gain],
                    [((s, 2 * ATTN_QKV), BF16)], [((1, ATTN_QKV), F32), ((1, ATTN_QKV), F32)])


def _to_streams(a, d):
    if d == 1:
        return a
    s, c = a.shape
    return a.reshape(s // d, d, c).transpose(1, 0, 2).reshape(s, c)


def _from_streams(a, d):
    if d == 1:
        return a
    s, c = a.shape
    return a.reshape(d, s // d, c).transpose(1, 0, 2).reshape(s, c)


def _attn_masks():
    row = lax.broadcasted_iota(jnp.int32, (BLK, BLK), 0)
    col = lax.broadcasted_iota(jnp.int32, (BLK, BLK), 1)
    rel_diag = row - col
    rel_prev = rel_diag + BLK
    return rel_diag, rel_prev


def attn_fwd(q, k, v, pattern, name, tq=512):
    s = q.shape[0]
    d = ATTN_DILATIONS[pattern]
    blocks_per_stream = (s // d) // BLK
    nsb = tq // BLK

    def body(q_ref, k_ref, v_ref, kp_ref, vp_ref, o_ref, l_ref):
        i = pl.program_id(0)
        rel_diag, rel_prev = _attn_masks()
        first = _lane_first_half((BLK, 2 * HD))
        rd_f = (rel_diag * d).astype(F32)
        rp_f = (rel_prev * d).astype(F32)
        for sb in range(nsb):
            rows = slice(sb * BLK, (sb + 1) * BLK)
            has_prev = ((i * nsb + sb) % blocks_per_stream != 0).astype(jnp.int32)
            m_diag = rel_diag >= 0
            m_prev = (rel_prev + (1 - has_prev) * (4 * BLK)) <= BLK
            for pr in range(GROUP_W // (2 * HD)):
                cols = slice(pr * 2 * HD, (pr + 1) * 2 * HD)
                qp = q_ref[rows, cols]
                kc, vc = k_ref[rows, cols], v_ref[rows, cols]
                if sb == 0:
                    kp, vp = kp_ref[:, cols], vp_ref[:, cols]
                else:
                    prows = slice((sb - 1) * BLK, sb * BLK)
                    kp, vp = k_ref[prows, cols], v_ref[prows, cols]
                outs, lses = [], []
                for e in range(2):
                    slope = _alibi_slope(pattern * HEADS_PER_PATTERN + 2 * pr + e)
                    qm = jnp.where(first if e == 0 else jnp.logical_not(first), qp, jnp.zeros_like(qp))
                    s1 = jnp.where(m_diag, _dot_nt(qm, kc) * 0.125 - slope * rd_f, NEG)
                    s0 = jnp.where(m_prev, _dot_nt(qm, kp) * 0.125 - slope * rp_f, NEG)
                    m = jnp.maximum(jnp.max(s1, axis=-1, keepdims=True), jnp.max(s0, axis=-1, keepdims=True))
                    p1 = jnp.exp(s1 - m)
                    p0 = jnp.exp(s0 - m)
                    l = jnp.sum(p1, axis=-1, keepdims=True) + jnp.sum(p0, axis=-1, keepdims=True)
                    inv = 1.0 / l
                    outs.append(_dot((p1 * inv).astype(BF16), vc) + _dot((p0 * inv).astype(BF16), vp))
                    lses.append(m + jnp.log(l))
                o_ref[rows, cols] = jnp.where(first, outs[0], outs[1])
                l_ref[rows, cols] = jnp.where(first, lses[0], lses[1])

    cur = pl.BlockSpec((tq, GROUP_W), lambda i: (i, 0))
    prev = pl.BlockSpec((BLK, GROUP_W), lambda i: (jnp.maximum(i * nsb - 1, 0), 0))
    return _call(body, name=name, out_shape=[_sds((s, GROUP_W), F32), _sds((s, GROUP_W), F32)], grid=(s // tq,),
                 in_specs=[cur, cur, cur, prev, prev], out_specs=[cur, cur])(q, k, v, k, v)


def attn_merge_fwd(os_, lses, name):
    s = os_[0].shape[0]

    def fn(i, o0, o1, o2, l0, l1, l2, out_ref):
        m = jnp.maximum(jnp.maximum(l0[...], l1[...]), l2[...])
        e0, e1, e2 = jnp.exp(l0[...] - m), jnp.exp(l1[...] - m), jnp.exp(l2[...] - m)
        inv = 1.0 / (e0 + e1 + e2)
        out_ref[...] = ((e0 * inv) * o0[...] + (e1 * inv) * o1[...] + (e2 * inv) * o2[...]).astype(BF16)

    return _rowwise(name, fn, list(os_) + list(lses), [], [((s, GROUP_W), BF16)])[0]


def attn_merge_bwd(d_out, os_, lses, name):
    s = d_out.shape[0]

    def fn(i, do_ref, o0, o1, o2, l0, l1, l2, d0, d1, d2, c0, c1, c2):
        first = _lane_first_half((do_ref.shape[0], 2 * HD))
        m = jnp.maximum(jnp.maximum(l0[...], l1[...]), l2[...])
        e0, e1, e2 = jnp.exp(l0[...] - m), jnp.exp(l1[...] - m), jnp.exp(l2[...] - m)
        inv = 1.0 / (e0 + e1 + e2)
        w0, w1, w2 = e0 * inv, e1 * inv, e2 * inv
        do = do_ref[...]
        prod = do * (w0 * o0[...] + w1 * o1[...] + w2 * o2[...])
        for pr in range(GROUP_W // (2 * HD)):
            cols = slice(pr * 2 * HD, (pr + 1) * 2 * HD)
            ta, tb = _pair_sum(prod[:, cols], first)
            t = jnp.where(first, ta, tb)
            for w, c_ref in ((w0, c0), (w1, c1), (w2, c2)):
                c_ref[:, cols] = w[:, cols] * t
        for w, d_ref in ((w0, d0), (w1, d1), (w2, d2)):
            d_ref[...] = (w * do).astype(BF16)

    shp = (s, GROUP_W)
    return _rowwise(name, fn, [d_out] + list(os_) + list(lses), [],
                    [(shp, BF16)] * 3 + [(shp, F32)] * 3)


def attn_bwd(q, k, v, d_o, cterm, lse, pattern, name, tq=512):
    s = q.shape[0]
    d = ATTN_DILATIONS[pattern]
    blocks_per_stream = (s // d) // BLK
    nsb = tq // BLK
    n_blocks = s // BLK

    def body(q_ref, k_ref, v_ref, do_ref, c_ref, l_ref, kp_ref, vp_ref, qn_ref, don_ref, cn_ref, ln_ref,
             dq_ref, dk_ref, dv_ref):
        i = pl.program_id(0)
        rel_diag, rel_prev = _attn_masks()
        first = _lane_first_half((BLK, 2 * HD))
        second = jnp.logical_not(first)
        rd_f = (rel_diag * d).astype(F32)
        rp_f = (rel_prev * d).astype(F32)
        m_diag = rel_diag >= 0
        dq_ref[...] = jnp.zeros_like(dq_ref)
        dk_ref[...] = jnp.zeros_like(dk_ref)
        dv_ref[...] = jnp.zeros_like(dv_ref)

        def pair(qp, dop, cp, lp, kp, vp, rel_f, mask):
            dq = dk = dv = None
            for e in range(2):
                lanes = first if e == 0 else second
                slope = slopes[e]
                qm = jnp.where(lanes, qp, jnp.zeros_like(qp))
                dom = jnp.where(lanes, dop, jnp.zeros_like(dop))
                km = jnp.where(lanes, kp, jnp.zeros_like(kp))
                sc = jnp.where(mask, _dot_nt(qm, kp) * 0.125 - slope * rel_f, NEG)
                pm = jnp.exp(sc - lp[:, e * HD:e * HD + 1])
                dl = pm * (_dot_nt(dom, vp) - cp[:, e * HD:e * HD + 1])
                dl16 = dl.astype(BF16)
                t_dq = _dot(dl16, km)
                t_dk = _dot_tn(dl16, qm)
                t_dv = _dot_tn(pm.astype(BF16), dom)
                dq = t_dq if dq is None else dq + t_dq
                dk = t_dk if dk is None else dk + t_dk
                dv = t_dv if dv is None else dv + t_dv
            return dq * 0.125, dk * 0.125, dv

        for pr in range(GROUP_W // (2 * HD)):
            cols = slice(pr * 2 * HD, (pr + 1) * 2 * HD)
            slopes = [_alibi_slope(pattern * HEADS_PER_PATTERN + 2 * pr + e) for e in range(2)]
            for sb in range(nsb + 1):
                gb = i * nsb + sb
                if sb < nsb:
                    rows = slice(sb * BLK, (sb + 1) * BLK)
                    qp, dop, cp, lp = q_ref[rows, cols], do_ref[rows, cols], c_ref[rows, cols], l_ref[rows, cols]
                else:
                    qp, dop, cp, lp = qn_ref[:, cols], don_ref[:, cols], cn_ref[:, cols], ln_ref[:, cols]
                if sb < nsb:
                    dq1, dk1, dv1 = pair(qp, dop, cp, lp, k_ref[rows, cols], v_ref[rows, cols], rd_f, m_diag)
                    dq_ref[rows, cols] += dq1
                    dk_ref[rows, cols] += dk1
                    dv_ref[rows, cols] += dv1
                valid = jnp.logical_and(gb % blocks_per_stream != 0, gb < n_blocks).astype(jnp.int32)
                m_prev = jnp.logical_and(rel_prev <= BLK, (rel_prev + (1 - valid) * (4 * BLK)) <= BLK)
                if sb == 0:
                    kp, vp = kp_ref[:, cols], vp_ref[:, cols]
                else:
                    prows = slice((sb - 1) * BLK, sb * BLK)
                    kp, vp = k_ref[prows, cols], v_ref[prows, cols]
                dq0, dk0, dv0 = pair(qp, dop, cp, lp, kp, vp, rp_f, m_prev)
                if sb < nsb:
                    dq_ref[rows, cols] += dq0
                if sb > 0:
                    dk_ref[prows, cols] += dk0
                    dv_ref[prows, cols] += dv0

    cur = pl.BlockSpec((tq, GROUP_W), lambda i: (i, 0))
    prev = pl.BlockSpec((BLK, GROUP_W), lambda i: (jnp.maximum(i * nsb - 1, 0), 0))
    nxt = pl.BlockSpec((BLK, GROUP_W), lambda i: (jnp.minimum((i + 1) * nsb, n_blocks - 1), 0))
    shp = _sds((s, GROUP_W), F32)
    return _call(body, name=name, out_shape=[shp, shp, shp], grid=(s // tq,),
                 in_specs=[cur] * 6 + [prev, prev] + [nxt] * 4, out_specs=[cur, cur, cur])(
                     q, k, v, d_o, cterm, lse, k, v, q, d_o, cterm, lse)


def _band_constants(d):
    row = lax.broadcasted_iota(jnp.int32, (2 * BLK, 2 * BLK), 0)
    col = lax.broadcasted_iota(jnp.int32, (2 * BLK, 2 * BLK), 1)
    rel = BLK + jnp.where(row >= BLK, row - BLK, row) - col
    band = jnp.logical_and(rel >= 0, rel <= BLK)
    return (rel * d).astype(F32), band, (col >= BLK).astype(jnp.int32)


def _stack_heads(x, first):
    zero = jnp.zeros_like(x)
    return jnp.concatenate([jnp.where(first, x, zero), jnp.where(first, zero, x)], axis=0)


def _unstack_heads(x2, first):
    return jnp.where(first, x2[:BLK], x2[BLK:])


def _head_column(x):
    return jnp.concatenate([x[:, 0:1], x[:, HD:HD + 1]], axis=0)


def attn_fwd2(q, k, v, pattern, name, tq=512):
    s = q.shape[0]
    d = ATTN_DILATIONS[pattern]
    blocks_per_stream = (s // d) // BLK
    nsb = tq // BLK

    def body(q_ref, k_ref, v_ref, kp_ref, vp_ref, o_ref, l_ref):
        i = pl.program_id(0)
        rel_f, band, own = _band_constants(d)
        first = _lane_first_half((BLK, 2 * HD))
        upper = lax.broadcasted_iota(jnp.int32, (2 * BLK, 1), 0) < BLK
        for sb in range(nsb):
            rows = slice(sb * BLK, (sb + 1) * BLK)
            has_prev = ((i * nsb + sb) % blocks_per_stream != 0).astype(jnp.int32)
            mask = jnp.logical_and(band, (own + has_prev) > 0)
            for pr in range(GROUP_W // (2 * HD)):
                cols = slice(pr * 2 * HD, (pr + 1) * 2 * HD)
                if sb == 0:
                    kcat = jnp.concatenate([kp_ref[:, cols], k_ref[rows, cols]], axis=0)
                    vcat = jnp.concatenate([vp_ref[:, cols], v_ref[rows, cols]], axis=0)
                else:
                    both = slice((sb - 1) * BLK, (sb + 1) * BLK)
                    kcat, vcat = k_ref[both, cols], v_ref[both, cols]
                h0 = pattern * HEADS_PER_PATTERN + 2 * pr
                slope = jnp.where(upper, _alibi_slope(h0), _alibi_slope(h0 + 1))
                sc = _dot_nt(_stack_heads(q_ref[rows, cols], first), kcat) * 0.125 - slope * rel_f
                sc = jnp.where(mask, sc, NEG)
                m = jnp.max(sc, axis=-1, keepdims=True)
                p = jnp.exp(sc - m)
                l = jnp.sum(p, axis=-1, keepdims=True)
                o2 = _dot((p * (1.0 / l)).astype(BF16), vcat)
                o_ref[rows, cols] = _unstack_heads(o2, first)
                lse = m + jnp.log(l)
                l_ref[rows, cols] = jnp.where(first, lse[:BLK], lse[BLK:])

    cur = pl.BlockSpec((tq, GROUP_W), lambda i: (i, 0))
    prev = pl.BlockSpec((BLK, GROUP_W), lambda i: (jnp.maximum(i * nsb - 1, 0), 0))
    return _call(body, name=name, out_shape=[_sds((s, GROUP_W), F32), _sds((s, GROUP_W), F32)], grid=(s // tq,),
                 in_specs=[cur, cur, cur, prev, prev], out_specs=[cur, cur])(q, k, v, k, v)


def attn_bwd2(q, k, v, d_o, cterm, lse, pattern, name, tq=512):
    s = q.shape[0]
    d = ATTN_DILATIONS[pattern]
    blocks_per_stream = (s // d) // BLK
    nsb = tq // BLK
    n_blocks = s // BLK

    def body(q_ref, k_ref, v_ref, do_ref, c_ref, l_ref, kp_ref, vp_ref, qn_ref, kn_ref, vn_ref, don_ref, cn_ref,
             ln_ref, dq_ref, dk_ref, dv_ref):
        i = pl.program_id(0)
        rel_f, band, own = _band_constants(d)
        first = _lane_first_half((BLK, 2 * HD))
        upper = lax.broadcasted_iota(jnp.int32, (2 * BLK, 1), 0) < BLK
        dk_ref[...] = jnp.zeros_like(dk_ref)
        dv_ref[...] = jnp.zeros_like(dv_ref)
        for sb in range(nsb + 1):
            gb = i * nsb + sb
            rows = slice(sb * BLK, (sb + 1) * BLK)
            before = slice((sb - 1) * BLK, sb * BLK)
            inside = (gb < n_blocks).astype(jnp.int32)
            has_prev = jnp.logical_and(gb % blocks_per_stream != 0, gb < n_blocks).astype(jnp.int32)
            mask = jnp.logical_and(band, (own * inside + has_prev) > 0)
            for pr in range(GROUP_W // (2 * HD)):
                cols = slice(pr * 2 * HD, (pr + 1) * 2 * HD)
                if sb == 0:
                    kcat = jnp.concatenate([kp_ref[:, cols], k_ref[rows, cols]], axis=0)
                    vcat = jnp.concatenate([vp_ref[:, cols], v_ref[rows, cols]], axis=0)
                elif sb == nsb:
                    kcat = jnp.concatenate([k_ref[before, cols], kn_ref[:, cols]], axis=0)
                    vcat = jnp.concatenate([v_ref[before, cols], vn_ref[:, cols]], axis=0)
                else:
                    both = slice((sb - 1) * BLK, (sb + 1) * BLK)
                    kcat, vcat = k_ref[both, cols], v_ref[both, cols]
                if sb < nsb:
                    qp, dop, cp, lp = q_ref[rows, cols], do_ref[rows, cols], c_ref[rows, cols], l_ref[rows, cols]
                else:
                    qp, dop, cp, lp = qn_ref[:, cols], don_ref[:, cols], cn_ref[:, cols], ln_ref[:, cols]
                h0 = pattern * HEADS_PER_PATTERN + 2 * pr
                slope = jnp.where(upper, _alibi_slope(h0), _alibi_slope(h0 + 1))
                q2 = _stack_heads(qp, first)
                do2 = _stack_heads(dop, first)
                sc = jnp.where(mask, _dot_nt(q2, kcat) * 0.125 - slope * rel_f, NEG)
                pm = jnp.exp(sc - _head_column(lp))
                dl = (pm * (_dot_nt(do2, vcat) - _head_column(cp))).astype(BF16)
                if sb < nsb:
                    dq_ref[rows, cols] = _unstack_heads(_dot(dl, kcat), first) * 0.125
                dk2 = _dot_tn(dl, q2) * 0.125
                dv2 = _dot_tn(pm.astype(BF16), do2)
                if sb > 0:
                    dk_ref[before, cols] += dk2[:BLK]
                    dv_ref[before, cols] += dv2[:BLK]
                if sb < nsb:
                    dk_ref[rows, cols] += dk2[BLK:]
                    dv_ref[rows, cols] += dv2[BLK:]

    cur = pl.BlockSpec((tq, GROUP_W), lambda i: (i, 0))
    prev = pl.BlockSpec((BLK, GROUP_W), lambda i: (jnp.maximum(i * nsb - 1, 0), 0))
    nxt = pl.BlockSpec((BLK, GROUP_W), lambda i: (jnp.minimum((i + 1) * nsb, n_blocks - 1), 0))
    shp = _sds((s, GROUP_W), F32)
    return _call(body, name=name, out_shape=[shp, shp, shp], grid=(s // tq,),
                 in_specs=[cur] * 6 + [prev, prev] + [nxt] * 6, out_specs=[cur, cur, cur])(
                     q, k, v, d_o, cterm, lse, k, v, q, k, v, d_o, cterm, lse)


HALO = 16
CONV_TQ = 512


def conv_fwd(p, w, b, name):
    s = p.shape[0]
    tq = CONV_TQ
    ncol = SSD_CONV_DIM // GROUP_W
    cb0 = COL_XBC // GROUP_W

    def body(u_ref, up_ref, w_ref, b_ref, c_ref, xc_ref):
        i = pl.program_id(0)
        prev = up_ref[...].astype(F32) * (i > 0).astype(F32)
        ext = jnp.concatenate([prev, u_ref[...].astype(F32)], axis=0)
        acc = b_ref[...] + w_ref[SSD_CONV - 1:SSD_CONV, :] * ext[HALO:HALO + tq]
        for kk in range(SSD_CONV - 1):
            acc += w_ref[kk:kk + 1, :] * pltpu.roll(ext, SSD_CONV - 1 - kk, 0)[HALO:HALO + tq]
        c_ref[...] = acc.astype(BF16)
        xc_ref[...] = (acc * _sigmoid(acc)).astype(BF16)

    cur_in = pl.BlockSpec((tq, GROUP_W), lambda i, j: (i, cb0 + j))
    prev_in = pl.BlockSpec((HALO, GROUP_W), lambda i, j: (jnp.maximum(i * (tq // HALO) - 1, 0), cb0 + j))
    cur_out = pl.BlockSpec((tq, GROUP_W), lambda i, j: (i, j))
    shp = _sds((s, SSD_CONV_DIM), BF16)
    return _call(body, name=name, out_shape=[shp, shp], grid=(s // tq, ncol),
                 in_specs=[cur_in, prev_in, pl.BlockSpec((SSD_CONV, GROUP_W), lambda i, j: (0, j)),
                           pl.BlockSpec((1, GROUP_W), lambda i, j: (0, j))],
                 out_specs=[cur_out, cur_out])(p, p, w, b)


def conv_bwd(p, cpre, dxs, d_b, d_c, w, name):
    s = p.shape[0]
    tq = CONV_TQ
    ncol = SSD_CONV_DIM // GROUP_W
    n_xs = SSD_INNER // GROUP_W
    cb0 = COL_XBC // GROUP_W
    nt = s // tq

    def body(u_ref, up_ref, c_ref, cn_ref, dx_ref, dxn_ref, dbm_ref, dbmn_ref, dcm_ref, dcmn_ref, w_ref,
             du_ref, dw_ref, db_ref):
        j, i = pl.program_id(0), pl.program_id(1)

        def dpre(c16, dx):
            c = c16.astype(F32)
            sg = _sigmoid(c)
            return dx * (sg * (1.0 + c * (1.0 - sg)))

        def pick(a_ref, b_ref, c_ref_):
            return jnp.where(j < n_xs, a_ref[...], jnp.where(j == n_xs, b_ref[...], c_ref_[...]))

        dc = dpre(c_ref[...], pick(dx_ref, dbm_ref, dcm_ref))
        dcn = dpre(cn_ref[...], pick(dxn_ref, dbmn_ref, dcmn_ref)) * (i < nt - 1).astype(F32)
        dext = jnp.concatenate([dc, dcn], axis=0)
        prev = up_ref[...].astype(F32) * (i > 0).astype(F32)
        uext = jnp.concatenate([prev, u_ref[...].astype(F32)], axis=0)

        @pl.when(i == 0)
        def _():
            dw_ref[...] = jnp.zeros_like(dw_ref)
            db_ref[...] = jnp.zeros_like(db_ref)

        du = w_ref[SSD_CONV - 1:SSD_CONV, :] * dc
        for kk in range(SSD_CONV - 1):
            sh = SSD_CONV - 1 - kk
            du += w_ref[kk:kk + 1, :] * pltpu.roll(dext, tq + HALO - sh, 0)[0:tq]
        du_ref[...] = du.astype(BF16)
        for kk in range(SSD_CONV):
            shifted = uext if kk == SSD_CONV - 1 else pltpu.roll(uext, SSD_CONV - 1 - kk, 0)
            dw_ref[kk:kk + 1, :] += jnp.sum(dc * shifted[HALO:HALO + tq], axis=0, keepdims=True)
        db_ref[...] += jnp.sum(dc, axis=0, keepdims=True)

    hb = tq // HALO
    cur_p = pl.BlockSpec((tq, GROUP_W), lambda j, i: (i, cb0 + j))
    prev_p = pl.BlockSpec((HALO, GROUP_W), lambda j, i: (jnp.maximum(i * hb - 1, 0), cb0 + j))
    cur = pl.BlockSpec((tq, GROUP_W), lambda j, i: (i, j))
    nxt = pl.BlockSpec((HALO, GROUP_W), lambda j, i: (jnp.minimum((i + 1) * hb, s // HALO - 1), j))

    def piece(first_tile, n_tiles):
        def on(j):
            return jnp.logical_and(j >= first_tile, j < first_tile + n_tiles)

        def col(j):
            return jnp.clip(j - first_tile, 0, n_tiles - 1)

        return (pl.BlockSpec((tq, GROUP_W), lambda j, i: (jnp.where(on(j), i, 0), col(j))),
                pl.BlockSpec((HALO, GROUP_W),
                             lambda j, i: (jnp.where(on(j), jnp.minimum((i + 1) * hb, s // HALO - 1), 0), col(j))))

    return _call(body, name=name,
                 out_shape=[_sds((s, SSD_CONV_DIM), BF16), _sds((8, SSD_CONV_DIM), F32), _sds((1, SSD_CONV_DIM), F32)],
                 grid=(ncol, nt),
                 in_specs=[cur_p, prev_p, cur, nxt, *piece(0, n_xs), *piece(n_xs, 1), *piece(n_xs + 1, 1),
                           pl.BlockSpec((SSD_CONV, GROUP_W), lambda j, i: (0, j))],
                 out_specs=[cur, pl.BlockSpec((8, GROUP_W), lambda j, i: (0, j)),
                            pl.BlockSpec((1, GROUP_W), lambda j, i: (0, j))])(
                                p, p, cpre, cpre, dxs, dxs, d_b, d_b, d_c, d_c, w)


def _softplus(x):
    return jnp.maximum(x, 0.0) + jnp.log(1.0 + jnp.exp(-jnp.abs(x)))


def _ssd_decays(dtr_ref, dtrt_ref, bias_ref, biast_ref, alog_ref, alogt_ref):
    row = lax.broadcasted_iota(jnp.int32, (BLK, BLK), 0)
    col = lax.broadcasted_iota(jnp.int32, (BLK, BLK), 1)
    lower = (row >= col).astype(F32)
    upper = (row <= col).astype(F32)
    dtb = dtr_ref[...] + bias_ref[...]
    dt = _softplus(dtb)
    a = dt * (-jnp.exp(alog_ref[...]))
    cs = _dot_hi(lower, a)
    a_t = _softplus(dtrt_ref[...] + biast_ref[...]) * (-jnp.exp(alogt_ref[...]))
    cs_t = _dot_hi(a_t, upper)
    return dtb, dt, cs, cs_t, row, col, upper


SSD_GROUPS_PER_STEP = 4


def _per_group(body, gps, kinds):
    def wrapped(*refs):
        for gi in range(gps):
            args, pos = [], 0
            for kind, n in kinds:
                if kind == "each":
                    args.append(refs[pos + gi])
                    pos += gps
                    continue
                ref = refs[pos]
                pos += 1
                if kind == "cols":
                    args.append(ref.at[:, gi * n:(gi + 1) * n])
                else:
                    args.append(ref.at[gi] if n == 1 else ref.at[pl.ds(gi * n, n)])
            body(*args)

    return wrapped


def ssd_fwd(p, xc, dtg, dtg_t, params, gn, name):
    s = p.shape[0]
    nc = s // BLK
    bias, bias_t, alog, alog_t, dskip = params

    def body(xs_ref, b_ref, c_ref, z_ref, dtr_ref, dtrt_ref, bias_ref, biast_ref, alog_ref, alogt_ref, dsk_ref,
             gn_ref, y_ref, sin_ref, hp_ref, h_ref):
        c_idx = pl.program_id(1)

        @pl.when(c_idx == 0)
        def _():
            h_ref[...] = jnp.zeros_like(h_ref)

        _, dt, cs, cs_t, row, col, _ = _ssd_decays(dtr_ref, dtrt_ref, bias_ref, biast_ref, alog_ref, alogt_ref)
        first = _lane_first_half((BLK, 2 * HD))
        first_row = _lane_first_half((1, 2 * HD))
        tril = row >= col
        b16, c16 = b_ref[...], c_ref[...]
        cb = _dot_nt(c16, b16)
        n_pairs = GROUP_W // (2 * HD)
        tot = cs[BLK - 1:BLK, :]
        exp_cs, exp_rest, exp_tot = jnp.exp(cs), jnp.exp(tot - cs), jnp.exp(tot)

        def per_head(v, mask):
            return jnp.concatenate([jnp.where(mask, v[:, 2 * pr:2 * pr + 1], v[:, 2 * pr + 1:2 * pr + 2])
                                    for pr in range(n_pairs)], axis=1)

        xs = xs_ref[...].astype(F32)
        xt = xs * per_head(dt, first)
        xt16 = xt.astype(BF16)
        hstate = jnp.concatenate([h_ref[pr] for pr in range(n_pairs)], axis=1)
        for pr in range(n_pairs):
            hp_ref[pr] = h_ref[pr]
        y_off = per_head(exp_cs, first) * _dot(c16, hstate.astype(BF16))
        new = per_head(exp_tot, first_row) * hstate + _dot_tn(b16, (per_head(exp_rest, first) * xt).astype(BF16))
        for pr in range(n_pairs):
            h_ref[pr] = new[:, pr * 2 * HD:(pr + 1) * 2 * HD]
        y_diag = []
        for pr in range(n_pairs):
            cols = slice(pr * 2 * HD, (pr + 1) * 2 * HD)
            m2 = jnp.concatenate(
                [(cb * jnp.exp(jnp.where(tril, cs[:, h:h + 1] - cs_t[h:h + 1, :], NEG))).astype(BF16)
                 for h in (2 * pr, 2 * pr + 1)], axis=1)
            y_diag.append(_dot(m2, _stack_heads(xt16[:, cols], first)))
        y = jnp.concatenate(y_diag, axis=1) + y_off + xs * per_head(dsk_ref[...], first_row)
        y_ref[...] = y
        zv = z_ref[...].astype(F32)
        yz = y * (zv * _sigmoid(zv))
        r = lax.rsqrt(jnp.mean(yz * yz, axis=-1, keepdims=True) + EPS)
        sin_ref[...] = (yz * r * gn_ref[...]).astype(BF16)

    gps = SSD_GROUPS_PER_STEP
    wide, narrow, lead = ("cols", GROUP_W), ("cols", BLK), ("lead", 1)
    kinds = [wide, narrow, narrow, ("each", 0)] + [lead] * 7 + [wide, wide, wide, lead, ("lead", 4)]
    wide_w, narrow_w = GROUP_W * gps, BLK * gps
    gparam = pl.BlockSpec((gps, 1, 8), lambda g, c: (g, 0, 0))
    gparam_t = pl.BlockSpec((gps, 8, 1), lambda g, c: (g, 0, 0))
    z_specs = [pl.BlockSpec((BLK, GROUP_W), functools.partial(lambda g, c, gi: (c, COL_Z // GROUP_W + gps * g + gi),
                                                              gi=gi)) for gi in range(gps)]
    return _call(
        _per_group(body, gps, kinds), name=name,
        out_shape=[_sds((s, SSD_INNER), F32), _sds((s, SSD_INNER), BF16),
                   _sds((SSD_GROUPS, nc, 4, BLK, 2 * HD), F32)],
        grid=(SSD_GROUPS // gps, nc),
        in_specs=[pl.BlockSpec((BLK, wide_w), lambda g, c: (c, g)),
                  pl.BlockSpec((BLK, narrow_w), lambda g, c: (c, SSD_INNER // narrow_w + g)),
                  pl.BlockSpec((BLK, narrow_w), lambda g, c: (c, (SSD_INNER + SSD_GROUPS * BLK) // narrow_w + g)),
                  *z_specs,
                  pl.BlockSpec((gps, BLK, 8), lambda g, c: (g, c, 0)),
                  pl.BlockSpec((gps, 8, BLK), lambda g, c: (g, 0, c)),
                  gparam, gparam_t, gparam, gparam_t, gparam,
                  pl.BlockSpec((1, wide_w), lambda g, c: (0, g))],
        out_specs=[pl.BlockSpec((BLK, wide_w), lambda g, c: (c, g)),
                   pl.BlockSpec((BLK, wide_w), lambda g, c: (c, g)),
                   pl.BlockSpec((gps, None, 4, BLK, 2 * HD), lambda g, c: (g, c, 0, 0, 0))],
        scratch_shapes=[pltpu.VMEM((4 * gps, BLK, 2 * HD), F32)],
    )(xc, xc, xc, *([p] * gps), dtg, dtg_t, bias, bias_t, alog, alog_t, dskip, gn)


def ssd_bwd(p, xc, y, d_sin, hprev, dtg, dtg_t, params, gn, name):
    s = p.shape[0]
    nc = s // BLK
    bias, bias_t, alog, alog_t, dskip = params

    def body(xs_ref, b_ref, c_ref, z_ref, y_ref, dsin_ref, hp_ref, dtr_ref, dtrt_ref, bias_ref,
             biast_ref, alog_ref, alogt_ref, dsk_ref, gn_ref,
             dxs_ref, db_ref, dc_ref, dz_ref, ddt_ref, da_ref, dbias_ref, ddsk_ref, dgn_ref, dh_ref):
        c_idx = pl.program_id(1)

        @pl.when(c_idx == 0)
        def _():
            dh_ref[...] = jnp.zeros_like(dh_ref)
            da_ref[...] = jnp.zeros_like(da_ref)
            dbias_ref[...] = jnp.zeros_like(dbias_ref)
            ddsk_ref[...] = jnp.zeros_like(ddsk_ref)
            dgn_ref[...] = jnp.zeros_like(dgn_ref)

        dtb, dt, cs, cs_t, row, col, upper = _ssd_decays(dtr_ref, dtrt_ref, bias_ref, biast_ref, alog_ref, alogt_ref)
        first = _lane_first_half((BLK, 2 * HD))
        second = jnp.logical_not(first)
        first_row = _lane_first_half((1, 2 * HD))
        tril = row >= col
        triu = row <= col
        last_row = lax.broadcasted_iota(jnp.int32, (BLK, 1), 0) == BLK - 1
        lane8 = lax.broadcasted_iota(jnp.int32, (BLK, 8), 1)

        yv = y_ref[...]
        zv = z_ref[...].astype(F32)
        sg = _sigmoid(zv)
        yz = yv * (zv * sg)
        r = lax.rsqrt(jnp.mean(yz * yz, axis=-1, keepdims=True) + EPS)
        yzn = yz * r
        dsn = dsin_ref[...]
        dgn_ref[...] += jnp.sum(dsn * yzn, axis=0, keepdims=True)
        dsn = dsn * gn_ref[...]
        dyz = r * (dsn - yzn * jnp.mean(dsn * yzn, axis=-1, keepdims=True))
        dy = dyz * (zv * sg)
        dz_ref[...] = (dyz * yv * (sg * (1.0 + zv * (1.0 - sg)))).astype(BF16)
        xs_all = xs_ref[...].astype(F32)
        ddsk_ref[...] += jnp.sum(dy * xs_all, axis=0, keepdims=True)

        b16, c16 = b_ref[...], c_ref[...]
        cb = _dot_nt(c16, b16)
        cb_t = _dot_nt(b16, c16)
        n_pairs = GROUP_W // (2 * HD)
        tot = cs[BLK - 1:BLK, :]
        exp_cs, exp_rest, exp_tot = jnp.exp(cs), jnp.exp(tot - cs), jnp.exp(tot)

        def per_head(v, mask):
            return jnp.concatenate([jnp.where(mask, v[:, 2 * pr:2 * pr + 1], v[:, 2 * pr + 1:2 * pr + 2])
                                    for pr in range(n_pairs)], axis=1)

        def head_sums(v):
            out = jnp.zeros((BLK, 8), F32)
            for pr in range(n_pairs):
                sa, sb = _pair_sum(v[:, pr * 2 * HD:(pr + 1) * 2 * HD], first)
                out = jnp.where(lane8 == 2 * pr, sa, jnp.where(lane8 == 2 * pr + 1, sb, out))
            return out

        dt_w, e_w, f_w = per_head(dt, first), per_head(exp_cs, first), per_head(exp_rest, first)
        xt = xs_all * dt_w
        xt16 = xt.astype(BF16)
        hstate = jnp.concatenate([hp_ref[pr] for pr in range(n_pairs)], axis=1)
        h16 = hstate.astype(BF16)
        dhn = jnp.concatenate([dh_ref[pr] for pr in range(n_pairs)], axis=1)
        dhn16 = dhn.astype(BF16)
        edy16 = (e_w * dy).astype(BF16)
        y_off = e_w * _dot(c16, h16)
        dcs_all = head_sums(dy * y_off)
        dc_acc = _dot_nt(edy16, h16)
        zmat = _dot(b16, dhn16)
        t_all = head_sums(zmat * xt) * exp_rest
        hh_rows = jnp.sum(head_sums(dhn * hstate), axis=0, keepdims=True)
        dtot = jnp.sum(t_all, axis=0, keepdims=True) + hh_rows * exp_tot
        dcs_all = dcs_all - t_all + jnp.where(last_row, dtot, 0.0)
        fxt16 = (f_w * xt).astype(BF16)
        db_acc = _dot_nt(fxt16, dhn16)
        dh_new = _dot_tn(c16, edy16) + per_head(exp_tot, first_row) * dhn
        for pr in range(n_pairs):
            dh_ref[pr] = dh_new[:, pr * 2 * HD:(pr + 1) * 2 * HD]
        g_sum = jnp.zeros((BLK, BLK), F32)
        gt_sum = jnp.zeros((BLK, BLK), F32)
        d_xt_parts = []
        for pr in range(n_pairs):
            cols = slice(pr * 2 * HD, (pr + 1) * 2 * HD)
            dym2 = _stack_heads(dy[:, cols].astype(BF16), first)
            d_m2 = _dot_nt(dym2, xt16[:, cols])
            d_mt2 = _dot_nt(xt16[:, cols], dym2)
            mt2 = []
            for e, h in enumerate((2 * pr, 2 * pr + 1)):
                cs_c, cs_r = cs[:, h:h + 1], cs_t[h:h + 1, :]
                decay = jnp.exp(jnp.where(tril, cs_c - cs_r, NEG))
                decay_t = jnp.exp(jnp.where(triu, cs_r - cs_c, NEG))
                gm = d_m2[e * BLK:(e + 1) * BLK] * decay
                gmt = d_mt2[:, e * BLK:(e + 1) * BLK] * decay_t
                g_sum += gm
                gt_sum += gmt
                dcs_h = jnp.sum(gm * cb, axis=-1, keepdims=True) - jnp.sum(gmt * cb_t, axis=-1, keepdims=True)
                dcs_all = dcs_all + jnp.where(lane8 == h, dcs_h, 0.0)
                mt2.append((cb_t * decay_t).astype(BF16))
            d_xt_parts.append(_dot(jnp.concatenate(mt2, axis=1), dym2))
        d_xt = jnp.concatenate(d_xt_parts, axis=1) + f_w * zmat
        dxs_ref[...] = dy * per_head(dsk_ref[...], first_row) + d_xt * dt_w
        ddtx_all = head_sums(d_xt * xs_all)

        dc_ref[...] = dc_acc + _dot(g_sum.astype(BF16), b16)
        db_ref[...] = db_acc + _dot(gt_sum.astype(BF16), c16)
        d_a = _dot_hi(upper, dcs_all)
        a_neg = -jnp.exp(alog_ref[...])
        ddt = ddtx_all + d_a * a_neg
        da_ref[...] += jnp.sum(d_a * dt, axis=0, keepdims=True)
        ddtr = ddt * _sigmoid(dtb)
        ddt_ref[...] = ddtr
        dbias_ref[...] += jnp.sum(ddtr, axis=0, keepdims=True)

    gps = SSD_GROUPS_PER_STEP
    k_wide, k_narrow, k_lead = ("cols", GROUP_W), ("cols", BLK), ("lead", 1)
    kinds = ([k_wide, k_narrow, k_narrow, ("each", 0), k_wide, k_wide] + [k_lead] * 8 + [k_wide]
             + [k_wide, k_narrow, k_narrow, k_wide] + [k_lead] * 4 + [k_wide] + [("lead", 4)])
    wide_w, narrow_w = GROUP_W * gps, BLK * gps
    rc = lambda c: nc - 1 - c
    gparam = pl.BlockSpec((gps, 1, 8), lambda g, c: (g, 0, 0))
    gparam_t = pl.BlockSpec((gps, 8, 1), lambda g, c: (g, 0, 0))
    wide = pl.BlockSpec((BLK, wide_w), lambda g, c: (rc(c), g))
    narrow = pl.BlockSpec((BLK, narrow_w), lambda g, c: (rc(c), g))
    z_specs = [pl.BlockSpec((BLK, GROUP_W),
                            functools.partial(lambda g, c, gi: (rc(c), COL_Z // GROUP_W + gps * g + gi), gi=gi))
               for gi in range(gps)]
    return _call(
        _per_group(body, gps, kinds), name=name,
        out_shape=[_sds((s, SSD_INNER), F32), _sds((s, GROUP_W), F32), _sds((s, GROUP_W), F32),
                   _sds((s, SSD_INNER), BF16), _sds((SSD_GROUPS, s, 8), F32),
                   _sds((SSD_GROUPS, 1, 8), F32), _sds((SSD_GROUPS, 1, 8), F32),
                   _sds((SSD_GROUPS, 1, GROUP_W), F32), _sds((1, SSD_INNER), F32)],
        grid=(SSD_GROUPS // gps, nc),
        in_specs=[wide,
                  pl.BlockSpec((BLK, narrow_w), lambda g, c: (rc(c), SSD_INNER // narrow_w + g)),
                  pl.BlockSpec((BLK, narrow_w), lambda g, c: (rc(c), (SSD_INNER + SSD_GROUPS * BLK) // narrow_w + g)),
                  *z_specs,
                  wide, wide,
                  pl.BlockSpec((gps, None, 4, BLK, 2 * HD), lambda g, c: (g, rc(c), 0, 0, 0)),
                  pl.BlockSpec((gps, BLK, 8), lambda g, c: (g, rc(c), 0)),
                  pl.BlockSpec((gps, 8, BLK), lambda g, c: (g, 0, rc(c))),
                  gparam, gparam_t, gparam, gparam_t, gparam,
                  pl.BlockSpec((1, wide_w), lambda g, c: (0, g))],
        out_specs=[wide, narrow, narrow, wide,
                   pl.BlockSpec((gps, BLK, 8), lambda g, c: (g, rc(c), 0)),
                   gparam, gparam,
                   pl.BlockSpec((gps, 1, GROUP_W), lambda g, c: (g, 0, 0)),
                   pl.BlockSpec((1, wide_w), lambda g, c: (0, g))],
        scratch_shapes=[pltpu.VMEM((4 * gps, BLK, 2 * HD), F32)],
    )(xc, xc, xc, *([p] * gps), y, d_sin, hprev, dtg, dtg_t, bias, bias_t, alog, alog_t, dskip, gn)


def merge_fwd(p, a, sbr, name, tm=512):
    s = p.shape[0]
    nj = D_MODEL // GROUP_W

    def body(ga_ref, gs_ref, a_ref, s_ref, o_ref):
        o_ref[...] = (_sigmoid(ga_ref[...].astype(F32)) * a_ref[...]
                      + _sigmoid(gs_ref[...].astype(F32)) * s_ref[...]).astype(BF16)

    blk = pl.BlockSpec((tm, GROUP_W), lambda i, j: (i, j))
    return _call(body, name=name, out_shape=_sds((s, D_MODEL), BF16), grid=(s // tm, nj),
                 in_specs=[pl.BlockSpec((tm, GROUP_W), lambda i, j: (i, COL_GA // GROUP_W + j)),
                           pl.BlockSpec((tm, GROUP_W), lambda i, j: (i, COL_GS // GROUP_W + j)), blk, blk],
                 out_specs=blk)(p, p, a, sbr)


def merge_bwd(p, a, sbr, dmerged, name, tm=512):
    s = p.shape[0]
    nj = D_MODEL // GROUP_W

    def body(ga_ref, gs_ref, a_ref, s_ref, dm_ref, da_ref, ds_ref, dga_ref, dgs_ref):
        dm = dm_ref[...]
        sa = _sigmoid(ga_ref[...].astype(F32))
        ss = _sigmoid(gs_ref[...].astype(F32))
        da_ref[...] = (dm * sa).astype(BF16)
        ds_ref[...] = (dm * ss).astype(BF16)
        dga_ref[...] = (dm * a_ref[...] * sa * (1.0 - sa)).astype(BF16)
        dgs_ref[...] = (dm * s_ref[...] * ss * (1.0 - ss)).astype(BF16)

    blk = pl.BlockSpec((tm, GROUP_W), lambda i, j: (i, j))
    shp = _sds((s, D_MODEL), BF16)
    return _call(body, name=name, out_shape=[shp] * 4, grid=(s // tm, nj),
                 in_specs=[pl.BlockSpec((tm, GROUP_W), lambda i, j: (i, COL_GA // GROUP_W + j)),
                           pl.BlockSpec((tm, GROUP_W), lambda i, j: (i, COL_GS // GROUP_W + j)), blk, blk, blk],
                 out_specs=[blk] * 4)(p, p, a, sbr, dmerged)


def _group_major(v):
    return v.reshape(SSD_GROUPS, 1, 8), v.reshape(SSD_GROUPS, 8, 1)


def mixer_forward(x, w, rider=None, later_weights=None):
    s = x.shape[0]
    h = rms_fwd(x, w["mix_norm"], "mix_rms")
    p = matmul_nn(h, w["w_in_main"], "mix_proj", BF16, tm=1024, tn=512, rider=rider)
    rode = None
    if rider is not None:
        p, rode = p
        w = dict(w, **later_weights(rode))
    dt_raw = matmul_nn(h, w["w_in_dt"], "mix_proj_dt", F32, tm=1024, tn=DT_PAD)
    qn, kn = qk_norm_fwd(p, w["q_gain"], w["k_gain"], "qk_norm")
    streams, os_, lses = [], [], []
    for g, d in enumerate(ATTN_DILATIONS):
        cols = slice(g * GROUP_W, (g + 1) * GROUP_W)
        qs, ks = _to_streams(qn[:, cols], d), _to_streams(kn[:, cols], d)
        vs = _to_streams(p[:, COL_V + g * GROUP_W:COL_V + (g + 1) * GROUP_W], d)
        o, lse = attn_fwd2(qs, ks, vs, g, f"attn_fwd{g}")
        streams.append((qs, ks, vs, lse))
        os_.append(_from_streams(o, d))
        lses.append(_from_streams(lse, d))
    attn_o = attn_merge_fwd(os_, lses, "attn_merge")
    cpre, xc = conv_fwd(p, w["conv_w"], w["conv_b"], "conv_fwd")
    dtg = dt_raw[:, :SSD_HEADS].reshape(s, SSD_GROUPS, 8).transpose(1, 0, 2)
    dtg_t = dtg.transpose(0, 2, 1)
    params = (*_group_major(w["dt_bias"]), *_group_major(w["a_log"]), _group_major(w["d_skip"])[0])
    y, s_in, hprev = ssd_fwd(p, xc, dtg, dtg_t, params, w["ssd_norm"], "ssd_fwd")
    a = matmul_nn(attn_o, w["w_attn_branch"], "attn_branch", F32, tm=1024, tn=512)
    sbr = matmul_nn(s_in, w["w_ssd_branch"], "ssd_branch", F32, tm=1024, tn=512)
    merged = merge_fwd(p, a, sbr, "merge")
    x_out = matmul_nn(merged, w["w_out"], "mix_out", F32, tm=1024, tn=512, res=x)
    saved = dict(h=h, p=p, streams=streams, os=os_, lses=lses, attn_o=attn_o, cpre=cpre, xc=xc, dtg=dtg,
                 dtg_t=dtg_t, params=params, y=y, s_in=s_in, hprev=hprev, a=a, sbr=sbr, merged=merged, w=w)
    return x_out, saved, rode


def mixer_backward(dx_out, x, sv, ride_early=None, ride_late=None):
    s = x.shape[0]
    p = sv["p"]
    w = sv["w"]
    g = {}
    dmerged = matmul_nt(dx_out, w["w_out"], "d_merged", F32, tm=1024, tn=512, tk=1024)
    g["w_out"] = matmul_tn(sv["merged"], dx_out, "dw_out", tn=512, ts=1024)
    da, ds, dga, dgs = merge_bwd(p, sv["a"], sv["sbr"], dmerged, "merge_bwd")
    g["w_attn_branch"] = matmul_tn(sv["attn_o"], da, "dw_attn_branch", tn=512, ts=1024)
    g["w_ssd_branch"] = matmul_tn(sv["s_in"], ds, "dw_ssd_branch", tn=512, ts=1024)
    d_attn_o = matmul_nt(da, w["w_attn_branch"], "d_attn_o", F32, tm=1024, tn=512, tk=1024)
    d_sin = matmul_nt(ds, w["w_ssd_branch"], "d_ssd_in", F32, tm=1024, tn=512, tk=1024)
    dxs, d_b, d_c, dz, ddt, d_asum, d_bias, d_dsk, d_gn = ssd_bwd(
        p, sv["xc"], sv["y"], d_sin, sv["hprev"], sv["dtg"], sv["dtg_t"], sv["params"], w["ssd_norm"], "ssd_bwd")
    dxbc, d_convw, d_convb = conv_bwd(p, sv["cpre"], dxs, d_b, d_c, w["conv_w"], "conv_bwd")
    g["conv_w"] = d_convw[:SSD_CONV]
    g["conv_b"] = d_convb
    g["dt_bias"] = d_bias.reshape(1, SSD_HEADS)
    g["a_log"] = (d_asum * (-jnp.exp(sv["params"][2]))).reshape(1, SSD_HEADS)
    g["d_skip"] = jnp.sum(d_dsk.reshape(SSD_HEADS, HD), axis=1).reshape(1, SSD_HEADS)
    g["ssd_norm"] = d_gn
    merged_bwd = attn_merge_bwd(d_attn_o, sv["os"], sv["lses"], "attn_merge_bwd")
    dqs, dks, dvs = [], [], []
    for gi, d in enumerate(ATTN_DILATIONS):
        qs, ks, vs, lse = sv["streams"][gi]
        d_o = _to_streams(merged_bwd[gi], d)
        cterm = _to_streams(merged_bwd[3 + gi], d)
        dq, dk, dv = attn_bwd2(qs, ks, vs, d_o, cterm, lse, gi, f"attn_bwd{gi}")
        dqs.append(_from_streams(dq, d))
        dks.append(_from_streams(dk, d))
        dvs.append(_from_streams(dv, d).astype(BF16))
    dqk, d_qg, d_kg = qk_norm_bwd(p, dqs, dks, w["q_gain"], w["k_gain"], "qk_norm_bwd")
    g["q_norm"] = jnp.sum(d_qg.reshape(N_ATTN_HEADS, HD), axis=0).reshape(1, HD)
    g["k_norm"] = jnp.sum(d_kg.reshape(N_ATTN_HEADS, HD), axis=0).reshape(1, HD)
    dp = [dqk, jnp.concatenate(dvs, axis=1), dz, dxbc, dga, dgs]
    ddt_pad = jnp.pad(ddt.transpose(1, 0, 2).reshape(s, SSD_HEADS), ((0, 0), (0, DT_PAD - SSD_HEADS)))
    if ride_early is not None:
        g["w_in_main"], g["rode_early"] = matmul_tn_pieces(sv["h"], dp, "dw_in", tn=512, ts=1024,
                                                           rider=ride_early(g))
    else:
        g["w_in_main"] = matmul_tn_pieces(sv["h"], dp, "dw_in", tn=512, ts=1024)
    g["w_in_dt"] = matmul_tn(sv["h"], ddt_pad, "dw_in_dt", tn=DT_PAD, ts=1024)
    if ride_late is not None:
        dh_main, g["rode_late"] = matmul_nt_pieces(dp, w["w_in_main"], "d_mix_h", F32, tm=1024, tn=1024, tk=512,
                                                   rider=ride_late(g))
    else:
        dh_main = matmul_nt_pieces(dp, w["w_in_main"], "d_mix_h", F32, tm=1024, tn=512, tk=512)
    dh_dt = matmul_nt(ddt_pad, w["w_in_dt"], "d_mix_h_dt", F32, tm=1024, tn=1024, tk=DT_PAD)
    dx, g["mix_norm"] = rms_bwd([dh_main, dh_dt], x, w["mix_norm"], dx_out, "mix_drms")
    return dx, g


ANY = pl.BlockSpec(memory_space=pl.ANY)


def _place():
    x, y, c = lax.axis_index("x"), lax.axis_index("y"), lax.axis_index("c")
    chips = [(1 - x, y), (x, 1 - y), (1 - x, 1 - y)]
    return x, y, c, 2 * x + y, chips


def _comm_call(body, *, name, out_shape, n_in, scratch_shapes, aliases=None):
    return pl.pallas_call(
        body, out_shape=out_shape, in_specs=[ANY] * n_in, out_specs=[ANY] * len(out_shape),
        scratch_shapes=scratch_shapes, input_output_aliases=aliases or {}, name=name,
        compiler_params=pltpu.CompilerParams(has_side_effects=True))


def gather_weights(shards, small):
    n = len(shards)
    halves = [a.shape[0] // 2 for a in shards]
    out_shape = [_sds((N_CHIP,) + a.shape, a.dtype) for a in shards] + [_sds((N_CHIP,) + small.shape, small.dtype)]

    def body(*refs):
        ins, outs = refs[:n + 1], refs[n + 1:2 * n + 2]
        send1, recv1, send2, recv2, local = refs[2 * n + 2:]
        x, y, c, me, chips = _place()
        sibling = (x, y, 1 - c)

        def rows(k, chip, core):
            if k == n:
                return outs[k].at[chip]
            return outs[k].at[chip, pl.ds(core * halves[k], halves[k])]

        def level1(k, t, incoming):
            chip = 2 * chips[t][0] + chips[t][1]
            src = ins[k] if k == n else ins[k].at[pl.ds(c * halves[k], halves[k])]
            return pltpu.make_async_remote_copy(
                src_ref=src, dst_ref=rows(k, chip if incoming else me, c), send_sem=send1.at[3 * k + t],
                recv_sem=recv1.at[3 * k + t], device_id=(*chips[t], c), device_id_type=MESH)

        def level2(k, t, incoming):
            chip = 2 * chips[t][0] + chips[t][1]
            core = (1 - c) if incoming else c
            return pltpu.make_async_remote_copy(
                src_ref=rows(k, chip, core), dst_ref=rows(k, chip, core), send_sem=send2.at[3 * k + t],
                recv_sem=recv2.at[3 * k + t], device_id=sibling, device_id_type=MESH)

        own = [pltpu.make_async_copy(ins[k], outs[k].at[me], local.at[k]) for k in range(n + 1)]
        for cp in own:
            cp.start()
        first = [level1(k, t, False) for k in range(n + 1) for t in range(3)]
        for cp in first:
            cp.start()
        passed = []
        for k in range(n + 1):
            for t in range(3):
                level1(k, t, True).wait_recv()
                if k < n:
                    cp = level2(k, t, False)
                    cp.start()
                    passed.append(cp)
        for k in range(n):
            for t in range(3):
                level2(k, t, True).wait_recv()
        for cp in first + passed:
            cp.wait_send()
        for cp in own:
            cp.wait()

    dma = pltpu.SemaphoreType.DMA
    return _comm_call(body, name="gather_weights", out_shape=out_shape, n_in=n + 1,
                      scratch_shapes=[dma((3 * n + 3,)), dma((3 * n + 3,)), dma((3 * n,)), dma((3 * n,)),
                                      dma((n + 1,))])(*shards, small)


def reduce_to_sibling(grads):
    n = len(grads)
    halves = [a.shape[1] // 2 for a in grads]
    shapes = [_sds((N_CHIP, h, a.shape[2]), a.dtype) for a, h in zip(grads, halves)]

    def body(*refs):
        ins, got, kept = refs[:n], refs[n:2 * n], refs[2 * n:3 * n]
        send, recv, local = refs[3 * n:]
        x, y, c, _, _ = _place()
        copies, locals_ = [], []
        for k in range(n):
            h = halves[k]
            locals_.append(pltpu.make_async_copy(ins[k].at[:, pl.ds(c * h, h)], kept[k], local.at[k]))
            copies.append(pltpu.make_async_remote_copy(
                src_ref=ins[k].at[:, pl.ds((1 - c) * h, h)], dst_ref=got[k], send_sem=send.at[k], recv_sem=recv.at[k],
                device_id=(x, y, 1 - c), device_id_type=MESH))
        for cp in locals_ + copies:
            cp.start()
        for cp in copies:
            cp.wait_recv()
        for cp in copies:
            cp.wait_send()
        for cp in locals_:
            cp.wait()

    dma = pltpu.SemaphoreType.DMA
    res = _comm_call(body, name="reduce_to_sibling", out_shape=shapes + shapes, n_in=n,
                     scratch_shapes=[dma((n,)), dma((n,)), dma((n,))])(*grads)
    return res[:n], res[n:]


def reduce_to_owner(sums):
    n = len(sums)
    shapes = [_sds(a.shape, a.dtype) for a in sums]

    def body(*refs):
        ins, outs = refs[:n], refs[n:2 * n]
        send, recv, local = refs[2 * n:]
        x, y, c, me, chips = _place()
        copies, locals_ = [], []
        for k in range(n):
            locals_.append(pltpu.make_async_copy(ins[k].at[me], outs[k].at[3], local.at[k]))
            for t in range(3):
                chip = 2 * chips[t][0] + chips[t][1]
                copies.append(pltpu.make_async_remote_copy(
                    src_ref=ins[k].at[chip], dst_ref=outs[k].at[t], send_sem=send.at[3 * k + t],
                    recv_sem=recv.at[3 * k + t], device_id=(*chips[t], c), device_id_type=MESH))
        for cp in locals_ + copies:
            cp.start()
        for cp in copies:
            cp.wait_recv()
        for cp in copies:
            cp.wait_send()
        for cp in locals_:
            cp.wait()

    dma = pltpu.SemaphoreType.DMA
    return _comm_call(body, name="reduce_to_owner", out_shape=shapes, n_in=n,
                      scratch_shapes=[dma((3 * n,)), dma((3 * n,)), dma((n,))])(*sums)


def share_with_sibling(halves_):
    n = len(halves_)
    shapes = [_sds((2 * a.shape[0], a.shape[1]), a.dtype) for a in halves_]

    def body(*refs):
        ins, outs = refs[:n], refs[n:2 * n]
        send, recv, local = refs[2 * n:]
        x, y, c, _, _ = _place()
        copies, locals_ = [], []
        for k in range(n):
            h = ins[k].shape[0]
            mine = outs[k].at[pl.ds(c * h, h)]
            locals_.append(pltpu.make_async_copy(ins[k], mine, local.at[k]))
            copies.append(pltpu.make_async_remote_copy(
                src_ref=ins[k], dst_ref=mine, send_sem=send.at[k], recv_sem=recv.at[k],
                device_id=(x, y, 1 - c), device_id_type=MESH))
        for cp in locals_ + copies:
            cp.start()
        for cp in copies:
            cp.wait_recv()
        for cp in copies:
            cp.wait_send()
        for cp in locals_:
            cp.wait()

    dma = pltpu.SemaphoreType.DMA
    return _comm_call(body, name="share_with_sibling", out_shape=shapes, n_in=n,
                      scratch_shapes=[dma((n,)), dma((n,)), dma((n,))])(*halves_)


def _cores():
    c = lax.axis_index("c")
    return jnp.stack([c, 1 - c]).astype(jnp.int32)


def _staged_call(body, *, name, grid, in_specs, out_specs, out_shape, scratch_shapes):
    return pl.pallas_call(
        body, out_shape=out_shape, name=name,
        grid_spec=pltpu.PrefetchScalarGridSpec(num_scalar_prefetch=1, grid=grid, in_specs=in_specs,
                                               out_specs=out_specs, scratch_shapes=scratch_shapes),
        compiler_params=pltpu.CompilerParams(dimension_semantics=("arbitrary",) * len(grid),
                                             vmem_limit_bytes=V7X_VMEM_LIMIT, has_side_effects=True))


def gather_rider(shards, tiles):
    dma = pltpu.SemaphoreType.DMA
    n = len(shards)
    geo = [(a.shape[0] // 2, tm, (a.shape[0] // 2) // tm) for a, tm in zip(shards, tiles)]
    scratch = []
    for a, (h, tm, nk) in zip(shards, geo):
        scratch += [pltpu.VMEM((N_CHIP,) + a.shape, a.dtype), dma((3, nk)), dma((3, nk)), dma((3, nk)), dma((3, nk)),
                    dma((nk + 2,))]

    def copies(j, in_ref, scr):
        buf, send1, recv1, send2, recv2, local = scr[6 * j:6 * j + 6]
        h, tm, nk = geo[j]
        x, y, c, me, chips = _place()
        chip_of = [2 * chips[t][0] + chips[t][1] for t in range(3)]

        def rows(chip, core, k):
            return buf.at[chip, pl.ds(core * h + k * tm, tm)]

        def mine(k):
            if k == nk:
                return pltpu.make_async_copy(in_ref.at[pl.ds((1 - c) * h, h)], buf.at[me, pl.ds((1 - c) * h, h)],
                                             local.at[nk])
            return pltpu.make_async_copy(in_ref.at[pl.ds(c * h + k * tm, tm)], rows(me, c, k), local.at[k])

        def level1(t, k, incoming):
            place = rows(chip_of[t] if incoming else me, c, k)
            return pltpu.make_async_remote_copy(src_ref=place, dst_ref=place, send_sem=send1.at[t, k],
                                                recv_sem=recv1.at[t, k], device_id=(*chips[t], c), device_id_type=MESH)

        def level2(t, k, incoming):
            place = rows(chip_of[t], (1 - c) if incoming else c, k)
            return pltpu.make_async_remote_copy(src_ref=place, dst_ref=place, send_sem=send2.at[t, k],
                                                recv_sem=recv2.at[t, k], device_id=(x, y, 1 - c),
                                                device_id_type=MESH)

        return buf, local, nk, mine, level1, level2

    def start(ins, outs, scr):
        for j in range(n):
            _, _, nk, mine, _, _ = copies(j, ins[j], scr)
            for k in range(nk + 1):
                mine(k).start()
        for j in range(n):
            _, _, nk, mine, level1, _ = copies(j, ins[j], scr)
            for k in range(nk):
                mine(k).wait()
                for t in range(3):
                    level1(t, k, False).start()

    def finish(ins, outs, scr):
        for j in range(n):
            _, _, nk, _, level1, level2 = copies(j, ins[j], scr)
            for k in range(nk):
                for t in range(3):
                    level1(t, k, True).wait_recv()
                    level2(t, k, False).start()
        for j in range(n):
            buf, local, nk, mine, level1, level2 = copies(j, ins[j], scr)
            for k in range(nk):
                for t in range(3):
                    level2(t, k, True).wait_recv()
            for k in range(nk):
                for t in range(3):
                    level1(t, k, False).wait_send()
                    level2(t, k, False).wait_send()
            mine(nk).wait()
            pltpu.make_async_copy(buf, outs[j], local.at[nk + 1]).start()
        for j in range(n):
            buf, local, nk, _, _, _ = copies(j, ins[j], scr)
            pltpu.make_async_copy(buf, outs[j], local.at[nk + 1]).wait()

    return Rider(list(shards), [_sds((N_CHIP,) + a.shape, a.dtype) for a in shards], scratch, start, finish)


def run_alone(rider, name):
    return _call(lambda: None, name=name, out_shape=[], in_specs=[], out_specs=[], grid=(1,), rider=rider)()[1]


def sibling_sum(g, tm, name):
    _, r, cdim = g.shape
    h = r // 2
    ni = h // tm
    dma = pltpu.SemaphoreType.DMA

    def body(cores_ref, keep_ref, give_ref, out_ref, slot, send, recv):
        par = (pl.program_id(0) * ni + pl.program_id(1)) % 2
        x, y, c, _, _ = _place()
        cp = pltpu.make_async_remote_copy(src_ref=give_ref, dst_ref=slot.at[par], send_sem=send.at[par],
                                          recv_sem=recv.at[par], device_id=(x, y, 1 - c), device_id_type=MESH)
        cp.start()
        cp.wait_recv()
        out_ref[...] = (keep_ref[...].astype(F32) + slot[par].astype(F32)).astype(out_ref.dtype)
        cp.wait_send()

    flat = g.reshape(N_CHIP * r, cdim)
    return _staged_call(
        body, name=name, grid=(N_CHIP, ni),
        in_specs=[pl.BlockSpec((tm, cdim), lambda j, i, cores: ((2 * j + cores[0]) * ni + i, 0)),
                  pl.BlockSpec((tm, cdim), lambda j, i, cores: ((2 * j + cores[1]) * ni + i, 0))],
        out_specs=pl.BlockSpec((None, tm, cdim), lambda j, i, cores: (j, i, 0)),
        out_shape=_sds((N_CHIP, h, cdim), g.dtype),
        scratch_shapes=[pltpu.VMEM((2, tm, cdim), g.dtype), dma((2,)), dma((2,))],
    )(_cores(), flat, flat)


def owner_sum_rider(sums, tiles):
    dma = pltpu.SemaphoreType.DMA
    n = len(sums)
    geo = [(a.shape[1], tm, a.shape[1] // tm) for a, tm in zip(sums, tiles)]
    scratch = []
    for a, (h, tm, nk) in zip(sums, geo):
        cdim = a.shape[2]
        scratch += [pltpu.VMEM(a.shape, a.dtype), pltpu.VMEM((3, h, cdim), a.dtype), pltpu.VMEM((2, h, cdim), F32),
                    dma((3, nk)), dma((3, nk)), dma((nk,)), dma((nk,)), dma((2,))]

    def copies(j, scr):
        part, got, res, send, recv, send2, recv2, local = scr[8 * j:8 * j + 8]
        h, tm, nk = geo[j]
        x, y, c, me, chips = _place()

        def to_owner(t, k):
            chip = 2 * chips[t][0] + chips[t][1]
            return pltpu.make_async_remote_copy(
                src_ref=part.at[chip, pl.ds(k * tm, tm)], dst_ref=got.at[t, pl.ds(k * tm, tm)],
                send_sem=send.at[t, k], recv_sem=recv.at[t, k], device_id=(*chips[t], c), device_id_type=MESH)

        def to_sibling(k):
            place = res.at[c, pl.ds(k * tm, tm)]
            return pltpu.make_async_remote_copy(src_ref=place, dst_ref=place, send_sem=send2.at[k],
                                                recv_sem=recv2.at[k], device_id=(x, y, 1 - c), device_id_type=MESH)

        return part, got, res, local, to_owner, to_sibling, (tm, nk, c, me)

    def start(ins, outs, scr):
        for j in range(n):
            part, _, _, local, _, _, _ = copies(j, scr)
            pltpu.make_async_copy(ins[j], part, local.at[0]).start()
        for j in range(n):
            part, _, _, local, to_owner, _, (tm, nk, c, me) = copies(j, scr)
            pltpu.make_async_copy(ins[j], part, local.at[0]).wait()
            for k in range(nk):
                for t in range(3):
                    to_owner(t, k).start()

    def finish(ins, outs, scr):
        for j in range(n):
            part, got, res, _, to_owner, to_sibling, (tm, nk, c, me) = copies(j, scr)
            for k in range(nk):
                rows = pl.ds(k * tm, tm)
                for t in range(3):
                    to_owner(t, k).wait_recv()
                acc = part[me, rows, :].astype(F32)
                for t in range(3):
                    acc = acc + got[t, rows, :].astype(F32)
                res[c, rows, :] = acc
                to_sibling(k).start()
        for j in range(n):
            _, _, res, local, to_owner, to_sibling, (tm, nk, c, me) = copies(j, scr)
            for k in range(nk):
                to_sibling(k).wait_recv()
            for k in range(nk):
                to_sibling(k).wait_send()
                for t in range(3):
                    to_owner(t, k).wait_send()
            pltpu.make_async_copy(res, outs[j], local.at[1]).start()
        for j in range(n):
            _, _, res, local, _, _, _ = copies(j, scr)
            pltpu.make_async_copy(res, outs[j], local.at[1]).wait()

    return Rider(list(sums), [_sds((2, a.shape[1], a.shape[2]), F32) for a in sums], scratch, start, finish)


def gather_conv_w(w):
    def body(in_ref, out_ref, send, recv):
        x, y, c, me, chips = _place()
        out_ref[me] = in_ref[...]
        copies = []
        for t in range(3):
            copies.append(pltpu.make_async_remote_copy(
                src_ref=out_ref.at[me], dst_ref=out_ref.at[me], send_sem=send.at[t], recv_sem=recv.at[t],
                device_id=(*chips[t], c), device_id_type=MESH))
        for cp in copies:
            cp.start()
        for cp in copies:
            cp.wait_recv()
        for cp in copies:
            cp.wait_send()

    dma = pltpu.SemaphoreType.DMA
    vmem = pl.BlockSpec(memory_space=pltpu.VMEM)
    return pl.pallas_call(
        body, out_shape=_sds((N_CHIP,) + w.shape, w.dtype), in_specs=[vmem], out_specs=vmem, name="gather_conv_w",
        scratch_shapes=[dma((3,)), dma((3,))],
        compiler_params=pltpu.CompilerParams(has_side_effects=True))(w)


N_DEV = 8
SMALL_ROWS = 32
SMALL_LANES = 1024


def all_reduce_small(arrays):
    n_arr = len(arrays)
    places = []
    for k, a in enumerate(arrays):
        for ri in range(a.shape[0]):
            for c0 in range(0, a.shape[1], SMALL_LANES):
                places.append((k, ri, c0, min(SMALL_LANES, a.shape[1] - c0), len(places)))
    assert len(places) <= SMALL_ROWS

    def body(*refs):
        ins, outs = refs[:n_arr], refs[n_arr:2 * n_arr]
        buf, send, recv = refs[2 * n_arr:]
        x, y, c, _, _ = _place()
        me = 4 * x + 2 * y + c
        buf[me] = jnp.zeros((SMALL_ROWS, SMALL_LANES), F32)
        for k, ri, c0, width, row in places:
            buf[me, row:row + 1, 0:width] = ins[k][ri:ri + 1, c0:c0 + width]
        copies = []
        for r in range(1, N_DEV):
            px = (1 - x) if r & 4 else x
            py = (1 - y) if r & 2 else y
            pc = (1 - c) if r & 1 else c
            copies.append(pltpu.make_async_remote_copy(
                src_ref=buf.at[me], dst_ref=buf.at[me], send_sem=send.at[r - 1], recv_sem=recv.at[r - 1],
                device_id=(px, py, pc), device_id_type=MESH))
        for cp in copies:
            cp.start()
        for cp in copies:
            cp.wait_recv()
        for cp in copies:
            cp.wait_send()
        acc = buf[0]
        for j in range(1, N_DEV):
            acc = acc + buf[j]
        for k, ri, c0, width, row in places:
            outs[k][ri:ri + 1, c0:c0 + width] = acc[row:row + 1, 0:width]

    dma = pltpu.SemaphoreType.DMA
    vmem = pl.BlockSpec(memory_space=pltpu.VMEM)
    return pl.pallas_call(
        body, out_shape=[_sds(a.shape, F32) for a in arrays], in_specs=[vmem] * n_arr, out_specs=[vmem] * n_arr,
        name="all_reduce_small",
        scratch_shapes=[pltpu.VMEM((N_DEV, SMALL_ROWS, SMALL_LANES), F32), dma((N_DEV - 1,)), dma((N_DEV - 1,))],
        compiler_params=pltpu.CompilerParams(has_side_effects=True))(*arrays)


def _row_tile(rows, limit, multiple):
    return max(t for t in range(multiple, min(rows, limit) + 1, multiple) if rows % t == 0)


def add_pair(a, b, name):
    _, h, c = a.shape

    def body(a_ref, b_ref, o_ref):
        o_ref[...] = (a_ref[...].astype(F32) + b_ref[...].astype(F32)).astype(o_ref.dtype)

    blk = pl.BlockSpec((None, h, c), lambda j: (j, 0, 0))
    return _call(body, name=name, out_shape=_sds(a.shape, a.dtype), grid=(N_CHIP,), in_specs=[blk, blk],
                 out_specs=blk)(a, b)


def sum_slots(buf, name):
    _, h, c = buf.shape
    tm = _row_tile(h, 256, 16)

    def body(b_ref, o_ref):
        acc = b_ref[3].astype(F32)
        for t in range(3):
            acc = acc + b_ref[t].astype(F32)
        o_ref[...] = acc

    return _call(body, name=name, out_shape=_sds((h, c), F32), grid=(h // tm,),
                 in_specs=[pl.BlockSpec((N_CHIP, tm, c), lambda i: (0, i, 0))],
                 out_specs=pl.BlockSpec((tm, c), lambda i: (i, 0)))(buf)


def _adamw_math(w, g, m, v):
    c1 = 1.0 - ADAM_B1 ** ADAM_STEP
    c2 = 1.0 - ADAM_B2 ** ADAM_STEP
    m2 = ADAM_B1 * m + (1.0 - ADAM_B1) * g
    v2 = ADAM_B2 * v + (1.0 - ADAM_B2) * (g * g)
    return -ADAM_LR * ((m2 / c1) / (jnp.sqrt(v2 / c2) + ADAM_EPS) + ADAM_WD * w), m2, v2


def adamw(w, g, row_off, m, v, name):
    _, r, c = w.shape
    tm = r if r < 8 else _row_tile(math.gcd(r, row_off) if row_off else r, 128, 8)

    def body(w_ref, g_ref, m_ref, v_ref, go_ref, d_ref, m2_ref, v2_ref):
        gv = g_ref[...]
        go_ref[...] = gv
        d_ref[...], m2_ref[...], v2_ref[...] = _adamw_math(w_ref[...], gv, m_ref[...], v_ref[...])

    blk = pl.BlockSpec((None, tm, c), lambda i: (0, i, 0))
    shp = _sds((1, r, c), F32)
    return _call(body, name=name, out_shape=[shp] * 4, grid=(r // tm,),
                 in_specs=[blk, pl.BlockSpec((tm, c), lambda i: (row_off // tm + i, 0)), blk, blk],
                 out_specs=[blk] * 4)(w, g, m, v)


def adamw_small(ws, gs, ms, vs):
    n = len(ws)

    def body(*refs):
        ins, outs = refs[:4 * n], refs[4 * n:]
        for k in range(n):
            w_ref, g_ref, m_ref, v_ref = (ins[j * n + k] for j in range(4))
            outs[k][...], outs[n + k][...], outs[2 * n + k][...] = _adamw_math(w_ref[...], g_ref[...], m_ref[...],
                                                                               v_ref[...])

    vmem = pl.BlockSpec(memory_space=pltpu.VMEM)
    shapes = [_sds(w.shape, F32) for w in ws] * 3
    res = pl.pallas_call(body, out_shape=shapes, in_specs=[vmem] * (4 * n), out_specs=[vmem] * (3 * n),
                         name="adamw_small")(*ws, *gs, *ms, *vs)
    return res[:n], res[n:2 * n], res[2 * n:]


BIG = ("ffn1_w_gate", "ffn1_w_up", "ffn1_w_down", "w_in", "w_attn_branch", "w_ssd_branch", "w_out",
       "ffn2_w_gate", "ffn2_w_up", "ffn2_w_down")
SMALL = ("ffn1_norm", "mix_norm", "q_norm", "k_norm", "conv_b", "dt_bias", "a_log", "d_skip", "ssd_norm", "ffn2_norm")
WEIGHTS = ("ffn1_norm", "ffn1_w_gate", "ffn1_w_up", "ffn1_w_down", "mix_norm", "w_in", "q_norm", "k_norm", "conv_w",
           "conv_b", "dt_bias", "a_log", "d_skip", "ssd_norm", "w_attn_branch", "w_ssd_branch", "w_out", "ffn2_norm",
           "ffn2_w_gate", "ffn2_w_up", "ffn2_w_down")
CONV_SHARD = SSD_CONV_DIM // N_CHIP
CLASSES = {
    "ffn1_in": (("ffn1_w_gate", 1024), ("ffn1_w_up", 1024)),
    "ffn1_out": (("ffn1_w_down", 704),),
    "mix_in": (("w_in", 1024),),
    "mix_attn": (("w_attn_branch", 512),),
    "late_out": (("ffn2_w_down", 704), ("w_ssd_branch", 512), ("w_out", 256)),
    "ffn2_in": (("ffn2_w_gate", 1024), ("ffn2_w_up", 1024)),
}
CLASS_TILE = {"ffn1_in": 256, "ffn1_out": 176, "mix_in": 128, "mix_attn": 256, "late_out": 368, "ffn2_in": 256,
              "mix_in_top": 128, "mix_in_bottom": 128}
SIBLING_TILE = {"ffn1_in": 1024, "ffn1_out": 352, "mix_attn": 256, "late_out": 736, "ffn2_in": 1024,
                "mix_in_top": 256, "mix_in_bottom": 256}


def _pack_small(vals, conv_part, loss_part=None):
    flat = [vals[k].reshape(-1) for k in SMALL]
    flat.append(jnp.zeros((SSD_CONV * SSD_CONV_DIM,), F32) if conv_part is None else conv_part.reshape(-1))
    flat.append(jnp.zeros((1,), F32) if loss_part is None else loss_part.reshape(1))
    flat = jnp.concatenate(flat)
    return jnp.pad(flat, (0, SMALL_ROWS * D_MODEL - flat.shape[0])).reshape(SMALL_ROWS, D_MODEL)


def _unpack_small(pack, like):
    flat = pack.reshape(-1)
    out, off = {}, 0
    for k in SMALL:
        n = like[k].size
        out[k] = flat[off:off + n].reshape(like[k].shape)
        off += n
    conv = flat[off:off + SSD_CONV * SSD_CONV_DIM].reshape(SSD_CONV, SSD_CONV_DIM)
    return out, conv, flat[off + SSD_CONV * SSD_CONV_DIM]


def _chip_major_cols(a):
    r = a.shape[0]
    return a.reshape(r, N_CHIP, -1).transpose(1, 0, 2)


def _from_chip_major_cols(a):
    return a.transpose(1, 0, 2).reshape(a.shape[1], -1)


def kernel(x, ffn1_norm, ffn1_w_gate, ffn1_w_up, ffn1_w_down, mix_norm, w_in, q_norm, k_norm, conv_w, conv_b, dt_bias, a_log, d_skip, ssd_norm, w_attn_branch, w_ssd_branch, w_out, ffn2_norm, ffn2_w_gate, ffn2_w_up, ffn2_w_down, loss_target, m_ffn1_norm, m_ffn1_w_gate, m_ffn1_w_up, m_ffn1_w_down, m_mix_norm, m_w_in, m_q_norm, m_k_norm, m_conv_w, m_conv_b, m_dt_bias, m_a_log, m_d_skip, m_ssd_norm, m_w_attn_branch, m_w_ssd_branch, m_w_out, m_ffn2_norm, m_ffn2_w_gate, m_ffn2_w_up, m_ffn2_w_down, v_ffn1_norm, v_ffn1_w_gate, v_ffn1_w_up, v_ffn1_w_down, v_mix_norm, v_w_in, v_q_norm, v_k_norm, v_conv_w, v_conv_b, v_dt_bias, v_a_log, v_d_skip, v_ssd_norm, v_w_attn_branch, v_w_ssd_branch, v_w_out, v_ffn2_norm, v_ffn2_w_gate, v_ffn2_w_up, v_ffn2_w_down):
    env = dict(locals())
    wts = {k: env[k] for k in WEIGHTS}
    moms = {k: env["m_" + k] for k in WEIGHTS}
    vars_ = {k: env["v_" + k] for k in WEIGHTS}
    x0 = x[0]
    target = loss_target[0]

    def gather(classes):
        shards = [jnp.concatenate([wts[k][0] for k, _ in CLASSES[c]], axis=0).astype(BF16) for c in classes]
        return gather_rider(shards, [CLASS_TILE[c] for c in classes])

    def reducer(classes, parts):
        sums = [sibling_sum(p, SIBLING_TILE[c], f"sibling_sum_{c}") for c, p in zip(classes, parts)]
        return owner_sum_rider(sums, [CLASS_TILE[c] for c in classes])

    x1, saved1, (w_ffn1_in,), (w_ffn1_out, w_mix_in, w_mix_attn) = ffn_forward(
        x0, ffn1_norm, lambda rode: rode[0], lambda rode: rode[0], "ffn1", rms_rider=gather(["ffn1_in"]),
        up_rider=gather(["ffn1_out", "mix_in", "mix_attn"]))
    dt0, dt1 = IN_DT0 - 3 * IN_SHARD, IN_DT1 - 3 * IN_SHARD
    mixer_w = dict(
        mix_norm=mix_norm,
        w_in_main=jnp.concatenate([w_mix_in[0], w_mix_in[1], w_mix_in[2], w_mix_in[3][:, :dt0], w_mix_in[3][:, dt1:]],
                                  axis=1),
        w_in_dt=jnp.pad(w_mix_in[3][:, dt0:dt1], ((0, 0), (0, DT_PAD - SSD_HEADS))),
        q_gain=jnp.tile(q_norm, (1, 2)), k_gain=jnp.tile(k_norm, (1, 2)),
        conv_w=_from_chip_major_cols(gather_conv_w(conv_w[0])), conv_b=conv_b, dt_bias=dt_bias, a_log=a_log,
        d_skip=d_skip, ssd_norm=ssd_norm, w_attn_branch=_from_chip_major_cols(w_mix_attn))

    def later_weights(rode):
        late = rode[0]
        return dict(w_ssd_branch=late[:, 704:1216].reshape(SSD_INNER, D_MODEL),
                    w_out=late[:, 1216:1472].reshape(D_MODEL, D_MODEL))

    x2, saved_mix, (w_late_out, w_ffn2_in) = mixer_forward(x1, mixer_w, gather(["late_out", "ffn2_in"]), later_weights)
    x3, saved2, _, _ = ffn_forward(x2, ffn2_norm, lambda rode: w_ffn2_in, lambda rode: w_late_out, "ffn2")
    dx3, sq = loss_grad(x3, target, "loss")

    grads = {}
    dx2, grads["ffn2_norm"], d_ffn2_in, d_ffn2_down = ffn_backward(dx3, x2, ffn2_norm, w_ffn2_in, w_late_out, saved2,
                                                                   "ffn2")

    def ride_early(g):
        late = jnp.concatenate([d_ffn2_down, g["w_ssd_branch"].reshape(N_CHIP, -1, D_MODEL),
                                g["w_out"].reshape(N_CHIP, -1, D_MODEL)], axis=1)
        return reducer(["ffn2_in", "late_out"], [d_ffn2_in, late])

    g_in_rows = {}

    def ride_late(g):
        main = g["w_in_main"]
        last = jnp.concatenate([main[:, 3 * IN_SHARD:IN_DT0], g["w_in_dt"][:, :SSD_HEADS], main[:, IN_DT0:]], axis=1)
        for part, rows in (("top", slice(0, D_MODEL // 2)), ("bottom", slice(D_MODEL // 2, D_MODEL))):
            g_in_rows[part] = jnp.stack([main[rows, j * IN_SHARD:(j + 1) * IN_SHARD] for j in range(3)]
                                        + [last[rows]])
        return reducer(["mix_in_top", "mix_attn"], [g_in_rows["top"], _chip_major_cols(g["w_attn_branch"])])

    dx1, gmix = mixer_backward(dx2, x1, saved_mix, ride_early, ride_late)
    dx0, grads["ffn1_norm"], rode_in, rode_out = ffn_backward(
        dx1, x0, ffn1_norm, w_ffn1_in, w_ffn1_out, saved1, "ffn1",
        ride_down=lambda d: reducer(["mix_in_bottom", "ffn1_out"], [g_in_rows["bottom"], d]),
        ride_in=lambda d: reducer(["ffn1_in"], [d]))
    for k in ("mix_norm", "q_norm", "k_norm", "conv_b", "dt_bias", "a_log", "d_skip", "ssd_norm"):
        grads[k] = gmix[k]
    reduced = dict(zip(("ffn2_in", "late_out", "mix_in_top", "mix_attn", "ffn1_in", "mix_in_bottom", "ffn1_out"),
                       (*gmix["rode_early"], *gmix["rode_late"], rode_in[0], *rode_out)))
    reduced = {c: r.reshape(-1, r.shape[2]) for c, r in reduced.items()}
    reduced["mix_in"] = jnp.concatenate([reduced.pop("mix_in_top"), reduced.pop("mix_in_bottom")], axis=0)
    summed = all_reduce_small([grads[k] for k in SMALL]
                              + [gmix["conv_w"], (0.5 * jnp.sum(sq) / D_MODEL).reshape(1, 1)])
    g_small = dict(zip(SMALL, summed))
    loss = summed[-1].reshape(())
    chip = 2 * lax.axis_index("x") + lax.axis_index("y")
    g_conv = lax.dynamic_slice_in_dim(summed[-2], chip * CONV_SHARD, CONV_SHARD, axis=1)

    g_final, delta, new_m, new_v = dict(g_small), {}, {}, {}

    def update(k, g_arr, row_off):
        w, m, v = wts[k], moms[k], vars_[k]
        rows, cols = w.shape[1:]
        if cols % 128:
            res = adamw(jnp.swapaxes(w, 1, 2), g_arr[row_off:row_off + rows].T, 0, jnp.swapaxes(m, 1, 2),
                        jnp.swapaxes(v, 1, 2), f"adamw_{k}")
            res = [jnp.swapaxes(r, 1, 2) for r in res]
        else:
            res = adamw(w, g_arr, row_off, m, v, f"adamw_{k}")
        g_final[k], delta[k], new_m[k], new_v[k] = res

    for cls, members in CLASSES.items():
        off = 0
        for k, rows in members:
            update(k, reduced[cls], off)
            off += rows
    update("conv_w", g_conv, 0)
    small = adamw_small(*([d[k] for k in SMALL] for d in (wts, g_small, moms, vars_)))
    for res, vals in zip((delta, new_m, new_v), small):
        res.update(zip(SMALL, vals))

    return (loss, dx0[None], *[g_final[k] for k in WEIGHTS], *[delta[k] for k in WEIGHTS],
            *[new_m[k] for k in WEIGHTS], *[new_v[k] for k in WEIGHTS])
```

```python
import collections
import functools
import math

import jax
import jax.numpy as jnp
from jax import lax
from jax.experimental import pallas as pl
from jax.experimental.pallas import tpu as pltpu

F32 = jnp.float32
BF16 = jnp.bfloat16
MESH = pl.DeviceIdType.MESH

EPS = 1e-6
D_MODEL = 1024
D_FF = 2816
N_CHIP = 4
FF_SHARD = D_FF // N_CHIP
HD = 64
BLK = 128
ATTN_DILATIONS = (1, 4, 16)
HEADS_PER_PATTERN = 8
N_ATTN_HEADS = 24
ALIBI_MAX_EXP = 8.0
ATTN_QKV = 1536
GROUP_W = 512
SSD_INNER = 2048
SSD_HEADS = 32
SSD_GROUPS = 4
SSD_CONV = 4
SSD_CONV_DIM = 3072
IN_COLS = 11808
IN_DT0, IN_DT1 = 9728, 9760
IN_SHARD = IN_COLS // 4
COL_K, COL_V, COL_Z, COL_XBC, COL_GA, COL_GS, P_COLS = 1536, 3072, 4608, 6656, 9728, 10752, 11776
DT_PAD = 128

ADAM_LR, ADAM_B1, ADAM_B2, ADAM_EPS, ADAM_WD, ADAM_STEP = 0.001, 0.9, 0.999, 1e-08, 0.01, 10

V7X_VMEM_LIMIT = 56 * 1024 * 1024
NEG = -1e30


Rider = collections.namedtuple("Rider", "arrays out_shape scratch start finish")
Rider.__doc__ = """An exchange between devices that rides in a compute kernel: its copies are started in the host's
first grid step and waited for in its last, so they travel while the host computes.  arrays / out_shape: extra HBM
operands and results; scratch: extra scratch; start, finish: f(in_refs, out_refs, scratch_refs)."""


def _call(body, *, name, out_shape, in_specs, out_specs, grid=(), scratch_shapes=(), aliases=None, rider=None):
    params = dict(dimension_semantics=("arbitrary",) * len(grid), vmem_limit_bytes=V7X_VMEM_LIMIT)
    if rider is None:
        return pl.pallas_call(
            body, out_shape=out_shape, grid=grid, in_specs=in_specs, out_specs=out_specs,
            scratch_shapes=scratch_shapes, input_output_aliases=aliases or {}, name=name,
            compiler_params=pltpu.CompilerParams(**params))
    single = not isinstance(out_shape, (list, tuple))
    main_out = [out_shape] if single else list(out_shape)
    main_specs = [out_specs] if single else list(out_specs)
    n_in, n_out, n_scr = len(in_specs), len(main_out), len(scratch_shapes)
    r_in, r_out = len(rider.arrays), len(rider.out_shape)

    def wrapped(*refs):
        ins, refs = refs[:n_in], refs[n_in:]
        r_ins, refs = refs[:r_in], refs[r_in:]
        outs, refs = refs[:n_out], refs[n_out:]
        r_outs, refs = refs[:r_out], refs[r_out:]
        scr, r_scr = refs[:n_scr], refs[n_scr:]
        first = last = None
        for axis, size in enumerate(grid):
            at_start, at_end = pl.program_id(axis) == 0, pl.program_id(axis) == size - 1
            first = at_start if first is None else jnp.logical_and(first, at_start)
            last = at_end if last is None else jnp.logical_and(last, at_end)

        @pl.when(first)
        def _():
            rider.start(r_ins, r_outs, r_scr)

        body(*ins, *outs, *scr)

        @pl.when(last)
        def _():
            rider.finish(r_ins, r_outs, r_scr)

    hbm = pl.BlockSpec(memory_space=pl.ANY)
    call = pl.pallas_call(
        wrapped, out_shape=main_out + list(rider.out_shape), grid=grid, in_specs=list(in_specs) + [hbm] * r_in,
        out_specs=main_specs + [hbm] * r_out, scratch_shapes=list(scratch_shapes) + list(rider.scratch), name=name,
        compiler_params=pltpu.CompilerParams(has_side_effects=True, **params))

    def run(*args):
        res = call(*args, *rider.arrays)
        main = res[:n_out]
        return (main[0] if single else main), res[n_out:]

    return run


def _sds(shape, dtype):
    return jax.ShapeDtypeStruct(tuple(shape), dtype)


def _dot(a, b):
    return jnp.dot(a, b, preferred_element_type=F32)


def _dot_nt(a, b):
    return lax.dot_general(a, b, (((1,), (1,)), ((), ())), preferred_element_type=F32)


def _dot_tn(a, b):
    return lax.dot_general(a, b, (((0,), (0,)), ((), ())), preferred_element_type=F32)


def _dot_hi(a, b):
    return jnp.dot(a, b, preferred_element_type=F32, precision=lax.Precision.HIGHEST)


def _sigmoid(x):
    return pl.reciprocal(1.0 + jnp.exp(-x), approx=True)


def _lane_first_half(shape):
    return lax.broadcasted_iota(jnp.int32, shape, len(shape) - 1) < HD


def _pair_sum(x, first):
    s_all = jnp.sum(x, axis=-1, keepdims=True)
    s_a = jnp.sum(jnp.where(first, x, 0.0), axis=-1, keepdims=True)
    return s_a, s_all - s_a


def _rowwise(name, fn, rows, consts, outs, accs=(), tm=512, rider=None):
    n_rows = None
    in_arrays, in_specs = [], []
    for r in rows:
        if isinstance(r, tuple):
            arr, w, cb = r
            spec = pl.BlockSpec((tm, w), functools.partial(lambda i, cb: (i, cb), cb=cb))
        else:
            arr = r
            spec = pl.BlockSpec((tm, arr.shape[1]), lambda i: (i, 0))
        n_rows = arr.shape[0]
        in_arrays.append(arr)
        in_specs.append(spec)
    for c in consts:
        in_arrays.append(c)
        in_specs.append(pl.BlockSpec(c.shape, functools.partial(lambda i, n: (0,) * n, n=c.ndim)))
    out_shape = [_sds(s, d) for s, d in outs] + [_sds(s, d) for s, d in accs]
    out_specs = [pl.BlockSpec((tm, s[1]), lambda i: (i, 0)) for s, _ in outs]
    out_specs += [pl.BlockSpec(s, functools.partial(lambda i, n: (0,) * n, n=len(s))) for s, _ in accs]

    def body(*refs):
        fn(pl.program_id(0), *refs)

    res = _call(body, name=name, out_shape=out_shape, grid=(n_rows // tm,), in_specs=in_specs,
                out_specs=out_specs, rider=rider)(*in_arrays)
    return res


def rms_fwd(x, gain, name, rider=None):
    def fn(i, x_ref, g_ref, h_ref):
        xv = x_ref[...]
        r = lax.rsqrt(jnp.mean(xv * xv, axis=-1, keepdims=True) + EPS)
        h_ref[...] = (xv * r * g_ref[...]).astype(h_ref.dtype)

    res = _rowwise(name, fn, [x], [gain], [(x.shape, BF16)], rider=rider)
    return res[0] if rider is None else (res[0][0], res[1])


def rms_bwd(dhs, x, gain, dx_in, name):
    n = len(dhs)

    def fn(i, *refs):
        dh_refs, (x_ref, dxin_ref, g_ref, dx_ref, dg_ref) = refs[:n], refs[n:]
        dh = dh_refs[0][...]
        for r in dh_refs[1:]:
            dh = dh + r[...]
        xv = x_ref[...]
        r = lax.rsqrt(jnp.mean(xv * xv, axis=-1, keepdims=True) + EPS)
        xn = xv * r
        dxn = dh * g_ref[...]
        dx_ref[...] = dxin_ref[...] + r * (dxn - xn * jnp.mean(dxn * xn, axis=-1, keepdims=True))

        @pl.when(i == 0)
        def _():
            dg_ref[...] = jnp.zeros_like(dg_ref)

        dg_ref[...] += jnp.sum(dh * xn, axis=0, keepdims=True)

    return _rowwise(name, fn, list(dhs) + [x, dx_in], [gain], [(x.shape, F32)], [((1, x.shape[1]), F32)])


def loss_grad(y, target, name):
    def fn(i, y_ref, t_ref, dy_ref, sq_ref):
        err = y_ref[...] - t_ref[...]
        dy_ref[...] = err * (1.0 / y_ref.shape[1])

        @pl.when(i == 0)
        def _():
            sq_ref[...] = jnp.zeros_like(sq_ref)

        sq_ref[...] += jnp.sum(err * err, axis=0, keepdims=True)

    return _rowwise(name, fn, [y, target], [], [(y.shape, F32)], [((1, y.shape[1]), F32)])


def matmul_nn(a, b, name, out_dtype, tm, tn, res=None, scale=1.0, rider=None):
    s, k = a.shape
    n = b.shape[1]

    def body(*refs):
        if res is None:
            a_ref, b_ref, o_ref = refs
            o_ref[...] = _dot(a_ref[...], b_ref[...]).astype(o_ref.dtype)
        else:
            a_ref, b_ref, r_ref, o_ref = refs
            o_ref[...] = (r_ref[...] + scale * _dot(a_ref[...], b_ref[...])).astype(o_ref.dtype)

    in_specs = [pl.BlockSpec((tm, k), lambda i, j: (i, 0)), pl.BlockSpec((k, tn), lambda i, j: (0, j))]
    args = [a, b]
    if res is not None:
        in_specs.append(pl.BlockSpec((tm, tn), lambda i, j: (i, j)))
        args.append(res)
    return _call(body, name=name, out_shape=_sds((s, n), out_dtype), grid=(s // tm, n // tn), in_specs=in_specs,
                 out_specs=pl.BlockSpec((tm, tn), lambda i, j: (i, j)), rider=rider)(*args)


def matmul_nt(a, b, name, out_dtype, tm, tn, tk, rider=None):
    s, k = a.shape
    n = b.shape[0]
    nk = k // tk

    def body(a_ref, b_ref, o_ref, acc_ref):
        kk = pl.program_id(2)

        @pl.when(kk == 0)
        def _():
            acc_ref[...] = jnp.zeros_like(acc_ref)

        acc_ref[...] += _dot_nt(a_ref[...].astype(BF16), b_ref[...])

        @pl.when(kk == nk - 1)
        def _():
            o_ref[...] = acc_ref[...].astype(o_ref.dtype)

    return _call(body, name=name, out_shape=_sds((s, n), out_dtype), grid=(s // tm, n // tn, nk),
                 in_specs=[pl.BlockSpec((tm, tk), lambda i, j, kk: (i, kk)),
                           pl.BlockSpec((tn, tk), lambda i, j, kk: (j, kk))],
                 out_specs=pl.BlockSpec((tm, tn), lambda i, j, kk: (i, j)),
                 scratch_shapes=[pltpu.VMEM((tm, tn), F32)], rider=rider)(a, b)


def matmul_tn(a, b, name, tn, ts, a_scale=None, b_scale=None, rider=None):
    s, m = a.shape
    n = b.shape[1]
    ns = s // ts

    def body(a_ref, b_ref, o_ref, acc_ref):
        ss = pl.program_id(1)

        @pl.when(ss == 0)
        def _():
            acc_ref[...] = jnp.zeros_like(acc_ref)

        av, bv = a_ref[...], b_ref[...]
        if a_scale is not None:
            av = av * a_scale
        if b_scale is not None:
            bv = bv * b_scale
        acc_ref[...] += _dot_tn(av.astype(BF16), bv.astype(BF16))

        @pl.when(ss == ns - 1)
        def _():
            o_ref[...] = acc_ref[...].astype(o_ref.dtype)

    return _call(body, name=name, out_shape=_sds((m, n), BF16), grid=(n // tn, ns),
                 in_specs=[pl.BlockSpec((ts, m), lambda j, ss: (ss, 0)), pl.BlockSpec((ts, tn), lambda j, ss: (ss, j))],
                 out_specs=pl.BlockSpec((m, tn), lambda j, ss: (0, j)),
                 scratch_shapes=[pltpu.VMEM((m, tn), F32)], rider=rider)(a, b)


def _piece_specs(pieces, tile, rows_tile, tile_axis_first):
    specs, ranges, t0 = [], [], 0
    for a in pieces:
        n = a.shape[1] // tile

        def index(*ids, t0=t0, n=n):
            t, r = (ids[0], ids[1]) if tile_axis_first else (ids[2], ids[0])
            on = jnp.logical_and(t >= t0, t < t0 + n)
            return jnp.where(on, r, 0), jnp.clip(t - t0, 0, n - 1)

        specs.append(pl.BlockSpec((rows_tile, tile), index))
        ranges.append((t0, n))
        t0 += n
    return specs, ranges


def matmul_tn_pieces(a, pieces, name, tn, ts, rider=None):
    s, m = a.shape
    ns = s // ts
    specs, ranges = _piece_specs(pieces, tn, ts, True)
    n_total = sum(n for _, n in ranges)

    def body(a_ref, *refs):
        b_refs, o_ref, acc_ref = refs[:len(pieces)], refs[-2], refs[-1]
        j, ss = pl.program_id(0), pl.program_id(1)

        @pl.when(ss == 0)
        def _():
            acc_ref[...] = jnp.zeros_like(acc_ref)

        for b_ref, (t0, n) in zip(b_refs, ranges):
            @pl.when(jnp.logical_and(j >= t0, j < t0 + n))
            def _(b_ref=b_ref):
                acc_ref[...] += _dot_tn(a_ref[...], b_ref[...])

        @pl.when(ss == ns - 1)
        def _():
            o_ref[...] = acc_ref[...].astype(o_ref.dtype)

    return _call(body, name=name, out_shape=_sds((m, n_total * tn), BF16), grid=(n_total, ns),
                 in_specs=[pl.BlockSpec((ts, m), lambda j, ss: (ss, 0))] + specs,
                 out_specs=pl.BlockSpec((m, tn), lambda j, ss: (0, j)),
                 scratch_shapes=[pltpu.VMEM((m, tn), F32)], rider=rider)(a, *pieces)


def matmul_nt_pieces(pieces, b, name, out_dtype, tm, tn, tk, rider=None):
    s = pieces[0].shape[0]
    n = b.shape[0]
    specs, ranges = _piece_specs(pieces, tk, tm, False)
    nk = sum(cnt for _, cnt in ranges)

    def body(*refs):
        a_refs, b_ref, o_ref, acc_ref = refs[:len(pieces)], refs[-3], refs[-2], refs[-1]
        kk = pl.program_id(2)

        @pl.when(kk == 0)
        def _():
            acc_ref[...] = jnp.zeros_like(acc_ref)

        for a_ref, (t0, cnt) in zip(a_refs, ranges):
            @pl.when(jnp.logical_and(kk >= t0, kk < t0 + cnt))
            def _(a_ref=a_ref):
                acc_ref[...] += _dot_nt(a_ref[...], b_ref[...])

        @pl.when(kk == nk - 1)
        def _():
            o_ref[...] = acc_ref[...].astype(o_ref.dtype)

    return _call(body, name=name, out_shape=_sds((s, n), out_dtype), grid=(s // tm, n // tn, nk),
                 in_specs=specs + [pl.BlockSpec((tn, tk), lambda i, j, kk: (j, kk))],
                 out_specs=pl.BlockSpec((tm, tn), lambda i, j, kk: (i, j)),
                 scratch_shapes=[pltpu.VMEM((tm, tn), F32)], rider=rider)(*pieces, b)


def ffn_up(h, w704, gate_blk, up_blk, name, tm=512, rider=None):
    s = h.shape[0]

    def body(h_ref, wg_ref, wu_ref, g_ref, u_ref, a_ref):
        hv = h_ref[...]
        g = _dot(hv, wg_ref[...])
        u = _dot(hv, wu_ref[...])
        g_ref[...] = g.astype(BF16)
        u_ref[...] = u.astype(BF16)
        a_ref[...] = (g * _sigmoid(g) * u).astype(BF16)

    ospec = pl.BlockSpec((None, tm, FF_SHARD), lambda j, i: (j, i, 0))
    shp = _sds((N_CHIP, s, FF_SHARD), BF16)
    return _call(body, name=name, out_shape=[shp, shp, shp], grid=(N_CHIP, s // tm),
                 in_specs=[pl.BlockSpec((tm, D_MODEL), lambda j, i: (i, 0)),
                           pl.BlockSpec((None, D_MODEL, FF_SHARD), lambda j, i: (j, gate_blk, 0)),
                           pl.BlockSpec((None, D_MODEL, FF_SHARD), lambda j, i: (j, up_blk, 0))],
                 out_specs=[ospec, ospec, ospec], rider=rider)(h, w704, w704)


def ffn_down(a, w1024, blk, x, name, tm=512):
    s = x.shape[0]

    def body(a_ref, wd_ref, x_ref, o_ref):
        acc = _dot(a_ref[0], wd_ref[0])
        for j in range(1, N_CHIP):
            acc += _dot(a_ref[j], wd_ref[j])
        o_ref[...] = x_ref[...] + 0.5 * acc

    return _call(body, name=name, out_shape=_sds((s, D_MODEL), F32), grid=(s // tm,),
                 in_specs=[pl.BlockSpec((N_CHIP, tm, FF_SHARD), lambda i: (0, i, 0)),
                           pl.BlockSpec((N_CHIP, FF_SHARD, D_MODEL), lambda i: (0, blk, 0)),
                           pl.BlockSpec((tm, D_MODEL), lambda i: (i, 0))],
                 out_specs=pl.BlockSpec((tm, D_MODEL), lambda i: (i, 0)))(a, w1024, x)


def ffn_bwd_hidden(dx, w1024, blk, g, u, name, tm=1024, rider=None):
    s = dx.shape[0]

    def body(dx_ref, wd_ref, g_ref, u_ref, dg_ref, du_ref):
        dy = (0.5 * dx_ref[...]).astype(BF16)
        da = _dot_nt(dy, wd_ref[...])
        gv = g_ref[...].astype(F32)
        uv = u_ref[...].astype(F32)
        sg = _sigmoid(gv)
        dg_ref[...] = (da * uv * (sg * (1.0 + gv * (1.0 - sg)))).astype(BF16)
        du_ref[...] = (da * gv * sg).astype(BF16)

    hspec = pl.BlockSpec((None, tm, FF_SHARD), lambda j, i: (j, i, 0))
    shp = _sds((N_CHIP, s, FF_SHARD), BF16)
    return _call(body, name=name, out_shape=[shp, shp], grid=(N_CHIP, s // tm),
                 in_specs=[pl.BlockSpec((tm, D_MODEL), lambda j, i: (i, 0)),
                           pl.BlockSpec((None, FF_SHARD, D_MODEL), lambda j, i: (j, blk, 0)), hspec, hspec],
                 out_specs=[hspec, hspec], rider=rider)(dx, w1024, g, u)


def ffn_bwd_input(dg, du, w704, gate_blk, up_blk, name, tm=512, rider=None):
    s = dg.shape[1]

    def body(dg_ref, du_ref, wg_ref, wu_ref, o_ref):
        acc = _dot_nt(dg_ref[0], wg_ref[0]) + _dot_nt(du_ref[0], wu_ref[0])
        for j in range(1, N_CHIP):
            acc += _dot_nt(dg_ref[j], wg_ref[j]) + _dot_nt(du_ref[j], wu_ref[j])
        o_ref[...] = acc

    hspec = pl.BlockSpec((N_CHIP, tm, FF_SHARD), lambda i: (0, i, 0))
    return _call(body, name=name, out_shape=_sds((s, D_MODEL), F32), grid=(s // tm,),
                 in_specs=[hspec, hspec,
                           pl.BlockSpec((N_CHIP, D_MODEL, FF_SHARD), lambda i: (0, gate_blk, 0), pl.Buffered(1)),
                           pl.BlockSpec((N_CHIP, D_MODEL, FF_SHARD), lambda i: (0, up_blk, 0), pl.Buffered(1))],
                 out_specs=pl.BlockSpec((tm, D_MODEL), lambda i: (i, 0)), rider=rider)(dg, du, w704, w704)


def ffn_wgrad_in(h, dgu, name, ts=1024):
    s = h.shape[0]
    ns = s // ts

    def body(h_ref, d_ref, o_ref, acc_ref):
        ss = pl.program_id(1)

        @pl.when(ss == 0)
        def _():
            acc_ref[...] = jnp.zeros_like(acc_ref)

        acc_ref[...] += _dot_tn(h_ref[...], d_ref[...])

        @pl.when(ss == ns - 1)
        def _():
            o_ref[...] = acc_ref[...].astype(BF16)

    return _call(body, name=name, out_shape=_sds((N_CHIP, D_MODEL, FF_SHARD), BF16), grid=(N_CHIP, ns),
                 in_specs=[pl.BlockSpec((ts, D_MODEL), lambda j, ss: (ss, 0)),
                           pl.BlockSpec((None, ts, FF_SHARD), lambda j, ss: (j, ss, 0))],
                 out_specs=pl.BlockSpec((None, D_MODEL, FF_SHARD), lambda j, ss: (j, 0, 0)),
                 scratch_shapes=[pltpu.VMEM((D_MODEL, FF_SHARD), F32)])(h, dgu)


def ffn_wgrad_down(a, dx, name, ts=1024):
    s = dx.shape[0]
    ns = s // ts

    def body(a_ref, dx_ref, o_ref, acc_ref):
        ss = pl.program_id(1)

        @pl.when(ss == 0)
        def _():
            acc_ref[...] = jnp.zeros_like(acc_ref)

        acc_ref[...] += _dot_tn(a_ref[...], (0.5 * dx_ref[...]).astype(BF16))

        @pl.when(ss == ns - 1)
        def _():
            o_ref[...] = acc_ref[...].astype(BF16)

    return _call(body, name=name, out_shape=_sds((N_CHIP, FF_SHARD, D_MODEL), BF16), grid=(N_CHIP, ns),
                 in_specs=[pl.BlockSpec((None, ts, FF_SHARD), lambda j, ss: (j, ss, 0)),
                           pl.BlockSpec((ts, D_MODEL), lambda j, ss: (ss, 0))],
                 out_specs=pl.BlockSpec((None, FF_SHARD, D_MODEL), lambda j, ss: (j, 0, 0)),
                 scratch_shapes=[pltpu.VMEM((FF_SHARD, D_MODEL), F32)])(a, dx)


def ffn_forward(x, gain, get_w704, get_w1024, tag, rms_rider=None, up_rider=None):
    h = rms_fwd(x, gain, f"{tag}_rms", rider=rms_rider)
    h, rode_rms = h if rms_rider is not None else (h, None)
    res = ffn_up(h, get_w704(rode_rms), 0, 1, f"{tag}_up", rider=up_rider)
    (g, u, a), rode_up = res if up_rider is not None else (res, None)
    y = ffn_down(a, get_w1024(rode_up), 0, x, f"{tag}_down")
    return y, (h, g, u, a), rode_rms, rode_up


def ffn_backward(dy, x, gain, w704, w1024, saved, tag, ride_down=None, ride_in=None):
    h, g, u, a = saved
    d_wd = ffn_wgrad_down(a, dy, f"{tag}_dwd")
    if ride_down is not None:
        (dg, du), d_wd = ffn_bwd_hidden(dy, w1024, 0, g, u, f"{tag}_dhid", tm=512, rider=ride_down(d_wd))
    else:
        dg, du = ffn_bwd_hidden(dy, w1024, 0, g, u, f"{tag}_dhid")
    d_win = jnp.concatenate([ffn_wgrad_in(h, dg, f"{tag}_dwg"), ffn_wgrad_in(h, du, f"{tag}_dwu")], axis=1)
    if ride_in is not None:
        dh, d_win = ffn_bwd_input(dg, du, w704, 0, 1, f"{tag}_dh", rider=ride_in(d_win))
    else:
        dh = ffn_bwd_input(dg, du, w704, 0, 1, f"{tag}_dh")
    dx, d_gain = rms_bwd([dh], x, gain, dy, f"{tag}_drms")
    return dx, d_gain, d_win, d_wd


def _alibi_slope(head):
    return float(2.0 ** (-ALIBI_MAX_EXP * (head + 1) / N_ATTN_HEADS))


def _head_norm(t, gain_pair, first):
    sa, sb = _pair_sum(t * t, first)
    r = jnp.where(first, lax.rsqrt(sa * (1.0 / HD) + EPS), lax.rsqrt(sb * (1.0 / HD) + EPS))
    return t * r * gain_pair, r


def qk_norm_fwd(p, q_gain, k_gain, name):
    s = p.shape[0]

    def fn(i, q_ref, k_ref, qg_ref, kg_ref, qn_ref, kn_ref):
        first = _lane_first_half((q_ref.shape[0], 2 * HD))
        for src, g_ref, dst in ((q_ref, qg_ref, qn_ref), (k_ref, kg_ref, kn_ref)):
            for pr in range(ATTN_QKV // (2 * HD)):
                cols = slice(pr * 2 * HD, (pr + 1) * 2 * HD)
                y, _ = _head_norm(src[:, cols].astype(F32), g_ref[...], first)
                dst[:, cols] = y.astype(BF16)

    return _rowwise(name, fn, [(p, ATTN_QKV, 0), (p, ATTN_QKV, 1)], [q_gain, k_gain],
                    [((s, ATTN_QKV), BF16), ((s, ATTN_QKV), BF16)])


def qk_norm_bwd(p, dqs, dks, q_gain, k_gain, name):
    s = p.shape[0]
    pairs_per_pattern = GROUP_W // (2 * HD)

    def fn(i, q_ref, k_ref, dq0, dq1, dq2, dk0, dk1, dk2, qg_ref, kg_ref, dqk_ref, dqg_ref, dkg_ref):
        first = _lane_first_half((q_ref.shape[0], 2 * HD))

        @pl.when(i == 0)
        def _():
            dqg_ref[...] = jnp.zeros_like(dqg_ref)
            dkg_ref[...] = jnp.zeros_like(dkg_ref)

        for src, d_refs, g_ref, dst, dg_ref in (
                (q_ref, (dq0, dq1, dq2), qg_ref, dqk_ref.at[:, 0:ATTN_QKV], dqg_ref),
                (k_ref, (dk0, dk1, dk2), kg_ref, dqk_ref.at[:, ATTN_QKV:2 * ATTN_QKV], dkg_ref)):
            for pr in range(ATTN_QKV // (2 * HD)):
                cols = slice(pr * 2 * HD, (pr + 1) * 2 * HD)
                t = src[:, cols].astype(F32)
                sa, sb = _pair_sum(t * t, first)
                r = jnp.where(first, lax.rsqrt(sa * (1.0 / HD) + EPS), lax.rsqrt(sb * (1.0 / HD) + EPS))
                xn = t * r
                within = (pr % pairs_per_pattern) * 2 * HD
                dy = d_refs[pr // pairs_per_pattern][:, within:within + 2 * HD]
                dg_ref[:, cols] += jnp.sum(dy * xn, axis=0, keepdims=True)
                dxn = dy * g_ref[...]
                ma, mb = _pair_sum(dxn * xn, first)
                mean = jnp.where(first, ma, mb) * (1.0 / HD)
                dst[:, cols] = (r * (dxn - xn * mean)).astype(BF16)

    return _rowwise(name, fn, [(p, ATTN_QKV, 0), (p, ATTN_QKV, 1)] + list(dqs) + list(dks), [q_gain, k_gain],
                    [((s, 2 * ATTN_QKV), BF16)], [((1, ATTN_QKV), F32), ((1, ATTN_QKV), F32)])


def _to_streams(a, d):
    if d == 1:
        return a
    s, c = a.shape
    return a.reshape(s // d, d, c).transpose(1, 0, 2).reshape(s, c)


def _from_streams(a, d):
    if d == 1:
        return a
    s, c = a.shape
    return a.reshape(d, s // d, c).transpose(1, 0, 2).reshape(s, c)


def _attn_masks():
    row = lax.broadcasted_iota(jnp.int32, (BLK, BLK), 0)
    col = lax.broadcasted_iota(jnp.int32, (BLK, BLK), 1)
    rel_diag = row - col
    rel_prev = rel_diag + BLK
    return rel_diag, rel_prev


def attn_fwd(q, k, v, pattern, name, tq=512):
    s = q.shape[0]
    d = ATTN_DILATIONS[pattern]
    blocks_per_stream = (s // d) // BLK
    nsb = tq // BLK

    def body(q_ref, k_ref, v_ref, kp_ref, vp_ref, o_ref, l_ref):
        i = pl.program_id(0)
        rel_diag, rel_prev = _attn_masks()
        first = _lane_first_half((BLK, 2 * HD))
        rd_f = (rel_diag * d).astype(F32)
        rp_f = (rel_prev * d).astype(F32)
        for sb in range(nsb):
            rows = slice(sb * BLK, (sb + 1) * BLK)
            has_prev = ((i * nsb + sb) % blocks_per_stream != 0).astype(jnp.int32)
            m_diag = rel_diag >= 0
            m_prev = (rel_prev + (1 - has_prev) * (4 * BLK)) <= BLK
            for pr in range(GROUP_W // (2 * HD)):
                cols = slice(pr * 2 * HD, (pr + 1) * 2 * HD)
                qp = q_ref[rows, cols]
                kc, vc = k_ref[rows, cols], v_ref[rows, cols]
                if sb == 0:
                    kp, vp = kp_ref[:, cols], vp_ref[:, cols]
                else:
                    prows = slice((sb - 1) * BLK, sb * BLK)
                    kp, vp = k_ref[prows, cols], v_ref[prows, cols]
                outs, lses = [], []
                for e in range(2):
                    slope = _alibi_slope(pattern * HEADS_PER_PATTERN + 2 * pr + e)
                    qm = jnp.where(first if e == 0 else jnp.logical_not(first), qp, jnp.zeros_like(qp))
                    s1 = jnp.where(m_diag, _dot_nt(qm, kc) * 0.125 - slope * rd_f, NEG)
                    s0 = jnp.where(m_prev, _dot_nt(qm, kp) * 0.125 - slope * rp_f, NEG)
                    m = jnp.maximum(jnp.max(s1, axis=-1, keepdims=True), jnp.max(s0, axis=-1, keepdims=True))
                    p1 = jnp.exp(s1 - m)
                    p0 = jnp.exp(s0 - m)
                    l = jnp.sum(p1, axis=-1, keepdims=True) + jnp.sum(p0, axis=-1, keepdims=True)
                    inv = 1.0 / l
                    outs.append(_dot((p1 * inv).astype(BF16), vc) + _dot((p0 * inv).astype(BF16), vp))
                    lses.append(m + jnp.log(l))
                o_ref[rows, cols] = jnp.where(first, outs[0], outs[1])
                l_ref[rows, cols] = jnp.where(first, lses[0], lses[1])

    cur = pl.BlockSpec((tq, GROUP_W), lambda i: (i, 0))
    prev = pl.BlockSpec((BLK, GROUP_W), lambda i: (jnp.maximum(i * nsb - 1, 0), 0))
    return _call(body, name=name, out_shape=[_sds((s, GROUP_W), F32), _sds((s, GROUP_W), F32)], grid=(s // tq,),
                 in_specs=[cur, cur, cur, prev, prev], out_specs=[cur, cur])(q, k, v, k, v)


def attn_merge_fwd(os_, lses, name):
    s = os_[0].shape[0]

    def fn(i, o0, o1, o2, l0, l1, l2, out_ref):
        m = jnp.maximum(jnp.maximum(l0[...], l1[...]), l2[...])
        e0, e1, e2 = jnp.exp(l0[...] - m), jnp.exp(l1[...] - m), jnp.exp(l2[...] - m)
        inv = 1.0 / (e0 + e1 + e2)
        out_ref[...] = ((e0 * inv) * o0[...] + (e1 * inv) * o1[...] + (e2 * inv) * o2[...]).astype(BF16)

    return _rowwise(name, fn, list(os_) + list(lses), [], [((s, GROUP_W), BF16)])[0]


def attn_merge_bwd(d_out, os_, lses, name):
    s = d_out.shape[0]

    def fn(i, do_ref, o0, o1, o2, l0, l1, l2, d0, d1, d2, c0, c1, c2):
        first = _lane_first_half((do_ref.shape[0], 2 * HD))
        m = jnp.maximum(jnp.maximum(l0[...], l1[...]), l2[...])
        e0, e1, e2 = jnp.exp(l0[...] - m), jnp.exp(l1[...] - m), jnp.exp(l2[...] - m)
        inv = 1.0 / (e0 + e1 + e2)
        w0, w1, w2 = e0 * inv, e1 * inv, e2 * inv
        do = do_ref[...]
        prod = do * (w0 * o0[...] + w1 * o1[...] + w2 * o2[...])
        for pr in range(GROUP_W // (2 * HD)):
            cols = slice(pr * 2 * HD, (pr + 1) * 2 * HD)
            ta, tb = _pair_sum(prod[:, cols], first)
            t = jnp.where(first, ta, tb)
            for w, c_ref in ((w0, c0), (w1, c1), (w2, c2)):
                c_ref[:, cols] = w[:, cols] * t
        for w, d_ref in ((w0, d0), (w1, d1), (w2, d2)):
            d_ref[...] = (w * do).astype(BF16)

    shp = (s, GROUP_W)
    return _rowwise(name, fn, [d_out] + list(os_) + list(lses), [],
                    [(shp, BF16)] * 3 + [(shp, F32)] * 3)


def attn_bwd(q, k, v, d_o, cterm, lse, pattern, name, tq=512):
    s = q.shape[0]
    d = ATTN_DILATIONS[pattern]
    blocks_per_stream = (s // d) // BLK
    nsb = tq // BLK
    n_blocks = s // BLK

    def body(q_ref, k_ref, v_ref, do_ref, c_ref, l_ref, kp_ref, vp_ref, qn_ref, don_ref, cn_ref, ln_ref,
             dq_ref, dk_ref, dv_ref):
        i = pl.program_id(0)
        rel_diag, rel_prev = _attn_masks()
        first = _lane_first_half((BLK, 2 * HD))
        second = jnp.logical_not(first)
        rd_f = (rel_diag * d).astype(F32)
        rp_f = (rel_prev * d).astype(F32)
        m_diag = rel_diag >= 0
        dq_ref[...] = jnp.zeros_like(dq_ref)
        dk_ref[...] = jnp.zeros_like(dk_ref)
        dv_ref[...] = jnp.zeros_like(dv_ref)

        def pair(qp, dop, cp, lp, kp, vp, rel_f, mask):
            dq = dk = dv = None
            for e in range(2):
                lanes = first if e == 0 else second
                slope = slopes[e]
                qm = jnp.where(lanes, qp, jnp.zeros_like(qp))
                dom = jnp.where(lanes, dop, jnp.zeros_like(dop))
                km = jnp.where(lanes, kp, jnp.zeros_like(kp))
                sc = jnp.where(mask, _dot_nt(qm, kp) * 0.125 - slope * rel_f, NEG)
                pm = jnp.exp(sc - lp[:, e * HD:e * HD + 1])
                dl = pm * (_dot_nt(dom, vp) - cp[:, e * HD:e * HD + 1])
                dl16 = dl.astype(BF16)
                t_dq = _dot(dl16, km)
                t_dk = _dot_tn(dl16, qm)
                t_dv = _dot_tn(pm.astype(BF16), dom)
                dq = t_dq if dq is None else dq + t_dq
                dk = t_dk if dk is None else dk + t_dk
                dv = t_dv if dv is None else dv + t_dv
            return dq * 0.125, dk * 0.125, dv

        for pr in range(GROUP_W // (2 * HD)):
            cols = slice(pr * 2 * HD, (pr + 1) * 2 * HD)
            slopes = [_alibi_slope(pattern * HEADS_PER_PATTERN + 2 * pr + e) for e in range(2)]
            for sb in range(nsb + 1):
                gb = i * nsb + sb
                if sb < nsb:
                    rows = slice(sb * BLK, (sb + 1) * BLK)
                    qp, dop, cp, lp = q_ref[rows, cols], do_ref[rows, cols], c_ref[rows, cols], l_ref[rows, cols]
                else:
                    qp, dop, cp, lp = qn_ref[:, cols], don_ref[:, cols], cn_ref[:, cols], ln_ref[:, cols]
                if sb < nsb:
                    dq1, dk1, dv1 = pair(qp, dop, cp, lp, k_ref[rows, cols], v_ref[rows, cols], rd_f, m_diag)
                    dq_ref[rows, cols] += dq1
                    dk_ref[rows, cols] += dk1
                    dv_ref[rows, cols] += dv1
                valid = jnp.logical_and(gb % blocks_per_stream != 0, gb < n_blocks).astype(jnp.int32)
                m_prev = jnp.logical_and(rel_prev <= BLK, (rel_prev + (1 - valid) * (4 * BLK)) <= BLK)
                if sb == 0:
                    kp, vp = kp_ref[:, cols], vp_ref[:, cols]
                else:
                    prows = slice((sb - 1) * BLK, sb * BLK)
                    kp, vp = k_ref[prows, cols], v_ref[prows, cols]
                dq0, dk0, dv0 = pair(qp, dop, cp, lp, kp, vp, rp_f, m_prev)
                if sb < nsb:
                    dq_ref[rows, cols] += dq0
                if sb > 0:
                    dk_ref[prows, cols] += dk0
                    dv_ref[prows, cols] += dv0

    cur = pl.BlockSpec((tq, GROUP_W), lambda i: (i, 0))
    prev = pl.BlockSpec((BLK, GROUP_W), lambda i: (jnp.maximum(i * nsb - 1, 0), 0))
    nxt = pl.BlockSpec((BLK, GROUP_W), lambda i: (jnp.minimum((i + 1) * nsb, n_blocks - 1), 0))
    shp = _sds((s, GROUP_W), F32)
    return _call(body, name=name, out_shape=[shp, shp, shp], grid=(s // tq,),
                 in_specs=[cur] * 6 + [prev, prev] + [nxt] * 4, out_specs=[cur, cur, cur])(
                     q, k, v, d_o, cterm, lse, k, v, q, d_o, cterm, lse)


def _band_constants(d):
    row = lax.broadcasted_iota(jnp.int32, (2 * BLK, 2 * BLK), 0)
    col = lax.broadcasted_iota(jnp.int32, (2 * BLK, 2 * BLK), 1)
    rel = BLK + jnp.where(row >= BLK, row - BLK, row) - col
    band = jnp.logical_and(rel >= 0, rel <= BLK)
    return (rel * d).astype(F32), band, (col >= BLK).astype(jnp.int32)


def _stack_heads(x, first):
    zero = jnp.zeros_like(x)
    return jnp.concatenate([jnp.where(first, x, zero), jnp.where(first, zero, x)], axis=0)


def _unstack_heads(x2, first):
    return jnp.where(first, x2[:BLK], x2[BLK:])


def _head_column(x):
    return jnp.concatenate([x[:, 0:1], x[:, HD:HD + 1]], axis=0)


def attn_fwd2(q, k, v, pattern, name, tq=512):
    s = q.shape[0]
    d = ATTN_DILATIONS[pattern]
    blocks_per_stream = (s // d) // BLK
    nsb = tq // BLK

    def body(q_ref, k_ref, v_ref, kp_ref, vp_ref, o_ref, l_ref):
        i = pl.program_id(0)
        rel_f, band, own = _band_constants(d)
        first = _lane_first_half((BLK, 2 * HD))
        upper = lax.broadcasted_iota(jnp.int32, (2 * BLK, 1), 0) < BLK
        for sb in range(nsb):
            rows = slice(sb * BLK, (sb + 1) * BLK)
            has_prev = ((i * nsb + sb) % blocks_per_stream != 0).astype(jnp.int32)
            mask = jnp.logical_and(band, (own + has_prev) > 0)
            for pr in range(GROUP_W // (2 * HD)):
                cols = slice(pr * 2 * HD, (pr + 1) * 2 * HD)
                if sb == 0:
                    kcat = jnp.concatenate([kp_ref[:, cols], k_ref[rows, cols]], axis=0)
                    vcat = jnp.concatenate([vp_ref[:, cols], v_ref[rows, cols]], axis=0)
                else:
                    both = slice((sb - 1) * BLK, (sb + 1) * BLK)
                    kcat, vcat = k_ref[both, cols], v_ref[both, cols]
                h0 = pattern * HEADS_PER_PATTERN + 2 * pr
                slope = jnp.where(upper, _alibi_slope(h0), _alibi_slope(h0 + 1))
                sc = _dot_nt(_stack_heads(q_ref[rows, cols], first), kcat) * 0.125 - slope * rel_f
                sc = jnp.where(mask, sc, NEG)
                m = jnp.max(sc, axis=-1, keepdims=True)
                p = jnp.exp(sc - m)
                l = jnp.sum(p, axis=-1, keepdims=True)
                o2 = _dot((p * (1.0 / l)).astype(BF16), vcat)
                o_ref[rows, cols] = _unstack_heads(o2, first)
                lse = m + jnp.log(l)
                l_ref[rows, cols] = jnp.where(first, lse[:BLK], lse[BLK:])

    cur = pl.BlockSpec((tq, GROUP_W), lambda i: (i, 0))
    prev = pl.BlockSpec((BLK, GROUP_W), lambda i: (jnp.maximum(i * nsb - 1, 0), 0))
    return _call(body, name=name, out_shape=[_sds((s, GROUP_W), F32), _sds((s, GROUP_W), F32)], grid=(s // tq,),
                 in_specs=[cur, cur, cur, prev, prev], out_specs=[cur, cur])(q, k, v, k, v)


def attn_bwd2(q, k, v, d_o, cterm, lse, pattern, name, tq=512):
    s = q.shape[0]
    d = ATTN_DILATIONS[pattern]
    blocks_per_stream = (s // d) // BLK
    nsb = tq // BLK
    n_blocks = s // BLK

    def body(q_ref, k_ref, v_ref, do_ref, c_ref, l_ref, kp_ref, vp_ref, qn_ref, kn_ref, vn_ref, don_ref, cn_ref,
             ln_ref, dq_ref, dk_ref, dv_ref):
        i = pl.program_id(0)
        rel_f, band, own = _band_constants(d)
        first = _lane_first_half((BLK, 2 * HD))
        upper = lax.broadcasted_iota(jnp.int32, (2 * BLK, 1), 0) < BLK
        dk_ref[...] = jnp.zeros_like(dk_ref)
        dv_ref[...] = jnp.zeros_like(dv_ref)
        for sb in range(nsb + 1):
            gb = i * nsb + sb
            rows = slice(sb * BLK, (sb + 1) * BLK)
            before = slice((sb - 1) * BLK, sb * BLK)
            inside = (gb < n_blocks).astype(jnp.int32)
            has_prev = jnp.logical_and(gb % blocks_per_stream != 0, gb < n_blocks).astype(jnp.int32)
            mask = jnp.logical_and(band, (own * inside + has_prev) > 0)
            for pr in range(GROUP_W // (2 * HD)):
                cols = slice(pr * 2 * HD, (pr + 1) * 2 * HD)
                if sb == 0:
                    kcat = jnp.concatenate([kp_ref[:, cols], k_ref[rows, cols]], axis=0)
                    vcat = jnp.concatenate([vp_ref[:, cols], v_ref[rows, cols]], axis=0)
                elif sb == nsb:
                    kcat = jnp.concatenate([k_ref[before, cols], kn_ref[:, cols]], axis=0)
                    vcat = jnp.concatenate([v_ref[before, cols], vn_ref[:, cols]], axis=0)
                else:
                    both = slice((sb - 1) * BLK, (sb + 1) * BLK)
                    kcat, vcat = k_ref[both, cols], v_ref[both, cols]
                if sb < nsb:
                    qp, dop, cp, lp = q_ref[rows, cols], do_ref[rows, cols], c_ref[rows, cols], l_ref[rows, cols]
                else:
                    qp, dop, cp, lp = qn_ref[:, cols], don_ref[:, cols], cn_ref[:, cols], ln_ref[:, cols]
                h0 = pattern * HEADS_PER_PATTERN + 2 * pr
                slope = jnp.where(upper, _alibi_slope(h0), _alibi_slope(h0 + 1))
                q2 = _stack_heads(qp, first)
                do2 = _stack_heads(dop, first)
                sc = jnp.where(mask, _dot_nt(q2, kcat) * 0.125 - slope * rel_f, NEG)
                pm = jnp.exp(sc - _head_column(lp))
                dl = (pm * (_dot_nt(do2, vcat) - _head_column(cp))).astype(BF16)
                if sb < nsb:
                    dq_ref[rows, cols] = _unstack_heads(_dot(dl, kcat), first) * 0.125
                dk2 = _dot_tn(dl, q2) * 0.125
                dv2 = _dot_tn(pm.astype(BF16), do2)
                if sb > 0:
                    dk_ref[before, cols] += dk2[:BLK]
                    dv_ref[before, cols] += dv2[:BLK]
                if sb < nsb:
                    dk_ref[rows, cols] += dk2[BLK:]
                    dv_ref[rows, cols] += dv2[BLK:]

    cur = pl.BlockSpec((tq, GROUP_W), lambda i: (i, 0))
    prev = pl.BlockSpec((BLK, GROUP_W), lambda i: (jnp.maximum(i * nsb - 1, 0), 0))
    nxt = pl.BlockSpec((BLK, GROUP_W), lambda i: (jnp.minimum((i + 1) * nsb, n_blocks - 1), 0))
    shp = _sds((s, GROUP_W), F32)
    return _call(body, name=name, out_shape=[shp, shp, shp], grid=(s // tq,),
                 in_specs=[cur] * 6 + [prev, prev] + [nxt] * 6, out_specs=[cur, cur, cur])(
                     q, k, v, d_o, cterm, lse, k, v, q, k, v, d_o, cterm, lse)


HALO = 16
CONV_TQ = 512


def conv_fwd(p, w, b, name):
    s = p.shape[0]
    tq = CONV_TQ
    ncol = SSD_CONV_DIM // GROUP_W
    cb0 = COL_XBC // GROUP_W

    def body(u_ref, up_ref, w_ref, b_ref, c_ref, xc_ref):
        i = pl.program_id(0)
        prev = up_ref[...].astype(F32) * (i > 0).astype(F32)
        ext = jnp.concatenate([prev, u_ref[...].astype(F32)], axis=0)
        acc = b_ref[...] + w_ref[SSD_CONV - 1:SSD_CONV, :] * ext[HALO:HALO + tq]
        for kk in range(SSD_CONV - 1):
            acc += w_ref[kk:kk + 1, :] * pltpu.roll(ext, SSD_CONV - 1 - kk, 0)[HALO:HALO + tq]
        c_ref[...] = acc.astype(BF16)
        xc_ref[...] = (acc * _sigmoid(acc)).astype(BF16)

    cur_in = pl.BlockSpec((tq, GROUP_W), lambda i, j: (i, cb0 + j))
    prev_in = pl.BlockSpec((HALO, GROUP_W), lambda i, j: (jnp.maximum(i * (tq // HALO) - 1, 0), cb0 + j))
    cur_out = pl.BlockSpec((tq, GROUP_W), lambda i, j: (i, j))
    shp = _sds((s, SSD_CONV_DIM), BF16)
    return _call(body, name=name, out_shape=[shp, shp], grid=(s // tq, ncol),
                 in_specs=[cur_in, prev_in, pl.BlockSpec((SSD_CONV, GROUP_W), lambda i, j: (0, j)),
                           pl.BlockSpec((1, GROUP_W), lambda i, j: (0, j))],
                 out_specs=[cur_out, cur_out])(p, p, w, b)


def conv_bwd(p, cpre, dxs, d_b, d_c, w, name):
    s = p.shape[0]
    tq = CONV_TQ
    ncol = SSD_CONV_DIM // GROUP_W
    n_xs = SSD_INNER // GROUP_W
    cb0 = COL_XBC // GROUP_W
    nt = s // tq

    def body(u_ref, up_ref, c_ref, cn_ref, dx_ref, dxn_ref, dbm_ref, dbmn_ref, dcm_ref, dcmn_ref, w_ref,
             du_ref, dw_ref, db_ref):
        j, i = pl.program_id(0), pl.program_id(1)

        def dpre(c16, dx):
            c = c16.astype(F32)
            sg = _sigmoid(c)
            return dx * (sg * (1.0 + c * (1.0 - sg)))

        def pick(a_ref, b_ref, c_ref_):
            return jnp.where(j < n_xs, a_ref[...], jnp.where(j == n_xs, b_ref[...], c_ref_[...]))

        dc = dpre(c_ref[...], pick(dx_ref, dbm_ref, dcm_ref))
        dcn = dpre(cn_ref[...], pick(dxn_ref, dbmn_ref, dcmn_ref)) * (i < nt - 1).astype(F32)
        dext = jnp.concatenate([dc, dcn], axis=0)
        prev = up_ref[...].astype(F32) * (i > 0).astype(F32)
        uext = jnp.concatenate([prev, u_ref[...].astype(F32)], axis=0)

        @pl.when(i == 0)
        def _():
            dw_ref[...] = jnp.zeros_like(dw_ref)
            db_ref[...] = jnp.zeros_like(db_ref)

        du = w_ref[SSD_CONV - 1:SSD_CONV, :] * dc
        for kk in range(SSD_CONV - 1):
            sh = SSD_CONV - 1 - kk
            du += w_ref[kk:kk + 1, :] * pltpu.roll(dext, tq + HALO - sh, 0)[0:tq]
        du_ref[...] = du.astype(BF16)
        for kk in range(SSD_CONV):
            shifted = uext if kk == SSD_CONV - 1 else pltpu.roll(uext, SSD_CONV - 1 - kk, 0)
            dw_ref[kk:kk + 1, :] += jnp.sum(dc * shifted[HALO:HALO + tq], axis=0, keepdims=True)
        db_ref[...] += jnp.sum(dc, axis=0, keepdims=True)

    hb = tq // HALO
    cur_p = pl.BlockSpec((tq, GROUP_W), lambda j, i: (i, cb0 + j))
    prev_p = pl.BlockSpec((HALO, GROUP_W), lambda j, i: (jnp.maximum(i * hb - 1, 0), cb0 + j))
    cur = pl.BlockSpec((tq, GROUP_W), lambda j, i: (i, j))
    nxt = pl.BlockSpec((HALO, GROUP_W), lambda j, i: (jnp.minimum((i + 1) * hb, s // HALO - 1), j))

    def piece(first_tile, n_tiles):
        def on(j):
            return jnp.logical_and(j >= first_tile, j < first_tile + n_tiles)

        def col(j):
            return jnp.clip(j - first_tile, 0, n_tiles - 1)

        return (pl.BlockSpec((tq, GROUP_W), lambda j, i: (jnp.where(on(j), i, 0), col(j))),
                pl.BlockSpec((HALO, GROUP_W),
                             lambda j, i: (jnp.where(on(j), jnp.minimum((i + 1) * hb, s // HALO - 1), 0), col(j))))

    return _call(body, name=name,
                 out_shape=[_sds((s, SSD_CONV_DIM), BF16), _sds((8, SSD_CONV_DIM), F32), _sds((1, SSD_CONV_DIM), F32)],
                 grid=(ncol, nt),
                 in_specs=[cur_p, prev_p, cur, nxt, *piece(0, n_xs), *piece(n_xs, 1), *piece(n_xs + 1, 1),
                           pl.BlockSpec((SSD_CONV, GROUP_W), lambda j, i: (0, j))],
                 out_specs=[cur, pl.BlockSpec((8, GROUP_W), lambda j, i: (0, j)),
                            pl.BlockSpec((1, GROUP_W), lambda j, i: (0, j))])(
                                p, p, cpre, cpre, dxs, dxs, d_b, d_b, d_c, d_c, w)


def _softplus(x):
    return jnp.maximum(x, 0.0) + jnp.log(1.0 + jnp.exp(-jnp.abs(x)))


def _ssd_decays(dtr_ref, dtrt_ref, bias_ref, biast_ref, alog_ref, alogt_ref):
    row = lax.broadcasted_iota(jnp.int32, (BLK, BLK), 0)
    col = lax.broadcasted_iota(jnp.int32, (BLK, BLK), 1)
    lower = (row >= col).astype(F32)
    upper = (row <= col).astype(F32)
    dtb = dtr_ref[...] + bias_ref[...]
    dt = _softplus(dtb)
    a = dt * (-jnp.exp(alog_ref[...]))
    cs = _dot_hi(lower, a)
    a_t = _softplus(dtrt_ref[...] + biast_ref[...]) * (-jnp.exp(alogt_ref[...]))
    cs_t = _dot_hi(a_t, upper)
    return dtb, dt, cs, cs_t, row, col, upper


SSD_GROUPS_PER_STEP = 4


def _per_group(body, gps, kinds):
    def wrapped(*refs):
        for gi in range(gps):
            args, pos = [], 0
            for kind, n in kinds:
                if kind == "each":
                    args.append(refs[pos + gi])
                    pos += gps
                    continue
                ref = refs[pos]
                pos += 1
                if kind == "cols":
                    args.append(ref.at[:, gi * n:(gi + 1) * n])
                else:
                    args.append(ref.at[gi] if n == 1 else ref.at[pl.ds(gi * n, n)])
            body(*args)

    return wrapped


def ssd_fwd(p, xc, dtg, dtg_t, params, gn, name):
    s = p.shape[0]
    nc = s // BLK
    bias, bias_t, alog, alog_t, dskip = params

    def body(xs_ref, b_ref, c_ref, z_ref, dtr_ref, dtrt_ref, bias_ref, biast_ref, alog_ref, alogt_ref, dsk_ref,
             gn_ref, y_ref, sin_ref, hp_ref, h_ref):
        c_idx = pl.program_id(1)

        @pl.when(c_idx == 0)
        def _():
            h_ref[...] = jnp.zeros_like(h_ref)

        _, dt, cs, cs_t, row, col, _ = _ssd_decays(dtr_ref, dtrt_ref, bias_ref, biast_ref, alog_ref, alogt_ref)
        first = _lane_first_half((BLK, 2 * HD))
        first_row = _lane_first_half((1, 2 * HD))
        tril = row >= col
        b16, c16 = b_ref[...], c_ref[...]
        cb = _dot_nt(c16, b16)
        n_pairs = GROUP_W // (2 * HD)
        tot = cs[BLK - 1:BLK, :]
        exp_cs, exp_rest, exp_tot = jnp.exp(cs), jnp.exp(tot - cs), jnp.exp(tot)

        def per_head(v, mask):
            return jnp.concatenate([jnp.where(mask, v[:, 2 * pr:2 * pr + 1], v[:, 2 * pr + 1:2 * pr + 2])
                                    for pr in range(n_pairs)], axis=1)

        xs = xs_ref[...].astype(F32)
        xt = xs * per_head(dt, first)
        xt16 = xt.astype(BF16)
        hstate = jnp.concatenate([h_ref[pr] for pr in range(n_pairs)], axis=1)
        for pr in range(n_pairs):
            hp_ref[pr] = h_ref[pr]
        y_off = per_head(exp_cs, first) * _dot(c16, hstate.astype(BF16))
        new = per_head(exp_tot, first_row) * hstate + _dot_tn(b16, (per_head(exp_rest, first) * xt).astype(BF16))
        for pr in range(n_pairs):
            h_ref[pr] = new[:, pr * 2 * HD:(pr + 1) * 2 * HD]
        y_diag = []
        for pr in range(n_pairs):
            cols = slice(pr * 2 * HD, (pr + 1) * 2 * HD)
            m2 = jnp.concatenate(
                [(cb * jnp.exp(jnp.where(tril, cs[:, h:h + 1] - cs_t[h:h + 1, :], NEG))).astype(BF16)
                 for h in (2 * pr, 2 * pr + 1)], axis=1)
            y_diag.append(_dot(m2, _stack_heads(xt16[:, cols], first)))
        y = jnp.concatenate(y_diag, axis=1) + y_off + xs * per_head(dsk_ref[...], first_row)
        y_ref[...] = y
        zv = z_ref[...].astype(F32)
        yz = y * (zv * _sigmoid(zv))
        r = lax.rsqrt(jnp.mean(yz * yz, axis=-1, keepdims=True) + EPS)
        sin_ref[...] = (yz * r * gn_ref[...]).astype(BF16)

    gps = SSD_GROUPS_PER_STEP
    wide, narrow, lead = ("cols", GROUP_W), ("cols", BLK), ("lead", 1)
    kinds = [wide, narrow, narrow, ("each", 0)] + [lead] * 7 + [wide, wide, wide, lead, ("lead", 4)]
    wide_w, narrow_w = GROUP_W * gps, BLK * gps
    gparam = pl.BlockSpec((gps, 1, 8), lambda g, c: (g, 0, 0))
    gparam_t = pl.BlockSpec((gps, 8, 1), lambda g, c: (g, 0, 0))
    z_specs = [pl.BlockSpec((BLK, GROUP_W), functools.partial(lambda g, c, gi: (c, COL_Z // GROUP_W + gps * g + gi),
                                                              gi=gi)) for gi in range(gps)]
    return _call(
        _per_group(body, gps, kinds), name=name,
        out_shape=[_sds((s, SSD_INNER), F32), _sds((s, SSD_INNER), BF16),
                   _sds((SSD_GROUPS, nc, 4, BLK, 2 * HD), F32)],
        grid=(SSD_GROUPS // gps, nc),
        in_specs=[pl.BlockSpec((BLK, wide_w), lambda g, c: (c, g)),
                  pl.BlockSpec((BLK, narrow_w), lambda g, c: (c, SSD_INNER // narrow_w + g)),
                  pl.BlockSpec((BLK, narrow_w), lambda g, c: (c, (SSD_INNER + SSD_GROUPS * BLK) // narrow_w + g)),
                  *z_specs,
                  pl.BlockSpec((gps, BLK, 8), lambda g, c: (g, c, 0)),
                  pl.BlockSpec((gps, 8, BLK), lambda g, c: (g, 0, c)),
                  gparam, gparam_t, gparam, gparam_t, gparam,
                  pl.BlockSpec((1, wide_w), lambda g, c: (0, g))],
        out_specs=[pl.BlockSpec((BLK, wide_w), lambda g, c: (c, g)),
                   pl.BlockSpec((BLK, wide_w), lambda g, c: (c, g)),
                   pl.BlockSpec((gps, None, 4, BLK, 2 * HD), lambda g, c: (g, c, 0, 0, 0))],
        scratch_shapes=[pltpu.VMEM((4 * gps, BLK, 2 * HD), F32)],
    )(xc, xc, xc, *([p] * gps), dtg, dtg_t, bias, bias_t, alog, alog_t, dskip, gn)


def ssd_bwd(p, xc, y, d_sin, hprev, dtg, dtg_t, params, gn, name):
    s = p.shape[0]
    nc = s // BLK
    bias, bias_t, alog, alog_t, dskip = params

    def body(xs_ref, b_ref, c_ref, z_ref, y_ref, dsin_ref, hp_ref, dtr_ref, dtrt_ref, bias_ref,
             biast_ref, alog_ref, alogt_ref, dsk_ref, gn_ref,
             dxs_ref, db_ref, dc_ref, dz_ref, ddt_ref, da_ref, dbias_ref, ddsk_ref, dgn_ref, dh_ref):
        c_idx = pl.program_id(1)

        @pl.when(c_idx == 0)
        def _():
            dh_ref[...] = jnp.zeros_like(dh_ref)
            da_ref[...] = jnp.zeros_like(da_ref)
            dbias_ref[...] = jnp.zeros_like(dbias_ref)
            ddsk_ref[...] = jnp.zeros_like(ddsk_ref)
            dgn_ref[...] = jnp.zeros_like(dgn_ref)

        dtb, dt, cs, cs_t, row, col, upper = _ssd_decays(dtr_ref, dtrt_ref, bias_ref, biast_ref, alog_ref, alogt_ref)
        first = _lane_first_half((BLK, 2 * HD))
        second = jnp.logical_not(first)
        first_row = _lane_first_half((1, 2 * HD))
        tril = row >= col
        triu = row <= col
        last_row = lax.broadcasted_iota(jnp.int32, (BLK, 1), 0) == BLK - 1
        lane8 = lax.broadcasted_iota(jnp.int32, (BLK, 8), 1)

        yv = y_ref[...]
        zv = z_ref[...].astype(F32)
        sg = _sigmoid(zv)
        yz = yv * (zv * sg)
        r = lax.rsqrt(jnp.mean(yz * yz, axis=-1, keepdims=True) + EPS)
        yzn = yz * r
        dsn = dsin_ref[...]
        dgn_ref[...] += jnp.sum(dsn * yzn, axis=0, keepdims=True)
        dsn = dsn * gn_ref[...]
        dyz = r * (dsn - yzn * jnp.mean(dsn * yzn, axis=-1, keepdims=True))
        dy = dyz * (zv * sg)
        dz_ref[...] = (dyz * yv * (sg * (1.0 + zv * (1.0 - sg)))).astype(BF16)
        xs_all = xs_ref[...].astype(F32)
        ddsk_ref[...] += jnp.sum(dy * xs_all, axis=0, keepdims=True)

        b16, c16 = b_ref[...], c_ref[...]
        cb = _dot_nt(c16, b16)
        cb_t = _dot_nt(b16, c16)
        n_pairs = GROUP_W // (2 * HD)
        tot = cs[BLK - 1:BLK, :]
        exp_cs, exp_rest, exp_tot = jnp.exp(cs), jnp.exp(tot - cs), jnp.exp(tot)

        def per_head(v, mask):
            return jnp.concatenate([jnp.where(mask, v[:, 2 * pr:2 * pr + 1], v[:, 2 * pr + 1:2 * pr + 2])
                                    for pr in range(n_pairs)], axis=1)

        def head_sums(v):
            out = jnp.zeros((BLK, 8), F32)
            for pr in range(n_pairs):
                sa, sb = _pair_sum(v[:, pr * 2 * HD:(pr + 1) * 2 * HD], first)
                out = jnp.where(lane8 == 2 * pr, sa, jnp.where(lane8 == 2 * pr + 1, sb, out))
            return out

        dt_w, e_w, f_w = per_head(dt, first), per_head(exp_cs, first), per_head(exp_rest, first)
        xt = xs_all * dt_w
        xt16 = xt.astype(BF16)
        hstate = jnp.concatenate([hp_ref[pr] for pr in range(n_pairs)], axis=1)
        h16 = hstate.astype(BF16)
        dhn = jnp.concatenate([dh_ref[pr] for pr in range(n_pairs)], axis=1)
        dhn16 = dhn.astype(BF16)
        edy16 = (e_w * dy).astype(BF16)
        y_off = e_w * _dot(c16, h16)
        dcs_all = head_sums(dy * y_off)
        dc_acc = _dot_nt(edy16, h16)
        zmat = _dot(b16, dhn16)
        t_all = head_sums(zmat * xt) * exp_rest
        hh_rows = jnp.sum(head_sums(dhn * hstate), axis=0, keepdims=True)
        dtot = jnp.sum(t_all, axis=0, keepdims=True) + hh_rows * exp_tot
        dcs_all = dcs_all - t_all + jnp.where(last_row, dtot, 0.0)
        fxt16 = (f_w * xt).astype(BF16)
        db_acc = _dot_nt(fxt16, dhn16)
        dh_new = _dot_tn(c16, edy16) + per_head(exp_tot, first_row) * dhn
        for pr in range(n_pairs):
            dh_ref[pr] = dh_new[:, pr * 2 * HD:(pr + 1) * 2 * HD]
        g_sum = jnp.zeros((BLK, BLK), F32)
        gt_sum = jnp.zeros((BLK, BLK), F32)
        d_xt_parts = []
        for pr in range(n_pairs):
            cols = slice(pr * 2 * HD, (pr + 1) * 2 * HD)
            dym2 = _stack_heads(dy[:, cols].astype(BF16), first)
            d_m2 = _dot_nt(dym2, xt16[:, cols])
            d_mt2 = _dot_nt(xt16[:, cols], dym2)
            mt2 = []
            for e, h in enumerate((2 * pr, 2 * pr + 1)):
                cs_c, cs_r = cs[:, h:h + 1], cs_t[h:h + 1, :]
                decay = jnp.exp(jnp.where(tril, cs_c - cs_r, NEG))
                decay_t = jnp.exp(jnp.where(triu, cs_r - cs_c, NEG))
                gm = d_m2[e * BLK:(e + 1) * BLK] * decay
                gmt = d_mt2[:, e * BLK:(e + 1) * BLK] * decay_t
                g_sum += gm
                gt_sum += gmt
                dcs_h = jnp.sum(gm * cb, axis=-1, keepdims=True) - jnp.sum(gmt * cb_t, axis=-1, keepdims=True)
                dcs_all = dcs_all + jnp.where(lane8 == h, dcs_h, 0.0)
                mt2.append((cb_t * decay_t).astype(BF16))
            d_xt_parts.append(_dot(jnp.concatenate(mt2, axis=1), dym2))
        d_xt = jnp.concatenate(d_xt_parts, axis=1) + f_w * zmat
        dxs_ref[...] = dy * per_head(dsk_ref[...], first_row) + d_xt * dt_w
        ddtx_all = head_sums(d_xt * xs_all)

        dc_ref[...] = dc_acc + _dot(g_sum.astype(BF16), b16)
        db_ref[...] = db_acc + _dot(gt_sum.astype(BF16), c16)
        d_a = _dot_hi(upper, dcs_all)
        a_neg = -jnp.exp(alog_ref[...])
        ddt = ddtx_all + d_a * a_neg
        da_ref[...] += jnp.sum(d_a * dt, axis=0, keepdims=True)
        ddtr = ddt * _sigmoid(dtb)
        ddt_ref[...] = ddtr
        dbias_ref[...] += jnp.sum(ddtr, axis=0, keepdims=True)

    gps = SSD_GROUPS_PER_STEP
    k_wide, k_narrow, k_lead = ("cols", GROUP_W), ("cols", BLK), ("lead", 1)
    kinds = ([k_wide, k_narrow, k_narrow, ("each", 0), k_wide, k_wide] + [k_lead] * 8 + [k_wide]
             + [k_wide, k_narrow, k_narrow, k_wide] + [k_lead] * 4 + [k_wide] + [("lead", 4)])
    wide_w, narrow_w = GROUP_W * gps, BLK * gps
    rc = lambda c: nc - 1 - c
    gparam = pl.BlockSpec((gps, 1, 8), lambda g, c: (g, 0, 0))
    gparam_t = pl.BlockSpec((gps, 8, 1), lambda g, c: (g, 0, 0))
    wide = pl.BlockSpec((BLK, wide_w), lambda g, c: (rc(c), g))
    narrow = pl.BlockSpec((BLK, narrow_w), lambda g, c: (rc(c), g))
    z_specs = [pl.BlockSpec((BLK, GROUP_W),
                            functools.partial(lambda g, c, gi: (rc(c), COL_Z // GROUP_W + gps * g + gi), gi=gi))
               for gi in range(gps)]
    return _call(
        _per_group(body, gps, kinds), name=name,
        out_shape=[_sds((s, SSD_INNER), F32), _sds((s, GROUP_W), F32), _sds((s, GROUP_W), F32),
                   _sds((s, SSD_INNER), BF16), _sds((SSD_GROUPS, s, 8), F32),
                   _sds((SSD_GROUPS, 1, 8), F32), _sds((SSD_GROUPS, 1, 8), F32),
                   _sds((SSD_GROUPS, 1, GROUP_W), F32), _sds((1, SSD_INNER), F32)],
        grid=(SSD_GROUPS // gps, nc),
        in_specs=[wide,
                  pl.BlockSpec((BLK, narrow_w), lambda g, c: (rc(c), SSD_INNER // narrow_w + g)),
                  pl.BlockSpec((BLK, narrow_w), lambda g, c: (rc(c), (SSD_INNER + SSD_GROUPS * BLK) // narrow_w + g)),
                  *z_specs,
                  wide, wide,
                  pl.BlockSpec((gps, None, 4, BLK, 2 * HD), lambda g, c: (g, rc(c), 0, 0, 0)),
                  pl.BlockSpec((gps, BLK, 8), lambda g, c: (g, rc(c), 0)),
                  pl.BlockSpec((gps, 8, BLK), lambda g, c: (g, 0, rc(c))),
                  gparam, gparam_t, gparam, gparam_t, gparam,
                  pl.BlockSpec((1, wide_w), lambda g, c: (0, g))],
        out_specs=[wide, narrow, narrow, wide,
                   pl.BlockSpec((gps, BLK, 8), lambda g, c: (g, rc(c), 0)),
                   gparam, gparam,
                   pl.BlockSpec((gps, 1, GROUP_W), lambda g, c: (g, 0, 0)),
                   pl.BlockSpec((1, wide_w), lambda g, c: (0, g))],
        scratch_shapes=[pltpu.VMEM((4 * gps, BLK, 2 * HD), F32)],
    )(xc, xc, xc, *([p] * gps), y, d_sin, hprev, dtg, dtg_t, bias, bias_t, alog, alog_t, dskip, gn)


def merge_fwd(p, a, sbr, name, tm=512):
    s = p.shape[0]
    nj = D_MODEL // GROUP_W

    def body(ga_ref, gs_ref, a_ref, s_ref, o_ref):
        o_ref[...] = (_sigmoid(ga_ref[...].astype(F32)) * a_ref[...]
                      + _sigmoid(gs_ref[...].astype(F32)) * s_ref[...]).astype(BF16)

    blk = pl.BlockSpec((tm, GROUP_W), lambda i, j: (i, j))
    return _call(body, name=name, out_shape=_sds((s, D_MODEL), BF16), grid=(s // tm, nj),
                 in_specs=[pl.BlockSpec((tm, GROUP_W), lambda i, j: (i, COL_GA // GROUP_W + j)),
                           pl.BlockSpec((tm, GROUP_W), lambda i, j: (i, COL_GS // GROUP_W + j)), blk, blk],
                 out_specs=blk)(p, p, a, sbr)


def merge_bwd(p, a, sbr, dmerged, name, tm=512):
    s = p.shape[0]
    nj = D_MODEL // GROUP_W

    def body(ga_ref, gs_ref, a_ref, s_ref, dm_ref, da_ref, ds_ref, dga_ref, dgs_ref):
        dm = dm_ref[...]
        sa = _sigmoid(ga_ref[...].astype(F32))
        ss = _sigmoid(gs_ref[...].astype(F32))
        da_ref[...] = (dm * sa).astype(BF16)
        ds_ref[...] = (dm * ss).astype(BF16)
        dga_ref[...] = (dm * a_ref[...] * sa * (1.0 - sa)).astype(BF16)
        dgs_ref[...] = (dm * s_ref[...] * ss * (1.0 - ss)).astype(BF16)

    blk = pl.BlockSpec((tm, GROUP_W), lambda i, j: (i, j))
    shp = _sds((s, D_MODEL), BF16)
    return _call(body, name=name, out_shape=[shp] * 4, grid=(s // tm, nj),
                 in_specs=[pl.BlockSpec((tm, GROUP_W), lambda i, j: (i, COL_GA // GROUP_W + j)),
                           pl.BlockSpec((tm, GROUP_W), lambda i, j: (i, COL_GS // GROUP_W + j)), blk, blk, blk],
                 out_specs=[blk] * 4)(p, p, a, sbr, dmerged)


def _group_major(v):
    return v.reshape(SSD_GROUPS, 1, 8), v.reshape(SSD_GROUPS, 8, 1)


def mixer_forward(x, w, rider=None, later_weights=None):
    s = x.shape[0]
    h = rms_fwd(x, w["mix_norm"], "mix_rms")
    p = matmul_nn(h, w["w_in_main"], "mix_proj", BF16, tm=1024, tn=512, rider=rider)
    rode = None
    if rider is not None:
        p, rode = p
        w = dict(w, **later_weights(rode))
    dt_raw = matmul_nn(h, w["w_in_dt"], "mix_proj_dt", F32, tm=1024, tn=DT_PAD)
    qn, kn = qk_norm_fwd(p, w["q_gain"], w["k_gain"], "qk_norm")
    streams, os_, lses = [], [], []
    for g, d in enumerate(ATTN_DILATIONS):
        cols = slice(g * GROUP_W, (g + 1) * GROUP_W)
        qs, ks = _to_streams(qn[:, cols], d), _to_streams(kn[:, cols], d)
        vs = _to_streams(p[:, COL_V + g * GROUP_W:COL_V + (g + 1) * GROUP_W], d)
        o, lse = attn_fwd2(qs, ks, vs, g, f"attn_fwd{g}")
        streams.append((qs, ks, vs, lse))
        os_.append(_from_streams(o, d))
        lses.append(_from_streams(lse, d))
    attn_o = attn_merge_fwd(os_, lses, "attn_merge")
    cpre, xc = conv_fwd(p, w["conv_w"], w["conv_b"], "conv_fwd")
    dtg = dt_raw[:, :SSD_HEADS].reshape(s, SSD_GROUPS, 8).transpose(1, 0, 2)
    dtg_t = dtg.transpose(0, 2, 1)
    params = (*_group_major(w["dt_bias"]), *_group_major(w["a_log"]), _group_major(w["d_skip"])[0])
    y, s_in, hprev = ssd_fwd(p, xc, dtg, dtg_t, params, w["ssd_norm"], "ssd_fwd")
    a = matmul_nn(attn_o, w["w_attn_branch"], "attn_branch", F32, tm=1024, tn=512)
    sbr = matmul_nn(s_in, w["w_ssd_branch"], "ssd_branch", F32, tm=1024, tn=512)
    merged = merge_fwd(p, a, sbr, "merge")
    x_out = matmul_nn(merged, w["w_out"], "mix_out", F32, tm=1024, tn=512, res=x)
    saved = dict(h=h, p=p, streams=streams, os=os_, lses=lses, attn_o=attn_o, cpre=cpre, xc=xc, dtg=dtg,
                 dtg_t=dtg_t, params=params, y=y, s_in=s_in, hprev=hprev, a=a, sbr=sbr, merged=merged, w=w)
    return x_out, saved, rode


def mixer_backward(dx_out, x, sv, ride_early=None, ride_late=None):
    s = x.shape[0]
    p = sv["p"]
    w = sv["w"]
    g = {}
    dmerged = matmul_nt(dx_out, w["w_out"], "d_merged", F32, tm=1024, tn=512, tk=1024)
    g["w_out"] = matmul_tn(sv["merged"], dx_out, "dw_out", tn=512, ts=1024)
    da, ds, dga, dgs = merge_bwd(p, sv["a"], sv["sbr"], dmerged, "merge_bwd")
    g["w_attn_branch"] = matmul_tn(sv["attn_o"], da, "dw_attn_branch", tn=512, ts=1024)
    g["w_ssd_branch"] = matmul_tn(sv["s_in"], ds, "dw_ssd_branch", tn=512, ts=1024)
    d_attn_o = matmul_nt(da, w["w_attn_branch"], "d_attn_o", F32, tm=1024, tn=512, tk=1024)
    d_sin = matmul_nt(ds, w["w_ssd_branch"], "d_ssd_in", F32, tm=1024, tn=512, tk=1024)
    dxs, d_b, d_c, dz, ddt, d_asum, d_bias, d_dsk, d_gn = ssd_bwd(
        p, sv["xc"], sv["y"], d_sin, sv["hprev"], sv["dtg"], sv["dtg_t"], sv["params"], w["ssd_norm"], "ssd_bwd")
    dxbc, d_convw, d_convb = conv_bwd(p, sv["cpre"], dxs, d_b, d_c, w["conv_w"], "conv_bwd")
    g["conv_w"] = d_convw[:SSD_CONV]
    g["conv_b"] = d_convb
    g["dt_bias"] = d_bias.reshape(1, SSD_HEADS)
    g["a_log"] = (d_asum * (-jnp.exp(sv["params"][2]))).reshape(1, SSD_HEADS)
    g["d_skip"] = jnp.sum(d_dsk.reshape(SSD_HEADS, HD), axis=1).reshape(1, SSD_HEADS)
    g["ssd_norm"] = d_gn
    merged_bwd = attn_merge_bwd(d_attn_o, sv["os"], sv["lses"], "attn_merge_bwd")
    dqs, dks, dvs = [], [], []
    for gi, d in enumerate(ATTN_DILATIONS):
        qs, ks, vs, lse = sv["streams"][gi]
        d_o = _to_streams(merged_bwd[gi], d)
        cterm = _to_streams(merged_bwd[3 + gi], d)
        dq, dk, dv = attn_bwd2(qs, ks, vs, d_o, cterm, lse, gi, f"attn_bwd{gi}")
        dqs.append(_from_streams(dq, d))
        dks.append(_from_streams(dk, d))
        dvs.append(_from_streams(dv, d).astype(BF16))
    dqk, d_qg, d_kg = qk_norm_bwd(p, dqs, dks, w["q_gain"], w["k_gain"], "qk_norm_bwd")
    g["q_norm"] = jnp.sum(d_qg.reshape(N_ATTN_HEADS, HD), axis=0).reshape(1, HD)
    g["k_norm"] = jnp.sum(d_kg.reshape(N_ATTN_HEADS, HD), axis=0).reshape(1, HD)
    dp = [dqk, jnp.concatenate(dvs, axis=1), dz, dxbc, dga, dgs]
    ddt_pad = jnp.pad(ddt.transpose(1, 0, 2).reshape(s, SSD_HEADS), ((0, 0), (0, DT_PAD - SSD_HEADS)))
    if ride_early is not None:
        g["w_in_main"], g["rode_early"] = matmul_tn_pieces(sv["h"], dp, "dw_in", tn=512, ts=1024,
                                                           rider=ride_early(g))
    else:
        g["w_in_main"] = matmul_tn_pieces(sv["h"], dp, "dw_in", tn=512, ts=1024)
    g["w_in_dt"] = matmul_tn(sv["h"], ddt_pad, "dw_in_dt", tn=DT_PAD, ts=1024)
    if ride_late is not None:
        dh_main, g["rode_late"] = matmul_nt_pieces(dp, w["w_in_main"], "d_mix_h", F32, tm=1024, tn=1024, tk=512,
                                                   rider=ride_late(g))
    else:
        dh_main = matmul_nt_pieces(dp, w["w_in_main"], "d_mix_h", F32, tm=1024, tn=512, tk=512)
    dh_dt = matmul_nt(ddt_pad, w["w_in_dt"], "d_mix_h_dt", F32, tm=1024, tn=1024, tk=DT_PAD)
    dx, g["mix_norm"] = rms_bwd([dh_main, dh_dt], x, w["mix_norm"], dx_out, "mix_drms")
    return dx, g


ANY = pl.BlockSpec(memory_space=pl.ANY)


def _place():
    x, y, c = lax.axis_index("x"), lax.axis_index("y"), lax.axis_index("c")
    chips = [(1 - x, y), (x, 1 - y), (1 - x, 1 - y)]
    return x, y, c, 2 * x + y, chips


def _comm_call(body, *, name, out_shape, n_in, scratch_shapes, aliases=None):
    return pl.pallas_call(
        body, out_shape=out_shape, in_specs=[ANY] * n_in, out_specs=[ANY] * len(out_shape),
        scratch_shapes=scratch_shapes, input_output_aliases=aliases or {}, name=name,
        compiler_params=pltpu.CompilerParams(has_side_effects=True))


def gather_weights(shards, small):
    n = len(shards)
    halves = [a.shape[0] // 2 for a in shards]
    out_shape = [_sds((N_CHIP,) + a.shape, a.dtype) for a in shards] + [_sds((N_CHIP,) + small.shape, small.dtype)]

    def body(*refs):
        ins, outs = refs[:n + 1], refs[n + 1:2 * n + 2]
        send1, recv1, send2, recv2, local = refs[2 * n + 2:]
        x, y, c, me, chips = _place()
        sibling = (x, y, 1 - c)

        def rows(k, chip, core):
            if k == n:
                return outs[k].at[chip]
            return outs[k].at[chip, pl.ds(core * halves[k], halves[k])]

        def level1(k, t, incoming):
            chip = 2 * chips[t][0] + chips[t][1]
            src = ins[k] if k == n else ins[k].at[pl.ds(c * halves[k], halves[k])]
            return pltpu.make_async_remote_copy(
                src_ref=src, dst_ref=rows(k, chip if incoming else me, c), send_sem=send1.at[3 * k + t],
                recv_sem=recv1.at[3 * k + t], device_id=(*chips[t], c), device_id_type=MESH)

        def level2(k, t, incoming):
            chip = 2 * chips[t][0] + chips[t][1]
            core = (1 - c) if incoming else c
            return pltpu.make_async_remote_copy(
                src_ref=rows(k, chip, core), dst_ref=rows(k, chip, core), send_sem=send2.at[3 * k + t],
                recv_sem=recv2.at[3 * k + t], device_id=sibling, device_id_type=MESH)

        own = [pltpu.make_async_copy(ins[k], outs[k].at[me], local.at[k]) for k in range(n + 1)]
        for cp in own:
            cp.start()
        first = [level1(k, t, False) for k in range(n + 1) for t in range(3)]
        for cp in first:
            cp.start()
        passed = []
        for k in range(n + 1):
            for t in range(3):
                level1(k, t, True).wait_recv()
                if k < n:
                    cp = level2(k, t, False)
                    cp.start()
                    passed.append(cp)
        for k in range(n):
            for t in range(3):
                level2(k, t, True).wait_recv()
        for cp in first + passed:
            cp.wait_send()
        for cp in own:
            cp.wait()

    dma = pltpu.SemaphoreType.DMA
    return _comm_call(body, name="gather_weights", out_shape=out_shape, n_in=n + 1,
                      scratch_shapes=[dma((3 * n + 3,)), dma((3 * n + 3,)), dma((3 * n,)), dma((3 * n,)),
                                      dma((n + 1,))])(*shards, small)


def reduce_to_sibling(grads):
    n = len(grads)
    halves = [a.shape[1] // 2 for a in grads]
    shapes = [_sds((N_CHIP, h, a.shape[2]), a.dtype) for a, h in zip(grads, halves)]

    def body(*refs):
        ins, got, kept = refs[:n], refs[n:2 * n], refs[2 * n:3 * n]
        send, recv, local = refs[3 * n:]
        x, y, c, _, _ = _place()
        copies, locals_ = [], []
        for k in range(n):
            h = halves[k]
            locals_.append(pltpu.make_async_copy(ins[k].at[:, pl.ds(c * h, h)], kept[k], local.at[k]))
            copies.append(pltpu.make_async_remote_copy(
                src_ref=ins[k].at[:, pl.ds((1 - c) * h, h)], dst_ref=got[k], send_sem=send.at[k], recv_sem=recv.at[k],
                device_id=(x, y, 1 - c), device_id_type=MESH))
        for cp in locals_ + copies:
            cp.start()
        for cp in copies:
            cp.wait_recv()
        for cp in copies:
            cp.wait_send()
        for cp in locals_:
            cp.wait()

    dma = pltpu.SemaphoreType.DMA
    res = _comm_call(body, name="reduce_to_sibling", out_shape=shapes + shapes, n_in=n,
                     scratch_shapes=[dma((n,)), dma((n,)), dma((n,))])(*grads)
    return res[:n], res[n:]


def reduce_to_owner(sums):
    n = len(sums)
    shapes = [_sds(a.shape, a.dtype) for a in sums]

    def body(*refs):
        ins, outs = refs[:n], refs[n:2 * n]
        send, recv, local = refs[2 * n:]
        x, y, c, me, chips = _place()
        copies, locals_ = [], []
        for k in range(n):
            locals_.append(pltpu.make_async_copy(ins[k].at[me], outs[k].at[3], local.at[k]))
            for t in range(3):
                chip = 2 * chips[t][0] + chips[t][1]
                copies.append(pltpu.make_async_remote_copy(
                    src_ref=ins[k].at[chip], dst_ref=outs[k].at[t], send_sem=send.at[3 * k + t],
                    recv_sem=recv.at[3 * k + t], device_id=(*chips[t], c), device_id_type=MESH))
        for cp in locals_ + copies:
            cp.start()
        for cp in copies:
            cp.wait_recv()
        for cp in copies:
            cp.wait_send()
        for cp in locals_:
            cp.wait()

    dma = pltpu.SemaphoreType.DMA
    return _comm_call(body, name="reduce_to_owner", out_shape=shapes, n_in=n,
                      scratch_shapes=[dma((3 * n,)), dma((3 * n,)), dma((n,))])(*sums)


def share_with_sibling(halves_):
    n = len(halves_)
    shapes = [_sds((2 * a.shape[0], a.shape[1]), a.dtype) for a in halves_]

    def body(*refs):
        ins, outs = refs[:n], refs[n:2 * n]
        send, recv, local = refs[2 * n:]
        x, y, c, _, _ = _place()
        copies, locals_ = [], []
        for k in range(n):
            h = ins[k].shape[0]
            mine = outs[k].at[pl.ds(c * h, h)]
            locals_.append(pltpu.make_async_copy(ins[k], mine, local.at[k]))
            copies.append(pltpu.make_async_remote_copy(
                src_ref=ins[k], dst_ref=mine, send_sem=send.at[k], recv_sem=recv.at[k],
                device_id=(x, y, 1 - c), device_id_type=MESH))
        for cp in locals_ + copies:
            cp.start()
        for cp in copies:
            cp.wait_recv()
        for cp in copies:
            cp.wait_send()
        for cp in locals_:
            cp.wait()

    dma = pltpu.SemaphoreType.DMA
    return _comm_call(body, name="share_with_sibling", out_shape=shapes, n_in=n,
                      scratch_shapes=[dma((n,)), dma((n,)), dma((n,))])(*halves_)


def _cores():
    c = lax.axis_index("c")
    return jnp.stack([c, 1 - c]).astype(jnp.int32)


def _staged_call(body, *, name, grid, in_specs, out_specs, out_shape, scratch_shapes):
    return pl.pallas_call(
        body, out_shape=out_shape, name=name,
        grid_spec=pltpu.PrefetchScalarGridSpec(num_scalar_prefetch=1, grid=grid, in_specs=in_specs,
                                               out_specs=out_specs, scratch_shapes=scratch_shapes),
        compiler_params=pltpu.CompilerParams(dimension_semantics=("arbitrary",) * len(grid),
                                             vmem_limit_bytes=V7X_VMEM_LIMIT, has_side_effects=True))


def gather_rider(shards, tiles):
    dma = pltpu.SemaphoreType.DMA
    n = len(shards)
    geo = [(a.shape[0] // 2, tm, (a.shape[0] // 2) // tm) for a, tm in zip(shards, tiles)]
    scratch = []
    for a, (h, tm, nk) in zip(shards, geo):
        scratch += [pltpu.VMEM((N_CHIP,) + a.shape, a.dtype), dma((3, nk)), dma((3, nk)), dma((3, nk)), dma((3, nk)),
                    dma((nk + 2,))]

    def copies(j, in_ref, scr):
        buf, send1, recv1, send2, recv2, local = scr[6 * j:6 * j + 6]
        h, tm, nk = geo[j]
        x, y, c, me, chips = _place()
        chip_of = [2 * chips[t][0] + chips[t][1] for t in range(3)]

        def rows(chip, core, k):
            return buf.at[chip, pl.ds(core * h + k * tm, tm)]

        def mine(k):
            if k == nk:
                return pltpu.make_async_copy(in_ref.at[pl.ds((1 - c) * h, h)], buf.at[me, pl.ds((1 - c) * h, h)],
                                             local.at[nk])
            return pltpu.make_async_copy(in_ref.at[pl.ds(c * h + k * tm, tm)], rows(me, c, k), local.at[k])

        def level1(t, k, incoming):
            place = rows(chip_of[t] if incoming else me, c, k)
            return pltpu.make_async_remote_copy(src_ref=place, dst_ref=place, send_sem=send1.at[t, k],
                                                recv_sem=recv1.at[t, k], device_id=(*chips[t], c), device_id_type=MESH)

        def level2(t, k, incoming):
            place = rows(chip_of[t], (1 - c) if incoming else c, k)
            return pltpu.make_async_remote_copy(src_ref=place, dst_ref=place, send_sem=send2.at[t, k],
                                                recv_sem=recv2.at[t, k], device_id=(x, y, 1 - c),
                                                device_id_type=MESH)

        return buf, local, nk, mine, level1, level2

    def start(ins, outs, scr):
        for j in range(n):
            _, _, nk, mine, _, _ = copies(j, ins[j], scr)
            for k in range(nk + 1):
                mine(k).start()
        for j in range(n):
            _, _, nk, mine, level1, _ = copies(j, ins[j], scr)
            for k in range(nk):
                mine(k).wait()
                for t in range(3):
                    level1(t, k, False).start()

    def finish(ins, outs, scr):
        for j in range(n):
            _, _, nk, _, level1, level2 = copies(j, ins[j], scr)
            for k in range(nk):
                for t in range(3):
                    level1(t, k, True).wait_recv()
                    level2(t, k, False).start()
        for j in range(n):
            buf, local, nk, mine, level1, level2 = copies(j, ins[j], scr)
            for k in range(nk):
                for t in range(3):
                    level2(t, k, True).wait_recv()
            for k in range(nk):
                for t in range(3):
                    level1(t, k, False).wait_send()
                    level2(t, k, False).wait_send()
            mine(nk).wait()
            pltpu.make_async_copy(buf, outs[j], local.at[nk + 1]).start()
        for j in range(n):
            buf, local, nk, _, _, _ = copies(j, ins[j], scr)
            pltpu.make_async_copy(buf, outs[j], local.at[nk + 1]).wait()

    return Rider(list(shards), [_sds((N_CHIP,) + a.shape, a.dtype) for a in shards], scratch, start, finish)


def run_alone(rider, name):
    return _call(lambda: None, name=name, out_shape=[], in_specs=[], out_specs=[], grid=(1,), rider=rider)()[1]


def sibling_sum(g, tm, name):
    _, r, cdim = g.shape
    h = r // 2
    ni = h // tm
    dma = pltpu.SemaphoreType.DMA

    def body(cores_ref, keep_ref, give_ref, out_ref, slot, send, recv):
        par = (pl.program_id(0) * ni + pl.program_id(1)) % 2
        x, y, c, _, _ = _place()
        cp = pltpu.make_async_remote_copy(src_ref=give_ref, dst_ref=slot.at[par], send_sem=send.at[par],
                                          recv_sem=recv.at[par], device_id=(x, y, 1 - c), device_id_type=MESH)
        cp.start()
        cp.wait_recv()
        out_ref[...] = (keep_ref[...].astype(F32) + slot[par].astype(F32)).astype(out_ref.dtype)
        cp.wait_send()

    flat = g.reshape(N_CHIP * r, cdim)
    return _staged_call(
        body, name=name, grid=(N_CHIP, ni),
        in_specs=[pl.BlockSpec((tm, cdim), lambda j, i, cores: ((2 * j + cores[0]) * ni + i, 0)),
                  pl.BlockSpec((tm, cdim), lambda j, i, cores: ((2 * j + cores[1]) * ni + i, 0))],
        out_specs=pl.BlockSpec((None, tm, cdim), lambda j, i, cores: (j, i, 0)),
        out_shape=_sds((N_CHIP, h, cdim), g.dtype),
        scratch_shapes=[pltpu.VMEM((2, tm, cdim), g.dtype), dma((2,)), dma((2,))],
    )(_cores(), flat, flat)


def owner_sum_rider(sums, tiles):
    dma = pltpu.SemaphoreType.DMA
    n = len(sums)
    geo = [(a.shape[1], tm, a.shape[1] // tm) for a, tm in zip(sums, tiles)]
    scratch = []
    for a, (h, tm, nk) in zip(sums, geo):
        cdim = a.shape[2]
        scratch += [pltpu.VMEM(a.shape, a.dtype), pltpu.VMEM((3, h, cdim), a.dtype), pltpu.VMEM((2, h, cdim), F32),
                    dma((3, nk)), dma((3, nk)), dma((nk,)), dma((nk,)), dma((2,))]

    def copies(j, scr):
        part, got, res, send, recv, send2, recv2, local = scr[8 * j:8 * j + 8]
        h, tm, nk = geo[j]
        x, y, c, me, chips = _place()

        def to_owner(t, k):
            chip = 2 * chips[t][0] + chips[t][1]
            return pltpu.make_async_remote_copy(
                src_ref=part.at[chip, pl.ds(k * tm, tm)], dst_ref=got.at[t, pl.ds(k * tm, tm)],
                send_sem=send.at[t, k], recv_sem=recv.at[t, k], device_id=(*chips[t], c), device_id_type=MESH)

        def to_sibling(k):
            place = res.at[c, pl.ds(k * tm, tm)]
            return pltpu.make_async_remote_copy(src_ref=place, dst_ref=place, send_sem=send2.at[k],
                                                recv_sem=recv2.at[k], device_id=(x, y, 1 - c), device_id_type=MESH)

        return part, got, res, local, to_owner, to_sibling, (tm, nk, c, me)

    def start(ins, outs, scr):
        for j in range(n):
            part, _, _, local, _, _, _ = copies(j, scr)
            pltpu.make_async_copy(ins[j], part, local.at[0]).start()
        for j in range(n):
            part, _, _, local, to_owner, _, (tm, nk, c, me) = copies(j, scr)
            pltpu.make_async_copy(ins[j], part, local.at[0]).wait()
            for k in range(nk):
                for t in range(3):
                    to_owner(t, k).start()

    def finish(ins, outs, scr):
        for j in range(n):
            part, got, res, _, to_owner, to_sibling, (tm, nk, c, me) = copies(j, scr)
            for k in range(nk):
                rows = pl.ds(k * tm, tm)
                for t in range(3):
                    to_owner(t, k).wait_recv()
                acc = part[me, rows, :].astype(F32)
                for t in range(3):
                    acc = acc + got[t, rows, :].astype(F32)
                res[c, rows, :] = acc
                to_sibling(k).start()
        for j in range(n):
            _, _, res, local, to_owner, to_sibling, (tm, nk, c, me) = copies(j, scr)
            for k in range(nk):
                to_sibling(k).wait_recv()
            for k in range(nk):
                to_sibling(k).wait_send()
                for t in range(3):
                    to_owner(t, k).wait_send()
            pltpu.make_async_copy(res, outs[j], local.at[1]).start()
        for j in range(n):
            _, _, res, local, _, _, _ = copies(j, scr)
            pltpu.make_async_copy(res, outs[j], local.at[1]).wait()

    return Rider(list(sums), [_sds((2, a.shape[1], a.shape[2]), F32) for a in sums], scratch, start, finish)


def gather_conv_w(w):
    def body(in_ref, out_ref, send, recv):
        x, y, c, me, chips = _place()
        out_ref[me] = in_ref[...]
        copies = []
        for t in range(3):
            copies.append(pltpu.make_async_remote_copy(
                src_ref=out_ref.at[me], dst_ref=out_ref.at[me], send_sem=send.at[t], recv_sem=recv.at[t],
                device_id=(*chips[t], c), device_id_type=MESH))
        for cp in copies:
            cp.start()
        for cp in copies:
            cp.wait_recv()
        for cp in copies:
            cp.wait_send()

    dma = pltpu.SemaphoreType.DMA
    vmem = pl.BlockSpec(memory_space=pltpu.VMEM)
    return pl.pallas_call(
        body, out_shape=_sds((N_CHIP,) + w.shape, w.dtype), in_specs=[vmem], out_specs=vmem, name="gather_conv_w",
        scratch_shapes=[dma((3,)), dma((3,))],
        compiler_params=pltpu.CompilerParams(has_side_effects=True))(w)


N_DEV = 8
SMALL_ROWS = 32
SMALL_LANES = 1024


def all_reduce_small(arrays):
    n_arr = len(arrays)
    places = []
    for k, a in enumerate(arrays):
        for ri in range(a.shape[0]):
            for c0 in range(0, a.shape[1], SMALL_LANES):
                places.append((k, ri, c0, min(SMALL_LANES, a.shape[1] - c0), len(places)))
    assert len(places) <= SMALL_ROWS

    def body(*refs):
        ins, outs = refs[:n_arr], refs[n_arr:2 * n_arr]
        buf, send, recv = refs[2 * n_arr:]
        x, y, c, _, _ = _place()
        me = 4 * x + 2 * y + c
        buf[me] = jnp.zeros((SMALL_ROWS, SMALL_LANES), F32)
        for k, ri, c0, width, row in places:
            buf[me, row:row + 1, 0:width] = ins[k][ri:ri + 1, c0:c0 + width]
        copies = []
        for r in range(1, N_DEV):
            px = (1 - x) if r & 4 else x
            py = (1 - y) if r & 2 else y
            pc = (1 - c) if r & 1 else c
            copies.append(pltpu.make_async_remote_copy(
                src_ref=buf.at[me], dst_ref=buf.at[me], send_sem=send.at[r - 1], recv_sem=recv.at[r - 1],
                device_id=(px, py, pc), device_id_type=MESH))
        for cp in copies:
            cp.start()
        for cp in copies:
            cp.wait_recv()
        for cp in copies:
            cp.wait_send()
        acc = buf[0]
        for j in range(1, N_DEV):
            acc = acc + buf[j]
        for k, ri, c0, width, row in places:
            outs[k][ri:ri + 1, c0:c0 + width] = acc[row:row + 1, 0:width]

    dma = pltpu.SemaphoreType.DMA
    vmem = pl.BlockSpec(memory_space=pltpu.VMEM)
    return pl.pallas_call(
        body, out_shape=[_sds(a.shape, F32) for a in arrays], in_specs=[vmem] * n_arr, out_specs=[vmem] * n_arr,
        name="all_reduce_small",
        scratch_shapes=[pltpu.VMEM((N_DEV, SMALL_ROWS, SMALL_LANES), F32), dma((N_DEV - 1,)), dma((N_DEV - 1,))],
        compiler_params=pltpu.CompilerParams(has_side_effects=True))(*arrays)


def _row_tile(rows, limit, multiple):
    return max(t for t in range(multiple, min(rows, limit) + 1, multiple) if rows % t == 0)


def add_pair(a, b, name):
    _, h, c = a.shape

    def body(a_ref, b_ref, o_ref):
        o_ref[...] = (a_ref[...].astype(F32) + b_ref[...].astype(F32)).astype(o_ref.dtype)

    blk = pl.BlockSpec((None, h, c), lambda j: (j, 0, 0))
    return _call(body, name=name, out_shape=_sds(a.shape, a.dtype), grid=(N_CHIP,), in_specs=[blk, blk],
                 out_specs=blk)(a, b)


def sum_slots(buf, name):
    _, h, c = buf.shape
    tm = _row_tile(h, 256, 16)

    def body(b_ref, o_ref):
        acc = b_ref[3].astype(F32)
        for t in range(3):
            acc = acc + b_ref[t].astype(F32)
        o_ref[...] = acc

    return _call(body, name=name, out_shape=_sds((h, c), F32), grid=(h // tm,),
                 in_specs=[pl.BlockSpec((N_CHIP, tm, c), lambda i: (0, i, 0))],
                 out_specs=pl.BlockSpec((tm, c), lambda i: (i, 0)))(buf)


def _adamw_math(w, g, m, v):
    c1 = 1.0 - ADAM_B1 ** ADAM_STEP
    c2 = 1.0 - ADAM_B2 ** ADAM_STEP
    m2 = ADAM_B1 * m + (1.0 - ADAM_B1) * g
    v2 = ADAM_B2 * v + (1.0 - ADAM_B2) * (g * g)
    return -ADAM_LR * ((m2 / c1) / (jnp.sqrt(v2 / c2) + ADAM_EPS) + ADAM_WD * w), m2, v2


def adamw(w, g, row_off, m, v, name):
    _, r, c = w.shape
    tm = r if r < 8 else _row_tile(math.gcd(r, row_off) if row_off else r, 128, 8)

    def body(w_ref, g_ref, m_ref, v_ref, go_ref, d_ref, m2_ref, v2_ref):
        gv = g_ref[...]
        go_ref[...] = gv
        d_ref[...], m2_ref[...], v2_ref[...] = _adamw_math(w_ref[...], gv, m_ref[...], v_ref[...])

    blk = pl.BlockSpec((None, tm, c), lambda i: (0, i, 0))
    shp = _sds((1, r, c), F32)
    return _call(body, name=name, out_shape=[shp] * 4, grid=(r // tm,),
                 in_specs=[blk, pl.BlockSpec((tm, c), lambda i: (row_off // tm + i, 0)), blk, blk],
                 out_specs=[blk] * 4)(w, g, m, v)


def adamw_small(ws, gs, ms, vs):
    n = len(ws)

    def body(*refs):
        ins, outs = refs[:4 * n], refs[4 * n:]
        for k in range(n):
            w_ref, g_ref, m_ref, v_ref = (ins[j * n + k] for j in range(4))
            outs[k][...], outs[n + k][...], outs[2 * n + k][...] = _adamw_math(w_ref[...], g_ref[...], m_ref[...],
                                                                               v_ref[...])

    vmem = pl.BlockSpec(memory_space=pltpu.VMEM)
    shapes = [_sds(w.shape, F32) for w in ws] * 3
    res = pl.pallas_call(body, out_shape=shapes, in_specs=[vmem] * (4 * n), out_specs=[vmem] * (3 * n),
                         name="adamw_small")(*ws, *gs, *ms, *vs)
    return res[:n], res[n:2 * n], res[2 * n:]


BIG = ("ffn1_w_gate", "ffn1_w_up", "ffn1_w_down", "w_in", "w_attn_branch", "w_ssd_branch", "w_out",
       "ffn2_w_gate", "ffn2_w_up", "ffn2_w_down")
SMALL = ("ffn1_norm", "mix_norm", "q_norm", "k_norm", "conv_b", "dt_bias", "a_log", "d_skip", "ssd_norm", "ffn2_norm")
WEIGHTS = ("ffn1_norm", "ffn1_w_gate", "ffn1_w_up", "ffn1_w_down", "mix_norm", "w_in", "q_norm", "k_norm", "conv_w",
           "conv_b", "dt_bias", "a_log", "d_skip", "ssd_norm", "w_attn_branch", "w_ssd_branch", "w_out", "ffn2_norm",
           "ffn2_w_gate", "ffn2_w_up", "ffn2_w_down")
CONV_SHARD = SSD_CONV_DIM // N_CHIP
CLASSES = {
    "ffn1_in": (("ffn1_w_gate", 1024), ("ffn1_w_up", 1024)),
    "ffn1_out": (("ffn1_w_down", 704),),
    "mix_in": (("w_in", 1024),),
    "mix_attn": (("w_attn_branch", 512),),
    "late_out": (("ffn2_w_down", 704), ("w_ssd_branch", 512), ("w_out", 256)),
    "ffn2_in": (("ffn2_w_gate", 1024), ("ffn2_w_up", 1024)),
}
CLASS_TILE = {"ffn1_in": 256, "ffn1_out": 176, "mix_in": 128, "mix_attn": 256, "late_out": 368, "ffn2_in": 256,
              "mix_in_top": 128, "mix_in_bottom": 128}
SIBLING_TILE = {"ffn1_in": 1024, "ffn1_out": 352, "mix_attn": 256, "late_out": 736, "ffn2_in": 1024,
                "mix_in_top": 256, "mix_in_bottom": 256}


def _pack_small(vals, conv_part, loss_part=None):
    flat = [vals[k].reshape(-1) for k in SMALL]
    flat.append(jnp.zeros((SSD_CONV * SSD_CONV_DIM,), F32) if conv_part is None else conv_part.reshape(-1))
    flat.append(jnp.zeros((1,), F32) if loss_part is None else loss_part.reshape(1))
    flat = jnp.concatenate(flat)
    return jnp.pad(flat, (0, SMALL_ROWS * D_MODEL - flat.shape[0])).reshape(SMALL_ROWS, D_MODEL)


def _unpack_small(pack, like):
    flat = pack.reshape(-1)
    out, off = {}, 0
    for k in SMALL:
        n = like[k].size
        out[k] = flat[off:off + n].reshape(like[k].shape)
        off += n
    conv = flat[off:off + SSD_CONV * SSD_CONV_DIM].reshape(SSD_CONV, SSD_CONV_DIM)
    return out, conv, flat[off + SSD_CONV * SSD_CONV_DIM]


def _chip_major_cols(a):
    r = a.shape[0]
    return a.reshape(r, N_CHIP, -1).transpose(1, 0, 2)


def _from_chip_major_cols(a):
    return a.transpose(1, 0, 2).reshape(a.shape[1], -1)


def kernel(x, ffn1_norm, ffn1_w_gate, ffn1_w_up, ffn1_w_down, mix_norm, w_in, q_norm, k_norm, conv_w, conv_b, dt_bias, a_log, d_skip, ssd_norm, w_attn_branch, w_ssd_branch, w_out, ffn2_norm, ffn2_w_gate, ffn2_w_up, ffn2_w_down, loss_target, m_ffn1_norm, m_ffn1_w_gate, m_ffn1_w_up, m_ffn1_w_down, m_mix_norm, m_w_in, m_q_norm, m_k_norm, m_conv_w, m_conv_b, m_dt_bias, m_a_log, m_d_skip, m_ssd_norm, m_w_attn_branch, m_w_ssd_branch, m_w_out, m_ffn2_norm, m_ffn2_w_gate, m_ffn2_w_up, m_ffn2_w_down, v_ffn1_norm, v_ffn1_w_gate, v_ffn1_w_up, v_ffn1_w_down, v_mix_norm, v_w_in, v_q_norm, v_k_norm, v_conv_w, v_conv_b, v_dt_bias, v_a_log, v_d_skip, v_ssd_norm, v_w_attn_branch, v_w_ssd_branch, v_w_out, v_ffn2_norm, v_ffn2_w_gate, v_ffn2_w_up, v_ffn2_w_down):
    env = dict(locals())
    wts = {k: env[k] for k in WEIGHTS}
    moms = {k: env["m_" + k] for k in WEIGHTS}
    vars_ = {k: env["v_" + k] for k in WEIGHTS}
    x0 = x[0]
    target = loss_target[0]

    def gather(classes):
        shards = [jnp.concatenate([wts[k][0] for k, _ in CLASSES[c]], axis=0).astype(BF16) for c in classes]
        return gather_rider(shards, [CLASS_TILE[c] for c in classes])

    def reducer(classes, parts):
        sums = [sibling_sum(p, SIBLING_TILE[c], f"sibling_sum_{c}") for c, p in zip(classes, parts)]
        return owner_sum_rider(sums, [CLASS_TILE[c] for c in classes])

    x1, saved1, (w_ffn1_in,), (w_ffn1_out, w_mix_in, w_mix_attn) = ffn_forward(
        x0, ffn1_norm, lambda rode: rode[0], lambda rode: rode[0], "ffn1", rms_rider=gather(["ffn1_in"]),
        up_rider=gather(["ffn1_out", "mix_in", "mix_attn"]))
    dt0, dt1 = IN_DT0 - 3 * IN_SHARD, IN_DT1 - 3 * IN_SHARD
    mixer_w = dict(
        mix_norm=mix_norm,
        w_in_main=jnp.concatenate([w_mix_in[0], w_mix_in[1], w_mix_in[2], w_mix_in[3][:, :dt0], w_mix_in[3][:, dt1:]],
                                  axis=1),
        w_in_dt=jnp.pad(w_mix_in[3][:, dt0:dt1], ((0, 0), (0, DT_PAD - SSD_HEADS))),
        q_gain=jnp.tile(q_norm, (1, 2)), k_gain=jnp.tile(k_norm, (1, 2)),
        conv_w=_from_chip_major_cols(gather_conv_w(conv_w[0])), conv_b=conv_b, dt_bias=dt_bias, a_log=a_log,
        d_skip=d_skip, ssd_norm=ssd_norm, w_attn_branch=_from_chip_major_cols(w_mix_attn))

    def later_weights(rode):
        late = rode[0]
        return dict(w_ssd_branch=late[:, 704:1216].reshape(SSD_INNER, D_MODEL),
                    w_out=late[:, 1216:1472].reshape(D_MODEL, D_MODEL))

    x2, saved_mix, (w_late_out, w_ffn2_in) = mixer_forward(x1, mixer_w, gather(["late_out", "ffn2_in"]), later_weights)
    x3, saved2, _, _ = ffn_forward(x2, ffn2_norm, lambda rode: w_ffn2_in, lambda rode: w_late_out, "ffn2")
    dx3, sq = loss_grad(x3, target, "loss")

    grads = {}
    dx2, grads["ffn2_norm"], d_ffn2_in, d_ffn2_down = ffn_backward(dx3, x2, ffn2_norm, w_ffn2_in, w_late_out, saved2,
                                                                   "ffn2")

    def ride_early(g):
        late = jnp.concatenate([d_ffn2_down, g["w_ssd_branch"].reshape(N_CHIP, -1, D_MODEL),
                                g["w_out"].reshape(N_CHIP, -1, D_MODEL)], axis=1)
        return reducer(["ffn2_in", "late_out"], [d_ffn2_in, late])

    g_in_rows = {}

    def ride_late(g):
        main = g["w_in_main"]
        last = jnp.concatenate([main[:, 3 * IN_SHARD:IN_DT0], g["w_in_dt"][:, :SSD_HEADS], main[:, IN_DT0:]], axis=1)
        for part, rows in (("top", slice(0, D_MODEL // 2)), ("bottom", slice(D_MODEL // 2, D_MODEL))):
            g_in_rows[part] = jnp.stack([main[rows, j * IN_SHARD:(j + 1) * IN_SHARD] for j in range(3)]
                                        + [last[rows]])
        return reducer(["mix_in_top", "mix_attn"], [g_in_rows["top"], _chip_major_cols(g["w_attn_branch"])])

    dx1, gmix = mixer_backward(dx2, x1, saved_mix, ride_early, ride_late)
    dx0, grads["ffn1_norm"], rode_in, rode_out = ffn_backward(
        dx1, x0, ffn1_norm, w_ffn1_in, w_ffn1_out, saved1, "ffn1",
        ride_down=lambda d: reducer(["mix_in_bottom", "ffn1_out"], [g_in_rows["bottom"], d]),
        ride_in=lambda d: reducer(["ffn1_in"], [d]))
    for k in ("mix_norm", "q_norm", "k_norm", "conv_b", "dt_bias", "a_log", "d_skip", "ssd_norm"):
        grads[k] = gmix[k]
    reduced = dict(zip(("ffn2_in", "late_out", "mix_in_top", "mix_attn", "ffn1_in", "mix_in_bottom", "ffn1_out"),
                       (*gmix["rode_early"], *gmix["rode_late"], rode_in[0], *rode_out)))
    reduced = {c: r.reshape(-1, r.shape[2]) for c, r in reduced.items()}
    reduced["mix_in"] = jnp.concatenate([reduced.pop("mix_in_top"), reduced.pop("mix_in_bottom")], axis=0)
    summed = all_reduce_small([grads[k] for k in SMALL]
                              + [gmix["conv_w"], (0.5 * jnp.sum(sq) / D_MODEL).reshape(1, 1)])
    g_small = dict(zip(SMALL, summed))
    loss = summed[-1].reshape(())
    chip = 2 * lax.axis_index("x") + lax.axis_index("y")
    g_conv = lax.dynamic_slice_in_dim(summed[-2], chip * CONV_SHARD, CONV_SHARD, axis=1)

    g_final, delta, new_m, new_v = dict(g_small), {}, {}, {}

    def update(k, g_arr, row_off):
        w, m, v = wts[k], moms[k], vars_[k]
        rows, cols = w.shape[1:]
        if cols % 128:
            res = adamw(jnp.swapaxes(w, 1, 2), g_arr[row_off:row_off + rows].T, 0, jnp.swapaxes(m, 1, 2),
                        jnp.swapaxes(v, 1, 2), f"adamw_{k}")
            res = [jnp.swapaxes(r, 1, 2) for r in res]
        else:
            res = adamw(w, g_arr, row_off, m, v, f"adamw_{k}")
        g_final[k], delta[k], new_m[k], new_v[k] = res

    for cls, members in CLASSES.items():
        off = 0
        for k, rows in members:
            update(k, reduced[cls], off)
            off += rows
    update("conv_w", g_conv, 0)
    small = adamw_small(*([d[k] for k in SMALL] for d in (wts, g_small, moms, vars_)))
    for res, vals in zip((delta, new_m, new_v), small):
        res.update(zip(SMALL, vals))

    return (loss, dx0[None], *[g_final[k] for k in WEIGHTS], *[delta[k] for k in WEIGHTS],
            *[new_m[k] for k in WEIGHTS], *[new_v[k] for k in WEIGHTS])
```

```python
import collections
import functools
import math

import jax
import jax.numpy as jnp
from jax import lax
from jax.experimental import pallas as pl
from jax.experimental.pallas import tpu as pltpu

F32 = jnp.float32
BF16 = jnp.bfloat16
MESH = pl.DeviceIdType.MESH

EPS = 1e-6
D_MODEL = 1024
D_FF = 2816
N_CHIP = 4
FF_SHARD = D_FF // N_CHIP
HD = 64
BLK = 128
ATTN_DILATIONS = (1, 4, 16)
HEADS_PER_PATTERN = 8
N_ATTN_HEADS = 24
ALIBI_MAX_EXP = 8.0
ATTN_QKV = 1536
GROUP_W = 512
SSD_INNER = 2048
SSD_HEADS = 32
SSD_GROUPS = 4
SSD_CONV = 4
SSD_CONV_DIM = 3072
IN_COLS = 11808
IN_DT0, IN_DT1 = 9728, 9760
IN_SHARD = IN_COLS // 4
COL_K, COL_V, COL_Z, COL_XBC, COL_GA, COL_GS, P_COLS = 1536, 3072, 4608, 6656, 9728, 10752, 11776
DT_PAD = 128

ADAM_LR, ADAM_B1, ADAM_B2, ADAM_EPS, ADAM_WD, ADAM_STEP = 0.001, 0.9, 0.999, 1e-08, 0.01, 10

V7X_VMEM_LIMIT = 56 * 1024 * 1024
NEG = -1e30


Rider = collections.namedtuple("Rider", "arrays out_shape scratch start finish")
Rider.__doc__ = """An exchange between devices that rides in a compute kernel: its copies are started in the host's
first grid step and waited for in its last, so they travel while the host computes.  arrays / out_shape: extra HBM
operands and results; scratch: extra scratch; start, finish: f(in_refs, out_refs, scratch_refs)."""


def _call(body, *, name, out_shape, in_specs, out_specs, grid=(), scratch_shapes=(), aliases=None, rider=None):
    params = dict(dimension_semantics=("arbitrary",) * len(grid), vmem_limit_bytes=V7X_VMEM_LIMIT)
    if rider is None:
        return pl.pallas_call(
            body, out_shape=out_shape, grid=grid, in_specs=in_specs, out_specs=out_specs,
            scratch_shapes=scratch_shapes, input_output_aliases=aliases or {}, name=name,
            compiler_params=pltpu.CompilerParams(**params))
    single = not isinstance(out_shape, (list, tuple))
    main_out = [out_shape] if single else list(out_shape)
    main_specs = [out_specs] if single else list(out_specs)
    n_in, n_out, n_scr = len(in_specs), len(main_out), len(scratch_shapes)
    r_in, r_out = len(rider.arrays), len(rider.out_shape)

    def wrapped(*refs):
        ins, refs = refs[:n_in], refs[n_in:]
        r_ins, refs = refs[:r_in], refs[r_in:]
        outs, refs = refs[:n_out], refs[n_out:]
        r_outs, refs = refs[:r_out], refs[r_out:]
        scr, r_scr = refs[:n_scr], refs[n_scr:]
        first = last = None
        for axis, size in enumerate(grid):
            at_start, at_end = pl.program_id(axis) == 0, pl.program_id(axis) == size - 1
            first = at_start if first is None else jnp.logical_and(first, at_start)
            last = at_end if last is None else jnp.logical_and(last, at_end)

        @pl.when(first)
        def _():
            rider.start(r_ins, r_outs, r_scr)

        body(*ins, *outs, *scr)

        @pl.when(last)
        def _():
            rider.finish(r_ins, r_outs, r_scr)

    hbm = pl.BlockSpec(memory_space=pl.ANY)
    call = pl.pallas_call(
        wrapped, out_shape=main_out + list(rider.out_shape), grid=grid, in_specs=list(in_specs) + [hbm] * r_in,
        out_specs=main_specs + [hbm] * r_out, scratch_shapes=list(scratch_shapes) + list(rider.scratch), name=name,
        compiler_params=pltpu.CompilerParams(has_side_effects=True, **params))

    def run(*args):
        res = call(*args, *rider.arrays)
        main = res[:n_out]
        return (main[0] if single else main), res[n_out:]

    return run


def _sds(shape, dtype):
    return jax.ShapeDtypeStruct(tuple(shape), dtype)


def _dot(a, b):
    return jnp.dot(a, b, preferred_element_type=F32)


def _dot_nt(a, b):
    return lax.dot_general(a, b, (((1,), (1,)), ((), ())), preferred_element_type=F32)


def _dot_tn(a, b):
    return lax.dot_general(a, b, (((0,), (0,)), ((), ())), preferred_element_type=F32)


def _dot_hi(a, b):
    return jnp.dot(a, b, preferred_element_type=F32, precision=lax.Precision.HIGHEST)


def _sigmoid(x):
    return 1.0 / (1.0 + jnp.exp(-x))


def _lane_first_half(shape):
    return lax.broadcasted_iota(jnp.int32, shape, len(shape) - 1) < HD


def _pair_sum(x, first):
    s_all = jnp.sum(x, axis=-1, keepdims=True)
    s_a = jnp.sum(jnp.where(first, x, 0.0), axis=-1, keepdims=True)
    return s_a, s_all - s_a


def _rowwise(name, fn, rows, consts, outs, accs=(), tm=512, rider=None):
    n_rows = None
    in_arrays, in_specs = [], []
    for r in rows:
        if isinstance(r, tuple):
            arr, w, cb = r
            spec = pl.BlockSpec((tm, w), functools.partial(lambda i, cb: (i, cb), cb=cb))
        else:
            arr = r
            spec = pl.BlockSpec((tm, arr.shape[1]), lambda i: (i, 0))
        n_rows = arr.shape[0]
        in_arrays.append(arr)
        in_specs.append(spec)
    for c in consts:
        in_arrays.append(c)
        in_specs.append(pl.BlockSpec(c.shape, functools.partial(lambda i, n: (0,) * n, n=c.ndim)))
    out_shape = [_sds(s, d) for s, d in outs] + [_sds(s, d) for s, d in accs]
    out_specs = [pl.BlockSpec((tm, s[1]), lambda i: (i, 0)) for s, _ in outs]
    out_specs += [pl.BlockSpec(s, functools.partial(lambda i, n: (0,) * n, n=len(s))) for s, _ in accs]

    def body(*refs):
        fn(pl.program_id(0), *refs)

    res = _call(body, name=name, out_shape=out_shape, grid=(n_rows // tm,), in_specs=in_specs,
                out_specs=out_specs, rider=rider)(*in_arrays)
    return res


def rms_fwd(x, gain, name, rider=None):
    def fn(i, x_ref, g_ref, h_ref):
        xv = x_ref[...]
        r = lax.rsqrt(jnp.mean(xv * xv, axis=-1, keepdims=True) + EPS)
        h_ref[...] = (xv * r * g_ref[...]).astype(h_ref.dtype)

    res = _rowwise(name, fn, [x], [gain], [(x.shape, BF16)], rider=rider)
    return res[0] if rider is None else (res[0][0], res[1])


def rms_bwd(dhs, x, gain, dx_in, name):
    n = len(dhs)

    def fn(i, *refs):
        dh_refs, (x_ref, dxin_ref, g_ref, dx_ref, dg_ref) = refs[:n], refs[n:]
        dh = dh_refs[0][...]
        for r in dh_refs[1:]:
            dh = dh + r[...]
        xv = x_ref[...]
        r = lax.rsqrt(jnp.mean(xv * xv, axis=-1, keepdims=True) + EPS)
        xn = xv * r
        dxn = dh * g_ref[...]
        dx_ref[...] = dxin_ref[...] + r * (dxn - xn * jnp.mean(dxn * xn, axis=-1, keepdims=True))

        @pl.when(i == 0)
        def _():
            dg_ref[...] = jnp.zeros_like(dg_ref)

        dg_ref[...] += jnp.sum(dh * xn, axis=0, keepdims=True)

    return _rowwise(name, fn, list(dhs) + [x, dx_in], [gain], [(x.shape, F32)], [((1, x.shape[1]), F32)])


def loss_grad(y, target, name):
    def fn(i, y_ref, t_ref, dy_ref, sq_ref):
        err = y_ref[...] - t_ref[...]
        dy_ref[...] = err * (1.0 / y_ref.shape[1])

        @pl.when(i == 0)
        def _():
            sq_ref[...] = jnp.zeros_like(sq_ref)

        sq_ref[...] += jnp.sum(err * err, axis=0, keepdims=True)

    return _rowwise(name, fn, [y, target], [], [(y.shape, F32)], [((1, y.shape[1]), F32)])


def matmul_nn(a, b, name, out_dtype, tm, tn, res=None, scale=1.0, rider=None):
    s, k = a.shape
    n = b.shape[1]

    def body(*refs):
        if res is None:
            a_ref, b_ref, o_ref = refs
            o_ref[...] = _dot(a_ref[...], b_ref[...]).astype(o_ref.dtype)
        else:
            a_ref, b_ref, r_ref, o_ref = refs
            o_ref[...] = (r_ref[...] + scale * _dot(a_ref[...], b_ref[...])).astype(o_ref.dtype)

    in_specs = [pl.BlockSpec((tm, k), lambda i, j: (i, 0)), pl.BlockSpec((k, tn), lambda i, j: (0, j))]
    args = [a, b]
    if res is not None:
        in_specs.append(pl.BlockSpec((tm, tn), lambda i, j: (i, j)))
        args.append(res)
    return _call(body, name=name, out_shape=_sds((s, n), out_dtype), grid=(s // tm, n // tn), in_specs=in_specs,
                 out_specs=pl.BlockSpec((tm, tn), lambda i, j: (i, j)), rider=rider)(*args)


def matmul_nt(a, b, name, out_dtype, tm, tn, tk, rider=None):
    s, k = a.shape
    n = b.shape[0]
    nk = k // tk

    def body(a_ref, b_ref, o_ref, acc_ref):
        kk = pl.program_id(2)

        @pl.when(kk == 0)
        def _():
            acc_ref[...] = jnp.zeros_like(acc_ref)

        acc_ref[...] += _dot_nt(a_ref[...].astype(BF16), b_ref[...])

        @pl.when(kk == nk - 1)
        def _():
            o_ref[...] = acc_ref[...].astype(o_ref.dtype)

    return _call(body, name=name, out_shape=_sds((s, n), out_dtype), grid=(s // tm, n // tn, nk),
                 in_specs=[pl.BlockSpec((tm, tk), lambda i, j, kk: (i, kk)),
                           pl.BlockSpec((tn, tk), lambda i, j, kk: (j, kk))],
                 out_specs=pl.BlockSpec((tm, tn), lambda i, j, kk: (i, j)),
                 scratch_shapes=[pltpu.VMEM((tm, tn), F32)], rider=rider)(a, b)


def matmul_tn(a, b, name, tn, ts, a_scale=None, b_scale=None, rider=None):
    s, m = a.shape
    n = b.shape[1]
    ns = s // ts

    def body(a_ref, b_ref, o_ref, acc_ref):
        ss = pl.program_id(1)

        @pl.when(ss == 0)
        def _():
            acc_ref[...] = jnp.zeros_like(acc_ref)

        av, bv = a_ref[...], b_ref[...]
        if a_scale is not None:
            av = av * a_scale
        if b_scale is not None:
            bv = bv * b_scale
        acc_ref[...] += _dot_tn(av.astype(BF16), bv.astype(BF16))

        @pl.when(ss == ns - 1)
        def _():
            o_ref[...] = acc_ref[...].astype(o_ref.dtype)

    return _call(body, name=name, out_shape=_sds((m, n), BF16), grid=(n // tn, ns),
                 in_specs=[pl.BlockSpec((ts, m), lambda j, ss: (ss, 0)), pl.BlockSpec((ts, tn), lambda j, ss: (ss, j))],
                 out_specs=pl.BlockSpec((m, tn), lambda j, ss: (0, j)),
                 scratch_shapes=[pltpu.VMEM((m, tn), F32)], rider=rider)(a, b)


def _piece_specs(pieces, tile, rows_tile, tile_axis_first):
    specs, ranges, t0 = [], [], 0
    for a in pieces:
        n = a.shape[1] // tile

        def index(*ids, t0=t0, n=n):
            t, r = (ids[0], ids[1]) if tile_axis_first else (ids[2], ids[0])
            on = jnp.logical_and(t >= t0, t < t0 + n)
            return jnp.where(on, r, 0), jnp.clip(t - t0, 0, n - 1)

        specs.append(pl.BlockSpec((rows_tile, tile), index))
        ranges.append((t0, n))
        t0 += n
    return specs, ranges


def matmul_tn_pieces(a, pieces, name, tn, ts, rider=None):
    s, m = a.shape
    ns = s // ts
    specs, ranges = _piece_specs(pieces, tn, ts, True)
    n_total = sum(n for _, n in ranges)

    def body(a_ref, *refs):
        b_refs, o_ref, acc_ref = refs[:len(pieces)], refs[-2], refs[-1]
        j, ss = pl.program_id(0), pl.program_id(1)

        @pl.when(ss == 0)
        def _():
            acc_ref[...] = jnp.zeros_like(acc_ref)

        for b_ref, (t0, n) in zip(b_refs, ranges):
            @pl.when(jnp.logical_and(j >= t0, j < t0 + n))
            def _(b_ref=b_ref):
                acc_ref[...] += _dot_tn(a_ref[...], b_ref[...])

        @pl.when(ss == ns - 1)
        def _():
            o_ref[...] = acc_ref[...].astype(o_ref.dtype)

    return _call(body, name=name, out_shape=_sds((m, n_total * tn), BF16), grid=(n_total, ns),
                 in_specs=[pl.BlockSpec((ts, m), lambda j, ss: (ss, 0))] + specs,
                 out_specs=pl.BlockSpec((m, tn), lambda j, ss: (0, j)),
                 scratch_shapes=[pltpu.VMEM((m, tn), F32)], rider=rider)(a, *pieces)


def matmul_nt_pieces(pieces, b, name, out_dtype, tm, tn, tk, rider=None):
    s = pieces[0].shape[0]
    n = b.shape[0]
    specs, ranges = _piece_specs(pieces, tk, tm, False)
    nk = sum(cnt for _, cnt in ranges)

    def body(*refs):
        a_refs, b_ref, o_ref, acc_ref = refs[:len(pieces)], refs[-3], refs[-2], refs[-1]
        kk = pl.program_id(2)

        @pl.when(kk == 0)
        def _():
            acc_ref[...] = jnp.zeros_like(acc_ref)

        for a_ref, (t0, cnt) in zip(a_refs, ranges):
            @pl.when(jnp.logical_and(kk >= t0, kk < t0 + cnt))
            def _(a_ref=a_ref):
                acc_ref[...] += _dot_nt(a_ref[...], b_ref[...])

        @pl.when(kk == nk - 1)
        def _():
            o_ref[...] = acc_ref[...].astype(o_ref.dtype)

    return _call(body, name=name, out_shape=_sds((s, n), out_dtype), grid=(s // tm, n // tn, nk),
                 in_specs=specs + [pl.BlockSpec((tn, tk), lambda i, j, kk: (j, kk))],
                 out_specs=pl.BlockSpec((tm, tn), lambda i, j, kk: (i, j)),
                 scratch_shapes=[pltpu.VMEM((tm, tn), F32)], rider=rider)(*pieces, b)


def ffn_up(h, w704, gate_blk, up_blk, name, tm=512, rider=None):
    s = h.shape[0]

    def body(h_ref, wg_ref, wu_ref, g_ref, u_ref, a_ref):
        hv = h_ref[...]
        g = _dot(hv, wg_ref[...])
        u = _dot(hv, wu_ref[...])
        g_ref[...] = g.astype(BF16)
        u_ref[...] = u.astype(BF16)
        a_ref[...] = (g * _sigmoid(g) * u).astype(BF16)

    ospec = pl.BlockSpec((None, tm, FF_SHARD), lambda j, i: (j, i, 0))
    shp = _sds((N_CHIP, s, FF_SHARD), BF16)
    return _call(body, name=name, out_shape=[shp, shp, shp], grid=(N_CHIP, s // tm),
                 in_specs=[pl.BlockSpec((tm, D_MODEL), lambda j, i: (i, 0)),
                           pl.BlockSpec((None, D_MODEL, FF_SHARD), lambda j, i: (j, gate_blk, 0)),
                           pl.BlockSpec((None, D_MODEL, FF_SHARD), lambda j, i: (j, up_blk, 0))],
                 out_specs=[ospec, ospec, ospec], rider=rider)(h, w704, w704)


def ffn_down(a, w1024, blk, x, name, tm=512):
    s = x.shape[0]

    def body(a_ref, wd_ref, x_ref, o_ref):
        acc = _dot(a_ref[0], wd_ref[0])
        for j in range(1, N_CHIP):
            acc += _dot(a_ref[j], wd_ref[j])
        o_ref[...] = x_ref[...] + 0.5 * acc

    return _call(body, name=name, out_shape=_sds((s, D_MODEL), F32), grid=(s // tm,),
                 in_specs=[pl.BlockSpec((N_CHIP, tm, FF_SHARD), lambda i: (0, i, 0)),
                           pl.BlockSpec((N_CHIP, FF_SHARD, D_MODEL), lambda i: (0, blk, 0)),
                           pl.BlockSpec((tm, D_MODEL), lambda i: (i, 0))],
                 out_specs=pl.BlockSpec((tm, D_MODEL), lambda i: (i, 0)))(a, w1024, x)


def ffn_bwd_hidden(dx, w1024, blk, g, u, name, tm=1024, rider=None):
    s = dx.shape[0]

    def body(dx_ref, wd_ref, g_ref, u_ref, dg_ref, du_ref):
        dy = (0.5 * dx_ref[...]).astype(BF16)
        da = _dot_nt(dy, wd_ref[...])
        gv = g_ref[...].astype(F32)
        uv = u_ref[...].astype(F32)
        sg = _sigmoid(gv)
        dg_ref[...] = (da * uv * (sg * (1.0 + gv * (1.0 - sg)))).astype(BF16)
        du_ref[...] = (da * gv * sg).astype(BF16)

    hspec = pl.BlockSpec((None, tm, FF_SHARD), lambda j, i: (j, i, 0))
    shp = _sds((N_CHIP, s, FF_SHARD), BF16)
    return _call(body, name=name, out_shape=[shp, shp], grid=(N_CHIP, s // tm),
                 in_specs=[pl.BlockSpec((tm, D_MODEL), lambda j, i: (i, 0)),
                           pl.BlockSpec((None, FF_SHARD, D_MODEL), lambda j, i: (j, blk, 0)), hspec, hspec],
                 out_specs=[hspec, hspec], rider=rider)(dx, w1024, g, u)


def ffn_bwd_input(dg, du, w704, gate_blk, up_blk, name, tm=512, rider=None):
    s = dg.shape[1]

    def body(dg_ref, du_ref, wg_ref, wu_ref, o_ref):
        acc = _dot_nt(dg_ref[0], wg_ref[0]) + _dot_nt(du_ref[0], wu_ref[0])
        for j in range(1, N_CHIP):
            acc += _dot_nt(dg_ref[j], wg_ref[j]) + _dot_nt(du_ref[j], wu_ref[j])
        o_ref[...] = acc

    hspec = pl.BlockSpec((N_CHIP, tm, FF_SHARD), lambda i: (0, i, 0))
    return _call(body, name=name, out_shape=_sds((s, D_MODEL), F32), grid=(s // tm,),
                 in_specs=[hspec, hspec,
                           pl.BlockSpec((N_CHIP, D_MODEL, FF_SHARD), lambda i: (0, gate_blk, 0), pl.Buffered(1)),
                           pl.BlockSpec((N_CHIP, D_MODEL, FF_SHARD), lambda i: (0, up_blk, 0), pl.Buffered(1))],
                 out_specs=pl.BlockSpec((tm, D_MODEL), lambda i: (i, 0)), rider=rider)(dg, du, w704, w704)


def ffn_wgrad_in(h, dgu, name, ts=1024):
    s = h.shape[0]
    ns = s // ts

    def body(h_ref, d_ref, o_ref, acc_ref):
        ss = pl.program_id(1)

        @pl.when(ss == 0)
        def _():
            acc_ref[...] = jnp.zeros_like(acc_ref)

        acc_ref[...] += _dot_tn(h_ref[...], d_ref[...])

        @pl.when(ss == ns - 1)
        def _():
            o_ref[...] = acc_ref[...].astype(BF16)

    return _call(body, name=name, out_shape=_sds((N_CHIP, D_MODEL, FF_SHARD), BF16), grid=(N_CHIP, ns),
                 in_specs=[pl.BlockSpec((ts, D_MODEL), lambda j, ss: (ss, 0)),
                           pl.BlockSpec((None, ts, FF_SHARD), lambda j, ss: (j, ss, 0))],
                 out_specs=pl.BlockSpec((None, D_MODEL, FF_SHARD), lambda j, ss: (j, 0, 0)),
                 scratch_shapes=[pltpu.VMEM((D_MODEL, FF_SHARD), F32)])(h, dgu)


def ffn_wgrad_down(a, dx, name, ts=1024):
    s = dx.shape[0]
    ns = s // ts

    def body(a_ref, dx_ref, o_ref, acc_ref):
        ss = pl.program_id(1)

        @pl.when(ss == 0)
        def _():
            acc_ref[...] = jnp.zeros_like(acc_ref)

        acc_ref[...] += _dot_tn(a_ref[...], (0.5 * dx_ref[...]).astype(BF16))

        @pl.when(ss == ns - 1)
        def _():
            o_ref[...] = acc_ref[...].astype(BF16)

    return _call(body, name=name, out_shape=_sds((N_CHIP, FF_SHARD, D_MODEL), BF16), grid=(N_CHIP, ns),
                 in_specs=[pl.BlockSpec((None, ts, FF_SHARD), lambda j, ss: (j, ss, 0)),
                           pl.BlockSpec((ts, D_MODEL), lambda j, ss: (ss, 0))],
                 out_specs=pl.BlockSpec((None, FF_SHARD, D_MODEL), lambda j, ss: (j, 0, 0)),
                 scratch_shapes=[pltpu.VMEM((FF_SHARD, D_MODEL), F32)])(a, dx)


def ffn_forward(x, gain, get_w704, get_w1024, tag, rms_rider=None, up_rider=None):
    h = rms_fwd(x, gain, f"{tag}_rms", rider=rms_rider)
    h, rode_rms = h if rms_rider is not None else (h, None)
    res = ffn_up(h, get_w704(rode_rms), 0, 1, f"{tag}_up", rider=up_rider)
    (g, u, a), rode_up = res if up_rider is not None else (res, None)
    y = ffn_down(a, get_w1024(rode_up), 0, x, f"{tag}_down")
    return y, (h, g, u, a), rode_rms, rode_up


def ffn_backward(dy, x, gain, w704, w1024, saved, tag, ride_down=None, ride_in=None):
    h, g, u, a = saved
    d_wd = ffn_wgrad_down(a, dy, f"{tag}_dwd")
    if ride_down is not None:
        (dg, du), d_wd = ffn_bwd_hidden(dy, w1024, 0, g, u, f"{tag}_dhid", tm=512, rider=ride_down(d_wd))
    else:
        dg, du = ffn_bwd_hidden(dy, w1024, 0, g, u, f"{tag}_dhid")
    d_win = jnp.concatenate([ffn_wgrad_in(h, dg, f"{tag}_dwg"), ffn_wgrad_in(h, du, f"{tag}_dwu")], axis=1)
    if ride_in is not None:
        dh, d_win = ffn_bwd_input(dg, du, w704, 0, 1, f"{tag}_dh", rider=ride_in(d_win))
    else:
        dh = ffn_bwd_input(dg, du, w704, 0, 1, f"{tag}_dh")
    dx, d_gain = rms_bwd([dh], x, gain, dy, f"{tag}_drms")
    return dx, d_gain, d_win, d_wd


def _alibi_slope(head):
    return float(2.0 ** (-ALIBI_MAX_EXP * (head + 1) / N_ATTN_HEADS))


def _same_head():
    row = lax.broadcasted_iota(jnp.int32, (2 * HD, 2 * HD), 0)
    col = lax.broadcasted_iota(jnp.int32, (2 * HD, 2 * HD), 1)
    return ((row < HD) == (col < HD)).astype(BF16)


def _head_sums(x, same_head):
    hi = x.astype(BF16)
    lo = (x - hi.astype(F32)).astype(BF16)
    return _dot(hi, same_head) + _dot(lo, same_head)


def _head_norm(t, gain_pair, same_head):
    r = lax.rsqrt(_head_sums(t * t, same_head) * (1.0 / HD) + EPS)
    return t * r * gain_pair, r


def qk_norm_fwd(p, q_gain, k_gain, name):
    s = p.shape[0]

    def fn(i, q_ref, k_ref, qg_ref, kg_ref, qn_ref, kn_ref):
        same_head = _same_head()
        for src, g_ref, dst in ((q_ref, qg_ref, qn_ref), (k_ref, kg_ref, kn_ref)):
            for pr in range(ATTN_QKV // (2 * HD)):
                cols = slice(pr * 2 * HD, (pr + 1) * 2 * HD)
                y, _ = _head_norm(src[:, cols].astype(F32), g_ref[...], same_head)
                dst[:, cols] = y.astype(BF16)

    return _rowwise(name, fn, [(p, ATTN_QKV, 0), (p, ATTN_QKV, 1)], [q_gain, k_gain],
                    [((s, ATTN_QKV), BF16), ((s, ATTN_QKV), BF16)])


def qk_norm_bwd(p, dqs, dks, q_gain, k_gain, name):
    s = p.shape[0]
    pairs_per_pattern = GROUP_W // (2 * HD)

    def fn(i, q_ref, k_ref, dq0, dq1, dq2, dk0, dk1, dk2, qg_ref, kg_ref, dqk_ref, dqg_ref, dkg_ref):
        same_head = _same_head()

        @pl.when(i == 0)
        def _():
            dqg_ref[...] = jnp.zeros_like(dqg_ref)
            dkg_ref[...] = jnp.zeros_like(dkg_ref)

        for src, d_refs, g_ref, dst, dg_ref in (
                (q_ref, (dq0, dq1, dq2), qg_ref, dqk_ref.at[:, 0:ATTN_QKV], dqg_ref),
                (k_ref, (dk0, dk1, dk2), kg_ref, dqk_ref.at[:, ATTN_QKV:2 * ATTN_QKV], dkg_ref)):
            for pr in range(ATTN_QKV // (2 * HD)):
                cols = slice(pr * 2 * HD, (pr + 1) * 2 * HD)
                t = src[:, cols].astype(F32)
                r = lax.rsqrt(_head_sums(t * t, same_head) * (1.0 / HD) + EPS)
                xn = t * r
                within = (pr % pairs_per_pattern) * 2 * HD
                dy = d_refs[pr // pairs_per_pattern][:, within:within + 2 * HD]
                dg_ref[:, cols] += jnp.sum(dy * xn, axis=0, keepdims=True)
                dxn = dy * g_ref[...]
                mean = _head_sums(dxn * xn, same_head) * (1.0 / HD)
                dst[:, cols] = (r * (dxn - xn * mean)).astype(BF16)

    return _rowwise(name, fn, [(p, ATTN_QKV, 0), (p, ATTN_QKV, 1)] + list(dqs) + list(dks), [q_gain, k_gain],
                    [((s, 2 * ATTN_QKV), BF16)], [((1, ATTN_QKV), F32), ((1, ATTN_QKV), F32)])


def _to_streams(a, d):
    if d == 1:
        return a
    s, c = a.shape
    return a.reshape(s // d, d, c).transpose(1, 0, 2).reshape(s, c)


def _from_streams(a, d):
    if d == 1:
        return a
    s, c = a.shape
    return a.reshape(d, s // d, c).transpose(1, 0, 2).reshape(s, c)


def _attn_masks():
    row = lax.broadcasted_iota(jnp.int32, (BLK, BLK), 0)
    col = lax.broadcasted_iota(jnp.int32, (BLK, BLK), 1)
    rel_diag = row - col
    rel_prev = rel_diag + BLK
    return rel_diag, rel_prev


def attn_fwd(q, k, v, pattern, name, tq=512):
    s = q.shape[0]
    d = ATTN_DILATIONS[pattern]
    blocks_per_stream = (s // d) // BLK
    nsb = tq // BLK

    def body(q_ref, k_ref, v_ref, kp_ref, vp_ref, o_ref, l_ref):
        i = pl.program_id(0)
        rel_diag, rel_prev = _attn_masks()
        first = _lane_first_half((BLK, 2 * HD))
        rd_f = (rel_diag * d).astype(F32)
        rp_f = (rel_prev * d).astype(F32)
        for sb in range(nsb):
            rows = slice(sb * BLK, (sb + 1) * BLK)
            has_prev = ((i * nsb + sb) % blocks_per_stream != 0).astype(jnp.int32)
            m_diag = rel_diag >= 0
            m_prev = (rel_prev + (1 - has_prev) * (4 * BLK)) <= BLK
            for pr in range(GROUP_W // (2 * HD)):
                cols = slice(pr * 2 * HD, (pr + 1) * 2 * HD)
                qp = q_ref[rows, cols]
                kc, vc = k_ref[rows, cols], v_ref[rows, cols]
                if sb == 0:
                    kp, vp = kp_ref[:, cols], vp_ref[:, cols]
                else:
                    prows = slice((sb - 1) * BLK, sb * BLK)
                    kp, vp = k_ref[prows, cols], v_ref[prows, cols]
                outs, lses = [], []
                for e in range(2):
                    slope = _alibi_slope(pattern * HEADS_PER_PATTERN + 2 * pr + e)
                    qm = jnp.where(first if e == 0 else jnp.logical_not(first), qp, jnp.zeros_like(qp))
                    s1 = jnp.where(m_diag, _dot_nt(qm, kc) * 0.125 - slope * rd_f, NEG)
                    s0 = jnp.where(m_prev, _dot_nt(qm, kp) * 0.125 - slope * rp_f, NEG)
                    m = jnp.maximum(jnp.max(s1, axis=-1, keepdims=True), jnp.max(s0, axis=-1, keepdims=True))
                    p1 = jnp.exp(s1 - m)
                    p0 = jnp.exp(s0 - m)
                    l = jnp.sum(p1, axis=-1, keepdims=True) + jnp.sum(p0, axis=-1, keepdims=True)
                    inv = 1.0 / l
                    outs.append(_dot((p1 * inv).astype(BF16), vc) + _dot((p0 * inv).astype(BF16), vp))
                    lses.append(m + jnp.log(l))
                o_ref[rows, cols] = jnp.where(first, outs[0], outs[1])
                l_ref[rows, cols] = jnp.where(first, lses[0], lses[1])

    cur = pl.BlockSpec((tq, GROUP_W), lambda i: (i, 0))
    prev = pl.BlockSpec((BLK, GROUP_W), lambda i: (jnp.maximum(i * nsb - 1, 0), 0))
    return _call(body, name=name, out_shape=[_sds((s, GROUP_W), F32), _sds((s, GROUP_W), F32)], grid=(s // tq,),
                 in_specs=[cur, cur, cur, prev, prev], out_specs=[cur, cur])(q, k, v, k, v)


def attn_merge_fwd(os_, lses, name):
    s = os_[0].shape[0]

    def fn(i, o0, o1, o2, l0, l1, l2, out_ref):
        m = jnp.maximum(jnp.maximum(l0[...], l1[...]), l2[...])
        e0, e1, e2 = jnp.exp(l0[...] - m), jnp.exp(l1[...] - m), jnp.exp(l2[...] - m)
        inv = 1.0 / (e0 + e1 + e2)
        out_ref[...] = ((e0 * inv) * o0[...] + (e1 * inv) * o1[...] + (e2 * inv) * o2[...]).astype(BF16)

    return _rowwise(name, fn, list(os_) + list(lses), [], [((s, GROUP_W), BF16)])[0]


def attn_merge_bwd(d_out, os_, lses, name):
    s = d_out.shape[0]

    def fn(i, do_ref, o0, o1, o2, l0, l1, l2, d0, d1, d2, c0, c1, c2):
        first = _lane_first_half((do_ref.shape[0], 2 * HD))
        m = jnp.maximum(jnp.maximum(l0[...], l1[...]), l2[...])
        e0, e1, e2 = jnp.exp(l0[...] - m), jnp.exp(l1[...] - m), jnp.exp(l2[...] - m)
        inv = 1.0 / (e0 + e1 + e2)
        w0, w1, w2 = e0 * inv, e1 * inv, e2 * inv
        do = do_ref[...]
        prod = do * (w0 * o0[...] + w1 * o1[...] + w2 * o2[...])
        same_head = _same_head()
        for pr in range(GROUP_W // (2 * HD)):
            cols = slice(pr * 2 * HD, (pr + 1) * 2 * HD)
            t = _head_sums(prod[:, cols], same_head)
            for w, c_ref in ((w0, c0), (w1, c1), (w2, c2)):
                c_ref[:, cols] = w[:, cols] * t
        for w, d_ref in ((w0, d0), (w1, d1), (w2, d2)):
            d_ref[...] = (w * do).astype(BF16)

    shp = (s, GROUP_W)
    return _rowwise(name, fn, [d_out] + list(os_) + list(lses), [],
                    [(shp, BF16)] * 3 + [(shp, F32)] * 3)


def attn_bwd(q, k, v, d_o, cterm, lse, pattern, name, tq=512):
    s = q.shape[0]
    d = ATTN_DILATIONS[pattern]
    blocks_per_stream = (s // d) // BLK
    nsb = tq // BLK
    n_blocks = s // BLK

    def body(q_ref, k_ref, v_ref, do_ref, c_ref, l_ref, kp_ref, vp_ref, qn_ref, don_ref, cn_ref, ln_ref,
             dq_ref, dk_ref, dv_ref):
        i = pl.program_id(0)
        rel_diag, rel_prev = _attn_masks()
        first = _lane_first_half((BLK, 2 * HD))
        second = jnp.logical_not(first)
        rd_f = (rel_diag * d).astype(F32)
        rp_f = (rel_prev * d).astype(F32)
        m_diag = rel_diag >= 0
        dq_ref[...] = jnp.zeros_like(dq_ref)
        dk_ref[...] = jnp.zeros_like(dk_ref)
        dv_ref[...] = jnp.zeros_like(dv_ref)

        def pair(qp, dop, cp, lp, kp, vp, rel_f, mask):
            dq = dk = dv = None
            for e in range(2):
                lanes = first if e == 0 else second
                slope = slopes[e]
                qm = jnp.where(lanes, qp, jnp.zeros_like(qp))
                dom = jnp.where(lanes, dop, jnp.zeros_like(dop))
                km = jnp.where(lanes, kp, jnp.zeros_like(kp))
                sc = jnp.where(mask, _dot_nt(qm, kp) * 0.125 - slope * rel_f, NEG)
                pm = jnp.exp(sc - lp[:, e * HD:e * HD + 1])
                dl = pm * (_dot_nt(dom, vp) - cp[:, e * HD:e * HD + 1])
                dl16 = dl.astype(BF16)
                t_dq = _dot(dl16, km)
                t_dk = _dot_tn(dl16, qm)
                t_dv = _dot_tn(pm.astype(BF16), dom)
                dq = t_dq if dq is None else dq + t_dq
                dk = t_dk if dk is None else dk + t_dk
                dv = t_dv if dv is None else dv + t_dv
            return dq * 0.125, dk * 0.125, dv

        for pr in range(GROUP_W // (2 * HD)):
            cols = slice(pr * 2 * HD, (pr + 1) * 2 * HD)
            slopes = [_alibi_slope(pattern * HEADS_PER_PATTERN + 2 * pr + e) for e in range(2)]
            for sb in range(nsb + 1):
                gb = i * nsb + sb
                if sb < nsb:
                    rows = slice(sb * BLK, (sb + 1) * BLK)
                    qp, dop, cp, lp = q_ref[rows, cols], do_ref[rows, cols], c_ref[rows, cols], l_ref[rows, cols]
                else:
                    qp, dop, cp, lp = qn_ref[:, cols], don_ref[:, cols], cn_ref[:, cols], ln_ref[:, cols]
                if sb < nsb:
                    dq1, dk1, dv1 = pair(qp, dop, cp, lp, k_ref[rows, cols], v_ref[rows, cols], rd_f, m_diag)
                    dq_ref[rows, cols] += dq1
                    dk_ref[rows, cols] += dk1
                    dv_ref[rows, cols] += dv1
                valid = jnp.logical_and(gb % blocks_per_stream != 0, gb < n_blocks).astype(jnp.int32)
                m_prev = jnp.logical_and(rel_prev <= BLK, (rel_prev + (1 - valid) * (4 * BLK)) <= BLK)
                if sb == 0:
                    kp, vp = kp_ref[:, cols], vp_ref[:, cols]
                else:
                    prows = slice((sb - 1) * BLK, sb * BLK)
                    kp, vp = k_ref[prows, cols], v_ref[prows, cols]
                dq0, dk0, dv0 = pair(qp, dop, cp, lp, kp, vp, rp_f, m_prev)
                if sb < nsb:
                    dq_ref[rows, cols] += dq0
                if sb > 0:
                    dk_ref[prows, cols] += dk0
                    dv_ref[prows, cols] += dv0

    cur = pl.BlockSpec((tq, GROUP_W), lambda i: (i, 0))
    prev = pl.BlockSpec((BLK, GROUP_W), lambda i: (jnp.maximum(i * nsb - 1, 0), 0))
    nxt = pl.BlockSpec((BLK, GROUP_W), lambda i: (jnp.minimum((i + 1) * nsb, n_blocks - 1), 0))
    shp = _sds((s, GROUP_W), F32)
    return _call(body, name=name, out_shape=[shp, shp, shp], grid=(s // tq,),
                 in_specs=[cur] * 6 + [prev, prev] + [nxt] * 4, out_specs=[cur, cur, cur])(
                     q, k, v, d_o, cterm, lse, k, v, q, d_o, cterm, lse)


def _band_constants(d):
    row = lax.broadcasted_iota(jnp.int32, (2 * BLK, 2 * BLK), 0)
    col = lax.broadcasted_iota(jnp.int32, (2 * BLK, 2 * BLK), 1)
    rel = BLK + jnp.where(row >= BLK, row - BLK, row) - col
    band = jnp.logical_and(rel >= 0, rel <= BLK)
    return (rel * d).astype(F32), band, (col >= BLK).astype(jnp.int32)


def _stack_heads(x, first):
    zero = jnp.zeros_like(x)
    return jnp.concatenate([jnp.where(first, x, zero), jnp.where(first, zero, x)], axis=0)


def _unstack_heads(x2, first):
    return jnp.where(first, x2[:BLK], x2[BLK:])


def _head_column(x):
    return jnp.concatenate([x[:, 0:1], x[:, HD:HD + 1]], axis=0)


def attn_fwd2(q, k, v, pattern, name, tq=512):
    s = q.shape[0]
    d = ATTN_DILATIONS[pattern]
    blocks_per_stream = (s // d) // BLK
    nsb = tq // BLK

    def body(q_ref, k_ref, v_ref, kp_ref, vp_ref, o_ref, l_ref):
        i = pl.program_id(0)
        rel_f, band, own = _band_constants(d)
        first = _lane_first_half((BLK, 2 * HD))
        upper = lax.broadcasted_iota(jnp.int32, (2 * BLK, 1), 0) < BLK
        for sb in range(nsb):
            rows = slice(sb * BLK, (sb + 1) * BLK)
            has_prev = ((i * nsb + sb) % blocks_per_stream != 0).astype(jnp.int32)
            mask = jnp.logical_and(band, (own + has_prev) > 0)
            for pr in range(GROUP_W // (2 * HD)):
                cols = slice(pr * 2 * HD, (pr + 1) * 2 * HD)
                if sb == 0:
                    kcat = jnp.concatenate([kp_ref[:, cols], k_ref[rows, cols]], axis=0)
                    vcat = jnp.concatenate([vp_ref[:, cols], v_ref[rows, cols]], axis=0)
                else:
                    both = slice((sb - 1) * BLK, (sb + 1) * BLK)
                    kcat, vcat = k_ref[both, cols], v_ref[both, cols]
                h0 = pattern * HEADS_PER_PATTERN + 2 * pr
                slope = jnp.where(upper, _alibi_slope(h0), _alibi_slope(h0 + 1))
                sc = _dot_nt(_stack_heads(q_ref[rows, cols], first), kcat) * 0.125 - slope * rel_f
                sc = jnp.where(mask, sc, NEG)
                m = jnp.max(sc, axis=-1, keepdims=True)
                p = jnp.exp(sc - m)
                l = jnp.sum(p, axis=-1, keepdims=True)
                o2 = _dot((p * (1.0 / l)).astype(BF16), vcat)
                o_ref[rows, cols] = _unstack_heads(o2, first)
                lse = m + jnp.log(l)
                l_ref[rows, cols] = jnp.where(first, lse[:BLK], lse[BLK:])

    cur = pl.BlockSpec((tq, GROUP_W), lambda i: (i, 0))
    prev = pl.BlockSpec((BLK, GROUP_W), lambda i: (jnp.maximum(i * nsb - 1, 0), 0))
    return _call(body, name=name, out_shape=[_sds((s, GROUP_W), F32), _sds((s, GROUP_W), F32)], grid=(s // tq,),
                 in_specs=[cur, cur, cur, prev, prev], out_specs=[cur, cur])(q, k, v, k, v)


def attn_bwd2(q, k, v, d_o, cterm, lse, pattern, name, tq=512):
    s = q.shape[0]
    d = ATTN_DILATIONS[pattern]
    blocks_per_stream = (s // d) // BLK
    nsb = tq // BLK
    n_blocks = s // BLK

    def body(q_ref, k_ref, v_ref, do_ref, c_ref, l_ref, kp_ref, vp_ref, qn_ref, kn_ref, vn_ref, don_ref, cn_ref,
             ln_ref, dq_ref, dk_ref, dv_ref):
        i = pl.program_id(0)
        rel_f, band, own = _band_constants(d)
        first = _lane_first_half((BLK, 2 * HD))
        upper = lax.broadcasted_iota(jnp.int32, (2 * BLK, 1), 0) < BLK
        dk_ref[...] = jnp.zeros_like(dk_ref)
        dv_ref[...] = jnp.zeros_like(dv_ref)
        for sb in range(nsb + 1):
            gb = i * nsb + sb
            rows = slice(sb * BLK, (sb + 1) * BLK)
            before = slice((sb - 1) * BLK, sb * BLK)
            inside = (gb < n_blocks).astype(jnp.int32)
            has_prev = jnp.logical_and(gb % blocks_per_stream != 0, gb < n_blocks).astype(jnp.int32)
            mask = jnp.logical_and(band, (own * inside + has_prev) > 0)
            for pr in range(GROUP_W // (2 * HD)):
                cols = slice(pr * 2 * HD, (pr + 1) * 2 * HD)
                if sb == 0:
                    kcat = jnp.concatenate([kp_ref[:, cols], k_ref[rows, cols]], axis=0)
                    vcat = jnp.concatenate([vp_ref[:, cols], v_ref[rows, cols]], axis=0)
                elif sb == nsb:
                    kcat = jnp.concatenate([k_ref[before, cols], kn_ref[:, cols]], axis=0)
                    vcat = jnp.concatenate([v_ref[before, cols], vn_ref[:, cols]], axis=0)
                else:
                    both = slice((sb - 1) * BLK, (sb + 1) * BLK)
                    kcat, vcat = k_ref[both, cols], v_ref[both, cols]
                if sb < nsb:
                    qp, dop, cp, lp = q_ref[rows, cols], do_ref[rows, cols], c_ref[rows, cols], l_ref[rows, cols]
                else:
                    qp, dop, cp, lp = qn_ref[:, cols], don_ref[:, cols], cn_ref[:, cols], ln_ref[:, cols]
                h0 = pattern * HEADS_PER_PATTERN + 2 * pr
                slope = jnp.where(upper, _alibi_slope(h0), _alibi_slope(h0 + 1))
                q2 = _stack_heads(qp, first)
                do2 = _stack_heads(dop, first)
                sc = jnp.where(mask, _dot_nt(q2, kcat) * 0.125 - slope * rel_f, NEG)
                pm = jnp.exp(sc - _head_column(lp))
                dl = (pm * (_dot_nt(do2, vcat) - _head_column(cp))).astype(BF16)
                if sb < nsb:
                    dq_ref[rows, cols] = _unstack_heads(_dot(dl, kcat), first) * 0.125
                dk2 = _dot_tn(dl, q2) * 0.125
                dv2 = _dot_tn(pm.astype(BF16), do2)
                if sb > 0:
                    dk_ref[before, cols] += dk2[:BLK]
                    dv_ref[before, cols] += dv2[:BLK]
                if sb < nsb:
                    dk_ref[rows, cols] += dk2[BLK:]
                    dv_ref[rows, cols] += dv2[BLK:]

    cur = pl.BlockSpec((tq, GROUP_W), lambda i: (i, 0))
    prev = pl.BlockSpec((BLK, GROUP_W), lambda i: (jnp.maximum(i * nsb - 1, 0), 0))
    nxt = pl.BlockSpec((BLK, GROUP_W), lambda i: (jnp.minimum((i + 1) * nsb, n_blocks - 1), 0))
    shp = _sds((s, GROUP_W), F32)
    return _call(body, name=name, out_shape=[shp, shp, shp], grid=(s // tq,),
                 in_specs=[cur] * 6 + [prev, prev] + [nxt] * 6, out_specs=[cur, cur, cur])(
                     q, k, v, d_o, cterm, lse, k, v, q, k, v, d_o, cterm, lse)


HALO = 16
CONV_TQ = 512


def conv_fwd(p, w, b, name):
    s = p.shape[0]
    tq = CONV_TQ
    ncol = SSD_CONV_DIM // GROUP_W
    cb0 = COL_XBC // GROUP_W

    def body(u_ref, up_ref, w_ref, b_ref, c_ref, xc_ref):
        i = pl.program_id(0)
        prev = up_ref[...].astype(F32) * (i > 0).astype(F32)
        ext = jnp.concatenate([prev, u_ref[...].astype(F32)], axis=0)
        acc = b_ref[...] + w_ref[SSD_CONV - 1:SSD_CONV, :] * ext[HALO:HALO + tq]
        for kk in range(SSD_CONV - 1):
            acc += w_ref[kk:kk + 1, :] * pltpu.roll(ext, SSD_CONV - 1 - kk, 0)[HALO:HALO + tq]
        c_ref[...] = acc.astype(BF16)
        xc_ref[...] = (acc * _sigmoid(acc)).astype(BF16)

    cur_in = pl.BlockSpec((tq, GROUP_W), lambda i, j: (i, cb0 + j))
    prev_in = pl.BlockSpec((HALO, GROUP_W), lambda i, j: (jnp.maximum(i * (tq // HALO) - 1, 0), cb0 + j))
    cur_out = pl.BlockSpec((tq, GROUP_W), lambda i, j: (i, j))
    shp = _sds((s, SSD_CONV_DIM), BF16)
    return _call(body, name=name, out_shape=[shp, shp], grid=(s // tq, ncol),
                 in_specs=[cur_in, prev_in, pl.BlockSpec((SSD_CONV, GROUP_W), lambda i, j: (0, j)),
                           pl.BlockSpec((1, GROUP_W), lambda i, j: (0, j))],
                 out_specs=[cur_out, cur_out])(p, p, w, b)


def conv_bwd(p, cpre, dxs, d_b, d_c, w, name):
    s = p.shape[0]
    tq = CONV_TQ
    ncol = SSD_CONV_DIM // GROUP_W
    n_xs = SSD_INNER // GROUP_W
    cb0 = COL_XBC // GROUP_W
    nt = s // tq

    def body(u_ref, c_ref, cn_ref, dx_ref, dxn_ref, dbm_ref, dbmn_ref, dcm_ref, dcmn_ref, w_ref,
             du_ref, dw_ref, db_ref):
        j, i = pl.program_id(0), pl.program_id(1)

        def dpre(c16, dx):
            c = c16.astype(F32)
            sg = _sigmoid(c)
            return dx * (sg * (1.0 + c * (1.0 - sg)))

        def pick(a_ref, b_ref, c_ref_):
            return jnp.where(j < n_xs, a_ref[...], jnp.where(j == n_xs, b_ref[...], c_ref_[...]))

        dc = dpre(c_ref[...], pick(dx_ref, dbm_ref, dcm_ref))
        dcn = dpre(cn_ref[...], pick(dxn_ref, dbmn_ref, dcmn_ref)) * (i < nt - 1).astype(F32)
        dext = jnp.concatenate([dc, dcn], axis=0)
        u = u_ref[...].astype(F32)

        @pl.when(i == 0)
        def _():
            dw_ref[...] = jnp.zeros_like(dw_ref)
            db_ref[...] = jnp.zeros_like(db_ref)

        du = w_ref[SSD_CONV - 1:SSD_CONV, :] * dc
        dw_ref[SSD_CONV - 1:SSD_CONV, :] += jnp.sum(dc * u, axis=0, keepdims=True)
        for kk in range(SSD_CONV - 1):
            sh = SSD_CONV - 1 - kk
            ahead = pltpu.roll(dext, tq + HALO - sh, 0)[0:tq]
            du += w_ref[kk:kk + 1, :] * ahead
            dw_ref[kk:kk + 1, :] += jnp.sum(ahead * u, axis=0, keepdims=True)
        du_ref[...] = du.astype(BF16)
        db_ref[...] += jnp.sum(dc, axis=0, keepdims=True)

    hb = tq // HALO
    cur_p = pl.BlockSpec((tq, GROUP_W), lambda j, i: (i, cb0 + j))
    cur = pl.BlockSpec((tq, GROUP_W), lambda j, i: (i, j))
    nxt = pl.BlockSpec((HALO, GROUP_W), lambda j, i: (jnp.minimum((i + 1) * hb, s // HALO - 1), j))

    def piece(first_tile, n_tiles):
        def on(j):
            return jnp.logical_and(j >= first_tile, j < first_tile + n_tiles)

        def col(j):
            return jnp.clip(j - first_tile, 0, n_tiles - 1)

        return (pl.BlockSpec((tq, GROUP_W), lambda j, i: (jnp.where(on(j), i, 0), col(j))),
                pl.BlockSpec((HALO, GROUP_W),
                             lambda j, i: (jnp.where(on(j), jnp.minimum((i + 1) * hb, s // HALO - 1), 0), col(j))))

    return _call(body, name=name,
                 out_shape=[_sds((s, SSD_CONV_DIM), BF16), _sds((8, SSD_CONV_DIM), F32), _sds((1, SSD_CONV_DIM), F32)],
                 grid=(ncol, nt),
                 in_specs=[cur_p, cur, nxt, *piece(0, n_xs), *piece(n_xs, 1), *piece(n_xs + 1, 1),
                           pl.BlockSpec((SSD_CONV, GROUP_W), lambda j, i: (0, j))],
                 out_specs=[cur, pl.BlockSpec((8, GROUP_W), lambda j, i: (0, j)),
                            pl.BlockSpec((1, GROUP_W), lambda j, i: (0, j))])(
                                p, cpre, cpre, dxs, dxs, d_b, d_b, d_c, d_c, w)


def _softplus(x):
    return jnp.maximum(x, 0.0) + jnp.log(1.0 + jnp.exp(-jnp.abs(x)))


def _ssd_decays(dtr_ref, dtrt_ref, bias_ref, biast_ref, alog_ref, alogt_ref):
    row = lax.broadcasted_iota(jnp.int32, (BLK, BLK), 0)
    col = lax.broadcasted_iota(jnp.int32, (BLK, BLK), 1)
    lower = (row >= col).astype(F32)
    upper = (row <= col).astype(F32)
    dtb = dtr_ref[...] + bias_ref[...]
    dt = _softplus(dtb)
    a = dt * (-jnp.exp(alog_ref[...]))
    cs = _dot_hi(lower, a)
    a_t = _softplus(dtrt_ref[...] + biast_ref[...]) * (-jnp.exp(alogt_ref[...]))
    cs_t = _dot_hi(a_t, upper)
    return dtb, dt, cs, cs_t, row, col, upper


SSD_GROUPS_PER_STEP = 4


def _per_group(body, gps, kinds):
    def wrapped(*refs):
        for gi in range(gps):
            args, pos = [], 0
            for kind, n in kinds:
                if kind == "each":
                    args.append(refs[pos + gi])
                    pos += gps
                    continue
                ref = refs[pos]
                pos += 1
                if kind == "cols":
                    args.append(ref.at[:, gi * n:(gi + 1) * n])
                else:
                    args.append(ref.at[gi] if n == 1 else ref.at[pl.ds(gi * n, n)])
            body(*args)

    return wrapped


def ssd_fwd(p, xc, dtg, dtg_t, params, gn, name):
    s = p.shape[0]
    nc = s // BLK
    bias, bias_t, alog, alog_t, dskip = params

    def body(xs_ref, b_ref, c_ref, z_ref, dtr_ref, dtrt_ref, bias_ref, biast_ref, alog_ref, alogt_ref, dsk_ref,
             gn_ref, y_ref, sin_ref, hp_ref, h_ref):
        c_idx = pl.program_id(1)

        @pl.when(c_idx == 0)
        def _():
            h_ref[...] = jnp.zeros_like(h_ref)

        _, dt, cs, cs_t, row, col, _ = _ssd_decays(dtr_ref, dtrt_ref, bias_ref, biast_ref, alog_ref, alogt_ref)
        first = _lane_first_half((BLK, 2 * HD))
        first_row = _lane_first_half((1, 2 * HD))
        tril = row >= col
        b16, c16 = b_ref[...], c_ref[...]
        cb = _dot_nt(c16, b16)
        n_pairs = GROUP_W // (2 * HD)
        tot = cs[BLK - 1:BLK, :]
        exp_cs, exp_rest, exp_tot = jnp.exp(cs), jnp.exp(tot - cs), jnp.exp(tot)

        def per_head(v, mask):
            return jnp.concatenate([jnp.where(mask, v[:, 2 * pr:2 * pr + 1], v[:, 2 * pr + 1:2 * pr + 2])
                                    for pr in range(n_pairs)], axis=1)

        xs = xs_ref[...].astype(F32)
        xt = xs * per_head(dt, first)
        xt16 = xt.astype(BF16)
        hstate = jnp.concatenate([h_ref[pr] for pr in range(n_pairs)], axis=1)
        for pr in range(n_pairs):
            hp_ref[pr] = h_ref[pr]
        y_off = per_head(exp_cs, first) * _dot(c16, hstate.astype(BF16))
        new = per_head(exp_tot, first_row) * hstate + _dot_tn(b16, (per_head(exp_rest, first) * xt).astype(BF16))
        for pr in range(n_pairs):
            h_ref[pr] = new[:, pr * 2 * HD:(pr + 1) * 2 * HD]
        y_diag = []
        for pr in range(n_pairs):
            cols = slice(pr * 2 * HD, (pr + 1) * 2 * HD)
            m2 = jnp.concatenate(
                [(cb * jnp.exp(jnp.where(tril, cs[:, h:h + 1] - cs_t[h:h + 1, :], NEG))).astype(BF16)
                 for h in (2 * pr, 2 * pr + 1)], axis=1)
            y_diag.append(_dot(m2, _stack_heads(xt16[:, cols], first)))
        y = jnp.concatenate(y_diag, axis=1) + y_off + xs * per_head(dsk_ref[...], first_row)
        y_ref[...] = y
        zv = z_ref[...].astype(F32)
        yz = y * (zv * _sigmoid(zv))
        r = lax.rsqrt(jnp.mean(yz * yz, axis=-1, keepdims=True) + EPS)
        sin_ref[...] = (yz * r * gn_ref[...]).astype(BF16)

    gps = SSD_GROUPS_PER_STEP
    wide, narrow, lead = ("cols", GROUP_W), ("cols", BLK), ("lead", 1)
    kinds = [wide, narrow, narrow, ("each", 0)] + [lead] * 7 + [wide, wide, wide, lead, ("lead", 4)]
    wide_w, narrow_w = GROUP_W * gps, BLK * gps
    gparam = pl.BlockSpec((gps, 1, 8), lambda g, c: (g, 0, 0))
    gparam_t = pl.BlockSpec((gps, 8, 1), lambda g, c: (g, 0, 0))
    z_specs = [pl.BlockSpec((BLK, GROUP_W), functools.partial(lambda g, c, gi: (c, COL_Z // GROUP_W + gps * g + gi),
                                                              gi=gi)) for gi in range(gps)]
    return _call(
        _per_group(body, gps, kinds), name=name,
        out_shape=[_sds((s, SSD_INNER), F32), _sds((s, SSD_INNER), BF16),
                   _sds((SSD_GROUPS, nc, 4, BLK, 2 * HD), F32)],
        grid=(SSD_GROUPS // gps, nc),
        in_specs=[pl.BlockSpec((BLK, wide_w), lambda g, c: (c, g)),
                  pl.BlockSpec((BLK, narrow_w), lambda g, c: (c, SSD_INNER // narrow_w + g)),
                  pl.BlockSpec((BLK, narrow_w), lambda g, c: (c, (SSD_INNER + SSD_GROUPS * BLK) // narrow_w + g)),
                  *z_specs,
                  pl.BlockSpec((gps, BLK, 8), lambda g, c: (g, c, 0)),
                  pl.BlockSpec((gps, 8, BLK), lambda g, c: (g, 0, c)),
                  gparam, gparam_t, gparam, gparam_t, gparam,
                  pl.BlockSpec((1, wide_w), lambda g, c: (0, g))],
        out_specs=[pl.BlockSpec((BLK, wide_w), lambda g, c: (c, g)),
                   pl.BlockSpec((BLK, wide_w), lambda g, c: (c, g)),
                   pl.BlockSpec((gps, None, 4, BLK, 2 * HD), lambda g, c: (g, c, 0, 0, 0))],
        scratch_shapes=[pltpu.VMEM((4 * gps, BLK, 2 * HD), F32)],
    )(xc, xc, xc, *([p] * gps), dtg, dtg_t, bias, bias_t, alog, alog_t, dskip, gn)


def ssd_bwd(p, xc, y, d_sin, hprev, dtg, dtg_t, params, gn, name):
    s = p.shape[0]
    nc = s // BLK
    bias, bias_t, alog, alog_t, dskip = params

    def body(xs_ref, b_ref, c_ref, z_ref, y_ref, dsin_ref, hp_ref, dtr_ref, dtrt_ref, bias_ref,
             biast_ref, alog_ref, alogt_ref, dsk_ref, gn_ref,
             dxs_ref, db_ref, dc_ref, dz_ref, ddt_ref, da_ref, dbias_ref, ddsk_ref, dgn_ref, dh_ref):
        c_idx = pl.program_id(1)

        @pl.when(c_idx == 0)
        def _():
            dh_ref[...] = jnp.zeros_like(dh_ref)
            da_ref[...] = jnp.zeros_like(da_ref)
            dbias_ref[...] = jnp.zeros_like(dbias_ref)
            ddsk_ref[...] = jnp.zeros_like(ddsk_ref)
            dgn_ref[...] = jnp.zeros_like(dgn_ref)

        dtb, dt, cs, cs_t, row, col, upper = _ssd_decays(dtr_ref, dtrt_ref, bias_ref, biast_ref, alog_ref, alogt_ref)
        first = _lane_first_half((BLK, 2 * HD))
        second = jnp.logical_not(first)
        first_row = _lane_first_half((1, 2 * HD))
        tril = row >= col
        triu = row <= col
        last_row = lax.broadcasted_iota(jnp.int32, (BLK, 1), 0) == BLK - 1
        lane8 = lax.broadcasted_iota(jnp.int32, (BLK, 8), 1)

        yv = y_ref[...]
        zv = z_ref[...].astype(F32)
        sg = _sigmoid(zv)
        yz = yv * (zv * sg)
        r = lax.rsqrt(jnp.mean(yz * yz, axis=-1, keepdims=True) + EPS)
        yzn = yz * r
        dsn = dsin_ref[...]
        dgn_ref[...] += jnp.sum(dsn * yzn, axis=0, keepdims=True)
        dsn = dsn * gn_ref[...]
        dyz = r * (dsn - yzn * jnp.mean(dsn * yzn, axis=-1, keepdims=True))
        dy = dyz * (zv * sg)
        dz_ref[...] = (dyz * yv * (sg * (1.0 + zv * (1.0 - sg)))).astype(BF16)
        xs_all = xs_ref[...].astype(F32)
        ddsk_ref[...] += jnp.sum(dy * xs_all, axis=0, keepdims=True)

        b16, c16 = b_ref[...], c_ref[...]
        cb = _dot_nt(c16, b16)
        cb_t = _dot_nt(b16, c16)
        n_pairs = GROUP_W // (2 * HD)
        tot = cs[BLK - 1:BLK, :]
        exp_cs, exp_rest, exp_tot = jnp.exp(cs), jnp.exp(tot - cs), jnp.exp(tot)

        def per_head(v, mask):
            return jnp.concatenate([jnp.where(mask, v[:, 2 * pr:2 * pr + 1], v[:, 2 * pr + 1:2 * pr + 2])
                                    for pr in range(n_pairs)], axis=1)

        head_of_lane = (lax.broadcasted_iota(jnp.int32, (GROUP_W, 8), 0) // HD
                        == lax.broadcasted_iota(jnp.int32, (GROUP_W, 8), 1)).astype(BF16)

        def head_sums(v):
            hi = v.astype(BF16)
            lo = (v - hi.astype(F32)).astype(BF16)
            return _dot(hi, head_of_lane) + _dot(lo, head_of_lane)

        dt_w, e_w, f_w = per_head(dt, first), per_head(exp_cs, first), per_head(exp_rest, first)
        xt = xs_all * dt_w
        xt16 = xt.astype(BF16)
        hstate = jnp.concatenate([hp_ref[pr] for pr in range(n_pairs)], axis=1)
        h16 = hstate.astype(BF16)
        dhn = jnp.concatenate([dh_ref[pr] for pr in range(n_pairs)], axis=1)
        dhn16 = dhn.astype(BF16)
        edy16 = (e_w * dy).astype(BF16)
        y_off = e_w * _dot(c16, h16)
        dcs_all = head_sums(dy * y_off)
        dc_acc = _dot_nt(edy16, h16)
        zmat = _dot(b16, dhn16)
        t_all = head_sums(zmat * xt) * exp_rest
        hh_rows = jnp.sum(head_sums(dhn * hstate), axis=0, keepdims=True)
        dtot = jnp.sum(t_all, axis=0, keepdims=True) + hh_rows * exp_tot
        dcs_all = dcs_all - t_all + jnp.where(last_row, dtot, 0.0)
        fxt16 = (f_w * xt).astype(BF16)
        db_acc = _dot_nt(fxt16, dhn16)
        dh_new = _dot_tn(c16, edy16) + per_head(exp_tot, first_row) * dhn
        for pr in range(n_pairs):
            dh_ref[pr] = dh_new[:, pr * 2 * HD:(pr + 1) * 2 * HD]
        g_sum = jnp.zeros((BLK, BLK), F32)
        gt_sum = jnp.zeros((BLK, BLK), F32)
        d_xt_parts = []
        for pr in range(n_pairs):
            cols = slice(pr * 2 * HD, (pr + 1) * 2 * HD)
            dym2 = _stack_heads(dy[:, cols].astype(BF16), first)
            d_m2 = _dot_nt(dym2, xt16[:, cols])
            d_mt2 = _dot_nt(xt16[:, cols], dym2)
            mt2 = []
            for e, h in enumerate((2 * pr, 2 * pr + 1)):
                cs_c, cs_r = cs[:, h:h + 1], cs_t[h:h + 1, :]
                decay = jnp.exp(jnp.where(tril, cs_c - cs_r, NEG))
                decay_t = jnp.exp(jnp.where(triu, cs_r - cs_c, NEG))
                gm = d_m2[e * BLK:(e + 1) * BLK] * decay
                gmt = d_mt2[:, e * BLK:(e + 1) * BLK] * decay_t
                g_sum += gm
                gt_sum += gmt
                dcs_h = jnp.sum(gm * cb, axis=-1, keepdims=True) - jnp.sum(gmt * cb_t, axis=-1, keepdims=True)
                dcs_all = dcs_all + jnp.where(lane8 == h, dcs_h, 0.0)
                mt2.append((cb_t * decay_t).astype(BF16))
            d_xt_parts.append(_dot(jnp.concatenate(mt2, axis=1), dym2))
        d_xt = jnp.concatenate(d_xt_parts, axis=1) + f_w * zmat
        dxs_ref[...] = dy * per_head(dsk_ref[...], first_row) + d_xt * dt_w
        ddtx_all = head_sums(d_xt * xs_all)

        dc_ref[...] = dc_acc + _dot(g_sum.astype(BF16), b16)
        db_ref[...] = db_acc + _dot(gt_sum.astype(BF16), c16)
        d_a = _dot_hi(upper, dcs_all)
        a_neg = -jnp.exp(alog_ref[...])
        ddt = ddtx_all + d_a * a_neg
        da_ref[...] += jnp.sum(d_a * dt, axis=0, keepdims=True)
        ddtr = ddt * _sigmoid(dtb)
        ddt_ref[...] = ddtr
        dbias_ref[...] += jnp.sum(ddtr, axis=0, keepdims=True)

    gps = SSD_GROUPS_PER_STEP
    k_wide, k_narrow, k_lead = ("cols", GROUP_W), ("cols", BLK), ("lead", 1)
    kinds = ([k_wide, k_narrow, k_narrow, ("each", 0), k_wide, k_wide] + [k_lead] * 8 + [k_wide]
             + [k_wide, k_narrow, k_narrow, k_wide] + [k_lead] * 4 + [k_wide] + [("lead", 4)])
    wide_w, narrow_w = GROUP_W * gps, BLK * gps
    rc = lambda c: nc - 1 - c
    gparam = pl.BlockSpec((gps, 1, 8), lambda g, c: (g, 0, 0))
    gparam_t = pl.BlockSpec((gps, 8, 1), lambda g, c: (g, 0, 0))
    wide = pl.BlockSpec((BLK, wide_w), lambda g, c: (rc(c), g))
    narrow = pl.BlockSpec((BLK, narrow_w), lambda g, c: (rc(c), g))
    z_specs = [pl.BlockSpec((BLK, GROUP_W),
                            functools.partial(lambda g, c, gi: (rc(c), COL_Z // GROUP_W + gps * g + gi), gi=gi))
               for gi in range(gps)]
    return _call(
        _per_group(body, gps, kinds), name=name,
        out_shape=[_sds((s, SSD_INNER), F32), _sds((s, GROUP_W), F32), _sds((s, GROUP_W), F32),
                   _sds((s, SSD_INNER), BF16), _sds((SSD_GROUPS, s, 8), F32),
                   _sds((SSD_GROUPS, 1, 8), F32), _sds((SSD_GROUPS, 1, 8), F32),
                   _sds((SSD_GROUPS, 1, GROUP_W), F32), _sds((1, SSD_INNER), F32)],
        grid=(SSD_GROUPS // gps, nc),
        in_specs=[wide,
                  pl.BlockSpec((BLK, narrow_w), lambda g, c: (rc(c), SSD_INNER // narrow_w + g)),
                  pl.BlockSpec((BLK, narrow_w), lambda g, c: (rc(c), (SSD_INNER + SSD_GROUPS * BLK) // narrow_w + g)),
                  *z_specs,
                  wide, wide,
                  pl.BlockSpec((gps, None, 4, BLK, 2 * HD), lambda g, c: (g, rc(c), 0, 0, 0)),
                  pl.BlockSpec((gps, BLK, 8), lambda g, c: (g, rc(c), 0)),
                  pl.BlockSpec((gps, 8, BLK), lambda g, c: (g, 0, rc(c))),
                  gparam, gparam_t, gparam, gparam_t, gparam,
                  pl.BlockSpec((1, wide_w), lambda g, c: (0, g))],
        out_specs=[wide, narrow, narrow, wide,
                   pl.BlockSpec((gps, BLK, 8), lambda g, c: (g, rc(c), 0)),
                   gparam, gparam,
                   pl.BlockSpec((gps, 1, GROUP_W), lambda g, c: (g, 0, 0)),
                   pl.BlockSpec((1, wide_w), lambda g, c: (0, g))],
        scratch_shapes=[pltpu.VMEM((4 * gps, BLK, 2 * HD), F32)],
    )(xc, xc, xc, *([p] * gps), y, d_sin, hprev, dtg, dtg_t, bias, bias_t, alog, alog_t, dskip, gn)


def merge_fwd(p, a, sbr, name, tm=512):
    s = p.shape[0]
    nj = D_MODEL // GROUP_W

    def body(ga_ref, gs_ref, a_ref, s_ref, o_ref):
        o_ref[...] = (_sigmoid(ga_ref[...].astype(F32)) * a_ref[...]
                      + _sigmoid(gs_ref[...].astype(F32)) * s_ref[...]).astype(BF16)

    blk = pl.BlockSpec((tm, GROUP_W), lambda i, j: (i, j))
    return _call(body, name=name, out_shape=_sds((s, D_MODEL), BF16), grid=(s // tm, nj),
                 in_specs=[pl.BlockSpec((tm, GROUP_W), lambda i, j: (i, COL_GA // GROUP_W + j)),
                           pl.BlockSpec((tm, GROUP_W), lambda i, j: (i, COL_GS // GROUP_W + j)), blk, blk],
                 out_specs=blk)(p, p, a, sbr)


def merge_bwd(p, a, sbr, dmerged, name, tm=512):
    s = p.shape[0]
    nj = D_MODEL // GROUP_W

    def body(ga_ref, gs_ref, a_ref, s_ref, dm_ref, da_ref, ds_ref, dga_ref, dgs_ref):
        dm = dm_ref[...]
        sa = _sigmoid(ga_ref[...].astype(F32))
        ss = _sigmoid(gs_ref[...].astype(F32))
        da_ref[...] = (dm * sa).astype(BF16)
        ds_ref[...] = (dm * ss).astype(BF16)
        dga_ref[...] = (dm * a_ref[...] * sa * (1.0 - sa)).astype(BF16)
        dgs_ref[...] = (dm * s_ref[...] * ss * (1.0 - ss)).astype(BF16)

    blk = pl.BlockSpec((tm, GROUP_W), lambda i, j: (i, j))
    shp = _sds((s, D_MODEL), BF16)
    return _call(body, name=name, out_shape=[shp] * 4, grid=(s // tm, nj),
                 in_specs=[pl.BlockSpec((tm, GROUP_W), lambda i, j: (i, COL_GA // GROUP_W + j)),
                           pl.BlockSpec((tm, GROUP_W), lambda i, j: (i, COL_GS // GROUP_W + j)), blk, blk, blk],
                 out_specs=[blk] * 4)(p, p, a, sbr, dmerged)


def _group_major(v):
    return v.reshape(SSD_GROUPS, 1, 8), v.reshape(SSD_GROUPS, 8, 1)


def mixer_forward(x, w, rider=None, later_weights=None):
    s = x.shape[0]
    h = rms_fwd(x, w["mix_norm"], "mix_rms")
    p = matmul_nn(h, w["w_in_main"], "mix_proj", BF16, tm=1024, tn=512, rider=rider)
    rode = None
    if rider is not None:
        p, rode = p
        w = dict(w, **later_weights(rode))
    dt_raw = matmul_nn(h, w["w_in_dt"], "mix_proj_dt", F32, tm=1024, tn=DT_PAD)
    qn, kn = qk_norm_fwd(p, w["q_gain"], w["k_gain"], "qk_norm")
    streams, os_, lses = [], [], []
    for g, d in enumerate(ATTN_DILATIONS):
        cols = slice(g * GROUP_W, (g + 1) * GROUP_W)
        qs, ks = _to_streams(qn[:, cols], d), _to_streams(kn[:, cols], d)
        vs = _to_streams(p[:, COL_V + g * GROUP_W:COL_V + (g + 1) * GROUP_W], d)
        o, lse = attn_fwd2(qs, ks, vs, g, f"attn_fwd{g}")
        streams.append((qs, ks, vs, lse))
        os_.append(_from_streams(o, d))
        lses.append(_from_streams(lse, d))
    attn_o = attn_merge_fwd(os_, lses, "attn_merge")
    cpre, xc = conv_fwd(p, w["conv_w"], w["conv_b"], "conv_fwd")
    dtg = dt_raw[:, :SSD_HEADS].reshape(s, SSD_GROUPS, 8).transpose(1, 0, 2)
    dtg_t = dtg.transpose(0, 2, 1)
    params = (*_group_major(w["dt_bias"]), *_group_major(w["a_log"]), _group_major(w["d_skip"])[0])
    y, s_in, hprev = ssd_fwd(p, xc, dtg, dtg_t, params, w["ssd_norm"], "ssd_fwd")
    a = matmul_nn(attn_o, w["w_attn_branch"], "attn_branch", F32, tm=1024, tn=512)
    sbr = matmul_nn(s_in, w["w_ssd_branch"], "ssd_branch", F32, tm=1024, tn=512)
    merged = merge_fwd(p, a, sbr, "merge")
    x_out = matmul_nn(merged, w["w_out"], "mix_out", F32, tm=1024, tn=512, res=x)
    saved = dict(h=h, p=p, streams=streams, os=os_, lses=lses, attn_o=attn_o, cpre=cpre, xc=xc, dtg=dtg,
                 dtg_t=dtg_t, params=params, y=y, s_in=s_in, hprev=hprev, a=a, sbr=sbr, merged=merged, w=w)
    return x_out, saved, rode


def mixer_backward(dx_out, x, sv, ride_early=None, ride_late=None):
    s = x.shape[0]
    p = sv["p"]
    w = sv["w"]
    g = {}
    dmerged = matmul_nt(dx_out, w["w_out"], "d_merged", F32, tm=1024, tn=512, tk=1024)
    g["w_out"] = matmul_tn(sv["merged"], dx_out, "dw_out", tn=512, ts=1024)
    da, ds, dga, dgs = merge_bwd(p, sv["a"], sv["sbr"], dmerged, "merge_bwd")
    g["w_attn_branch"] = matmul_tn(sv["attn_o"], da, "dw_attn_branch", tn=512, ts=1024)
    g["w_ssd_branch"] = matmul_tn(sv["s_in"], ds, "dw_ssd_branch", tn=512, ts=1024)
    d_attn_o = matmul_nt(da, w["w_attn_branch"], "d_attn_o", F32, tm=1024, tn=512, tk=1024)
    d_sin = matmul_nt(ds, w["w_ssd_branch"], "d_ssd_in", F32, tm=1024, tn=512, tk=1024)
    dxs, d_b, d_c, dz, ddt, d_asum, d_bias, d_dsk, d_gn = ssd_bwd(
        p, sv["xc"], sv["y"], d_sin, sv["hprev"], sv["dtg"], sv["dtg_t"], sv["params"], w["ssd_norm"], "ssd_bwd")
    dxbc, d_convw, d_convb = conv_bwd(p, sv["cpre"], dxs, d_b, d_c, w["conv_w"], "conv_bwd")
    g["conv_w"] = d_convw[:SSD_CONV]
    g["conv_b"] = d_convb
    g["dt_bias"] = d_bias.reshape(1, SSD_HEADS)
    g["a_log"] = (d_asum * (-jnp.exp(sv["params"][2]))).reshape(1, SSD_HEADS)
    g["d_skip"] = jnp.sum(d_dsk.reshape(SSD_HEADS, HD), axis=1).reshape(1, SSD_HEADS)
    g["ssd_norm"] = d_gn
    merged_bwd = attn_merge_bwd(d_attn_o, sv["os"], sv["lses"], "attn_merge_bwd")
    dqs, dks, dvs = [], [], []
    for gi, d in enumerate(ATTN_DILATIONS):
        qs, ks, vs, lse = sv["streams"][gi]
        d_o = _to_streams(merged_bwd[gi], d)
        cterm = _to_streams(merged_bwd[3 + gi], d)
        dq, dk, dv = attn_bwd2(qs, ks, vs, d_o, cterm, lse, gi, f"attn_bwd{gi}")
        dqs.append(_from_streams(dq, d))
        dks.append(_from_streams(dk, d))
        dvs.append(_from_streams(dv, d).astype(BF16))
    dqk, d_qg, d_kg = qk_norm_bwd(p, dqs, dks, w["q_gain"], w["k_gain"], "qk_norm_bwd")
    g["q_norm"] = jnp.sum(d_qg.reshape(N_ATTN_HEADS, HD), axis=0).reshape(1, HD)
    g["k_norm"] = jnp.sum(d_kg.reshape(N_ATTN_HEADS, HD), axis=0).reshape(1, HD)
    dp = [dqk, jnp.concatenate(dvs, axis=1), dz, dxbc, dga, dgs]
    ddt_pad = jnp.pad(ddt.transpose(1, 0, 2).reshape(s, SSD_HEADS), ((0, 0), (0, DT_PAD - SSD_HEADS)))
    if ride_early is not None:
        g["w_in_main"], g["rode_early"] = matmul_tn_pieces(sv["h"], dp, "dw_in", tn=512, ts=1024,
                                                           rider=ride_early(g))
    else:
        g["w_in_main"] = matmul_tn_pieces(sv["h"], dp, "dw_in", tn=512, ts=1024)
    g["w_in_dt"] = matmul_tn(sv["h"], ddt_pad, "dw_in_dt", tn=DT_PAD, ts=1024)
    if ride_late is not None:
        dh_main, g["rode_late"] = matmul_nt_pieces(dp, w["w_in_main"], "d_mix_h", F32, tm=1024, tn=1024, tk=512,
                                                   rider=ride_late(g))
    else:
        dh_main = matmul_nt_pieces(dp, w["w_in_main"], "d_mix_h", F32, tm=1024, tn=512, tk=512)
    dh_dt = matmul_nt(ddt_pad, w["w_in_dt"], "d_mix_h_dt", F32, tm=1024, tn=1024, tk=DT_PAD)
    dx, g["mix_norm"] = rms_bwd([dh_main, dh_dt], x, w["mix_norm"], dx_out, "mix_drms")
    return dx, g


ANY = pl.BlockSpec(memory_space=pl.ANY)


def _place():
    x, y, c = lax.axis_index("x"), lax.axis_index("y"), lax.axis_index("c")
    chips = [(1 - x, y), (x, 1 - y), (1 - x, 1 - y)]
    return x, y, c, 2 * x + y, chips


def _comm_call(body, *, name, out_shape, n_in, scratch_shapes, aliases=None):
    return pl.pallas_call(
        body, out_shape=out_shape, in_specs=[ANY] * n_in, out_specs=[ANY] * len(out_shape),
        scratch_shapes=scratch_shapes, input_output_aliases=aliases or {}, name=name,
        compiler_params=pltpu.CompilerParams(has_side_effects=True))


def gather_weights(shards, small):
    n = len(shards)
    halves = [a.shape[0] // 2 for a in shards]
    out_shape = [_sds((N_CHIP,) + a.shape, a.dtype) for a in shards] + [_sds((N_CHIP,) + small.shape, small.dtype)]

    def body(*refs):
        ins, outs = refs[:n + 1], refs[n + 1:2 * n + 2]
        send1, recv1, send2, recv2, local = refs[2 * n + 2:]
        x, y, c, me, chips = _place()
        sibling = (x, y, 1 - c)

        def rows(k, chip, core):
            if k == n:
                return outs[k].at[chip]
            return outs[k].at[chip, pl.ds(core * halves[k], halves[k])]

        def level1(k, t, incoming):
            chip = 2 * chips[t][0] + chips[t][1]
            src = ins[k] if k == n else ins[k].at[pl.ds(c * halves[k], halves[k])]
            return pltpu.make_async_remote_copy(
                src_ref=src, dst_ref=rows(k, chip if incoming else me, c), send_sem=send1.at[3 * k + t],
                recv_sem=recv1.at[3 * k + t], device_id=(*chips[t], c), device_id_type=MESH)

        def level2(k, t, incoming):
            chip = 2 * chips[t][0] + chips[t][1]
            core = (1 - c) if incoming else c
            return pltpu.make_async_remote_copy(
                src_ref=rows(k, chip, core), dst_ref=rows(k, chip, core), send_sem=send2.at[3 * k + t],
                recv_sem=recv2.at[3 * k + t], device_id=sibling, device_id_type=MESH)

        own = [pltpu.make_async_copy(ins[k], outs[k].at[me], local.at[k]) for k in range(n + 1)]
        for cp in own:
            cp.start()
        first = [level1(k, t, False) for k in range(n + 1) for t in range(3)]
        for cp in first:
            cp.start()
        passed = []
        for k in range(n + 1):
            for t in range(3):
                level1(k, t, True).wait_recv()
                if k < n:
                    cp = level2(k, t, False)
                    cp.start()
                    passed.append(cp)
        for k in range(n):
            for t in range(3):
                level2(k, t, True).wait_recv()
        for cp in first + passed:
            cp.wait_send()
        for cp in own:
            cp.wait()

    dma = pltpu.SemaphoreType.DMA
    return _comm_call(body, name="gather_weights", out_shape=out_shape, n_in=n + 1,
                      scratch_shapes=[dma((3 * n + 3,)), dma((3 * n + 3,)), dma((3 * n,)), dma((3 * n,)),
                                      dma((n + 1,))])(*shards, small)


def reduce_to_sibling(grads):
    n = len(grads)
    halves = [a.shape[1] // 2 for a in grads]
    shapes = [_sds((N_CHIP, h, a.shape[2]), a.dtype) for a, h in zip(grads, halves)]

    def body(*refs):
        ins, got, kept = refs[:n], refs[n:2 * n], refs[2 * n:3 * n]
        send, recv, local = refs[3 * n:]
        x, y, c, _, _ = _place()
        copies, locals_ = [], []
        for k in range(n):
            h = halves[k]
            locals_.append(pltpu.make_async_copy(ins[k].at[:, pl.ds(c * h, h)], kept[k], local.at[k]))
            copies.append(pltpu.make_async_remote_copy(
                src_ref=ins[k].at[:, pl.ds((1 - c) * h, h)], dst_ref=got[k], send_sem=send.at[k], recv_sem=recv.at[k],
                device_id=(x, y, 1 - c), device_id_type=MESH))
        for cp in locals_ + copies:
            cp.start()
        for cp in copies:
            cp.wait_recv()
        for cp in copies:
            cp.wait_send()
        for cp in locals_:
            cp.wait()

    dma = pltpu.SemaphoreType.DMA
    res = _comm_call(body, name="reduce_to_sibling", out_shape=shapes + shapes, n_in=n,
                     scratch_shapes=[dma((n,)), dma((n,)), dma((n,))])(*grads)
    return res[:n], res[n:]


def reduce_to_owner(sums):
    n = len(sums)
    shapes = [_sds(a.shape, a.dtype) for a in sums]

    def body(*refs):
        ins, outs = refs[:n], refs[n:2 * n]
        send, recv, local = refs[2 * n:]
        x, y, c, me, chips = _place()
        copies, locals_ = [], []
        for k in range(n):
            locals_.append(pltpu.make_async_copy(ins[k].at[me], outs[k].at[3], local.at[k]))
            for t in range(3):
                chip = 2 * chips[t][0] + chips[t][1]
                copies.append(pltpu.make_async_remote_copy(
                    src_ref=ins[k].at[chip], dst_ref=outs[k].at[t], send_sem=send.at[3 * k + t],
                    recv_sem=recv.at[3 * k + t], device_id=(*chips[t], c), device_id_type=MESH))
        for cp in locals_ + copies:
            cp.start()
        for cp in copies:
            cp.wait_recv()
        for cp in copies:
            cp.wait_send()
        for cp in locals_:
            cp.wait()

    dma = pltpu.SemaphoreType.DMA
    return _comm_call(body, name="reduce_to_owner", out_shape=shapes, n_in=n,
                      scratch_shapes=[dma((3 * n,)), dma((3 * n,)), dma((n,))])(*sums)


def share_with_sibling(halves_):
    n = len(halves_)
    shapes = [_sds((2 * a.shape[0], a.shape[1]), a.dtype) for a in halves_]

    def body(*refs):
        ins, outs = refs[:n], refs[n:2 * n]
        send, recv, local = refs[2 * n:]
        x, y, c, _, _ = _place()
        copies, locals_ = [], []
        for k in range(n):
            h = ins[k].shape[0]
            mine = outs[k].at[pl.ds(c * h, h)]
            locals_.append(pltpu.make_async_copy(ins[k], mine, local.at[k]))
            copies.append(pltpu.make_async_remote_copy(
                src_ref=ins[k], dst_ref=mine, send_sem=send.at[k], recv_sem=recv.at[k],
                device_id=(x, y, 1 - c), device_id_type=MESH))
        for cp in locals_ + copies:
            cp.start()
        for cp in copies:
            cp.wait_recv()
        for cp in copies:
            cp.wait_send()
        for cp in locals_:
            cp.wait()

    dma = pltpu.SemaphoreType.DMA
    return _comm_call(body, name="share_with_sibling", out_shape=shapes, n_in=n,
                      scratch_shapes=[dma((n,)), dma((n,)), dma((n,))])(*halves_)


def _cores():
    c = lax.axis_index("c")
    return jnp.stack([c, 1 - c]).astype(jnp.int32)


def _staged_call(body, *, name, grid, in_specs, out_specs, out_shape, scratch_shapes):
    return pl.pallas_call(
        body, out_shape=out_shape, name=name,
        grid_spec=pltpu.PrefetchScalarGridSpec(num_scalar_prefetch=1, grid=grid, in_specs=in_specs,
                                               out_specs=out_specs, scratch_shapes=scratch_shapes),
        compiler_params=pltpu.CompilerParams(dimension_semantics=("arbitrary",) * len(grid),
                                             vmem_limit_bytes=V7X_VMEM_LIMIT, has_side_effects=True))


def gather_rider(shards, tiles):
    dma = pltpu.SemaphoreType.DMA
    n = len(shards)
    geo = [(a.shape[0] // 2, tm, (a.shape[0] // 2) // tm) for a, tm in zip(shards, tiles)]
    scratch = []
    for a, (h, tm, nk) in zip(shards, geo):
        scratch += [pltpu.VMEM((N_CHIP,) + a.shape, a.dtype), dma((3, nk)), dma((3, nk)), dma((3, nk)), dma((3, nk)),
                    dma((nk + 2,))]

    def copies(j, in_ref, scr):
        buf, send1, recv1, send2, recv2, local = scr[6 * j:6 * j + 6]
        h, tm, nk = geo[j]
        x, y, c, me, chips = _place()
        chip_of = [2 * chips[t][0] + chips[t][1] for t in range(3)]

        def rows(chip, core, k):
            return buf.at[chip, pl.ds(core * h + k * tm, tm)]

        def mine(k):
            if k == nk:
                return pltpu.make_async_copy(in_ref.at[pl.ds((1 - c) * h, h)], buf.at[me, pl.ds((1 - c) * h, h)],
                                             local.at[nk])
            return pltpu.make_async_copy(in_ref.at[pl.ds(c * h + k * tm, tm)], rows(me, c, k), local.at[k])

        def level1(t, k, incoming):
            place = rows(chip_of[t] if incoming else me, c, k)
            return pltpu.make_async_remote_copy(src_ref=place, dst_ref=place, send_sem=send1.at[t, k],
                                                recv_sem=recv1.at[t, k], device_id=(*chips[t], c), device_id_type=MESH)

        def level2(t, k, incoming):
            place = rows(chip_of[t], (1 - c) if incoming else c, k)
            return pltpu.make_async_remote_copy(src_ref=place, dst_ref=place, send_sem=send2.at[t, k],
                                                recv_sem=recv2.at[t, k], device_id=(x, y, 1 - c),
                                                device_id_type=MESH)

        return buf, local, nk, mine, level1, level2

    def start(ins, outs, scr):
        for j in range(n):
            _, _, nk, mine, _, _ = copies(j, ins[j], scr)
            for k in range(nk + 1):
                mine(k).start()
        for j in range(n):
            _, _, nk, mine, level1, _ = copies(j, ins[j], scr)
            for k in range(nk):
                mine(k).wait()
                for t in range(3):
                    level1(t, k, False).start()

    def finish(ins, outs, scr):
        for j in range(n):
            _, _, nk, _, level1, level2 = copies(j, ins[j], scr)
            for k in range(nk):
                for t in range(3):
                    level1(t, k, True).wait_recv()
                    level2(t, k, False).start()
        for j in range(n):
            buf, local, nk, mine, level1, level2 = copies(j, ins[j], scr)
            for k in range(nk):
                for t in range(3):
                    level2(t, k, True).wait_recv()
            for k in range(nk):
                for t in range(3):
                    level1(t, k, False).wait_send()
                    level2(t, k, False).wait_send()
            mine(nk).wait()
            pltpu.make_async_copy(buf, outs[j], local.at[nk + 1]).start()
        for j in range(n):
            buf, local, nk, _, _, _ = copies(j, ins[j], scr)
            pltpu.make_async_copy(buf, outs[j], local.at[nk + 1]).wait()

    return Rider(list(shards), [_sds((N_CHIP,) + a.shape, a.dtype) for a in shards], scratch, start, finish)


def run_alone(rider, name):
    return _call(lambda: None, name=name, out_shape=[], in_specs=[], out_specs=[], grid=(1,), rider=rider)()[1]


def sibling_sum(g, tm, name):
    _, r, cdim = g.shape
    h = r // 2
    ni = h // tm
    dma = pltpu.SemaphoreType.DMA

    def body(cores_ref, keep_ref, give_ref, out_ref, slot, send, recv):
        par = (pl.program_id(0) * ni + pl.program_id(1)) % 2
        x, y, c, _, _ = _place()
        cp = pltpu.make_async_remote_copy(src_ref=give_ref, dst_ref=slot.at[par], send_sem=send.at[par],
                                          recv_sem=recv.at[par], device_id=(x, y, 1 - c), device_id_type=MESH)
        cp.start()
        cp.wait_recv()
        out_ref[...] = (keep_ref[...].astype(F32) + slot[par].astype(F32)).astype(out_ref.dtype)
        cp.wait_send()

    flat = g.reshape(N_CHIP * r, cdim)
    return _staged_call(
        body, name=name, grid=(N_CHIP, ni),
        in_specs=[pl.BlockSpec((tm, cdim), lambda j, i, cores: ((2 * j + cores[0]) * ni + i, 0)),
                  pl.BlockSpec((tm, cdim), lambda j, i, cores: ((2 * j + cores[1]) * ni + i, 0))],
        out_specs=pl.BlockSpec((None, tm, cdim), lambda j, i, cores: (j, i, 0)),
        out_shape=_sds((N_CHIP, h, cdim), g.dtype),
        scratch_shapes=[pltpu.VMEM((2, tm, cdim), g.dtype), dma((2,)), dma((2,))],
    )(_cores(), flat, flat)


def owner_sum_rider(sums, tiles):
    dma = pltpu.SemaphoreType.DMA
    n = len(sums)
    geo = [(a.shape[1], tm, a.shape[1] // tm) for a, tm in zip(sums, tiles)]
    scratch = []
    for a, (h, tm, nk) in zip(sums, geo):
        cdim = a.shape[2]
        scratch += [pltpu.VMEM(a.shape, a.dtype), pltpu.VMEM((3, h, cdim), a.dtype), pltpu.VMEM((2, h, cdim), F32),
                    dma((3, nk)), dma((3, nk)), dma((nk,)), dma((nk,)), dma((2,))]

    def copies(j, scr):
        part, got, res, send, recv, send2, recv2, local = scr[8 * j:8 * j + 8]
        h, tm, nk = geo[j]
        x, y, c, me, chips = _place()

        def to_owner(t, k):
            chip = 2 * chips[t][0] + chips[t][1]
            return pltpu.make_async_remote_copy(
                src_ref=part.at[chip, pl.ds(k * tm, tm)], dst_ref=got.at[t, pl.ds(k * tm, tm)],
                send_sem=send.at[t, k], recv_sem=recv.at[t, k], device_id=(*chips[t], c), device_id_type=MESH)

        def to_sibling(k):
            place = res.at[c, pl.ds(k * tm, tm)]
            return pltpu.make_async_remote_copy(src_ref=place, dst_ref=place, send_sem=send2.at[k],
                                                recv_sem=recv2.at[k], device_id=(x, y, 1 - c), device_id_type=MESH)

        return part, got, res, local, to_owner, to_sibling, (tm, nk, c, me)

    def start(ins, outs, scr):
        for j in range(n):
            part, _, _, local, _, _, _ = copies(j, scr)
            pltpu.make_async_copy(ins[j], part, local.at[0]).start()
        for j in range(n):
            part, _, _, local, to_owner, _, (tm, nk, c, me) = copies(j, scr)
            pltpu.make_async_copy(ins[j], part, local.at[0]).wait()
            for k in range(nk):
                for t in range(3):
                    to_owner(t, k).start()

    def finish(ins, outs, scr):
        for j in range(n):
            part, got, res, _, to_owner, to_sibling, (tm, nk, c, me) = copies(j, scr)
            for k in range(nk):
                rows = pl.ds(k * tm, tm)
                for t in range(3):
                    to_owner(t, k).wait_recv()
                acc = part[me, rows, :].astype(F32)
                for t in range(3):
                    acc = acc + got[t, rows, :].astype(F32)
                res[c, rows, :] = acc
                to_sibling(k).start()
        for j in range(n):
            _, _, res, local, to_owner, to_sibling, (tm, nk, c, me) = copies(j, scr)
            for k in range(nk):
                to_sibling(k).wait_recv()
            for k in range(nk):
                to_sibling(k).wait_send()
                for t in range(3):
                    to_owner(t, k).wait_send()
            pltpu.make_async_copy(res, outs[j], local.at[1]).start()
        for j in range(n):
            _, _, res, local, _, _, _ = copies(j, scr)
            pltpu.make_async_copy(res, outs[j], local.at[1]).wait()

    return Rider(list(sums), [_sds((2, a.shape[1], a.shape[2]), F32) for a in sums], scratch, start, finish)


def gather_conv_w(w):
    def body(in_ref, out_ref, send, recv):
        x, y, c, me, chips = _place()
        out_ref[me] = in_ref[...]
        copies = []
        for t in range(3):
            copies.append(pltpu.make_async_remote_copy(
                src_ref=out_ref.at[me], dst_ref=out_ref.at[me], send_sem=send.at[t], recv_sem=recv.at[t],
                device_id=(*chips[t], c), device_id_type=MESH))
        for cp in copies:
            cp.start()
        for cp in copies:
            cp.wait_recv()
        for cp in copies:
            cp.wait_send()

    dma = pltpu.SemaphoreType.DMA
    vmem = pl.BlockSpec(memory_space=pltpu.VMEM)
    return pl.pallas_call(
        body, out_shape=_sds((N_CHIP,) + w.shape, w.dtype), in_specs=[vmem], out_specs=vmem, name="gather_conv_w",
        scratch_shapes=[dma((3,)), dma((3,))],
        compiler_params=pltpu.CompilerParams(has_side_effects=True))(w)


N_DEV = 8
SMALL_ROWS = 32
SMALL_LANES = 1024


def all_reduce_small(arrays):
    n_arr = len(arrays)
    places = []
    for k, a in enumerate(arrays):
        for ri in range(a.shape[0]):
            for c0 in range(0, a.shape[1], SMALL_LANES):
                places.append((k, ri, c0, min(SMALL_LANES, a.shape[1] - c0), len(places)))
    assert len(places) <= SMALL_ROWS

    def body(*refs):
        ins, outs = refs[:n_arr], refs[n_arr:2 * n_arr]
        buf, send, recv = refs[2 * n_arr:]
        x, y, c, _, _ = _place()
        me = 4 * x + 2 * y + c
        buf[me] = jnp.zeros((SMALL_ROWS, SMALL_LANES), F32)
        for k, ri, c0, width, row in places:
            buf[me, row:row + 1, 0:width] = ins[k][ri:ri + 1, c0:c0 + width]
        copies = []
        for r in range(1, N_DEV):
            px = (1 - x) if r & 4 else x
            py = (1 - y) if r & 2 else y
            pc = (1 - c) if r & 1 else c
            copies.append(pltpu.make_async_remote_copy(
                src_ref=buf.at[me], dst_ref=buf.at[me], send_sem=send.at[r - 1], recv_sem=recv.at[r - 1],
                device_id=(px, py, pc), device_id_type=MESH))
        for cp in copies:
            cp.start()
        for cp in copies:
            cp.wait_recv()
        for cp in copies:
            cp.wait_send()
        acc = buf[0]
        for j in range(1, N_DEV):
            acc = acc + buf[j]
        for k, ri, c0, width, row in places:
            outs[k][ri:ri + 1, c0:c0 + width] = acc[row:row + 1, 0:width]

    dma = pltpu.SemaphoreType.DMA
    vmem = pl.BlockSpec(memory_space=pltpu.VMEM)
    return pl.pallas_call(
        body, out_shape=[_sds(a.shape, F32) for a in arrays], in_specs=[vmem] * n_arr, out_specs=[vmem] * n_arr,
        name="all_reduce_small",
        scratch_shapes=[pltpu.VMEM((N_DEV, SMALL_ROWS, SMALL_LANES), F32), dma((N_DEV - 1,)), dma((N_DEV - 1,))],
        compiler_params=pltpu.CompilerParams(has_side_effects=True))(*arrays)


def _row_tile(rows, limit, multiple):
    return max(t for t in range(multiple, min(rows, limit) + 1, multiple) if rows % t == 0)


def add_pair(a, b, name):
    _, h, c = a.shape

    def body(a_ref, b_ref, o_ref):
        o_ref[...] = (a_ref[...].astype(F32) + b_ref[...].astype(F32)).astype(o_ref.dtype)

    blk = pl.BlockSpec((None, h, c), lambda j: (j, 0, 0))
    return _call(body, name=name, out_shape=_sds(a.shape, a.dtype), grid=(N_CHIP,), in_specs=[blk, blk],
                 out_specs=blk)(a, b)


def sum_slots(buf, name):
    _, h, c = buf.shape
    tm = _row_tile(h, 256, 16)

    def body(b_ref, o_ref):
        acc = b_ref[3].astype(F32)
        for t in range(3):
            acc = acc + b_ref[t].astype(F32)
        o_ref[...] = acc

    return _call(body, name=name, out_shape=_sds((h, c), F32), grid=(h // tm,),
                 in_specs=[pl.BlockSpec((N_CHIP, tm, c), lambda i: (0, i, 0))],
                 out_specs=pl.BlockSpec((tm, c), lambda i: (i, 0)))(buf)


def _adamw_math(w, g, m, v):
    c1 = 1.0 - ADAM_B1 ** ADAM_STEP
    c2 = 1.0 - ADAM_B2 ** ADAM_STEP
    m2 = ADAM_B1 * m + (1.0 - ADAM_B1) * g
    v2 = ADAM_B2 * v + (1.0 - ADAM_B2) * (g * g)
    return -ADAM_LR * ((m2 / c1) / (jnp.sqrt(v2 / c2) + ADAM_EPS) + ADAM_WD * w), m2, v2


def adamw(w, g, row_off, m, v, name):
    _, r, c = w.shape
    tm = r if r < 8 else _row_tile(math.gcd(r, row_off) if row_off else r, 128, 8)

    def body(w_ref, g_ref, m_ref, v_ref, go_ref, d_ref, m2_ref, v2_ref):
        gv = g_ref[...]
        go_ref[...] = gv
        d_ref[...], m2_ref[...], v2_ref[...] = _adamw_math(w_ref[...], gv, m_ref[...], v_ref[...])

    blk = pl.BlockSpec((None, tm, c), lambda i: (0, i, 0))
    shp = _sds((1, r, c), F32)
    return _call(body, name=name, out_shape=[shp] * 4, grid=(r // tm,),
                 in_specs=[blk, pl.BlockSpec((tm, c), lambda i: (row_off // tm + i, 0)), blk, blk],
                 out_specs=[blk] * 4)(w, g, m, v)


def adamw_small(ws, gs, ms, vs):
    n = len(ws)

    def body(*refs):
        ins, outs = refs[:4 * n], refs[4 * n:]
        for k in range(n):
            w_ref, g_ref, m_ref, v_ref = (ins[j * n + k] for j in range(4))
            outs[k][...], outs[n + k][...], outs[2 * n + k][...] = _adamw_math(w_ref[...], g_ref[...], m_ref[...],
                                                                               v_ref[...])

    vmem = pl.BlockSpec(memory_space=pltpu.VMEM)
    shapes = [_sds(w.shape, F32) for w in ws] * 3
    res = pl.pallas_call(body, out_shape=shapes, in_specs=[vmem] * (4 * n), out_specs=[vmem] * (3 * n),
                         name="adamw_small")(*ws, *gs, *ms, *vs)
    return res[:n], res[n:2 * n], res[2 * n:]


BIG = ("ffn1_w_gate", "ffn1_w_up", "ffn1_w_down", "w_in", "w_attn_branch", "w_ssd_branch", "w_out",
       "ffn2_w_gate", "ffn2_w_up", "ffn2_w_down")
SMALL = ("ffn1_norm", "mix_norm", "q_norm", "k_norm", "conv_b", "dt_bias", "a_log", "d_skip", "ssd_norm", "ffn2_norm")
WEIGHTS = ("ffn1_norm", "ffn1_w_gate", "ffn1_w_up", "ffn1_w_down", "mix_norm", "w_in", "q_norm", "k_norm", "conv_w",
           "conv_b", "dt_bias", "a_log", "d_skip", "ssd_norm", "w_attn_branch", "w_ssd_branch", "w_out", "ffn2_norm",
           "ffn2_w_gate", "ffn2_w_up", "ffn2_w_down")
CONV_SHARD = SSD_CONV_DIM // N_CHIP
CLASSES = {
    "ffn1_in": (("ffn1_w_gate", 1024), ("ffn1_w_up", 1024)),
    "ffn1_out": (("ffn1_w_down", 704),),
    "mix_in": (("w_in", 1024),),
    "mix_attn": (("w_attn_branch", 512),),
    "late_out": (("ffn2_w_down", 704), ("w_ssd_branch", 512), ("w_out", 256)),
    "ffn2_in": (("ffn2_w_gate", 1024), ("ffn2_w_up", 1024)),
}
CLASS_TILE = {"ffn1_in": 256, "ffn1_out": 176, "mix_in": 128, "mix_attn": 256, "late_out": 368, "ffn2_in": 256,
              "mix_in_top": 128, "mix_in_bottom": 128}
SIBLING_TILE = {"ffn1_in": 1024, "ffn1_out": 352, "mix_attn": 256, "late_out": 736, "ffn2_in": 1024,
                "mix_in_top": 256, "mix_in_bottom": 256}


def _pack_small(vals, conv_part, loss_part=None):
    flat = [vals[k].reshape(-1) for k in SMALL]
    flat.append(jnp.zeros((SSD_CONV * SSD_CONV_DIM,), F32) if conv_part is None else conv_part.reshape(-1))
    flat.append(jnp.zeros((1,), F32) if loss_part is None else loss_part.reshape(1))
    flat = jnp.concatenate(flat)
    return jnp.pad(flat, (0, SMALL_ROWS * D_MODEL - flat.shape[0])).reshape(SMALL_ROWS, D_MODEL)


def _unpack_small(pack, like):
    flat = pack.reshape(-1)
    out, off = {}, 0
    for k in SMALL:
        n = like[k].size
        out[k] = flat[off:off + n].reshape(like[k].shape)
        off += n
    conv = flat[off:off + SSD_CONV * SSD_CONV_DIM].reshape(SSD_CONV, SSD_CONV_DIM)
    return out, conv, flat[off + SSD_CONV * SSD_CONV_DIM]


def _chip_major_cols(a):
    r = a.shape[0]
    return a.reshape(r, N_CHIP, -1).transpose(1, 0, 2)


def _from_chip_major_cols(a):
    return a.transpose(1, 0, 2).reshape(a.shape[1], -1)


def kernel(x, ffn1_norm, ffn1_w_gate, ffn1_w_up, ffn1_w_down, mix_norm, w_in, q_norm, k_norm, conv_w, conv_b, dt_bias, a_log, d_skip, ssd_norm, w_attn_branch, w_ssd_branch, w_out, ffn2_norm, ffn2_w_gate, ffn2_w_up, ffn2_w_down, loss_target, m_ffn1_norm, m_ffn1_w_gate, m_ffn1_w_up, m_ffn1_w_down, m_mix_norm, m_w_in, m_q_norm, m_k_norm, m_conv_w, m_conv_b, m_dt_bias, m_a_log, m_d_skip, m_ssd_norm, m_w_attn_branch, m_w_ssd_branch, m_w_out, m_ffn2_norm, m_ffn2_w_gate, m_ffn2_w_up, m_ffn2_w_down, v_ffn1_norm, v_ffn1_w_gate, v_ffn1_w_up, v_ffn1_w_down, v_mix_norm, v_w_in, v_q_norm, v_k_norm, v_conv_w, v_conv_b, v_dt_bias, v_a_log, v_d_skip, v_ssd_norm, v_w_attn_branch, v_w_ssd_branch, v_w_out, v_ffn2_norm, v_ffn2_w_gate, v_ffn2_w_up, v_ffn2_w_down):
    env = dict(locals())
    wts = {k: env[k] for k in WEIGHTS}
    moms = {k: env["m_" + k] for k in WEIGHTS}
    vars_ = {k: env["v_" + k] for k in WEIGHTS}
    x0 = x[0]
    target = loss_target[0]

    def gather(classes):
        shards = [jnp.concatenate([wts[k][0] for k, _ in CLASSES[c]], axis=0).astype(BF16) for c in classes]
        return gather_rider(shards, [CLASS_TILE[c] for c in classes])

    def reducer(classes, parts):
        sums = [sibling_sum(p, SIBLING_TILE[c], f"sibling_sum_{c}") for c, p in zip(classes, parts)]
        return owner_sum_rider(sums, [CLASS_TILE[c] for c in classes])

    x1, saved1, (w_ffn1_in,), (w_ffn1_out, w_mix_in, w_mix_attn) = ffn_forward(
        x0, ffn1_norm, lambda rode: rode[0], lambda rode: rode[0], "ffn1", rms_rider=gather(["ffn1_in"]),
        up_rider=gather(["ffn1_out", "mix_in", "mix_attn"]))
    dt0, dt1 = IN_DT0 - 3 * IN_SHARD, IN_DT1 - 3 * IN_SHARD
    mixer_w = dict(
        mix_norm=mix_norm,
        w_in_main=jnp.concatenate([w_mix_in[0], w_mix_in[1], w_mix_in[2], w_mix_in[3][:, :dt0], w_mix_in[3][:, dt1:]],
                                  axis=1),
        w_in_dt=jnp.pad(w_mix_in[3][:, dt0:dt1], ((0, 0), (0, DT_PAD - SSD_HEADS))),
        q_gain=jnp.tile(q_norm, (1, 2)), k_gain=jnp.tile(k_norm, (1, 2)),
        conv_w=_from_chip_major_cols(gather_conv_w(conv_w[0])), conv_b=conv_b, dt_bias=dt_bias, a_log=a_log,
        d_skip=d_skip, ssd_norm=ssd_norm, w_attn_branch=_from_chip_major_cols(w_mix_attn))

    def later_weights(rode):
        late = rode[0]
        return dict(w_ssd_branch=late[:, 704:1216].reshape(SSD_INNER, D_MODEL),
                    w_out=late[:, 1216:1472].reshape(D_MODEL, D_MODEL))

    x2, saved_mix, (w_late_out, w_ffn2_in) = mixer_forward(x1, mixer_w, gather(["late_out", "ffn2_in"]), later_weights)
    x3, saved2, _, _ = ffn_forward(x2, ffn2_norm, lambda rode: w_ffn2_in, lambda rode: w_late_out, "ffn2")
    dx3, sq = loss_grad(x3, target, "loss")

    grads = {}
    dx2, grads["ffn2_norm"], d_ffn2_in, d_ffn2_down = ffn_backward(dx3, x2, ffn2_norm, w_ffn2_in, w_late_out, saved2,
                                                                   "ffn2")

    def ride_early(g):
        late = jnp.concatenate([d_ffn2_down, g["w_ssd_branch"].reshape(N_CHIP, -1, D_MODEL),
                                g["w_out"].reshape(N_CHIP, -1, D_MODEL)], axis=1)
        return reducer(["ffn2_in", "late_out"], [d_ffn2_in, late])

    g_in_rows = {}

    def ride_late(g):
        main = g["w_in_main"]
        last = jnp.concatenate([main[:, 3 * IN_SHARD:IN_DT0], g["w_in_dt"][:, :SSD_HEADS], main[:, IN_DT0:]], axis=1)
        for part, rows in (("top", slice(0, D_MODEL // 2)), ("bottom", slice(D_MODEL // 2, D_MODEL))):
            g_in_rows[part] = jnp.stack([main[rows, j * IN_SHARD:(j + 1) * IN_SHARD] for j in range(3)]
                                        + [last[rows]])
        return reducer(["mix_in_top", "mix_attn"], [g_in_rows["top"], _chip_major_cols(g["w_attn_branch"])])

    dx1, gmix = mixer_backward(dx2, x1, saved_mix, ride_early, ride_late)
    dx0, grads["ffn1_norm"], rode_in, rode_out = ffn_backward(
        dx1, x0, ffn1_norm, w_ffn1_in, w_ffn1_out, saved1, "ffn1",
        ride_down=lambda d: reducer(["mix_in_bottom", "ffn1_out"], [g_in_rows["bottom"], d]),
        ride_in=lambda d: reducer(["ffn1_in"], [d]))
    for k in ("mix_norm", "q_norm", "k_norm", "conv_b", "dt_bias", "a_log", "d_skip", "ssd_norm"):
        grads[k] = gmix[k]
    reduced = dict(zip(("ffn2_in", "late_out", "mix_in_top", "mix_attn", "ffn1_in", "mix_in_bottom", "ffn1_out"),
                       (*gmix["rode_early"], *gmix["rode_late"], rode_in[0], *rode_out)))
    reduced = {c: r.reshape(-1, r.shape[2]) for c, r in reduced.items()}
    reduced["mix_in"] = jnp.concatenate([reduced.pop("mix_in_top"), reduced.pop("mix_in_bottom")], axis=0)
    summed = all_reduce_small([grads[k] for k in SMALL]
                              + [gmix["conv_w"], (0.5 * jnp.sum(sq) / D_MODEL).reshape(1, 1)])
    g_small = dict(zip(SMALL, summed))
    loss = summed[-1].reshape(())
    chip = 2 * lax.axis_index("x") + lax.axis_index("y")
    g_conv = lax.dynamic_slice_in_dim(summed[-2], chip * CONV_SHARD, CONV_SHARD, axis=1)

    g_final, delta, new_m, new_v = dict(g_small), {}, {}, {}

    def update(k, g_arr, row_off):
        w, m, v = wts[k], moms[k], vars_[k]
        rows, cols = w.shape[1:]
        if cols % 128:
            res = adamw(jnp.swapaxes(w, 1, 2), g_arr[row_off:row_off + rows].T, 0, jnp.swapaxes(m, 1, 2),
                        jnp.swapaxes(v, 1, 2), f"adamw_{k}")
            res = [jnp.swapaxes(r, 1, 2) for r in res]
        else:
            res = adamw(w, g_arr, row_off, m, v, f"adamw_{k}")
        g_final[k], delta[k], new_m[k], new_v[k] = res

    for cls, members in CLASSES.items():
        off = 0
        for k, rows in members:
            update(k, reduced[cls], off)
            off += rows
    update("conv_w", g_conv, 0)
    small = adamw_small(*([d[k] for k in SMALL] for d in (wts, g_small, moms, vars_)))
    for res, vals in zip((delta, new_m, new_v), small):
        res.update(zip(SMALL, vals))

    return (loss, dx0[None], *[g_final[k] for k in WEIGHTS], *[delta[k] for k in WEIGHTS],
            *[new_m[k] for k in WEIGHTS], *[new_v[k] for k in WEIGHTS])
```

```python
import collections
import functools
import math

import jax
import jax.numpy as jnp
from jax import lax
from jax.experimental import pallas as pl
from jax.experimental.pallas import tpu as pltpu

F32 = jnp.float32
BF16 = jnp.bfloat16
MESH = pl.DeviceIdType.MESH

EPS = 1e-6
D_MODEL = 1024
D_FF = 2816
N_CHIP = 4
FF_SHARD = D_FF // N_CHIP
HD = 64
BLK = 128
ATTN_DILATIONS = (1, 4, 16)
HEADS_PER_PATTERN = 8
N_ATTN_HEADS = 24
ALIBI_MAX_EXP = 8.0
ATTN_QKV = 1536
GROUP_W = 512
SSD_INNER = 2048
SSD_HEADS = 32
SSD_GROUPS = 4
SSD_CONV = 4
SSD_CONV_DIM = 3072
IN_COLS = 11808
IN_DT0, IN_DT1 = 9728, 9760
IN_SHARD = IN_COLS // 4
COL_K, COL_V, COL_Z, COL_XBC, COL_GA, COL_GS, P_COLS = 1536, 3072, 4608, 6656, 9728, 10752, 11776
DT_PAD = 128

ADAM_LR, ADAM_B1, ADAM_B2, ADAM_EPS, ADAM_WD, ADAM_STEP = 0.001, 0.9, 0.999, 1e-08, 0.01, 10

V7X_VMEM_LIMIT = 56 * 1024 * 1024
NEG = -1e30


Rider = collections.namedtuple("Rider", "arrays out_shape scratch start finish")
Rider.__doc__ = """An exchange between devices that rides in a compute kernel: its copies are started in the host's
first grid step and waited for in its last, so they travel while the host computes.  arrays / out_shape: extra HBM
operands and results; scratch: extra scratch; start, finish: f(in_refs, out_refs, scratch_refs)."""


def _call(body, *, name, out_shape, in_specs, out_specs, grid=(), scratch_shapes=(), aliases=None, rider=None):
    params = dict(dimension_semantics=("arbitrary",) * len(grid), vmem_limit_bytes=V7X_VMEM_LIMIT)
    if rider is None:
        return pl.pallas_call(
            body, out_shape=out_shape, grid=grid, in_specs=in_specs, out_specs=out_specs,
            scratch_shapes=scratch_shapes, input_output_aliases=aliases or {}, name=name,
            compiler_params=pltpu.CompilerParams(**params))
    single = not isinstance(out_shape, (list, tuple))
    main_out = [out_shape] if single else list(out_shape)
    main_specs = [out_specs] if single else list(out_specs)
    n_in, n_out, n_scr = len(in_specs), len(main_out), len(scratch_shapes)
    r_in, r_out = len(rider.arrays), len(rider.out_shape)

    def wrapped(*refs):
        ins, refs = refs[:n_in], refs[n_in:]
        r_ins, refs = refs[:r_in], refs[r_in:]
        outs, refs = refs[:n_out], refs[n_out:]
        r_outs, refs = refs[:r_out], refs[r_out:]
        scr, r_scr = refs[:n_scr], refs[n_scr:]
        first = last = None
        for axis, size in enumerate(grid):
            at_start, at_end = pl.program_id(axis) == 0, pl.program_id(axis) == size - 1
            first = at_start if first is None else jnp.logical_and(first, at_start)
            last = at_end if last is None else jnp.logical_and(last, at_end)

        @pl.when(first)
        def _():
            rider.start(r_ins, r_outs, r_scr)

        body(*ins, *outs, *scr)

        @pl.when(last)
        def _():
            rider.finish(r_ins, r_outs, r_scr)

    hbm = pl.BlockSpec(memory_space=pl.ANY)
    call = pl.pallas_call(
        wrapped, out_shape=main_out + list(rider.out_shape), grid=grid, in_specs=list(in_specs) + [hbm] * r_in,
        out_specs=main_specs + [hbm] * r_out, scratch_shapes=list(scratch_shapes) + list(rider.scratch), name=name,
        compiler_params=pltpu.CompilerParams(has_side_effects=True, **params))

    def run(*args):
        res = call(*args, *rider.arrays)
        main = res[:n_out]
        return (main[0] if single else main), res[n_out:]

    return run


def _sds(shape, dtype):
    return jax.ShapeDtypeStruct(tuple(shape), dtype)


def _dot(a, b):
    return jnp.dot(a, b, preferred_element_type=F32)


def _dot_nt(a, b):
    return lax.dot_general(a, b, (((1,), (1,)), ((), ())), preferred_element_type=F32)


def _dot_tn(a, b):
    return lax.dot_general(a, b, (((0,), (0,)), ((), ())), preferred_element_type=F32)


def _dot_hi(a, b):
    return jnp.dot(a, b, preferred_element_type=F32, precision=lax.Precision.HIGHEST)


def _sigmoid(x):
    return 1.0 / (1.0 + jnp.exp(-x))


def _lane_first_half(shape):
    return lax.broadcasted_iota(jnp.int32, shape, len(shape) - 1) < HD


def _pair_sum(x, first):
    s_all = jnp.sum(x, axis=-1, keepdims=True)
    s_a = jnp.sum(jnp.where(first, x, 0.0), axis=-1, keepdims=True)
    return s_a, s_all - s_a


def _rowwise(name, fn, rows, consts, outs, accs=(), tm=512, rider=None):
    n_rows = None
    in_arrays, in_specs = [], []
    for r in rows:
        if isinstance(r, tuple):
            arr, w, cb = r
            spec = pl.BlockSpec((tm, w), functools.partial(lambda i, cb: (i, cb), cb=cb))
        else:
            arr = r
            spec = pl.BlockSpec((tm, arr.shape[1]), lambda i: (i, 0))
        n_rows = arr.shape[0]
        in_arrays.append(arr)
        in_specs.append(spec)
    for c in consts:
        in_arrays.append(c)
        in_specs.append(pl.BlockSpec(c.shape, functools.partial(lambda i, n: (0,) * n, n=c.ndim)))
    out_shape = [_sds(s, d) for s, d in outs] + [_sds(s, d) for s, d in accs]
    out_specs = [pl.BlockSpec((tm, s[1]), lambda i: (i, 0)) for s, _ in outs]
    out_specs += [pl.BlockSpec(s, functools.partial(lambda i, n: (0,) * n, n=len(s))) for s, _ in accs]

    def body(*refs):
        fn(pl.program_id(0), *refs)

    res = _call(body, name=name, out_shape=out_shape, grid=(n_rows // tm,), in_specs=in_specs,
                out_specs=out_specs, rider=rider)(*in_arrays)
    return res


def rms_fwd(x, gain, name, rider=None):
    def fn(i, x_ref, g_ref, h_ref):
        xv = x_ref[...]
        r = lax.rsqrt(jnp.mean(xv * xv, axis=-1, keepdims=True) + EPS)
        h_ref[...] = (xv * r * g_ref[...]).astype(h_ref.dtype)

    res = _rowwise(name, fn, [x], [gain], [(x.shape, BF16)], rider=rider)
    return res[0] if rider is None else (res[0][0], res[1])


def rms_bwd(dhs, x, gain, dx_in, name):
    n = len(dhs)

    def fn(i, *refs):
        dh_refs, (x_ref, dxin_ref, g_ref, dx_ref, dg_ref) = refs[:n], refs[n:]
        dh = dh_refs[0][...]
        for r in dh_refs[1:]:
            dh = dh + r[...]
        xv = x_ref[...]
        r = lax.rsqrt(jnp.mean(xv * xv, axis=-1, keepdims=True) + EPS)
        xn = xv * r
        dxn = dh * g_ref[...]
        dx_ref[...] = dxin_ref[...] + r * (dxn - xn * jnp.mean(dxn * xn, axis=-1, keepdims=True))

        @pl.when(i == 0)
        def _():
            dg_ref[...] = jnp.zeros_like(dg_ref)

        dg_ref[...] += jnp.sum(dh * xn, axis=0, keepdims=True)

    return _rowwise(name, fn, list(dhs) + [x, dx_in], [gain], [(x.shape, F32)], [((1, x.shape[1]), F32)])


def loss_grad(y, target, name):
    def fn(i, y_ref, t_ref, dy_ref, sq_ref):
        err = y_ref[...] - t_ref[...]
        dy_ref[...] = err * (1.0 / y_ref.shape[1])

        @pl.when(i == 0)
        def _():
            sq_ref[...] = jnp.zeros_like(sq_ref)

        sq_ref[...] += jnp.sum(err * err, axis=0, keepdims=True)

    return _rowwise(name, fn, [y, target], [], [(y.shape, F32)], [((1, y.shape[1]), F32)])


def matmul_nn(a, b, name, out_dtype, tm, tn, res=None, scale=1.0, rider=None):
    s, k = a.shape
    n = b.shape[1]

    def body(*refs):
        if res is None:
            a_ref, b_ref, o_ref = refs
            o_ref[...] = _dot(a_ref[...], b_ref[...]).astype(o_ref.dtype)
        else:
            a_ref, b_ref, r_ref, o_ref = refs
            o_ref[...] = (r_ref[...] + scale * _dot(a_ref[...], b_ref[...])).astype(o_ref.dtype)

    in_specs = [pl.BlockSpec((tm, k), lambda i, j: (i, 0)), pl.BlockSpec((k, tn), lambda i, j: (0, j))]
    args = [a, b]
    if res is not None:
        in_specs.append(pl.BlockSpec((tm, tn), lambda i, j: (i, j)))
        args.append(res)
    return _call(body, name=name, out_shape=_sds((s, n), out_dtype), grid=(s // tm, n // tn), in_specs=in_specs,
                 out_specs=pl.BlockSpec((tm, tn), lambda i, j: (i, j)), rider=rider)(*args)


def matmul_nt(a, b, name, out_dtype, tm, tn, tk, rider=None):
    s, k = a.shape
    n = b.shape[0]
    nk = k // tk

    def body(a_ref, b_ref, o_ref, acc_ref):
        kk = pl.program_id(2)

        @pl.when(kk == 0)
        def _():
            acc_ref[...] = jnp.zeros_like(acc_ref)

        acc_ref[...] += _dot_nt(a_ref[...].astype(BF16), b_ref[...])

        @pl.when(kk == nk - 1)
        def _():
            o_ref[...] = acc_ref[...].astype(o_ref.dtype)

    return _call(body, name=name, out_shape=_sds((s, n), out_dtype), grid=(s // tm, n // tn, nk),
                 in_specs=[pl.BlockSpec((tm, tk), lambda i, j, kk: (i, kk)),
                           pl.BlockSpec((tn, tk), lambda i, j, kk: (j, kk))],
                 out_specs=pl.BlockSpec((tm, tn), lambda i, j, kk: (i, j)),
                 scratch_shapes=[pltpu.VMEM((tm, tn), F32)], rider=rider)(a, b)


def matmul_tn(a, b, name, tn, ts, a_scale=None, b_scale=None, rider=None):
    s, m = a.shape
    n = b.shape[1]
    ns = s // ts

    def body(a_ref, b_ref, o_ref, acc_ref):
        ss = pl.program_id(1)

        @pl.when(ss == 0)
        def _():
            acc_ref[...] = jnp.zeros_like(acc_ref)

        av, bv = a_ref[...], b_ref[...]
        if a_scale is not None:
            av = av * a_scale
        if b_scale is not None:
            bv = bv * b_scale
        acc_ref[...] += _dot_tn(av.astype(BF16), bv.astype(BF16))

        @pl.when(ss == ns - 1)
        def _():
            o_ref[...] = acc_ref[...].astype(o_ref.dtype)

    return _call(body, name=name, out_shape=_sds((m, n), BF16), grid=(n // tn, ns),
                 in_specs=[pl.BlockSpec((ts, m), lambda j, ss: (ss, 0)), pl.BlockSpec((ts, tn), lambda j, ss: (ss, j))],
                 out_specs=pl.BlockSpec((m, tn), lambda j, ss: (0, j)),
                 scratch_shapes=[pltpu.VMEM((m, tn), F32)], rider=rider)(a, b)


def _piece_specs(pieces, tile, rows_tile, tile_axis_first):
    specs, ranges, t0 = [], [], 0
    for a in pieces:
        n = a.shape[1] // tile

        def index(*ids, t0=t0, n=n):
            t, r = (ids[0], ids[1]) if tile_axis_first else (ids[2], ids[0])
            on = jnp.logical_and(t >= t0, t < t0 + n)
            return jnp.where(on, r, 0), jnp.clip(t - t0, 0, n - 1)

        specs.append(pl.BlockSpec((rows_tile, tile), index))
        ranges.append((t0, n))
        t0 += n
    return specs, ranges


def matmul_tn_pieces(a, pieces, name, tn, ts, rider=None):
    s, m = a.shape
    ns = s // ts
    specs, ranges = _piece_specs(pieces, tn, ts, True)
    n_total = sum(n for _, n in ranges)

    def body(a_ref, *refs):
        b_refs, o_ref, acc_ref = refs[:len(pieces)], refs[-2], refs[-1]
        j, ss = pl.program_id(0), pl.program_id(1)

        @pl.when(ss == 0)
        def _():
            acc_ref[...] = jnp.zeros_like(acc_ref)

        for b_ref, (t0, n) in zip(b_refs, ranges):
            @pl.when(jnp.logical_and(j >= t0, j < t0 + n))
            def _(b_ref=b_ref):
                acc_ref[...] += _dot_tn(a_ref[...], b_ref[...])

        @pl.when(ss == ns - 1)
        def _():
            o_ref[...] = acc_ref[...].astype(o_ref.dtype)

    return _call(body, name=name, out_shape=_sds((m, n_total * tn), BF16), grid=(n_total, ns),
                 in_specs=[pl.BlockSpec((ts, m), lambda j, ss: (ss, 0))] + specs,
                 out_specs=pl.BlockSpec((m, tn), lambda j, ss: (0, j)),
                 scratch_shapes=[pltpu.VMEM((m, tn), F32)], rider=rider)(a, *pieces)


def matmul_nt_pieces(pieces, b, name, out_dtype, tm, tn, tk, rider=None):
    s = pieces[0].shape[0]
    n = b.shape[0]
    specs, ranges = _piece_specs(pieces, tk, tm, False)
    nk = sum(cnt for _, cnt in ranges)

    def body(*refs):
        a_refs, b_ref, o_ref, acc_ref = refs[:len(pieces)], refs[-3], refs[-2], refs[-1]
        kk = pl.program_id(2)

        @pl.when(kk == 0)
        def _():
            acc_ref[...] = jnp.zeros_like(acc_ref)

        for a_ref, (t0, cnt) in zip(a_refs, ranges):
            @pl.when(jnp.logical_and(kk >= t0, kk < t0 + cnt))
            def _(a_ref=a_ref):
                acc_ref[...] += _dot_nt(a_ref[...], b_ref[...])

        @pl.when(kk == nk - 1)
        def _():
            o_ref[...] = acc_ref[...].astype(o_ref.dtype)

    return _call(body, name=name, out_shape=_sds((s, n), out_dtype), grid=(s // tm, n // tn, nk),
                 in_specs=specs + [pl.BlockSpec((tn, tk), lambda i, j, kk: (j, kk))],
                 out_specs=pl.BlockSpec((tm, tn), lambda i, j, kk: (i, j)),
                 scratch_shapes=[pltpu.VMEM((tm, tn), F32)], rider=rider)(*pieces, b)


def ffn_up(h, w704, gate_blk, up_blk, name, tm=512, rider=None):
    s = h.shape[0]

    def body(h_ref, wg_ref, wu_ref, g_ref, u_ref, a_ref):
        hv = h_ref[...]
        g = _dot(hv, wg_ref[...])
        u = _dot(hv, wu_ref[...])
        g_ref[...] = g.astype(BF16)
        u_ref[...] = u.astype(BF16)
        a_ref[...] = (g * _sigmoid(g) * u).astype(BF16)

    ospec = pl.BlockSpec((None, tm, FF_SHARD), lambda j, i: (j, i, 0))
    shp = _sds((N_CHIP, s, FF_SHARD), BF16)
    return _call(body, name=name, out_shape=[shp, shp, shp], grid=(N_CHIP, s // tm),
                 in_specs=[pl.BlockSpec((tm, D_MODEL), lambda j, i: (i, 0)),
                           pl.BlockSpec((None, D_MODEL, FF_SHARD), lambda j, i: (j, gate_blk, 0)),
                           pl.BlockSpec((None, D_MODEL, FF_SHARD), lambda j, i: (j, up_blk, 0))],
                 out_specs=[ospec, ospec, ospec], rider=rider)(h, w704, w704)


def ffn_down(a, w1024, blk, x, name, tm=512):
    s = x.shape[0]

    def body(a_ref, wd_ref, x_ref, o_ref):
        acc = _dot(a_ref[0], wd_ref[0])
        for j in range(1, N_CHIP):
            acc += _dot(a_ref[j], wd_ref[j])
        o_ref[...] = x_ref[...] + 0.5 * acc

    return _call(body, name=name, out_shape=_sds((s, D_MODEL), F32), grid=(s // tm,),
                 in_specs=[pl.BlockSpec((N_CHIP, tm, FF_SHARD), lambda i: (0, i, 0)),
                           pl.BlockSpec((N_CHIP, FF_SHARD, D_MODEL), lambda i: (0, blk, 0)),
                           pl.BlockSpec((tm, D_MODEL), lambda i: (i, 0))],
                 out_specs=pl.BlockSpec((tm, D_MODEL), lambda i: (i, 0)))(a, w1024, x)


def ffn_bwd_hidden(dx, w1024, blk, g, u, name, tm=1024, rider=None):
    s = dx.shape[0]

    def body(dx_ref, wd_ref, g_ref, u_ref, dg_ref, du_ref):
        dy = (0.5 * dx_ref[...]).astype(BF16)
        da = _dot_nt(dy, wd_ref[...])
        gv = g_ref[...].astype(F32)
        uv = u_ref[...].astype(F32)
        sg = _sigmoid(gv)
        dg_ref[...] = (da * uv * (sg * (1.0 + gv * (1.0 - sg)))).astype(BF16)
        du_ref[...] = (da * gv * sg).astype(BF16)

    hspec = pl.BlockSpec((None, tm, FF_SHARD), lambda j, i: (j, i, 0))
    shp = _sds((N_CHIP, s, FF_SHARD), BF16)
    return _call(body, name=name, out_shape=[shp, shp], grid=(N_CHIP, s // tm),
                 in_specs=[pl.BlockSpec((tm, D_MODEL), lambda j, i: (i, 0)),
                           pl.BlockSpec((None, FF_SHARD, D_MODEL), lambda j, i: (j, blk, 0)), hspec, hspec],
                 out_specs=[hspec, hspec], rider=rider)(dx, w1024, g, u)


def ffn_bwd_input(dg, du, w704, gate_blk, up_blk, name, tm=512, rider=None):
    s = dg.shape[1]

    def body(dg_ref, du_ref, wg_ref, wu_ref, o_ref):
        acc = _dot_nt(dg_ref[0], wg_ref[0]) + _dot_nt(du_ref[0], wu_ref[0])
        for j in range(1, N_CHIP):
            acc += _dot_nt(dg_ref[j], wg_ref[j]) + _dot_nt(du_ref[j], wu_ref[j])
        o_ref[...] = acc

    hspec = pl.BlockSpec((N_CHIP, tm, FF_SHARD), lambda i: (0, i, 0))
    return _call(body, name=name, out_shape=_sds((s, D_MODEL), F32), grid=(s // tm,),
                 in_specs=[hspec, hspec,
                           pl.BlockSpec((N_CHIP, D_MODEL, FF_SHARD), lambda i: (0, gate_blk, 0), pl.Buffered(1)),
                           pl.BlockSpec((N_CHIP, D_MODEL, FF_SHARD), lambda i: (0, up_blk, 0), pl.Buffered(1))],
                 out_specs=pl.BlockSpec((tm, D_MODEL), lambda i: (i, 0)), rider=rider)(dg, du, w704, w704)


def ffn_wgrad_in(h, dgu, name, ts=1024):
    s = h.shape[0]
    ns = s // ts

    def body(h_ref, d_ref, o_ref, acc_ref):
        ss = pl.program_id(1)

        @pl.when(ss == 0)
        def _():
            acc_ref[...] = jnp.zeros_like(acc_ref)

        acc_ref[...] += _dot_tn(h_ref[...], d_ref[...])

        @pl.when(ss == ns - 1)
        def _():
            o_ref[...] = acc_ref[...].astype(BF16)

    return _call(body, name=name, out_shape=_sds((N_CHIP, D_MODEL, FF_SHARD), BF16), grid=(N_CHIP, ns),
                 in_specs=[pl.BlockSpec((ts, D_MODEL), lambda j, ss: (ss, 0)),
                           pl.BlockSpec((None, ts, FF_SHARD), lambda j, ss: (j, ss, 0))],
                 out_specs=pl.BlockSpec((None, D_MODEL, FF_SHARD), lambda j, ss: (j, 0, 0)),
                 scratch_shapes=[pltpu.VMEM((D_MODEL, FF_SHARD), F32)])(h, dgu)


def ffn_wgrad_down(a, dx, name, ts=1024):
    s = dx.shape[0]
    ns = s // ts

    def body(a_ref, dx_ref, o_ref, acc_ref):
        ss = pl.program_id(1)

        @pl.when(ss == 0)
        def _():
            acc_ref[...] = jnp.zeros_like(acc_ref)

        acc_ref[...] += _dot_tn(a_ref[...], (0.5 * dx_ref[...]).astype(BF16))

        @pl.when(ss == ns - 1)
        def _():
            o_ref[...] = acc_ref[...].astype(BF16)

    return _call(body, name=name, out_shape=_sds((N_CHIP, FF_SHARD, D_MODEL), BF16), grid=(N_CHIP, ns),
                 in_specs=[pl.BlockSpec((None, ts, FF_SHARD), lambda j, ss: (j, ss, 0)),
                           pl.BlockSpec((ts, D_MODEL), lambda j, ss: (ss, 0))],
                 out_specs=pl.BlockSpec((None, FF_SHARD, D_MODEL), lambda j, ss: (j, 0, 0)),
                 scratch_shapes=[pltpu.VMEM((FF_SHARD, D_MODEL), F32)])(a, dx)


def ffn_forward(x, gain, get_w704, get_w1024, tag, rms_rider=None, up_rider=None):
    h = rms_fwd(x, gain, f"{tag}_rms", rider=rms_rider)
    h, rode_rms = h if rms_rider is not None else (h, None)
    res = ffn_up(h, get_w704(rode_rms), 0, 1, f"{tag}_up", rider=up_rider)
    (g, u, a), rode_up = res if up_rider is not None else (res, None)
    y = ffn_down(a, get_w1024(rode_up), 0, x, f"{tag}_down")
    return y, (h, g, u, a), rode_rms, rode_up


def ffn_backward(dy, x, gain, w704, w1024, saved, tag, ride_down=None, ride_in=None):
    h, g, u, a = saved
    d_wd = ffn_wgrad_down(a, dy, f"{tag}_dwd")
    if ride_down is not None:
        (dg, du), d_wd = ffn_bwd_hidden(dy, w1024, 0, g, u, f"{tag}_dhid", tm=512, rider=ride_down(d_wd))
    else:
        dg, du = ffn_bwd_hidden(dy, w1024, 0, g, u, f"{tag}_dhid")
    d_win = jnp.concatenate([ffn_wgrad_in(h, dg, f"{tag}_dwg"), ffn_wgrad_in(h, du, f"{tag}_dwu")], axis=1)
    if ride_in is not None:
        dh, d_win = ffn_bwd_input(dg, du, w704, 0, 1, f"{tag}_dh", rider=ride_in(d_win))
    else:
        dh = ffn_bwd_input(dg, du, w704, 0, 1, f"{tag}_dh")
    dx, d_gain = rms_bwd([dh], x, gain, dy, f"{tag}_drms")
    return dx, d_gain, d_win, d_wd


def _alibi_slope(head):
    return float(2.0 ** (-ALIBI_MAX_EXP * (head + 1) / N_ATTN_HEADS))


def _same_head():
    row = lax.broadcasted_iota(jnp.int32, (2 * HD, 2 * HD), 0)
    col = lax.broadcasted_iota(jnp.int32, (2 * HD, 2 * HD), 1)
    return ((row < HD) == (col < HD)).astype(BF16)


def _head_sums(x, same_head):
    hi = x.astype(BF16)
    lo = (x - hi.astype(F32)).astype(BF16)
    return _dot(hi, same_head) + _dot(lo, same_head)


def _head_norm(t, gain_pair, same_head):
    r = lax.rsqrt(_head_sums(t * t, same_head) * (1.0 / HD) + EPS)
    return t * r * gain_pair, r


def qk_norm_fwd(p, q_gain, k_gain, name):
    s = p.shape[0]

    def fn(i, q_ref, k_ref, qg_ref, kg_ref, qn_ref, kn_ref):
        same_head = _same_head()
        for src, g_ref, dst in ((q_ref, qg_ref, qn_ref), (k_ref, kg_ref, kn_ref)):
            for pr in range(ATTN_QKV // (2 * HD)):
                cols = slice(pr * 2 * HD, (pr + 1) * 2 * HD)
                y, _ = _head_norm(src[:, cols].astype(F32), g_ref[...], same_head)
                dst[:, cols] = y.astype(BF16)

    return _rowwise(name, fn, [(p, ATTN_QKV, 0), (p, ATTN_QKV, 1)], [q_gain, k_gain],
                    [((s, ATTN_QKV), BF16), ((s, ATTN_QKV), BF16)])


def qk_norm_bwd(p, dqs, dks, q_gain, k_gain, name):
    s = p.shape[0]
    pairs_per_pattern = GROUP_W // (2 * HD)

    def fn(i, q_ref, k_ref, dq0, dq1, dq2, dk0, dk1, dk2, qg_ref, kg_ref, dqk_ref, dqg_ref, dkg_ref):
        same_head = _same_head()

        @pl.when(i == 0)
        def _():
            dqg_ref[...] = jnp.zeros_like(dqg_ref)
            dkg_ref[...] = jnp.zeros_like(dkg_ref)

        for src, d_refs, g_ref, dst, dg_ref in (
                (q_ref, (dq0, dq1, dq2), qg_ref, dqk_ref.at[:, 0:ATTN_QKV], dqg_ref),
                (k_ref, (dk0, dk1, dk2), kg_ref, dqk_ref.at[:, ATTN_QKV:2 * ATTN_QKV], dkg_ref)):
            for pr in range(ATTN_QKV // (2 * HD)):
                cols = slice(pr * 2 * HD, (pr + 1) * 2 * HD)
                t = src[:, cols].astype(F32)
                r = lax.rsqrt(_head_sums(t * t, same_head) * (1.0 / HD) + EPS)
                xn = t * r
                within = (pr % pairs_per_pattern) * 2 * HD
                dy = d_refs[pr // pairs_per_pattern][:, within:within + 2 * HD]
                dg_ref[:, cols] += jnp.sum(dy * xn, axis=0, keepdims=True)
                dxn = dy * g_ref[...]
                mean = _head_sums(dxn * xn, same_head) * (1.0 / HD)
                dst[:, cols] = (r * (dxn - xn * mean)).astype(BF16)

    return _rowwise(name, fn, [(p, ATTN_QKV, 0), (p, ATTN_QKV, 1)] + list(dqs) + list(dks), [q_gain, k_gain],
                    [((s, 2 * ATTN_QKV), BF16)], [((1, ATTN_QKV), F32), ((1, ATTN_QKV), F32)])


def _to_streams(a, d):
    if d == 1:
        return a
    s, c = a.shape
    return a.reshape(s // d, d, c).transpose(1, 0, 2).reshape(s, c)


def _from_streams(a, d):
    if d == 1:
        return a
    s, c = a.shape
    return a.reshape(d, s // d, c).transpose(1, 0, 2).reshape(s, c)


def _attn_masks():
    row = lax.broadcasted_iota(jnp.int32, (BLK, BLK), 0)
    col = lax.broadcasted_iota(jnp.int32, (BLK, BLK), 1)
    rel_diag = row - col
    rel_prev = rel_diag + BLK
    return rel_diag, rel_prev


def attn_fwd(q, k, v, pattern, name, tq=512):
    s = q.shape[0]
    d = ATTN_DILATIONS[pattern]
    blocks_per_stream = (s // d) // BLK
    nsb = tq // BLK

    def body(q_ref, k_ref, v_ref, kp_ref, vp_ref, o_ref, l_ref):
        i = pl.program_id(0)
        rel_diag, rel_prev = _attn_masks()
        first = _lane_first_half((BLK, 2 * HD))
        rd_f = (rel_diag * d).astype(F32)
        rp_f = (rel_prev * d).astype(F32)
        for sb in range(nsb):
            rows = slice(sb * BLK, (sb + 1) * BLK)
            has_prev = ((i * nsb + sb) % blocks_per_stream != 0).astype(jnp.int32)
            m_diag = rel_diag >= 0
            m_prev = (rel_prev + (1 - has_prev) * (4 * BLK)) <= BLK
            for pr in range(GROUP_W // (2 * HD)):
                cols = slice(pr * 2 * HD, (pr + 1) * 2 * HD)
                qp = q_ref[rows, cols]
                kc, vc = k_ref[rows, cols], v_ref[rows, cols]
                if sb == 0:
                    kp, vp = kp_ref[:, cols], vp_ref[:, cols]
                else:
                    prows = slice((sb - 1) * BLK, sb * BLK)
                    kp, vp = k_ref[prows, cols], v_ref[prows, cols]
                outs, lses = [], []
                for e in range(2):
                    slope = _alibi_slope(pattern * HEADS_PER_PATTERN + 2 * pr + e)
                    qm = jnp.where(first if e == 0 else jnp.logical_not(first), qp, jnp.zeros_like(qp))
                    s1 = jnp.where(m_diag, _dot_nt(qm, kc) * 0.125 - slope * rd_f, NEG)
                    s0 = jnp.where(m_prev, _dot_nt(qm, kp) * 0.125 - slope * rp_f, NEG)
                    m = jnp.maximum(jnp.max(s1, axis=-1, keepdims=True), jnp.max(s0, axis=-1, keepdims=True))
                    p1 = jnp.exp(s1 - m)
                    p0 = jnp.exp(s0 - m)
                    l = jnp.sum(p1, axis=-1, keepdims=True) + jnp.sum(p0, axis=-1, keepdims=True)
                    inv = 1.0 / l
                    outs.append(_dot((p1 * inv).astype(BF16), vc) + _dot((p0 * inv).astype(BF16), vp))
                    lses.append(m + jnp.log(l))
                o_ref[rows, cols] = jnp.where(first, outs[0], outs[1])
                l_ref[rows, cols] = jnp.where(first, lses[0], lses[1])

    cur = pl.BlockSpec((tq, GROUP_W), lambda i: (i, 0))
    prev = pl.BlockSpec((BLK, GROUP_W), lambda i: (jnp.maximum(i * nsb - 1, 0), 0))
    return _call(body, name=name, out_shape=[_sds((s, GROUP_W), F32), _sds((s, GROUP_W), F32)], grid=(s // tq,),
                 in_specs=[cur, cur, cur, prev, prev], out_specs=[cur, cur])(q, k, v, k, v)


def attn_merge_fwd(os_, lses, name):
    s = os_[0].shape[0]

    def fn(i, o0, o1, o2, l0, l1, l2, out_ref):
        m = jnp.maximum(jnp.maximum(l0[...], l1[...]), l2[...])
        e0, e1, e2 = jnp.exp(l0[...] - m), jnp.exp(l1[...] - m), jnp.exp(l2[...] - m)
        inv = 1.0 / (e0 + e1 + e2)
        out_ref[...] = ((e0 * inv) * o0[...] + (e1 * inv) * o1[...] + (e2 * inv) * o2[...]).astype(BF16)

    return _rowwise(name, fn, list(os_) + list(lses), [], [((s, GROUP_W), BF16)])[0]


def attn_merge_bwd(d_out, os_, lses, name):
    s = d_out.shape[0]

    def fn(i, do_ref, o0, o1, o2, l0, l1, l2, d0, d1, d2, c0, c1, c2):
        first = _lane_first_half((do_ref.shape[0], 2 * HD))
        m = jnp.maximum(jnp.maximum(l0[...], l1[...]), l2[...])
        e0, e1, e2 = jnp.exp(l0[...] - m), jnp.exp(l1[...] - m), jnp.exp(l2[...] - m)
        inv = 1.0 / (e0 + e1 + e2)
        w0, w1, w2 = e0 * inv, e1 * inv, e2 * inv
        do = do_ref[...]
        prod = do * (w0 * o0[...] + w1 * o1[...] + w2 * o2[...])
        same_head = _same_head()
        for pr in range(GROUP_W // (2 * HD)):
            cols = slice(pr * 2 * HD, (pr + 1) * 2 * HD)
            t = _head_sums(prod[:, cols], same_head)
            for w, c_ref in ((w0, c0), (w1, c1), (w2, c2)):
                c_ref[:, cols] = w[:, cols] * t
        for w, d_ref in ((w0, d0), (w1, d1), (w2, d2)):
            d_ref[...] = (w * do).astype(BF16)

    shp = (s, GROUP_W)
    return _rowwise(name, fn, [d_out] + list(os_) + list(lses), [],
                    [(shp, BF16)] * 3 + [(shp, F32)] * 3)


def attn_bwd(q, k, v, d_o, cterm, lse, pattern, name, tq=512):
    s = q.shape[0]
    d = ATTN_DILATIONS[pattern]
    blocks_per_stream = (s // d) // BLK
    nsb = tq // BLK
    n_blocks = s // BLK

    def body(q_ref, k_ref, v_ref, do_ref, c_ref, l_ref, kp_ref, vp_ref, qn_ref, don_ref, cn_ref, ln_ref,
             dq_ref, dk_ref, dv_ref):
        i = pl.program_id(0)
        rel_diag, rel_prev = _attn_masks()
        first = _lane_first_half((BLK, 2 * HD))
        second = jnp.logical_not(first)
        rd_f = (rel_diag * d).astype(F32)
        rp_f = (rel_prev * d).astype(F32)
        m_diag = rel_diag >= 0
        dq_ref[...] = jnp.zeros_like(dq_ref)
        dk_ref[...] = jnp.zeros_like(dk_ref)
        dv_ref[...] = jnp.zeros_like(dv_ref)

        def pair(qp, dop, cp, lp, kp, vp, rel_f, mask):
            dq = dk = dv = None
            for e in range(2):
                lanes = first if e == 0 else second
                slope = slopes[e]
                qm = jnp.where(lanes, qp, jnp.zeros_like(qp))
                dom = jnp.where(lanes, dop, jnp.zeros_like(dop))
                km = jnp.where(lanes, kp, jnp.zeros_like(kp))
                sc = jnp.where(mask, _dot_nt(qm, kp) * 0.125 - slope * rel_f, NEG)
                pm = jnp.exp(sc - lp[:, e * HD:e * HD + 1])
                dl = pm * (_dot_nt(dom, vp) - cp[:, e * HD:e * HD + 1])
                dl16 = dl.astype(BF16)
                t_dq = _dot(dl16, km)
                t_dk = _dot_tn(dl16, qm)
                t_dv = _dot_tn(pm.astype(BF16), dom)
                dq = t_dq if dq is None else dq + t_dq
                dk = t_dk if dk is None else dk + t_dk
                dv = t_dv if dv is None else dv + t_dv
            return dq * 0.125, dk * 0.125, dv

        for pr in range(GROUP_W // (2 * HD)):
            cols = slice(pr * 2 * HD, (pr + 1) * 2 * HD)
            slopes = [_alibi_slope(pattern * HEADS_PER_PATTERN + 2 * pr + e) for e in range(2)]
            for sb in range(nsb + 1):
                gb = i * nsb + sb
                if sb < nsb:
                    rows = slice(sb * BLK, (sb + 1) * BLK)
                    qp, dop, cp, lp = q_ref[rows, cols], do_ref[rows, cols], c_ref[rows, cols], l_ref[rows, cols]
                else:
                    qp, dop, cp, lp = qn_ref[:, cols], don_ref[:, cols], cn_ref[:, cols], ln_ref[:, cols]
                if sb < nsb:
                    dq1, dk1, dv1 = pair(qp, dop, cp, lp, k_ref[rows, cols], v_ref[rows, cols], rd_f, m_diag)
                    dq_ref[rows, cols] += dq1
                    dk_ref[rows, cols] += dk1
                    dv_ref[rows, cols] += dv1
                valid = jnp.logical_and(gb % blocks_per_stream != 0, gb < n_blocks).astype(jnp.int32)
                m_prev = jnp.logical_and(rel_prev <= BLK, (rel_prev + (1 - valid) * (4 * BLK)) <= BLK)
                if sb == 0:
                    kp, vp = kp_ref[:, cols], vp_ref[:, cols]
                else:
                    prows = slice((sb - 1) * BLK, sb * BLK)
                    kp, vp = k_ref[prows, cols], v_ref[prows, cols]
                dq0, dk0, dv0 = pair(qp, dop, cp, lp, kp, vp, rp_f, m_prev)
                if sb < nsb:
                    dq_ref[rows, cols] += dq0
                if sb > 0:
                    dk_ref[prows, cols] += dk0
                    dv_ref[prows, cols] += dv0

    cur = pl.BlockSpec((tq, GROUP_W), lambda i: (i, 0))
    prev = pl.BlockSpec((BLK, GROUP_W), lambda i: (jnp.maximum(i * nsb - 1, 0), 0))
    nxt = pl.BlockSpec((BLK, GROUP_W), lambda i: (jnp.minimum((i + 1) * nsb, n_blocks - 1), 0))
    shp = _sds((s, GROUP_W), F32)
    return _call(body, name=name, out_shape=[shp, shp, shp], grid=(s // tq,),
                 in_specs=[cur] * 6 + [prev, prev] + [nxt] * 4, out_specs=[cur, cur, cur])(
                     q, k, v, d_o, cterm, lse, k, v, q, d_o, cterm, lse)


def _band_constants(d):
    row = lax.broadcasted_iota(jnp.int32, (2 * BLK, 2 * BLK), 0)
    col = lax.broadcasted_iota(jnp.int32, (2 * BLK, 2 * BLK), 1)
    rel = BLK + jnp.where(row >= BLK, row - BLK, row) - col
    band = jnp.logical_and(rel >= 0, rel <= BLK)
    return (rel * d).astype(F32), band, (col >= BLK).astype(jnp.int32)


def _stack_heads(x, first):
    zero = jnp.zeros_like(x)
    return jnp.concatenate([jnp.where(first, x, zero), jnp.where(first, zero, x)], axis=0)


def _unstack_heads(x2, first):
    return jnp.where(first, x2[:BLK], x2[BLK:])


def _head_column(x):
    return jnp.concatenate([x[:, 0:1], x[:, HD:HD + 1]], axis=0)


def attn_fwd2(q, k, v, pattern, name, tq=512):
    s = q.shape[0]
    d = ATTN_DILATIONS[pattern]
    blocks_per_stream = (s // d) // BLK
    nsb = tq // BLK

    def body(q_ref, k_ref, v_ref, kp_ref, vp_ref, o_ref, l_ref):
        i = pl.program_id(0)
        rel_f, band, own = _band_constants(d)
        first = _lane_first_half((BLK, 2 * HD))
        upper = lax.broadcasted_iota(jnp.int32, (2 * BLK, 1), 0) < BLK
        for sb in range(nsb):
            rows = slice(sb * BLK, (sb + 1) * BLK)
            has_prev = ((i * nsb + sb) % blocks_per_stream != 0).astype(jnp.int32)
            mask = jnp.logical_and(band, (own + has_prev) > 0)
            for pr in range(GROUP_W // (2 * HD)):
                cols = slice(pr * 2 * HD, (pr + 1) * 2 * HD)
                if sb == 0:
                    kcat = jnp.concatenate([kp_ref[:, cols], k_ref[rows, cols]], axis=0)
                    vcat = jnp.concatenate([vp_ref[:, cols], v_ref[rows, cols]], axis=0)
                else:
                    both = slice((sb - 1) * BLK, (sb + 1) * BLK)
                    kcat, vcat = k_ref[both, cols], v_ref[both, cols]
                h0 = pattern * HEADS_PER_PATTERN + 2 * pr
                slope = jnp.where(upper, _alibi_slope(h0), _alibi_slope(h0 + 1))
                sc = _dot_nt(_stack_heads(q_ref[rows, cols], first), kcat) * 0.125 - slope * rel_f
                sc = jnp.where(mask, sc, NEG)
                m = jnp.max(sc, axis=-1, keepdims=True)
                p = jnp.exp(sc - m)
                l = jnp.sum(p, axis=-1, keepdims=True)
                o2 = _dot((p * (1.0 / l)).astype(BF16), vcat)
                o_ref[rows, cols] = _unstack_heads(o2, first)
                lse = m + jnp.log(l)
                l_ref[rows, cols] = jnp.where(first, lse[:BLK], lse[BLK:])

    cur = pl.BlockSpec((tq, GROUP_W), lambda i: (i, 0))
    prev = pl.BlockSpec((BLK, GROUP_W), lambda i: (jnp.maximum(i * nsb - 1, 0), 0))
    return _call(body, name=name, out_shape=[_sds((s, GROUP_W), F32), _sds((s, GROUP_W), F32)], grid=(s // tq,),
                 in_specs=[cur, cur, cur, prev, prev], out_specs=[cur, cur])(q, k, v, k, v)


def attn_bwd2(q, k, v, d_o, cterm, lse, pattern, name, tq=512):
    s = q.shape[0]
    d = ATTN_DILATIONS[pattern]
    blocks_per_stream = (s // d) // BLK
    nsb = tq // BLK
    n_blocks = s // BLK

    def body(q_ref, k_ref, v_ref, do_ref, c_ref, l_ref, kp_ref, vp_ref, qn_ref, kn_ref, vn_ref, don_ref, cn_ref,
             ln_ref, dq_ref, dk_ref, dv_ref):
        i = pl.program_id(0)
        rel_f, band, own = _band_constants(d)
        first = _lane_first_half((BLK, 2 * HD))
        upper = lax.broadcasted_iota(jnp.int32, (2 * BLK, 1), 0) < BLK
        dk_ref[...] = jnp.zeros_like(dk_ref)
        dv_ref[...] = jnp.zeros_like(dv_ref)
        for sb in range(nsb + 1):
            gb = i * nsb + sb
            rows = slice(sb * BLK, (sb + 1) * BLK)
            before = slice((sb - 1) * BLK, sb * BLK)
            inside = (gb < n_blocks).astype(jnp.int32)
            has_prev = jnp.logical_and(gb % blocks_per_stream != 0, gb < n_blocks).astype(jnp.int32)
            mask = jnp.logical_and(band, (own * inside + has_prev) > 0)
            for pr in range(GROUP_W // (2 * HD)):
                cols = slice(pr * 2 * HD, (pr + 1) * 2 * HD)
                if sb == 0:
                    kcat = jnp.concatenate([kp_ref[:, cols], k_ref[rows, cols]], axis=0)
                    vcat = jnp.concatenate([vp_ref[:, cols], v_ref[rows, cols]], axis=0)
                elif sb == nsb:
                    kcat = jnp.concatenate([k_ref[before, cols], kn_ref[:, cols]], axis=0)
                    vcat = jnp.concatenate([v_ref[before, cols], vn_ref[:, cols]], axis=0)
                else:
                    both = slice((sb - 1) * BLK, (sb + 1) * BLK)
                    kcat, vcat = k_ref[both, cols], v_ref[both, cols]
                if sb < nsb:
                    qp, dop, cp, lp = q_ref[rows, cols], do_ref[rows, cols], c_ref[rows, cols], l_ref[rows, cols]
                else:
                    qp, dop, cp, lp = qn_ref[:, cols], don_ref[:, cols], cn_ref[:, cols], ln_ref[:, cols]
                h0 = pattern * HEADS_PER_PATTERN + 2 * pr
                slope = jnp.where(upper, _alibi_slope(h0), _alibi_slope(h0 + 1))
                q2 = _stack_heads(qp, first)
                do2 = _stack_heads(dop, first)
                sc = jnp.where(mask, _dot_nt(q2, kcat) * 0.125 - slope * rel_f, NEG)
                pm = jnp.exp(sc - _head_column(lp))
                dl = (pm * (_dot_nt(do2, vcat) - _head_column(cp))).astype(BF16)
                if sb < nsb:
                    dq_ref[rows, cols] = _unstack_heads(_dot(dl, kcat), first) * 0.125
                dk2 = _dot_tn(dl, q2) * 0.125
                dv2 = _dot_tn(pm.astype(BF16), do2)
                if sb > 0:
                    dk_ref[before, cols] += dk2[:BLK]
                    dv_ref[before, cols] += dv2[:BLK]
                if sb < nsb:
                    dk_ref[rows, cols] += dk2[BLK:]
                    dv_ref[rows, cols] += dv2[BLK:]

    cur = pl.BlockSpec((tq, GROUP_W), lambda i: (i, 0))
    prev = pl.BlockSpec((BLK, GROUP_W), lambda i: (jnp.maximum(i * nsb - 1, 0), 0))
    nxt = pl.BlockSpec((BLK, GROUP_W), lambda i: (jnp.minimum((i + 1) * nsb, n_blocks - 1), 0))
    shp = _sds((s, GROUP_W), F32)
    return _call(body, name=name, out_shape=[shp, shp, shp], grid=(s // tq,),
                 in_specs=[cur] * 6 + [prev, prev] + [nxt] * 6, out_specs=[cur, cur, cur])(
                     q, k, v, d_o, cterm, lse, k, v, q, k, v, d_o, cterm, lse)


HALO = 16
CONV_TQ = 512


def conv_fwd(p, w, b, name):
    s = p.shape[0]
    tq = CONV_TQ
    ncol = SSD_CONV_DIM // GROUP_W
    cb0 = COL_XBC // GROUP_W

    def body(u_ref, up_ref, w_ref, b_ref, c_ref, xc_ref):
        i = pl.program_id(0)
        prev = up_ref[...].astype(F32) * (i > 0).astype(F32)
        ext = jnp.concatenate([prev, u_ref[...].astype(F32)], axis=0)
        acc = b_ref[...] + w_ref[SSD_CONV - 1:SSD_CONV, :] * ext[HALO:HALO + tq]
        for kk in range(SSD_CONV - 1):
            acc += w_ref[kk:kk + 1, :] * pltpu.roll(ext, SSD_CONV - 1 - kk, 0)[HALO:HALO + tq]
        c_ref[...] = acc.astype(BF16)
        xc_ref[...] = (acc * _sigmoid(acc)).astype(BF16)

    cur_in = pl.BlockSpec((tq, GROUP_W), lambda i, j: (i, cb0 + j))
    prev_in = pl.BlockSpec((HALO, GROUP_W), lambda i, j: (jnp.maximum(i * (tq // HALO) - 1, 0), cb0 + j))
    cur_out = pl.BlockSpec((tq, GROUP_W), lambda i, j: (i, j))
    shp = _sds((s, SSD_CONV_DIM), BF16)
    return _call(body, name=name, out_shape=[shp, shp], grid=(s // tq, ncol),
                 in_specs=[cur_in, prev_in, pl.BlockSpec((SSD_CONV, GROUP_W), lambda i, j: (0, j)),
                           pl.BlockSpec((1, GROUP_W), lambda i, j: (0, j))],
                 out_specs=[cur_out, cur_out])(p, p, w, b)


def conv_bwd(p, cpre, dxs, d_b, d_c, w, name):
    s = p.shape[0]
    tq = CONV_TQ
    ncol = SSD_CONV_DIM // GROUP_W
    n_xs = SSD_INNER // GROUP_W
    cb0 = COL_XBC // GROUP_W
    nt = s // tq

    def body(u_ref, c_ref, cn_ref, dx_ref, dxn_ref, dbm_ref, dbmn_ref, dcm_ref, dcmn_ref, w_ref,
             du_ref, dw_ref, db_ref):
        j, i = pl.program_id(0), pl.program_id(1)

        def dpre(c16, dx):
            c = c16.astype(F32)
            sg = _sigmoid(c)
            return dx * (sg * (1.0 + c * (1.0 - sg)))

        def pick(a_ref, b_ref, c_ref_):
            return jnp.where(j < n_xs, a_ref[...], jnp.where(j == n_xs, b_ref[...], c_ref_[...]))

        dc = dpre(c_ref[...], pick(dx_ref, dbm_ref, dcm_ref))
        dcn = dpre(cn_ref[...], pick(dxn_ref, dbmn_ref, dcmn_ref)) * (i < nt - 1).astype(F32)
        dext = jnp.concatenate([dc, dcn], axis=0)
        u = u_ref[...].astype(F32)

        @pl.when(i == 0)
        def _():
            dw_ref[...] = jnp.zeros_like(dw_ref)
            db_ref[...] = jnp.zeros_like(db_ref)

        du = w_ref[SSD_CONV - 1:SSD_CONV, :] * dc
        dw_ref[SSD_CONV - 1:SSD_CONV, :] += jnp.sum(dc * u, axis=0, keepdims=True)
        for kk in range(SSD_CONV - 1):
            sh = SSD_CONV - 1 - kk
            ahead = pltpu.roll(dext, tq + HALO - sh, 0)[0:tq]
            du += w_ref[kk:kk + 1, :] * ahead
            dw_ref[kk:kk + 1, :] += jnp.sum(ahead * u, axis=0, keepdims=True)
        du_ref[...] = du.astype(BF16)
        db_ref[...] += jnp.sum(dc, axis=0, keepdims=True)

    hb = tq // HALO
    cur_p = pl.BlockSpec((tq, GROUP_W), lambda j, i: (i, cb0 + j))
    cur = pl.BlockSpec((tq, GROUP_W), lambda j, i: (i, j))
    nxt = pl.BlockSpec((HALO, GROUP_W), lambda j, i: (jnp.minimum((i + 1) * hb, s // HALO - 1), j))

    def piece(first_tile, n_tiles):
        def on(j):
            return jnp.logical_and(j >= first_tile, j < first_tile + n_tiles)

        def col(j):
            return jnp.clip(j - first_tile, 0, n_tiles - 1)

        return (pl.BlockSpec((tq, GROUP_W), lambda j, i: (jnp.where(on(j), i, 0), col(j))),
                pl.BlockSpec((HALO, GROUP_W),
                             lambda j, i: (jnp.where(on(j), jnp.minimum((i + 1) * hb, s // HALO - 1), 0), col(j))))

    return _call(body, name=name,
                 out_shape=[_sds((s, SSD_CONV_DIM), BF16), _sds((8, SSD_CONV_DIM), F32), _sds((1, SSD_CONV_DIM), F32)],
                 grid=(ncol, nt),
                 in_specs=[cur_p, cur, nxt, *piece(0, n_xs), *piece(n_xs, 1), *piece(n_xs + 1, 1),
                           pl.BlockSpec((SSD_CONV, GROUP_W), lambda j, i: (0, j))],
                 out_specs=[cur, pl.BlockSpec((8, GROUP_W), lambda j, i: (0, j)),
                            pl.BlockSpec((1, GROUP_W), lambda j, i: (0, j))])(
                                p, cpre, cpre, dxs, dxs, d_b, d_b, d_c, d_c, w)


def _softplus(x):
    return jnp.maximum(x, 0.0) + jnp.log(1.0 + jnp.exp(-jnp.abs(x)))


def _ssd_decays(dtr_ref, dtrt_ref, bias_ref, biast_ref, alog_ref, alogt_ref):
    row = lax.broadcasted_iota(jnp.int32, (BLK, BLK), 0)
    col = lax.broadcasted_iota(jnp.int32, (BLK, BLK), 1)
    lower = (row >= col).astype(F32)
    upper = (row <= col).astype(F32)
    dtb = dtr_ref[...] + bias_ref[...]
    dt = _softplus(dtb)
    a = dt * (-jnp.exp(alog_ref[...]))
    cs = _dot_hi(lower, a)
    a_t = _softplus(dtrt_ref[...] + biast_ref[...]) * (-jnp.exp(alogt_ref[...]))
    cs_t = _dot_hi(a_t, upper)
    return dtb, dt, cs, cs_t, row, col, upper


SSD_GROUPS_PER_STEP = 4


def _per_group(body, gps, kinds):
    def wrapped(*refs):
        for gi in range(gps):
            args, pos = [], 0
            for kind, n in kinds:
                if kind == "each":
                    args.append(refs[pos + gi])
                    pos += gps
                    continue
                ref = refs[pos]
                pos += 1
                if kind == "cols":
                    args.append(ref.at[:, gi * n:(gi + 1) * n])
                else:
                    args.append(ref.at[gi] if n == 1 else ref.at[pl.ds(gi * n, n)])
            body(*args)

    return wrapped


def ssd_fwd(p, xc, dtg, dtg_t, params, gn, name):
    s = p.shape[0]
    nc = s // BLK
    bias, bias_t, alog, alog_t, dskip = params

    def body(xs_ref, b_ref, c_ref, z_ref, dtr_ref, dtrt_ref, bias_ref, biast_ref, alog_ref, alogt_ref, dsk_ref,
             gn_ref, y_ref, sin_ref, hp_ref, h_ref):
        c_idx = pl.program_id(1)

        @pl.when(c_idx == 0)
        def _():
            h_ref[...] = jnp.zeros_like(h_ref)

        _, dt, cs, cs_t, row, col, _ = _ssd_decays(dtr_ref, dtrt_ref, bias_ref, biast_ref, alog_ref, alogt_ref)
        first = _lane_first_half((BLK, 2 * HD))
        first_row = _lane_first_half((1, 2 * HD))
        tril = row >= col
        b16, c16 = b_ref[...], c_ref[...]
        cb = _dot_nt(c16, b16)
        n_pairs = GROUP_W // (2 * HD)
        tot = cs[BLK - 1:BLK, :]
        exp_cs, exp_rest, exp_tot = jnp.exp(cs), jnp.exp(tot - cs), jnp.exp(tot)

        lanes_of_head = (lax.broadcasted_iota(jnp.int32, (8, GROUP_W), 1) // HD
                         == lax.broadcasted_iota(jnp.int32, (8, GROUP_W), 0)).astype(BF16)

        def per_head(v, mask):
            if v.shape[0] == 1:
                return jnp.concatenate([jnp.where(mask, v[:, 2 * pr:2 * pr + 1], v[:, 2 * pr + 1:2 * pr + 2])
                                        for pr in range(n_pairs)], axis=1)
            hi = v.astype(BF16)
            lo = (v - hi.astype(F32)).astype(BF16)
            return _dot(hi, lanes_of_head) + _dot(lo, lanes_of_head)

        xs = xs_ref[...].astype(F32)
        xt = xs * per_head(dt, first)
        xt16 = xt.astype(BF16)
        hstate = jnp.concatenate([h_ref[pr] for pr in range(n_pairs)], axis=1)
        for pr in range(n_pairs):
            hp_ref[pr] = h_ref[pr]
        y_off = per_head(exp_cs, first) * _dot(c16, hstate.astype(BF16))
        new = per_head(exp_tot, first_row) * hstate + _dot_tn(b16, (per_head(exp_rest, first) * xt).astype(BF16))
        for pr in range(n_pairs):
            h_ref[pr] = new[:, pr * 2 * HD:(pr + 1) * 2 * HD]
        y_diag = []
        for pr in range(n_pairs):
            cols = slice(pr * 2 * HD, (pr + 1) * 2 * HD)
            m2 = jnp.concatenate(
                [(cb * jnp.exp(jnp.where(tril, cs[:, h:h + 1] - cs_t[h:h + 1, :], NEG))).astype(BF16)
                 for h in (2 * pr, 2 * pr + 1)], axis=1)
            y_diag.append(_dot(m2, _stack_heads(xt16[:, cols], first)))
        y = jnp.concatenate(y_diag, axis=1) + y_off + xs * per_head(dsk_ref[...], first_row)
        y_ref[...] = y
        zv = z_ref[...].astype(F32)
        yz = y * (zv * _sigmoid(zv))
        r = lax.rsqrt(jnp.mean(yz * yz, axis=-1, keepdims=True) + EPS)
        sin_ref[...] = (yz * r * gn_ref[...]).astype(BF16)

    gps = SSD_GROUPS_PER_STEP
    wide, narrow, lead = ("cols", GROUP_W), ("cols", BLK), ("lead", 1)
    kinds = [wide, narrow, narrow, ("each", 0)] + [lead] * 7 + [wide, wide, wide, lead, ("lead", 4)]
    wide_w, narrow_w = GROUP_W * gps, BLK * gps
    gparam = pl.BlockSpec((gps, 1, 8), lambda g, c: (g, 0, 0))
    gparam_t = pl.BlockSpec((gps, 8, 1), lambda g, c: (g, 0, 0))
    z_specs = [pl.BlockSpec((BLK, GROUP_W), functools.partial(lambda g, c, gi: (c, COL_Z // GROUP_W + gps * g + gi),
                                                              gi=gi)) for gi in range(gps)]
    return _call(
        _per_group(body, gps, kinds), name=name,
        out_shape=[_sds((s, SSD_INNER), F32), _sds((s, SSD_INNER), BF16),
                   _sds((SSD_GROUPS, nc, 4, BLK, 2 * HD), F32)],
        grid=(SSD_GROUPS // gps, nc),
        in_specs=[pl.BlockSpec((BLK, wide_w), lambda g, c: (c, g)),
                  pl.BlockSpec((BLK, narrow_w), lambda g, c: (c, SSD_INNER // narrow_w + g)),
                  pl.BlockSpec((BLK, narrow_w), lambda g, c: (c, (SSD_INNER + SSD_GROUPS * BLK) // narrow_w + g)),
                  *z_specs,
                  pl.BlockSpec((gps, BLK, 8), lambda g, c: (g, c, 0)),
                  pl.BlockSpec((gps, 8, BLK), lambda g, c: (g, 0, c)),
                  gparam, gparam_t, gparam, gparam_t, gparam,
                  pl.BlockSpec((1, wide_w), lambda g, c: (0, g))],
        out_specs=[pl.BlockSpec((BLK, wide_w), lambda g, c: (c, g)),
                   pl.BlockSpec((BLK, wide_w), lambda g, c: (c, g)),
                   pl.BlockSpec((gps, None, 4, BLK, 2 * HD), lambda g, c: (g, c, 0, 0, 0))],
        scratch_shapes=[pltpu.VMEM((4 * gps, BLK, 2 * HD), F32)],
    )(xc, xc, xc, *([p] * gps), dtg, dtg_t, bias, bias_t, alog, alog_t, dskip, gn)


def ssd_bwd(p, xc, y, d_sin, hprev, dtg, dtg_t, params, gn, name):
    s = p.shape[0]
    nc = s // BLK
    bias, bias_t, alog, alog_t, dskip = params

    def body(xs_ref, b_ref, c_ref, z_ref, y_ref, dsin_ref, hp_ref, dtr_ref, dtrt_ref, bias_ref,
             biast_ref, alog_ref, alogt_ref, dsk_ref, gn_ref,
             dxs_ref, db_ref, dc_ref, dz_ref, ddt_ref, da_ref, dbias_ref, ddsk_ref, dgn_ref, dh_ref):
        c_idx = pl.program_id(1)

        @pl.when(c_idx == 0)
        def _():
            dh_ref[...] = jnp.zeros_like(dh_ref)
            da_ref[...] = jnp.zeros_like(da_ref)
            dbias_ref[...] = jnp.zeros_like(dbias_ref)
            ddsk_ref[...] = jnp.zeros_like(ddsk_ref)
            dgn_ref[...] = jnp.zeros_like(dgn_ref)

        dtb, dt, cs, cs_t, row, col, upper = _ssd_decays(dtr_ref, dtrt_ref, bias_ref, biast_ref, alog_ref, alogt_ref)
        first = _lane_first_half((BLK, 2 * HD))
        second = jnp.logical_not(first)
        first_row = _lane_first_half((1, 2 * HD))
        tril = row >= col
        triu = row <= col
        last_row = lax.broadcasted_iota(jnp.int32, (BLK, 1), 0) == BLK - 1
        lane8 = lax.broadcasted_iota(jnp.int32, (BLK, 8), 1)

        yv = y_ref[...]
        zv = z_ref[...].astype(F32)
        sg = _sigmoid(zv)
        yz = yv * (zv * sg)
        r = lax.rsqrt(jnp.mean(yz * yz, axis=-1, keepdims=True) + EPS)
        yzn = yz * r
        dsn = dsin_ref[...]
        dgn_ref[...] += jnp.sum(dsn * yzn, axis=0, keepdims=True)
        dsn = dsn * gn_ref[...]
        dyz = r * (dsn - yzn * jnp.mean(dsn * yzn, axis=-1, keepdims=True))
        dy = dyz * (zv * sg)
        dz_ref[...] = (dyz * yv * (sg * (1.0 + zv * (1.0 - sg)))).astype(BF16)
        xs_all = xs_ref[...].astype(F32)
        ddsk_ref[...] += jnp.sum(dy * xs_all, axis=0, keepdims=True)

        b16, c16 = b_ref[...], c_ref[...]
        cb = _dot_nt(c16, b16)
        cb_t = _dot_nt(b16, c16)
        n_pairs = GROUP_W // (2 * HD)
        tot = cs[BLK - 1:BLK, :]
        exp_cs, exp_rest, exp_tot = jnp.exp(cs), jnp.exp(tot - cs), jnp.exp(tot)

        lanes_of_head = (lax.broadcasted_iota(jnp.int32, (8, GROUP_W), 1) // HD
                         == lax.broadcasted_iota(jnp.int32, (8, GROUP_W), 0)).astype(BF16)

        def per_head(v, mask):
            if v.shape[0] == 1:
                return jnp.concatenate([jnp.where(mask, v[:, 2 * pr:2 * pr + 1], v[:, 2 * pr + 1:2 * pr + 2])
                                        for pr in range(n_pairs)], axis=1)
            hi = v.astype(BF16)
            lo = (v - hi.astype(F32)).astype(BF16)
            return _dot(hi, lanes_of_head) + _dot(lo, lanes_of_head)

        head_of_lane = (lax.broadcasted_iota(jnp.int32, (GROUP_W, 8), 0) // HD
                        == lax.broadcasted_iota(jnp.int32, (GROUP_W, 8), 1)).astype(BF16)

        def head_sums(v):
            hi = v.astype(BF16)
            lo = (v - hi.astype(F32)).astype(BF16)
            return _dot(hi, head_of_lane) + _dot(lo, head_of_lane)

        dt_w, e_w, f_w = per_head(dt, first), per_head(exp_cs, first), per_head(exp_rest, first)
        xt = xs_all * dt_w
        xt16 = xt.astype(BF16)
        hstate = jnp.concatenate([hp_ref[pr] for pr in range(n_pairs)], axis=1)
        h16 = hstate.astype(BF16)
        dhn = jnp.concatenate([dh_ref[pr] for pr in range(n_pairs)], axis=1)
        dhn16 = dhn.astype(BF16)
        edy16 = (e_w * dy).astype(BF16)
        y_off = e_w * _dot(c16, h16)
        dcs_all = head_sums(dy * y_off)
        dc_acc = _dot_nt(edy16, h16)
        zmat = _dot(b16, dhn16)
        t_all = head_sums(zmat * xt) * exp_rest
        hh_rows = jnp.sum(head_sums(dhn * hstate), axis=0, keepdims=True)
        dtot = jnp.sum(t_all, axis=0, keepdims=True) + hh_rows * exp_tot
        dcs_all = dcs_all - t_all + jnp.where(last_row, dtot, 0.0)
        fxt16 = (f_w * xt).astype(BF16)
        db_acc = _dot_nt(fxt16, dhn16)
        dh_new = _dot_tn(c16, edy16) + per_head(exp_tot, first_row) * dhn
        for pr in range(n_pairs):
            dh_ref[pr] = dh_new[:, pr * 2 * HD:(pr + 1) * 2 * HD]
        g_sum = jnp.zeros((BLK, BLK), F32)
        gt_sum = jnp.zeros((BLK, BLK), F32)
        d_xt_parts = []
        for pr in range(n_pairs):
            cols = slice(pr * 2 * HD, (pr + 1) * 2 * HD)
            dym2 = _stack_heads(dy[:, cols].astype(BF16), first)
            d_m2 = _dot_nt(dym2, xt16[:, cols])
            d_mt2 = _dot_nt(xt16[:, cols], dym2)
            mt2 = []
            for e, h in enumerate((2 * pr, 2 * pr + 1)):
                cs_c, cs_r = cs[:, h:h + 1], cs_t[h:h + 1, :]
                decay = jnp.exp(jnp.where(tril, cs_c - cs_r, NEG))
                decay_t = jnp.exp(jnp.where(triu, cs_r - cs_c, NEG))
                gm = d_m2[e * BLK:(e + 1) * BLK] * decay
                gmt = d_mt2[:, e * BLK:(e + 1) * BLK] * decay_t
                g_sum += gm
                gt_sum += gmt
                dcs_h = jnp.sum(gm * cb, axis=-1, keepdims=True) - jnp.sum(gmt * cb_t, axis=-1, keepdims=True)
                dcs_all = dcs_all + jnp.where(lane8 == h, dcs_h, 0.0)
                mt2.append((cb_t * decay_t).astype(BF16))
            d_xt_parts.append(_dot(jnp.concatenate(mt2, axis=1), dym2))
        d_xt = jnp.concatenate(d_xt_parts, axis=1) + f_w * zmat
        dxs_ref[...] = dy * per_head(dsk_ref[...], first_row) + d_xt * dt_w
        ddtx_all = head_sums(d_xt * xs_all)

        dc_ref[...] = dc_acc + _dot(g_sum.astype(BF16), b16)
        db_ref[...] = db_acc + _dot(gt_sum.astype(BF16), c16)
        d_a = _dot_hi(upper, dcs_all)
        a_neg = -jnp.exp(alog_ref[...])
        ddt = ddtx_all + d_a * a_neg
        da_ref[...] += jnp.sum(d_a * dt, axis=0, keepdims=True)
        ddtr = ddt * _sigmoid(dtb)
        ddt_ref[...] = ddtr
        dbias_ref[...] += jnp.sum(ddtr, axis=0, keepdims=True)

    gps = SSD_GROUPS_PER_STEP
    k_wide, k_narrow, k_lead = ("cols", GROUP_W), ("cols", BLK), ("lead", 1)
    kinds = ([k_wide, k_narrow, k_narrow, ("each", 0), k_wide, k_wide] + [k_lead] * 8 + [k_wide]
             + [k_wide, k_narrow, k_narrow, k_wide] + [k_lead] * 4 + [k_wide] + [("lead", 4)])
    wide_w, narrow_w = GROUP_W * gps, BLK * gps
    rc = lambda c: nc - 1 - c
    gparam = pl.BlockSpec((gps, 1, 8), lambda g, c: (g, 0, 0))
    gparam_t = pl.BlockSpec((gps, 8, 1), lambda g, c: (g, 0, 0))
    wide = pl.BlockSpec((BLK, wide_w), lambda g, c: (rc(c), g))
    narrow = pl.BlockSpec((BLK, narrow_w), lambda g, c: (rc(c), g))
    z_specs = [pl.BlockSpec((BLK, GROUP_W),
                            functools.partial(lambda g, c, gi: (rc(c), COL_Z // GROUP_W + gps * g + gi), gi=gi))
               for gi in range(gps)]
    return _call(
        _per_group(body, gps, kinds), name=name,
        out_shape=[_sds((s, SSD_INNER), F32), _sds((s, GROUP_W), F32), _sds((s, GROUP_W), F32),
                   _sds((s, SSD_INNER), BF16), _sds((SSD_GROUPS, s, 8), F32),
                   _sds((SSD_GROUPS, 1, 8), F32), _sds((SSD_GROUPS, 1, 8), F32),
                   _sds((SSD_GROUPS, 1, GROUP_W), F32), _sds((1, SSD_INNER), F32)],
        grid=(SSD_GROUPS // gps, nc),
        in_specs=[wide,
                  pl.BlockSpec((BLK, narrow_w), lambda g, c: (rc(c), SSD_INNER // narrow_w + g)),
                  pl.BlockSpec((BLK, narrow_w), lambda g, c: (rc(c), (SSD_INNER + SSD_GROUPS * BLK) // narrow_w + g)),
                  *z_specs,
                  wide, wide,
                  pl.BlockSpec((gps, None, 4, BLK, 2 * HD), lambda g, c: (g, rc(c), 0, 0, 0)),
                  pl.BlockSpec((gps, BLK, 8), lambda g, c: (g, rc(c), 0)),
                  pl.BlockSpec((gps, 8, BLK), lambda g, c: (g, 0, rc(c))),
                  gparam, gparam_t, gparam, gparam_t, gparam,
                  pl.BlockSpec((1, wide_w), lambda g, c: (0, g))],
        out_specs=[wide, narrow, narrow, wide,
                   pl.BlockSpec((gps, BLK, 8), lambda g, c: (g, rc(c), 0)),
                   gparam, gparam,
                   pl.BlockSpec((gps, 1, GROUP_W), lambda g, c: (g, 0, 0)),
                   pl.BlockSpec((1, wide_w), lambda g, c: (0, g))],
        scratch_shapes=[pltpu.VMEM((4 * gps, BLK, 2 * HD), F32)],
    )(xc, xc, xc, *([p] * gps), y, d_sin, hprev, dtg, dtg_t, bias, bias_t, alog, alog_t, dskip, gn)


def merge_fwd(p, a, sbr, name, tm=512):
    s = p.shape[0]
    nj = D_MODEL // GROUP_W

    def body(ga_ref, gs_ref, a_ref, s_ref, o_ref):
        o_ref[...] = (_sigmoid(ga_ref[...].astype(F32)) * a_ref[...]
                      + _sigmoid(gs_ref[...].astype(F32)) * s_ref[...]).astype(BF16)

    blk = pl.BlockSpec((tm, GROUP_W), lambda i, j: (i, j))
    return _call(body, name=name, out_shape=_sds((s, D_MODEL), BF16), grid=(s // tm, nj),
                 in_specs=[pl.BlockSpec((tm, GROUP_W), lambda i, j: (i, COL_GA // GROUP_W + j)),
                           pl.BlockSpec((tm, GROUP_W), lambda i, j: (i, COL_GS // GROUP_W + j)), blk, blk],
                 out_specs=blk)(p, p, a, sbr)


def merge_bwd(p, a, sbr, dmerged, name, tm=512):
    s = p.shape[0]
    nj = D_MODEL // GROUP_W

    def body(ga_ref, gs_ref, a_ref, s_ref, dm_ref, da_ref, ds_ref, dga_ref, dgs_ref):
        dm = dm_ref[...]
        sa = _sigmoid(ga_ref[...].astype(F32))
        ss = _sigmoid(gs_ref[...].astype(F32))
        da_ref[...] = (dm * sa).astype(BF16)
        ds_ref[...] = (dm * ss).astype(BF16)
        dga_ref[...] = (dm * a_ref[...] * sa * (1.0 - sa)).astype(BF16)
        dgs_ref[...] = (dm * s_ref[...] * ss * (1.0 - ss)).astype(BF16)

    blk = pl.BlockSpec((tm, GROUP_W), lambda i, j: (i, j))
    shp = _sds((s, D_MODEL), BF16)
    return _call(body, name=name, out_shape=[shp] * 4, grid=(s // tm, nj),
                 in_specs=[pl.BlockSpec((tm, GROUP_W), lambda i, j: (i, COL_GA // GROUP_W + j)),
                           pl.BlockSpec((tm, GROUP_W), lambda i, j: (i, COL_GS // GROUP_W + j)), blk, blk, blk],
                 out_specs=[blk] * 4)(p, p, a, sbr, dmerged)


def _group_major(v):
    return v.reshape(SSD_GROUPS, 1, 8), v.reshape(SSD_GROUPS, 8, 1)


def mixer_forward(x, w, rider=None, later_weights=None):
    s = x.shape[0]
    h = rms_fwd(x, w["mix_norm"], "mix_rms")
    p = matmul_nn(h, w["w_in_main"], "mix_proj", BF16, tm=1024, tn=512, rider=rider)
    rode = None
    if rider is not None:
        p, rode = p
        w = dict(w, **later_weights(rode))
    dt_raw = matmul_nn(h, w["w_in_dt"], "mix_proj_dt", F32, tm=1024, tn=DT_PAD)
    qn, kn = qk_norm_fwd(p, w["q_gain"], w["k_gain"], "qk_norm")
    streams, os_, lses = [], [], []
    for g, d in enumerate(ATTN_DILATIONS):
        cols = slice(g * GROUP_W, (g + 1) * GROUP_W)
        qs, ks = _to_streams(qn[:, cols], d), _to_streams(kn[:, cols], d)
        vs = _to_streams(p[:, COL_V + g * GROUP_W:COL_V + (g + 1) * GROUP_W], d)
        o, lse = attn_fwd2(qs, ks, vs, g, f"attn_fwd{g}")
        streams.append((qs, ks, vs, lse))
        os_.append(_from_streams(o, d))
        lses.append(_from_streams(lse, d))
    attn_o = attn_merge_fwd(os_, lses, "attn_merge")
    cpre, xc = conv_fwd(p, w["conv_w"], w["conv_b"], "conv_fwd")
    dtg = dt_raw[:, :SSD_HEADS].reshape(s, SSD_GROUPS, 8).transpose(1, 0, 2)
    dtg_t = dtg.transpose(0, 2, 1)
    params = (*_group_major(w["dt_bias"]), *_group_major(w["a_log"]), _group_major(w["d_skip"])[0])
    y, s_in, hprev = ssd_fwd(p, xc, dtg, dtg_t, params, w["ssd_norm"], "ssd_fwd")
    a = matmul_nn(attn_o, w["w_attn_branch"], "attn_branch", F32, tm=1024, tn=512)
    sbr = matmul_nn(s_in, w["w_ssd_branch"], "ssd_branch", F32, tm=1024, tn=512)
    merged = merge_fwd(p, a, sbr, "merge")
    x_out = matmul_nn(merged, w["w_out"], "mix_out", F32, tm=1024, tn=512, res=x)
    saved = dict(h=h, p=p, streams=streams, os=os_, lses=lses, attn_o=attn_o, cpre=cpre, xc=xc, dtg=dtg,
                 dtg_t=dtg_t, params=params, y=y, s_in=s_in, hprev=hprev, a=a, sbr=sbr, merged=merged, w=w)
    return x_out, saved, rode


def mixer_backward(dx_out, x, sv, ride_early=None, ride_late=None):
    s = x.shape[0]
    p = sv["p"]
    w = sv["w"]
    g = {}
    dmerged = matmul_nt(dx_out, w["w_out"], "d_merged", F32, tm=1024, tn=512, tk=1024)
    g["w_out"] = matmul_tn(sv["merged"], dx_out, "dw_out", tn=512, ts=1024)
    da, ds, dga, dgs = merge_bwd(p, sv["a"], sv["sbr"], dmerged, "merge_bwd")
    g["w_attn_branch"] = matmul_tn(sv["attn_o"], da, "dw_attn_branch", tn=512, ts=1024)
    g["w_ssd_branch"] = matmul_tn(sv["s_in"], ds, "dw_ssd_branch", tn=512, ts=1024)
    d_attn_o = matmul_nt(da, w["w_attn_branch"], "d_attn_o", F32, tm=1024, tn=512, tk=1024)
    d_sin = matmul_nt(ds, w["w_ssd_branch"], "d_ssd_in", F32, tm=1024, tn=512, tk=1024)
    dxs, d_b, d_c, dz, ddt, d_asum, d_bias, d_dsk, d_gn = ssd_bwd(
        p, sv["xc"], sv["y"], d_sin, sv["hprev"], sv["dtg"], sv["dtg_t"], sv["params"], w["ssd_norm"], "ssd_bwd")
    dxbc, d_convw, d_convb = conv_bwd(p, sv["cpre"], dxs, d_b, d_c, w["conv_w"], "conv_bwd")
    g["conv_w"] = d_convw[:SSD_CONV]
    g["conv_b"] = d_convb
    g["dt_bias"] = d_bias.reshape(1, SSD_HEADS)
    g["a_log"] = (d_asum * (-jnp.exp(sv["params"][2]))).reshape(1, SSD_HEADS)
    g["d_skip"] = jnp.sum(d_dsk.reshape(SSD_HEADS, HD), axis=1).reshape(1, SSD_HEADS)
    g["ssd_norm"] = d_gn
    merged_bwd = attn_merge_bwd(d_attn_o, sv["os"], sv["lses"], "attn_merge_bwd")
    dqs, dks, dvs = [], [], []
    for gi, d in enumerate(ATTN_DILATIONS):
        qs, ks, vs, lse = sv["streams"][gi]
        d_o = _to_streams(merged_bwd[gi], d)
        cterm = _to_streams(merged_bwd[3 + gi], d)
        dq, dk, dv = attn_bwd2(qs, ks, vs, d_o, cterm, lse, gi, f"attn_bwd{gi}")
        dqs.append(_from_streams(dq, d))
        dks.append(_from_streams(dk, d))
        dvs.append(_from_streams(dv, d).astype(BF16))
    dqk, d_qg, d_kg = qk_norm_bwd(p, dqs, dks, w["q_gain"], w["k_gain"], "qk_norm_bwd")
    g["q_norm"] = jnp.sum(d_qg.reshape(N_ATTN_HEADS, HD), axis=0).reshape(1, HD)
    g["k_norm"] = jnp.sum(d_kg.reshape(N_ATTN_HEADS, HD), axis=0).reshape(1, HD)
    dp = [dqk, jnp.concatenate(dvs, axis=1), dz, dxbc, dga, dgs]
    ddt_pad = jnp.pad(ddt.transpose(1, 0, 2).reshape(s, SSD_HEADS), ((0, 0), (0, DT_PAD - SSD_HEADS)))
    if ride_early is not None:
        g["w_in_main"], g["rode_early"] = matmul_tn_pieces(sv["h"], dp, "dw_in", tn=512, ts=1024,
                                                           rider=ride_early(g))
    else:
        g["w_in_main"] = matmul_tn_pieces(sv["h"], dp, "dw_in", tn=512, ts=1024)
    g["w_in_dt"] = matmul_tn(sv["h"], ddt_pad, "dw_in_dt", tn=DT_PAD, ts=1024)
    if ride_late is not None:
        dh_main, g["rode_late"] = matmul_nt_pieces(dp, w["w_in_main"], "d_mix_h", F32, tm=1024, tn=1024, tk=512,
                                                   rider=ride_late(g))
    else:
        dh_main = matmul_nt_pieces(dp, w["w_in_main"], "d_mix_h", F32, tm=1024, tn=512, tk=512)
    dh_dt = matmul_nt(ddt_pad, w["w_in_dt"], "d_mix_h_dt", F32, tm=1024, tn=1024, tk=DT_PAD)
    dx, g["mix_norm"] = rms_bwd([dh_main, dh_dt], x, w["mix_norm"], dx_out, "mix_drms")
    return dx, g


ANY = pl.BlockSpec(memory_space=pl.ANY)


def _place():
    x, y, c = lax.axis_index("x"), lax.axis_index("y"), lax.axis_index("c")
    chips = [(1 - x, y), (x, 1 - y), (1 - x, 1 - y)]
    return x, y, c, 2 * x + y, chips


def _comm_call(body, *, name, out_shape, n_in, scratch_shapes, aliases=None):
    return pl.pallas_call(
        body, out_shape=out_shape, in_specs=[ANY] * n_in, out_specs=[ANY] * len(out_shape),
        scratch_shapes=scratch_shapes, input_output_aliases=aliases or {}, name=name,
        compiler_params=pltpu.CompilerParams(has_side_effects=True))


def gather_weights(shards, small):
    n = len(shards)
    halves = [a.shape[0] // 2 for a in shards]
    out_shape = [_sds((N_CHIP,) + a.shape, a.dtype) for a in shards] + [_sds((N_CHIP,) + small.shape, small.dtype)]

    def body(*refs):
        ins, outs = refs[:n + 1], refs[n + 1:2 * n + 2]
        send1, recv1, send2, recv2, local = refs[2 * n + 2:]
        x, y, c, me, chips = _place()
        sibling = (x, y, 1 - c)

        def rows(k, chip, core):
            if k == n:
                return outs[k].at[chip]
            return outs[k].at[chip, pl.ds(core * halves[k], halves[k])]

        def level1(k, t, incoming):
            chip = 2 * chips[t][0] + chips[t][1]
            src = ins[k] if k == n else ins[k].at[pl.ds(c * halves[k], halves[k])]
            return pltpu.make_async_remote_copy(
                src_ref=src, dst_ref=rows(k, chip if incoming else me, c), send_sem=send1.at[3 * k + t],
                recv_sem=recv1.at[3 * k + t], device_id=(*chips[t], c), device_id_type=MESH)

        def level2(k, t, incoming):
            chip = 2 * chips[t][0] + chips[t][1]
            core = (1 - c) if incoming else c
            return pltpu.make_async_remote_copy(
                src_ref=rows(k, chip, core), dst_ref=rows(k, chip, core), send_sem=send2.at[3 * k + t],
                recv_sem=recv2.at[3 * k + t], device_id=sibling, device_id_type=MESH)

        own = [pltpu.make_async_copy(ins[k], outs[k].at[me], local.at[k]) for k in range(n + 1)]
        for cp in own:
            cp.start()
        first = [level1(k, t, False) for k in range(n + 1) for t in range(3)]
        for cp in first:
            cp.start()
        passed = []
        for k in range(n + 1):
            for t in range(3):
                level1(k, t, True).wait_recv()
                if k < n:
                    cp = level2(k, t, False)
                    cp.start()
                    passed.append(cp)
        for k in range(n):
            for t in range(3):
                level2(k, t, True).wait_recv()
        for cp in first + passed:
            cp.wait_send()
        for cp in own:
            cp.wait()

    dma = pltpu.SemaphoreType.DMA
    return _comm_call(body, name="gather_weights", out_shape=out_shape, n_in=n + 1,
                      scratch_shapes=[dma((3 * n + 3,)), dma((3 * n + 3,)), dma((3 * n,)), dma((3 * n,)),
                                      dma((n + 1,))])(*shards, small)


def reduce_to_sibling(grads):
    n = len(grads)
    halves = [a.shape[1] // 2 for a in grads]
    shapes = [_sds((N_CHIP, h, a.shape[2]), a.dtype) for a, h in zip(grads, halves)]

    def body(*refs):
        ins, got, kept = refs[:n], refs[n:2 * n], refs[2 * n:3 * n]
        send, recv, local = refs[3 * n:]
        x, y, c, _, _ = _place()
        copies, locals_ = [], []
        for k in range(n):
            h = halves[k]
            locals_.append(pltpu.make_async_copy(ins[k].at[:, pl.ds(c * h, h)], kept[k], local.at[k]))
            copies.append(pltpu.make_async_remote_copy(
                src_ref=ins[k].at[:, pl.ds((1 - c) * h, h)], dst_ref=got[k], send_sem=send.at[k], recv_sem=recv.at[k],
                device_id=(x, y, 1 - c), device_id_type=MESH))
        for cp in locals_ + copies:
            cp.start()
        for cp in copies:
            cp.wait_recv()
        for cp in copies:
            cp.wait_send()
        for cp in locals_:
            cp.wait()

    dma = pltpu.SemaphoreType.DMA
    res = _comm_call(body, name="reduce_to_sibling", out_shape=shapes + shapes, n_in=n,
                     scratch_shapes=[dma((n,)), dma((n,)), dma((n,))])(*grads)
    return res[:n], res[n:]


def reduce_to_owner(sums):
    n = len(sums)
    shapes = [_sds(a.shape, a.dtype) for a in sums]

    def body(*refs):
        ins, outs = refs[:n], refs[n:2 * n]
        send, recv, local = refs[2 * n:]
        x, y, c, me, chips = _place()
        copies, locals_ = [], []
        for k in range(n):
            locals_.append(pltpu.make_async_copy(ins[k].at[me], outs[k].at[3], local.at[k]))
            for t in range(3):
                chip = 2 * chips[t][0] + chips[t][1]
                copies.append(pltpu.make_async_remote_copy(
                    src_ref=ins[k].at[chip], dst_ref=outs[k].at[t], send_sem=send.at[3 * k + t],
                    recv_sem=recv.at[3 * k + t], device_id=(*chips[t], c), device_id_type=MESH))
        for cp in locals_ + copies:
            cp.start()
        for cp in copies:
            cp.wait_recv()
        for cp in copies:
            cp.wait_send()
        for cp in locals_:
            cp.wait()

    dma = pltpu.SemaphoreType.DMA
    return _comm_call(body, name="reduce_to_owner", out_shape=shapes, n_in=n,
                      scratch_shapes=[dma((3 * n,)), dma((3 * n,)), dma((n,))])(*sums)


def share_with_sibling(halves_):
    n = len(halves_)
    shapes = [_sds((2 * a.shape[0], a.shape[1]), a.dtype) for a in halves_]

    def body(*refs):
        ins, outs = refs[:n], refs[n:2 * n]
        send, recv, local = refs[2 * n:]
        x, y, c, _, _ = _place()
        copies, locals_ = [], []
        for k in range(n):
            h = ins[k].shape[0]
            mine = outs[k].at[pl.ds(c * h, h)]
            locals_.append(pltpu.make_async_copy(ins[k], mine, local.at[k]))
            copies.append(pltpu.make_async_remote_copy(
                src_ref=ins[k], dst_ref=mine, send_sem=send.at[k], recv_sem=recv.at[k],
                device_id=(x, y, 1 - c), device_id_type=MESH))
        for cp in locals_ + copies:
            cp.start()
        for cp in copies:
            cp.wait_recv()
        for cp in copies:
            cp.wait_send()
        for cp in locals_:
            cp.wait()

    dma = pltpu.SemaphoreType.DMA
    return _comm_call(body, name="share_with_sibling", out_shape=shapes, n_in=n,
                      scratch_shapes=[dma((n,)), dma((n,)), dma((n,))])(*halves_)


def _cores():
    c = lax.axis_index("c")
    return jnp.stack([c, 1 - c]).astype(jnp.int32)


def _staged_call(body, *, name, grid, in_specs, out_specs, out_shape, scratch_shapes):
    return pl.pallas_call(
        body, out_shape=out_shape, name=name,
        grid_spec=pltpu.PrefetchScalarGridSpec(num_scalar_prefetch=1, grid=grid, in_specs=in_specs,
                                               out_specs=out_specs, scratch_shapes=scratch_shapes),
        compiler_params=pltpu.CompilerParams(dimension_semantics=("arbitrary",) * len(grid),
                                             vmem_limit_bytes=V7X_VMEM_LIMIT, has_side_effects=True))


def gather_rider(shards, tiles):
    dma = pltpu.SemaphoreType.DMA
    n = len(shards)
    geo = [(a.shape[0] // 2, tm, (a.shape[0] // 2) // tm) for a, tm in zip(shards, tiles)]
    scratch = []
    for a, (h, tm, nk) in zip(shards, geo):
        scratch += [pltpu.VMEM((N_CHIP,) + a.shape, a.dtype), dma((3, nk)), dma((3, nk)), dma((3, nk)), dma((3, nk)),
                    dma((nk + 2,))]

    def copies(j, in_ref, scr):
        buf, send1, recv1, send2, recv2, local = scr[6 * j:6 * j + 6]
        h, tm, nk = geo[j]
        x, y, c, me, chips = _place()
        chip_of = [2 * chips[t][0] + chips[t][1] for t in range(3)]

        def rows(chip, core, k):
            return buf.at[chip, pl.ds(core * h + k * tm, tm)]

        def mine(k):
            if k == nk:
                return pltpu.make_async_copy(in_ref.at[pl.ds((1 - c) * h, h)], buf.at[me, pl.ds((1 - c) * h, h)],
                                             local.at[nk])
            return pltpu.make_async_copy(in_ref.at[pl.ds(c * h + k * tm, tm)], rows(me, c, k), local.at[k])

        def level1(t, k, incoming):
            place = rows(chip_of[t] if incoming else me, c, k)
            return pltpu.make_async_remote_copy(src_ref=place, dst_ref=place, send_sem=send1.at[t, k],
                                                recv_sem=recv1.at[t, k], device_id=(*chips[t], c), device_id_type=MESH)

        def level2(t, k, incoming):
            place = rows(chip_of[t], (1 - c) if incoming else c, k)
            return pltpu.make_async_remote_copy(src_ref=place, dst_ref=place, send_sem=send2.at[t, k],
                                                recv_sem=recv2.at[t, k], device_id=(x, y, 1 - c),
                                                device_id_type=MESH)

        return buf, local, nk, mine, level1, level2

    def start(ins, outs, scr):
        for j in range(n):
            _, _, nk, mine, _, _ = copies(j, ins[j], scr)
            for k in range(nk + 1):
                mine(k).start()
        for j in range(n):
            _, _, nk, mine, level1, _ = copies(j, ins[j], scr)
            for k in range(nk):
                mine(k).wait()
                for t in range(3):
                    level1(t, k, False).start()

    def finish(ins, outs, scr):
        for j in range(n):
            _, _, nk, _, level1, level2 = copies(j, ins[j], scr)
            for k in range(nk):
                for t in range(3):
                    level1(t, k, True).wait_recv()
                    level2(t, k, False).start()
        for j in range(n):
            buf, local, nk, mine, level1, level2 = copies(j, ins[j], scr)
            for k in range(nk):
                for t in range(3):
                    level2(t, k, True).wait_recv()
            for k in range(nk):
                for t in range(3):
                    level1(t, k, False).wait_send()
                    level2(t, k, False).wait_send()
            mine(nk).wait()
            pltpu.make_async_copy(buf, outs[j], local.at[nk + 1]).start()
        for j in range(n):
            buf, local, nk, _, _, _ = copies(j, ins[j], scr)
            pltpu.make_async_copy(buf, outs[j], local.at[nk + 1]).wait()

    return Rider(list(shards), [_sds((N_CHIP,) + a.shape, a.dtype) for a in shards], scratch, start, finish)


def run_alone(rider, name):
    return _call(lambda: None, name=name, out_shape=[], in_specs=[], out_specs=[], grid=(1,), rider=rider)()[1]


def sibling_sum(g, tm, name):
    _, r, cdim = g.shape
    h = r // 2
    ni = h // tm
    dma = pltpu.SemaphoreType.DMA

    def body(cores_ref, keep_ref, give_ref, out_ref, slot, send, recv):
        par = (pl.program_id(0) * ni + pl.program_id(1)) % 2
        x, y, c, _, _ = _place()
        cp = pltpu.make_async_remote_copy(src_ref=give_ref, dst_ref=slot.at[par], send_sem=send.at[par],
                                          recv_sem=recv.at[par], device_id=(x, y, 1 - c), device_id_type=MESH)
        cp.start()
        cp.wait_recv()
        out_ref[...] = (keep_ref[...].astype(F32) + slot[par].astype(F32)).astype(out_ref.dtype)
        cp.wait_send()

    flat = g.reshape(N_CHIP * r, cdim)
    return _staged_call(
        body, name=name, grid=(N_CHIP, ni),
        in_specs=[pl.BlockSpec((tm, cdim), lambda j, i, cores: ((2 * j + cores[0]) * ni + i, 0)),
                  pl.BlockSpec((tm, cdim), lambda j, i, cores: ((2 * j + cores[1]) * ni + i, 0))],
        out_specs=pl.BlockSpec((None, tm, cdim), lambda j, i, cores: (j, i, 0)),
        out_shape=_sds((N_CHIP, h, cdim), g.dtype),
        scratch_shapes=[pltpu.VMEM((2, tm, cdim), g.dtype), dma((2,)), dma((2,))],
    )(_cores(), flat, flat)


def owner_sum_rider(sums, tiles):
    dma = pltpu.SemaphoreType.DMA
    n = len(sums)
    geo = [(a.shape[1], tm, a.shape[1] // tm) for a, tm in zip(sums, tiles)]
    scratch = []
    for a, (h, tm, nk) in zip(sums, geo):
        cdim = a.shape[2]
        scratch += [pltpu.VMEM(a.shape, a.dtype), pltpu.VMEM((3, h, cdim), a.dtype), pltpu.VMEM((2, h, cdim), F32),
                    dma((3, nk)), dma((3, nk)), dma((nk,)), dma((nk,)), dma((2,))]

    def copies(j, scr):
        part, got, res, send, recv, send2, recv2, local = scr[8 * j:8 * j + 8]
        h, tm, nk = geo[j]
        x, y, c, me, chips = _place()

        def to_owner(t, k):
            chip = 2 * chips[t][0] + chips[t][1]
            return pltpu.make_async_remote_copy(
                src_ref=part.at[chip, pl.ds(k * tm, tm)], dst_ref=got.at[t, pl.ds(k * tm, tm)],
                send_sem=send.at[t, k], recv_sem=recv.at[t, k], device_id=(*chips[t], c), device_id_type=MESH)

        def to_sibling(k):
            place = res.at[c, pl.ds(k * tm, tm)]
            return pltpu.make_async_remote_copy(src_ref=place, dst_ref=place, send_sem=send2.at[k],
                                                recv_sem=recv2.at[k], device_id=(x, y, 1 - c), device_id_type=MESH)

        return part, got, res, local, to_owner, to_sibling, (tm, nk, c, me)

    def start(ins, outs, scr):
        for j in range(n):
            part, _, _, local, _, _, _ = copies(j, scr)
            pltpu.make_async_copy(ins[j], part, local.at[0]).start()
        for j in range(n):
            part, _, _, local, to_owner, _, (tm, nk, c, me) = copies(j, scr)
            pltpu.make_async_copy(ins[j], part, local.at[0]).wait()
            for k in range(nk):
                for t in range(3):
                    to_owner(t, k).start()

    def finish(ins, outs, scr):
        for j in range(n):
            part, got, res, _, to_owner, to_sibling, (tm, nk, c, me) = copies(j, scr)
            for k in range(nk):
                rows = pl.ds(k * tm, tm)
                for t in range(3):
                    to_owner(t, k).wait_recv()
                acc = part[me, rows, :].astype(F32)
                for t in range(3):
                    acc = acc + got[t, rows, :].astype(F32)
                res[c, rows, :] = acc
                to_sibling(k).start()
        for j in range(n):
            _, _, res, local, to_owner, to_sibling, (tm, nk, c, me) = copies(j, scr)
            for k in range(nk):
                to_sibling(k).wait_recv()
            for k in range(nk):
                to_sibling(k).wait_send()
                for t in range(3):
                    to_owner(t, k).wait_send()
            pltpu.make_async_copy(res, outs[j], local.at[1]).start()
        for j in range(n):
            _, _, res, local, _, _, _ = copies(j, scr)
            pltpu.make_async_copy(res, outs[j], local.at[1]).wait()

    return Rider(list(sums), [_sds((2, a.shape[1], a.shape[2]), F32) for a in sums], scratch, start, finish)


def gather_conv_w(w):
    def body(in_ref, out_ref, send, recv):
        x, y, c, me, chips = _place()
        out_ref[me] = in_ref[...]
        copies = []
        for t in range(3):
            copies.append(pltpu.make_async_remote_copy(
                src_ref=out_ref.at[me], dst_ref=out_ref.at[me], send_sem=send.at[t], recv_sem=recv.at[t],
                device_id=(*chips[t], c), device_id_type=MESH))
        for cp in copies:
            cp.start()
        for cp in copies:
            cp.wait_recv()
        for cp in copies:
            cp.wait_send()

    dma = pltpu.SemaphoreType.DMA
    vmem = pl.BlockSpec(memory_space=pltpu.VMEM)
    return pl.pallas_call(
        body, out_shape=_sds((N_CHIP,) + w.shape, w.dtype), in_specs=[vmem], out_specs=vmem, name="gather_conv_w",
        scratch_shapes=[dma((3,)), dma((3,))],
        compiler_params=pltpu.CompilerParams(has_side_effects=True))(w)


N_DEV = 8
SMALL_ROWS = 32
SMALL_LANES = 1024


def all_reduce_small(arrays):
    n_arr = len(arrays)
    places = []
    for k, a in enumerate(arrays):
        for ri in range(a.shape[0]):
            for c0 in range(0, a.shape[1], SMALL_LANES):
                places.append((k, ri, c0, min(SMALL_LANES, a.shape[1] - c0), len(places)))
    assert len(places) <= SMALL_ROWS

    def body(*refs):
        ins, outs = refs[:n_arr], refs[n_arr:2 * n_arr]
        buf, send, recv = refs[2 * n_arr:]
        x, y, c, _, _ = _place()
        me = 4 * x + 2 * y + c
        buf[me] = jnp.zeros((SMALL_ROWS, SMALL_LANES), F32)
        for k, ri, c0, width, row in places:
            buf[me, row:row + 1, 0:width] = ins[k][ri:ri + 1, c0:c0 + width]
        copies = []
        for r in range(1, N_DEV):
            px = (1 - x) if r & 4 else x
            py = (1 - y) if r & 2 else y
            pc = (1 - c) if r & 1 else c
            copies.append(pltpu.make_async_remote_copy(
                src_ref=buf.at[me], dst_ref=buf.at[me], send_sem=send.at[r - 1], recv_sem=recv.at[r - 1],
                device_id=(px, py, pc), device_id_type=MESH))
        for cp in copies:
            cp.start()
        for cp in copies:
            cp.wait_recv()
        for cp in copies:
            cp.wait_send()
        acc = buf[0]
        for j in range(1, N_DEV):
            acc = acc + buf[j]
        for k, ri, c0, width, row in places:
            outs[k][ri:ri + 1, c0:c0 + width] = acc[row:row + 1, 0:width]

    dma = pltpu.SemaphoreType.DMA
    vmem = pl.BlockSpec(memory_space=pltpu.VMEM)
    return pl.pallas_call(
        body, out_shape=[_sds(a.shape, F32) for a in arrays], in_specs=[vmem] * n_arr, out_specs=[vmem] * n_arr,
        name="all_reduce_small",
        scratch_shapes=[pltpu.VMEM((N_DEV, SMALL_ROWS, SMALL_LANES), F32), dma((N_DEV - 1,)), dma((N_DEV - 1,))],
        compiler_params=pltpu.CompilerParams(has_side_effects=True))(*arrays)


def _row_tile(rows, limit, multiple):
    return max(t for t in range(multiple, min(rows, limit) + 1, multiple) if rows % t == 0)


def add_pair(a, b, name):
    _, h, c = a.shape

    def body(a_ref, b_ref, o_ref):
        o_ref[...] = (a_ref[...].astype(F32) + b_ref[...].astype(F32)).astype(o_ref.dtype)

    blk = pl.BlockSpec((None, h, c), lambda j: (j, 0, 0))
    return _call(body, name=name, out_shape=_sds(a.shape, a.dtype), grid=(N_CHIP,), in_specs=[blk, blk],
                 out_specs=blk)(a, b)


def sum_slots(buf, name):
    _, h, c = buf.shape
    tm = _row_tile(h, 256, 16)

    def body(b_ref, o_ref):
        acc = b_ref[3].astype(F32)
        for t in range(3):
            acc = acc + b_ref[t].astype(F32)
        o_ref[...] = acc

    return _call(body, name=name, out_shape=_sds((h, c), F32), grid=(h // tm,),
                 in_specs=[pl.BlockSpec((N_CHIP, tm, c), lambda i: (0, i, 0))],
                 out_specs=pl.BlockSpec((tm, c), lambda i: (i, 0)))(buf)


def _adamw_math(w, g, m, v):
    c1 = 1.0 - ADAM_B1 ** ADAM_STEP
    c2 = 1.0 - ADAM_B2 ** ADAM_STEP
    m2 = ADAM_B1 * m + (1.0 - ADAM_B1) * g
    v2 = ADAM_B2 * v + (1.0 - ADAM_B2) * (g * g)
    return -ADAM_LR * ((m2 / c1) / (jnp.sqrt(v2 / c2) + ADAM_EPS) + ADAM_WD * w), m2, v2


def adamw(w, g, row_off, m, v, name):
    _, r, c = w.shape
    tm = r if r < 8 else _row_tile(math.gcd(r, row_off) if row_off else r, 128, 8)

    def body(w_ref, g_ref, m_ref, v_ref, go_ref, d_ref, m2_ref, v2_ref):
        gv = g_ref[...]
        go_ref[...] = gv
        d_ref[...], m2_ref[...], v2_ref[...] = _adamw_math(w_ref[...], gv, m_ref[...], v_ref[...])

    blk = pl.BlockSpec((None, tm, c), lambda i: (0, i, 0))
    shp = _sds((1, r, c), F32)
    return _call(body, name=name, out_shape=[shp] * 4, grid=(r // tm,),
                 in_specs=[blk, pl.BlockSpec((tm, c), lambda i: (row_off // tm + i, 0)), blk, blk],
                 out_specs=[blk] * 4)(w, g, m, v)


def adamw_small(ws, gs, ms, vs):
    n = len(ws)

    def body(*refs):
        ins, outs = refs[:4 * n], refs[4 * n:]
        for k in range(n):
            w_ref, g_ref, m_ref, v_ref = (ins[j * n + k] for j in range(4))
            outs[k][...], outs[n + k][...], outs[2 * n + k][...] = _adamw_math(w_ref[...], g_ref[...], m_ref[...],
                                                                               v_ref[...])

    vmem = pl.BlockSpec(memory_space=pltpu.VMEM)
    shapes = [_sds(w.shape, F32) for w in ws] * 3
    res = pl.pallas_call(body, out_shape=shapes, in_specs=[vmem] * (4 * n), out_specs=[vmem] * (3 * n),
                         name="adamw_small")(*ws, *gs, *ms, *vs)
    return res[:n], res[n:2 * n], res[2 * n:]


BIG = ("ffn1_w_gate", "ffn1_w_up", "ffn1_w_down", "w_in", "w_attn_branch", "w_ssd_branch", "w_out",
       "ffn2_w_gate", "ffn2_w_up", "ffn2_w_down")
SMALL = ("ffn1_norm", "mix_norm", "q_norm", "k_norm", "conv_b", "dt_bias", "a_log", "d_skip", "ssd_norm", "ffn2_norm")
WEIGHTS = ("ffn1_norm", "ffn1_w_gate", "ffn1_w_up", "ffn1_w_down", "mix_norm", "w_in", "q_norm", "k_norm", "conv_w",
           "conv_b", "dt_bias", "a_log", "d_skip", "ssd_norm", "w_attn_branch", "w_ssd_branch", "w_out", "ffn2_norm",
           "ffn2_w_gate", "ffn2_w_up", "ffn2_w_down")
CONV_SHARD = SSD_CONV_DIM // N_CHIP
CLASSES = {
    "ffn1_in": (("ffn1_w_gate", 1024), ("ffn1_w_up", 1024)),
    "ffn1_out": (("ffn1_w_down", 704),),
    "mix_in": (("w_in", 1024),),
    "mix_attn": (("w_attn_branch", 512),),
    "late_out": (("ffn2_w_down", 704), ("w_ssd_branch", 512), ("w_out", 256)),
    "ffn2_in": (("ffn2_w_gate", 1024), ("ffn2_w_up", 1024)),
}
CLASS_TILE = {"ffn1_in": 256, "ffn1_out": 176, "mix_in": 128, "mix_attn": 256, "late_out": 368, "ffn2_in": 256,
              "mix_in_top": 128, "mix_in_bottom": 128}
SIBLING_TILE = {"ffn1_in": 1024, "ffn1_out": 352, "mix_attn": 256, "late_out": 736, "ffn2_in": 1024,
                "mix_in_top": 256, "mix_in_bottom": 256}


def _pack_small(vals, conv_part, loss_part=None):
    flat = [vals[k].reshape(-1) for k in SMALL]
    flat.append(jnp.zeros((SSD_CONV * SSD_CONV_DIM,), F32) if conv_part is None else conv_part.reshape(-1))
    flat.append(jnp.zeros((1,), F32) if loss_part is None else loss_part.reshape(1))
    flat = jnp.concatenate(flat)
    return jnp.pad(flat, (0, SMALL_ROWS * D_MODEL - flat.shape[0])).reshape(SMALL_ROWS, D_MODEL)


def _unpack_small(pack, like):
    flat = pack.reshape(-1)
    out, off = {}, 0
    for k in SMALL:
        n = like[k].size
        out[k] = flat[off:off + n].reshape(like[k].shape)
        off += n
    conv = flat[off:off + SSD_CONV * SSD_CONV_DIM].reshape(SSD_CONV, SSD_CONV_DIM)
    return out, conv, flat[off + SSD_CONV * SSD_CONV_DIM]


def _chip_major_cols(a):
    r = a.shape[0]
    return a.reshape(r, N_CHIP, -1).transpose(1, 0, 2)


def _from_chip_major_cols(a):
    return a.transpose(1, 0, 2).reshape(a.shape[1], -1)


def kernel(x, ffn1_norm, ffn1_w_gate, ffn1_w_up, ffn1_w_down, mix_norm, w_in, q_norm, k_norm, conv_w, conv_b, dt_bias, a_log, d_skip, ssd_norm, w_attn_branch, w_ssd_branch, w_out, ffn2_norm, ffn2_w_gate, ffn2_w_up, ffn2_w_down, loss_target, m_ffn1_norm, m_ffn1_w_gate, m_ffn1_w_up, m_ffn1_w_down, m_mix_norm, m_w_in, m_q_norm, m_k_norm, m_conv_w, m_conv_b, m_dt_bias, m_a_log, m_d_skip, m_ssd_norm, m_w_attn_branch, m_w_ssd_branch, m_w_out, m_ffn2_norm, m_ffn2_w_gate, m_ffn2_w_up, m_ffn2_w_down, v_ffn1_norm, v_ffn1_w_gate, v_ffn1_w_up, v_ffn1_w_down, v_mix_norm, v_w_in, v_q_norm, v_k_norm, v_conv_w, v_conv_b, v_dt_bias, v_a_log, v_d_skip, v_ssd_norm, v_w_attn_branch, v_w_ssd_branch, v_w_out, v_ffn2_norm, v_ffn2_w_gate, v_ffn2_w_up, v_ffn2_w_down):
    env = dict(locals())
    wts = {k: env[k] for k in WEIGHTS}
    moms = {k: env["m_" + k] for k in WEIGHTS}
    vars_ = {k: env["v_" + k] for k in WEIGHTS}
    x0 = x[0]
    target = loss_target[0]

    def gather(classes):
        shards = [jnp.concatenate([wts[k][0] for k, _ in CLASSES[c]], axis=0).astype(BF16) for c in classes]
        return gather_rider(shards, [CLASS_TILE[c] for c in classes])

    def reducer(classes, parts):
        sums = [sibling_sum(p, SIBLING_TILE[c], f"sibling_sum_{c}") for c, p in zip(classes, parts)]
        return owner_sum_rider(sums, [CLASS_TILE[c] for c in classes])

    x1, saved1, (w_ffn1_in,), (w_ffn1_out, w_mix_in, w_mix_attn) = ffn_forward(
        x0, ffn1_norm, lambda rode: rode[0], lambda rode: rode[0], "ffn1", rms_rider=gather(["ffn1_in"]),
        up_rider=gather(["ffn1_out", "mix_in", "mix_attn"]))
    dt0, dt1 = IN_DT0 - 3 * IN_SHARD, IN_DT1 - 3 * IN_SHARD
    mixer_w = dict(
        mix_norm=mix_norm,
        w_in_main=jnp.concatenate([w_mix_in[0], w_mix_in[1], w_mix_in[2], w_mix_in[3][:, :dt0], w_mix_in[3][:, dt1:]],
                                  axis=1),
        w_in_dt=jnp.pad(w_mix_in[3][:, dt0:dt1], ((0, 0), (0, DT_PAD - SSD_HEADS))),
        q_gain=jnp.tile(q_norm, (1, 2)), k_gain=jnp.tile(k_norm, (1, 2)),
        conv_w=_from_chip_major_cols(gather_conv_w(conv_w[0])), conv_b=conv_b, dt_bias=dt_bias, a_log=a_log,
        d_skip=d_skip, ssd_norm=ssd_norm, w_attn_branch=_from_chip_major_cols(w_mix_attn))

    def later_weights(rode):
        late = rode[0]
        return dict(w_ssd_branch=late[:, 704:1216].reshape(SSD_INNER, D_MODEL),
                    w_out=late[:, 1216:1472].reshape(D_MODEL, D_MODEL))

    x2, saved_mix, (w_late_out, w_ffn2_in) = mixer_forward(x1, mixer_w, gather(["late_out", "ffn2_in"]), later_weights)
    x3, saved2, _, _ = ffn_forward(x2, ffn2_norm, lambda rode: w_ffn2_in, lambda rode: w_late_out, "ffn2")
    dx3, sq = loss_grad(x3, target, "loss")

    grads = {}
    dx2, grads["ffn2_norm"], d_ffn2_in, d_ffn2_down = ffn_backward(dx3, x2, ffn2_norm, w_ffn2_in, w_late_out, saved2,
                                                                   "ffn2")

    def ride_early(g):
        late = jnp.concatenate([d_ffn2_down, g["w_ssd_branch"].reshape(N_CHIP, -1, D_MODEL),
                                g["w_out"].reshape(N_CHIP, -1, D_MODEL)], axis=1)
        return reducer(["ffn2_in", "late_out"], [d_ffn2_in, late])

    g_in_rows = {}

    def ride_late(g):
        main = g["w_in_main"]
        last = jnp.concatenate([main[:, 3 * IN_SHARD:IN_DT0], g["w_in_dt"][:, :SSD_HEADS], main[:, IN_DT0:]], axis=1)
        for part, rows in (("top", slice(0, D_MODEL // 2)), ("bottom", slice(D_MODEL // 2, D_MODEL))):
            g_in_rows[part] = jnp.stack([main[rows, j * IN_SHARD:(j + 1) * IN_SHARD] for j in range(3)]
                                        + [last[rows]])
        return reducer(["mix_in_top", "mix_attn"], [g_in_rows["top"], _chip_major_cols(g["w_attn_branch"])])

    dx1, gmix = mixer_backward(dx2, x1, saved_mix, ride_early, ride_late)
    dx0, grads["ffn1_norm"], rode_in, rode_out = ffn_backward(
        dx1, x0, ffn1_norm, w_ffn1_in, w_ffn1_out, saved1, "ffn1",
        ride_down=lambda d: reducer(["mix_in_bottom", "ffn1_out"], [g_in_rows["bottom"], d]),
        ride_in=lambda d: reducer(["ffn1_in"], [d]))
    for k in ("mix_norm", "q_norm", "k_norm", "conv_b", "dt_bias", "a_log", "d_skip", "ssd_norm"):
        grads[k] = gmix[k]
    reduced = dict(zip(("ffn2_in", "late_out", "mix_in_top", "mix_attn", "ffn1_in", "mix_in_bottom", "ffn1_out"),
                       (*gmix["rode_early"], *gmix["rode_late"], rode_in[0], *rode_out)))
    reduced = {c: r.reshape(-1, r.shape[2]) for c, r in reduced.items()}
    reduced["mix_in"] = jnp.concatenate([reduced.pop("mix_in_top"), reduced.pop("mix_in_bottom")], axis=0)
    summed = all_reduce_small([grads[k] for k in SMALL]
                              + [gmix["conv_w"], (0.5 * jnp.sum(sq) / D_MODEL).reshape(1, 1)])
    g_small = dict(zip(SMALL, summed))
    loss = summed[-1].reshape(())
    chip = 2 * lax.axis_index("x") + lax.axis_index("y")
    g_conv = lax.dynamic_slice_in_dim(summed[-2], chip * CONV_SHARD, CONV_SHARD, axis=1)

    g_final, delta, new_m, new_v = dict(g_small), {}, {}, {}

    def update(k, g_arr, row_off):
        w, m, v = wts[k], moms[k], vars_[k]
        rows, cols = w.shape[1:]
        if cols % 128:
            res = adamw(jnp.swapaxes(w, 1, 2), g_arr[row_off:row_off + rows].T, 0, jnp.swapaxes(m, 1, 2),
                        jnp.swapaxes(v, 1, 2), f"adamw_{k}")
            res = [jnp.swapaxes(r, 1, 2) for r in res]
        else:
            res = adamw(w, g_arr, row_off, m, v, f"adamw_{k}")
        g_final[k], delta[k], new_m[k], new_v[k] = res

    for cls, members in CLASSES.items():
        off = 0
        for k, rows in members:
            update(k, reduced[cls], off)
            off += rows
    update("conv_w", g_conv, 0)
    small = adamw_small(*([d[k] for k in SMALL] for d in (wts, g_small, moms, vars_)))
    for res, vals in zip((delta, new_m, new_v), small):
        res.update(zip(SMALL, vals))

    return (loss, dx0[None], *[g_final[k] for k in WEIGHTS], *[delta[k] for k in WEIGHTS],
            *[new_m[k] for k in WEIGHTS], *[new_v[k] for k in WEIGHTS])
```

```python
import collections
import functools
import math

import jax
import jax.numpy as jnp
from jax import lax
from jax.experimental import pallas as pl
from jax.experimental.pallas import tpu as pltpu

F32 = jnp.float32
BF16 = jnp.bfloat16
MESH = pl.DeviceIdType.MESH

EPS = 1e-6
D_MODEL = 1024
D_FF = 2816
N_CHIP = 4
FF_SHARD = D_FF // N_CHIP
HD = 64
BLK = 128
ATTN_DILATIONS = (1, 4, 16)
HEADS_PER_PATTERN = 8
N_ATTN_HEADS = 24
ALIBI_MAX_EXP = 8.0
ATTN_QKV = 1536
GROUP_W = 512
SSD_INNER = 2048
SSD_HEADS = 32
SSD_GROUPS = 4
SSD_CONV = 4
SSD_CONV_DIM = 3072
IN_COLS = 11808
IN_DT0, IN_DT1 = 9728, 9760
IN_SHARD = IN_COLS // 4
COL_K, COL_V, COL_Z, COL_XBC, COL_GA, COL_GS, P_COLS = 1536, 3072, 4608, 6656, 9728, 10752, 11776
DT_PAD = 128

ADAM_LR, ADAM_B1, ADAM_B2, ADAM_EPS, ADAM_WD, ADAM_STEP = 0.001, 0.9, 0.999, 1e-08, 0.01, 10

V7X_VMEM_LIMIT = 56 * 1024 * 1024
NEG = -1e30


Rider = collections.namedtuple("Rider", "arrays out_shape scratch start finish")
Rider.__doc__ = """An exchange between devices that rides in a compute kernel: its copies are started in the host's
first grid step and waited for in its last, so they travel while the host computes.  arrays / out_shape: extra HBM
operands and results; scratch: extra scratch; start, finish: f(in_refs, out_refs, scratch_refs)."""


def _call(body, *, name, out_shape, in_specs, out_specs, grid=(), scratch_shapes=(), aliases=None, rider=None):
    params = dict(dimension_semantics=("arbitrary",) * len(grid), vmem_limit_bytes=V7X_VMEM_LIMIT)
    if rider is None:
        return pl.pallas_call(
            body, out_shape=out_shape, grid=grid, in_specs=in_specs, out_specs=out_specs,
            scratch_shapes=scratch_shapes, input_output_aliases=aliases or {}, name=name,
            compiler_params=pltpu.CompilerParams(**params))
    single = not isinstance(out_shape, (list, tuple))
    main_out = [out_shape] if single else list(out_shape)
    main_specs = [out_specs] if single else list(out_specs)
    n_in, n_out, n_scr = len(in_specs), len(main_out), len(scratch_shapes)
    r_in, r_out = len(rider.arrays), len(rider.out_shape)

    def wrapped(*refs):
        ins, refs = refs[:n_in], refs[n_in:]
        r_ins, refs = refs[:r_in], refs[r_in:]
        outs, refs = refs[:n_out], refs[n_out:]
        r_outs, refs = refs[:r_out], refs[r_out:]
        scr, r_scr = refs[:n_scr], refs[n_scr:]
        first = last = None
        for axis, size in enumerate(grid):
            at_start, at_end = pl.program_id(axis) == 0, pl.program_id(axis) == size - 1
            first = at_start if first is None else jnp.logical_and(first, at_start)
            last = at_end if last is None else jnp.logical_and(last, at_end)

        @pl.when(first)
        def _():
            rider.start(r_ins, r_outs, r_scr)

        body(*ins, *outs, *scr)

        @pl.when(last)
        def _():
            rider.finish(r_ins, r_outs, r_scr)

    hbm = pl.BlockSpec(memory_space=pl.ANY)
    call = pl.pallas_call(
        wrapped, out_shape=main_out + list(rider.out_shape), grid=grid, in_specs=list(in_specs) + [hbm] * r_in,
        out_specs=main_specs + [hbm] * r_out, scratch_shapes=list(scratch_shapes) + list(rider.scratch), name=name,
        compiler_params=pltpu.CompilerParams(has_side_effects=True, **params))

    def run(*args):
        res = call(*args, *rider.arrays)
        main = res[:n_out]
        return (main[0] if single else main), res[n_out:]

    return run


def _sds(shape, dtype):
    return jax.ShapeDtypeStruct(tuple(shape), dtype)


def _dot(a, b):
    return jnp.dot(a, b, preferred_element_type=F32)


def _dot_nt(a, b):
    return lax.dot_general(a, b, (((1,), (1,)), ((), ())), preferred_element_type=F32)


def _dot_tn(a, b):
    return lax.dot_general(a, b, (((0,), (0,)), ((), ())), preferred_element_type=F32)


def _dot_hi(a, b):
    return jnp.dot(a, b, preferred_element_type=F32, precision=lax.Precision.HIGHEST)


def _sigmoid(x):
    return 1.0 / (1.0 + jnp.exp(-x))


def _lane_first_half(shape):
    return lax.broadcasted_iota(jnp.int32, shape, len(shape) - 1) < HD


def _pair_sum(x, first):
    s_all = jnp.sum(x, axis=-1, keepdims=True)
    s_a = jnp.sum(jnp.where(first, x, 0.0), axis=-1, keepdims=True)
    return s_a, s_all - s_a


def _rowwise(name, fn, rows, consts, outs, accs=(), tm=512, rider=None):
    n_rows = None
    in_arrays, in_specs = [], []
    for r in rows:
        if isinstance(r, tuple):
            arr, w, cb = r
            spec = pl.BlockSpec((tm, w), functools.partial(lambda i, cb: (i, cb), cb=cb))
        else:
            arr = r
            spec = pl.BlockSpec((tm, arr.shape[1]), lambda i: (i, 0))
        n_rows = arr.shape[0]
        in_arrays.append(arr)
        in_specs.append(spec)
    for c in consts:
        in_arrays.append(c)
        in_specs.append(pl.BlockSpec(c.shape, functools.partial(lambda i, n: (0,) * n, n=c.ndim)))
    out_shape = [_sds(s, d) for s, d in outs] + [_sds(s, d) for s, d in accs]
    out_specs = [pl.BlockSpec((tm, s[1]), lambda i: (i, 0)) for s, _ in outs]
    out_specs += [pl.BlockSpec(s, functools.partial(lambda i, n: (0,) * n, n=len(s))) for s, _ in accs]

    def body(*refs):
        fn(pl.program_id(0), *refs)

    res = _call(body, name=name, out_shape=out_shape, grid=(n_rows // tm,), in_specs=in_specs,
                out_specs=out_specs, rider=rider)(*in_arrays)
    return res


def rms_fwd(x, gain, name, rider=None):
    def fn(i, x_ref, g_ref, h_ref):
        xv = x_ref[...]
        r = lax.rsqrt(jnp.mean(xv * xv, axis=-1, keepdims=True) + EPS)
        h_ref[...] = (xv * r * g_ref[...]).astype(h_ref.dtype)

    res = _rowwise(name, fn, [x], [gain], [(x.shape, BF16)], rider=rider)
    return res[0] if rider is None else (res[0][0], res[1])


def rms_bwd(dhs, x, gain, dx_in, name):
    n = len(dhs)

    def fn(i, *refs):
        dh_refs, (x_ref, dxin_ref, g_ref, dx_ref, dg_ref) = refs[:n], refs[n:]
        dh = dh_refs[0][...]
        for r in dh_refs[1:]:
            dh = dh + r[...]
        xv = x_ref[...]
        r = lax.rsqrt(jnp.mean(xv * xv, axis=-1, keepdims=True) + EPS)
        xn = xv * r
        dxn = dh * g_ref[...]
        dx_ref[...] = dxin_ref[...] + r * (dxn - xn * jnp.mean(dxn * xn, axis=-1, keepdims=True))

        @pl.when(i == 0)
        def _():
            dg_ref[...] = jnp.zeros_like(dg_ref)

        dg_ref[...] += jnp.sum(dh * xn, axis=0, keepdims=True)

    return _rowwise(name, fn, list(dhs) + [x, dx_in], [gain], [(x.shape, F32)], [((1, x.shape[1]), F32)])


def loss_grad(y, target, name):
    def fn(i, y_ref, t_ref, dy_ref, sq_ref):
        err = y_ref[...] - t_ref[...]
        dy_ref[...] = err * (1.0 / y_ref.shape[1])

        @pl.when(i == 0)
        def _():
            sq_ref[...] = jnp.zeros_like(sq_ref)

        sq_ref[...] += jnp.sum(err * err, axis=0, keepdims=True)

    return _rowwise(name, fn, [y, target], [], [(y.shape, F32)], [((1, y.shape[1]), F32)])


def matmul_nn(a, b, name, out_dtype, tm, tn, res=None, scale=1.0, rider=None):
    s, k = a.shape
    n = b.shape[1]

    def body(*refs):
        if res is None:
            a_ref, b_ref, o_ref = refs
            o_ref[...] = _dot(a_ref[...], b_ref[...]).astype(o_ref.dtype)
        else:
            a_ref, b_ref, r_ref, o_ref = refs
            o_ref[...] = (r_ref[...] + scale * _dot(a_ref[...], b_ref[...])).astype(o_ref.dtype)

    in_specs = [pl.BlockSpec((tm, k), lambda i, j: (i, 0)), pl.BlockSpec((k, tn), lambda i, j: (0, j))]
    args = [a, b]
    if res is not None:
        in_specs.append(pl.BlockSpec((tm, tn), lambda i, j: (i, j)))
        args.append(res)
    return _call(body, name=name, out_shape=_sds((s, n), out_dtype), grid=(s // tm, n // tn), in_specs=in_specs,
                 out_specs=pl.BlockSpec((tm, tn), lambda i, j: (i, j)), rider=rider)(*args)


def matmul_nt(a, b, name, out_dtype, tm, tn, tk, rider=None):
    s, k = a.shape
    n = b.shape[0]
    nk = k // tk

    def body(a_ref, b_ref, o_ref, acc_ref):
        kk = pl.program_id(2)

        @pl.when(kk == 0)
        def _():
            acc_ref[...] = jnp.zeros_like(acc_ref)

        acc_ref[...] += _dot_nt(a_ref[...].astype(BF16), b_ref[...])

        @pl.when(kk == nk - 1)
        def _():
            o_ref[...] = acc_ref[...].astype(o_ref.dtype)

    return _call(body, name=name, out_shape=_sds((s, n), out_dtype), grid=(s // tm, n // tn, nk),
                 in_specs=[pl.BlockSpec((tm, tk), lambda i, j, kk: (i, kk)),
                           pl.BlockSpec((tn, tk), lambda i, j, kk: (j, kk))],
                 out_specs=pl.BlockSpec((tm, tn), lambda i, j, kk: (i, j)),
                 scratch_shapes=[pltpu.VMEM((tm, tn), F32)], rider=rider)(a, b)


def matmul_tn(a, b, name, tn, ts, a_scale=None, b_scale=None, rider=None):
    s, m = a.shape
    n = b.shape[1]
    ns = s // ts

    def body(a_ref, b_ref, o_ref, acc_ref):
        ss = pl.program_id(1)

        @pl.when(ss == 0)
        def _():
            acc_ref[...] = jnp.zeros_like(acc_ref)

        av, bv = a_ref[...], b_ref[...]
        if a_scale is not None:
            av = av * a_scale
        if b_scale is not None:
            bv = bv * b_scale
        acc_ref[...] += _dot_tn(av.astype(BF16), bv.astype(BF16))

        @pl.when(ss == ns - 1)
        def _():
            o_ref[...] = acc_ref[...].astype(o_ref.dtype)

    return _call(body, name=name, out_shape=_sds((m, n), BF16), grid=(n // tn, ns),
                 in_specs=[pl.BlockSpec((ts, m), lambda j, ss: (ss, 0)), pl.BlockSpec((ts, tn), lambda j, ss: (ss, j))],
                 out_specs=pl.BlockSpec((m, tn), lambda j, ss: (0, j)),
                 scratch_shapes=[pltpu.VMEM((m, tn), F32)], rider=rider)(a, b)


def _piece_specs(pieces, tile, rows_tile, tile_axis_first):
    specs, ranges, t0 = [], [], 0
    for a in pieces:
        n = a.shape[1] // tile

        def index(*ids, t0=t0, n=n):
            t, r = (ids[0], ids[1]) if tile_axis_first else (ids[2], ids[0])
            on = jnp.logical_and(t >= t0, t < t0 + n)
            return jnp.where(on, r, 0), jnp.clip(t - t0, 0, n - 1)

        specs.append(pl.BlockSpec((rows_tile, tile), index))
        ranges.append((t0, n))
        t0 += n
    return specs, ranges


def matmul_tn_pieces(a, pieces, name, tn, ts, rider=None):
    s, m = a.shape
    ns = s // ts
    specs, ranges = _piece_specs(pieces, tn, ts, True)
    n_total = sum(n for _, n in ranges)

    def body(a_ref, *refs):
        b_refs, o_ref, acc_ref = refs[:len(pieces)], refs[-2], refs[-1]
        j, ss = pl.program_id(0), pl.program_id(1)

        @pl.when(ss == 0)
        def _():
            acc_ref[...] = jnp.zeros_like(acc_ref)

        for b_ref, (t0, n) in zip(b_refs, ranges):
            @pl.when(jnp.logical_and(j >= t0, j < t0 + n))
            def _(b_ref=b_ref):
                acc_ref[...] += _dot_tn(a_ref[...], b_ref[...])

        @pl.when(ss == ns - 1)
        def _():
            o_ref[...] = acc_ref[...].astype(o_ref.dtype)

    return _call(body, name=name, out_shape=_sds((m, n_total * tn), BF16), grid=(n_total, ns),
                 in_specs=[pl.BlockSpec((ts, m), lambda j, ss: (ss, 0))] + specs,
                 out_specs=pl.BlockSpec((m, tn), lambda j, ss: (0, j)),
                 scratch_shapes=[pltpu.VMEM((m, tn), F32)], rider=rider)(a, *pieces)


def matmul_nt_pieces(pieces, b, name, out_dtype, tm, tn, tk, rider=None):
    s = pieces[0].shape[0]
    n = b.shape[0]
    specs, ranges = _piece_specs(pieces, tk, tm, False)
    nk = sum(cnt for _, cnt in ranges)

    def body(*refs):
        a_refs, b_ref, o_ref, acc_ref = refs[:len(pieces)], refs[-3], refs[-2], refs[-1]
        kk = pl.program_id(2)

        @pl.when(kk == 0)
        def _():
            acc_ref[...] = jnp.zeros_like(acc_ref)

        for a_ref, (t0, cnt) in zip(a_refs, ranges):
            @pl.when(jnp.logical_and(kk >= t0, kk < t0 + cnt))
            def _(a_ref=a_ref):
                acc_ref[...] += _dot_nt(a_ref[...], b_ref[...])

        @pl.when(kk == nk - 1)
        def _():
            o_ref[...] = acc_ref[...].astype(o_ref.dtype)

    return _call(body, name=name, out_shape=_sds((s, n), out_dtype), grid=(s // tm, n // tn, nk),
                 in_specs=specs + [pl.BlockSpec((tn, tk), lambda i, j, kk: (j, kk))],
                 out_specs=pl.BlockSpec((tm, tn), lambda i, j, kk: (i, j)),
                 scratch_shapes=[pltpu.VMEM((tm, tn), F32)], rider=rider)(*pieces, b)


def ffn_up(h, w704, gate_blk, up_blk, name, tm=512, rider=None):
    s = h.shape[0]

    def body(h_ref, wg_ref, wu_ref, g_ref, u_ref, a_ref):
        hv = h_ref[...]
        g = _dot(hv, wg_ref[...])
        u = _dot(hv, wu_ref[...])
        g_ref[...] = g.astype(BF16)
        u_ref[...] = u.astype(BF16)
        a_ref[...] = (g * _sigmoid(g) * u).astype(BF16)

    ospec = pl.BlockSpec((None, tm, FF_SHARD), lambda j, i: (j, i, 0))
    shp = _sds((N_CHIP, s, FF_SHARD), BF16)
    return _call(body, name=name, out_shape=[shp, shp, shp], grid=(N_CHIP, s // tm),
                 in_specs=[pl.BlockSpec((tm, D_MODEL), lambda j, i: (i, 0)),
                           pl.BlockSpec((None, D_MODEL, FF_SHARD), lambda j, i: (j, gate_blk, 0)),
                           pl.BlockSpec((None, D_MODEL, FF_SHARD), lambda j, i: (j, up_blk, 0))],
                 out_specs=[ospec, ospec, ospec], rider=rider)(h, w704, w704)


def ffn_down(a, w1024, blk, x, name, tm=512, rider=None):
    s = x.shape[0]

    def body(a_ref, wd_ref, x_ref, o_ref):
        acc = _dot(a_ref[0], wd_ref[0])
        for j in range(1, N_CHIP):
            acc += _dot(a_ref[j], wd_ref[j])
        o_ref[...] = x_ref[...] + 0.5 * acc

    return _call(body, name=name, out_shape=_sds((s, D_MODEL), F32), grid=(s // tm,),
                 in_specs=[pl.BlockSpec((N_CHIP, tm, FF_SHARD), lambda i: (0, i, 0)),
                           pl.BlockSpec((N_CHIP, FF_SHARD, D_MODEL), lambda i: (0, blk, 0)),
                           pl.BlockSpec((tm, D_MODEL), lambda i: (i, 0))],
                 out_specs=pl.BlockSpec((tm, D_MODEL), lambda i: (i, 0)), rider=rider)(a, w1024, x)


def ffn_bwd_hidden(dx, w1024, blk, g, u, name, tm=1024, rider=None):
    s = dx.shape[0]

    def body(dx_ref, wd_ref, g_ref, u_ref, dg_ref, du_ref):
        dy = (0.5 * dx_ref[...]).astype(BF16)
        da = _dot_nt(dy, wd_ref[...])
        gv = g_ref[...].astype(F32)
        uv = u_ref[...].astype(F32)
        sg = _sigmoid(gv)
        dg_ref[...] = (da * uv * (sg * (1.0 + gv * (1.0 - sg)))).astype(BF16)
        du_ref[...] = (da * gv * sg).astype(BF16)

    hspec = pl.BlockSpec((None, tm, FF_SHARD), lambda j, i: (j, i, 0))
    shp = _sds((N_CHIP, s, FF_SHARD), BF16)
    return _call(body, name=name, out_shape=[shp, shp], grid=(N_CHIP, s // tm),
                 in_specs=[pl.BlockSpec((tm, D_MODEL), lambda j, i: (i, 0)),
                           pl.BlockSpec((None, FF_SHARD, D_MODEL), lambda j, i: (j, blk, 0)), hspec, hspec],
                 out_specs=[hspec, hspec], rider=rider)(dx, w1024, g, u)


def ffn_bwd_input(dg, du, w704, gate_blk, up_blk, name, tm=512, rider=None):
    s = dg.shape[1]

    def body(dg_ref, du_ref, wg_ref, wu_ref, o_ref):
        acc = _dot_nt(dg_ref[0], wg_ref[0]) + _dot_nt(du_ref[0], wu_ref[0])
        for j in range(1, N_CHIP):
            acc += _dot_nt(dg_ref[j], wg_ref[j]) + _dot_nt(du_ref[j], wu_ref[j])
        o_ref[...] = acc

    hspec = pl.BlockSpec((N_CHIP, tm, FF_SHARD), lambda i: (0, i, 0))
    return _call(body, name=name, out_shape=_sds((s, D_MODEL), F32), grid=(s // tm,),
                 in_specs=[hspec, hspec,
                           pl.BlockSpec((N_CHIP, D_MODEL, FF_SHARD), lambda i: (0, gate_blk, 0), pl.Buffered(1)),
                           pl.BlockSpec((N_CHIP, D_MODEL, FF_SHARD), lambda i: (0, up_blk, 0), pl.Buffered(1))],
                 out_specs=pl.BlockSpec((tm, D_MODEL), lambda i: (i, 0)), rider=rider)(dg, du, w704, w704)


def ffn_wgrad_in(h, dgu, name, ts=1024, rider=None):
    s = h.shape[0]
    ns = s // ts

    def body(h_ref, d_ref, o_ref, acc_ref):
        ss = pl.program_id(1)

        @pl.when(ss == 0)
        def _():
            acc_ref[...] = jnp.zeros_like(acc_ref)

        acc_ref[...] += _dot_tn(h_ref[...], d_ref[...])

        @pl.when(ss == ns - 1)
        def _():
            o_ref[...] = acc_ref[...].astype(BF16)

    return _call(body, name=name, out_shape=_sds((N_CHIP, D_MODEL, FF_SHARD), BF16), grid=(N_CHIP, ns),
                 in_specs=[pl.BlockSpec((ts, D_MODEL), lambda j, ss: (ss, 0)),
                           pl.BlockSpec((None, ts, FF_SHARD), lambda j, ss: (j, ss, 0))],
                 out_specs=pl.BlockSpec((None, D_MODEL, FF_SHARD), lambda j, ss: (j, 0, 0)),
                 scratch_shapes=[pltpu.VMEM((D_MODEL, FF_SHARD), F32)], rider=rider)(h, dgu)


def ffn_wgrad_down(a, dx, name, ts=1024):
    s = dx.shape[0]
    ns = s // ts

    def body(a_ref, dx_ref, o_ref, acc_ref):
        ss = pl.program_id(1)

        @pl.when(ss == 0)
        def _():
            acc_ref[...] = jnp.zeros_like(acc_ref)

        acc_ref[...] += _dot_tn(a_ref[...], (0.5 * dx_ref[...]).astype(BF16))

        @pl.when(ss == ns - 1)
        def _():
            o_ref[...] = acc_ref[...].astype(BF16)

    return _call(body, name=name, out_shape=_sds((N_CHIP, FF_SHARD, D_MODEL), BF16), grid=(N_CHIP, ns),
                 in_specs=[pl.BlockSpec((None, ts, FF_SHARD), lambda j, ss: (j, ss, 0)),
                           pl.BlockSpec((ts, D_MODEL), lambda j, ss: (ss, 0))],
                 out_specs=pl.BlockSpec((None, FF_SHARD, D_MODEL), lambda j, ss: (j, 0, 0)),
                 scratch_shapes=[pltpu.VMEM((FF_SHARD, D_MODEL), F32)])(a, dx)


def ffn_forward(x, gain, get_w704, get_w1024, tag, rms_rider=None, up_rider=None, down_rider=None):
    h = rms_fwd(x, gain, f"{tag}_rms", rider=rms_rider)
    h, rode_rms = h if rms_rider is not None else (h, None)
    res = ffn_up(h, get_w704(rode_rms), 0, 1, f"{tag}_up", rider=up_rider)
    (g, u, a), rode_up = res if up_rider is not None else (res, None)
    y = ffn_down(a, get_w1024(rode_up), 0, x, f"{tag}_down", rider=down_rider)
    y, rode_down = y if down_rider is not None else (y, None)
    return y, (h, g, u, a), rode_rms, rode_up, rode_down


def ffn_backward(dy, x, gain, w704, w1024, saved, tag, hidden_rider=None, ride_down=None, ride_in=None):
    h, g, u, a = saved
    d_wd = ffn_wgrad_down(a, dy, f"{tag}_dwd")
    rode_hidden = None
    if hidden_rider is not None:
        (dg, du), rode_hidden = ffn_bwd_hidden(dy, w1024, 0, g, u, f"{tag}_dhid", tm=512, rider=hidden_rider)
    else:
        dg, du = ffn_bwd_hidden(dy, w1024, 0, g, u, f"{tag}_dhid")
    if ride_down is not None:
        d_wg, d_wd = ffn_wgrad_in(h, dg, f"{tag}_dwg", rider=ride_down(d_wd))
    else:
        d_wg = ffn_wgrad_in(h, dg, f"{tag}_dwg")
    d_win = jnp.concatenate([d_wg, ffn_wgrad_in(h, du, f"{tag}_dwu")], axis=1)
    if ride_in is not None:
        dh, d_win = ffn_bwd_input(dg, du, w704, 0, 1, f"{tag}_dh", rider=ride_in(d_win))
    else:
        dh = ffn_bwd_input(dg, du, w704, 0, 1, f"{tag}_dh")
    dx, d_gain = rms_bwd([dh], x, gain, dy, f"{tag}_drms")
    return dx, d_gain, d_win, d_wd, rode_hidden


def _alibi_slope(head):
    return float(2.0 ** (-ALIBI_MAX_EXP * (head + 1) / N_ATTN_HEADS))


def _same_head():
    row = lax.broadcasted_iota(jnp.int32, (2 * HD, 2 * HD), 0)
    col = lax.broadcasted_iota(jnp.int32, (2 * HD, 2 * HD), 1)
    return ((row < HD) == (col < HD)).astype(BF16)


def _head_sums(x, same_head):
    hi = x.astype(BF16)
    lo = (x - hi.astype(F32)).astype(BF16)
    return _dot(hi, same_head) + _dot(lo, same_head)


def _head_norm(t, gain_pair, same_head):
    r = lax.rsqrt(_head_sums(t * t, same_head) * (1.0 / HD) + EPS)
    return t * r * gain_pair, r


def qk_norm_fwd(p, q_gain, k_gain, name):
    s = p.shape[0]

    def fn(i, q_ref, k_ref, qg_ref, kg_ref, qn_ref, kn_ref):
        same_head = _same_head()
        for src, g_ref, dst in ((q_ref, qg_ref, qn_ref), (k_ref, kg_ref, kn_ref)):
            for pr in range(ATTN_QKV // (2 * HD)):
                cols = slice(pr * 2 * HD, (pr + 1) * 2 * HD)
                y, _ = _head_norm(src[:, cols].astype(F32), g_ref[...], same_head)
                dst[:, cols] = y.astype(BF16)

    return _rowwise(name, fn, [(p, ATTN_QKV, 0), (p, ATTN_QKV, 1)], [q_gain, k_gain],
                    [((s, ATTN_QKV), BF16), ((s, ATTN_QKV), BF16)])


def qk_norm_bwd(p, dqs, dks, q_gain, k_gain, name):
    s = p.shape[0]
    pairs_per_pattern = GROUP_W // (2 * HD)

    def fn(i, q_ref, k_ref, dq0, dq1, dq2, dk0, dk1, dk2, qg_ref, kg_ref, dqk_ref, dqg_ref, dkg_ref):
        same_head = _same_head()

        @pl.when(i == 0)
        def _():
            dqg_ref[...] = jnp.zeros_like(dqg_ref)
            dkg_ref[...] = jnp.zeros_like(dkg_ref)

        for src, d_refs, g_ref, dst, dg_ref in (
                (q_ref, (dq0, dq1, dq2), qg_ref, dqk_ref.at[:, 0:ATTN_QKV], dqg_ref),
                (k_ref, (dk0, dk1, dk2), kg_ref, dqk_ref.at[:, ATTN_QKV:2 * ATTN_QKV], dkg_ref)):
            for pr in range(ATTN_QKV // (2 * HD)):
                cols = slice(pr * 2 * HD, (pr + 1) * 2 * HD)
                t = src[:, cols].astype(F32)
                r = lax.rsqrt(_head_sums(t * t, same_head) * (1.0 / HD) + EPS)
                xn = t * r
                within = (pr % pairs_per_pattern) * 2 * HD
                dy = d_refs[pr // pairs_per_pattern][:, within:within + 2 * HD]
                dg_ref[:, cols] += jnp.sum(dy * xn, axis=0, keepdims=True)
                dxn = dy * g_ref[...]
                mean = _head_sums(dxn * xn, same_head) * (1.0 / HD)
                dst[:, cols] = (r * (dxn - xn * mean)).astype(BF16)

    return _rowwise(name, fn, [(p, ATTN_QKV, 0), (p, ATTN_QKV, 1)] + list(dqs) + list(dks), [q_gain, k_gain],
                    [((s, 2 * ATTN_QKV), BF16)], [((1, ATTN_QKV), F32), ((1, ATTN_QKV), F32)])


def _to_streams(a, d):
    if d == 1:
        return a
    s, c = a.shape
    return a.reshape(s // d, d, c).transpose(1, 0, 2).reshape(s, c)


def _from_streams(a, d):
    if d == 1:
        return a
    s, c = a.shape
    return a.reshape(d, s // d, c).transpose(1, 0, 2).reshape(s, c)


def _attn_masks():
    row = lax.broadcasted_iota(jnp.int32, (BLK, BLK), 0)
    col = lax.broadcasted_iota(jnp.int32, (BLK, BLK), 1)
    rel_diag = row - col
    rel_prev = rel_diag + BLK
    return rel_diag, rel_prev


def attn_fwd(q, k, v, pattern, name, tq=512):
    s = q.shape[0]
    d = ATTN_DILATIONS[pattern]
    blocks_per_stream = (s // d) // BLK
    nsb = tq // BLK

    def body(q_ref, k_ref, v_ref, kp_ref, vp_ref, o_ref, l_ref):
        i = pl.program_id(0)
        rel_diag, rel_prev = _attn_masks()
        first = _lane_first_half((BLK, 2 * HD))
        rd_f = (rel_diag * d).astype(F32)
        rp_f = (rel_prev * d).astype(F32)
        for sb in range(nsb):
            rows = slice(sb * BLK, (sb + 1) * BLK)
            has_prev = ((i * nsb + sb) % blocks_per_stream != 0).astype(jnp.int32)
            m_diag = rel_diag >= 0
            m_prev = (rel_prev + (1 - has_prev) * (4 * BLK)) <= BLK
            for pr in range(GROUP_W // (2 * HD)):
                cols = slice(pr * 2 * HD, (pr + 1) * 2 * HD)
                qp = q_ref[rows, cols]
                kc, vc = k_ref[rows, cols], v_ref[rows, cols]
                if sb == 0:
                    kp, vp = kp_ref[:, cols], vp_ref[:, cols]
                else:
                    prows = slice((sb - 1) * BLK, sb * BLK)
                    kp, vp = k_ref[prows, cols], v_ref[prows, cols]
                outs, lses = [], []
                for e in range(2):
                    slope = _alibi_slope(pattern * HEADS_PER_PATTERN + 2 * pr + e)
                    qm = jnp.where(first if e == 0 else jnp.logical_not(first), qp, jnp.zeros_like(qp))
                    s1 = jnp.where(m_diag, _dot_nt(qm, kc) * 0.125 - slope * rd_f, NEG)
                    s0 = jnp.where(m_prev, _dot_nt(qm, kp) * 0.125 - slope * rp_f, NEG)
                    m = jnp.maximum(jnp.max(s1, axis=-1, keepdims=True), jnp.max(s0, axis=-1, keepdims=True))
                    p1 = jnp.exp(s1 - m)
                    p0 = jnp.exp(s0 - m)
                    l = jnp.sum(p1, axis=-1, keepdims=True) + jnp.sum(p0, axis=-1, keepdims=True)
                    inv = 1.0 / l
                    outs.append(_dot((p1 * inv).astype(BF16), vc) + _dot((p0 * inv).astype(BF16), vp))
                    lses.append(m + jnp.log(l))
                o_ref[rows, cols] = jnp.where(first, outs[0], outs[1])
                l_ref[rows, cols] = jnp.where(first, lses[0], lses[1])

    cur = pl.BlockSpec((tq, GROUP_W), lambda i: (i, 0))
    prev = pl.BlockSpec((BLK, GROUP_W), lambda i: (jnp.maximum(i * nsb - 1, 0), 0))
    return _call(body, name=name, out_shape=[_sds((s, GROUP_W), F32), _sds((s, GROUP_W), F32)], grid=(s // tq,),
                 in_specs=[cur, cur, cur, prev, prev], out_specs=[cur, cur])(q, k, v, k, v)


def attn_merge_fwd(os_, lses, name):
    s = os_[0].shape[0]

    def fn(i, o0, o1, o2, l0, l1, l2, out_ref):
        m = jnp.maximum(jnp.maximum(l0[...], l1[...]), l2[...])
        e0, e1, e2 = jnp.exp(l0[...] - m), jnp.exp(l1[...] - m), jnp.exp(l2[...] - m)
        inv = 1.0 / (e0 + e1 + e2)
        out_ref[...] = ((e0 * inv) * o0[...] + (e1 * inv) * o1[...] + (e2 * inv) * o2[...]).astype(BF16)

    return _rowwise(name, fn, list(os_) + list(lses), [], [((s, GROUP_W), BF16)])[0]


def attn_merge_bwd(d_out, os_, lses, name):
    s = d_out.shape[0]

    def fn(i, do_ref, o0, o1, o2, l0, l1, l2, d0, d1, d2, c0, c1, c2):
        first = _lane_first_half((do_ref.shape[0], 2 * HD))
        m = jnp.maximum(jnp.maximum(l0[...], l1[...]), l2[...])
        e0, e1, e2 = jnp.exp(l0[...] - m), jnp.exp(l1[...] - m), jnp.exp(l2[...] - m)
        inv = 1.0 / (e0 + e1 + e2)
        w0, w1, w2 = e0 * inv, e1 * inv, e2 * inv
        do = do_ref[...]
        prod = do * (w0 * o0[...] + w1 * o1[...] + w2 * o2[...])
        same_head = _same_head()
        for pr in range(GROUP_W // (2 * HD)):
            cols = slice(pr * 2 * HD, (pr + 1) * 2 * HD)
            t = _head_sums(prod[:, cols], same_head)
            for w, c_ref in ((w0, c0), (w1, c1), (w2, c2)):
                c_ref[:, cols] = w[:, cols] * t
        for w, d_ref in ((w0, d0), (w1, d1), (w2, d2)):
            d_ref[...] = (w * do).astype(BF16)

    shp = (s, GROUP_W)
    return _rowwise(name, fn, [d_out] + list(os_) + list(lses), [],
                    [(shp, BF16)] * 3 + [(shp, F32)] * 3)


def attn_bwd(q, k, v, d_o, cterm, lse, pattern, name, tq=512):
    s = q.shape[0]
    d = ATTN_DILATIONS[pattern]
    blocks_per_stream = (s // d) // BLK
    nsb = tq // BLK
    n_blocks = s // BLK

    def body(q_ref, k_ref, v_ref, do_ref, c_ref, l_ref, kp_ref, vp_ref, qn_ref, don_ref, cn_ref, ln_ref,
             dq_ref, dk_ref, dv_ref):
        i = pl.program_id(0)
        rel_diag, rel_prev = _attn_masks()
        first = _lane_first_half((BLK, 2 * HD))
        second = jnp.logical_not(first)
        rd_f = (rel_diag * d).astype(F32)
        rp_f = (rel_prev * d).astype(F32)
        m_diag = rel_diag >= 0
        dq_ref[...] = jnp.zeros_like(dq_ref)
        dk_ref[...] = jnp.zeros_like(dk_ref)
        dv_ref[...] = jnp.zeros_like(dv_ref)

        def pair(qp, dop, cp, lp, kp, vp, rel_f, mask):
            dq = dk = dv = None
            for e in range(2):
                lanes = first if e == 0 else second
                slope = slopes[e]
                qm = jnp.where(lanes, qp, jnp.zeros_like(qp))
                dom = jnp.where(lanes, dop, jnp.zeros_like(dop))
                km = jnp.where(lanes, kp, jnp.zeros_like(kp))
                sc = jnp.where(mask, _dot_nt(qm, kp) * 0.125 - slope * rel_f, NEG)
                pm = jnp.exp(sc - lp[:, e * HD:e * HD + 1])
                dl = pm * (_dot_nt(dom, vp) - cp[:, e * HD:e * HD + 1])
                dl16 = dl.astype(BF16)
                t_dq = _dot(dl16, km)
                t_dk = _dot_tn(dl16, qm)
                t_dv = _dot_tn(pm.astype(BF16), dom)
                dq = t_dq if dq is None else dq + t_dq
                dk = t_dk if dk is None else dk + t_dk
                dv = t_dv if dv is None else dv + t_dv
            return dq * 0.125, dk * 0.125, dv

        for pr in range(GROUP_W // (2 * HD)):
            cols = slice(pr * 2 * HD, (pr + 1) * 2 * HD)
            slopes = [_alibi_slope(pattern * HEADS_PER_PATTERN + 2 * pr + e) for e in range(2)]
            for sb in range(nsb + 1):
                gb = i * nsb + sb
                if sb < nsb:
                    rows = slice(sb * BLK, (sb + 1) * BLK)
                    qp, dop, cp, lp = q_ref[rows, cols], do_ref[rows, cols], c_ref[rows, cols], l_ref[rows, cols]
                else:
                    qp, dop, cp, lp = qn_ref[:, cols], don_ref[:, cols], cn_ref[:, cols], ln_ref[:, cols]
                if sb < nsb:
                    dq1, dk1, dv1 = pair(qp, dop, cp, lp, k_ref[rows, cols], v_ref[rows, cols], rd_f, m_diag)
                    dq_ref[rows, cols] += dq1
                    dk_ref[rows, cols] += dk1
                    dv_ref[rows, cols] += dv1
                valid = jnp.logical_and(gb % blocks_per_stream != 0, gb < n_blocks).astype(jnp.int32)
                m_prev = jnp.logical_and(rel_prev <= BLK, (rel_prev + (1 - valid) * (4 * BLK)) <= BLK)
                if sb == 0:
                    kp, vp = kp_ref[:, cols], vp_ref[:, cols]
                else:
                    prows = slice((sb - 1) * BLK, sb * BLK)
                    kp, vp = k_ref[prows, cols], v_ref[prows, cols]
                dq0, dk0, dv0 = pair(qp, dop, cp, lp, kp, vp, rp_f, m_prev)
                if sb < nsb:
                    dq_ref[rows, cols] += dq0
                if sb > 0:
                    dk_ref[prows, cols] += dk0
                    dv_ref[prows, cols] += dv0

    cur = pl.BlockSpec((tq, GROUP_W), lambda i: (i, 0))
    prev = pl.BlockSpec((BLK, GROUP_W), lambda i: (jnp.maximum(i * nsb - 1, 0), 0))
    nxt = pl.BlockSpec((BLK, GROUP_W), lambda i: (jnp.minimum((i + 1) * nsb, n_blocks - 1), 0))
    shp = _sds((s, GROUP_W), F32)
    return _call(body, name=name, out_shape=[shp, shp, shp], grid=(s // tq,),
                 in_specs=[cur] * 6 + [prev, prev] + [nxt] * 4, out_specs=[cur, cur, cur])(
                     q, k, v, d_o, cterm, lse, k, v, q, d_o, cterm, lse)


def _band_constants(d):
    row = lax.broadcasted_iota(jnp.int32, (2 * BLK, 2 * BLK), 0)
    col = lax.broadcasted_iota(jnp.int32, (2 * BLK, 2 * BLK), 1)
    rel = BLK + jnp.where(row >= BLK, row - BLK, row) - col
    band = jnp.logical_and(rel >= 0, rel <= BLK)
    return (rel * d).astype(F32), band, (col >= BLK).astype(jnp.int32)


def _stack_heads(x, first):
    zero = jnp.zeros_like(x)
    return jnp.concatenate([jnp.where(first, x, zero), jnp.where(first, zero, x)], axis=0)


def _unstack_heads(x2, first):
    return jnp.where(first, x2[:BLK], x2[BLK:])


def _head_column(x):
    return jnp.concatenate([x[:, 0:1], x[:, HD:HD + 1]], axis=0)


def attn_fwd2(q, k, v, pattern, name, tq=512):
    s = q.shape[0]
    d = ATTN_DILATIONS[pattern]
    blocks_per_stream = (s // d) // BLK
    nsb = tq // BLK

    def body(q_ref, k_ref, v_ref, kp_ref, vp_ref, o_ref, l_ref):
        i = pl.program_id(0)
        rel_f, band, own = _band_constants(d)
        first = _lane_first_half((BLK, 2 * HD))
        upper = lax.broadcasted_iota(jnp.int32, (2 * BLK, 1), 0) < BLK
        for sb in range(nsb):
            rows = slice(sb * BLK, (sb + 1) * BLK)
            has_prev = ((i * nsb + sb) % blocks_per_stream != 0).astype(jnp.int32)
            mask = jnp.logical_and(band, (own + has_prev) > 0)
            for pr in range(GROUP_W // (2 * HD)):
                cols = slice(pr * 2 * HD, (pr + 1) * 2 * HD)
                if sb == 0:
                    kcat = jnp.concatenate([kp_ref[:, cols], k_ref[rows, cols]], axis=0)
                    vcat = jnp.concatenate([vp_ref[:, cols], v_ref[rows, cols]], axis=0)
                else:
                    both = slice((sb - 1) * BLK, (sb + 1) * BLK)
                    kcat, vcat = k_ref[both, cols], v_ref[both, cols]
                h0 = pattern * HEADS_PER_PATTERN + 2 * pr
                slope = jnp.where(upper, _alibi_slope(h0), _alibi_slope(h0 + 1))
                sc = _dot_nt(_stack_heads(q_ref[rows, cols], first), kcat) * 0.125 - slope * rel_f
                sc = jnp.where(mask, sc, NEG)
                m = jnp.max(sc, axis=-1, keepdims=True)
                p = jnp.exp(sc - m)
                l = jnp.sum(p, axis=-1, keepdims=True)
                o2 = _dot((p * (1.0 / l)).astype(BF16), vcat)
                o_ref[rows, cols] = _unstack_heads(o2, first)
                lse = m + jnp.log(l)
                l_ref[rows, cols] = jnp.where(first, lse[:BLK], lse[BLK:])

    cur = pl.BlockSpec((tq, GROUP_W), lambda i: (i, 0))
    prev = pl.BlockSpec((BLK, GROUP_W), lambda i: (jnp.maximum(i * nsb - 1, 0), 0))
    return _call(body, name=name, out_shape=[_sds((s, GROUP_W), F32), _sds((s, GROUP_W), F32)], grid=(s // tq,),
                 in_specs=[cur, cur, cur, prev, prev], out_specs=[cur, cur])(q, k, v, k, v)


def attn_bwd2(q, k, v, d_o, cterm, lse, pattern, name, tq=512):
    s = q.shape[0]
    d = ATTN_DILATIONS[pattern]
    blocks_per_stream = (s // d) // BLK
    nsb = tq // BLK
    n_blocks = s // BLK

    def body(q_ref, k_ref, v_ref, do_ref, c_ref, l_ref, kp_ref, vp_ref, qn_ref, kn_ref, vn_ref, don_ref, cn_ref,
             ln_ref, dq_ref, dk_ref, dv_ref):
        i = pl.program_id(0)
        rel_f, band, own = _band_constants(d)
        first = _lane_first_half((BLK, 2 * HD))
        upper = lax.broadcasted_iota(jnp.int32, (2 * BLK, 1), 0) < BLK
        dk_ref[...] = jnp.zeros_like(dk_ref)
        dv_ref[...] = jnp.zeros_like(dv_ref)
        for sb in range(nsb + 1):
            gb = i * nsb + sb
            rows = slice(sb * BLK, (sb + 1) * BLK)
            before = slice((sb - 1) * BLK, sb * BLK)
            inside = (gb < n_blocks).astype(jnp.int32)
            has_prev = jnp.logical_and(gb % blocks_per_stream != 0, gb < n_blocks).astype(jnp.int32)
            mask = jnp.logical_and(band, (own * inside + has_prev) > 0)
            for pr in range(GROUP_W // (2 * HD)):
                cols = slice(pr * 2 * HD, (pr + 1) * 2 * HD)
                if sb == 0:
                    kcat = jnp.concatenate([kp_ref[:, cols], k_ref[rows, cols]], axis=0)
                    vcat = jnp.concatenate([vp_ref[:, cols], v_ref[rows, cols]], axis=0)
                elif sb == nsb:
                    kcat = jnp.concatenate([k_ref[before, cols], kn_ref[:, cols]], axis=0)
                    vcat = jnp.concatenate([v_ref[before, cols], vn_ref[:, cols]], axis=0)
                else:
                    both = slice((sb - 1) * BLK, (sb + 1) * BLK)
                    kcat, vcat = k_ref[both, cols], v_ref[both, cols]
                if sb < nsb:
                    qp, dop, cp, lp = q_ref[rows, cols], do_ref[rows, cols], c_ref[rows, cols], l_ref[rows, cols]
                else:
                    qp, dop, cp, lp = qn_ref[:, cols], don_ref[:, cols], cn_ref[:, cols], ln_ref[:, cols]
                h0 = pattern * HEADS_PER_PATTERN + 2 * pr
                slope = jnp.where(upper, _alibi_slope(h0), _alibi_slope(h0 + 1))
                q2 = _stack_heads(qp, first)
                do2 = _stack_heads(dop, first)
                sc = jnp.where(mask, _dot_nt(q2, kcat) * 0.125 - slope * rel_f, NEG)
                pm = jnp.exp(sc - _head_column(lp))
                dl = (pm * (_dot_nt(do2, vcat) - _head_column(cp))).astype(BF16)
                if sb < nsb:
                    dq_ref[rows, cols] = _unstack_heads(_dot(dl, kcat), first) * 0.125
                dk2 = _dot_tn(dl, q2) * 0.125
                dv2 = _dot_tn(pm.astype(BF16), do2)
                if sb > 0:
                    dk_ref[before, cols] += dk2[:BLK]
                    dv_ref[before, cols] += dv2[:BLK]
                if sb < nsb:
                    dk_ref[rows, cols] += dk2[BLK:]
                    dv_ref[rows, cols] += dv2[BLK:]

    cur = pl.BlockSpec((tq, GROUP_W), lambda i: (i, 0))
    prev = pl.BlockSpec((BLK, GROUP_W), lambda i: (jnp.maximum(i * nsb - 1, 0), 0))
    nxt = pl.BlockSpec((BLK, GROUP_W), lambda i: (jnp.minimum((i + 1) * nsb, n_blocks - 1), 0))
    shp = _sds((s, GROUP_W), F32)
    return _call(body, name=name, out_shape=[shp, shp, shp], grid=(s // tq,),
                 in_specs=[cur] * 6 + [prev, prev] + [nxt] * 6, out_specs=[cur, cur, cur])(
                     q, k, v, d_o, cterm, lse, k, v, q, k, v, d_o, cterm, lse)


HALO = 16
CONV_TQ = 512


def conv_fwd(p, w, b, name):
    s = p.shape[0]
    tq = CONV_TQ
    ncol = SSD_CONV_DIM // GROUP_W
    cb0 = COL_XBC // GROUP_W

    def body(u_ref, up_ref, w_ref, b_ref, c_ref, xc_ref):
        i = pl.program_id(0)
        prev = up_ref[...].astype(F32) * (i > 0).astype(F32)
        ext = jnp.concatenate([prev, u_ref[...].astype(F32)], axis=0)
        acc = b_ref[...] + w_ref[SSD_CONV - 1:SSD_CONV, :] * ext[HALO:HALO + tq]
        for kk in range(SSD_CONV - 1):
            acc += w_ref[kk:kk + 1, :] * pltpu.roll(ext, SSD_CONV - 1 - kk, 0)[HALO:HALO + tq]
        c_ref[...] = acc.astype(BF16)
        xc_ref[...] = (acc * _sigmoid(acc)).astype(BF16)

    cur_in = pl.BlockSpec((tq, GROUP_W), lambda i, j: (i, cb0 + j))
    prev_in = pl.BlockSpec((HALO, GROUP_W), lambda i, j: (jnp.maximum(i * (tq // HALO) - 1, 0), cb0 + j))
    cur_out = pl.BlockSpec((tq, GROUP_W), lambda i, j: (i, j))
    shp = _sds((s, SSD_CONV_DIM), BF16)
    return _call(body, name=name, out_shape=[shp, shp], grid=(s // tq, ncol),
                 in_specs=[cur_in, prev_in, pl.BlockSpec((SSD_CONV, GROUP_W), lambda i, j: (0, j)),
                           pl.BlockSpec((1, GROUP_W), lambda i, j: (0, j))],
                 out_specs=[cur_out, cur_out])(p, p, w, b)


def conv_bwd(p, cpre, dxs, d_b, d_c, w, name):
    s = p.shape[0]
    tq = CONV_TQ
    ncol = SSD_CONV_DIM // GROUP_W
    n_xs = SSD_INNER // GROUP_W
    cb0 = COL_XBC // GROUP_W
    nt = s // tq

    def body(u_ref, c_ref, cn_ref, dx_ref, dxn_ref, dbm_ref, dbmn_ref, dcm_ref, dcmn_ref, w_ref,
             du_ref, dw_ref, db_ref):
        j, i = pl.program_id(0), pl.program_id(1)

        def dpre(c16, dx):
            c = c16.astype(F32)
            sg = _sigmoid(c)
            return dx * (sg * (1.0 + c * (1.0 - sg)))

        def pick(a_ref, b_ref, c_ref_):
            return jnp.where(j < n_xs, a_ref[...], jnp.where(j == n_xs, b_ref[...], c_ref_[...]))

        dc = dpre(c_ref[...], pick(dx_ref, dbm_ref, dcm_ref))
        dcn = dpre(cn_ref[...], pick(dxn_ref, dbmn_ref, dcmn_ref)) * (i < nt - 1).astype(F32)
        dext = jnp.concatenate([dc, dcn], axis=0)
        u = u_ref[...].astype(F32)

        @pl.when(i == 0)
        def _():
            dw_ref[...] = jnp.zeros_like(dw_ref)
            db_ref[...] = jnp.zeros_like(db_ref)

        du = w_ref[SSD_CONV - 1:SSD_CONV, :] * dc
        dw_ref[SSD_CONV - 1:SSD_CONV, :] += jnp.sum(dc * u, axis=0, keepdims=True)
        for kk in range(SSD_CONV - 1):
            sh = SSD_CONV - 1 - kk
            ahead = pltpu.roll(dext, tq + HALO - sh, 0)[0:tq]
            du += w_ref[kk:kk + 1, :] * ahead
            dw_ref[kk:kk + 1, :] += jnp.sum(ahead * u, axis=0, keepdims=True)
        du_ref[...] = du.astype(BF16)
        db_ref[...] += jnp.sum(dc, axis=0, keepdims=True)

    hb = tq // HALO
    cur_p = pl.BlockSpec((tq, GROUP_W), lambda j, i: (i, cb0 + j))
    cur = pl.BlockSpec((tq, GROUP_W), lambda j, i: (i, j))
    nxt = pl.BlockSpec((HALO, GROUP_W), lambda j, i: (jnp.minimum((i + 1) * hb, s // HALO - 1), j))

    def piece(first_tile, n_tiles):
        def on(j):
            return jnp.logical_and(j >= first_tile, j < first_tile + n_tiles)

        def col(j):
            return jnp.clip(j - first_tile, 0, n_tiles - 1)

        return (pl.BlockSpec((tq, GROUP_W), lambda j, i: (jnp.where(on(j), i, 0), col(j))),
                pl.BlockSpec((HALO, GROUP_W),
                             lambda j, i: (jnp.where(on(j), jnp.minimum((i + 1) * hb, s // HALO - 1), 0), col(j))))

    return _call(body, name=name,
                 out_shape=[_sds((s, SSD_CONV_DIM), BF16), _sds((8, SSD_CONV_DIM), F32), _sds((1, SSD_CONV_DIM), F32)],
                 grid=(ncol, nt),
                 in_specs=[cur_p, cur, nxt, *piece(0, n_xs), *piece(n_xs, 1), *piece(n_xs + 1, 1),
                           pl.BlockSpec((SSD_CONV, GROUP_W), lambda j, i: (0, j))],
                 out_specs=[cur, pl.BlockSpec((8, GROUP_W), lambda j, i: (0, j)),
                            pl.BlockSpec((1, GROUP_W), lambda j, i: (0, j))])(
                                p, cpre, cpre, dxs, dxs, d_b, d_b, d_c, d_c, w)


def _softplus(x):
    return jnp.maximum(x, 0.0) + jnp.log(1.0 + jnp.exp(-jnp.abs(x)))


def _ssd_decays(dtr_ref, dtrt_ref, bias_ref, biast_ref, alog_ref, alogt_ref):
    row = lax.broadcasted_iota(jnp.int32, (BLK, BLK), 0)
    col = lax.broadcasted_iota(jnp.int32, (BLK, BLK), 1)
    lower = (row >= col).astype(F32)
    upper = (row <= col).astype(F32)
    dtb = dtr_ref[...] + bias_ref[...]
    dt = _softplus(dtb)
    a = dt * (-jnp.exp(alog_ref[...]))
    cs = _dot_hi(lower, a)
    a_t = _softplus(dtrt_ref[...] + biast_ref[...]) * (-jnp.exp(alogt_ref[...]))
    cs_t = _dot_hi(a_t, upper)
    return dtb, dt, cs, cs_t, row, col, upper


SSD_GROUPS_PER_STEP = 4


def _per_group(body, gps, kinds):
    def wrapped(*refs):
        for gi in range(gps):
            args, pos = [], 0
            for kind, n in kinds:
                if kind == "each":
                    args.append(refs[pos + gi])
                    pos += gps
                    continue
                ref = refs[pos]
                pos += 1
                if kind == "cols":
                    args.append(ref.at[:, gi * n:(gi + 1) * n])
                else:
                    args.append(ref.at[gi] if n == 1 else ref.at[pl.ds(gi * n, n)])
            body(*args)

    return wrapped


def ssd_fwd(p, xc, dtg, dtg_t, params, gn, name):
    s = p.shape[0]
    nc = s // BLK
    bias, bias_t, alog, alog_t, dskip = params

    def body(xs_ref, b_ref, c_ref, z_ref, dtr_ref, dtrt_ref, bias_ref, biast_ref, alog_ref, alogt_ref, dsk_ref,
             gn_ref, y_ref, sin_ref, hp_ref, h_ref):
        c_idx = pl.program_id(1)

        @pl.when(c_idx == 0)
        def _():
            h_ref[...] = jnp.zeros_like(h_ref)

        _, dt, cs, cs_t, row, col, _ = _ssd_decays(dtr_ref, dtrt_ref, bias_ref, biast_ref, alog_ref, alogt_ref)
        first = _lane_first_half((BLK, 2 * HD))
        first_row = _lane_first_half((1, 2 * HD))
        tril = row >= col
        b16, c16 = b_ref[...], c_ref[...]
        cb = _dot_nt(c16, b16)
        n_pairs = GROUP_W // (2 * HD)
        tot = cs[BLK - 1:BLK, :]
        exp_cs, exp_rest, exp_tot = jnp.exp(cs), jnp.exp(tot - cs), jnp.exp(tot)

        lanes_of_head = (lax.broadcasted_iota(jnp.int32, (8, GROUP_W), 1) // HD
                         == lax.broadcasted_iota(jnp.int32, (8, GROUP_W), 0)).astype(BF16)

        def per_head(v, mask):
            if v.shape[0] == 1:
                return jnp.concatenate([jnp.where(mask, v[:, 2 * pr:2 * pr + 1], v[:, 2 * pr + 1:2 * pr + 2])
                                        for pr in range(n_pairs)], axis=1)
            hi = v.astype(BF16)
            lo = (v - hi.astype(F32)).astype(BF16)
            return _dot(hi, lanes_of_head) + _dot(lo, lanes_of_head)

        xs = xs_ref[...].astype(F32)
        xt = xs * per_head(dt, first)
        xt16 = xt.astype(BF16)
        hstate = jnp.concatenate([h_ref[pr] for pr in range(n_pairs)], axis=1)
        for pr in range(n_pairs):
            hp_ref[pr] = h_ref[pr]
        y_off = per_head(exp_cs, first) * _dot(c16, hstate.astype(BF16))
        new = per_head(exp_tot, first_row) * hstate + _dot_tn(b16, (per_head(exp_rest, first) * xt).astype(BF16))
        for pr in range(n_pairs):
            h_ref[pr] = new[:, pr * 2 * HD:(pr + 1) * 2 * HD]
        y_diag = []
        for pr in range(n_pairs):
            cols = slice(pr * 2 * HD, (pr + 1) * 2 * HD)
            m2 = jnp.concatenate(
                [(cb * jnp.exp(jnp.where(tril, cs[:, h:h + 1] - cs_t[h:h + 1, :], NEG))).astype(BF16)
                 for h in (2 * pr, 2 * pr + 1)], axis=1)
            y_diag.append(_dot(m2, _stack_heads(xt16[:, cols], first)))
        y = jnp.concatenate(y_diag, axis=1) + y_off + xs * per_head(dsk_ref[...], first_row)
        y_ref[...] = y
        zv = z_ref[...].astype(F32)
        yz = y * (zv * _sigmoid(zv))
        r = lax.rsqrt(jnp.mean(yz * yz, axis=-1, keepdims=True) + EPS)
        sin_ref[...] = (yz * r * gn_ref[...]).astype(BF16)

    gps = SSD_GROUPS_PER_STEP
    wide, narrow, lead = ("cols", GROUP_W), ("cols", BLK), ("lead", 1)
    kinds = [wide, narrow, narrow, ("each", 0)] + [lead] * 7 + [wide, wide, wide, lead, ("lead", 4)]
    wide_w, narrow_w = GROUP_W * gps, BLK * gps
    gparam = pl.BlockSpec((gps, 1, 8), lambda g, c: (g, 0, 0))
    gparam_t = pl.BlockSpec((gps, 8, 1), lambda g, c: (g, 0, 0))
    z_specs = [pl.BlockSpec((BLK, GROUP_W), functools.partial(lambda g, c, gi: (c, COL_Z // GROUP_W + gps * g + gi),
                                                              gi=gi)) for gi in range(gps)]
    return _call(
        _per_group(body, gps, kinds), name=name,
        out_shape=[_sds((s, SSD_INNER), F32), _sds((s, SSD_INNER), BF16),
                   _sds((SSD_GROUPS, nc, 4, BLK, 2 * HD), F32)],
        grid=(SSD_GROUPS // gps, nc),
        in_specs=[pl.BlockSpec((BLK, wide_w), lambda g, c: (c, g)),
                  pl.BlockSpec((BLK, narrow_w), lambda g, c: (c, SSD_INNER // narrow_w + g)),
                  pl.BlockSpec((BLK, narrow_w), lambda g, c: (c, (SSD_INNER + SSD_GROUPS * BLK) // narrow_w + g)),
                  *z_specs,
                  pl.BlockSpec((gps, BLK, 8), lambda g, c: (g, c, 0)),
                  pl.BlockSpec((gps, 8, BLK), lambda g, c: (g, 0, c)),
                  gparam, gparam_t, gparam, gparam_t, gparam,
                  pl.BlockSpec((1, wide_w), lambda g, c: (0, g))],
        out_specs=[pl.BlockSpec((BLK, wide_w), lambda g, c: (c, g)),
                   pl.BlockSpec((BLK, wide_w), lambda g, c: (c, g)),
                   pl.BlockSpec((gps, None, 4, BLK, 2 * HD), lambda g, c: (g, c, 0, 0, 0))],
        scratch_shapes=[pltpu.VMEM((4 * gps, BLK, 2 * HD), F32)],
    )(xc, xc, xc, *([p] * gps), dtg, dtg_t, bias, bias_t, alog, alog_t, dskip, gn)


def ssd_bwd(p, xc, y, d_sin, hprev, dtg, dtg_t, params, gn, name):
    s = p.shape[0]
    nc = s // BLK
    bias, bias_t, alog, alog_t, dskip = params

    def body(xs_ref, b_ref, c_ref, z_ref, y_ref, dsin_ref, hp_ref, dtr_ref, dtrt_ref, bias_ref,
             biast_ref, alog_ref, alogt_ref, dsk_ref, gn_ref,
             dxs_ref, db_ref, dc_ref, dz_ref, ddt_ref, da_ref, dbias_ref, ddsk_ref, dgn_ref, dh_ref):
        c_idx = pl.program_id(1)

        @pl.when(c_idx == 0)
        def _():
            dh_ref[...] = jnp.zeros_like(dh_ref)
            da_ref[...] = jnp.zeros_like(da_ref)
            dbias_ref[...] = jnp.zeros_like(dbias_ref)
            ddsk_ref[...] = jnp.zeros_like(ddsk_ref)
            dgn_ref[...] = jnp.zeros_like(dgn_ref)

        dtb, dt, cs, cs_t, row, col, upper = _ssd_decays(dtr_ref, dtrt_ref, bias_ref, biast_ref, alog_ref, alogt_ref)
        first = _lane_first_half((BLK, 2 * HD))
        second = jnp.logical_not(first)
        first_row = _lane_first_half((1, 2 * HD))
        tril = row >= col
        triu = row <= col
        last_row = lax.broadcasted_iota(jnp.int32, (BLK, 1), 0) == BLK - 1
        lane8 = lax.broadcasted_iota(jnp.int32, (BLK, 8), 1)

        yv = y_ref[...]
        zv = z_ref[...].astype(F32)
        sg = _sigmoid(zv)
        yz = yv * (zv * sg)
        r = lax.rsqrt(jnp.mean(yz * yz, axis=-1, keepdims=True) + EPS)
        yzn = yz * r
        dsn = dsin_ref[...]
        dgn_ref[...] += jnp.sum(dsn * yzn, axis=0, keepdims=True)
        dsn = dsn * gn_ref[...]
        dyz = r * (dsn - yzn * jnp.mean(dsn * yzn, axis=-1, keepdims=True))
        dy = dyz * (zv * sg)
        dz_ref[...] = (dyz * yv * (sg * (1.0 + zv * (1.0 - sg)))).astype(BF16)
        xs_all = xs_ref[...].astype(F32)
        ddsk_ref[...] += jnp.sum(dy * xs_all, axis=0, keepdims=True)

        b16, c16 = b_ref[...], c_ref[...]
        cb = _dot_nt(c16, b16)
        cb_t = _dot_nt(b16, c16)
        n_pairs = GROUP_W // (2 * HD)
        tot = cs[BLK - 1:BLK, :]
        exp_cs, exp_rest, exp_tot = jnp.exp(cs), jnp.exp(tot - cs), jnp.exp(tot)

        lanes_of_head = (lax.broadcasted_iota(jnp.int32, (8, GROUP_W), 1) // HD
                         == lax.broadcasted_iota(jnp.int32, (8, GROUP_W), 0)).astype(BF16)

        def per_head(v, mask):
            if v.shape[0] == 1:
                return jnp.concatenate([jnp.where(mask, v[:, 2 * pr:2 * pr + 1], v[:, 2 * pr + 1:2 * pr + 2])
                                        for pr in range(n_pairs)], axis=1)
            hi = v.astype(BF16)
            lo = (v - hi.astype(F32)).astype(BF16)
            return _dot(hi, lanes_of_head) + _dot(lo, lanes_of_head)

        head_of_lane = (lax.broadcasted_iota(jnp.int32, (GROUP_W, 8), 0) // HD
                        == lax.broadcasted_iota(jnp.int32, (GROUP_W, 8), 1)).astype(BF16)

        def head_sums(v):
            hi = v.astype(BF16)
            lo = (v - hi.astype(F32)).astype(BF16)
            return _dot(hi, head_of_lane) + _dot(lo, head_of_lane)

        dt_w, e_w, f_w = per_head(dt, first), per_head(exp_cs, first), per_head(exp_rest, first)
        xt = xs_all * dt_w
        xt16 = xt.astype(BF16)
        hstate = jnp.concatenate([hp_ref[pr] for pr in range(n_pairs)], axis=1)
        h16 = hstate.astype(BF16)
        dhn = jnp.concatenate([dh_ref[pr] for pr in range(n_pairs)], axis=1)
        dhn16 = dhn.astype(BF16)
        edy16 = (e_w * dy).astype(BF16)
        y_off = e_w * _dot(c16, h16)
        dcs_all = head_sums(dy * y_off)
        dc_acc = _dot_nt(edy16, h16)
        zmat = _dot(b16, dhn16)
        t_all = head_sums(zmat * xt) * exp_rest
        hh_rows = jnp.sum(head_sums(dhn * hstate), axis=0, keepdims=True)
        dtot = jnp.sum(t_all, axis=0, keepdims=True) + hh_rows * exp_tot
        dcs_all = dcs_all - t_all + jnp.where(last_row, dtot, 0.0)
        fxt16 = (f_w * xt).astype(BF16)
        db_acc = _dot_nt(fxt16, dhn16)
        dh_new = _dot_tn(c16, edy16) + per_head(exp_tot, first_row) * dhn
        for pr in range(n_pairs):
            dh_ref[pr] = dh_new[:, pr * 2 * HD:(pr + 1) * 2 * HD]
        g_sum = jnp.zeros((BLK, BLK), F32)
        gt_sum = jnp.zeros((BLK, BLK), F32)
        d_xt_parts = []
        for pr in range(n_pairs):
            cols = slice(pr * 2 * HD, (pr + 1) * 2 * HD)
            dym2 = _stack_heads(dy[:, cols].astype(BF16), first)
            d_m2 = _dot_nt(dym2, xt16[:, cols])
            d_mt2 = _dot_nt(xt16[:, cols], dym2)
            mt2 = []
            for e, h in enumerate((2 * pr, 2 * pr + 1)):
                cs_c, cs_r = cs[:, h:h + 1], cs_t[h:h + 1, :]
                decay = jnp.exp(jnp.where(tril, cs_c - cs_r, NEG))
                decay_t = jnp.exp(jnp.where(triu, cs_r - cs_c, NEG))
                gm = d_m2[e * BLK:(e + 1) * BLK] * decay
                gmt = d_mt2[:, e * BLK:(e + 1) * BLK] * decay_t
                g_sum += gm
                gt_sum += gmt
                dcs_h = jnp.sum(gm * cb, axis=-1, keepdims=True) - jnp.sum(gmt * cb_t, axis=-1, keepdims=True)
                dcs_all = dcs_all + jnp.where(lane8 == h, dcs_h, 0.0)
                mt2.append((cb_t * decay_t).astype(BF16))
            d_xt_parts.append(_dot(jnp.concatenate(mt2, axis=1), dym2))
        d_xt = jnp.concatenate(d_xt_parts, axis=1) + f_w * zmat
        dxs_ref[...] = dy * per_head(dsk_ref[...], first_row) + d_xt * dt_w
        ddtx_all = head_sums(d_xt * xs_all)

        dc_ref[...] = dc_acc + _dot(g_sum.astype(BF16), b16)
        db_ref[...] = db_acc + _dot(gt_sum.astype(BF16), c16)
        d_a = _dot_hi(upper, dcs_all)
        a_neg = -jnp.exp(alog_ref[...])
        ddt = ddtx_all + d_a * a_neg
        da_ref[...] += jnp.sum(d_a * dt, axis=0, keepdims=True)
        ddtr = ddt * _sigmoid(dtb)
        ddt_ref[...] = ddtr
        dbias_ref[...] += jnp.sum(ddtr, axis=0, keepdims=True)

    gps = SSD_GROUPS_PER_STEP
    k_wide, k_narrow, k_lead = ("cols", GROUP_W), ("cols", BLK), ("lead", 1)
    kinds = ([k_wide, k_narrow, k_narrow, ("each", 0), k_wide, k_wide] + [k_lead] * 8 + [k_wide]
             + [k_wide, k_narrow, k_narrow, k_wide] + [k_lead] * 4 + [k_wide] + [("lead", 4)])
    wide_w, narrow_w = GROUP_W * gps, BLK * gps
    rc = lambda c: nc - 1 - c
    gparam = pl.BlockSpec((gps, 1, 8), lambda g, c: (g, 0, 0))
    gparam_t = pl.BlockSpec((gps, 8, 1), lambda g, c: (g, 0, 0))
    wide = pl.BlockSpec((BLK, wide_w), lambda g, c: (rc(c), g))
    narrow = pl.BlockSpec((BLK, narrow_w), lambda g, c: (rc(c), g))
    z_specs = [pl.BlockSpec((BLK, GROUP_W),
                            functools.partial(lambda g, c, gi: (rc(c), COL_Z // GROUP_W + gps * g + gi), gi=gi))
               for gi in range(gps)]
    return _call(
        _per_group(body, gps, kinds), name=name,
        out_shape=[_sds((s, SSD_INNER), F32), _sds((s, GROUP_W), F32), _sds((s, GROUP_W), F32),
                   _sds((s, SSD_INNER), BF16), _sds((SSD_GROUPS, s, 8), F32),
                   _sds((SSD_GROUPS, 1, 8), F32), _sds((SSD_GROUPS, 1, 8), F32),
                   _sds((SSD_GROUPS, 1, GROUP_W), F32), _sds((1, SSD_INNER), F32)],
        grid=(SSD_GROUPS // gps, nc),
        in_specs=[wide,
                  pl.BlockSpec((BLK, narrow_w), lambda g, c: (rc(c), SSD_INNER // narrow_w + g)),
                  pl.BlockSpec((BLK, narrow_w), lambda g, c: (rc(c), (SSD_INNER + SSD_GROUPS * BLK) // narrow_w + g)),
                  *z_specs,
                  wide, wide,
                  pl.BlockSpec((gps, None, 4, BLK, 2 * HD), lambda g, c: (g, rc(c), 0, 0, 0)),
                  pl.BlockSpec((gps, BLK, 8), lambda g, c: (g, rc(c), 0)),
                  pl.BlockSpec((gps, 8, BLK), lambda g, c: (g, 0, rc(c))),
                  gparam, gparam_t, gparam, gparam_t, gparam,
                  pl.BlockSpec((1, wide_w), lambda g, c: (0, g))],
        out_specs=[wide, narrow, narrow, wide,
                   pl.BlockSpec((gps, BLK, 8), lambda g, c: (g, rc(c), 0)),
                   gparam, gparam,
                   pl.BlockSpec((gps, 1, GROUP_W), lambda g, c: (g, 0, 0)),
                   pl.BlockSpec((1, wide_w), lambda g, c: (0, g))],
        scratch_shapes=[pltpu.VMEM((4 * gps, BLK, 2 * HD), F32)],
    )(xc, xc, xc, *([p] * gps), y, d_sin, hprev, dtg, dtg_t, bias, bias_t, alog, alog_t, dskip, gn)


def merge_fwd(p, a, sbr, name, tm=512):
    s = p.shape[0]
    nj = D_MODEL // GROUP_W

    def body(ga_ref, gs_ref, a_ref, s_ref, o_ref):
        o_ref[...] = (_sigmoid(ga_ref[...].astype(F32)) * a_ref[...]
                      + _sigmoid(gs_ref[...].astype(F32)) * s_ref[...]).astype(BF16)

    blk = pl.BlockSpec((tm, GROUP_W), lambda i, j: (i, j))
    return _call(body, name=name, out_shape=_sds((s, D_MODEL), BF16), grid=(s // tm, nj),
                 in_specs=[pl.BlockSpec((tm, GROUP_W), lambda i, j: (i, COL_GA // GROUP_W + j)),
                           pl.BlockSpec((tm, GROUP_W), lambda i, j: (i, COL_GS // GROUP_W + j)), blk, blk],
                 out_specs=blk)(p, p, a, sbr)


def merge_bwd(p, a, sbr, dmerged, name, tm=512):
    s = p.shape[0]
    nj = D_MODEL // GROUP_W

    def body(ga_ref, gs_ref, a_ref, s_ref, dm_ref, da_ref, ds_ref, dga_ref, dgs_ref):
        dm = dm_ref[...]
        sa = _sigmoid(ga_ref[...].astype(F32))
        ss = _sigmoid(gs_ref[...].astype(F32))
        da_ref[...] = (dm * sa).astype(BF16)
        ds_ref[...] = (dm * ss).astype(BF16)
        dga_ref[...] = (dm * a_ref[...] * sa * (1.0 - sa)).astype(BF16)
        dgs_ref[...] = (dm * s_ref[...] * ss * (1.0 - ss)).astype(BF16)

    blk = pl.BlockSpec((tm, GROUP_W), lambda i, j: (i, j))
    shp = _sds((s, D_MODEL), BF16)
    return _call(body, name=name, out_shape=[shp] * 4, grid=(s // tm, nj),
                 in_specs=[pl.BlockSpec((tm, GROUP_W), lambda i, j: (i, COL_GA // GROUP_W + j)),
                           pl.BlockSpec((tm, GROUP_W), lambda i, j: (i, COL_GS // GROUP_W + j)), blk, blk, blk],
                 out_specs=[blk] * 4)(p, p, a, sbr, dmerged)


def _group_major(v):
    return v.reshape(SSD_GROUPS, 1, 8), v.reshape(SSD_GROUPS, 8, 1)


def mixer_forward(x, w, rider=None, later_weights=None):
    s = x.shape[0]
    h = rms_fwd(x, w["mix_norm"], "mix_rms")
    p = matmul_nn(h, w["w_in_main"], "mix_proj", BF16, tm=1024, tn=512, rider=rider)
    rode = None
    if rider is not None:
        p, rode = p
        w = dict(w, **later_weights(rode))
    dt_raw = matmul_nn(h, w["w_in_dt"], "mix_proj_dt", F32, tm=1024, tn=DT_PAD)
    qn, kn = qk_norm_fwd(p, w["q_gain"], w["k_gain"], "qk_norm")
    streams, os_, lses = [], [], []
    for g, d in enumerate(ATTN_DILATIONS):
        cols = slice(g * GROUP_W, (g + 1) * GROUP_W)
        qs, ks = _to_streams(qn[:, cols], d), _to_streams(kn[:, cols], d)
        vs = _to_streams(p[:, COL_V + g * GROUP_W:COL_V + (g + 1) * GROUP_W], d)
        o, lse = attn_fwd2(qs, ks, vs, g, f"attn_fwd{g}")
        streams.append((qs, ks, vs, lse))
        os_.append(_from_streams(o, d))
        lses.append(_from_streams(lse, d))
    attn_o = attn_merge_fwd(os_, lses, "attn_merge")
    cpre, xc = conv_fwd(p, w["conv_w"], w["conv_b"], "conv_fwd")
    dtg = dt_raw[:, :SSD_HEADS].reshape(s, SSD_GROUPS, 8).transpose(1, 0, 2)
    dtg_t = dtg.transpose(0, 2, 1)
    params = (*_group_major(w["dt_bias"]), *_group_major(w["a_log"]), _group_major(w["d_skip"])[0])
    y, s_in, hprev = ssd_fwd(p, xc, dtg, dtg_t, params, w["ssd_norm"], "ssd_fwd")
    a = matmul_nn(attn_o, w["w_attn_branch"], "attn_branch", F32, tm=1024, tn=512)
    sbr = matmul_nn(s_in, w["w_ssd_branch"], "ssd_branch", F32, tm=1024, tn=512)
    merged = merge_fwd(p, a, sbr, "merge")
    x_out = matmul_nn(merged, w["w_out"], "mix_out", F32, tm=1024, tn=512, res=x)
    saved = dict(h=h, p=p, streams=streams, os=os_, lses=lses, attn_o=attn_o, cpre=cpre, xc=xc, dtg=dtg,
                 dtg_t=dtg_t, params=params, y=y, s_in=s_in, hprev=hprev, a=a, sbr=sbr, merged=merged, w=w)
    return x_out, saved, rode


def mixer_backward(dx_out, x, sv, ride_early=None, ride_late=None):
    s = x.shape[0]
    p = sv["p"]
    w = sv["w"]
    g = {}
    dmerged = matmul_nt(dx_out, w["w_out"], "d_merged", F32, tm=1024, tn=512, tk=1024)
    g["w_out"] = matmul_tn(sv["merged"], dx_out, "dw_out", tn=512, ts=1024)
    da, ds, dga, dgs = merge_bwd(p, sv["a"], sv["sbr"], dmerged, "merge_bwd")
    g["w_attn_branch"] = matmul_tn(sv["attn_o"], da, "dw_attn_branch", tn=512, ts=1024)
    g["w_ssd_branch"] = matmul_tn(sv["s_in"], ds, "dw_ssd_branch", tn=512, ts=1024)
    d_attn_o = matmul_nt(da, w["w_attn_branch"], "d_attn_o", F32, tm=1024, tn=512, tk=1024)
    d_sin = matmul_nt(ds, w["w_ssd_branch"], "d_ssd_in", F32, tm=1024, tn=512, tk=1024)
    dxs, d_b, d_c, dz, ddt, d_asum, d_bias, d_dsk, d_gn = ssd_bwd(
        p, sv["xc"], sv["y"], d_sin, sv["hprev"], sv["dtg"], sv["dtg_t"], sv["params"], w["ssd_norm"], "ssd_bwd")
    dxbc, d_convw, d_convb = conv_bwd(p, sv["cpre"], dxs, d_b, d_c, w["conv_w"], "conv_bwd")
    g["conv_w"] = d_convw[:SSD_CONV]
    g["conv_b"] = d_convb
    g["dt_bias"] = d_bias.reshape(1, SSD_HEADS)
    g["a_log"] = (d_asum * (-jnp.exp(sv["params"][2]))).reshape(1, SSD_HEADS)
    g["d_skip"] = jnp.sum(d_dsk.reshape(SSD_HEADS, HD), axis=1).reshape(1, SSD_HEADS)
    g["ssd_norm"] = d_gn
    merged_bwd = attn_merge_bwd(d_attn_o, sv["os"], sv["lses"], "attn_merge_bwd")
    dqs, dks, dvs = [], [], []
    for gi, d in enumerate(ATTN_DILATIONS):
        qs, ks, vs, lse = sv["streams"][gi]
        d_o = _to_streams(merged_bwd[gi], d)
        cterm = _to_streams(merged_bwd[3 + gi], d)
        dq, dk, dv = attn_bwd2(qs, ks, vs, d_o, cterm, lse, gi, f"attn_bwd{gi}")
        dqs.append(_from_streams(dq, d))
        dks.append(_from_streams(dk, d))
        dvs.append(_from_streams(dv, d).astype(BF16))
    dqk, d_qg, d_kg = qk_norm_bwd(p, dqs, dks, w["q_gain"], w["k_gain"], "qk_norm_bwd")
    g["q_norm"] = jnp.sum(d_qg.reshape(N_ATTN_HEADS, HD), axis=0).reshape(1, HD)
    g["k_norm"] = jnp.sum(d_kg.reshape(N_ATTN_HEADS, HD), axis=0).reshape(1, HD)
    dp = [dqk, jnp.concatenate(dvs, axis=1), dz, dxbc, dga, dgs]
    ddt_pad = jnp.pad(ddt.transpose(1, 0, 2).reshape(s, SSD_HEADS), ((0, 0), (0, DT_PAD - SSD_HEADS)))
    if ride_early is not None:
        g["w_in_main"], g["rode_early"] = matmul_tn_pieces(sv["h"], dp, "dw_in", tn=512, ts=1024,
                                                           rider=ride_early(g))
    else:
        g["w_in_main"] = matmul_tn_pieces(sv["h"], dp, "dw_in", tn=512, ts=1024)
    g["w_in_dt"] = matmul_tn(sv["h"], ddt_pad, "dw_in_dt", tn=DT_PAD, ts=1024)
    if ride_late is not None:
        dh_main, g["rode_late"] = matmul_nt_pieces(dp, w["w_in_main"], "d_mix_h", F32, tm=1024, tn=1024, tk=512,
                                                   rider=ride_late(g))
    else:
        dh_main = matmul_nt_pieces(dp, w["w_in_main"], "d_mix_h", F32, tm=1024, tn=512, tk=512)
    dh_dt = matmul_nt(ddt_pad, w["w_in_dt"], "d_mix_h_dt", F32, tm=1024, tn=1024, tk=DT_PAD)
    dx, g["mix_norm"] = rms_bwd([dh_main, dh_dt], x, w["mix_norm"], dx_out, "mix_drms")
    return dx, g


ANY = pl.BlockSpec(memory_space=pl.ANY)


def _place():
    x, y, c = lax.axis_index("x"), lax.axis_index("y"), lax.axis_index("c")
    chips = [(1 - x, y), (x, 1 - y), (1 - x, 1 - y)]
    return x, y, c, 2 * x + y, chips


def _comm_call(body, *, name, out_shape, n_in, scratch_shapes, aliases=None):
    return pl.pallas_call(
        body, out_shape=out_shape, in_specs=[ANY] * n_in, out_specs=[ANY] * len(out_shape),
        scratch_shapes=scratch_shapes, input_output_aliases=aliases or {}, name=name,
        compiler_params=pltpu.CompilerParams(has_side_effects=True))


def gather_weights(shards, small):
    n = len(shards)
    halves = [a.shape[0] // 2 for a in shards]
    out_shape = [_sds((N_CHIP,) + a.shape, a.dtype) for a in shards] + [_sds((N_CHIP,) + small.shape, small.dtype)]

    def body(*refs):
        ins, outs = refs[:n + 1], refs[n + 1:2 * n + 2]
        send1, recv1, send2, recv2, local = refs[2 * n + 2:]
        x, y, c, me, chips = _place()
        sibling = (x, y, 1 - c)

        def rows(k, chip, core):
            if k == n:
                return outs[k].at[chip]
            return outs[k].at[chip, pl.ds(core * halves[k], halves[k])]

        def level1(k, t, incoming):
            chip = 2 * chips[t][0] + chips[t][1]
            src = ins[k] if k == n else ins[k].at[pl.ds(c * halves[k], halves[k])]
            return pltpu.make_async_remote_copy(
                src_ref=src, dst_ref=rows(k, chip if incoming else me, c), send_sem=send1.at[3 * k + t],
                recv_sem=recv1.at[3 * k + t], device_id=(*chips[t], c), device_id_type=MESH)

        def level2(k, t, incoming):
            chip = 2 * chips[t][0] + chips[t][1]
            core = (1 - c) if incoming else c
            return pltpu.make_async_remote_copy(
                src_ref=rows(k, chip, core), dst_ref=rows(k, chip, core), send_sem=send2.at[3 * k + t],
                recv_sem=recv2.at[3 * k + t], device_id=sibling, device_id_type=MESH)

        own = [pltpu.make_async_copy(ins[k], outs[k].at[me], local.at[k]) for k in range(n + 1)]
        for cp in own:
            cp.start()
        first = [level1(k, t, False) for k in range(n + 1) for t in range(3)]
        for cp in first:
            cp.start()
        passed = []
        for k in range(n + 1):
            for t in range(3):
                level1(k, t, True).wait_recv()
                if k < n:
                    cp = level2(k, t, False)
                    cp.start()
                    passed.append(cp)
        for k in range(n):
            for t in range(3):
                level2(k, t, True).wait_recv()
        for cp in first + passed:
            cp.wait_send()
        for cp in own:
            cp.wait()

    dma = pltpu.SemaphoreType.DMA
    return _comm_call(body, name="gather_weights", out_shape=out_shape, n_in=n + 1,
                      scratch_shapes=[dma((3 * n + 3,)), dma((3 * n + 3,)), dma((3 * n,)), dma((3 * n,)),
                                      dma((n + 1,))])(*shards, small)


def reduce_to_sibling(grads):
    n = len(grads)
    halves = [a.shape[1] // 2 for a in grads]
    shapes = [_sds((N_CHIP, h, a.shape[2]), a.dtype) for a, h in zip(grads, halves)]

    def body(*refs):
        ins, got, kept = refs[:n], refs[n:2 * n], refs[2 * n:3 * n]
        send, recv, local = refs[3 * n:]
        x, y, c, _, _ = _place()
        copies, locals_ = [], []
        for k in range(n):
            h = halves[k]
            locals_.append(pltpu.make_async_copy(ins[k].at[:, pl.ds(c * h, h)], kept[k], local.at[k]))
            copies.append(pltpu.make_async_remote_copy(
                src_ref=ins[k].at[:, pl.ds((1 - c) * h, h)], dst_ref=got[k], send_sem=send.at[k], recv_sem=recv.at[k],
                device_id=(x, y, 1 - c), device_id_type=MESH))
        for cp in locals_ + copies:
            cp.start()
        for cp in copies:
            cp.wait_recv()
        for cp in copies:
            cp.wait_send()
        for cp in locals_:
            cp.wait()

    dma = pltpu.SemaphoreType.DMA
    res = _comm_call(body, name="reduce_to_sibling", out_shape=shapes + shapes, n_in=n,
                     scratch_shapes=[dma((n,)), dma((n,)), dma((n,))])(*grads)
    return res[:n], res[n:]


def reduce_to_owner(sums):
    n = len(sums)
    shapes = [_sds(a.shape, a.dtype) for a in sums]

    def body(*refs):
        ins, outs = refs[:n], refs[n:2 * n]
        send, recv, local = refs[2 * n:]
        x, y, c, me, chips = _place()
        copies, locals_ = [], []
        for k in range(n):
            locals_.append(pltpu.make_async_copy(ins[k].at[me], outs[k].at[3], local.at[k]))
            for t in range(3):
                chip = 2 * chips[t][0] + chips[t][1]
                copies.append(pltpu.make_async_remote_copy(
                    src_ref=ins[k].at[chip], dst_ref=outs[k].at[t], send_sem=send.at[3 * k + t],
                    recv_sem=recv.at[3 * k + t], device_id=(*chips[t], c), device_id_type=MESH))
        for cp in locals_ + copies:
            cp.start()
        for cp in copies:
            cp.wait_recv()
        for cp in copies:
            cp.wait_send()
        for cp in locals_:
            cp.wait()

    dma = pltpu.SemaphoreType.DMA
    return _comm_call(body, name="reduce_to_owner", out_shape=shapes, n_in=n,
                      scratch_shapes=[dma((3 * n,)), dma((3 * n,)), dma((n,))])(*sums)


def share_with_sibling(halves_):
    n = len(halves_)
    shapes = [_sds((2 * a.shape[0], a.shape[1]), a.dtype) for a in halves_]

    def body(*refs):
        ins, outs = refs[:n], refs[n:2 * n]
        send, recv, local = refs[2 * n:]
        x, y, c, _, _ = _place()
        copies, locals_ = [], []
        for k in range(n):
            h = ins[k].shape[0]
            mine = outs[k].at[pl.ds(c * h, h)]
            locals_.append(pltpu.make_async_copy(ins[k], mine, local.at[k]))
            copies.append(pltpu.make_async_remote_copy(
                src_ref=ins[k], dst_ref=mine, send_sem=send.at[k], recv_sem=recv.at[k],
                device_id=(x, y, 1 - c), device_id_type=MESH))
        for cp in locals_ + copies:
            cp.start()
        for cp in copies:
            cp.wait_recv()
        for cp in copies:
            cp.wait_send()
        for cp in locals_:
            cp.wait()

    dma = pltpu.SemaphoreType.DMA
    return _comm_call(body, name="share_with_sibling", out_shape=shapes, n_in=n,
                      scratch_shapes=[dma((n,)), dma((n,)), dma((n,))])(*halves_)


def _cores():
    c = lax.axis_index("c")
    return jnp.stack([c, 1 - c]).astype(jnp.int32)


def _staged_call(body, *, name, grid, in_specs, out_specs, out_shape, scratch_shapes):
    return pl.pallas_call(
        body, out_shape=out_shape, name=name,
        grid_spec=pltpu.PrefetchScalarGridSpec(num_scalar_prefetch=1, grid=grid, in_specs=in_specs,
                                               out_specs=out_specs, scratch_shapes=scratch_shapes),
        compiler_params=pltpu.CompilerParams(dimension_semantics=("arbitrary",) * len(grid),
                                             vmem_limit_bytes=V7X_VMEM_LIMIT, has_side_effects=True))


def gather_rider(shards, tiles):
    dma = pltpu.SemaphoreType.DMA
    n = len(shards)
    geo = [(a.shape[0] // 2, tm, (a.shape[0] // 2) // tm) for a, tm in zip(shards, tiles)]
    scratch = []
    for a, (h, tm, nk) in zip(shards, geo):
        scratch += [pltpu.VMEM((N_CHIP,) + a.shape, a.dtype), dma((3, nk)), dma((3, nk)), dma((3, nk)), dma((3, nk)),
                    dma((nk + 2,))]

    def copies(j, in_ref, scr):
        buf, send1, recv1, send2, recv2, local = scr[6 * j:6 * j + 6]
        h, tm, nk = geo[j]
        x, y, c, me, chips = _place()
        chip_of = [2 * chips[t][0] + chips[t][1] for t in range(3)]

        def rows(chip, core, k):
            return buf.at[chip, pl.ds(core * h + k * tm, tm)]

        def mine(k):
            if k == nk:
                return pltpu.make_async_copy(in_ref.at[pl.ds((1 - c) * h, h)], buf.at[me, pl.ds((1 - c) * h, h)],
                                             local.at[nk])
            return pltpu.make_async_copy(in_ref.at[pl.ds(c * h + k * tm, tm)], rows(me, c, k), local.at[k])

        def level1(t, k, incoming):
            place = rows(chip_of[t] if incoming else me, c, k)
            return pltpu.make_async_remote_copy(src_ref=place, dst_ref=place, send_sem=send1.at[t, k],
                                                recv_sem=recv1.at[t, k], device_id=(*chips[t], c), device_id_type=MESH)

        def level2(t, k, incoming):
            place = rows(chip_of[t], (1 - c) if incoming else c, k)
            return pltpu.make_async_remote_copy(src_ref=place, dst_ref=place, send_sem=send2.at[t, k],
                                                recv_sem=recv2.at[t, k], device_id=(x, y, 1 - c),
                                                device_id_type=MESH)

        return buf, local, nk, mine, level1, level2

    def start(ins, outs, scr):
        for j in range(n):
            _, _, nk, mine, _, _ = copies(j, ins[j], scr)
            for k in range(nk + 1):
                mine(k).start()
        for j in range(n):
            _, _, nk, mine, level1, _ = copies(j, ins[j], scr)
            for k in range(nk):
                mine(k).wait()
                for t in range(3):
                    level1(t, k, False).start()

    def finish(ins, outs, scr):
        for j in range(n):
            _, _, nk, _, level1, level2 = copies(j, ins[j], scr)
            for k in range(nk):
                for t in range(3):
                    level1(t, k, True).wait_recv()
                    level2(t, k, False).start()
        for j in range(n):
            buf, local, nk, mine, level1, level2 = copies(j, ins[j], scr)
            for k in range(nk):
                for t in range(3):
                    level2(t, k, True).wait_recv()
            for k in range(nk):
                for t in range(3):
                    level1(t, k, False).wait_send()
                    level2(t, k, False).wait_send()
            mine(nk).wait()
            pltpu.make_async_copy(buf, outs[j], local.at[nk + 1]).start()
        for j in range(n):
            buf, local, nk, _, _, _ = copies(j, ins[j], scr)
            pltpu.make_async_copy(buf, outs[j], local.at[nk + 1]).wait()

    return Rider(list(shards), [_sds((N_CHIP,) + a.shape, a.dtype) for a in shards], scratch, start, finish)


def run_alone(rider, name):
    return _call(lambda: None, name=name, out_shape=[], in_specs=[], out_specs=[], grid=(1,), rider=rider)()[1]


def sibling_sum(g, tm, name):
    _, r, cdim = g.shape
    h = r // 2
    ni = h // tm
    dma = pltpu.SemaphoreType.DMA

    def body(cores_ref, keep_ref, give_ref, out_ref, slot, send, recv):
        par = (pl.program_id(0) * ni + pl.program_id(1)) % 2
        x, y, c, _, _ = _place()
        cp = pltpu.make_async_remote_copy(src_ref=give_ref, dst_ref=slot.at[par], send_sem=send.at[par],
                                          recv_sem=recv.at[par], device_id=(x, y, 1 - c), device_id_type=MESH)
        cp.start()
        cp.wait_recv()
        out_ref[...] = (keep_ref[...].astype(F32) + slot[par].astype(F32)).astype(out_ref.dtype)
        cp.wait_send()

    flat = g.reshape(N_CHIP * r, cdim)
    return _staged_call(
        body, name=name, grid=(N_CHIP, ni),
        in_specs=[pl.BlockSpec((tm, cdim), lambda j, i, cores: ((2 * j + cores[0]) * ni + i, 0)),
                  pl.BlockSpec((tm, cdim), lambda j, i, cores: ((2 * j + cores[1]) * ni + i, 0))],
        out_specs=pl.BlockSpec((None, tm, cdim), lambda j, i, cores: (j, i, 0)),
        out_shape=_sds((N_CHIP, h, cdim), g.dtype),
        scratch_shapes=[pltpu.VMEM((2, tm, cdim), g.dtype), dma((2,)), dma((2,))],
    )(_cores(), flat, flat)


def owner_sum_rider(sums, tiles):
    dma = pltpu.SemaphoreType.DMA
    n = len(sums)
    geo = [(a.shape[1], tm, a.shape[1] // tm) for a, tm in zip(sums, tiles)]
    scratch = []
    for a, (h, tm, nk) in zip(sums, geo):
        cdim = a.shape[2]
        scratch += [pltpu.VMEM(a.shape, a.dtype), pltpu.VMEM((3, h, cdim), a.dtype), pltpu.VMEM((2, h, cdim), F32),
                    dma((3, nk)), dma((3, nk)), dma((nk,)), dma((nk,)), dma((2,))]

    def copies(j, scr):
        part, got, res, send, recv, send2, recv2, local = scr[8 * j:8 * j + 8]
        h, tm, nk = geo[j]
        x, y, c, me, chips = _place()

        def to_owner(t, k):
            chip = 2 * chips[t][0] + chips[t][1]
            return pltpu.make_async_remote_copy(
                src_ref=part.at[chip, pl.ds(k * tm, tm)], dst_ref=got.at[t, pl.ds(k * tm, tm)],
                send_sem=send.at[t, k], recv_sem=recv.at[t, k], device_id=(*chips[t], c), device_id_type=MESH)

        def to_sibling(k):
            place = res.at[c, pl.ds(k * tm, tm)]
            return pltpu.make_async_remote_copy(src_ref=place, dst_ref=place, send_sem=send2.at[k],
                                                recv_sem=recv2.at[k], device_id=(x, y, 1 - c), device_id_type=MESH)

        return part, got, res, local, to_owner, to_sibling, (tm, nk, c, me)

    def start(ins, outs, scr):
        for j in range(n):
            part, _, _, local, _, _, _ = copies(j, scr)
            pltpu.make_async_copy(ins[j], part, local.at[0]).start()
        for j in range(n):
            part, _, _, local, to_owner, _, (tm, nk, c, me) = copies(j, scr)
            pltpu.make_async_copy(ins[j], part, local.at[0]).wait()
            for k in range(nk):
                for t in range(3):
                    to_owner(t, k).start()

    def finish(ins, outs, scr):
        for j in range(n):
            part, got, res, _, to_owner, to_sibling, (tm, nk, c, me) = copies(j, scr)
            for k in range(nk):
                rows = pl.ds(k * tm, tm)
                for t in range(3):
                    to_owner(t, k).wait_recv()
                acc = part[me, rows, :].astype(F32)
                for t in range(3):
                    acc = acc + got[t, rows, :].astype(F32)
                res[c, rows, :] = acc
                to_sibling(k).start()
        for j in range(n):
            _, _, res, local, to_owner, to_sibling, (tm, nk, c, me) = copies(j, scr)
            for k in range(nk):
                to_sibling(k).wait_recv()
            for k in range(nk):
                to_sibling(k).wait_send()
                for t in range(3):
                    to_owner(t, k).wait_send()
            pltpu.make_async_copy(res, outs[j], local.at[1]).start()
        for j in range(n):
            _, _, res, local, _, _, _ = copies(j, scr)
            pltpu.make_async_copy(res, outs[j], local.at[1]).wait()

    return Rider(list(sums), [_sds((2, a.shape[1], a.shape[2]), F32) for a in sums], scratch, start, finish)


def gather_conv_w(w):
    def body(in_ref, out_ref, send, recv):
        x, y, c, me, chips = _place()
        out_ref[me] = in_ref[...]
        copies = []
        for t in range(3):
            copies.append(pltpu.make_async_remote_copy(
                src_ref=out_ref.at[me], dst_ref=out_ref.at[me], send_sem=send.at[t], recv_sem=recv.at[t],
                device_id=(*chips[t], c), device_id_type=MESH))
        for cp in copies:
            cp.start()
        for cp in copies:
            cp.wait_recv()
        for cp in copies:
            cp.wait_send()

    dma = pltpu.SemaphoreType.DMA
    vmem = pl.BlockSpec(memory_space=pltpu.VMEM)
    return pl.pallas_call(
        body, out_shape=_sds((N_CHIP,) + w.shape, w.dtype), in_specs=[vmem], out_specs=vmem, name="gather_conv_w",
        scratch_shapes=[dma((3,)), dma((3,))],
        compiler_params=pltpu.CompilerParams(has_side_effects=True))(w)


N_DEV = 8
SMALL_ROWS = 32
SMALL_LANES = 1024


def all_reduce_small(arrays):
    n_arr = len(arrays)
    places = []
    for k, a in enumerate(arrays):
        for ri in range(a.shape[0]):
            for c0 in range(0, a.shape[1], SMALL_LANES):
                places.append((k, ri, c0, min(SMALL_LANES, a.shape[1] - c0), len(places)))
    assert len(places) <= SMALL_ROWS

    def body(*refs):
        ins, outs = refs[:n_arr], refs[n_arr:2 * n_arr]
        buf, send, recv = refs[2 * n_arr:]
        x, y, c, _, _ = _place()
        me = 4 * x + 2 * y + c
        buf[me] = jnp.zeros((SMALL_ROWS, SMALL_LANES), F32)
        for k, ri, c0, width, row in places:
            buf[me, row:row + 1, 0:width] = ins[k][ri:ri + 1, c0:c0 + width]
        copies = []
        for r in range(1, N_DEV):
            px = (1 - x) if r & 4 else x
            py = (1 - y) if r & 2 else y
            pc = (1 - c) if r & 1 else c
            copies.append(pltpu.make_async_remote_copy(
                src_ref=buf.at[me], dst_ref=buf.at[me], send_sem=send.at[r - 1], recv_sem=recv.at[r - 1],
                device_id=(px, py, pc), device_id_type=MESH))
        for cp in copies:
            cp.start()
        for cp in copies:
            cp.wait_recv()
        for cp in copies:
            cp.wait_send()
        acc = buf[0]
        for j in range(1, N_DEV):
            acc = acc + buf[j]
        for k, ri, c0, width, row in places:
            outs[k][ri:ri + 1, c0:c0 + width] = acc[row:row + 1, 0:width]

    dma = pltpu.SemaphoreType.DMA
    vmem = pl.BlockSpec(memory_space=pltpu.VMEM)
    return pl.pallas_call(
        body, out_shape=[_sds(a.shape, F32) for a in arrays], in_specs=[vmem] * n_arr, out_specs=[vmem] * n_arr,
        name="all_reduce_small",
        scratch_shapes=[pltpu.VMEM((N_DEV, SMALL_ROWS, SMALL_LANES), F32), dma((N_DEV - 1,)), dma((N_DEV - 1,))],
        compiler_params=pltpu.CompilerParams(has_side_effects=True))(*arrays)


def _row_tile(rows, limit, multiple):
    return max(t for t in range(multiple, min(rows, limit) + 1, multiple) if rows % t == 0)


def add_pair(a, b, name):
    _, h, c = a.shape

    def body(a_ref, b_ref, o_ref):
        o_ref[...] = (a_ref[...].astype(F32) + b_ref[...].astype(F32)).astype(o_ref.dtype)

    blk = pl.BlockSpec((None, h, c), lambda j: (j, 0, 0))
    return _call(body, name=name, out_shape=_sds(a.shape, a.dtype), grid=(N_CHIP,), in_specs=[blk, blk],
                 out_specs=blk)(a, b)


def sum_slots(buf, name):
    _, h, c = buf.shape
    tm = _row_tile(h, 256, 16)

    def body(b_ref, o_ref):
        acc = b_ref[3].astype(F32)
        for t in range(3):
            acc = acc + b_ref[t].astype(F32)
        o_ref[...] = acc

    return _call(body, name=name, out_shape=_sds((h, c), F32), grid=(h // tm,),
                 in_specs=[pl.BlockSpec((N_CHIP, tm, c), lambda i: (0, i, 0))],
                 out_specs=pl.BlockSpec((tm, c), lambda i: (i, 0)))(buf)


def _adamw_math(w, g, m, v):
    c1 = 1.0 - ADAM_B1 ** ADAM_STEP
    c2 = 1.0 - ADAM_B2 ** ADAM_STEP
    m2 = ADAM_B1 * m + (1.0 - ADAM_B1) * g
    v2 = ADAM_B2 * v + (1.0 - ADAM_B2) * (g * g)
    return -ADAM_LR * ((m2 / c1) / (jnp.sqrt(v2 / c2) + ADAM_EPS) + ADAM_WD * w), m2, v2


def adamw(w, g, row_off, m, v, name):
    _, r, c = w.shape
    tm = r if r < 8 else _row_tile(math.gcd(r, row_off) if row_off else r, 128, 8)

    def body(w_ref, g_ref, m_ref, v_ref, go_ref, d_ref, m2_ref, v2_ref):
        gv = g_ref[...]
        go_ref[...] = gv
        d_ref[...], m2_ref[...], v2_ref[...] = _adamw_math(w_ref[...], gv, m_ref[...], v_ref[...])

    blk = pl.BlockSpec((None, tm, c), lambda i: (0, i, 0))
    shp = _sds((1, r, c), F32)
    return _call(body, name=name, out_shape=[shp] * 4, grid=(r // tm,),
                 in_specs=[blk, pl.BlockSpec((tm, c), lambda i: (row_off // tm + i, 0)), blk, blk],
                 out_specs=[blk] * 4)(w, g, m, v)


def adamw_small(ws, gs, ms, vs):
    n = len(ws)

    def body(*refs):
        ins, outs = refs[:4 * n], refs[4 * n:]
        for k in range(n):
            w_ref, g_ref, m_ref, v_ref = (ins[j * n + k] for j in range(4))
            outs[k][...], outs[n + k][...], outs[2 * n + k][...] = _adamw_math(w_ref[...], g_ref[...], m_ref[...],
                                                                               v_ref[...])

    vmem = pl.BlockSpec(memory_space=pltpu.VMEM)
    shapes = [_sds(w.shape, F32) for w in ws] * 3
    res = pl.pallas_call(body, out_shape=shapes, in_specs=[vmem] * (4 * n), out_specs=[vmem] * (3 * n),
                         name="adamw_small")(*ws, *gs, *ms, *vs)
    return res[:n], res[n:2 * n], res[2 * n:]


BIG = ("ffn1_w_gate", "ffn1_w_up", "ffn1_w_down", "w_in", "w_attn_branch", "w_ssd_branch", "w_out",
       "ffn2_w_gate", "ffn2_w_up", "ffn2_w_down")
SMALL = ("ffn1_norm", "mix_norm", "q_norm", "k_norm", "conv_b", "dt_bias", "a_log", "d_skip", "ssd_norm", "ffn2_norm")
WEIGHTS = ("ffn1_norm", "ffn1_w_gate", "ffn1_w_up", "ffn1_w_down", "mix_norm", "w_in", "q_norm", "k_norm", "conv_w",
           "conv_b", "dt_bias", "a_log", "d_skip", "ssd_norm", "w_attn_branch", "w_ssd_branch", "w_out", "ffn2_norm",
           "ffn2_w_gate", "ffn2_w_up", "ffn2_w_down")
CONV_SHARD = SSD_CONV_DIM // N_CHIP
CLASSES = {
    "ffn1_in": (("ffn1_w_gate", 1024), ("ffn1_w_up", 1024)),
    "ffn1_out": (("ffn1_w_down", 704),),
    "mix_in": (("w_in", 1024),),
    "mix_attn": (("w_attn_branch", 512),),
    "late_out": (("ffn2_w_down", 704), ("w_ssd_branch", 512), ("w_out", 256)),
    "ffn2_in": (("ffn2_w_gate", 1024), ("ffn2_w_up", 1024)),
}
CLASS_TILE = {"ffn1_in": 256, "ffn1_out": 176, "mix_in": 128, "mix_attn": 256, "late_out": 368, "ffn2_in": 256,
              "mix_in_top": 128, "mix_in_bottom": 128}
SIBLING_TILE = {"ffn1_in": 1024, "ffn1_out": 352, "mix_attn": 256, "late_out": 736, "ffn2_in": 1024,
                "mix_in_top": 256, "mix_in_bottom": 256}


def _pack_small(vals, conv_part, loss_part=None):
    flat = [vals[k].reshape(-1) for k in SMALL]
    flat.append(jnp.zeros((SSD_CONV * SSD_CONV_DIM,), F32) if conv_part is None else conv_part.reshape(-1))
    flat.append(jnp.zeros((1,), F32) if loss_part is None else loss_part.reshape(1))
    flat = jnp.concatenate(flat)
    return jnp.pad(flat, (0, SMALL_ROWS * D_MODEL - flat.shape[0])).reshape(SMALL_ROWS, D_MODEL)


def _unpack_small(pack, like):
    flat = pack.reshape(-1)
    out, off = {}, 0
    for k in SMALL:
        n = like[k].size
        out[k] = flat[off:off + n].reshape(like[k].shape)
        off += n
    conv = flat[off:off + SSD_CONV * SSD_CONV_DIM].reshape(SSD_CONV, SSD_CONV_DIM)
    return out, conv, flat[off + SSD_CONV * SSD_CONV_DIM]


def _chip_major_cols(a):
    r = a.shape[0]
    return a.reshape(r, N_CHIP, -1).transpose(1, 0, 2)


def _from_chip_major_cols(a):
    return a.transpose(1, 0, 2).reshape(a.shape[1], -1)


def kernel(x, ffn1_norm, ffn1_w_gate, ffn1_w_up, ffn1_w_down, mix_norm, w_in, q_norm, k_norm, conv_w, conv_b, dt_bias, a_log, d_skip, ssd_norm, w_attn_branch, w_ssd_branch, w_out, ffn2_norm, ffn2_w_gate, ffn2_w_up, ffn2_w_down, loss_target, m_ffn1_norm, m_ffn1_w_gate, m_ffn1_w_up, m_ffn1_w_down, m_mix_norm, m_w_in, m_q_norm, m_k_norm, m_conv_w, m_conv_b, m_dt_bias, m_a_log, m_d_skip, m_ssd_norm, m_w_attn_branch, m_w_ssd_branch, m_w_out, m_ffn2_norm, m_ffn2_w_gate, m_ffn2_w_up, m_ffn2_w_down, v_ffn1_norm, v_ffn1_w_gate, v_ffn1_w_up, v_ffn1_w_down, v_mix_norm, v_w_in, v_q_norm, v_k_norm, v_conv_w, v_conv_b, v_dt_bias, v_a_log, v_d_skip, v_ssd_norm, v_w_attn_branch, v_w_ssd_branch, v_w_out, v_ffn2_norm, v_ffn2_w_gate, v_ffn2_w_up, v_ffn2_w_down):
    env = dict(locals())
    wts = {k: env[k] for k in WEIGHTS}
    moms = {k: env["m_" + k] for k in WEIGHTS}
    vars_ = {k: env["v_" + k] for k in WEIGHTS}
    x0 = x[0]
    target = loss_target[0]

    def gather(classes, more=()):
        shards = [jnp.concatenate([wts[k][0] for k, _ in CLASSES[c]], axis=0).astype(BF16) for c in classes]
        return gather_rider(shards + [a for a, _ in more], [CLASS_TILE[c] for c in classes] + [t for _, t in more])

    def reducer(classes, parts):
        sums = [sibling_sum(p, SIBLING_TILE[c], f"sibling_sum_{c}") for c, p in zip(classes, parts)]
        return owner_sum_rider(sums, [CLASS_TILE[c] for c in classes])

    half = D_MODEL // 2
    in_tile = CLASS_TILE["mix_in_top"]
    x1, saved1, (w_ffn1_in,), (w_ffn1_out, w_mix_attn, w_in_top), (w_in_bottom,) = ffn_forward(
        x0, ffn1_norm, lambda rode: rode[0], lambda rode: rode[0], "ffn1", rms_rider=gather(["ffn1_in"]),
        up_rider=gather(["ffn1_out", "mix_attn"], [(w_in[0, :half].astype(BF16), in_tile)]),
        down_rider=gather([], [(w_in[0, half:].astype(BF16), in_tile)]))
    dt0, dt1 = IN_DT0 - 3 * IN_SHARD, IN_DT1 - 3 * IN_SHARD

    def in_columns(w4):
        return jnp.concatenate([w4[0], w4[1], w4[2], w4[3][:, :dt0], w4[3][:, dt1:]], axis=1), w4[3][:, dt0:dt1]

    (main_top, dt_top), (main_bottom, dt_bottom) = in_columns(w_in_top), in_columns(w_in_bottom)
    mixer_w = dict(
        mix_norm=mix_norm,
        w_in_main=jnp.concatenate([main_top, main_bottom], axis=0),
        w_in_dt=jnp.pad(jnp.concatenate([dt_top, dt_bottom], axis=0), ((0, 0), (0, DT_PAD - SSD_HEADS))),
        q_gain=jnp.tile(q_norm, (1, 2)), k_gain=jnp.tile(k_norm, (1, 2)),
        conv_w=_from_chip_major_cols(gather_conv_w(conv_w[0])), conv_b=conv_b, dt_bias=dt_bias, a_log=a_log,
        d_skip=d_skip, ssd_norm=ssd_norm, w_attn_branch=_from_chip_major_cols(w_mix_attn))

    def later_weights(rode):
        late = rode[0]
        return dict(w_ssd_branch=late[:, 704:1216].reshape(SSD_INNER, D_MODEL),
                    w_out=late[:, 1216:1472].reshape(D_MODEL, D_MODEL))

    x2, saved_mix, (w_late_out, w_ffn2_in) = mixer_forward(x1, mixer_w, gather(["late_out", "ffn2_in"]), later_weights)
    x3, saved2, _, _, _ = ffn_forward(x2, ffn2_norm, lambda rode: w_ffn2_in, lambda rode: w_late_out, "ffn2")
    dx3, sq = loss_grad(x3, target, "loss")

    grads = {}
    dx2, grads["ffn2_norm"], d_ffn2_in, d_ffn2_down, _ = ffn_backward(dx3, x2, ffn2_norm, w_ffn2_in, w_late_out,
                                                                      saved2, "ffn2")

    def ride_early(g):
        late = jnp.concatenate([d_ffn2_down, g["w_ssd_branch"].reshape(N_CHIP, -1, D_MODEL),
                                g["w_out"].reshape(N_CHIP, -1, D_MODEL)], axis=1)
        return reducer(["ffn2_in", "late_out"], [d_ffn2_in, late])

    g_in_rows = {}

    def ride_late(g):
        main = g["w_in_main"]
        last = jnp.concatenate([main[:, 3 * IN_SHARD:IN_DT0], g["w_in_dt"][:, :SSD_HEADS], main[:, IN_DT0:]], axis=1)
        for part, rows in (("top", slice(0, D_MODEL // 2)), ("bottom", slice(D_MODEL // 2, D_MODEL))):
            g_in_rows[part] = jnp.stack([main[rows, j * IN_SHARD:(j + 1) * IN_SHARD] for j in range(3)]
                                        + [last[rows]])
        return reducer(["mix_in_top", "mix_attn"], [g_in_rows["top"], _chip_major_cols(g["w_attn_branch"])])

    dx1, gmix = mixer_backward(dx2, x1, saved_mix, ride_early, ride_late)
    dx0, grads["ffn1_norm"], rode_in, rode_out, rode_hidden = ffn_backward(
        dx1, x0, ffn1_norm, w_ffn1_in, w_ffn1_out, saved1, "ffn1",
        hidden_rider=reducer(["mix_in_bottom"], [g_in_rows["bottom"]]),
        ride_down=lambda d: reducer(["ffn1_out"], [d]), ride_in=lambda d: reducer(["ffn1_in"], [d]))
    for k in ("mix_norm", "q_norm", "k_norm", "conv_b", "dt_bias", "a_log", "d_skip", "ssd_norm"):
        grads[k] = gmix[k]
    reduced = dict(zip(("ffn2_in", "late_out", "mix_in_top", "mix_attn", "ffn1_in", "ffn1_out", "mix_in_bottom"),
                       (*gmix["rode_early"], *gmix["rode_late"], rode_in[0], rode_out[0], rode_hidden[0])))
    reduced = {c: r.reshape(-1, r.shape[2]) for c, r in reduced.items()}
    reduced["mix_in"] = jnp.concatenate([reduced.pop("mix_in_top"), reduced.pop("mix_in_bottom")], axis=0)
    summed = all_reduce_small([grads[k] for k in SMALL]
                              + [gmix["conv_w"], (0.5 * jnp.sum(sq) / D_MODEL).reshape(1, 1)])
    g_small = dict(zip(SMALL, summed))
    loss = summed[-1].reshape(())
    chip = 2 * lax.axis_index("x") + lax.axis_index("y")
    g_conv = lax.dynamic_slice_in_dim(summed[-2], chip * CONV_SHARD, CONV_SHARD, axis=1)

    g_final, delta, new_m, new_v = dict(g_small), {}, {}, {}

    def update(k, g_arr, row_off):
        w, m, v = wts[k], moms[k], vars_[k]
        rows, cols = w.shape[1:]
        if cols % 128:
            res = adamw(jnp.swapaxes(w, 1, 2), g_arr[row_off:row_off + rows].T, 0, jnp.swapaxes(m, 1, 2),
                        jnp.swapaxes(v, 1, 2), f"adamw_{k}")
            res = [jnp.swapaxes(r, 1, 2) for r in res]
        else:
            res = adamw(w, g_arr, row_off, m, v, f"adamw_{k}")
        g_final[k], delta[k], new_m[k], new_v[k] = res

    for cls, members in CLASSES.items():
        off = 0
        for k, rows in members:
            update(k, reduced[cls], off)
            off += rows
    update("conv_w", g_conv, 0)
    small = adamw_small(*([d[k] for k in SMALL] for d in (wts, g_small, moms, vars_)))
    for res, vals in zip((delta, new_m, new_v), small):
        res.update(zip(SMALL, vals))

    return (loss, dx0[None], *[g_final[k] for k in WEIGHTS], *[delta[k] for k in WEIGHTS],
            *[new_m[k] for k in WEIGHTS], *[new_v[k] for k in WEIGHTS])
```

```python
import collections
import functools
import math

import jax
import jax.numpy as jnp
from jax import lax
from jax.experimental import pallas as pl
from jax.experimental.pallas import tpu as pltpu

F32 = jnp.float32
BF16 = jnp.bfloat16
MESH = pl.DeviceIdType.MESH

EPS = 1e-6
D_MODEL = 1024
D_FF = 2816
N_CHIP = 4
FF_SHARD = D_FF // N_CHIP
HD = 64
BLK = 128
ATTN_DILATIONS = (1, 4, 16)
HEADS_PER_PATTERN = 8
N_ATTN_HEADS = 24
ALIBI_MAX_EXP = 8.0
ATTN_QKV = 1536
GROUP_W = 512
SSD_INNER = 2048
SSD_HEADS = 32
SSD_GROUPS = 4
SSD_CONV = 4
SSD_CONV_DIM = 3072
IN_COLS = 11808
IN_DT0, IN_DT1 = 9728, 9760
IN_SHARD = IN_COLS // 4
COL_K, COL_V, COL_Z, COL_XBC, COL_GA, COL_GS, P_COLS = 1536, 3072, 4608, 6656, 9728, 10752, 11776
DT_PAD = 128

ADAM_LR, ADAM_B1, ADAM_B2, ADAM_EPS, ADAM_WD, ADAM_STEP = 0.001, 0.9, 0.999, 1e-08, 0.01, 10

V7X_VMEM_LIMIT = 56 * 1024 * 1024
NEG = -1e30


Rider = collections.namedtuple("Rider", "arrays out_shape scratch start finish")
Rider.__doc__ = """An exchange between devices that rides in a compute kernel: its copies are started in the host's
first grid step and waited for in its last, so they travel while the host computes.  arrays / out_shape: extra HBM
operands and results; scratch: extra scratch; start, finish: f(in_refs, out_refs, scratch_refs)."""


def _call(body, *, name, out_shape, in_specs, out_specs, grid=(), scratch_shapes=(), aliases=None, rider=None):
    params = dict(dimension_semantics=("arbitrary",) * len(grid), vmem_limit_bytes=V7X_VMEM_LIMIT)
    if rider is None:
        return pl.pallas_call(
            body, out_shape=out_shape, grid=grid, in_specs=in_specs, out_specs=out_specs,
            scratch_shapes=scratch_shapes, input_output_aliases=aliases or {}, name=name,
            compiler_params=pltpu.CompilerParams(**params))
    single = not isinstance(out_shape, (list, tuple))
    main_out = [out_shape] if single else list(out_shape)
    main_specs = [out_specs] if single else list(out_specs)
    n_in, n_out, n_scr = len(in_specs), len(main_out), len(scratch_shapes)
    r_in, r_out = len(rider.arrays), len(rider.out_shape)

    def wrapped(*refs):
        ins, refs = refs[:n_in], refs[n_in:]
        r_ins, refs = refs[:r_in], refs[r_in:]
        outs, refs = refs[:n_out], refs[n_out:]
        r_outs, refs = refs[:r_out], refs[r_out:]
        scr, r_scr = refs[:n_scr], refs[n_scr:]
        first = last = None
        for axis, size in enumerate(grid):
            at_start, at_end = pl.program_id(axis) == 0, pl.program_id(axis) == size - 1
            first = at_start if first is None else jnp.logical_and(first, at_start)
            last = at_end if last is None else jnp.logical_and(last, at_end)

        @pl.when(first)
        def _():
            rider.start(r_ins, r_outs, r_scr)

        body(*ins, *outs, *scr)

        @pl.when(last)
        def _():
            rider.finish(r_ins, r_outs, r_scr)

    hbm = pl.BlockSpec(memory_space=pl.ANY)
    call = pl.pallas_call(
        wrapped, out_shape=main_out + list(rider.out_shape), grid=grid, in_specs=list(in_specs) + [hbm] * r_in,
        out_specs=main_specs + [hbm] * r_out, scratch_shapes=list(scratch_shapes) + list(rider.scratch), name=name,
        compiler_params=pltpu.CompilerParams(has_side_effects=True, **params))

    def run(*args):
        res = call(*args, *rider.arrays)
        main = res[:n_out]
        return (main[0] if single else main), res[n_out:]

    return run


def _sds(shape, dtype):
    return jax.ShapeDtypeStruct(tuple(shape), dtype)


def _dot(a, b):
    return jnp.dot(a, b, preferred_element_type=F32)


def _dot_nt(a, b):
    return lax.dot_general(a, b, (((1,), (1,)), ((), ())), preferred_element_type=F32)


def _dot_tn(a, b):
    return lax.dot_general(a, b, (((0,), (0,)), ((), ())), preferred_element_type=F32)


def _dot_hi(a, b):
    return jnp.dot(a, b, preferred_element_type=F32, precision=lax.Precision.HIGHEST)


def _sigmoid(x):
    return 1.0 / (1.0 + jnp.exp(-x))


def _lane_first_half(shape):
    return lax.broadcasted_iota(jnp.int32, shape, len(shape) - 1) < HD


def _pair_sum(x, first):
    s_all = jnp.sum(x, axis=-1, keepdims=True)
    s_a = jnp.sum(jnp.where(first, x, 0.0), axis=-1, keepdims=True)
    return s_a, s_all - s_a


def _rowwise(name, fn, rows, consts, outs, accs=(), tm=512, rider=None):
    n_rows = None
    in_arrays, in_specs = [], []
    for r in rows:
        if isinstance(r, tuple):
            arr, w, cb = r
            spec = pl.BlockSpec((tm, w), functools.partial(lambda i, cb: (i, cb), cb=cb))
        else:
            arr = r
            spec = pl.BlockSpec((tm, arr.shape[1]), lambda i: (i, 0))
        n_rows = arr.shape[0]
        in_arrays.append(arr)
        in_specs.append(spec)
    for c in consts:
        in_arrays.append(c)
        in_specs.append(pl.BlockSpec(c.shape, functools.partial(lambda i, n: (0,) * n, n=c.ndim)))
    out_shape = [_sds(s, d) for s, d in outs] + [_sds(s, d) for s, d in accs]
    out_specs = [pl.BlockSpec((tm, s[1]), lambda i: (i, 0)) for s, _ in outs]
    out_specs += [pl.BlockSpec(s, functools.partial(lambda i, n: (0,) * n, n=len(s))) for s, _ in accs]

    def body(*refs):
        fn(pl.program_id(0), *refs)

    res = _call(body, name=name, out_shape=out_shape, grid=(n_rows // tm,), in_specs=in_specs,
                out_specs=out_specs, rider=rider)(*in_arrays)
    return res


def rms_fwd(x, gain, name, rider=None):
    def fn(i, x_ref, g_ref, h_ref):
        xv = x_ref[...]
        r = lax.rsqrt(jnp.mean(xv * xv, axis=-1, keepdims=True) + EPS)
        h_ref[...] = (xv * r * g_ref[...]).astype(h_ref.dtype)

    res = _rowwise(name, fn, [x], [gain], [(x.shape, BF16)], rider=rider)
    return res[0] if rider is None else (res[0][0], res[1])


def rms_bwd(dhs, x, gain, dx_in, name):
    n = len(dhs)

    def fn(i, *refs):
        dh_refs, (x_ref, dxin_ref, g_ref, dx_ref, dg_ref) = refs[:n], refs[n:]
        dh = dh_refs[0][...]
        for r in dh_refs[1:]:
            dh = dh + r[...]
        xv = x_ref[...]
        r = lax.rsqrt(jnp.mean(xv * xv, axis=-1, keepdims=True) + EPS)
        xn = xv * r
        dxn = dh * g_ref[...]
        dx_ref[...] = dxin_ref[...] + r * (dxn - xn * jnp.mean(dxn * xn, axis=-1, keepdims=True))

        @pl.when(i == 0)
        def _():
            dg_ref[...] = jnp.zeros_like(dg_ref)

        dg_ref[...] += jnp.sum(dh * xn, axis=0, keepdims=True)

    return _rowwise(name, fn, list(dhs) + [x, dx_in], [gain], [(x.shape, F32)], [((1, x.shape[1]), F32)])


def loss_grad(y, target, name):
    def fn(i, y_ref, t_ref, dy_ref, sq_ref):
        err = y_ref[...] - t_ref[...]
        dy_ref[...] = err * (1.0 / y_ref.shape[1])

        @pl.when(i == 0)
        def _():
            sq_ref[...] = jnp.zeros_like(sq_ref)

        sq_ref[...] += jnp.sum(err * err, axis=0, keepdims=True)

    return _rowwise(name, fn, [y, target], [], [(y.shape, F32)], [((1, y.shape[1]), F32)])


def matmul_nn(a, b, name, out_dtype, tm, tn, res=None, scale=1.0, rider=None):
    s, k = a.shape
    n = b.shape[1]

    def body(*refs):
        if res is None:
            a_ref, b_ref, o_ref = refs
            o_ref[...] = _dot(a_ref[...], b_ref[...]).astype(o_ref.dtype)
        else:
            a_ref, b_ref, r_ref, o_ref = refs
            o_ref[...] = (r_ref[...] + scale * _dot(a_ref[...], b_ref[...])).astype(o_ref.dtype)

    in_specs = [pl.BlockSpec((tm, k), lambda i, j: (i, 0)), pl.BlockSpec((k, tn), lambda i, j: (0, j))]
    args = [a, b]
    if res is not None:
        in_specs.append(pl.BlockSpec((tm, tn), lambda i, j: (i, j)))
        args.append(res)
    return _call(body, name=name, out_shape=_sds((s, n), out_dtype), grid=(s // tm, n // tn), in_specs=in_specs,
                 out_specs=pl.BlockSpec((tm, tn), lambda i, j: (i, j)), rider=rider)(*args)


def matmul_nt(a, b, name, out_dtype, tm, tn, tk, rider=None):
    s, k = a.shape
    n = b.shape[0]
    nk = k // tk

    def body(a_ref, b_ref, o_ref, acc_ref):
        kk = pl.program_id(2)

        @pl.when(kk == 0)
        def _():
            acc_ref[...] = jnp.zeros_like(acc_ref)

        acc_ref[...] += _dot_nt(a_ref[...].astype(BF16), b_ref[...])

        @pl.when(kk == nk - 1)
        def _():
            o_ref[...] = acc_ref[...].astype(o_ref.dtype)

    return _call(body, name=name, out_shape=_sds((s, n), out_dtype), grid=(s // tm, n // tn, nk),
                 in_specs=[pl.BlockSpec((tm, tk), lambda i, j, kk: (i, kk)),
                           pl.BlockSpec((tn, tk), lambda i, j, kk: (j, kk))],
                 out_specs=pl.BlockSpec((tm, tn), lambda i, j, kk: (i, j)),
                 scratch_shapes=[pltpu.VMEM((tm, tn), F32)], rider=rider)(a, b)


def matmul_tn(a, b, name, tn, ts, a_scale=None, b_scale=None, rider=None):
    s, m = a.shape
    n = b.shape[1]
    ns = s // ts

    def body(a_ref, b_ref, o_ref, acc_ref):
        ss = pl.program_id(1)

        @pl.when(ss == 0)
        def _():
            acc_ref[...] = jnp.zeros_like(acc_ref)

        av, bv = a_ref[...], b_ref[...]
        if a_scale is not None:
            av = av * a_scale
        if b_scale is not None:
            bv = bv * b_scale
        acc_ref[...] += _dot_tn(av.astype(BF16), bv.astype(BF16))

        @pl.when(ss == ns - 1)
        def _():
            o_ref[...] = acc_ref[...].astype(o_ref.dtype)

    return _call(body, name=name, out_shape=_sds((m, n), BF16), grid=(n // tn, ns),
                 in_specs=[pl.BlockSpec((ts, m), lambda j, ss: (ss, 0)), pl.BlockSpec((ts, tn), lambda j, ss: (ss, j))],
                 out_specs=pl.BlockSpec((m, tn), lambda j, ss: (0, j)),
                 scratch_shapes=[pltpu.VMEM((m, tn), F32)], rider=rider)(a, b)


def _piece_specs(pieces, tile, rows_tile, tile_axis_first):
    specs, ranges, t0 = [], [], 0
    for a in pieces:
        n = a.shape[1] // tile

        def index(*ids, t0=t0, n=n):
            t, r = (ids[0], ids[1]) if tile_axis_first else (ids[2], ids[0])
            on = jnp.logical_and(t >= t0, t < t0 + n)
            return jnp.where(on, r, 0), jnp.clip(t - t0, 0, n - 1)

        specs.append(pl.BlockSpec((rows_tile, tile), index))
        ranges.append((t0, n))
        t0 += n
    return specs, ranges


def matmul_tn_pieces(a, pieces, name, tn, ts, rider=None):
    s, m = a.shape
    ns = s // ts
    specs, ranges = _piece_specs(pieces, tn, ts, True)
    n_total = sum(n for _, n in ranges)

    def body(a_ref, *refs):
        b_refs, o_ref, acc_ref = refs[:len(pieces)], refs[-2], refs[-1]
        j, ss = pl.program_id(0), pl.program_id(1)

        @pl.when(ss == 0)
        def _():
            acc_ref[...] = jnp.zeros_like(acc_ref)

        for b_ref, (t0, n) in zip(b_refs, ranges):
            @pl.when(jnp.logical_and(j >= t0, j < t0 + n))
            def _(b_ref=b_ref):
                acc_ref[...] += _dot_tn(a_ref[...], b_ref[...])

        @pl.when(ss == ns - 1)
        def _():
            o_ref[...] = acc_ref[...].astype(o_ref.dtype)

    return _call(body, name=name, out_shape=_sds((m, n_total * tn), BF16), grid=(n_total, ns),
                 in_specs=[pl.BlockSpec((ts, m), lambda j, ss: (ss, 0))] + specs,
                 out_specs=pl.BlockSpec((m, tn), lambda j, ss: (0, j)),
                 scratch_shapes=[pltpu.VMEM((m, tn), F32)], rider=rider)(a, *pieces)


def matmul_nt_pieces(pieces, b, name, out_dtype, tm, tn, tk, rider=None):
    s = pieces[0].shape[0]
    n = b.shape[0]
    specs, ranges = _piece_specs(pieces, tk, tm, False)
    nk = sum(cnt for _, cnt in ranges)

    def body(*refs):
        a_refs, b_ref, o_ref, acc_ref = refs[:len(pieces)], refs[-3], refs[-2], refs[-1]
        kk = pl.program_id(2)

        @pl.when(kk == 0)
        def _():
            acc_ref[...] = jnp.zeros_like(acc_ref)

        for a_ref, (t0, cnt) in zip(a_refs, ranges):
            @pl.when(jnp.logical_and(kk >= t0, kk < t0 + cnt))
            def _(a_ref=a_ref):
                acc_ref[...] += _dot_nt(a_ref[...], b_ref[...])

        @pl.when(kk == nk - 1)
        def _():
            o_ref[...] = acc_ref[...].astype(o_ref.dtype)

    return _call(body, name=name, out_shape=_sds((s, n), out_dtype), grid=(s // tm, n // tn, nk),
                 in_specs=specs + [pl.BlockSpec((tn, tk), lambda i, j, kk: (j, kk))],
                 out_specs=pl.BlockSpec((tm, tn), lambda i, j, kk: (i, j)),
                 scratch_shapes=[pltpu.VMEM((tm, tn), F32)], rider=rider)(*pieces, b)


def ffn_up(h, w704, gate_blk, up_blk, name, tm=512, rider=None):
    s = h.shape[0]

    def body(h_ref, wg_ref, wu_ref, g_ref, u_ref, a_ref):
        hv = h_ref[...]
        g = _dot(hv, wg_ref[...])
        u = _dot(hv, wu_ref[...])
        g_ref[...] = g.astype(BF16)
        u_ref[...] = u.astype(BF16)
        a_ref[...] = (g * _sigmoid(g) * u).astype(BF16)

    ospec = pl.BlockSpec((None, tm, FF_SHARD), lambda j, i: (j, i, 0))
    shp = _sds((N_CHIP, s, FF_SHARD), BF16)
    return _call(body, name=name, out_shape=[shp, shp, shp], grid=(N_CHIP, s // tm),
                 in_specs=[pl.BlockSpec((tm, D_MODEL), lambda j, i: (i, 0)),
                           pl.BlockSpec((None, D_MODEL, FF_SHARD), lambda j, i: (j, gate_blk, 0)),
                           pl.BlockSpec((None, D_MODEL, FF_SHARD), lambda j, i: (j, up_blk, 0))],
                 out_specs=[ospec, ospec, ospec], rider=rider)(h, w704, w704)


def ffn_down(a, w1024, blk, x, name, tm=512, rider=None, target=None):
    s = x.shape[0]

    def block_out(a_ref, wd_ref, x_ref):
        acc = _dot(a_ref[0], wd_ref[0])
        for j in range(1, N_CHIP):
            acc += _dot(a_ref[j], wd_ref[j])
        return x_ref[...] + 0.5 * acc

    def body(a_ref, wd_ref, x_ref, o_ref):
        o_ref[...] = block_out(a_ref, wd_ref, x_ref)

    def loss_body(a_ref, wd_ref, x_ref, t_ref, dy_ref, sq_ref):
        err = block_out(a_ref, wd_ref, x_ref) - t_ref[...]
        dy_ref[...] = err * (1.0 / D_MODEL)

        @pl.when(pl.program_id(0) == 0)
        def _():
            sq_ref[...] = jnp.zeros_like(sq_ref)

        sq_ref[...] += jnp.sum(err * err, axis=0, keepdims=True)

    rows = pl.BlockSpec((tm, D_MODEL), lambda i: (i, 0))
    in_specs = [pl.BlockSpec((N_CHIP, tm, FF_SHARD), lambda i: (0, i, 0)),
                pl.BlockSpec((N_CHIP, FF_SHARD, D_MODEL), lambda i: (0, blk, 0)), rows]
    if target is None:
        return _call(body, name=name, out_shape=_sds((s, D_MODEL), F32), grid=(s // tm,), in_specs=in_specs,
                     out_specs=rows, rider=rider)(a, w1024, x)
    return _call(loss_body, name=name, out_shape=[_sds((s, D_MODEL), F32), _sds((1, D_MODEL), F32)], grid=(s // tm,),
                 in_specs=in_specs + [rows], out_specs=[rows, pl.BlockSpec((1, D_MODEL), lambda i: (0, 0))],
                 rider=rider)(a, w1024, x, target)


def ffn_bwd_hidden(dx, w1024, blk, g, u, name, tm=1024, rider=None):
    s = dx.shape[0]

    def body(dx_ref, wd_ref, g_ref, u_ref, dg_ref, du_ref):
        dy = (0.5 * dx_ref[...]).astype(BF16)
        da = _dot_nt(dy, wd_ref[...])
        gv = g_ref[...].astype(F32)
        uv = u_ref[...].astype(F32)
        sg = _sigmoid(gv)
        dg_ref[...] = (da * uv * (sg * (1.0 + gv * (1.0 - sg)))).astype(BF16)
        du_ref[...] = (da * gv * sg).astype(BF16)

    hspec = pl.BlockSpec((None, tm, FF_SHARD), lambda j, i: (j, i, 0))
    shp = _sds((N_CHIP, s, FF_SHARD), BF16)
    return _call(body, name=name, out_shape=[shp, shp], grid=(N_CHIP, s // tm),
                 in_specs=[pl.BlockSpec((tm, D_MODEL), lambda j, i: (i, 0)),
                           pl.BlockSpec((None, FF_SHARD, D_MODEL), lambda j, i: (j, blk, 0)), hspec, hspec],
                 out_specs=[hspec, hspec], rider=rider)(dx, w1024, g, u)


def ffn_bwd_input(dg, du, w704, gate_blk, up_blk, x, gain, dy, name, tm=512, rider=None):
    s = dg.shape[1]

    def body(dg_ref, du_ref, wg_ref, wu_ref, x_ref, gain_ref, dy_ref, dx_ref, dgain_ref):
        dh = _dot_nt(dg_ref[0], wg_ref[0]) + _dot_nt(du_ref[0], wu_ref[0])
        for j in range(1, N_CHIP):
            dh += _dot_nt(dg_ref[j], wg_ref[j]) + _dot_nt(du_ref[j], wu_ref[j])
        xv = x_ref[...]
        r = lax.rsqrt(jnp.mean(xv * xv, axis=-1, keepdims=True) + EPS)
        xn = xv * r
        dxn = dh * gain_ref[...]
        dx_ref[...] = dy_ref[...] + r * (dxn - xn * jnp.mean(dxn * xn, axis=-1, keepdims=True))

        @pl.when(pl.program_id(0) == 0)
        def _():
            dgain_ref[...] = jnp.zeros_like(dgain_ref)

        dgain_ref[...] += jnp.sum(dh * xn, axis=0, keepdims=True)

    hspec = pl.BlockSpec((N_CHIP, tm, FF_SHARD), lambda i: (0, i, 0))
    rows = pl.BlockSpec((tm, D_MODEL), lambda i: (i, 0))
    whole = pl.BlockSpec((1, D_MODEL), lambda i: (0, 0))
    return _call(body, name=name, out_shape=[_sds((s, D_MODEL), F32), _sds((1, D_MODEL), F32)], grid=(s // tm,),
                 in_specs=[hspec, hspec,
                           pl.BlockSpec((N_CHIP, D_MODEL, FF_SHARD), lambda i: (0, gate_blk, 0), pl.Buffered(1)),
                           pl.BlockSpec((N_CHIP, D_MODEL, FF_SHARD), lambda i: (0, up_blk, 0), pl.Buffered(1)),
                           rows, whole, rows],
                 out_specs=[rows, whole], rider=rider)(dg, du, w704, w704, x, gain, dy)


def ffn_wgrad_in(h, dgu, name, ts=1024, rider=None):
    s = h.shape[0]
    ns = s // ts

    def body(h_ref, d_ref, o_ref, acc_ref):
        ss = pl.program_id(1)

        @pl.when(ss == 0)
        def _():
            acc_ref[...] = jnp.zeros_like(acc_ref)

        acc_ref[...] += _dot_tn(h_ref[...], d_ref[...])

        @pl.when(ss == ns - 1)
        def _():
            o_ref[...] = acc_ref[...].astype(BF16)

    return _call(body, name=name, out_shape=_sds((N_CHIP, D_MODEL, FF_SHARD), BF16), grid=(N_CHIP, ns),
                 in_specs=[pl.BlockSpec((ts, D_MODEL), lambda j, ss: (ss, 0)),
                           pl.BlockSpec((None, ts, FF_SHARD), lambda j, ss: (j, ss, 0))],
                 out_specs=pl.BlockSpec((None, D_MODEL, FF_SHARD), lambda j, ss: (j, 0, 0)),
                 scratch_shapes=[pltpu.VMEM((D_MODEL, FF_SHARD), F32)], rider=rider)(h, dgu)


def ffn_wgrad_down(a, dx, name, ts=1024):
    s = dx.shape[0]
    ns = s // ts

    def body(a_ref, dx_ref, o_ref, acc_ref):
        ss = pl.program_id(1)

        @pl.when(ss == 0)
        def _():
            acc_ref[...] = jnp.zeros_like(acc_ref)

        acc_ref[...] += _dot_tn(a_ref[...], (0.5 * dx_ref[...]).astype(BF16))

        @pl.when(ss == ns - 1)
        def _():
            o_ref[...] = acc_ref[...].astype(BF16)

    return _call(body, name=name, out_shape=_sds((N_CHIP, FF_SHARD, D_MODEL), BF16), grid=(N_CHIP, ns),
                 in_specs=[pl.BlockSpec((None, ts, FF_SHARD), lambda j, ss: (j, ss, 0)),
                           pl.BlockSpec((ts, D_MODEL), lambda j, ss: (ss, 0))],
                 out_specs=pl.BlockSpec((None, FF_SHARD, D_MODEL), lambda j, ss: (j, 0, 0)),
                 scratch_shapes=[pltpu.VMEM((FF_SHARD, D_MODEL), F32)])(a, dx)


def ffn_forward(x, gain, get_w704, get_w1024, tag, rms_rider=None, up_rider=None, down_rider=None, target=None):
    h = rms_fwd(x, gain, f"{tag}_rms", rider=rms_rider)
    h, rode_rms = h if rms_rider is not None else (h, None)
    res = ffn_up(h, get_w704(rode_rms), 0, 1, f"{tag}_up", rider=up_rider)
    (g, u, a), rode_up = res if up_rider is not None else (res, None)
    y = ffn_down(a, get_w1024(rode_up), 0, x, f"{tag}_down", rider=down_rider, target=target)
    y, rode_down = y if down_rider is not None else (y, None)
    return y, (h, g, u, a), rode_rms, rode_up, rode_down


def ffn_backward(dy, x, gain, w704, w1024, saved, tag, hidden_rider=None, ride_down=None, ride_in=None):
    h, g, u, a = saved
    d_wd = ffn_wgrad_down(a, dy, f"{tag}_dwd")
    rode_hidden = None
    if hidden_rider is not None:
        (dg, du), rode_hidden = ffn_bwd_hidden(dy, w1024, 0, g, u, f"{tag}_dhid", tm=512, rider=hidden_rider)
    else:
        dg, du = ffn_bwd_hidden(dy, w1024, 0, g, u, f"{tag}_dhid")
    if ride_down is not None:
        d_wg, d_wd = ffn_wgrad_in(h, dg, f"{tag}_dwg", rider=ride_down(d_wd))
    else:
        d_wg = ffn_wgrad_in(h, dg, f"{tag}_dwg")
    d_win = jnp.concatenate([d_wg, ffn_wgrad_in(h, du, f"{tag}_dwu")], axis=1)
    if ride_in is not None:
        (dx, d_gain), d_win = ffn_bwd_input(dg, du, w704, 0, 1, x, gain, dy, f"{tag}_dh", tm=256,
                                            rider=ride_in(d_win))
    else:
        dx, d_gain = ffn_bwd_input(dg, du, w704, 0, 1, x, gain, dy, f"{tag}_dh")
    return dx, d_gain, d_win, d_wd, rode_hidden


def _alibi_slope(head):
    return float(2.0 ** (-ALIBI_MAX_EXP * (head + 1) / N_ATTN_HEADS))


def _same_head():
    row = lax.broadcasted_iota(jnp.int32, (2 * HD, 2 * HD), 0)
    col = lax.broadcasted_iota(jnp.int32, (2 * HD, 2 * HD), 1)
    return ((row < HD) == (col < HD)).astype(BF16)


def _head_sums(x, same_head):
    hi = x.astype(BF16)
    lo = (x - hi.astype(F32)).astype(BF16)
    return _dot(hi, same_head) + _dot(lo, same_head)


def _head_norm(t, gain_pair, same_head):
    r = lax.rsqrt(_head_sums(t * t, same_head) * (1.0 / HD) + EPS)
    return t * r * gain_pair, r


def qk_norm_fwd(p, q_gain, k_gain, name):
    s = p.shape[0]

    def fn(i, q_ref, k_ref, qg_ref, kg_ref, qn_ref, kn_ref):
        same_head = _same_head()
        for src, g_ref, dst in ((q_ref, qg_ref, qn_ref), (k_ref, kg_ref, kn_ref)):
            for pr in range(ATTN_QKV // (2 * HD)):
                cols = slice(pr * 2 * HD, (pr + 1) * 2 * HD)
                y, _ = _head_norm(src[:, cols].astype(F32), g_ref[...], same_head)
                dst[:, cols] = y.astype(BF16)

    return _rowwise(name, fn, [(p, ATTN_QKV, 0), (p, ATTN_QKV, 1)], [q_gain, k_gain],
                    [((s, ATTN_QKV), BF16), ((s, ATTN_QKV), BF16)])


def qk_norm_bwd(p, dqs, dks, q_gain, k_gain, name):
    s = p.shape[0]
    pairs_per_pattern = GROUP_W // (2 * HD)

    def fn(i, q_ref, k_ref, dq0, dq1, dq2, dk0, dk1, dk2, qg_ref, kg_ref, dqk_ref, dqg_ref, dkg_ref):
        same_head = _same_head()

        @pl.when(i == 0)
        def _():
            dqg_ref[...] = jnp.zeros_like(dqg_ref)
            dkg_ref[...] = jnp.zeros_like(dkg_ref)

        for src, d_refs, g_ref, dst, dg_ref in (
                (q_ref, (dq0, dq1, dq2), qg_ref, dqk_ref.at[:, 0:ATTN_QKV], dqg_ref),
                (k_ref, (dk0, dk1, dk2), kg_ref, dqk_ref.at[:, ATTN_QKV:2 * ATTN_QKV], dkg_ref)):
            for pr in range(ATTN_QKV // (2 * HD)):
                cols = slice(pr * 2 * HD, (pr + 1) * 2 * HD)
                t = src[:, cols].astype(F32)
                r = lax.rsqrt(_head_sums(t * t, same_head) * (1.0 / HD) + EPS)
                xn = t * r
                within = (pr % pairs_per_pattern) * 2 * HD
                dy = d_refs[pr // pairs_per_pattern][:, within:within + 2 * HD]
                dg_ref[:, cols] += jnp.sum(dy * xn, axis=0, keepdims=True)
                dxn = dy * g_ref[...]
                mean = _head_sums(dxn * xn, same_head) * (1.0 / HD)
                dst[:, cols] = (r * (dxn - xn * mean)).astype(BF16)

    return _rowwise(name, fn, [(p, ATTN_QKV, 0), (p, ATTN_QKV, 1)] + list(dqs) + list(dks), [q_gain, k_gain],
                    [((s, 2 * ATTN_QKV), BF16)], [((1, ATTN_QKV), F32), ((1, ATTN_QKV), F32)])


def _to_streams(a, d):
    if d == 1:
        return a
    s, c = a.shape
    return a.reshape(s // d, d, c).transpose(1, 0, 2).reshape(s, c)


def _from_streams(a, d):
    if d == 1:
        return a
    s, c = a.shape
    return a.reshape(d, s // d, c).transpose(1, 0, 2).reshape(s, c)


def _attn_masks():
    row = lax.broadcasted_iota(jnp.int32, (BLK, BLK), 0)
    col = lax.broadcasted_iota(jnp.int32, (BLK, BLK), 1)
    rel_diag = row - col
    rel_prev = rel_diag + BLK
    return rel_diag, rel_prev


def attn_fwd(q, k, v, pattern, name, tq=512):
    s = q.shape[0]
    d = ATTN_DILATIONS[pattern]
    blocks_per_stream = (s // d) // BLK
    nsb = tq // BLK

    def body(q_ref, k_ref, v_ref, kp_ref, vp_ref, o_ref, l_ref):
        i = pl.program_id(0)
        rel_diag, rel_prev = _attn_masks()
        first = _lane_first_half((BLK, 2 * HD))
        rd_f = (rel_diag * d).astype(F32)
        rp_f = (rel_prev * d).astype(F32)
        for sb in range(nsb):
            rows = slice(sb * BLK, (sb + 1) * BLK)
            has_prev = ((i * nsb + sb) % blocks_per_stream != 0).astype(jnp.int32)
            m_diag = rel_diag >= 0
            m_prev = (rel_prev + (1 - has_prev) * (4 * BLK)) <= BLK
            for pr in range(GROUP_W // (2 * HD)):
                cols = slice(pr * 2 * HD, (pr + 1) * 2 * HD)
                qp = q_ref[rows, cols]
                kc, vc = k_ref[rows, cols], v_ref[rows, cols]
                if sb == 0:
                    kp, vp = kp_ref[:, cols], vp_ref[:, cols]
                else:
                    prows = slice((sb - 1) * BLK, sb * BLK)
                    kp, vp = k_ref[prows, cols], v_ref[prows, cols]
                outs, lses = [], []
                for e in range(2):
                    slope = _alibi_slope(pattern * HEADS_PER_PATTERN + 2 * pr + e)
                    qm = jnp.where(first if e == 0 else jnp.logical_not(first), qp, jnp.zeros_like(qp))
                    s1 = jnp.where(m_diag, _dot_nt(qm, kc) * 0.125 - slope * rd_f, NEG)
                    s0 = jnp.where(m_prev, _dot_nt(qm, kp) * 0.125 - slope * rp_f, NEG)
                    m = jnp.maximum(jnp.max(s1, axis=-1, keepdims=True), jnp.max(s0, axis=-1, keepdims=True))
                    p1 = jnp.exp(s1 - m)
                    p0 = jnp.exp(s0 - m)
                    l = jnp.sum(p1, axis=-1, keepdims=True) + jnp.sum(p0, axis=-1, keepdims=True)
                    inv = 1.0 / l
                    outs.append(_dot((p1 * inv).astype(BF16), vc) + _dot((p0 * inv).astype(BF16), vp))
                    lses.append(m + jnp.log(l))
                o_ref[rows, cols] = jnp.where(first, outs[0], outs[1])
                l_ref[rows, cols] = jnp.where(first, lses[0], lses[1])

    cur = pl.BlockSpec((tq, GROUP_W), lambda i: (i, 0))
    prev = pl.BlockSpec((BLK, GROUP_W), lambda i: (jnp.maximum(i * nsb - 1, 0), 0))
    return _call(body, name=name, out_shape=[_sds((s, GROUP_W), F32), _sds((s, GROUP_W), F32)], grid=(s // tq,),
                 in_specs=[cur, cur, cur, prev, prev], out_specs=[cur, cur])(q, k, v, k, v)


def attn_merge_fwd(os_, lses, name):
    s = os_[0].shape[0]

    def fn(i, o0, o1, o2, l0, l1, l2, out_ref):
        m = jnp.maximum(jnp.maximum(l0[...], l1[...]), l2[...])
        e0, e1, e2 = jnp.exp(l0[...] - m), jnp.exp(l1[...] - m), jnp.exp(l2[...] - m)
        inv = 1.0 / (e0 + e1 + e2)
        out_ref[...] = ((e0 * inv) * o0[...] + (e1 * inv) * o1[...] + (e2 * inv) * o2[...]).astype(BF16)

    return _rowwise(name, fn, list(os_) + list(lses), [], [((s, GROUP_W), BF16)])[0]


def attn_merge_bwd(d_out, os_, lses, name):
    s = d_out.shape[0]

    def fn(i, do_ref, o0, o1, o2, l0, l1, l2, d0, d1, d2, c0, c1, c2):
        first = _lane_first_half((do_ref.shape[0], 2 * HD))
        m = jnp.maximum(jnp.maximum(l0[...], l1[...]), l2[...])
        e0, e1, e2 = jnp.exp(l0[...] - m), jnp.exp(l1[...] - m), jnp.exp(l2[...] - m)
        inv = 1.0 / (e0 + e1 + e2)
        w0, w1, w2 = e0 * inv, e1 * inv, e2 * inv
        do = do_ref[...]
        prod = do * (w0 * o0[...] + w1 * o1[...] + w2 * o2[...])
        same_head = _same_head()
        for pr in range(GROUP_W // (2 * HD)):
            cols = slice(pr * 2 * HD, (pr + 1) * 2 * HD)
            t = _head_sums(prod[:, cols], same_head)
            for w, c_ref in ((w0, c0), (w1, c1), (w2, c2)):
                c_ref[:, cols] = w[:, cols] * t
        for w, d_ref in ((w0, d0), (w1, d1), (w2, d2)):
            d_ref[...] = (w * do).astype(BF16)

    shp = (s, GROUP_W)
    return _rowwise(name, fn, [d_out] + list(os_) + list(lses), [],
                    [(shp, BF16)] * 3 + [(shp, F32)] * 3)


def attn_bwd(q, k, v, d_o, cterm, lse, pattern, name, tq=512):
    s = q.shape[0]
    d = ATTN_DILATIONS[pattern]
    blocks_per_stream = (s // d) // BLK
    nsb = tq // BLK
    n_blocks = s // BLK

    def body(q_ref, k_ref, v_ref, do_ref, c_ref, l_ref, kp_ref, vp_ref, qn_ref, don_ref, cn_ref, ln_ref,
             dq_ref, dk_ref, dv_ref):
        i = pl.program_id(0)
        rel_diag, rel_prev = _attn_masks()
        first = _lane_first_half((BLK, 2 * HD))
        second = jnp.logical_not(first)
        rd_f = (rel_diag * d).astype(F32)
        rp_f = (rel_prev * d).astype(F32)
        m_diag = rel_diag >= 0
        dq_ref[...] = jnp.zeros_like(dq_ref)
        dk_ref[...] = jnp.zeros_like(dk_ref)
        dv_ref[...] = jnp.zeros_like(dv_ref)

        def pair(qp, dop, cp, lp, kp, vp, rel_f, mask):
            dq = dk = dv = None
            for e in range(2):
                lanes = first if e == 0 else second
                slope = slopes[e]
                qm = jnp.where(lanes, qp, jnp.zeros_like(qp))
                dom = jnp.where(lanes, dop, jnp.zeros_like(dop))
                km = jnp.where(lanes, kp, jnp.zeros_like(kp))
                sc = jnp.where(mask, _dot_nt(qm, kp) * 0.125 - slope * rel_f, NEG)
                pm = jnp.exp(sc - lp[:, e * HD:e * HD + 1])
                dl = pm * (_dot_nt(dom, vp) - cp[:, e * HD:e * HD + 1])
                dl16 = dl.astype(BF16)
                t_dq = _dot(dl16, km)
                t_dk = _dot_tn(dl16, qm)
                t_dv = _dot_tn(pm.astype(BF16), dom)
                dq = t_dq if dq is None else dq + t_dq
                dk = t_dk if dk is None else dk + t_dk
                dv = t_dv if dv is None else dv + t_dv
            return dq * 0.125, dk * 0.125, dv

        for pr in range(GROUP_W // (2 * HD)):
            cols = slice(pr * 2 * HD, (pr + 1) * 2 * HD)
            slopes = [_alibi_slope(pattern * HEADS_PER_PATTERN + 2 * pr + e) for e in range(2)]
            for sb in range(nsb + 1):
                gb = i * nsb + sb
                if sb < nsb:
                    rows = slice(sb * BLK, (sb + 1) * BLK)
                    qp, dop, cp, lp = q_ref[rows, cols], do_ref[rows, cols], c_ref[rows, cols], l_ref[rows, cols]
                else:
                    qp, dop, cp, lp = qn_ref[:, cols], don_ref[:, cols], cn_ref[:, cols], ln_ref[:, cols]
                if sb < nsb:
                    dq1, dk1, dv1 = pair(qp, dop, cp, lp, k_ref[rows, cols], v_ref[rows, cols], rd_f, m_diag)
                    dq_ref[rows, cols] += dq1
                    dk_ref[rows, cols] += dk1
                    dv_ref[rows, cols] += dv1
                valid = jnp.logical_and(gb % blocks_per_stream != 0, gb < n_blocks).astype(jnp.int32)
                m_prev = jnp.logical_and(rel_prev <= BLK, (rel_prev + (1 - valid) * (4 * BLK)) <= BLK)
                if sb == 0:
                    kp, vp = kp_ref[:, cols], vp_ref[:, cols]
                else:
                    prows = slice((sb - 1) * BLK, sb * BLK)
                    kp, vp = k_ref[prows, cols], v_ref[prows, cols]
                dq0, dk0, dv0 = pair(qp, dop, cp, lp, kp, vp, rp_f, m_prev)
                if sb < nsb:
                    dq_ref[rows, cols] += dq0
                if sb > 0:
                    dk_ref[prows, cols] += dk0
                    dv_ref[prows, cols] += dv0

    cur = pl.BlockSpec((tq, GROUP_W), lambda i: (i, 0))
    prev = pl.BlockSpec((BLK, GROUP_W), lambda i: (jnp.maximum(i * nsb - 1, 0), 0))
    nxt = pl.BlockSpec((BLK, GROUP_W), lambda i: (jnp.minimum((i + 1) * nsb, n_blocks - 1), 0))
    shp = _sds((s, GROUP_W), F32)
    return _call(body, name=name, out_shape=[shp, shp, shp], grid=(s // tq,),
                 in_specs=[cur] * 6 + [prev, prev] + [nxt] * 4, out_specs=[cur, cur, cur])(
                     q, k, v, d_o, cterm, lse, k, v, q, d_o, cterm, lse)


def _band_constants(d):
    row = lax.broadcasted_iota(jnp.int32, (2 * BLK, 2 * BLK), 0)
    col = lax.broadcasted_iota(jnp.int32, (2 * BLK, 2 * BLK), 1)
    rel = BLK + jnp.where(row >= BLK, row - BLK, row) - col
    band = jnp.logical_and(rel >= 0, rel <= BLK)
    return (rel * d).astype(F32), band, (col >= BLK).astype(jnp.int32)


def _stack_heads(x, first):
    zero = jnp.zeros_like(x)
    return jnp.concatenate([jnp.where(first, x, zero), jnp.where(first, zero, x)], axis=0)


def _unstack_heads(x2, first):
    return jnp.where(first, x2[:BLK], x2[BLK:])


def _head_column(x):
    return jnp.concatenate([x[:, 0:1], x[:, HD:HD + 1]], axis=0)


def attn_fwd2(q, k, v, pattern, name, tq=512):
    s = q.shape[0]
    d = ATTN_DILATIONS[pattern]
    blocks_per_stream = (s // d) // BLK
    nsb = tq // BLK

    def body(q_ref, k_ref, v_ref, kp_ref, vp_ref, o_ref, l_ref):
        i = pl.program_id(0)
        rel_f, band, own = _band_constants(d)
        first = _lane_first_half((BLK, 2 * HD))
        upper = lax.broadcasted_iota(jnp.int32, (2 * BLK, 1), 0) < BLK
        for sb in range(nsb):
            rows = slice(sb * BLK, (sb + 1) * BLK)
            has_prev = ((i * nsb + sb) % blocks_per_stream != 0).astype(jnp.int32)
            mask = jnp.logical_and(band, (own + has_prev) > 0)
            for pr in range(GROUP_W // (2 * HD)):
                cols = slice(pr * 2 * HD, (pr + 1) * 2 * HD)
                if sb == 0:
                    kcat = jnp.concatenate([kp_ref[:, cols], k_ref[rows, cols]], axis=0)
                    vcat = jnp.concatenate([vp_ref[:, cols], v_ref[rows, cols]], axis=0)
                else:
                    both = slice((sb - 1) * BLK, (sb + 1) * BLK)
                    kcat, vcat = k_ref[both, cols], v_ref[both, cols]
                h0 = pattern * HEADS_PER_PATTERN + 2 * pr
                slope = jnp.where(upper, _alibi_slope(h0), _alibi_slope(h0 + 1))
                sc = _dot_nt(_stack_heads(q_ref[rows, cols], first), kcat) * 0.125 - slope * rel_f
                sc = jnp.where(mask, sc, NEG)
                m = jnp.max(sc, axis=-1, keepdims=True)
                p = jnp.exp(sc - m)
                l = jnp.sum(p, axis=-1, keepdims=True)
                o2 = _dot((p * (1.0 / l)).astype(BF16), vcat)
                o_ref[rows, cols] = _unstack_heads(o2, first)
                lse = m + jnp.log(l)
                l_ref[rows, cols] = jnp.where(first, lse[:BLK], lse[BLK:])

    cur = pl.BlockSpec((tq, GROUP_W), lambda i: (i, 0))
    prev = pl.BlockSpec((BLK, GROUP_W), lambda i: (jnp.maximum(i * nsb - 1, 0), 0))
    return _call(body, name=name, out_shape=[_sds((s, GROUP_W), F32), _sds((s, GROUP_W), F32)], grid=(s // tq,),
                 in_specs=[cur, cur, cur, prev, prev], out_specs=[cur, cur])(q, k, v, k, v)


def attn_bwd2(q, k, v, d_o, cterm, lse, pattern, name, tq=512):
    s = q.shape[0]
    d = ATTN_DILATIONS[pattern]
    blocks_per_stream = (s // d) // BLK
    nsb = tq // BLK
    n_blocks = s // BLK

    def body(q_ref, k_ref, v_ref, do_ref, c_ref, l_ref, kp_ref, vp_ref, qn_ref, kn_ref, vn_ref, don_ref, cn_ref,
             ln_ref, dq_ref, dk_ref, dv_ref):
        i = pl.program_id(0)
        rel_f, band, own = _band_constants(d)
        first = _lane_first_half((BLK, 2 * HD))
        upper = lax.broadcasted_iota(jnp.int32, (2 * BLK, 1), 0) < BLK
        dk_ref[...] = jnp.zeros_like(dk_ref)
        dv_ref[...] = jnp.zeros_like(dv_ref)
        for sb in range(nsb + 1):
            gb = i * nsb + sb
            rows = slice(sb * BLK, (sb + 1) * BLK)
            before = slice((sb - 1) * BLK, sb * BLK)
            inside = (gb < n_blocks).astype(jnp.int32)
            has_prev = jnp.logical_and(gb % blocks_per_stream != 0, gb < n_blocks).astype(jnp.int32)
            mask = jnp.logical_and(band, (own * inside + has_prev) > 0)
            for pr in range(GROUP_W // (2 * HD)):
                cols = slice(pr * 2 * HD, (pr + 1) * 2 * HD)
                if sb == 0:
                    kcat = jnp.concatenate([kp_ref[:, cols], k_ref[rows, cols]], axis=0)
                    vcat = jnp.concatenate([vp_ref[:, cols], v_ref[rows, cols]], axis=0)
                elif sb == nsb:
                    kcat = jnp.concatenate([k_ref[before, cols], kn_ref[:, cols]], axis=0)
                    vcat = jnp.concatenate([v_ref[before, cols], vn_ref[:, cols]], axis=0)
                else:
                    both = slice((sb - 1) * BLK, (sb + 1) * BLK)
                    kcat, vcat = k_ref[both, cols], v_ref[both, cols]
                if sb < nsb:
                    qp, dop, cp, lp = q_ref[rows, cols], do_ref[rows, cols], c_ref[rows, cols], l_ref[rows, cols]
                else:
                    qp, dop, cp, lp = qn_ref[:, cols], don_ref[:, cols], cn_ref[:, cols], ln_ref[:, cols]
                h0 = pattern * HEADS_PER_PATTERN + 2 * pr
                slope = jnp.where(upper, _alibi_slope(h0), _alibi_slope(h0 + 1))
                q2 = _stack_heads(qp, first)
                do2 = _stack_heads(dop, first)
                sc = jnp.where(mask, _dot_nt(q2, kcat) * 0.125 - slope * rel_f, NEG)
                pm = jnp.exp(sc - _head_column(lp))
                dl = (pm * (_dot_nt(do2, vcat) - _head_column(cp))).astype(BF16)
                if sb < nsb:
                    dq_ref[rows, cols] = _unstack_heads(_dot(dl, kcat), first) * 0.125
                dk2 = _dot_tn(dl, q2) * 0.125
                dv2 = _dot_tn(pm.astype(BF16), do2)
                if sb > 0:
                    dk_ref[before, cols] += dk2[:BLK]
                    dv_ref[before, cols] += dv2[:BLK]
                if sb < nsb:
                    dk_ref[rows, cols] += dk2[BLK:]
                    dv_ref[rows, cols] += dv2[BLK:]

    cur = pl.BlockSpec((tq, GROUP_W), lambda i: (i, 0))
    prev = pl.BlockSpec((BLK, GROUP_W), lambda i: (jnp.maximum(i * nsb - 1, 0), 0))
    nxt = pl.BlockSpec((BLK, GROUP_W), lambda i: (jnp.minimum((i + 1) * nsb, n_blocks - 1), 0))
    shp = _sds((s, GROUP_W), F32)
    return _call(body, name=name, out_shape=[shp, shp, shp], grid=(s // tq,),
                 in_specs=[cur] * 6 + [prev, prev] + [nxt] * 6, out_specs=[cur, cur, cur])(
                     q, k, v, d_o, cterm, lse, k, v, q, k, v, d_o, cterm, lse)


HALO = 16
CONV_TQ = 512


def conv_fwd(p, w, b, name):
    s = p.shape[0]
    tq = CONV_TQ
    ncol = SSD_CONV_DIM // GROUP_W
    cb0 = COL_XBC // GROUP_W

    def body(u_ref, up_ref, w_ref, b_ref, c_ref, xc_ref):
        i = pl.program_id(0)
        prev = up_ref[...].astype(F32) * (i > 0).astype(F32)
        ext = jnp.concatenate([prev, u_ref[...].astype(F32)], axis=0)
        acc = b_ref[...] + w_ref[SSD_CONV - 1:SSD_CONV, :] * ext[HALO:HALO + tq]
        for kk in range(SSD_CONV - 1):
            acc += w_ref[kk:kk + 1, :] * pltpu.roll(ext, SSD_CONV - 1 - kk, 0)[HALO:HALO + tq]
        c_ref[...] = acc.astype(BF16)
        xc_ref[...] = (acc * _sigmoid(acc)).astype(BF16)

    cur_in = pl.BlockSpec((tq, GROUP_W), lambda i, j: (i, cb0 + j))
    prev_in = pl.BlockSpec((HALO, GROUP_W), lambda i, j: (jnp.maximum(i * (tq // HALO) - 1, 0), cb0 + j))
    cur_out = pl.BlockSpec((tq, GROUP_W), lambda i, j: (i, j))
    shp = _sds((s, SSD_CONV_DIM), BF16)
    return _call(body, name=name, out_shape=[shp, shp], grid=(s // tq, ncol),
                 in_specs=[cur_in, prev_in, pl.BlockSpec((SSD_CONV, GROUP_W), lambda i, j: (0, j)),
                           pl.BlockSpec((1, GROUP_W), lambda i, j: (0, j))],
                 out_specs=[cur_out, cur_out])(p, p, w, b)


def conv_bwd(p, cpre, dxs, d_b, d_c, w, name):
    s = p.shape[0]
    tq = CONV_TQ
    ncol = SSD_CONV_DIM // GROUP_W
    n_xs = SSD_INNER // GROUP_W
    cb0 = COL_XBC // GROUP_W
    nt = s // tq

    def body(u_ref, c_ref, cn_ref, dx_ref, dxn_ref, dbm_ref, dbmn_ref, dcm_ref, dcmn_ref, w_ref,
             du_ref, dw_ref, db_ref):
        j, i = pl.program_id(0), pl.program_id(1)

        def dpre(c16, dx):
            c = c16.astype(F32)
            sg = _sigmoid(c)
            return dx * (sg * (1.0 + c * (1.0 - sg)))

        def pick(a_ref, b_ref, c_ref_):
            return jnp.where(j < n_xs, a_ref[...], jnp.where(j == n_xs, b_ref[...], c_ref_[...]))

        dc = dpre(c_ref[...], pick(dx_ref, dbm_ref, dcm_ref))
        dcn = dpre(cn_ref[...], pick(dxn_ref, dbmn_ref, dcmn_ref)) * (i < nt - 1).astype(F32)
        dext = jnp.concatenate([dc, dcn], axis=0)
        u = u_ref[...].astype(F32)

        @pl.when(i == 0)
        def _():
            dw_ref[...] = jnp.zeros_like(dw_ref)
            db_ref[...] = jnp.zeros_like(db_ref)

        du = w_ref[SSD_CONV - 1:SSD_CONV, :] * dc
        dw_ref[SSD_CONV - 1:SSD_CONV, :] += jnp.sum(dc * u, axis=0, keepdims=True)
        for kk in range(SSD_CONV - 1):
            sh = SSD_CONV - 1 - kk
            ahead = pltpu.roll(dext, tq + HALO - sh, 0)[0:tq]
            du += w_ref[kk:kk + 1, :] * ahead
            dw_ref[kk:kk + 1, :] += jnp.sum(ahead * u, axis=0, keepdims=True)
        du_ref[...] = du.astype(BF16)
        db_ref[...] += jnp.sum(dc, axis=0, keepdims=True)

    hb = tq // HALO
    cur_p = pl.BlockSpec((tq, GROUP_W), lambda j, i: (i, cb0 + j))
    cur = pl.BlockSpec((tq, GROUP_W), lambda j, i: (i, j))
    nxt = pl.BlockSpec((HALO, GROUP_W), lambda j, i: (jnp.minimum((i + 1) * hb, s // HALO - 1), j))

    def piece(first_tile, n_tiles):
        def on(j):
            return jnp.logical_and(j >= first_tile, j < first_tile + n_tiles)

        def col(j):
            return jnp.clip(j - first_tile, 0, n_tiles - 1)

        return (pl.BlockSpec((tq, GROUP_W), lambda j, i: (jnp.where(on(j), i, 0), col(j))),
                pl.BlockSpec((HALO, GROUP_W),
                             lambda j, i: (jnp.where(on(j), jnp.minimum((i + 1) * hb, s // HALO - 1), 0), col(j))))

    return _call(body, name=name,
                 out_shape=[_sds((s, SSD_CONV_DIM), BF16), _sds((8, SSD_CONV_DIM), F32), _sds((1, SSD_CONV_DIM), F32)],
                 grid=(ncol, nt),
                 in_specs=[cur_p, cur, nxt, *piece(0, n_xs), *piece(n_xs, 1), *piece(n_xs + 1, 1),
                           pl.BlockSpec((SSD_CONV, GROUP_W), lambda j, i: (0, j))],
                 out_specs=[cur, pl.BlockSpec((8, GROUP_W), lambda j, i: (0, j)),
                            pl.BlockSpec((1, GROUP_W), lambda j, i: (0, j))])(
                                p, cpre, cpre, dxs, dxs, d_b, d_b, d_c, d_c, w)


def _softplus(x):
    return jnp.maximum(x, 0.0) + jnp.log(1.0 + jnp.exp(-jnp.abs(x)))


def _ssd_decays(dtr_ref, dtrt_ref, bias_ref, biast_ref, alog_ref, alogt_ref):
    row = lax.broadcasted_iota(jnp.int32, (BLK, BLK), 0)
    col = lax.broadcasted_iota(jnp.int32, (BLK, BLK), 1)
    lower = (row >= col).astype(F32)
    upper = (row <= col).astype(F32)
    dtb = dtr_ref[...] + bias_ref[...]
    dt = _softplus(dtb)
    a = dt * (-jnp.exp(alog_ref[...]))
    cs = _dot_hi(lower, a)
    a_t = _softplus(dtrt_ref[...] + biast_ref[...]) * (-jnp.exp(alogt_ref[...]))
    cs_t = _dot_hi(a_t, upper)
    return dtb, dt, cs, cs_t, row, col, upper


SSD_GROUPS_PER_STEP = 4


def _per_group(body, gps, kinds):
    def wrapped(*refs):
        for gi in range(gps):
            args, pos = [], 0
            for kind, n in kinds:
                if kind == "each":
                    args.append(refs[pos + gi])
                    pos += gps
                    continue
                ref = refs[pos]
                pos += 1
                if kind == "cols":
                    args.append(ref.at[:, gi * n:(gi + 1) * n])
                else:
                    args.append(ref.at[gi] if n == 1 else ref.at[pl.ds(gi * n, n)])
            body(*args)

    return wrapped


def ssd_fwd(p, xc, dtg, dtg_t, params, gn, name):
    s = p.shape[0]
    nc = s // BLK
    bias, bias_t, alog, alog_t, dskip = params

    def body(xs_ref, b_ref, c_ref, z_ref, dtr_ref, dtrt_ref, bias_ref, biast_ref, alog_ref, alogt_ref, dsk_ref,
             gn_ref, y_ref, sin_ref, hp_ref, h_ref):
        c_idx = pl.program_id(1)

        @pl.when(c_idx == 0)
        def _():
            h_ref[...] = jnp.zeros_like(h_ref)

        _, dt, cs, cs_t, row, col, _ = _ssd_decays(dtr_ref, dtrt_ref, bias_ref, biast_ref, alog_ref, alogt_ref)
        first = _lane_first_half((BLK, 2 * HD))
        first_row = _lane_first_half((1, 2 * HD))
        tril = row >= col
        b16, c16 = b_ref[...], c_ref[...]
        cb = _dot_nt(c16, b16)
        n_pairs = GROUP_W // (2 * HD)
        tot = cs[BLK - 1:BLK, :]
        exp_cs, exp_rest, exp_tot = jnp.exp(cs), jnp.exp(tot - cs), jnp.exp(tot)

        lanes_of_head = (lax.broadcasted_iota(jnp.int32, (8, GROUP_W), 1) // HD
                         == lax.broadcasted_iota(jnp.int32, (8, GROUP_W), 0)).astype(BF16)

        def per_head(v, mask):
            if v.shape[0] == 1:
                return jnp.concatenate([jnp.where(mask, v[:, 2 * pr:2 * pr + 1], v[:, 2 * pr + 1:2 * pr + 2])
                                        for pr in range(n_pairs)], axis=1)
            hi = v.astype(BF16)
            lo = (v - hi.astype(F32)).astype(BF16)
            return _dot(hi, lanes_of_head) + _dot(lo, lanes_of_head)

        xs = xs_ref[...].astype(F32)
        xt = xs * per_head(dt, first)
        xt16 = xt.astype(BF16)
        hstate = jnp.concatenate([h_ref[pr] for pr in range(n_pairs)], axis=1)
        for pr in range(n_pairs):
            hp_ref[pr] = h_ref[pr]
        y_off = per_head(exp_cs, first) * _dot(c16, hstate.astype(BF16))
        new = per_head(exp_tot, first_row) * hstate + _dot_tn(b16, (per_head(exp_rest, first) * xt).astype(BF16))
        for pr in range(n_pairs):
            h_ref[pr] = new[:, pr * 2 * HD:(pr + 1) * 2 * HD]
        y_diag = []
        for pr in range(n_pairs):
            cols = slice(pr * 2 * HD, (pr + 1) * 2 * HD)
            m2 = jnp.concatenate(
                [(cb * jnp.exp(jnp.where(tril, cs[:, h:h + 1] - cs_t[h:h + 1, :], NEG))).astype(BF16)
                 for h in (2 * pr, 2 * pr + 1)], axis=1)
            y_diag.append(_dot(m2, _stack_heads(xt16[:, cols], first)))
        y = jnp.concatenate(y_diag, axis=1) + y_off + xs * per_head(dsk_ref[...], first_row)
        y_ref[...] = y
        zv = z_ref[...].astype(F32)
        yz = y * (zv * _sigmoid(zv))
        r = lax.rsqrt(jnp.mean(yz * yz, axis=-1, keepdims=True) + EPS)
        sin_ref[...] = (yz * r * gn_ref[...]).astype(BF16)

    gps = SSD_GROUPS_PER_STEP
    wide, narrow, lead = ("cols", GROUP_W), ("cols", BLK), ("lead", 1)
    kinds = [wide, narrow, narrow, ("each", 0)] + [lead] * 7 + [wide, wide, wide, lead, ("lead", 4)]
    wide_w, narrow_w = GROUP_W * gps, BLK * gps
    gparam = pl.BlockSpec((gps, 1, 8), lambda g, c: (g, 0, 0))
    gparam_t = pl.BlockSpec((gps, 8, 1), lambda g, c: (g, 0, 0))
    z_specs = [pl.BlockSpec((BLK, GROUP_W), functools.partial(lambda g, c, gi: (c, COL_Z // GROUP_W + gps * g + gi),
                                                              gi=gi)) for gi in range(gps)]
    return _call(
        _per_group(body, gps, kinds), name=name,
        out_shape=[_sds((s, SSD_INNER), F32), _sds((s, SSD_INNER), BF16),
                   _sds((SSD_GROUPS, nc, 4, BLK, 2 * HD), F32)],
        grid=(SSD_GROUPS // gps, nc),
        in_specs=[pl.BlockSpec((BLK, wide_w), lambda g, c: (c, g)),
                  pl.BlockSpec((BLK, narrow_w), lambda g, c: (c, SSD_INNER // narrow_w + g)),
                  pl.BlockSpec((BLK, narrow_w), lambda g, c: (c, (SSD_INNER + SSD_GROUPS * BLK) // narrow_w + g)),
                  *z_specs,
                  pl.BlockSpec((gps, BLK, 8), lambda g, c: (g, c, 0)),
                  pl.BlockSpec((gps, 8, BLK), lambda g, c: (g, 0, c)),
                  gparam, gparam_t, gparam, gparam_t, gparam,
                  pl.BlockSpec((1, wide_w), lambda g, c: (0, g))],
        out_specs=[pl.BlockSpec((BLK, wide_w), lambda g, c: (c, g)),
                   pl.BlockSpec((BLK, wide_w), lambda g, c: (c, g)),
                   pl.BlockSpec((gps, None, 4, BLK, 2 * HD), lambda g, c: (g, c, 0, 0, 0))],
        scratch_shapes=[pltpu.VMEM((4 * gps, BLK, 2 * HD), F32)],
    )(xc, xc, xc, *([p] * gps), dtg, dtg_t, bias, bias_t, alog, alog_t, dskip, gn)


def ssd_bwd(p, xc, y, d_sin, hprev, dtg, dtg_t, params, gn, name):
    s = p.shape[0]
    nc = s // BLK
    bias, bias_t, alog, alog_t, dskip = params

    def body(xs_ref, b_ref, c_ref, z_ref, y_ref, dsin_ref, hp_ref, dtr_ref, dtrt_ref, bias_ref,
             biast_ref, alog_ref, alogt_ref, dsk_ref, gn_ref,
             dxs_ref, db_ref, dc_ref, dz_ref, ddt_ref, da_ref, dbias_ref, ddsk_ref, dgn_ref, dh_ref):
        c_idx = pl.program_id(1)

        @pl.when(c_idx == 0)
        def _():
            dh_ref[...] = jnp.zeros_like(dh_ref)
            da_ref[...] = jnp.zeros_like(da_ref)
            dbias_ref[...] = jnp.zeros_like(dbias_ref)
            ddsk_ref[...] = jnp.zeros_like(ddsk_ref)
            dgn_ref[...] = jnp.zeros_like(dgn_ref)

        dtb, dt, cs, cs_t, row, col, upper = _ssd_decays(dtr_ref, dtrt_ref, bias_ref, biast_ref, alog_ref, alogt_ref)
        first = _lane_first_half((BLK, 2 * HD))
        second = jnp.logical_not(first)
        first_row = _lane_first_half((1, 2 * HD))
        tril = row >= col
        triu = row <= col
        last_row = lax.broadcasted_iota(jnp.int32, (BLK, 1), 0) == BLK - 1
        lane8 = lax.broadcasted_iota(jnp.int32, (BLK, 8), 1)

        yv = y_ref[...]
        zv = z_ref[...].astype(F32)
        sg = _sigmoid(zv)
        yz = yv * (zv * sg)
        r = lax.rsqrt(jnp.mean(yz * yz, axis=-1, keepdims=True) + EPS)
        yzn = yz * r
        dsn = dsin_ref[...]
        dgn_ref[...] += jnp.sum(dsn * yzn, axis=0, keepdims=True)
        dsn = dsn * gn_ref[...]
        dyz = r * (dsn - yzn * jnp.mean(dsn * yzn, axis=-1, keepdims=True))
        dy = dyz * (zv * sg)
        dz_ref[...] = (dyz * yv * (sg * (1.0 + zv * (1.0 - sg)))).astype(BF16)
        xs_all = xs_ref[...].astype(F32)
        ddsk_ref[...] += jnp.sum(dy * xs_all, axis=0, keepdims=True)

        b16, c16 = b_ref[...], c_ref[...]
        cb = _dot_nt(c16, b16)
        cb_t = _dot_nt(b16, c16)
        n_pairs = GROUP_W // (2 * HD)
        tot = cs[BLK - 1:BLK, :]
        exp_cs, exp_rest, exp_tot = jnp.exp(cs), jnp.exp(tot - cs), jnp.exp(tot)

        lanes_of_head = (lax.broadcasted_iota(jnp.int32, (8, GROUP_W), 1) // HD
                         == lax.broadcasted_iota(jnp.int32, (8, GROUP_W), 0)).astype(BF16)

        def per_head(v, mask):
            if v.shape[0] == 1:
                return jnp.concatenate([jnp.where(mask, v[:, 2 * pr:2 * pr + 1], v[:, 2 * pr + 1:2 * pr + 2])
                                        for pr in range(n_pairs)], axis=1)
            hi = v.astype(BF16)
            lo = (v - hi.astype(F32)).astype(BF16)
            return _dot(hi, lanes_of_head) + _dot(lo, lanes_of_head)

        head_of_lane = (lax.broadcasted_iota(jnp.int32, (GROUP_W, 8), 0) // HD
                        == lax.broadcasted_iota(jnp.int32, (GROUP_W, 8), 1)).astype(BF16)

        def head_sums(v):
            hi = v.astype(BF16)
            lo = (v - hi.astype(F32)).astype(BF16)
            return _dot(hi, head_of_lane) + _dot(lo, head_of_lane)

        dt_w, e_w, f_w = per_head(dt, first), per_head(exp_cs, first), per_head(exp_rest, first)
        xt = xs_all * dt_w
        xt16 = xt.astype(BF16)
        hstate = jnp.concatenate([hp_ref[pr] for pr in range(n_pairs)], axis=1)
        h16 = hstate.astype(BF16)
        dhn = jnp.concatenate([dh_ref[pr] for pr in range(n_pairs)], axis=1)
        dhn16 = dhn.astype(BF16)
        edy16 = (e_w * dy).astype(BF16)
        y_off = e_w * _dot(c16, h16)
        dcs_all = head_sums(dy * y_off)
        dc_acc = _dot_nt(edy16, h16)
        zmat = _dot(b16, dhn16)
        t_all = head_sums(zmat * xt) * exp_rest
        hh_rows = jnp.sum(head_sums(dhn * hstate), axis=0, keepdims=True)
        dtot = jnp.sum(t_all, axis=0, keepdims=True) + hh_rows * exp_tot
        dcs_all = dcs_all - t_all + jnp.where(last_row, dtot, 0.0)
        fxt16 = (f_w * xt).astype(BF16)
        db_acc = _dot_nt(fxt16, dhn16)
        dh_new = _dot_tn(c16, edy16) + per_head(exp_tot, first_row) * dhn
        for pr in range(n_pairs):
            dh_ref[pr] = dh_new[:, pr * 2 * HD:(pr + 1) * 2 * HD]
        g_sum = jnp.zeros((BLK, BLK), F32)
        gt_sum = jnp.zeros((BLK, BLK), F32)
        d_xt_parts = []
        for pr in range(n_pairs):
            cols = slice(pr * 2 * HD, (pr + 1) * 2 * HD)
            dym2 = _stack_heads(dy[:, cols].astype(BF16), first)
            d_m2 = _dot_nt(dym2, xt16[:, cols])
            d_mt2 = _dot_nt(xt16[:, cols], dym2)
            mt2 = []
            for e, h in enumerate((2 * pr, 2 * pr + 1)):
                cs_c, cs_r = cs[:, h:h + 1], cs_t[h:h + 1, :]
                decay = jnp.exp(jnp.where(tril, cs_c - cs_r, NEG))
                decay_t = jnp.exp(jnp.where(triu, cs_r - cs_c, NEG))
                gm = d_m2[e * BLK:(e + 1) * BLK] * decay
                gmt = d_mt2[:, e * BLK:(e + 1) * BLK] * decay_t
                g_sum += gm
                gt_sum += gmt
                dcs_h = jnp.sum(gm * cb, axis=-1, keepdims=True) - jnp.sum(gmt * cb_t, axis=-1, keepdims=True)
                dcs_all = dcs_all + jnp.where(lane8 == h, dcs_h, 0.0)
                mt2.append((cb_t * decay_t).astype(BF16))
            d_xt_parts.append(_dot(jnp.concatenate(mt2, axis=1), dym2))
        d_xt = jnp.concatenate(d_xt_parts, axis=1) + f_w * zmat
        dxs_ref[...] = dy * per_head(dsk_ref[...], first_row) + d_xt * dt_w
        ddtx_all = head_sums(d_xt * xs_all)

        dc_ref[...] = dc_acc + _dot(g_sum.astype(BF16), b16)
        db_ref[...] = db_acc + _dot(gt_sum.astype(BF16), c16)
        d_a = _dot_hi(upper, dcs_all)
        a_neg = -jnp.exp(alog_ref[...])
        ddt = ddtx_all + d_a * a_neg
        da_ref[...] += jnp.sum(d_a * dt, axis=0, keepdims=True)
        ddtr = ddt * _sigmoid(dtb)
        ddt_ref[...] = ddtr
        dbias_ref[...] += jnp.sum(ddtr, axis=0, keepdims=True)

    gps = SSD_GROUPS_PER_STEP
    k_wide, k_narrow, k_lead = ("cols", GROUP_W), ("cols", BLK), ("lead", 1)
    kinds = ([k_wide, k_narrow, k_narrow, ("each", 0), k_wide, k_wide] + [k_lead] * 8 + [k_wide]
             + [k_wide, k_narrow, k_narrow, k_wide] + [k_lead] * 4 + [k_wide] + [("lead", 4)])
    wide_w, narrow_w = GROUP_W * gps, BLK * gps
    rc = lambda c: nc - 1 - c
    gparam = pl.BlockSpec((gps, 1, 8), lambda g, c: (g, 0, 0))
    gparam_t = pl.BlockSpec((gps, 8, 1), lambda g, c: (g, 0, 0))
    wide = pl.BlockSpec((BLK, wide_w), lambda g, c: (rc(c), g))
    narrow = pl.BlockSpec((BLK, narrow_w), lambda g, c: (rc(c), g))
    z_specs = [pl.BlockSpec((BLK, GROUP_W),
                            functools.partial(lambda g, c, gi: (rc(c), COL_Z // GROUP_W + gps * g + gi), gi=gi))
               for gi in range(gps)]
    return _call(
        _per_group(body, gps, kinds), name=name,
        out_shape=[_sds((s, SSD_INNER), F32), _sds((s, GROUP_W), F32), _sds((s, GROUP_W), F32),
                   _sds((s, SSD_INNER), BF16), _sds((SSD_GROUPS, s, 8), F32),
                   _sds((SSD_GROUPS, 1, 8), F32), _sds((SSD_GROUPS, 1, 8), F32),
                   _sds((SSD_GROUPS, 1, GROUP_W), F32), _sds((1, SSD_INNER), F32)],
        grid=(SSD_GROUPS // gps, nc),
        in_specs=[wide,
                  pl.BlockSpec((BLK, narrow_w), lambda g, c: (rc(c), SSD_INNER // narrow_w + g)),
                  pl.BlockSpec((BLK, narrow_w), lambda g, c: (rc(c), (SSD_INNER + SSD_GROUPS * BLK) // narrow_w + g)),
                  *z_specs,
                  wide, wide,
                  pl.BlockSpec((gps, None, 4, BLK, 2 * HD), lambda g, c: (g, rc(c), 0, 0, 0)),
                  pl.BlockSpec((gps, BLK, 8), lambda g, c: (g, rc(c), 0)),
                  pl.BlockSpec((gps, 8, BLK), lambda g, c: (g, 0, rc(c))),
                  gparam, gparam_t, gparam, gparam_t, gparam,
                  pl.BlockSpec((1, wide_w), lambda g, c: (0, g))],
        out_specs=[wide, narrow, narrow, wide,
                   pl.BlockSpec((gps, BLK, 8), lambda g, c: (g, rc(c), 0)),
                   gparam, gparam,
                   pl.BlockSpec((gps, 1, GROUP_W), lambda g, c: (g, 0, 0)),
                   pl.BlockSpec((1, wide_w), lambda g, c: (0, g))],
        scratch_shapes=[pltpu.VMEM((4 * gps, BLK, 2 * HD), F32)],
    )(xc, xc, xc, *([p] * gps), y, d_sin, hprev, dtg, dtg_t, bias, bias_t, alog, alog_t, dskip, gn)


def merge_fwd(p, a, sbr, name, tm=512):
    s = p.shape[0]
    nj = D_MODEL // GROUP_W

    def body(ga_ref, gs_ref, a_ref, s_ref, o_ref):
        o_ref[...] = (_sigmoid(ga_ref[...].astype(F32)) * a_ref[...]
                      + _sigmoid(gs_ref[...].astype(F32)) * s_ref[...]).astype(BF16)

    blk = pl.BlockSpec((tm, GROUP_W), lambda i, j: (i, j))
    return _call(body, name=name, out_shape=_sds((s, D_MODEL), BF16), grid=(s // tm, nj),
                 in_specs=[pl.BlockSpec((tm, GROUP_W), lambda i, j: (i, COL_GA // GROUP_W + j)),
                           pl.BlockSpec((tm, GROUP_W), lambda i, j: (i, COL_GS // GROUP_W + j)), blk, blk],
                 out_specs=blk)(p, p, a, sbr)


def merge_bwd(p, a, sbr, dmerged, name, tm=512):
    s = p.shape[0]
    nj = D_MODEL // GROUP_W

    def body(ga_ref, gs_ref, a_ref, s_ref, dm_ref, da_ref, ds_ref, dga_ref, dgs_ref):
        dm = dm_ref[...]
        sa = _sigmoid(ga_ref[...].astype(F32))
        ss = _sigmoid(gs_ref[...].astype(F32))
        da_ref[...] = (dm * sa).astype(BF16)
        ds_ref[...] = (dm * ss).astype(BF16)
        dga_ref[...] = (dm * a_ref[...] * sa * (1.0 - sa)).astype(BF16)
        dgs_ref[...] = (dm * s_ref[...] * ss * (1.0 - ss)).astype(BF16)

    blk = pl.BlockSpec((tm, GROUP_W), lambda i, j: (i, j))
    shp = _sds((s, D_MODEL), BF16)
    return _call(body, name=name, out_shape=[shp] * 4, grid=(s // tm, nj),
                 in_specs=[pl.BlockSpec((tm, GROUP_W), lambda i, j: (i, COL_GA // GROUP_W + j)),
                           pl.BlockSpec((tm, GROUP_W), lambda i, j: (i, COL_GS // GROUP_W + j)), blk, blk, blk],
                 out_specs=[blk] * 4)(p, p, a, sbr, dmerged)


def _group_major(v):
    return v.reshape(SSD_GROUPS, 1, 8), v.reshape(SSD_GROUPS, 8, 1)


def mixer_forward(x, w, rider=None, later_weights=None):
    s = x.shape[0]
    h = rms_fwd(x, w["mix_norm"], "mix_rms")
    p = matmul_nn(h, w["w_in_main"], "mix_proj", BF16, tm=1024, tn=512, rider=rider)
    rode = None
    if rider is not None:
        p, rode = p
        w = dict(w, **later_weights(rode))
    dt_raw = matmul_nn(h, w["w_in_dt"], "mix_proj_dt", F32, tm=1024, tn=DT_PAD)
    qn, kn = qk_norm_fwd(p, w["q_gain"], w["k_gain"], "qk_norm")
    streams, os_, lses = [], [], []
    for g, d in enumerate(ATTN_DILATIONS):
        cols = slice(g * GROUP_W, (g + 1) * GROUP_W)
        qs, ks = _to_streams(qn[:, cols], d), _to_streams(kn[:, cols], d)
        vs = _to_streams(p[:, COL_V + g * GROUP_W:COL_V + (g + 1) * GROUP_W], d)
        o, lse = attn_fwd2(qs, ks, vs, g, f"attn_fwd{g}")
        streams.append((qs, ks, vs, lse))
        os_.append(_from_streams(o, d))
        lses.append(_from_streams(lse, d))
    attn_o = attn_merge_fwd(os_, lses, "attn_merge")
    cpre, xc = conv_fwd(p, w["conv_w"], w["conv_b"], "conv_fwd")
    dtg = dt_raw[:, :SSD_HEADS].reshape(s, SSD_GROUPS, 8).transpose(1, 0, 2)
    dtg_t = dtg.transpose(0, 2, 1)
    params = (*_group_major(w["dt_bias"]), *_group_major(w["a_log"]), _group_major(w["d_skip"])[0])
    y, s_in, hprev = ssd_fwd(p, xc, dtg, dtg_t, params, w["ssd_norm"], "ssd_fwd")
    a = matmul_nn(attn_o, w["w_attn_branch"], "attn_branch", F32, tm=1024, tn=512)
    sbr = matmul_nn(s_in, w["w_ssd_branch"], "ssd_branch", F32, tm=1024, tn=512)
    merged = merge_fwd(p, a, sbr, "merge")
    x_out = matmul_nn(merged, w["w_out"], "mix_out", F32, tm=1024, tn=512, res=x)
    saved = dict(h=h, p=p, streams=streams, os=os_, lses=lses, attn_o=attn_o, cpre=cpre, xc=xc, dtg=dtg,
                 dtg_t=dtg_t, params=params, y=y, s_in=s_in, hprev=hprev, a=a, sbr=sbr, merged=merged, w=w)
    return x_out, saved, rode


def mixer_backward(dx_out, x, sv, ride_early=None, ride_late=None):
    s = x.shape[0]
    p = sv["p"]
    w = sv["w"]
    g = {}
    dmerged = matmul_nt(dx_out, w["w_out"], "d_merged", F32, tm=1024, tn=512, tk=1024)
    g["w_out"] = matmul_tn(sv["merged"], dx_out, "dw_out", tn=512, ts=1024)
    da, ds, dga, dgs = merge_bwd(p, sv["a"], sv["sbr"], dmerged, "merge_bwd")
    g["w_attn_branch"] = matmul_tn(sv["attn_o"], da, "dw_attn_branch", tn=512, ts=1024)
    g["w_ssd_branch"] = matmul_tn(sv["s_in"], ds, "dw_ssd_branch", tn=512, ts=1024)
    d_attn_o = matmul_nt(da, w["w_attn_branch"], "d_attn_o", F32, tm=1024, tn=512, tk=1024)
    d_sin = matmul_nt(ds, w["w_ssd_branch"], "d_ssd_in", F32, tm=1024, tn=512, tk=1024)
    dxs, d_b, d_c, dz, ddt, d_asum, d_bias, d_dsk, d_gn = ssd_bwd(
        p, sv["xc"], sv["y"], d_sin, sv["hprev"], sv["dtg"], sv["dtg_t"], sv["params"], w["ssd_norm"], "ssd_bwd")
    dxbc, d_convw, d_convb = conv_bwd(p, sv["cpre"], dxs, d_b, d_c, w["conv_w"], "conv_bwd")
    g["conv_w"] = d_convw[:SSD_CONV]
    g["conv_b"] = d_convb
    g["dt_bias"] = d_bias.reshape(1, SSD_HEADS)
    g["a_log"] = (d_asum * (-jnp.exp(sv["params"][2]))).reshape(1, SSD_HEADS)
    g["d_skip"] = jnp.sum(d_dsk.reshape(SSD_HEADS, HD), axis=1).reshape(1, SSD_HEADS)
    g["ssd_norm"] = d_gn
    merged_bwd = attn_merge_bwd(d_attn_o, sv["os"], sv["lses"], "attn_merge_bwd")
    dqs, dks, dvs = [], [], []
    for gi, d in enumerate(ATTN_DILATIONS):
        qs, ks, vs, lse = sv["streams"][gi]
        d_o = _to_streams(merged_bwd[gi], d)
        cterm = _to_streams(merged_bwd[3 + gi], d)
        dq, dk, dv = attn_bwd2(qs, ks, vs, d_o, cterm, lse, gi, f"attn_bwd{gi}")
        dqs.append(_from_streams(dq, d))
        dks.append(_from_streams(dk, d))
        dvs.append(_from_streams(dv, d).astype(BF16))
    dqk, d_qg, d_kg = qk_norm_bwd(p, dqs, dks, w["q_gain"], w["k_gain"], "qk_norm_bwd")
    g["q_norm"] = jnp.sum(d_qg.reshape(N_ATTN_HEADS, HD), axis=0).reshape(1, HD)
    g["k_norm"] = jnp.sum(d_kg.reshape(N_ATTN_HEADS, HD), axis=0).reshape(1, HD)
    dp = [dqk, jnp.concatenate(dvs, axis=1), dz, dxbc, dga, dgs]
    ddt_pad = jnp.pad(ddt.transpose(1, 0, 2).reshape(s, SSD_HEADS), ((0, 0), (0, DT_PAD - SSD_HEADS)))
    if ride_early is not None:
        g["w_in_main"], g["rode_early"] = matmul_tn_pieces(sv["h"], dp, "dw_in", tn=512, ts=1024,
                                                           rider=ride_early(g))
    else:
        g["w_in_main"] = matmul_tn_pieces(sv["h"], dp, "dw_in", tn=512, ts=1024)
    g["w_in_dt"] = matmul_tn(sv["h"], ddt_pad, "dw_in_dt", tn=DT_PAD, ts=1024)
    if ride_late is not None:
        dh_main, g["rode_late"] = matmul_nt_pieces(dp, w["w_in_main"], "d_mix_h", F32, tm=1024, tn=1024, tk=512,
                                                   rider=ride_late(g))
    else:
        dh_main = matmul_nt_pieces(dp, w["w_in_main"], "d_mix_h", F32, tm=1024, tn=512, tk=512)
    dh_dt = matmul_nt(ddt_pad, w["w_in_dt"], "d_mix_h_dt", F32, tm=1024, tn=1024, tk=DT_PAD)
    dx, g["mix_norm"] = rms_bwd([dh_main, dh_dt], x, w["mix_norm"], dx_out, "mix_drms")
    return dx, g


ANY = pl.BlockSpec(memory_space=pl.ANY)


def _place():
    x, y, c = lax.axis_index("x"), lax.axis_index("y"), lax.axis_index("c")
    chips = [(1 - x, y), (x, 1 - y), (1 - x, 1 - y)]
    return x, y, c, 2 * x + y, chips


def _comm_call(body, *, name, out_shape, n_in, scratch_shapes, aliases=None):
    return pl.pallas_call(
        body, out_shape=out_shape, in_specs=[ANY] * n_in, out_specs=[ANY] * len(out_shape),
        scratch_shapes=scratch_shapes, input_output_aliases=aliases or {}, name=name,
        compiler_params=pltpu.CompilerParams(has_side_effects=True))


def gather_weights(shards, small):
    n = len(shards)
    halves = [a.shape[0] // 2 for a in shards]
    out_shape = [_sds((N_CHIP,) + a.shape, a.dtype) for a in shards] + [_sds((N_CHIP,) + small.shape, small.dtype)]

    def body(*refs):
        ins, outs = refs[:n + 1], refs[n + 1:2 * n + 2]
        send1, recv1, send2, recv2, local = refs[2 * n + 2:]
        x, y, c, me, chips = _place()
        sibling = (x, y, 1 - c)

        def rows(k, chip, core):
            if k == n:
                return outs[k].at[chip]
            return outs[k].at[chip, pl.ds(core * halves[k], halves[k])]

        def level1(k, t, incoming):
            chip = 2 * chips[t][0] + chips[t][1]
            src = ins[k] if k == n else ins[k].at[pl.ds(c * halves[k], halves[k])]
            return pltpu.make_async_remote_copy(
                src_ref=src, dst_ref=rows(k, chip if incoming else me, c), send_sem=send1.at[3 * k + t],
                recv_sem=recv1.at[3 * k + t], device_id=(*chips[t], c), device_id_type=MESH)

        def level2(k, t, incoming):
            chip = 2 * chips[t][0] + chips[t][1]
            core = (1 - c) if incoming else c
            return pltpu.make_async_remote_copy(
                src_ref=rows(k, chip, core), dst_ref=rows(k, chip, core), send_sem=send2.at[3 * k + t],
                recv_sem=recv2.at[3 * k + t], device_id=sibling, device_id_type=MESH)

        own = [pltpu.make_async_copy(ins[k], outs[k].at[me], local.at[k]) for k in range(n + 1)]
        for cp in own:
            cp.start()
        first = [level1(k, t, False) for k in range(n + 1) for t in range(3)]
        for cp in first:
            cp.start()
        passed = []
        for k in range(n + 1):
            for t in range(3):
                level1(k, t, True).wait_recv()
                if k < n:
                    cp = level2(k, t, False)
                    cp.start()
                    passed.append(cp)
        for k in range(n):
            for t in range(3):
                level2(k, t, True).wait_recv()
        for cp in first + passed:
            cp.wait_send()
        for cp in own:
            cp.wait()

    dma = pltpu.SemaphoreType.DMA
    return _comm_call(body, name="gather_weights", out_shape=out_shape, n_in=n + 1,
                      scratch_shapes=[dma((3 * n + 3,)), dma((3 * n + 3,)), dma((3 * n,)), dma((3 * n,)),
                                      dma((n + 1,))])(*shards, small)


def reduce_to_sibling(grads):
    n = len(grads)
    halves = [a.shape[1] // 2 for a in grads]
    shapes = [_sds((N_CHIP, h, a.shape[2]), a.dtype) for a, h in zip(grads, halves)]

    def body(*refs):
        ins, got, kept = refs[:n], refs[n:2 * n], refs[2 * n:3 * n]
        send, recv, local = refs[3 * n:]
        x, y, c, _, _ = _place()
        copies, locals_ = [], []
        for k in range(n):
            h = halves[k]
            locals_.append(pltpu.make_async_copy(ins[k].at[:, pl.ds(c * h, h)], kept[k], local.at[k]))
            copies.append(pltpu.make_async_remote_copy(
                src_ref=ins[k].at[:, pl.ds((1 - c) * h, h)], dst_ref=got[k], send_sem=send.at[k], recv_sem=recv.at[k],
                device_id=(x, y, 1 - c), device_id_type=MESH))
        for cp in locals_ + copies:
            cp.start()
        for cp in copies:
            cp.wait_recv()
        for cp in copies:
            cp.wait_send()
        for cp in locals_:
            cp.wait()

    dma = pltpu.SemaphoreType.DMA
    res = _comm_call(body, name="reduce_to_sibling", out_shape=shapes + shapes, n_in=n,
                     scratch_shapes=[dma((n,)), dma((n,)), dma((n,))])(*grads)
    return res[:n], res[n:]


def reduce_to_owner(sums):
    n = len(sums)
    shapes = [_sds(a.shape, a.dtype) for a in sums]

    def body(*refs):
        ins, outs = refs[:n], refs[n:2 * n]
        send, recv, local = refs[2 * n:]
        x, y, c, me, chips = _place()
        copies, locals_ = [], []
        for k in range(n):
            locals_.append(pltpu.make_async_copy(ins[k].at[me], outs[k].at[3], local.at[k]))
            for t in range(3):
                chip = 2 * chips[t][0] + chips[t][1]
                copies.append(pltpu.make_async_remote_copy(
                    src_ref=ins[k].at[chip], dst_ref=outs[k].at[t], send_sem=send.at[3 * k + t],
                    recv_sem=recv.at[3 * k + t], device_id=(*chips[t], c), device_id_type=MESH))
        for cp in locals_ + copies:
            cp.start()
        for cp in copies:
            cp.wait_recv()
        for cp in copies:
            cp.wait_send()
        for cp in locals_:
            cp.wait()

    dma = pltpu.SemaphoreType.DMA
    return _comm_call(body, name="reduce_to_owner", out_shape=shapes, n_in=n,
                      scratch_shapes=[dma((3 * n,)), dma((3 * n,)), dma((n,))])(*sums)


def share_with_sibling(halves_):
    n = len(halves_)
    shapes = [_sds((2 * a.shape[0], a.shape[1]), a.dtype) for a in halves_]

    def body(*refs):
        ins, outs = refs[:n], refs[n:2 * n]
        send, recv, local = refs[2 * n:]
        x, y, c, _, _ = _place()
        copies, locals_ = [], []
        for k in range(n):
            h = ins[k].shape[0]
            mine = outs[k].at[pl.ds(c * h, h)]
            locals_.append(pltpu.make_async_copy(ins[k], mine, local.at[k]))
            copies.append(pltpu.make_async_remote_copy(
                src_ref=ins[k], dst_ref=mine, send_sem=send.at[k], recv_sem=recv.at[k],
                device_id=(x, y, 1 - c), device_id_type=MESH))
        for cp in locals_ + copies:
            cp.start()
        for cp in copies:
            cp.wait_recv()
        for cp in copies:
            cp.wait_send()
        for cp in locals_:
            cp.wait()

    dma = pltpu.SemaphoreType.DMA
    return _comm_call(body, name="share_with_sibling", out_shape=shapes, n_in=n,
                      scratch_shapes=[dma((n,)), dma((n,)), dma((n,))])(*halves_)


def _cores():
    c = lax.axis_index("c")
    return jnp.stack([c, 1 - c]).astype(jnp.int32)


def _staged_call(body, *, name, grid, in_specs, out_specs, out_shape, scratch_shapes):
    return pl.pallas_call(
        body, out_shape=out_shape, name=name,
        grid_spec=pltpu.PrefetchScalarGridSpec(num_scalar_prefetch=1, grid=grid, in_specs=in_specs,
                                               out_specs=out_specs, scratch_shapes=scratch_shapes),
        compiler_params=pltpu.CompilerParams(dimension_semantics=("arbitrary",) * len(grid),
                                             vmem_limit_bytes=V7X_VMEM_LIMIT, has_side_effects=True))


def gather_rider(shards, tiles):
    dma = pltpu.SemaphoreType.DMA
    n = len(shards)
    geo = [(a.shape[0] // 2, tm, (a.shape[0] // 2) // tm) for a, tm in zip(shards, tiles)]
    scratch = []
    for a, (h, tm, nk) in zip(shards, geo):
        scratch += [pltpu.VMEM((N_CHIP,) + a.shape, a.dtype), dma((3, nk)), dma((3, nk)), dma((3, nk)), dma((3, nk)),
                    dma((nk + 2,))]

    def copies(j, in_ref, scr):
        buf, send1, recv1, send2, recv2, local = scr[6 * j:6 * j + 6]
        h, tm, nk = geo[j]
        x, y, c, me, chips = _place()
        chip_of = [2 * chips[t][0] + chips[t][1] for t in range(3)]

        def rows(chip, core, k):
            return buf.at[chip, pl.ds(core * h + k * tm, tm)]

        def mine(k):
            if k == nk:
                return pltpu.make_async_copy(in_ref.at[pl.ds((1 - c) * h, h)], buf.at[me, pl.ds((1 - c) * h, h)],
                                             local.at[nk])
            return pltpu.make_async_copy(in_ref.at[pl.ds(c * h + k * tm, tm)], rows(me, c, k), local.at[k])

        def level1(t, k, incoming):
            place = rows(chip_of[t] if incoming else me, c, k)
            return pltpu.make_async_remote_copy(src_ref=place, dst_ref=place, send_sem=send1.at[t, k],
                                                recv_sem=recv1.at[t, k], device_id=(*chips[t], c), device_id_type=MESH)

        def level2(t, k, incoming):
            place = rows(chip_of[t], (1 - c) if incoming else c, k)
            return pltpu.make_async_remote_copy(src_ref=place, dst_ref=place, send_sem=send2.at[t, k],
                                                recv_sem=recv2.at[t, k], device_id=(x, y, 1 - c),
                                                device_id_type=MESH)

        return buf, local, nk, mine, level1, level2

    def start(ins, outs, scr):
        for j in range(n):
            _, _, nk, mine, _, _ = copies(j, ins[j], scr)
            for k in range(nk + 1):
                mine(k).start()
        for j in range(n):
            _, _, nk, mine, level1, _ = copies(j, ins[j], scr)
            for k in range(nk):
                mine(k).wait()
                for t in range(3):
                    level1(t, k, False).start()

    def finish(ins, outs, scr):
        for j in range(n):
            _, _, nk, _, level1, level2 = copies(j, ins[j], scr)
            for k in range(nk):
                for t in range(3):
                    level1(t, k, True).wait_recv()
                    level2(t, k, False).start()
        for j in range(n):
            buf, local, nk, mine, level1, level2 = copies(j, ins[j], scr)
            for k in range(nk):
                for t in range(3):
                    level2(t, k, True).wait_recv()
            for k in range(nk):
                for t in range(3):
                    level1(t, k, False).wait_send()
                    level2(t, k, False).wait_send()
            mine(nk).wait()
            pltpu.make_async_copy(buf, outs[j], local.at[nk + 1]).start()
        for j in range(n):
            buf, local, nk, _, _, _ = copies(j, ins[j], scr)
            pltpu.make_async_copy(buf, outs[j], local.at[nk + 1]).wait()

    return Rider(list(shards), [_sds((N_CHIP,) + a.shape, a.dtype) for a in shards], scratch, start, finish)


def run_alone(rider, name):
    return _call(lambda: None, name=name, out_shape=[], in_specs=[], out_specs=[], grid=(1,), rider=rider)()[1]


def sibling_sum(g, tm, name):
    _, r, cdim = g.shape
    h = r // 2
    ni = h // tm
    dma = pltpu.SemaphoreType.DMA

    def body(cores_ref, keep_ref, give_ref, out_ref, slot, send, recv):
        par = (pl.program_id(0) * ni + pl.program_id(1)) % 2
        x, y, c, _, _ = _place()
        cp = pltpu.make_async_remote_copy(src_ref=give_ref, dst_ref=slot.at[par], send_sem=send.at[par],
                                          recv_sem=recv.at[par], device_id=(x, y, 1 - c), device_id_type=MESH)
        cp.start()
        cp.wait_recv()
        out_ref[...] = (keep_ref[...].astype(F32) + slot[par].astype(F32)).astype(out_ref.dtype)
        cp.wait_send()

    flat = g.reshape(N_CHIP * r, cdim)
    return _staged_call(
        body, name=name, grid=(N_CHIP, ni),
        in_specs=[pl.BlockSpec((tm, cdim), lambda j, i, cores: ((2 * j + cores[0]) * ni + i, 0)),
                  pl.BlockSpec((tm, cdim), lambda j, i, cores: ((2 * j + cores[1]) * ni + i, 0))],
        out_specs=pl.BlockSpec((None, tm, cdim), lambda j, i, cores: (j, i, 0)),
        out_shape=_sds((N_CHIP, h, cdim), g.dtype),
        scratch_shapes=[pltpu.VMEM((2, tm, cdim), g.dtype), dma((2,)), dma((2,))],
    )(_cores(), flat, flat)


def owner_sum_rider(sums, tiles):
    dma = pltpu.SemaphoreType.DMA
    n = len(sums)
    geo = [(a.shape[1], tm, a.shape[1] // tm) for a, tm in zip(sums, tiles)]
    scratch = []
    for a, (h, tm, nk) in zip(sums, geo):
        cdim = a.shape[2]
        scratch += [pltpu.VMEM(a.shape, a.dtype), pltpu.VMEM((3, h, cdim), a.dtype), pltpu.VMEM((2, h, cdim), F32),
                    dma((3, nk)), dma((3, nk)), dma((nk,)), dma((nk,)), dma((2,))]

    def copies(j, scr):
        part, got, res, send, recv, send2, recv2, local = scr[8 * j:8 * j + 8]
        h, tm, nk = geo[j]
        x, y, c, me, chips = _place()

        def to_owner(t, k):
            chip = 2 * chips[t][0] + chips[t][1]
            return pltpu.make_async_remote_copy(
                src_ref=part.at[chip, pl.ds(k * tm, tm)], dst_ref=got.at[t, pl.ds(k * tm, tm)],
                send_sem=send.at[t, k], recv_sem=recv.at[t, k], device_id=(*chips[t], c), device_id_type=MESH)

        def to_sibling(k):
            place = res.at[c, pl.ds(k * tm, tm)]
            return pltpu.make_async_remote_copy(src_ref=place, dst_ref=place, send_sem=send2.at[k],
                                                recv_sem=recv2.at[k], device_id=(x, y, 1 - c), device_id_type=MESH)

        return part, got, res, local, to_owner, to_sibling, (tm, nk, c, me)

    def start(ins, outs, scr):
        for j in range(n):
            part, _, _, local, _, _, _ = copies(j, scr)
            pltpu.make_async_copy(ins[j], part, local.at[0]).start()
        for j in range(n):
            part, _, _, local, to_owner, _, (tm, nk, c, me) = copies(j, scr)
            pltpu.make_async_copy(ins[j], part, local.at[0]).wait()
            for k in range(nk):
                for t in range(3):
                    to_owner(t, k).start()

    def finish(ins, outs, scr):
        for j in range(n):
            part, got, res, _, to_owner, to_sibling, (tm, nk, c, me) = copies(j, scr)
            for k in range(nk):
                rows = pl.ds(k * tm, tm)
                for t in range(3):
                    to_owner(t, k).wait_recv()
                acc = part[me, rows, :].astype(F32)
                for t in range(3):
                    acc = acc + got[t, rows, :].astype(F32)
                res[c, rows, :] = acc
                to_sibling(k).start()
        for j in range(n):
            _, _, res, local, to_owner, to_sibling, (tm, nk, c, me) = copies(j, scr)
            for k in range(nk):
                to_sibling(k).wait_recv()
            for k in range(nk):
                to_sibling(k).wait_send()
                for t in range(3):
                    to_owner(t, k).wait_send()
            pltpu.make_async_copy(res, outs[j], local.at[1]).start()
        for j in range(n):
            _, _, res, local, _, _, _ = copies(j, scr)
            pltpu.make_async_copy(res, outs[j], local.at[1]).wait()

    return Rider(list(sums), [_sds((2, a.shape[1], a.shape[2]), F32) for a in sums], scratch, start, finish)


def gather_conv_w(w):
    def body(in_ref, out_ref, send, recv):
        x, y, c, me, chips = _place()
        out_ref[me] = in_ref[...]
        copies = []
        for t in range(3):
            copies.append(pltpu.make_async_remote_copy(
                src_ref=out_ref.at[me], dst_ref=out_ref.at[me], send_sem=send.at[t], recv_sem=recv.at[t],
                device_id=(*chips[t], c), device_id_type=MESH))
        for cp in copies:
            cp.start()
        for cp in copies:
            cp.wait_recv()
        for cp in copies:
            cp.wait_send()

    dma = pltpu.SemaphoreType.DMA
    vmem = pl.BlockSpec(memory_space=pltpu.VMEM)
    return pl.pallas_call(
        body, out_shape=_sds((N_CHIP,) + w.shape, w.dtype), in_specs=[vmem], out_specs=vmem, name="gather_conv_w",
        scratch_shapes=[dma((3,)), dma((3,))],
        compiler_params=pltpu.CompilerParams(has_side_effects=True))(w)


N_DEV = 8
SMALL_ROWS = 32
SMALL_LANES = 1024


def all_reduce_small(arrays):
    n_arr = len(arrays)
    places = []
    for k, a in enumerate(arrays):
        for ri in range(a.shape[0]):
            for c0 in range(0, a.shape[1], SMALL_LANES):
                places.append((k, ri, c0, min(SMALL_LANES, a.shape[1] - c0), len(places)))
    assert len(places) <= SMALL_ROWS

    def body(*refs):
        ins, outs = refs[:n_arr], refs[n_arr:2 * n_arr]
        buf, send, recv = refs[2 * n_arr:]
        x, y, c, _, _ = _place()
        me = 4 * x + 2 * y + c
        buf[me] = jnp.zeros((SMALL_ROWS, SMALL_LANES), F32)
        for k, ri, c0, width, row in places:
            buf[me, row:row + 1, 0:width] = ins[k][ri:ri + 1, c0:c0 + width]
        copies = []
        for r in range(1, N_DEV):
            px = (1 - x) if r & 4 else x
            py = (1 - y) if r & 2 else y
            pc = (1 - c) if r & 1 else c
            copies.append(pltpu.make_async_remote_copy(
                src_ref=buf.at[me], dst_ref=buf.at[me], send_sem=send.at[r - 1], recv_sem=recv.at[r - 1],
                device_id=(px, py, pc), device_id_type=MESH))
        for cp in copies:
            cp.start()
        for cp in copies:
            cp.wait_recv()
        for cp in copies:
            cp.wait_send()
        acc = buf[0]
        for j in range(1, N_DEV):
            acc = acc + buf[j]
        for k, ri, c0, width, row in places:
            outs[k][ri:ri + 1, c0:c0 + width] = acc[row:row + 1, 0:width]

    dma = pltpu.SemaphoreType.DMA
    vmem = pl.BlockSpec(memory_space=pltpu.VMEM)
    return pl.pallas_call(
        body, out_shape=[_sds(a.shape, F32) for a in arrays], in_specs=[vmem] * n_arr, out_specs=[vmem] * n_arr,
        name="all_reduce_small",
        scratch_shapes=[pltpu.VMEM((N_DEV, SMALL_ROWS, SMALL_LANES), F32), dma((N_DEV - 1,)), dma((N_DEV - 1,))],
        compiler_params=pltpu.CompilerParams(has_side_effects=True))(*arrays)


def _row_tile(rows, limit, multiple):
    return max(t for t in range(multiple, min(rows, limit) + 1, multiple) if rows % t == 0)


def add_pair(a, b, name):
    _, h, c = a.shape

    def body(a_ref, b_ref, o_ref):
        o_ref[...] = (a_ref[...].astype(F32) + b_ref[...].astype(F32)).astype(o_ref.dtype)

    blk = pl.BlockSpec((None, h, c), lambda j: (j, 0, 0))
    return _call(body, name=name, out_shape=_sds(a.shape, a.dtype), grid=(N_CHIP,), in_specs=[blk, blk],
                 out_specs=blk)(a, b)


def sum_slots(buf, name):
    _, h, c = buf.shape
    tm = _row_tile(h, 256, 16)

    def body(b_ref, o_ref):
        acc = b_ref[3].astype(F32)
        for t in range(3):
            acc = acc + b_ref[t].astype(F32)
        o_ref[...] = acc

    return _call(body, name=name, out_shape=_sds((h, c), F32), grid=(h // tm,),
                 in_specs=[pl.BlockSpec((N_CHIP, tm, c), lambda i: (0, i, 0))],
                 out_specs=pl.BlockSpec((tm, c), lambda i: (i, 0)))(buf)


def _adamw_math(w, g, m, v):
    c1 = 1.0 - ADAM_B1 ** ADAM_STEP
    c2 = 1.0 - ADAM_B2 ** ADAM_STEP
    m2 = ADAM_B1 * m + (1.0 - ADAM_B1) * g
    v2 = ADAM_B2 * v + (1.0 - ADAM_B2) * (g * g)
    return -ADAM_LR * ((m2 / c1) / (jnp.sqrt(v2 / c2) + ADAM_EPS) + ADAM_WD * w), m2, v2


def adamw(w, g, row_off, m, v, name):
    _, r, c = w.shape
    tm = r if r < 8 else _row_tile(math.gcd(r, row_off) if row_off else r, 128, 8)

    def body(w_ref, g_ref, m_ref, v_ref, go_ref, d_ref, m2_ref, v2_ref):
        gv = g_ref[...]
        go_ref[...] = gv
        d_ref[...], m2_ref[...], v2_ref[...] = _adamw_math(w_ref[...], gv, m_ref[...], v_ref[...])

    blk = pl.BlockSpec((None, tm, c), lambda i: (0, i, 0))
    shp = _sds((1, r, c), F32)
    return _call(body, name=name, out_shape=[shp] * 4, grid=(r // tm,),
                 in_specs=[blk, pl.BlockSpec((tm, c), lambda i: (row_off // tm + i, 0)), blk, blk],
                 out_specs=[blk] * 4)(w, g, m, v)


def adamw_small(ws, gs, ms, vs):
    n = len(ws)

    def body(*refs):
        ins, outs = refs[:4 * n], refs[4 * n:]
        for k in range(n):
            w_ref, g_ref, m_ref, v_ref = (ins[j * n + k] for j in range(4))
            outs[k][...], outs[n + k][...], outs[2 * n + k][...] = _adamw_math(w_ref[...], g_ref[...], m_ref[...],
                                                                               v_ref[...])

    vmem = pl.BlockSpec(memory_space=pltpu.VMEM)
    shapes = [_sds(w.shape, F32) for w in ws] * 3
    res = pl.pallas_call(body, out_shape=shapes, in_specs=[vmem] * (4 * n), out_specs=[vmem] * (3 * n),
                         name="adamw_small")(*ws, *gs, *ms, *vs)
    return res[:n], res[n:2 * n], res[2 * n:]


BIG = ("ffn1_w_gate", "ffn1_w_up", "ffn1_w_down", "w_in", "w_attn_branch", "w_ssd_branch", "w_out",
       "ffn2_w_gate", "ffn2_w_up", "ffn2_w_down")
SMALL = ("ffn1_norm", "mix_norm", "q_norm", "k_norm", "conv_b", "dt_bias", "a_log", "d_skip", "ssd_norm", "ffn2_norm")
WEIGHTS = ("ffn1_norm", "ffn1_w_gate", "ffn1_w_up", "ffn1_w_down", "mix_norm", "w_in", "q_norm", "k_norm", "conv_w",
           "conv_b", "dt_bias", "a_log", "d_skip", "ssd_norm", "w_attn_branch", "w_ssd_branch", "w_out", "ffn2_norm",
           "ffn2_w_gate", "ffn2_w_up", "ffn2_w_down")
CONV_SHARD = SSD_CONV_DIM // N_CHIP
CLASSES = {
    "ffn1_in": (("ffn1_w_gate", 1024), ("ffn1_w_up", 1024)),
    "ffn1_out": (("ffn1_w_down", 704),),
    "mix_in": (("w_in", 1024),),
    "mix_attn": (("w_attn_branch", 512),),
    "late_out": (("ffn2_w_down", 704), ("w_ssd_branch", 512), ("w_out", 256)),
    "ffn2_in": (("ffn2_w_gate", 1024), ("ffn2_w_up", 1024)),
}
CLASS_TILE = {"ffn1_in": 256, "ffn1_out": 176, "mix_in": 128, "mix_attn": 256, "late_out": 368, "ffn2_in": 256,
              "mix_in_top": 128, "mix_in_bottom": 128}
SIBLING_TILE = {"ffn1_in": 1024, "ffn1_out": 352, "mix_attn": 256, "late_out": 736, "ffn2_in": 1024,
                "mix_in_top": 256, "mix_in_bottom": 256}


def _pack_small(vals, conv_part, loss_part=None):
    flat = [vals[k].reshape(-1) for k in SMALL]
    flat.append(jnp.zeros((SSD_CONV * SSD_CONV_DIM,), F32) if conv_part is None else conv_part.reshape(-1))
    flat.append(jnp.zeros((1,), F32) if loss_part is None else loss_part.reshape(1))
    flat = jnp.concatenate(flat)
    return jnp.pad(flat, (0, SMALL_ROWS * D_MODEL - flat.shape[0])).reshape(SMALL_ROWS, D_MODEL)


def _unpack_small(pack, like):
    flat = pack.reshape(-1)
    out, off = {}, 0
    for k in SMALL:
        n = like[k].size
        out[k] = flat[off:off + n].reshape(like[k].shape)
        off += n
    conv = flat[off:off + SSD_CONV * SSD_CONV_DIM].reshape(SSD_CONV, SSD_CONV_DIM)
    return out, conv, flat[off + SSD_CONV * SSD_CONV_DIM]


def _chip_major_cols(a):
    r = a.shape[0]
    return a.reshape(r, N_CHIP, -1).transpose(1, 0, 2)


def _from_chip_major_cols(a):
    return a.transpose(1, 0, 2).reshape(a.shape[1], -1)


def kernel(x, ffn1_norm, ffn1_w_gate, ffn1_w_up, ffn1_w_down, mix_norm, w_in, q_norm, k_norm, conv_w, conv_b, dt_bias, a_log, d_skip, ssd_norm, w_attn_branch, w_ssd_branch, w_out, ffn2_norm, ffn2_w_gate, ffn2_w_up, ffn2_w_down, loss_target, m_ffn1_norm, m_ffn1_w_gate, m_ffn1_w_up, m_ffn1_w_down, m_mix_norm, m_w_in, m_q_norm, m_k_norm, m_conv_w, m_conv_b, m_dt_bias, m_a_log, m_d_skip, m_ssd_norm, m_w_attn_branch, m_w_ssd_branch, m_w_out, m_ffn2_norm, m_ffn2_w_gate, m_ffn2_w_up, m_ffn2_w_down, v_ffn1_norm, v_ffn1_w_gate, v_ffn1_w_up, v_ffn1_w_down, v_mix_norm, v_w_in, v_q_norm, v_k_norm, v_conv_w, v_conv_b, v_dt_bias, v_a_log, v_d_skip, v_ssd_norm, v_w_attn_branch, v_w_ssd_branch, v_w_out, v_ffn2_norm, v_ffn2_w_gate, v_ffn2_w_up, v_ffn2_w_down):
    env = dict(locals())
    wts = {k: env[k] for k in WEIGHTS}
    moms = {k: env["m_" + k] for k in WEIGHTS}
    vars_ = {k: env["v_" + k] for k in WEIGHTS}
    x0 = x[0]
    target = loss_target[0]

    def gather(classes, more=()):
        shards = [jnp.concatenate([wts[k][0] for k, _ in CLASSES[c]], axis=0).astype(BF16) for c in classes]
        return gather_rider(shards + [a for a, _ in more], [CLASS_TILE[c] for c in classes] + [t for _, t in more])

    def reducer(classes, parts):
        sums = [sibling_sum(p, SIBLING_TILE[c], f"sibling_sum_{c}") for c, p in zip(classes, parts)]
        return owner_sum_rider(sums, [CLASS_TILE[c] for c in classes])

    half = D_MODEL // 2
    in_tile = CLASS_TILE["mix_in_top"]
    x1, saved1, (w_ffn1_in,), (w_ffn1_out, w_mix_attn, w_in_top), (w_in_bottom,) = ffn_forward(
        x0, ffn1_norm, lambda rode: rode[0], lambda rode: rode[0], "ffn1", rms_rider=gather(["ffn1_in"]),
        up_rider=gather(["ffn1_out", "mix_attn"], [(w_in[0, :half].astype(BF16), in_tile)]),
        down_rider=gather([], [(w_in[0, half:].astype(BF16), in_tile)]))
    dt0, dt1 = IN_DT0 - 3 * IN_SHARD, IN_DT1 - 3 * IN_SHARD

    def in_columns(w4):
        return jnp.concatenate([w4[0], w4[1], w4[2], w4[3][:, :dt0], w4[3][:, dt1:]], axis=1), w4[3][:, dt0:dt1]

    (main_top, dt_top), (main_bottom, dt_bottom) = in_columns(w_in_top), in_columns(w_in_bottom)
    mixer_w = dict(
        mix_norm=mix_norm,
        w_in_main=jnp.concatenate([main_top, main_bottom], axis=0),
        w_in_dt=jnp.pad(jnp.concatenate([dt_top, dt_bottom], axis=0), ((0, 0), (0, DT_PAD - SSD_HEADS))),
        q_gain=jnp.tile(q_norm, (1, 2)), k_gain=jnp.tile(k_norm, (1, 2)),
        conv_w=_from_chip_major_cols(gather_conv_w(conv_w[0])), conv_b=conv_b, dt_bias=dt_bias, a_log=a_log,
        d_skip=d_skip, ssd_norm=ssd_norm, w_attn_branch=_from_chip_major_cols(w_mix_attn))

    def later_weights(rode):
        late = rode[0]
        return dict(w_ssd_branch=late[:, 704:1216].reshape(SSD_INNER, D_MODEL),
                    w_out=late[:, 1216:1472].reshape(D_MODEL, D_MODEL))

    x2, saved_mix, (w_late_out, w_ffn2_in) = mixer_forward(x1, mixer_w, gather(["late_out", "ffn2_in"]), later_weights)
    (dx3, sq), saved2, _, _, _ = ffn_forward(x2, ffn2_norm, lambda rode: w_ffn2_in, lambda rode: w_late_out, "ffn2",
                                             target=target)

    grads = {}
    dx2, grads["ffn2_norm"], d_ffn2_in, d_ffn2_down, _ = ffn_backward(dx3, x2, ffn2_norm, w_ffn2_in, w_late_out,
                                                                      saved2, "ffn2")

    def ride_early(g):
        late = jnp.concatenate([d_ffn2_down, g["w_ssd_branch"].reshape(N_CHIP, -1, D_MODEL),
                                g["w_out"].reshape(N_CHIP, -1, D_MODEL)], axis=1)
        return reducer(["ffn2_in", "late_out"], [d_ffn2_in, late])

    g_in_rows = {}

    def ride_late(g):
        main = g["w_in_main"]
        last = jnp.concatenate([main[:, 3 * IN_SHARD:IN_DT0], g["w_in_dt"][:, :SSD_HEADS], main[:, IN_DT0:]], axis=1)
        for part, rows in (("top", slice(0, D_MODEL // 2)), ("bottom", slice(D_MODEL // 2, D_MODEL))):
            g_in_rows[part] = jnp.stack([main[rows, j * IN_SHARD:(j + 1) * IN_SHARD] for j in range(3)]
                                        + [last[rows]])
        return reducer(["mix_in_top", "mix_attn"], [g_in_rows["top"], _chip_major_cols(g["w_attn_branch"])])

    dx1, gmix = mixer_backward(dx2, x1, saved_mix, ride_early, ride_late)
    dx0, grads["ffn1_norm"], rode_in, rode_out, rode_hidden = ffn_backward(
        dx1, x0, ffn1_norm, w_ffn1_in, w_ffn1_out, saved1, "ffn1",
        hidden_rider=reducer(["mix_in_bottom"], [g_in_rows["bottom"]]),
        ride_down=lambda d: reducer(["ffn1_out"], [d]), ride_in=lambda d: reducer(["ffn1_in"], [d]))
    for k in ("mix_norm", "q_norm", "k_norm", "conv_b", "dt_bias", "a_log", "d_skip", "ssd_norm"):
        grads[k] = gmix[k]
    reduced = dict(zip(("ffn2_in", "late_out", "mix_in_top", "mix_attn", "ffn1_in", "ffn1_out", "mix_in_bottom"),
                       (*gmix["rode_early"], *gmix["rode_late"], rode_in[0], rode_out[0], rode_hidden[0])))
    reduced = {c: r.reshape(-1, r.shape[2]) for c, r in reduced.items()}
    reduced["mix_in"] = jnp.concatenate([reduced.pop("mix_in_top"), reduced.pop("mix_in_bottom")], axis=0)
    summed = all_reduce_small([grads[k] for k in SMALL]
                              + [gmix["conv_w"], (0.5 * jnp.sum(sq) / D_MODEL).reshape(1, 1)])
    g_small = dict(zip(SMALL, summed))
    loss = summed[-1].reshape(())
    chip = 2 * lax.axis_index("x") + lax.axis_index("y")
    g_conv = lax.dynamic_slice_in_dim(summed[-2], chip * CONV_SHARD, CONV_SHARD, axis=1)

    g_final, delta, new_m, new_v = dict(g_small), {}, {}, {}

    def update(k, g_arr, row_off):
        w, m, v = wts[k], moms[k], vars_[k]
        rows, cols = w.shape[1:]
        if cols % 128:
            res = adamw(jnp.swapaxes(w, 1, 2), g_arr[row_off:row_off + rows].T, 0, jnp.swapaxes(m, 1, 2),
                        jnp.swapaxes(v, 1, 2), f"adamw_{k}")
            res = [jnp.swapaxes(r, 1, 2) for r in res]
        else:
            res = adamw(w, g_arr, row_off, m, v, f"adamw_{k}")
        g_final[k], delta[k], new_m[k], new_v[k] = res

    for cls, members in CLASSES.items():
        off = 0
        for k, rows in members:
            update(k, reduced[cls], off)
            off += rows
    update("conv_w", g_conv, 0)
    small = adamw_small(*([d[k] for k in SMALL] for d in (wts, g_small, moms, vars_)))
    for res, vals in zip((delta, new_m, new_v), small):
        res.update(zip(SMALL, vals))

    return (loss, dx0[None], *[g_final[k] for k in WEIGHTS], *[delta[k] for k in WEIGHTS],
            *[new_m[k] for k in WEIGHTS], *[new_v[k] for k in WEIGHTS])
```

```python
import collections
import functools
import math

import jax
import jax.numpy as jnp
from jax import lax
from jax.experimental import pallas as pl
from jax.experimental.pallas import tpu as pltpu

F32 = jnp.float32
BF16 = jnp.bfloat16
MESH = pl.DeviceIdType.MESH

EPS = 1e-6
D_MODEL = 1024
D_FF = 2816
N_CHIP = 4
FF_SHARD = D_FF // N_CHIP
HD = 64
BLK = 128
ATTN_DILATIONS = (1, 4, 16)
HEADS_PER_PATTERN = 8
N_ATTN_HEADS = 24
ALIBI_MAX_EXP = 8.0
ATTN_QKV = 1536
GROUP_W = 512
SSD_INNER = 2048
SSD_HEADS = 32
SSD_GROUPS = 4
SSD_CONV = 4
SSD_CONV_DIM = 3072
IN_COLS = 11808
IN_DT0, IN_DT1 = 9728, 9760
IN_SHARD = IN_COLS // 4
COL_K, COL_V, COL_Z, COL_XBC, COL_GA, COL_GS, P_COLS = 1536, 3072, 4608, 6656, 9728, 10752, 11776
DT_PAD = 128

ADAM_LR, ADAM_B1, ADAM_B2, ADAM_EPS, ADAM_WD, ADAM_STEP = 0.001, 0.9, 0.999, 1e-08, 0.01, 10

V7X_VMEM_LIMIT = 56 * 1024 * 1024
NEG = -1e30


Rider = collections.namedtuple("Rider", "arrays out_shape scratch start finish")
Rider.__doc__ = """An exchange between devices that rides in a compute kernel: its copies are started in the host's
first grid step and waited for in its last, so they travel while the host computes.  arrays / out_shape: extra HBM
operands and results; scratch: extra scratch; start, finish: f(in_refs, out_refs, scratch_refs)."""


def _call(body, *, name, out_shape, in_specs, out_specs, grid=(), scratch_shapes=(), aliases=None, rider=None):
    params = dict(dimension_semantics=("arbitrary",) * len(grid), vmem_limit_bytes=V7X_VMEM_LIMIT)
    if rider is None:
        return pl.pallas_call(
            body, out_shape=out_shape, grid=grid, in_specs=in_specs, out_specs=out_specs,
            scratch_shapes=scratch_shapes, input_output_aliases=aliases or {}, name=name,
            compiler_params=pltpu.CompilerParams(**params))
    single = not isinstance(out_shape, (list, tuple))
    main_out = [out_shape] if single else list(out_shape)
    main_specs = [out_specs] if single else list(out_specs)
    n_in, n_out, n_scr = len(in_specs), len(main_out), len(scratch_shapes)
    r_in, r_out = len(rider.arrays), len(rider.out_shape)

    def wrapped(*refs):
        ins, refs = refs[:n_in], refs[n_in:]
        r_ins, refs = refs[:r_in], refs[r_in:]
        outs, refs = refs[:n_out], refs[n_out:]
        r_outs, refs = refs[:r_out], refs[r_out:]
        scr, r_scr = refs[:n_scr], refs[n_scr:]
        first = last = None
        for axis, size in enumerate(grid):
            at_start, at_end = pl.program_id(axis) == 0, pl.program_id(axis) == size - 1
            first = at_start if first is None else jnp.logical_and(first, at_start)
            last = at_end if last is None else jnp.logical_and(last, at_end)

        @pl.when(first)
        def _():
            rider.start(r_ins, r_outs, r_scr)

        body(*ins, *outs, *scr)

        @pl.when(last)
        def _():
            rider.finish(r_ins, r_outs, r_scr)

    hbm = pl.BlockSpec(memory_space=pl.ANY)
    call = pl.pallas_call(
        wrapped, out_shape=main_out + list(rider.out_shape), grid=grid, in_specs=list(in_specs) + [hbm] * r_in,
        out_specs=main_specs + [hbm] * r_out, scratch_shapes=list(scratch_shapes) + list(rider.scratch), name=name,
        compiler_params=pltpu.CompilerParams(has_side_effects=True, **params))

    def run(*args):
        res = call(*args, *rider.arrays)
        main = res[:n_out]
        return (main[0] if single else main), res[n_out:]

    return run


def _sds(shape, dtype):
    return jax.ShapeDtypeStruct(tuple(shape), dtype)


def _dot(a, b):
    return jnp.dot(a, b, preferred_element_type=F32)


def _dot_nt(a, b):
    return lax.dot_general(a, b, (((1,), (1,)), ((), ())), preferred_element_type=F32)


def _dot_tn(a, b):
    return lax.dot_general(a, b, (((0,), (0,)), ((), ())), preferred_element_type=F32)


def _dot_hi(a, b):
    return jnp.dot(a, b, preferred_element_type=F32, precision=lax.Precision.HIGHEST)


def _sigmoid(x):
    return 1.0 / (1.0 + jnp.exp(-x))


def _lane_first_half(shape):
    return lax.broadcasted_iota(jnp.int32, shape, len(shape) - 1) < HD


def _pair_sum(x, first):
    s_all = jnp.sum(x, axis=-1, keepdims=True)
    s_a = jnp.sum(jnp.where(first, x, 0.0), axis=-1, keepdims=True)
    return s_a, s_all - s_a


def _rowwise(name, fn, rows, consts, outs, accs=(), tm=512, rider=None):
    n_rows = None
    in_arrays, in_specs = [], []
    for r in rows:
        if isinstance(r, tuple):
            arr, w, cb = r
            spec = pl.BlockSpec((tm, w), functools.partial(lambda i, cb: (i, cb), cb=cb))
        else:
            arr = r
            spec = pl.BlockSpec((tm, arr.shape[1]), lambda i: (i, 0))
        n_rows = arr.shape[0]
        in_arrays.append(arr)
        in_specs.append(spec)
    for c in consts:
        in_arrays.append(c)
        in_specs.append(pl.BlockSpec(c.shape, functools.partial(lambda i, n: (0,) * n, n=c.ndim)))
    out_shape = [_sds(s, d) for s, d in outs] + [_sds(s, d) for s, d in accs]
    out_specs = [pl.BlockSpec((tm, s[1]), lambda i: (i, 0)) for s, _ in outs]
    out_specs += [pl.BlockSpec(s, functools.partial(lambda i, n: (0,) * n, n=len(s))) for s, _ in accs]

    def body(*refs):
        fn(pl.program_id(0), *refs)

    res = _call(body, name=name, out_shape=out_shape, grid=(n_rows // tm,), in_specs=in_specs,
                out_specs=out_specs, rider=rider)(*in_arrays)
    return res


def rms_fwd(x, gain, name, rider=None):
    def fn(i, x_ref, g_ref, h_ref):
        xv = x_ref[...]
        r = lax.rsqrt(jnp.mean(xv * xv, axis=-1, keepdims=True) + EPS)
        h_ref[...] = (xv * r * g_ref[...]).astype(h_ref.dtype)

    res = _rowwise(name, fn, [x], [gain], [(x.shape, BF16)], rider=rider)
    return res[0] if rider is None else (res[0][0], res[1])


def rms_bwd(dhs, x, gain, dx_in, name):
    n = len(dhs)

    def fn(i, *refs):
        dh_refs, (x_ref, dxin_ref, g_ref, dx_ref, dg_ref) = refs[:n], refs[n:]
        dh = dh_refs[0][...]
        for r in dh_refs[1:]:
            dh = dh + r[...]
        xv = x_ref[...]
        r = lax.rsqrt(jnp.mean(xv * xv, axis=-1, keepdims=True) + EPS)
        xn = xv * r
        dxn = dh * g_ref[...]
        dx_ref[...] = dxin_ref[...] + r * (dxn - xn * jnp.mean(dxn * xn, axis=-1, keepdims=True))

        @pl.when(i == 0)
        def _():
            dg_ref[...] = jnp.zeros_like(dg_ref)

        dg_ref[...] += jnp.sum(dh * xn, axis=0, keepdims=True)

    return _rowwise(name, fn, list(dhs) + [x, dx_in], [gain], [(x.shape, F32)], [((1, x.shape[1]), F32)])


def loss_grad(y, target, name):
    def fn(i, y_ref, t_ref, dy_ref, sq_ref):
        err = y_ref[...] - t_ref[...]
        dy_ref[...] = err * (1.0 / y_ref.shape[1])

        @pl.when(i == 0)
        def _():
            sq_ref[...] = jnp.zeros_like(sq_ref)

        sq_ref[...] += jnp.sum(err * err, axis=0, keepdims=True)

    return _rowwise(name, fn, [y, target], [], [(y.shape, F32)], [((1, y.shape[1]), F32)])


def matmul_nn(a, b, name, out_dtype, tm, tn, res=None, scale=1.0, rider=None):
    s, k = a.shape
    n = b.shape[1]

    def body(*refs):
        if res is None:
            a_ref, b_ref, o_ref = refs
            o_ref[...] = _dot(a_ref[...], b_ref[...]).astype(o_ref.dtype)
        else:
            a_ref, b_ref, r_ref, o_ref = refs
            o_ref[...] = (r_ref[...] + scale * _dot(a_ref[...], b_ref[...])).astype(o_ref.dtype)

    in_specs = [pl.BlockSpec((tm, k), lambda i, j: (i, 0)), pl.BlockSpec((k, tn), lambda i, j: (0, j))]
    args = [a, b]
    if res is not None:
        in_specs.append(pl.BlockSpec((tm, tn), lambda i, j: (i, j)))
        args.append(res)
    return _call(body, name=name, out_shape=_sds((s, n), out_dtype), grid=(s // tm, n // tn), in_specs=in_specs,
                 out_specs=pl.BlockSpec((tm, tn), lambda i, j: (i, j)), rider=rider)(*args)


def matmul_nt(a, b, name, out_dtype, tm, tn, tk, rider=None):
    s, k = a.shape
    n = b.shape[0]
    nk = k // tk

    def body(a_ref, b_ref, o_ref, acc_ref):
        kk = pl.program_id(2)

        @pl.when(kk == 0)
        def _():
            acc_ref[...] = jnp.zeros_like(acc_ref)

        acc_ref[...] += _dot_nt(a_ref[...].astype(BF16), b_ref[...])

        @pl.when(kk == nk - 1)
        def _():
            o_ref[...] = acc_ref[...].astype(o_ref.dtype)

    return _call(body, name=name, out_shape=_sds((s, n), out_dtype), grid=(s // tm, n // tn, nk),
                 in_specs=[pl.BlockSpec((tm, tk), lambda i, j, kk: (i, kk)),
                           pl.BlockSpec((tn, tk), lambda i, j, kk: (j, kk))],
                 out_specs=pl.BlockSpec((tm, tn), lambda i, j, kk: (i, j)),
                 scratch_shapes=[pltpu.VMEM((tm, tn), F32)], rider=rider)(a, b)


def matmul_tn(a, b, name, tn, ts, a_scale=None, b_scale=None, rider=None):
    s, m = a.shape
    n = b.shape[1]
    ns = s // ts

    def body(a_ref, b_ref, o_ref, acc_ref):
        ss = pl.program_id(1)

        @pl.when(ss == 0)
        def _():
            acc_ref[...] = jnp.zeros_like(acc_ref)

        av, bv = a_ref[...], b_ref[...]
        if a_scale is not None:
            av = av * a_scale
        if b_scale is not None:
            bv = bv * b_scale
        acc_ref[...] += _dot_tn(av.astype(BF16), bv.astype(BF16))

        @pl.when(ss == ns - 1)
        def _():
            o_ref[...] = acc_ref[...].astype(o_ref.dtype)

    return _call(body, name=name, out_shape=_sds((m, n), BF16), grid=(n // tn, ns),
                 in_specs=[pl.BlockSpec((ts, m), lambda j, ss: (ss, 0)), pl.BlockSpec((ts, tn), lambda j, ss: (ss, j))],
                 out_specs=pl.BlockSpec((m, tn), lambda j, ss: (0, j)),
                 scratch_shapes=[pltpu.VMEM((m, tn), F32)], rider=rider)(a, b)


def _piece_specs(pieces, tile, rows_tile, tile_axis_first):
    specs, ranges, t0 = [], [], 0
    for a in pieces:
        n = a.shape[1] // tile

        def index(*ids, t0=t0, n=n):
            t, r = (ids[0], ids[1]) if tile_axis_first else (ids[2], ids[0])
            on = jnp.logical_and(t >= t0, t < t0 + n)
            return jnp.where(on, r, 0), jnp.clip(t - t0, 0, n - 1)

        specs.append(pl.BlockSpec((rows_tile, tile), index))
        ranges.append((t0, n))
        t0 += n
    return specs, ranges


def matmul_tn_pieces(a, pieces, name, tn, ts, rider=None):
    s, m = a.shape
    ns = s // ts
    specs, ranges = _piece_specs(pieces, tn, ts, True)
    n_total = sum(n for _, n in ranges)

    def body(a_ref, *refs):
        b_refs, o_ref, acc_ref = refs[:len(pieces)], refs[-2], refs[-1]
        j, ss = pl.program_id(0), pl.program_id(1)

        @pl.when(ss == 0)
        def _():
            acc_ref[...] = jnp.zeros_like(acc_ref)

        for b_ref, (t0, n) in zip(b_refs, ranges):
            @pl.when(jnp.logical_and(j >= t0, j < t0 + n))
            def _(b_ref=b_ref):
                acc_ref[...] += _dot_tn(a_ref[...], b_ref[...])

        @pl.when(ss == ns - 1)
        def _():
            o_ref[...] = acc_ref[...].astype(o_ref.dtype)

    return _call(body, name=name, out_shape=_sds((m, n_total * tn), BF16), grid=(n_total, ns),
                 in_specs=[pl.BlockSpec((ts, m), lambda j, ss: (ss, 0))] + specs,
                 out_specs=pl.BlockSpec((m, tn), lambda j, ss: (0, j)),
                 scratch_shapes=[pltpu.VMEM((m, tn), F32)], rider=rider)(a, *pieces)


def matmul_nt_pieces(pieces, b, name, out_dtype, tm, tn, tk, rider=None):
    s = pieces[0].shape[0]
    n = b.shape[0]
    specs, ranges = _piece_specs(pieces, tk, tm, False)
    nk = sum(cnt for _, cnt in ranges)

    def body(*refs):
        a_refs, b_ref, o_ref, acc_ref = refs[:len(pieces)], refs[-3], refs[-2], refs[-1]
        kk = pl.program_id(2)

        @pl.when(kk == 0)
        def _():
            acc_ref[...] = jnp.zeros_like(acc_ref)

        for a_ref, (t0, cnt) in zip(a_refs, ranges):
            @pl.when(jnp.logical_and(kk >= t0, kk < t0 + cnt))
            def _(a_ref=a_ref):
                acc_ref[...] += _dot_nt(a_ref[...], b_ref[...])

        @pl.when(kk == nk - 1)
        def _():
            o_ref[...] = acc_ref[...].astype(o_ref.dtype)

    return _call(body, name=name, out_shape=_sds((s, n), out_dtype), grid=(s // tm, n // tn, nk),
                 in_specs=specs + [pl.BlockSpec((tn, tk), lambda i, j, kk: (j, kk))],
                 out_specs=pl.BlockSpec((tm, tn), lambda i, j, kk: (i, j)),
                 scratch_shapes=[pltpu.VMEM((tm, tn), F32)], rider=rider)(*pieces, b)


def ffn_up(h, w704, gate_blk, up_blk, name, tm=512, rider=None):
    s = h.shape[0]

    def body(h_ref, wg_ref, wu_ref, g_ref, u_ref, a_ref):
        hv = h_ref[...]
        g = _dot(hv, wg_ref[...])
        u = _dot(hv, wu_ref[...])
        g_ref[...] = g.astype(BF16)
        u_ref[...] = u.astype(BF16)
        a_ref[...] = (g * _sigmoid(g) * u).astype(BF16)

    ospec = pl.BlockSpec((None, tm, FF_SHARD), lambda j, i: (j, i, 0))
    shp = _sds((N_CHIP, s, FF_SHARD), BF16)
    return _call(body, name=name, out_shape=[shp, shp, shp], grid=(N_CHIP, s // tm),
                 in_specs=[pl.BlockSpec((tm, D_MODEL), lambda j, i: (i, 0)),
                           pl.BlockSpec((None, D_MODEL, FF_SHARD), lambda j, i: (j, gate_blk, 0)),
                           pl.BlockSpec((None, D_MODEL, FF_SHARD), lambda j, i: (j, up_blk, 0))],
                 out_specs=[ospec, ospec, ospec], rider=rider)(h, w704, w704)


def ffn_down(a, w1024, blk, x, name, tm=512, rider=None, target=None):
    s = x.shape[0]

    def block_out(a_ref, wd_ref, x_ref):
        acc = _dot(a_ref[0], wd_ref[0])
        for j in range(1, N_CHIP):
            acc += _dot(a_ref[j], wd_ref[j])
        return x_ref[...] + 0.5 * acc

    def body(a_ref, wd_ref, x_ref, o_ref):
        o_ref[...] = block_out(a_ref, wd_ref, x_ref)

    def loss_body(a_ref, wd_ref, x_ref, t_ref, dy_ref, sq_ref):
        err = block_out(a_ref, wd_ref, x_ref) - t_ref[...]
        dy_ref[...] = err * (1.0 / D_MODEL)

        @pl.when(pl.program_id(0) == 0)
        def _():
            sq_ref[...] = jnp.zeros_like(sq_ref)

        sq_ref[...] += jnp.sum(err * err, axis=0, keepdims=True)

    rows = pl.BlockSpec((tm, D_MODEL), lambda i: (i, 0))
    in_specs = [pl.BlockSpec((N_CHIP, tm, FF_SHARD), lambda i: (0, i, 0)),
                pl.BlockSpec((N_CHIP, FF_SHARD, D_MODEL), lambda i: (0, blk, 0)), rows]
    if target is None:
        return _call(body, name=name, out_shape=_sds((s, D_MODEL), F32), grid=(s // tm,), in_specs=in_specs,
                     out_specs=rows, rider=rider)(a, w1024, x)
    return _call(loss_body, name=name, out_shape=[_sds((s, D_MODEL), F32), _sds((1, D_MODEL), F32)], grid=(s // tm,),
                 in_specs=in_specs + [rows], out_specs=[rows, pl.BlockSpec((1, D_MODEL), lambda i: (0, 0))],
                 rider=rider)(a, w1024, x, target)


def ffn_bwd_hidden(dx, w1024, blk, g, u, name, tm=1024, rider=None):
    s = dx.shape[0]

    def body(dx_ref, wd_ref, g_ref, u_ref, dg_ref, du_ref):
        dy = (0.5 * dx_ref[...]).astype(BF16)
        da = _dot_nt(dy, wd_ref[...])
        gv = g_ref[...].astype(F32)
        uv = u_ref[...].astype(F32)
        sg = _sigmoid(gv)
        dg_ref[...] = (da * uv * (sg * (1.0 + gv * (1.0 - sg)))).astype(BF16)
        du_ref[...] = (da * gv * sg).astype(BF16)

    hspec = pl.BlockSpec((None, tm, FF_SHARD), lambda j, i: (j, i, 0))
    shp = _sds((N_CHIP, s, FF_SHARD), BF16)
    return _call(body, name=name, out_shape=[shp, shp], grid=(N_CHIP, s // tm),
                 in_specs=[pl.BlockSpec((tm, D_MODEL), lambda j, i: (i, 0)),
                           pl.BlockSpec((None, FF_SHARD, D_MODEL), lambda j, i: (j, blk, 0)), hspec, hspec],
                 out_specs=[hspec, hspec], rider=rider)(dx, w1024, g, u)


def ffn_bwd_input(dg, du, w704, gate_blk, up_blk, x, gain, dy, name, tm=512, rider=None):
    s = dg.shape[1]

    def body(dg_ref, du_ref, wg_ref, wu_ref, x_ref, gain_ref, dy_ref, dx_ref, dgain_ref):
        dh = _dot_nt(dg_ref[0], wg_ref[0]) + _dot_nt(du_ref[0], wu_ref[0])
        for j in range(1, N_CHIP):
            dh += _dot_nt(dg_ref[j], wg_ref[j]) + _dot_nt(du_ref[j], wu_ref[j])
        xv = x_ref[...]
        r = lax.rsqrt(jnp.mean(xv * xv, axis=-1, keepdims=True) + EPS)
        xn = xv * r
        dxn = dh * gain_ref[...]
        dx_ref[...] = dy_ref[...] + r * (dxn - xn * jnp.mean(dxn * xn, axis=-1, keepdims=True))

        @pl.when(pl.program_id(0) == 0)
        def _():
            dgain_ref[...] = jnp.zeros_like(dgain_ref)

        dgain_ref[...] += jnp.sum(dh * xn, axis=0, keepdims=True)

    hspec = pl.BlockSpec((N_CHIP, tm, FF_SHARD), lambda i: (0, i, 0))
    rows = pl.BlockSpec((tm, D_MODEL), lambda i: (i, 0))
    whole = pl.BlockSpec((1, D_MODEL), lambda i: (0, 0))
    return _call(body, name=name, out_shape=[_sds((s, D_MODEL), F32), _sds((1, D_MODEL), F32)], grid=(s // tm,),
                 in_specs=[hspec, hspec,
                           pl.BlockSpec((N_CHIP, D_MODEL, FF_SHARD), lambda i: (0, gate_blk, 0), pl.Buffered(1)),
                           pl.BlockSpec((N_CHIP, D_MODEL, FF_SHARD), lambda i: (0, up_blk, 0), pl.Buffered(1)),
                           rows, whole, rows],
                 out_specs=[rows, whole], rider=rider)(dg, du, w704, w704, x, gain, dy)


def ffn_wgrad_in(h, dgu, name, ts=1024, rider=None):
    s = h.shape[0]
    ns = s // ts

    def body(h_ref, d_ref, o_ref, acc_ref):
        ss = pl.program_id(1)

        @pl.when(ss == 0)
        def _():
            acc_ref[...] = jnp.zeros_like(acc_ref)

        acc_ref[...] += _dot_tn(h_ref[...], d_ref[...])

        @pl.when(ss == ns - 1)
        def _():
            o_ref[...] = acc_ref[...].astype(BF16)

    return _call(body, name=name, out_shape=_sds((N_CHIP, D_MODEL, FF_SHARD), BF16), grid=(N_CHIP, ns),
                 in_specs=[pl.BlockSpec((ts, D_MODEL), lambda j, ss: (ss, 0)),
                           pl.BlockSpec((None, ts, FF_SHARD), lambda j, ss: (j, ss, 0))],
                 out_specs=pl.BlockSpec((None, D_MODEL, FF_SHARD), lambda j, ss: (j, 0, 0)),
                 scratch_shapes=[pltpu.VMEM((D_MODEL, FF_SHARD), F32)], rider=rider)(h, dgu)


def ffn_wgrad_down(a, dx, name, ts=1024):
    s = dx.shape[0]
    ns = s // ts

    def body(a_ref, dx_ref, o_ref, acc_ref):
        ss = pl.program_id(1)

        @pl.when(ss == 0)
        def _():
            acc_ref[...] = jnp.zeros_like(acc_ref)

        acc_ref[...] += _dot_tn(a_ref[...], (0.5 * dx_ref[...]).astype(BF16))

        @pl.when(ss == ns - 1)
        def _():
            o_ref[...] = acc_ref[...].astype(BF16)

    return _call(body, name=name, out_shape=_sds((N_CHIP, FF_SHARD, D_MODEL), BF16), grid=(N_CHIP, ns),
                 in_specs=[pl.BlockSpec((None, ts, FF_SHARD), lambda j, ss: (j, ss, 0)),
                           pl.BlockSpec((ts, D_MODEL), lambda j, ss: (ss, 0))],
                 out_specs=pl.BlockSpec((None, FF_SHARD, D_MODEL), lambda j, ss: (j, 0, 0)),
                 scratch_shapes=[pltpu.VMEM((FF_SHARD, D_MODEL), F32)])(a, dx)


def ffn_forward(x, gain, get_w704, get_w1024, tag, rms_rider=None, up_rider=None, down_rider=None, target=None):
    h = rms_fwd(x, gain, f"{tag}_rms", rider=rms_rider)
    h, rode_rms = h if rms_rider is not None else (h, None)
    res = ffn_up(h, get_w704(rode_rms), 0, 1, f"{tag}_up", rider=up_rider)
    (g, u, a), rode_up = res if up_rider is not None else (res, None)
    y = ffn_down(a, get_w1024(rode_up), 0, x, f"{tag}_down", rider=down_rider, target=target)
    y, rode_down = y if down_rider is not None else (y, None)
    return y, (h, g, u, a), rode_rms, rode_up, rode_down


def ffn_backward(dy, x, gain, w704, w1024, saved, tag, hidden_rider=None, ride_down=None, ride_in=None):
    h, g, u, a = saved
    d_wd = ffn_wgrad_down(a, dy, f"{tag}_dwd")
    rode_hidden = None
    if hidden_rider is not None:
        (dg, du), rode_hidden = ffn_bwd_hidden(dy, w1024, 0, g, u, f"{tag}_dhid", tm=512, rider=hidden_rider)
    else:
        dg, du = ffn_bwd_hidden(dy, w1024, 0, g, u, f"{tag}_dhid")
    if ride_down is not None:
        d_wg, d_wd = ffn_wgrad_in(h, dg, f"{tag}_dwg", rider=ride_down(d_wd))
    else:
        d_wg = ffn_wgrad_in(h, dg, f"{tag}_dwg")
    d_win = jnp.concatenate([d_wg, ffn_wgrad_in(h, du, f"{tag}_dwu")], axis=1)
    if ride_in is not None:
        (dx, d_gain), d_win = ffn_bwd_input(dg, du, w704, 0, 1, x, gain, dy, f"{tag}_dh", tm=256,
                                            rider=ride_in(d_win))
    else:
        dx, d_gain = ffn_bwd_input(dg, du, w704, 0, 1, x, gain, dy, f"{tag}_dh")
    return dx, d_gain, d_win, d_wd, rode_hidden


def _alibi_slope(head):
    return float(2.0 ** (-ALIBI_MAX_EXP * (head + 1) / N_ATTN_HEADS))


def _same_head():
    row = lax.broadcasted_iota(jnp.int32, (2 * HD, 2 * HD), 0)
    col = lax.broadcasted_iota(jnp.int32, (2 * HD, 2 * HD), 1)
    return ((row < HD) == (col < HD)).astype(BF16)


def _head_sums(x, same_head):
    hi = x.astype(BF16)
    lo = (x - hi.astype(F32)).astype(BF16)
    return _dot(hi, same_head) + _dot(lo, same_head)


def _head_norm(t, gain_pair, same_head):
    r = lax.rsqrt(_head_sums(t * t, same_head) * (1.0 / HD) + EPS)
    return t * r * gain_pair, r


def qk_norm_fwd(p, q_gain, k_gain, name):
    s = p.shape[0]

    def fn(i, q_ref, k_ref, qg_ref, kg_ref, qn_ref, kn_ref):
        same_head = _same_head()
        for src, g_ref, dst in ((q_ref, qg_ref, qn_ref), (k_ref, kg_ref, kn_ref)):
            for pr in range(ATTN_QKV // (2 * HD)):
                cols = slice(pr * 2 * HD, (pr + 1) * 2 * HD)
                y, _ = _head_norm(src[:, cols].astype(F32), g_ref[...], same_head)
                dst[:, cols] = y.astype(BF16)

    return _rowwise(name, fn, [(p, ATTN_QKV, 0), (p, ATTN_QKV, 1)], [q_gain, k_gain],
                    [((s, ATTN_QKV), BF16), ((s, ATTN_QKV), BF16)])


def qk_norm_bwd(p, dqs, dks, q_gain, k_gain, name):
    s = p.shape[0]
    pairs_per_pattern = GROUP_W // (2 * HD)

    def fn(i, q_ref, k_ref, dq0, dq1, dq2, dk0, dk1, dk2, qg_ref, kg_ref, dqk_ref, dqg_ref, dkg_ref):
        same_head = _same_head()

        @pl.when(i == 0)
        def _():
            dqg_ref[...] = jnp.zeros_like(dqg_ref)
            dkg_ref[...] = jnp.zeros_like(dkg_ref)

        for src, d_refs, g_ref, dst, dg_ref in (
                (q_ref, (dq0, dq1, dq2), qg_ref, dqk_ref.at[:, 0:ATTN_QKV], dqg_ref),
                (k_ref, (dk0, dk1, dk2), kg_ref, dqk_ref.at[:, ATTN_QKV:2 * ATTN_QKV], dkg_ref)):
            for pr in range(ATTN_QKV // (2 * HD)):
                cols = slice(pr * 2 * HD, (pr + 1) * 2 * HD)
                t = src[:, cols].astype(F32)
                r = lax.rsqrt(_head_sums(t * t, same_head) * (1.0 / HD) + EPS)
                xn = t * r
                within = (pr % pairs_per_pattern) * 2 * HD
                dy = d_refs[pr // pairs_per_pattern][:, within:within + 2 * HD]
                dg_ref[:, cols] += jnp.sum(dy * xn, axis=0, keepdims=True)
                dxn = dy * g_ref[...]
                mean = _head_sums(dxn * xn, same_head) * (1.0 / HD)
                dst[:, cols] = (r * (dxn - xn * mean)).astype(BF16)

    return _rowwise(name, fn, [(p, ATTN_QKV, 0), (p, ATTN_QKV, 1)] + list(dqs) + list(dks), [q_gain, k_gain],
                    [((s, 2 * ATTN_QKV), BF16)], [((1, ATTN_QKV), F32), ((1, ATTN_QKV), F32)])


def _to_streams(a, d):
    if d == 1:
        return a
    s, c = a.shape
    return a.reshape(s // d, d, c).transpose(1, 0, 2).reshape(s, c)


def _from_streams(a, d):
    if d == 1:
        return a
    s, c = a.shape
    return a.reshape(d, s // d, c).transpose(1, 0, 2).reshape(s, c)


def _attn_masks():
    row = lax.broadcasted_iota(jnp.int32, (BLK, BLK), 0)
    col = lax.broadcasted_iota(jnp.int32, (BLK, BLK), 1)
    rel_diag = row - col
    rel_prev = rel_diag + BLK
    return rel_diag, rel_prev


def attn_fwd(q, k, v, pattern, name, tq=512):
    s = q.shape[0]
    d = ATTN_DILATIONS[pattern]
    blocks_per_stream = (s // d) // BLK
    nsb = tq // BLK

    def body(q_ref, k_ref, v_ref, kp_ref, vp_ref, o_ref, l_ref):
        i = pl.program_id(0)
        rel_diag, rel_prev = _attn_masks()
        first = _lane_first_half((BLK, 2 * HD))
        rd_f = (rel_diag * d).astype(F32)
        rp_f = (rel_prev * d).astype(F32)
        for sb in range(nsb):
            rows = slice(sb * BLK, (sb + 1) * BLK)
            has_prev = ((i * nsb + sb) % blocks_per_stream != 0).astype(jnp.int32)
            m_diag = rel_diag >= 0
            m_prev = (rel_prev + (1 - has_prev) * (4 * BLK)) <= BLK
            for pr in range(GROUP_W // (2 * HD)):
                cols = slice(pr * 2 * HD, (pr + 1) * 2 * HD)
                qp = q_ref[rows, cols]
                kc, vc = k_ref[rows, cols], v_ref[rows, cols]
                if sb == 0:
                    kp, vp = kp_ref[:, cols], vp_ref[:, cols]
                else:
                    prows = slice((sb - 1) * BLK, sb * BLK)
                    kp, vp = k_ref[prows, cols], v_ref[prows, cols]
                outs, lses = [], []
                for e in range(2):
                    slope = _alibi_slope(pattern * HEADS_PER_PATTERN + 2 * pr + e)
                    qm = jnp.where(first if e == 0 else jnp.logical_not(first), qp, jnp.zeros_like(qp))
                    s1 = jnp.where(m_diag, _dot_nt(qm, kc) * 0.125 - slope * rd_f, NEG)
                    s0 = jnp.where(m_prev, _dot_nt(qm, kp) * 0.125 - slope * rp_f, NEG)
                    m = jnp.maximum(jnp.max(s1, axis=-1, keepdims=True), jnp.max(s0, axis=-1, keepdims=True))
                    p1 = jnp.exp(s1 - m)
                    p0 = jnp.exp(s0 - m)
                    l = jnp.sum(p1, axis=-1, keepdims=True) + jnp.sum(p0, axis=-1, keepdims=True)
                    inv = 1.0 / l
                    outs.append(_dot((p1 * inv).astype(BF16), vc) + _dot((p0 * inv).astype(BF16), vp))
                    lses.append(m + jnp.log(l))
                o_ref[rows, cols] = jnp.where(first, outs[0], outs[1])
                l_ref[rows, cols] = jnp.where(first, lses[0], lses[1])

    cur = pl.BlockSpec((tq, GROUP_W), lambda i: (i, 0))
    prev = pl.BlockSpec((BLK, GROUP_W), lambda i: (jnp.maximum(i * nsb - 1, 0), 0))
    return _call(body, name=name, out_shape=[_sds((s, GROUP_W), F32), _sds((s, GROUP_W), F32)], grid=(s // tq,),
                 in_specs=[cur, cur, cur, prev, prev], out_specs=[cur, cur])(q, k, v, k, v)


def attn_merge_fwd(os_, lses, name):
    s = os_[0].shape[0]

    def fn(i, o0, o1, o2, l0, l1, l2, out_ref):
        m = jnp.maximum(jnp.maximum(l0[...], l1[...]), l2[...])
        e0, e1, e2 = jnp.exp(l0[...] - m), jnp.exp(l1[...] - m), jnp.exp(l2[...] - m)
        inv = 1.0 / (e0 + e1 + e2)
        out_ref[...] = ((e0 * inv) * o0[...] + (e1 * inv) * o1[...] + (e2 * inv) * o2[...]).astype(BF16)

    return _rowwise(name, fn, list(os_) + list(lses), [], [((s, GROUP_W), BF16)])[0]


def attn_merge_bwd(d_out, os_, lses, name):
    s = d_out.shape[0]

    def fn(i, do_ref, o0, o1, o2, l0, l1, l2, d0, d1, d2, c0, c1, c2):
        first = _lane_first_half((do_ref.shape[0], 2 * HD))
        m = jnp.maximum(jnp.maximum(l0[...], l1[...]), l2[...])
        e0, e1, e2 = jnp.exp(l0[...] - m), jnp.exp(l1[...] - m), jnp.exp(l2[...] - m)
        inv = 1.0 / (e0 + e1 + e2)
        w0, w1, w2 = e0 * inv, e1 * inv, e2 * inv
        do = do_ref[...]
        prod = do * (w0 * o0[...] + w1 * o1[...] + w2 * o2[...])
        same_head = _same_head()
        for pr in range(GROUP_W // (2 * HD)):
            cols = slice(pr * 2 * HD, (pr + 1) * 2 * HD)
            t = _head_sums(prod[:, cols], same_head)
            for w, c_ref in ((w0, c0), (w1, c1), (w2, c2)):
                c_ref[:, cols] = w[:, cols] * t
        for w, d_ref in ((w0, d0), (w1, d1), (w2, d2)):
            d_ref[...] = (w * do).astype(BF16)

    shp = (s, GROUP_W)
    return _rowwise(name, fn, [d_out] + list(os_) + list(lses), [],
                    [(shp, BF16)] * 3 + [(shp, F32)] * 3)


def attn_bwd(q, k, v, d_o, cterm, lse, pattern, name, tq=512):
    s = q.shape[0]
    d = ATTN_DILATIONS[pattern]
    blocks_per_stream = (s // d) // BLK
    nsb = tq // BLK
    n_blocks = s // BLK

    def body(q_ref, k_ref, v_ref, do_ref, c_ref, l_ref, kp_ref, vp_ref, qn_ref, don_ref, cn_ref, ln_ref,
             dq_ref, dk_ref, dv_ref):
        i = pl.program_id(0)
        rel_diag, rel_prev = _attn_masks()
        first = _lane_first_half((BLK, 2 * HD))
        second = jnp.logical_not(first)
        rd_f = (rel_diag * d).astype(F32)
        rp_f = (rel_prev * d).astype(F32)
        m_diag = rel_diag >= 0
        dq_ref[...] = jnp.zeros_like(dq_ref)
        dk_ref[...] = jnp.zeros_like(dk_ref)
        dv_ref[...] = jnp.zeros_like(dv_ref)

        def pair(qp, dop, cp, lp, kp, vp, rel_f, mask):
            dq = dk = dv = None
            for e in range(2):
                lanes = first if e == 0 else second
                slope = slopes[e]
                qm = jnp.where(lanes, qp, jnp.zeros_like(qp))
                dom = jnp.where(lanes, dop, jnp.zeros_like(dop))
                km = jnp.where(lanes, kp, jnp.zeros_like(kp))
                sc = jnp.where(mask, _dot_nt(qm, kp) * 0.125 - slope * rel_f, NEG)
                pm = jnp.exp(sc - lp[:, e * HD:e * HD + 1])
                dl = pm * (_dot_nt(dom, vp) - cp[:, e * HD:e * HD + 1])
                dl16 = dl.astype(BF16)
                t_dq = _dot(dl16, km)
                t_dk = _dot_tn(dl16, qm)
                t_dv = _dot_tn(pm.astype(BF16), dom)
                dq = t_dq if dq is None else dq + t_dq
                dk = t_dk if dk is None else dk + t_dk
                dv = t_dv if dv is None else dv + t_dv
            return dq * 0.125, dk * 0.125, dv

        for pr in range(GROUP_W // (2 * HD)):
            cols = slice(pr * 2 * HD, (pr + 1) * 2 * HD)
            slopes = [_alibi_slope(pattern * HEADS_PER_PATTERN + 2 * pr + e) for e in range(2)]
            for sb in range(nsb + 1):
                gb = i * nsb + sb
                if sb < nsb:
                    rows = slice(sb * BLK, (sb + 1) * BLK)
                    qp, dop, cp, lp = q_ref[rows, cols], do_ref[rows, cols], c_ref[rows, cols], l_ref[rows, cols]
                else:
                    qp, dop, cp, lp = qn_ref[:, cols], don_ref[:, cols], cn_ref[:, cols], ln_ref[:, cols]
                if sb < nsb:
                    dq1, dk1, dv1 = pair(qp, dop, cp, lp, k_ref[rows, cols], v_ref[rows, cols], rd_f, m_diag)
                    dq_ref[rows, cols] += dq1
                    dk_ref[rows, cols] += dk1
                    dv_ref[rows, cols] += dv1
                valid = jnp.logical_and(gb % blocks_per_stream != 0, gb < n_blocks).astype(jnp.int32)
                m_prev = jnp.logical_and(rel_prev <= BLK, (rel_prev + (1 - valid) * (4 * BLK)) <= BLK)
                if sb == 0:
                    kp, vp = kp_ref[:, cols], vp_ref[:, cols]
                else:
                    prows = slice((sb - 1) * BLK, sb * BLK)
                    kp, vp = k_ref[prows, cols], v_ref[prows, cols]
                dq0, dk0, dv0 = pair(qp, dop, cp, lp, kp, vp, rp_f, m_prev)
                if sb < nsb:
                    dq_ref[rows, cols] += dq0
                if sb > 0:
                    dk_ref[prows, cols] += dk0
                    dv_ref[prows, cols] += dv0

    cur = pl.BlockSpec((tq, GROUP_W), lambda i: (i, 0))
    prev = pl.BlockSpec((BLK, GROUP_W), lambda i: (jnp.maximum(i * nsb - 1, 0), 0))
    nxt = pl.BlockSpec((BLK, GROUP_W), lambda i: (jnp.minimum((i + 1) * nsb, n_blocks - 1), 0))
    shp = _sds((s, GROUP_W), F32)
    return _call(body, name=name, out_shape=[shp, shp, shp], grid=(s // tq,),
                 in_specs=[cur] * 6 + [prev, prev] + [nxt] * 4, out_specs=[cur, cur, cur])(
                     q, k, v, d_o, cterm, lse, k, v, q, d_o, cterm, lse)


def _band_constants(d):
    row = lax.broadcasted_iota(jnp.int32, (2 * BLK, 2 * BLK), 0)
    col = lax.broadcasted_iota(jnp.int32, (2 * BLK, 2 * BLK), 1)
    rel = BLK + jnp.where(row >= BLK, row - BLK, row) - col
    band = jnp.logical_and(rel >= 0, rel <= BLK)
    return (rel * d).astype(F32), band, (col >= BLK).astype(jnp.int32)


def _stack_heads(x, first):
    zero = jnp.zeros_like(x)
    return jnp.concatenate([jnp.where(first, x, zero), jnp.where(first, zero, x)], axis=0)


def _unstack_heads(x2, first):
    return jnp.where(first, x2[:BLK], x2[BLK:])


def _head_column(x):
    return jnp.concatenate([x[:, 0:1], x[:, HD:HD + 1]], axis=0)


def attn_fwd2(q, k, v, pattern, name, tq=512):
    s = q.shape[0]
    d = ATTN_DILATIONS[pattern]
    blocks_per_stream = (s // d) // BLK
    nsb = tq // BLK

    def body(q_ref, k_ref, v_ref, kp_ref, vp_ref, o_ref, l_ref):
        i = pl.program_id(0)
        rel_f, band, own = _band_constants(d)
        first = _lane_first_half((BLK, 2 * HD))
        upper = lax.broadcasted_iota(jnp.int32, (2 * BLK, 1), 0) < BLK
        for sb in range(nsb):
            rows = slice(sb * BLK, (sb + 1) * BLK)
            has_prev = ((i * nsb + sb) % blocks_per_stream != 0).astype(jnp.int32)
            mask = jnp.logical_and(band, (own + has_prev) > 0)
            for pr in range(GROUP_W // (2 * HD)):
                cols = slice(pr * 2 * HD, (pr + 1) * 2 * HD)
                if sb == 0:
                    kcat = jnp.concatenate([kp_ref[:, cols], k_ref[rows, cols]], axis=0)
                    vcat = jnp.concatenate([vp_ref[:, cols], v_ref[rows, cols]], axis=0)
                else:
                    both = slice((sb - 1) * BLK, (sb + 1) * BLK)
                    kcat, vcat = k_ref[both, cols], v_ref[both, cols]
                h0 = pattern * HEADS_PER_PATTERN + 2 * pr
                slope = jnp.where(upper, _alibi_slope(h0), _alibi_slope(h0 + 1))
                sc = _dot_nt(_stack_heads(q_ref[rows, cols], first), kcat) * 0.125 - slope * rel_f
                sc = jnp.where(mask, sc, NEG)
                m = jnp.max(sc, axis=-1, keepdims=True)
                p = jnp.exp(sc - m)
                l = jnp.sum(p, axis=-1, keepdims=True)
                o2 = _dot((p * (1.0 / l)).astype(BF16), vcat)
                o_ref[rows, cols] = _unstack_heads(o2, first)
                lse = m + jnp.log(l)
                l_ref[rows, cols] = jnp.where(first, lse[:BLK], lse[BLK:])

    cur = pl.BlockSpec((tq, GROUP_W), lambda i: (i, 0))
    prev = pl.BlockSpec((BLK, GROUP_W), lambda i: (jnp.maximum(i * nsb - 1, 0), 0))
    return _call(body, name=name, out_shape=[_sds((s, GROUP_W), F32), _sds((s, GROUP_W), F32)], grid=(s // tq,),
                 in_specs=[cur, cur, cur, prev, prev], out_specs=[cur, cur])(q, k, v, k, v)


def attn_bwd2(q, k, v, d_o, cterm, lse, pattern, name, tq=1024):
    s = q.shape[0]
    d = ATTN_DILATIONS[pattern]
    blocks_per_stream = (s // d) // BLK
    nsb = tq // BLK
    n_blocks = s // BLK

    def body(q_ref, k_ref, v_ref, do_ref, c_ref, l_ref, kp_ref, vp_ref, qn_ref, kn_ref, vn_ref, don_ref, cn_ref,
             ln_ref, dq_ref, dk_ref, dv_ref):
        i = pl.program_id(0)
        rel_f, band, own = _band_constants(d)
        first = _lane_first_half((BLK, 2 * HD))
        upper = lax.broadcasted_iota(jnp.int32, (2 * BLK, 1), 0) < BLK
        dk_ref[...] = jnp.zeros_like(dk_ref)
        dv_ref[...] = jnp.zeros_like(dv_ref)
        for sb in range(nsb + 1):
            gb = i * nsb + sb
            rows = slice(sb * BLK, (sb + 1) * BLK)
            before = slice((sb - 1) * BLK, sb * BLK)
            inside = (gb < n_blocks).astype(jnp.int32)
            has_prev = jnp.logical_and(gb % blocks_per_stream != 0, gb < n_blocks).astype(jnp.int32)
            mask = jnp.logical_and(band, (own * inside + has_prev) > 0)
            for pr in range(GROUP_W // (2 * HD)):
                cols = slice(pr * 2 * HD, (pr + 1) * 2 * HD)
                if sb == 0:
                    kcat = jnp.concatenate([kp_ref[:, cols], k_ref[rows, cols]], axis=0)
                    vcat = jnp.concatenate([vp_ref[:, cols], v_ref[rows, cols]], axis=0)
                elif sb == nsb:
                    kcat = jnp.concatenate([k_ref[before, cols], kn_ref[:, cols]], axis=0)
                    vcat = jnp.concatenate([v_ref[before, cols], vn_ref[:, cols]], axis=0)
                else:
                    both = slice((sb - 1) * BLK, (sb + 1) * BLK)
                    kcat, vcat = k_ref[both, cols], v_ref[both, cols]
                if sb < nsb:
                    qp, dop, cp, lp = q_ref[rows, cols], do_ref[rows, cols], c_ref[rows, cols], l_ref[rows, cols]
                else:
                    qp, dop, cp, lp = qn_ref[:, cols], don_ref[:, cols], cn_ref[:, cols], ln_ref[:, cols]
                h0 = pattern * HEADS_PER_PATTERN + 2 * pr
                slope = jnp.where(upper, _alibi_slope(h0), _alibi_slope(h0 + 1))
                q2 = _stack_heads(qp, first)
                do2 = _stack_heads(dop, first)
                sc = jnp.where(mask, _dot_nt(q2, kcat) * 0.125 - slope * rel_f, NEG)
                pm = jnp.exp(sc - _head_column(lp))
                dl = (pm * (_dot_nt(do2, vcat) - _head_column(cp))).astype(BF16)
                if sb < nsb:
                    dq_ref[rows, cols] = _unstack_heads(_dot(dl, kcat), first) * 0.125
                dk2 = _dot_tn(dl, q2) * 0.125
                dv2 = _dot_tn(pm.astype(BF16), do2)
                if sb > 0:
                    dk_ref[before, cols] += dk2[:BLK]
                    dv_ref[before, cols] += dv2[:BLK]
                if sb < nsb:
                    dk_ref[rows, cols] += dk2[BLK:]
                    dv_ref[rows, cols] += dv2[BLK:]

    cur = pl.BlockSpec((tq, GROUP_W), lambda i: (i, 0))
    prev = pl.BlockSpec((BLK, GROUP_W), lambda i: (jnp.maximum(i * nsb - 1, 0), 0))
    nxt = pl.BlockSpec((BLK, GROUP_W), lambda i: (jnp.minimum((i + 1) * nsb, n_blocks - 1), 0))
    shp = _sds((s, GROUP_W), F32)
    return _call(body, name=name, out_shape=[shp, shp, shp], grid=(s // tq,),
                 in_specs=[cur] * 6 + [prev, prev] + [nxt] * 6, out_specs=[cur, cur, cur])(
                     q, k, v, d_o, cterm, lse, k, v, q, k, v, d_o, cterm, lse)


HALO = 16
CONV_TQ = 512


def conv_fwd(p, w, b, name):
    s = p.shape[0]
    tq = CONV_TQ
    ncol = SSD_CONV_DIM // GROUP_W
    cb0 = COL_XBC // GROUP_W

    def body(u_ref, up_ref, w_ref, b_ref, c_ref, xc_ref):
        i = pl.program_id(0)
        prev = (up_ref[...].astype(F32) * (i > 0).astype(F32)).astype(BF16)
        ext = jnp.concatenate([prev, u_ref[...]], axis=0)
        row = lax.broadcasted_iota(jnp.int32, (BLK, BLK + HALO), 0)
        col = lax.broadcasted_iota(jnp.int32, (BLK, BLK + HALO), 1)
        for blk in range(tq // BLK):
            lead = ext[blk * BLK:blk * BLK + BLK + HALO]
            acc = b_ref[...] + w_ref[SSD_CONV - 1:SSD_CONV, :] * lead[HALO:].astype(F32)
            for kk in range(SSD_CONV - 1):
                pick = (col == row + HALO - (SSD_CONV - 1 - kk)).astype(BF16)
                acc += w_ref[kk:kk + 1, :] * _dot(pick, lead)
            rows = slice(blk * BLK, (blk + 1) * BLK)
            c_ref[rows, :] = acc.astype(BF16)
            xc_ref[rows, :] = (acc * _sigmoid(acc)).astype(BF16)

    cur_in = pl.BlockSpec((tq, GROUP_W), lambda i, j: (i, cb0 + j))
    prev_in = pl.BlockSpec((HALO, GROUP_W), lambda i, j: (jnp.maximum(i * (tq // HALO) - 1, 0), cb0 + j))
    cur_out = pl.BlockSpec((tq, GROUP_W), lambda i, j: (i, j))
    shp = _sds((s, SSD_CONV_DIM), BF16)
    return _call(body, name=name, out_shape=[shp, shp], grid=(s // tq, ncol),
                 in_specs=[cur_in, prev_in, pl.BlockSpec((SSD_CONV, GROUP_W), lambda i, j: (0, j)),
                           pl.BlockSpec((1, GROUP_W), lambda i, j: (0, j))],
                 out_specs=[cur_out, cur_out])(p, p, w, b)


def conv_bwd(p, cpre, dxs, d_b, d_c, w, name):
    s = p.shape[0]
    tq = CONV_TQ
    ncol = SSD_CONV_DIM // GROUP_W
    n_xs = SSD_INNER // GROUP_W
    cb0 = COL_XBC // GROUP_W
    nt = s // tq

    def body(u_ref, c_ref, cn_ref, dx_ref, dxn_ref, dbm_ref, dbmn_ref, dcm_ref, dcmn_ref, w_ref,
             du_ref, dw_ref, db_ref):
        j, i = pl.program_id(0), pl.program_id(1)

        def dpre(c16, dx):
            c = c16.astype(F32)
            sg = _sigmoid(c)
            return dx * (sg * (1.0 + c * (1.0 - sg)))

        def pick(a_ref, b_ref, c_ref_):
            return jnp.where(j < n_xs, a_ref[...], jnp.where(j == n_xs, b_ref[...], c_ref_[...]))

        dc = dpre(c_ref[...], pick(dx_ref, dbm_ref, dcm_ref))
        dcn = dpre(cn_ref[...], pick(dxn_ref, dbmn_ref, dcmn_ref)) * (i < nt - 1).astype(F32)
        dext = jnp.concatenate([dc, dcn], axis=0)
        u = u_ref[...].astype(F32)

        @pl.when(i == 0)
        def _():
            dw_ref[...] = jnp.zeros_like(dw_ref)
            db_ref[...] = jnp.zeros_like(db_ref)

        du = w_ref[SSD_CONV - 1:SSD_CONV, :] * dc
        dw_ref[SSD_CONV - 1:SSD_CONV, :] += jnp.sum(dc * u, axis=0, keepdims=True)
        for kk in range(SSD_CONV - 1):
            sh = SSD_CONV - 1 - kk
            ahead = pltpu.roll(dext, tq + HALO - sh, 0)[0:tq]
            du += w_ref[kk:kk + 1, :] * ahead
            dw_ref[kk:kk + 1, :] += jnp.sum(ahead * u, axis=0, keepdims=True)
        du_ref[...] = du.astype(BF16)
        db_ref[...] += jnp.sum(dc, axis=0, keepdims=True)

    hb = tq // HALO
    cur_p = pl.BlockSpec((tq, GROUP_W), lambda j, i: (i, cb0 + j))
    cur = pl.BlockSpec((tq, GROUP_W), lambda j, i: (i, j))
    nxt = pl.BlockSpec((HALO, GROUP_W), lambda j, i: (jnp.minimum((i + 1) * hb, s // HALO - 1), j))

    def piece(first_tile, n_tiles):
        def on(j):
            return jnp.logical_and(j >= first_tile, j < first_tile + n_tiles)

        def col(j):
            return jnp.clip(j - first_tile, 0, n_tiles - 1)

        return (pl.BlockSpec((tq, GROUP_W), lambda j, i: (jnp.where(on(j), i, 0), col(j))),
                pl.BlockSpec((HALO, GROUP_W),
                             lambda j, i: (jnp.where(on(j), jnp.minimum((i + 1) * hb, s // HALO - 1), 0), col(j))))

    return _call(body, name=name,
                 out_shape=[_sds((s, SSD_CONV_DIM), BF16), _sds((8, SSD_CONV_DIM), F32), _sds((1, SSD_CONV_DIM), F32)],
                 grid=(ncol, nt),
                 in_specs=[cur_p, cur, nxt, *piece(0, n_xs), *piece(n_xs, 1), *piece(n_xs + 1, 1),
                           pl.BlockSpec((SSD_CONV, GROUP_W), lambda j, i: (0, j))],
                 out_specs=[cur, pl.BlockSpec((8, GROUP_W), lambda j, i: (0, j)),
                            pl.BlockSpec((1, GROUP_W), lambda j, i: (0, j))])(
                                p, cpre, cpre, dxs, dxs, d_b, d_b, d_c, d_c, w)


def _softplus(x):
    return jnp.maximum(x, 0.0) + jnp.log(1.0 + jnp.exp(-jnp.abs(x)))


def _ssd_decays(dtr_ref, dtrt_ref, bias_ref, biast_ref, alog_ref, alogt_ref):
    row = lax.broadcasted_iota(jnp.int32, (BLK, BLK), 0)
    col = lax.broadcasted_iota(jnp.int32, (BLK, BLK), 1)
    lower = (row >= col).astype(F32)
    upper = (row <= col).astype(F32)
    dtb = dtr_ref[...] + bias_ref[...]
    dt = _softplus(dtb)
    a = dt * (-jnp.exp(alog_ref[...]))
    cs = _dot_hi(lower, a)
    a_t = _softplus(dtrt_ref[...] + biast_ref[...]) * (-jnp.exp(alogt_ref[...]))
    cs_t = _dot_hi(a_t, upper)
    return dtb, dt, cs, cs_t, row, col, upper


SSD_GROUPS_PER_STEP = 4


def _per_group(body, gps, kinds):
    def wrapped(*refs):
        for gi in range(gps):
            args, pos = [], 0
            for kind, n in kinds:
                if kind == "each":
                    args.append(refs[pos + gi])
                    pos += gps
                    continue
                ref = refs[pos]
                pos += 1
                if kind == "cols":
                    args.append(ref.at[:, gi * n:(gi + 1) * n])
                else:
                    args.append(ref.at[gi] if n == 1 else ref.at[pl.ds(gi * n, n)])
            body(*args)

    return wrapped


def ssd_fwd(p, xc, dtg, dtg_t, params, gn, name):
    s = p.shape[0]
    nc = s // BLK
    bias, bias_t, alog, alog_t, dskip = params

    def body(xs_ref, b_ref, c_ref, z_ref, dtr_ref, dtrt_ref, bias_ref, biast_ref, alog_ref, alogt_ref, dsk_ref,
             gn_ref, y_ref, sin_ref, hp_ref, h_ref):
        c_idx = pl.program_id(1)

        @pl.when(c_idx == 0)
        def _():
            h_ref[...] = jnp.zeros_like(h_ref)

        _, dt, cs, cs_t, row, col, _ = _ssd_decays(dtr_ref, dtrt_ref, bias_ref, biast_ref, alog_ref, alogt_ref)
        first = _lane_first_half((BLK, 2 * HD))
        first_row = _lane_first_half((1, 2 * HD))
        tril = row >= col
        b16, c16 = b_ref[...], c_ref[...]
        cb = _dot_nt(c16, b16)
        n_pairs = GROUP_W // (2 * HD)
        tot = cs[BLK - 1:BLK, :]
        exp_cs, exp_rest, exp_tot = jnp.exp(cs), jnp.exp(tot - cs), jnp.exp(tot)

        lanes_of_head = (lax.broadcasted_iota(jnp.int32, (8, GROUP_W), 1) // HD
                         == lax.broadcasted_iota(jnp.int32, (8, GROUP_W), 0)).astype(BF16)

        def per_head(v, mask):
            if v.shape[0] == 1:
                return jnp.concatenate([jnp.where(mask, v[:, 2 * pr:2 * pr + 1], v[:, 2 * pr + 1:2 * pr + 2])
                                        for pr in range(n_pairs)], axis=1)
            hi = v.astype(BF16)
            lo = (v - hi.astype(F32)).astype(BF16)
            return _dot(hi, lanes_of_head) + _dot(lo, lanes_of_head)

        xs = xs_ref[...].astype(F32)
        xt = xs * per_head(dt, first)
        xt16 = xt.astype(BF16)
        hstate = jnp.concatenate([h_ref[pr] for pr in range(n_pairs)], axis=1)
        for pr in range(n_pairs):
            hp_ref[pr] = h_ref[pr]
        y_off = per_head(exp_cs, first) * _dot(c16, hstate.astype(BF16))
        new = per_head(exp_tot, first_row) * hstate + _dot_tn(b16, (per_head(exp_rest, first) * xt).astype(BF16))
        for pr in range(n_pairs):
            h_ref[pr] = new[:, pr * 2 * HD:(pr + 1) * 2 * HD]
        y_diag = []
        for pr in range(n_pairs):
            cols = slice(pr * 2 * HD, (pr + 1) * 2 * HD)
            m2 = jnp.concatenate(
                [(cb * jnp.exp(jnp.where(tril, cs[:, h:h + 1] - cs_t[h:h + 1, :], NEG))).astype(BF16)
                 for h in (2 * pr, 2 * pr + 1)], axis=1)
            y_diag.append(_dot(m2, _stack_heads(xt16[:, cols], first)))
        y = jnp.concatenate(y_diag, axis=1) + y_off + xs * per_head(dsk_ref[...], first_row)
        y_ref[...] = y
        zv = z_ref[...].astype(F32)
        yz = y * (zv * _sigmoid(zv))
        r = lax.rsqrt(jnp.mean(yz * yz, axis=-1, keepdims=True) + EPS)
        sin_ref[...] = (yz * r * gn_ref[...]).astype(BF16)

    gps = SSD_GROUPS_PER_STEP
    wide, narrow, lead = ("cols", GROUP_W), ("cols", BLK), ("lead", 1)
    kinds = [wide, narrow, narrow, ("each", 0)] + [lead] * 7 + [wide, wide, wide, lead, ("lead", 4)]
    wide_w, narrow_w = GROUP_W * gps, BLK * gps
    gparam = pl.BlockSpec((gps, 1, 8), lambda g, c: (g, 0, 0))
    gparam_t = pl.BlockSpec((gps, 8, 1), lambda g, c: (g, 0, 0))
    z_specs = [pl.BlockSpec((BLK, GROUP_W), functools.partial(lambda g, c, gi: (c, COL_Z // GROUP_W + gps * g + gi),
                                                              gi=gi)) for gi in range(gps)]
    return _call(
        _per_group(body, gps, kinds), name=name,
        out_shape=[_sds((s, SSD_INNER), F32), _sds((s, SSD_INNER), BF16),
                   _sds((SSD_GROUPS, nc, 4, BLK, 2 * HD), F32)],
        grid=(SSD_GROUPS // gps, nc),
        in_specs=[pl.BlockSpec((BLK, wide_w), lambda g, c: (c, g)),
                  pl.BlockSpec((BLK, narrow_w), lambda g, c: (c, SSD_INNER // narrow_w + g)),
                  pl.BlockSpec((BLK, narrow_w), lambda g, c: (c, (SSD_INNER + SSD_GROUPS * BLK) // narrow_w + g)),
                  *z_specs,
                  pl.BlockSpec((gps, BLK, 8), lambda g, c: (g, c, 0)),
                  pl.BlockSpec((gps, 8, BLK), lambda g, c: (g, 0, c)),
                  gparam, gparam_t, gparam, gparam_t, gparam,
                  pl.BlockSpec((1, wide_w), lambda g, c: (0, g))],
        out_specs=[pl.BlockSpec((BLK, wide_w), lambda g, c: (c, g)),
                   pl.BlockSpec((BLK, wide_w), lambda g, c: (c, g)),
                   pl.BlockSpec((gps, None, 4, BLK, 2 * HD), lambda g, c: (g, c, 0, 0, 0))],
        scratch_shapes=[pltpu.VMEM((4 * gps, BLK, 2 * HD), F32)],
    )(xc, xc, xc, *([p] * gps), dtg, dtg_t, bias, bias_t, alog, alog_t, dskip, gn)


def ssd_bwd(p, xc, y, d_sin, hprev, dtg, dtg_t, params, gn, name):
    s = p.shape[0]
    nc = s // BLK
    bias, bias_t, alog, alog_t, dskip = params

    def body(xs_ref, b_ref, c_ref, z_ref, y_ref, dsin_ref, hp_ref, dtr_ref, dtrt_ref, bias_ref,
             biast_ref, alog_ref, alogt_ref, dsk_ref, gn_ref,
             dxs_ref, db_ref, dc_ref, dz_ref, ddt_ref, da_ref, dbias_ref, ddsk_ref, dgn_ref, dh_ref):
        c_idx = pl.program_id(1)

        @pl.when(c_idx == 0)
        def _():
            dh_ref[...] = jnp.zeros_like(dh_ref)
            da_ref[...] = jnp.zeros_like(da_ref)
            dbias_ref[...] = jnp.zeros_like(dbias_ref)
            ddsk_ref[...] = jnp.zeros_like(ddsk_ref)
            dgn_ref[...] = jnp.zeros_like(dgn_ref)

        dtb, dt, cs, cs_t, row, col, upper = _ssd_decays(dtr_ref, dtrt_ref, bias_ref, biast_ref, alog_ref, alogt_ref)
        first = _lane_first_half((BLK, 2 * HD))
        second = jnp.logical_not(first)
        first_row = _lane_first_half((1, 2 * HD))
        tril = row >= col
        triu = row <= col
        last_row = lax.broadcasted_iota(jnp.int32, (BLK, 1), 0) == BLK - 1
        lane8 = lax.broadcasted_iota(jnp.int32, (BLK, 8), 1)

        yv = y_ref[...]
        zv = z_ref[...].astype(F32)
        sg = _sigmoid(zv)
        yz = yv * (zv * sg)
        r = lax.rsqrt(jnp.mean(yz * yz, axis=-1, keepdims=True) + EPS)
        yzn = yz * r
        dsn = dsin_ref[...]
        dgn_ref[...] += jnp.sum(dsn * yzn, axis=0, keepdims=True)
        dsn = dsn * gn_ref[...]
        dyz = r * (dsn - yzn * jnp.mean(dsn * yzn, axis=-1, keepdims=True))
        dy = dyz * (zv * sg)
        dz_ref[...] = (dyz * yv * (sg * (1.0 + zv * (1.0 - sg)))).astype(BF16)
        xs_all = xs_ref[...].astype(F32)
        ddsk_ref[...] += jnp.sum(dy * xs_all, axis=0, keepdims=True)

        b16, c16 = b_ref[...], c_ref[...]
        cb = _dot_nt(c16, b16)
        cb_t = _dot_nt(b16, c16)
        n_pairs = GROUP_W // (2 * HD)
        tot = cs[BLK - 1:BLK, :]
        exp_cs, exp_rest, exp_tot = jnp.exp(cs), jnp.exp(tot - cs), jnp.exp(tot)

        lanes_of_head = (lax.broadcasted_iota(jnp.int32, (8, GROUP_W), 1) // HD
                         == lax.broadcasted_iota(jnp.int32, (8, GROUP_W), 0)).astype(BF16)

        def per_head(v, mask):
            if v.shape[0] == 1:
                return jnp.concatenate([jnp.where(mask, v[:, 2 * pr:2 * pr + 1], v[:, 2 * pr + 1:2 * pr + 2])
                                        for pr in range(n_pairs)], axis=1)
            hi = v.astype(BF16)
            lo = (v - hi.astype(F32)).astype(BF16)
            return _dot(hi, lanes_of_head) + _dot(lo, lanes_of_head)

        head_of_lane = (lax.broadcasted_iota(jnp.int32, (GROUP_W, 8), 0) // HD
                        == lax.broadcasted_iota(jnp.int32, (GROUP_W, 8), 1)).astype(BF16)

        def head_sums(v):
            hi = v.astype(BF16)
            lo = (v - hi.astype(F32)).astype(BF16)
            return _dot(hi, head_of_lane) + _dot(lo, head_of_lane)

        dt_w, e_w, f_w = per_head(dt, first), per_head(exp_cs, first), per_head(exp_rest, first)
        xt = xs_all * dt_w
        xt16 = xt.astype(BF16)
        hstate = jnp.concatenate([hp_ref[pr] for pr in range(n_pairs)], axis=1)
        h16 = hstate.astype(BF16)
        dhn = jnp.concatenate([dh_ref[pr] for pr in range(n_pairs)], axis=1)
        dhn16 = dhn.astype(BF16)
        edy16 = (e_w * dy).astype(BF16)
        y_off = e_w * _dot(c16, h16)
        dcs_all = head_sums(dy * y_off)
        dc_acc = _dot_nt(edy16, h16)
        zmat = _dot(b16, dhn16)
        t_all = head_sums(zmat * xt) * exp_rest
        hh_rows = jnp.sum(head_sums(dhn * hstate), axis=0, keepdims=True)
        dtot = jnp.sum(t_all, axis=0, keepdims=True) + hh_rows * exp_tot
        dcs_all = dcs_all - t_all + jnp.where(last_row, dtot, 0.0)
        fxt16 = (f_w * xt).astype(BF16)
        db_acc = _dot_nt(fxt16, dhn16)
        dh_new = _dot_tn(c16, edy16) + per_head(exp_tot, first_row) * dhn
        for pr in range(n_pairs):
            dh_ref[pr] = dh_new[:, pr * 2 * HD:(pr + 1) * 2 * HD]
        g_sum = jnp.zeros((BLK, BLK), F32)
        gt_sum = jnp.zeros((BLK, BLK), F32)
        d_xt_parts = []
        for pr in range(n_pairs):
            cols = slice(pr * 2 * HD, (pr + 1) * 2 * HD)
            dym2 = _stack_heads(dy[:, cols].astype(BF16), first)
            d_m2 = _dot_nt(dym2, xt16[:, cols])
            d_mt2 = _dot_nt(xt16[:, cols], dym2)
            mt2 = []
            for e, h in enumerate((2 * pr, 2 * pr + 1)):
                cs_c, cs_r = cs[:, h:h + 1], cs_t[h:h + 1, :]
                decay = jnp.exp(jnp.where(tril, cs_c - cs_r, NEG))
                decay_t = jnp.exp(jnp.where(triu, cs_r - cs_c, NEG))
                gm = d_m2[e * BLK:(e + 1) * BLK] * decay
                gmt = d_mt2[:, e * BLK:(e + 1) * BLK] * decay_t
                g_sum += gm
                gt_sum += gmt
                dcs_h = jnp.sum(gm * cb, axis=-1, keepdims=True) - jnp.sum(gmt * cb_t, axis=-1, keepdims=True)
                dcs_all = dcs_all + jnp.where(lane8 == h, dcs_h, 0.0)
                mt2.append((cb_t * decay_t).astype(BF16))
            d_xt_parts.append(_dot(jnp.concatenate(mt2, axis=1), dym2))
        d_xt = jnp.concatenate(d_xt_parts, axis=1) + f_w * zmat
        dxs_ref[...] = dy * per_head(dsk_ref[...], first_row) + d_xt * dt_w
        ddtx_all = head_sums(d_xt * xs_all)

        dc_ref[...] = dc_acc + _dot(g_sum.astype(BF16), b16)
        db_ref[...] = db_acc + _dot(gt_sum.astype(BF16), c16)
        d_a = _dot_hi(upper, dcs_all)
        a_neg = -jnp.exp(alog_ref[...])
        ddt = ddtx_all + d_a * a_neg
        da_ref[...] += jnp.sum(d_a * dt, axis=0, keepdims=True)
        ddtr = ddt * _sigmoid(dtb)
        ddt_ref[...] = ddtr
        dbias_ref[...] += jnp.sum(ddtr, axis=0, keepdims=True)

    gps = SSD_GROUPS_PER_STEP
    k_wide, k_narrow, k_lead = ("cols", GROUP_W), ("cols", BLK), ("lead", 1)
    kinds = ([k_wide, k_narrow, k_narrow, ("each", 0), k_wide, k_wide] + [k_lead] * 8 + [k_wide]
             + [k_wide, k_narrow, k_narrow, k_wide] + [k_lead] * 4 + [k_wide] + [("lead", 4)])
    wide_w, narrow_w = GROUP_W * gps, BLK * gps
    rc = lambda c: nc - 1 - c
    gparam = pl.BlockSpec((gps, 1, 8), lambda g, c: (g, 0, 0))
    gparam_t = pl.BlockSpec((gps, 8, 1), lambda g, c: (g, 0, 0))
    wide = pl.BlockSpec((BLK, wide_w), lambda g, c: (rc(c), g))
    narrow = pl.BlockSpec((BLK, narrow_w), lambda g, c: (rc(c), g))
    z_specs = [pl.BlockSpec((BLK, GROUP_W),
                            functools.partial(lambda g, c, gi: (rc(c), COL_Z // GROUP_W + gps * g + gi), gi=gi))
               for gi in range(gps)]
    return _call(
        _per_group(body, gps, kinds), name=name,
        out_shape=[_sds((s, SSD_INNER), F32), _sds((s, GROUP_W), F32), _sds((s, GROUP_W), F32),
                   _sds((s, SSD_INNER), BF16), _sds((SSD_GROUPS, s, 8), F32),
                   _sds((SSD_GROUPS, 1, 8), F32), _sds((SSD_GROUPS, 1, 8), F32),
                   _sds((SSD_GROUPS, 1, GROUP_W), F32), _sds((1, SSD_INNER), F32)],
        grid=(SSD_GROUPS // gps, nc),
        in_specs=[wide,
                  pl.BlockSpec((BLK, narrow_w), lambda g, c: (rc(c), SSD_INNER // narrow_w + g)),
                  pl.BlockSpec((BLK, narrow_w), lambda g, c: (rc(c), (SSD_INNER + SSD_GROUPS * BLK) // narrow_w + g)),
                  *z_specs,
                  wide, wide,
                  pl.BlockSpec((gps, None, 4, BLK, 2 * HD), lambda g, c: (g, rc(c), 0, 0, 0)),
                  pl.BlockSpec((gps, BLK, 8), lambda g, c: (g, rc(c), 0)),
                  pl.BlockSpec((gps, 8, BLK), lambda g, c: (g, 0, rc(c))),
                  gparam, gparam_t, gparam, gparam_t, gparam,
                  pl.BlockSpec((1, wide_w), lambda g, c: (0, g))],
        out_specs=[wide, narrow, narrow, wide,
                   pl.BlockSpec((gps, BLK, 8), lambda g, c: (g, rc(c), 0)),
                   gparam, gparam,
                   pl.BlockSpec((gps, 1, GROUP_W), lambda g, c: (g, 0, 0)),
                   pl.BlockSpec((1, wide_w), lambda g, c: (0, g))],
        scratch_shapes=[pltpu.VMEM((4 * gps, BLK, 2 * HD), F32)],
    )(xc, xc, xc, *([p] * gps), y, d_sin, hprev, dtg, dtg_t, bias, bias_t, alog, alog_t, dskip, gn)


def merge_fwd(p, a, sbr, name, tm=512):
    s = p.shape[0]
    nj = D_MODEL // GROUP_W

    def body(ga_ref, gs_ref, a_ref, s_ref, o_ref):
        o_ref[...] = (_sigmoid(ga_ref[...].astype(F32)) * a_ref[...]
                      + _sigmoid(gs_ref[...].astype(F32)) * s_ref[...]).astype(BF16)

    blk = pl.BlockSpec((tm, GROUP_W), lambda i, j: (i, j))
    return _call(body, name=name, out_shape=_sds((s, D_MODEL), BF16), grid=(s // tm, nj),
                 in_specs=[pl.BlockSpec((tm, GROUP_W), lambda i, j: (i, COL_GA // GROUP_W + j)),
                           pl.BlockSpec((tm, GROUP_W), lambda i, j: (i, COL_GS // GROUP_W + j)), blk, blk],
                 out_specs=blk)(p, p, a, sbr)


def merge_bwd(p, a, sbr, dmerged, name, tm=512):
    s = p.shape[0]
    nj = D_MODEL // GROUP_W

    def body(ga_ref, gs_ref, a_ref, s_ref, dm_ref, da_ref, ds_ref, dga_ref, dgs_ref):
        dm = dm_ref[...]
        sa = _sigmoid(ga_ref[...].astype(F32))
        ss = _sigmoid(gs_ref[...].astype(F32))
        da_ref[...] = (dm * sa).astype(BF16)
        ds_ref[...] = (dm * ss).astype(BF16)
        dga_ref[...] = (dm * a_ref[...] * sa * (1.0 - sa)).astype(BF16)
        dgs_ref[...] = (dm * s_ref[...] * ss * (1.0 - ss)).astype(BF16)

    blk = pl.BlockSpec((tm, GROUP_W), lambda i, j: (i, j))
    shp = _sds((s, D_MODEL), BF16)
    return _call(body, name=name, out_shape=[shp] * 4, grid=(s // tm, nj),
                 in_specs=[pl.BlockSpec((tm, GROUP_W), lambda i, j: (i, COL_GA // GROUP_W + j)),
                           pl.BlockSpec((tm, GROUP_W), lambda i, j: (i, COL_GS // GROUP_W + j)), blk, blk, blk],
                 out_specs=[blk] * 4)(p, p, a, sbr, dmerged)


def _group_major(v):
    return v.reshape(SSD_GROUPS, 1, 8), v.reshape(SSD_GROUPS, 8, 1)


def mixer_forward(x, w, rider=None, later_weights=None):
    s = x.shape[0]
    h = rms_fwd(x, w["mix_norm"], "mix_rms")
    p = matmul_nn(h, w["w_in_main"], "mix_proj", BF16, tm=1024, tn=512, rider=rider)
    rode = None
    if rider is not None:
        p, rode = p
        w = dict(w, **later_weights(rode))
    dt_raw = matmul_nn(h, w["w_in_dt"], "mix_proj_dt", F32, tm=1024, tn=DT_PAD)
    qn, kn = qk_norm_fwd(p, w["q_gain"], w["k_gain"], "qk_norm")
    streams, os_, lses = [], [], []
    for g, d in enumerate(ATTN_DILATIONS):
        cols = slice(g * GROUP_W, (g + 1) * GROUP_W)
        qs, ks = _to_streams(qn[:, cols], d), _to_streams(kn[:, cols], d)
        vs = _to_streams(p[:, COL_V + g * GROUP_W:COL_V + (g + 1) * GROUP_W], d)
        o, lse = attn_fwd2(qs, ks, vs, g, f"attn_fwd{g}")
        streams.append((qs, ks, vs, lse))
        os_.append(_from_streams(o, d))
        lses.append(_from_streams(lse, d))
    attn_o = attn_merge_fwd(os_, lses, "attn_merge")
    cpre, xc = conv_fwd(p, w["conv_w"], w["conv_b"], "conv_fwd")
    dtg = dt_raw[:, :SSD_HEADS].reshape(s, SSD_GROUPS, 8).transpose(1, 0, 2)
    dtg_t = dtg.transpose(0, 2, 1)
    params = (*_group_major(w["dt_bias"]), *_group_major(w["a_log"]), _group_major(w["d_skip"])[0])
    y, s_in, hprev = ssd_fwd(p, xc, dtg, dtg_t, params, w["ssd_norm"], "ssd_fwd")
    a = matmul_nn(attn_o, w["w_attn_branch"], "attn_branch", F32, tm=1024, tn=512)
    sbr = matmul_nn(s_in, w["w_ssd_branch"], "ssd_branch", F32, tm=1024, tn=512)
    merged = merge_fwd(p, a, sbr, "merge")
    x_out = matmul_nn(merged, w["w_out"], "mix_out", F32, tm=1024, tn=512, res=x)
    saved = dict(h=h, p=p, streams=streams, os=os_, lses=lses, attn_o=attn_o, cpre=cpre, xc=xc, dtg=dtg,
                 dtg_t=dtg_t, params=params, y=y, s_in=s_in, hprev=hprev, a=a, sbr=sbr, merged=merged, w=w)
    return x_out, saved, rode


def mixer_backward(dx_out, x, sv, ride_early=None, ride_late=None):
    s = x.shape[0]
    p = sv["p"]
    w = sv["w"]
    g = {}
    dmerged = matmul_nt(dx_out, w["w_out"], "d_merged", F32, tm=1024, tn=512, tk=1024)
    g["w_out"] = matmul_tn(sv["merged"], dx_out, "dw_out", tn=512, ts=1024)
    da, ds, dga, dgs = merge_bwd(p, sv["a"], sv["sbr"], dmerged, "merge_bwd")
    g["w_attn_branch"] = matmul_tn(sv["attn_o"], da, "dw_attn_branch", tn=512, ts=1024)
    g["w_ssd_branch"] = matmul_tn(sv["s_in"], ds, "dw_ssd_branch", tn=512, ts=1024)
    d_attn_o = matmul_nt(da, w["w_attn_branch"], "d_attn_o", F32, tm=1024, tn=512, tk=1024)
    d_sin = matmul_nt(ds, w["w_ssd_branch"], "d_ssd_in", F32, tm=1024, tn=512, tk=1024)
    dxs, d_b, d_c, dz, ddt, d_asum, d_bias, d_dsk, d_gn = ssd_bwd(
        p, sv["xc"], sv["y"], d_sin, sv["hprev"], sv["dtg"], sv["dtg_t"], sv["params"], w["ssd_norm"], "ssd_bwd")
    dxbc, d_convw, d_convb = conv_bwd(p, sv["cpre"], dxs, d_b, d_c, w["conv_w"], "conv_bwd")
    g["conv_w"] = d_convw[:SSD_CONV]
    g["conv_b"] = d_convb
    g["dt_bias"] = d_bias.reshape(1, SSD_HEADS)
    g["a_log"] = (d_asum * (-jnp.exp(sv["params"][2]))).reshape(1, SSD_HEADS)
    g["d_skip"] = jnp.sum(d_dsk.reshape(SSD_HEADS, HD), axis=1).reshape(1, SSD_HEADS)
    g["ssd_norm"] = d_gn
    merged_bwd = attn_merge_bwd(d_attn_o, sv["os"], sv["lses"], "attn_merge_bwd")
    dqs, dks, dvs = [], [], []
    for gi, d in enumerate(ATTN_DILATIONS):
        qs, ks, vs, lse = sv["streams"][gi]
        d_o = _to_streams(merged_bwd[gi], d)
        cterm = _to_streams(merged_bwd[3 + gi], d)
        dq, dk, dv = attn_bwd2(qs, ks, vs, d_o, cterm, lse, gi, f"attn_bwd{gi}")
        dqs.append(_from_streams(dq, d))
        dks.append(_from_streams(dk, d))
        dvs.append(_from_streams(dv, d).astype(BF16))
    dqk, d_qg, d_kg = qk_norm_bwd(p, dqs, dks, w["q_gain"], w["k_gain"], "qk_norm_bwd")
    g["q_norm"] = jnp.sum(d_qg.reshape(N_ATTN_HEADS, HD), axis=0).reshape(1, HD)
    g["k_norm"] = jnp.sum(d_kg.reshape(N_ATTN_HEADS, HD), axis=0).reshape(1, HD)
    dp = [dqk, jnp.concatenate(dvs, axis=1), dz, dxbc, dga, dgs]
    ddt_pad = jnp.pad(ddt.transpose(1, 0, 2).reshape(s, SSD_HEADS), ((0, 0), (0, DT_PAD - SSD_HEADS)))
    if ride_early is not None:
        g["w_in_main"], g["rode_early"] = matmul_tn_pieces(sv["h"], dp, "dw_in", tn=512, ts=1024,
                                                           rider=ride_early(g))
    else:
        g["w_in_main"] = matmul_tn_pieces(sv["h"], dp, "dw_in", tn=512, ts=1024)
    g["w_in_dt"] = matmul_tn(sv["h"], ddt_pad, "dw_in_dt", tn=DT_PAD, ts=1024)
    if ride_late is not None:
        dh_main, g["rode_late"] = matmul_nt_pieces(dp, w["w_in_main"], "d_mix_h", F32, tm=1024, tn=1024, tk=512,
                                                   rider=ride_late(g))
    else:
        dh_main = matmul_nt_pieces(dp, w["w_in_main"], "d_mix_h", F32, tm=1024, tn=512, tk=512)
    dh_dt = matmul_nt(ddt_pad, w["w_in_dt"], "d_mix_h_dt", F32, tm=1024, tn=1024, tk=DT_PAD)
    dx, g["mix_norm"] = rms_bwd([dh_main, dh_dt], x, w["mix_norm"], dx_out, "mix_drms")
    return dx, g


ANY = pl.BlockSpec(memory_space=pl.ANY)


def _place():
    x, y, c = lax.axis_index("x"), lax.axis_index("y"), lax.axis_index("c")
    chips = [(1 - x, y), (x, 1 - y), (1 - x, 1 - y)]
    return x, y, c, 2 * x + y, chips


def _comm_call(body, *, name, out_shape, n_in, scratch_shapes, aliases=None):
    return pl.pallas_call(
        body, out_shape=out_shape, in_specs=[ANY] * n_in, out_specs=[ANY] * len(out_shape),
        scratch_shapes=scratch_shapes, input_output_aliases=aliases or {}, name=name,
        compiler_params=pltpu.CompilerParams(has_side_effects=True))


def gather_weights(shards, small):
    n = len(shards)
    halves = [a.shape[0] // 2 for a in shards]
    out_shape = [_sds((N_CHIP,) + a.shape, a.dtype) for a in shards] + [_sds((N_CHIP,) + small.shape, small.dtype)]

    def body(*refs):
        ins, outs = refs[:n + 1], refs[n + 1:2 * n + 2]
        send1, recv1, send2, recv2, local = refs[2 * n + 2:]
        x, y, c, me, chips = _place()
        sibling = (x, y, 1 - c)

        def rows(k, chip, core):
            if k == n:
                return outs[k].at[chip]
            return outs[k].at[chip, pl.ds(core * halves[k], halves[k])]

        def level1(k, t, incoming):
            chip = 2 * chips[t][0] + chips[t][1]
            src = ins[k] if k == n else ins[k].at[pl.ds(c * halves[k], halves[k])]
            return pltpu.make_async_remote_copy(
                src_ref=src, dst_ref=rows(k, chip if incoming else me, c), send_sem=send1.at[3 * k + t],
                recv_sem=recv1.at[3 * k + t], device_id=(*chips[t], c), device_id_type=MESH)

        def level2(k, t, incoming):
            chip = 2 * chips[t][0] + chips[t][1]
            core = (1 - c) if incoming else c
            return pltpu.make_async_remote_copy(
                src_ref=rows(k, chip, core), dst_ref=rows(k, chip, core), send_sem=send2.at[3 * k + t],
                recv_sem=recv2.at[3 * k + t], device_id=sibling, device_id_type=MESH)

        own = [pltpu.make_async_copy(ins[k], outs[k].at[me], local.at[k]) for k in range(n + 1)]
        for cp in own:
            cp.start()
        first = [level1(k, t, False) for k in range(n + 1) for t in range(3)]
        for cp in first:
            cp.start()
        passed = []
        for k in range(n + 1):
            for t in range(3):
                level1(k, t, True).wait_recv()
                if k < n:
                    cp = level2(k, t, False)
                    cp.start()
                    passed.append(cp)
        for k in range(n):
            for t in range(3):
                level2(k, t, True).wait_recv()
        for cp in first + passed:
            cp.wait_send()
        for cp in own:
            cp.wait()

    dma = pltpu.SemaphoreType.DMA
    return _comm_call(body, name="gather_weights", out_shape=out_shape, n_in=n + 1,
                      scratch_shapes=[dma((3 * n + 3,)), dma((3 * n + 3,)), dma((3 * n,)), dma((3 * n,)),
                                      dma((n + 1,))])(*shards, small)


def reduce_to_sibling(grads):
    n = len(grads)
    halves = [a.shape[1] // 2 for a in grads]
    shapes = [_sds((N_CHIP, h, a.shape[2]), a.dtype) for a, h in zip(grads, halves)]

    def body(*refs):
        ins, got, kept = refs[:n], refs[n:2 * n], refs[2 * n:3 * n]
        send, recv, local = refs[3 * n:]
        x, y, c, _, _ = _place()
        copies, locals_ = [], []
        for k in range(n):
            h = halves[k]
            locals_.append(pltpu.make_async_copy(ins[k].at[:, pl.ds(c * h, h)], kept[k], local.at[k]))
            copies.append(pltpu.make_async_remote_copy(
                src_ref=ins[k].at[:, pl.ds((1 - c) * h, h)], dst_ref=got[k], send_sem=send.at[k], recv_sem=recv.at[k],
                device_id=(x, y, 1 - c), device_id_type=MESH))
        for cp in locals_ + copies:
            cp.start()
        for cp in copies:
            cp.wait_recv()
        for cp in copies:
            cp.wait_send()
        for cp in locals_:
            cp.wait()

    dma = pltpu.SemaphoreType.DMA
    res = _comm_call(body, name="reduce_to_sibling", out_shape=shapes + shapes, n_in=n,
                     scratch_shapes=[dma((n,)), dma((n,)), dma((n,))])(*grads)
    return res[:n], res[n:]


def reduce_to_owner(sums):
    n = len(sums)
    shapes = [_sds(a.shape, a.dtype) for a in sums]

    def body(*refs):
        ins, outs = refs[:n], refs[n:2 * n]
        send, recv, local = refs[2 * n:]
        x, y, c, me, chips = _place()
        copies, locals_ = [], []
        for k in range(n):
            locals_.append(pltpu.make_async_copy(ins[k].at[me], outs[k].at[3], local.at[k]))
            for t in range(3):
                chip = 2 * chips[t][0] + chips[t][1]
                copies.append(pltpu.make_async_remote_copy(
                    src_ref=ins[k].at[chip], dst_ref=outs[k].at[t], send_sem=send.at[3 * k + t],
                    recv_sem=recv.at[3 * k + t], device_id=(*chips[t], c), device_id_type=MESH))
        for cp in locals_ + copies:
            cp.start()
        for cp in copies:
            cp.wait_recv()
        for cp in copies:
            cp.wait_send()
        for cp in locals_:
            cp.wait()

    dma = pltpu.SemaphoreType.DMA
    return _comm_call(body, name="reduce_to_owner", out_shape=shapes, n_in=n,
                      scratch_shapes=[dma((3 * n,)), dma((3 * n,)), dma((n,))])(*sums)


def share_with_sibling(halves_):
    n = len(halves_)
    shapes = [_sds((2 * a.shape[0], a.shape[1]), a.dtype) for a in halves_]

    def body(*refs):
        ins, outs = refs[:n], refs[n:2 * n]
        send, recv, local = refs[2 * n:]
        x, y, c, _, _ = _place()
        copies, locals_ = [], []
        for k in range(n):
            h = ins[k].shape[0]
            mine = outs[k].at[pl.ds(c * h, h)]
            locals_.append(pltpu.make_async_copy(ins[k], mine, local.at[k]))
            copies.append(pltpu.make_async_remote_copy(
                src_ref=ins[k], dst_ref=mine, send_sem=send.at[k], recv_sem=recv.at[k],
                device_id=(x, y, 1 - c), device_id_type=MESH))
        for cp in locals_ + copies:
            cp.start()
        for cp in copies:
            cp.wait_recv()
        for cp in copies:
            cp.wait_send()
        for cp in locals_:
            cp.wait()

    dma = pltpu.SemaphoreType.DMA
    return _comm_call(body, name="share_with_sibling", out_shape=shapes, n_in=n,
                      scratch_shapes=[dma((n,)), dma((n,)), dma((n,))])(*halves_)


def _cores():
    c = lax.axis_index("c")
    return jnp.stack([c, 1 - c]).astype(jnp.int32)


def _staged_call(body, *, name, grid, in_specs, out_specs, out_shape, scratch_shapes):
    return pl.pallas_call(
        body, out_shape=out_shape, name=name,
        grid_spec=pltpu.PrefetchScalarGridSpec(num_scalar_prefetch=1, grid=grid, in_specs=in_specs,
                                               out_specs=out_specs, scratch_shapes=scratch_shapes),
        compiler_params=pltpu.CompilerParams(dimension_semantics=("arbitrary",) * len(grid),
                                             vmem_limit_bytes=V7X_VMEM_LIMIT, has_side_effects=True))


def gather_rider(shards, tiles):
    dma = pltpu.SemaphoreType.DMA
    n = len(shards)
    geo = [(a.shape[0] // 2, tm, (a.shape[0] // 2) // tm) for a, tm in zip(shards, tiles)]
    scratch = []
    for a, (h, tm, nk) in zip(shards, geo):
        scratch += [pltpu.VMEM((N_CHIP,) + a.shape, a.dtype), dma((3, nk)), dma((3, nk)), dma((3, nk)), dma((3, nk)),
                    dma((nk + 2,))]

    def copies(j, in_ref, scr):
        buf, send1, recv1, send2, recv2, local = scr[6 * j:6 * j + 6]
        h, tm, nk = geo[j]
        x, y, c, me, chips = _place()
        chip_of = [2 * chips[t][0] + chips[t][1] for t in range(3)]

        def rows(chip, core, k):
            return buf.at[chip, pl.ds(core * h + k * tm, tm)]

        def mine(k):
            if k == nk:
                return pltpu.make_async_copy(in_ref.at[pl.ds((1 - c) * h, h)], buf.at[me, pl.ds((1 - c) * h, h)],
                                             local.at[nk])
            return pltpu.make_async_copy(in_ref.at[pl.ds(c * h + k * tm, tm)], rows(me, c, k), local.at[k])

        def level1(t, k, incoming):
            place = rows(chip_of[t] if incoming else me, c, k)
            return pltpu.make_async_remote_copy(src_ref=place, dst_ref=place, send_sem=send1.at[t, k],
                                                recv_sem=recv1.at[t, k], device_id=(*chips[t], c), device_id_type=MESH)

        def level2(t, k, incoming):
            place = rows(chip_of[t], (1 - c) if incoming else c, k)
            return pltpu.make_async_remote_copy(src_ref=place, dst_ref=place, send_sem=send2.at[t, k],
                                                recv_sem=recv2.at[t, k], device_id=(x, y, 1 - c),
                                                device_id_type=MESH)

        return buf, local, nk, mine, level1, level2

    def start(ins, outs, scr):
        for j in range(n):
            _, _, nk, mine, _, _ = copies(j, ins[j], scr)
            for k in range(nk + 1):
                mine(k).start()
        for j in range(n):
            _, _, nk, mine, level1, _ = copies(j, ins[j], scr)
            for k in range(nk):
                mine(k).wait()
                for t in range(3):
                    level1(t, k, False).start()

    def finish(ins, outs, scr):
        for j in range(n):
            _, _, nk, _, level1, level2 = copies(j, ins[j], scr)
            for k in range(nk):
                for t in range(3):
                    level1(t, k, True).wait_recv()
                    level2(t, k, False).start()
        for j in range(n):
            buf, local, nk, mine, level1, level2 = copies(j, ins[j], scr)
            for k in range(nk):
                for t in range(3):
                    level2(t, k, True).wait_recv()
            for k in range(nk):
                for t in range(3):
                    level1(t, k, False).wait_send()
                    level2(t, k, False).wait_send()
            mine(nk).wait()
            pltpu.make_async_copy(buf, outs[j], local.at[nk + 1]).start()
        for j in range(n):
            buf, local, nk, _, _, _ = copies(j, ins[j], scr)
            pltpu.make_async_copy(buf, outs[j], local.at[nk + 1]).wait()

    return Rider(list(shards), [_sds((N_CHIP,) + a.shape, a.dtype) for a in shards], scratch, start, finish)


def run_alone(rider, name):
    return _call(lambda: None, name=name, out_shape=[], in_specs=[], out_specs=[], grid=(1,), rider=rider)()[1]


def sibling_sum(g, tm, name):
    _, r, cdim = g.shape
    h = r // 2
    ni = h // tm
    dma = pltpu.SemaphoreType.DMA

    def body(cores_ref, keep_ref, give_ref, out_ref, slot, send, recv):
        par = (pl.program_id(0) * ni + pl.program_id(1)) % 2
        x, y, c, _, _ = _place()
        cp = pltpu.make_async_remote_copy(src_ref=give_ref, dst_ref=slot.at[par], send_sem=send.at[par],
                                          recv_sem=recv.at[par], device_id=(x, y, 1 - c), device_id_type=MESH)
        cp.start()
        cp.wait_recv()
        out_ref[...] = (keep_ref[...].astype(F32) + slot[par].astype(F32)).astype(out_ref.dtype)
        cp.wait_send()

    flat = g.reshape(N_CHIP * r, cdim)
    return _staged_call(
        body, name=name, grid=(N_CHIP, ni),
        in_specs=[pl.BlockSpec((tm, cdim), lambda j, i, cores: ((2 * j + cores[0]) * ni + i, 0)),
                  pl.BlockSpec((tm, cdim), lambda j, i, cores: ((2 * j + cores[1]) * ni + i, 0))],
        out_specs=pl.BlockSpec((None, tm, cdim), lambda j, i, cores: (j, i, 0)),
        out_shape=_sds((N_CHIP, h, cdim), g.dtype),
        scratch_shapes=[pltpu.VMEM((2, tm, cdim), g.dtype), dma((2,)), dma((2,))],
    )(_cores(), flat, flat)


def owner_sum_rider(sums, tiles):
    dma = pltpu.SemaphoreType.DMA
    n = len(sums)
    geo = [(a.shape[1], tm, a.shape[1] // tm) for a, tm in zip(sums, tiles)]
    scratch = []
    for a, (h, tm, nk) in zip(sums, geo):
        cdim = a.shape[2]
        scratch += [pltpu.VMEM(a.shape, a.dtype), pltpu.VMEM((3, h, cdim), a.dtype), pltpu.VMEM((2, h, cdim), F32),
                    dma((3, nk)), dma((3, nk)), dma((nk,)), dma((nk,)), dma((2,))]

    def copies(j, scr):
        part, got, res, send, recv, send2, recv2, local = scr[8 * j:8 * j + 8]
        h, tm, nk = geo[j]
        x, y, c, me, chips = _place()

        def to_owner(t, k):
            chip = 2 * chips[t][0] + chips[t][1]
            return pltpu.make_async_remote_copy(
                src_ref=part.at[chip, pl.ds(k * tm, tm)], dst_ref=got.at[t, pl.ds(k * tm, tm)],
                send_sem=send.at[t, k], recv_sem=recv.at[t, k], device_id=(*chips[t], c), device_id_type=MESH)

        def to_sibling(k):
            place = res.at[c, pl.ds(k * tm, tm)]
            return pltpu.make_async_remote_copy(src_ref=place, dst_ref=place, send_sem=send2.at[k],
                                                recv_sem=recv2.at[k], device_id=(x, y, 1 - c), device_id_type=MESH)

        return part, got, res, local, to_owner, to_sibling, (tm, nk, c, me)

    def start(ins, outs, scr):
        for j in range(n):
            part, _, _, local, _, _, _ = copies(j, scr)
            pltpu.make_async_copy(ins[j], part, local.at[0]).start()
        for j in range(n):
            part, _, _, local, to_owner, _, (tm, nk, c, me) = copies(j, scr)
            pltpu.make_async_copy(ins[j], part, local.at[0]).wait()
            for k in range(nk):
                for t in range(3):
                    to_owner(t, k).start()

    def finish(ins, outs, scr):
        for j in range(n):
            part, got, res, _, to_owner, to_sibling, (tm, nk, c, me) = copies(j, scr)
            for k in range(nk):
                rows = pl.ds(k * tm, tm)
                for t in range(3):
                    to_owner(t, k).wait_recv()
                acc = part[me, rows, :].astype(F32)
                for t in range(3):
                    acc = acc + got[t, rows, :].astype(F32)
                res[c, rows, :] = acc
                to_sibling(k).start()
        for j in range(n):
            _, _, res, local, to_owner, to_sibling, (tm, nk, c, me) = copies(j, scr)
            for k in range(nk):
                to_sibling(k).wait_recv()
            for k in range(nk):
                to_sibling(k).wait_send()
                for t in range(3):
                    to_owner(t, k).wait_send()
            pltpu.make_async_copy(res, outs[j], local.at[1]).start()
        for j in range(n):
            _, _, res, local, _, _, _ = copies(j, scr)
            pltpu.make_async_copy(res, outs[j], local.at[1]).wait()

    return Rider(list(sums), [_sds((2, a.shape[1], a.shape[2]), F32) for a in sums], scratch, start, finish)


def gather_conv_w(w):
    def body(in_ref, out_ref, send, recv):
        x, y, c, me, chips = _place()
        out_ref[me] = in_ref[...]
        copies = []
        for t in range(3):
            copies.append(pltpu.make_async_remote_copy(
                src_ref=out_ref.at[me], dst_ref=out_ref.at[me], send_sem=send.at[t], recv_sem=recv.at[t],
                device_id=(*chips[t], c), device_id_type=MESH))
        for cp in copies:
            cp.start()
        for cp in copies:
            cp.wait_recv()
        for cp in copies:
            cp.wait_send()

    dma = pltpu.SemaphoreType.DMA
    vmem = pl.BlockSpec(memory_space=pltpu.VMEM)
    return pl.pallas_call(
        body, out_shape=_sds((N_CHIP,) + w.shape, w.dtype), in_specs=[vmem], out_specs=vmem, name="gather_conv_w",
        scratch_shapes=[dma((3,)), dma((3,))],
        compiler_params=pltpu.CompilerParams(has_side_effects=True))(w)


N_DEV = 8
SMALL_ROWS = 32
SMALL_LANES = 1024


def all_reduce_small(arrays):
    n_arr = len(arrays)
    places = []
    for k, a in enumerate(arrays):
        for ri in range(a.shape[0]):
            for c0 in range(0, a.shape[1], SMALL_LANES):
                places.append((k, ri, c0, min(SMALL_LANES, a.shape[1] - c0), len(places)))
    assert len(places) <= SMALL_ROWS

    def body(*refs):
        ins, outs = refs[:n_arr], refs[n_arr:2 * n_arr]
        buf, send, recv = refs[2 * n_arr:]
        x, y, c, _, _ = _place()
        me = 4 * x + 2 * y + c
        buf[me] = jnp.zeros((SMALL_ROWS, SMALL_LANES), F32)
        for k, ri, c0, width, row in places:
            buf[me, row:row + 1, 0:width] = ins[k][ri:ri + 1, c0:c0 + width]
        copies = []
        for r in range(1, N_DEV):
            px = (1 - x) if r & 4 else x
            py = (1 - y) if r & 2 else y
            pc = (1 - c) if r & 1 else c
            copies.append(pltpu.make_async_remote_copy(
                src_ref=buf.at[me], dst_ref=buf.at[me], send_sem=send.at[r - 1], recv_sem=recv.at[r - 1],
                device_id=(px, py, pc), device_id_type=MESH))
        for cp in copies:
            cp.start()
        for cp in copies:
            cp.wait_recv()
        for cp in copies:
            cp.wait_send()
        acc = buf[0]
        for j in range(1, N_DEV):
            acc = acc + buf[j]
        for k, ri, c0, width, row in places:
            outs[k][ri:ri + 1, c0:c0 + width] = acc[row:row + 1, 0:width]

    dma = pltpu.SemaphoreType.DMA
    vmem = pl.BlockSpec(memory_space=pltpu.VMEM)
    return pl.pallas_call(
        body, out_shape=[_sds(a.shape, F32) for a in arrays], in_specs=[vmem] * n_arr, out_specs=[vmem] * n_arr,
        name="all_reduce_small",
        scratch_shapes=[pltpu.VMEM((N_DEV, SMALL_ROWS, SMALL_LANES), F32), dma((N_DEV - 1,)), dma((N_DEV - 1,))],
        compiler_params=pltpu.CompilerParams(has_side_effects=True))(*arrays)


def _row_tile(rows, limit, multiple):
    return max(t for t in range(multiple, min(rows, limit) + 1, multiple) if rows % t == 0)


def add_pair(a, b, name):
    _, h, c = a.shape

    def body(a_ref, b_ref, o_ref):
        o_ref[...] = (a_ref[...].astype(F32) + b_ref[...].astype(F32)).astype(o_ref.dtype)

    blk = pl.BlockSpec((None, h, c), lambda j: (j, 0, 0))
    return _call(body, name=name, out_shape=_sds(a.shape, a.dtype), grid=(N_CHIP,), in_specs=[blk, blk],
                 out_specs=blk)(a, b)


def sum_slots(buf, name):
    _, h, c = buf.shape
    tm = _row_tile(h, 256, 16)

    def body(b_ref, o_ref):
        acc = b_ref[3].astype(F32)
        for t in range(3):
            acc = acc + b_ref[t].astype(F32)
        o_ref[...] = acc

    return _call(body, name=name, out_shape=_sds((h, c), F32), grid=(h // tm,),
                 in_specs=[pl.BlockSpec((N_CHIP, tm, c), lambda i: (0, i, 0))],
                 out_specs=pl.BlockSpec((tm, c), lambda i: (i, 0)))(buf)


def _adamw_math(w, g, m, v):
    c1 = 1.0 - ADAM_B1 ** ADAM_STEP
    c2 = 1.0 - ADAM_B2 ** ADAM_STEP
    m2 = ADAM_B1 * m + (1.0 - ADAM_B1) * g
    v2 = ADAM_B2 * v + (1.0 - ADAM_B2) * (g * g)
    return -ADAM_LR * ((m2 / c1) / (jnp.sqrt(v2 / c2) + ADAM_EPS) + ADAM_WD * w), m2, v2


def adamw(w, g, row_off, m, v, name):
    _, r, c = w.shape
    tm = r if r < 8 else _row_tile(math.gcd(r, row_off) if row_off else r, 128, 8)

    def body(w_ref, g_ref, m_ref, v_ref, go_ref, d_ref, m2_ref, v2_ref):
        gv = g_ref[...]
        go_ref[...] = gv
        d_ref[...], m2_ref[...], v2_ref[...] = _adamw_math(w_ref[...], gv, m_ref[...], v_ref[...])

    blk = pl.BlockSpec((None, tm, c), lambda i: (0, i, 0))
    shp = _sds((1, r, c), F32)
    return _call(body, name=name, out_shape=[shp] * 4, grid=(r // tm,),
                 in_specs=[blk, pl.BlockSpec((tm, c), lambda i: (row_off // tm + i, 0)), blk, blk],
                 out_specs=[blk] * 4)(w, g, m, v)


def adamw_small(ws, gs, ms, vs):
    n = len(ws)

    def body(*refs):
        ins, outs = refs[:4 * n], refs[4 * n:]
        for k in range(n):
            w_ref, g_ref, m_ref, v_ref = (ins[j * n + k] for j in range(4))
            outs[k][...], outs[n + k][...], outs[2 * n + k][...] = _adamw_math(w_ref[...], g_ref[...], m_ref[...],
                                                                               v_ref[...])

    vmem = pl.BlockSpec(memory_space=pltpu.VMEM)
    shapes = [_sds(w.shape, F32) for w in ws] * 3
    res = pl.pallas_call(body, out_shape=shapes, in_specs=[vmem] * (4 * n), out_specs=[vmem] * (3 * n),
                         name="adamw_small")(*ws, *gs, *ms, *vs)
    return res[:n], res[n:2 * n], res[2 * n:]


BIG = ("ffn1_w_gate", "ffn1_w_up", "ffn1_w_down", "w_in", "w_attn_branch", "w_ssd_branch", "w_out",
       "ffn2_w_gate", "ffn2_w_up", "ffn2_w_down")
SMALL = ("ffn1_norm", "mix_norm", "q_norm", "k_norm", "conv_b", "dt_bias", "a_log", "d_skip", "ssd_norm", "ffn2_norm")
WEIGHTS = ("ffn1_norm", "ffn1_w_gate", "ffn1_w_up", "ffn1_w_down", "mix_norm", "w_in", "q_norm", "k_norm", "conv_w",
           "conv_b", "dt_bias", "a_log", "d_skip", "ssd_norm", "w_attn_branch", "w_ssd_branch", "w_out", "ffn2_norm",
           "ffn2_w_gate", "ffn2_w_up", "ffn2_w_down")
CONV_SHARD = SSD_CONV_DIM // N_CHIP
CLASSES = {
    "ffn1_in": (("ffn1_w_gate", 1024), ("ffn1_w_up", 1024)),
    "ffn1_out": (("ffn1_w_down", 704),),
    "mix_in": (("w_in", 1024),),
    "mix_attn": (("w_attn_branch", 512),),
    "late_out": (("ffn2_w_down", 704), ("w_ssd_branch", 512), ("w_out", 256)),
    "ffn2_in": (("ffn2_w_gate", 1024), ("ffn2_w_up", 1024)),
}
CLASS_TILE = {"ffn1_in": 256, "ffn1_out": 176, "mix_in": 128, "mix_attn": 256, "late_out": 368, "ffn2_in": 256,
              "mix_in_top": 128, "mix_in_bottom": 128}
SIBLING_TILE = {"ffn1_in": 1024, "ffn1_out": 352, "mix_attn": 256, "late_out": 736, "ffn2_in": 1024,
                "mix_in_top": 256, "mix_in_bottom": 256}


def _pack_small(vals, conv_part, loss_part=None):
    flat = [vals[k].reshape(-1) for k in SMALL]
    flat.append(jnp.zeros((SSD_CONV * SSD_CONV_DIM,), F32) if conv_part is None else conv_part.reshape(-1))
    flat.append(jnp.zeros((1,), F32) if loss_part is None else loss_part.reshape(1))
    flat = jnp.concatenate(flat)
    return jnp.pad(flat, (0, SMALL_ROWS * D_MODEL - flat.shape[0])).reshape(SMALL_ROWS, D_MODEL)


def _unpack_small(pack, like):
    flat = pack.reshape(-1)
    out, off = {}, 0
    for k in SMALL:
        n = like[k].size
        out[k] = flat[off:off + n].reshape(like[k].shape)
        off += n
    conv = flat[off:off + SSD_CONV * SSD_CONV_DIM].reshape(SSD_CONV, SSD_CONV_DIM)
    return out, conv, flat[off + SSD_CONV * SSD_CONV_DIM]


def _chip_major_cols(a):
    r = a.shape[0]
    return a.reshape(r, N_CHIP, -1).transpose(1, 0, 2)


def _from_chip_major_cols(a):
    return a.transpose(1, 0, 2).reshape(a.shape[1], -1)


def kernel(x, ffn1_norm, ffn1_w_gate, ffn1_w_up, ffn1_w_down, mix_norm, w_in, q_norm, k_norm, conv_w, conv_b, dt_bias, a_log, d_skip, ssd_norm, w_attn_branch, w_ssd_branch, w_out, ffn2_norm, ffn2_w_gate, ffn2_w_up, ffn2_w_down, loss_target, m_ffn1_norm, m_ffn1_w_gate, m_ffn1_w_up, m_ffn1_w_down, m_mix_norm, m_w_in, m_q_norm, m_k_norm, m_conv_w, m_conv_b, m_dt_bias, m_a_log, m_d_skip, m_ssd_norm, m_w_attn_branch, m_w_ssd_branch, m_w_out, m_ffn2_norm, m_ffn2_w_gate, m_ffn2_w_up, m_ffn2_w_down, v_ffn1_norm, v_ffn1_w_gate, v_ffn1_w_up, v_ffn1_w_down, v_mix_norm, v_w_in, v_q_norm, v_k_norm, v_conv_w, v_conv_b, v_dt_bias, v_a_log, v_d_skip, v_ssd_norm, v_w_attn_branch, v_w_ssd_branch, v_w_out, v_ffn2_norm, v_ffn2_w_gate, v_ffn2_w_up, v_ffn2_w_down):
    env = dict(locals())
    wts = {k: env[k] for k in WEIGHTS}
    moms = {k: env["m_" + k] for k in WEIGHTS}
    vars_ = {k: env["v_" + k] for k in WEIGHTS}
    x0 = x[0]
    target = loss_target[0]

    def gather(classes, more=()):
        shards = [jnp.concatenate([wts[k][0] for k, _ in CLASSES[c]], axis=0).astype(BF16) for c in classes]
        return gather_rider(shards + [a for a, _ in more], [CLASS_TILE[c] for c in classes] + [t for _, t in more])

    def reducer(classes, parts):
        sums = [sibling_sum(p, SIBLING_TILE[c], f"sibling_sum_{c}") for c, p in zip(classes, parts)]
        return owner_sum_rider(sums, [CLASS_TILE[c] for c in classes])

    half = D_MODEL // 2
    in_tile = CLASS_TILE["mix_in_top"]
    x1, saved1, (w_ffn1_in,), (w_ffn1_out, w_mix_attn, w_in_top), (w_in_bottom,) = ffn_forward(
        x0, ffn1_norm, lambda rode: rode[0], lambda rode: rode[0], "ffn1", rms_rider=gather(["ffn1_in"]),
        up_rider=gather(["ffn1_out", "mix_attn"], [(w_in[0, :half].astype(BF16), in_tile)]),
        down_rider=gather([], [(w_in[0, half:].astype(BF16), in_tile)]))
    dt0, dt1 = IN_DT0 - 3 * IN_SHARD, IN_DT1 - 3 * IN_SHARD

    def in_columns(w4):
        return jnp.concatenate([w4[0], w4[1], w4[2], w4[3][:, :dt0], w4[3][:, dt1:]], axis=1), w4[3][:, dt0:dt1]

    (main_top, dt_top), (main_bottom, dt_bottom) = in_columns(w_in_top), in_columns(w_in_bottom)
    mixer_w = dict(
        mix_norm=mix_norm,
        w_in_main=jnp.concatenate([main_top, main_bottom], axis=0),
        w_in_dt=jnp.pad(jnp.concatenate([dt_top, dt_bottom], axis=0), ((0, 0), (0, DT_PAD - SSD_HEADS))),
        q_gain=jnp.tile(q_norm, (1, 2)), k_gain=jnp.tile(k_norm, (1, 2)),
        conv_w=_from_chip_major_cols(gather_conv_w(conv_w[0])), conv_b=conv_b, dt_bias=dt_bias, a_log=a_log,
        d_skip=d_skip, ssd_norm=ssd_norm, w_attn_branch=_from_chip_major_cols(w_mix_attn))

    def later_weights(rode):
        late = rode[0]
        return dict(w_ssd_branch=late[:, 704:1216].reshape(SSD_INNER, D_MODEL),
                    w_out=late[:, 1216:1472].reshape(D_MODEL, D_MODEL))

    x2, saved_mix, (w_late_out, w_ffn2_in) = mixer_forward(x1, mixer_w, gather(["late_out", "ffn2_in"]), later_weights)
    (dx3, sq), saved2, _, _, _ = ffn_forward(x2, ffn2_norm, lambda rode: w_ffn2_in, lambda rode: w_late_out, "ffn2",
                                             target=target)

    grads = {}
    dx2, grads["ffn2_norm"], d_ffn2_in, d_ffn2_down, _ = ffn_backward(dx3, x2, ffn2_norm, w_ffn2_in, w_late_out,
                                                                      saved2, "ffn2")

    def ride_early(g):
        late = jnp.concatenate([d_ffn2_down, g["w_ssd_branch"].reshape(N_CHIP, -1, D_MODEL),
                                g["w_out"].reshape(N_CHIP, -1, D_MODEL)], axis=1)
        return reducer(["ffn2_in", "late_out"], [d_ffn2_in, late])

    g_in_rows = {}

    def ride_late(g):
        main = g["w_in_main"]
        last = jnp.concatenate([main[:, 3 * IN_SHARD:IN_DT0], g["w_in_dt"][:, :SSD_HEADS], main[:, IN_DT0:]], axis=1)
        for part, rows in (("top", slice(0, D_MODEL // 2)), ("bottom", slice(D_MODEL // 2, D_MODEL))):
            g_in_rows[part] = jnp.stack([main[rows, j * IN_SHARD:(j + 1) * IN_SHARD] for j in range(3)]
                                        + [last[rows]])
        return reducer(["mix_in_top", "mix_attn"], [g_in_rows["top"], _chip_major_cols(g["w_attn_branch"])])

    dx1, gmix = mixer_backward(dx2, x1, saved_mix, ride_early, ride_late)
    dx0, grads["ffn1_norm"], rode_in, rode_out, rode_hidden = ffn_backward(
        dx1, x0, ffn1_norm, w_ffn1_in, w_ffn1_out, saved1, "ffn1",
        hidden_rider=reducer(["mix_in_bottom"], [g_in_rows["bottom"]]),
        ride_down=lambda d: reducer(["ffn1_out"], [d]), ride_in=lambda d: reducer(["ffn1_in"], [d]))
    for k in ("mix_norm", "q_norm", "k_norm", "conv_b", "dt_bias", "a_log", "d_skip", "ssd_norm"):
        grads[k] = gmix[k]
    reduced = dict(zip(("ffn2_in", "late_out", "mix_in_top", "mix_attn", "ffn1_in", "ffn1_out", "mix_in_bottom"),
                       (*gmix["rode_early"], *gmix["rode_late"], rode_in[0], rode_out[0], rode_hidden[0])))
    reduced = {c: r.reshape(-1, r.shape[2]) for c, r in reduced.items()}
    reduced["mix_in"] = jnp.concatenate([reduced.pop("mix_in_top"), reduced.pop("mix_in_bottom")], axis=0)
    summed = all_reduce_small([grads[k] for k in SMALL]
                              + [gmix["conv_w"], (0.5 * jnp.sum(sq) / D_MODEL).reshape(1, 1)])
    g_small = dict(zip(SMALL, summed))
    loss = summed[-1].reshape(())
    chip = 2 * lax.axis_index("x") + lax.axis_index("y")
    g_conv = lax.dynamic_slice_in_dim(summed[-2], chip * CONV_SHARD, CONV_SHARD, axis=1)

    g_final, delta, new_m, new_v = dict(g_small), {}, {}, {}

    def update(k, g_arr, row_off):
        w, m, v = wts[k], moms[k], vars_[k]
        rows, cols = w.shape[1:]
        if cols % 128:
            res = adamw(jnp.swapaxes(w, 1, 2), g_arr[row_off:row_off + rows].T, 0, jnp.swapaxes(m, 1, 2),
                        jnp.swapaxes(v, 1, 2), f"adamw_{k}")
            res = [jnp.swapaxes(r, 1, 2) for r in res]
        else:
            res = adamw(w, g_arr, row_off, m, v, f"adamw_{k}")
        g_final[k], delta[k], new_m[k], new_v[k] = res

    for cls, members in CLASSES.items():
        off = 0
        for k, rows in members:
            update(k, reduced[cls], off)
            off += rows
    update("conv_w", g_conv, 0)
    small = adamw_small(*([d[k] for k in SMALL] for d in (wts, g_small, moms, vars_)))
    for res, vals in zip((delta, new_m, new_v), small):
        res.update(zip(SMALL, vals))

    return (loss, dx0[None], *[g_final[k] for k in WEIGHTS], *[delta[k] for k in WEIGHTS],
            *[new_m[k] for k in WEIGHTS], *[new_v[k] for k in WEIGHTS])
```

```python
import collections
import functools
import math

import jax
import jax.numpy as jnp
from jax import lax
from jax.experimental import pallas as pl
from jax.experimental.pallas import tpu as pltpu

F32 = jnp.float32
BF16 = jnp.bfloat16
MESH = pl.DeviceIdType.MESH

EPS = 1e-6
D_MODEL = 1024
D_FF = 2816
N_CHIP = 4
FF_SHARD = D_FF // N_CHIP
HD = 64
BLK = 128
ATTN_DILATIONS = (1, 4, 16)
HEADS_PER_PATTERN = 8
N_ATTN_HEADS = 24
ALIBI_MAX_EXP = 8.0
ATTN_QKV = 1536
GROUP_W = 512
SSD_INNER = 2048
SSD_HEADS = 32
SSD_GROUPS = 4
SSD_CONV = 4
SSD_CONV_DIM = 3072
IN_COLS = 11808
IN_DT0, IN_DT1 = 9728, 9760
IN_SHARD = IN_COLS // 4
COL_K, COL_V, COL_Z, COL_XBC, COL_GA, COL_GS, P_COLS = 1536, 3072, 4608, 6656, 9728, 10752, 11776
DT_PAD = 128

ADAM_LR, ADAM_B1, ADAM_B2, ADAM_EPS, ADAM_WD, ADAM_STEP = 0.001, 0.9, 0.999, 1e-08, 0.01, 10

V7X_VMEM_LIMIT = 56 * 1024 * 1024
NEG = -1e30


Rider = collections.namedtuple("Rider", "arrays out_shape scratch start finish")
Rider.__doc__ = """An exchange between devices that rides in a compute kernel: its copies are started in the host's
first grid step and waited for in its last, so they travel while the host computes.  arrays / out_shape: extra HBM
operands and results; scratch: extra scratch; start, finish: f(in_refs, out_refs, scratch_refs)."""


def _call(body, *, name, out_shape, in_specs, out_specs, grid=(), scratch_shapes=(), aliases=None, rider=None):
    params = dict(dimension_semantics=("arbitrary",) * len(grid), vmem_limit_bytes=V7X_VMEM_LIMIT)
    if rider is None:
        return pl.pallas_call(
            body, out_shape=out_shape, grid=grid, in_specs=in_specs, out_specs=out_specs,
            scratch_shapes=scratch_shapes, input_output_aliases=aliases or {}, name=name,
            compiler_params=pltpu.CompilerParams(**params))
    single = not isinstance(out_shape, (list, tuple))
    main_out = [out_shape] if single else list(out_shape)
    main_specs = [out_specs] if single else list(out_specs)
    n_in, n_out, n_scr = len(in_specs), len(main_out), len(scratch_shapes)
    r_in, r_out = len(rider.arrays), len(rider.out_shape)

    def wrapped(*refs):
        ins, refs = refs[:n_in], refs[n_in:]
        r_ins, refs = refs[:r_in], refs[r_in:]
        outs, refs = refs[:n_out], refs[n_out:]
        r_outs, refs = refs[:r_out], refs[r_out:]
        scr, r_scr = refs[:n_scr], refs[n_scr:]
        first = last = None
        for axis, size in enumerate(grid):
            at_start, at_end = pl.program_id(axis) == 0, pl.program_id(axis) == size - 1
            first = at_start if first is None else jnp.logical_and(first, at_start)
            last = at_end if last is None else jnp.logical_and(last, at_end)

        @pl.when(first)
        def _():
            rider.start(r_ins, r_outs, r_scr)

        body(*ins, *outs, *scr)

        @pl.when(last)
        def _():
            rider.finish(r_ins, r_outs, r_scr)

    hbm = pl.BlockSpec(memory_space=pl.ANY)
    call = pl.pallas_call(
        wrapped, out_shape=main_out + list(rider.out_shape), grid=grid, in_specs=list(in_specs) + [hbm] * r_in,
        out_specs=main_specs + [hbm] * r_out, scratch_shapes=list(scratch_shapes) + list(rider.scratch), name=name,
        compiler_params=pltpu.CompilerParams(has_side_effects=True, **params))

    def run(*args):
        res = call(*args, *rider.arrays)
        main = res[:n_out]
        return (main[0] if single else main), res[n_out:]

    return run


def _sds(shape, dtype):
    return jax.ShapeDtypeStruct(tuple(shape), dtype)


def _dot(a, b):
    return jnp.dot(a, b, preferred_element_type=F32)


def _dot_nt(a, b):
    return lax.dot_general(a, b, (((1,), (1,)), ((), ())), preferred_element_type=F32)


def _dot_tn(a, b):
    return lax.dot_general(a, b, (((0,), (0,)), ((), ())), preferred_element_type=F32)


def _dot_hi(a, b):
    return jnp.dot(a, b, preferred_element_type=F32, precision=lax.Precision.HIGHEST)


def _sigmoid(x):
    return 1.0 / (1.0 + jnp.exp(-x))


def _lane_first_half(shape):
    return lax.broadcasted_iota(jnp.int32, shape, len(shape) - 1) < HD


def _rowwise(name, fn, rows, consts, outs, accs=(), tm=512, rider=None):
    n_rows = None
    in_arrays, in_specs = [], []
    for r in rows:
        if isinstance(r, tuple):
            arr, w, cb = r
            spec = pl.BlockSpec((tm, w), functools.partial(lambda i, cb: (i, cb), cb=cb))
        else:
            arr = r
            spec = pl.BlockSpec((tm, arr.shape[1]), lambda i: (i, 0))
        n_rows = arr.shape[0]
        in_arrays.append(arr)
        in_specs.append(spec)
    for c in consts:
        in_arrays.append(c)
        in_specs.append(pl.BlockSpec(c.shape, functools.partial(lambda i, n: (0,) * n, n=c.ndim)))
    out_shape = [_sds(s, d) for s, d in outs] + [_sds(s, d) for s, d in accs]
    out_specs = [pl.BlockSpec((tm, s[1]), lambda i: (i, 0)) for s, _ in outs]
    out_specs += [pl.BlockSpec(s, functools.partial(lambda i, n: (0,) * n, n=len(s))) for s, _ in accs]

    def body(*refs):
        fn(pl.program_id(0), *refs)

    res = _call(body, name=name, out_shape=out_shape, grid=(n_rows // tm,), in_specs=in_specs,
                out_specs=out_specs, rider=rider)(*in_arrays)
    return res


def rms_fwd(x, gain, name, rider=None):
    def fn(i, x_ref, g_ref, h_ref):
        xv = x_ref[...]
        r = lax.rsqrt(jnp.mean(xv * xv, axis=-1, keepdims=True) + EPS)
        h_ref[...] = (xv * r * g_ref[...]).astype(h_ref.dtype)

    res = _rowwise(name, fn, [x], [gain], [(x.shape, BF16)], rider=rider)
    return res[0] if rider is None else (res[0][0], res[1])


def rms_bwd(dhs, x, gain, dx_in, name):
    n = len(dhs)

    def fn(i, *refs):
        dh_refs, (x_ref, dxin_ref, g_ref, dx_ref, dg_ref) = refs[:n], refs[n:]
        dh = dh_refs[0][...]
        for r in dh_refs[1:]:
            dh = dh + r[...]
        xv = x_ref[...]
        r = lax.rsqrt(jnp.mean(xv * xv, axis=-1, keepdims=True) + EPS)
        xn = xv * r
        dxn = dh * g_ref[...]
        dx_ref[...] = dxin_ref[...] + r * (dxn - xn * jnp.mean(dxn * xn, axis=-1, keepdims=True))

        @pl.when(i == 0)
        def _():
            dg_ref[...] = jnp.zeros_like(dg_ref)

        dg_ref[...] += jnp.sum(dh * xn, axis=0, keepdims=True)

    return _rowwise(name, fn, list(dhs) + [x, dx_in], [gain], [(x.shape, F32)], [((1, x.shape[1]), F32)])


def matmul_nn(a, b, name, out_dtype, tm, tn, res=None, scale=1.0, rider=None):
    s, k = a.shape
    n = b.shape[1]

    def body(*refs):
        if res is None:
            a_ref, b_ref, o_ref = refs
            o_ref[...] = _dot(a_ref[...], b_ref[...]).astype(o_ref.dtype)
        else:
            a_ref, b_ref, r_ref, o_ref = refs
            o_ref[...] = (r_ref[...] + scale * _dot(a_ref[...], b_ref[...])).astype(o_ref.dtype)

    in_specs = [pl.BlockSpec((tm, k), lambda i, j: (i, 0)), pl.BlockSpec((k, tn), lambda i, j: (0, j))]
    args = [a, b]
    if res is not None:
        in_specs.append(pl.BlockSpec((tm, tn), lambda i, j: (i, j)))
        args.append(res)
    return _call(body, name=name, out_shape=_sds((s, n), out_dtype), grid=(s // tm, n // tn), in_specs=in_specs,
                 out_specs=pl.BlockSpec((tm, tn), lambda i, j: (i, j)), rider=rider)(*args)


def matmul_nt(a, b, name, out_dtype, tm, tn, tk, rider=None):
    s, k = a.shape
    n = b.shape[0]
    nk = k // tk

    def body(a_ref, b_ref, o_ref, acc_ref):
        kk = pl.program_id(2)

        @pl.when(kk == 0)
        def _():
            acc_ref[...] = jnp.zeros_like(acc_ref)

        acc_ref[...] += _dot_nt(a_ref[...].astype(BF16), b_ref[...])

        @pl.when(kk == nk - 1)
        def _():
            o_ref[...] = acc_ref[...].astype(o_ref.dtype)

    return _call(body, name=name, out_shape=_sds((s, n), out_dtype), grid=(s // tm, n // tn, nk),
                 in_specs=[pl.BlockSpec((tm, tk), lambda i, j, kk: (i, kk)),
                           pl.BlockSpec((tn, tk), lambda i, j, kk: (j, kk))],
                 out_specs=pl.BlockSpec((tm, tn), lambda i, j, kk: (i, j)),
                 scratch_shapes=[pltpu.VMEM((tm, tn), F32)], rider=rider)(a, b)


def matmul_tn(a, b, name, tn, ts, a_scale=None, b_scale=None, rider=None):
    s, m = a.shape
    n = b.shape[1]
    ns = s // ts

    def body(a_ref, b_ref, o_ref, acc_ref):
        ss = pl.program_id(1)

        @pl.when(ss == 0)
        def _():
            acc_ref[...] = jnp.zeros_like(acc_ref)

        av, bv = a_ref[...], b_ref[...]
        if a_scale is not None:
            av = av * a_scale
        if b_scale is not None:
            bv = bv * b_scale
        acc_ref[...] += _dot_tn(av.astype(BF16), bv.astype(BF16))

        @pl.when(ss == ns - 1)
        def _():
            o_ref[...] = acc_ref[...].astype(o_ref.dtype)

    return _call(body, name=name, out_shape=_sds((m, n), BF16), grid=(n // tn, ns),
                 in_specs=[pl.BlockSpec((ts, m), lambda j, ss: (ss, 0)), pl.BlockSpec((ts, tn), lambda j, ss: (ss, j))],
                 out_specs=pl.BlockSpec((m, tn), lambda j, ss: (0, j)),
                 scratch_shapes=[pltpu.VMEM((m, tn), F32)], rider=rider)(a, b)


def _piece_specs(pieces, tile, rows_tile, tile_axis_first):
    specs, ranges, t0 = [], [], 0
    for a in pieces:
        n = a.shape[1] // tile

        def index(*ids, t0=t0, n=n):
            t, r = (ids[0], ids[1]) if tile_axis_first else (ids[2], ids[0])
            on = jnp.logical_and(t >= t0, t < t0 + n)
            return jnp.where(on, r, 0), jnp.clip(t - t0, 0, n - 1)

        specs.append(pl.BlockSpec((rows_tile, tile), index))
        ranges.append((t0, n))
        t0 += n
    return specs, ranges


def matmul_tn_pieces(a, pieces, name, tn, ts, rider=None):
    s, m = a.shape
    ns = s // ts
    specs, ranges = _piece_specs(pieces, tn, ts, True)
    n_total = sum(n for _, n in ranges)

    def body(a_ref, *refs):
        b_refs, o_ref, acc_ref = refs[:len(pieces)], refs[-2], refs[-1]
        j, ss = pl.program_id(0), pl.program_id(1)

        @pl.when(ss == 0)
        def _():
            acc_ref[...] = jnp.zeros_like(acc_ref)

        for b_ref, (t0, n) in zip(b_refs, ranges):
            @pl.when(jnp.logical_and(j >= t0, j < t0 + n))
            def _(b_ref=b_ref):
                acc_ref[...] += _dot_tn(a_ref[...], b_ref[...])

        @pl.when(ss == ns - 1)
        def _():
            o_ref[...] = acc_ref[...].astype(o_ref.dtype)

    return _call(body, name=name, out_shape=_sds((m, n_total * tn), BF16), grid=(n_total, ns),
                 in_specs=[pl.BlockSpec((ts, m), lambda j, ss: (ss, 0))] + specs,
                 out_specs=pl.BlockSpec((m, tn), lambda j, ss: (0, j)),
                 scratch_shapes=[pltpu.VMEM((m, tn), F32)], rider=rider)(a, *pieces)


def matmul_nt_pieces(pieces, b, name, out_dtype, tm, tn, tk, rider=None):
    s = pieces[0].shape[0]
    n = b.shape[0]
    specs, ranges = _piece_specs(pieces, tk, tm, False)
    nk = sum(cnt for _, cnt in ranges)

    def body(*refs):
        a_refs, b_ref, o_ref, acc_ref = refs[:len(pieces)], refs[-3], refs[-2], refs[-1]
        kk = pl.program_id(2)

        @pl.when(kk == 0)
        def _():
            acc_ref[...] = jnp.zeros_like(acc_ref)

        for a_ref, (t0, cnt) in zip(a_refs, ranges):
            @pl.when(jnp.logical_and(kk >= t0, kk < t0 + cnt))
            def _(a_ref=a_ref):
                acc_ref[...] += _dot_nt(a_ref[...], b_ref[...])

        @pl.when(kk == nk - 1)
        def _():
            o_ref[...] = acc_ref[...].astype(o_ref.dtype)

    return _call(body, name=name, out_shape=_sds((s, n), out_dtype), grid=(s // tm, n // tn, nk),
                 in_specs=specs + [pl.BlockSpec((tn, tk), lambda i, j, kk: (j, kk))],
                 out_specs=pl.BlockSpec((tm, tn), lambda i, j, kk: (i, j)),
                 scratch_shapes=[pltpu.VMEM((tm, tn), F32)], rider=rider)(*pieces, b)


def ffn_up(h, w704, gate_blk, up_blk, name, tm=512, rider=None):
    s = h.shape[0]

    def body(h_ref, wg_ref, wu_ref, g_ref, u_ref, a_ref):
        hv = h_ref[...]
        g = _dot(hv, wg_ref[...])
        u = _dot(hv, wu_ref[...])
        g_ref[...] = g.astype(BF16)
        u_ref[...] = u.astype(BF16)
        a_ref[...] = (g * _sigmoid(g) * u).astype(BF16)

    ospec = pl.BlockSpec((None, tm, FF_SHARD), lambda j, i: (j, i, 0))
    shp = _sds((N_CHIP, s, FF_SHARD), BF16)
    return _call(body, name=name, out_shape=[shp, shp, shp], grid=(N_CHIP, s // tm),
                 in_specs=[pl.BlockSpec((tm, D_MODEL), lambda j, i: (i, 0)),
                           pl.BlockSpec((None, D_MODEL, FF_SHARD), lambda j, i: (j, gate_blk, 0)),
                           pl.BlockSpec((None, D_MODEL, FF_SHARD), lambda j, i: (j, up_blk, 0))],
                 out_specs=[ospec, ospec, ospec], rider=rider)(h, w704, w704)


def ffn_down(a, w1024, blk, x, name, tm=512, rider=None, target=None):
    s = x.shape[0]

    def block_out(a_ref, wd_ref, x_ref):
        acc = _dot(a_ref[0], wd_ref[0])
        for j in range(1, N_CHIP):
            acc += _dot(a_ref[j], wd_ref[j])
        return x_ref[...] + 0.5 * acc

    def body(a_ref, wd_ref, x_ref, o_ref):
        o_ref[...] = block_out(a_ref, wd_ref, x_ref)

    def loss_body(a_ref, wd_ref, x_ref, t_ref, dy_ref, sq_ref):
        err = block_out(a_ref, wd_ref, x_ref) - t_ref[...]
        dy_ref[...] = err * (1.0 / D_MODEL)

        @pl.when(pl.program_id(0) == 0)
        def _():
            sq_ref[...] = jnp.zeros_like(sq_ref)

        sq_ref[...] += jnp.sum(err * err, axis=0, keepdims=True)

    rows = pl.BlockSpec((tm, D_MODEL), lambda i: (i, 0))
    in_specs = [pl.BlockSpec((N_CHIP, tm, FF_SHARD), lambda i: (0, i, 0)),
                pl.BlockSpec((N_CHIP, FF_SHARD, D_MODEL), lambda i: (0, blk, 0)), rows]
    if target is None:
        return _call(body, name=name, out_shape=_sds((s, D_MODEL), F32), grid=(s // tm,), in_specs=in_specs,
                     out_specs=rows, rider=rider)(a, w1024, x)
    return _call(loss_body, name=name, out_shape=[_sds((s, D_MODEL), F32), _sds((1, D_MODEL), F32)], grid=(s // tm,),
                 in_specs=in_specs + [rows], out_specs=[rows, pl.BlockSpec((1, D_MODEL), lambda i: (0, 0))],
                 rider=rider)(a, w1024, x, target)


def ffn_bwd_hidden(dx, w1024, blk, g, u, name, tm=1024, rider=None):
    s = dx.shape[0]

    def body(dx_ref, wd_ref, g_ref, u_ref, dg_ref, du_ref):
        dy = (0.5 * dx_ref[...]).astype(BF16)
        da = _dot_nt(dy, wd_ref[...])
        gv = g_ref[...].astype(F32)
        uv = u_ref[...].astype(F32)
        sg = _sigmoid(gv)
        dg_ref[...] = (da * uv * (sg * (1.0 + gv * (1.0 - sg)))).astype(BF16)
        du_ref[...] = (da * gv * sg).astype(BF16)

    hspec = pl.BlockSpec((None, tm, FF_SHARD), lambda j, i: (j, i, 0))
    shp = _sds((N_CHIP, s, FF_SHARD), BF16)
    return _call(body, name=name, out_shape=[shp, shp], grid=(N_CHIP, s // tm),
                 in_specs=[pl.BlockSpec((tm, D_MODEL), lambda j, i: (i, 0)),
                           pl.BlockSpec((None, FF_SHARD, D_MODEL), lambda j, i: (j, blk, 0)), hspec, hspec],
                 out_specs=[hspec, hspec], rider=rider)(dx, w1024, g, u)


def ffn_bwd_input(dg, du, w704, gate_blk, up_blk, x, gain, dy, name, tm=512, rider=None):
    s = dg.shape[1]

    def body(dg_ref, du_ref, wg_ref, wu_ref, x_ref, gain_ref, dy_ref, dx_ref, dgain_ref):
        dh = _dot_nt(dg_ref[0], wg_ref[0]) + _dot_nt(du_ref[0], wu_ref[0])
        for j in range(1, N_CHIP):
            dh += _dot_nt(dg_ref[j], wg_ref[j]) + _dot_nt(du_ref[j], wu_ref[j])
        xv = x_ref[...]
        r = lax.rsqrt(jnp.mean(xv * xv, axis=-1, keepdims=True) + EPS)
        xn = xv * r
        dxn = dh * gain_ref[...]
        dx_ref[...] = dy_ref[...] + r * (dxn - xn * jnp.mean(dxn * xn, axis=-1, keepdims=True))

        @pl.when(pl.program_id(0) == 0)
        def _():
            dgain_ref[...] = jnp.zeros_like(dgain_ref)

        dgain_ref[...] += jnp.sum(dh * xn, axis=0, keepdims=True)

    hspec = pl.BlockSpec((N_CHIP, tm, FF_SHARD), lambda i: (0, i, 0))
    rows = pl.BlockSpec((tm, D_MODEL), lambda i: (i, 0))
    whole = pl.BlockSpec((1, D_MODEL), lambda i: (0, 0))
    return _call(body, name=name, out_shape=[_sds((s, D_MODEL), F32), _sds((1, D_MODEL), F32)], grid=(s // tm,),
                 in_specs=[hspec, hspec,
                           pl.BlockSpec((N_CHIP, D_MODEL, FF_SHARD), lambda i: (0, gate_blk, 0), pl.Buffered(1)),
                           pl.BlockSpec((N_CHIP, D_MODEL, FF_SHARD), lambda i: (0, up_blk, 0), pl.Buffered(1)),
                           rows, whole, rows],
                 out_specs=[rows, whole], rider=rider)(dg, du, w704, w704, x, gain, dy)


def ffn_wgrad_in(h, dgu, name, ts=2048, rider=None):
    s = h.shape[0]
    ns = s // ts

    def body(h_ref, d_ref, o_ref, acc_ref):
        ss = pl.program_id(1)

        @pl.when(ss == 0)
        def _():
            acc_ref[...] = jnp.zeros_like(acc_ref)

        acc_ref[...] += _dot_tn(h_ref[...], d_ref[...])

        @pl.when(ss == ns - 1)
        def _():
            o_ref[...] = acc_ref[...].astype(BF16)

    return _call(body, name=name, out_shape=_sds((N_CHIP, D_MODEL, FF_SHARD), BF16), grid=(N_CHIP, ns),
                 in_specs=[pl.BlockSpec((ts, D_MODEL), lambda j, ss: (ss, 0)),
                           pl.BlockSpec((None, ts, FF_SHARD), lambda j, ss: (j, ss, 0))],
                 out_specs=pl.BlockSpec((None, D_MODEL, FF_SHARD), lambda j, ss: (j, 0, 0)),
                 scratch_shapes=[pltpu.VMEM((D_MODEL, FF_SHARD), F32)], rider=rider)(h, dgu)


def ffn_wgrad_down(a, dx, name, ts=2048):
    s = dx.shape[0]
    ns = s // ts

    def body(a_ref, dx_ref, o_ref, acc_ref):
        ss = pl.program_id(1)

        @pl.when(ss == 0)
        def _():
            acc_ref[...] = jnp.zeros_like(acc_ref)

        acc_ref[...] += _dot_tn(a_ref[...], (0.5 * dx_ref[...]).astype(BF16))

        @pl.when(ss == ns - 1)
        def _():
            o_ref[...] = acc_ref[...].astype(BF16)

    return _call(body, name=name, out_shape=_sds((N_CHIP, FF_SHARD, D_MODEL), BF16), grid=(N_CHIP, ns),
                 in_specs=[pl.BlockSpec((None, ts, FF_SHARD), lambda j, ss: (j, ss, 0)),
                           pl.BlockSpec((ts, D_MODEL), lambda j, ss: (ss, 0))],
                 out_specs=pl.BlockSpec((None, FF_SHARD, D_MODEL), lambda j, ss: (j, 0, 0)),
                 scratch_shapes=[pltpu.VMEM((FF_SHARD, D_MODEL), F32)])(a, dx)


def ffn_forward(x, gain, get_w704, get_w1024, tag, rms_rider=None, up_rider=None, down_rider=None, target=None):
    h = rms_fwd(x, gain, f"{tag}_rms", rider=rms_rider)
    h, rode_rms = h if rms_rider is not None else (h, None)
    res = ffn_up(h, get_w704(rode_rms), 0, 1, f"{tag}_up", rider=up_rider)
    (g, u, a), rode_up = res if up_rider is not None else (res, None)
    y = ffn_down(a, get_w1024(rode_up), 0, x, f"{tag}_down", rider=down_rider, target=target)
    y, rode_down = y if down_rider is not None else (y, None)
    return y, (h, g, u, a), rode_rms, rode_up, rode_down


def ffn_backward(dy, x, gain, w704, w1024, saved, tag, hidden_rider=None, ride_down=None, ride_in=None):
    h, g, u, a = saved
    d_wd = ffn_wgrad_down(a, dy, f"{tag}_dwd")
    rode_hidden = None
    if hidden_rider is not None:
        (dg, du), rode_hidden = ffn_bwd_hidden(dy, w1024, 0, g, u, f"{tag}_dhid", tm=512, rider=hidden_rider)
    else:
        dg, du = ffn_bwd_hidden(dy, w1024, 0, g, u, f"{tag}_dhid")
    if ride_down is not None:
        d_wg, d_wd = ffn_wgrad_in(h, dg, f"{tag}_dwg", rider=ride_down(d_wd))
    else:
        d_wg = ffn_wgrad_in(h, dg, f"{tag}_dwg")
    d_win = jnp.concatenate([d_wg, ffn_wgrad_in(h, du, f"{tag}_dwu")], axis=1)
    if ride_in is not None:
        (dx, d_gain), d_win = ffn_bwd_input(dg, du, w704, 0, 1, x, gain, dy, f"{tag}_dh", tm=256,
                                            rider=ride_in(d_win))
    else:
        dx, d_gain = ffn_bwd_input(dg, du, w704, 0, 1, x, gain, dy, f"{tag}_dh")
    return dx, d_gain, d_win, d_wd, rode_hidden


def _alibi_slope(head):
    return float(2.0 ** (-ALIBI_MAX_EXP * (head + 1) / N_ATTN_HEADS))


def _same_head():
    row = lax.broadcasted_iota(jnp.int32, (2 * HD, 2 * HD), 0)
    col = lax.broadcasted_iota(jnp.int32, (2 * HD, 2 * HD), 1)
    return ((row < HD) == (col < HD)).astype(BF16)


def _head_sums(x, same_head):
    hi = x.astype(BF16)
    lo = (x - hi.astype(F32)).astype(BF16)
    return _dot(hi, same_head) + _dot(lo, same_head)


def _head_norm(t, gain_pair, same_head):
    r = lax.rsqrt(_head_sums(t * t, same_head) * (1.0 / HD) + EPS)
    return t * r * gain_pair, r


def qk_norm_fwd(p, q_gain, k_gain, name):
    s = p.shape[0]

    def fn(i, q_ref, k_ref, qg_ref, kg_ref, qn_ref, kn_ref):
        same_head = _same_head()
        for src, g_ref, dst in ((q_ref, qg_ref, qn_ref), (k_ref, kg_ref, kn_ref)):
            for pr in range(ATTN_QKV // (2 * HD)):
                cols = slice(pr * 2 * HD, (pr + 1) * 2 * HD)
                y, _ = _head_norm(src[:, cols].astype(F32), g_ref[...], same_head)
                dst[:, cols] = y.astype(BF16)

    return _rowwise(name, fn, [(p, ATTN_QKV, 0), (p, ATTN_QKV, 1)], [q_gain, k_gain],
                    [((s, ATTN_QKV), BF16), ((s, ATTN_QKV), BF16)])


def qk_norm_bwd(p, dqs, dks, q_gain, k_gain, name):
    s = p.shape[0]
    pairs_per_pattern = GROUP_W // (2 * HD)

    def fn(i, q_ref, k_ref, dq0, dq1, dq2, dk0, dk1, dk2, qg_ref, kg_ref, dqk_ref, dqg_ref, dkg_ref):
        same_head = _same_head()

        @pl.when(i == 0)
        def _():
            dqg_ref[...] = jnp.zeros_like(dqg_ref)
            dkg_ref[...] = jnp.zeros_like(dkg_ref)

        for src, d_refs, g_ref, dst, dg_ref in (
                (q_ref, (dq0, dq1, dq2), qg_ref, dqk_ref.at[:, 0:ATTN_QKV], dqg_ref),
                (k_ref, (dk0, dk1, dk2), kg_ref, dqk_ref.at[:, ATTN_QKV:2 * ATTN_QKV], dkg_ref)):
            for pr in range(ATTN_QKV // (2 * HD)):
                cols = slice(pr * 2 * HD, (pr + 1) * 2 * HD)
                t = src[:, cols].astype(F32)
                r = lax.rsqrt(_head_sums(t * t, same_head) * (1.0 / HD) + EPS)
                xn = t * r
                within = (pr % pairs_per_pattern) * 2 * HD
                dy = d_refs[pr // pairs_per_pattern][:, within:within + 2 * HD]
                dg_ref[:, cols] += jnp.sum(dy * xn, axis=0, keepdims=True)
                dxn = dy * g_ref[...]
                mean = _head_sums(dxn * xn, same_head) * (1.0 / HD)
                dst[:, cols] = (r * (dxn - xn * mean)).astype(BF16)

    return _rowwise(name, fn, [(p, ATTN_QKV, 0), (p, ATTN_QKV, 1)] + list(dqs) + list(dks), [q_gain, k_gain],
                    [((s, 2 * ATTN_QKV), BF16)], [((1, ATTN_QKV), F32), ((1, ATTN_QKV), F32)])


def _to_streams(a, d):
    if d == 1:
        return a
    s, c = a.shape
    return a.reshape(s // d, d, c).transpose(1, 0, 2).reshape(s, c)


def _from_streams(a, d):
    if d == 1:
        return a
    s, c = a.shape
    return a.reshape(d, s // d, c).transpose(1, 0, 2).reshape(s, c)


def attn_merge_fwd(os_, lses, name):
    s = os_[0].shape[0]

    def fn(i, o0, o1, o2, l0, l1, l2, out_ref):
        m = jnp.maximum(jnp.maximum(l0[...], l1[...]), l2[...])
        e0, e1, e2 = jnp.exp(l0[...] - m), jnp.exp(l1[...] - m), jnp.exp(l2[...] - m)
        inv = 1.0 / (e0 + e1 + e2)
        out_ref[...] = ((e0 * inv) * o0[...] + (e1 * inv) * o1[...] + (e2 * inv) * o2[...]).astype(BF16)

    return _rowwise(name, fn, list(os_) + list(lses), [], [((s, GROUP_W), BF16)])[0]


def attn_merge_bwd(d_out, os_, lses, name):
    s = d_out.shape[0]

    def fn(i, do_ref, o0, o1, o2, l0, l1, l2, d0, d1, d2, c0, c1, c2):
        m = jnp.maximum(jnp.maximum(l0[...], l1[...]), l2[...])
        e0, e1, e2 = jnp.exp(l0[...] - m), jnp.exp(l1[...] - m), jnp.exp(l2[...] - m)
        inv = 1.0 / (e0 + e1 + e2)
        w0, w1, w2 = e0 * inv, e1 * inv, e2 * inv
        do = do_ref[...]
        prod = do * (w0 * o0[...] + w1 * o1[...] + w2 * o2[...])
        same_head = _same_head()
        for pr in range(GROUP_W // (2 * HD)):
            cols = slice(pr * 2 * HD, (pr + 1) * 2 * HD)
            t = _head_sums(prod[:, cols], same_head)
            for w, c_ref in ((w0, c0), (w1, c1), (w2, c2)):
                c_ref[:, cols] = w[:, cols] * t
        for w, d_ref in ((w0, d0), (w1, d1), (w2, d2)):
            d_ref[...] = (w * do).astype(BF16)

    shp = (s, GROUP_W)
    return _rowwise(name, fn, [d_out] + list(os_) + list(lses), [],
                    [(shp, BF16)] * 3 + [(shp, F32)] * 3)


def _band_constants(d):
    row = lax.broadcasted_iota(jnp.int32, (2 * BLK, 2 * BLK), 0)
    col = lax.broadcasted_iota(jnp.int32, (2 * BLK, 2 * BLK), 1)
    rel = BLK + jnp.where(row >= BLK, row - BLK, row) - col
    band = jnp.logical_and(rel >= 0, rel <= BLK)
    return (rel * d).astype(F32), band, (col >= BLK).astype(jnp.int32)


def _stack_heads(x, first):
    zero = jnp.zeros_like(x)
    return jnp.concatenate([jnp.where(first, x, zero), jnp.where(first, zero, x)], axis=0)


def _unstack_heads(x2, first):
    return jnp.where(first, x2[:BLK], x2[BLK:])


def _head_column(x):
    return jnp.concatenate([x[:, 0:1], x[:, HD:HD + 1]], axis=0)


def attn_fwd(q, k, v, pattern, name, tq=1024):
    s = q.shape[0]
    d = ATTN_DILATIONS[pattern]
    blocks_per_stream = (s // d) // BLK
    nsb = tq // BLK

    def body(q_ref, k_ref, v_ref, kp_ref, vp_ref, o_ref, l_ref):
        i = pl.program_id(0)
        rel_f, band, own = _band_constants(d)
        first = _lane_first_half((BLK, 2 * HD))
        upper = lax.broadcasted_iota(jnp.int32, (2 * BLK, 1), 0) < BLK
        for sb in range(nsb):
            rows = slice(sb * BLK, (sb + 1) * BLK)
            has_prev = ((i * nsb + sb) % blocks_per_stream != 0).astype(jnp.int32)
            mask = jnp.logical_and(band, (own + has_prev) > 0)
            for pr in range(GROUP_W // (2 * HD)):
                cols = slice(pr * 2 * HD, (pr + 1) * 2 * HD)
                if sb == 0:
                    kcat = jnp.concatenate([kp_ref[:, cols], k_ref[rows, cols]], axis=0)
                    vcat = jnp.concatenate([vp_ref[:, cols], v_ref[rows, cols]], axis=0)
                else:
                    both = slice((sb - 1) * BLK, (sb + 1) * BLK)
                    kcat, vcat = k_ref[both, cols], v_ref[both, cols]
                h0 = pattern * HEADS_PER_PATTERN + 2 * pr
                slope = jnp.where(upper, _alibi_slope(h0), _alibi_slope(h0 + 1))
                sc = _dot_nt(_stack_heads(q_ref[rows, cols], first), kcat) * 0.125 - slope * rel_f
                sc = jnp.where(mask, sc, NEG)
                m = jnp.max(sc, axis=-1, keepdims=True)
                p = jnp.exp(sc - m)
                l = jnp.sum(p, axis=-1, keepdims=True)
                o2 = _dot((p * (1.0 / l)).astype(BF16), vcat)
                o_ref[rows, cols] = _unstack_heads(o2, first)
                lse = m + jnp.log(l)
                l_ref[rows, cols] = jnp.where(first, lse[:BLK], lse[BLK:])

    cur = pl.BlockSpec((tq, GROUP_W), lambda i: (i, 0))
    prev = pl.BlockSpec((BLK, GROUP_W), lambda i: (jnp.maximum(i * nsb - 1, 0), 0))
    return _call(body, name=name, out_shape=[_sds((s, GROUP_W), F32), _sds((s, GROUP_W), F32)], grid=(s // tq,),
                 in_specs=[cur, cur, cur, prev, prev], out_specs=[cur, cur])(q, k, v, k, v)


def attn_bwd(q, k, v, d_o, cterm, lse, pattern, name, tq=1024):
    s = q.shape[0]
    d = ATTN_DILATIONS[pattern]
    blocks_per_stream = (s // d) // BLK
    nsb = tq // BLK
    n_blocks = s // BLK

    def body(q_ref, k_ref, v_ref, do_ref, c_ref, l_ref, kp_ref, vp_ref, qn_ref, kn_ref, vn_ref, don_ref, cn_ref,
             ln_ref, dq_ref, dk_ref, dv_ref):
        i = pl.program_id(0)
        rel_f, band, own = _band_constants(d)
        first = _lane_first_half((BLK, 2 * HD))
        upper = lax.broadcasted_iota(jnp.int32, (2 * BLK, 1), 0) < BLK
        dk_ref[...] = jnp.zeros_like(dk_ref)
        dv_ref[...] = jnp.zeros_like(dv_ref)
        for sb in range(nsb + 1):
            gb = i * nsb + sb
            rows = slice(sb * BLK, (sb + 1) * BLK)
            before = slice((sb - 1) * BLK, sb * BLK)
            inside = (gb < n_blocks).astype(jnp.int32)
            has_prev = jnp.logical_and(gb % blocks_per_stream != 0, gb < n_blocks).astype(jnp.int32)
            mask = jnp.logical_and(band, (own * inside + has_prev) > 0)
            for pr in range(GROUP_W // (2 * HD)):
                cols = slice(pr * 2 * HD, (pr + 1) * 2 * HD)
                if sb == 0:
                    kcat = jnp.concatenate([kp_ref[:, cols], k_ref[rows, cols]], axis=0)
                    vcat = jnp.concatenate([vp_ref[:, cols], v_ref[rows, cols]], axis=0)
                elif sb == nsb:
                    kcat = jnp.concatenate([k_ref[before, cols], kn_ref[:, cols]], axis=0)
                    vcat = jnp.concatenate([v_ref[before, cols], vn_ref[:, cols]], axis=0)
                else:
                    both = slice((sb - 1) * BLK, (sb + 1) * BLK)
                    kcat, vcat = k_ref[both, cols], v_ref[both, cols]
                if sb < nsb:
                    qp, dop, cp, lp = q_ref[rows, cols], do_ref[rows, cols], c_ref[rows, cols], l_ref[rows, cols]
                else:
                    qp, dop, cp, lp = qn_ref[:, cols], don_ref[:, cols], cn_ref[:, cols], ln_ref[:, cols]
                h0 = pattern * HEADS_PER_PATTERN + 2 * pr
                slope = jnp.where(upper, _alibi_slope(h0), _alibi_slope(h0 + 1))
                q2 = _stack_heads(qp, first)
                do2 = _stack_heads(dop, first)
                sc = jnp.where(mask, _dot_nt(q2, kcat) * 0.125 - slope * rel_f, NEG)
                pm = jnp.exp(sc - _head_column(lp))
                dl = (pm * (_dot_nt(do2, vcat) - _head_column(cp))).astype(BF16)
                if sb < nsb:
                    dq_ref[rows, cols] = _unstack_heads(_dot(dl, kcat), first) * 0.125
                dk2 = _dot_tn(dl, q2) * 0.125
                dv2 = _dot_tn(pm.astype(BF16), do2)
                if sb > 0:
                    dk_ref[before, cols] += dk2[:BLK]
                    dv_ref[before, cols] += dv2[:BLK]
                if sb < nsb:
                    dk_ref[rows, cols] += dk2[BLK:]
                    dv_ref[rows, cols] += dv2[BLK:]

    cur = pl.BlockSpec((tq, GROUP_W), lambda i: (i, 0))
    prev = pl.BlockSpec((BLK, GROUP_W), lambda i: (jnp.maximum(i * nsb - 1, 0), 0))
    nxt = pl.BlockSpec((BLK, GROUP_W), lambda i: (jnp.minimum((i + 1) * nsb, n_blocks - 1), 0))
    shp = _sds((s, GROUP_W), F32)
    return _call(body, name=name, out_shape=[shp, shp, shp], grid=(s // tq,),
                 in_specs=[cur] * 6 + [prev, prev] + [nxt] * 6, out_specs=[cur, cur, cur])(
                     q, k, v, d_o, cterm, lse, k, v, q, k, v, d_o, cterm, lse)


HALO = 16
CONV_TQ = 512


def conv_fwd(p, w, b, name):
    s = p.shape[0]
    tq = CONV_TQ
    ncol = SSD_CONV_DIM // GROUP_W
    cb0 = COL_XBC // GROUP_W

    def body(u_ref, up_ref, w_ref, b_ref, c_ref, xc_ref):
        i = pl.program_id(0)
        prev = (up_ref[...].astype(F32) * (i > 0).astype(F32)).astype(BF16)
        ext = jnp.concatenate([prev, u_ref[...]], axis=0)
        row = lax.broadcasted_iota(jnp.int32, (BLK, BLK + HALO), 0)
        col = lax.broadcasted_iota(jnp.int32, (BLK, BLK + HALO), 1)
        for blk in range(tq // BLK):
            lead = ext[blk * BLK:blk * BLK + BLK + HALO]
            acc = b_ref[...] + w_ref[SSD_CONV - 1:SSD_CONV, :] * lead[HALO:].astype(F32)
            for kk in range(SSD_CONV - 1):
                pick = (col == row + HALO - (SSD_CONV - 1 - kk)).astype(BF16)
                acc += w_ref[kk:kk + 1, :] * _dot(pick, lead)
            rows = slice(blk * BLK, (blk + 1) * BLK)
            c_ref[rows, :] = acc.astype(BF16)
            xc_ref[rows, :] = (acc * _sigmoid(acc)).astype(BF16)

    cur_in = pl.BlockSpec((tq, GROUP_W), lambda i, j: (i, cb0 + j))
    prev_in = pl.BlockSpec((HALO, GROUP_W), lambda i, j: (jnp.maximum(i * (tq // HALO) - 1, 0), cb0 + j))
    cur_out = pl.BlockSpec((tq, GROUP_W), lambda i, j: (i, j))
    shp = _sds((s, SSD_CONV_DIM), BF16)
    return _call(body, name=name, out_shape=[shp, shp], grid=(s // tq, ncol),
                 in_specs=[cur_in, prev_in, pl.BlockSpec((SSD_CONV, GROUP_W), lambda i, j: (0, j)),
                           pl.BlockSpec((1, GROUP_W), lambda i, j: (0, j))],
                 out_specs=[cur_out, cur_out])(p, p, w, b)


def conv_bwd(p, cpre, dxs, d_b, d_c, w, name):
    s = p.shape[0]
    tq = CONV_TQ
    ncol = SSD_CONV_DIM // GROUP_W
    n_xs = SSD_INNER // GROUP_W
    cb0 = COL_XBC // GROUP_W
    nt = s // tq

    def body(u_ref, c_ref, cn_ref, dx_ref, dxn_ref, dbm_ref, dbmn_ref, dcm_ref, dcmn_ref, w_ref,
             du_ref, dw_ref, db_ref):
        j, i = pl.program_id(0), pl.program_id(1)

        def dpre(c16, dx):
            c = c16.astype(F32)
            sg = _sigmoid(c)
            return dx * (sg * (1.0 + c * (1.0 - sg)))

        def pick(a_ref, b_ref, c_ref_):
            return jnp.where(j < n_xs, a_ref[...], jnp.where(j == n_xs, b_ref[...], c_ref_[...]))

        dc = dpre(c_ref[...], pick(dx_ref, dbm_ref, dcm_ref))
        dcn = dpre(cn_ref[...], pick(dxn_ref, dbmn_ref, dcmn_ref)) * (i < nt - 1).astype(F32)
        dext = jnp.concatenate([dc, dcn], axis=0)
        u = u_ref[...].astype(F32)

        @pl.when(i == 0)
        def _():
            dw_ref[...] = jnp.zeros_like(dw_ref)
            db_ref[...] = jnp.zeros_like(db_ref)

        du = w_ref[SSD_CONV - 1:SSD_CONV, :] * dc
        dw_ref[SSD_CONV - 1:SSD_CONV, :] += jnp.sum(dc * u, axis=0, keepdims=True)
        for kk in range(SSD_CONV - 1):
            sh = SSD_CONV - 1 - kk
            ahead = pltpu.roll(dext, tq + HALO - sh, 0)[0:tq]
            du += w_ref[kk:kk + 1, :] * ahead
            dw_ref[kk:kk + 1, :] += jnp.sum(ahead * u, axis=0, keepdims=True)
        du_ref[...] = du.astype(BF16)
        db_ref[...] += jnp.sum(dc, axis=0, keepdims=True)

    hb = tq // HALO
    cur_p = pl.BlockSpec((tq, GROUP_W), lambda j, i: (i, cb0 + j))
    cur = pl.BlockSpec((tq, GROUP_W), lambda j, i: (i, j))
    nxt = pl.BlockSpec((HALO, GROUP_W), lambda j, i: (jnp.minimum((i + 1) * hb, s // HALO - 1), j))

    def piece(first_tile, n_tiles):
        def on(j):
            return jnp.logical_and(j >= first_tile, j < first_tile + n_tiles)

        def col(j):
            return jnp.clip(j - first_tile, 0, n_tiles - 1)

        return (pl.BlockSpec((tq, GROUP_W), lambda j, i: (jnp.where(on(j), i, 0), col(j))),
                pl.BlockSpec((HALO, GROUP_W),
                             lambda j, i: (jnp.where(on(j), jnp.minimum((i + 1) * hb, s // HALO - 1), 0), col(j))))

    return _call(body, name=name,
                 out_shape=[_sds((s, SSD_CONV_DIM), BF16), _sds((8, SSD_CONV_DIM), F32), _sds((1, SSD_CONV_DIM), F32)],
                 grid=(ncol, nt),
                 in_specs=[cur_p, cur, nxt, *piece(0, n_xs), *piece(n_xs, 1), *piece(n_xs + 1, 1),
                           pl.BlockSpec((SSD_CONV, GROUP_W), lambda j, i: (0, j))],
                 out_specs=[cur, pl.BlockSpec((8, GROUP_W), lambda j, i: (0, j)),
                            pl.BlockSpec((1, GROUP_W), lambda j, i: (0, j))])(
                                p, cpre, cpre, dxs, dxs, d_b, d_b, d_c, d_c, w)


def _softplus(x):
    return jnp.maximum(x, 0.0) + jnp.log(1.0 + jnp.exp(-jnp.abs(x)))


def _ssd_decays(dtr_ref, dtrt_ref, bias_ref, biast_ref, alog_ref, alogt_ref):
    row = lax.broadcasted_iota(jnp.int32, (BLK, BLK), 0)
    col = lax.broadcasted_iota(jnp.int32, (BLK, BLK), 1)
    lower = (row >= col).astype(F32)
    upper = (row <= col).astype(F32)
    dtb = dtr_ref[...] + bias_ref[...]
    dt = _softplus(dtb)
    a = dt * (-jnp.exp(alog_ref[...]))
    cs = _dot_hi(lower, a)
    a_t = _softplus(dtrt_ref[...] + biast_ref[...]) * (-jnp.exp(alogt_ref[...]))
    cs_t = _dot_hi(a_t, upper)
    return dtb, dt, cs, cs_t, row, col, upper


SSD_GROUPS_PER_STEP = 4


def _per_group(body, gps, kinds):
    def wrapped(*refs):
        for gi in range(gps):
            args, pos = [], 0
            for kind, n in kinds:
                if kind == "each":
                    args.append(refs[pos + gi])
                    pos += gps
                    continue
                ref = refs[pos]
                pos += 1
                if kind == "cols":
                    args.append(ref.at[:, gi * n:(gi + 1) * n])
                else:
                    args.append(ref.at[gi] if n == 1 else ref.at[pl.ds(gi * n, n)])
            body(*args)

    return wrapped


def ssd_fwd(p, xc, dtg, dtg_t, params, gn, name):
    s = p.shape[0]
    nc = s // BLK
    bias, bias_t, alog, alog_t, dskip = params

    def body(xs_ref, b_ref, c_ref, z_ref, dtr_ref, dtrt_ref, bias_ref, biast_ref, alog_ref, alogt_ref, dsk_ref,
             gn_ref, y_ref, sin_ref, hp_ref, h_ref):
        c_idx = pl.program_id(1)

        @pl.when(c_idx == 0)
        def _():
            h_ref[...] = jnp.zeros_like(h_ref)

        _, dt, cs, cs_t, row, col, _ = _ssd_decays(dtr_ref, dtrt_ref, bias_ref, biast_ref, alog_ref, alogt_ref)
        first = _lane_first_half((BLK, 2 * HD))
        first_row = _lane_first_half((1, 2 * HD))
        tril = row >= col
        b16, c16 = b_ref[...], c_ref[...]
        cb = _dot_nt(c16, b16)
        n_pairs = GROUP_W // (2 * HD)
        tot = cs[BLK - 1:BLK, :]
        exp_cs, exp_rest, exp_tot = jnp.exp(cs), jnp.exp(tot - cs), jnp.exp(tot)

        lanes_of_head = (lax.broadcasted_iota(jnp.int32, (8, GROUP_W), 1) // HD
                         == lax.broadcasted_iota(jnp.int32, (8, GROUP_W), 0)).astype(BF16)

        def per_head(v, mask):
            if v.shape[0] == 1:
                return jnp.concatenate([jnp.where(mask, v[:, 2 * pr:2 * pr + 1], v[:, 2 * pr + 1:2 * pr + 2])
                                        for pr in range(n_pairs)], axis=1)
            hi = v.astype(BF16)
            lo = (v - hi.astype(F32)).astype(BF16)
            return _dot(hi, lanes_of_head) + _dot(lo, lanes_of_head)

        xs = xs_ref[...].astype(F32)
        xt = xs * per_head(dt, first)
        xt16 = xt.astype(BF16)
        hstate = jnp.concatenate([h_ref[pr] for pr in range(n_pairs)], axis=1)
        for pr in range(n_pairs):
            hp_ref[pr] = h_ref[pr]
        y_off = per_head(exp_cs, first) * _dot(c16, hstate.astype(BF16))
        new = per_head(exp_tot, first_row) * hstate + _dot_tn(b16, (per_head(exp_rest, first) * xt).astype(BF16))
        for pr in range(n_pairs):
            h_ref[pr] = new[:, pr * 2 * HD:(pr + 1) * 2 * HD]
        y_diag = []
        for pr in range(n_pairs):
            cols = slice(pr * 2 * HD, (pr + 1) * 2 * HD)
            m2 = jnp.concatenate(
                [(cb * jnp.exp(jnp.where(tril, cs[:, h:h + 1] - cs_t[h:h + 1, :], NEG))).astype(BF16)
                 for h in (2 * pr, 2 * pr + 1)], axis=1)
            y_diag.append(_dot(m2, _stack_heads(xt16[:, cols], first)))
        y = jnp.concatenate(y_diag, axis=1) + y_off + xs * per_head(dsk_ref[...], first_row)
        y_ref[...] = y
        zv = z_ref[...].astype(F32)
        yz = y * (zv * _sigmoid(zv))
        r = lax.rsqrt(jnp.mean(yz * yz, axis=-1, keepdims=True) + EPS)
        sin_ref[...] = (yz * r * gn_ref[...]).astype(BF16)

    gps = SSD_GROUPS_PER_STEP
    wide, narrow, lead = ("cols", GROUP_W), ("cols", BLK), ("lead", 1)
    kinds = [wide, narrow, narrow, ("each", 0)] + [lead] * 7 + [wide, wide, wide, lead, ("lead", 4)]
    wide_w, narrow_w = GROUP_W * gps, BLK * gps
    gparam = pl.BlockSpec((gps, 1, 8), lambda g, c: (g, 0, 0))
    gparam_t = pl.BlockSpec((gps, 8, 1), lambda g, c: (g, 0, 0))
    z_specs = [pl.BlockSpec((BLK, GROUP_W), functools.partial(lambda g, c, gi: (c, COL_Z // GROUP_W + gps * g + gi),
                                                              gi=gi)) for gi in range(gps)]
    return _call(
        _per_group(body, gps, kinds), name=name,
        out_shape=[_sds((s, SSD_INNER), F32), _sds((s, SSD_INNER), BF16),
                   _sds((SSD_GROUPS, nc, 4, BLK, 2 * HD), F32)],
        grid=(SSD_GROUPS // gps, nc),
        in_specs=[pl.BlockSpec((BLK, wide_w), lambda g, c: (c, g)),
                  pl.BlockSpec((BLK, narrow_w), lambda g, c: (c, SSD_INNER // narrow_w + g)),
                  pl.BlockSpec((BLK, narrow_w), lambda g, c: (c, (SSD_INNER + SSD_GROUPS * BLK) // narrow_w + g)),
                  *z_specs,
                  pl.BlockSpec((gps, BLK, 8), lambda g, c: (g, c, 0)),
                  pl.BlockSpec((gps, 8, BLK), lambda g, c: (g, 0, c)),
                  gparam, gparam_t, gparam, gparam_t, gparam,
                  pl.BlockSpec((1, wide_w), lambda g, c: (0, g))],
        out_specs=[pl.BlockSpec((BLK, wide_w), lambda g, c: (c, g)),
                   pl.BlockSpec((BLK, wide_w), lambda g, c: (c, g)),
                   pl.BlockSpec((gps, None, 4, BLK, 2 * HD), lambda g, c: (g, c, 0, 0, 0))],
        scratch_shapes=[pltpu.VMEM((4 * gps, BLK, 2 * HD), F32)],
    )(xc, xc, xc, *([p] * gps), dtg, dtg_t, bias, bias_t, alog, alog_t, dskip, gn)


def ssd_bwd(p, xc, y, d_sin, hprev, dtg, dtg_t, params, gn, name):
    s = p.shape[0]
    nc = s // BLK
    bias, bias_t, alog, alog_t, dskip = params

    def body(xs_ref, b_ref, c_ref, z_ref, y_ref, dsin_ref, hp_ref, dtr_ref, dtrt_ref, bias_ref,
             biast_ref, alog_ref, alogt_ref, dsk_ref, gn_ref,
             dxs_ref, db_ref, dc_ref, dz_ref, ddt_ref, da_ref, dbias_ref, ddsk_ref, dgn_ref, dh_ref):
        c_idx = pl.program_id(1)

        @pl.when(c_idx == 0)
        def _():
            dh_ref[...] = jnp.zeros_like(dh_ref)
            da_ref[...] = jnp.zeros_like(da_ref)
            dbias_ref[...] = jnp.zeros_like(dbias_ref)
            ddsk_ref[...] = jnp.zeros_like(ddsk_ref)
            dgn_ref[...] = jnp.zeros_like(dgn_ref)

        dtb, dt, cs, cs_t, row, col, upper = _ssd_decays(dtr_ref, dtrt_ref, bias_ref, biast_ref, alog_ref, alogt_ref)
        first = _lane_first_half((BLK, 2 * HD))
        first_row = _lane_first_half((1, 2 * HD))
        tril = row >= col
        triu = row <= col
        last_row = lax.broadcasted_iota(jnp.int32, (BLK, 1), 0) == BLK - 1
        lane8 = lax.broadcasted_iota(jnp.int32, (BLK, 8), 1)

        yv = y_ref[...]
        zv = z_ref[...].astype(F32)
        sg = _sigmoid(zv)
        yz = yv * (zv * sg)
        r = lax.rsqrt(jnp.mean(yz * yz, axis=-1, keepdims=True) + EPS)
        yzn = yz * r
        dsn = dsin_ref[...]
        dgn_ref[...] += jnp.sum(dsn * yzn, axis=0, keepdims=True)
        dsn = dsn * gn_ref[...]
        dyz = r * (dsn - yzn * jnp.mean(dsn * yzn, axis=-1, keepdims=True))
        dy = dyz * (zv * sg)
        dz_ref[...] = (dyz * yv * (sg * (1.0 + zv * (1.0 - sg)))).astype(BF16)
        xs_all = xs_ref[...].astype(F32)
        ddsk_ref[...] += jnp.sum(dy * xs_all, axis=0, keepdims=True)

        b16, c16 = b_ref[...], c_ref[...]
        cb = _dot_nt(c16, b16)
        cb_t = _dot_nt(b16, c16)
        n_pairs = GROUP_W // (2 * HD)
        tot = cs[BLK - 1:BLK, :]
        exp_cs, exp_rest, exp_tot = jnp.exp(cs), jnp.exp(tot - cs), jnp.exp(tot)

        lanes_of_head = (lax.broadcasted_iota(jnp.int32, (8, GROUP_W), 1) // HD
                         == lax.broadcasted_iota(jnp.int32, (8, GROUP_W), 0)).astype(BF16)

        def per_head(v, mask):
            if v.shape[0] == 1:
                return jnp.concatenate([jnp.where(mask, v[:, 2 * pr:2 * pr + 1], v[:, 2 * pr + 1:2 * pr + 2])
                                        for pr in range(n_pairs)], axis=1)
            hi = v.astype(BF16)
            lo = (v - hi.astype(F32)).astype(BF16)
            return _dot(hi, lanes_of_head) + _dot(lo, lanes_of_head)

        head_of_lane = (lax.broadcasted_iota(jnp.int32, (GROUP_W, 8), 0) // HD
                        == lax.broadcasted_iota(jnp.int32, (GROUP_W, 8), 1)).astype(BF16)

        def head_sums(v):
            hi = v.astype(BF16)
            lo = (v - hi.astype(F32)).astype(BF16)
            return _dot(hi, head_of_lane) + _dot(lo, head_of_lane)

        dt_w, e_w, f_w = per_head(dt, first), per_head(exp_cs, first), per_head(exp_rest, first)
        xt = xs_all * dt_w
        xt16 = xt.astype(BF16)
        hstate = jnp.concatenate([hp_ref[pr] for pr in range(n_pairs)], axis=1)
        h16 = hstate.astype(BF16)
        dhn = jnp.concatenate([dh_ref[pr] for pr in range(n_pairs)], axis=1)
        dhn16 = dhn.astype(BF16)
        edy16 = (e_w * dy).astype(BF16)
        y_off = e_w * _dot(c16, h16)
        dcs_all = head_sums(dy * y_off)
        dc_acc = _dot_nt(edy16, h16)
        zmat = _dot(b16, dhn16)
        t_all = head_sums(zmat * xt) * exp_rest
        hh_rows = jnp.sum(head_sums(dhn * hstate), axis=0, keepdims=True)
        dtot = jnp.sum(t_all, axis=0, keepdims=True) + hh_rows * exp_tot
        dcs_all = dcs_all - t_all + jnp.where(last_row, dtot, 0.0)
        fxt16 = (f_w * xt).astype(BF16)
        db_acc = _dot_nt(fxt16, dhn16)
        dh_new = _dot_tn(c16, edy16) + per_head(exp_tot, first_row) * dhn
        for pr in range(n_pairs):
            dh_ref[pr] = dh_new[:, pr * 2 * HD:(pr + 1) * 2 * HD]
        g_sum = jnp.zeros((BLK, BLK), F32)
        gt_sum = jnp.zeros((BLK, BLK), F32)
        d_xt_parts = []
        for pr in range(n_pairs):
            cols = slice(pr * 2 * HD, (pr + 1) * 2 * HD)
            dym2 = _stack_heads(dy[:, cols].astype(BF16), first)
            d_m2 = _dot_nt(dym2, xt16[:, cols])
            d_mt2 = _dot_nt(xt16[:, cols], dym2)
            mt2 = []
            for e, h in enumerate((2 * pr, 2 * pr + 1)):
                cs_c, cs_r = cs[:, h:h + 1], cs_t[h:h + 1, :]
                decay = jnp.exp(jnp.where(tril, cs_c - cs_r, NEG))
                decay_t = jnp.exp(jnp.where(triu, cs_r - cs_c, NEG))
                gm = d_m2[e * BLK:(e + 1) * BLK] * decay
                gmt = d_mt2[:, e * BLK:(e + 1) * BLK] * decay_t
                g_sum += gm
                gt_sum += gmt
                dcs_h = jnp.sum(gm * cb, axis=-1, keepdims=True) - jnp.sum(gmt * cb_t, axis=-1, keepdims=True)
                dcs_all = dcs_all + jnp.where(lane8 == h, dcs_h, 0.0)
                mt2.append((cb_t * decay_t).astype(BF16))
            d_xt_parts.append(_dot(jnp.concatenate(mt2, axis=1), dym2))
        d_xt = jnp.concatenate(d_xt_parts, axis=1) + f_w * zmat
        dxs_ref[...] = dy * per_head(dsk_ref[...], first_row) + d_xt * dt_w
        ddtx_all = head_sums(d_xt * xs_all)

        dc_ref[...] = dc_acc + _dot(g_sum.astype(BF16), b16)
        db_ref[...] = db_acc + _dot(gt_sum.astype(BF16), c16)
        d_a = _dot_hi(upper, dcs_all)
        a_neg = -jnp.exp(alog_ref[...])
        ddt = ddtx_all + d_a * a_neg
        da_ref[...] += jnp.sum(d_a * dt, axis=0, keepdims=True)
        ddtr = ddt * _sigmoid(dtb)
        ddt_ref[...] = ddtr
        dbias_ref[...] += jnp.sum(ddtr, axis=0, keepdims=True)

    gps = SSD_GROUPS_PER_STEP
    k_wide, k_narrow, k_lead = ("cols", GROUP_W), ("cols", BLK), ("lead", 1)
    kinds = ([k_wide, k_narrow, k_narrow, ("each", 0), k_wide, k_wide] + [k_lead] * 8 + [k_wide]
             + [k_wide, k_narrow, k_narrow, k_wide] + [k_lead] * 4 + [k_wide] + [("lead", 4)])
    wide_w, narrow_w = GROUP_W * gps, BLK * gps
    rc = lambda c: nc - 1 - c
    gparam = pl.BlockSpec((gps, 1, 8), lambda g, c: (g, 0, 0))
    gparam_t = pl.BlockSpec((gps, 8, 1), lambda g, c: (g, 0, 0))
    wide = pl.BlockSpec((BLK, wide_w), lambda g, c: (rc(c), g))
    narrow = pl.BlockSpec((BLK, narrow_w), lambda g, c: (rc(c), g))
    z_specs = [pl.BlockSpec((BLK, GROUP_W),
                            functools.partial(lambda g, c, gi: (rc(c), COL_Z // GROUP_W + gps * g + gi), gi=gi))
               for gi in range(gps)]
    return _call(
        _per_group(body, gps, kinds), name=name,
        out_shape=[_sds((s, SSD_INNER), F32), _sds((s, GROUP_W), F32), _sds((s, GROUP_W), F32),
                   _sds((s, SSD_INNER), BF16), _sds((SSD_GROUPS, s, 8), F32),
                   _sds((SSD_GROUPS, 1, 8), F32), _sds((SSD_GROUPS, 1, 8), F32),
                   _sds((SSD_GROUPS, 1, GROUP_W), F32), _sds((1, SSD_INNER), F32)],
        grid=(SSD_GROUPS // gps, nc),
        in_specs=[wide,
                  pl.BlockSpec((BLK, narrow_w), lambda g, c: (rc(c), SSD_INNER // narrow_w + g)),
                  pl.BlockSpec((BLK, narrow_w), lambda g, c: (rc(c), (SSD_INNER + SSD_GROUPS * BLK) // narrow_w + g)),
                  *z_specs,
                  wide, wide,
                  pl.BlockSpec((gps, None, 4, BLK, 2 * HD), lambda g, c: (g, rc(c), 0, 0, 0)),
                  pl.BlockSpec((gps, BLK, 8), lambda g, c: (g, rc(c), 0)),
                  pl.BlockSpec((gps, 8, BLK), lambda g, c: (g, 0, rc(c))),
                  gparam, gparam_t, gparam, gparam_t, gparam,
                  pl.BlockSpec((1, wide_w), lambda g, c: (0, g))],
        out_specs=[wide, narrow, narrow, wide,
                   pl.BlockSpec((gps, BLK, 8), lambda g, c: (g, rc(c), 0)),
                   gparam, gparam,
                   pl.BlockSpec((gps, 1, GROUP_W), lambda g, c: (g, 0, 0)),
                   pl.BlockSpec((1, wide_w), lambda g, c: (0, g))],
        scratch_shapes=[pltpu.VMEM((4 * gps, BLK, 2 * HD), F32)],
    )(xc, xc, xc, *([p] * gps), y, d_sin, hprev, dtg, dtg_t, bias, bias_t, alog, alog_t, dskip, gn)


def merge_fwd(p, a, sbr, name, tm=512):
    s = p.shape[0]
    nj = D_MODEL // GROUP_W

    def body(ga_ref, gs_ref, a_ref, s_ref, o_ref):
        o_ref[...] = (_sigmoid(ga_ref[...].astype(F32)) * a_ref[...]
                      + _sigmoid(gs_ref[...].astype(F32)) * s_ref[...]).astype(BF16)

    blk = pl.BlockSpec((tm, GROUP_W), lambda i, j: (i, j))
    return _call(body, name=name, out_shape=_sds((s, D_MODEL), BF16), grid=(s // tm, nj),
                 in_specs=[pl.BlockSpec((tm, GROUP_W), lambda i, j: (i, COL_GA // GROUP_W + j)),
                           pl.BlockSpec((tm, GROUP_W), lambda i, j: (i, COL_GS // GROUP_W + j)), blk, blk],
                 out_specs=blk)(p, p, a, sbr)


def merge_bwd(p, a, sbr, dmerged, name, tm=512):
    s = p.shape[0]
    nj = D_MODEL // GROUP_W

    def body(ga_ref, gs_ref, a_ref, s_ref, dm_ref, da_ref, ds_ref, dga_ref, dgs_ref):
        dm = dm_ref[...]
        sa = _sigmoid(ga_ref[...].astype(F32))
        ss = _sigmoid(gs_ref[...].astype(F32))
        da_ref[...] = (dm * sa).astype(BF16)
        ds_ref[...] = (dm * ss).astype(BF16)
        dga_ref[...] = (dm * a_ref[...] * sa * (1.0 - sa)).astype(BF16)
        dgs_ref[...] = (dm * s_ref[...] * ss * (1.0 - ss)).astype(BF16)

    blk = pl.BlockSpec((tm, GROUP_W), lambda i, j: (i, j))
    shp = _sds((s, D_MODEL), BF16)
    return _call(body, name=name, out_shape=[shp] * 4, grid=(s // tm, nj),
                 in_specs=[pl.BlockSpec((tm, GROUP_W), lambda i, j: (i, COL_GA // GROUP_W + j)),
                           pl.BlockSpec((tm, GROUP_W), lambda i, j: (i, COL_GS // GROUP_W + j)), blk, blk, blk],
                 out_specs=[blk] * 4)(p, p, a, sbr, dmerged)


def _group_major(v):
    return v.reshape(SSD_GROUPS, 1, 8), v.reshape(SSD_GROUPS, 8, 1)


def mixer_forward(x, w, rider=None, later_weights=None):
    s = x.shape[0]
    h = rms_fwd(x, w["mix_norm"], "mix_rms")
    p = matmul_nn(h, w["w_in_main"], "mix_proj", BF16, tm=1024, tn=512, rider=rider)
    rode = None
    if rider is not None:
        p, rode = p
        w = dict(w, **later_weights(rode))
    dt_raw = matmul_nn(h, w["w_in_dt"], "mix_proj_dt", F32, tm=1024, tn=DT_PAD)
    qn, kn = qk_norm_fwd(p, w["q_gain"], w["k_gain"], "qk_norm")
    streams, os_, lses = [], [], []
    for g, d in enumerate(ATTN_DILATIONS):
        cols = slice(g * GROUP_W, (g + 1) * GROUP_W)
        qs, ks = _to_streams(qn[:, cols], d), _to_streams(kn[:, cols], d)
        vs = _to_streams(p[:, COL_V + g * GROUP_W:COL_V + (g + 1) * GROUP_W], d)
        o, lse = attn_fwd(qs, ks, vs, g, f"attn_fwd{g}")
        streams.append((qs, ks, vs, lse))
        os_.append(_from_streams(o, d))
        lses.append(_from_streams(lse, d))
    attn_o = attn_merge_fwd(os_, lses, "attn_merge")
    cpre, xc = conv_fwd(p, w["conv_w"], w["conv_b"], "conv_fwd")
    dtg = dt_raw[:, :SSD_HEADS].reshape(s, SSD_GROUPS, 8).transpose(1, 0, 2)
    dtg_t = dtg.transpose(0, 2, 1)
    params = (*_group_major(w["dt_bias"]), *_group_major(w["a_log"]), _group_major(w["d_skip"])[0])
    y, s_in, hprev = ssd_fwd(p, xc, dtg, dtg_t, params, w["ssd_norm"], "ssd_fwd")
    a = matmul_nn(attn_o, w["w_attn_branch"], "attn_branch", F32, tm=1024, tn=512)
    sbr = matmul_nn(s_in, w["w_ssd_branch"], "ssd_branch", F32, tm=1024, tn=512)
    merged = merge_fwd(p, a, sbr, "merge")
    x_out = matmul_nn(merged, w["w_out"], "mix_out", F32, tm=1024, tn=512, res=x)
    saved = dict(h=h, p=p, streams=streams, os=os_, lses=lses, attn_o=attn_o, cpre=cpre, xc=xc, dtg=dtg,
                 dtg_t=dtg_t, params=params, y=y, s_in=s_in, hprev=hprev, a=a, sbr=sbr, merged=merged, w=w)
    return x_out, saved, rode


def mixer_backward(dx_out, x, sv, ride_early=None, ride_late=None):
    s = x.shape[0]
    p = sv["p"]
    w = sv["w"]
    g = {}
    dmerged = matmul_nt(dx_out, w["w_out"], "d_merged", F32, tm=1024, tn=512, tk=1024)
    g["w_out"] = matmul_tn(sv["merged"], dx_out, "dw_out", tn=512, ts=1024)
    da, ds, dga, dgs = merge_bwd(p, sv["a"], sv["sbr"], dmerged, "merge_bwd")
    g["w_attn_branch"] = matmul_tn(sv["attn_o"], da, "dw_attn_branch", tn=512, ts=1024)
    g["w_ssd_branch"] = matmul_tn(sv["s_in"], ds, "dw_ssd_branch", tn=512, ts=1024)
    d_attn_o = matmul_nt(da, w["w_attn_branch"], "d_attn_o", F32, tm=1024, tn=512, tk=1024)
    d_sin = matmul_nt(ds, w["w_ssd_branch"], "d_ssd_in", F32, tm=1024, tn=512, tk=1024)
    dxs, d_b, d_c, dz, ddt, d_asum, d_bias, d_dsk, d_gn = ssd_bwd(
        p, sv["xc"], sv["y"], d_sin, sv["hprev"], sv["dtg"], sv["dtg_t"], sv["params"], w["ssd_norm"], "ssd_bwd")
    dxbc, d_convw, d_convb = conv_bwd(p, sv["cpre"], dxs, d_b, d_c, w["conv_w"], "conv_bwd")
    g["conv_w"] = d_convw[:SSD_CONV]
    g["conv_b"] = d_convb
    g["dt_bias"] = d_bias.reshape(1, SSD_HEADS)
    g["a_log"] = (d_asum * (-jnp.exp(sv["params"][2]))).reshape(1, SSD_HEADS)
    g["d_skip"] = jnp.sum(d_dsk.reshape(SSD_HEADS, HD), axis=1).reshape(1, SSD_HEADS)
    g["ssd_norm"] = d_gn
    merged_bwd = attn_merge_bwd(d_attn_o, sv["os"], sv["lses"], "attn_merge_bwd")
    dqs, dks, dvs = [], [], []
    for gi, d in enumerate(ATTN_DILATIONS):
        qs, ks, vs, lse = sv["streams"][gi]
        d_o = _to_streams(merged_bwd[gi], d)
        cterm = _to_streams(merged_bwd[3 + gi], d)
        dq, dk, dv = attn_bwd(qs, ks, vs, d_o, cterm, lse, gi, f"attn_bwd{gi}")
        dqs.append(_from_streams(dq, d))
        dks.append(_from_streams(dk, d))
        dvs.append(_from_streams(dv, d).astype(BF16))
    dqk, d_qg, d_kg = qk_norm_bwd(p, dqs, dks, w["q_gain"], w["k_gain"], "qk_norm_bwd")
    g["q_norm"] = jnp.sum(d_qg.reshape(N_ATTN_HEADS, HD), axis=0).reshape(1, HD)
    g["k_norm"] = jnp.sum(d_kg.reshape(N_ATTN_HEADS, HD), axis=0).reshape(1, HD)
    dp = [dqk, jnp.concatenate(dvs, axis=1), dz, dxbc, dga, dgs]
    ddt_pad = jnp.pad(ddt.transpose(1, 0, 2).reshape(s, SSD_HEADS), ((0, 0), (0, DT_PAD - SSD_HEADS)))
    if ride_early is not None:
        g["w_in_main"], g["rode_early"] = matmul_tn_pieces(sv["h"], dp, "dw_in", tn=512, ts=1024,
                                                           rider=ride_early(g))
    else:
        g["w_in_main"] = matmul_tn_pieces(sv["h"], dp, "dw_in", tn=512, ts=1024)
    g["w_in_dt"] = matmul_tn(sv["h"], ddt_pad, "dw_in_dt", tn=DT_PAD, ts=1024)
    if ride_late is not None:
        dh_main, g["rode_late"] = matmul_nt_pieces(dp, w["w_in_main"], "d_mix_h", F32, tm=1024, tn=1024, tk=512,
                                                   rider=ride_late(g))
    else:
        dh_main = matmul_nt_pieces(dp, w["w_in_main"], "d_mix_h", F32, tm=1024, tn=512, tk=512)
    dh_dt = matmul_nt(ddt_pad, w["w_in_dt"], "d_mix_h_dt", F32, tm=1024, tn=1024, tk=DT_PAD)
    dx, g["mix_norm"] = rms_bwd([dh_main, dh_dt], x, w["mix_norm"], dx_out, "mix_drms")
    return dx, g


def _place():
    x, y, c = lax.axis_index("x"), lax.axis_index("y"), lax.axis_index("c")
    chips = [(1 - x, y), (x, 1 - y), (1 - x, 1 - y)]
    return x, y, c, 2 * x + y, chips


def _cores():
    c = lax.axis_index("c")
    return jnp.stack([c, 1 - c]).astype(jnp.int32)


def _staged_call(body, *, name, grid, in_specs, out_specs, out_shape, scratch_shapes):
    return pl.pallas_call(
        body, out_shape=out_shape, name=name,
        grid_spec=pltpu.PrefetchScalarGridSpec(num_scalar_prefetch=1, grid=grid, in_specs=in_specs,
                                               out_specs=out_specs, scratch_shapes=scratch_shapes),
        compiler_params=pltpu.CompilerParams(dimension_semantics=("arbitrary",) * len(grid),
                                             vmem_limit_bytes=V7X_VMEM_LIMIT, has_side_effects=True))


def gather_rider(shards, tiles):
    dma = pltpu.SemaphoreType.DMA
    n = len(shards)
    geo = [(a.shape[0] // 2, tm, (a.shape[0] // 2) // tm) for a, tm in zip(shards, tiles)]
    scratch = []
    for a, (h, tm, nk) in zip(shards, geo):
        scratch += [pltpu.VMEM((N_CHIP,) + a.shape, a.dtype), dma((3, nk)), dma((3, nk)), dma((3, nk)), dma((3, nk)),
                    dma((nk + 2,))]

    def copies(j, in_ref, scr):
        buf, send1, recv1, send2, recv2, local = scr[6 * j:6 * j + 6]
        h, tm, nk = geo[j]
        x, y, c, me, chips = _place()
        chip_of = [2 * chips[t][0] + chips[t][1] for t in range(3)]

        def rows(chip, core, k):
            return buf.at[chip, pl.ds(core * h + k * tm, tm)]

        def mine(k):
            if k == nk:
                return pltpu.make_async_copy(in_ref.at[pl.ds((1 - c) * h, h)], buf.at[me, pl.ds((1 - c) * h, h)],
                                             local.at[nk])
            return pltpu.make_async_copy(in_ref.at[pl.ds(c * h + k * tm, tm)], rows(me, c, k), local.at[k])

        def level1(t, k, incoming):
            place = rows(chip_of[t] if incoming else me, c, k)
            return pltpu.make_async_remote_copy(src_ref=place, dst_ref=place, send_sem=send1.at[t, k],
                                                recv_sem=recv1.at[t, k], device_id=(*chips[t], c), device_id_type=MESH)

        def level2(t, k, incoming):
            place = rows(chip_of[t], (1 - c) if incoming else c, k)
            return pltpu.make_async_remote_copy(src_ref=place, dst_ref=place, send_sem=send2.at[t, k],
                                                recv_sem=recv2.at[t, k], device_id=(x, y, 1 - c),
                                                device_id_type=MESH)

        return buf, local, nk, mine, level1, level2

    def start(ins, outs, scr):
        for j in range(n):
            _, _, nk, mine, _, _ = copies(j, ins[j], scr)
            for k in range(nk + 1):
                mine(k).start()
        for j in range(n):
            _, _, nk, mine, level1, _ = copies(j, ins[j], scr)
            for k in range(nk):
                mine(k).wait()
                for t in range(3):
                    level1(t, k, False).start()

    def finish(ins, outs, scr):
        for j in range(n):
            _, _, nk, _, level1, level2 = copies(j, ins[j], scr)
            for k in range(nk):
                for t in range(3):
                    level1(t, k, True).wait_recv()
                    level2(t, k, False).start()
        for j in range(n):
            buf, local, nk, mine, level1, level2 = copies(j, ins[j], scr)
            for k in range(nk):
                for t in range(3):
                    level2(t, k, True).wait_recv()
            for k in range(nk):
                for t in range(3):
                    level1(t, k, False).wait_send()
                    level2(t, k, False).wait_send()
            mine(nk).wait()
            pltpu.make_async_copy(buf, outs[j], local.at[nk + 1]).start()
        for j in range(n):
            buf, local, nk, _, _, _ = copies(j, ins[j], scr)
            pltpu.make_async_copy(buf, outs[j], local.at[nk + 1]).wait()

    return Rider(list(shards), [_sds((N_CHIP,) + a.shape, a.dtype) for a in shards], scratch, start, finish)


def sibling_sum(g, tm, name):
    _, r, cdim = g.shape
    h = r // 2
    ni = h // tm
    dma = pltpu.SemaphoreType.DMA

    def body(cores_ref, keep_ref, give_ref, out_ref, slot, send, recv):
        par = (pl.program_id(0) * ni + pl.program_id(1)) % 2
        x, y, c, _, _ = _place()
        cp = pltpu.make_async_remote_copy(src_ref=give_ref, dst_ref=slot.at[par], send_sem=send.at[par],
                                          recv_sem=recv.at[par], device_id=(x, y, 1 - c), device_id_type=MESH)
        cp.start()
        cp.wait_recv()
        out_ref[...] = (keep_ref[...].astype(F32) + slot[par].astype(F32)).astype(out_ref.dtype)
        cp.wait_send()

    flat = g.reshape(N_CHIP * r, cdim)
    return _staged_call(
        body, name=name, grid=(N_CHIP, ni),
        in_specs=[pl.BlockSpec((tm, cdim), lambda j, i, cores: ((2 * j + cores[0]) * ni + i, 0)),
                  pl.BlockSpec((tm, cdim), lambda j, i, cores: ((2 * j + cores[1]) * ni + i, 0))],
        out_specs=pl.BlockSpec((None, tm, cdim), lambda j, i, cores: (j, i, 0)),
        out_shape=_sds((N_CHIP, h, cdim), g.dtype),
        scratch_shapes=[pltpu.VMEM((2, tm, cdim), g.dtype), dma((2,)), dma((2,))],
    )(_cores(), flat, flat)


def owner_sum_rider(sums, tiles):
    dma = pltpu.SemaphoreType.DMA
    n = len(sums)
    geo = [(a.shape[1], tm, a.shape[1] // tm) for a, tm in zip(sums, tiles)]
    scratch = []
    for a, (h, tm, nk) in zip(sums, geo):
        cdim = a.shape[2]
        scratch += [pltpu.VMEM(a.shape, a.dtype), pltpu.VMEM((3, h, cdim), a.dtype), pltpu.VMEM((2, h, cdim), F32),
                    dma((3, nk)), dma((3, nk)), dma((nk,)), dma((nk,)), dma((2,))]

    def copies(j, scr):
        part, got, res, send, recv, send2, recv2, local = scr[8 * j:8 * j + 8]
        h, tm, nk = geo[j]
        x, y, c, me, chips = _place()

        def to_owner(t, k):
            chip = 2 * chips[t][0] + chips[t][1]
            return pltpu.make_async_remote_copy(
                src_ref=part.at[chip, pl.ds(k * tm, tm)], dst_ref=got.at[t, pl.ds(k * tm, tm)],
                send_sem=send.at[t, k], recv_sem=recv.at[t, k], device_id=(*chips[t], c), device_id_type=MESH)

        def to_sibling(k):
            place = res.at[c, pl.ds(k * tm, tm)]
            return pltpu.make_async_remote_copy(src_ref=place, dst_ref=place, send_sem=send2.at[k],
                                                recv_sem=recv2.at[k], device_id=(x, y, 1 - c), device_id_type=MESH)

        return part, got, res, local, to_owner, to_sibling, (tm, nk, c, me)

    def start(ins, outs, scr):
        for j in range(n):
            part, _, _, local, _, _, _ = copies(j, scr)
            pltpu.make_async_copy(ins[j], part, local.at[0]).start()
        for j in range(n):
            part, _, _, local, to_owner, _, (tm, nk, c, me) = copies(j, scr)
            pltpu.make_async_copy(ins[j], part, local.at[0]).wait()
            for k in range(nk):
                for t in range(3):
                    to_owner(t, k).start()

    def finish(ins, outs, scr):
        for j in range(n):
            part, got, res, _, to_owner, to_sibling, (tm, nk, c, me) = copies(j, scr)
            for k in range(nk):
                rows = pl.ds(k * tm, tm)
                for t in range(3):
                    to_owner(t, k).wait_recv()
                acc = part[me, rows, :].astype(F32)
                for t in range(3):
                    acc = acc + got[t, rows, :].astype(F32)
                res[c, rows, :] = acc
                to_sibling(k).start()
        for j in range(n):
            _, _, res, local, to_owner, to_sibling, (tm, nk, c, me) = copies(j, scr)
            for k in range(nk):
                to_sibling(k).wait_recv()
            for k in range(nk):
                to_sibling(k).wait_send()
                for t in range(3):
                    to_owner(t, k).wait_send()
            pltpu.make_async_copy(res, outs[j], local.at[1]).start()
        for j in range(n):
            _, _, res, local, _, _, _ = copies(j, scr)
            pltpu.make_async_copy(res, outs[j], local.at[1]).wait()

    return Rider(list(sums), [_sds((2, a.shape[1], a.shape[2]), F32) for a in sums], scratch, start, finish)


def gather_conv_w(w):
    def body(in_ref, out_ref, send, recv):
        x, y, c, me, chips = _place()
        out_ref[me] = in_ref[...]
        copies = []
        for t in range(3):
            copies.append(pltpu.make_async_remote_copy(
                src_ref=out_ref.at[me], dst_ref=out_ref.at[me], send_sem=send.at[t], recv_sem=recv.at[t],
                device_id=(*chips[t], c), device_id_type=MESH))
        for cp in copies:
            cp.start()
        for cp in copies:
            cp.wait_recv()
        for cp in copies:
            cp.wait_send()

    dma = pltpu.SemaphoreType.DMA
    vmem = pl.BlockSpec(memory_space=pltpu.VMEM)
    return pl.pallas_call(
        body, out_shape=_sds((N_CHIP,) + w.shape, w.dtype), in_specs=[vmem], out_specs=vmem, name="gather_conv_w",
        scratch_shapes=[dma((3,)), dma((3,))],
        compiler_params=pltpu.CompilerParams(has_side_effects=True))(w)


N_DEV = 8
SMALL_ROWS = 32
SMALL_LANES = 1024


def all_reduce_small(arrays):
    n_arr = len(arrays)
    places = []
    for k, a in enumerate(arrays):
        for ri in range(a.shape[0]):
            for c0 in range(0, a.shape[1], SMALL_LANES):
                places.append((k, ri, c0, min(SMALL_LANES, a.shape[1] - c0), len(places)))
    assert len(places) <= SMALL_ROWS

    def body(*refs):
        ins, outs = refs[:n_arr], refs[n_arr:2 * n_arr]
        buf, send, recv = refs[2 * n_arr:]
        x, y, c, _, _ = _place()
        me = 4 * x + 2 * y + c
        buf[me] = jnp.zeros((SMALL_ROWS, SMALL_LANES), F32)
        for k, ri, c0, width, row in places:
            buf[me, row:row + 1, 0:width] = ins[k][ri:ri + 1, c0:c0 + width]
        copies = []
        for r in range(1, N_DEV):
            px = (1 - x) if r & 4 else x
            py = (1 - y) if r & 2 else y
            pc = (1 - c) if r & 1 else c
            copies.append(pltpu.make_async_remote_copy(
                src_ref=buf.at[me], dst_ref=buf.at[me], send_sem=send.at[r - 1], recv_sem=recv.at[r - 1],
                device_id=(px, py, pc), device_id_type=MESH))
        for cp in copies:
            cp.start()
        for cp in copies:
            cp.wait_recv()
        for cp in copies:
            cp.wait_send()
        acc = buf[0]
        for j in range(1, N_DEV):
            acc = acc + buf[j]
        for k, ri, c0, width, row in places:
            outs[k][ri:ri + 1, c0:c0 + width] = acc[row:row + 1, 0:width]

    dma = pltpu.SemaphoreType.DMA
    vmem = pl.BlockSpec(memory_space=pltpu.VMEM)
    return pl.pallas_call(
        body, out_shape=[_sds(a.shape, F32) for a in arrays], in_specs=[vmem] * n_arr, out_specs=[vmem] * n_arr,
        name="all_reduce_small",
        scratch_shapes=[pltpu.VMEM((N_DEV, SMALL_ROWS, SMALL_LANES), F32), dma((N_DEV - 1,)), dma((N_DEV - 1,))],
        compiler_params=pltpu.CompilerParams(has_side_effects=True))(*arrays)


def _row_tile(rows, limit, multiple):
    return max(t for t in range(multiple, min(rows, limit) + 1, multiple) if rows % t == 0)


def _adamw_math(w, g, m, v):
    c1 = 1.0 - ADAM_B1 ** ADAM_STEP
    c2 = 1.0 - ADAM_B2 ** ADAM_STEP
    m2 = ADAM_B1 * m + (1.0 - ADAM_B1) * g
    v2 = ADAM_B2 * v + (1.0 - ADAM_B2) * (g * g)
    return -ADAM_LR * ((m2 / c1) / (jnp.sqrt(v2 / c2) + ADAM_EPS) + ADAM_WD * w), m2, v2


def adamw(w, g, row_off, m, v, name):
    _, r, c = w.shape
    tm = r if r < 8 else _row_tile(math.gcd(r, row_off) if row_off else r, 128, 8)

    def body(w_ref, g_ref, m_ref, v_ref, go_ref, d_ref, m2_ref, v2_ref):
        gv = g_ref[...]
        go_ref[...] = gv
        d_ref[...], m2_ref[...], v2_ref[...] = _adamw_math(w_ref[...], gv, m_ref[...], v_ref[...])

    blk = pl.BlockSpec((None, tm, c), lambda i: (0, i, 0))
    shp = _sds((1, r, c), F32)
    return _call(body, name=name, out_shape=[shp] * 4, grid=(r // tm,),
                 in_specs=[blk, pl.BlockSpec((tm, c), lambda i: (row_off // tm + i, 0)), blk, blk],
                 out_specs=[blk] * 4)(w, g, m, v)


def adamw_small(ws, gs, ms, vs):
    n = len(ws)

    def body(*refs):
        ins, outs = refs[:4 * n], refs[4 * n:]
        for k in range(n):
            w_ref, g_ref, m_ref, v_ref = (ins[j * n + k] for j in range(4))
            outs[k][...], outs[n + k][...], outs[2 * n + k][...] = _adamw_math(w_ref[...], g_ref[...], m_ref[...],
                                                                               v_ref[...])

    vmem = pl.BlockSpec(memory_space=pltpu.VMEM)
    shapes = [_sds(w.shape, F32) for w in ws] * 3
    res = pl.pallas_call(body, out_shape=shapes, in_specs=[vmem] * (4 * n), out_specs=[vmem] * (3 * n),
                         name="adamw_small")(*ws, *gs, *ms, *vs)
    return res[:n], res[n:2 * n], res[2 * n:]


SMALL = ("ffn1_norm", "mix_norm", "q_norm", "k_norm", "conv_b", "dt_bias", "a_log", "d_skip", "ssd_norm", "ffn2_norm")
WEIGHTS = ("ffn1_norm", "ffn1_w_gate", "ffn1_w_up", "ffn1_w_down", "mix_norm", "w_in", "q_norm", "k_norm", "conv_w",
           "conv_b", "dt_bias", "a_log", "d_skip", "ssd_norm", "w_attn_branch", "w_ssd_branch", "w_out", "ffn2_norm",
           "ffn2_w_gate", "ffn2_w_up", "ffn2_w_down")
CONV_SHARD = SSD_CONV_DIM // N_CHIP
CLASSES = {
    "ffn1_in": (("ffn1_w_gate", 1024), ("ffn1_w_up", 1024)),
    "ffn1_out": (("ffn1_w_down", 704),),
    "mix_in": (("w_in", 1024),),
    "mix_attn": (("w_attn_branch", 512),),
    "late_out": (("ffn2_w_down", 704), ("w_ssd_branch", 512), ("w_out", 256)),
    "ffn2_in": (("ffn2_w_gate", 1024), ("ffn2_w_up", 1024)),
}
CLASS_TILE = {"ffn1_in": 256, "ffn1_out": 176, "mix_attn": 256, "late_out": 368, "ffn2_in": 256,
              "mix_in_top": 128, "mix_in_bottom": 128}
SIBLING_TILE = {"ffn1_in": 1024, "ffn1_out": 352, "mix_attn": 256, "late_out": 736, "ffn2_in": 1024,
                "mix_in_top": 256, "mix_in_bottom": 256}


def _chip_major_cols(a):
    r = a.shape[0]
    return a.reshape(r, N_CHIP, -1).transpose(1, 0, 2)


def _from_chip_major_cols(a):
    return a.transpose(1, 0, 2).reshape(a.shape[1], -1)


def kernel(x, ffn1_norm, ffn1_w_gate, ffn1_w_up, ffn1_w_down, mix_norm, w_in, q_norm, k_norm, conv_w, conv_b, dt_bias, a_log, d_skip, ssd_norm, w_attn_branch, w_ssd_branch, w_out, ffn2_norm, ffn2_w_gate, ffn2_w_up, ffn2_w_down, loss_target, m_ffn1_norm, m_ffn1_w_gate, m_ffn1_w_up, m_ffn1_w_down, m_mix_norm, m_w_in, m_q_norm, m_k_norm, m_conv_w, m_conv_b, m_dt_bias, m_a_log, m_d_skip, m_ssd_norm, m_w_attn_branch, m_w_ssd_branch, m_w_out, m_ffn2_norm, m_ffn2_w_gate, m_ffn2_w_up, m_ffn2_w_down, v_ffn1_norm, v_ffn1_w_gate, v_ffn1_w_up, v_ffn1_w_down, v_mix_norm, v_w_in, v_q_norm, v_k_norm, v_conv_w, v_conv_b, v_dt_bias, v_a_log, v_d_skip, v_ssd_norm, v_w_attn_branch, v_w_ssd_branch, v_w_out, v_ffn2_norm, v_ffn2_w_gate, v_ffn2_w_up, v_ffn2_w_down):
    env = dict(locals())
    wts = {k: env[k] for k in WEIGHTS}
    moms = {k: env["m_" + k] for k in WEIGHTS}
    vars_ = {k: env["v_" + k] for k in WEIGHTS}
    x0 = x[0]
    target = loss_target[0]

    def gather(classes, more=()):
        shards = [jnp.concatenate([wts[k][0] for k, _ in CLASSES[c]], axis=0).astype(BF16) for c in classes]
        return gather_rider(shards + [a for a, _ in more], [CLASS_TILE[c] for c in classes] + [t for _, t in more])

    def reducer(classes, parts):
        sums = [sibling_sum(p, SIBLING_TILE[c], f"sibling_sum_{c}") for c, p in zip(classes, parts)]
        return owner_sum_rider(sums, [CLASS_TILE[c] for c in classes])

    half = D_MODEL // 2
    in_tile = CLASS_TILE["mix_in_top"]
    x1, saved1, (w_ffn1_in,), (w_ffn1_out, w_mix_attn, w_in_top), (w_in_bottom,) = ffn_forward(
        x0, ffn1_norm, lambda rode: rode[0], lambda rode: rode[0], "ffn1", rms_rider=gather(["ffn1_in"]),
        up_rider=gather(["ffn1_out", "mix_attn"], [(w_in[0, :half].astype(BF16), in_tile)]),
        down_rider=gather([], [(w_in[0, half:].astype(BF16), in_tile)]))
    dt0, dt1 = IN_DT0 - 3 * IN_SHARD, IN_DT1 - 3 * IN_SHARD

    def in_columns(w4):
        return jnp.concatenate([w4[0], w4[1], w4[2], w4[3][:, :dt0], w4[3][:, dt1:]], axis=1), w4[3][:, dt0:dt1]

    (main_top, dt_top), (main_bottom, dt_bottom) = in_columns(w_in_top), in_columns(w_in_bottom)
    mixer_w = dict(
        mix_norm=mix_norm,
        w_in_main=jnp.concatenate([main_top, main_bottom], axis=0),
        w_in_dt=jnp.pad(jnp.concatenate([dt_top, dt_bottom], axis=0), ((0, 0), (0, DT_PAD - SSD_HEADS))),
        q_gain=jnp.tile(q_norm, (1, 2)), k_gain=jnp.tile(k_norm, (1, 2)),
        conv_w=_from_chip_major_cols(gather_conv_w(conv_w[0])), conv_b=conv_b, dt_bias=dt_bias, a_log=a_log,
        d_skip=d_skip, ssd_norm=ssd_norm, w_attn_branch=_from_chip_major_cols(w_mix_attn))

    def later_weights(rode):
        late = rode[0]
        return dict(w_ssd_branch=late[:, 704:1216].reshape(SSD_INNER, D_MODEL),
                    w_out=late[:, 1216:1472].reshape(D_MODEL, D_MODEL))

    x2, saved_mix, (w_late_out, w_ffn2_in) = mixer_forward(x1, mixer_w, gather(["late_out", "ffn2_in"]), later_weights)
    (dx3, sq), saved2, _, _, _ = ffn_forward(x2, ffn2_norm, lambda rode: w_ffn2_in, lambda rode: w_late_out, "ffn2",
                                             target=target)

    grads = {}
    dx2, grads["ffn2_norm"], d_ffn2_in, d_ffn2_down, _ = ffn_backward(dx3, x2, ffn2_norm, w_ffn2_in, w_late_out,
                                                                      saved2, "ffn2")

    def ride_early(g):
        late = jnp.concatenate([d_ffn2_down, g["w_ssd_branch"].reshape(N_CHIP, -1, D_MODEL),
                                g["w_out"].reshape(N_CHIP, -1, D_MODEL)], axis=1)
        return reducer(["ffn2_in", "late_out"], [d_ffn2_in, late])

    g_in_rows = {}

    def ride_late(g):
        main = g["w_in_main"]
        last = jnp.concatenate([main[:, 3 * IN_SHARD:IN_DT0], g["w_in_dt"][:, :SSD_HEADS], main[:, IN_DT0:]], axis=1)
        for part, rows in (("top", slice(0, D_MODEL // 2)), ("bottom", slice(D_MODEL // 2, D_MODEL))):
            g_in_rows[part] = jnp.stack([main[rows, j * IN_SHARD:(j + 1) * IN_SHARD] for j in range(3)]
                                        + [last[rows]])
        return reducer(["mix_in_top", "mix_attn"], [g_in_rows["top"], _chip_major_cols(g["w_attn_branch"])])

    dx1, gmix = mixer_backward(dx2, x1, saved_mix, ride_early, ride_late)
    dx0, grads["ffn1_norm"], rode_in, rode_out, rode_hidden = ffn_backward(
        dx1, x0, ffn1_norm, w_ffn1_in, w_ffn1_out, saved1, "ffn1",
        hidden_rider=reducer(["mix_in_bottom"], [g_in_rows["bottom"]]),
        ride_down=lambda d: reducer(["ffn1_out"], [d]), ride_in=lambda d: reducer(["ffn1_in"], [d]))
    for k in ("mix_norm", "q_norm", "k_norm", "conv_b", "dt_bias", "a_log", "d_skip", "ssd_norm"):
        grads[k] = gmix[k]
    reduced = dict(zip(("ffn2_in", "late_out", "mix_in_top", "mix_attn", "ffn1_in", "ffn1_out", "mix_in_bottom"),
                       (*gmix["rode_early"], *gmix["rode_late"], rode_in[0], rode_out[0], rode_hidden[0])))
    reduced = {c: r.reshape(-1, r.shape[2]) for c, r in reduced.items()}
    reduced["mix_in"] = jnp.concatenate([reduced.pop("mix_in_top"), reduced.pop("mix_in_bottom")], axis=0)
    summed = all_reduce_small([grads[k] for k in SMALL]
                              + [gmix["conv_w"], (0.5 * jnp.sum(sq) / D_MODEL).reshape(1, 1)])
    g_small = dict(zip(SMALL, summed))
    loss = summed[-1].reshape(())
    chip = 2 * lax.axis_index("x") + lax.axis_index("y")
    g_conv = lax.dynamic_slice_in_dim(summed[-2], chip * CONV_SHARD, CONV_SHARD, axis=1)

    g_final, delta, new_m, new_v = dict(g_small), {}, {}, {}

    def update(k, g_arr, row_off):
        w, m, v = wts[k], moms[k], vars_[k]
        rows, cols = w.shape[1:]
        if cols % 128:
            res = adamw(jnp.swapaxes(w, 1, 2), g_arr[row_off:row_off + rows].T, 0, jnp.swapaxes(m, 1, 2),
                        jnp.swapaxes(v, 1, 2), f"adamw_{k}")
            res = [jnp.swapaxes(r, 1, 2) for r in res]
        else:
            res = adamw(w, g_arr, row_off, m, v, f"adamw_{k}")
        g_final[k], delta[k], new_m[k], new_v[k] = res

    for cls, members in CLASSES.items():
        off = 0
        for k, rows in members:
            update(k, reduced[cls], off)
            off += rows
    update("conv_w", g_conv, 0)
    small = adamw_small(*([d[k] for k in SMALL] for d in (wts, g_small, moms, vars_)))
    for res, vals in zip((delta, new_m, new_v), small):
        res.update(zip(SMALL, vals))

    return (loss, dx0[None], *[g_final[k] for k in WEIGHTS], *[delta[k] for k in WEIGHTS],
            *[new_m[k] for k in WEIGHTS], *[new_v[k] for k in WEIGHTS])
```

```python
import collections
import functools
import math

import jax
import jax.numpy as jnp
from jax import lax
from jax.experimental import pallas as pl
from jax.experimental.pallas import tpu as pltpu

F32 = jnp.float32
BF16 = jnp.bfloat16
MESH = pl.DeviceIdType.MESH

EPS = 1e-6
D_MODEL = 1024
D_FF = 2816
N_CHIP = 4
FF_SHARD = D_FF // N_CHIP
HD = 64
BLK = 128
ATTN_DILATIONS = (1, 4, 16)
HEADS_PER_PATTERN = 8
N_ATTN_HEADS = 24
ALIBI_MAX_EXP = 8.0
ATTN_QKV = 1536
GROUP_W = 512
SSD_INNER = 2048
SSD_HEADS = 32
SSD_GROUPS = 4
SSD_CONV = 4
SSD_CONV_DIM = 3072
IN_COLS = 11808
IN_DT0, IN_DT1 = 9728, 9760
IN_SHARD = IN_COLS // 4
COL_K, COL_V, COL_Z, COL_XBC, COL_GA, COL_GS, P_COLS = 1536, 3072, 4608, 6656, 9728, 10752, 11776
DT_PAD = 128

ADAM_LR, ADAM_B1, ADAM_B2, ADAM_EPS, ADAM_WD, ADAM_STEP = 0.001, 0.9, 0.999, 1e-08, 0.01, 10

V7X_VMEM_LIMIT = 56 * 1024 * 1024
NEG = -1e30


Rider = collections.namedtuple("Rider", "arrays out_shape scratch start finish")
Rider.__doc__ = """An exchange between devices that rides in a compute kernel: its copies are started in the host's
first grid step and waited for in its last, so they travel while the host computes.  arrays / out_shape: extra HBM
operands and results; scratch: extra scratch; start, finish: f(in_refs, out_refs, scratch_refs)."""


def _call(body, *, name, out_shape, in_specs, out_specs, grid=(), scratch_shapes=(), aliases=None, rider=None):
    params = dict(dimension_semantics=("arbitrary",) * len(grid), vmem_limit_bytes=V7X_VMEM_LIMIT)
    if rider is None:
        return pl.pallas_call(
            body, out_shape=out_shape, grid=grid, in_specs=in_specs, out_specs=out_specs,
            scratch_shapes=scratch_shapes, input_output_aliases=aliases or {}, name=name,
            compiler_params=pltpu.CompilerParams(**params))
    single = not isinstance(out_shape, (list, tuple))
    main_out = [out_shape] if single else list(out_shape)
    main_specs = [out_specs] if single else list(out_specs)
    n_in, n_out, n_scr = len(in_specs), len(main_out), len(scratch_shapes)
    r_in, r_out = len(rider.arrays), len(rider.out_shape)

    def wrapped(*refs):
        ins, refs = refs[:n_in], refs[n_in:]
        r_ins, refs = refs[:r_in], refs[r_in:]
        outs, refs = refs[:n_out], refs[n_out:]
        r_outs, refs = refs[:r_out], refs[r_out:]
        scr, r_scr = refs[:n_scr], refs[n_scr:]
        first = last = None
        for axis, size in enumerate(grid):
            at_start, at_end = pl.program_id(axis) == 0, pl.program_id(axis) == size - 1
            first = at_start if first is None else jnp.logical_and(first, at_start)
            last = at_end if last is None else jnp.logical_and(last, at_end)

        @pl.when(first)
        def _():
            rider.start(r_ins, r_outs, r_scr)

        body(*ins, *outs, *scr)

        @pl.when(last)
        def _():
            rider.finish(r_ins, r_outs, r_scr)

    hbm = pl.BlockSpec(memory_space=pl.ANY)
    call = pl.pallas_call(
        wrapped, out_shape=main_out + list(rider.out_shape), grid=grid, in_specs=list(in_specs) + [hbm] * r_in,
        out_specs=main_specs + [hbm] * r_out, scratch_shapes=list(scratch_shapes) + list(rider.scratch), name=name,
        compiler_params=pltpu.CompilerParams(has_side_effects=True, **params))

    def run(*args):
        res = call(*args, *rider.arrays)
        main = res[:n_out]
        return (main[0] if single else main), res[n_out:]

    return run


def _sds(shape, dtype):
    return jax.ShapeDtypeStruct(tuple(shape), dtype)


def _dot(a, b):
    return jnp.dot(a, b, preferred_element_type=F32)


def _dot_nt(a, b):
    return lax.dot_general(a, b, (((1,), (1,)), ((), ())), preferred_element_type=F32)


def _dot_tn(a, b):
    return lax.dot_general(a, b, (((0,), (0,)), ((), ())), preferred_element_type=F32)


def _dot_hi(a, b):
    return jnp.dot(a, b, preferred_element_type=F32, precision=lax.Precision.HIGHEST)


def _sigmoid(x):
    return 1.0 / (1.0 + jnp.exp(-x))


def _lane_first_half(shape):
    return lax.broadcasted_iota(jnp.int32, shape, len(shape) - 1) < HD


def _rowwise(name, fn, rows, consts, outs, accs=(), tm=512, rider=None):
    n_rows = None
    in_arrays, in_specs = [], []
    for r in rows:
        if isinstance(r, tuple):
            arr, w, cb = r
            spec = pl.BlockSpec((tm, w), functools.partial(lambda i, cb: (i, cb), cb=cb))
        else:
            arr = r
            spec = pl.BlockSpec((tm, arr.shape[1]), lambda i: (i, 0))
        n_rows = arr.shape[0]
        in_arrays.append(arr)
        in_specs.append(spec)
    for c in consts:
        in_arrays.append(c)
        in_specs.append(pl.BlockSpec(c.shape, functools.partial(lambda i, n: (0,) * n, n=c.ndim)))
    out_shape = [_sds(s, d) for s, d in outs] + [_sds(s, d) for s, d in accs]
    out_specs = [pl.BlockSpec((tm, s[1]), lambda i: (i, 0)) for s, _ in outs]
    out_specs += [pl.BlockSpec(s, functools.partial(lambda i, n: (0,) * n, n=len(s))) for s, _ in accs]

    def body(*refs):
        fn(pl.program_id(0), *refs)

    res = _call(body, name=name, out_shape=out_shape, grid=(n_rows // tm,), in_specs=in_specs,
                out_specs=out_specs, rider=rider)(*in_arrays)
    return res


def rms_fwd(x, gain, name, rider=None):
    def fn(i, x_ref, g_ref, h_ref):
        xv = x_ref[...]
        r = lax.rsqrt(jnp.mean(xv * xv, axis=-1, keepdims=True) + EPS)
        h_ref[...] = (xv * r * g_ref[...]).astype(h_ref.dtype)

    res = _rowwise(name, fn, [x], [gain], [(x.shape, BF16)], rider=rider)
    return res[0] if rider is None else (res[0][0], res[1])


def rms_bwd(dhs, x, gain, dx_in, name):
    n = len(dhs)

    def fn(i, *refs):
        dh_refs, (x_ref, dxin_ref, g_ref, dx_ref, dg_ref) = refs[:n], refs[n:]
        dh = dh_refs[0][...]
        for r in dh_refs[1:]:
            dh = dh + r[...]
        xv = x_ref[...]
        r = lax.rsqrt(jnp.mean(xv * xv, axis=-1, keepdims=True) + EPS)
        xn = xv * r
        dxn = dh * g_ref[...]
        dx_ref[...] = dxin_ref[...] + r * (dxn - xn * jnp.mean(dxn * xn, axis=-1, keepdims=True))

        @pl.when(i == 0)
        def _():
            dg_ref[...] = jnp.zeros_like(dg_ref)

        dg_ref[...] += jnp.sum(dh * xn, axis=0, keepdims=True)

    return _rowwise(name, fn, list(dhs) + [x, dx_in], [gain], [(x.shape, F32)], [((1, x.shape[1]), F32)])


def matmul_nn(a, b, name, out_dtype, tm, tn, res=None, scale=1.0, rider=None):
    s, k = a.shape
    n = b.shape[1]

    def body(*refs):
        if res is None:
            a_ref, b_ref, o_ref = refs
            o_ref[...] = _dot(a_ref[...], b_ref[...]).astype(o_ref.dtype)
        else:
            a_ref, b_ref, r_ref, o_ref = refs
            o_ref[...] = (r_ref[...] + scale * _dot(a_ref[...], b_ref[...])).astype(o_ref.dtype)

    in_specs = [pl.BlockSpec((tm, k), lambda i, j: (i, 0)), pl.BlockSpec((k, tn), lambda i, j: (0, j))]
    args = [a, b]
    if res is not None:
        in_specs.append(pl.BlockSpec((tm, tn), lambda i, j: (i, j)))
        args.append(res)
    return _call(body, name=name, out_shape=_sds((s, n), out_dtype), grid=(s // tm, n // tn), in_specs=in_specs,
                 out_specs=pl.BlockSpec((tm, tn), lambda i, j: (i, j)), rider=rider)(*args)


def matmul_nt(a, b, name, out_dtype, tm, tn, tk, rider=None):
    s, k = a.shape
    n = b.shape[0]
    nk = k // tk

    def body(a_ref, b_ref, o_ref, acc_ref):
        kk = pl.program_id(2)

        @pl.when(kk == 0)
        def _():
            acc_ref[...] = jnp.zeros_like(acc_ref)

        acc_ref[...] += _dot_nt(a_ref[...].astype(BF16), b_ref[...])

        @pl.when(kk == nk - 1)
        def _():
            o_ref[...] = acc_ref[...].astype(o_ref.dtype)

    return _call(body, name=name, out_shape=_sds((s, n), out_dtype), grid=(s // tm, n // tn, nk),
                 in_specs=[pl.BlockSpec((tm, tk), lambda i, j, kk: (i, kk)),
                           pl.BlockSpec((tn, tk), lambda i, j, kk: (j, kk))],
                 out_specs=pl.BlockSpec((tm, tn), lambda i, j, kk: (i, j)),
                 scratch_shapes=[pltpu.VMEM((tm, tn), F32)], rider=rider)(a, b)


def matmul_tn(a, b, name, tn, ts, a_scale=None, b_scale=None, rider=None):
    s, m = a.shape
    n = b.shape[1]
    ns = s // ts

    def body(a_ref, b_ref, o_ref, acc_ref):
        ss = pl.program_id(1)

        @pl.when(ss == 0)
        def _():
            acc_ref[...] = jnp.zeros_like(acc_ref)

        av, bv = a_ref[...], b_ref[...]
        if a_scale is not None:
            av = av * a_scale
        if b_scale is not None:
            bv = bv * b_scale
        acc_ref[...] += _dot_tn(av.astype(BF16), bv.astype(BF16))

        @pl.when(ss == ns - 1)
        def _():
            o_ref[...] = acc_ref[...].astype(o_ref.dtype)

    return _call(body, name=name, out_shape=_sds((m, n), BF16), grid=(n // tn, ns),
                 in_specs=[pl.BlockSpec((ts, m), lambda j, ss: (ss, 0)), pl.BlockSpec((ts, tn), lambda j, ss: (ss, j))],
                 out_specs=pl.BlockSpec((m, tn), lambda j, ss: (0, j)),
                 scratch_shapes=[pltpu.VMEM((m, tn), F32)], rider=rider)(a, b)


def _piece_specs(pieces, tile, rows_tile, tile_axis_first):
    specs, ranges, t0 = [], [], 0
    for a in pieces:
        n = a.shape[1] // tile

        def index(*ids, t0=t0, n=n):
            t, r = (ids[0], ids[1]) if tile_axis_first else (ids[2], ids[0])
            on = jnp.logical_and(t >= t0, t < t0 + n)
            return jnp.where(on, r, 0), jnp.clip(t - t0, 0, n - 1)

        specs.append(pl.BlockSpec((rows_tile, tile), index))
        ranges.append((t0, n))
        t0 += n
    return specs, ranges


def matmul_tn_pieces(a, pieces, name, tn, ts, rider=None):
    s, m = a.shape
    ns = s // ts
    specs, ranges = _piece_specs(pieces, tn, ts, True)
    n_total = sum(n for _, n in ranges)

    def body(a_ref, *refs):
        b_refs, o_ref, acc_ref = refs[:len(pieces)], refs[-2], refs[-1]
        j, ss = pl.program_id(0), pl.program_id(1)

        @pl.when(ss == 0)
        def _():
            acc_ref[...] = jnp.zeros_like(acc_ref)

        for b_ref, (t0, n) in zip(b_refs, ranges):
            @pl.when(jnp.logical_and(j >= t0, j < t0 + n))
            def _(b_ref=b_ref):
                acc_ref[...] += _dot_tn(a_ref[...], b_ref[...])

        @pl.when(ss == ns - 1)
        def _():
            o_ref[...] = acc_ref[...].astype(o_ref.dtype)

    return _call(body, name=name, out_shape=_sds((m, n_total * tn), BF16), grid=(n_total, ns),
                 in_specs=[pl.BlockSpec((ts, m), lambda j, ss: (ss, 0))] + specs,
                 out_specs=pl.BlockSpec((m, tn), lambda j, ss: (0, j)),
                 scratch_shapes=[pltpu.VMEM((m, tn), F32)], rider=rider)(a, *pieces)


def matmul_nt_pieces(pieces, b, name, out_dtype, tm, tn, tk, rider=None):
    s = pieces[0].shape[0]
    n = b.shape[0]
    specs, ranges = _piece_specs(pieces, tk, tm, False)
    nk = sum(cnt for _, cnt in ranges)

    def body(*refs):
        a_refs, b_ref, o_ref, acc_ref = refs[:len(pieces)], refs[-3], refs[-2], refs[-1]
        kk = pl.program_id(2)

        @pl.when(kk == 0)
        def _():
            acc_ref[...] = jnp.zeros_like(acc_ref)

        for a_ref, (t0, cnt) in zip(a_refs, ranges):
            @pl.when(jnp.logical_and(kk >= t0, kk < t0 + cnt))
            def _(a_ref=a_ref):
                acc_ref[...] += _dot_nt(a_ref[...], b_ref[...])

        @pl.when(kk == nk - 1)
        def _():
            o_ref[...] = acc_ref[...].astype(o_ref.dtype)

    return _call(body, name=name, out_shape=_sds((s, n), out_dtype), grid=(s // tm, n // tn, nk),
                 in_specs=specs + [pl.BlockSpec((tn, tk), lambda i, j, kk: (j, kk))],
                 out_specs=pl.BlockSpec((tm, tn), lambda i, j, kk: (i, j)),
                 scratch_shapes=[pltpu.VMEM((tm, tn), F32)], rider=rider)(*pieces, b)


def ffn_up(h, w704, gate_blk, up_blk, name, tm=512, rider=None):
    s = h.shape[0]

    def body(h_ref, wg_ref, wu_ref, g_ref, u_ref, a_ref):
        hv = h_ref[...]
        g = _dot(hv, wg_ref[...])
        u = _dot(hv, wu_ref[...])
        g_ref[...] = g.astype(BF16)
        u_ref[...] = u.astype(BF16)
        a_ref[...] = (g * _sigmoid(g) * u).astype(BF16)

    ospec = pl.BlockSpec((None, tm, FF_SHARD), lambda j, i: (j, i, 0))
    shp = _sds((N_CHIP, s, FF_SHARD), BF16)
    return _call(body, name=name, out_shape=[shp, shp, shp], grid=(N_CHIP, s // tm),
                 in_specs=[pl.BlockSpec((tm, D_MODEL), lambda j, i: (i, 0)),
                           pl.BlockSpec((None, D_MODEL, FF_SHARD), lambda j, i: (j, gate_blk, 0)),
                           pl.BlockSpec((None, D_MODEL, FF_SHARD), lambda j, i: (j, up_blk, 0))],
                 out_specs=[ospec, ospec, ospec], rider=rider)(h, w704, w704)


def ffn_down(a, w1024, blk, x, name, tm=512, rider=None, target=None):
    s = x.shape[0]

    def block_out(a_ref, wd_ref, x_ref):
        acc = _dot(a_ref[0], wd_ref[0])
        for j in range(1, N_CHIP):
            acc += _dot(a_ref[j], wd_ref[j])
        return x_ref[...] + 0.5 * acc

    def body(a_ref, wd_ref, x_ref, o_ref):
        o_ref[...] = block_out(a_ref, wd_ref, x_ref)

    def loss_body(a_ref, wd_ref, x_ref, t_ref, dy_ref, sq_ref):
        err = block_out(a_ref, wd_ref, x_ref) - t_ref[...]
        dy_ref[...] = err * (1.0 / D_MODEL)

        @pl.when(pl.program_id(0) == 0)
        def _():
            sq_ref[...] = jnp.zeros_like(sq_ref)

        sq_ref[...] += jnp.sum(err * err, axis=0, keepdims=True)

    rows = pl.BlockSpec((tm, D_MODEL), lambda i: (i, 0))
    in_specs = [pl.BlockSpec((N_CHIP, tm, FF_SHARD), lambda i: (0, i, 0)),
                pl.BlockSpec((N_CHIP, FF_SHARD, D_MODEL), lambda i: (0, blk, 0)), rows]
    if target is None:
        return _call(body, name=name, out_shape=_sds((s, D_MODEL), F32), grid=(s // tm,), in_specs=in_specs,
                     out_specs=rows, rider=rider)(a, w1024, x)
    return _call(loss_body, name=name, out_shape=[_sds((s, D_MODEL), F32), _sds((1, D_MODEL), F32)], grid=(s // tm,),
                 in_specs=in_specs + [rows], out_specs=[rows, pl.BlockSpec((1, D_MODEL), lambda i: (0, 0))],
                 rider=rider)(a, w1024, x, target)


def ffn_bwd_hidden(dx, w1024, blk, g, u, name, tm=1024, rider=None):
    s = dx.shape[0]

    def body(dx_ref, wd_ref, g_ref, u_ref, dg_ref, du_ref):
        dy = (0.5 * dx_ref[...]).astype(BF16)
        da = _dot_nt(dy, wd_ref[...])
        gv = g_ref[...].astype(F32)
        uv = u_ref[...].astype(F32)
        sg = _sigmoid(gv)
        dg_ref[...] = (da * uv * (sg * (1.0 + gv * (1.0 - sg)))).astype(BF16)
        du_ref[...] = (da * gv * sg).astype(BF16)

    hspec = pl.BlockSpec((None, tm, FF_SHARD), lambda j, i: (j, i, 0))
    shp = _sds((N_CHIP, s, FF_SHARD), BF16)
    return _call(body, name=name, out_shape=[shp, shp], grid=(N_CHIP, s // tm),
                 in_specs=[pl.BlockSpec((tm, D_MODEL), lambda j, i: (i, 0)),
                           pl.BlockSpec((None, FF_SHARD, D_MODEL), lambda j, i: (j, blk, 0)), hspec, hspec],
                 out_specs=[hspec, hspec], rider=rider)(dx, w1024, g, u)


def ffn_bwd_input(dg, du, w704, gate_blk, up_blk, x, gain, dy, name, tm=512, rider=None):
    s = dg.shape[1]

    def body(dg_ref, du_ref, wg_ref, wu_ref, x_ref, gain_ref, dy_ref, dx_ref, dgain_ref):
        dh = _dot_nt(dg_ref[0], wg_ref[0]) + _dot_nt(du_ref[0], wu_ref[0])
        for j in range(1, N_CHIP):
            dh += _dot_nt(dg_ref[j], wg_ref[j]) + _dot_nt(du_ref[j], wu_ref[j])
        xv = x_ref[...]
        r = lax.rsqrt(jnp.mean(xv * xv, axis=-1, keepdims=True) + EPS)
        xn = xv * r
        dxn = dh * gain_ref[...]
        dx_ref[...] = dy_ref[...] + r * (dxn - xn * jnp.mean(dxn * xn, axis=-1, keepdims=True))

        @pl.when(pl.program_id(0) == 0)
        def _():
            dgain_ref[...] = jnp.zeros_like(dgain_ref)

        dgain_ref[...] += jnp.sum(dh * xn, axis=0, keepdims=True)

    hspec = pl.BlockSpec((N_CHIP, tm, FF_SHARD), lambda i: (0, i, 0))
    rows = pl.BlockSpec((tm, D_MODEL), lambda i: (i, 0))
    whole = pl.BlockSpec((1, D_MODEL), lambda i: (0, 0))
    return _call(body, name=name, out_shape=[_sds((s, D_MODEL), F32), _sds((1, D_MODEL), F32)], grid=(s // tm,),
                 in_specs=[hspec, hspec,
                           pl.BlockSpec((N_CHIP, D_MODEL, FF_SHARD), lambda i: (0, gate_blk, 0), pl.Buffered(1)),
                           pl.BlockSpec((N_CHIP, D_MODEL, FF_SHARD), lambda i: (0, up_blk, 0), pl.Buffered(1)),
                           rows, whole, rows],
                 out_specs=[rows, whole], rider=rider)(dg, du, w704, w704, x, gain, dy)


def ffn_wgrad_in(h, dgu, name, ts=2048, rider=None):
    s = h.shape[0]
    ns = s // ts

    def body(h_ref, d_ref, o_ref, acc_ref):
        ss = pl.program_id(1)

        @pl.when(ss == 0)
        def _():
            acc_ref[...] = jnp.zeros_like(acc_ref)

        acc_ref[...] += _dot_tn(h_ref[...], d_ref[...])

        @pl.when(ss == ns - 1)
        def _():
            o_ref[...] = acc_ref[...].astype(BF16)

    return _call(body, name=name, out_shape=_sds((N_CHIP, D_MODEL, FF_SHARD), BF16), grid=(N_CHIP, ns),
                 in_specs=[pl.BlockSpec((ts, D_MODEL), lambda j, ss: (ss, 0)),
                           pl.BlockSpec((None, ts, FF_SHARD), lambda j, ss: (j, ss, 0))],
                 out_specs=pl.BlockSpec((None, D_MODEL, FF_SHARD), lambda j, ss: (j, 0, 0)),
                 scratch_shapes=[pltpu.VMEM((D_MODEL, FF_SHARD), F32)], rider=rider)(h, dgu)


def ffn_wgrad_down(a, dx, name, ts=2048):
    s = dx.shape[0]
    ns = s // ts

    def body(a_ref, dx_ref, o_ref, acc_ref):
        ss = pl.program_id(1)

        @pl.when(ss == 0)
        def _():
            acc_ref[...] = jnp.zeros_like(acc_ref)

        acc_ref[...] += _dot_tn(a_ref[...], (0.5 * dx_ref[...]).astype(BF16))

        @pl.when(ss == ns - 1)
        def _():
            o_ref[...] = acc_ref[...].astype(BF16)

    return _call(body, name=name, out_shape=_sds((N_CHIP, FF_SHARD, D_MODEL), BF16), grid=(N_CHIP, ns),
                 in_specs=[pl.BlockSpec((None, ts, FF_SHARD), lambda j, ss: (j, ss, 0)),
                           pl.BlockSpec((ts, D_MODEL), lambda j, ss: (ss, 0))],
                 out_specs=pl.BlockSpec((None, FF_SHARD, D_MODEL), lambda j, ss: (j, 0, 0)),
                 scratch_shapes=[pltpu.VMEM((FF_SHARD, D_MODEL), F32)])(a, dx)


def ffn_forward(x, gain, get_w704, get_w1024, tag, rms_rider=None, up_rider=None, down_rider=None, target=None):
    h = rms_fwd(x, gain, f"{tag}_rms", rider=rms_rider)
    h, rode_rms = h if rms_rider is not None else (h, None)
    res = ffn_up(h, get_w704(rode_rms), 0, 1, f"{tag}_up", rider=up_rider)
    (g, u, a), rode_up = res if up_rider is not None else (res, None)
    y = ffn_down(a, get_w1024(rode_up), 0, x, f"{tag}_down", rider=down_rider, target=target)
    y, rode_down = y if down_rider is not None else (y, None)
    return y, (h, g, u, a), rode_rms, rode_up, rode_down


def ffn_backward(dy, x, gain, w704, w1024, saved, tag, hidden_rider=None, ride_down=None, ride_in=None):
    h, g, u, a = saved
    d_wd = ffn_wgrad_down(a, dy, f"{tag}_dwd")
    rode_hidden = None
    if hidden_rider is not None:
        (dg, du), rode_hidden = ffn_bwd_hidden(dy, w1024, 0, g, u, f"{tag}_dhid", tm=512, rider=hidden_rider)
    else:
        dg, du = ffn_bwd_hidden(dy, w1024, 0, g, u, f"{tag}_dhid")
    if ride_down is not None:
        d_wg, d_wd = ffn_wgrad_in(h, dg, f"{tag}_dwg", rider=ride_down(d_wd))
    else:
        d_wg = ffn_wgrad_in(h, dg, f"{tag}_dwg")
    d_win = jnp.concatenate([d_wg, ffn_wgrad_in(h, du, f"{tag}_dwu")], axis=1)
    if ride_in is not None:
        (dx, d_gain), d_win = ffn_bwd_input(dg, du, w704, 0, 1, x, gain, dy, f"{tag}_dh", tm=256,
                                            rider=ride_in(d_win))
    else:
        dx, d_gain = ffn_bwd_input(dg, du, w704, 0, 1, x, gain, dy, f"{tag}_dh")
    return dx, d_gain, d_win, d_wd, rode_hidden


def _alibi_slope(head):
    return float(2.0 ** (-ALIBI_MAX_EXP * (head + 1) / N_ATTN_HEADS))


def _same_head():
    row = lax.broadcasted_iota(jnp.int32, (2 * HD, 2 * HD), 0)
    col = lax.broadcasted_iota(jnp.int32, (2 * HD, 2 * HD), 1)
    return ((row < HD) == (col < HD)).astype(BF16)


def _head_sums(x, same_head):
    hi = x.astype(BF16)
    lo = (x - hi.astype(F32)).astype(BF16)
    return _dot(hi, same_head) + _dot(lo, same_head)


def _head_norm(t, gain_pair, same_head):
    r = lax.rsqrt(_head_sums(t * t, same_head) * (1.0 / HD) + EPS)
    return t * r * gain_pair, r


def qk_norm_fwd(p, q_gain, k_gain, name):
    s = p.shape[0]

    def fn(i, q_ref, k_ref, qg_ref, kg_ref, qn_ref, kn_ref):
        same_head = _same_head()
        for src, g_ref, dst in ((q_ref, qg_ref, qn_ref), (k_ref, kg_ref, kn_ref)):
            for pr in range(ATTN_QKV // (2 * HD)):
                cols = slice(pr * 2 * HD, (pr + 1) * 2 * HD)
                y, _ = _head_norm(src[:, cols].astype(F32), g_ref[...], same_head)
                dst[:, cols] = y.astype(BF16)

    return _rowwise(name, fn, [(p, ATTN_QKV, 0), (p, ATTN_QKV, 1)], [q_gain, k_gain],
                    [((s, ATTN_QKV), BF16), ((s, ATTN_QKV), BF16)])


def qk_norm_bwd(p, dqs, dks, q_gain, k_gain, name):
    s = p.shape[0]
    pairs_per_pattern = GROUP_W // (2 * HD)

    def fn(i, q_ref, k_ref, dq0, dq1, dq2, dk0, dk1, dk2, qg_ref, kg_ref, dqk_ref, dqg_ref, dkg_ref):
        same_head = _same_head()

        @pl.when(i == 0)
        def _():
            dqg_ref[...] = jnp.zeros_like(dqg_ref)
            dkg_ref[...] = jnp.zeros_like(dkg_ref)

        for src, d_refs, g_ref, dst, dg_ref in (
                (q_ref, (dq0, dq1, dq2), qg_ref, dqk_ref.at[:, 0:ATTN_QKV], dqg_ref),
                (k_ref, (dk0, dk1, dk2), kg_ref, dqk_ref.at[:, ATTN_QKV:2 * ATTN_QKV], dkg_ref)):
            for pr in range(ATTN_QKV // (2 * HD)):
                cols = slice(pr * 2 * HD, (pr + 1) * 2 * HD)
                t = src[:, cols].astype(F32)
                r = lax.rsqrt(_head_sums(t * t, same_head) * (1.0 / HD) + EPS)
                xn = t * r
                within = (pr % pairs_per_pattern) * 2 * HD
                dy = d_refs[pr // pairs_per_pattern][:, within:within + 2 * HD]
                dg_ref[:, cols] += jnp.sum(dy * xn, axis=0, keepdims=True)
                dxn = dy * g_ref[...]
                mean = _head_sums(dxn * xn, same_head) * (1.0 / HD)
                dst[:, cols] = (r * (dxn - xn * mean)).astype(BF16)

    return _rowwise(name, fn, [(p, ATTN_QKV, 0), (p, ATTN_QKV, 1)] + list(dqs) + list(dks), [q_gain, k_gain],
                    [((s, 2 * ATTN_QKV), BF16)], [((1, ATTN_QKV), F32), ((1, ATTN_QKV), F32)])


def _to_streams(a, d):
    if d == 1:
        return a
    s, c = a.shape
    return a.reshape(s // d, d, c).transpose(1, 0, 2).reshape(s, c)


def _from_streams(a, d):
    if d == 1:
        return a
    s, c = a.shape
    return a.reshape(d, s // d, c).transpose(1, 0, 2).reshape(s, c)


def attn_merge_fwd(os_, lses, name):
    s = os_[0].shape[0]

    def fn(i, o0, o1, o2, l0, l1, l2, out_ref):
        m = jnp.maximum(jnp.maximum(l0[...], l1[...]), l2[...])
        e0, e1, e2 = jnp.exp(l0[...] - m), jnp.exp(l1[...] - m), jnp.exp(l2[...] - m)
        inv = 1.0 / (e0 + e1 + e2)
        out_ref[...] = ((e0 * inv) * o0[...] + (e1 * inv) * o1[...] + (e2 * inv) * o2[...]).astype(BF16)

    return _rowwise(name, fn, list(os_) + list(lses), [], [((s, GROUP_W), BF16)])[0]


def attn_merge_bwd(d_out, os_, lses, name):
    s = d_out.shape[0]

    def fn(i, do_ref, o0, o1, o2, l0, l1, l2, d0, d1, d2, c0, c1, c2):
        m = jnp.maximum(jnp.maximum(l0[...], l1[...]), l2[...])
        e0, e1, e2 = jnp.exp(l0[...] - m), jnp.exp(l1[...] - m), jnp.exp(l2[...] - m)
        inv = 1.0 / (e0 + e1 + e2)
        w0, w1, w2 = e0 * inv, e1 * inv, e2 * inv
        do = do_ref[...]
        prod = do * (w0 * o0[...] + w1 * o1[...] + w2 * o2[...])
        same_head = _same_head()
        for pr in range(GROUP_W // (2 * HD)):
            cols = slice(pr * 2 * HD, (pr + 1) * 2 * HD)
            t = _head_sums(prod[:, cols], same_head)
            for w, c_ref in ((w0, c0), (w1, c1), (w2, c2)):
                c_ref[:, cols] = w[:, cols] * t
        for w, d_ref in ((w0, d0), (w1, d1), (w2, d2)):
            d_ref[...] = (w * do).astype(BF16)

    shp = (s, GROUP_W)
    return _rowwise(name, fn, [d_out] + list(os_) + list(lses), [],
                    [(shp, BF16)] * 3 + [(shp, F32)] * 3)


def _band_constants(d):
    row = lax.broadcasted_iota(jnp.int32, (2 * BLK, 2 * BLK), 0)
    col = lax.broadcasted_iota(jnp.int32, (2 * BLK, 2 * BLK), 1)
    rel = BLK + jnp.where(row >= BLK, row - BLK, row) - col
    band = jnp.logical_and(rel >= 0, rel <= BLK)
    return (rel * d).astype(F32), band, (col >= BLK).astype(jnp.int32)


def _stack_heads(x, first):
    zero = jnp.zeros_like(x)
    return jnp.concatenate([jnp.where(first, x, zero), jnp.where(first, zero, x)], axis=0)


def _unstack_heads(x2, first):
    return jnp.where(first, x2[:BLK], x2[BLK:])


def _head_column(x):
    return jnp.concatenate([x[:, 0:1], x[:, HD:HD + 1]], axis=0)


def attn_fwd(q, k, v, pattern, name, tq=1024):
    s = q.shape[0]
    d = ATTN_DILATIONS[pattern]
    blocks_per_stream = (s // d) // BLK
    nsb = tq // BLK

    def body(q_ref, k_ref, v_ref, kp_ref, vp_ref, o_ref, l_ref):
        i = pl.program_id(0)
        rel_f, band, own = _band_constants(d)
        first = _lane_first_half((BLK, 2 * HD))
        upper = lax.broadcasted_iota(jnp.int32, (2 * BLK, 1), 0) < BLK
        for sb in range(nsb):
            rows = slice(sb * BLK, (sb + 1) * BLK)
            has_prev = ((i * nsb + sb) % blocks_per_stream != 0).astype(jnp.int32)
            mask = jnp.logical_and(band, (own + has_prev) > 0)
            for pr in range(GROUP_W // (2 * HD)):
                cols = slice(pr * 2 * HD, (pr + 1) * 2 * HD)
                if sb == 0:
                    kcat = jnp.concatenate([kp_ref[:, cols], k_ref[rows, cols]], axis=0)
                    vcat = jnp.concatenate([vp_ref[:, cols], v_ref[rows, cols]], axis=0)
                else:
                    both = slice((sb - 1) * BLK, (sb + 1) * BLK)
                    kcat, vcat = k_ref[both, cols], v_ref[both, cols]
                h0 = pattern * HEADS_PER_PATTERN + 2 * pr
                slope = jnp.where(upper, _alibi_slope(h0), _alibi_slope(h0 + 1))
                sc = _dot_nt(_stack_heads(q_ref[rows, cols], first), kcat) * 0.125 - slope * rel_f
                sc = jnp.where(mask, sc, NEG)
                m = jnp.max(sc, axis=-1, keepdims=True)
                p = jnp.exp(sc - m)
                l = jnp.sum(p, axis=-1, keepdims=True)
                o2 = _dot((p * (1.0 / l)).astype(BF16), vcat)
                o_ref[rows, cols] = _unstack_heads(o2, first)
                lse = m + jnp.log(l)
                l_ref[rows, cols] = jnp.where(first, lse[:BLK], lse[BLK:])

    cur = pl.BlockSpec((tq, GROUP_W), lambda i: (i, 0))
    prev = pl.BlockSpec((BLK, GROUP_W), lambda i: (jnp.maximum(i * nsb - 1, 0), 0))
    return _call(body, name=name, out_shape=[_sds((s, GROUP_W), F32), _sds((s, GROUP_W), F32)], grid=(s // tq,),
                 in_specs=[cur, cur, cur, prev, prev], out_specs=[cur, cur])(q, k, v, k, v)


def attn_bwd(q, k, v, d_o, cterm, lse, pattern, name, tq=1024):
    s = q.shape[0]
    d = ATTN_DILATIONS[pattern]
    blocks_per_stream = (s // d) // BLK
    nsb = tq // BLK
    n_blocks = s // BLK

    def body(q_ref, k_ref, v_ref, do_ref, c_ref, l_ref, kp_ref, vp_ref, qn_ref, kn_ref, vn_ref, don_ref, cn_ref,
             ln_ref, dq_ref, dk_ref, dv_ref):
        i = pl.program_id(0)
        rel_f, band, own = _band_constants(d)
        first = _lane_first_half((BLK, 2 * HD))
        upper = lax.broadcasted_iota(jnp.int32, (2 * BLK, 1), 0) < BLK
        dk_ref[...] = jnp.zeros_like(dk_ref)
        dv_ref[...] = jnp.zeros_like(dv_ref)
        for sb in range(nsb + 1):
            gb = i * nsb + sb
            rows = slice(sb * BLK, (sb + 1) * BLK)
            before = slice((sb - 1) * BLK, sb * BLK)
            inside = (gb < n_blocks).astype(jnp.int32)
            has_prev = jnp.logical_and(gb % blocks_per_stream != 0, gb < n_blocks).astype(jnp.int32)
            mask = jnp.logical_and(band, (own * inside + has_prev) > 0)
            for pr in range(GROUP_W // (2 * HD)):
                cols = slice(pr * 2 * HD, (pr + 1) * 2 * HD)
                if sb == 0:
                    kcat = jnp.concatenate([kp_ref[:, cols], k_ref[rows, cols]], axis=0)
                    vcat = jnp.concatenate([vp_ref[:, cols], v_ref[rows, cols]], axis=0)
                elif sb == nsb:
                    kcat = jnp.concatenate([k_ref[before, cols], kn_ref[:, cols]], axis=0)
                    vcat = jnp.concatenate([v_ref[before, cols], vn_ref[:, cols]], axis=0)
                else:
                    both = slice((sb - 1) * BLK, (sb + 1) * BLK)
                    kcat, vcat = k_ref[both, cols], v_ref[both, cols]
                if sb < nsb:
                    qp, dop, cp, lp = q_ref[rows, cols], do_ref[rows, cols], c_ref[rows, cols], l_ref[rows, cols]
                else:
                    qp, dop, cp, lp = qn_ref[:, cols], don_ref[:, cols], cn_ref[:, cols], ln_ref[:, cols]
                h0 = pattern * HEADS_PER_PATTERN + 2 * pr
                slope = jnp.where(upper, _alibi_slope(h0), _alibi_slope(h0 + 1))
                q2 = _stack_heads(qp, first)
                do2 = _stack_heads(dop, first)
                sc = jnp.where(mask, _dot_nt(q2, kcat) * 0.125 - slope * rel_f, NEG)
                pm = jnp.exp(sc - _head_column(lp))
                dl = (pm * (_dot_nt(do2, vcat) - _head_column(cp))).astype(BF16)
                if sb < nsb:
                    dq_ref[rows, cols] = _unstack_heads(_dot(dl, kcat), first) * 0.125
                dk2 = _dot_tn(dl, q2) * 0.125
                dv2 = _dot_tn(pm.astype(BF16), do2)
                if sb > 0:
                    dk_ref[before, cols] += dk2[:BLK]
                    dv_ref[before, cols] += dv2[:BLK]
                if sb < nsb:
                    dk_ref[rows, cols] += dk2[BLK:]
                    dv_ref[rows, cols] += dv2[BLK:]

    cur = pl.BlockSpec((tq, GROUP_W), lambda i: (i, 0))
    prev = pl.BlockSpec((BLK, GROUP_W), lambda i: (jnp.maximum(i * nsb - 1, 0), 0))
    nxt = pl.BlockSpec((BLK, GROUP_W), lambda i: (jnp.minimum((i + 1) * nsb, n_blocks - 1), 0))
    shp = _sds((s, GROUP_W), F32)
    return _call(body, name=name, out_shape=[shp, shp, shp], grid=(s // tq,),
                 in_specs=[cur] * 6 + [prev, prev] + [nxt] * 6, out_specs=[cur, cur, cur])(
                     q, k, v, d_o, cterm, lse, k, v, q, k, v, d_o, cterm, lse)


HALO = 16
CONV_TQ = 512


def conv_fwd(p, w, b, name):
    s = p.shape[0]
    tq = CONV_TQ
    ncol = SSD_CONV_DIM // GROUP_W
    cb0 = COL_XBC // GROUP_W

    def body(u_ref, up_ref, w_ref, b_ref, c_ref, xc_ref):
        i = pl.program_id(0)
        prev = (up_ref[...].astype(F32) * (i > 0).astype(F32)).astype(BF16)
        ext = jnp.concatenate([prev, u_ref[...]], axis=0)
        row = lax.broadcasted_iota(jnp.int32, (BLK, BLK + HALO), 0)
        col = lax.broadcasted_iota(jnp.int32, (BLK, BLK + HALO), 1)
        for blk in range(tq // BLK):
            lead = ext[blk * BLK:blk * BLK + BLK + HALO]
            acc = b_ref[...] + w_ref[SSD_CONV - 1:SSD_CONV, :] * lead[HALO:].astype(F32)
            for kk in range(SSD_CONV - 1):
                pick = (col == row + HALO - (SSD_CONV - 1 - kk)).astype(BF16)
                acc += w_ref[kk:kk + 1, :] * _dot(pick, lead)
            rows = slice(blk * BLK, (blk + 1) * BLK)
            c_ref[rows, :] = acc.astype(BF16)
            xc_ref[rows, :] = (acc * _sigmoid(acc)).astype(BF16)

    cur_in = pl.BlockSpec((tq, GROUP_W), lambda i, j: (i, cb0 + j))
    prev_in = pl.BlockSpec((HALO, GROUP_W), lambda i, j: (jnp.maximum(i * (tq // HALO) - 1, 0), cb0 + j))
    cur_out = pl.BlockSpec((tq, GROUP_W), lambda i, j: (i, j))
    shp = _sds((s, SSD_CONV_DIM), BF16)
    return _call(body, name=name, out_shape=[shp, shp], grid=(s // tq, ncol),
                 in_specs=[cur_in, prev_in, pl.BlockSpec((SSD_CONV, GROUP_W), lambda i, j: (0, j)),
                           pl.BlockSpec((1, GROUP_W), lambda i, j: (0, j))],
                 out_specs=[cur_out, cur_out])(p, p, w, b)


def conv_bwd(p, cpre, dxs, d_b, d_c, w, name):
    s = p.shape[0]
    tq = CONV_TQ
    ncol = SSD_CONV_DIM // GROUP_W
    n_xs = SSD_INNER // GROUP_W
    cb0 = COL_XBC // GROUP_W
    nt = s // tq

    def body(u_ref, c_ref, cn_ref, dx_ref, dxn_ref, dbm_ref, dbmn_ref, dcm_ref, dcmn_ref, w_ref,
             du_ref, dw_ref, db_ref):
        j, i = pl.program_id(0), pl.program_id(1)

        def dpre(c16, dx):
            c = c16.astype(F32)
            sg = _sigmoid(c)
            return dx * (sg * (1.0 + c * (1.0 - sg)))

        def pick(a_ref, b_ref, c_ref_):
            return jnp.where(j < n_xs, a_ref[...], jnp.where(j == n_xs, b_ref[...], c_ref_[...]))

        dc = dpre(c_ref[...], pick(dx_ref, dbm_ref, dcm_ref))
        dcn = dpre(cn_ref[...], pick(dxn_ref, dbmn_ref, dcmn_ref)) * (i < nt - 1).astype(F32)
        dext = jnp.concatenate([dc, dcn], axis=0)
        u = u_ref[...].astype(F32)

        @pl.when(i == 0)
        def _():
            dw_ref[...] = jnp.zeros_like(dw_ref)
            db_ref[...] = jnp.zeros_like(db_ref)

        du = w_ref[SSD_CONV - 1:SSD_CONV, :] * dc
        dw_ref[SSD_CONV - 1:SSD_CONV, :] += jnp.sum(dc * u, axis=0, keepdims=True)
        for kk in range(SSD_CONV - 1):
            sh = SSD_CONV - 1 - kk
            ahead = pltpu.roll(dext, tq + HALO - sh, 0)[0:tq]
            du += w_ref[kk:kk + 1, :] * ahead
            dw_ref[kk:kk + 1, :] += jnp.sum(ahead * u, axis=0, keepdims=True)
        du_ref[...] = du.astype(BF16)
        db_ref[...] += jnp.sum(dc, axis=0, keepdims=True)

    hb = tq // HALO
    cur_p = pl.BlockSpec((tq, GROUP_W), lambda j, i: (i, cb0 + j))
    cur = pl.BlockSpec((tq, GROUP_W), lambda j, i: (i, j))
    nxt = pl.BlockSpec((HALO, GROUP_W), lambda j, i: (jnp.minimum((i + 1) * hb, s // HALO - 1), j))

    def piece(first_tile, n_tiles):
        def on(j):
            return jnp.logical_and(j >= first_tile, j < first_tile + n_tiles)

        def col(j):
            return jnp.clip(j - first_tile, 0, n_tiles - 1)

        return (pl.BlockSpec((tq, GROUP_W), lambda j, i: (jnp.where(on(j), i, 0), col(j))),
                pl.BlockSpec((HALO, GROUP_W),
                             lambda j, i: (jnp.where(on(j), jnp.minimum((i + 1) * hb, s // HALO - 1), 0), col(j))))

    return _call(body, name=name,
                 out_shape=[_sds((s, SSD_CONV_DIM), BF16), _sds((8, SSD_CONV_DIM), F32), _sds((1, SSD_CONV_DIM), F32)],
                 grid=(ncol, nt),
                 in_specs=[cur_p, cur, nxt, *piece(0, n_xs), *piece(n_xs, 1), *piece(n_xs + 1, 1),
                           pl.BlockSpec((SSD_CONV, GROUP_W), lambda j, i: (0, j))],
                 out_specs=[cur, pl.BlockSpec((8, GROUP_W), lambda j, i: (0, j)),
                            pl.BlockSpec((1, GROUP_W), lambda j, i: (0, j))])(
                                p, cpre, cpre, dxs, dxs, d_b, d_b, d_c, d_c, w)


def _softplus(x):
    return jnp.maximum(x, 0.0) + jnp.log(1.0 + jnp.exp(-jnp.abs(x)))


def _ssd_decays(dtr_ref, dtrt_ref, bias_ref, biast_ref, alog_ref, alogt_ref):
    row = lax.broadcasted_iota(jnp.int32, (BLK, BLK), 0)
    col = lax.broadcasted_iota(jnp.int32, (BLK, BLK), 1)
    lower = (row >= col).astype(F32)
    upper = (row <= col).astype(F32)
    dtb = dtr_ref[...] + bias_ref[...]
    dt = _softplus(dtb)
    a = dt * (-jnp.exp(alog_ref[...]))
    cs = _dot_hi(lower, a)
    a_t = _softplus(dtrt_ref[...] + biast_ref[...]) * (-jnp.exp(alogt_ref[...]))
    cs_t = _dot_hi(a_t, upper)
    return dtb, dt, cs, cs_t, row, col, upper


SSD_GROUPS_PER_STEP = 4


def _per_group(body, gps, kinds):
    def wrapped(*refs):
        for gi in range(gps):
            args, pos = [], 0
            for kind, n in kinds:
                if kind == "each":
                    args.append(refs[pos + gi])
                    pos += gps
                    continue
                ref = refs[pos]
                pos += 1
                if kind == "cols":
                    args.append(ref.at[:, gi * n:(gi + 1) * n])
                else:
                    args.append(ref.at[gi] if n == 1 else ref.at[pl.ds(gi * n, n)])
            body(*args)

    return wrapped


def ssd_fwd(p, xc, dtg, dtg_t, params, gn, name):
    s = p.shape[0]
    nc = s // BLK
    bias, bias_t, alog, alog_t, dskip = params

    def body(xs_ref, b_ref, c_ref, z_ref, dtr_ref, dtrt_ref, bias_ref, biast_ref, alog_ref, alogt_ref, dsk_ref,
             gn_ref, y_ref, sin_ref, hp_ref, h_ref):
        c_idx = pl.program_id(1)

        @pl.when(c_idx == 0)
        def _():
            h_ref[...] = jnp.zeros_like(h_ref)

        _, dt, cs, cs_t, row, col, _ = _ssd_decays(dtr_ref, dtrt_ref, bias_ref, biast_ref, alog_ref, alogt_ref)
        first = _lane_first_half((BLK, 2 * HD))
        first_row = _lane_first_half((1, 2 * HD))
        tril = row >= col
        b16, c16 = b_ref[...], c_ref[...]
        cb = _dot_nt(c16, b16)
        n_pairs = GROUP_W // (2 * HD)
        tot = cs[BLK - 1:BLK, :]
        exp_cs, exp_rest, exp_tot = jnp.exp(cs), jnp.exp(tot - cs), jnp.exp(tot)

        lanes_of_head = (lax.broadcasted_iota(jnp.int32, (8, GROUP_W), 1) // HD
                         == lax.broadcasted_iota(jnp.int32, (8, GROUP_W), 0)).astype(BF16)

        def per_head(v, mask):
            if v.shape[0] == 1:
                return jnp.concatenate([jnp.where(mask, v[:, 2 * pr:2 * pr + 1], v[:, 2 * pr + 1:2 * pr + 2])
                                        for pr in range(n_pairs)], axis=1)
            hi = v.astype(BF16)
            lo = (v - hi.astype(F32)).astype(BF16)
            return _dot(hi, lanes_of_head) + _dot(lo, lanes_of_head)

        xs = xs_ref[...].astype(F32)
        xt = xs * per_head(dt, first)
        xt16 = xt.astype(BF16)
        hstate = jnp.concatenate([h_ref[pr] for pr in range(n_pairs)], axis=1)
        for pr in range(n_pairs):
            hp_ref[pr] = h_ref[pr]
        y_off = per_head(exp_cs, first) * _dot(c16, hstate.astype(BF16))
        new = per_head(exp_tot, first_row) * hstate + _dot_tn(b16, (per_head(exp_rest, first) * xt).astype(BF16))
        for pr in range(n_pairs):
            h_ref[pr] = new[:, pr * 2 * HD:(pr + 1) * 2 * HD]
        y_diag = []
        for pr in range(n_pairs):
            cols = slice(pr * 2 * HD, (pr + 1) * 2 * HD)
            m2 = jnp.concatenate(
                [(cb * jnp.exp(jnp.where(tril, cs[:, h:h + 1] - cs_t[h:h + 1, :], NEG))).astype(BF16)
                 for h in (2 * pr, 2 * pr + 1)], axis=1)
            y_diag.append(_dot(m2, _stack_heads(xt16[:, cols], first)))
        y = jnp.concatenate(y_diag, axis=1) + y_off + xs * per_head(dsk_ref[...], first_row)
        y_ref[...] = y
        zv = z_ref[...].astype(F32)
        yz = y * (zv * _sigmoid(zv))
        r = lax.rsqrt(jnp.mean(yz * yz, axis=-1, keepdims=True) + EPS)
        sin_ref[...] = (yz * r * gn_ref[...]).astype(BF16)

    gps = SSD_GROUPS_PER_STEP
    wide, narrow, lead = ("cols", GROUP_W), ("cols", BLK), ("lead", 1)
    kinds = [wide, narrow, narrow, ("each", 0)] + [lead] * 7 + [wide, wide, wide, lead, ("lead", 4)]
    wide_w, narrow_w = GROUP_W * gps, BLK * gps
    gparam = pl.BlockSpec((gps, 1, 8), lambda g, c: (g, 0, 0))
    gparam_t = pl.BlockSpec((gps, 8, 1), lambda g, c: (g, 0, 0))
    z_specs = [pl.BlockSpec((BLK, GROUP_W), functools.partial(lambda g, c, gi: (c, COL_Z // GROUP_W + gps * g + gi),
                                                              gi=gi)) for gi in range(gps)]
    return _call(
        _per_group(body, gps, kinds), name=name,
        out_shape=[_sds((s, SSD_INNER), F32), _sds((s, SSD_INNER), BF16),
                   _sds((SSD_GROUPS, nc, 4, BLK, 2 * HD), F32)],
        grid=(SSD_GROUPS // gps, nc),
        in_specs=[pl.BlockSpec((BLK, wide_w), lambda g, c: (c, g)),
                  pl.BlockSpec((BLK, narrow_w), lambda g, c: (c, SSD_INNER // narrow_w + g)),
                  pl.BlockSpec((BLK, narrow_w), lambda g, c: (c, (SSD_INNER + SSD_GROUPS * BLK) // narrow_w + g)),
                  *z_specs,
                  pl.BlockSpec((gps, BLK, 8), lambda g, c: (g, c, 0)),
                  pl.BlockSpec((gps, 8, BLK), lambda g, c: (g, 0, c)),
                  gparam, gparam_t, gparam, gparam_t, gparam,
                  pl.BlockSpec((1, wide_w), lambda g, c: (0, g))],
        out_specs=[pl.BlockSpec((BLK, wide_w), lambda g, c: (c, g)),
                   pl.BlockSpec((BLK, wide_w), lambda g, c: (c, g)),
                   pl.BlockSpec((gps, None, 4, BLK, 2 * HD), lambda g, c: (g, c, 0, 0, 0))],
        scratch_shapes=[pltpu.VMEM((4 * gps, BLK, 2 * HD), F32)],
    )(xc, xc, xc, *([p] * gps), dtg, dtg_t, bias, bias_t, alog, alog_t, dskip, gn)


def ssd_bwd(p, xc, y, d_sin, hprev, dtg, dtg_t, params, gn, name):
    s = p.shape[0]
    nc = s // BLK
    bias, bias_t, alog, alog_t, dskip = params

    def body(xs_ref, b_ref, c_ref, z_ref, y_ref, dsin_ref, hp_ref, dtr_ref, dtrt_ref, bias_ref,
             biast_ref, alog_ref, alogt_ref, dsk_ref, gn_ref,
             dxs_ref, db_ref, dc_ref, dz_ref, ddt_ref, da_ref, dbias_ref, ddsk_ref, dgn_ref, dh_ref):
        c_idx = pl.program_id(1)

        @pl.when(c_idx == 0)
        def _():
            dh_ref[...] = jnp.zeros_like(dh_ref)
            da_ref[...] = jnp.zeros_like(da_ref)
            dbias_ref[...] = jnp.zeros_like(dbias_ref)
            ddsk_ref[...] = jnp.zeros_like(ddsk_ref)
            dgn_ref[...] = jnp.zeros_like(dgn_ref)

        dtb, dt, cs, cs_t, row, col, upper = _ssd_decays(dtr_ref, dtrt_ref, bias_ref, biast_ref, alog_ref, alogt_ref)
        first = _lane_first_half((BLK, 2 * HD))
        first_row = _lane_first_half((1, 2 * HD))
        tril = row >= col
        triu = row <= col
        last_row = lax.broadcasted_iota(jnp.int32, (BLK, 1), 0) == BLK - 1
        lane8 = lax.broadcasted_iota(jnp.int32, (BLK, 8), 1)

        yv = y_ref[...]
        zv = z_ref[...].astype(F32)
        sg = _sigmoid(zv)
        yz = yv * (zv * sg)
        r = lax.rsqrt(jnp.mean(yz * yz, axis=-1, keepdims=True) + EPS)
        yzn = yz * r
        dsn = dsin_ref[...]
        dgn_ref[...] += jnp.sum(dsn * yzn, axis=0, keepdims=True)
        dsn = dsn * gn_ref[...]
        dyz = r * (dsn - yzn * jnp.mean(dsn * yzn, axis=-1, keepdims=True))
        dy = dyz * (zv * sg)
        dz_ref[...] = (dyz * yv * (sg * (1.0 + zv * (1.0 - sg)))).astype(BF16)
        xs_all = xs_ref[...].astype(F32)
        ddsk_ref[...] += jnp.sum(dy * xs_all, axis=0, keepdims=True)

        b16, c16 = b_ref[...], c_ref[...]
        cb = _dot_nt(c16, b16)
        cb_t = _dot_nt(b16, c16)
        n_pairs = GROUP_W // (2 * HD)
        tot = cs[BLK - 1:BLK, :]
        exp_cs, exp_rest, exp_tot = jnp.exp(cs), jnp.exp(tot - cs), jnp.exp(tot)

        lanes_of_head = (lax.broadcasted_iota(jnp.int32, (8, GROUP_W), 1) // HD
                         == lax.broadcasted_iota(jnp.int32, (8, GROUP_W), 0)).astype(BF16)

        def per_head(v, mask):
            if v.shape[0] == 1:
                return jnp.concatenate([jnp.where(mask, v[:, 2 * pr:2 * pr + 1], v[:, 2 * pr + 1:2 * pr + 2])
                                        for pr in range(n_pairs)], axis=1)
            hi = v.astype(BF16)
            lo = (v - hi.astype(F32)).astype(BF16)
            return _dot(hi, lanes_of_head) + _dot(lo, lanes_of_head)

        head_of_lane = (lax.broadcasted_iota(jnp.int32, (GROUP_W, 8), 0) // HD
                        == lax.broadcasted_iota(jnp.int32, (GROUP_W, 8), 1)).astype(BF16)

        def head_sums(v):
            hi = v.astype(BF16)
            lo = (v - hi.astype(F32)).astype(BF16)
            return _dot(hi, head_of_lane) + _dot(lo, head_of_lane)

        dt_w, e_w, f_w = per_head(dt, first), per_head(exp_cs, first), per_head(exp_rest, first)
        xt = xs_all * dt_w
        xt16 = xt.astype(BF16)
        hstate = jnp.concatenate([hp_ref[pr] for pr in range(n_pairs)], axis=1)
        h16 = hstate.astype(BF16)
        dhn = jnp.concatenate([dh_ref[pr] for pr in range(n_pairs)], axis=1)
        dhn16 = dhn.astype(BF16)
        edy16 = (e_w * dy).astype(BF16)
        y_off = e_w * _dot(c16, h16)
        dcs_all = head_sums(dy * y_off)
        dc_acc = _dot_nt(edy16, h16)
        zmat = _dot(b16, dhn16)
        t_all = head_sums(zmat * xt) * exp_rest
        hh_rows = jnp.sum(head_sums(dhn * hstate), axis=0, keepdims=True)
        dtot = jnp.sum(t_all, axis=0, keepdims=True) + hh_rows * exp_tot
        dcs_all = dcs_all - t_all + jnp.where(last_row, dtot, 0.0)
        fxt16 = (f_w * xt).astype(BF16)
        db_acc = _dot_nt(fxt16, dhn16)
        dh_new = _dot_tn(c16, edy16) + per_head(exp_tot, first_row) * dhn
        for pr in range(n_pairs):
            dh_ref[pr] = dh_new[:, pr * 2 * HD:(pr + 1) * 2 * HD]
        g_sum = jnp.zeros((BLK, BLK), F32)
        gt_sum = jnp.zeros((BLK, BLK), F32)
        d_xt_parts = []
        for pr in range(n_pairs):
            cols = slice(pr * 2 * HD, (pr + 1) * 2 * HD)
            dym2 = _stack_heads(dy[:, cols].astype(BF16), first)
            d_m2 = _dot_nt(dym2, xt16[:, cols])
            d_mt2 = _dot_nt(xt16[:, cols], dym2)
            mt2 = []
            for e, h in enumerate((2 * pr, 2 * pr + 1)):
                cs_c, cs_r = cs[:, h:h + 1], cs_t[h:h + 1, :]
                decay = jnp.exp(jnp.where(tril, cs_c - cs_r, NEG))
                decay_t = jnp.exp(jnp.where(triu, cs_r - cs_c, NEG))
                gm = d_m2[e * BLK:(e + 1) * BLK] * decay
                gmt = d_mt2[:, e * BLK:(e + 1) * BLK] * decay_t
                g_sum += gm
                gt_sum += gmt
                dcs_h = jnp.sum(gm * cb, axis=-1, keepdims=True) - jnp.sum(gmt * cb_t, axis=-1, keepdims=True)
                dcs_all = dcs_all + jnp.where(lane8 == h, dcs_h, 0.0)
                mt2.append((cb_t * decay_t).astype(BF16))
            d_xt_parts.append(_dot(jnp.concatenate(mt2, axis=1), dym2))
        d_xt = jnp.concatenate(d_xt_parts, axis=1) + f_w * zmat
        dxs_ref[...] = dy * per_head(dsk_ref[...], first_row) + d_xt * dt_w
        ddtx_all = head_sums(d_xt * xs_all)

        dc_ref[...] = dc_acc + _dot(g_sum.astype(BF16), b16)
        db_ref[...] = db_acc + _dot(gt_sum.astype(BF16), c16)
        d_a = _dot_hi(upper, dcs_all)
        a_neg = -jnp.exp(alog_ref[...])
        ddt = ddtx_all + d_a * a_neg
        da_ref[...] += jnp.sum(d_a * dt, axis=0, keepdims=True)
        ddtr = ddt * _sigmoid(dtb)
        ddt_ref[...] = ddtr
        dbias_ref[...] += jnp.sum(ddtr, axis=0, keepdims=True)

    gps = SSD_GROUPS_PER_STEP
    k_wide, k_narrow, k_lead = ("cols", GROUP_W), ("cols", BLK), ("lead", 1)
    kinds = ([k_wide, k_narrow, k_narrow, ("each", 0), k_wide, k_wide] + [k_lead] * 8 + [k_wide]
             + [k_wide, k_narrow, k_narrow, k_wide] + [k_lead] * 4 + [k_wide] + [("lead", 4)])
    wide_w, narrow_w = GROUP_W * gps, BLK * gps
    rc = lambda c: nc - 1 - c
    gparam = pl.BlockSpec((gps, 1, 8), lambda g, c: (g, 0, 0))
    gparam_t = pl.BlockSpec((gps, 8, 1), lambda g, c: (g, 0, 0))
    wide = pl.BlockSpec((BLK, wide_w), lambda g, c: (rc(c), g))
    narrow = pl.BlockSpec((BLK, narrow_w), lambda g, c: (rc(c), g))
    z_specs = [pl.BlockSpec((BLK, GROUP_W),
                            functools.partial(lambda g, c, gi: (rc(c), COL_Z // GROUP_W + gps * g + gi), gi=gi))
               for gi in range(gps)]
    return _call(
        _per_group(body, gps, kinds), name=name,
        out_shape=[_sds((s, SSD_INNER), F32), _sds((s, GROUP_W), F32), _sds((s, GROUP_W), F32),
                   _sds((s, SSD_INNER), BF16), _sds((SSD_GROUPS, s, 8), F32),
                   _sds((SSD_GROUPS, 1, 8), F32), _sds((SSD_GROUPS, 1, 8), F32),
                   _sds((SSD_GROUPS, 1, GROUP_W), F32), _sds((1, SSD_INNER), F32)],
        grid=(SSD_GROUPS // gps, nc),
        in_specs=[wide,
                  pl.BlockSpec((BLK, narrow_w), lambda g, c: (rc(c), SSD_INNER // narrow_w + g)),
                  pl.BlockSpec((BLK, narrow_w), lambda g, c: (rc(c), (SSD_INNER + SSD_GROUPS * BLK) // narrow_w + g)),
                  *z_specs,
                  wide, wide,
                  pl.BlockSpec((gps, None, 4, BLK, 2 * HD), lambda g, c: (g, rc(c), 0, 0, 0)),
                  pl.BlockSpec((gps, BLK, 8), lambda g, c: (g, rc(c), 0)),
                  pl.BlockSpec((gps, 8, BLK), lambda g, c: (g, 0, rc(c))),
                  gparam, gparam_t, gparam, gparam_t, gparam,
                  pl.BlockSpec((1, wide_w), lambda g, c: (0, g))],
        out_specs=[wide, narrow, narrow, wide,
                   pl.BlockSpec((gps, BLK, 8), lambda g, c: (g, rc(c), 0)),
                   gparam, gparam,
                   pl.BlockSpec((gps, 1, GROUP_W), lambda g, c: (g, 0, 0)),
                   pl.BlockSpec((1, wide_w), lambda g, c: (0, g))],
        scratch_shapes=[pltpu.VMEM((4 * gps, BLK, 2 * HD), F32)],
    )(xc, xc, xc, *([p] * gps), y, d_sin, hprev, dtg, dtg_t, bias, bias_t, alog, alog_t, dskip, gn)


def merge_fwd(p, a, sbr, name, tm=512):
    s = p.shape[0]
    nj = D_MODEL // GROUP_W

    def body(ga_ref, gs_ref, a_ref, s_ref, o_ref):
        o_ref[...] = (_sigmoid(ga_ref[...].astype(F32)) * a_ref[...]
                      + _sigmoid(gs_ref[...].astype(F32)) * s_ref[...]).astype(BF16)

    blk = pl.BlockSpec((tm, GROUP_W), lambda i, j: (i, j))
    return _call(body, name=name, out_shape=_sds((s, D_MODEL), BF16), grid=(s // tm, nj),
                 in_specs=[pl.BlockSpec((tm, GROUP_W), lambda i, j: (i, COL_GA // GROUP_W + j)),
                           pl.BlockSpec((tm, GROUP_W), lambda i, j: (i, COL_GS // GROUP_W + j)), blk, blk],
                 out_specs=blk)(p, p, a, sbr)


def merge_bwd(p, a, sbr, dmerged, name, tm=512):
    s = p.shape[0]
    nj = D_MODEL // GROUP_W

    def body(ga_ref, gs_ref, a_ref, s_ref, dm_ref, da_ref, ds_ref, dga_ref, dgs_ref):
        dm = dm_ref[...]
        sa = _sigmoid(ga_ref[...].astype(F32))
        ss = _sigmoid(gs_ref[...].astype(F32))
        da_ref[...] = (dm * sa).astype(BF16)
        ds_ref[...] = (dm * ss).astype(BF16)
        dga_ref[...] = (dm * a_ref[...] * sa * (1.0 - sa)).astype(BF16)
        dgs_ref[...] = (dm * s_ref[...] * ss * (1.0 - ss)).astype(BF16)

    blk = pl.BlockSpec((tm, GROUP_W), lambda i, j: (i, j))
    shp = _sds((s, D_MODEL), BF16)
    return _call(body, name=name, out_shape=[shp] * 4, grid=(s // tm, nj),
                 in_specs=[pl.BlockSpec((tm, GROUP_W), lambda i, j: (i, COL_GA // GROUP_W + j)),
                           pl.BlockSpec((tm, GROUP_W), lambda i, j: (i, COL_GS // GROUP_W + j)), blk, blk, blk],
                 out_specs=[blk] * 4)(p, p, a, sbr, dmerged)


def _group_major(v):
    return v.reshape(SSD_GROUPS, 1, 8), v.reshape(SSD_GROUPS, 8, 1)


def mixer_forward(x, w, rider=None, later_weights=None):
    s = x.shape[0]
    h = rms_fwd(x, w["mix_norm"], "mix_rms")
    p = matmul_nn(h, w["w_in_main"], "mix_proj", BF16, tm=2048, tn=512, rider=rider)
    rode = None
    if rider is not None:
        p, rode = p
        w = dict(w, **later_weights(rode))
    dt_raw = matmul_nn(h, w["w_in_dt"], "mix_proj_dt", F32, tm=1024, tn=DT_PAD)
    qn, kn = qk_norm_fwd(p, w["q_gain"], w["k_gain"], "qk_norm")
    streams, os_, lses = [], [], []
    for g, d in enumerate(ATTN_DILATIONS):
        cols = slice(g * GROUP_W, (g + 1) * GROUP_W)
        qs, ks = _to_streams(qn[:, cols], d), _to_streams(kn[:, cols], d)
        vs = _to_streams(p[:, COL_V + g * GROUP_W:COL_V + (g + 1) * GROUP_W], d)
        o, lse = attn_fwd(qs, ks, vs, g, f"attn_fwd{g}")
        streams.append((qs, ks, vs, lse))
        os_.append(_from_streams(o, d))
        lses.append(_from_streams(lse, d))
    attn_o = attn_merge_fwd(os_, lses, "attn_merge")
    cpre, xc = conv_fwd(p, w["conv_w"], w["conv_b"], "conv_fwd")
    dtg = dt_raw[:, :SSD_HEADS].reshape(s, SSD_GROUPS, 8).transpose(1, 0, 2)
    dtg_t = dtg.transpose(0, 2, 1)
    params = (*_group_major(w["dt_bias"]), *_group_major(w["a_log"]), _group_major(w["d_skip"])[0])
    y, s_in, hprev = ssd_fwd(p, xc, dtg, dtg_t, params, w["ssd_norm"], "ssd_fwd")
    a = matmul_nn(attn_o, w["w_attn_branch"], "attn_branch", F32, tm=1024, tn=512)
    sbr = matmul_nn(s_in, w["w_ssd_branch"], "ssd_branch", F32, tm=1024, tn=512)
    merged = merge_fwd(p, a, sbr, "merge")
    x_out = matmul_nn(merged, w["w_out"], "mix_out", F32, tm=1024, tn=512, res=x)
    saved = dict(h=h, p=p, streams=streams, os=os_, lses=lses, attn_o=attn_o, cpre=cpre, xc=xc, dtg=dtg,
                 dtg_t=dtg_t, params=params, y=y, s_in=s_in, hprev=hprev, a=a, sbr=sbr, merged=merged, w=w)
    return x_out, saved, rode


def mixer_backward(dx_out, x, sv, ride_early=None, ride_late=None):
    s = x.shape[0]
    p = sv["p"]
    w = sv["w"]
    g = {}
    dmerged = matmul_nt(dx_out, w["w_out"], "d_merged", F32, tm=1024, tn=512, tk=1024)
    g["w_out"] = matmul_tn(sv["merged"], dx_out, "dw_out", tn=512, ts=2048)
    da, ds, dga, dgs = merge_bwd(p, sv["a"], sv["sbr"], dmerged, "merge_bwd")
    g["w_attn_branch"] = matmul_tn(sv["attn_o"], da, "dw_attn_branch", tn=512, ts=2048)
    g["w_ssd_branch"] = matmul_tn(sv["s_in"], ds, "dw_ssd_branch", tn=512, ts=2048)
    d_attn_o = matmul_nt(da, w["w_attn_branch"], "d_attn_o", F32, tm=1024, tn=512, tk=1024)
    d_sin = matmul_nt(ds, w["w_ssd_branch"], "d_ssd_in", F32, tm=1024, tn=512, tk=1024)
    dxs, d_b, d_c, dz, ddt, d_asum, d_bias, d_dsk, d_gn = ssd_bwd(
        p, sv["xc"], sv["y"], d_sin, sv["hprev"], sv["dtg"], sv["dtg_t"], sv["params"], w["ssd_norm"], "ssd_bwd")
    dxbc, d_convw, d_convb = conv_bwd(p, sv["cpre"], dxs, d_b, d_c, w["conv_w"], "conv_bwd")
    g["conv_w"] = d_convw[:SSD_CONV]
    g["conv_b"] = d_convb
    g["dt_bias"] = d_bias.reshape(1, SSD_HEADS)
    g["a_log"] = (d_asum * (-jnp.exp(sv["params"][2]))).reshape(1, SSD_HEADS)
    g["d_skip"] = jnp.sum(d_dsk.reshape(SSD_HEADS, HD), axis=1).reshape(1, SSD_HEADS)
    g["ssd_norm"] = d_gn
    merged_bwd = attn_merge_bwd(d_attn_o, sv["os"], sv["lses"], "attn_merge_bwd")
    dqs, dks, dvs = [], [], []
    for gi, d in enumerate(ATTN_DILATIONS):
        qs, ks, vs, lse = sv["streams"][gi]
        d_o = _to_streams(merged_bwd[gi], d)
        cterm = _to_streams(merged_bwd[3 + gi], d)
        dq, dk, dv = attn_bwd(qs, ks, vs, d_o, cterm, lse, gi, f"attn_bwd{gi}")
        dqs.append(_from_streams(dq, d))
        dks.append(_from_streams(dk, d))
        dvs.append(_from_streams(dv, d).astype(BF16))
    dqk, d_qg, d_kg = qk_norm_bwd(p, dqs, dks, w["q_gain"], w["k_gain"], "qk_norm_bwd")
    g["q_norm"] = jnp.sum(d_qg.reshape(N_ATTN_HEADS, HD), axis=0).reshape(1, HD)
    g["k_norm"] = jnp.sum(d_kg.reshape(N_ATTN_HEADS, HD), axis=0).reshape(1, HD)
    dp = [dqk, jnp.concatenate(dvs, axis=1), dz, dxbc, dga, dgs]
    ddt_pad = jnp.pad(ddt.transpose(1, 0, 2).reshape(s, SSD_HEADS), ((0, 0), (0, DT_PAD - SSD_HEADS)))
    if ride_early is not None:
        g["w_in_main"], g["rode_early"] = matmul_tn_pieces(sv["h"], dp, "dw_in", tn=512, ts=1024,
                                                           rider=ride_early(g))
    else:
        g["w_in_main"] = matmul_tn_pieces(sv["h"], dp, "dw_in", tn=512, ts=1024)
    g["w_in_dt"] = matmul_tn(sv["h"], ddt_pad, "dw_in_dt", tn=DT_PAD, ts=1024)
    if ride_late is not None:
        dh_main, g["rode_late"] = matmul_nt_pieces(dp, w["w_in_main"], "d_mix_h", F32, tm=1024, tn=1024, tk=512,
                                                   rider=ride_late(g))
    else:
        dh_main = matmul_nt_pieces(dp, w["w_in_main"], "d_mix_h", F32, tm=1024, tn=512, tk=512)
    dh_dt = matmul_nt(ddt_pad, w["w_in_dt"], "d_mix_h_dt", F32, tm=1024, tn=1024, tk=DT_PAD)
    dx, g["mix_norm"] = rms_bwd([dh_main, dh_dt], x, w["mix_norm"], dx_out, "mix_drms")
    return dx, g


def _place():
    x, y, c = lax.axis_index("x"), lax.axis_index("y"), lax.axis_index("c")
    chips = [(1 - x, y), (x, 1 - y), (1 - x, 1 - y)]
    return x, y, c, 2 * x + y, chips


def _cores():
    c = lax.axis_index("c")
    return jnp.stack([c, 1 - c]).astype(jnp.int32)


def _staged_call(body, *, name, grid, in_specs, out_specs, out_shape, scratch_shapes):
    return pl.pallas_call(
        body, out_shape=out_shape, name=name,
        grid_spec=pltpu.PrefetchScalarGridSpec(num_scalar_prefetch=1, grid=grid, in_specs=in_specs,
                                               out_specs=out_specs, scratch_shapes=scratch_shapes),
        compiler_params=pltpu.CompilerParams(dimension_semantics=("arbitrary",) * len(grid),
                                             vmem_limit_bytes=V7X_VMEM_LIMIT, has_side_effects=True))


def gather_rider(shards, tiles):
    dma = pltpu.SemaphoreType.DMA
    n = len(shards)
    geo = [(a.shape[0] // 2, tm, (a.shape[0] // 2) // tm) for a, tm in zip(shards, tiles)]
    scratch = []
    for a, (h, tm, nk) in zip(shards, geo):
        scratch += [pltpu.VMEM((N_CHIP,) + a.shape, a.dtype), dma((3, nk)), dma((3, nk)), dma((3, nk)), dma((3, nk)),
                    dma((nk + 2,))]

    def copies(j, in_ref, scr):
        buf, send1, recv1, send2, recv2, local = scr[6 * j:6 * j + 6]
        h, tm, nk = geo[j]
        x, y, c, me, chips = _place()
        chip_of = [2 * chips[t][0] + chips[t][1] for t in range(3)]

        def rows(chip, core, k):
            return buf.at[chip, pl.ds(core * h + k * tm, tm)]

        def mine(k):
            if k == nk:
                return pltpu.make_async_copy(in_ref.at[pl.ds((1 - c) * h, h)], buf.at[me, pl.ds((1 - c) * h, h)],
                                             local.at[nk])
            return pltpu.make_async_copy(in_ref.at[pl.ds(c * h + k * tm, tm)], rows(me, c, k), local.at[k])

        def level1(t, k, incoming):
            place = rows(chip_of[t] if incoming else me, c, k)
            return pltpu.make_async_remote_copy(src_ref=place, dst_ref=place, send_sem=send1.at[t, k],
                                                recv_sem=recv1.at[t, k], device_id=(*chips[t], c), device_id_type=MESH)

        def level2(t, k, incoming):
            place = rows(chip_of[t], (1 - c) if incoming else c, k)
            return pltpu.make_async_remote_copy(src_ref=place, dst_ref=place, send_sem=send2.at[t, k],
                                                recv_sem=recv2.at[t, k], device_id=(x, y, 1 - c),
                                                device_id_type=MESH)

        return buf, local, nk, mine, level1, level2

    def start(ins, outs, scr):
        for j in range(n):
            _, _, nk, mine, _, _ = copies(j, ins[j], scr)
            for k in range(nk + 1):
                mine(k).start()
        for j in range(n):
            _, _, nk, mine, level1, _ = copies(j, ins[j], scr)
            for k in range(nk):
                mine(k).wait()
                for t in range(3):
                    level1(t, k, False).start()

    def finish(ins, outs, scr):
        for j in range(n):
            _, _, nk, _, level1, level2 = copies(j, ins[j], scr)
            for k in range(nk):
                for t in range(3):
                    level1(t, k, True).wait_recv()
                    level2(t, k, False).start()
        for j in range(n):
            buf, local, nk, mine, level1, level2 = copies(j, ins[j], scr)
            for k in range(nk):
                for t in range(3):
                    level2(t, k, True).wait_recv()
            for k in range(nk):
                for t in range(3):
                    level1(t, k, False).wait_send()
                    level2(t, k, False).wait_send()
            mine(nk).wait()
            pltpu.make_async_copy(buf, outs[j], local.at[nk + 1]).start()
        for j in range(n):
            buf, local, nk, _, _, _ = copies(j, ins[j], scr)
            pltpu.make_async_copy(buf, outs[j], local.at[nk + 1]).wait()

    return Rider(list(shards), [_sds((N_CHIP,) + a.shape, a.dtype) for a in shards], scratch, start, finish)


def sibling_sum(g, tm, name):
    _, r, cdim = g.shape
    h = r // 2
    ni = h // tm
    dma = pltpu.SemaphoreType.DMA

    def body(cores_ref, keep_ref, give_ref, out_ref, slot, send, recv):
        par = (pl.program_id(0) * ni + pl.program_id(1)) % 2
        x, y, c, _, _ = _place()
        cp = pltpu.make_async_remote_copy(src_ref=give_ref, dst_ref=slot.at[par], send_sem=send.at[par],
                                          recv_sem=recv.at[par], device_id=(x, y, 1 - c), device_id_type=MESH)
        cp.start()
        cp.wait_recv()
        out_ref[...] = (keep_ref[...].astype(F32) + slot[par].astype(F32)).astype(out_ref.dtype)
        cp.wait_send()

    flat = g.reshape(N_CHIP * r, cdim)
    return _staged_call(
        body, name=name, grid=(N_CHIP, ni),
        in_specs=[pl.BlockSpec((tm, cdim), lambda j, i, cores: ((2 * j + cores[0]) * ni + i, 0)),
                  pl.BlockSpec((tm, cdim), lambda j, i, cores: ((2 * j + cores[1]) * ni + i, 0))],
        out_specs=pl.BlockSpec((None, tm, cdim), lambda j, i, cores: (j, i, 0)),
        out_shape=_sds((N_CHIP, h, cdim), g.dtype),
        scratch_shapes=[pltpu.VMEM((2, tm, cdim), g.dtype), dma((2,)), dma((2,))],
    )(_cores(), flat, flat)


def owner_sum_rider(sums, tiles):
    dma = pltpu.SemaphoreType.DMA
    n = len(sums)
    geo = [(a.shape[1], tm, a.shape[1] // tm) for a, tm in zip(sums, tiles)]
    scratch = []
    for a, (h, tm, nk) in zip(sums, geo):
        cdim = a.shape[2]
        scratch += [pltpu.VMEM(a.shape, a.dtype), pltpu.VMEM((3, h, cdim), a.dtype), pltpu.VMEM((2, h, cdim), F32),
                    dma((3, nk)), dma((3, nk)), dma((nk,)), dma((nk,)), dma((2,))]

    def copies(j, scr):
        part, got, res, send, recv, send2, recv2, local = scr[8 * j:8 * j + 8]
        h, tm, nk = geo[j]
        x, y, c, me, chips = _place()

        def to_owner(t, k):
            chip = 2 * chips[t][0] + chips[t][1]
            return pltpu.make_async_remote_copy(
                src_ref=part.at[chip, pl.ds(k * tm, tm)], dst_ref=got.at[t, pl.ds(k * tm, tm)],
                send_sem=send.at[t, k], recv_sem=recv.at[t, k], device_id=(*chips[t], c), device_id_type=MESH)

        def to_sibling(k):
            place = res.at[c, pl.ds(k * tm, tm)]
            return pltpu.make_async_remote_copy(src_ref=place, dst_ref=place, send_sem=send2.at[k],
                                                recv_sem=recv2.at[k], device_id=(x, y, 1 - c), device_id_type=MESH)

        return part, got, res, local, to_owner, to_sibling, (tm, nk, c, me)

    def start(ins, outs, scr):
        for j in range(n):
            part, _, _, local, _, _, _ = copies(j, scr)
            pltpu.make_async_copy(ins[j], part, local.at[0]).start()
        for j in range(n):
            part, _, _, local, to_owner, _, (tm, nk, c, me) = copies(j, scr)
            pltpu.make_async_copy(ins[j], part, local.at[0]).wait()
            for k in range(nk):
                for t in range(3):
                    to_owner(t, k).start()

    def finish(ins, outs, scr):
        for j in range(n):
            part, got, res, _, to_owner, to_sibling, (tm, nk, c, me) = copies(j, scr)
            for k in range(nk):
                rows = pl.ds(k * tm, tm)
                for t in range(3):
                    to_owner(t, k).wait_recv()
                acc = part[me, rows, :].astype(F32)
                for t in range(3):
                    acc = acc + got[t, rows, :].astype(F32)
                res[c, rows, :] = acc
                to_sibling(k).start()
        for j in range(n):
            _, _, res, local, to_owner, to_sibling, (tm, nk, c, me) = copies(j, scr)
            for k in range(nk):
                to_sibling(k).wait_recv()
            for k in range(nk):
                to_sibling(k).wait_send()
                for t in range(3):
                    to_owner(t, k).wait_send()
            pltpu.make_async_copy(res, outs[j], local.at[1]).start()
        for j in range(n):
            _, _, res, local, _, _, _ = copies(j, scr)
            pltpu.make_async_copy(res, outs[j], local.at[1]).wait()

    return Rider(list(sums), [_sds((2, a.shape[1], a.shape[2]), F32) for a in sums], scratch, start, finish)


def gather_conv_w(w):
    def body(in_ref, out_ref, send, recv):
        x, y, c, me, chips = _place()
        out_ref[me] = in_ref[...]
        copies = []
        for t in range(3):
            copies.append(pltpu.make_async_remote_copy(
                src_ref=out_ref.at[me], dst_ref=out_ref.at[me], send_sem=send.at[t], recv_sem=recv.at[t],
                device_id=(*chips[t], c), device_id_type=MESH))
        for cp in copies:
            cp.start()
        for cp in copies:
            cp.wait_recv()
        for cp in copies:
            cp.wait_send()

    dma = pltpu.SemaphoreType.DMA
    vmem = pl.BlockSpec(memory_space=pltpu.VMEM)
    return pl.pallas_call(
        body, out_shape=_sds((N_CHIP,) + w.shape, w.dtype), in_specs=[vmem], out_specs=vmem, name="gather_conv_w",
        scratch_shapes=[dma((3,)), dma((3,))],
        compiler_params=pltpu.CompilerParams(has_side_effects=True))(w)


N_DEV = 8
SMALL_ROWS = 32
SMALL_LANES = 1024


def all_reduce_small(arrays):
    n_arr = len(arrays)
    places = []
    for k, a in enumerate(arrays):
        for ri in range(a.shape[0]):
            for c0 in range(0, a.shape[1], SMALL_LANES):
                places.append((k, ri, c0, min(SMALL_LANES, a.shape[1] - c0), len(places)))
    assert len(places) <= SMALL_ROWS

    def body(*refs):
        ins, outs = refs[:n_arr], refs[n_arr:2 * n_arr]
        buf, send, recv = refs[2 * n_arr:]
        x, y, c, _, _ = _place()
        me = 4 * x + 2 * y + c
        buf[me] = jnp.zeros((SMALL_ROWS, SMALL_LANES), F32)
        for k, ri, c0, width, row in places:
            buf[me, row:row + 1, 0:width] = ins[k][ri:ri + 1, c0:c0 + width]
        copies = []
        for r in range(1, N_DEV):
            px = (1 - x) if r & 4 else x
            py = (1 - y) if r & 2 else y
            pc = (1 - c) if r & 1 else c
            copies.append(pltpu.make_async_remote_copy(
                src_ref=buf.at[me], dst_ref=buf.at[me], send_sem=send.at[r - 1], recv_sem=recv.at[r - 1],
                device_id=(px, py, pc), device_id_type=MESH))
        for cp in copies:
            cp.start()
        for cp in copies:
            cp.wait_recv()
        for cp in copies:
            cp.wait_send()
        acc = buf[0]
        for j in range(1, N_DEV):
            acc = acc + buf[j]
        for k, ri, c0, width, row in places:
            outs[k][ri:ri + 1, c0:c0 + width] = acc[row:row + 1, 0:width]

    dma = pltpu.SemaphoreType.DMA
    vmem = pl.BlockSpec(memory_space=pltpu.VMEM)
    return pl.pallas_call(
        body, out_shape=[_sds(a.shape, F32) for a in arrays], in_specs=[vmem] * n_arr, out_specs=[vmem] * n_arr,
        name="all_reduce_small",
        scratch_shapes=[pltpu.VMEM((N_DEV, SMALL_ROWS, SMALL_LANES), F32), dma((N_DEV - 1,)), dma((N_DEV - 1,))],
        compiler_params=pltpu.CompilerParams(has_side_effects=True))(*arrays)


def _row_tile(rows, limit, multiple):
    return max(t for t in range(multiple, min(rows, limit) + 1, multiple) if rows % t == 0)


def _adamw_math(w, g, m, v):
    c1 = 1.0 - ADAM_B1 ** ADAM_STEP
    c2 = 1.0 - ADAM_B2 ** ADAM_STEP
    m2 = ADAM_B1 * m + (1.0 - ADAM_B1) * g
    v2 = ADAM_B2 * v + (1.0 - ADAM_B2) * (g * g)
    return -ADAM_LR * ((m2 / c1) / (jnp.sqrt(v2 / c2) + ADAM_EPS) + ADAM_WD * w), m2, v2


def adamw(w, g, row_off, m, v, name):
    _, r, c = w.shape
    tm = r if r < 8 else _row_tile(math.gcd(r, row_off) if row_off else r, 128, 8)

    def body(w_ref, g_ref, m_ref, v_ref, go_ref, d_ref, m2_ref, v2_ref):
        gv = g_ref[...]
        go_ref[...] = gv
        d_ref[...], m2_ref[...], v2_ref[...] = _adamw_math(w_ref[...], gv, m_ref[...], v_ref[...])

    blk = pl.BlockSpec((None, tm, c), lambda i: (0, i, 0))
    shp = _sds((1, r, c), F32)
    return _call(body, name=name, out_shape=[shp] * 4, grid=(r // tm,),
                 in_specs=[blk, pl.BlockSpec((tm, c), lambda i: (row_off // tm + i, 0)), blk, blk],
                 out_specs=[blk] * 4)(w, g, m, v)


def adamw_small(ws, gs, ms, vs):
    n = len(ws)

    def body(*refs):
        ins, outs = refs[:4 * n], refs[4 * n:]
        for k in range(n):
            w_ref, g_ref, m_ref, v_ref = (ins[j * n + k] for j in range(4))
            outs[k][...], outs[n + k][...], outs[2 * n + k][...] = _adamw_math(w_ref[...], g_ref[...], m_ref[...],
                                                                               v_ref[...])

    vmem = pl.BlockSpec(memory_space=pltpu.VMEM)
    shapes = [_sds(w.shape, F32) for w in ws] * 3
    res = pl.pallas_call(body, out_shape=shapes, in_specs=[vmem] * (4 * n), out_specs=[vmem] * (3 * n),
                         name="adamw_small")(*ws, *gs, *ms, *vs)
    return res[:n], res[n:2 * n], res[2 * n:]


SMALL = ("ffn1_norm", "mix_norm", "q_norm", "k_norm", "conv_b", "dt_bias", "a_log", "d_skip", "ssd_norm", "ffn2_norm")
WEIGHTS = ("ffn1_norm", "ffn1_w_gate", "ffn1_w_up", "ffn1_w_down", "mix_norm", "w_in", "q_norm", "k_norm", "conv_w",
           "conv_b", "dt_bias", "a_log", "d_skip", "ssd_norm", "w_attn_branch", "w_ssd_branch", "w_out", "ffn2_norm",
           "ffn2_w_gate", "ffn2_w_up", "ffn2_w_down")
CONV_SHARD = SSD_CONV_DIM // N_CHIP
CLASSES = {
    "ffn1_in": (("ffn1_w_gate", 1024), ("ffn1_w_up", 1024)),
    "ffn1_out": (("ffn1_w_down", 704),),
    "mix_in": (("w_in", 1024),),
    "mix_attn": (("w_attn_branch", 512),),
    "late_out": (("ffn2_w_down", 704), ("w_ssd_branch", 512), ("w_out", 256)),
    "ffn2_in": (("ffn2_w_gate", 1024), ("ffn2_w_up", 1024)),
}
CLASS_TILE = {"ffn1_in": 256, "ffn1_out": 176, "mix_attn": 256, "late_out": 368, "ffn2_in": 256,
              "mix_in_top": 128, "mix_in_bottom": 128}
SIBLING_TILE = {"ffn1_in": 1024, "ffn1_out": 352, "mix_attn": 256, "late_out": 736, "ffn2_in": 1024,
                "mix_in_top": 256, "mix_in_bottom": 256}


def _chip_major_cols(a):
    r = a.shape[0]
    return a.reshape(r, N_CHIP, -1).transpose(1, 0, 2)


def _from_chip_major_cols(a):
    return a.transpose(1, 0, 2).reshape(a.shape[1], -1)


def kernel(x, ffn1_norm, ffn1_w_gate, ffn1_w_up, ffn1_w_down, mix_norm, w_in, q_norm, k_norm, conv_w, conv_b, dt_bias, a_log, d_skip, ssd_norm, w_attn_branch, w_ssd_branch, w_out, ffn2_norm, ffn2_w_gate, ffn2_w_up, ffn2_w_down, loss_target, m_ffn1_norm, m_ffn1_w_gate, m_ffn1_w_up, m_ffn1_w_down, m_mix_norm, m_w_in, m_q_norm, m_k_norm, m_conv_w, m_conv_b, m_dt_bias, m_a_log, m_d_skip, m_ssd_norm, m_w_attn_branch, m_w_ssd_branch, m_w_out, m_ffn2_norm, m_ffn2_w_gate, m_ffn2_w_up, m_ffn2_w_down, v_ffn1_norm, v_ffn1_w_gate, v_ffn1_w_up, v_ffn1_w_down, v_mix_norm, v_w_in, v_q_norm, v_k_norm, v_conv_w, v_conv_b, v_dt_bias, v_a_log, v_d_skip, v_ssd_norm, v_w_attn_branch, v_w_ssd_branch, v_w_out, v_ffn2_norm, v_ffn2_w_gate, v_ffn2_w_up, v_ffn2_w_down):
    env = dict(locals())
    wts = {k: env[k] for k in WEIGHTS}
    moms = {k: env["m_" + k] for k in WEIGHTS}
    vars_ = {k: env["v_" + k] for k in WEIGHTS}
    x0 = x[0]
    target = loss_target[0]

    def gather(classes, more=()):
        shards = [jnp.concatenate([wts[k][0] for k, _ in CLASSES[c]], axis=0).astype(BF16) for c in classes]
        return gather_rider(shards + [a for a, _ in more], [CLASS_TILE[c] for c in classes] + [t for _, t in more])

    def reducer(classes, parts):
        sums = [sibling_sum(p, SIBLING_TILE[c], f"sibling_sum_{c}") for c, p in zip(classes, parts)]
        return owner_sum_rider(sums, [CLASS_TILE[c] for c in classes])

    half = D_MODEL // 2
    in_tile = CLASS_TILE["mix_in_top"]
    x1, saved1, (w_ffn1_in,), (w_ffn1_out, w_mix_attn, w_in_top), (w_in_bottom,) = ffn_forward(
        x0, ffn1_norm, lambda rode: rode[0], lambda rode: rode[0], "ffn1", rms_rider=gather(["ffn1_in"]),
        up_rider=gather(["ffn1_out", "mix_attn"], [(w_in[0, :half].astype(BF16), in_tile)]),
        down_rider=gather([], [(w_in[0, half:].astype(BF16), in_tile)]))
    dt0, dt1 = IN_DT0 - 3 * IN_SHARD, IN_DT1 - 3 * IN_SHARD

    def in_columns(w4):
        return jnp.concatenate([w4[0], w4[1], w4[2], w4[3][:, :dt0], w4[3][:, dt1:]], axis=1), w4[3][:, dt0:dt1]

    (main_top, dt_top), (main_bottom, dt_bottom) = in_columns(w_in_top), in_columns(w_in_bottom)
    mixer_w = dict(
        mix_norm=mix_norm,
        w_in_main=jnp.concatenate([main_top, main_bottom], axis=0),
        w_in_dt=jnp.pad(jnp.concatenate([dt_top, dt_bottom], axis=0), ((0, 0), (0, DT_PAD - SSD_HEADS))),
        q_gain=jnp.tile(q_norm, (1, 2)), k_gain=jnp.tile(k_norm, (1, 2)),
        conv_w=_from_chip_major_cols(gather_conv_w(conv_w[0])), conv_b=conv_b, dt_bias=dt_bias, a_log=a_log,
        d_skip=d_skip, ssd_norm=ssd_norm, w_attn_branch=_from_chip_major_cols(w_mix_attn))

    def later_weights(rode):
        late = rode[0]
        return dict(w_ssd_branch=late[:, 704:1216].reshape(SSD_INNER, D_MODEL),
                    w_out=late[:, 1216:1472].reshape(D_MODEL, D_MODEL))

    x2, saved_mix, (w_late_out, w_ffn2_in) = mixer_forward(x1, mixer_w, gather(["late_out", "ffn2_in"]), later_weights)
    (dx3, sq), saved2, _, _, _ = ffn_forward(x2, ffn2_norm, lambda rode: w_ffn2_in, lambda rode: w_late_out, "ffn2",
                                             target=target)

    grads = {}
    dx2, grads["ffn2_norm"], d_ffn2_in, d_ffn2_down, _ = ffn_backward(dx3, x2, ffn2_norm, w_ffn2_in, w_late_out,
                                                                      saved2, "ffn2")

    def ride_early(g):
        late = jnp.concatenate([d_ffn2_down, g["w_ssd_branch"].reshape(N_CHIP, -1, D_MODEL),
                                g["w_out"].reshape(N_CHIP, -1, D_MODEL)], axis=1)
        return reducer(["ffn2_in", "late_out"], [d_ffn2_in, late])

    g_in_rows = {}

    def ride_late(g):
        main = g["w_in_main"]
        last = jnp.concatenate([main[:, 3 * IN_SHARD:IN_DT0], g["w_in_dt"][:, :SSD_HEADS], main[:, IN_DT0:]], axis=1)
        for part, rows in (("top", slice(0, D_MODEL // 2)), ("bottom", slice(D_MODEL // 2, D_MODEL))):
            g_in_rows[part] = jnp.stack([main[rows, j * IN_SHARD:(j + 1) * IN_SHARD] for j in range(3)]
                                        + [last[rows]])
        return reducer(["mix_in_top", "mix_attn"], [g_in_rows["top"], _chip_major_cols(g["w_attn_branch"])])

    dx1, gmix = mixer_backward(dx2, x1, saved_mix, ride_early, ride_late)
    dx0, grads["ffn1_norm"], rode_in, rode_out, rode_hidden = ffn_backward(
        dx1, x0, ffn1_norm, w_ffn1_in, w_ffn1_out, saved1, "ffn1",
        hidden_rider=reducer(["mix_in_bottom"], [g_in_rows["bottom"]]),
        ride_down=lambda d: reducer(["ffn1_out"], [d]), ride_in=lambda d: reducer(["ffn1_in"], [d]))
    for k in ("mix_norm", "q_norm", "k_norm", "conv_b", "dt_bias", "a_log", "d_skip", "ssd_norm"):
        grads[k] = gmix[k]
    reduced = dict(zip(("ffn2_in", "late_out", "mix_in_top", "mix_attn", "ffn1_in", "ffn1_out", "mix_in_bottom"),
                       (*gmix["rode_early"], *gmix["rode_late"], rode_in[0], rode_out[0], rode_hidden[0])))
    reduced = {c: r.reshape(-1, r.shape[2]) for c, r in reduced.items()}
    reduced["mix_in"] = jnp.concatenate([reduced.pop("mix_in_top"), reduced.pop("mix_in_bottom")], axis=0)
    summed = all_reduce_small([grads[k] for k in SMALL]
                              + [gmix["conv_w"], (0.5 * jnp.sum(sq) / D_MODEL).reshape(1, 1)])
    g_small = dict(zip(SMALL, summed))
    loss = summed[-1].reshape(())
    chip = 2 * lax.axis_index("x") + lax.axis_index("y")
    g_conv = lax.dynamic_slice_in_dim(summed[-2], chip * CONV_SHARD, CONV_SHARD, axis=1)

    g_final, delta, new_m, new_v = dict(g_small), {}, {}, {}

    def update(k, g_arr, row_off):
        w, m, v = wts[k], moms[k], vars_[k]
        rows, cols = w.shape[1:]
        if cols % 128:
            res = adamw(jnp.swapaxes(w, 1, 2), g_arr[row_off:row_off + rows].T, 0, jnp.swapaxes(m, 1, 2),
                        jnp.swapaxes(v, 1, 2), f"adamw_{k}")
            res = [jnp.swapaxes(r, 1, 2) for r in res]
        else:
            res = adamw(w, g_arr, row_off, m, v, f"adamw_{k}")
        g_final[k], delta[k], new_m[k], new_v[k] = res

    for cls, members in CLASSES.items():
        off = 0
        for k, rows in members:
            update(k, reduced[cls], off)
            off += rows
    update("conv_w", g_conv, 0)
    small = adamw_small(*([d[k] for k in SMALL] for d in (wts, g_small, moms, vars_)))
    for res, vals in zip((delta, new_m, new_v), small):
        res.update(zip(SMALL, vals))

    return (loss, dx0[None], *[g_final[k] for k in WEIGHTS], *[delta[k] for k in WEIGHTS],
            *[new_m[k] for k in WEIGHTS], *[new_v[k] for k in WEIGHTS])
```

```python
import collections
import functools
import math

import jax
import jax.numpy as jnp
from jax import lax
from jax.experimental import pallas as pl
from jax.experimental.pallas import tpu as pltpu

F32 = jnp.float32
BF16 = jnp.bfloat16
MESH = pl.DeviceIdType.MESH

EPS = 1e-6
D_MODEL = 1024
D_FF = 2816
N_CHIP = 4
FF_SHARD = D_FF // N_CHIP
HD = 64
BLK = 128
ATTN_DILATIONS = (1, 4, 16)
HEADS_PER_PATTERN = 8
N_ATTN_HEADS = 24
ALIBI_MAX_EXP = 8.0
ATTN_QKV = 1536
GROUP_W = 512
SSD_INNER = 2048
SSD_HEADS = 32
SSD_GROUPS = 4
SSD_CONV = 4
SSD_CONV_DIM = 3072
IN_COLS = 11808
IN_DT0, IN_DT1 = 9728, 9760
IN_SHARD = IN_COLS // 4
COL_K, COL_V, COL_Z, COL_XBC, COL_GA, COL_GS, P_COLS = 1536, 3072, 4608, 6656, 9728, 10752, 11776
DT_PAD = 128

ADAM_LR, ADAM_B1, ADAM_B2, ADAM_EPS, ADAM_WD, ADAM_STEP = 0.001, 0.9, 0.999, 1e-08, 0.01, 10

V7X_VMEM_LIMIT = 56 * 1024 * 1024
NEG = -1e30


Rider = collections.namedtuple("Rider", "arrays out_shape scratch start finish")
Rider.__doc__ = """An exchange between devices that rides in a compute kernel: its copies are started in the host's
first grid step and waited for in its last, so they travel while the host computes.  arrays / out_shape: extra HBM
operands and results; scratch: extra scratch; start, finish: f(in_refs, out_refs, scratch_refs)."""


def _call(body, *, name, out_shape, in_specs, out_specs, grid=(), scratch_shapes=(), aliases=None, rider=None):
    params = dict(dimension_semantics=("arbitrary",) * len(grid), vmem_limit_bytes=V7X_VMEM_LIMIT)
    if rider is None:
        return pl.pallas_call(
            body, out_shape=out_shape, grid=grid, in_specs=in_specs, out_specs=out_specs,
            scratch_shapes=scratch_shapes, input_output_aliases=aliases or {}, name=name,
            compiler_params=pltpu.CompilerParams(**params))
    single = not isinstance(out_shape, (list, tuple))
    main_out = [out_shape] if single else list(out_shape)
    main_specs = [out_specs] if single else list(out_specs)
    n_in, n_out, n_scr = len(in_specs), len(main_out), len(scratch_shapes)
    r_in, r_out = len(rider.arrays), len(rider.out_shape)

    def wrapped(*refs):
        ins, refs = refs[:n_in], refs[n_in:]
        r_ins, refs = refs[:r_in], refs[r_in:]
        outs, refs = refs[:n_out], refs[n_out:]
        r_outs, refs = refs[:r_out], refs[r_out:]
        scr, r_scr = refs[:n_scr], refs[n_scr:]
        first = last = None
        for axis, size in enumerate(grid):
            at_start, at_end = pl.program_id(axis) == 0, pl.program_id(axis) == size - 1
            first = at_start if first is None else jnp.logical_and(first, at_start)
            last = at_end if last is None else jnp.logical_and(last, at_end)

        @pl.when(first)
        def _():
            rider.start(r_ins, r_outs, r_scr)

        body(*ins, *outs, *scr)

        @pl.when(last)
        def _():
            rider.finish(r_ins, r_outs, r_scr)

    hbm = pl.BlockSpec(memory_space=pl.ANY)
    call = pl.pallas_call(
        wrapped, out_shape=main_out + list(rider.out_shape), grid=grid, in_specs=list(in_specs) + [hbm] * r_in,
        out_specs=main_specs + [hbm] * r_out, scratch_shapes=list(scratch_shapes) + list(rider.scratch), name=name,
        compiler_params=pltpu.CompilerParams(has_side_effects=True, **params))

    def run(*args):
        res = call(*args, *rider.arrays)
        main = res[:n_out]
        return (main[0] if single else main), res[n_out:]

    return run


def _sds(shape, dtype):
    return jax.ShapeDtypeStruct(tuple(shape), dtype)


def _dot(a, b):
    return jnp.dot(a, b, preferred_element_type=F32)


def _dot_nt(a, b):
    return lax.dot_general(a, b, (((1,), (1,)), ((), ())), preferred_element_type=F32)


def _dot_tn(a, b):
    return lax.dot_general(a, b, (((0,), (0,)), ((), ())), preferred_element_type=F32)


def _dot_hi(a, b):
    return jnp.dot(a, b, preferred_element_type=F32, precision=lax.Precision.HIGHEST)


def _sigmoid(x):
    return 1.0 / (1.0 + jnp.exp(-x))


def _lane_first_half(shape):
    return lax.broadcasted_iota(jnp.int32, shape, len(shape) - 1) < HD


def _rowwise(name, fn, rows, consts, outs, accs=(), tm=512, rider=None):
    n_rows = None
    in_arrays, in_specs = [], []
    for r in rows:
        if isinstance(r, tuple):
            arr, w, cb = r
            spec = pl.BlockSpec((tm, w), functools.partial(lambda i, cb: (i, cb), cb=cb))
        else:
            arr = r
            spec = pl.BlockSpec((tm, arr.shape[1]), lambda i: (i, 0))
        n_rows = arr.shape[0]
        in_arrays.append(arr)
        in_specs.append(spec)
    for c in consts:
        in_arrays.append(c)
        in_specs.append(pl.BlockSpec(c.shape, functools.partial(lambda i, n: (0,) * n, n=c.ndim)))
    out_shape = [_sds(s, d) for s, d in outs] + [_sds(s, d) for s, d in accs]
    out_specs = [pl.BlockSpec((tm, s[1]), lambda i: (i, 0)) for s, _ in outs]
    out_specs += [pl.BlockSpec(s, functools.partial(lambda i, n: (0,) * n, n=len(s))) for s, _ in accs]

    def body(*refs):
        fn(pl.program_id(0), *refs)

    res = _call(body, name=name, out_shape=out_shape, grid=(n_rows // tm,), in_specs=in_specs,
                out_specs=out_specs, rider=rider)(*in_arrays)
    return res


def rms_fwd(x, gain, name, rider=None):
    def fn(i, x_ref, g_ref, h_ref):
        xv = x_ref[...]
        r = lax.rsqrt(jnp.mean(xv * xv, axis=-1, keepdims=True) + EPS)
        h_ref[...] = (xv * r * g_ref[...]).astype(h_ref.dtype)

    res = _rowwise(name, fn, [x], [gain], [(x.shape, BF16)], rider=rider)
    return res[0] if rider is None else (res[0][0], res[1])


def rms_bwd(dhs, x, gain, dx_in, name):
    n = len(dhs)

    def fn(i, *refs):
        dh_refs, (x_ref, dxin_ref, g_ref, dx_ref, dg_ref) = refs[:n], refs[n:]
        dh = dh_refs[0][...]
        for r in dh_refs[1:]:
            dh = dh + r[...]
        xv = x_ref[...]
        r = lax.rsqrt(jnp.mean(xv * xv, axis=-1, keepdims=True) + EPS)
        xn = xv * r
        dxn = dh * g_ref[...]
        dx_ref[...] = dxin_ref[...] + r * (dxn - xn * jnp.mean(dxn * xn, axis=-1, keepdims=True))

        @pl.when(i == 0)
        def _():
            dg_ref[...] = jnp.zeros_like(dg_ref)

        dg_ref[...] += jnp.sum(dh * xn, axis=0, keepdims=True)

    return _rowwise(name, fn, list(dhs) + [x, dx_in], [gain], [(x.shape, F32)], [((1, x.shape[1]), F32)])


def matmul_nn(a, b, name, out_dtype, tm, tn, res=None, scale=1.0, rider=None):
    s, k = a.shape
    n = b.shape[1]

    def body(*refs):
        if res is None:
            a_ref, b_ref, o_ref = refs
            o_ref[...] = _dot(a_ref[...], b_ref[...]).astype(o_ref.dtype)
        else:
            a_ref, b_ref, r_ref, o_ref = refs
            o_ref[...] = (r_ref[...] + scale * _dot(a_ref[...], b_ref[...])).astype(o_ref.dtype)

    in_specs = [pl.BlockSpec((tm, k), lambda i, j: (i, 0)), pl.BlockSpec((k, tn), lambda i, j: (0, j))]
    args = [a, b]
    if res is not None:
        in_specs.append(pl.BlockSpec((tm, tn), lambda i, j: (i, j)))
        args.append(res)
    return _call(body, name=name, out_shape=_sds((s, n), out_dtype), grid=(s // tm, n // tn), in_specs=in_specs,
                 out_specs=pl.BlockSpec((tm, tn), lambda i, j: (i, j)), rider=rider)(*args)


def matmul_nt(a, b, name, out_dtype, tm, tn, tk, rider=None):
    s, k = a.shape
    n = b.shape[0]
    nk = k // tk

    def body(a_ref, b_ref, o_ref, acc_ref):
        kk = pl.program_id(2)

        @pl.when(kk == 0)
        def _():
            acc_ref[...] = jnp.zeros_like(acc_ref)

        acc_ref[...] += _dot_nt(a_ref[...].astype(BF16), b_ref[...])

        @pl.when(kk == nk - 1)
        def _():
            o_ref[...] = acc_ref[...].astype(o_ref.dtype)

    return _call(body, name=name, out_shape=_sds((s, n), out_dtype), grid=(s // tm, n // tn, nk),
                 in_specs=[pl.BlockSpec((tm, tk), lambda i, j, kk: (i, kk)),
                           pl.BlockSpec((tn, tk), lambda i, j, kk: (j, kk))],
                 out_specs=pl.BlockSpec((tm, tn), lambda i, j, kk: (i, j)),
                 scratch_shapes=[pltpu.VMEM((tm, tn), F32)], rider=rider)(a, b)


def matmul_tn(a, b, name, tn, ts, a_scale=None, b_scale=None, rider=None):
    s, m = a.shape
    n = b.shape[1]
    ns = s // ts

    def body(a_ref, b_ref, o_ref, acc_ref):
        ss = pl.program_id(1)

        @pl.when(ss == 0)
        def _():
            acc_ref[...] = jnp.zeros_like(acc_ref)

        av, bv = a_ref[...], b_ref[...]
        if a_scale is not None:
            av = av * a_scale
        if b_scale is not None:
            bv = bv * b_scale
        acc_ref[...] += _dot_tn(av.astype(BF16), bv.astype(BF16))

        @pl.when(ss == ns - 1)
        def _():
            o_ref[...] = acc_ref[...].astype(o_ref.dtype)

    return _call(body, name=name, out_shape=_sds((m, n), BF16), grid=(n // tn, ns),
                 in_specs=[pl.BlockSpec((ts, m), lambda j, ss: (ss, 0)), pl.BlockSpec((ts, tn), lambda j, ss: (ss, j))],
                 out_specs=pl.BlockSpec((m, tn), lambda j, ss: (0, j)),
                 scratch_shapes=[pltpu.VMEM((m, tn), F32)], rider=rider)(a, b)


def _piece_specs(pieces, tile, rows_tile, tile_axis_first):
    specs, ranges, t0 = [], [], 0
    for a in pieces:
        n = a.shape[1] // tile

        def index(*ids, t0=t0, n=n):
            t, r = (ids[0], ids[1]) if tile_axis_first else (ids[2], ids[0])
            on = jnp.logical_and(t >= t0, t < t0 + n)
            return jnp.where(on, r, 0), jnp.clip(t - t0, 0, n - 1)

        specs.append(pl.BlockSpec((rows_tile, tile), index))
        ranges.append((t0, n))
        t0 += n
    return specs, ranges


def matmul_tn_pieces(a, pieces, name, tn, ts, rider=None):
    s, m = a.shape
    ns = s // ts
    specs, ranges = _piece_specs(pieces, tn, ts, True)
    n_total = sum(n for _, n in ranges)

    def body(a_ref, *refs):
        b_refs, o_ref, acc_ref = refs[:len(pieces)], refs[-2], refs[-1]
        j, ss = pl.program_id(0), pl.program_id(1)

        @pl.when(ss == 0)
        def _():
            acc_ref[...] = jnp.zeros_like(acc_ref)

        for b_ref, (t0, n) in zip(b_refs, ranges):
            @pl.when(jnp.logical_and(j >= t0, j < t0 + n))
            def _(b_ref=b_ref):
                acc_ref[...] += _dot_tn(a_ref[...], b_ref[...])

        @pl.when(ss == ns - 1)
        def _():
            o_ref[...] = acc_ref[...].astype(o_ref.dtype)

    return _call(body, name=name, out_shape=_sds((m, n_total * tn), BF16), grid=(n_total, ns),
                 in_specs=[pl.BlockSpec((ts, m), lambda j, ss: (ss, 0))] + specs,
                 out_specs=pl.BlockSpec((m, tn), lambda j, ss: (0, j)),
                 scratch_shapes=[pltpu.VMEM((m, tn), F32)], rider=rider)(a, *pieces)


def matmul_nt_pieces(pieces, b, name, out_dtype, tm, tn, tk, rider=None):
    s = pieces[0].shape[0]
    n = b.shape[0]
    specs, ranges = _piece_specs(pieces, tk, tm, False)
    nk = sum(cnt for _, cnt in ranges)

    def body(*refs):
        a_refs, b_ref, o_ref, acc_ref = refs[:len(pieces)], refs[-3], refs[-2], refs[-1]
        kk = pl.program_id(2)

        @pl.when(kk == 0)
        def _():
            acc_ref[...] = jnp.zeros_like(acc_ref)

        for a_ref, (t0, cnt) in zip(a_refs, ranges):
            @pl.when(jnp.logical_and(kk >= t0, kk < t0 + cnt))
            def _(a_ref=a_ref):
                acc_ref[...] += _dot_nt(a_ref[...], b_ref[...])

        @pl.when(kk == nk - 1)
        def _():
            o_ref[...] = acc_ref[...].astype(o_ref.dtype)

    return _call(body, name=name, out_shape=_sds((s, n), out_dtype), grid=(s // tm, n // tn, nk),
                 in_specs=specs + [pl.BlockSpec((tn, tk), lambda i, j, kk: (j, kk))],
                 out_specs=pl.BlockSpec((tm, tn), lambda i, j, kk: (i, j)),
                 scratch_shapes=[pltpu.VMEM((tm, tn), F32)], rider=rider)(*pieces, b)


def ffn_up(h, w704, gate_blk, up_blk, name, tm=512, rider=None):
    s = h.shape[0]

    def body(h_ref, wg_ref, wu_ref, g_ref, u_ref, a_ref):
        hv = h_ref[...]
        g = _dot(hv, wg_ref[...])
        u = _dot(hv, wu_ref[...])
        g_ref[...] = g.astype(BF16)
        u_ref[...] = u.astype(BF16)
        a_ref[...] = (g * _sigmoid(g) * u).astype(BF16)

    ospec = pl.BlockSpec((None, tm, FF_SHARD), lambda j, i: (j, i, 0))
    shp = _sds((N_CHIP, s, FF_SHARD), BF16)
    return _call(body, name=name, out_shape=[shp, shp, shp], grid=(N_CHIP, s // tm),
                 in_specs=[pl.BlockSpec((tm, D_MODEL), lambda j, i: (i, 0)),
                           pl.BlockSpec((None, D_MODEL, FF_SHARD), lambda j, i: (j, gate_blk, 0)),
                           pl.BlockSpec((None, D_MODEL, FF_SHARD), lambda j, i: (j, up_blk, 0))],
                 out_specs=[ospec, ospec, ospec], rider=rider)(h, w704, w704)


def ffn_down(a, w1024, blk, x, name, tm=512, rider=None, target=None):
    s = x.shape[0]

    def block_out(a_ref, wd_ref, x_ref):
        acc = _dot(a_ref[0], wd_ref[0])
        for j in range(1, N_CHIP):
            acc += _dot(a_ref[j], wd_ref[j])
        return x_ref[...] + 0.5 * acc

    def body(a_ref, wd_ref, x_ref, o_ref):
        o_ref[...] = block_out(a_ref, wd_ref, x_ref)

    def loss_body(a_ref, wd_ref, x_ref, t_ref, dy_ref, sq_ref):
        err = block_out(a_ref, wd_ref, x_ref) - t_ref[...]
        dy_ref[...] = err * (1.0 / D_MODEL)

        @pl.when(pl.program_id(0) == 0)
        def _():
            sq_ref[...] = jnp.zeros_like(sq_ref)

        sq_ref[...] += jnp.sum(err * err, axis=0, keepdims=True)

    rows = pl.BlockSpec((tm, D_MODEL), lambda i: (i, 0))
    in_specs = [pl.BlockSpec((N_CHIP, tm, FF_SHARD), lambda i: (0, i, 0)),
                pl.BlockSpec((N_CHIP, FF_SHARD, D_MODEL), lambda i: (0, blk, 0)), rows]
    if target is None:
        return _call(body, name=name, out_shape=_sds((s, D_MODEL), F32), grid=(s // tm,), in_specs=in_specs,
                     out_specs=rows, rider=rider)(a, w1024, x)
    return _call(loss_body, name=name, out_shape=[_sds((s, D_MODEL), F32), _sds((1, D_MODEL), F32)], grid=(s // tm,),
                 in_specs=in_specs + [rows], out_specs=[rows, pl.BlockSpec((1, D_MODEL), lambda i: (0, 0))],
                 rider=rider)(a, w1024, x, target)


def ffn_bwd_hidden(dx, w1024, blk, g, u, name, tm=1024, rider=None):
    s = dx.shape[0]

    def body(dx_ref, wd_ref, g_ref, u_ref, dg_ref, du_ref):
        dy = (0.5 * dx_ref[...]).astype(BF16)
        da = _dot_nt(dy, wd_ref[...])
        gv = g_ref[...].astype(F32)
        uv = u_ref[...].astype(F32)
        sg = _sigmoid(gv)
        dg_ref[...] = (da * uv * (sg * (1.0 + gv * (1.0 - sg)))).astype(BF16)
        du_ref[...] = (da * gv * sg).astype(BF16)

    hspec = pl.BlockSpec((None, tm, FF_SHARD), lambda j, i: (j, i, 0))
    shp = _sds((N_CHIP, s, FF_SHARD), BF16)
    return _call(body, name=name, out_shape=[shp, shp], grid=(N_CHIP, s // tm),
                 in_specs=[pl.BlockSpec((tm, D_MODEL), lambda j, i: (i, 0)),
                           pl.BlockSpec((None, FF_SHARD, D_MODEL), lambda j, i: (j, blk, 0)), hspec, hspec],
                 out_specs=[hspec, hspec], rider=rider)(dx, w1024, g, u)


def ffn_bwd_input(dg, du, w704, gate_blk, up_blk, x, gain, dy, name, tm=512, rider=None):
    s = dg.shape[1]

    def body(dg_ref, du_ref, wg_ref, wu_ref, x_ref, gain_ref, dy_ref, dx_ref, dgain_ref):
        dh = _dot_nt(dg_ref[0], wg_ref[0]) + _dot_nt(du_ref[0], wu_ref[0])
        for j in range(1, N_CHIP):
            dh += _dot_nt(dg_ref[j], wg_ref[j]) + _dot_nt(du_ref[j], wu_ref[j])
        xv = x_ref[...]
        r = lax.rsqrt(jnp.mean(xv * xv, axis=-1, keepdims=True) + EPS)
        xn = xv * r
        dxn = dh * gain_ref[...]
        dx_ref[...] = dy_ref[...] + r * (dxn - xn * jnp.mean(dxn * xn, axis=-1, keepdims=True))

        @pl.when(pl.program_id(0) == 0)
        def _():
            dgain_ref[...] = jnp.zeros_like(dgain_ref)

        dgain_ref[...] += jnp.sum(dh * xn, axis=0, keepdims=True)

    hspec = pl.BlockSpec((N_CHIP, tm, FF_SHARD), lambda i: (0, i, 0))
    rows = pl.BlockSpec((tm, D_MODEL), lambda i: (i, 0))
    whole = pl.BlockSpec((1, D_MODEL), lambda i: (0, 0))
    return _call(body, name=name, out_shape=[_sds((s, D_MODEL), F32), _sds((1, D_MODEL), F32)], grid=(s // tm,),
                 in_specs=[hspec, hspec,
                           pl.BlockSpec((N_CHIP, D_MODEL, FF_SHARD), lambda i: (0, gate_blk, 0), pl.Buffered(1)),
                           pl.BlockSpec((N_CHIP, D_MODEL, FF_SHARD), lambda i: (0, up_blk, 0), pl.Buffered(1)),
                           rows, whole, rows],
                 out_specs=[rows, whole], rider=rider)(dg, du, w704, w704, x, gain, dy)


def ffn_wgrad_in(h, dgu, name, ts=2048, rider=None):
    s = h.shape[0]
    ns = s // ts

    def body(h_ref, d_ref, o_ref, acc_ref):
        ss = pl.program_id(1)

        @pl.when(ss == 0)
        def _():
            acc_ref[...] = jnp.zeros_like(acc_ref)

        acc_ref[...] += _dot_tn(h_ref[...], d_ref[...])

        @pl.when(ss == ns - 1)
        def _():
            o_ref[...] = acc_ref[...].astype(BF16)

    return _call(body, name=name, out_shape=_sds((N_CHIP, D_MODEL, FF_SHARD), BF16), grid=(N_CHIP, ns),
                 in_specs=[pl.BlockSpec((ts, D_MODEL), lambda j, ss: (ss, 0)),
                           pl.BlockSpec((None, ts, FF_SHARD), lambda j, ss: (j, ss, 0))],
                 out_specs=pl.BlockSpec((None, D_MODEL, FF_SHARD), lambda j, ss: (j, 0, 0)),
                 scratch_shapes=[pltpu.VMEM((D_MODEL, FF_SHARD), F32)], rider=rider)(h, dgu)


def ffn_wgrad_down(a, dx, name, ts=2048):
    s = dx.shape[0]
    ns = s // ts

    def body(a_ref, dx_ref, o_ref, acc_ref):
        ss = pl.program_id(1)

        @pl.when(ss == 0)
        def _():
            acc_ref[...] = jnp.zeros_like(acc_ref)

        acc_ref[...] += _dot_tn(a_ref[...], (0.5 * dx_ref[...]).astype(BF16))

        @pl.when(ss == ns - 1)
        def _():
            o_ref[...] = acc_ref[...].astype(BF16)

    return _call(body, name=name, out_shape=_sds((N_CHIP, FF_SHARD, D_MODEL), BF16), grid=(N_CHIP, ns),
                 in_specs=[pl.BlockSpec((None, ts, FF_SHARD), lambda j, ss: (j, ss, 0)),
                           pl.BlockSpec((ts, D_MODEL), lambda j, ss: (ss, 0))],
                 out_specs=pl.BlockSpec((None, FF_SHARD, D_MODEL), lambda j, ss: (j, 0, 0)),
                 scratch_shapes=[pltpu.VMEM((FF_SHARD, D_MODEL), F32)])(a, dx)


def ffn_forward(x, gain, get_w704, get_w1024, tag, rms_rider=None, up_rider=None, down_rider=None, target=None):
    h = rms_fwd(x, gain, f"{tag}_rms", rider=rms_rider)
    h, rode_rms = h if rms_rider is not None else (h, None)
    res = ffn_up(h, get_w704(rode_rms), 0, 1, f"{tag}_up", tm=1024 if up_rider is None else 512, rider=up_rider)
    (g, u, a), rode_up = res if up_rider is not None else (res, None)
    y = ffn_down(a, get_w1024(rode_up), 0, x, f"{tag}_down", rider=down_rider, target=target)
    y, rode_down = y if down_rider is not None else (y, None)
    return y, (h, g, u, a), rode_rms, rode_up, rode_down


def ffn_backward(dy, x, gain, w704, w1024, saved, tag, hidden_rider=None, ride_down=None, ride_in=None):
    h, g, u, a = saved
    d_wd = ffn_wgrad_down(a, dy, f"{tag}_dwd")
    rode_hidden = None
    if hidden_rider is not None:
        (dg, du), rode_hidden = ffn_bwd_hidden(dy, w1024, 0, g, u, f"{tag}_dhid", tm=512, rider=hidden_rider)
    else:
        dg, du = ffn_bwd_hidden(dy, w1024, 0, g, u, f"{tag}_dhid")
    if ride_down is not None:
        d_wg, d_wd = ffn_wgrad_in(h, dg, f"{tag}_dwg", rider=ride_down(d_wd))
    else:
        d_wg = ffn_wgrad_in(h, dg, f"{tag}_dwg")
    d_win = jnp.concatenate([d_wg, ffn_wgrad_in(h, du, f"{tag}_dwu")], axis=1)
    if ride_in is not None:
        (dx, d_gain), d_win = ffn_bwd_input(dg, du, w704, 0, 1, x, gain, dy, f"{tag}_dh", tm=256,
                                            rider=ride_in(d_win))
    else:
        dx, d_gain = ffn_bwd_input(dg, du, w704, 0, 1, x, gain, dy, f"{tag}_dh")
    return dx, d_gain, d_win, d_wd, rode_hidden


def _alibi_slope(head):
    return float(2.0 ** (-ALIBI_MAX_EXP * (head + 1) / N_ATTN_HEADS))


def _same_head():
    row = lax.broadcasted_iota(jnp.int32, (2 * HD, 2 * HD), 0)
    col = lax.broadcasted_iota(jnp.int32, (2 * HD, 2 * HD), 1)
    return ((row < HD) == (col < HD)).astype(BF16)


def _head_sums(x, same_head):
    hi = x.astype(BF16)
    lo = (x - hi.astype(F32)).astype(BF16)
    return _dot(hi, same_head) + _dot(lo, same_head)


def _head_norm(t, gain_pair, same_head):
    r = lax.rsqrt(_head_sums(t * t, same_head) * (1.0 / HD) + EPS)
    return t * r * gain_pair, r


def qk_norm_fwd(p, q_gain, k_gain, name):
    s = p.shape[0]

    def fn(i, q_ref, k_ref, qg_ref, kg_ref, qn_ref, kn_ref):
        same_head = _same_head()
        for src, g_ref, dst in ((q_ref, qg_ref, qn_ref), (k_ref, kg_ref, kn_ref)):
            for pr in range(ATTN_QKV // (2 * HD)):
                cols = slice(pr * 2 * HD, (pr + 1) * 2 * HD)
                y, _ = _head_norm(src[:, cols].astype(F32), g_ref[...], same_head)
                dst[:, cols] = y.astype(BF16)

    return _rowwise(name, fn, [(p, ATTN_QKV, 0), (p, ATTN_QKV, 1)], [q_gain, k_gain],
                    [((s, ATTN_QKV), BF16), ((s, ATTN_QKV), BF16)])


def qk_norm_bwd(p, dqs, dks, q_gain, k_gain, name):
    s = p.shape[0]
    pairs_per_pattern = GROUP_W // (2 * HD)

    def fn(i, q_ref, k_ref, dq0, dq1, dq2, dk0, dk1, dk2, qg_ref, kg_ref, dqk_ref, dqg_ref, dkg_ref):
        same_head = _same_head()

        @pl.when(i == 0)
        def _():
            dqg_ref[...] = jnp.zeros_like(dqg_ref)
            dkg_ref[...] = jnp.zeros_like(dkg_ref)

        for src, d_refs, g_ref, dst, dg_ref in (
                (q_ref, (dq0, dq1, dq2), qg_ref, dqk_ref.at[:, 0:ATTN_QKV], dqg_ref),
                (k_ref, (dk0, dk1, dk2), kg_ref, dqk_ref.at[:, ATTN_QKV:2 * ATTN_QKV], dkg_ref)):
            for pr in range(ATTN_QKV // (2 * HD)):
                cols = slice(pr * 2 * HD, (pr + 1) * 2 * HD)
                t = src[:, cols].astype(F32)
                r = lax.rsqrt(_head_sums(t * t, same_head) * (1.0 / HD) + EPS)
                xn = t * r
                within = (pr % pairs_per_pattern) * 2 * HD
                dy = d_refs[pr // pairs_per_pattern][:, within:within + 2 * HD]
                dg_ref[:, cols] += jnp.sum(dy * xn, axis=0, keepdims=True)
                dxn = dy * g_ref[...]
                mean = _head_sums(dxn * xn, same_head) * (1.0 / HD)
                dst[:, cols] = (r * (dxn - xn * mean)).astype(BF16)

    return _rowwise(name, fn, [(p, ATTN_QKV, 0), (p, ATTN_QKV, 1)] + list(dqs) + list(dks), [q_gain, k_gain],
                    [((s, 2 * ATTN_QKV), BF16)], [((1, ATTN_QKV), F32), ((1, ATTN_QKV), F32)])


def _to_streams(a, d):
    if d == 1:
        return a
    s, c = a.shape
    return a.reshape(s // d, d, c).transpose(1, 0, 2).reshape(s, c)


def _from_streams(a, d):
    if d == 1:
        return a
    s, c = a.shape
    return a.reshape(d, s // d, c).transpose(1, 0, 2).reshape(s, c)


def attn_merge_fwd(os_, lses, name):
    s = os_[0].shape[0]

    def fn(i, o0, o1, o2, l0, l1, l2, out_ref):
        m = jnp.maximum(jnp.maximum(l0[...], l1[...]), l2[...])
        e0, e1, e2 = jnp.exp(l0[...] - m), jnp.exp(l1[...] - m), jnp.exp(l2[...] - m)
        inv = 1.0 / (e0 + e1 + e2)
        out_ref[...] = ((e0 * inv) * o0[...] + (e1 * inv) * o1[...] + (e2 * inv) * o2[...]).astype(BF16)

    return _rowwise(name, fn, list(os_) + list(lses), [], [((s, GROUP_W), BF16)])[0]


def attn_merge_bwd(d_out, os_, lses, name):
    s = d_out.shape[0]

    def fn(i, do_ref, o0, o1, o2, l0, l1, l2, d0, d1, d2, c0, c1, c2):
        m = jnp.maximum(jnp.maximum(l0[...], l1[...]), l2[...])
        e0, e1, e2 = jnp.exp(l0[...] - m), jnp.exp(l1[...] - m), jnp.exp(l2[...] - m)
        inv = 1.0 / (e0 + e1 + e2)
        w0, w1, w2 = e0 * inv, e1 * inv, e2 * inv
        do = do_ref[...]
        prod = do * (w0 * o0[...] + w1 * o1[...] + w2 * o2[...])
        same_head = _same_head()
        for pr in range(GROUP_W // (2 * HD)):
            cols = slice(pr * 2 * HD, (pr + 1) * 2 * HD)
            t = _head_sums(prod[:, cols], same_head)
            for w, c_ref in ((w0, c0), (w1, c1), (w2, c2)):
                c_ref[:, cols] = w[:, cols] * t
        for w, d_ref in ((w0, d0), (w1, d1), (w2, d2)):
            d_ref[...] = (w * do).astype(BF16)

    shp = (s, GROUP_W)
    return _rowwise(name, fn, [d_out] + list(os_) + list(lses), [],
                    [(shp, BF16)] * 3 + [(shp, F32)] * 3)


def _band_constants(d):
    row = lax.broadcasted_iota(jnp.int32, (2 * BLK, 2 * BLK), 0)
    col = lax.broadcasted_iota(jnp.int32, (2 * BLK, 2 * BLK), 1)
    rel = BLK + jnp.where(row >= BLK, row - BLK, row) - col
    band = jnp.logical_and(rel >= 0, rel <= BLK)
    return (rel * d).astype(F32), band, (col >= BLK).astype(jnp.int32)


def _stack_heads(x, first):
    zero = jnp.zeros_like(x)
    return jnp.concatenate([jnp.where(first, x, zero), jnp.where(first, zero, x)], axis=0)


def _unstack_heads(x2, first):
    return jnp.where(first, x2[:BLK], x2[BLK:])


def _head_column(x):
    return jnp.concatenate([x[:, 0:1], x[:, HD:HD + 1]], axis=0)


def attn_fwd(q, k, v, pattern, name, tq=1024):
    s = q.shape[0]
    d = ATTN_DILATIONS[pattern]
    blocks_per_stream = (s // d) // BLK
    nsb = tq // BLK

    def body(q_ref, k_ref, v_ref, kp_ref, vp_ref, o_ref, l_ref):
        i = pl.program_id(0)
        rel_f, band, own = _band_constants(d)
        first = _lane_first_half((BLK, 2 * HD))
        upper = lax.broadcasted_iota(jnp.int32, (2 * BLK, 1), 0) < BLK
        for sb in range(nsb):
            rows = slice(sb * BLK, (sb + 1) * BLK)
            has_prev = ((i * nsb + sb) % blocks_per_stream != 0).astype(jnp.int32)
            mask = jnp.logical_and(band, (own + has_prev) > 0)
            for pr in range(GROUP_W // (2 * HD)):
                cols = slice(pr * 2 * HD, (pr + 1) * 2 * HD)
                if sb == 0:
                    kcat = jnp.concatenate([kp_ref[:, cols], k_ref[rows, cols]], axis=0)
                    vcat = jnp.concatenate([vp_ref[:, cols], v_ref[rows, cols]], axis=0)
                else:
                    both = slice((sb - 1) * BLK, (sb + 1) * BLK)
                    kcat, vcat = k_ref[both, cols], v_ref[both, cols]
                h0 = pattern * HEADS_PER_PATTERN + 2 * pr
                slope = jnp.where(upper, _alibi_slope(h0), _alibi_slope(h0 + 1))
                sc = _dot_nt(_stack_heads(q_ref[rows, cols], first), kcat) * 0.125 - slope * rel_f
                sc = jnp.where(mask, sc, NEG)
                m = jnp.max(sc, axis=-1, keepdims=True)
                p = jnp.exp(sc - m)
                l = jnp.sum(p, axis=-1, keepdims=True)
                o2 = _dot((p * (1.0 / l)).astype(BF16), vcat)
                o_ref[rows, cols] = _unstack_heads(o2, first)
                lse = m + jnp.log(l)
                l_ref[rows, cols] = jnp.where(first, lse[:BLK], lse[BLK:])

    cur = pl.BlockSpec((tq, GROUP_W), lambda i: (i, 0))
    prev = pl.BlockSpec((BLK, GROUP_W), lambda i: (jnp.maximum(i * nsb - 1, 0), 0))
    return _call(body, name=name, out_shape=[_sds((s, GROUP_W), F32), _sds((s, GROUP_W), F32)], grid=(s // tq,),
                 in_specs=[cur, cur, cur, prev, prev], out_specs=[cur, cur])(q, k, v, k, v)


def attn_bwd(q, k, v, d_o, cterm, lse, pattern, name, tq=1024):
    s = q.shape[0]
    d = ATTN_DILATIONS[pattern]
    blocks_per_stream = (s // d) // BLK
    nsb = tq // BLK
    n_blocks = s // BLK

    def body(q_ref, k_ref, v_ref, do_ref, c_ref, l_ref, kp_ref, vp_ref, qn_ref, kn_ref, vn_ref, don_ref, cn_ref,
             ln_ref, dq_ref, dk_ref, dv_ref):
        i = pl.program_id(0)
        rel_f, band, own = _band_constants(d)
        first = _lane_first_half((BLK, 2 * HD))
        upper = lax.broadcasted_iota(jnp.int32, (2 * BLK, 1), 0) < BLK
        dk_ref[...] = jnp.zeros_like(dk_ref)
        dv_ref[...] = jnp.zeros_like(dv_ref)
        for sb in range(nsb + 1):
            gb = i * nsb + sb
            rows = slice(sb * BLK, (sb + 1) * BLK)
            before = slice((sb - 1) * BLK, sb * BLK)
            inside = (gb < n_blocks).astype(jnp.int32)
            has_prev = jnp.logical_and(gb % blocks_per_stream != 0, gb < n_blocks).astype(jnp.int32)
            mask = jnp.logical_and(band, (own * inside + has_prev) > 0)
            for pr in range(GROUP_W // (2 * HD)):
                cols = slice(pr * 2 * HD, (pr + 1) * 2 * HD)
                if sb == 0:
                    kcat = jnp.concatenate([kp_ref[:, cols], k_ref[rows, cols]], axis=0)
                    vcat = jnp.concatenate([vp_ref[:, cols], v_ref[rows, cols]], axis=0)
                elif sb == nsb:
                    kcat = jnp.concatenate([k_ref[before, cols], kn_ref[:, cols]], axis=0)
                    vcat = jnp.concatenate([v_ref[before, cols], vn_ref[:, cols]], axis=0)
                else:
                    both = slice((sb - 1) * BLK, (sb + 1) * BLK)
                    kcat, vcat = k_ref[both, cols], v_ref[both, cols]
                if sb < nsb:
                    qp, dop, cp, lp = q_ref[rows, cols], do_ref[rows, cols], c_ref[rows, cols], l_ref[rows, cols]
                else:
                    qp, dop, cp, lp = qn_ref[:, cols], don_ref[:, cols], cn_ref[:, cols], ln_ref[:, cols]
                h0 = pattern * HEADS_PER_PATTERN + 2 * pr
                slope = jnp.where(upper, _alibi_slope(h0), _alibi_slope(h0 + 1))
                q2 = _stack_heads(qp, first)
                do2 = _stack_heads(dop, first)
                sc = jnp.where(mask, _dot_nt(q2, kcat) * 0.125 - slope * rel_f, NEG)
                pm = jnp.exp(sc - _head_column(lp))
                dl = (pm * (_dot_nt(do2, vcat) - _head_column(cp))).astype(BF16)
                if sb < nsb:
                    dq_ref[rows, cols] = _unstack_heads(_dot(dl, kcat), first) * 0.125
                dk2 = _dot_tn(dl, q2) * 0.125
                dv2 = _dot_tn(pm.astype(BF16), do2)
                if sb > 0:
                    dk_ref[before, cols] += dk2[:BLK]
                    dv_ref[before, cols] += dv2[:BLK]
                if sb < nsb:
                    dk_ref[rows, cols] += dk2[BLK:]
                    dv_ref[rows, cols] += dv2[BLK:]

    cur = pl.BlockSpec((tq, GROUP_W), lambda i: (i, 0))
    prev = pl.BlockSpec((BLK, GROUP_W), lambda i: (jnp.maximum(i * nsb - 1, 0), 0))
    nxt = pl.BlockSpec((BLK, GROUP_W), lambda i: (jnp.minimum((i + 1) * nsb, n_blocks - 1), 0))
    shp = _sds((s, GROUP_W), F32)
    return _call(body, name=name, out_shape=[shp, shp, shp], grid=(s // tq,),
                 in_specs=[cur] * 6 + [prev, prev] + [nxt] * 6, out_specs=[cur, cur, cur])(
                     q, k, v, d_o, cterm, lse, k, v, q, k, v, d_o, cterm, lse)


HALO = 16
CONV_TQ = 1024


def conv_fwd(p, w, b, name):
    s = p.shape[0]
    tq = CONV_TQ
    ncol = SSD_CONV_DIM // GROUP_W
    cb0 = COL_XBC // GROUP_W

    def body(u_ref, up_ref, w_ref, b_ref, c_ref, xc_ref):
        i = pl.program_id(0)
        prev = (up_ref[...].astype(F32) * (i > 0).astype(F32)).astype(BF16)
        ext = jnp.concatenate([prev, u_ref[...]], axis=0)
        row = lax.broadcasted_iota(jnp.int32, (BLK, BLK + HALO), 0)
        col = lax.broadcasted_iota(jnp.int32, (BLK, BLK + HALO), 1)
        for blk in range(tq // BLK):
            lead = ext[blk * BLK:blk * BLK + BLK + HALO]
            acc = b_ref[...] + w_ref[SSD_CONV - 1:SSD_CONV, :] * lead[HALO:].astype(F32)
            for kk in range(SSD_CONV - 1):
                pick = (col == row + HALO - (SSD_CONV - 1 - kk)).astype(BF16)
                acc += w_ref[kk:kk + 1, :] * _dot(pick, lead)
            rows = slice(blk * BLK, (blk + 1) * BLK)
            c_ref[rows, :] = acc.astype(BF16)
            xc_ref[rows, :] = (acc * _sigmoid(acc)).astype(BF16)

    cur_in = pl.BlockSpec((tq, GROUP_W), lambda i, j: (i, cb0 + j))
    prev_in = pl.BlockSpec((HALO, GROUP_W), lambda i, j: (jnp.maximum(i * (tq // HALO) - 1, 0), cb0 + j))
    cur_out = pl.BlockSpec((tq, GROUP_W), lambda i, j: (i, j))
    shp = _sds((s, SSD_CONV_DIM), BF16)
    return _call(body, name=name, out_shape=[shp, shp], grid=(s // tq, ncol),
                 in_specs=[cur_in, prev_in, pl.BlockSpec((SSD_CONV, GROUP_W), lambda i, j: (0, j)),
                           pl.BlockSpec((1, GROUP_W), lambda i, j: (0, j))],
                 out_specs=[cur_out, cur_out])(p, p, w, b)


def conv_bwd(p, cpre, dxs, d_b, d_c, w, name):
    s = p.shape[0]
    tq = CONV_TQ
    ncol = SSD_CONV_DIM // GROUP_W
    n_xs = SSD_INNER // GROUP_W
    cb0 = COL_XBC // GROUP_W
    nt = s // tq

    def body(u_ref, c_ref, cn_ref, dx_ref, dxn_ref, dbm_ref, dbmn_ref, dcm_ref, dcmn_ref, w_ref,
             du_ref, dw_ref, db_ref):
        j, i = pl.program_id(0), pl.program_id(1)

        def dpre(c16, dx):
            c = c16.astype(F32)
            sg = _sigmoid(c)
            return dx * (sg * (1.0 + c * (1.0 - sg)))

        def pick(a_ref, b_ref, c_ref_):
            return jnp.where(j < n_xs, a_ref[...], jnp.where(j == n_xs, b_ref[...], c_ref_[...]))

        dc = dpre(c_ref[...], pick(dx_ref, dbm_ref, dcm_ref))
        dcn = dpre(cn_ref[...], pick(dxn_ref, dbmn_ref, dcmn_ref)) * (i < nt - 1).astype(F32)
        dext = jnp.concatenate([dc, dcn], axis=0)
        u = u_ref[...].astype(F32)

        @pl.when(i == 0)
        def _():
            dw_ref[...] = jnp.zeros_like(dw_ref)
            db_ref[...] = jnp.zeros_like(db_ref)

        du = w_ref[SSD_CONV - 1:SSD_CONV, :] * dc
        dw_ref[SSD_CONV - 1:SSD_CONV, :] += jnp.sum(dc * u, axis=0, keepdims=True)
        for kk in range(SSD_CONV - 1):
            sh = SSD_CONV - 1 - kk
            ahead = pltpu.roll(dext, tq + HALO - sh, 0)[0:tq]
            du += w_ref[kk:kk + 1, :] * ahead
            dw_ref[kk:kk + 1, :] += jnp.sum(ahead * u, axis=0, keepdims=True)
        du_ref[...] = du.astype(BF16)
        db_ref[...] += jnp.sum(dc, axis=0, keepdims=True)

    hb = tq // HALO
    cur_p = pl.BlockSpec((tq, GROUP_W), lambda j, i: (i, cb0 + j))
    cur = pl.BlockSpec((tq, GROUP_W), lambda j, i: (i, j))
    nxt = pl.BlockSpec((HALO, GROUP_W), lambda j, i: (jnp.minimum((i + 1) * hb, s // HALO - 1), j))

    def piece(first_tile, n_tiles):
        def on(j):
            return jnp.logical_and(j >= first_tile, j < first_tile + n_tiles)

        def col(j):
            return jnp.clip(j - first_tile, 0, n_tiles - 1)

        return (pl.BlockSpec((tq, GROUP_W), lambda j, i: (jnp.where(on(j), i, 0), col(j))),
                pl.BlockSpec((HALO, GROUP_W),
                             lambda j, i: (jnp.where(on(j), jnp.minimum((i + 1) * hb, s // HALO - 1), 0), col(j))))

    return _call(body, name=name,
                 out_shape=[_sds((s, SSD_CONV_DIM), BF16), _sds((8, SSD_CONV_DIM), F32), _sds((1, SSD_CONV_DIM), F32)],
                 grid=(ncol, nt),
                 in_specs=[cur_p, cur, nxt, *piece(0, n_xs), *piece(n_xs, 1), *piece(n_xs + 1, 1),
                           pl.BlockSpec((SSD_CONV, GROUP_W), lambda j, i: (0, j))],
                 out_specs=[cur, pl.BlockSpec((8, GROUP_W), lambda j, i: (0, j)),
                            pl.BlockSpec((1, GROUP_W), lambda j, i: (0, j))])(
                                p, cpre, cpre, dxs, dxs, d_b, d_b, d_c, d_c, w)


def _softplus(x):
    return jnp.maximum(x, 0.0) + jnp.log(1.0 + jnp.exp(-jnp.abs(x)))


def _ssd_decays(dtr_ref, dtrt_ref, bias_ref, biast_ref, alog_ref, alogt_ref):
    row = lax.broadcasted_iota(jnp.int32, (BLK, BLK), 0)
    col = lax.broadcasted_iota(jnp.int32, (BLK, BLK), 1)
    lower = (row >= col).astype(F32)
    upper = (row <= col).astype(F32)
    dtb = dtr_ref[...] + bias_ref[...]
    dt = _softplus(dtb)
    a = dt * (-jnp.exp(alog_ref[...]))
    cs = _dot_hi(lower, a)
    a_t = _softplus(dtrt_ref[...] + biast_ref[...]) * (-jnp.exp(alogt_ref[...]))
    cs_t = _dot_hi(a_t, upper)
    return dtb, dt, cs, cs_t, row, col, upper


SSD_GROUPS_PER_STEP = 4


def _per_group(body, gps, kinds):
    def wrapped(*refs):
        for gi in range(gps):
            args, pos = [], 0
            for kind, n in kinds:
                if kind == "each":
                    args.append(refs[pos + gi])
                    pos += gps
                    continue
                ref = refs[pos]
                pos += 1
                if kind == "cols":
                    args.append(ref.at[:, gi * n:(gi + 1) * n])
                else:
                    args.append(ref.at[gi] if n == 1 else ref.at[pl.ds(gi * n, n)])
            body(*args)

    return wrapped


def ssd_fwd(p, xc, dtg, dtg_t, params, gn, name):
    s = p.shape[0]
    nc = s // BLK
    bias, bias_t, alog, alog_t, dskip = params

    def body(xs_ref, b_ref, c_ref, z_ref, dtr_ref, dtrt_ref, bias_ref, biast_ref, alog_ref, alogt_ref, dsk_ref,
             gn_ref, y_ref, sin_ref, hp_ref, h_ref):
        c_idx = pl.program_id(1)

        @pl.when(c_idx == 0)
        def _():
            h_ref[...] = jnp.zeros_like(h_ref)

        _, dt, cs, cs_t, row, col, _ = _ssd_decays(dtr_ref, dtrt_ref, bias_ref, biast_ref, alog_ref, alogt_ref)
        first = _lane_first_half((BLK, 2 * HD))
        first_row = _lane_first_half((1, 2 * HD))
        tril = row >= col
        b16, c16 = b_ref[...], c_ref[...]
        cb = _dot_nt(c16, b16)
        n_pairs = GROUP_W // (2 * HD)
        tot = cs[BLK - 1:BLK, :]
        exp_cs, exp_rest, exp_tot = jnp.exp(cs), jnp.exp(tot - cs), jnp.exp(tot)

        lanes_of_head = (lax.broadcasted_iota(jnp.int32, (8, GROUP_W), 1) // HD
                         == lax.broadcasted_iota(jnp.int32, (8, GROUP_W), 0)).astype(BF16)

        def per_head(v, mask):
            if v.shape[0] == 1:
                return jnp.concatenate([jnp.where(mask, v[:, 2 * pr:2 * pr + 1], v[:, 2 * pr + 1:2 * pr + 2])
                                        for pr in range(n_pairs)], axis=1)
            hi = v.astype(BF16)
            lo = (v - hi.astype(F32)).astype(BF16)
            return _dot(hi, lanes_of_head) + _dot(lo, lanes_of_head)

        xs = xs_ref[...].astype(F32)
        xt = xs * per_head(dt, first)
        xt16 = xt.astype(BF16)
        hstate = jnp.concatenate([h_ref[pr] for pr in range(n_pairs)], axis=1)
        for pr in range(n_pairs):
            hp_ref[pr] = h_ref[pr]
        y_off = per_head(exp_cs, first) * _dot(c16, hstate.astype(BF16))
        new = per_head(exp_tot, first_row) * hstate + _dot_tn(b16, (per_head(exp_rest, first) * xt).astype(BF16))
        for pr in range(n_pairs):
            h_ref[pr] = new[:, pr * 2 * HD:(pr + 1) * 2 * HD]
        y_diag = []
        for pr in range(n_pairs):
            cols = slice(pr * 2 * HD, (pr + 1) * 2 * HD)
            m2 = jnp.concatenate(
                [(cb * jnp.exp(jnp.where(tril, cs[:, h:h + 1] - cs_t[h:h + 1, :], NEG))).astype(BF16)
                 for h in (2 * pr, 2 * pr + 1)], axis=1)
            y_diag.append(_dot(m2, _stack_heads(xt16[:, cols], first)))
        y = jnp.concatenate(y_diag, axis=1) + y_off + xs * per_head(dsk_ref[...], first_row)
        y_ref[...] = y
        zv = z_ref[...].astype(F32)
        yz = y * (zv * _sigmoid(zv))
        r = lax.rsqrt(jnp.mean(yz * yz, axis=-1, keepdims=True) + EPS)
        sin_ref[...] = (yz * r * gn_ref[...]).astype(BF16)

    gps = SSD_GROUPS_PER_STEP
    wide, narrow, lead = ("cols", GROUP_W), ("cols", BLK), ("lead", 1)
    kinds = [wide, narrow, narrow, ("each", 0)] + [lead] * 7 + [wide, wide, wide, lead, ("lead", 4)]
    wide_w, narrow_w = GROUP_W * gps, BLK * gps
    gparam = pl.BlockSpec((gps, 1, 8), lambda g, c: (g, 0, 0))
    gparam_t = pl.BlockSpec((gps, 8, 1), lambda g, c: (g, 0, 0))
    z_specs = [pl.BlockSpec((BLK, GROUP_W), functools.partial(lambda g, c, gi: (c, COL_Z // GROUP_W + gps * g + gi),
                                                              gi=gi)) for gi in range(gps)]
    return _call(
        _per_group(body, gps, kinds), name=name,
        out_shape=[_sds((s, SSD_INNER), F32), _sds((s, SSD_INNER), BF16),
                   _sds((SSD_GROUPS, nc, 4, BLK, 2 * HD), F32)],
        grid=(SSD_GROUPS // gps, nc),
        in_specs=[pl.BlockSpec((BLK, wide_w), lambda g, c: (c, g)),
                  pl.BlockSpec((BLK, narrow_w), lambda g, c: (c, SSD_INNER // narrow_w + g)),
                  pl.BlockSpec((BLK, narrow_w), lambda g, c: (c, (SSD_INNER + SSD_GROUPS * BLK) // narrow_w + g)),
                  *z_specs,
                  pl.BlockSpec((gps, BLK, 8), lambda g, c: (g, c, 0)),
                  pl.BlockSpec((gps, 8, BLK), lambda g, c: (g, 0, c)),
                  gparam, gparam_t, gparam, gparam_t, gparam,
                  pl.BlockSpec((1, wide_w), lambda g, c: (0, g))],
        out_specs=[pl.BlockSpec((BLK, wide_w), lambda g, c: (c, g)),
                   pl.BlockSpec((BLK, wide_w), lambda g, c: (c, g)),
                   pl.BlockSpec((gps, None, 4, BLK, 2 * HD), lambda g, c: (g, c, 0, 0, 0))],
        scratch_shapes=[pltpu.VMEM((4 * gps, BLK, 2 * HD), F32)],
    )(xc, xc, xc, *([p] * gps), dtg, dtg_t, bias, bias_t, alog, alog_t, dskip, gn)


def ssd_bwd(p, xc, y, d_sin, hprev, dtg, dtg_t, params, gn, name):
    s = p.shape[0]
    nc = s // BLK
    bias, bias_t, alog, alog_t, dskip = params

    def body(xs_ref, b_ref, c_ref, z_ref, y_ref, dsin_ref, hp_ref, dtr_ref, dtrt_ref, bias_ref,
             biast_ref, alog_ref, alogt_ref, dsk_ref, gn_ref,
             dxs_ref, db_ref, dc_ref, dz_ref, ddt_ref, da_ref, dbias_ref, ddsk_ref, dgn_ref, dh_ref):
        c_idx = pl.program_id(1)

        @pl.when(c_idx == 0)
        def _():
            dh_ref[...] = jnp.zeros_like(dh_ref)
            da_ref[...] = jnp.zeros_like(da_ref)
            dbias_ref[...] = jnp.zeros_like(dbias_ref)
            ddsk_ref[...] = jnp.zeros_like(ddsk_ref)
            dgn_ref[...] = jnp.zeros_like(dgn_ref)

        dtb, dt, cs, cs_t, row, col, upper = _ssd_decays(dtr_ref, dtrt_ref, bias_ref, biast_ref, alog_ref, alogt_ref)
        first = _lane_first_half((BLK, 2 * HD))
        first_row = _lane_first_half((1, 2 * HD))
        tril = row >= col
        triu = row <= col
        last_row = lax.broadcasted_iota(jnp.int32, (BLK, 1), 0) == BLK - 1
        lane8 = lax.broadcasted_iota(jnp.int32, (BLK, 8), 1)

        yv = y_ref[...]
        zv = z_ref[...].astype(F32)
        sg = _sigmoid(zv)
        yz = yv * (zv * sg)
        r = lax.rsqrt(jnp.mean(yz * yz, axis=-1, keepdims=True) + EPS)
        yzn = yz * r
        dsn = dsin_ref[...]
        dgn_ref[...] += jnp.sum(dsn * yzn, axis=0, keepdims=True)
        dsn = dsn * gn_ref[...]
        dyz = r * (dsn - yzn * jnp.mean(dsn * yzn, axis=-1, keepdims=True))
        dy = dyz * (zv * sg)
        dz_ref[...] = (dyz * yv * (sg * (1.0 + zv * (1.0 - sg)))).astype(BF16)
        xs_all = xs_ref[...].astype(F32)
        ddsk_ref[...] += jnp.sum(dy * xs_all, axis=0, keepdims=True)

        b16, c16 = b_ref[...], c_ref[...]
        cb = _dot_nt(c16, b16)
        cb_t = _dot_nt(b16, c16)
        n_pairs = GROUP_W // (2 * HD)
        tot = cs[BLK - 1:BLK, :]
        exp_cs, exp_rest, exp_tot = jnp.exp(cs), jnp.exp(tot - cs), jnp.exp(tot)

        lanes_of_head = (lax.broadcasted_iota(jnp.int32, (8, GROUP_W), 1) // HD
                         == lax.broadcasted_iota(jnp.int32, (8, GROUP_W), 0)).astype(BF16)

        def per_head(v, mask):
            if v.shape[0] == 1:
                return jnp.concatenate([jnp.where(mask, v[:, 2 * pr:2 * pr + 1], v[:, 2 * pr + 1:2 * pr + 2])
                                        for pr in range(n_pairs)], axis=1)
            hi = v.astype(BF16)
            lo = (v - hi.astype(F32)).astype(BF16)
            return _dot(hi, lanes_of_head) + _dot(lo, lanes_of_head)

        head_of_lane = (lax.broadcasted_iota(jnp.int32, (GROUP_W, 8), 0) // HD
                        == lax.broadcasted_iota(jnp.int32, (GROUP_W, 8), 1)).astype(BF16)

        def head_sums(v):
            hi = v.astype(BF16)
            lo = (v - hi.astype(F32)).astype(BF16)
            return _dot(hi, head_of_lane) + _dot(lo, head_of_lane)

        dt_w, e_w, f_w = per_head(dt, first), per_head(exp_cs, first), per_head(exp_rest, first)
        xt = xs_all * dt_w
        xt16 = xt.astype(BF16)
        hstate = jnp.concatenate([hp_ref[pr] for pr in range(n_pairs)], axis=1)
        h16 = hstate.astype(BF16)
        dhn = jnp.concatenate([dh_ref[pr] for pr in range(n_pairs)], axis=1)
        dhn16 = dhn.astype(BF16)
        edy16 = (e_w * dy).astype(BF16)
        y_off = e_w * _dot(c16, h16)
        dcs_all = head_sums(dy * y_off)
        dc_acc = _dot_nt(edy16, h16)
        zmat = _dot(b16, dhn16)
        t_all = head_sums(zmat * xt) * exp_rest
        hh_rows = jnp.sum(head_sums(dhn * hstate), axis=0, keepdims=True)
        dtot = jnp.sum(t_all, axis=0, keepdims=True) + hh_rows * exp_tot
        dcs_all = dcs_all - t_all + jnp.where(last_row, dtot, 0.0)
        fxt16 = (f_w * xt).astype(BF16)
        db_acc = _dot_nt(fxt16, dhn16)
        dh_new = _dot_tn(c16, edy16) + per_head(exp_tot, first_row) * dhn
        for pr in range(n_pairs):
            dh_ref[pr] = dh_new[:, pr * 2 * HD:(pr + 1) * 2 * HD]
        g_sum = jnp.zeros((BLK, BLK), F32)
        gt_sum = jnp.zeros((BLK, BLK), F32)
        d_xt_parts = []
        for pr in range(n_pairs):
            cols = slice(pr * 2 * HD, (pr + 1) * 2 * HD)
            dym2 = _stack_heads(dy[:, cols].astype(BF16), first)
            d_m2 = _dot_nt(dym2, xt16[:, cols])
            d_mt2 = _dot_nt(xt16[:, cols], dym2)
            mt2 = []
            for e, h in enumerate((2 * pr, 2 * pr + 1)):
                cs_c, cs_r = cs[:, h:h + 1], cs_t[h:h + 1, :]
                decay = jnp.exp(jnp.where(tril, cs_c - cs_r, NEG))
                decay_t = jnp.exp(jnp.where(triu, cs_r - cs_c, NEG))
                gm = d_m2[e * BLK:(e + 1) * BLK] * decay
                gmt = d_mt2[:, e * BLK:(e + 1) * BLK] * decay_t
                g_sum += gm
                gt_sum += gmt
                dcs_h = jnp.sum(gm * cb, axis=-1, keepdims=True) - jnp.sum(gmt * cb_t, axis=-1, keepdims=True)
                dcs_all = dcs_all + jnp.where(lane8 == h, dcs_h, 0.0)
                mt2.append((cb_t * decay_t).astype(BF16))
            d_xt_parts.append(_dot(jnp.concatenate(mt2, axis=1), dym2))
        d_xt = jnp.concatenate(d_xt_parts, axis=1) + f_w * zmat
        dxs_ref[...] = dy * per_head(dsk_ref[...], first_row) + d_xt * dt_w
        ddtx_all = head_sums(d_xt * xs_all)

        dc_ref[...] = dc_acc + _dot(g_sum.astype(BF16), b16)
        db_ref[...] = db_acc + _dot(gt_sum.astype(BF16), c16)
        d_a = _dot_hi(upper, dcs_all)
        a_neg = -jnp.exp(alog_ref[...])
        ddt = ddtx_all + d_a * a_neg
        da_ref[...] += jnp.sum(d_a * dt, axis=0, keepdims=True)
        ddtr = ddt * _sigmoid(dtb)
        ddt_ref[...] = ddtr
        dbias_ref[...] += jnp.sum(ddtr, axis=0, keepdims=True)

    gps = SSD_GROUPS_PER_STEP
    k_wide, k_narrow, k_lead = ("cols", GROUP_W), ("cols", BLK), ("lead", 1)
    kinds = ([k_wide, k_narrow, k_narrow, ("each", 0), k_wide, k_wide] + [k_lead] * 8 + [k_wide]
             + [k_wide, k_narrow, k_narrow, k_wide] + [k_lead] * 4 + [k_wide] + [("lead", 4)])
    wide_w, narrow_w = GROUP_W * gps, BLK * gps
    rc = lambda c: nc - 1 - c
    gparam = pl.BlockSpec((gps, 1, 8), lambda g, c: (g, 0, 0))
    gparam_t = pl.BlockSpec((gps, 8, 1), lambda g, c: (g, 0, 0))
    wide = pl.BlockSpec((BLK, wide_w), lambda g, c: (rc(c), g))
    narrow = pl.BlockSpec((BLK, narrow_w), lambda g, c: (rc(c), g))
    z_specs = [pl.BlockSpec((BLK, GROUP_W),
                            functools.partial(lambda g, c, gi: (rc(c), COL_Z // GROUP_W + gps * g + gi), gi=gi))
               for gi in range(gps)]
    return _call(
        _per_group(body, gps, kinds), name=name,
        out_shape=[_sds((s, SSD_INNER), F32), _sds((s, GROUP_W), F32), _sds((s, GROUP_W), F32),
                   _sds((s, SSD_INNER), BF16), _sds((SSD_GROUPS, s, 8), F32),
                   _sds((SSD_GROUPS, 1, 8), F32), _sds((SSD_GROUPS, 1, 8), F32),
                   _sds((SSD_GROUPS, 1, GROUP_W), F32), _sds((1, SSD_INNER), F32)],
        grid=(SSD_GROUPS // gps, nc),
        in_specs=[wide,
                  pl.BlockSpec((BLK, narrow_w), lambda g, c: (rc(c), SSD_INNER // narrow_w + g)),
                  pl.BlockSpec((BLK, narrow_w), lambda g, c: (rc(c), (SSD_INNER + SSD_GROUPS * BLK) // narrow_w + g)),
                  *z_specs,
                  wide, wide,
                  pl.BlockSpec((gps, None, 4, BLK, 2 * HD), lambda g, c: (g, rc(c), 0, 0, 0)),
                  pl.BlockSpec((gps, BLK, 8), lambda g, c: (g, rc(c), 0)),
                  pl.BlockSpec((gps, 8, BLK), lambda g, c: (g, 0, rc(c))),
                  gparam, gparam_t, gparam, gparam_t, gparam,
                  pl.BlockSpec((1, wide_w), lambda g, c: (0, g))],
        out_specs=[wide, narrow, narrow, wide,
                   pl.BlockSpec((gps, BLK, 8), lambda g, c: (g, rc(c), 0)),
                   gparam, gparam,
                   pl.BlockSpec((gps, 1, GROUP_W), lambda g, c: (g, 0, 0)),
                   pl.BlockSpec((1, wide_w), lambda g, c: (0, g))],
        scratch_shapes=[pltpu.VMEM((4 * gps, BLK, 2 * HD), F32)],
    )(xc, xc, xc, *([p] * gps), y, d_sin, hprev, dtg, dtg_t, bias, bias_t, alog, alog_t, dskip, gn)


def merge_fwd(p, a, sbr, name, tm=512):
    s = p.shape[0]
    nj = D_MODEL // GROUP_W

    def body(ga_ref, gs_ref, a_ref, s_ref, o_ref):
        o_ref[...] = (_sigmoid(ga_ref[...].astype(F32)) * a_ref[...]
                      + _sigmoid(gs_ref[...].astype(F32)) * s_ref[...]).astype(BF16)

    blk = pl.BlockSpec((tm, GROUP_W), lambda i, j: (i, j))
    return _call(body, name=name, out_shape=_sds((s, D_MODEL), BF16), grid=(s // tm, nj),
                 in_specs=[pl.BlockSpec((tm, GROUP_W), lambda i, j: (i, COL_GA // GROUP_W + j)),
                           pl.BlockSpec((tm, GROUP_W), lambda i, j: (i, COL_GS // GROUP_W + j)), blk, blk],
                 out_specs=blk)(p, p, a, sbr)


def merge_bwd(p, a, sbr, dmerged, name, tm=512):
    s = p.shape[0]
    nj = D_MODEL // GROUP_W

    def body(ga_ref, gs_ref, a_ref, s_ref, dm_ref, da_ref, ds_ref, dga_ref, dgs_ref):
        dm = dm_ref[...]
        sa = _sigmoid(ga_ref[...].astype(F32))
        ss = _sigmoid(gs_ref[...].astype(F32))
        da_ref[...] = (dm * sa).astype(BF16)
        ds_ref[...] = (dm * ss).astype(BF16)
        dga_ref[...] = (dm * a_ref[...] * sa * (1.0 - sa)).astype(BF16)
        dgs_ref[...] = (dm * s_ref[...] * ss * (1.0 - ss)).astype(BF16)

    blk = pl.BlockSpec((tm, GROUP_W), lambda i, j: (i, j))
    shp = _sds((s, D_MODEL), BF16)
    return _call(body, name=name, out_shape=[shp] * 4, grid=(s // tm, nj),
                 in_specs=[pl.BlockSpec((tm, GROUP_W), lambda i, j: (i, COL_GA // GROUP_W + j)),
                           pl.BlockSpec((tm, GROUP_W), lambda i, j: (i, COL_GS // GROUP_W + j)), blk, blk, blk],
                 out_specs=[blk] * 4)(p, p, a, sbr, dmerged)


def _group_major(v):
    return v.reshape(SSD_GROUPS, 1, 8), v.reshape(SSD_GROUPS, 8, 1)


def mixer_forward(x, w, rider=None, later_weights=None):
    s = x.shape[0]
    h = rms_fwd(x, w["mix_norm"], "mix_rms")
    p = matmul_nn(h, w["w_in_main"], "mix_proj", BF16, tm=2048, tn=512, rider=rider)
    rode = None
    if rider is not None:
        p, rode = p
        w = dict(w, **later_weights(rode))
    dt_raw = matmul_nn(h, w["w_in_dt"], "mix_proj_dt", F32, tm=1024, tn=DT_PAD)
    qn, kn = qk_norm_fwd(p, w["q_gain"], w["k_gain"], "qk_norm")
    streams, os_, lses = [], [], []
    for g, d in enumerate(ATTN_DILATIONS):
        cols = slice(g * GROUP_W, (g + 1) * GROUP_W)
        qs, ks = _to_streams(qn[:, cols], d), _to_streams(kn[:, cols], d)
        vs = _to_streams(p[:, COL_V + g * GROUP_W:COL_V + (g + 1) * GROUP_W], d)
        o, lse = attn_fwd(qs, ks, vs, g, f"attn_fwd{g}")
        streams.append((qs, ks, vs, lse))
        os_.append(_from_streams(o, d))
        lses.append(_from_streams(lse, d))
    attn_o = attn_merge_fwd(os_, lses, "attn_merge")
    cpre, xc = conv_fwd(p, w["conv_w"], w["conv_b"], "conv_fwd")
    dtg = dt_raw[:, :SSD_HEADS].reshape(s, SSD_GROUPS, 8).transpose(1, 0, 2)
    dtg_t = dtg.transpose(0, 2, 1)
    params = (*_group_major(w["dt_bias"]), *_group_major(w["a_log"]), _group_major(w["d_skip"])[0])
    y, s_in, hprev = ssd_fwd(p, xc, dtg, dtg_t, params, w["ssd_norm"], "ssd_fwd")
    a = matmul_nn(attn_o, w["w_attn_branch"], "attn_branch", F32, tm=1024, tn=512)
    sbr = matmul_nn(s_in, w["w_ssd_branch"], "ssd_branch", F32, tm=1024, tn=512)
    merged = merge_fwd(p, a, sbr, "merge")
    x_out = matmul_nn(merged, w["w_out"], "mix_out", F32, tm=1024, tn=512, res=x)
    saved = dict(h=h, p=p, streams=streams, os=os_, lses=lses, attn_o=attn_o, cpre=cpre, xc=xc, dtg=dtg,
                 dtg_t=dtg_t, params=params, y=y, s_in=s_in, hprev=hprev, a=a, sbr=sbr, merged=merged, w=w)
    return x_out, saved, rode


def mixer_backward(dx_out, x, sv, ride_early=None, ride_late=None):
    s = x.shape[0]
    p = sv["p"]
    w = sv["w"]
    g = {}
    dmerged = matmul_nt(dx_out, w["w_out"], "d_merged", F32, tm=1024, tn=512, tk=1024)
    g["w_out"] = matmul_tn(sv["merged"], dx_out, "dw_out", tn=512, ts=2048)
    da, ds, dga, dgs = merge_bwd(p, sv["a"], sv["sbr"], dmerged, "merge_bwd")
    g["w_attn_branch"] = matmul_tn(sv["attn_o"], da, "dw_attn_branch", tn=512, ts=2048)
    g["w_ssd_branch"] = matmul_tn(sv["s_in"], ds, "dw_ssd_branch", tn=512, ts=2048)
    d_attn_o = matmul_nt(da, w["w_attn_branch"], "d_attn_o", F32, tm=1024, tn=512, tk=1024)
    d_sin = matmul_nt(ds, w["w_ssd_branch"], "d_ssd_in", F32, tm=1024, tn=512, tk=1024)
    dxs, d_b, d_c, dz, ddt, d_asum, d_bias, d_dsk, d_gn = ssd_bwd(
        p, sv["xc"], sv["y"], d_sin, sv["hprev"], sv["dtg"], sv["dtg_t"], sv["params"], w["ssd_norm"], "ssd_bwd")
    dxbc, d_convw, d_convb = conv_bwd(p, sv["cpre"], dxs, d_b, d_c, w["conv_w"], "conv_bwd")
    g["conv_w"] = d_convw[:SSD_CONV]
    g["conv_b"] = d_convb
    g["dt_bias"] = d_bias.reshape(1, SSD_HEADS)
    g["a_log"] = (d_asum * (-jnp.exp(sv["params"][2]))).reshape(1, SSD_HEADS)
    g["d_skip"] = jnp.sum(d_dsk.reshape(SSD_HEADS, HD), axis=1).reshape(1, SSD_HEADS)
    g["ssd_norm"] = d_gn
    merged_bwd = attn_merge_bwd(d_attn_o, sv["os"], sv["lses"], "attn_merge_bwd")
    dqs, dks, dvs = [], [], []
    for gi, d in enumerate(ATTN_DILATIONS):
        qs, ks, vs, lse = sv["streams"][gi]
        d_o = _to_streams(merged_bwd[gi], d)
        cterm = _to_streams(merged_bwd[3 + gi], d)
        dq, dk, dv = attn_bwd(qs, ks, vs, d_o, cterm, lse, gi, f"attn_bwd{gi}")
        dqs.append(_from_streams(dq, d))
        dks.append(_from_streams(dk, d))
        dvs.append(_from_streams(dv, d).astype(BF16))
    dqk, d_qg, d_kg = qk_norm_bwd(p, dqs, dks, w["q_gain"], w["k_gain"], "qk_norm_bwd")
    g["q_norm"] = jnp.sum(d_qg.reshape(N_ATTN_HEADS, HD), axis=0).reshape(1, HD)
    g["k_norm"] = jnp.sum(d_kg.reshape(N_ATTN_HEADS, HD), axis=0).reshape(1, HD)
    dp = [dqk, jnp.concatenate(dvs, axis=1), dz, dxbc, dga, dgs]
    ddt_pad = jnp.pad(ddt.transpose(1, 0, 2).reshape(s, SSD_HEADS), ((0, 0), (0, DT_PAD - SSD_HEADS)))
    if ride_early is not None:
        g["w_in_main"], g["rode_early"] = matmul_tn_pieces(sv["h"], dp, "dw_in", tn=512, ts=1024,
                                                           rider=ride_early(g))
    else:
        g["w_in_main"] = matmul_tn_pieces(sv["h"], dp, "dw_in", tn=512, ts=1024)
    g["w_in_dt"] = matmul_tn(sv["h"], ddt_pad, "dw_in_dt", tn=DT_PAD, ts=1024)
    if ride_late is not None:
        dh_main, g["rode_late"] = matmul_nt_pieces(dp, w["w_in_main"], "d_mix_h", F32, tm=1024, tn=1024, tk=512,
                                                   rider=ride_late(g))
    else:
        dh_main = matmul_nt_pieces(dp, w["w_in_main"], "d_mix_h", F32, tm=1024, tn=512, tk=512)
    dh_dt = matmul_nt(ddt_pad, w["w_in_dt"], "d_mix_h_dt", F32, tm=1024, tn=1024, tk=DT_PAD)
    dx, g["mix_norm"] = rms_bwd([dh_main, dh_dt], x, w["mix_norm"], dx_out, "mix_drms")
    return dx, g


def _place():
    x, y, c = lax.axis_index("x"), lax.axis_index("y"), lax.axis_index("c")
    chips = [(1 - x, y), (x, 1 - y), (1 - x, 1 - y)]
    return x, y, c, 2 * x + y, chips


def _cores():
    c = lax.axis_index("c")
    return jnp.stack([c, 1 - c]).astype(jnp.int32)


def _staged_call(body, *, name, grid, in_specs, out_specs, out_shape, scratch_shapes):
    return pl.pallas_call(
        body, out_shape=out_shape, name=name,
        grid_spec=pltpu.PrefetchScalarGridSpec(num_scalar_prefetch=1, grid=grid, in_specs=in_specs,
                                               out_specs=out_specs, scratch_shapes=scratch_shapes),
        compiler_params=pltpu.CompilerParams(dimension_semantics=("arbitrary",) * len(grid),
                                             vmem_limit_bytes=V7X_VMEM_LIMIT, has_side_effects=True))


def gather_rider(shards, tiles):
    dma = pltpu.SemaphoreType.DMA
    n = len(shards)
    geo = [(a.shape[0] // 2, tm, (a.shape[0] // 2) // tm) for a, tm in zip(shards, tiles)]
    scratch = []
    for a, (h, tm, nk) in zip(shards, geo):
        scratch += [pltpu.VMEM((N_CHIP,) + a.shape, a.dtype), dma((3, nk)), dma((3, nk)), dma((3, nk)), dma((3, nk)),
                    dma((nk + 2,))]

    def copies(j, in_ref, scr):
        buf, send1, recv1, send2, recv2, local = scr[6 * j:6 * j + 6]
        h, tm, nk = geo[j]
        x, y, c, me, chips = _place()
        chip_of = [2 * chips[t][0] + chips[t][1] for t in range(3)]

        def rows(chip, core, k):
            return buf.at[chip, pl.ds(core * h + k * tm, tm)]

        def mine(k):
            if k == nk:
                return pltpu.make_async_copy(in_ref.at[pl.ds((1 - c) * h, h)], buf.at[me, pl.ds((1 - c) * h, h)],
                                             local.at[nk])
            return pltpu.make_async_copy(in_ref.at[pl.ds(c * h + k * tm, tm)], rows(me, c, k), local.at[k])

        def level1(t, k, incoming):
            place = rows(chip_of[t] if incoming else me, c, k)
            return pltpu.make_async_remote_copy(src_ref=place, dst_ref=place, send_sem=send1.at[t, k],
                                                recv_sem=recv1.at[t, k], device_id=(*chips[t], c), device_id_type=MESH)

        def level2(t, k, incoming):
            place = rows(chip_of[t], (1 - c) if incoming else c, k)
            return pltpu.make_async_remote_copy(src_ref=place, dst_ref=place, send_sem=send2.at[t, k],
                                                recv_sem=recv2.at[t, k], device_id=(x, y, 1 - c),
                                                device_id_type=MESH)

        return buf, local, nk, mine, level1, level2

    def start(ins, outs, scr):
        for j in range(n):
            _, _, nk, mine, _, _ = copies(j, ins[j], scr)
            for k in range(nk + 1):
                mine(k).start()
        for j in range(n):
            _, _, nk, mine, level1, _ = copies(j, ins[j], scr)
            for k in range(nk):
                mine(k).wait()
                for t in range(3):
                    level1(t, k, False).start()

    def finish(ins, outs, scr):
        for j in range(n):
            _, _, nk, _, level1, level2 = copies(j, ins[j], scr)
            for k in range(nk):
                for t in range(3):
                    level1(t, k, True).wait_recv()
                    level2(t, k, False).start()
        for j in range(n):
            buf, local, nk, mine, level1, level2 = copies(j, ins[j], scr)
            for k in range(nk):
                for t in range(3):
                    level2(t, k, True).wait_recv()
            for k in range(nk):
                for t in range(3):
                    level1(t, k, False).wait_send()
                    level2(t, k, False).wait_send()
            mine(nk).wait()
            pltpu.make_async_copy(buf, outs[j], local.at[nk + 1]).start()
        for j in range(n):
            buf, local, nk, _, _, _ = copies(j, ins[j], scr)
            pltpu.make_async_copy(buf, outs[j], local.at[nk + 1]).wait()

    return Rider(list(shards), [_sds((N_CHIP,) + a.shape, a.dtype) for a in shards], scratch, start, finish)


def sibling_sum(g, tm, name):
    _, r, cdim = g.shape
    h = r // 2
    ni = h // tm
    dma = pltpu.SemaphoreType.DMA

    def body(cores_ref, keep_ref, give_ref, out_ref, slot, send, recv):
        par = (pl.program_id(0) * ni + pl.program_id(1)) % 2
        x, y, c, _, _ = _place()
        cp = pltpu.make_async_remote_copy(src_ref=give_ref, dst_ref=slot.at[par], send_sem=send.at[par],
                                          recv_sem=recv.at[par], device_id=(x, y, 1 - c), device_id_type=MESH)
        cp.start()
        cp.wait_recv()
        out_ref[...] = (keep_ref[...].astype(F32) + slot[par].astype(F32)).astype(out_ref.dtype)
        cp.wait_send()

    flat = g.reshape(N_CHIP * r, cdim)
    return _staged_call(
        body, name=name, grid=(N_CHIP, ni),
        in_specs=[pl.BlockSpec((tm, cdim), lambda j, i, cores: ((2 * j + cores[0]) * ni + i, 0)),
                  pl.BlockSpec((tm, cdim), lambda j, i, cores: ((2 * j + cores[1]) * ni + i, 0))],
        out_specs=pl.BlockSpec((None, tm, cdim), lambda j, i, cores: (j, i, 0)),
        out_shape=_sds((N_CHIP, h, cdim), g.dtype),
        scratch_shapes=[pltpu.VMEM((2, tm, cdim), g.dtype), dma((2,)), dma((2,))],
    )(_cores(), flat, flat)


def owner_sum_rider(sums, tiles):
    dma = pltpu.SemaphoreType.DMA
    n = len(sums)
    geo = [(a.shape[1], tm, a.shape[1] // tm) for a, tm in zip(sums, tiles)]
    scratch = []
    for a, (h, tm, nk) in zip(sums, geo):
        cdim = a.shape[2]
        scratch += [pltpu.VMEM(a.shape, a.dtype), pltpu.VMEM((3, h, cdim), a.dtype), pltpu.VMEM((2, h, cdim), F32),
                    dma((3, nk)), dma((3, nk)), dma((nk,)), dma((nk,)), dma((2,))]

    def copies(j, scr):
        part, got, res, send, recv, send2, recv2, local = scr[8 * j:8 * j + 8]
        h, tm, nk = geo[j]
        x, y, c, me, chips = _place()

        def to_owner(t, k):
            chip = 2 * chips[t][0] + chips[t][1]
            return pltpu.make_async_remote_copy(
                src_ref=part.at[chip, pl.ds(k * tm, tm)], dst_ref=got.at[t, pl.ds(k * tm, tm)],
                send_sem=send.at[t, k], recv_sem=recv.at[t, k], device_id=(*chips[t], c), device_id_type=MESH)

        def to_sibling(k):
            place = res.at[c, pl.ds(k * tm, tm)]
            return pltpu.make_async_remote_copy(src_ref=place, dst_ref=place, send_sem=send2.at[k],
                                                recv_sem=recv2.at[k], device_id=(x, y, 1 - c), device_id_type=MESH)

        return part, got, res, local, to_owner, to_sibling, (tm, nk, c, me)

    def start(ins, outs, scr):
        for j in range(n):
            part, _, _, local, _, _, _ = copies(j, scr)
            pltpu.make_async_copy(ins[j], part, local.at[0]).start()
        for j in range(n):
            part, _, _, local, to_owner, _, (tm, nk, c, me) = copies(j, scr)
            pltpu.make_async_copy(ins[j], part, local.at[0]).wait()
            for k in range(nk):
                for t in range(3):
                    to_owner(t, k).start()

    def finish(ins, outs, scr):
        for j in range(n):
            part, got, res, _, to_owner, to_sibling, (tm, nk, c, me) = copies(j, scr)
            for k in range(nk):
                rows = pl.ds(k * tm, tm)
                for t in range(3):
                    to_owner(t, k).wait_recv()
                acc = part[me, rows, :].astype(F32)
                for t in range(3):
                    acc = acc + got[t, rows, :].astype(F32)
                res[c, rows, :] = acc
                to_sibling(k).start()
        for j in range(n):
            _, _, res, local, to_owner, to_sibling, (tm, nk, c, me) = copies(j, scr)
            for k in range(nk):
                to_sibling(k).wait_recv()
            for k in range(nk):
                to_sibling(k).wait_send()
                for t in range(3):
                    to_owner(t, k).wait_send()
            pltpu.make_async_copy(res, outs[j], local.at[1]).start()
        for j in range(n):
            _, _, res, local, _, _, _ = copies(j, scr)
            pltpu.make_async_copy(res, outs[j], local.at[1]).wait()

    return Rider(list(sums), [_sds((2, a.shape[1], a.shape[2]), F32) for a in sums], scratch, start, finish)


def gather_conv_w(w):
    def body(in_ref, out_ref, send, recv):
        x, y, c, me, chips = _place()
        out_ref[me] = in_ref[...]
        copies = []
        for t in range(3):
            copies.append(pltpu.make_async_remote_copy(
                src_ref=out_ref.at[me], dst_ref=out_ref.at[me], send_sem=send.at[t], recv_sem=recv.at[t],
                device_id=(*chips[t], c), device_id_type=MESH))
        for cp in copies:
            cp.start()
        for cp in copies:
            cp.wait_recv()
        for cp in copies:
            cp.wait_send()

    dma = pltpu.SemaphoreType.DMA
    vmem = pl.BlockSpec(memory_space=pltpu.VMEM)
    return pl.pallas_call(
        body, out_shape=_sds((N_CHIP,) + w.shape, w.dtype), in_specs=[vmem], out_specs=vmem, name="gather_conv_w",
        scratch_shapes=[dma((3,)), dma((3,))],
        compiler_params=pltpu.CompilerParams(has_side_effects=True))(w)


N_DEV = 8
SMALL_ROWS = 32
SMALL_LANES = 1024


def all_reduce_small(arrays):
    n_arr = len(arrays)
    places = []
    for k, a in enumerate(arrays):
        for ri in range(a.shape[0]):
            for c0 in range(0, a.shape[1], SMALL_LANES):
                places.append((k, ri, c0, min(SMALL_LANES, a.shape[1] - c0), len(places)))
    assert len(places) <= SMALL_ROWS

    def body(*refs):
        ins, outs = refs[:n_arr], refs[n_arr:2 * n_arr]
        buf, send, recv = refs[2 * n_arr:]
        x, y, c, _, _ = _place()
        me = 4 * x + 2 * y + c
        buf[me] = jnp.zeros((SMALL_ROWS, SMALL_LANES), F32)
        for k, ri, c0, width, row in places:
            buf[me, row:row + 1, 0:width] = ins[k][ri:ri + 1, c0:c0 + width]
        copies = []
        for r in range(1, N_DEV):
            px = (1 - x) if r & 4 else x
            py = (1 - y) if r & 2 else y
            pc = (1 - c) if r & 1 else c
            copies.append(pltpu.make_async_remote_copy(
                src_ref=buf.at[me], dst_ref=buf.at[me], send_sem=send.at[r - 1], recv_sem=recv.at[r - 1],
                device_id=(px, py, pc), device_id_type=MESH))
        for cp in copies:
            cp.start()
        for cp in copies:
            cp.wait_recv()
        for cp in copies:
            cp.wait_send()
        acc = buf[0]
        for j in range(1, N_DEV):
            acc = acc + buf[j]
        for k, ri, c0, width, row in places:
            outs[k][ri:ri + 1, c0:c0 + width] = acc[row:row + 1, 0:width]

    dma = pltpu.SemaphoreType.DMA
    vmem = pl.BlockSpec(memory_space=pltpu.VMEM)
    return pl.pallas_call(
        body, out_shape=[_sds(a.shape, F32) for a in arrays], in_specs=[vmem] * n_arr, out_specs=[vmem] * n_arr,
        name="all_reduce_small",
        scratch_shapes=[pltpu.VMEM((N_DEV, SMALL_ROWS, SMALL_LANES), F32), dma((N_DEV - 1,)), dma((N_DEV - 1,))],
        compiler_params=pltpu.CompilerParams(has_side_effects=True))(*arrays)


def _row_tile(rows, limit, multiple):
    return max(t for t in range(multiple, min(rows, limit) + 1, multiple) if rows % t == 0)


def _adamw_math(w, g, m, v):
    c1 = 1.0 - ADAM_B1 ** ADAM_STEP
    c2 = 1.0 - ADAM_B2 ** ADAM_STEP
    m2 = ADAM_B1 * m + (1.0 - ADAM_B1) * g
    v2 = ADAM_B2 * v + (1.0 - ADAM_B2) * (g * g)
    return -ADAM_LR * ((m2 / c1) / (jnp.sqrt(v2 / c2) + ADAM_EPS) + ADAM_WD * w), m2, v2


def adamw(w, g, row_off, m, v, name):
    _, r, c = w.shape
    tm = r if r < 8 else _row_tile(math.gcd(r, row_off) if row_off else r, 128, 8)

    def body(w_ref, g_ref, m_ref, v_ref, go_ref, d_ref, m2_ref, v2_ref):
        gv = g_ref[...]
        go_ref[...] = gv
        d_ref[...], m2_ref[...], v2_ref[...] = _adamw_math(w_ref[...], gv, m_ref[...], v_ref[...])

    blk = pl.BlockSpec((None, tm, c), lambda i: (0, i, 0))
    shp = _sds((1, r, c), F32)
    return _call(body, name=name, out_shape=[shp] * 4, grid=(r // tm,),
                 in_specs=[blk, pl.BlockSpec((tm, c), lambda i: (row_off // tm + i, 0)), blk, blk],
                 out_specs=[blk] * 4)(w, g, m, v)


def adamw_small(ws, gs, ms, vs):
    n = len(ws)

    def body(*refs):
        ins, outs = refs[:4 * n], refs[4 * n:]
        for k in range(n):
            w_ref, g_ref, m_ref, v_ref = (ins[j * n + k] for j in range(4))
            outs[k][...], outs[n + k][...], outs[2 * n + k][...] = _adamw_math(w_ref[...], g_ref[...], m_ref[...],
                                                                               v_ref[...])

    vmem = pl.BlockSpec(memory_space=pltpu.VMEM)
    shapes = [_sds(w.shape, F32) for w in ws] * 3
    res = pl.pallas_call(body, out_shape=shapes, in_specs=[vmem] * (4 * n), out_specs=[vmem] * (3 * n),
                         name="adamw_small")(*ws, *gs, *ms, *vs)
    return res[:n], res[n:2 * n], res[2 * n:]


SMALL = ("ffn1_norm", "mix_norm", "q_norm", "k_norm", "conv_b", "dt_bias", "a_log", "d_skip", "ssd_norm", "ffn2_norm")
WEIGHTS = ("ffn1_norm", "ffn1_w_gate", "ffn1_w_up", "ffn1_w_down", "mix_norm", "w_in", "q_norm", "k_norm", "conv_w",
           "conv_b", "dt_bias", "a_log", "d_skip", "ssd_norm", "w_attn_branch", "w_ssd_branch", "w_out", "ffn2_norm",
           "ffn2_w_gate", "ffn2_w_up", "ffn2_w_down")
CONV_SHARD = SSD_CONV_DIM // N_CHIP
CLASSES = {
    "ffn1_in": (("ffn1_w_gate", 1024), ("ffn1_w_up", 1024)),
    "ffn1_out": (("ffn1_w_down", 704),),
    "mix_in": (("w_in", 1024),),
    "mix_attn": (("w_attn_branch", 512),),
    "late_out": (("ffn2_w_down", 704), ("w_ssd_branch", 512), ("w_out", 256)),
    "ffn2_in": (("ffn2_w_gate", 1024), ("ffn2_w_up", 1024)),
}
CLASS_TILE = {"ffn1_in": 256, "ffn1_out": 176, "mix_attn": 256, "late_out": 368, "ffn2_in": 256,
              "mix_in_top": 128, "mix_in_bottom": 128}
SIBLING_TILE = {"ffn1_in": 1024, "ffn1_out": 352, "mix_attn": 256, "late_out": 736, "ffn2_in": 1024,
                "mix_in_top": 256, "mix_in_bottom": 256}


def _chip_major_cols(a):
    r = a.shape[0]
    return a.reshape(r, N_CHIP, -1).transpose(1, 0, 2)


def _from_chip_major_cols(a):
    return a.transpose(1, 0, 2).reshape(a.shape[1], -1)


def kernel(x, ffn1_norm, ffn1_w_gate, ffn1_w_up, ffn1_w_down, mix_norm, w_in, q_norm, k_norm, conv_w, conv_b, dt_bias, a_log, d_skip, ssd_norm, w_attn_branch, w_ssd_branch, w_out, ffn2_norm, ffn2_w_gate, ffn2_w_up, ffn2_w_down, loss_target, m_ffn1_norm, m_ffn1_w_gate, m_ffn1_w_up, m_ffn1_w_down, m_mix_norm, m_w_in, m_q_norm, m_k_norm, m_conv_w, m_conv_b, m_dt_bias, m_a_log, m_d_skip, m_ssd_norm, m_w_attn_branch, m_w_ssd_branch, m_w_out, m_ffn2_norm, m_ffn2_w_gate, m_ffn2_w_up, m_ffn2_w_down, v_ffn1_norm, v_ffn1_w_gate, v_ffn1_w_up, v_ffn1_w_down, v_mix_norm, v_w_in, v_q_norm, v_k_norm, v_conv_w, v_conv_b, v_dt_bias, v_a_log, v_d_skip, v_ssd_norm, v_w_attn_branch, v_w_ssd_branch, v_w_out, v_ffn2_norm, v_ffn2_w_gate, v_ffn2_w_up, v_ffn2_w_down):
    env = dict(locals())
    wts = {k: env[k] for k in WEIGHTS}
    moms = {k: env["m_" + k] for k in WEIGHTS}
    vars_ = {k: env["v_" + k] for k in WEIGHTS}
    x0 = x[0]
    target = loss_target[0]

    def gather(classes, more=()):
        shards = [jnp.concatenate([wts[k][0] for k, _ in CLASSES[c]], axis=0).astype(BF16) for c in classes]
        return gather_rider(shards + [a for a, _ in more], [CLASS_TILE[c] for c in classes] + [t for _, t in more])

    def reducer(classes, parts):
        sums = [sibling_sum(p, SIBLING_TILE[c], f"sibling_sum_{c}") for c, p in zip(classes, parts)]
        return owner_sum_rider(sums, [CLASS_TILE[c] for c in classes])

    half = D_MODEL // 2
    in_tile = CLASS_TILE["mix_in_top"]
    x1, saved1, (w_ffn1_in,), (w_ffn1_out, w_mix_attn, w_in_top), (w_in_bottom,) = ffn_forward(
        x0, ffn1_norm, lambda rode: rode[0], lambda rode: rode[0], "ffn1", rms_rider=gather(["ffn1_in"]),
        up_rider=gather(["ffn1_out", "mix_attn"], [(w_in[0, :half].astype(BF16), in_tile)]),
        down_rider=gather([], [(w_in[0, half:].astype(BF16), in_tile)]))
    dt0, dt1 = IN_DT0 - 3 * IN_SHARD, IN_DT1 - 3 * IN_SHARD

    def in_columns(w4):
        return jnp.concatenate([w4[0], w4[1], w4[2], w4[3][:, :dt0], w4[3][:, dt1:]], axis=1), w4[3][:, dt0:dt1]

    (main_top, dt_top), (main_bottom, dt_bottom) = in_columns(w_in_top), in_columns(w_in_bottom)
    mixer_w = dict(
        mix_norm=mix_norm,
        w_in_main=jnp.concatenate([main_top, main_bottom], axis=0),
        w_in_dt=jnp.pad(jnp.concatenate([dt_top, dt_bottom], axis=0), ((0, 0), (0, DT_PAD - SSD_HEADS))),
        q_gain=jnp.tile(q_norm, (1, 2)), k_gain=jnp.tile(k_norm, (1, 2)),
        conv_w=_from_chip_major_cols(gather_conv_w(conv_w[0])), conv_b=conv_b, dt_bias=dt_bias, a_log=a_log,
        d_skip=d_skip, ssd_norm=ssd_norm, w_attn_branch=_from_chip_major_cols(w_mix_attn))

    def later_weights(rode):
        late = rode[0]
        return dict(w_ssd_branch=late[:, 704:1216].reshape(SSD_INNER, D_MODEL),
                    w_out=late[:, 1216:1472].reshape(D_MODEL, D_MODEL))

    x2, saved_mix, (w_late_out, w_ffn2_in) = mixer_forward(x1, mixer_w, gather(["late_out", "ffn2_in"]), later_weights)
    (dx3, sq), saved2, _, _, _ = ffn_forward(x2, ffn2_norm, lambda rode: w_ffn2_in, lambda rode: w_late_out, "ffn2",
                                             target=target)

    grads = {}
    dx2, grads["ffn2_norm"], d_ffn2_in, d_ffn2_down, _ = ffn_backward(dx3, x2, ffn2_norm, w_ffn2_in, w_late_out,
                                                                      saved2, "ffn2")

    def ride_early(g):
        late = jnp.concatenate([d_ffn2_down, g["w_ssd_branch"].reshape(N_CHIP, -1, D_MODEL),
                                g["w_out"].reshape(N_CHIP, -1, D_MODEL)], axis=1)
        return reducer(["ffn2_in", "late_out"], [d_ffn2_in, late])

    g_in_rows = {}

    def ride_late(g):
        main = g["w_in_main"]
        last = jnp.concatenate([main[:, 3 * IN_SHARD:IN_DT0], g["w_in_dt"][:, :SSD_HEADS], main[:, IN_DT0:]], axis=1)
        for part, rows in (("top", slice(0, D_MODEL // 2)), ("bottom", slice(D_MODEL // 2, D_MODEL))):
            g_in_rows[part] = jnp.stack([main[rows, j * IN_SHARD:(j + 1) * IN_SHARD] for j in range(3)]
                                        + [last[rows]])
        return reducer(["mix_in_top", "mix_attn"], [g_in_rows["top"], _chip_major_cols(g["w_attn_branch"])])

    dx1, gmix = mixer_backward(dx2, x1, saved_mix, ride_early, ride_late)
    dx0, grads["ffn1_norm"], rode_in, rode_out, rode_hidden = ffn_backward(
        dx1, x0, ffn1_norm, w_ffn1_in, w_ffn1_out, saved1, "ffn1",
        hidden_rider=reducer(["mix_in_bottom"], [g_in_rows["bottom"]]),
        ride_down=lambda d: reducer(["ffn1_out"], [d]), ride_in=lambda d: reducer(["ffn1_in"], [d]))
    for k in ("mix_norm", "q_norm", "k_norm", "conv_b", "dt_bias", "a_log", "d_skip", "ssd_norm"):
        grads[k] = gmix[k]
    reduced = dict(zip(("ffn2_in", "late_out", "mix_in_top", "mix_attn", "ffn1_in", "ffn1_out", "mix_in_bottom"),
                       (*gmix["rode_early"], *gmix["rode_late"], rode_in[0], rode_out[0], rode_hidden[0])))
    reduced = {c: r.reshape(-1, r.shape[2]) for c, r in reduced.items()}
    reduced["mix_in"] = jnp.concatenate([reduced.pop("mix_in_top"), reduced.pop("mix_in_bottom")], axis=0)
    summed = all_reduce_small([grads[k] for k in SMALL]
                              + [gmix["conv_w"], (0.5 * jnp.sum(sq) / D_MODEL).reshape(1, 1)])
    g_small = dict(zip(SMALL, summed))
    loss = summed[-1].reshape(())
    chip = 2 * lax.axis_index("x") + lax.axis_index("y")
    g_conv = lax.dynamic_slice_in_dim(summed[-2], chip * CONV_SHARD, CONV_SHARD, axis=1)

    g_final, delta, new_m, new_v = dict(g_small), {}, {}, {}

    def update(k, g_arr, row_off):
        w, m, v = wts[k], moms[k], vars_[k]
        rows, cols = w.shape[1:]
        if cols % 128:
            res = adamw(jnp.swapaxes(w, 1, 2), g_arr[row_off:row_off + rows].T, 0, jnp.swapaxes(m, 1, 2),
                        jnp.swapaxes(v, 1, 2), f"adamw_{k}")
            res = [jnp.swapaxes(r, 1, 2) for r in res]
        else:
            res = adamw(w, g_arr, row_off, m, v, f"adamw_{k}")
        g_final[k], delta[k], new_m[k], new_v[k] = res

    for cls, members in CLASSES.items():
        off = 0
        for k, rows in members:
            update(k, reduced[cls], off)
            off += rows
    update("conv_w", g_conv, 0)
    small = adamw_small(*([d[k] for k in SMALL] for d in (wts, g_small, moms, vars_)))
    for res, vals in zip((delta, new_m, new_v), small):
        res.update(zip(SMALL, vals))

    return (loss, dx0[None], *[g_final[k] for k in WEIGHTS], *[delta[k] for k in WEIGHTS],
            *[new_m[k] for k in WEIGHTS], *[new_v[k] for k in WEIGHTS])
```

```python
import collections
import functools
import math

import jax
import jax.numpy as jnp
from jax import lax
from jax.experimental import pallas as pl
from jax.experimental.pallas import tpu as pltpu

F32 = jnp.float32
BF16 = jnp.bfloat16
MESH = pl.DeviceIdType.MESH

EPS = 1e-6
D_MODEL = 1024
D_FF = 2816
N_CHIP = 4
FF_SHARD = D_FF // N_CHIP
HD = 64
BLK = 128
ATTN_DILATIONS = (1, 4, 16)
HEADS_PER_PATTERN = 8
N_ATTN_HEADS = 24
ALIBI_MAX_EXP = 8.0
ATTN_QKV = 1536
GROUP_W = 512
SSD_INNER = 2048
SSD_HEADS = 32
SSD_GROUPS = 4
SSD_CONV = 4
SSD_CONV_DIM = 3072
IN_COLS = 11808
IN_DT0, IN_DT1 = 9728, 9760
IN_SHARD = IN_COLS // 4
COL_K, COL_V, COL_Z, COL_XBC, COL_GA, COL_GS, P_COLS = 1536, 3072, 4608, 6656, 9728, 10752, 11776
DT_PAD = 128

ADAM_LR, ADAM_B1, ADAM_B2, ADAM_EPS, ADAM_WD, ADAM_STEP = 0.001, 0.9, 0.999, 1e-08, 0.01, 10

V7X_VMEM_LIMIT = 56 * 1024 * 1024
NEG = -1e30


Rider = collections.namedtuple("Rider", "arrays out_shape scratch start finish")
Rider.__doc__ = """An exchange between devices that rides in a compute kernel: its copies are started in the host's
first grid step and waited for in its last, so they travel while the host computes.  arrays / out_shape: extra HBM
operands and results; scratch: extra scratch; start, finish: f(in_refs, out_refs, scratch_refs)."""


def _call(body, *, name, out_shape, in_specs, out_specs, grid=(), scratch_shapes=(), aliases=None, rider=None):
    params = dict(dimension_semantics=("arbitrary",) * len(grid), vmem_limit_bytes=V7X_VMEM_LIMIT)
    if rider is None:
        return pl.pallas_call(
            body, out_shape=out_shape, grid=grid, in_specs=in_specs, out_specs=out_specs,
            scratch_shapes=scratch_shapes, input_output_aliases=aliases or {}, name=name,
            compiler_params=pltpu.CompilerParams(**params))
    single = not isinstance(out_shape, (list, tuple))
    main_out = [out_shape] if single else list(out_shape)
    main_specs = [out_specs] if single else list(out_specs)
    n_in, n_out, n_scr = len(in_specs), len(main_out), len(scratch_shapes)
    r_in, r_out = len(rider.arrays), len(rider.out_shape)

    def wrapped(*refs):
        ins, refs = refs[:n_in], refs[n_in:]
        r_ins, refs = refs[:r_in], refs[r_in:]
        outs, refs = refs[:n_out], refs[n_out:]
        r_outs, refs = refs[:r_out], refs[r_out:]
        scr, r_scr = refs[:n_scr], refs[n_scr:]
        first = last = None
        for axis, size in enumerate(grid):
            at_start, at_end = pl.program_id(axis) == 0, pl.program_id(axis) == size - 1
            first = at_start if first is None else jnp.logical_and(first, at_start)
            last = at_end if last is None else jnp.logical_and(last, at_end)

        @pl.when(first)
        def _():
            rider.start(r_ins, r_outs, r_scr)

        body(*ins, *outs, *scr)

        @pl.when(last)
        def _():
            rider.finish(r_ins, r_outs, r_scr)

    hbm = pl.BlockSpec(memory_space=pl.ANY)
    call = pl.pallas_call(
        wrapped, out_shape=main_out + list(rider.out_shape), grid=grid, in_specs=list(in_specs) + [hbm] * r_in,
        out_specs=main_specs + [hbm] * r_out, scratch_shapes=list(scratch_shapes) + list(rider.scratch), name=name,
        compiler_params=pltpu.CompilerParams(has_side_effects=True, **params))

    def run(*args):
        res = call(*args, *rider.arrays)
        main = res[:n_out]
        return (main[0] if single else main), res[n_out:]

    return run


def _sds(shape, dtype):
    return jax.ShapeDtypeStruct(tuple(shape), dtype)


def _dot(a, b):
    return jnp.dot(a, b, preferred_element_type=F32)


def _dot_nt(a, b):
    return lax.dot_general(a, b, (((1,), (1,)), ((), ())), preferred_element_type=F32)


def _dot_tn(a, b):
    return lax.dot_general(a, b, (((0,), (0,)), ((), ())), preferred_element_type=F32)


def _dot_hi(a, b):
    return jnp.dot(a, b, preferred_element_type=F32, precision=lax.Precision.HIGHEST)


def _sigmoid(x):
    return 1.0 / (1.0 + jnp.exp(-x))


def _lane_first_half(shape):
    return lax.broadcasted_iota(jnp.int32, shape, len(shape) - 1) < HD


def _rowwise(name, fn, rows, consts, outs, accs=(), tm=512, rider=None):
    n_rows = None
    in_arrays, in_specs = [], []
    for r in rows:
        if isinstance(r, tuple):
            arr, w, cb = r
            spec = pl.BlockSpec((tm, w), functools.partial(lambda i, cb: (i, cb), cb=cb))
        else:
            arr = r
            spec = pl.BlockSpec((tm, arr.shape[1]), lambda i: (i, 0))
        n_rows = arr.shape[0]
        in_arrays.append(arr)
        in_specs.append(spec)
    for c in consts:
        in_arrays.append(c)
        in_specs.append(pl.BlockSpec(c.shape, functools.partial(lambda i, n: (0,) * n, n=c.ndim)))
    out_shape = [_sds(s, d) for s, d in outs] + [_sds(s, d) for s, d in accs]
    out_specs = [pl.BlockSpec((tm, s[1]), lambda i: (i, 0)) for s, _ in outs]
    out_specs += [pl.BlockSpec(s, functools.partial(lambda i, n: (0,) * n, n=len(s))) for s, _ in accs]

    def body(*refs):
        fn(pl.program_id(0), *refs)

    res = _call(body, name=name, out_shape=out_shape, grid=(n_rows // tm,), in_specs=in_specs,
                out_specs=out_specs, rider=rider)(*in_arrays)
    return res


def rms_fwd(x, gain, name, rider=None):
    def fn(i, x_ref, g_ref, h_ref):
        xv = x_ref[...]
        r = lax.rsqrt(jnp.mean(xv * xv, axis=-1, keepdims=True) + EPS)
        h_ref[...] = (xv * r * g_ref[...]).astype(h_ref.dtype)

    res = _rowwise(name, fn, [x], [gain], [(x.shape, BF16)], tm=1024, rider=rider)
    return res[0] if rider is None else (res[0][0], res[1])


def rms_bwd(dhs, x, gain, dx_in, name):
    n = len(dhs)

    def fn(i, *refs):
        dh_refs, (x_ref, dxin_ref, g_ref, dx_ref, dg_ref) = refs[:n], refs[n:]
        dh = dh_refs[0][...]
        for r in dh_refs[1:]:
            dh = dh + r[...]
        xv = x_ref[...]
        r = lax.rsqrt(jnp.mean(xv * xv, axis=-1, keepdims=True) + EPS)
        xn = xv * r
        dxn = dh * g_ref[...]
        dx_ref[...] = dxin_ref[...] + r * (dxn - xn * jnp.mean(dxn * xn, axis=-1, keepdims=True))

        @pl.when(i == 0)
        def _():
            dg_ref[...] = jnp.zeros_like(dg_ref)

        dg_ref[...] += jnp.sum(dh * xn, axis=0, keepdims=True)

    return _rowwise(name, fn, list(dhs) + [x, dx_in], [gain], [(x.shape, F32)], [((1, x.shape[1]), F32)])


def matmul_nn(a, b, name, out_dtype, tm, tn, res=None, scale=1.0, rider=None):
    s, k = a.shape
    n = b.shape[1]

    def body(*refs):
        if res is None:
            a_ref, b_ref, o_ref = refs
            o_ref[...] = _dot(a_ref[...], b_ref[...]).astype(o_ref.dtype)
        else:
            a_ref, b_ref, r_ref, o_ref = refs
            o_ref[...] = (r_ref[...] + scale * _dot(a_ref[...], b_ref[...])).astype(o_ref.dtype)

    in_specs = [pl.BlockSpec((tm, k), lambda i, j: (i, 0)), pl.BlockSpec((k, tn), lambda i, j: (0, j))]
    args = [a, b]
    if res is not None:
        in_specs.append(pl.BlockSpec((tm, tn), lambda i, j: (i, j)))
        args.append(res)
    return _call(body, name=name, out_shape=_sds((s, n), out_dtype), grid=(s // tm, n // tn), in_specs=in_specs,
                 out_specs=pl.BlockSpec((tm, tn), lambda i, j: (i, j)), rider=rider)(*args)


def matmul_nt(a, b, name, out_dtype, tm, tn, tk, rider=None):
    s, k = a.shape
    n = b.shape[0]
    nk = k // tk

    def body(a_ref, b_ref, o_ref, acc_ref):
        kk = pl.program_id(2)

        @pl.when(kk == 0)
        def _():
            acc_ref[...] = jnp.zeros_like(acc_ref)

        acc_ref[...] += _dot_nt(a_ref[...].astype(BF16), b_ref[...])

        @pl.when(kk == nk - 1)
        def _():
            o_ref[...] = acc_ref[...].astype(o_ref.dtype)

    return _call(body, name=name, out_shape=_sds((s, n), out_dtype), grid=(s // tm, n // tn, nk),
                 in_specs=[pl.BlockSpec((tm, tk), lambda i, j, kk: (i, kk)),
                           pl.BlockSpec((tn, tk), lambda i, j, kk: (j, kk))],
                 out_specs=pl.BlockSpec((tm, tn), lambda i, j, kk: (i, j)),
                 scratch_shapes=[pltpu.VMEM((tm, tn), F32)], rider=rider)(a, b)


def matmul_tn(a, b, name, tn, ts, a_scale=None, b_scale=None, rider=None):
    s, m = a.shape
    n = b.shape[1]
    ns = s // ts

    def body(a_ref, b_ref, o_ref, acc_ref):
        ss = pl.program_id(1)

        @pl.when(ss == 0)
        def _():
            acc_ref[...] = jnp.zeros_like(acc_ref)

        av, bv = a_ref[...], b_ref[...]
        if a_scale is not None:
            av = av * a_scale
        if b_scale is not None:
            bv = bv * b_scale
        acc_ref[...] += _dot_tn(av.astype(BF16), bv.astype(BF16))

        @pl.when(ss == ns - 1)
        def _():
            o_ref[...] = acc_ref[...].astype(o_ref.dtype)

    return _call(body, name=name, out_shape=_sds((m, n), BF16), grid=(n // tn, ns),
                 in_specs=[pl.BlockSpec((ts, m), lambda j, ss: (ss, 0)), pl.BlockSpec((ts, tn), lambda j, ss: (ss, j))],
                 out_specs=pl.BlockSpec((m, tn), lambda j, ss: (0, j)),
                 scratch_shapes=[pltpu.VMEM((m, tn), F32)], rider=rider)(a, b)


def _piece_specs(pieces, tile, rows_tile, tile_axis_first):
    specs, ranges, t0 = [], [], 0
    for a in pieces:
        n = a.shape[1] // tile

        def index(*ids, t0=t0, n=n):
            t, r = (ids[0], ids[1]) if tile_axis_first else (ids[2], ids[0])
            on = jnp.logical_and(t >= t0, t < t0 + n)
            return jnp.where(on, r, 0), jnp.clip(t - t0, 0, n - 1)

        specs.append(pl.BlockSpec((rows_tile, tile), index))
        ranges.append((t0, n))
        t0 += n
    return specs, ranges


def matmul_tn_pieces(a, pieces, name, tn, ts, rider=None):
    s, m = a.shape
    ns = s // ts
    specs, ranges = _piece_specs(pieces, tn, ts, True)
    n_total = sum(n for _, n in ranges)

    def body(a_ref, *refs):
        b_refs, o_ref, acc_ref = refs[:len(pieces)], refs[-2], refs[-1]
        j, ss = pl.program_id(0), pl.program_id(1)

        @pl.when(ss == 0)
        def _():
            acc_ref[...] = jnp.zeros_like(acc_ref)

        for b_ref, (t0, n) in zip(b_refs, ranges):
            @pl.when(jnp.logical_and(j >= t0, j < t0 + n))
            def _(b_ref=b_ref):
                acc_ref[...] += _dot_tn(a_ref[...], b_ref[...])

        @pl.when(ss == ns - 1)
        def _():
            o_ref[...] = acc_ref[...].astype(o_ref.dtype)

    return _call(body, name=name, out_shape=_sds((m, n_total * tn), BF16), grid=(n_total, ns),
                 in_specs=[pl.BlockSpec((ts, m), lambda j, ss: (ss, 0))] + specs,
                 out_specs=pl.BlockSpec((m, tn), lambda j, ss: (0, j)),
                 scratch_shapes=[pltpu.VMEM((m, tn), F32)], rider=rider)(a, *pieces)


def matmul_nt_pieces(pieces, b, name, out_dtype, tm, tn, tk, rider=None):
    s = pieces[0].shape[0]
    n = b.shape[0]
    specs, ranges = _piece_specs(pieces, tk, tm, False)
    nk = sum(cnt for _, cnt in ranges)

    def body(*refs):
        a_refs, b_ref, o_ref, acc_ref = refs[:len(pieces)], refs[-3], refs[-2], refs[-1]
        kk = pl.program_id(2)

        @pl.when(kk == 0)
        def _():
            acc_ref[...] = jnp.zeros_like(acc_ref)

        for a_ref, (t0, cnt) in zip(a_refs, ranges):
            @pl.when(jnp.logical_and(kk >= t0, kk < t0 + cnt))
            def _(a_ref=a_ref):
                acc_ref[...] += _dot_nt(a_ref[...], b_ref[...])

        @pl.when(kk == nk - 1)
        def _():
            o_ref[...] = acc_ref[...].astype(o_ref.dtype)

    return _call(body, name=name, out_shape=_sds((s, n), out_dtype), grid=(s // tm, n // tn, nk),
                 in_specs=specs + [pl.BlockSpec((tn, tk), lambda i, j, kk: (j, kk))],
                 out_specs=pl.BlockSpec((tm, tn), lambda i, j, kk: (i, j)),
                 scratch_shapes=[pltpu.VMEM((tm, tn), F32)], rider=rider)(*pieces, b)


def ffn_up(h, w704, gate_blk, up_blk, name, tm=512, rider=None):
    s = h.shape[0]

    def body(h_ref, wg_ref, wu_ref, g_ref, u_ref, a_ref):
        hv = h_ref[...]
        g = _dot(hv, wg_ref[...])
        u = _dot(hv, wu_ref[...])
        g_ref[...] = g.astype(BF16)
        u_ref[...] = u.astype(BF16)
        a_ref[...] = (g * _sigmoid(g) * u).astype(BF16)

    ospec = pl.BlockSpec((None, tm, FF_SHARD), lambda j, i: (j, i, 0))
    shp = _sds((N_CHIP, s, FF_SHARD), BF16)
    return _call(body, name=name, out_shape=[shp, shp, shp], grid=(N_CHIP, s // tm),
                 in_specs=[pl.BlockSpec((tm, D_MODEL), lambda j, i: (i, 0)),
                           pl.BlockSpec((None, D_MODEL, FF_SHARD), lambda j, i: (j, gate_blk, 0)),
                           pl.BlockSpec((None, D_MODEL, FF_SHARD), lambda j, i: (j, up_blk, 0))],
                 out_specs=[ospec, ospec, ospec], rider=rider)(h, w704, w704)


def ffn_down(a, w1024, blk, x, name, tm=512, rider=None, target=None):
    s = x.shape[0]

    def block_out(a_ref, wd_ref, x_ref):
        acc = _dot(a_ref[0], wd_ref[0])
        for j in range(1, N_CHIP):
            acc += _dot(a_ref[j], wd_ref[j])
        return x_ref[...] + 0.5 * acc

    def body(a_ref, wd_ref, x_ref, o_ref):
        o_ref[...] = block_out(a_ref, wd_ref, x_ref)

    def loss_body(a_ref, wd_ref, x_ref, t_ref, dy_ref, sq_ref):
        err = block_out(a_ref, wd_ref, x_ref) - t_ref[...]
        dy_ref[...] = err * (1.0 / D_MODEL)

        @pl.when(pl.program_id(0) == 0)
        def _():
            sq_ref[...] = jnp.zeros_like(sq_ref)

        sq_ref[...] += jnp.sum(err * err, axis=0, keepdims=True)

    rows = pl.BlockSpec((tm, D_MODEL), lambda i: (i, 0))
    in_specs = [pl.BlockSpec((N_CHIP, tm, FF_SHARD), lambda i: (0, i, 0)),
                pl.BlockSpec((N_CHIP, FF_SHARD, D_MODEL), lambda i: (0, blk, 0)), rows]
    if target is None:
        return _call(body, name=name, out_shape=_sds((s, D_MODEL), F32), grid=(s // tm,), in_specs=in_specs,
                     out_specs=rows, rider=rider)(a, w1024, x)
    return _call(loss_body, name=name, out_shape=[_sds((s, D_MODEL), F32), _sds((1, D_MODEL), F32)], grid=(s // tm,),
                 in_specs=in_specs + [rows], out_specs=[rows, pl.BlockSpec((1, D_MODEL), lambda i: (0, 0))],
                 rider=rider)(a, w1024, x, target)


def ffn_bwd_hidden(dx, w1024, blk, g, u, name, tm=1024, rider=None):
    s = dx.shape[0]

    def body(dx_ref, wd_ref, g_ref, u_ref, dg_ref, du_ref):
        dy = (0.5 * dx_ref[...]).astype(BF16)
        da = _dot_nt(dy, wd_ref[...])
        gv = g_ref[...].astype(F32)
        uv = u_ref[...].astype(F32)
        sg = _sigmoid(gv)
        dg_ref[...] = (da * uv * (sg * (1.0 + gv * (1.0 - sg)))).astype(BF16)
        du_ref[...] = (da * gv * sg).astype(BF16)

    hspec = pl.BlockSpec((None, tm, FF_SHARD), lambda j, i: (j, i, 0))
    shp = _sds((N_CHIP, s, FF_SHARD), BF16)
    return _call(body, name=name, out_shape=[shp, shp], grid=(N_CHIP, s // tm),
                 in_specs=[pl.BlockSpec((tm, D_MODEL), lambda j, i: (i, 0)),
                           pl.BlockSpec((None, FF_SHARD, D_MODEL), lambda j, i: (j, blk, 0)), hspec, hspec],
                 out_specs=[hspec, hspec], rider=rider)(dx, w1024, g, u)


def ffn_bwd_input(dg, du, w704, gate_blk, up_blk, x, gain, dy, name, tm=512, rider=None):
    s = dg.shape[1]

    def body(dg_ref, du_ref, wg_ref, wu_ref, x_ref, gain_ref, dy_ref, dx_ref, dgain_ref):
        dh = _dot_nt(dg_ref[0], wg_ref[0]) + _dot_nt(du_ref[0], wu_ref[0])
        for j in range(1, N_CHIP):
            dh += _dot_nt(dg_ref[j], wg_ref[j]) + _dot_nt(du_ref[j], wu_ref[j])
        xv = x_ref[...]
        r = lax.rsqrt(jnp.mean(xv * xv, axis=-1, keepdims=True) + EPS)
        xn = xv * r
        dxn = dh * gain_ref[...]
        dx_ref[...] = dy_ref[...] + r * (dxn - xn * jnp.mean(dxn * xn, axis=-1, keepdims=True))

        @pl.when(pl.program_id(0) == 0)
        def _():
            dgain_ref[...] = jnp.zeros_like(dgain_ref)

        dgain_ref[...] += jnp.sum(dh * xn, axis=0, keepdims=True)

    hspec = pl.BlockSpec((N_CHIP, tm, FF_SHARD), lambda i: (0, i, 0))
    rows = pl.BlockSpec((tm, D_MODEL), lambda i: (i, 0))
    whole = pl.BlockSpec((1, D_MODEL), lambda i: (0, 0))
    return _call(body, name=name, out_shape=[_sds((s, D_MODEL), F32), _sds((1, D_MODEL), F32)], grid=(s // tm,),
                 in_specs=[hspec, hspec,
                           pl.BlockSpec((N_CHIP, D_MODEL, FF_SHARD), lambda i: (0, gate_blk, 0), pl.Buffered(1)),
                           pl.BlockSpec((N_CHIP, D_MODEL, FF_SHARD), lambda i: (0, up_blk, 0), pl.Buffered(1)),
                           rows, whole, rows],
                 out_specs=[rows, whole], rider=rider)(dg, du, w704, w704, x, gain, dy)


def ffn_wgrad_in(h, dgu, name, ts=2048, rider=None):
    s = h.shape[0]
    ns = s // ts

    def body(h_ref, d_ref, o_ref, acc_ref):
        ss = pl.program_id(1)

        @pl.when(ss == 0)
        def _():
            acc_ref[...] = jnp.zeros_like(acc_ref)

        acc_ref[...] += _dot_tn(h_ref[...], d_ref[...])

        @pl.when(ss == ns - 1)
        def _():
            o_ref[...] = acc_ref[...].astype(BF16)

    return _call(body, name=name, out_shape=_sds((N_CHIP, D_MODEL, FF_SHARD), BF16), grid=(N_CHIP, ns),
                 in_specs=[pl.BlockSpec((ts, D_MODEL), lambda j, ss: (ss, 0)),
                           pl.BlockSpec((None, ts, FF_SHARD), lambda j, ss: (j, ss, 0))],
                 out_specs=pl.BlockSpec((None, D_MODEL, FF_SHARD), lambda j, ss: (j, 0, 0)),
                 scratch_shapes=[pltpu.VMEM((D_MODEL, FF_SHARD), F32)], rider=rider)(h, dgu)


def ffn_wgrad_down(a, dx, name, ts=2048):
    s = dx.shape[0]
    ns = s // ts

    def body(a_ref, dx_ref, o_ref, acc_ref):
        ss = pl.program_id(1)

        @pl.when(ss == 0)
        def _():
            acc_ref[...] = jnp.zeros_like(acc_ref)

        acc_ref[...] += _dot_tn(a_ref[...], (0.5 * dx_ref[...]).astype(BF16))

        @pl.when(ss == ns - 1)
        def _():
            o_ref[...] = acc_ref[...].astype(BF16)

    return _call(body, name=name, out_shape=_sds((N_CHIP, FF_SHARD, D_MODEL), BF16), grid=(N_CHIP, ns),
                 in_specs=[pl.BlockSpec((None, ts, FF_SHARD), lambda j, ss: (j, ss, 0)),
                           pl.BlockSpec((ts, D_MODEL), lambda j, ss: (ss, 0))],
                 out_specs=pl.BlockSpec((None, FF_SHARD, D_MODEL), lambda j, ss: (j, 0, 0)),
                 scratch_shapes=[pltpu.VMEM((FF_SHARD, D_MODEL), F32)])(a, dx)


def ffn_forward(x, gain, get_w704, get_w1024, tag, rms_rider=None, up_rider=None, down_rider=None, target=None):
    h = rms_fwd(x, gain, f"{tag}_rms", rider=rms_rider)
    h, rode_rms = h if rms_rider is not None else (h, None)
    res = ffn_up(h, get_w704(rode_rms), 0, 1, f"{tag}_up", tm=1024 if up_rider is None else 512, rider=up_rider)
    (g, u, a), rode_up = res if up_rider is not None else (res, None)
    y = ffn_down(a, get_w1024(rode_up), 0, x, f"{tag}_down", tm=1024 if down_rider is None else 512,
                 rider=down_rider, target=target)
    y, rode_down = y if down_rider is not None else (y, None)
    return y, (h, g, u, a), rode_rms, rode_up, rode_down


def ffn_backward(dy, x, gain, w704, w1024, saved, tag, hidden_rider=None, ride_down=None, ride_in=None):
    h, g, u, a = saved
    d_wd = ffn_wgrad_down(a, dy, f"{tag}_dwd")
    rode_hidden = None
    if hidden_rider is not None:
        (dg, du), rode_hidden = ffn_bwd_hidden(dy, w1024, 0, g, u, f"{tag}_dhid", tm=512, rider=hidden_rider)
    else:
        dg, du = ffn_bwd_hidden(dy, w1024, 0, g, u, f"{tag}_dhid")
    if ride_down is not None:
        d_wg, d_wd = ffn_wgrad_in(h, dg, f"{tag}_dwg", rider=ride_down(d_wd))
    else:
        d_wg = ffn_wgrad_in(h, dg, f"{tag}_dwg")
    d_win = jnp.concatenate([d_wg, ffn_wgrad_in(h, du, f"{tag}_dwu")], axis=1)
    if ride_in is not None:
        (dx, d_gain), d_win = ffn_bwd_input(dg, du, w704, 0, 1, x, gain, dy, f"{tag}_dh", tm=256,
                                            rider=ride_in(d_win))
    else:
        dx, d_gain = ffn_bwd_input(dg, du, w704, 0, 1, x, gain, dy, f"{tag}_dh")
    return dx, d_gain, d_win, d_wd, rode_hidden


def _alibi_slope(head):
    return float(2.0 ** (-ALIBI_MAX_EXP * (head + 1) / N_ATTN_HEADS))


def _same_head():
    row = lax.broadcasted_iota(jnp.int32, (2 * HD, 2 * HD), 0)
    col = lax.broadcasted_iota(jnp.int32, (2 * HD, 2 * HD), 1)
    return ((row < HD) == (col < HD)).astype(BF16)


def _head_sums(x, same_head):
    hi = x.astype(BF16)
    lo = (x - hi.astype(F32)).astype(BF16)
    return _dot(hi, same_head) + _dot(lo, same_head)


def _head_norm(t, gain_pair, same_head):
    r = lax.rsqrt(_head_sums(t * t, same_head) * (1.0 / HD) + EPS)
    return t * r * gain_pair, r


def qk_norm_fwd(p, q_gain, k_gain, name):
    s = p.shape[0]

    def fn(i, q_ref, k_ref, qg_ref, kg_ref, qn_ref, kn_ref):
        same_head = _same_head()
        for src, g_ref, dst in ((q_ref, qg_ref, qn_ref), (k_ref, kg_ref, kn_ref)):
            for pr in range(ATTN_QKV // (2 * HD)):
                cols = slice(pr * 2 * HD, (pr + 1) * 2 * HD)
                y, _ = _head_norm(src[:, cols].astype(F32), g_ref[...], same_head)
                dst[:, cols] = y.astype(BF16)

    return _rowwise(name, fn, [(p, ATTN_QKV, 0), (p, ATTN_QKV, 1)], [q_gain, k_gain],
                    [((s, ATTN_QKV), BF16), ((s, ATTN_QKV), BF16)])


def qk_norm_bwd(p, dqs, dks, q_gain, k_gain, name):
    s = p.shape[0]
    pairs_per_pattern = GROUP_W // (2 * HD)

    def fn(i, q_ref, k_ref, dq0, dq1, dq2, dk0, dk1, dk2, qg_ref, kg_ref, dqk_ref, dqg_ref, dkg_ref):
        same_head = _same_head()

        @pl.when(i == 0)
        def _():
            dqg_ref[...] = jnp.zeros_like(dqg_ref)
            dkg_ref[...] = jnp.zeros_like(dkg_ref)

        for src, d_refs, g_ref, dst, dg_ref in (
                (q_ref, (dq0, dq1, dq2), qg_ref, dqk_ref.at[:, 0:ATTN_QKV], dqg_ref),
                (k_ref, (dk0, dk1, dk2), kg_ref, dqk_ref.at[:, ATTN_QKV:2 * ATTN_QKV], dkg_ref)):
            for pr in range(ATTN_QKV // (2 * HD)):
                cols = slice(pr * 2 * HD, (pr + 1) * 2 * HD)
                t = src[:, cols].astype(F32)
                r = lax.rsqrt(_head_sums(t * t, same_head) * (1.0 / HD) + EPS)
                xn = t * r
                within = (pr % pairs_per_pattern) * 2 * HD
                dy = d_refs[pr // pairs_per_pattern][:, within:within + 2 * HD]
                dg_ref[:, cols] += jnp.sum(dy * xn, axis=0, keepdims=True)
                dxn = dy * g_ref[...]
                mean = _head_sums(dxn * xn, same_head) * (1.0 / HD)
                dst[:, cols] = (r * (dxn - xn * mean)).astype(BF16)

    return _rowwise(name, fn, [(p, ATTN_QKV, 0), (p, ATTN_QKV, 1)] + list(dqs) + list(dks), [q_gain, k_gain],
                    [((s, 2 * ATTN_QKV), BF16)], [((1, ATTN_QKV), F32), ((1, ATTN_QKV), F32)])


def _to_streams(a, d):
    if d == 1:
        return a
    s, c = a.shape
    return a.reshape(s // d, d, c).transpose(1, 0, 2).reshape(s, c)


def _from_streams(a, d):
    if d == 1:
        return a
    s, c = a.shape
    return a.reshape(d, s // d, c).transpose(1, 0, 2).reshape(s, c)


def attn_merge_fwd(os_, lses, name):
    s = os_[0].shape[0]

    def fn(i, o0, o1, o2, l0, l1, l2, out_ref):
        m = jnp.maximum(jnp.maximum(l0[...], l1[...]), l2[...])
        e0, e1, e2 = jnp.exp(l0[...] - m), jnp.exp(l1[...] - m), jnp.exp(l2[...] - m)
        inv = 1.0 / (e0 + e1 + e2)
        out_ref[...] = ((e0 * inv) * o0[...] + (e1 * inv) * o1[...] + (e2 * inv) * o2[...]).astype(BF16)

    return _rowwise(name, fn, list(os_) + list(lses), [], [((s, GROUP_W), BF16)], tm=1024)[0]


def attn_merge_bwd(d_out, os_, lses, name):
    s = d_out.shape[0]

    def fn(i, do_ref, o0, o1, o2, l0, l1, l2, d0, d1, d2, c0, c1, c2):
        m = jnp.maximum(jnp.maximum(l0[...], l1[...]), l2[...])
        e0, e1, e2 = jnp.exp(l0[...] - m), jnp.exp(l1[...] - m), jnp.exp(l2[...] - m)
        inv = 1.0 / (e0 + e1 + e2)
        w0, w1, w2 = e0 * inv, e1 * inv, e2 * inv
        do = do_ref[...]
        prod = do * (w0 * o0[...] + w1 * o1[...] + w2 * o2[...])
        same_head = _same_head()
        for pr in range(GROUP_W // (2 * HD)):
            cols = slice(pr * 2 * HD, (pr + 1) * 2 * HD)
            t = _head_sums(prod[:, cols], same_head)
            for w, c_ref in ((w0, c0), (w1, c1), (w2, c2)):
                c_ref[:, cols] = w[:, cols] * t
        for w, d_ref in ((w0, d0), (w1, d1), (w2, d2)):
            d_ref[...] = (w * do).astype(BF16)

    shp = (s, GROUP_W)
    return _rowwise(name, fn, [d_out] + list(os_) + list(lses), [],
                    [(shp, BF16)] * 3 + [(shp, F32)] * 3)


def _band_constants(d):
    row = lax.broadcasted_iota(jnp.int32, (2 * BLK, 2 * BLK), 0)
    col = lax.broadcasted_iota(jnp.int32, (2 * BLK, 2 * BLK), 1)
    rel = BLK + jnp.where(row >= BLK, row - BLK, row) - col
    band = jnp.logical_and(rel >= 0, rel <= BLK)
    return (rel * d).astype(F32), band, (col >= BLK).astype(jnp.int32)


def _stack_heads(x, first):
    zero = jnp.zeros_like(x)
    return jnp.concatenate([jnp.where(first, x, zero), jnp.where(first, zero, x)], axis=0)


def _unstack_heads(x2, first):
    return jnp.where(first, x2[:BLK], x2[BLK:])


def _head_column(x):
    return jnp.concatenate([x[:, 0:1], x[:, HD:HD + 1]], axis=0)


def attn_fwd(q, k, v, pattern, name, tq=1024):
    s = q.shape[0]
    d = ATTN_DILATIONS[pattern]
    blocks_per_stream = (s // d) // BLK
    nsb = tq // BLK

    def body(q_ref, k_ref, v_ref, kp_ref, vp_ref, o_ref, l_ref):
        i = pl.program_id(0)
        rel_f, band, own = _band_constants(d)
        first = _lane_first_half((BLK, 2 * HD))
        upper = lax.broadcasted_iota(jnp.int32, (2 * BLK, 1), 0) < BLK
        for sb in range(nsb):
            rows = slice(sb * BLK, (sb + 1) * BLK)
            has_prev = ((i * nsb + sb) % blocks_per_stream != 0).astype(jnp.int32)
            mask = jnp.logical_and(band, (own + has_prev) > 0)
            for pr in range(GROUP_W // (2 * HD)):
                cols = slice(pr * 2 * HD, (pr + 1) * 2 * HD)
                if sb == 0:
                    kcat = jnp.concatenate([kp_ref[:, cols], k_ref[rows, cols]], axis=0)
                    vcat = jnp.concatenate([vp_ref[:, cols], v_ref[rows, cols]], axis=0)
                else:
                    both = slice((sb - 1) * BLK, (sb + 1) * BLK)
                    kcat, vcat = k_ref[both, cols], v_ref[both, cols]
                h0 = pattern * HEADS_PER_PATTERN + 2 * pr
                slope = jnp.where(upper, _alibi_slope(h0), _alibi_slope(h0 + 1))
                sc = _dot_nt(_stack_heads(q_ref[rows, cols], first), kcat) * 0.125 - slope * rel_f
                sc = jnp.where(mask, sc, NEG)
                m = jnp.max(sc, axis=-1, keepdims=True)
                p = jnp.exp(sc - m)
                l = jnp.sum(p, axis=-1, keepdims=True)
                o2 = _dot((p * (1.0 / l)).astype(BF16), vcat)
                o_ref[rows, cols] = _unstack_heads(o2, first)
                lse = m + jnp.log(l)
                l_ref[rows, cols] = jnp.where(first, lse[:BLK], lse[BLK:])

    cur = pl.BlockSpec((tq, GROUP_W), lambda i: (i, 0))
    prev = pl.BlockSpec((BLK, GROUP_W), lambda i: (jnp.maximum(i * nsb - 1, 0), 0))
    return _call(body, name=name, out_shape=[_sds((s, GROUP_W), F32), _sds((s, GROUP_W), F32)], grid=(s // tq,),
                 in_specs=[cur, cur, cur, prev, prev], out_specs=[cur, cur])(q, k, v, k, v)


def attn_bwd(q, k, v, d_o, cterm, lse, pattern, name, tq=1024):
    s = q.shape[0]
    d = ATTN_DILATIONS[pattern]
    blocks_per_stream = (s // d) // BLK
    nsb = tq // BLK
    n_blocks = s // BLK

    def body(q_ref, k_ref, v_ref, do_ref, c_ref, l_ref, kp_ref, vp_ref, qn_ref, kn_ref, vn_ref, don_ref, cn_ref,
             ln_ref, dq_ref, dk_ref, dv_ref):
        i = pl.program_id(0)
        rel_f, band, own = _band_constants(d)
        first = _lane_first_half((BLK, 2 * HD))
        upper = lax.broadcasted_iota(jnp.int32, (2 * BLK, 1), 0) < BLK
        dk_ref[...] = jnp.zeros_like(dk_ref)
        dv_ref[...] = jnp.zeros_like(dv_ref)
        for sb in range(nsb + 1):
            gb = i * nsb + sb
            rows = slice(sb * BLK, (sb + 1) * BLK)
            before = slice((sb - 1) * BLK, sb * BLK)
            inside = (gb < n_blocks).astype(jnp.int32)
            has_prev = jnp.logical_and(gb % blocks_per_stream != 0, gb < n_blocks).astype(jnp.int32)
            mask = jnp.logical_and(band, (own * inside + has_prev) > 0)
            for pr in range(GROUP_W // (2 * HD)):
                cols = slice(pr * 2 * HD, (pr + 1) * 2 * HD)
                if sb == 0:
                    kcat = jnp.concatenate([kp_ref[:, cols], k_ref[rows, cols]], axis=0)
                    vcat = jnp.concatenate([vp_ref[:, cols], v_ref[rows, cols]], axis=0)
                elif sb == nsb:
                    kcat = jnp.concatenate([k_ref[before, cols], kn_ref[:, cols]], axis=0)
                    vcat = jnp.concatenate([v_ref[before, cols], vn_ref[:, cols]], axis=0)
                else:
                    both = slice((sb - 1) * BLK, (sb + 1) * BLK)
                    kcat, vcat = k_ref[both, cols], v_ref[both, cols]
                if sb < nsb:
                    qp, dop, cp, lp = q_ref[rows, cols], do_ref[rows, cols], c_ref[rows, cols], l_ref[rows, cols]
                else:
                    qp, dop, cp, lp = qn_ref[:, cols], don_ref[:, cols], cn_ref[:, cols], ln_ref[:, cols]
                h0 = pattern * HEADS_PER_PATTERN + 2 * pr
                slope = jnp.where(upper, _alibi_slope(h0), _alibi_slope(h0 + 1))
                q2 = _stack_heads(qp, first)
                do2 = _stack_heads(dop, first)
                sc = jnp.where(mask, _dot_nt(q2, kcat) * 0.125 - slope * rel_f, NEG)
                pm = jnp.exp(sc - _head_column(lp))
                dl = (pm * (_dot_nt(do2, vcat) - _head_column(cp))).astype(BF16)
                if sb < nsb:
                    dq_ref[rows, cols] = _unstack_heads(_dot(dl, kcat), first) * 0.125
                dk2 = _dot_tn(dl, q2) * 0.125
                dv2 = _dot_tn(pm.astype(BF16), do2)
                if sb > 0:
                    dk_ref[before, cols] += dk2[:BLK]
                    dv_ref[before, cols] += dv2[:BLK]
                if sb < nsb:
                    dk_ref[rows, cols] += dk2[BLK:]
                    dv_ref[rows, cols] += dv2[BLK:]

    cur = pl.BlockSpec((tq, GROUP_W), lambda i: (i, 0))
    prev = pl.BlockSpec((BLK, GROUP_W), lambda i: (jnp.maximum(i * nsb - 1, 0), 0))
    nxt = pl.BlockSpec((BLK, GROUP_W), lambda i: (jnp.minimum((i + 1) * nsb, n_blocks - 1), 0))
    shp = _sds((s, GROUP_W), F32)
    return _call(body, name=name, out_shape=[shp, shp, shp], grid=(s // tq,),
                 in_specs=[cur] * 6 + [prev, prev] + [nxt] * 6, out_specs=[cur, cur, cur])(
                     q, k, v, d_o, cterm, lse, k, v, q, k, v, d_o, cterm, lse)


HALO = 16
CONV_TQ = 1024


def conv_fwd(p, w, b, name):
    s = p.shape[0]
    tq = CONV_TQ
    ncol = SSD_CONV_DIM // GROUP_W
    cb0 = COL_XBC // GROUP_W

    def body(u_ref, up_ref, w_ref, b_ref, c_ref, xc_ref):
        i = pl.program_id(0)
        prev = (up_ref[...].astype(F32) * (i > 0).astype(F32)).astype(BF16)
        ext = jnp.concatenate([prev, u_ref[...]], axis=0)
        row = lax.broadcasted_iota(jnp.int32, (BLK, BLK + HALO), 0)
        col = lax.broadcasted_iota(jnp.int32, (BLK, BLK + HALO), 1)
        for blk in range(tq // BLK):
            lead = ext[blk * BLK:blk * BLK + BLK + HALO]
            acc = b_ref[...] + w_ref[SSD_CONV - 1:SSD_CONV, :] * lead[HALO:].astype(F32)
            for kk in range(SSD_CONV - 1):
                pick = (col == row + HALO - (SSD_CONV - 1 - kk)).astype(BF16)
                acc += w_ref[kk:kk + 1, :] * _dot(pick, lead)
            rows = slice(blk * BLK, (blk + 1) * BLK)
            c_ref[rows, :] = acc.astype(BF16)
            xc_ref[rows, :] = (acc * _sigmoid(acc)).astype(BF16)

    cur_in = pl.BlockSpec((tq, GROUP_W), lambda i, j: (i, cb0 + j))
    prev_in = pl.BlockSpec((HALO, GROUP_W), lambda i, j: (jnp.maximum(i * (tq // HALO) - 1, 0), cb0 + j))
    cur_out = pl.BlockSpec((tq, GROUP_W), lambda i, j: (i, j))
    shp = _sds((s, SSD_CONV_DIM), BF16)
    return _call(body, name=name, out_shape=[shp, shp], grid=(s // tq, ncol),
                 in_specs=[cur_in, prev_in, pl.BlockSpec((SSD_CONV, GROUP_W), lambda i, j: (0, j)),
                           pl.BlockSpec((1, GROUP_W), lambda i, j: (0, j))],
                 out_specs=[cur_out, cur_out])(p, p, w, b)


def conv_bwd(p, cpre, dxs, d_b, d_c, w, name):
    s = p.shape[0]
    tq = CONV_TQ
    ncol = SSD_CONV_DIM // GROUP_W
    n_xs = SSD_INNER // GROUP_W
    cb0 = COL_XBC // GROUP_W
    nt = s // tq

    def body(u_ref, c_ref, cn_ref, dx_ref, dxn_ref, dbm_ref, dbmn_ref, dcm_ref, dcmn_ref, w_ref,
             du_ref, dw_ref, db_ref):
        j, i = pl.program_id(0), pl.program_id(1)

        def dpre(c16, dx):
            c = c16.astype(F32)
            sg = _sigmoid(c)
            return dx * (sg * (1.0 + c * (1.0 - sg)))

        def pick(a_ref, b_ref, c_ref_):
            return jnp.where(j < n_xs, a_ref[...], jnp.where(j == n_xs, b_ref[...], c_ref_[...]))

        dc = dpre(c_ref[...], pick(dx_ref, dbm_ref, dcm_ref))
        dcn = dpre(cn_ref[...], pick(dxn_ref, dbmn_ref, dcmn_ref)) * (i < nt - 1).astype(F32)
        dext = jnp.concatenate([dc, dcn], axis=0)
        u = u_ref[...].astype(F32)

        @pl.when(i == 0)
        def _():
            dw_ref[...] = jnp.zeros_like(dw_ref)
            db_ref[...] = jnp.zeros_like(db_ref)

        du = w_ref[SSD_CONV - 1:SSD_CONV, :] * dc
        dw_ref[SSD_CONV - 1:SSD_CONV, :] += jnp.sum(dc * u, axis=0, keepdims=True)
        for kk in range(SSD_CONV - 1):
            sh = SSD_CONV - 1 - kk
            ahead = pltpu.roll(dext, tq + HALO - sh, 0)[0:tq]
            du += w_ref[kk:kk + 1, :] * ahead
            dw_ref[kk:kk + 1, :] += jnp.sum(ahead * u, axis=0, keepdims=True)
        du_ref[...] = du.astype(BF16)
        db_ref[...] += jnp.sum(dc, axis=0, keepdims=True)

    hb = tq // HALO
    cur_p = pl.BlockSpec((tq, GROUP_W), lambda j, i: (i, cb0 + j))
    cur = pl.BlockSpec((tq, GROUP_W), lambda j, i: (i, j))
    nxt = pl.BlockSpec((HALO, GROUP_W), lambda j, i: (jnp.minimum((i + 1) * hb, s // HALO - 1), j))

    def piece(first_tile, n_tiles):
        def on(j):
            return jnp.logical_and(j >= first_tile, j < first_tile + n_tiles)

        def col(j):
            return jnp.clip(j - first_tile, 0, n_tiles - 1)

        return (pl.BlockSpec((tq, GROUP_W), lambda j, i: (jnp.where(on(j), i, 0), col(j))),
                pl.BlockSpec((HALO, GROUP_W),
                             lambda j, i: (jnp.where(on(j), jnp.minimum((i + 1) * hb, s // HALO - 1), 0), col(j))))

    return _call(body, name=name,
                 out_shape=[_sds((s, SSD_CONV_DIM), BF16), _sds((8, SSD_CONV_DIM), F32), _sds((1, SSD_CONV_DIM), F32)],
                 grid=(ncol, nt),
                 in_specs=[cur_p, cur, nxt, *piece(0, n_xs), *piece(n_xs, 1), *piece(n_xs + 1, 1),
                           pl.BlockSpec((SSD_CONV, GROUP_W), lambda j, i: (0, j))],
                 out_specs=[cur, pl.BlockSpec((8, GROUP_W), lambda j, i: (0, j)),
                            pl.BlockSpec((1, GROUP_W), lambda j, i: (0, j))])(
                                p, cpre, cpre, dxs, dxs, d_b, d_b, d_c, d_c, w)


def _softplus(x):
    return jnp.maximum(x, 0.0) + jnp.log(1.0 + jnp.exp(-jnp.abs(x)))


def _ssd_decays(dtr_ref, dtrt_ref, bias_ref, biast_ref, alog_ref, alogt_ref):
    row = lax.broadcasted_iota(jnp.int32, (BLK, BLK), 0)
    col = lax.broadcasted_iota(jnp.int32, (BLK, BLK), 1)
    lower = (row >= col).astype(F32)
    upper = (row <= col).astype(F32)
    dtb = dtr_ref[...] + bias_ref[...]
    dt = _softplus(dtb)
    a = dt * (-jnp.exp(alog_ref[...]))
    cs = _dot_hi(lower, a)
    a_t = _softplus(dtrt_ref[...] + biast_ref[...]) * (-jnp.exp(alogt_ref[...]))
    cs_t = _dot_hi(a_t, upper)
    return dtb, dt, cs, cs_t, row, col, upper


SSD_GROUPS_PER_STEP = 4


def _per_group(body, gps, kinds):
    def wrapped(*refs):
        for gi in range(gps):
            args, pos = [], 0
            for kind, n in kinds:
                if kind == "each":
                    args.append(refs[pos + gi])
                    pos += gps
                    continue
                ref = refs[pos]
                pos += 1
                if kind == "cols":
                    args.append(ref.at[:, gi * n:(gi + 1) * n])
                else:
                    args.append(ref.at[gi] if n == 1 else ref.at[pl.ds(gi * n, n)])
            body(*args)

    return wrapped


def ssd_fwd(p, xc, dtg, dtg_t, params, gn, name):
    s = p.shape[0]
    nc = s // BLK
    bias, bias_t, alog, alog_t, dskip = params

    def body(xs_ref, b_ref, c_ref, z_ref, dtr_ref, dtrt_ref, bias_ref, biast_ref, alog_ref, alogt_ref, dsk_ref,
             gn_ref, y_ref, sin_ref, hp_ref, h_ref):
        c_idx = pl.program_id(1)

        @pl.when(c_idx == 0)
        def _():
            h_ref[...] = jnp.zeros_like(h_ref)

        _, dt, cs, cs_t, row, col, _ = _ssd_decays(dtr_ref, dtrt_ref, bias_ref, biast_ref, alog_ref, alogt_ref)
        first = _lane_first_half((BLK, 2 * HD))
        first_row = _lane_first_half((1, 2 * HD))
        tril = row >= col
        b16, c16 = b_ref[...], c_ref[...]
        cb = _dot_nt(c16, b16)
        n_pairs = GROUP_W // (2 * HD)
        tot = cs[BLK - 1:BLK, :]
        exp_cs, exp_rest, exp_tot = jnp.exp(cs), jnp.exp(tot - cs), jnp.exp(tot)

        lanes_of_head = (lax.broadcasted_iota(jnp.int32, (8, GROUP_W), 1) // HD
                         == lax.broadcasted_iota(jnp.int32, (8, GROUP_W), 0)).astype(BF16)

        def per_head(v, mask):
            if v.shape[0] == 1:
                return jnp.concatenate([jnp.where(mask, v[:, 2 * pr:2 * pr + 1], v[:, 2 * pr + 1:2 * pr + 2])
                                        for pr in range(n_pairs)], axis=1)
            hi = v.astype(BF16)
            lo = (v - hi.astype(F32)).astype(BF16)
            return _dot(hi, lanes_of_head) + _dot(lo, lanes_of_head)

        xs = xs_ref[...].astype(F32)
        xt = xs * per_head(dt, first)
        xt16 = xt.astype(BF16)
        hstate = jnp.concatenate([h_ref[pr] for pr in range(n_pairs)], axis=1)
        for pr in range(n_pairs):
            hp_ref[pr] = h_ref[pr]
        y_off = per_head(exp_cs, first) * _dot(c16, hstate.astype(BF16))
        new = per_head(exp_tot, first_row) * hstate + _dot_tn(b16, (per_head(exp_rest, first) * xt).astype(BF16))
        for pr in range(n_pairs):
            h_ref[pr] = new[:, pr * 2 * HD:(pr + 1) * 2 * HD]
        y_diag = []
        for pr in range(n_pairs):
            cols = slice(pr * 2 * HD, (pr + 1) * 2 * HD)
            m2 = jnp.concatenate(
                [(cb * jnp.exp(jnp.where(tril, cs[:, h:h + 1] - cs_t[h:h + 1, :], NEG))).astype(BF16)
                 for h in (2 * pr, 2 * pr + 1)], axis=1)
            y_diag.append(_dot(m2, _stack_heads(xt16[:, cols], first)))
        y = jnp.concatenate(y_diag, axis=1) + y_off + xs * per_head(dsk_ref[...], first_row)
        y_ref[...] = y
        zv = z_ref[...].astype(F32)
        yz = y * (zv * _sigmoid(zv))
        r = lax.rsqrt(jnp.mean(yz * yz, axis=-1, keepdims=True) + EPS)
        sin_ref[...] = (yz * r * gn_ref[...]).astype(BF16)

    gps = SSD_GROUPS_PER_STEP
    wide, narrow, lead = ("cols", GROUP_W), ("cols", BLK), ("lead", 1)
    kinds = [wide, narrow, narrow, ("each", 0)] + [lead] * 7 + [wide, wide, wide, lead, ("lead", 4)]
    wide_w, narrow_w = GROUP_W * gps, BLK * gps
    gparam = pl.BlockSpec((gps, 1, 8), lambda g, c: (g, 0, 0))
    gparam_t = pl.BlockSpec((gps, 8, 1), lambda g, c: (g, 0, 0))
    z_specs = [pl.BlockSpec((BLK, GROUP_W), functools.partial(lambda g, c, gi: (c, COL_Z // GROUP_W + gps * g + gi),
                                                              gi=gi)) for gi in range(gps)]
    return _call(
        _per_group(body, gps, kinds), name=name,
        out_shape=[_sds((s, SSD_INNER), F32), _sds((s, SSD_INNER), BF16),
                   _sds((SSD_GROUPS, nc, 4, BLK, 2 * HD), F32)],
        grid=(SSD_GROUPS // gps, nc),
        in_specs=[pl.BlockSpec((BLK, wide_w), lambda g, c: (c, g)),
                  pl.BlockSpec((BLK, narrow_w), lambda g, c: (c, SSD_INNER // narrow_w + g)),
                  pl.BlockSpec((BLK, narrow_w), lambda g, c: (c, (SSD_INNER + SSD_GROUPS * BLK) // narrow_w + g)),
                  *z_specs,
                  pl.BlockSpec((gps, BLK, 8), lambda g, c: (g, c, 0)),
                  pl.BlockSpec((gps, 8, BLK), lambda g, c: (g, 0, c)),
                  gparam, gparam_t, gparam, gparam_t, gparam,
                  pl.BlockSpec((1, wide_w), lambda g, c: (0, g))],
        out_specs=[pl.BlockSpec((BLK, wide_w), lambda g, c: (c, g)),
                   pl.BlockSpec((BLK, wide_w), lambda g, c: (c, g)),
                   pl.BlockSpec((gps, None, 4, BLK, 2 * HD), lambda g, c: (g, c, 0, 0, 0))],
        scratch_shapes=[pltpu.VMEM((4 * gps, BLK, 2 * HD), F32)],
    )(xc, xc, xc, *([p] * gps), dtg, dtg_t, bias, bias_t, alog, alog_t, dskip, gn)


def ssd_bwd(p, xc, y, d_sin, hprev, dtg, dtg_t, params, gn, name):
    s = p.shape[0]
    nc = s // BLK
    bias, bias_t, alog, alog_t, dskip = params

    def body(xs_ref, b_ref, c_ref, z_ref, y_ref, dsin_ref, hp_ref, dtr_ref, dtrt_ref, bias_ref,
             biast_ref, alog_ref, alogt_ref, dsk_ref, gn_ref,
             dxs_ref, db_ref, dc_ref, dz_ref, ddt_ref, da_ref, dbias_ref, ddsk_ref, dgn_ref, dh_ref):
        c_idx = pl.program_id(1)

        @pl.when(c_idx == 0)
        def _():
            dh_ref[...] = jnp.zeros_like(dh_ref)
            da_ref[...] = jnp.zeros_like(da_ref)
            dbias_ref[...] = jnp.zeros_like(dbias_ref)
            ddsk_ref[...] = jnp.zeros_like(ddsk_ref)
            dgn_ref[...] = jnp.zeros_like(dgn_ref)

        dtb, dt, cs, cs_t, row, col, upper = _ssd_decays(dtr_ref, dtrt_ref, bias_ref, biast_ref, alog_ref, alogt_ref)
        first = _lane_first_half((BLK, 2 * HD))
        first_row = _lane_first_half((1, 2 * HD))
        tril = row >= col
        triu = row <= col
        last_row = lax.broadcasted_iota(jnp.int32, (BLK, 1), 0) == BLK - 1
        lane8 = lax.broadcasted_iota(jnp.int32, (BLK, 8), 1)

        yv = y_ref[...]
        zv = z_ref[...].astype(F32)
        sg = _sigmoid(zv)
        yz = yv * (zv * sg)
        r = lax.rsqrt(jnp.mean(yz * yz, axis=-1, keepdims=True) + EPS)
        yzn = yz * r
        dsn = dsin_ref[...]
        dgn_ref[...] += jnp.sum(dsn * yzn, axis=0, keepdims=True)
        dsn = dsn * gn_ref[...]
        dyz = r * (dsn - yzn * jnp.mean(dsn * yzn, axis=-1, keepdims=True))
        dy = dyz * (zv * sg)
        dz_ref[...] = (dyz * yv * (sg * (1.0 + zv * (1.0 - sg)))).astype(BF16)
        xs_all = xs_ref[...].astype(F32)
        ddsk_ref[...] += jnp.sum(dy * xs_all, axis=0, keepdims=True)

        b16, c16 = b_ref[...], c_ref[...]
        cb = _dot_nt(c16, b16)
        cb_t = _dot_nt(b16, c16)
        n_pairs = GROUP_W // (2 * HD)
        tot = cs[BLK - 1:BLK, :]
        exp_cs, exp_rest, exp_tot = jnp.exp(cs), jnp.exp(tot - cs), jnp.exp(tot)

        lanes_of_head = (lax.broadcasted_iota(jnp.int32, (8, GROUP_W), 1) // HD
                         == lax.broadcasted_iota(jnp.int32, (8, GROUP_W), 0)).astype(BF16)

        def per_head(v, mask):
            if v.shape[0] == 1:
                return jnp.concatenate([jnp.where(mask, v[:, 2 * pr:2 * pr + 1], v[:, 2 * pr + 1:2 * pr + 2])
                                        for pr in range(n_pairs)], axis=1)
            hi = v.astype(BF16)
            lo = (v - hi.astype(F32)).astype(BF16)
            return _dot(hi, lanes_of_head) + _dot(lo, lanes_of_head)

        head_of_lane = (lax.broadcasted_iota(jnp.int32, (GROUP_W, 8), 0) // HD
                        == lax.broadcasted_iota(jnp.int32, (GROUP_W, 8), 1)).astype(BF16)

        def head_sums(v):
            hi = v.astype(BF16)
            lo = (v - hi.astype(F32)).astype(BF16)
            return _dot(hi, head_of_lane) + _dot(lo, head_of_lane)

        dt_w, e_w, f_w = per_head(dt, first), per_head(exp_cs, first), per_head(exp_rest, first)
        xt = xs_all * dt_w
        xt16 = xt.astype(BF16)
        hstate = jnp.concatenate([hp_ref[pr] for pr in range(n_pairs)], axis=1)
        h16 = hstate.astype(BF16)
        dhn = jnp.concatenate([dh_ref[pr] for pr in range(n_pairs)], axis=1)
        dhn16 = dhn.astype(BF16)
        edy16 = (e_w * dy).astype(BF16)
        y_off = e_w * _dot(c16, h16)
        dcs_all = head_sums(dy * y_off)
        dc_acc = _dot_nt(edy16, h16)
        zmat = _dot(b16, dhn16)
        t_all = head_sums(zmat * xt) * exp_rest
        hh_rows = jnp.sum(head_sums(dhn * hstate), axis=0, keepdims=True)
        dtot = jnp.sum(t_all, axis=0, keepdims=True) + hh_rows * exp_tot
        dcs_all = dcs_all - t_all + jnp.where(last_row, dtot, 0.0)
        fxt16 = (f_w * xt).astype(BF16)
        db_acc = _dot_nt(fxt16, dhn16)
        dh_new = _dot_tn(c16, edy16) + per_head(exp_tot, first_row) * dhn
        for pr in range(n_pairs):
            dh_ref[pr] = dh_new[:, pr * 2 * HD:(pr + 1) * 2 * HD]
        g_sum = jnp.zeros((BLK, BLK), F32)
        gt_sum = jnp.zeros((BLK, BLK), F32)
        d_xt_parts = []
        for pr in range(n_pairs):
            cols = slice(pr * 2 * HD, (pr + 1) * 2 * HD)
            dym2 = _stack_heads(dy[:, cols].astype(BF16), first)
            d_m2 = _dot_nt(dym2, xt16[:, cols])
            d_mt2 = _dot_nt(xt16[:, cols], dym2)
            mt2 = []
            for e, h in enumerate((2 * pr, 2 * pr + 1)):
                cs_c, cs_r = cs[:, h:h + 1], cs_t[h:h + 1, :]
                decay = jnp.exp(jnp.where(tril, cs_c - cs_r, NEG))
                decay_t = jnp.exp(jnp.where(triu, cs_r - cs_c, NEG))
                gm = d_m2[e * BLK:(e + 1) * BLK] * decay
                gmt = d_mt2[:, e * BLK:(e + 1) * BLK] * decay_t
                g_sum += gm
                gt_sum += gmt
                dcs_h = jnp.sum(gm * cb, axis=-1, keepdims=True) - jnp.sum(gmt * cb_t, axis=-1, keepdims=True)
                dcs_all = dcs_all + jnp.where(lane8 == h, dcs_h, 0.0)
                mt2.append((cb_t * decay_t).astype(BF16))
            d_xt_parts.append(_dot(jnp.concatenate(mt2, axis=1), dym2))
        d_xt = jnp.concatenate(d_xt_parts, axis=1) + f_w * zmat
        dxs_ref[...] = dy * per_head(dsk_ref[...], first_row) + d_xt * dt_w
        ddtx_all = head_sums(d_xt * xs_all)

        dc_ref[...] = dc_acc + _dot(g_sum.astype(BF16), b16)
        db_ref[...] = db_acc + _dot(gt_sum.astype(BF16), c16)
        d_a = _dot_hi(upper, dcs_all)
        a_neg = -jnp.exp(alog_ref[...])
        ddt = ddtx_all + d_a * a_neg
        da_ref[...] += jnp.sum(d_a * dt, axis=0, keepdims=True)
        ddtr = ddt * _sigmoid(dtb)
        ddt_ref[...] = ddtr
        dbias_ref[...] += jnp.sum(ddtr, axis=0, keepdims=True)

    gps = SSD_GROUPS_PER_STEP
    k_wide, k_narrow, k_lead = ("cols", GROUP_W), ("cols", BLK), ("lead", 1)
    kinds = ([k_wide, k_narrow, k_narrow, ("each", 0), k_wide, k_wide] + [k_lead] * 8 + [k_wide]
             + [k_wide, k_narrow, k_narrow, k_wide] + [k_lead] * 4 + [k_wide] + [("lead", 4)])
    wide_w, narrow_w = GROUP_W * gps, BLK * gps
    rc = lambda c: nc - 1 - c
    gparam = pl.BlockSpec((gps, 1, 8), lambda g, c: (g, 0, 0))
    gparam_t = pl.BlockSpec((gps, 8, 1), lambda g, c: (g, 0, 0))
    wide = pl.BlockSpec((BLK, wide_w), lambda g, c: (rc(c), g))
    narrow = pl.BlockSpec((BLK, narrow_w), lambda g, c: (rc(c), g))
    z_specs = [pl.BlockSpec((BLK, GROUP_W),
                            functools.partial(lambda g, c, gi: (rc(c), COL_Z // GROUP_W + gps * g + gi), gi=gi))
               for gi in range(gps)]
    return _call(
        _per_group(body, gps, kinds), name=name,
        out_shape=[_sds((s, SSD_INNER), F32), _sds((s, GROUP_W), F32), _sds((s, GROUP_W), F32),
                   _sds((s, SSD_INNER), BF16), _sds((SSD_GROUPS, s, 8), F32),
                   _sds((SSD_GROUPS, 1, 8), F32), _sds((SSD_GROUPS, 1, 8), F32),
                   _sds((SSD_GROUPS, 1, GROUP_W), F32), _sds((1, SSD_INNER), F32)],
        grid=(SSD_GROUPS // gps, nc),
        in_specs=[wide,
                  pl.BlockSpec((BLK, narrow_w), lambda g, c: (rc(c), SSD_INNER // narrow_w + g)),
                  pl.BlockSpec((BLK, narrow_w), lambda g, c: (rc(c), (SSD_INNER + SSD_GROUPS * BLK) // narrow_w + g)),
                  *z_specs,
                  wide, wide,
                  pl.BlockSpec((gps, None, 4, BLK, 2 * HD), lambda g, c: (g, rc(c), 0, 0, 0)),
                  pl.BlockSpec((gps, BLK, 8), lambda g, c: (g, rc(c), 0)),
                  pl.BlockSpec((gps, 8, BLK), lambda g, c: (g, 0, rc(c))),
                  gparam, gparam_t, gparam, gparam_t, gparam,
                  pl.BlockSpec((1, wide_w), lambda g, c: (0, g))],
        out_specs=[wide, narrow, narrow, wide,
                   pl.BlockSpec((gps, BLK, 8), lambda g, c: (g, rc(c), 0)),
                   gparam, gparam,
                   pl.BlockSpec((gps, 1, GROUP_W), lambda g, c: (g, 0, 0)),
                   pl.BlockSpec((1, wide_w), lambda g, c: (0, g))],
        scratch_shapes=[pltpu.VMEM((4 * gps, BLK, 2 * HD), F32)],
    )(xc, xc, xc, *([p] * gps), y, d_sin, hprev, dtg, dtg_t, bias, bias_t, alog, alog_t, dskip, gn)


def merge_fwd(p, a, sbr, name, tm=1024):
    s = p.shape[0]
    nj = D_MODEL // GROUP_W

    def body(ga_ref, gs_ref, a_ref, s_ref, o_ref):
        o_ref[...] = (_sigmoid(ga_ref[...].astype(F32)) * a_ref[...]
                      + _sigmoid(gs_ref[...].astype(F32)) * s_ref[...]).astype(BF16)

    blk = pl.BlockSpec((tm, GROUP_W), lambda i, j: (i, j))
    return _call(body, name=name, out_shape=_sds((s, D_MODEL), BF16), grid=(s // tm, nj),
                 in_specs=[pl.BlockSpec((tm, GROUP_W), lambda i, j: (i, COL_GA // GROUP_W + j)),
                           pl.BlockSpec((tm, GROUP_W), lambda i, j: (i, COL_GS // GROUP_W + j)), blk, blk],
                 out_specs=blk)(p, p, a, sbr)


def merge_bwd(p, a, sbr, dmerged, name, tm=1024):
    s = p.shape[0]
    nj = D_MODEL // GROUP_W

    def body(ga_ref, gs_ref, a_ref, s_ref, dm_ref, da_ref, ds_ref, dga_ref, dgs_ref):
        dm = dm_ref[...]
        sa = _sigmoid(ga_ref[...].astype(F32))
        ss = _sigmoid(gs_ref[...].astype(F32))
        da_ref[...] = (dm * sa).astype(BF16)
        ds_ref[...] = (dm * ss).astype(BF16)
        dga_ref[...] = (dm * a_ref[...] * sa * (1.0 - sa)).astype(BF16)
        dgs_ref[...] = (dm * s_ref[...] * ss * (1.0 - ss)).astype(BF16)

    blk = pl.BlockSpec((tm, GROUP_W), lambda i, j: (i, j))
    shp = _sds((s, D_MODEL), BF16)
    return _call(body, name=name, out_shape=[shp] * 4, grid=(s // tm, nj),
                 in_specs=[pl.BlockSpec((tm, GROUP_W), lambda i, j: (i, COL_GA // GROUP_W + j)),
                           pl.BlockSpec((tm, GROUP_W), lambda i, j: (i, COL_GS // GROUP_W + j)), blk, blk, blk],
                 out_specs=[blk] * 4)(p, p, a, sbr, dmerged)


def _group_major(v):
    return v.reshape(SSD_GROUPS, 1, 8), v.reshape(SSD_GROUPS, 8, 1)


def mixer_forward(x, w, rider=None, later_weights=None):
    s = x.shape[0]
    h = rms_fwd(x, w["mix_norm"], "mix_rms")
    p = matmul_nn(h, w["w_in_main"], "mix_proj", BF16, tm=2048, tn=512, rider=rider)
    rode = None
    if rider is not None:
        p, rode = p
        w = dict(w, **later_weights(rode))
    dt_raw = matmul_nn(h, w["w_in_dt"], "mix_proj_dt", F32, tm=1024, tn=DT_PAD)
    qn, kn = qk_norm_fwd(p, w["q_gain"], w["k_gain"], "qk_norm")
    streams, os_, lses = [], [], []
    for g, d in enumerate(ATTN_DILATIONS):
        cols = slice(g * GROUP_W, (g + 1) * GROUP_W)
        qs, ks = _to_streams(qn[:, cols], d), _to_streams(kn[:, cols], d)
        vs = _to_streams(p[:, COL_V + g * GROUP_W:COL_V + (g + 1) * GROUP_W], d)
        o, lse = attn_fwd(qs, ks, vs, g, f"attn_fwd{g}")
        streams.append((qs, ks, vs, lse))
        os_.append(_from_streams(o, d))
        lses.append(_from_streams(lse, d))
    attn_o = attn_merge_fwd(os_, lses, "attn_merge")
    cpre, xc = conv_fwd(p, w["conv_w"], w["conv_b"], "conv_fwd")
    dtg = dt_raw[:, :SSD_HEADS].reshape(s, SSD_GROUPS, 8).transpose(1, 0, 2)
    dtg_t = dtg.transpose(0, 2, 1)
    params = (*_group_major(w["dt_bias"]), *_group_major(w["a_log"]), _group_major(w["d_skip"])[0])
    y, s_in, hprev = ssd_fwd(p, xc, dtg, dtg_t, params, w["ssd_norm"], "ssd_fwd")
    a = matmul_nn(attn_o, w["w_attn_branch"], "attn_branch", F32, tm=1024, tn=512)
    sbr = matmul_nn(s_in, w["w_ssd_branch"], "ssd_branch", F32, tm=1024, tn=512)
    merged = merge_fwd(p, a, sbr, "merge")
    x_out = matmul_nn(merged, w["w_out"], "mix_out", F32, tm=1024, tn=512, res=x)
    saved = dict(h=h, p=p, streams=streams, os=os_, lses=lses, attn_o=attn_o, cpre=cpre, xc=xc, dtg=dtg,
                 dtg_t=dtg_t, params=params, y=y, s_in=s_in, hprev=hprev, a=a, sbr=sbr, merged=merged, w=w)
    return x_out, saved, rode


def mixer_backward(dx_out, x, sv, ride_early=None, ride_late=None):
    s = x.shape[0]
    p = sv["p"]
    w = sv["w"]
    g = {}
    dmerged = matmul_nt(dx_out, w["w_out"], "d_merged", F32, tm=1024, tn=512, tk=1024)
    g["w_out"] = matmul_tn(sv["merged"], dx_out, "dw_out", tn=512, ts=2048)
    da, ds, dga, dgs = merge_bwd(p, sv["a"], sv["sbr"], dmerged, "merge_bwd")
    g["w_attn_branch"] = matmul_tn(sv["attn_o"], da, "dw_attn_branch", tn=512, ts=2048)
    g["w_ssd_branch"] = matmul_tn(sv["s_in"], ds, "dw_ssd_branch", tn=512, ts=2048)
    d_attn_o = matmul_nt(da, w["w_attn_branch"], "d_attn_o", F32, tm=1024, tn=512, tk=1024)
    d_sin = matmul_nt(ds, w["w_ssd_branch"], "d_ssd_in", F32, tm=1024, tn=512, tk=1024)
    dxs, d_b, d_c, dz, ddt, d_asum, d_bias, d_dsk, d_gn = ssd_bwd(
        p, sv["xc"], sv["y"], d_sin, sv["hprev"], sv["dtg"], sv["dtg_t"], sv["params"], w["ssd_norm"], "ssd_bwd")
    dxbc, d_convw, d_convb = conv_bwd(p, sv["cpre"], dxs, d_b, d_c, w["conv_w"], "conv_bwd")
    g["conv_w"] = d_convw[:SSD_CONV]
    g["conv_b"] = d_convb
    g["dt_bias"] = d_bias.reshape(1, SSD_HEADS)
    g["a_log"] = (d_asum * (-jnp.exp(sv["params"][2]))).reshape(1, SSD_HEADS)
    g["d_skip"] = jnp.sum(d_dsk.reshape(SSD_HEADS, HD), axis=1).reshape(1, SSD_HEADS)
    g["ssd_norm"] = d_gn
    merged_bwd = attn_merge_bwd(d_attn_o, sv["os"], sv["lses"], "attn_merge_bwd")
    dqs, dks, dvs = [], [], []
    for gi, d in enumerate(ATTN_DILATIONS):
        qs, ks, vs, lse = sv["streams"][gi]
        d_o = _to_streams(merged_bwd[gi], d)
        cterm = _to_streams(merged_bwd[3 + gi], d)
        dq, dk, dv = attn_bwd(qs, ks, vs, d_o, cterm, lse, gi, f"attn_bwd{gi}")
        dqs.append(_from_streams(dq, d))
        dks.append(_from_streams(dk, d))
        dvs.append(_from_streams(dv, d).astype(BF16))
    dqk, d_qg, d_kg = qk_norm_bwd(p, dqs, dks, w["q_gain"], w["k_gain"], "qk_norm_bwd")
    g["q_norm"] = jnp.sum(d_qg.reshape(N_ATTN_HEADS, HD), axis=0).reshape(1, HD)
    g["k_norm"] = jnp.sum(d_kg.reshape(N_ATTN_HEADS, HD), axis=0).reshape(1, HD)
    dp = [dqk, jnp.concatenate(dvs, axis=1), dz, dxbc, dga, dgs]
    ddt_pad = jnp.pad(ddt.transpose(1, 0, 2).reshape(s, SSD_HEADS), ((0, 0), (0, DT_PAD - SSD_HEADS)))
    if ride_early is not None:
        g["w_in_main"], g["rode_early"] = matmul_tn_pieces(sv["h"], dp, "dw_in", tn=512, ts=1024,
                                                           rider=ride_early(g))
    else:
        g["w_in_main"] = matmul_tn_pieces(sv["h"], dp, "dw_in", tn=512, ts=1024)
    g["w_in_dt"] = matmul_tn(sv["h"], ddt_pad, "dw_in_dt", tn=DT_PAD, ts=1024)
    if ride_late is not None:
        dh_main, g["rode_late"] = matmul_nt_pieces(dp, w["w_in_main"], "d_mix_h", F32, tm=1024, tn=1024, tk=512,
                                                   rider=ride_late(g))
    else:
        dh_main = matmul_nt_pieces(dp, w["w_in_main"], "d_mix_h", F32, tm=1024, tn=512, tk=512)
    dh_dt = matmul_nt(ddt_pad, w["w_in_dt"], "d_mix_h_dt", F32, tm=1024, tn=1024, tk=DT_PAD)
    dx, g["mix_norm"] = rms_bwd([dh_main, dh_dt], x, w["mix_norm"], dx_out, "mix_drms")
    return dx, g


def _place():
    x, y, c = lax.axis_index("x"), lax.axis_index("y"), lax.axis_index("c")
    chips = [(1 - x, y), (x, 1 - y), (1 - x, 1 - y)]
    return x, y, c, 2 * x + y, chips


def _cores():
    c = lax.axis_index("c")
    return jnp.stack([c, 1 - c]).astype(jnp.int32)


def _staged_call(body, *, name, grid, in_specs, out_specs, out_shape, scratch_shapes):
    return pl.pallas_call(
        body, out_shape=out_shape, name=name,
        grid_spec=pltpu.PrefetchScalarGridSpec(num_scalar_prefetch=1, grid=grid, in_specs=in_specs,
                                               out_specs=out_specs, scratch_shapes=scratch_shapes),
        compiler_params=pltpu.CompilerParams(dimension_semantics=("arbitrary",) * len(grid),
                                             vmem_limit_bytes=V7X_VMEM_LIMIT, has_side_effects=True))


def gather_rider(shards, tiles):
    dma = pltpu.SemaphoreType.DMA
    n = len(shards)
    geo = [(a.shape[0] // 2, tm, (a.shape[0] // 2) // tm) for a, tm in zip(shards, tiles)]
    scratch = []
    for a, (h, tm, nk) in zip(shards, geo):
        scratch += [pltpu.VMEM((N_CHIP,) + a.shape, a.dtype), dma((3, nk)), dma((3, nk)), dma((3, nk)), dma((3, nk)),
                    dma((nk + 2,))]

    def copies(j, in_ref, scr):
        buf, send1, recv1, send2, recv2, local = scr[6 * j:6 * j + 6]
        h, tm, nk = geo[j]
        x, y, c, me, chips = _place()
        chip_of = [2 * chips[t][0] + chips[t][1] for t in range(3)]

        def rows(chip, core, k):
            return buf.at[chip, pl.ds(core * h + k * tm, tm)]

        def mine(k):
            if k == nk:
                return pltpu.make_async_copy(in_ref.at[pl.ds((1 - c) * h, h)], buf.at[me, pl.ds((1 - c) * h, h)],
                                             local.at[nk])
            return pltpu.make_async_copy(in_ref.at[pl.ds(c * h + k * tm, tm)], rows(me, c, k), local.at[k])

        def level1(t, k, incoming):
            place = rows(chip_of[t] if incoming else me, c, k)
            return pltpu.make_async_remote_copy(src_ref=place, dst_ref=place, send_sem=send1.at[t, k],
                                                recv_sem=recv1.at[t, k], device_id=(*chips[t], c), device_id_type=MESH)

        def level2(t, k, incoming):
            place = rows(chip_of[t], (1 - c) if incoming else c, k)
            return pltpu.make_async_remote_copy(src_ref=place, dst_ref=place, send_sem=send2.at[t, k],
                                                recv_sem=recv2.at[t, k], device_id=(x, y, 1 - c),
                                                device_id_type=MESH)

        return buf, local, nk, mine, level1, level2

    def start(ins, outs, scr):
        for j in range(n):
            _, _, nk, mine, _, _ = copies(j, ins[j], scr)
            for k in range(nk + 1):
                mine(k).start()
        for j in range(n):
            _, _, nk, mine, level1, _ = copies(j, ins[j], scr)
            for k in range(nk):
                mine(k).wait()
                for t in range(3):
                    level1(t, k, False).start()

    def finish(ins, outs, scr):
        for j in range(n):
            _, _, nk, _, level1, level2 = copies(j, ins[j], scr)
            for k in range(nk):
                for t in range(3):
                    level1(t, k, True).wait_recv()
                    level2(t, k, False).start()
        for j in range(n):
            buf, local, nk, mine, level1, level2 = copies(j, ins[j], scr)
            for k in range(nk):
                for t in range(3):
                    level2(t, k, True).wait_recv()
            for k in range(nk):
                for t in range(3):
                    level1(t, k, False).wait_send()
                    level2(t, k, False).wait_send()
            mine(nk).wait()
            pltpu.make_async_copy(buf, outs[j], local.at[nk + 1]).start()
        for j in range(n):
            buf, local, nk, _, _, _ = copies(j, ins[j], scr)
            pltpu.make_async_copy(buf, outs[j], local.at[nk + 1]).wait()

    return Rider(list(shards), [_sds((N_CHIP,) + a.shape, a.dtype) for a in shards], scratch, start, finish)


def sibling_sum(g, tm, name):
    _, r, cdim = g.shape
    h = r // 2
    ni = h // tm
    dma = pltpu.SemaphoreType.DMA

    def body(cores_ref, keep_ref, give_ref, out_ref, slot, send, recv):
        par = (pl.program_id(0) * ni + pl.program_id(1)) % 2
        x, y, c, _, _ = _place()
        cp = pltpu.make_async_remote_copy(src_ref=give_ref, dst_ref=slot.at[par], send_sem=send.at[par],
                                          recv_sem=recv.at[par], device_id=(x, y, 1 - c), device_id_type=MESH)
        cp.start()
        cp.wait_recv()
        out_ref[...] = (keep_ref[...].astype(F32) + slot[par].astype(F32)).astype(out_ref.dtype)
        cp.wait_send()

    flat = g.reshape(N_CHIP * r, cdim)
    return _staged_call(
        body, name=name, grid=(N_CHIP, ni),
        in_specs=[pl.BlockSpec((tm, cdim), lambda j, i, cores: ((2 * j + cores[0]) * ni + i, 0)),
                  pl.BlockSpec((tm, cdim), lambda j, i, cores: ((2 * j + cores[1]) * ni + i, 0))],
        out_specs=pl.BlockSpec((None, tm, cdim), lambda j, i, cores: (j, i, 0)),
        out_shape=_sds((N_CHIP, h, cdim), g.dtype),
        scratch_shapes=[pltpu.VMEM((2, tm, cdim), g.dtype), dma((2,)), dma((2,))],
    )(_cores(), flat, flat)


def owner_sum_rider(sums, tiles):
    dma = pltpu.SemaphoreType.DMA
    n = len(sums)
    geo = [(a.shape[1], tm, a.shape[1] // tm) for a, tm in zip(sums, tiles)]
    scratch = []
    for a, (h, tm, nk) in zip(sums, geo):
        cdim = a.shape[2]
        scratch += [pltpu.VMEM(a.shape, a.dtype), pltpu.VMEM((3, h, cdim), a.dtype), pltpu.VMEM((2, h, cdim), F32),
                    dma((3, nk)), dma((3, nk)), dma((nk,)), dma((nk,)), dma((2,))]

    def copies(j, scr):
        part, got, res, send, recv, send2, recv2, local = scr[8 * j:8 * j + 8]
        h, tm, nk = geo[j]
        x, y, c, me, chips = _place()

        def to_owner(t, k):
            chip = 2 * chips[t][0] + chips[t][1]
            return pltpu.make_async_remote_copy(
                src_ref=part.at[chip, pl.ds(k * tm, tm)], dst_ref=got.at[t, pl.ds(k * tm, tm)],
                send_sem=send.at[t, k], recv_sem=recv.at[t, k], device_id=(*chips[t], c), device_id_type=MESH)

        def to_sibling(k):
            place = res.at[c, pl.ds(k * tm, tm)]
            return pltpu.make_async_remote_copy(src_ref=place, dst_ref=place, send_sem=send2.at[k],
                                                recv_sem=recv2.at[k], device_id=(x, y, 1 - c), device_id_type=MESH)

        return part, got, res, local, to_owner, to_sibling, (tm, nk, c, me)

    def start(ins, outs, scr):
        for j in range(n):
            part, _, _, local, _, _, _ = copies(j, scr)
            pltpu.make_async_copy(ins[j], part, local.at[0]).start()
        for j in range(n):
            part, _, _, local, to_owner, _, (tm, nk, c, me) = copies(j, scr)
            pltpu.make_async_copy(ins[j], part, local.at[0]).wait()
            for k in range(nk):
                for t in range(3):
                    to_owner(t, k).start()

    def finish(ins, outs, scr):
        for j in range(n):
            part, got, res, _, to_owner, to_sibling, (tm, nk, c, me) = copies(j, scr)
            for k in range(nk):
                rows = pl.ds(k * tm, tm)
                for t in range(3):
                    to_owner(t, k).wait_recv()
                acc = part[me, rows, :].astype(F32)
                for t in range(3):
                    acc = acc + got[t, rows, :].astype(F32)
                res[c, rows, :] = acc
                to_sibling(k).start()
        for j in range(n):
            _, _, res, local, to_owner, to_sibling, (tm, nk, c, me) = copies(j, scr)
            for k in range(nk):
                to_sibling(k).wait_recv()
            for k in range(nk):
                to_sibling(k).wait_send()
                for t in range(3):
                    to_owner(t, k).wait_send()
            pltpu.make_async_copy(res, outs[j], local.at[1]).start()
        for j in range(n):
            _, _, res, local, _, _, _ = copies(j, scr)
            pltpu.make_async_copy(res, outs[j], local.at[1]).wait()

    return Rider(list(sums), [_sds((2, a.shape[1], a.shape[2]), F32) for a in sums], scratch, start, finish)


def gather_conv_w(w):
    def body(in_ref, out_ref, send, recv):
        x, y, c, me, chips = _place()
        out_ref[me] = in_ref[...]
        copies = []
        for t in range(3):
            copies.append(pltpu.make_async_remote_copy(
                src_ref=out_ref.at[me], dst_ref=out_ref.at[me], send_sem=send.at[t], recv_sem=recv.at[t],
                device_id=(*chips[t], c), device_id_type=MESH))
        for cp in copies:
            cp.start()
        for cp in copies:
            cp.wait_recv()
        for cp in copies:
            cp.wait_send()

    dma = pltpu.SemaphoreType.DMA
    vmem = pl.BlockSpec(memory_space=pltpu.VMEM)
    return pl.pallas_call(
        body, out_shape=_sds((N_CHIP,) + w.shape, w.dtype), in_specs=[vmem], out_specs=vmem, name="gather_conv_w",
        scratch_shapes=[dma((3,)), dma((3,))],
        compiler_params=pltpu.CompilerParams(has_side_effects=True))(w)


N_DEV = 8
SMALL_ROWS = 32
SMALL_LANES = 1024


def all_reduce_small(arrays):
    n_arr = len(arrays)
    places = []
    for k, a in enumerate(arrays):
        for ri in range(a.shape[0]):
            for c0 in range(0, a.shape[1], SMALL_LANES):
                places.append((k, ri, c0, min(SMALL_LANES, a.shape[1] - c0), len(places)))
    assert len(places) <= SMALL_ROWS

    def body(*refs):
        ins, outs = refs[:n_arr], refs[n_arr:2 * n_arr]
        buf, send, recv = refs[2 * n_arr:]
        x, y, c, _, _ = _place()
        me = 4 * x + 2 * y + c
        buf[me] = jnp.zeros((SMALL_ROWS, SMALL_LANES), F32)
        for k, ri, c0, width, row in places:
            buf[me, row:row + 1, 0:width] = ins[k][ri:ri + 1, c0:c0 + width]
        copies = []
        for r in range(1, N_DEV):
            px = (1 - x) if r & 4 else x
            py = (1 - y) if r & 2 else y
            pc = (1 - c) if r & 1 else c
            copies.append(pltpu.make_async_remote_copy(
                src_ref=buf.at[me], dst_ref=buf.at[me], send_sem=send.at[r - 1], recv_sem=recv.at[r - 1],
                device_id=(px, py, pc), device_id_type=MESH))
        for cp in copies:
            cp.start()
        for cp in copies:
            cp.wait_recv()
        for cp in copies:
            cp.wait_send()
        acc = buf[0]
        for j in range(1, N_DEV):
            acc = acc + buf[j]
        for k, ri, c0, width, row in places:
            outs[k][ri:ri + 1, c0:c0 + width] = acc[row:row + 1, 0:width]

    dma = pltpu.SemaphoreType.DMA
    vmem = pl.BlockSpec(memory_space=pltpu.VMEM)
    return pl.pallas_call(
        body, out_shape=[_sds(a.shape, F32) for a in arrays], in_specs=[vmem] * n_arr, out_specs=[vmem] * n_arr,
        name="all_reduce_small",
        scratch_shapes=[pltpu.VMEM((N_DEV, SMALL_ROWS, SMALL_LANES), F32), dma((N_DEV - 1,)), dma((N_DEV - 1,))],
        compiler_params=pltpu.CompilerParams(has_side_effects=True))(*arrays)


def _row_tile(rows, limit, multiple):
    return max(t for t in range(multiple, min(rows, limit) + 1, multiple) if rows % t == 0)


def _adamw_math(w, g, m, v):
    c1 = 1.0 - ADAM_B1 ** ADAM_STEP
    c2 = 1.0 - ADAM_B2 ** ADAM_STEP
    m2 = ADAM_B1 * m + (1.0 - ADAM_B1) * g
    v2 = ADAM_B2 * v + (1.0 - ADAM_B2) * (g * g)
    return -ADAM_LR * ((m2 / c1) / (jnp.sqrt(v2 / c2) + ADAM_EPS) + ADAM_WD * w), m2, v2


def adamw(w, g, row_off, m, v, name):
    _, r, c = w.shape
    tm = r if r < 8 else _row_tile(math.gcd(r, row_off) if row_off else r, 128, 8)

    def body(w_ref, g_ref, m_ref, v_ref, go_ref, d_ref, m2_ref, v2_ref):
        gv = g_ref[...]
        go_ref[...] = gv
        d_ref[...], m2_ref[...], v2_ref[...] = _adamw_math(w_ref[...], gv, m_ref[...], v_ref[...])

    blk = pl.BlockSpec((None, tm, c), lambda i: (0, i, 0))
    shp = _sds((1, r, c), F32)
    return _call(body, name=name, out_shape=[shp] * 4, grid=(r // tm,),
                 in_specs=[blk, pl.BlockSpec((tm, c), lambda i: (row_off // tm + i, 0)), blk, blk],
                 out_specs=[blk] * 4)(w, g, m, v)


def adamw_small(ws, gs, ms, vs):
    n = len(ws)

    def body(*refs):
        ins, outs = refs[:4 * n], refs[4 * n:]
        for k in range(n):
            w_ref, g_ref, m_ref, v_ref = (ins[j * n + k] for j in range(4))
            outs[k][...], outs[n + k][...], outs[2 * n + k][...] = _adamw_math(w_ref[...], g_ref[...], m_ref[...],
                                                                               v_ref[...])

    vmem = pl.BlockSpec(memory_space=pltpu.VMEM)
    shapes = [_sds(w.shape, F32) for w in ws] * 3
    res = pl.pallas_call(body, out_shape=shapes, in_specs=[vmem] * (4 * n), out_specs=[vmem] * (3 * n),
                         name="adamw_small")(*ws, *gs, *ms, *vs)
    return res[:n], res[n:2 * n], res[2 * n:]


SMALL = ("ffn1_norm", "mix_norm", "q_norm", "k_norm", "conv_b", "dt_bias", "a_log", "d_skip", "ssd_norm", "ffn2_norm")
WEIGHTS = ("ffn1_norm", "ffn1_w_gate", "ffn1_w_up", "ffn1_w_down", "mix_norm", "w_in", "q_norm", "k_norm", "conv_w",
           "conv_b", "dt_bias", "a_log", "d_skip", "ssd_norm", "w_attn_branch", "w_ssd_branch", "w_out", "ffn2_norm",
           "ffn2_w_gate", "ffn2_w_up", "ffn2_w_down")
CONV_SHARD = SSD_CONV_DIM // N_CHIP
CLASSES = {
    "ffn1_in": (("ffn1_w_gate", 1024), ("ffn1_w_up", 1024)),
    "ffn1_out": (("ffn1_w_down", 704),),
    "mix_in": (("w_in", 1024),),
    "mix_attn": (("w_attn_branch", 512),),
    "late_out": (("ffn2_w_down", 704), ("w_ssd_branch", 512), ("w_out", 256)),
    "ffn2_in": (("ffn2_w_gate", 1024), ("ffn2_w_up", 1024)),
}
CLASS_TILE = {"ffn1_in": 256, "ffn1_out": 176, "mix_attn": 256, "late_out": 368, "ffn2_in": 256,
              "mix_in_top": 128, "mix_in_bottom": 128}
SIBLING_TILE = {"ffn1_in": 1024, "ffn1_out": 352, "mix_attn": 256, "late_out": 736, "ffn2_in": 1024,
                "mix_in_top": 256, "mix_in_bottom": 256}


def _chip_major_cols(a):
    r = a.shape[0]
    return a.reshape(r, N_CHIP, -1).transpose(1, 0, 2)


def _from_chip_major_cols(a):
    return a.transpose(1, 0, 2).reshape(a.shape[1], -1)


def kernel(x, ffn1_norm, ffn1_w_gate, ffn1_w_up, ffn1_w_down, mix_norm, w_in, q_norm, k_norm, conv_w, conv_b, dt_bias, a_log, d_skip, ssd_norm, w_attn_branch, w_ssd_branch, w_out, ffn2_norm, ffn2_w_gate, ffn2_w_up, ffn2_w_down, loss_target, m_ffn1_norm, m_ffn1_w_gate, m_ffn1_w_up, m_ffn1_w_down, m_mix_norm, m_w_in, m_q_norm, m_k_norm, m_conv_w, m_conv_b, m_dt_bias, m_a_log, m_d_skip, m_ssd_norm, m_w_attn_branch, m_w_ssd_branch, m_w_out, m_ffn2_norm, m_ffn2_w_gate, m_ffn2_w_up, m_ffn2_w_down, v_ffn1_norm, v_ffn1_w_gate, v_ffn1_w_up, v_ffn1_w_down, v_mix_norm, v_w_in, v_q_norm, v_k_norm, v_conv_w, v_conv_b, v_dt_bias, v_a_log, v_d_skip, v_ssd_norm, v_w_attn_branch, v_w_ssd_branch, v_w_out, v_ffn2_norm, v_ffn2_w_gate, v_ffn2_w_up, v_ffn2_w_down):
    env = dict(locals())
    wts = {k: env[k] for k in WEIGHTS}
    moms = {k: env["m_" + k] for k in WEIGHTS}
    vars_ = {k: env["v_" + k] for k in WEIGHTS}
    x0 = x[0]
    target = loss_target[0]

    def gather(classes, more=()):
        shards = [jnp.concatenate([wts[k][0] for k, _ in CLASSES[c]], axis=0).astype(BF16) for c in classes]
        return gather_rider(shards + [a for a, _ in more], [CLASS_TILE[c] for c in classes] + [t for _, t in more])

    def reducer(classes, parts):
        sums = [sibling_sum(p, SIBLING_TILE[c], f"sibling_sum_{c}") for c, p in zip(classes, parts)]
        return owner_sum_rider(sums, [CLASS_TILE[c] for c in classes])

    half = D_MODEL // 2
    in_tile = CLASS_TILE["mix_in_top"]
    x1, saved1, (w_ffn1_in,), (w_ffn1_out, w_mix_attn, w_in_top), (w_in_bottom,) = ffn_forward(
        x0, ffn1_norm, lambda rode: rode[0], lambda rode: rode[0], "ffn1", rms_rider=gather(["ffn1_in"]),
        up_rider=gather(["ffn1_out", "mix_attn"], [(w_in[0, :half].astype(BF16), in_tile)]),
        down_rider=gather([], [(w_in[0, half:].astype(BF16), in_tile)]))
    dt0, dt1 = IN_DT0 - 3 * IN_SHARD, IN_DT1 - 3 * IN_SHARD

    def in_columns(w4):
        return jnp.concatenate([w4[0], w4[1], w4[2], w4[3][:, :dt0], w4[3][:, dt1:]], axis=1), w4[3][:, dt0:dt1]

    (main_top, dt_top), (main_bottom, dt_bottom) = in_columns(w_in_top), in_columns(w_in_bottom)
    mixer_w = dict(
        mix_norm=mix_norm,
        w_in_main=jnp.concatenate([main_top, main_bottom], axis=0),
        w_in_dt=jnp.pad(jnp.concatenate([dt_top, dt_bottom], axis=0), ((0, 0), (0, DT_PAD - SSD_HEADS))),
        q_gain=jnp.tile(q_norm, (1, 2)), k_gain=jnp.tile(k_norm, (1, 2)),
        conv_w=_from_chip_major_cols(gather_conv_w(conv_w[0])), conv_b=conv_b, dt_bias=dt_bias, a_log=a_log,
        d_skip=d_skip, ssd_norm=ssd_norm, w_attn_branch=_from_chip_major_cols(w_mix_attn))

    def later_weights(rode):
        late = rode[0]
        return dict(w_ssd_branch=late[:, 704:1216].reshape(SSD_INNER, D_MODEL),
                    w_out=late[:, 1216:1472].reshape(D_MODEL, D_MODEL))

    x2, saved_mix, (w_late_out, w_ffn2_in) = mixer_forward(x1, mixer_w, gather(["late_out", "ffn2_in"]), later_weights)
    (dx3, sq), saved2, _, _, _ = ffn_forward(x2, ffn2_norm, lambda rode: w_ffn2_in, lambda rode: w_late_out, "ffn2",
                                             target=target)

    grads = {}
    dx2, grads["ffn2_norm"], d_ffn2_in, d_ffn2_down, _ = ffn_backward(dx3, x2, ffn2_norm, w_ffn2_in, w_late_out,
                                                                      saved2, "ffn2")

    def ride_early(g):
        late = jnp.concatenate([d_ffn2_down, g["w_ssd_branch"].reshape(N_CHIP, -1, D_MODEL),
                                g["w_out"].reshape(N_CHIP, -1, D_MODEL)], axis=1)
        return reducer(["ffn2_in", "late_out"], [d_ffn2_in, late])

    g_in_rows = {}

    def ride_late(g):
        main = g["w_in_main"]
        last = jnp.concatenate([main[:, 3 * IN_SHARD:IN_DT0], g["w_in_dt"][:, :SSD_HEADS], main[:, IN_DT0:]], axis=1)
        for part, rows in (("top", slice(0, D_MODEL // 2)), ("bottom", slice(D_MODEL // 2, D_MODEL))):
            g_in_rows[part] = jnp.stack([main[rows, j * IN_SHARD:(j + 1) * IN_SHARD] for j in range(3)]
                                        + [last[rows]])
        return reducer(["mix_in_top", "mix_attn"], [g_in_rows["top"], _chip_major_cols(g["w_attn_branch"])])

    dx1, gmix = mixer_backward(dx2, x1, saved_mix, ride_early, ride_late)
    dx0, grads["ffn1_norm"], rode_in, rode_out, rode_hidden = ffn_backward(
        dx1, x0, ffn1_norm, w_ffn1_in, w_ffn1_out, saved1, "ffn1",
        hidden_rider=reducer(["mix_in_bottom"], [g_in_rows["bottom"]]),
        ride_down=lambda d: reducer(["ffn1_out"], [d]), ride_in=lambda d: reducer(["ffn1_in"], [d]))
    for k in ("mix_norm", "q_norm", "k_norm", "conv_b", "dt_bias", "a_log", "d_skip", "ssd_norm"):
        grads[k] = gmix[k]
    reduced = dict(zip(("ffn2_in", "late_out", "mix_in_top", "mix_attn", "ffn1_in", "ffn1_out", "mix_in_bottom"),
                       (*gmix["rode_early"], *gmix["rode_late"], rode_in[0], rode_out[0], rode_hidden[0])))
    reduced = {c: r.reshape(-1, r.shape[2]) for c, r in reduced.items()}
    reduced["mix_in"] = jnp.concatenate([reduced.pop("mix_in_top"), reduced.pop("mix_in_bottom")], axis=0)
    summed = all_reduce_small([grads[k] for k in SMALL]
                              + [gmix["conv_w"], (0.5 * jnp.sum(sq) / D_MODEL).reshape(1, 1)])
    g_small = dict(zip(SMALL, summed))
    loss = summed[-1].reshape(())
    chip = 2 * lax.axis_index("x") + lax.axis_index("y")
    g_conv = lax.dynamic_slice_in_dim(summed[-2], chip * CONV_SHARD, CONV_SHARD, axis=1)

    g_final, delta, new_m, new_v = dict(g_small), {}, {}, {}

    def update(k, g_arr, row_off):
        w, m, v = wts[k], moms[k], vars_[k]
        rows, cols = w.shape[1:]
        if cols % 128:
            res = adamw(jnp.swapaxes(w, 1, 2), g_arr[row_off:row_off + rows].T, 0, jnp.swapaxes(m, 1, 2),
                        jnp.swapaxes(v, 1, 2), f"adamw_{k}")
            res = [jnp.swapaxes(r, 1, 2) for r in res]
        else:
            res = adamw(w, g_arr, row_off, m, v, f"adamw_{k}")
        g_final[k], delta[k], new_m[k], new_v[k] = res

    for cls, members in CLASSES.items():
        off = 0
        for k, rows in members:
            update(k, reduced[cls], off)
            off += rows
    update("conv_w", g_conv, 0)
    small = adamw_small(*([d[k] for k in SMALL] for d in (wts, g_small, moms, vars_)))
    for res, vals in zip((delta, new_m, new_v), small):
        res.update(zip(SMALL, vals))

    return (loss, dx0[None], *[g_final[k] for k in WEIGHTS], *[delta[k] for k in WEIGHTS],
            *[new_m[k] for k in WEIGHTS], *[new_v[k] for k in WEIGHTS])
```

```python
import collections
import functools
import math

import jax
import jax.numpy as jnp
from jax import lax
from jax.experimental import pallas as pl
from jax.experimental.pallas import tpu as pltpu

F32 = jnp.float32
BF16 = jnp.bfloat16
MESH = pl.DeviceIdType.MESH

EPS = 1e-6
D_MODEL = 1024
D_FF = 2816
N_CHIP = 4
FF_SHARD = D_FF // N_CHIP
HD = 64
BLK = 128
ATTN_DILATIONS = (1, 4, 16)
HEADS_PER_PATTERN = 8
N_ATTN_HEADS = 24
ALIBI_MAX_EXP = 8.0
ATTN_QKV = 1536
GROUP_W = 512
SSD_INNER = 2048
SSD_HEADS = 32
SSD_GROUPS = 4
SSD_CONV = 4
SSD_CONV_DIM = 3072
IN_COLS = 11808
IN_DT0, IN_DT1 = 9728, 9760
IN_SHARD = IN_COLS // 4
COL_K, COL_V, COL_Z, COL_XBC, COL_GA, COL_GS, P_COLS = 1536, 3072, 4608, 6656, 9728, 10752, 11776
DT_PAD = 128

ADAM_LR, ADAM_B1, ADAM_B2, ADAM_EPS, ADAM_WD, ADAM_STEP = 0.001, 0.9, 0.999, 1e-08, 0.01, 10

V7X_VMEM_LIMIT = 56 * 1024 * 1024
NEG = -1e30


Rider = collections.namedtuple("Rider", "arrays out_shape scratch start finish")
Rider.__doc__ = """An exchange between devices that rides in a compute kernel: its copies are started in the host's
first grid step and waited for in its last, so they travel while the host computes.  arrays / out_shape: extra HBM
operands and results; scratch: extra scratch; start, finish: f(in_refs, out_refs, scratch_refs)."""


def _call(body, *, name, out_shape, in_specs, out_specs, grid=(), scratch_shapes=(), aliases=None, rider=None):
    params = dict(dimension_semantics=("arbitrary",) * len(grid), vmem_limit_bytes=V7X_VMEM_LIMIT)
    if rider is None:
        return pl.pallas_call(
            body, out_shape=out_shape, grid=grid, in_specs=in_specs, out_specs=out_specs,
            scratch_shapes=scratch_shapes, input_output_aliases=aliases or {}, name=name,
            compiler_params=pltpu.CompilerParams(**params))
    single = not isinstance(out_shape, (list, tuple))
    main_out = [out_shape] if single else list(out_shape)
    main_specs = [out_specs] if single else list(out_specs)
    n_in, n_out, n_scr = len(in_specs), len(main_out), len(scratch_shapes)
    r_in, r_out = len(rider.arrays), len(rider.out_shape)

    def wrapped(*refs):
        ins, refs = refs[:n_in], refs[n_in:]
        r_ins, refs = refs[:r_in], refs[r_in:]
        outs, refs = refs[:n_out], refs[n_out:]
        r_outs, refs = refs[:r_out], refs[r_out:]
        scr, r_scr = refs[:n_scr], refs[n_scr:]
        first = last = None
        for axis, size in enumerate(grid):
            at_start, at_end = pl.program_id(axis) == 0, pl.program_id(axis) == size - 1
            first = at_start if first is None else jnp.logical_and(first, at_start)
            last = at_end if last is None else jnp.logical_and(last, at_end)

        @pl.when(first)
        def _():
            rider.start(r_ins, r_outs, r_scr)

        body(*ins, *outs, *scr)

        @pl.when(last)
        def _():
            rider.finish(r_ins, r_outs, r_scr)

    hbm = pl.BlockSpec(memory_space=pl.ANY)
    call = pl.pallas_call(
        wrapped, out_shape=main_out + list(rider.out_shape), grid=grid, in_specs=list(in_specs) + [hbm] * r_in,
        out_specs=main_specs + [hbm] * r_out, scratch_shapes=list(scratch_shapes) + list(rider.scratch), name=name,
        compiler_params=pltpu.CompilerParams(has_side_effects=True, **params))

    def run(*args):
        res = call(*args, *rider.arrays)
        main = res[:n_out]
        return (main[0] if single else main), res[n_out:]

    return run


def _sds(shape, dtype):
    return jax.ShapeDtypeStruct(tuple(shape), dtype)


def _dot(a, b):
    return jnp.dot(a, b, preferred_element_type=F32)


def _dot_nt(a, b):
    return lax.dot_general(a, b, (((1,), (1,)), ((), ())), preferred_element_type=F32)


def _dot_tn(a, b):
    return lax.dot_general(a, b, (((0,), (0,)), ((), ())), preferred_element_type=F32)


def _dot_hi(a, b):
    return jnp.dot(a, b, preferred_element_type=F32, precision=lax.Precision.HIGHEST)


def _sigmoid(x):
    return 1.0 / (1.0 + jnp.exp(-x))


def _lane_first_half(shape):
    return lax.broadcasted_iota(jnp.int32, shape, len(shape) - 1) < HD


def _rowwise(name, fn, rows, consts, outs, accs=(), tm=512, rider=None):
    n_rows = None
    in_arrays, in_specs = [], []
    for r in rows:
        if isinstance(r, tuple):
            arr, w, cb = r
            spec = pl.BlockSpec((tm, w), functools.partial(lambda i, cb: (i, cb), cb=cb))
        else:
            arr = r
            spec = pl.BlockSpec((tm, arr.shape[1]), lambda i: (i, 0))
        n_rows = arr.shape[0]
        in_arrays.append(arr)
        in_specs.append(spec)
    for c in consts:
        in_arrays.append(c)
        in_specs.append(pl.BlockSpec(c.shape, functools.partial(lambda i, n: (0,) * n, n=c.ndim)))
    out_shape = [_sds(s, d) for s, d in outs] + [_sds(s, d) for s, d in accs]
    out_specs = [pl.BlockSpec((tm, s[1]), lambda i: (i, 0)) for s, _ in outs]
    out_specs += [pl.BlockSpec(s, functools.partial(lambda i, n: (0,) * n, n=len(s))) for s, _ in accs]

    def body(*refs):
        fn(pl.program_id(0), *refs)

    res = _call(body, name=name, out_shape=out_shape, grid=(n_rows // tm,), in_specs=in_specs,
                out_specs=out_specs, rider=rider)(*in_arrays)
    return res


def rms_fwd(x, gain, name, rider=None):
    def fn(i, x_ref, g_ref, h_ref):
        xv = x_ref[...]
        r = lax.rsqrt(jnp.mean(xv * xv, axis=-1, keepdims=True) + EPS)
        h_ref[...] = (xv * r * g_ref[...]).astype(h_ref.dtype)

    res = _rowwise(name, fn, [x], [gain], [(x.shape, BF16)], tm=1024, rider=rider)
    return res[0] if rider is None else (res[0][0], res[1])


def rms_bwd(dhs, x, gain, dx_in, name):
    n = len(dhs)

    def fn(i, *refs):
        dh_refs, (x_ref, dxin_ref, g_ref, dx_ref, dg_ref) = refs[:n], refs[n:]
        dh = dh_refs[0][...]
        for r in dh_refs[1:]:
            dh = dh + r[...]
        xv = x_ref[...]
        r = lax.rsqrt(jnp.mean(xv * xv, axis=-1, keepdims=True) + EPS)
        xn = xv * r
        dxn = dh * g_ref[...]
        dx_ref[...] = dxin_ref[...] + r * (dxn - xn * jnp.mean(dxn * xn, axis=-1, keepdims=True))

        @pl.when(i == 0)
        def _():
            dg_ref[...] = jnp.zeros_like(dg_ref)

        dg_ref[...] += jnp.sum(dh * xn, axis=0, keepdims=True)

    return _rowwise(name, fn, list(dhs) + [x, dx_in], [gain], [(x.shape, F32)], [((1, x.shape[1]), F32)])


def matmul_nn(a, b, name, out_dtype, tm, tn, res=None, scale=1.0, rider=None):
    s, k = a.shape
    n = b.shape[1]

    def body(*refs):
        if res is None:
            a_ref, b_ref, o_ref = refs
            o_ref[...] = _dot(a_ref[...], b_ref[...]).astype(o_ref.dtype)
        else:
            a_ref, b_ref, r_ref, o_ref = refs
            o_ref[...] = (r_ref[...] + scale * _dot(a_ref[...], b_ref[...])).astype(o_ref.dtype)

    in_specs = [pl.BlockSpec((tm, k), lambda i, j: (i, 0)), pl.BlockSpec((k, tn), lambda i, j: (0, j))]
    args = [a, b]
    if res is not None:
        in_specs.append(pl.BlockSpec((tm, tn), lambda i, j: (i, j)))
        args.append(res)
    return _call(body, name=name, out_shape=_sds((s, n), out_dtype), grid=(s // tm, n // tn), in_specs=in_specs,
                 out_specs=pl.BlockSpec((tm, tn), lambda i, j: (i, j)), rider=rider)(*args)


def matmul_nt(a, b, name, out_dtype, tm, tn, tk, rider=None):
    s, k = a.shape
    n = b.shape[0]
    nk = k // tk

    def body(a_ref, b_ref, o_ref, acc_ref):
        kk = pl.program_id(2)

        @pl.when(kk == 0)
        def _():
            acc_ref[...] = jnp.zeros_like(acc_ref)

        acc_ref[...] += _dot_nt(a_ref[...].astype(BF16), b_ref[...])

        @pl.when(kk == nk - 1)
        def _():
            o_ref[...] = acc_ref[...].astype(o_ref.dtype)

    return _call(body, name=name, out_shape=_sds((s, n), out_dtype), grid=(s // tm, n // tn, nk),
                 in_specs=[pl.BlockSpec((tm, tk), lambda i, j, kk: (i, kk)),
                           pl.BlockSpec((tn, tk), lambda i, j, kk: (j, kk))],
                 out_specs=pl.BlockSpec((tm, tn), lambda i, j, kk: (i, j)),
                 scratch_shapes=[pltpu.VMEM((tm, tn), F32)], rider=rider)(a, b)


def matmul_tn(a, b, name, tn, ts, a_scale=None, b_scale=None, rider=None):
    s, m = a.shape
    n = b.shape[1]
    ns = s // ts

    def body(a_ref, b_ref, o_ref, acc_ref):
        ss = pl.program_id(1)

        @pl.when(ss == 0)
        def _():
            acc_ref[...] = jnp.zeros_like(acc_ref)

        av, bv = a_ref[...], b_ref[...]
        if a_scale is not None:
            av = av * a_scale
        if b_scale is not None:
            bv = bv * b_scale
        acc_ref[...] += _dot_tn(av.astype(BF16), bv.astype(BF16))

        @pl.when(ss == ns - 1)
        def _():
            o_ref[...] = acc_ref[...].astype(o_ref.dtype)

    return _call(body, name=name, out_shape=_sds((m, n), BF16), grid=(n // tn, ns),
                 in_specs=[pl.BlockSpec((ts, m), lambda j, ss: (ss, 0)), pl.BlockSpec((ts, tn), lambda j, ss: (ss, j))],
                 out_specs=pl.BlockSpec((m, tn), lambda j, ss: (0, j)),
                 scratch_shapes=[pltpu.VMEM((m, tn), F32)], rider=rider)(a, b)


def _piece_specs(pieces, tile, rows_tile, tile_axis_first):
    specs, ranges, t0 = [], [], 0
    for a in pieces:
        n = a.shape[1] // tile

        def index(*ids, t0=t0, n=n):
            t, r = (ids[0], ids[1]) if tile_axis_first else (ids[2], ids[0])
            on = jnp.logical_and(t >= t0, t < t0 + n)
            return jnp.where(on, r, 0), jnp.clip(t - t0, 0, n - 1)

        specs.append(pl.BlockSpec((rows_tile, tile), index))
        ranges.append((t0, n))
        t0 += n
    return specs, ranges


def matmul_tn_pieces(a, pieces, name, tn, ts, rider=None):
    s, m = a.shape
    ns = s // ts
    specs, ranges = _piece_specs(pieces, tn, ts, True)
    n_total = sum(n for _, n in ranges)

    def body(a_ref, *refs):
        b_refs, o_ref, acc_ref = refs[:len(pieces)], refs[-2], refs[-1]
        j, ss = pl.program_id(0), pl.program_id(1)

        @pl.when(ss == 0)
        def _():
            acc_ref[...] = jnp.zeros_like(acc_ref)

        for b_ref, (t0, n) in zip(b_refs, ranges):
            @pl.when(jnp.logical_and(j >= t0, j < t0 + n))
            def _(b_ref=b_ref):
                acc_ref[...] += _dot_tn(a_ref[...], b_ref[...])

        @pl.when(ss == ns - 1)
        def _():
            o_ref[...] = acc_ref[...].astype(o_ref.dtype)

    return _call(body, name=name, out_shape=_sds((m, n_total * tn), BF16), grid=(n_total, ns),
                 in_specs=[pl.BlockSpec((ts, m), lambda j, ss: (ss, 0))] + specs,
                 out_specs=pl.BlockSpec((m, tn), lambda j, ss: (0, j)),
                 scratch_shapes=[pltpu.VMEM((m, tn), F32)], rider=rider)(a, *pieces)


def matmul_nt_pieces(pieces, b, name, out_dtype, tm, tn, tk, rider=None):
    s = pieces[0].shape[0]
    n = b.shape[0]
    specs, ranges = _piece_specs(pieces, tk, tm, False)
    nk = sum(cnt for _, cnt in ranges)

    def body(*refs):
        a_refs, b_ref, o_ref, acc_ref = refs[:len(pieces)], refs[-3], refs[-2], refs[-1]
        kk = pl.program_id(2)

        @pl.when(kk == 0)
        def _():
            acc_ref[...] = jnp.zeros_like(acc_ref)

        for a_ref, (t0, cnt) in zip(a_refs, ranges):
            @pl.when(jnp.logical_and(kk >= t0, kk < t0 + cnt))
            def _(a_ref=a_ref):
                acc_ref[...] += _dot_nt(a_ref[...], b_ref[...])

        @pl.when(kk == nk - 1)
        def _():
            o_ref[...] = acc_ref[...].astype(o_ref.dtype)

    return _call(body, name=name, out_shape=_sds((s, n), out_dtype), grid=(s // tm, n // tn, nk),
                 in_specs=specs + [pl.BlockSpec((tn, tk), lambda i, j, kk: (j, kk))],
                 out_specs=pl.BlockSpec((tm, tn), lambda i, j, kk: (i, j)),
                 scratch_shapes=[pltpu.VMEM((tm, tn), F32)], rider=rider)(*pieces, b)


def ffn_up(h, w704, gate_blk, up_blk, name, tm=512, rider=None):
    s = h.shape[0]

    def body(h_ref, wg_ref, wu_ref, g_ref, u_ref, a_ref):
        hv = h_ref[...]
        g = _dot(hv, wg_ref[...])
        u = _dot(hv, wu_ref[...])
        g_ref[...] = g.astype(BF16)
        u_ref[...] = u.astype(BF16)
        a_ref[...] = (g * _sigmoid(g) * u).astype(BF16)

    ospec = pl.BlockSpec((None, tm, FF_SHARD), lambda j, i: (j, i, 0))
    shp = _sds((N_CHIP, s, FF_SHARD), BF16)
    return _call(body, name=name, out_shape=[shp, shp, shp], grid=(N_CHIP, s // tm),
                 in_specs=[pl.BlockSpec((tm, D_MODEL), lambda j, i: (i, 0)),
                           pl.BlockSpec((None, D_MODEL, FF_SHARD), lambda j, i: (j, gate_blk, 0)),
                           pl.BlockSpec((None, D_MODEL, FF_SHARD), lambda j, i: (j, up_blk, 0))],
                 out_specs=[ospec, ospec, ospec], rider=rider)(h, w704, w704)


def ffn_down(a, w1024, blk, x, name, tm=512, rider=None, target=None):
    s = x.shape[0]

    def block_out(a_ref, wd_ref, x_ref):
        acc = _dot(a_ref[0], wd_ref[0])
        for j in range(1, N_CHIP):
            acc += _dot(a_ref[j], wd_ref[j])
        return x_ref[...] + 0.5 * acc

    def body(a_ref, wd_ref, x_ref, o_ref):
        o_ref[...] = block_out(a_ref, wd_ref, x_ref)

    def loss_body(a_ref, wd_ref, x_ref, t_ref, dy_ref, sq_ref):
        err = block_out(a_ref, wd_ref, x_ref) - t_ref[...]
        dy_ref[...] = err * (1.0 / D_MODEL)

        @pl.when(pl.program_id(0) == 0)
        def _():
            sq_ref[...] = jnp.zeros_like(sq_ref)

        sq_ref[...] += jnp.sum(err * err, axis=0, keepdims=True)

    rows = pl.BlockSpec((tm, D_MODEL), lambda i: (i, 0))
    in_specs = [pl.BlockSpec((N_CHIP, tm, FF_SHARD), lambda i: (0, i, 0)),
                pl.BlockSpec((N_CHIP, FF_SHARD, D_MODEL), lambda i: (0, blk, 0), pl.Buffered(1)), rows]
    if target is None:
        return _call(body, name=name, out_shape=_sds((s, D_MODEL), F32), grid=(s // tm,), in_specs=in_specs,
                     out_specs=rows, rider=rider)(a, w1024, x)
    return _call(loss_body, name=name, out_shape=[_sds((s, D_MODEL), F32), _sds((1, D_MODEL), F32)], grid=(s // tm,),
                 in_specs=in_specs + [rows], out_specs=[rows, pl.BlockSpec((1, D_MODEL), lambda i: (0, 0))],
                 rider=rider)(a, w1024, x, target)


def ffn_bwd_hidden(dx, w1024, blk, g, u, name, tm=1024, rider=None):
    s = dx.shape[0]

    def body(dx_ref, wd_ref, g_ref, u_ref, dg_ref, du_ref):
        dy = (0.5 * dx_ref[...]).astype(BF16)
        da = _dot_nt(dy, wd_ref[...])
        gv = g_ref[...].astype(F32)
        uv = u_ref[...].astype(F32)
        sg = _sigmoid(gv)
        dg_ref[...] = (da * uv * (sg * (1.0 + gv * (1.0 - sg)))).astype(BF16)
        du_ref[...] = (da * gv * sg).astype(BF16)

    hspec = pl.BlockSpec((None, tm, FF_SHARD), lambda j, i: (j, i, 0))
    shp = _sds((N_CHIP, s, FF_SHARD), BF16)
    return _call(body, name=name, out_shape=[shp, shp], grid=(N_CHIP, s // tm),
                 in_specs=[pl.BlockSpec((tm, D_MODEL), lambda j, i: (i, 0)),
                           pl.BlockSpec((None, FF_SHARD, D_MODEL), lambda j, i: (j, blk, 0)), hspec, hspec],
                 out_specs=[hspec, hspec], rider=rider)(dx, w1024, g, u)


def ffn_bwd_input(dg, du, w704, gate_blk, up_blk, x, gain, dy, name, tm=512, rider=None):
    s = dg.shape[1]

    def body(dg_ref, du_ref, wg_ref, wu_ref, x_ref, gain_ref, dy_ref, dx_ref, dgain_ref):
        dh = _dot_nt(dg_ref[0], wg_ref[0]) + _dot_nt(du_ref[0], wu_ref[0])
        for j in range(1, N_CHIP):
            dh += _dot_nt(dg_ref[j], wg_ref[j]) + _dot_nt(du_ref[j], wu_ref[j])
        xv = x_ref[...]
        r = lax.rsqrt(jnp.mean(xv * xv, axis=-1, keepdims=True) + EPS)
        xn = xv * r
        dxn = dh * gain_ref[...]
        dx_ref[...] = dy_ref[...] + r * (dxn - xn * jnp.mean(dxn * xn, axis=-1, keepdims=True))

        @pl.when(pl.program_id(0) == 0)
        def _():
            dgain_ref[...] = jnp.zeros_like(dgain_ref)

        dgain_ref[...] += jnp.sum(dh * xn, axis=0, keepdims=True)

    hspec = pl.BlockSpec((N_CHIP, tm, FF_SHARD), lambda i: (0, i, 0))
    rows = pl.BlockSpec((tm, D_MODEL), lambda i: (i, 0))
    whole = pl.BlockSpec((1, D_MODEL), lambda i: (0, 0))
    return _call(body, name=name, out_shape=[_sds((s, D_MODEL), F32), _sds((1, D_MODEL), F32)], grid=(s // tm,),
                 in_specs=[hspec, hspec,
                           pl.BlockSpec((N_CHIP, D_MODEL, FF_SHARD), lambda i: (0, gate_blk, 0), pl.Buffered(1)),
                           pl.BlockSpec((N_CHIP, D_MODEL, FF_SHARD), lambda i: (0, up_blk, 0), pl.Buffered(1)),
                           rows, whole, rows],
                 out_specs=[rows, whole], rider=rider)(dg, du, w704, w704, x, gain, dy)


def ffn_wgrad_in(h, dgu, name, ts=2048, rider=None):
    s = h.shape[0]
    ns = s // ts

    def body(h_ref, d_ref, o_ref, acc_ref):
        ss = pl.program_id(1)

        @pl.when(ss == 0)
        def _():
            acc_ref[...] = jnp.zeros_like(acc_ref)

        acc_ref[...] += _dot_tn(h_ref[...], d_ref[...])

        @pl.when(ss == ns - 1)
        def _():
            o_ref[...] = acc_ref[...].astype(BF16)

    return _call(body, name=name, out_shape=_sds((N_CHIP, D_MODEL, FF_SHARD), BF16), grid=(N_CHIP, ns),
                 in_specs=[pl.BlockSpec((ts, D_MODEL), lambda j, ss: (ss, 0)),
                           pl.BlockSpec((None, ts, FF_SHARD), lambda j, ss: (j, ss, 0))],
                 out_specs=pl.BlockSpec((None, D_MODEL, FF_SHARD), lambda j, ss: (j, 0, 0)),
                 scratch_shapes=[pltpu.VMEM((D_MODEL, FF_SHARD), F32)], rider=rider)(h, dgu)


def ffn_wgrad_down(a, dx, name, ts=2048):
    s = dx.shape[0]
    ns = s // ts

    def body(a_ref, dx_ref, o_ref, acc_ref):
        ss = pl.program_id(1)

        @pl.when(ss == 0)
        def _():
            acc_ref[...] = jnp.zeros_like(acc_ref)

        acc_ref[...] += _dot_tn(a_ref[...], (0.5 * dx_ref[...]).astype(BF16))

        @pl.when(ss == ns - 1)
        def _():
            o_ref[...] = acc_ref[...].astype(BF16)

    return _call(body, name=name, out_shape=_sds((N_CHIP, FF_SHARD, D_MODEL), BF16), grid=(N_CHIP, ns),
                 in_specs=[pl.BlockSpec((None, ts, FF_SHARD), lambda j, ss: (j, ss, 0)),
                           pl.BlockSpec((ts, D_MODEL), lambda j, ss: (ss, 0))],
                 out_specs=pl.BlockSpec((None, FF_SHARD, D_MODEL), lambda j, ss: (j, 0, 0)),
                 scratch_shapes=[pltpu.VMEM((FF_SHARD, D_MODEL), F32)])(a, dx)


def ffn_forward(x, gain, get_w704, get_w1024, tag, rms_rider=None, up_rider=None, down_rider=None, target=None):
    h = rms_fwd(x, gain, f"{tag}_rms", rider=rms_rider)
    h, rode_rms = h if rms_rider is not None else (h, None)
    res = ffn_up(h, get_w704(rode_rms), 0, 1, f"{tag}_up", tm=1024 if up_rider is None else 512, rider=up_rider)
    (g, u, a), rode_up = res if up_rider is not None else (res, None)
    y = ffn_down(a, get_w1024(rode_up), 0, x, f"{tag}_down", tm=1024, rider=down_rider, target=target)
    y, rode_down = y if down_rider is not None else (y, None)
    return y, (h, g, u, a), rode_rms, rode_up, rode_down


def ffn_backward(dy, x, gain, w704, w1024, saved, tag, hidden_rider=None, ride_down=None, ride_in=None):
    h, g, u, a = saved
    d_wd = ffn_wgrad_down(a, dy, f"{tag}_dwd")
    rode_hidden = None
    if hidden_rider is not None:
        (dg, du), rode_hidden = ffn_bwd_hidden(dy, w1024, 0, g, u, f"{tag}_dhid", rider=hidden_rider)
    else:
        dg, du = ffn_bwd_hidden(dy, w1024, 0, g, u, f"{tag}_dhid")
    if ride_down is not None:
        d_wg, d_wd = ffn_wgrad_in(h, dg, f"{tag}_dwg", rider=ride_down(d_wd))
    else:
        d_wg = ffn_wgrad_in(h, dg, f"{tag}_dwg")
    d_win = jnp.concatenate([d_wg, ffn_wgrad_in(h, du, f"{tag}_dwu")], axis=1)
    if ride_in is not None:
        (dx, d_gain), d_win = ffn_bwd_input(dg, du, w704, 0, 1, x, gain, dy, f"{tag}_dh", tm=256,
                                            rider=ride_in(d_win))
    else:
        dx, d_gain = ffn_bwd_input(dg, du, w704, 0, 1, x, gain, dy, f"{tag}_dh")
    return dx, d_gain, d_win, d_wd, rode_hidden


def _alibi_slope(head):
    return float(2.0 ** (-ALIBI_MAX_EXP * (head + 1) / N_ATTN_HEADS))


def _same_head():
    row = lax.broadcasted_iota(jnp.int32, (2 * HD, 2 * HD), 0)
    col = lax.broadcasted_iota(jnp.int32, (2 * HD, 2 * HD), 1)
    return ((row < HD) == (col < HD)).astype(BF16)


def _head_sums(x, same_head):
    hi = x.astype(BF16)
    lo = (x - hi.astype(F32)).astype(BF16)
    return _dot(hi, same_head) + _dot(lo, same_head)


def _head_norm(t, gain_pair, same_head):
    r = lax.rsqrt(_head_sums(t * t, same_head) * (1.0 / HD) + EPS)
    return t * r * gain_pair, r


def qk_norm_fwd(p, q_gain, k_gain, name):
    s = p.shape[0]

    def fn(i, q_ref, k_ref, qg_ref, kg_ref, qn_ref, kn_ref):
        same_head = _same_head()
        for src, g_ref, dst in ((q_ref, qg_ref, qn_ref), (k_ref, kg_ref, kn_ref)):
            for pr in range(ATTN_QKV // (2 * HD)):
                cols = slice(pr * 2 * HD, (pr + 1) * 2 * HD)
                y, _ = _head_norm(src[:, cols].astype(F32), g_ref[...], same_head)
                dst[:, cols] = y.astype(BF16)

    return _rowwise(name, fn, [(p, ATTN_QKV, 0), (p, ATTN_QKV, 1)], [q_gain, k_gain],
                    [((s, ATTN_QKV), BF16), ((s, ATTN_QKV), BF16)])


def qk_norm_bwd(p, dqs, dks, q_gain, k_gain, name):
    s = p.shape[0]
    pairs_per_pattern = GROUP_W // (2 * HD)

    def fn(i, q_ref, k_ref, dq0, dq1, dq2, dk0, dk1, dk2, qg_ref, kg_ref, dqk_ref, dqg_ref, dkg_ref):
        same_head = _same_head()

        @pl.when(i == 0)
        def _():
            dqg_ref[...] = jnp.zeros_like(dqg_ref)
            dkg_ref[...] = jnp.zeros_like(dkg_ref)

        for src, d_refs, g_ref, dst, dg_ref in (
                (q_ref, (dq0, dq1, dq2), qg_ref, dqk_ref.at[:, 0:ATTN_QKV], dqg_ref),
                (k_ref, (dk0, dk1, dk2), kg_ref, dqk_ref.at[:, ATTN_QKV:2 * ATTN_QKV], dkg_ref)):
            for pr in range(ATTN_QKV // (2 * HD)):
                cols = slice(pr * 2 * HD, (pr + 1) * 2 * HD)
                t = src[:, cols].astype(F32)
                r = lax.rsqrt(_head_sums(t * t, same_head) * (1.0 / HD) + EPS)
                xn = t * r
                within = (pr % pairs_per_pattern) * 2 * HD
                dy = d_refs[pr // pairs_per_pattern][:, within:within + 2 * HD]
                dg_ref[:, cols] += jnp.sum(dy * xn, axis=0, keepdims=True)
                dxn = dy * g_ref[...]
                mean = _head_sums(dxn * xn, same_head) * (1.0 / HD)
                dst[:, cols] = (r * (dxn - xn * mean)).astype(BF16)

    return _rowwise(name, fn, [(p, ATTN_QKV, 0), (p, ATTN_QKV, 1)] + list(dqs) + list(dks), [q_gain, k_gain],
                    [((s, 2 * ATTN_QKV), BF16)], [((1, ATTN_QKV), F32), ((1, ATTN_QKV), F32)])


def _to_streams(a, d):
    if d == 1:
        return a
    s, c = a.shape
    return a.reshape(s // d, d, c).transpose(1, 0, 2).reshape(s, c)


def _from_streams(a, d):
    if d == 1:
        return a
    s, c = a.shape
    return a.reshape(d, s // d, c).transpose(1, 0, 2).reshape(s, c)


def attn_merge_fwd(os_, lses, name):
    s = os_[0].shape[0]

    def fn(i, o0, o1, o2, l0, l1, l2, out_ref):
        m = jnp.maximum(jnp.maximum(l0[...], l1[...]), l2[...])
        e0, e1, e2 = jnp.exp(l0[...] - m), jnp.exp(l1[...] - m), jnp.exp(l2[...] - m)
        inv = 1.0 / (e0 + e1 + e2)
        out_ref[...] = ((e0 * inv) * o0[...] + (e1 * inv) * o1[...] + (e2 * inv) * o2[...]).astype(BF16)

    return _rowwise(name, fn, list(os_) + list(lses), [], [((s, GROUP_W), BF16)], tm=1024)[0]


def attn_merge_bwd(d_out, os_, lses, name):
    s = d_out.shape[0]

    def fn(i, do_ref, o0, o1, o2, l0, l1, l2, d0, d1, d2, c0, c1, c2):
        m = jnp.maximum(jnp.maximum(l0[...], l1[...]), l2[...])
        e0, e1, e2 = jnp.exp(l0[...] - m), jnp.exp(l1[...] - m), jnp.exp(l2[...] - m)
        inv = 1.0 / (e0 + e1 + e2)
        w0, w1, w2 = e0 * inv, e1 * inv, e2 * inv
        do = do_ref[...]
        prod = do * (w0 * o0[...] + w1 * o1[...] + w2 * o2[...])
        same_head = _same_head()
        for pr in range(GROUP_W // (2 * HD)):
            cols = slice(pr * 2 * HD, (pr + 1) * 2 * HD)
            t = _head_sums(prod[:, cols], same_head)
            for w, c_ref in ((w0, c0), (w1, c1), (w2, c2)):
                c_ref[:, cols] = w[:, cols] * t
        for w, d_ref in ((w0, d0), (w1, d1), (w2, d2)):
            d_ref[...] = (w * do).astype(BF16)

    shp = (s, GROUP_W)
    return _rowwise(name, fn, [d_out] + list(os_) + list(lses), [],
                    [(shp, BF16)] * 3 + [(shp, F32)] * 3)


def _band_constants(d):
    row = lax.broadcasted_iota(jnp.int32, (2 * BLK, 2 * BLK), 0)
    col = lax.broadcasted_iota(jnp.int32, (2 * BLK, 2 * BLK), 1)
    rel = BLK + jnp.where(row >= BLK, row - BLK, row) - col
    band = jnp.logical_and(rel >= 0, rel <= BLK)
    return (rel * d).astype(F32), band, (col >= BLK).astype(jnp.int32)


def _stack_heads(x, first):
    zero = jnp.zeros_like(x)
    return jnp.concatenate([jnp.where(first, x, zero), jnp.where(first, zero, x)], axis=0)


def _unstack_heads(x2, first):
    return jnp.where(first, x2[:BLK], x2[BLK:])


def _head_column(x):
    return jnp.concatenate([x[:, 0:1], x[:, HD:HD + 1]], axis=0)


def attn_fwd(q, k, v, pattern, name, tq=1024):
    s = q.shape[0]
    d = ATTN_DILATIONS[pattern]
    blocks_per_stream = (s // d) // BLK
    nsb = tq // BLK

    def body(q_ref, k_ref, v_ref, kp_ref, vp_ref, o_ref, l_ref):
        i = pl.program_id(0)
        rel_f, band, own = _band_constants(d)
        first = _lane_first_half((BLK, 2 * HD))
        upper = lax.broadcasted_iota(jnp.int32, (2 * BLK, 1), 0) < BLK
        for sb in range(nsb):
            rows = slice(sb * BLK, (sb + 1) * BLK)
            has_prev = ((i * nsb + sb) % blocks_per_stream != 0).astype(jnp.int32)
            mask = jnp.logical_and(band, (own + has_prev) > 0)
            for pr in range(GROUP_W // (2 * HD)):
                cols = slice(pr * 2 * HD, (pr + 1) * 2 * HD)
                if sb == 0:
                    kcat = jnp.concatenate([kp_ref[:, cols], k_ref[rows, cols]], axis=0)
                    vcat = jnp.concatenate([vp_ref[:, cols], v_ref[rows, cols]], axis=0)
                else:
                    both = slice((sb - 1) * BLK, (sb + 1) * BLK)
                    kcat, vcat = k_ref[both, cols], v_ref[both, cols]
                h0 = pattern * HEADS_PER_PATTERN + 2 * pr
                slope = jnp.where(upper, _alibi_slope(h0), _alibi_slope(h0 + 1))
                sc = _dot_nt(_stack_heads(q_ref[rows, cols], first), kcat) * 0.125 - slope * rel_f
                sc = jnp.where(mask, sc, NEG)
                m = jnp.max(sc, axis=-1, keepdims=True)
                p = jnp.exp(sc - m)
                l = jnp.sum(p, axis=-1, keepdims=True)
                o2 = _dot((p * (1.0 / l)).astype(BF16), vcat)
                o_ref[rows, cols] = _unstack_heads(o2, first)
                lse = m + jnp.log(l)
                l_ref[rows, cols] = jnp.where(first, lse[:BLK], lse[BLK:])

    cur = pl.BlockSpec((tq, GROUP_W), lambda i: (i, 0))
    prev = pl.BlockSpec((BLK, GROUP_W), lambda i: (jnp.maximum(i * nsb - 1, 0), 0))
    return _call(body, name=name, out_shape=[_sds((s, GROUP_W), F32), _sds((s, GROUP_W), F32)], grid=(s // tq,),
                 in_specs=[cur, cur, cur, prev, prev], out_specs=[cur, cur])(q, k, v, k, v)


def attn_bwd(q, k, v, d_o, cterm, lse, pattern, name, tq=1024):
    s = q.shape[0]
    d = ATTN_DILATIONS[pattern]
    blocks_per_stream = (s // d) // BLK
    nsb = tq // BLK
    n_blocks = s // BLK

    def body(q_ref, k_ref, v_ref, do_ref, c_ref, l_ref, kp_ref, vp_ref, qn_ref, kn_ref, vn_ref, don_ref, cn_ref,
             ln_ref, dq_ref, dk_ref, dv_ref):
        i = pl.program_id(0)
        rel_f, band, own = _band_constants(d)
        first = _lane_first_half((BLK, 2 * HD))
        upper = lax.broadcasted_iota(jnp.int32, (2 * BLK, 1), 0) < BLK
        dk_ref[...] = jnp.zeros_like(dk_ref)
        dv_ref[...] = jnp.zeros_like(dv_ref)
        for sb in range(nsb + 1):
            gb = i * nsb + sb
            rows = slice(sb * BLK, (sb + 1) * BLK)
            before = slice((sb - 1) * BLK, sb * BLK)
            inside = (gb < n_blocks).astype(jnp.int32)
            has_prev = jnp.logical_and(gb % blocks_per_stream != 0, gb < n_blocks).astype(jnp.int32)
            mask = jnp.logical_and(band, (own * inside + has_prev) > 0)
            for pr in range(GROUP_W // (2 * HD)):
                cols = slice(pr * 2 * HD, (pr + 1) * 2 * HD)
                if sb == 0:
                    kcat = jnp.concatenate([kp_ref[:, cols], k_ref[rows, cols]], axis=0)
                    vcat = jnp.concatenate([vp_ref[:, cols], v_ref[rows, cols]], axis=0)
                elif sb == nsb:
                    kcat = jnp.concatenate([k_ref[before, cols], kn_ref[:, cols]], axis=0)
                    vcat = jnp.concatenate([v_ref[before, cols], vn_ref[:, cols]], axis=0)
                else:
                    both = slice((sb - 1) * BLK, (sb + 1) * BLK)
                    kcat, vcat = k_ref[both, cols], v_ref[both, cols]
                if sb < nsb:
                    qp, dop, cp, lp = q_ref[rows, cols], do_ref[rows, cols], c_ref[rows, cols], l_ref[rows, cols]
                else:
                    qp, dop, cp, lp = qn_ref[:, cols], don_ref[:, cols], cn_ref[:, cols], ln_ref[:, cols]
                h0 = pattern * HEADS_PER_PATTERN + 2 * pr
                slope = jnp.where(upper, _alibi_slope(h0), _alibi_slope(h0 + 1))
                q2 = _stack_heads(qp, first)
                do2 = _stack_heads(dop, first)
                sc = jnp.where(mask, _dot_nt(q2, kcat) * 0.125 - slope * rel_f, NEG)
                pm = jnp.exp(sc - _head_column(lp))
                dl = (pm * (_dot_nt(do2, vcat) - _head_column(cp))).astype(BF16)
                if sb < nsb:
                    dq_ref[rows, cols] = _unstack_heads(_dot(dl, kcat), first) * 0.125
                dk2 = _dot_tn(dl, q2) * 0.125
                dv2 = _dot_tn(pm.astype(BF16), do2)
                if sb > 0:
                    dk_ref[before, cols] += dk2[:BLK]
                    dv_ref[before, cols] += dv2[:BLK]
                if sb < nsb:
                    dk_ref[rows, cols] += dk2[BLK:]
                    dv_ref[rows, cols] += dv2[BLK:]

    cur = pl.BlockSpec((tq, GROUP_W), lambda i: (i, 0))
    prev = pl.BlockSpec((BLK, GROUP_W), lambda i: (jnp.maximum(i * nsb - 1, 0), 0))
    nxt = pl.BlockSpec((BLK, GROUP_W), lambda i: (jnp.minimum((i + 1) * nsb, n_blocks - 1), 0))
    shp = _sds((s, GROUP_W), F32)
    return _call(body, name=name, out_shape=[shp, shp, shp], grid=(s // tq,),
                 in_specs=[cur] * 6 + [prev, prev] + [nxt] * 6, out_specs=[cur, cur, cur])(
                     q, k, v, d_o, cterm, lse, k, v, q, k, v, d_o, cterm, lse)


HALO = 16
CONV_TQ = 1024


def conv_fwd(p, w, b, name):
    s = p.shape[0]
    tq = CONV_TQ
    ncol = SSD_CONV_DIM // GROUP_W
    cb0 = COL_XBC // GROUP_W

    def body(u_ref, up_ref, w_ref, b_ref, c_ref, xc_ref):
        i = pl.program_id(0)
        prev = (up_ref[...].astype(F32) * (i > 0).astype(F32)).astype(BF16)
        ext = jnp.concatenate([prev, u_ref[...]], axis=0)
        row = lax.broadcasted_iota(jnp.int32, (BLK, BLK + HALO), 0)
        col = lax.broadcasted_iota(jnp.int32, (BLK, BLK + HALO), 1)
        for blk in range(tq // BLK):
            lead = ext[blk * BLK:blk * BLK + BLK + HALO]
            acc = b_ref[...] + w_ref[SSD_CONV - 1:SSD_CONV, :] * lead[HALO:].astype(F32)
            for kk in range(SSD_CONV - 1):
                pick = (col == row + HALO - (SSD_CONV - 1 - kk)).astype(BF16)
                acc += w_ref[kk:kk + 1, :] * _dot(pick, lead)
            rows = slice(blk * BLK, (blk + 1) * BLK)
            c_ref[rows, :] = acc.astype(BF16)
            xc_ref[rows, :] = (acc * _sigmoid(acc)).astype(BF16)

    cur_in = pl.BlockSpec((tq, GROUP_W), lambda i, j: (i, cb0 + j))
    prev_in = pl.BlockSpec((HALO, GROUP_W), lambda i, j: (jnp.maximum(i * (tq // HALO) - 1, 0), cb0 + j))
    cur_out = pl.BlockSpec((tq, GROUP_W), lambda i, j: (i, j))
    shp = _sds((s, SSD_CONV_DIM), BF16)
    return _call(body, name=name, out_shape=[shp, shp], grid=(s // tq, ncol),
                 in_specs=[cur_in, prev_in, pl.BlockSpec((SSD_CONV, GROUP_W), lambda i, j: (0, j)),
                           pl.BlockSpec((1, GROUP_W), lambda i, j: (0, j))],
                 out_specs=[cur_out, cur_out])(p, p, w, b)


def conv_bwd(p, cpre, dxs, d_b, d_c, w, name):
    s = p.shape[0]
    tq = CONV_TQ
    ncol = SSD_CONV_DIM // GROUP_W
    n_xs = SSD_INNER // GROUP_W
    cb0 = COL_XBC // GROUP_W
    nt = s // tq

    def body(u_ref, c_ref, cn_ref, dx_ref, dxn_ref, dbm_ref, dbmn_ref, dcm_ref, dcmn_ref, w_ref,
             du_ref, dw_ref, db_ref):
        j, i = pl.program_id(0), pl.program_id(1)

        def dpre(c16, dx):
            c = c16.astype(F32)
            sg = _sigmoid(c)
            return dx * (sg * (1.0 + c * (1.0 - sg)))

        def pick(a_ref, b_ref, c_ref_):
            return jnp.where(j < n_xs, a_ref[...], jnp.where(j == n_xs, b_ref[...], c_ref_[...]))

        dc = dpre(c_ref[...], pick(dx_ref, dbm_ref, dcm_ref))
        dcn = dpre(cn_ref[...], pick(dxn_ref, dbmn_ref, dcmn_ref)) * (i < nt - 1).astype(F32)
        dext = jnp.concatenate([dc, dcn], axis=0)
        u = u_ref[...].astype(F32)

        @pl.when(i == 0)
        def _():
            dw_ref[...] = jnp.zeros_like(dw_ref)
            db_ref[...] = jnp.zeros_like(db_ref)

        du = w_ref[SSD_CONV - 1:SSD_CONV, :] * dc
        dw_ref[SSD_CONV - 1:SSD_CONV, :] += jnp.sum(dc * u, axis=0, keepdims=True)
        for kk in range(SSD_CONV - 1):
            sh = SSD_CONV - 1 - kk
            ahead = pltpu.roll(dext, tq + HALO - sh, 0)[0:tq]
            du += w_ref[kk:kk + 1, :] * ahead
            dw_ref[kk:kk + 1, :] += jnp.sum(ahead * u, axis=0, keepdims=True)
        du_ref[...] = du.astype(BF16)
        db_ref[...] += jnp.sum(dc, axis=0, keepdims=True)

    hb = tq // HALO
    cur_p = pl.BlockSpec((tq, GROUP_W), lambda j, i: (i, cb0 + j))
    cur = pl.BlockSpec((tq, GROUP_W), lambda j, i: (i, j))
    nxt = pl.BlockSpec((HALO, GROUP_W), lambda j, i: (jnp.minimum((i + 1) * hb, s // HALO - 1), j))

    def piece(first_tile, n_tiles):
        def on(j):
            return jnp.logical_and(j >= first_tile, j < first_tile + n_tiles)

        def col(j):
            return jnp.clip(j - first_tile, 0, n_tiles - 1)

        return (pl.BlockSpec((tq, GROUP_W), lambda j, i: (jnp.where(on(j), i, 0), col(j))),
                pl.BlockSpec((HALO, GROUP_W),
                             lambda j, i: (jnp.where(on(j), jnp.minimum((i + 1) * hb, s // HALO - 1), 0), col(j))))

    return _call(body, name=name,
                 out_shape=[_sds((s, SSD_CONV_DIM), BF16), _sds((8, SSD_CONV_DIM), F32), _sds((1, SSD_CONV_DIM), F32)],
                 grid=(ncol, nt),
                 in_specs=[cur_p, cur, nxt, *piece(0, n_xs), *piece(n_xs, 1), *piece(n_xs + 1, 1),
                           pl.BlockSpec((SSD_CONV, GROUP_W), lambda j, i: (0, j))],
                 out_specs=[cur, pl.BlockSpec((8, GROUP_W), lambda j, i: (0, j)),
                            pl.BlockSpec((1, GROUP_W), lambda j, i: (0, j))])(
                                p, cpre, cpre, dxs, dxs, d_b, d_b, d_c, d_c, w)


def _softplus(x):
    return jnp.maximum(x, 0.0) + jnp.log(1.0 + jnp.exp(-jnp.abs(x)))


def _ssd_decays(dtr_ref, dtrt_ref, bias_ref, biast_ref, alog_ref, alogt_ref):
    row = lax.broadcasted_iota(jnp.int32, (BLK, BLK), 0)
    col = lax.broadcasted_iota(jnp.int32, (BLK, BLK), 1)
    lower = (row >= col).astype(F32)
    upper = (row <= col).astype(F32)
    dtb = dtr_ref[...] + bias_ref[...]
    dt = _softplus(dtb)
    a = dt * (-jnp.exp(alog_ref[...]))
    cs = _dot_hi(lower, a)
    a_t = _softplus(dtrt_ref[...] + biast_ref[...]) * (-jnp.exp(alogt_ref[...]))
    cs_t = _dot_hi(a_t, upper)
    return dtb, dt, cs, cs_t, row, col, upper


SSD_GROUPS_PER_STEP = 4


def _per_group(body, gps, kinds):
    def wrapped(*refs):
        for gi in range(gps):
            args, pos = [], 0
            for kind, n in kinds:
                if kind == "each":
                    args.append(refs[pos + gi])
                    pos += gps
                    continue
                ref = refs[pos]
                pos += 1
                if kind == "cols":
                    args.append(ref.at[:, gi * n:(gi + 1) * n])
                else:
                    args.append(ref.at[gi] if n == 1 else ref.at[pl.ds(gi * n, n)])
            body(*args)

    return wrapped


def ssd_fwd(p, xc, dtg, dtg_t, params, gn, name):
    s = p.shape[0]
    nc = s // BLK
    bias, bias_t, alog, alog_t, dskip = params

    def body(xs_ref, b_ref, c_ref, z_ref, dtr_ref, dtrt_ref, bias_ref, biast_ref, alog_ref, alogt_ref, dsk_ref,
             gn_ref, y_ref, sin_ref, hp_ref, h_ref):
        c_idx = pl.program_id(1)

        @pl.when(c_idx == 0)
        def _():
            h_ref[...] = jnp.zeros_like(h_ref)

        _, dt, cs, cs_t, row, col, _ = _ssd_decays(dtr_ref, dtrt_ref, bias_ref, biast_ref, alog_ref, alogt_ref)
        first = _lane_first_half((BLK, 2 * HD))
        first_row = _lane_first_half((1, 2 * HD))
        tril = row >= col
        b16, c16 = b_ref[...], c_ref[...]
        cb = _dot_nt(c16, b16)
        n_pairs = GROUP_W // (2 * HD)
        tot = cs[BLK - 1:BLK, :]
        exp_cs, exp_rest, exp_tot = jnp.exp(cs), jnp.exp(tot - cs), jnp.exp(tot)

        lanes_of_head = (lax.broadcasted_iota(jnp.int32, (8, GROUP_W), 1) // HD
                         == lax.broadcasted_iota(jnp.int32, (8, GROUP_W), 0)).astype(BF16)

        def per_head(v, mask):
            if v.shape[0] == 1:
                return jnp.concatenate([jnp.where(mask, v[:, 2 * pr:2 * pr + 1], v[:, 2 * pr + 1:2 * pr + 2])
                                        for pr in range(n_pairs)], axis=1)
            hi = v.astype(BF16)
            lo = (v - hi.astype(F32)).astype(BF16)
            return _dot(hi, lanes_of_head) + _dot(lo, lanes_of_head)

        xs = xs_ref[...].astype(F32)
        xt = xs * per_head(dt, first)
        xt16 = xt.astype(BF16)
        hstate = jnp.concatenate([h_ref[pr] for pr in range(n_pairs)], axis=1)
        for pr in range(n_pairs):
            hp_ref[pr] = h_ref[pr]
        y_off = per_head(exp_cs, first) * _dot(c16, hstate.astype(BF16))
        new = per_head(exp_tot, first_row) * hstate + _dot_tn(b16, (per_head(exp_rest, first) * xt).astype(BF16))
        for pr in range(n_pairs):
            h_ref[pr] = new[:, pr * 2 * HD:(pr + 1) * 2 * HD]
        y_diag = []
        for pr in range(n_pairs):
            cols = slice(pr * 2 * HD, (pr + 1) * 2 * HD)
            m2 = jnp.concatenate(
                [(cb * jnp.exp(jnp.where(tril, cs[:, h:h + 1] - cs_t[h:h + 1, :], NEG))).astype(BF16)
                 for h in (2 * pr, 2 * pr + 1)], axis=1)
            y_diag.append(_dot(m2, _stack_heads(xt16[:, cols], first)))
        y = jnp.concatenate(y_diag, axis=1) + y_off + xs * per_head(dsk_ref[...], first_row)
        y_ref[...] = y
        zv = z_ref[...].astype(F32)
        yz = y * (zv * _sigmoid(zv))
        r = lax.rsqrt(jnp.mean(yz * yz, axis=-1, keepdims=True) + EPS)
        sin_ref[...] = (yz * r * gn_ref[...]).astype(BF16)

    gps = SSD_GROUPS_PER_STEP
    wide, narrow, lead = ("cols", GROUP_W), ("cols", BLK), ("lead", 1)
    kinds = [wide, narrow, narrow, ("each", 0)] + [lead] * 7 + [wide, wide, wide, lead, ("lead", 4)]
    wide_w, narrow_w = GROUP_W * gps, BLK * gps
    gparam = pl.BlockSpec((gps, 1, 8), lambda g, c: (g, 0, 0))
    gparam_t = pl.BlockSpec((gps, 8, 1), lambda g, c: (g, 0, 0))
    z_specs = [pl.BlockSpec((BLK, GROUP_W), functools.partial(lambda g, c, gi: (c, COL_Z // GROUP_W + gps * g + gi),
                                                              gi=gi)) for gi in range(gps)]
    return _call(
        _per_group(body, gps, kinds), name=name,
        out_shape=[_sds((s, SSD_INNER), F32), _sds((s, SSD_INNER), BF16),
                   _sds((SSD_GROUPS, nc, 4, BLK, 2 * HD), F32)],
        grid=(SSD_GROUPS // gps, nc),
        in_specs=[pl.BlockSpec((BLK, wide_w), lambda g, c: (c, g)),
                  pl.BlockSpec((BLK, narrow_w), lambda g, c: (c, SSD_INNER // narrow_w + g)),
                  pl.BlockSpec((BLK, narrow_w), lambda g, c: (c, (SSD_INNER + SSD_GROUPS * BLK) // narrow_w + g)),
                  *z_specs,
                  pl.BlockSpec((gps, BLK, 8), lambda g, c: (g, c, 0)),
                  pl.BlockSpec((gps, 8, BLK), lambda g, c: (g, 0, c)),
                  gparam, gparam_t, gparam, gparam_t, gparam,
                  pl.BlockSpec((1, wide_w), lambda g, c: (0, g))],
        out_specs=[pl.BlockSpec((BLK, wide_w), lambda g, c: (c, g)),
                   pl.BlockSpec((BLK, wide_w), lambda g, c: (c, g)),
                   pl.BlockSpec((gps, None, 4, BLK, 2 * HD), lambda g, c: (g, c, 0, 0, 0))],
        scratch_shapes=[pltpu.VMEM((4 * gps, BLK, 2 * HD), F32)],
    )(xc, xc, xc, *([p] * gps), dtg, dtg_t, bias, bias_t, alog, alog_t, dskip, gn)


def ssd_bwd(p, xc, y, d_sin, hprev, dtg, dtg_t, params, gn, name):
    s = p.shape[0]
    nc = s // BLK
    bias, bias_t, alog, alog_t, dskip = params

    def body(xs_ref, b_ref, c_ref, z_ref, y_ref, dsin_ref, hp_ref, dtr_ref, dtrt_ref, bias_ref,
             biast_ref, alog_ref, alogt_ref, dsk_ref, gn_ref,
             dxs_ref, db_ref, dc_ref, dz_ref, ddt_ref, da_ref, dbias_ref, ddsk_ref, dgn_ref, dh_ref):
        c_idx = pl.program_id(1)

        @pl.when(c_idx == 0)
        def _():
            dh_ref[...] = jnp.zeros_like(dh_ref)
            da_ref[...] = jnp.zeros_like(da_ref)
            dbias_ref[...] = jnp.zeros_like(dbias_ref)
            ddsk_ref[...] = jnp.zeros_like(ddsk_ref)
            dgn_ref[...] = jnp.zeros_like(dgn_ref)

        dtb, dt, cs, cs_t, row, col, upper = _ssd_decays(dtr_ref, dtrt_ref, bias_ref, biast_ref, alog_ref, alogt_ref)
        first = _lane_first_half((BLK, 2 * HD))
        first_row = _lane_first_half((1, 2 * HD))
        tril = row >= col
        triu = row <= col
        last_row = lax.broadcasted_iota(jnp.int32, (BLK, 1), 0) == BLK - 1
        lane8 = lax.broadcasted_iota(jnp.int32, (BLK, 8), 1)

        yv = y_ref[...]
        zv = z_ref[...].astype(F32)
        sg = _sigmoid(zv)
        yz = yv * (zv * sg)
        r = lax.rsqrt(jnp.mean(yz * yz, axis=-1, keepdims=True) + EPS)
        yzn = yz * r
        dsn = dsin_ref[...]
        dgn_ref[...] += jnp.sum(dsn * yzn, axis=0, keepdims=True)
        dsn = dsn * gn_ref[...]
        dyz = r * (dsn - yzn * jnp.mean(dsn * yzn, axis=-1, keepdims=True))
        dy = dyz * (zv * sg)
        dz_ref[...] = (dyz * yv * (sg * (1.0 + zv * (1.0 - sg)))).astype(BF16)
        xs_all = xs_ref[...].astype(F32)
        ddsk_ref[...] += jnp.sum(dy * xs_all, axis=0, keepdims=True)

        b16, c16 = b_ref[...], c_ref[...]
        cb = _dot_nt(c16, b16)
        cb_t = _dot_nt(b16, c16)
        n_pairs = GROUP_W // (2 * HD)
        tot = cs[BLK - 1:BLK, :]
        exp_cs, exp_rest, exp_tot = jnp.exp(cs), jnp.exp(tot - cs), jnp.exp(tot)

        lanes_of_head = (lax.broadcasted_iota(jnp.int32, (8, GROUP_W), 1) // HD
                         == lax.broadcasted_iota(jnp.int32, (8, GROUP_W), 0)).astype(BF16)

        def per_head(v, mask):
            if v.shape[0] == 1:
                return jnp.concatenate([jnp.where(mask, v[:, 2 * pr:2 * pr + 1], v[:, 2 * pr + 1:2 * pr + 2])
                                        for pr in range(n_pairs)], axis=1)
            hi = v.astype(BF16)
            lo = (v - hi.astype(F32)).astype(BF16)
            return _dot(hi, lanes_of_head) + _dot(lo, lanes_of_head)

        head_of_lane = (lax.broadcasted_iota(jnp.int32, (GROUP_W, 8), 0) // HD
                        == lax.broadcasted_iota(jnp.int32, (GROUP_W, 8), 1)).astype(BF16)

        def head_sums(v):
            hi = v.astype(BF16)
            lo = (v - hi.astype(F32)).astype(BF16)
            return _dot(hi, head_of_lane) + _dot(lo, head_of_lane)

        dt_w, e_w, f_w = per_head(dt, first), per_head(exp_cs, first), per_head(exp_rest, first)
        xt = xs_all * dt_w
        xt16 = xt.astype(BF16)
        hstate = jnp.concatenate([hp_ref[pr] for pr in range(n_pairs)], axis=1)
        h16 = hstate.astype(BF16)
        dhn = jnp.concatenate([dh_ref[pr] for pr in range(n_pairs)], axis=1)
        dhn16 = dhn.astype(BF16)
        edy16 = (e_w * dy).astype(BF16)
        y_off = e_w * _dot(c16, h16)
        dcs_all = head_sums(dy * y_off)
        dc_acc = _dot_nt(edy16, h16)
        zmat = _dot(b16, dhn16)
        t_all = head_sums(zmat * xt) * exp_rest
        hh_rows = jnp.sum(head_sums(dhn * hstate), axis=0, keepdims=True)
        dtot = jnp.sum(t_all, axis=0, keepdims=True) + hh_rows * exp_tot
        dcs_all = dcs_all - t_all + jnp.where(last_row, dtot, 0.0)
        fxt16 = (f_w * xt).astype(BF16)
        db_acc = _dot_nt(fxt16, dhn16)
        dh_new = _dot_tn(c16, edy16) + per_head(exp_tot, first_row) * dhn
        for pr in range(n_pairs):
            dh_ref[pr] = dh_new[:, pr * 2 * HD:(pr + 1) * 2 * HD]
        g_sum = jnp.zeros((BLK, BLK), F32)
        gt_sum = jnp.zeros((BLK, BLK), F32)
        d_xt_parts = []
        for pr in range(n_pairs):
            cols = slice(pr * 2 * HD, (pr + 1) * 2 * HD)
            dym2 = _stack_heads(dy[:, cols].astype(BF16), first)
            d_m2 = _dot_nt(dym2, xt16[:, cols])
            d_mt2 = _dot_nt(xt16[:, cols], dym2)
            mt2 = []
            for e, h in enumerate((2 * pr, 2 * pr + 1)):
                cs_c, cs_r = cs[:, h:h + 1], cs_t[h:h + 1, :]
                decay = jnp.exp(jnp.where(tril, cs_c - cs_r, NEG))
                decay_t = jnp.exp(jnp.where(triu, cs_r - cs_c, NEG))
                gm = d_m2[e * BLK:(e + 1) * BLK] * decay
                gmt = d_mt2[:, e * BLK:(e + 1) * BLK] * decay_t
                g_sum += gm
                gt_sum += gmt
                dcs_h = jnp.sum(gm * cb, axis=-1, keepdims=True) - jnp.sum(gmt * cb_t, axis=-1, keepdims=True)
                dcs_all = dcs_all + jnp.where(lane8 == h, dcs_h, 0.0)
                mt2.append((cb_t * decay_t).astype(BF16))
            d_xt_parts.append(_dot(jnp.concatenate(mt2, axis=1), dym2))
        d_xt = jnp.concatenate(d_xt_parts, axis=1) + f_w * zmat
        dxs_ref[...] = dy * per_head(dsk_ref[...], first_row) + d_xt * dt_w
        ddtx_all = head_sums(d_xt * xs_all)

        dc_ref[...] = dc_acc + _dot(g_sum.astype(BF16), b16)
        db_ref[...] = db_acc + _dot(gt_sum.astype(BF16), c16)
        d_a = _dot_hi(upper, dcs_all)
        a_neg = -jnp.exp(alog_ref[...])
        ddt = ddtx_all + d_a * a_neg
        da_ref[...] += jnp.sum(d_a * dt, axis=0, keepdims=True)
        ddtr = ddt * _sigmoid(dtb)
        ddt_ref[...] = ddtr
        dbias_ref[...] += jnp.sum(ddtr, axis=0, keepdims=True)

    gps = SSD_GROUPS_PER_STEP
    k_wide, k_narrow, k_lead = ("cols", GROUP_W), ("cols", BLK), ("lead", 1)
    kinds = ([k_wide, k_narrow, k_narrow, ("each", 0), k_wide, k_wide] + [k_lead] * 8 + [k_wide]
             + [k_wide, k_narrow, k_narrow, k_wide] + [k_lead] * 4 + [k_wide] + [("lead", 4)])
    wide_w, narrow_w = GROUP_W * gps, BLK * gps
    rc = lambda c: nc - 1 - c
    gparam = pl.BlockSpec((gps, 1, 8), lambda g, c: (g, 0, 0))
    gparam_t = pl.BlockSpec((gps, 8, 1), lambda g, c: (g, 0, 0))
    wide = pl.BlockSpec((BLK, wide_w), lambda g, c: (rc(c), g))
    narrow = pl.BlockSpec((BLK, narrow_w), lambda g, c: (rc(c), g))
    z_specs = [pl.BlockSpec((BLK, GROUP_W),
                            functools.partial(lambda g, c, gi: (rc(c), COL_Z // GROUP_W + gps * g + gi), gi=gi))
               for gi in range(gps)]
    return _call(
        _per_group(body, gps, kinds), name=name,
        out_shape=[_sds((s, SSD_INNER), F32), _sds((s, GROUP_W), F32), _sds((s, GROUP_W), F32),
                   _sds((s, SSD_INNER), BF16), _sds((SSD_GROUPS, s, 8), F32),
                   _sds((SSD_GROUPS, 1, 8), F32), _sds((SSD_GROUPS, 1, 8), F32),
                   _sds((SSD_GROUPS, 1, GROUP_W), F32), _sds((1, SSD_INNER), F32)],
        grid=(SSD_GROUPS // gps, nc),
        in_specs=[wide,
                  pl.BlockSpec((BLK, narrow_w), lambda g, c: (rc(c), SSD_INNER // narrow_w + g)),
                  pl.BlockSpec((BLK, narrow_w), lambda g, c: (rc(c), (SSD_INNER + SSD_GROUPS * BLK) // narrow_w + g)),
                  *z_specs,
                  wide, wide,
                  pl.BlockSpec((gps, None, 4, BLK, 2 * HD), lambda g, c: (g, rc(c), 0, 0, 0)),
                  pl.BlockSpec((gps, BLK, 8), lambda g, c: (g, rc(c), 0)),
                  pl.BlockSpec((gps, 8, BLK), lambda g, c: (g, 0, rc(c))),
                  gparam, gparam_t, gparam, gparam_t, gparam,
                  pl.BlockSpec((1, wide_w), lambda g, c: (0, g))],
        out_specs=[wide, narrow, narrow, wide,
                   pl.BlockSpec((gps, BLK, 8), lambda g, c: (g, rc(c), 0)),
                   gparam, gparam,
                   pl.BlockSpec((gps, 1, GROUP_W), lambda g, c: (g, 0, 0)),
                   pl.BlockSpec((1, wide_w), lambda g, c: (0, g))],
        scratch_shapes=[pltpu.VMEM((4 * gps, BLK, 2 * HD), F32)],
    )(xc, xc, xc, *([p] * gps), y, d_sin, hprev, dtg, dtg_t, bias, bias_t, alog, alog_t, dskip, gn)


def merge_fwd(p, a, sbr, name, tm=1024):
    s = p.shape[0]
    nj = D_MODEL // GROUP_W

    def body(ga_ref, gs_ref, a_ref, s_ref, o_ref):
        o_ref[...] = (_sigmoid(ga_ref[...].astype(F32)) * a_ref[...]
                      + _sigmoid(gs_ref[...].astype(F32)) * s_ref[...]).astype(BF16)

    blk = pl.BlockSpec((tm, GROUP_W), lambda i, j: (i, j))
    return _call(body, name=name, out_shape=_sds((s, D_MODEL), BF16), grid=(s // tm, nj),
                 in_specs=[pl.BlockSpec((tm, GROUP_W), lambda i, j: (i, COL_GA // GROUP_W + j)),
                           pl.BlockSpec((tm, GROUP_W), lambda i, j: (i, COL_GS // GROUP_W + j)), blk, blk],
                 out_specs=blk)(p, p, a, sbr)


def merge_bwd(p, a, sbr, dmerged, name, tm=1024):
    s = p.shape[0]
    nj = D_MODEL // GROUP_W

    def body(ga_ref, gs_ref, a_ref, s_ref, dm_ref, da_ref, ds_ref, dga_ref, dgs_ref):
        dm = dm_ref[...]
        sa = _sigmoid(ga_ref[...].astype(F32))
        ss = _sigmoid(gs_ref[...].astype(F32))
        da_ref[...] = (dm * sa).astype(BF16)
        ds_ref[...] = (dm * ss).astype(BF16)
        dga_ref[...] = (dm * a_ref[...] * sa * (1.0 - sa)).astype(BF16)
        dgs_ref[...] = (dm * s_ref[...] * ss * (1.0 - ss)).astype(BF16)

    blk = pl.BlockSpec((tm, GROUP_W), lambda i, j: (i, j))
    shp = _sds((s, D_MODEL), BF16)
    return _call(body, name=name, out_shape=[shp] * 4, grid=(s // tm, nj),
                 in_specs=[pl.BlockSpec((tm, GROUP_W), lambda i, j: (i, COL_GA // GROUP_W + j)),
                           pl.BlockSpec((tm, GROUP_W), lambda i, j: (i, COL_GS // GROUP_W + j)), blk, blk, blk],
                 out_specs=[blk] * 4)(p, p, a, sbr, dmerged)


def _group_major(v):
    return v.reshape(SSD_GROUPS, 1, 8), v.reshape(SSD_GROUPS, 8, 1)


def mixer_forward(x, w, rider=None, later_weights=None):
    s = x.shape[0]
    h = rms_fwd(x, w["mix_norm"], "mix_rms")
    p = matmul_nn(h, w["w_in_main"], "mix_proj", BF16, tm=2048, tn=512, rider=rider)
    rode = None
    if rider is not None:
        p, rode = p
        w = dict(w, **later_weights(rode))
    dt_raw = matmul_nn(h, w["w_in_dt"], "mix_proj_dt", F32, tm=1024, tn=DT_PAD)
    qn, kn = qk_norm_fwd(p, w["q_gain"], w["k_gain"], "qk_norm")
    streams, os_, lses = [], [], []
    for g, d in enumerate(ATTN_DILATIONS):
        cols = slice(g * GROUP_W, (g + 1) * GROUP_W)
        qs, ks = _to_streams(qn[:, cols], d), _to_streams(kn[:, cols], d)
        vs = _to_streams(p[:, COL_V + g * GROUP_W:COL_V + (g + 1) * GROUP_W], d)
        o, lse = attn_fwd(qs, ks, vs, g, f"attn_fwd{g}")
        streams.append((qs, ks, vs, lse))
        os_.append(_from_streams(o, d))
        lses.append(_from_streams(lse, d))
    attn_o = attn_merge_fwd(os_, lses, "attn_merge")
    cpre, xc = conv_fwd(p, w["conv_w"], w["conv_b"], "conv_fwd")
    dtg = dt_raw[:, :SSD_HEADS].reshape(s, SSD_GROUPS, 8).transpose(1, 0, 2)
    dtg_t = dtg.transpose(0, 2, 1)
    params = (*_group_major(w["dt_bias"]), *_group_major(w["a_log"]), _group_major(w["d_skip"])[0])
    y, s_in, hprev = ssd_fwd(p, xc, dtg, dtg_t, params, w["ssd_norm"], "ssd_fwd")
    a = matmul_nn(attn_o, w["w_attn_branch"], "attn_branch", F32, tm=1024, tn=512)
    sbr = matmul_nn(s_in, w["w_ssd_branch"], "ssd_branch", F32, tm=1024, tn=512)
    merged = merge_fwd(p, a, sbr, "merge")
    x_out = matmul_nn(merged, w["w_out"], "mix_out", F32, tm=1024, tn=512, res=x)
    saved = dict(h=h, p=p, streams=streams, os=os_, lses=lses, attn_o=attn_o, cpre=cpre, xc=xc, dtg=dtg,
                 dtg_t=dtg_t, params=params, y=y, s_in=s_in, hprev=hprev, a=a, sbr=sbr, merged=merged, w=w)
    return x_out, saved, rode


def mixer_backward(dx_out, x, sv, ride_early=None, ride_late=None):
    s = x.shape[0]
    p = sv["p"]
    w = sv["w"]
    g = {}
    dmerged = matmul_nt(dx_out, w["w_out"], "d_merged", F32, tm=1024, tn=512, tk=1024)
    g["w_out"] = matmul_tn(sv["merged"], dx_out, "dw_out", tn=512, ts=2048)
    da, ds, dga, dgs = merge_bwd(p, sv["a"], sv["sbr"], dmerged, "merge_bwd")
    g["w_attn_branch"] = matmul_tn(sv["attn_o"], da, "dw_attn_branch", tn=512, ts=2048)
    g["w_ssd_branch"] = matmul_tn(sv["s_in"], ds, "dw_ssd_branch", tn=512, ts=2048)
    d_attn_o = matmul_nt(da, w["w_attn_branch"], "d_attn_o", F32, tm=1024, tn=512, tk=1024)
    d_sin = matmul_nt(ds, w["w_ssd_branch"], "d_ssd_in", F32, tm=1024, tn=512, tk=1024)
    dxs, d_b, d_c, dz, ddt, d_asum, d_bias, d_dsk, d_gn = ssd_bwd(
        p, sv["xc"], sv["y"], d_sin, sv["hprev"], sv["dtg"], sv["dtg_t"], sv["params"], w["ssd_norm"], "ssd_bwd")
    dxbc, d_convw, d_convb = conv_bwd(p, sv["cpre"], dxs, d_b, d_c, w["conv_w"], "conv_bwd")
    g["conv_w"] = d_convw[:SSD_CONV]
    g["conv_b"] = d_convb
    g["dt_bias"] = d_bias.reshape(1, SSD_HEADS)
    g["a_log"] = (d_asum * (-jnp.exp(sv["params"][2]))).reshape(1, SSD_HEADS)
    g["d_skip"] = jnp.sum(d_dsk.reshape(SSD_HEADS, HD), axis=1).reshape(1, SSD_HEADS)
    g["ssd_norm"] = d_gn
    merged_bwd = attn_merge_bwd(d_attn_o, sv["os"], sv["lses"], "attn_merge_bwd")
    dqs, dks, dvs = [], [], []
    for gi, d in enumerate(ATTN_DILATIONS):
        qs, ks, vs, lse = sv["streams"][gi]
        d_o = _to_streams(merged_bwd[gi], d)
        cterm = _to_streams(merged_bwd[3 + gi], d)
        dq, dk, dv = attn_bwd(qs, ks, vs, d_o, cterm, lse, gi, f"attn_bwd{gi}")
        dqs.append(_from_streams(dq, d))
        dks.append(_from_streams(dk, d))
        dvs.append(_from_streams(dv, d).astype(BF16))
    dqk, d_qg, d_kg = qk_norm_bwd(p, dqs, dks, w["q_gain"], w["k_gain"], "qk_norm_bwd")
    g["q_norm"] = jnp.sum(d_qg.reshape(N_ATTN_HEADS, HD), axis=0).reshape(1, HD)
    g["k_norm"] = jnp.sum(d_kg.reshape(N_ATTN_HEADS, HD), axis=0).reshape(1, HD)
    dp = [dqk, jnp.concatenate(dvs, axis=1), dz, dxbc, dga, dgs]
    ddt_pad = jnp.pad(ddt.transpose(1, 0, 2).reshape(s, SSD_HEADS), ((0, 0), (0, DT_PAD - SSD_HEADS)))
    if ride_early is not None:
        g["w_in_main"], g["rode_early"] = matmul_tn_pieces(sv["h"], dp, "dw_in", tn=512, ts=1024,
                                                           rider=ride_early(g))
    else:
        g["w_in_main"] = matmul_tn_pieces(sv["h"], dp, "dw_in", tn=512, ts=1024)
    g["w_in_dt"] = matmul_tn(sv["h"], ddt_pad, "dw_in_dt", tn=DT_PAD, ts=1024)
    if ride_late is not None:
        dh_main, g["rode_late"] = matmul_nt_pieces(dp, w["w_in_main"], "d_mix_h", F32, tm=1024, tn=1024, tk=512,
                                                   rider=ride_late(g))
    else:
        dh_main = matmul_nt_pieces(dp, w["w_in_main"], "d_mix_h", F32, tm=1024, tn=512, tk=512)
    dh_dt = matmul_nt(ddt_pad, w["w_in_dt"], "d_mix_h_dt", F32, tm=1024, tn=1024, tk=DT_PAD)
    dx, g["mix_norm"] = rms_bwd([dh_main, dh_dt], x, w["mix_norm"], dx_out, "mix_drms")
    return dx, g


def _place():
    x, y, c = lax.axis_index("x"), lax.axis_index("y"), lax.axis_index("c")
    chips = [(1 - x, y), (x, 1 - y), (1 - x, 1 - y)]
    return x, y, c, 2 * x + y, chips


def _cores():
    c = lax.axis_index("c")
    return jnp.stack([c, 1 - c]).astype(jnp.int32)


def _staged_call(body, *, name, grid, in_specs, out_specs, out_shape, scratch_shapes):
    return pl.pallas_call(
        body, out_shape=out_shape, name=name,
        grid_spec=pltpu.PrefetchScalarGridSpec(num_scalar_prefetch=1, grid=grid, in_specs=in_specs,
                                               out_specs=out_specs, scratch_shapes=scratch_shapes),
        compiler_params=pltpu.CompilerParams(dimension_semantics=("arbitrary",) * len(grid),
                                             vmem_limit_bytes=V7X_VMEM_LIMIT, has_side_effects=True))


def gather_rider(shards, tiles):
    dma = pltpu.SemaphoreType.DMA
    n = len(shards)
    geo = [(a.shape[0] // 2, tm, (a.shape[0] // 2) // tm) for a, tm in zip(shards, tiles)]
    scratch = []
    for a, (h, tm, nk) in zip(shards, geo):
        scratch += [pltpu.VMEM((N_CHIP,) + a.shape, a.dtype), dma((3, nk)), dma((3, nk)), dma((3, nk)), dma((3, nk)),
                    dma((nk + 2,))]

    def copies(j, in_ref, scr):
        buf, send1, recv1, send2, recv2, local = scr[6 * j:6 * j + 6]
        h, tm, nk = geo[j]
        x, y, c, me, chips = _place()
        chip_of = [2 * chips[t][0] + chips[t][1] for t in range(3)]

        def rows(chip, core, k):
            return buf.at[chip, pl.ds(core * h + k * tm, tm)]

        def mine(k):
            if k == nk:
                return pltpu.make_async_copy(in_ref.at[pl.ds((1 - c) * h, h)], buf.at[me, pl.ds((1 - c) * h, h)],
                                             local.at[nk])
            return pltpu.make_async_copy(in_ref.at[pl.ds(c * h + k * tm, tm)], rows(me, c, k), local.at[k])

        def level1(t, k, incoming):
            place = rows(chip_of[t] if incoming else me, c, k)
            return pltpu.make_async_remote_copy(src_ref=place, dst_ref=place, send_sem=send1.at[t, k],
                                                recv_sem=recv1.at[t, k], device_id=(*chips[t], c), device_id_type=MESH)

        def level2(t, k, incoming):
            place = rows(chip_of[t], (1 - c) if incoming else c, k)
            return pltpu.make_async_remote_copy(src_ref=place, dst_ref=place, send_sem=send2.at[t, k],
                                                recv_sem=recv2.at[t, k], device_id=(x, y, 1 - c),
                                                device_id_type=MESH)

        return buf, local, nk, mine, level1, level2

    def start(ins, outs, scr):
        for j in range(n):
            _, _, nk, mine, _, _ = copies(j, ins[j], scr)
            for k in range(nk + 1):
                mine(k).start()
        for j in range(n):
            _, _, nk, mine, level1, _ = copies(j, ins[j], scr)
            for k in range(nk):
                mine(k).wait()
                for t in range(3):
                    level1(t, k, False).start()

    def finish(ins, outs, scr):
        for j in range(n):
            _, _, nk, _, level1, level2 = copies(j, ins[j], scr)
            for k in range(nk):
                for t in range(3):
                    level1(t, k, True).wait_recv()
                    level2(t, k, False).start()
        for j in range(n):
            buf, local, nk, mine, level1, level2 = copies(j, ins[j], scr)
            for k in range(nk):
                for t in range(3):
                    level2(t, k, True).wait_recv()
            for k in range(nk):
                for t in range(3):
                    level1(t, k, False).wait_send()
                    level2(t, k, False).wait_send()
            mine(nk).wait()
            pltpu.make_async_copy(buf, outs[j], local.at[nk + 1]).start()
        for j in range(n):
            buf, local, nk, _, _, _ = copies(j, ins[j], scr)
            pltpu.make_async_copy(buf, outs[j], local.at[nk + 1]).wait()

    return Rider(list(shards), [_sds((N_CHIP,) + a.shape, a.dtype) for a in shards], scratch, start, finish)


def sibling_sum(g, tm, name):
    _, r, cdim = g.shape
    h = r // 2
    ni = h // tm
    dma = pltpu.SemaphoreType.DMA

    def body(cores_ref, keep_ref, give_ref, out_ref, slot, send, recv):
        par = (pl.program_id(0) * ni + pl.program_id(1)) % 2
        x, y, c, _, _ = _place()
        cp = pltpu.make_async_remote_copy(src_ref=give_ref, dst_ref=slot.at[par], send_sem=send.at[par],
                                          recv_sem=recv.at[par], device_id=(x, y, 1 - c), device_id_type=MESH)
        cp.start()
        cp.wait_recv()
        out_ref[...] = (keep_ref[...].astype(F32) + slot[par].astype(F32)).astype(out_ref.dtype)
        cp.wait_send()

    flat = g.reshape(N_CHIP * r, cdim)
    return _staged_call(
        body, name=name, grid=(N_CHIP, ni),
        in_specs=[pl.BlockSpec((tm, cdim), lambda j, i, cores: ((2 * j + cores[0]) * ni + i, 0)),
                  pl.BlockSpec((tm, cdim), lambda j, i, cores: ((2 * j + cores[1]) * ni + i, 0))],
        out_specs=pl.BlockSpec((None, tm, cdim), lambda j, i, cores: (j, i, 0)),
        out_shape=_sds((N_CHIP, h, cdim), g.dtype),
        scratch_shapes=[pltpu.VMEM((2, tm, cdim), g.dtype), dma((2,)), dma((2,))],
    )(_cores(), flat, flat)


def owner_sum_rider(sums, tiles):
    dma = pltpu.SemaphoreType.DMA
    n = len(sums)
    geo = [(a.shape[1], tm, a.shape[1] // tm) for a, tm in zip(sums, tiles)]
    scratch = []
    for a, (h, tm, nk) in zip(sums, geo):
        cdim = a.shape[2]
        scratch += [pltpu.VMEM(a.shape, a.dtype), pltpu.VMEM((3, h, cdim), a.dtype), pltpu.VMEM((2, h, cdim), F32),
                    dma((3, nk)), dma((3, nk)), dma((nk,)), dma((nk,)), dma((2,))]

    def copies(j, scr):
        part, got, res, send, recv, send2, recv2, local = scr[8 * j:8 * j + 8]
        h, tm, nk = geo[j]
        x, y, c, me, chips = _place()

        def to_owner(t, k):
            chip = 2 * chips[t][0] + chips[t][1]
            return pltpu.make_async_remote_copy(
                src_ref=part.at[chip, pl.ds(k * tm, tm)], dst_ref=got.at[t, pl.ds(k * tm, tm)],
                send_sem=send.at[t, k], recv_sem=recv.at[t, k], device_id=(*chips[t], c), device_id_type=MESH)

        def to_sibling(k):
            place = res.at[c, pl.ds(k * tm, tm)]
            return pltpu.make_async_remote_copy(src_ref=place, dst_ref=place, send_sem=send2.at[k],
                                                recv_sem=recv2.at[k], device_id=(x, y, 1 - c), device_id_type=MESH)

        return part, got, res, local, to_owner, to_sibling, (tm, nk, c, me)

    def start(ins, outs, scr):
        for j in range(n):
            part, _, _, local, _, _, _ = copies(j, scr)
            pltpu.make_async_copy(ins[j], part, local.at[0]).start()
        for j in range(n):
            part, _, _, local, to_owner, _, (tm, nk, c, me) = copies(j, scr)
            pltpu.make_async_copy(ins[j], part, local.at[0]).wait()
            for k in range(nk):
                for t in range(3):
                    to_owner(t, k).start()

    def finish(ins, outs, scr):
        for j in range(n):
            part, got, res, _, to_owner, to_sibling, (tm, nk, c, me) = copies(j, scr)
            for k in range(nk):
                rows = pl.ds(k * tm, tm)
                for t in range(3):
                    to_owner(t, k).wait_recv()
                acc = part[me, rows, :].astype(F32)
                for t in range(3):
                    acc = acc + got[t, rows, :].astype(F32)
                res[c, rows, :] = acc
                to_sibling(k).start()
        for j in range(n):
            _, _, res, local, to_owner, to_sibling, (tm, nk, c, me) = copies(j, scr)
            for k in range(nk):
                to_sibling(k).wait_recv()
            for k in range(nk):
                to_sibling(k).wait_send()
                for t in range(3):
                    to_owner(t, k).wait_send()
            pltpu.make_async_copy(res, outs[j], local.at[1]).start()
        for j in range(n):
            _, _, res, local, _, _, _ = copies(j, scr)
            pltpu.make_async_copy(res, outs[j], local.at[1]).wait()

    return Rider(list(sums), [_sds((2, a.shape[1], a.shape[2]), F32) for a in sums], scratch, start, finish)


def gather_conv_w(w):
    def body(in_ref, out_ref, send, recv):
        x, y, c, me, chips = _place()
        out_ref[me] = in_ref[...]
        copies = []
        for t in range(3):
            copies.append(pltpu.make_async_remote_copy(
                src_ref=out_ref.at[me], dst_ref=out_ref.at[me], send_sem=send.at[t], recv_sem=recv.at[t],
                device_id=(*chips[t], c), device_id_type=MESH))
        for cp in copies:
            cp.start()
        for cp in copies:
            cp.wait_recv()
        for cp in copies:
            cp.wait_send()

    dma = pltpu.SemaphoreType.DMA
    vmem = pl.BlockSpec(memory_space=pltpu.VMEM)
    return pl.pallas_call(
        body, out_shape=_sds((N_CHIP,) + w.shape, w.dtype), in_specs=[vmem], out_specs=vmem, name="gather_conv_w",
        scratch_shapes=[dma((3,)), dma((3,))],
        compiler_params=pltpu.CompilerParams(has_side_effects=True))(w)


N_DEV = 8
SMALL_ROWS = 32
SMALL_LANES = 1024


def all_reduce_small(arrays):
    n_arr = len(arrays)
    places = []
    for k, a in enumerate(arrays):
        for ri in range(a.shape[0]):
            for c0 in range(0, a.shape[1], SMALL_LANES):
                places.append((k, ri, c0, min(SMALL_LANES, a.shape[1] - c0), len(places)))
    assert len(places) <= SMALL_ROWS

    def body(*refs):
        ins, outs = refs[:n_arr], refs[n_arr:2 * n_arr]
        buf, send, recv = refs[2 * n_arr:]
        x, y, c, _, _ = _place()
        me = 4 * x + 2 * y + c
        buf[me] = jnp.zeros((SMALL_ROWS, SMALL_LANES), F32)
        for k, ri, c0, width, row in places:
            buf[me, row:row + 1, 0:width] = ins[k][ri:ri + 1, c0:c0 + width]
        copies = []
        for r in range(1, N_DEV):
            px = (1 - x) if r & 4 else x
            py = (1 - y) if r & 2 else y
            pc = (1 - c) if r & 1 else c
            copies.append(pltpu.make_async_remote_copy(
                src_ref=buf.at[me], dst_ref=buf.at[me], send_sem=send.at[r - 1], recv_sem=recv.at[r - 1],
                device_id=(px, py, pc), device_id_type=MESH))
        for cp in copies:
            cp.start()
        for cp in copies:
            cp.wait_recv()
        for cp in copies:
            cp.wait_send()
        acc = buf[0]
        for j in range(1, N_DEV):
            acc = acc + buf[j]
        for k, ri, c0, width, row in places:
            outs[k][ri:ri + 1, c0:c0 + width] = acc[row:row + 1, 0:width]

    dma = pltpu.SemaphoreType.DMA
    vmem = pl.BlockSpec(memory_space=pltpu.VMEM)
    return pl.pallas_call(
        body, out_shape=[_sds(a.shape, F32) for a in arrays], in_specs=[vmem] * n_arr, out_specs=[vmem] * n_arr,
        name="all_reduce_small",
        scratch_shapes=[pltpu.VMEM((N_DEV, SMALL_ROWS, SMALL_LANES), F32), dma((N_DEV - 1,)), dma((N_DEV - 1,))],
        compiler_params=pltpu.CompilerParams(has_side_effects=True))(*arrays)


def _row_tile(rows, limit, multiple):
    return max(t for t in range(multiple, min(rows, limit) + 1, multiple) if rows % t == 0)


def _adamw_math(w, g, m, v):
    c1 = 1.0 - ADAM_B1 ** ADAM_STEP
    c2 = 1.0 - ADAM_B2 ** ADAM_STEP
    m2 = ADAM_B1 * m + (1.0 - ADAM_B1) * g
    v2 = ADAM_B2 * v + (1.0 - ADAM_B2) * (g * g)
    return -ADAM_LR * ((m2 / c1) / (jnp.sqrt(v2 / c2) + ADAM_EPS) + ADAM_WD * w), m2, v2


def adamw(w, g, row_off, m, v, name):
    _, r, c = w.shape
    tm = r if r < 8 else _row_tile(math.gcd(r, row_off) if row_off else r, 128, 8)

    def body(w_ref, g_ref, m_ref, v_ref, go_ref, d_ref, m2_ref, v2_ref):
        gv = g_ref[...]
        go_ref[...] = gv
        d_ref[...], m2_ref[...], v2_ref[...] = _adamw_math(w_ref[...], gv, m_ref[...], v_ref[...])

    blk = pl.BlockSpec((None, tm, c), lambda i: (0, i, 0))
    shp = _sds((1, r, c), F32)
    return _call(body, name=name, out_shape=[shp] * 4, grid=(r // tm,),
                 in_specs=[blk, pl.BlockSpec((tm, c), lambda i: (row_off // tm + i, 0)), blk, blk],
                 out_specs=[blk] * 4)(w, g, m, v)


def adamw_small(ws, gs, ms, vs):
    n = len(ws)

    def body(*refs):
        ins, outs = refs[:4 * n], refs[4 * n:]
        for k in range(n):
            w_ref, g_ref, m_ref, v_ref = (ins[j * n + k] for j in range(4))
            outs[k][...], outs[n + k][...], outs[2 * n + k][...] = _adamw_math(w_ref[...], g_ref[...], m_ref[...],
                                                                               v_ref[...])

    vmem = pl.BlockSpec(memory_space=pltpu.VMEM)
    shapes = [_sds(w.shape, F32) for w in ws] * 3
    res = pl.pallas_call(body, out_shape=shapes, in_specs=[vmem] * (4 * n), out_specs=[vmem] * (3 * n),
                         name="adamw_small")(*ws, *gs, *ms, *vs)
    return res[:n], res[n:2 * n], res[2 * n:]


SMALL = ("ffn1_norm", "mix_norm", "q_norm", "k_norm", "conv_b", "dt_bias", "a_log", "d_skip", "ssd_norm", "ffn2_norm")
WEIGHTS = ("ffn1_norm", "ffn1_w_gate", "ffn1_w_up", "ffn1_w_down", "mix_norm", "w_in", "q_norm", "k_norm", "conv_w",
           "conv_b", "dt_bias", "a_log", "d_skip", "ssd_norm", "w_attn_branch", "w_ssd_branch", "w_out", "ffn2_norm",
           "ffn2_w_gate", "ffn2_w_up", "ffn2_w_down")
CONV_SHARD = SSD_CONV_DIM // N_CHIP
CLASSES = {
    "ffn1_in": (("ffn1_w_gate", 1024), ("ffn1_w_up", 1024)),
    "ffn1_out": (("ffn1_w_down", 704),),
    "mix_in": (("w_in", 1024),),
    "mix_attn": (("w_attn_branch", 512),),
    "late_out": (("ffn2_w_down", 704), ("w_ssd_branch", 512), ("w_out", 256)),
    "ffn2_in": (("ffn2_w_gate", 1024), ("ffn2_w_up", 1024)),
}
CLASS_TILE = {"ffn1_in": 256, "ffn1_out": 176, "mix_attn": 256, "late_out": 368, "ffn2_in": 256,
              "mix_in_top": 128, "mix_in_bottom": 128}
SIBLING_TILE = {"ffn1_in": 1024, "ffn1_out": 352, "mix_attn": 256, "late_out": 736, "ffn2_in": 1024,
                "mix_in_top": 256, "mix_in_bottom": 256}


def _chip_major_cols(a):
    r = a.shape[0]
    return a.reshape(r, N_CHIP, -1).transpose(1, 0, 2)


def _from_chip_major_cols(a):
    return a.transpose(1, 0, 2).reshape(a.shape[1], -1)


def kernel(x, ffn1_norm, ffn1_w_gate, ffn1_w_up, ffn1_w_down, mix_norm, w_in, q_norm, k_norm, conv_w, conv_b, dt_bias, a_log, d_skip, ssd_norm, w_attn_branch, w_ssd_branch, w_out, ffn2_norm, ffn2_w_gate, ffn2_w_up, ffn2_w_down, loss_target, m_ffn1_norm, m_ffn1_w_gate, m_ffn1_w_up, m_ffn1_w_down, m_mix_norm, m_w_in, m_q_norm, m_k_norm, m_conv_w, m_conv_b, m_dt_bias, m_a_log, m_d_skip, m_ssd_norm, m_w_attn_branch, m_w_ssd_branch, m_w_out, m_ffn2_norm, m_ffn2_w_gate, m_ffn2_w_up, m_ffn2_w_down, v_ffn1_norm, v_ffn1_w_gate, v_ffn1_w_up, v_ffn1_w_down, v_mix_norm, v_w_in, v_q_norm, v_k_norm, v_conv_w, v_conv_b, v_dt_bias, v_a_log, v_d_skip, v_ssd_norm, v_w_attn_branch, v_w_ssd_branch, v_w_out, v_ffn2_norm, v_ffn2_w_gate, v_ffn2_w_up, v_ffn2_w_down):
    env = dict(locals())
    wts = {k: env[k] for k in WEIGHTS}
    moms = {k: env["m_" + k] for k in WEIGHTS}
    vars_ = {k: env["v_" + k] for k in WEIGHTS}
    x0 = x[0]
    target = loss_target[0]

    def gather(classes, more=()):
        shards = [jnp.concatenate([wts[k][0] for k, _ in CLASSES[c]], axis=0).astype(BF16) for c in classes]
        return gather_rider(shards + [a for a, _ in more], [CLASS_TILE[c] for c in classes] + [t for _, t in more])

    def reducer(classes, parts):
        sums = [sibling_sum(p, SIBLING_TILE[c], f"sibling_sum_{c}") for c, p in zip(classes, parts)]
        return owner_sum_rider(sums, [CLASS_TILE[c] for c in classes])

    half = D_MODEL // 2
    in_tile = CLASS_TILE["mix_in_top"]
    x1, saved1, (w_ffn1_in,), (w_ffn1_out, w_mix_attn, w_in_top), (w_in_bottom,) = ffn_forward(
        x0, ffn1_norm, lambda rode: rode[0], lambda rode: rode[0], "ffn1", rms_rider=gather(["ffn1_in"]),
        up_rider=gather(["ffn1_out", "mix_attn"], [(w_in[0, :half].astype(BF16), in_tile)]),
        down_rider=gather([], [(w_in[0, half:].astype(BF16), in_tile)]))
    dt0, dt1 = IN_DT0 - 3 * IN_SHARD, IN_DT1 - 3 * IN_SHARD

    def in_columns(w4):
        return jnp.concatenate([w4[0], w4[1], w4[2], w4[3][:, :dt0], w4[3][:, dt1:]], axis=1), w4[3][:, dt0:dt1]

    (main_top, dt_top), (main_bottom, dt_bottom) = in_columns(w_in_top), in_columns(w_in_bottom)
    mixer_w = dict(
        mix_norm=mix_norm,
        w_in_main=jnp.concatenate([main_top, main_bottom], axis=0),
        w_in_dt=jnp.pad(jnp.concatenate([dt_top, dt_bottom], axis=0), ((0, 0), (0, DT_PAD - SSD_HEADS))),
        q_gain=jnp.tile(q_norm, (1, 2)), k_gain=jnp.tile(k_norm, (1, 2)),
        conv_w=_from_chip_major_cols(gather_conv_w(conv_w[0])), conv_b=conv_b, dt_bias=dt_bias, a_log=a_log,
        d_skip=d_skip, ssd_norm=ssd_norm, w_attn_branch=_from_chip_major_cols(w_mix_attn))

    def later_weights(rode):
        late = rode[0]
        return dict(w_ssd_branch=late[:, 704:1216].reshape(SSD_INNER, D_MODEL),
                    w_out=late[:, 1216:1472].reshape(D_MODEL, D_MODEL))

    x2, saved_mix, (w_late_out, w_ffn2_in) = mixer_forward(x1, mixer_w, gather(["late_out", "ffn2_in"]), later_weights)
    (dx3, sq), saved2, _, _, _ = ffn_forward(x2, ffn2_norm, lambda rode: w_ffn2_in, lambda rode: w_late_out, "ffn2",
                                             target=target)

    grads = {}
    dx2, grads["ffn2_norm"], d_ffn2_in, d_ffn2_down, _ = ffn_backward(dx3, x2, ffn2_norm, w_ffn2_in, w_late_out,
                                                                      saved2, "ffn2")

    def ride_early(g):
        late = jnp.concatenate([d_ffn2_down, g["w_ssd_branch"].reshape(N_CHIP, -1, D_MODEL),
                                g["w_out"].reshape(N_CHIP, -1, D_MODEL)], axis=1)
        return reducer(["ffn2_in", "late_out"], [d_ffn2_in, late])

    g_in_rows = {}

    def ride_late(g):
        main = g["w_in_main"]
        last = jnp.concatenate([main[:, 3 * IN_SHARD:IN_DT0], g["w_in_dt"][:, :SSD_HEADS], main[:, IN_DT0:]], axis=1)
        for part, rows in (("top", slice(0, D_MODEL // 2)), ("bottom", slice(D_MODEL // 2, D_MODEL))):
            g_in_rows[part] = jnp.stack([main[rows, j * IN_SHARD:(j + 1) * IN_SHARD] for j in range(3)]
                                        + [last[rows]])
        return reducer(["mix_in_top", "mix_attn"], [g_in_rows["top"], _chip_major_cols(g["w_attn_branch"])])

    dx1, gmix = mixer_backward(dx2, x1, saved_mix, ride_early, ride_late)
    dx0, grads["ffn1_norm"], rode_in, rode_out, rode_hidden = ffn_backward(
        dx1, x0, ffn1_norm, w_ffn1_in, w_ffn1_out, saved1, "ffn1",
        hidden_rider=reducer(["mix_in_bottom"], [g_in_rows["bottom"]]),
        ride_down=lambda d: reducer(["ffn1_out"], [d]), ride_in=lambda d: reducer(["ffn1_in"], [d]))
    for k in ("mix_norm", "q_norm", "k_norm", "conv_b", "dt_bias", "a_log", "d_skip", "ssd_norm"):
        grads[k] = gmix[k]
    reduced = dict(zip(("ffn2_in", "late_out", "mix_in_top", "mix_attn", "ffn1_in", "ffn1_out", "mix_in_bottom"),
                       (*gmix["rode_early"], *gmix["rode_late"], rode_in[0], rode_out[0], rode_hidden[0])))
    reduced = {c: r.reshape(-1, r.shape[2]) for c, r in reduced.items()}
    reduced["mix_in"] = jnp.concatenate([reduced.pop("mix_in_top"), reduced.pop("mix_in_bottom")], axis=0)
    summed = all_reduce_small([grads[k] for k in SMALL]
                              + [gmix["conv_w"], (0.5 * jnp.sum(sq) / D_MODEL).reshape(1, 1)])
    g_small = dict(zip(SMALL, summed))
    loss = summed[-1].reshape(())
    chip = 2 * lax.axis_index("x") + lax.axis_index("y")
    g_conv = lax.dynamic_slice_in_dim(summed[-2], chip * CONV_SHARD, CONV_SHARD, axis=1)

    g_final, delta, new_m, new_v = dict(g_small), {}, {}, {}

    def update(k, g_arr, row_off):
        w, m, v = wts[k], moms[k], vars_[k]
        rows, cols = w.shape[1:]
        if cols % 128:
            res = adamw(jnp.swapaxes(w, 1, 2), g_arr[row_off:row_off + rows].T, 0, jnp.swapaxes(m, 1, 2),
                        jnp.swapaxes(v, 1, 2), f"adamw_{k}")
            res = [jnp.swapaxes(r, 1, 2) for r in res]
        else:
            res = adamw(w, g_arr, row_off, m, v, f"adamw_{k}")
        g_final[k], delta[k], new_m[k], new_v[k] = res

    for cls, members in CLASSES.items():
        off = 0
        for k, rows in members:
            update(k, reduced[cls], off)
            off += rows
    update("conv_w", g_conv, 0)
    small = adamw_small(*([d[k] for k in SMALL] for d in (wts, g_small, moms, vars_)))
    for res, vals in zip((delta, new_m, new_v), small):
        res.update(zip(SMALL, vals))

    return (loss, dx0[None], *[g_final[k] for k in WEIGHTS], *[delta[k] for k in WEIGHTS],
            *[new_m[k] for k in WEIGHTS], *[new_v[k] for k in WEIGHTS])
```
